```python
import jax
import jax.numpy as jnp
from jax import lax
import numpy as np

D_MODEL = 1024
BATCH = 16
SEQ = 2048
DEPTH = 1

GDN_HEADS = 8
GDN_DK = 128
GDN_DV = 128
GDN_CONV = 4
CHUNK = 64
SC_WIDTH = D_MODEL
SC_CONV = 3
D_FF_RAW = (8 * D_MODEL + 2) // 3
D_FF = (D_FF_RAW + 255) // 256 * 256
EPS = 1e-6
N_MOD = 6

QK_W = GDN_HEADS * GDN_DK
V_W = GDN_HEADS * GDN_DV
QKV_W = 2 * QK_W + V_W
Q_OFF = 0
K_OFF = Q_OFF + QK_W
V_OFF = K_OFF + QK_W
Z_OFF = V_OFF + V_W
A_OFF = Z_OFF + V_W
BETA_OFF = A_OFF + GDN_HEADS
SCB_OFF = BETA_OFF + GDN_HEADS
SCC_OFF = SCB_OFF + SC_WIDTH
SCX_OFF = SCC_OFF + SC_WIDTH
GA_OFF = SCX_OFF + SC_WIDTH
GB_OFF = GA_OFF + D_MODEL
IN_COLS = GB_OFF + D_MODEL

kernel_name = "cond_hybrid_gdn_shortconv_block"


def rms_norm(x, w):
    xf = x.astype(jnp.float32)
    y = xf * lax.rsqrt(jnp.mean(xf * xf, axis=-1, keepdims=True) + EPS)
    return (y * w.astype(jnp.float32)).astype(x.dtype)


def l2_normalize(x):
    xf = x.astype(jnp.float32)
    return (xf * lax.rsqrt(jnp.sum(xf * xf, axis=-1, keepdims=True) + EPS)).astype(x.dtype)


def modulate(h, shift, scale):
    return h * (1.0 + scale[:, None, :]) + shift[:, None, :]


def causal_depthwise_conv(x, w):
    width, ch = w.shape
    return lax.conv_general_dilated(
        x, w[:, None, :].astype(x.dtype), window_strides=(1,), padding=[(width - 1, 0)],
        dimension_numbers=("NWC", "WIO", "NWC"), feature_group_count=ch)


def gated_delta_rule_chunked(q, k, v, g, beta):
    f32 = jnp.float32
    out_dtype = v.dtype
    b, s, h, dk = q.shape
    dv = v.shape[-1]
    n = s // CHUNK

    def to_chunks(t):
        t = t.astype(f32).reshape((b, n, CHUNK, h) + t.shape[3:])
        return jnp.moveaxis(t, 3, 1)

    q, k, v, g, beta = map(to_chunks, (q, k, v, g, beta))
    q = q * (dk ** -0.5)
    g = jnp.cumsum(g, axis=-1)
    idx = jnp.arange(CHUNK)
    causal = idx[:, None] >= idx[None, :]
    strict = idx[:, None] > idx[None, :]
    decay = jnp.exp(jnp.where(causal, g[..., :, None] - g[..., None, :], -jnp.inf))

    k_beta = k * beta[..., None]
    v_beta = v * beta[..., None]
    lower = jnp.where(strict, jnp.einsum("bhncd,bhnsd->bhncs", k_beta, k) * decay, 0.0)
    eye = jnp.eye(CHUNK, dtype=f32)
    t_inv = lax.linalg.triangular_solve(eye + lower, jnp.broadcast_to(eye, lower.shape),
                                        left_side=True, lower=True, unit_diagonal=True)
    eg = jnp.exp(g)
    u = jnp.einsum("bhncs,bhnse->bhnce", t_inv, v_beta)
    w = jnp.einsum("bhncs,bhnsd->bhncd", t_inv, k_beta * eg[..., None])
    qk = jnp.where(causal, jnp.einsum("bhncd,bhnsd->bhncs", q, k) * decay, 0.0)
    q_g = q * eg[..., None]
    g_last = g[..., -1]
    k_dec = k * jnp.exp(g_last[..., None] - g)[..., None]

    def step(state, xs):
        q_c, qk_c, u_c, w_c, k_c, gl = xs
        v_new = u_c - jnp.einsum("bhcd,bhde->bhce", w_c, state)
        o = jnp.einsum("bhcd,bhde->bhce", q_c, state) + jnp.einsum("bhcs,bhse->bhce", qk_c, v_new)
        state = state * jnp.exp(gl)[..., None, None] + jnp.einsum("bhcd,bhce->bhde", k_c, v_new)
        return state, o

    xs = tuple(jnp.moveaxis(t, 2, 0) for t in (q_g, qk, u, w, k_dec, g_last))
    state0 = jnp.zeros((b, h, dk, dv), f32)
    _, o = lax.scan(step, state0, xs)
    o = jnp.transpose(o, (1, 0, 3, 2, 4)).reshape(b, s, h, dv)
    return o.astype(out_dtype)


def hybrid_mixer(h, w_in, gdn_conv_w, a_log, dt_bias, gdn_norm_w, w_gdn_proj, sc_conv_w, w_sc_out, w_o):
    b, s, _ = h.shape
    p = h @ w_in
    qkv = jax.nn.silu(causal_depthwise_conv(p[..., Q_OFF:Z_OFF], gdn_conv_w))
    q = l2_normalize(qkv[..., Q_OFF:K_OFF].reshape(b, s, GDN_HEADS, GDN_DK))
    k = l2_normalize(qkv[..., K_OFF:V_OFF].reshape(b, s, GDN_HEADS, GDN_DK))
    v = qkv[..., V_OFF:Z_OFF].reshape(b, s, GDN_HEADS, GDN_DV)
    z = p[..., Z_OFF:A_OFF].reshape(b, s, GDN_HEADS, GDN_DV)
    g = -jnp.exp(a_log) * jax.nn.softplus(p[..., A_OFF:BETA_OFF] + dt_bias)
    beta = jax.nn.sigmoid(p[..., BETA_OFF:SCB_OFF])
    o = gated_delta_rule_chunked(q, k, v, g, beta)
    o = rms_norm(o, gdn_norm_w) * jax.nn.silu(z)
    y_a = o.reshape(b, s, V_W) @ w_gdn_proj
    gb = p[..., SCB_OFF:SCC_OFF]
    gc = p[..., SCC_OFF:SCX_OFF]
    xin = p[..., SCX_OFF:GA_OFF]
    y_b = (gb * causal_depthwise_conv(gc * xin, sc_conv_w)) @ w_sc_out
    gate_a = jax.nn.sigmoid(p[..., GA_OFF:GB_OFF])
    gate_b = jax.nn.sigmoid(p[..., GB_OFF:IN_COLS])
    return (gate_a * y_a + gate_b * y_b) @ w_o


def swiglu(h, w_ffn_in, w_ffn_out):
    gu = h @ w_ffn_in
    return (jax.nn.silu(gu[..., :D_FF]) * gu[..., D_FF:]) @ w_ffn_out


def _fwd_setup_inputs(seed: int = 0) -> dict:
    key = jax.random.key(seed)
    ks = jax.random.split(key, 22)
    nrm = jax.random.normal
    D = D_MODEL
    x = nrm(ks[0], (BATCH, SEQ, D), jnp.float32)
    c = nrm(ks[1], (BATCH, D), jnp.float32)
    w_ada = nrm(ks[2], (DEPTH, D, N_MOD * D), jnp.float32) * (0.5 * D ** -0.5)
    b_ada = 0.02 * nrm(ks[3], (DEPTH, N_MOD * D), jnp.float32)
    norm1_w = 1.0 + 0.02 * nrm(ks[4], (DEPTH, D), jnp.float32)
    w_in = nrm(ks[5], (DEPTH, D, IN_COLS), jnp.float32) * D ** -0.5
    gdn_conv_w = nrm(ks[6], (DEPTH, GDN_CONV, QKV_W), jnp.float32) * GDN_CONV ** -0.5
    gdn_a_log = jnp.log(jax.random.uniform(ks[7], (DEPTH, GDN_HEADS), jnp.float32, 1.0, 16.0))
    dt = jnp.exp(jax.random.uniform(ks[8], (DEPTH, GDN_HEADS), jnp.float32,
                                    float(np.log(1e-3)), float(np.log(1e-1))))
    gdn_dt_bias = jnp.log(jnp.expm1(dt))
    gdn_norm_w = 1.0 + 0.02 * nrm(ks[9], (DEPTH, GDN_DV), jnp.float32)
    w_gdn_proj = nrm(ks[10], (DEPTH, V_W, D), jnp.float32) * V_W ** -0.5
    sc_conv_w = nrm(ks[11], (DEPTH, SC_CONV, SC_WIDTH), jnp.float32) * SC_CONV ** -0.5
    w_sc_out = nrm(ks[12], (DEPTH, SC_WIDTH, D), jnp.float32) * SC_WIDTH ** -0.5
    w_o = nrm(ks[13], (DEPTH, D, D), jnp.float32) * D ** -0.5
    norm2_w = 1.0 + 0.02 * nrm(ks[14], (DEPTH, D), jnp.float32)
    w_ffn_in = nrm(ks[15], (DEPTH, D, 2 * D_FF), jnp.float32) * D ** -0.5
    w_ffn_out = nrm(ks[16], (DEPTH, D_FF, D), jnp.float32) * D_FF ** -0.5
    w_ada_f = nrm(ks[17], (D, 2 * D), jnp.float32) * (0.5 * D ** -0.5)
    b_ada_f = 0.02 * nrm(ks[18], (2 * D,), jnp.float32)
    normf_w = 1.0 + 0.02 * nrm(ks[19], (D,), jnp.float32)
    return {"x": x, "c": c, "w_ada": w_ada, "b_ada": b_ada, "norm1_w": norm1_w, "w_in": w_in,
            "gdn_conv_w": gdn_conv_w, "gdn_a_log": gdn_a_log, "gdn_dt_bias": gdn_dt_bias,
            "gdn_norm_w": gdn_norm_w, "w_gdn_proj": w_gdn_proj, "sc_conv_w": sc_conv_w,
            "w_sc_out": w_sc_out, "w_o": w_o, "norm2_w": norm2_w, "w_ffn_in": w_ffn_in,
            "w_ffn_out": w_ffn_out, "w_ada_f": w_ada_f, "b_ada_f": b_ada_f, "normf_w": normf_w}


def _fwd_reference(x, c, w_ada, b_ada, norm1_w, w_in, gdn_conv_w, gdn_a_log, gdn_dt_bias, gdn_norm_w,
              w_gdn_proj, sc_conv_w, w_sc_out, w_o, norm2_w, w_ffn_in, w_ffn_out, w_ada_f, b_ada_f,
              normf_w):
    c_act = jax.nn.silu(c)
    for l in range(DEPTH):
        mod = c_act @ w_ada[l] + b_ada[l]
        sh1, sc1, g1, sh2, sc2, g2 = jnp.split(mod, N_MOD, axis=-1)
        h = modulate(rms_norm(x, norm1_w[l]), sh1, sc1)
        mix = hybrid_mixer(h, w_in[l], gdn_conv_w[l], gdn_a_log[l], gdn_dt_bias[l], gdn_norm_w[l],
                           w_gdn_proj[l], sc_conv_w[l], w_sc_out[l], w_o[l])
        x = x + g1[:, None, :] * mix
        h = modulate(rms_norm(x, norm2_w[l]), sh2, sc2)
        x = x + g2[:, None, :] * swiglu(h, w_ffn_in[l], w_ffn_out[l])
    shf, scf = jnp.split(c_act @ w_ada_f + b_ada_f, 2, axis=-1)
    return modulate(rms_norm(x, normf_w), shf, scf)


import jax as _jax
import jax.numpy as _jnp

TWIN_FORMAT = 'train_step'
FWD_PARAMS = ['x', 'c', 'w_ada', 'b_ada', 'norm1_w', 'w_in', 'gdn_conv_w', 'gdn_a_log', 'gdn_dt_bias', 'gdn_norm_w', 'w_gdn_proj', 'sc_conv_w', 'w_sc_out', 'w_o', 'norm2_w', 'w_ffn_in', 'w_ffn_out', 'w_ada_f', 'b_ada_f', 'normf_w']
TWIN_WEIGHTS = ['w_ada', 'b_ada', 'norm1_w', 'w_in', 'gdn_conv_w', 'gdn_a_log', 'gdn_dt_bias', 'gdn_norm_w', 'w_gdn_proj', 'sc_conv_w', 'w_sc_out', 'w_o', 'norm2_w', 'w_ffn_in', 'w_ffn_out', 'w_ada_f', 'b_ada_f', 'normf_w']
TWIN_DIFF_INPUT = 'x'
TWIN_INPUTS = ['x', 'c', 'w_ada', 'b_ada', 'norm1_w', 'w_in', 'gdn_conv_w', 'gdn_a_log', 'gdn_dt_bias', 'gdn_norm_w', 'w_gdn_proj', 'sc_conv_w', 'w_sc_out', 'w_o', 'norm2_w', 'w_ffn_in', 'w_ffn_out', 'w_ada_f', 'b_ada_f', 'normf_w', 'loss_target', 'm_w_ada', 'm_b_ada', 'm_norm1_w', 'm_w_in', 'm_gdn_conv_w', 'm_gdn_a_log', 'm_gdn_dt_bias', 'm_gdn_norm_w', 'm_w_gdn_proj', 'm_sc_conv_w', 'm_w_sc_out', 'm_w_o', 'm_norm2_w', 'm_w_ffn_in', 'm_w_ffn_out', 'm_w_ada_f', 'm_b_ada_f', 'm_normf_w', 'v_w_ada', 'v_b_ada', 'v_norm1_w', 'v_w_in', 'v_gdn_conv_w', 'v_gdn_a_log', 'v_gdn_dt_bias', 'v_gdn_norm_w', 'v_w_gdn_proj', 'v_sc_conv_w', 'v_w_sc_out', 'v_w_o', 'v_norm2_w', 'v_w_ffn_in', 'v_w_ffn_out', 'v_w_ada_f', 'v_b_ada_f', 'v_normf_w']
TWIN_OUTPUTS = ['loss', 'grad_x', 'grad_w_ada', 'grad_b_ada', 'grad_norm1_w', 'grad_w_in', 'grad_gdn_conv_w', 'grad_gdn_a_log', 'grad_gdn_dt_bias', 'grad_gdn_norm_w', 'grad_w_gdn_proj', 'grad_sc_conv_w', 'grad_w_sc_out', 'grad_w_o', 'grad_norm2_w', 'grad_w_ffn_in', 'grad_w_ffn_out', 'grad_w_ada_f', 'grad_b_ada_f', 'grad_normf_w', 'delta_w_ada', 'delta_b_ada', 'delta_norm1_w', 'delta_w_in', 'delta_gdn_conv_w', 'delta_gdn_a_log', 'delta_gdn_dt_bias', 'delta_gdn_norm_w', 'delta_w_gdn_proj', 'delta_sc_conv_w', 'delta_w_sc_out', 'delta_w_o', 'delta_norm2_w', 'delta_w_ffn_in', 'delta_w_ffn_out', 'delta_w_ada_f', 'delta_b_ada_f', 'delta_normf_w', 'new_m_w_ada', 'new_m_b_ada', 'new_m_norm1_w', 'new_m_w_in', 'new_m_gdn_conv_w', 'new_m_gdn_a_log', 'new_m_gdn_dt_bias', 'new_m_gdn_norm_w', 'new_m_w_gdn_proj', 'new_m_sc_conv_w', 'new_m_w_sc_out', 'new_m_w_o', 'new_m_norm2_w', 'new_m_w_ffn_in', 'new_m_w_ffn_out', 'new_m_w_ada_f', 'new_m_b_ada_f', 'new_m_normf_w', 'new_v_w_ada', 'new_v_b_ada', 'new_v_norm1_w', 'new_v_w_in', 'new_v_gdn_conv_w', 'new_v_gdn_a_log', 'new_v_gdn_dt_bias', 'new_v_gdn_norm_w', 'new_v_w_gdn_proj', 'new_v_sc_conv_w', 'new_v_w_sc_out', 'new_v_w_o', 'new_v_norm2_w', 'new_v_w_ffn_in', 'new_v_w_ffn_out', 'new_v_w_ada_f', 'new_v_b_ada_f', 'new_v_normf_w']
TWIN_LEAF_KINDS = {'loss': 'loss', 'grad_x': 'grad_x', 'grad_w_ada': 'grad_w', 'grad_b_ada': 'grad_w', 'grad_norm1_w': 'grad_w', 'grad_w_in': 'grad_w', 'grad_gdn_conv_w': 'grad_w', 'grad_gdn_a_log': 'grad_w', 'grad_gdn_dt_bias': 'grad_w', 'grad_gdn_norm_w': 'grad_w', 'grad_w_gdn_proj': 'grad_w', 'grad_sc_conv_w': 'grad_w', 'grad_w_sc_out': 'grad_w', 'grad_w_o': 'grad_w', 'grad_norm2_w': 'grad_w', 'grad_w_ffn_in': 'grad_w', 'grad_w_ffn_out': 'grad_w', 'grad_w_ada_f': 'grad_w', 'grad_b_ada_f': 'grad_w', 'grad_normf_w': 'grad_w', 'delta_w_ada': 'delta_w', 'delta_b_ada': 'delta_w', 'delta_norm1_w': 'delta_w', 'delta_w_in': 'delta_w', 'delta_gdn_conv_w': 'delta_w', 'delta_gdn_a_log': 'delta_w', 'delta_gdn_dt_bias': 'delta_w', 'delta_gdn_norm_w': 'delta_w', 'delta_w_gdn_proj': 'delta_w', 'delta_sc_conv_w': 'delta_w', 'delta_w_sc_out': 'delta_w', 'delta_w_o': 'delta_w', 'delta_norm2_w': 'delta_w', 'delta_w_ffn_in': 'delta_w', 'delta_w_ffn_out': 'delta_w', 'delta_w_ada_f': 'delta_w', 'delta_b_ada_f': 'delta_w', 'delta_normf_w': 'delta_w', 'new_m_w_ada': 'new_m', 'new_m_b_ada': 'new_m', 'new_m_norm1_w': 'new_m', 'new_m_w_in': 'new_m', 'new_m_gdn_conv_w': 'new_m', 'new_m_gdn_a_log': 'new_m', 'new_m_gdn_dt_bias': 'new_m', 'new_m_gdn_norm_w': 'new_m', 'new_m_w_gdn_proj': 'new_m', 'new_m_sc_conv_w': 'new_m', 'new_m_w_sc_out': 'new_m', 'new_m_w_o': 'new_m', 'new_m_norm2_w': 'new_m', 'new_m_w_ffn_in': 'new_m', 'new_m_w_ffn_out': 'new_m', 'new_m_w_ada_f': 'new_m', 'new_m_b_ada_f': 'new_m', 'new_m_normf_w': 'new_m', 'new_v_w_ada': 'new_v', 'new_v_b_ada': 'new_v', 'new_v_norm1_w': 'new_v', 'new_v_w_in': 'new_v', 'new_v_gdn_conv_w': 'new_v', 'new_v_gdn_a_log': 'new_v', 'new_v_gdn_dt_bias': 'new_v', 'new_v_gdn_norm_w': 'new_v', 'new_v_w_gdn_proj': 'new_v', 'new_v_sc_conv_w': 'new_v', 'new_v_w_sc_out': 'new_v', 'new_v_w_o': 'new_v', 'new_v_norm2_w': 'new_v', 'new_v_w_ffn_in': 'new_v', 'new_v_w_ffn_out': 'new_v', 'new_v_w_ada_f': 'new_v', 'new_v_b_ada_f': 'new_v', 'new_v_normf_w': 'new_v'}


def _forward(args):
    return _fwd_reference(*[args[k] for k in FWD_PARAMS])


def _output_shape():
    out = _jax.eval_shape(lambda: _forward(_fwd_setup_inputs(0)))
    return out.shape, out.dtype

N_MICROBATCH = 1
ADAM_LR = 0.001
ADAM_B1 = 0.9
ADAM_B2 = 0.999
ADAM_EPS = 1e-08
ADAM_WD = 0.01
ADAM_STEP = 10
PER_EXAMPLE_BATCH_AXIS = {'x': 0, 'c': 0, 'loss_target': 0}
SHARED_INPUTS = []
_WEIGHT_DTYPES = {'w_ada': _jnp.float32, 'b_ada': _jnp.float32, 'norm1_w': _jnp.float32, 'w_in': _jnp.float32, 'gdn_conv_w': _jnp.float32, 'gdn_a_log': _jnp.float32, 'gdn_dt_bias': _jnp.float32, 'gdn_norm_w': _jnp.float32, 'w_gdn_proj': _jnp.float32, 'sc_conv_w': _jnp.float32, 'w_sc_out': _jnp.float32, 'w_o': _jnp.float32, 'norm2_w': _jnp.float32, 'w_ffn_in': _jnp.float32, 'w_ffn_out': _jnp.float32, 'w_ada_f': _jnp.float32, 'b_ada_f': _jnp.float32, 'normf_w': _jnp.float32}
MOMENT_SCALE = {'w_ada': 5.444509e-01, 'b_ada': 8.901595e-01, 'norm1_w': 1.028721e-01, 'w_in': 3.988724e-02, 'gdn_conv_w': 4.730716e-02, 'gdn_a_log': 1.123445e-01, 'gdn_dt_bias': 1.151810e-01, 'gdn_norm_w': 3.648350e-01, 'w_gdn_proj': 1.222482e-01, 'sc_conv_w': 5.488790e-02, 'w_sc_out': 5.380979e-02, 'w_o': 1.148274e-01, 'norm2_w': 7.122449e-02, 'w_ffn_in': 5.221560e-02, 'w_ffn_out': 1.056533e-01, 'w_ada_f': 8.738078e+00, 'b_ada_f': 2.332878e+01, 'normf_w': 3.681646e+01}


def _to_microbatches(a, axis):
    t = _jnp.moveaxis(a, axis, 0)
    t = t.reshape((N_MICROBATCH, t.shape[0] // N_MICROBATCH) + t.shape[1:])
    return _jnp.moveaxis(t, 1, axis + 1)


def setup_inputs(seed: int = 0) -> dict:
    inp = _fwd_setup_inputs(seed)
    key = _jax.random.fold_in(_jax.random.key(seed), 7919)
    shape, _ = _output_shape()
    out = dict(inp)
    out["loss_target"] = _jax.random.normal(_jax.random.fold_in(key, 0), shape, _jnp.float32)
    for i, name in enumerate(TWIN_WEIGHTS):
        w = inp[name].astype(_jnp.float32)
        if MOMENT_SCALE is None:
            s = _jnp.sqrt(_jnp.mean(_jnp.square(w)) + 1e-30)
        else:
            s = MOMENT_SCALE[name]
        km, kv = _jax.random.split(_jax.random.fold_in(key, i + 1))
        out[name] = w
        out["m_" + name] = s * _jax.random.normal(km, w.shape, _jnp.float32)
        out["v_" + name] = (s * s) * _jax.random.uniform(kv, w.shape, _jnp.float32, 0.5, 1.5)
    if N_MICROBATCH > 1:
        for name, axis in PER_EXAMPLE_BATCH_AXIS.items():
            out[name] = _to_microbatches(out[name], axis)
    return {'x': out['x'], 'c': out['c'], 'w_ada': out['w_ada'], 'b_ada': out['b_ada'], 'norm1_w': out['norm1_w'], 'w_in': out['w_in'], 'gdn_conv_w': out['gdn_conv_w'], 'gdn_a_log': out['gdn_a_log'], 'gdn_dt_bias': out['gdn_dt_bias'], 'gdn_norm_w': out['gdn_norm_w'], 'w_gdn_proj': out['w_gdn_proj'], 'sc_conv_w': out['sc_conv_w'], 'w_sc_out': out['w_sc_out'], 'w_o': out['w_o'], 'norm2_w': out['norm2_w'], 'w_ffn_in': out['w_ffn_in'], 'w_ffn_out': out['w_ffn_out'], 'w_ada_f': out['w_ada_f'], 'b_ada_f': out['b_ada_f'], 'normf_w': out['normf_w'], 'loss_target': out['loss_target'], 'm_w_ada': out['m_w_ada'], 'm_b_ada': out['m_b_ada'], 'm_norm1_w': out['m_norm1_w'], 'm_w_in': out['m_w_in'], 'm_gdn_conv_w': out['m_gdn_conv_w'], 'm_gdn_a_log': out['m_gdn_a_log'], 'm_gdn_dt_bias': out['m_gdn_dt_bias'], 'm_gdn_norm_w': out['m_gdn_norm_w'], 'm_w_gdn_proj': out['m_w_gdn_proj'], 'm_sc_conv_w': out['m_sc_conv_w'], 'm_w_sc_out': out['m_w_sc_out'], 'm_w_o': out['m_w_o'], 'm_norm2_w': out['m_norm2_w'], 'm_w_ffn_in': out['m_w_ffn_in'], 'm_w_ffn_out': out['m_w_ffn_out'], 'm_w_ada_f': out['m_w_ada_f'], 'm_b_ada_f': out['m_b_ada_f'], 'm_normf_w': out['m_normf_w'], 'v_w_ada': out['v_w_ada'], 'v_b_ada': out['v_b_ada'], 'v_norm1_w': out['v_norm1_w'], 'v_w_in': out['v_w_in'], 'v_gdn_conv_w': out['v_gdn_conv_w'], 'v_gdn_a_log': out['v_gdn_a_log'], 'v_gdn_dt_bias': out['v_gdn_dt_bias'], 'v_gdn_norm_w': out['v_gdn_norm_w'], 'v_w_gdn_proj': out['v_w_gdn_proj'], 'v_sc_conv_w': out['v_sc_conv_w'], 'v_w_sc_out': out['v_w_sc_out'], 'v_w_o': out['v_w_o'], 'v_norm2_w': out['v_norm2_w'], 'v_w_ffn_in': out['v_w_ffn_in'], 'v_w_ffn_out': out['v_w_ffn_out'], 'v_w_ada_f': out['v_w_ada_f'], 'v_b_ada_f': out['v_b_ada_f'], 'v_normf_w': out['v_normf_w']}


def _loss(weights, diff, rest, loss_target):
    with _jax.named_scope("forward"):
        args = {**rest, TWIN_DIFF_INPUT: diff, **{k: w.astype(_WEIGHT_DTYPES[k]) for k, w in weights.items()}}
        y = _forward(args)
    with _jax.named_scope("loss_head"):
        err = _jnp.square(y.astype(_jnp.float32) - loss_target)
        return 0.5 * _jnp.sum(_jnp.mean(err, axis=-1)) if err.ndim else 0.5 * err


def _adamw(w, g, m, v):
    m = ADAM_B1 * m + (1.0 - ADAM_B1) * g
    v = ADAM_B2 * v + (1.0 - ADAM_B2) * _jnp.square(g)
    m_hat = m / (1.0 - ADAM_B1 ** ADAM_STEP)
    v_hat = v / (1.0 - ADAM_B2 ** ADAM_STEP)
    delta = -ADAM_LR * (m_hat / (_jnp.sqrt(v_hat) + ADAM_EPS) + ADAM_WD * w)
    return delta, m, v


def reference(x, c, w_ada, b_ada, norm1_w, w_in, gdn_conv_w, gdn_a_log, gdn_dt_bias, gdn_norm_w, w_gdn_proj, sc_conv_w, w_sc_out, w_o, norm2_w, w_ffn_in, w_ffn_out, w_ada_f, b_ada_f, normf_w, loss_target, m_w_ada, m_b_ada, m_norm1_w, m_w_in, m_gdn_conv_w, m_gdn_a_log, m_gdn_dt_bias, m_gdn_norm_w, m_w_gdn_proj, m_sc_conv_w, m_w_sc_out, m_w_o, m_norm2_w, m_w_ffn_in, m_w_ffn_out, m_w_ada_f, m_b_ada_f, m_normf_w, v_w_ada, v_b_ada, v_norm1_w, v_w_in, v_gdn_conv_w, v_gdn_a_log, v_gdn_dt_bias, v_gdn_norm_w, v_w_gdn_proj, v_sc_conv_w, v_w_sc_out, v_w_o, v_norm2_w, v_w_ffn_in, v_w_ffn_out, v_w_ada_f, v_b_ada_f, v_normf_w):
    given = dict(x=x, c=c, w_ada=w_ada, b_ada=b_ada, norm1_w=norm1_w, w_in=w_in, gdn_conv_w=gdn_conv_w, gdn_a_log=gdn_a_log, gdn_dt_bias=gdn_dt_bias, gdn_norm_w=gdn_norm_w, w_gdn_proj=w_gdn_proj, sc_conv_w=sc_conv_w, w_sc_out=w_sc_out, w_o=w_o, norm2_w=norm2_w, w_ffn_in=w_ffn_in, w_ffn_out=w_ffn_out, w_ada_f=w_ada_f, b_ada_f=b_ada_f, normf_w=normf_w, loss_target=loss_target, m_w_ada=m_w_ada, m_b_ada=m_b_ada, m_norm1_w=m_norm1_w, m_w_in=m_w_in, m_gdn_conv_w=m_gdn_conv_w, m_gdn_a_log=m_gdn_a_log, m_gdn_dt_bias=m_gdn_dt_bias, m_gdn_norm_w=m_gdn_norm_w, m_w_gdn_proj=m_w_gdn_proj, m_sc_conv_w=m_sc_conv_w, m_w_sc_out=m_w_sc_out, m_w_o=m_w_o, m_norm2_w=m_norm2_w, m_w_ffn_in=m_w_ffn_in, m_w_ffn_out=m_w_ffn_out, m_w_ada_f=m_w_ada_f, m_b_ada_f=m_b_ada_f, m_normf_w=m_normf_w, v_w_ada=v_w_ada, v_b_ada=v_b_ada, v_norm1_w=v_norm1_w, v_w_in=v_w_in, v_gdn_conv_w=v_gdn_conv_w, v_gdn_a_log=v_gdn_a_log, v_gdn_dt_bias=v_gdn_dt_bias, v_gdn_norm_w=v_gdn_norm_w, v_w_gdn_proj=v_w_gdn_proj, v_sc_conv_w=v_sc_conv_w, v_w_sc_out=v_w_sc_out, v_w_o=v_w_o, v_norm2_w=v_norm2_w, v_w_ffn_in=v_w_ffn_in, v_w_ffn_out=v_w_ffn_out, v_w_ada_f=v_w_ada_f, v_b_ada_f=v_b_ada_f, v_normf_w=v_normf_w)
    weights = {n: given[n] for n in TWIN_WEIGHTS}
    shared = {n: given[n] for n in SHARED_INPUTS}
    per_example = {n: given[n] for n in ['x', 'c']}
    grad_fn = _jax.value_and_grad(_loss, argnums=(0, 1))

    def one_microbatch(ex, loss_target):
        ex = dict(ex)
        diff = ex.pop(TWIN_DIFF_INPUT)
        return grad_fn(weights, diff, {**shared, **ex}, loss_target)

    if N_MICROBATCH == 1:
        loss, (grad_w, grad_x) = one_microbatch(per_example, given["loss_target"])
    else:
        def body(carry, xs):
            loss_sum, grad_sum = carry
            l_k, (gw_k, gx_k) = one_microbatch(xs[0], xs[1])
            with _jax.named_scope("update"):
                return (loss_sum + l_k, _jax.tree.map(_jnp.add, grad_sum, gw_k)), gx_k

        init = (_jnp.zeros((), _jnp.float32), _jax.tree.map(_jnp.zeros_like, weights))
        (loss, grad_w), grad_x = _jax.lax.scan(body, init, (per_example, given["loss_target"]))
    with _jax.named_scope("update"):
        delta_w, new_m, new_v = {}, {}, {}
        for n in TWIN_WEIGHTS:
            delta_w[n], new_m[n], new_v[n] = _adamw(weights[n], grad_w[n], given["m_" + n], given["v_" + n])
    return (loss, grad_x, *[grad_w[n] for n in TWIN_WEIGHTS], *[delta_w[n] for n in TWIN_WEIGHTS],
            *[new_m[n] for n in TWIN_WEIGHTS], *[new_v[n] for n in TWIN_WEIGHTS])
```

```python
import functools

import jax
import jax.numpy as jnp
from jax import lax
from jax.experimental import pallas as pl
from jax.experimental.pallas import tpu as pltpu

F32 = jnp.float32
MXU_DTYPE = jnp.bfloat16
NDEV = 8
CHUNK = 64
HEAD = 128
LANE = 128
EPS = 1e-6
ADAM_LR, ADAM_B1, ADAM_B2, ADAM_EPS, ADAM_WD, ADAM_STEP = 0.001, 0.9, 0.999, 1e-08, 0.01, 10
VMEM_LIMIT = 48 * 1024 * 1024
MESH_IDS = pl.DeviceIdType.MESH
HIGHEST = lax.Precision.HIGHEST


def _tile(n, cands=(512, 256, 128)):
    for c in cands:
        if n % c == 0:
            return c
    return n


def _cparams(*sem):
    return pltpu.CompilerParams(dimension_semantics=sem, vmem_limit_bytes=VMEM_LIMIT)


def _mm(a, b, *, ta=False, tb=False, add=None, out_dtype=F32, name):
    m, k = (a.shape[1], a.shape[0]) if ta else a.shape
    n = b.shape[0] if tb else b.shape[1]
    tm, tn = _tile(m), _tile(n)
    tk = k if k <= 1024 else _tile(k, (512,))
    nk = k // tk
    dims = (((0 if ta else 1,), (1 if tb else 0,)), ((), ()))
    has_add = add is not None

    def body(*refs):
        a_ref, b_ref = refs[0], refs[1]
        add_ref = refs[2] if has_add else None
        o_ref = refs[3] if has_add else refs[2]
        part = lax.dot_general(a_ref[...].astype(MXU_DTYPE), b_ref[...].astype(MXU_DTYPE), dims,
                               preferred_element_type=F32)

        def finish(acc):
            if has_add:
                acc = acc + add_ref[...]
            o_ref[...] = acc.astype(o_ref.dtype)

        if nk == 1:
            finish(part)
        else:
            acc_ref = refs[-1]
            kk = pl.program_id(2)

            @pl.when(kk == 0)
            def _():
                acc_ref[...] = part

            @pl.when(kk > 0)
            def _():
                acc_ref[...] += part

            @pl.when(kk == nk - 1)
            def _():
                finish(acc_ref[...])

    a_spec = pl.BlockSpec((tk, tm), lambda i, j, kk: (kk, i)) if ta else pl.BlockSpec((tm, tk), lambda i, j, kk: (i, kk))
    b_spec = pl.BlockSpec((tn, tk), lambda i, j, kk: (j, kk)) if tb else pl.BlockSpec((tk, tn), lambda i, j, kk: (kk, j))
    o_spec = pl.BlockSpec((tm, tn), lambda i, j, kk: (i, j))
    in_specs = [a_spec, b_spec] + ([o_spec] if has_add else [])
    args = [a, b] + ([add] if has_add else [])
    return pl.pallas_call(
        body, name=name, grid=(m // tm, n // tn, nk), in_specs=in_specs, out_specs=o_spec,
        out_shape=jax.ShapeDtypeStruct((m, n), out_dtype),
        scratch_shapes=[pltpu.VMEM((tm, tn), F32)] if nk > 1 else [],
        compiler_params=_cparams("parallel", "parallel", "arbitrary"),
    )(*args)


def _with_off(xs):
    return [x if isinstance(x, tuple) else (x, 0) for x in xs]


def _spec(kind, arr, off, ts, wb):
    w = arr.shape[-1] if wb is None else wb
    col = (lambda j: 0) if wb is None else functools.partial(lambda j, o: o + j, o=off)
    if kind == "tok":
        return pl.BlockSpec((None, ts, w), lambda j, b, i: (b, i, col(j)))
    if kind == "bat":
        return pl.BlockSpec((None, 1, w), lambda j, b, i: (b, 0, col(j)))
    if off is None:
        return pl.BlockSpec(arr.shape, lambda j, b, i: (0, 0))
    return pl.BlockSpec((arr.shape[0], w), lambda j, b, i: (0, col(j)))


def _in_specs(toks, bats, pars, cots, ts, wb):
    return ([_spec("tok", a, o, ts, wb) for a, o in toks] + [_spec("bat", a, o, ts, wb) for a, o in bats]
            + [_spec("par", a, o, ts, wb) for a, o in pars] + [_spec("tok", a, o, ts, wb) for a, o in cots])


def _tok_fwd(fn, toks, bats, pars, outs, *, name, ts, wb=None, cols=1):
    toks, bats, pars = _with_off(toks), _with_off(bats), _with_off(pars)
    bl, s, _ = toks[0][0].shape
    ts = min(ts, s)
    n_in = len(toks) + len(bats) + len(pars)

    def body(*refs):
        res = fn(*[r[...].astype(F32) for r in refs[:n_in]])
        for r, val in zip(refs[n_in:], res):
            r[...] = val.astype(r.dtype)

    out_specs = [pl.BlockSpec((None, ts, w if wb is None else wb), lambda j, b, i: (b, i, j)) for w, _ in outs]
    return pl.pallas_call(
        body, name=name, grid=(cols, bl, s // ts), in_specs=_in_specs(toks, bats, pars, [], ts, wb),
        out_specs=out_specs, out_shape=[jax.ShapeDtypeStruct((bl, s, w), dt) for w, dt in outs],
        compiler_params=_cparams("parallel", "parallel", "parallel"),
    )(*[a for a, _ in toks + bats + pars])


def _accumulate(ref, val, first):
    @pl.when(first)
    def _():
        ref[...] = val

    @pl.when(jnp.logical_not(first))
    def _():
        ref[...] += val


def _tok_bwd(fn, toks, bats, pars, cots, need, *, name, ts, wb=None, cols=1, tok_dtype=F32, loss=False):
    toks, bats, pars, cots = _with_off(toks), _with_off(bats), _with_off(pars), _with_off(cots)
    bl, s, _ = toks[0][0].shape
    ts = min(ts, s)
    nt, nb, npar, nc = len(toks), len(bats), len(pars), len(cots)
    n_in = nt + nb + npar

    def body(*refs):
        j, b, i = pl.program_id(0), pl.program_id(1), pl.program_id(2)
        outs, vjp = jax.vjp(fn, *[r[...].astype(F32) for r in refs[:n_in]])
        o = n_in + nc
        if loss:
            ct = (jnp.ones_like(outs[0]),)
            tot = jnp.broadcast_to(jnp.sum(outs[0], keepdims=True), (1, LANE))
            _accumulate(refs[o], tot, jnp.logical_and(b == 0, i == 0))
            o += 1
        else:
            ct = tuple(r[...].astype(F32) for r in refs[n_in:n_in + nc])
        grads = vjp(ct)
        for t in range(nt):
            if need[t]:
                refs[o][...] = grads[t].astype(refs[o].dtype)
                o += 1
        for t in range(nb):
            _accumulate(refs[o], grads[nt + t], i == 0)
            o += 1
        for t in range(npar):
            first = jnp.logical_and(b == 0, i == 0)
            if pars[t][1] is None:
                first = jnp.logical_and(first, j == 0)
            _accumulate(refs[o], grads[nt + nb + t], first)
            o += 1

    full = lambda arr: arr.shape[-1] if wb is None else wb * cols
    blk = lambda arr: arr.shape[-1] if wb is None else wb
    out_specs, out_shape = [], []
    if loss:
        out_specs.append(pl.BlockSpec((1, LANE), lambda j, b, i: (0, 0)))
        out_shape.append(jax.ShapeDtypeStruct((1, LANE), F32))
    for t in range(nt):
        if need[t]:
            out_specs.append(pl.BlockSpec((None, ts, blk(toks[t][0])), lambda j, b, i: (b, i, j)))
            dt = tok_dtype[t] if isinstance(tok_dtype, (list, tuple)) else tok_dtype
            out_shape.append(jax.ShapeDtypeStruct((bl, s, full(toks[t][0])), dt))
    for arr, _ in bats:
        out_specs.append(pl.BlockSpec((None, 1, blk(arr)), lambda j, b, i: (b, 0, j)))
        out_shape.append(jax.ShapeDtypeStruct((bl, 1, full(arr)), F32))
    for arr, off in pars:
        if off is None:
            out_specs.append(pl.BlockSpec(arr.shape, lambda j, b, i: (0, 0)))
            out_shape.append(jax.ShapeDtypeStruct(arr.shape, F32))
        else:
            out_specs.append(pl.BlockSpec((arr.shape[0], blk(arr)), lambda j, b, i: (0, j)))
            out_shape.append(jax.ShapeDtypeStruct((arr.shape[0], full(arr)), F32))
    res = list(pl.pallas_call(
        body, name=name, grid=(cols, bl, s // ts), in_specs=_in_specs(toks, bats, pars, cots, ts, wb),
        out_specs=out_specs, out_shape=out_shape, compiler_params=_cparams("arbitrary", "arbitrary", "arbitrary"),
    )(*[a for a, _ in toks + bats + pars + cots]))
    tot = res.pop(0) if loss else None
    dtoks = [res.pop(0) if need[t] else None for t in range(nt)]
    dbats = [res.pop(0) for _ in range(nb)]
    dpars = [res.pop(0) for _ in range(npar)]
    return (tot, dtoks, dbats, dpars) if loss else (dtoks, dbats, dpars)


def _silu(x):
    return x * jax.nn.sigmoid(x)


def _rms(x, w):
    return x * lax.rsqrt(jnp.mean(x * x, axis=-1, keepdims=True) + EPS) * w


def _f_norm_mod(x, shift, scale, w):
    return (_rms(x, w) * (1.0 + scale) + shift,)


def _f_norm_mod_skip(x, shift, scale, w):
    return _rms(x, w) * (1.0 + scale) + shift, x


def _f_res_norm_mod(x, mix, gate, shift, scale, w):
    x2 = x + gate * mix
    return x2, _rms(x2, w) * (1.0 + scale) + shift


def _f_gates(p, a_log, dt_bias, *, heads):
    z = p + dt_bias
    g = -jnp.exp(a_log) * (jnp.maximum(z, 0.0) + jnp.log1p(jnp.exp(jnp.minimum(z, -z))))
    lane = lax.broadcasted_iota(jnp.int32, p.shape, 1)
    return (jnp.where(lane < heads, g, jax.nn.sigmoid(p)),)


def _f_gdn_out(o, z, w):
    return (_rms(o, w) * _silu(z),)


def _f_merge(ga, gb, ya, yb):
    return (jax.nn.sigmoid(ga) * ya + jax.nn.sigmoid(gb) * yb,)


def _f_swiglu(a, b):
    return (_silu(a) * b,)


def _f_loss(x2, ff, tgt, gate, shift, scale, w):
    y = _rms(x2 + gate * ff, w) * (1.0 + scale) + shift
    return (0.5 * jnp.mean(jnp.square(y - tgt), axis=-1, keepdims=True),)


def _shift_down(x, s):
    if s == 0:
        return x
    row = lax.broadcasted_iota(jnp.int32, x.shape, 0)
    return jnp.where(row >= s, pltpu.roll(x, s, 0), 0.0)


def _shift_up(x, s):
    if s == 0:
        return x
    n = x.shape[0]
    row = lax.broadcasted_iota(jnp.int32, x.shape, 0)
    return jnp.where(row < n - s, pltpu.roll(x, n - s, 0), 0.0)


def _conv(x, w):
    width = w.shape[0]
    acc = w[width - 1:width, :] * x
    for j in range(width - 1):
        acc = acc + w[j:j + 1, :] * _shift_down(x, width - 1 - j)
    return acc


def _conv_bwd(dy, x, w, dw_ref, first):
    width = w.shape[0]
    dx = w[width - 1:width, :] * dy
    for j in range(width - 1):
        dx = dx + w[j:j + 1, :] * _shift_up(dy, width - 1 - j)
    for j in range(width):
        row = jnp.sum(dy * _shift_down(x, width - 1 - j), axis=0, keepdims=True)
        _accumulate(dw_ref.at[j:j + 1, :], row, first)
    return dx


def _qkv_act(xc, is_v, scale):
    a = _silu(xc)
    nrm = a * lax.rsqrt(jnp.sum(a * a, axis=-1, keepdims=True) + EPS) * scale
    return jnp.where(is_v, a, nrm)


def _qkv_consts(j, heads):
    is_v = j >= 2 * heads
    scale = jnp.where(j < heads, HEAD ** -0.5, 1.0).astype(F32)
    return is_v, scale


def _qkv_fwd(p, w, heads, name):
    bl, s, w3 = p.shape

    def body(p_ref, w_ref, o_ref):
        is_v, scale = _qkv_consts(pl.program_id(0), heads)
        o_ref[...] = _qkv_act(_conv(p_ref[...], w_ref[...]), is_v, scale)

    blk = pl.BlockSpec((None, s, HEAD), lambda j, b: (b, 0, j))
    return pl.pallas_call(
        body, name=name, grid=(w3 // HEAD, bl), in_specs=[blk, pl.BlockSpec((w.shape[0], HEAD), lambda j, b: (0, j))],
        out_specs=blk, out_shape=jax.ShapeDtypeStruct(p.shape, F32), compiler_params=_cparams("parallel", "parallel"),
    )(p, w)


def _qkv_bwd(p, w, dout, heads, name):
    bl, s, w3 = p.shape

    def body(p_ref, w_ref, d_ref, dp_ref, dw_ref):
        is_v, scale = _qkv_consts(pl.program_id(0), heads)
        x, wv = p_ref[...], w_ref[...]
        _, vjp = jax.vjp(lambda xc: _qkv_act(xc, is_v, scale), _conv(x, wv))
        (dxc,) = vjp(d_ref[...])
        dp_ref[...] = _conv_bwd(dxc, x, wv, dw_ref, pl.program_id(1) == 0).astype(dp_ref.dtype)

    blk = pl.BlockSpec((None, s, HEAD), lambda j, b: (b, 0, j))
    wblk = pl.BlockSpec((w.shape[0], HEAD), lambda j, b: (0, j))
    return pl.pallas_call(
        body, name=name, grid=(w3 // HEAD, bl), in_specs=[blk, wblk, blk], out_specs=[blk, wblk],
        out_shape=[jax.ShapeDtypeStruct(p.shape, MXU_DTYPE), jax.ShapeDtypeStruct(w.shape, F32)],
        compiler_params=_cparams("arbitrary", "arbitrary"),
    )(p, w, dout)


def _sc_specs(p, w):
    bl, s, w3 = p.shape
    nblk = w3 // 3 // LANE
    sec = lambda k: pl.BlockSpec((None, s, LANE), functools.partial(lambda j, b, k: (b, 0, k * nblk + j), k=k))
    return nblk, [sec(0), sec(1), sec(2)], pl.BlockSpec((w.shape[0], LANE), lambda j, b: (0, j)), \
        pl.BlockSpec((None, s, LANE), lambda j, b: (b, 0, j))


def _sc_fwd(p, w, name):
    bl, s, w3 = p.shape
    nblk, secs, wblk, oblk = _sc_specs(p, w)

    def body(b_ref, c_ref, x_ref, w_ref, o_ref):
        o_ref[...] = (b_ref[...] * _conv(c_ref[...] * x_ref[...], w_ref[...])).astype(o_ref.dtype)

    return pl.pallas_call(
        body, name=name, grid=(nblk, bl), in_specs=secs + [wblk], out_specs=oblk,
        out_shape=jax.ShapeDtypeStruct((bl, s, w3 // 3), MXU_DTYPE), compiler_params=_cparams("parallel", "parallel"),
    )(p, p, p, w)


def _sc_bwd(p, w, dout, name):
    bl, s, w3 = p.shape
    nblk, secs, wblk, oblk = _sc_specs(p, w)

    def body(b_ref, c_ref, x_ref, w_ref, d_ref, db_ref, dc_ref, dx_ref, dw_ref):
        gb, gc, xin, wv, d = b_ref[...], c_ref[...], x_ref[...], w_ref[...], d_ref[...]
        u = gc * xin
        db_ref[...] = (d * _conv(u, wv)).astype(db_ref.dtype)
        du = _conv_bwd(d * gb, u, wv, dw_ref, pl.program_id(1) == 0)
        dc_ref[...] = (du * xin).astype(dc_ref.dtype)
        dx_ref[...] = (du * gc).astype(dx_ref.dtype)

    act = jax.ShapeDtypeStruct((bl, s, w3 // 3), MXU_DTYPE)
    return pl.pallas_call(
        body, name=name, grid=(nblk, bl), in_specs=secs + [wblk, oblk], out_specs=[oblk, oblk, oblk, wblk],
        out_shape=[act, act, act, jax.ShapeDtypeStruct(w.shape, F32)], compiler_params=_cparams("arbitrary", "arbitrary"),
    )(p, p, p, w, dout)


def _bdot(a, b, ca, cb):
    return lax.dot_general(a.astype(MXU_DTYPE), b.astype(MXU_DTYPE), (((ca,), (cb,)), ((), ())),
                           preferred_element_type=F32)


def _hdot(a, b):
    return lax.dot_general(a, b, (((1,), (0,)), ((), ())), precision=HIGHEST, preferred_element_type=F32)


def _lane_col(x, idx):
    lane = lax.broadcasted_iota(jnp.int32, x.shape, 1)
    return jnp.sum(jnp.where(lane == idx, x, 0.0), axis=1, keepdims=True)


def _chunk_masks():
    r = lax.broadcasted_iota(jnp.int32, (CHUNK, CHUNK), 0)
    c = lax.broadcasted_iota(jnp.int32, (CHUNK, CHUNK), 1)
    return r == c, r >= c, r > c


def _tri_inv(low, eye):
    x = -low
    p = jnp.where(eye, 1.0, 0.0) + x
    span = 2
    while span < CHUNK:
        x = _hdot(x, x)
        p = p + _hdot(p, x)
        span *= 2
    return p


def _gdn_pre(q, k, v, gc, beta, masks):
    eye, causal, strict = masks
    gc_row = jnp.sum(jnp.where(eye, gc, 0.0), axis=0, keepdims=True)
    decay = jnp.where(causal, jnp.exp(jnp.where(causal, gc - gc_row, 0.0)), 0.0)
    eg = jnp.exp(gc)
    gl = gc[CHUNK - 1:CHUNK, :]
    kb, vb = k * beta, v * beta
    low = jnp.where(strict, _bdot(kb, k, 1, 1) * decay, 0.0)
    qk = jnp.where(causal, _bdot(q, k, 1, 1) * decay, 0.0)
    rest = jnp.exp(gl - gc)
    return dict(decay=decay, eg=eg, gl=gl, kb=kb, vb=vb, kbe=kb * eg, low=low, qk=qk, qg=q * eg, rest=rest, kdec=k * rest)


def _gdn_specs(qkv, gbeta, heads, rev):
    bl, s, w3 = qkv.shape
    d, n = w3 // 3, s // CHUNK
    at = (lambda c: n - 1 - c) if rev else (lambda c: c)
    assert d == heads * HEAD
    sec = pl.BlockSpec((None, CHUNK, w3), lambda b, c: (b, at(c), 0))
    gspec = pl.BlockSpec((None, CHUNK, LANE), lambda b, c: (b, at(c), 0))
    sspec = pl.BlockSpec((None, None, heads, HEAD, HEAD), lambda b, c: (b, at(c), 0, 0, 0))
    tspec = pl.BlockSpec((None, None, heads, CHUNK, CHUNK), lambda b, c: (b, at(c), 0, 0, 0))
    return bl, s, d, n, sec, gspec, sspec, tspec


def _gdn_fwd(qkv, gbeta, heads, name):
    bl, s, d, n, sec, gspec, sspec, tspec = _gdn_specs(qkv, gbeta, heads, False)

    def body(x_ref, g_ref, o_ref, s_ref, t_ref, st_ref):
        @pl.when(pl.program_id(1) == 0)
        def _():
            st_ref[...] = jnp.zeros_like(st_ref)

        masks = _chunk_masks()
        eye, causal, _ = masks
        gblk = g_ref[...]
        gc_all = _hdot(jnp.where(causal, 1.0, 0.0), gblk)
        for h in range(heads):
            sl = slice(h * HEAD, (h + 1) * HEAD)
            st = st_ref[h]
            q, k, v = (x_ref[:, sec * d + h * HEAD:sec * d + (h + 1) * HEAD] for sec in range(3))
            pre = _gdn_pre(q, k, v, _lane_col(gc_all, h), _lane_col(gblk, heads + h), masks)
            t = _tri_inv(pre["low"], eye)
            u, w = _bdot(t, pre["vb"], 1, 0), _bdot(t, pre["kbe"], 1, 0)
            vnew = u - _bdot(w, st, 1, 0)
            o_ref[:, sl] = _bdot(pre["qg"], st, 1, 0) + _bdot(pre["qk"], vnew, 1, 0)
            s_ref[h] = st
            t_ref[h] = t
            st_ref[h] = st * jnp.exp(pre["gl"]) + _bdot(pre["kdec"], vnew, 0, 0)

    return pl.pallas_call(
        body, name=name, grid=(bl, n), in_specs=[sec, gspec],
        out_specs=[pl.BlockSpec((None, CHUNK, d), lambda b, c: (b, c, 0)), sspec, tspec],
        out_shape=[jax.ShapeDtypeStruct((bl, s, d), F32), jax.ShapeDtypeStruct((bl, n, heads, HEAD, HEAD), F32),
                   jax.ShapeDtypeStruct((bl, n, heads, CHUNK, CHUNK), F32)],
        scratch_shapes=[pltpu.VMEM((heads, HEAD, HEAD), F32)], compiler_params=_cparams("arbitrary", "arbitrary"),
    )(qkv, gbeta)


def _gdn_bwd(qkv, gbeta, dout, s_all, t_all, heads, name):
    bl, s, d, n, sec, gspec, sspec, tspec = _gdn_specs(qkv, gbeta, heads, True)
    ospec = pl.BlockSpec((None, CHUNK, d), lambda b, c: (b, n - 1 - c, 0))

    def body(x_ref, g_ref, do_ref, s_ref, t_ref, dx_ref, dg_ref, ds_ref):
        @pl.when(pl.program_id(1) == 0)
        def _():
            ds_ref[...] = jnp.zeros_like(ds_ref)

        masks = _chunk_masks()
        eye, causal, strict = masks
        gblk = g_ref[...]
        gc_all = _hdot(jnp.where(causal, 1.0, 0.0), gblk)
        lane = lax.broadcasted_iota(jnp.int32, gblk.shape, 1)
        last_row = lax.broadcasted_iota(jnp.int32, (CHUNK, 1), 0) == CHUNK - 1
        rowsum = lambda a: jnp.sum(a, axis=1, keepdims=True)
        dgc_all = jnp.zeros_like(gblk)
        dbeta_all = jnp.zeros_like(gblk)
        for h in range(heads):
            sl = slice(h * HEAD, (h + 1) * HEAD)
            qs, ks, vs = (slice(sec * d + h * HEAD, sec * d + (h + 1) * HEAD) for sec in range(3))
            q, k, v, do = x_ref[:, qs], x_ref[:, ks], x_ref[:, vs], do_ref[:, sl]
            beta = _lane_col(gblk, heads + h)
            st, t, dsn = s_ref[h], t_ref[h], ds_ref[h]
            pre = _gdn_pre(q, k, v, _lane_col(gc_all, h), beta, masks)
            decay, eg, kb, vb, kbe, low, qk, qg, kdec = (pre[x] for x in ("decay", "eg", "kb", "vb", "kbe", "low", "qk", "qg", "kdec"))
            egl = jnp.exp(pre["gl"])
            u, w = _bdot(t, vb, 1, 0), _bdot(t, kbe, 1, 0)
            vnew = u - _bdot(w, st, 1, 0)
            dkdec = _bdot(vnew, dsn, 1, 1)
            dvnew = _bdot(kdec, dsn, 1, 0) + _bdot(qk, do, 0, 0)
            dgl = jnp.sum(dsn * st, keepdims=True) * egl
            dqg = _bdot(do, st, 1, 1)
            dqk = jnp.where(causal, _bdot(do, vnew, 1, 1), 0.0)
            dw = -_bdot(dvnew, st, 1, 1)
            ds_ref[h] = dsn * egl + _bdot(qg, do, 0, 0) - _bdot(w, dvnew, 0, 0)
            dt = _bdot(dvnew, vb, 1, 1) + _bdot(dw, kbe, 1, 1)
            dvb, dkbe = _bdot(t, dvnew, 0, 0), _bdot(t, dw, 0, 0)
            dlow = -jnp.where(strict, _bdot(t, _bdot(dt, t, 1, 1), 0, 0), 0.0)
            da, db = dlow * decay, dqk * decay
            m = dlow * low + dqk * qk
            kdk = dkdec * kdec
            col_of_m = jnp.sum(jnp.where(eye, jnp.sum(m, axis=0, keepdims=True), 0.0), axis=1, keepdims=True)
            dgc = rowsum(m) - col_of_m + rowsum(dqg * qg) + rowsum(dkbe * kbe) - rowsum(kdk)
            dgc = dgc + jnp.where(last_row, dgl + jnp.sum(kdk, keepdims=True), 0.0)
            dkb = _bdot(da, k, 1, 0) + dkbe * eg
            dx_ref[:, ks] = _bdot(da, kb, 0, 0) + _bdot(db, q, 0, 0) + dkdec * pre["rest"] + dkb * beta
            dx_ref[:, qs] = _bdot(db, k, 1, 0) + dqg * eg
            dx_ref[:, vs] = dvb * beta
            dbeta = rowsum(dkb * k) + rowsum(dvb * v)
            dgc_all = dgc_all + jnp.where(lane == h, dgc, 0.0)
            dbeta_all = dbeta_all + jnp.where(lane == heads + h, dbeta, 0.0)
        upper = jnp.where(jnp.logical_or(eye, jnp.logical_not(causal)), 1.0, 0.0)
        dg_ref[...] = _hdot(upper, dgc_all) + dbeta_all

    return pl.pallas_call(
        body, name=name, grid=(bl, n), in_specs=[sec, gspec, ospec, sspec, tspec], out_specs=[sec, gspec],
        out_shape=[jax.ShapeDtypeStruct(qkv.shape, F32), jax.ShapeDtypeStruct((bl, s, LANE), F32)],
        scratch_shapes=[pltpu.VMEM((heads, HEAD, HEAD), F32)], compiler_params=_cparams("arbitrary", "arbitrary"),
    )(qkv, gbeta, dout, s_all, t_all)


def _position():
    return lax.axis_index("x"), lax.axis_index("y"), lax.axis_index("c")


def _all_gather(x, *, name, hbm):
    space = pltpu.HBM if hbm else pltpu.VMEM

    def body(x_ref, out_ref, send_sems, recv_sems, local_sem):
        ax, ay, ac = _position()
        me, sibling = (ax, ay, ac), (ax, ay, 1 - ac)
        chips = [(1 - ax, ay), (ax, 1 - ay), (1 - ax, 1 - ay)]

        def slot(px, py, pc):
            return out_ref.at[4 * px + 2 * py + pc]

        def copy(k, block, to, src=None):
            return pltpu.make_async_remote_copy(
                src_ref=slot(*block) if src is None else src, dst_ref=slot(*block), send_sem=send_sems.at[k],
                recv_sem=recv_sems.at[k], device_id=to, device_id_type=MESH_IDS)

        mine = pltpu.make_async_copy(x_ref, slot(*me), local_sem)
        mine.start()
        first = [copy(0, me, sibling, src=x_ref)] + [copy(1 + j, me, (*chip, ac), src=x_ref) for j, chip in enumerate(chips)]
        for cp in first:
            cp.start()
        passed = [copy(4 + j, (*chip, ac), sibling) for j, chip in enumerate(chips)]
        for j, chip in enumerate(chips):
            copy(1 + j, (*chip, ac), me).wait_recv()
            passed[j].start()
        copy(0, sibling, me).wait_recv()
        for j, chip in enumerate(chips):
            copy(4 + j, (*chip, 1 - ac), me).wait_recv()
        for cp in first + passed:
            cp.wait_send()
        mine.wait()

    return pl.pallas_call(
        body, name=name, out_shape=jax.ShapeDtypeStruct((NDEV,) + x.shape, x.dtype),
        in_specs=[pl.BlockSpec(memory_space=space)], out_specs=pl.BlockSpec(memory_space=space),
        scratch_shapes=[pltpu.SemaphoreType.DMA((7,)), pltpu.SemaphoreType.DMA((7,)), pltpu.SemaphoreType.DMA],
    )(x)


def _exchange_in_chip(g, name):
    def body(g_ref, recv_ref, send_sems, recv_sems):
        ax, ay, ac = _position()
        copies = [pltpu.make_async_remote_copy(
            src_ref=g_ref.at[2 * q + (1 - ac)], dst_ref=recv_ref.at[q], send_sem=send_sems.at[q], recv_sem=recv_sems.at[q],
            device_id=(ax, ay, 1 - ac), device_id_type=MESH_IDS) for q in range(4)]
        for cp in copies:
            cp.start()
        for cp in copies:
            cp.wait_recv()
        for cp in copies:
            cp.wait_send()

    hbm = pl.BlockSpec(memory_space=pltpu.HBM)
    return pl.pallas_call(
        body, name=name, out_shape=jax.ShapeDtypeStruct((4,) + g.shape[1:], g.dtype), in_specs=[hbm], out_specs=hbm,
        scratch_shapes=[pltpu.SemaphoreType.DMA((4,)), pltpu.SemaphoreType.DMA((4,))],
    )(g)


def _exchange_chips(s1, name):
    def body(s_ref, recv_ref, send_sems, recv_sems):
        ax, ay, ac = _position()
        chips = [(1 - ax, ay), (ax, 1 - ay), (1 - ax, 1 - ay)]
        copies = [pltpu.make_async_remote_copy(
            src_ref=s_ref.at[2 * cx + cy], dst_ref=recv_ref.at[r], send_sem=send_sems.at[r], recv_sem=recv_sems.at[r],
            device_id=(cx, cy, ac), device_id_type=MESH_IDS) for r, (cx, cy) in enumerate(chips)]
        for cp in copies:
            cp.start()
        for cp in copies:
            cp.wait_recv()
        for cp in copies:
            cp.wait_send()

    hbm = pl.BlockSpec(memory_space=pltpu.HBM)
    return pl.pallas_call(
        body, name=name, out_shape=jax.ShapeDtypeStruct((3,) + s1.shape[1:], s1.dtype), in_specs=[hbm], out_specs=hbm,
        scratch_shapes=[pltpu.SemaphoreType.DMA((3,)), pltpu.SemaphoreType.DMA((3,))],
    )(s1)


def _sum_in_chip(g, recv, parity, name):
    _, r, w = g.shape
    tr = _tile(r, (256, 128))

    def body(p_ref, g_ref, r_ref, o_ref):
        o_ref[...] = g_ref[...] + r_ref[...]

    blk = pl.BlockSpec((None, tr, w), lambda q, i, p: (q, i, 0))
    grid_spec = pltpu.PrefetchScalarGridSpec(
        num_scalar_prefetch=1, grid=(4, r // tr),
        in_specs=[pl.BlockSpec((None, tr, w), lambda q, i, p: (2 * q + p[0], i, 0)), blk], out_specs=blk)
    return pl.pallas_call(body, name=name, grid_spec=grid_spec, out_shape=jax.ShapeDtypeStruct((4, r, w), F32),
                          compiler_params=_cparams("parallel", "parallel"))(parity, g, recv)


def _sum_chips(s1, recv, chip, name):
    _, r, w = s1.shape
    tr = _tile(r, (256, 128))

    def body(c_ref, s_ref, r0_ref, r1_ref, r2_ref, o_ref):
        o_ref[...] = ((s_ref[...] + r0_ref[...]) + r1_ref[...]) + r2_ref[...]

    rblk = lambda k: pl.BlockSpec((None, tr, w), functools.partial(lambda i, c, k: (k, i, 0), k=k))
    grid_spec = pltpu.PrefetchScalarGridSpec(
        num_scalar_prefetch=1, grid=(r // tr,),
        in_specs=[pl.BlockSpec((None, tr, w), lambda i, c: (c[0], i, 0)), rblk(0), rblk(1), rblk(2)],
        out_specs=pl.BlockSpec((tr, w), lambda i, c: (i, 0)))
    return pl.pallas_call(body, name=name, grid_spec=grid_spec, out_shape=jax.ShapeDtypeStruct((r, w), F32),
                          compiler_params=_cparams("parallel"))(chip, s1, recv, recv, recv)


def _silu_rows(x, name):
    def body(x_ref, o_ref):
        o_ref[...] = _silu(x_ref[...])

    return pl.pallas_call(body, name=name, out_shape=jax.ShapeDtypeStruct(x.shape, F32))(x)


def _row_sum(x, name):
    def body(x_ref, o_ref):
        acc = x_ref[0:1, :]
        for i in range(1, x.shape[0]):
            acc = acc + x_ref[i:i + 1, :]
        o_ref[...] = acc

    return pl.pallas_call(body, name=name, out_shape=jax.ShapeDtypeStruct((1, x.shape[1]), F32))(x)


def _adamw(w, g, m, v, name):
    cols = w.shape[-1]
    rows = w.size // cols
    tr = _tile(rows, (128,))

    def body(w_ref, g_ref, m_ref, v_ref, d_ref, mo_ref, vo_ref):
        grad = g_ref[...]
        m_new = ADAM_B1 * m_ref[...] + (1.0 - ADAM_B1) * grad
        v_new = ADAM_B2 * v_ref[...] + (1.0 - ADAM_B2) * jnp.square(grad)
        m_hat = m_new / (1.0 - ADAM_B1 ** ADAM_STEP)
        v_hat = v_new / (1.0 - ADAM_B2 ** ADAM_STEP)
        d_ref[...] = -ADAM_LR * (m_hat / (jnp.sqrt(v_hat) + ADAM_EPS) + ADAM_WD * w_ref[...])
        mo_ref[...] = m_new
        vo_ref[...] = v_new

    blk = pl.BlockSpec((tr, cols), lambda i: (i, 0))
    out = pl.pallas_call(
        body, name=name, grid=(rows // tr,), in_specs=[blk] * 4, out_specs=[blk] * 3,
        out_shape=[jax.ShapeDtypeStruct((rows, cols), F32)] * 3, compiler_params=_cparams("parallel"),
    )(*[t.reshape(rows, cols) for t in (w, g, m, v)])
    return [t.reshape(w.shape) for t in out]


def _pack(parts, width, row_mult, dtype):
    flat = jnp.concatenate([p.reshape(-1).astype(dtype) for p in parts])
    rows = -(-flat.shape[0] // (width * row_mult)) * row_mult
    return jnp.pad(flat, (0, rows * width - flat.shape[0])).reshape(rows, width)


def _pack_per_device(parts, width, row_mult):
    flat = jnp.concatenate([p.reshape(NDEV, -1) for p in parts], axis=1)
    rows = -(-flat.shape[1] // (width * row_mult)) * row_mult
    return jnp.pad(flat, ((0, 0), (0, rows * width - flat.shape[1]))).reshape(NDEV, rows, width)


def _unpack(flat, shapes):
    out, off = [], 0
    for shp in shapes:
        size = 1
        for dim in shp:
            size *= dim
        out.append(flat[:, off:off + size].reshape((flat.shape[0],) + tuple(shp)))
        off += size
    return out


def _cols_to_devices(a):
    r, c8 = a.shape
    return a.reshape(r, NDEV, c8 // NDEV).transpose(1, 0, 2)


def _devices_to_cols(a):
    _, r, c = a.shape
    return a.transpose(1, 0, 2).reshape(r, NDEV * c)


def kernel(x, c, w_ada, b_ada, norm1_w, w_in, gdn_conv_w, gdn_a_log, gdn_dt_bias, gdn_norm_w, w_gdn_proj, sc_conv_w, w_sc_out, w_o, norm2_w, w_ffn_in, w_ffn_out, w_ada_f, b_ada_f, normf_w, loss_target, m_w_ada, m_b_ada, m_norm1_w, m_w_in, m_gdn_conv_w, m_gdn_a_log, m_gdn_dt_bias, m_gdn_norm_w, m_w_gdn_proj, m_sc_conv_w, m_w_sc_out, m_w_o, m_norm2_w, m_w_ffn_in, m_w_ffn_out, m_w_ada_f, m_b_ada_f, m_normf_w, v_w_ada, v_b_ada, v_norm1_w, v_w_in, v_gdn_conv_w, v_gdn_a_log, v_gdn_dt_bias, v_gdn_norm_w, v_w_gdn_proj, v_sc_conv_w, v_w_sc_out, v_w_o, v_norm2_w, v_w_ffn_in, v_w_ffn_out, v_w_ada_f, v_b_ada_f, v_normf_w):
    bl, s, d = x.shape
    heads = gdn_a_log.shape[-1]
    dff = w_ffn_out.shape[1] * NDEV
    tok = bl * s
    ax, ay, ac = _position()
    dev = 4 * ax + 2 * ay + ac
    as_tok = lambda a: a.reshape(bl, s, a.shape[-1])
    as_mat = lambda a: a.reshape(tok, a.shape[-1])

    small = _all_gather(_pack([c, gdn_conv_w, sc_conv_w], LANE, 8, F32), name="gather_cond", hbm=False)
    c_all, conv_w, sc_w = _unpack(small.reshape(NDEV, -1), [(bl, d), gdn_conv_w.shape[1:], sc_conv_w.shape[1:]])
    c_act = _silu_rows(c_all.reshape(NDEV * bl, d), "cond_silu")
    conv_w, sc_w = _devices_to_cols(conv_w), _devices_to_cols(sc_w)
    n_ada, n_adaf = w_ada.shape[-1], w_ada_f.shape[-1]
    bias = jnp.broadcast_to(lax.dynamic_slice_in_dim(b_ada, dev * n_ada, n_ada, axis=1), (NDEV * bl, n_ada))
    biasf = jnp.broadcast_to(lax.dynamic_slice_in_dim(b_ada_f.reshape(1, -1), dev * n_adaf, n_adaf, axis=1), (NDEV * bl, n_adaf))
    mod_cols = _mm(c_act, w_ada[0], add=bias, name="ada_cols")
    modf_cols = _mm(c_act, w_ada_f, add=biasf, name="adaf_cols")
    mods = _all_gather(jnp.concatenate([mod_cols, modf_cols], axis=1), name="gather_mod", hbm=False)
    mod_all = mods[:, :, :n_ada].transpose(1, 0, 2).reshape(NDEV * bl, NDEV * n_ada)
    modf_all = mods[:, :, n_ada:].transpose(1, 0, 2).reshape(NDEV * bl, NDEV * n_adaf)
    my_rows = lambda a: lax.dynamic_slice_in_dim(a, dev * bl, bl, axis=0)
    sh1, sc1, g1, sh2, sc2, g2 = [t.reshape(bl, 1, d) for t in jnp.split(my_rows(mod_all), 6, axis=1)]
    shf, scf = [t.reshape(bl, 1, d) for t in jnp.split(my_rows(modf_all), 2, axis=1)]

    big = [w_in[0], w_gdn_proj[0], w_sc_out[0], w_o[0], w_ffn_in[0], w_ffn_out[0]]
    gathered = _all_gather(_pack(big, 8 * LANE, 128, MXU_DTYPE), name="gather_weights", hbm=True)
    wi, wgp, wso, wo, wfi, wfo = _unpack(gathered.reshape(NDEV, -1), [t.shape for t in big])
    wi, wfi = _devices_to_cols(wi), _devices_to_cols(wfi)
    wgp, wso, wo, wfo = (t.reshape(-1, d) for t in (wgp, wso, wo, wfo))
    o_z, o_ab, o_sc, o_ga, o_gb = 3 * d, 4 * d, 4 * d + 2 * heads, 7 * d + 2 * heads, 8 * d + 2 * heads
    w_qkv, w_z = wi[:, :o_z], wi[:, o_z:o_ab]
    w_ab = jnp.pad(wi[:, o_ab:o_sc], ((0, 0), (0, LANE - 2 * heads)))
    w_scs = [wi[:, o_sc + k * d:o_sc + (k + 1) * d] for k in range(3)]
    w_ga, w_gb = wi[:, o_ga:o_gb], wi[:, o_gb:]
    w_fa, w_fb = wfi[:, :dff], wfi[:, dff:]

    n1w, n2w, nfw = norm1_w.reshape(1, d), norm2_w.reshape(1, d), normf_w.reshape(1, d)
    lanes = lambda a: jnp.pad(a.reshape(1, -1), ((0, 0), (0, LANE - a.size)))
    a_log, dt_bias, gnw = lanes(gdn_a_log), lanes(gdn_dt_bias), gdn_norm_w.reshape(1, HEAD)
    f_gates = functools.partial(_f_gates, heads=heads)
    (h1,) = _tok_fwd(_f_norm_mod, [x], [sh1, sc1], [n1w], [(d, MXU_DTYPE)], name="norm1", ts=256)
    h1m = as_mat(h1)
    p_qkv = as_tok(_mm(h1m, w_qkv, name="in_qkv"))
    p_z = as_tok(_mm(h1m, w_z, name="in_z"))
    p_ab = as_tok(_mm(h1m, w_ab, name="in_ab"))
    p_sc = as_tok(_mm(h1m, jnp.concatenate(w_scs, axis=1), name="in_sc"))
    p_g = as_tok(_mm(h1m, jnp.concatenate([w_ga, w_gb], axis=1), name="in_gate"))
    qkv = _qkv_fwd(p_qkv, conv_w, heads, "qkv_conv")
    (gbeta,) = _tok_fwd(f_gates, [p_ab], [], [a_log, dt_bias], [(LANE, F32)], name="gates", ts=512)
    o, s_all, t_all = _gdn_fwd(qkv, gbeta, heads, "gdn")
    (og,) = _tok_fwd(_f_gdn_out, [o, p_z], [], [(gnw, None)], [(d, MXU_DTYPE)], name="gdn_out", ts=512, wb=HEAD, cols=heads)
    y_a = as_tok(_mm(as_mat(og), wgp, name="gdn_proj"))
    scp = _sc_fwd(p_sc, sc_w, "sc_conv")
    y_b = as_tok(_mm(as_mat(scp), wso, name="sc_out"))
    mcols = d // 512 if d % 512 == 0 else 1
    mwb = d // mcols
    merge_toks = [(p_g, 0), (p_g, mcols), y_a, y_b]
    (mrg,) = _tok_fwd(_f_merge, merge_toks, [], [], [(d, MXU_DTYPE)], name="merge", ts=256, wb=mwb, cols=mcols)
    mix = as_tok(_mm(as_mat(mrg), wo, name="mix_out"))
    x2, h2 = _tok_fwd(_f_res_norm_mod, [x, mix], [g1, sh2, sc2], [n2w], [(d, F32), (d, MXU_DTYPE)], name="norm2", ts=256)
    gu = as_tok(_mm(as_mat(h2), wfi, name="ffn_in"))
    fwb = _tile(dff, (256, 128))
    fcols = dff // fwb
    (act,) = _tok_fwd(_f_swiglu, [(gu, 0), (gu, fcols)], [], [], [(dff, MXU_DTYPE)], name="swiglu", ts=512, wb=fwb, cols=fcols)
    ff = as_tok(_mm(as_mat(act), wfo, name="ffn_out"))

    loss_l, (dx2, dff_out, _), (dg2, dshf, dscf), (dnfw,) = _tok_bwd(
        _f_loss, [x2, ff, loss_target], [g2, shf, scf], [nfw], [], [True, True, False], name="loss", ts=256, loss=True)
    dffm = as_mat(dff_out)
    dact = as_tok(_mm(dffm, wfo, tb=True, name="d_ffn_out"))
    gw_ffn_out = _mm(as_mat(act), dffm, ta=True, name="g_ffn_out")
    (dgu_a, dgu_b), _, _ = _tok_bwd(_f_swiglu, [(gu, 0), (gu, fcols)], [], [], [dact], [True, True], name="d_swiglu",
                                    ts=512, wb=fwb, cols=fcols, tok_dtype=MXU_DTYPE)
    dh2 = _mm(as_mat(dgu_a), w_fa, tb=True, name="d_ffn_in_a")
    dh2 = as_tok(_mm(as_mat(dgu_b), w_fb, tb=True, add=dh2, name="d_ffn_in_b"))
    h2m = as_mat(h2)
    gw_ffn_in = jnp.concatenate([_mm(h2m, as_mat(dgu_a), ta=True, name="g_ffn_in_a"),
                                 _mm(h2m, as_mat(dgu_b), ta=True, name="g_ffn_in_b")], axis=1)
    (dx_skip, dmix), (dg1, dsh2, dsc2), (dn2w,) = _tok_bwd(
        _f_res_norm_mod, [x, mix], [g1, sh2, sc2], [n2w], [dx2, dh2], [True, True], name="d_norm2", ts=256)
    dmixm = as_mat(dmix)
    dmrg = as_tok(_mm(dmixm, wo, tb=True, name="d_mix_out"))
    gw_o = _mm(as_mat(mrg), dmixm, ta=True, name="g_mix_out")
    (dga, dgb, dya, dyb), _, _ = _tok_bwd(_f_merge, merge_toks, [], [], [dmrg], [True] * 4, name="d_merge", ts=256,
                                          wb=mwb, cols=mcols, tok_dtype=MXU_DTYPE)
    dyam, dybm = as_mat(dya), as_mat(dyb)
    dog = as_tok(_mm(dyam, wgp, tb=True, name="d_gdn_proj"))
    gw_gdn_proj = _mm(as_mat(og), dyam, ta=True, name="g_gdn_proj")
    dscp = as_tok(_mm(dybm, wso, tb=True, name="d_sc_out"))
    gw_sc_out = _mm(as_mat(scp), dybm, ta=True, name="g_sc_out")
    dscb, dscc, dscx, g_sc_w = _sc_bwd(p_sc, sc_w, dscp, "d_sc_conv")
    (do, dz), _, (g_gnw,) = _tok_bwd(_f_gdn_out, [o, p_z], [], [(gnw, None)], [dog], [True, True], name="d_gdn_out",
                                     ts=512, wb=HEAD, cols=heads, tok_dtype=[F32, MXU_DTYPE])
    dqkv, dgbeta = _gdn_bwd(qkv, gbeta, do, s_all, t_all, heads, "d_gdn")
    dp_qkv, g_conv_w = _qkv_bwd(p_qkv, conv_w, dqkv, heads, "d_qkv_conv")
    (dp_ab,), _, (g_a_log, g_dt_bias) = _tok_bwd(f_gates, [p_ab], [], [a_log, dt_bias], [dgbeta], [True], name="d_gates",
                                                 ts=512, tok_dtype=MXU_DTYPE)
    sections = [(dp_qkv, w_qkv), (dz, w_z), (dp_ab, w_ab), (dscb, w_scs[0]), (dscc, w_scs[1]), (dscx, w_scs[2]),
                (dga, w_ga), (dgb, w_gb)]
    dh1, gw_in = None, []
    for k, (dp, wsec) in enumerate(sections):
        dh1 = _mm(as_mat(dp), wsec, tb=True, add=dh1, name=f"d_in_{k}")
        gw_in.append(_mm(h1m, as_mat(dp), ta=True, name=f"g_in_{k}"))
    gw_in[2] = gw_in[2][:, :2 * heads]
    gw_in = jnp.concatenate(gw_in, axis=1)
    (grad_x,), (dsh1, dsc1), (dn1w,) = _tok_bwd(_f_norm_mod_skip, [x], [sh1, sc1], [n1w], [as_tok(dh1), dx_skip], [True],
                                                name="d_norm1", ts=256)

    per_dev = [_cols_to_devices(gw_in), gw_gdn_proj, gw_sc_out, gw_o, _cols_to_devices(gw_ffn_in), gw_ffn_out]
    g_all = _pack_per_device(per_dev, 8 * LANE, 128)
    recv1 = _exchange_in_chip(g_all, "scatter_in_chip")
    s1 = _sum_in_chip(g_all, recv1, ac.reshape(1).astype(jnp.int32), "sum_in_chip")
    recv2 = _exchange_chips(s1, "scatter_chips")
    reduced = _sum_chips(s1, recv2, (2 * ax + ay).reshape(1).astype(jnp.int32), "sum_chips")
    g_w_in, g_w_gdn_proj, g_w_sc_out, g_w_o, g_w_ffn_in, g_w_ffn_out = [
        t[0] for t in _unpack(reduced.reshape(1, -1), [w_in.shape, w_gdn_proj.shape, w_sc_out.shape, w_o.shape,
                                                       w_ffn_in.shape, w_ffn_out.shape])]

    dmod = jnp.concatenate([t.reshape(bl, d) for t in (dsh1, dsc1, dg1, dsh2, dsc2, dg2)], axis=1)
    dmodf = jnp.concatenate([t.reshape(bl, d) for t in (dshf, dscf)], axis=1)
    summed_parts = [dn1w, dn2w, dnfw, g_gnw, g_a_log, g_dt_bias, g_conv_w, g_sc_w, loss_l]
    partial = _all_gather(_pack([dmod, dmodf] + summed_parts, LANE, 8, F32), name="gather_small", hbm=False)
    partial = partial.reshape(NDEV, -1)
    n_rows = bl * (6 * d + 2 * d)
    dmod_all, dmodf_all = _unpack(partial[:, :n_rows], [(bl, 6 * d), (bl, 2 * d)])
    dmod_all, dmodf_all = dmod_all.reshape(NDEV * bl, 6 * d), dmodf_all.reshape(NDEV * bl, 2 * d)
    totals = _row_sum(partial[:, n_rows:], "sum_small")
    t_n1w, t_n2w, t_nfw, t_gnw, t_a_log, t_dt_bias, t_conv_w, t_sc_w, t_loss = [
        t[0] for t in _unpack(totals, [p.shape for p in summed_parts])]
    my_cols = lambda a, n: lax.dynamic_slice_in_dim(a, dev * n, n, axis=1)
    grads = {
        "w_ada": _mm(c_act, my_cols(dmod_all, n_ada), ta=True, name="g_ada").reshape(w_ada.shape),
        "b_ada": _row_sum(dmod_all, "g_ada_bias").reshape(b_ada.shape),
        "norm1_w": t_n1w.reshape(norm1_w.shape),
        "w_in": g_w_in,
        "gdn_conv_w": my_cols(t_conv_w, gdn_conv_w.shape[-1]).reshape(gdn_conv_w.shape),
        "gdn_a_log": t_a_log[:, :heads].reshape(gdn_a_log.shape),
        "gdn_dt_bias": t_dt_bias[:, :heads].reshape(gdn_dt_bias.shape),
        "gdn_norm_w": t_gnw.reshape(gdn_norm_w.shape),
        "w_gdn_proj": g_w_gdn_proj,
        "sc_conv_w": my_cols(t_sc_w, sc_conv_w.shape[-1]).reshape(sc_conv_w.shape),
        "w_sc_out": g_w_sc_out,
        "w_o": g_w_o,
        "norm2_w": t_n2w.reshape(norm2_w.shape),
        "w_ffn_in": g_w_ffn_in,
        "w_ffn_out": g_w_ffn_out,
        "w_ada_f": _mm(c_act, my_cols(dmodf_all, n_adaf), ta=True, name="g_adaf").reshape(w_ada_f.shape),
        "b_ada_f": _row_sum(dmodf_all, "g_adaf_bias").reshape(b_ada_f.shape),
        "normf_w": t_nfw.reshape(normf_w.shape),
    }
    weights = dict(w_ada=w_ada, b_ada=b_ada, norm1_w=norm1_w, w_in=w_in, gdn_conv_w=gdn_conv_w, gdn_a_log=gdn_a_log,
                   gdn_dt_bias=gdn_dt_bias, gdn_norm_w=gdn_norm_w, w_gdn_proj=w_gdn_proj, sc_conv_w=sc_conv_w,
                   w_sc_out=w_sc_out, w_o=w_o, norm2_w=norm2_w, w_ffn_in=w_ffn_in, w_ffn_out=w_ffn_out, w_ada_f=w_ada_f,
                   b_ada_f=b_ada_f, normf_w=normf_w)
    m_in = [m_w_ada, m_b_ada, m_norm1_w, m_w_in, m_gdn_conv_w, m_gdn_a_log, m_gdn_dt_bias, m_gdn_norm_w, m_w_gdn_proj,
            m_sc_conv_w, m_w_sc_out, m_w_o, m_norm2_w, m_w_ffn_in, m_w_ffn_out, m_w_ada_f, m_b_ada_f, m_normf_w]
    v_in = [v_w_ada, v_b_ada, v_norm1_w, v_w_in, v_gdn_conv_w, v_gdn_a_log, v_gdn_dt_bias, v_gdn_norm_w, v_w_gdn_proj,
            v_sc_conv_w, v_w_sc_out, v_w_o, v_norm2_w, v_w_ffn_in, v_w_ffn_out, v_w_ada_f, v_b_ada_f, v_normf_w]
    deltas, new_m, new_v = [], [], []
    for (wname, wt), mt, vt in zip(weights.items(), m_in, v_in):
        dl, mn, vn = _adamw(wt, grads[wname], mt, vt, "adamw_" + wname)
        deltas.append(dl)
        new_m.append(mn)
        new_v.append(vn)
    loss = t_loss[0, 0]
    return (loss, grad_x, *[grads[k] for k in weights], *deltas, *new_m, *new_v)
```

```python
import functools

import jax
import jax.numpy as jnp
from jax import lax
from jax.experimental import pallas as pl
from jax.experimental.pallas import tpu as pltpu

F32 = jnp.float32
MXU_DTYPE = jnp.bfloat16
NDEV = 8
CHUNK = 64
HEAD = 128
LANE = 128
EPS = 1e-6
ADAM_LR, ADAM_B1, ADAM_B2, ADAM_EPS, ADAM_WD, ADAM_STEP = 0.001, 0.9, 0.999, 1e-08, 0.01, 10
VMEM_LIMIT = 48 * 1024 * 1024
MESH_IDS = pl.DeviceIdType.MESH
HIGHEST = lax.Precision.HIGHEST


def _tile(n, cands=(512, 256, 128)):
    for c in cands:
        if n % c == 0:
            return c
    return n


def _cparams(*sem):
    return pltpu.CompilerParams(dimension_semantics=sem, vmem_limit_bytes=VMEM_LIMIT)


def _mm(a, b, *, ta=False, tb=False, add=None, out_dtype=F32, name):
    m, k = (a.shape[1], a.shape[0]) if ta else a.shape
    n = b.shape[0] if tb else b.shape[1]
    tm, tn = _tile(m), _tile(n)
    tk = k if k <= 1024 else _tile(k, (512,))
    nk = k // tk
    dims = (((0 if ta else 1,), (1 if tb else 0,)), ((), ()))
    has_add = add is not None

    def body(*refs):
        a_ref, b_ref = refs[0], refs[1]
        add_ref = refs[2] if has_add else None
        o_ref = refs[3] if has_add else refs[2]
        part = lax.dot_general(a_ref[...].astype(MXU_DTYPE), b_ref[...].astype(MXU_DTYPE), dims,
                               preferred_element_type=F32)

        def finish(acc):
            if has_add:
                acc = acc + add_ref[...]
            o_ref[...] = acc.astype(o_ref.dtype)

        if nk == 1:
            finish(part)
        else:
            acc_ref = refs[-1]
            kk = pl.program_id(2)

            @pl.when(kk == 0)
            def _():
                acc_ref[...] = part

            @pl.when(kk > 0)
            def _():
                acc_ref[...] += part

            @pl.when(kk == nk - 1)
            def _():
                finish(acc_ref[...])

    a_spec = pl.BlockSpec((tk, tm), lambda i, j, kk: (kk, i)) if ta else pl.BlockSpec((tm, tk), lambda i, j, kk: (i, kk))
    b_spec = pl.BlockSpec((tn, tk), lambda i, j, kk: (j, kk)) if tb else pl.BlockSpec((tk, tn), lambda i, j, kk: (kk, j))
    o_spec = pl.BlockSpec((tm, tn), lambda i, j, kk: (i, j))
    in_specs = [a_spec, b_spec] + ([o_spec] if has_add else [])
    args = [a, b] + ([add] if has_add else [])
    return pl.pallas_call(
        body, name=name, grid=(m // tm, n // tn, nk), in_specs=in_specs, out_specs=o_spec,
        out_shape=jax.ShapeDtypeStruct((m, n), out_dtype),
        scratch_shapes=[pltpu.VMEM((tm, tn), F32)] if nk > 1 else [],
        compiler_params=_cparams("parallel", "parallel", "arbitrary"),
    )(*args)


def _with_off(xs):
    return [x if isinstance(x, tuple) else (x, 0) for x in xs]


def _spec(kind, arr, off, ts, wb):
    w = arr.shape[-1] if wb is None else wb
    col = (lambda j: 0) if wb is None else functools.partial(lambda j, o: o + j, o=off)
    if kind == "tok":
        return pl.BlockSpec((None, ts, w), lambda j, b, i: (b, i, col(j)))
    if kind == "bat":
        return pl.BlockSpec((None, 1, w), lambda j, b, i: (b, 0, col(j)))
    if off is None:
        return pl.BlockSpec(arr.shape, lambda j, b, i: (0, 0))
    return pl.BlockSpec((arr.shape[0], w), lambda j, b, i: (0, col(j)))


def _in_specs(toks, bats, pars, cots, ts, wb):
    return ([_spec("tok", a, o, ts, wb) for a, o in toks] + [_spec("bat", a, o, ts, wb) for a, o in bats]
            + [_spec("par", a, o, ts, wb) for a, o in pars] + [_spec("tok", a, o, ts, wb) for a, o in cots])


def _tok_fwd(fn, toks, bats, pars, outs, *, name, ts, wb=None, cols=1):
    toks, bats, pars = _with_off(toks), _with_off(bats), _with_off(pars)
    bl, s, _ = toks[0][0].shape
    ts = min(ts, s)
    n_in = len(toks) + len(bats) + len(pars)

    def body(*refs):
        res = fn(*[r[...].astype(F32) for r in refs[:n_in]])
        for r, val in zip(refs[n_in:], res):
            r[...] = val.astype(r.dtype)

    out_specs = [pl.BlockSpec((None, ts, w if wb is None else wb), lambda j, b, i: (b, i, j)) for w, _ in outs]
    return pl.pallas_call(
        body, name=name, grid=(cols, bl, s // ts), in_specs=_in_specs(toks, bats, pars, [], ts, wb),
        out_specs=out_specs, out_shape=[jax.ShapeDtypeStruct((bl, s, w), dt) for w, dt in outs],
        compiler_params=_cparams("parallel", "parallel", "parallel"),
    )(*[a for a, _ in toks + bats + pars])


def _accumulate(ref, val, first):
    @pl.when(first)
    def _():
        ref[...] = val

    @pl.when(jnp.logical_not(first))
    def _():
        ref[...] += val


def _tok_bwd(fn, toks, bats, pars, cots, need, *, name, ts, wb=None, cols=1, tok_dtype=F32, loss=False):
    toks, bats, pars, cots = _with_off(toks), _with_off(bats), _with_off(pars), _with_off(cots)
    bl, s, _ = toks[0][0].shape
    ts = min(ts, s)
    nt, nb, npar, nc = len(toks), len(bats), len(pars), len(cots)
    n_in = nt + nb + npar

    def body(*refs):
        j, b, i = pl.program_id(0), pl.program_id(1), pl.program_id(2)
        outs, vjp = jax.vjp(fn, *[r[...].astype(F32) for r in refs[:n_in]])
        o = n_in + nc
        if loss:
            ct = (jnp.ones_like(outs[0]),)
            tot = jnp.broadcast_to(jnp.sum(outs[0], keepdims=True), (1, LANE))
            _accumulate(refs[o], tot, jnp.logical_and(b == 0, i == 0))
            o += 1
        else:
            ct = tuple(r[...].astype(F32) for r in refs[n_in:n_in + nc])
        grads = vjp(ct)
        for t in range(nt):
            if need[t]:
                refs[o][...] = grads[t].astype(refs[o].dtype)
                o += 1
        for t in range(nb):
            _accumulate(refs[o], grads[nt + t], i == 0)
            o += 1
        for t in range(npar):
            first = jnp.logical_and(b == 0, i == 0)
            if pars[t][1] is None:
                first = jnp.logical_and(first, j == 0)
            _accumulate(refs[o], grads[nt + nb + t], first)
            o += 1

    full = lambda arr: arr.shape[-1] if wb is None else wb * cols
    blk = lambda arr: arr.shape[-1] if wb is None else wb
    out_specs, out_shape = [], []
    if loss:
        out_specs.append(pl.BlockSpec((1, LANE), lambda j, b, i: (0, 0)))
        out_shape.append(jax.ShapeDtypeStruct((1, LANE), F32))
    for t in range(nt):
        if need[t]:
            out_specs.append(pl.BlockSpec((None, ts, blk(toks[t][0])), lambda j, b, i: (b, i, j)))
            dt = tok_dtype[t] if isinstance(tok_dtype, (list, tuple)) else tok_dtype
            out_shape.append(jax.ShapeDtypeStruct((bl, s, full(toks[t][0])), dt))
    for arr, _ in bats:
        out_specs.append(pl.BlockSpec((None, 1, blk(arr)), lambda j, b, i: (b, 0, j)))
        out_shape.append(jax.ShapeDtypeStruct((bl, 1, full(arr)), F32))
    for arr, off in pars:
        if off is None:
            out_specs.append(pl.BlockSpec(arr.shape, lambda j, b, i: (0, 0)))
            out_shape.append(jax.ShapeDtypeStruct(arr.shape, F32))
        else:
            out_specs.append(pl.BlockSpec((arr.shape[0], blk(arr)), lambda j, b, i: (0, j)))
            out_shape.append(jax.ShapeDtypeStruct((arr.shape[0], full(arr)), F32))
    res = list(pl.pallas_call(
        body, name=name, grid=(cols, bl, s // ts), in_specs=_in_specs(toks, bats, pars, cots, ts, wb),
        out_specs=out_specs, out_shape=out_shape, compiler_params=_cparams("arbitrary", "arbitrary", "arbitrary"),
    )(*[a for a, _ in toks + bats + pars + cots]))
    tot = res.pop(0) if loss else None
    dtoks = [res.pop(0) if need[t] else None for t in range(nt)]
    dbats = [res.pop(0) for _ in range(nb)]
    dpars = [res.pop(0) for _ in range(npar)]
    return (tot, dtoks, dbats, dpars) if loss else (dtoks, dbats, dpars)


def _silu(x):
    return x * jax.nn.sigmoid(x)


def _rms(x, w):
    return x * lax.rsqrt(jnp.mean(x * x, axis=-1, keepdims=True) + EPS) * w


def _f_norm_mod(x, shift, scale, w):
    return (_rms(x, w) * (1.0 + scale) + shift,)


def _f_norm_mod_skip(x, shift, scale, w):
    return _rms(x, w) * (1.0 + scale) + shift, x


def _f_res_norm_mod(x, mix, gate, shift, scale, w):
    x2 = x + gate * mix
    return x2, _rms(x2, w) * (1.0 + scale) + shift


def _f_gates(p, a_log, dt_bias, *, heads):
    z = p + dt_bias
    g = -jnp.exp(a_log) * (jnp.maximum(z, 0.0) + jnp.log1p(jnp.exp(jnp.minimum(z, -z))))
    lane = lax.broadcasted_iota(jnp.int32, p.shape, 1)
    return (jnp.where(lane < heads, g, jax.nn.sigmoid(p)),)


def _f_gdn_out(o, z, w):
    return (_rms(o, w) * _silu(z),)


def _f_merge(ga, gb, ya, yb):
    return (jax.nn.sigmoid(ga) * ya + jax.nn.sigmoid(gb) * yb,)


def _f_swiglu(a, b):
    return (_silu(a) * b,)


def _f_loss(x2, ff, tgt, gate, shift, scale, w):
    y = _rms(x2 + gate * ff, w) * (1.0 + scale) + shift
    return (0.5 * jnp.mean(jnp.square(y - tgt), axis=-1, keepdims=True),)


def _shift_down(x, s):
    if s == 0:
        return x
    row = lax.broadcasted_iota(jnp.int32, x.shape, 0)
    return jnp.where(row >= s, pltpu.roll(x, s, 0), 0.0)


def _shift_up(x, s):
    if s == 0:
        return x
    n = x.shape[0]
    row = lax.broadcasted_iota(jnp.int32, x.shape, 0)
    return jnp.where(row < n - s, pltpu.roll(x, n - s, 0), 0.0)


def _conv(x, w):
    width = w.shape[0]
    acc = w[width - 1:width, :] * x
    for j in range(width - 1):
        acc = acc + w[j:j + 1, :] * _shift_down(x, width - 1 - j)
    return acc


def _conv_bwd(dy, x, w, dw_ref, first):
    width = w.shape[0]
    dx = w[width - 1:width, :] * dy
    for j in range(width - 1):
        dx = dx + w[j:j + 1, :] * _shift_up(dy, width - 1 - j)
    for j in range(width):
        row = jnp.sum(dy * _shift_down(x, width - 1 - j), axis=0, keepdims=True)
        _accumulate(dw_ref.at[j:j + 1, :], row, first)
    return dx


def _qkv_act(xc, is_v, scale):
    a = _silu(xc)
    nrm = a * lax.rsqrt(jnp.sum(a * a, axis=-1, keepdims=True) + EPS) * scale
    return jnp.where(is_v, a, nrm)


def _qkv_consts(j, heads):
    is_v = j >= 2 * heads
    scale = jnp.where(j < heads, HEAD ** -0.5, 1.0).astype(F32)
    return is_v, scale


def _qkv_fwd(p, w, heads, name):
    bl, s, w3 = p.shape

    def body(p_ref, w_ref, o_ref):
        is_v, scale = _qkv_consts(pl.program_id(0), heads)
        o_ref[...] = _qkv_act(_conv(p_ref[...], w_ref[...]), is_v, scale)

    blk = pl.BlockSpec((None, s, HEAD), lambda j, b: (b, 0, j))
    return pl.pallas_call(
        body, name=name, grid=(w3 // HEAD, bl), in_specs=[blk, pl.BlockSpec((w.shape[0], HEAD), lambda j, b: (0, j))],
        out_specs=blk, out_shape=jax.ShapeDtypeStruct(p.shape, F32), compiler_params=_cparams("parallel", "parallel"),
    )(p, w)


def _qkv_bwd(p, w, dout, heads, name):
    bl, s, w3 = p.shape

    def body(p_ref, w_ref, d_ref, dp_ref, dw_ref):
        is_v, scale = _qkv_consts(pl.program_id(0), heads)
        x, wv = p_ref[...], w_ref[...]
        _, vjp = jax.vjp(lambda xc: _qkv_act(xc, is_v, scale), _conv(x, wv))
        (dxc,) = vjp(d_ref[...])
        dp_ref[...] = _conv_bwd(dxc, x, wv, dw_ref, pl.program_id(1) == 0).astype(dp_ref.dtype)

    blk = pl.BlockSpec((None, s, HEAD), lambda j, b: (b, 0, j))
    wblk = pl.BlockSpec((w.shape[0], HEAD), lambda j, b: (0, j))
    return pl.pallas_call(
        body, name=name, grid=(w3 // HEAD, bl), in_specs=[blk, wblk, blk], out_specs=[blk, wblk],
        out_shape=[jax.ShapeDtypeStruct(p.shape, MXU_DTYPE), jax.ShapeDtypeStruct(w.shape, F32)],
        compiler_params=_cparams("arbitrary", "arbitrary"),
    )(p, w, dout)


def _sc_specs(p, w):
    bl, s, w3 = p.shape
    nblk = w3 // 3 // LANE
    sec = lambda k: pl.BlockSpec((None, s, LANE), functools.partial(lambda j, b, k: (b, 0, k * nblk + j), k=k))
    return nblk, [sec(0), sec(1), sec(2)], pl.BlockSpec((w.shape[0], LANE), lambda j, b: (0, j)), \
        pl.BlockSpec((None, s, LANE), lambda j, b: (b, 0, j))


def _sc_fwd(p, w, name):
    bl, s, w3 = p.shape
    nblk, secs, wblk, oblk = _sc_specs(p, w)

    def body(b_ref, c_ref, x_ref, w_ref, o_ref):
        o_ref[...] = (b_ref[...] * _conv(c_ref[...] * x_ref[...], w_ref[...])).astype(o_ref.dtype)

    return pl.pallas_call(
        body, name=name, grid=(nblk, bl), in_specs=secs + [wblk], out_specs=oblk,
        out_shape=jax.ShapeDtypeStruct((bl, s, w3 // 3), MXU_DTYPE), compiler_params=_cparams("parallel", "parallel"),
    )(p, p, p, w)


def _sc_bwd(p, w, dout, name):
    bl, s, w3 = p.shape
    nblk, secs, wblk, oblk = _sc_specs(p, w)

    def body(b_ref, c_ref, x_ref, w_ref, d_ref, db_ref, dc_ref, dx_ref, dw_ref):
        gb, gc, xin, wv, d = b_ref[...], c_ref[...], x_ref[...], w_ref[...], d_ref[...]
        u = gc * xin
        db_ref[...] = (d * _conv(u, wv)).astype(db_ref.dtype)
        du = _conv_bwd(d * gb, u, wv, dw_ref, pl.program_id(1) == 0)
        dc_ref[...] = (du * xin).astype(dc_ref.dtype)
        dx_ref[...] = (du * gc).astype(dx_ref.dtype)

    act = jax.ShapeDtypeStruct((bl, s, w3 // 3), MXU_DTYPE)
    return pl.pallas_call(
        body, name=name, grid=(nblk, bl), in_specs=secs + [wblk, oblk], out_specs=[oblk, oblk, oblk, wblk],
        out_shape=[act, act, act, jax.ShapeDtypeStruct(w.shape, F32)], compiler_params=_cparams("arbitrary", "arbitrary"),
    )(p, p, p, w, dout)


def _bdot(a, b, ca, cb):
    return lax.dot_general(a.astype(MXU_DTYPE), b.astype(MXU_DTYPE), (((ca,), (cb,)), ((), ())),
                           preferred_element_type=F32)


def _hdot(a, b):
    return lax.dot_general(a, b, (((1,), (0,)), ((), ())), precision=HIGHEST, preferred_element_type=F32)


def _lane_col(x, idx):
    lane = lax.broadcasted_iota(jnp.int32, x.shape, 1)
    return jnp.sum(jnp.where(lane == idx, x, 0.0), axis=1, keepdims=True)


def _chunk_masks():
    r = lax.broadcasted_iota(jnp.int32, (CHUNK, CHUNK), 0)
    c = lax.broadcasted_iota(jnp.int32, (CHUNK, CHUNK), 1)
    return r == c, r >= c, r > c


def _tri_inv(low, eye):
    x = -low
    p = jnp.where(eye, 1.0, 0.0) + x
    span = 2
    while span < CHUNK:
        x = _hdot(x, x)
        p = p + _hdot(p, x)
        span *= 2
    return p


def _gdn_pre(q, k, v, gc, beta, masks):
    eye, causal, strict = masks
    gc_row = jnp.sum(jnp.where(eye, gc, 0.0), axis=0, keepdims=True)
    decay = jnp.where(causal, jnp.exp(jnp.where(causal, gc - gc_row, 0.0)), 0.0)
    eg = jnp.exp(gc)
    gl = gc[CHUNK - 1:CHUNK, :]
    kb, vb = k * beta, v * beta
    low = jnp.where(strict, _bdot(kb, k, 1, 1) * decay, 0.0)
    qk = jnp.where(causal, _bdot(q, k, 1, 1) * decay, 0.0)
    rest = jnp.exp(gl - gc)
    return dict(decay=decay, eg=eg, gl=gl, kb=kb, vb=vb, kbe=kb * eg, low=low, qk=qk, qg=q * eg, rest=rest, kdec=k * rest)


def _gdn_specs(qkv, gbeta, heads, rev):
    bl, s, w3 = qkv.shape
    d, n = w3 // 3, s // CHUNK
    at = (lambda c: n - 1 - c) if rev else (lambda c: c)
    assert d == heads * HEAD
    sec = pl.BlockSpec((None, CHUNK, w3), lambda b, c: (b, at(c), 0))
    gspec = pl.BlockSpec((None, CHUNK, LANE), lambda b, c: (b, at(c), 0))
    sspec = pl.BlockSpec((None, None, heads, HEAD, HEAD), lambda b, c: (b, at(c), 0, 0, 0))
    tspec = pl.BlockSpec((None, None, heads, CHUNK, CHUNK), lambda b, c: (b, at(c), 0, 0, 0))
    return bl, s, d, n, sec, gspec, sspec, tspec


def _gdn_fwd(qkv, gbeta, heads, name):
    bl, s, d, n, sec, gspec, sspec, tspec = _gdn_specs(qkv, gbeta, heads, False)

    def body(x_ref, g_ref, o_ref, s_ref, t_ref, st_ref):
        @pl.when(pl.program_id(1) == 0)
        def _():
            st_ref[...] = jnp.zeros_like(st_ref)

        masks = _chunk_masks()
        eye, causal, _ = masks
        gblk = g_ref[...]
        gc_all = _hdot(jnp.where(causal, 1.0, 0.0), gblk)
        for h in range(heads):
            sl = slice(h * HEAD, (h + 1) * HEAD)
            st = st_ref[h]
            q, k, v = (x_ref[:, sec * d + h * HEAD:sec * d + (h + 1) * HEAD] for sec in range(3))
            pre = _gdn_pre(q, k, v, _lane_col(gc_all, h), _lane_col(gblk, heads + h), masks)
            t = _tri_inv(pre["low"], eye)
            u, w = _bdot(t, pre["vb"], 1, 0), _bdot(t, pre["kbe"], 1, 0)
            vnew = u - _bdot(w, st, 1, 0)
            o_ref[:, sl] = _bdot(pre["qg"], st, 1, 0) + _bdot(pre["qk"], vnew, 1, 0)
            s_ref[h] = st
            t_ref[h] = t
            st_ref[h] = st * jnp.exp(pre["gl"]) + _bdot(pre["kdec"], vnew, 0, 0)

    return pl.pallas_call(
        body, name=name, grid=(bl, n), in_specs=[sec, gspec],
        out_specs=[pl.BlockSpec((None, CHUNK, d), lambda b, c: (b, c, 0)), sspec, tspec],
        out_shape=[jax.ShapeDtypeStruct((bl, s, d), F32), jax.ShapeDtypeStruct((bl, n, heads, HEAD, HEAD), F32),
                   jax.ShapeDtypeStruct((bl, n, heads, CHUNK, CHUNK), F32)],
        scratch_shapes=[pltpu.VMEM((heads, HEAD, HEAD), F32)], compiler_params=_cparams("arbitrary", "arbitrary"),
    )(qkv, gbeta)


def _gdn_bwd(qkv, gbeta, dout, s_all, t_all, heads, name):
    bl, s, d, n, sec, gspec, sspec, tspec = _gdn_specs(qkv, gbeta, heads, True)
    ospec = pl.BlockSpec((None, CHUNK, d), lambda b, c: (b, n - 1 - c, 0))

    def body(x_ref, g_ref, do_ref, s_ref, t_ref, dx_ref, dg_ref, ds_ref):
        @pl.when(pl.program_id(1) == 0)
        def _():
            ds_ref[...] = jnp.zeros_like(ds_ref)

        masks = _chunk_masks()
        eye, causal, strict = masks
        gblk = g_ref[...]
        gc_all = _hdot(jnp.where(causal, 1.0, 0.0), gblk)
        lane = lax.broadcasted_iota(jnp.int32, gblk.shape, 1)
        last_row = lax.broadcasted_iota(jnp.int32, (CHUNK, 1), 0) == CHUNK - 1
        rowsum = lambda a: jnp.sum(a, axis=1, keepdims=True)
        dgc_all = jnp.zeros_like(gblk)
        dbeta_all = jnp.zeros_like(gblk)
        for h in range(heads):
            sl = slice(h * HEAD, (h + 1) * HEAD)
            qs, ks, vs = (slice(sec * d + h * HEAD, sec * d + (h + 1) * HEAD) for sec in range(3))
            q, k, v, do = x_ref[:, qs], x_ref[:, ks], x_ref[:, vs], do_ref[:, sl]
            beta = _lane_col(gblk, heads + h)
            st, t, dsn = s_ref[h], t_ref[h], ds_ref[h]
            pre = _gdn_pre(q, k, v, _lane_col(gc_all, h), beta, masks)
            decay, eg, kb, vb, kbe, low, qk, qg, kdec = (pre[x] for x in ("decay", "eg", "kb", "vb", "kbe", "low", "qk", "qg", "kdec"))
            egl = jnp.exp(pre["gl"])
            u, w = _bdot(t, vb, 1, 0), _bdot(t, kbe, 1, 0)
            vnew = u - _bdot(w, st, 1, 0)
            dkdec = _bdot(vnew, dsn, 1, 1)
            dvnew = _bdot(kdec, dsn, 1, 0) + _bdot(qk, do, 0, 0)
            dgl = jnp.sum(dsn * st, keepdims=True) * egl
            dqg = _bdot(do, st, 1, 1)
            dqk = jnp.where(causal, _bdot(do, vnew, 1, 1), 0.0)
            dw = -_bdot(dvnew, st, 1, 1)
            ds_ref[h] = dsn * egl + _bdot(qg, do, 0, 0) - _bdot(w, dvnew, 0, 0)
            dt = _bdot(dvnew, vb, 1, 1) + _bdot(dw, kbe, 1, 1)
            dvb, dkbe = _bdot(t, dvnew, 0, 0), _bdot(t, dw, 0, 0)
            dlow = -jnp.where(strict, _bdot(t, _bdot(dt, t, 1, 1), 0, 0), 0.0)
            da, db = dlow * decay, dqk * decay
            m = dlow * low + dqk * qk
            kdk = dkdec * kdec
            col_of_m = jnp.sum(jnp.where(eye, jnp.sum(m, axis=0, keepdims=True), 0.0), axis=1, keepdims=True)
            dgc = rowsum(m) - col_of_m + rowsum(dqg * qg) + rowsum(dkbe * kbe) - rowsum(kdk)
            dgc = dgc + jnp.where(last_row, dgl + jnp.sum(kdk, keepdims=True), 0.0)
            dkb = _bdot(da, k, 1, 0) + dkbe * eg
            dx_ref[:, ks] = _bdot(da, kb, 0, 0) + _bdot(db, q, 0, 0) + dkdec * pre["rest"] + dkb * beta
            dx_ref[:, qs] = _bdot(db, k, 1, 0) + dqg * eg
            dx_ref[:, vs] = dvb * beta
            dbeta = rowsum(dkb * k) + rowsum(dvb * v)
            dgc_all = dgc_all + jnp.where(lane == h, dgc, 0.0)
            dbeta_all = dbeta_all + jnp.where(lane == heads + h, dbeta, 0.0)
        upper = jnp.where(jnp.logical_or(eye, jnp.logical_not(causal)), 1.0, 0.0)
        dg_ref[...] = _hdot(upper, dgc_all) + dbeta_all

    return pl.pallas_call(
        body, name=name, grid=(bl, n), in_specs=[sec, gspec, ospec, sspec, tspec], out_specs=[sec, gspec],
        out_shape=[jax.ShapeDtypeStruct(qkv.shape, F32), jax.ShapeDtypeStruct((bl, s, LANE), F32)],
        scratch_shapes=[pltpu.VMEM((heads, HEAD, HEAD), F32)], compiler_params=_cparams("arbitrary", "arbitrary"),
    )(qkv, gbeta, dout, s_all, t_all)


def _position():
    return lax.axis_index("x"), lax.axis_index("y"), lax.axis_index("c")


def _all_gather(x, *, name, hbm):
    space = pltpu.HBM if hbm else pltpu.VMEM

    def body(x_ref, out_ref, send_sems, recv_sems, local_sem):
        ax, ay, ac = _position()
        me, sibling = (ax, ay, ac), (ax, ay, 1 - ac)
        chips = [(1 - ax, ay), (ax, 1 - ay), (1 - ax, 1 - ay)]

        def slot(px, py, pc):
            return out_ref.at[4 * px + 2 * py + pc]

        def copy(k, block, to, src=None):
            return pltpu.make_async_remote_copy(
                src_ref=slot(*block) if src is None else src, dst_ref=slot(*block), send_sem=send_sems.at[k],
                recv_sem=recv_sems.at[k], device_id=to, device_id_type=MESH_IDS)

        mine = pltpu.make_async_copy(x_ref, slot(*me), local_sem)
        mine.start()
        first = [copy(0, me, sibling, src=x_ref)] + [copy(1 + j, me, (*chip, ac), src=x_ref) for j, chip in enumerate(chips)]
        for cp in first:
            cp.start()
        passed = [copy(4 + j, (*chip, ac), sibling) for j, chip in enumerate(chips)]
        for j, chip in enumerate(chips):
            copy(1 + j, (*chip, ac), me).wait_recv()
            passed[j].start()
        copy(0, sibling, me).wait_recv()
        for j, chip in enumerate(chips):
            copy(4 + j, (*chip, 1 - ac), me).wait_recv()
        for cp in first + passed:
            cp.wait_send()
        mine.wait()

    return pl.pallas_call(
        body, name=name, out_shape=jax.ShapeDtypeStruct((NDEV,) + x.shape, x.dtype),
        in_specs=[pl.BlockSpec(memory_space=space)], out_specs=pl.BlockSpec(memory_space=space),
        scratch_shapes=[pltpu.SemaphoreType.DMA((7,)), pltpu.SemaphoreType.DMA((7,)), pltpu.SemaphoreType.DMA],
    )(x)


def _exchange_in_chip(g, name):
    def body(g_ref, recv_ref, send_sems, recv_sems):
        ax, ay, ac = _position()
        copies = [pltpu.make_async_remote_copy(
            src_ref=g_ref.at[2 * q + (1 - ac)], dst_ref=recv_ref.at[q], send_sem=send_sems.at[q], recv_sem=recv_sems.at[q],
            device_id=(ax, ay, 1 - ac), device_id_type=MESH_IDS) for q in range(4)]
        for cp in copies:
            cp.start()
        for cp in copies:
            cp.wait_recv()
        for cp in copies:
            cp.wait_send()

    hbm = pl.BlockSpec(memory_space=pltpu.HBM)
    return pl.pallas_call(
        body, name=name, out_shape=jax.ShapeDtypeStruct((4,) + g.shape[1:], g.dtype), in_specs=[hbm], out_specs=hbm,
        scratch_shapes=[pltpu.SemaphoreType.DMA((4,)), pltpu.SemaphoreType.DMA((4,))],
    )(g)


def _exchange_chips(s1, name):
    def body(s_ref, recv_ref, send_sems, recv_sems):
        ax, ay, ac = _position()
        chips = [(1 - ax, ay), (ax, 1 - ay), (1 - ax, 1 - ay)]
        copies = [pltpu.make_async_remote_copy(
            src_ref=s_ref.at[2 * cx + cy], dst_ref=recv_ref.at[r], send_sem=send_sems.at[r], recv_sem=recv_sems.at[r],
            device_id=(cx, cy, ac), device_id_type=MESH_IDS) for r, (cx, cy) in enumerate(chips)]
        for cp in copies:
            cp.start()
        for cp in copies:
            cp.wait_recv()
        for cp in copies:
            cp.wait_send()

    hbm = pl.BlockSpec(memory_space=pltpu.HBM)
    return pl.pallas_call(
        body, name=name, out_shape=jax.ShapeDtypeStruct((3,) + s1.shape[1:], s1.dtype), in_specs=[hbm], out_specs=hbm,
        scratch_shapes=[pltpu.SemaphoreType.DMA((3,)), pltpu.SemaphoreType.DMA((3,))],
    )(s1)


def _sum_in_chip(g, recv, parity, name):
    _, r, w = g.shape
    tr = _tile(r, (256, 128))

    def body(p_ref, g_ref, r_ref, o_ref):
        o_ref[...] = g_ref[...] + r_ref[...]

    blk = pl.BlockSpec((None, tr, w), lambda q, i, p: (q, i, 0))
    grid_spec = pltpu.PrefetchScalarGridSpec(
        num_scalar_prefetch=1, grid=(4, r // tr),
        in_specs=[pl.BlockSpec((None, tr, w), lambda q, i, p: (2 * q + p[0], i, 0)), blk], out_specs=blk)
    return pl.pallas_call(body, name=name, grid_spec=grid_spec, out_shape=jax.ShapeDtypeStruct((4, r, w), F32),
                          compiler_params=_cparams("parallel", "parallel"))(parity, g, recv)


def _sum_chips(s1, recv, chip, name):
    _, r, w = s1.shape
    tr = _tile(r, (256, 128))

    def body(c_ref, s_ref, r0_ref, r1_ref, r2_ref, o_ref):
        o_ref[...] = ((s_ref[...] + r0_ref[...]) + r1_ref[...]) + r2_ref[...]

    rblk = lambda k: pl.BlockSpec((None, tr, w), functools.partial(lambda i, c, k: (k, i, 0), k=k))
    grid_spec = pltpu.PrefetchScalarGridSpec(
        num_scalar_prefetch=1, grid=(r // tr,),
        in_specs=[pl.BlockSpec((None, tr, w), lambda i, c: (c[0], i, 0)), rblk(0), rblk(1), rblk(2)],
        out_specs=pl.BlockSpec((tr, w), lambda i, c: (i, 0)))
    return pl.pallas_call(body, name=name, grid_spec=grid_spec, out_shape=jax.ShapeDtypeStruct((r, w), F32),
                          compiler_params=_cparams("parallel"))(chip, s1, recv, recv, recv)


def _silu_rows(x, name):
    def body(x_ref, o_ref):
        o_ref[...] = _silu(x_ref[...])

    return pl.pallas_call(body, name=name, out_shape=jax.ShapeDtypeStruct(x.shape, F32))(x)


def _row_sum(x, name):
    def body(x_ref, o_ref):
        acc = x_ref[0:1, :]
        for i in range(1, x.shape[0]):
            acc = acc + x_ref[i:i + 1, :]
        o_ref[...] = acc

    return pl.pallas_call(body, name=name, out_shape=jax.ShapeDtypeStruct((1, x.shape[1]), F32))(x)


def _adamw(w, g, m, v, name):
    cols = w.shape[-1]
    rows = w.size // cols
    tr = _tile(rows, (128,))

    def body(w_ref, g_ref, m_ref, v_ref, d_ref, mo_ref, vo_ref):
        grad = g_ref[...]
        m_new = ADAM_B1 * m_ref[...] + (1.0 - ADAM_B1) * grad
        v_new = ADAM_B2 * v_ref[...] + (1.0 - ADAM_B2) * jnp.square(grad)
        m_hat = m_new / (1.0 - ADAM_B1 ** ADAM_STEP)
        v_hat = v_new / (1.0 - ADAM_B2 ** ADAM_STEP)
        d_ref[...] = -ADAM_LR * (m_hat / (jnp.sqrt(v_hat) + ADAM_EPS) + ADAM_WD * w_ref[...])
        mo_ref[...] = m_new
        vo_ref[...] = v_new

    blk = pl.BlockSpec((tr, cols), lambda i: (i, 0))
    out = pl.pallas_call(
        body, name=name, grid=(rows // tr,), in_specs=[blk] * 4, out_specs=[blk] * 3,
        out_shape=[jax.ShapeDtypeStruct((rows, cols), F32)] * 3, compiler_params=_cparams("parallel"),
    )(*[t.reshape(rows, cols) for t in (w, g, m, v)])
    return [t.reshape(w.shape) for t in out]


def _pack(parts, width, row_mult, dtype):
    flat = jnp.concatenate([p.reshape(-1).astype(dtype) for p in parts])
    rows = -(-flat.shape[0] // (width * row_mult)) * row_mult
    return jnp.pad(flat, (0, rows * width - flat.shape[0])).reshape(rows, width)


def _unpack(flat, shapes):
    out, off = [], 0
    for shp in shapes:
        size = 1
        for dim in shp:
            size *= dim
        out.append(flat[:, off:off + size].reshape((flat.shape[0],) + tuple(shp)))
        off += size
    return out


def _devices_to_cols(a):
    _, r, c = a.shape
    return a.transpose(1, 0, 2).reshape(r, NDEV * c)


def kernel(x, c, w_ada, b_ada, norm1_w, w_in, gdn_conv_w, gdn_a_log, gdn_dt_bias, gdn_norm_w, w_gdn_proj, sc_conv_w, w_sc_out, w_o, norm2_w, w_ffn_in, w_ffn_out, w_ada_f, b_ada_f, normf_w, loss_target, m_w_ada, m_b_ada, m_norm1_w, m_w_in, m_gdn_conv_w, m_gdn_a_log, m_gdn_dt_bias, m_gdn_norm_w, m_w_gdn_proj, m_sc_conv_w, m_w_sc_out, m_w_o, m_norm2_w, m_w_ffn_in, m_w_ffn_out, m_w_ada_f, m_b_ada_f, m_normf_w, v_w_ada, v_b_ada, v_norm1_w, v_w_in, v_gdn_conv_w, v_gdn_a_log, v_gdn_dt_bias, v_gdn_norm_w, v_w_gdn_proj, v_sc_conv_w, v_w_sc_out, v_w_o, v_norm2_w, v_w_ffn_in, v_w_ffn_out, v_w_ada_f, v_b_ada_f, v_normf_w):
    bl, s, d = x.shape
    heads = gdn_a_log.shape[-1]
    dff = w_ffn_out.shape[1] * NDEV
    tok = bl * s
    ax, ay, ac = _position()
    dev = 4 * ax + 2 * ay + ac
    as_tok = lambda a: a.reshape(bl, s, a.shape[-1])
    as_mat = lambda a: a.reshape(tok, a.shape[-1])

    small = _all_gather(_pack([c, gdn_conv_w, sc_conv_w], LANE, 8, F32), name="gather_cond", hbm=False)
    c_all, conv_w, sc_w = _unpack(small.reshape(NDEV, -1), [(bl, d), gdn_conv_w.shape[1:], sc_conv_w.shape[1:]])
    c_act = _silu_rows(c_all.reshape(NDEV * bl, d), "cond_silu")
    conv_w, sc_w = _devices_to_cols(conv_w), _devices_to_cols(sc_w)
    n_ada, n_adaf = w_ada.shape[-1], w_ada_f.shape[-1]
    bias = jnp.broadcast_to(lax.dynamic_slice_in_dim(b_ada, dev * n_ada, n_ada, axis=1), (NDEV * bl, n_ada))
    biasf = jnp.broadcast_to(lax.dynamic_slice_in_dim(b_ada_f.reshape(1, -1), dev * n_adaf, n_adaf, axis=1), (NDEV * bl, n_adaf))
    mod_cols = _mm(c_act, w_ada[0], add=bias, name="ada_cols")
    modf_cols = _mm(c_act, w_ada_f, add=biasf, name="adaf_cols")
    mods = _all_gather(jnp.concatenate([mod_cols, modf_cols], axis=1), name="gather_mod", hbm=False)
    mod_all = mods[:, :, :n_ada].transpose(1, 0, 2).reshape(NDEV * bl, NDEV * n_ada)
    modf_all = mods[:, :, n_ada:].transpose(1, 0, 2).reshape(NDEV * bl, NDEV * n_adaf)
    my_rows = lambda a: lax.dynamic_slice_in_dim(a, dev * bl, bl, axis=0)
    sh1, sc1, g1, sh2, sc2, g2 = [t.reshape(bl, 1, d) for t in jnp.split(my_rows(mod_all), 6, axis=1)]
    shf, scf = [t.reshape(bl, 1, d) for t in jnp.split(my_rows(modf_all), 2, axis=1)]

    big = [w_gdn_proj[0], w_sc_out[0], w_o[0], w_ffn_in[0].T, w_ffn_out[0], w_in[0].T]
    rows = [t.shape[0] for t in big]
    offs = [sum(rows[:i]) for i in range(len(rows))]
    pad_rows = -sum(rows) % 128
    send = jnp.concatenate([t.astype(MXU_DTYPE) for t in big] + [jnp.zeros((pad_rows, d), MXU_DTYPE)], axis=0)
    gathered = _all_gather(send, name="gather_weights", hbm=True)
    wgp, wso, wo, wt_fi, wfo, wt_in = [gathered[:, offs[i]:offs[i] + rows[i], :].reshape(NDEV * rows[i], d) for i in range(6)]
    o_z, o_ab, o_sc, o_ga, o_gb = 3 * d, 4 * d, 4 * d + 2 * heads, 7 * d + 2 * heads, 8 * d + 2 * heads
    wt_qkv, wt_z = wt_in[:o_z], wt_in[o_z:o_ab]
    wt_ab = jnp.pad(wt_in[o_ab:o_sc], ((0, LANE - 2 * heads), (0, 0)))
    wt_scs = [wt_in[o_sc + k * d:o_sc + (k + 1) * d] for k in range(3)]
    wt_ga, wt_gb = wt_in[o_ga:o_gb], wt_in[o_gb:]
    wt_fa, wt_fb = wt_fi[:dff], wt_fi[dff:]

    n1w, n2w, nfw = norm1_w.reshape(1, d), norm2_w.reshape(1, d), normf_w.reshape(1, d)
    lanes = lambda a: jnp.pad(a.reshape(1, -1), ((0, 0), (0, LANE - a.size)))
    a_log, dt_bias, gnw = lanes(gdn_a_log), lanes(gdn_dt_bias), gdn_norm_w.reshape(1, HEAD)
    f_gates = functools.partial(_f_gates, heads=heads)
    (h1,) = _tok_fwd(_f_norm_mod, [x], [sh1, sc1], [n1w], [(d, MXU_DTYPE)], name="norm1", ts=256)
    h1m = as_mat(h1)
    p_qkv = as_tok(_mm(h1m, wt_qkv, tb=True, name="in_qkv"))
    p_z = as_tok(_mm(h1m, wt_z, tb=True, name="in_z"))
    p_ab = as_tok(_mm(h1m, wt_ab, tb=True, name="in_ab"))
    p_sc = as_tok(_mm(h1m, wt_in[o_sc:o_ga], tb=True, name="in_sc"))
    p_g = as_tok(_mm(h1m, wt_in[o_ga:], tb=True, name="in_gate"))
    qkv = _qkv_fwd(p_qkv, conv_w, heads, "qkv_conv")
    (gbeta,) = _tok_fwd(f_gates, [p_ab], [], [a_log, dt_bias], [(LANE, F32)], name="gates", ts=512)
    o, s_all, t_all = _gdn_fwd(qkv, gbeta, heads, "gdn")
    (og,) = _tok_fwd(_f_gdn_out, [o, p_z], [], [(gnw, None)], [(d, MXU_DTYPE)], name="gdn_out", ts=512, wb=HEAD, cols=heads)
    y_a = as_tok(_mm(as_mat(og), wgp, name="gdn_proj"))
    scp = _sc_fwd(p_sc, sc_w, "sc_conv")
    y_b = as_tok(_mm(as_mat(scp), wso, name="sc_out"))
    mcols = d // 512 if d % 512 == 0 else 1
    mwb = d // mcols
    merge_toks = [(p_g, 0), (p_g, mcols), y_a, y_b]
    (mrg,) = _tok_fwd(_f_merge, merge_toks, [], [], [(d, MXU_DTYPE)], name="merge", ts=256, wb=mwb, cols=mcols)
    mix = as_tok(_mm(as_mat(mrg), wo, name="mix_out"))
    x2, h2 = _tok_fwd(_f_res_norm_mod, [x, mix], [g1, sh2, sc2], [n2w], [(d, F32), (d, MXU_DTYPE)], name="norm2", ts=256)
    gu = as_tok(_mm(as_mat(h2), wt_fi, tb=True, name="ffn_in"))
    fwb = _tile(dff, (256, 128))
    fcols = dff // fwb
    (act,) = _tok_fwd(_f_swiglu, [(gu, 0), (gu, fcols)], [], [], [(dff, MXU_DTYPE)], name="swiglu", ts=512, wb=fwb, cols=fcols)
    ff = as_tok(_mm(as_mat(act), wfo, name="ffn_out"))

    loss_l, (dx2, dff_out, _), (dg2, dshf, dscf), (dnfw,) = _tok_bwd(
        _f_loss, [x2, ff, loss_target], [g2, shf, scf], [nfw], [], [True, True, False], name="loss", ts=256, loss=True)
    dffm = as_mat(dff_out)
    dact = as_tok(_mm(dffm, wfo, tb=True, name="d_ffn_out"))
    gw_ffn_out = _mm(as_mat(act), dffm, ta=True, name="g_ffn_out")
    (dgu_a, dgu_b), _, _ = _tok_bwd(_f_swiglu, [(gu, 0), (gu, fcols)], [], [], [dact], [True, True], name="d_swiglu",
                                    ts=512, wb=fwb, cols=fcols, tok_dtype=MXU_DTYPE)
    dh2 = _mm(as_mat(dgu_a), wt_fa, name="d_ffn_in_a")
    dh2 = as_tok(_mm(as_mat(dgu_b), wt_fb, add=dh2, name="d_ffn_in_b"))
    h2m = as_mat(h2)
    gwt_ffn_in = jnp.concatenate([_mm(as_mat(dgu_a), h2m, ta=True, name="g_ffn_in_a"),
                                  _mm(as_mat(dgu_b), h2m, ta=True, name="g_ffn_in_b")], axis=0)
    (dx_skip, dmix), (dg1, dsh2, dsc2), (dn2w,) = _tok_bwd(
        _f_res_norm_mod, [x, mix], [g1, sh2, sc2], [n2w], [dx2, dh2], [True, True], name="d_norm2", ts=256)
    dmixm = as_mat(dmix)
    dmrg = as_tok(_mm(dmixm, wo, tb=True, name="d_mix_out"))
    gw_o = _mm(as_mat(mrg), dmixm, ta=True, name="g_mix_out")
    (dga, dgb, dya, dyb), _, _ = _tok_bwd(_f_merge, merge_toks, [], [], [dmrg], [True] * 4, name="d_merge", ts=256,
                                          wb=mwb, cols=mcols, tok_dtype=MXU_DTYPE)
    dyam, dybm = as_mat(dya), as_mat(dyb)
    dog = as_tok(_mm(dyam, wgp, tb=True, name="d_gdn_proj"))
    gw_gdn_proj = _mm(as_mat(og), dyam, ta=True, name="g_gdn_proj")
    dscp = as_tok(_mm(dybm, wso, tb=True, name="d_sc_out"))
    gw_sc_out = _mm(as_mat(scp), dybm, ta=True, name="g_sc_out")
    dscb, dscc, dscx, g_sc_w = _sc_bwd(p_sc, sc_w, dscp, "d_sc_conv")
    (do, dz), _, (g_gnw,) = _tok_bwd(_f_gdn_out, [o, p_z], [], [(gnw, None)], [dog], [True, True], name="d_gdn_out",
                                     ts=512, wb=HEAD, cols=heads, tok_dtype=[F32, MXU_DTYPE])
    dqkv, dgbeta = _gdn_bwd(qkv, gbeta, do, s_all, t_all, heads, "d_gdn")
    dp_qkv, g_conv_w = _qkv_bwd(p_qkv, conv_w, dqkv, heads, "d_qkv_conv")
    (dp_ab,), _, (g_a_log, g_dt_bias) = _tok_bwd(f_gates, [p_ab], [], [a_log, dt_bias], [dgbeta], [True], name="d_gates",
                                                 ts=512, tok_dtype=MXU_DTYPE)
    sections = [(dp_qkv, wt_qkv), (dz, wt_z), (dp_ab, wt_ab), (dscb, wt_scs[0]), (dscc, wt_scs[1]), (dscx, wt_scs[2]),
                (dga, wt_ga), (dgb, wt_gb)]
    dh1, gwt_in = None, []
    for k, (dp, wsec) in enumerate(sections):
        dh1 = _mm(as_mat(dp), wsec, add=dh1, name=f"d_in_{k}")
        gwt_in.append(_mm(as_mat(dp), h1m, ta=True, name=f"g_in_{k}"))
    gwt_in[2] = gwt_in[2][:2 * heads]
    gwt_in = jnp.concatenate(gwt_in, axis=0)
    (grad_x,), (dsh1, dsc1), (dn1w,) = _tok_bwd(_f_norm_mod_skip, [x], [sh1, sc1], [n1w], [as_tok(dh1), dx_skip], [True],
                                                name="d_norm1", ts=256)

    per_dev = [gw_gdn_proj, gw_sc_out, gw_o, gwt_ffn_in, gw_ffn_out, gwt_in]
    g_all = jnp.concatenate([t.reshape(NDEV, r, d) for t, r in zip(per_dev, rows)] + [jnp.zeros((NDEV, pad_rows, d), F32)], axis=1)
    recv1 = _exchange_in_chip(g_all, "scatter_in_chip")
    s1 = _sum_in_chip(g_all, recv1, ac.reshape(1).astype(jnp.int32), "sum_in_chip")
    recv2 = _exchange_chips(s1, "scatter_chips")
    reduced = _sum_chips(s1, recv2, (2 * ax + ay).reshape(1).astype(jnp.int32), "sum_chips")
    g_w_gdn_proj, g_w_sc_out, g_w_o, gt_w_ffn_in, g_w_ffn_out, gt_w_in = [reduced[offs[i]:offs[i] + rows[i]] for i in range(6)]
    g_w_gdn_proj, g_w_sc_out, g_w_o, g_w_ffn_out = (
        t.reshape(ref.shape) for t, ref in zip((g_w_gdn_proj, g_w_sc_out, g_w_o, g_w_ffn_out), (w_gdn_proj, w_sc_out, w_o, w_ffn_out)))
    g_w_in, g_w_ffn_in = gt_w_in.T.reshape(w_in.shape), gt_w_ffn_in.T.reshape(w_ffn_in.shape)

    dmod = jnp.concatenate([t.reshape(bl, d) for t in (dsh1, dsc1, dg1, dsh2, dsc2, dg2)], axis=1)
    dmodf = jnp.concatenate([t.reshape(bl, d) for t in (dshf, dscf)], axis=1)
    summed_parts = [dn1w, dn2w, dnfw, g_gnw, g_a_log, g_dt_bias, g_conv_w, g_sc_w, loss_l]
    partial = _all_gather(_pack([dmod, dmodf] + summed_parts, LANE, 8, F32), name="gather_small", hbm=False)
    partial = partial.reshape(NDEV, -1)
    n_rows = bl * (6 * d + 2 * d)
    dmod_all, dmodf_all = _unpack(partial[:, :n_rows], [(bl, 6 * d), (bl, 2 * d)])
    dmod_all, dmodf_all = dmod_all.reshape(NDEV * bl, 6 * d), dmodf_all.reshape(NDEV * bl, 2 * d)
    totals = _row_sum(partial[:, n_rows:], "sum_small")
    t_n1w, t_n2w, t_nfw, t_gnw, t_a_log, t_dt_bias, t_conv_w, t_sc_w, t_loss = [
        t[0] for t in _unpack(totals, [p.shape for p in summed_parts])]
    my_cols = lambda a, n: lax.dynamic_slice_in_dim(a, dev * n, n, axis=1)
    grads = {
        "w_ada": _mm(c_act, my_cols(dmod_all, n_ada), ta=True, name="g_ada").reshape(w_ada.shape),
        "b_ada": _row_sum(dmod_all, "g_ada_bias").reshape(b_ada.shape),
        "norm1_w": t_n1w.reshape(norm1_w.shape),
        "w_in": g_w_in,
        "gdn_conv_w": my_cols(t_conv_w, gdn_conv_w.shape[-1]).reshape(gdn_conv_w.shape),
        "gdn_a_log": t_a_log[:, :heads].reshape(gdn_a_log.shape),
        "gdn_dt_bias": t_dt_bias[:, :heads].reshape(gdn_dt_bias.shape),
        "gdn_norm_w": t_gnw.reshape(gdn_norm_w.shape),
        "w_gdn_proj": g_w_gdn_proj,
        "sc_conv_w": my_cols(t_sc_w, sc_conv_w.shape[-1]).reshape(sc_conv_w.shape),
        "w_sc_out": g_w_sc_out,
        "w_o": g_w_o,
        "norm2_w": t_n2w.reshape(norm2_w.shape),
        "w_ffn_in": g_w_ffn_in,
        "w_ffn_out": g_w_ffn_out,
        "w_ada_f": _mm(c_act, my_cols(dmodf_all, n_adaf), ta=True, name="g_adaf").reshape(w_ada_f.shape),
        "b_ada_f": _row_sum(dmodf_all, "g_adaf_bias").reshape(b_ada_f.shape),
        "normf_w": t_nfw.reshape(normf_w.shape),
    }
    weights = dict(w_ada=w_ada, b_ada=b_ada, norm1_w=norm1_w, w_in=w_in, gdn_conv_w=gdn_conv_w, gdn_a_log=gdn_a_log,
                   gdn_dt_bias=gdn_dt_bias, gdn_norm_w=gdn_norm_w, w_gdn_proj=w_gdn_proj, sc_conv_w=sc_conv_w,
                   w_sc_out=w_sc_out, w_o=w_o, norm2_w=norm2_w, w_ffn_in=w_ffn_in, w_ffn_out=w_ffn_out, w_ada_f=w_ada_f,
                   b_ada_f=b_ada_f, normf_w=normf_w)
    m_in = [m_w_ada, m_b_ada, m_norm1_w, m_w_in, m_gdn_conv_w, m_gdn_a_log, m_gdn_dt_bias, m_gdn_norm_w, m_w_gdn_proj,
            m_sc_conv_w, m_w_sc_out, m_w_o, m_norm2_w, m_w_ffn_in, m_w_ffn_out, m_w_ada_f, m_b_ada_f, m_normf_w]
    v_in = [v_w_ada, v_b_ada, v_norm1_w, v_w_in, v_gdn_conv_w, v_gdn_a_log, v_gdn_dt_bias, v_gdn_norm_w, v_w_gdn_proj,
            v_sc_conv_w, v_w_sc_out, v_w_o, v_norm2_w, v_w_ffn_in, v_w_ffn_out, v_w_ada_f, v_b_ada_f, v_normf_w]
    deltas, new_m, new_v = [], [], []
    for (wname, wt), mt, vt in zip(weights.items(), m_in, v_in):
        dl, mn, vn = _adamw(wt, grads[wname], mt, vt, "adamw_" + wname)
        deltas.append(dl)
        new_m.append(mn)
        new_v.append(vn)
    loss = t_loss[0, 0]
    return (loss, grad_x, *[grads[k] for k in weights], *deltas, *new_m, *new_v)
```

```python
import functools

import jax
import jax.numpy as jnp
from jax import lax
from jax.experimental import pallas as pl
from jax.experimental.pallas import tpu as pltpu

F32 = jnp.float32
MXU_DTYPE = jnp.bfloat16
NDEV = 8
CHUNK = 64
HEAD = 128
LANE = 128
EPS = 1e-6
ADAM_LR, ADAM_B1, ADAM_B2, ADAM_EPS, ADAM_WD, ADAM_STEP = 0.001, 0.9, 0.999, 1e-08, 0.01, 10
VMEM_LIMIT = 48 * 1024 * 1024
MESH_IDS = pl.DeviceIdType.MESH
HIGHEST = lax.Precision.HIGHEST


def _tile(n, cands=(512, 256, 128)):
    for c in cands:
        if n % c == 0:
            return c
    return n


def _cparams(*sem):
    return pltpu.CompilerParams(dimension_semantics=sem, vmem_limit_bytes=VMEM_LIMIT)


def _mm(a, b, *, ta=False, tb=False, add=None, out_dtype=F32, name):
    m, k = (a.shape[1], a.shape[0]) if ta else a.shape
    n = b.shape[0] if tb else b.shape[1]
    tm, tn = _tile(m), _tile(n)
    tk = k if k <= 1024 else _tile(k, (512,))
    nk = k // tk
    dims = (((0 if ta else 1,), (1 if tb else 0,)), ((), ()))
    has_add = add is not None

    def body(*refs):
        a_ref, b_ref = refs[0], refs[1]
        add_ref = refs[2] if has_add else None
        o_ref = refs[3] if has_add else refs[2]
        part = lax.dot_general(a_ref[...].astype(MXU_DTYPE), b_ref[...].astype(MXU_DTYPE), dims,
                               preferred_element_type=F32)

        def finish(acc):
            if has_add:
                acc = acc + add_ref[...]
            o_ref[...] = acc.astype(o_ref.dtype)

        if nk == 1:
            finish(part)
        else:
            acc_ref = refs[-1]
            kk = pl.program_id(2)

            @pl.when(kk == 0)
            def _():
                acc_ref[...] = part

            @pl.when(kk > 0)
            def _():
                acc_ref[...] += part

            @pl.when(kk == nk - 1)
            def _():
                finish(acc_ref[...])

    a_spec = pl.BlockSpec((tk, tm), lambda i, j, kk: (kk, i)) if ta else pl.BlockSpec((tm, tk), lambda i, j, kk: (i, kk))
    b_spec = pl.BlockSpec((tn, tk), lambda i, j, kk: (j, kk)) if tb else pl.BlockSpec((tk, tn), lambda i, j, kk: (kk, j))
    o_spec = pl.BlockSpec((tm, tn), lambda i, j, kk: (i, j))
    in_specs = [a_spec, b_spec] + ([o_spec] if has_add else [])
    args = [a, b] + ([add] if has_add else [])
    return pl.pallas_call(
        body, name=name, grid=(m // tm, n // tn, nk), in_specs=in_specs, out_specs=o_spec,
        out_shape=jax.ShapeDtypeStruct((m, n), out_dtype),
        scratch_shapes=[pltpu.VMEM((tm, tn), F32)] if nk > 1 else [],
        compiler_params=_cparams("parallel", "parallel", "arbitrary"),
    )(*args)


def _with_off(xs):
    return [x if isinstance(x, tuple) else (x, 0) for x in xs]


def _spec(kind, arr, off, ts, wb):
    w = arr.shape[-1] if wb is None else wb
    col = (lambda j: 0) if wb is None else functools.partial(lambda j, o: o + j, o=off)
    if kind == "tok":
        return pl.BlockSpec((None, ts, w), lambda j, b, i: (b, i, col(j)))
    if kind == "bat":
        return pl.BlockSpec((None, 1, w), lambda j, b, i: (b, 0, col(j)))
    if off is None:
        return pl.BlockSpec(arr.shape, lambda j, b, i: (0, 0))
    return pl.BlockSpec((arr.shape[0], w), lambda j, b, i: (0, col(j)))


def _in_specs(toks, bats, pars, cots, ts, wb):
    return ([_spec("tok", a, o, ts, wb) for a, o in toks] + [_spec("bat", a, o, ts, wb) for a, o in bats]
            + [_spec("par", a, o, ts, wb) for a, o in pars] + [_spec("tok", a, o, ts, wb) for a, o in cots])


def _tok_fwd(fn, toks, bats, pars, outs, *, name, ts, wb=None, cols=1):
    toks, bats, pars = _with_off(toks), _with_off(bats), _with_off(pars)
    bl, s, _ = toks[0][0].shape
    ts = min(ts, s)
    n_in = len(toks) + len(bats) + len(pars)

    def body(*refs):
        res = fn(*[r[...].astype(F32) for r in refs[:n_in]])
        for r, val in zip(refs[n_in:], res):
            r[...] = val.astype(r.dtype)

    out_specs = [pl.BlockSpec((None, ts, w if wb is None else wb), lambda j, b, i: (b, i, j)) for w, _ in outs]
    return pl.pallas_call(
        body, name=name, grid=(cols, bl, s // ts), in_specs=_in_specs(toks, bats, pars, [], ts, wb),
        out_specs=out_specs, out_shape=[jax.ShapeDtypeStruct((bl, s, w), dt) for w, dt in outs],
        compiler_params=_cparams("parallel", "parallel", "parallel"),
    )(*[a for a, _ in toks + bats + pars])


def _accumulate(ref, val, first):
    @pl.when(first)
    def _():
        ref[...] = val

    @pl.when(jnp.logical_not(first))
    def _():
        ref[...] += val


def _tok_bwd(fn, toks, bats, pars, cots, need, *, name, ts, wb=None, cols=1, tok_dtype=F32, loss=False):
    toks, bats, pars, cots = _with_off(toks), _with_off(bats), _with_off(pars), _with_off(cots)
    bl, s, _ = toks[0][0].shape
    ts = min(ts, s)
    nt, nb, npar, nc = len(toks), len(bats), len(pars), len(cots)
    n_in = nt + nb + npar

    def body(*refs):
        j, b, i = pl.program_id(0), pl.program_id(1), pl.program_id(2)
        outs, vjp = jax.vjp(fn, *[r[...].astype(F32) for r in refs[:n_in]])
        o = n_in + nc
        if loss:
            ct = (jnp.ones_like(outs[0]),)
            tot = jnp.broadcast_to(jnp.sum(outs[0], keepdims=True), (1, LANE))
            _accumulate(refs[o], tot, jnp.logical_and(b == 0, i == 0))
            o += 1
        else:
            ct = tuple(r[...].astype(F32) for r in refs[n_in:n_in + nc])
        grads = vjp(ct)
        for t in range(nt):
            if need[t]:
                refs[o][...] = grads[t].astype(refs[o].dtype)
                o += 1
        for t in range(nb):
            _accumulate(refs[o], grads[nt + t], i == 0)
            o += 1
        for t in range(npar):
            first = jnp.logical_and(b == 0, i == 0)
            if pars[t][1] is None:
                first = jnp.logical_and(first, j == 0)
            _accumulate(refs[o], grads[nt + nb + t], first)
            o += 1

    full = lambda arr: arr.shape[-1] if wb is None else wb * cols
    blk = lambda arr: arr.shape[-1] if wb is None else wb
    out_specs, out_shape = [], []
    if loss:
        out_specs.append(pl.BlockSpec((1, LANE), lambda j, b, i: (0, 0)))
        out_shape.append(jax.ShapeDtypeStruct((1, LANE), F32))
    for t in range(nt):
        if need[t]:
            out_specs.append(pl.BlockSpec((None, ts, blk(toks[t][0])), lambda j, b, i: (b, i, j)))
            dt = tok_dtype[t] if isinstance(tok_dtype, (list, tuple)) else tok_dtype
            out_shape.append(jax.ShapeDtypeStruct((bl, s, full(toks[t][0])), dt))
    for arr, _ in bats:
        out_specs.append(pl.BlockSpec((None, 1, blk(arr)), lambda j, b, i: (b, 0, j)))
        out_shape.append(jax.ShapeDtypeStruct((bl, 1, full(arr)), F32))
    for arr, off in pars:
        if off is None:
            out_specs.append(pl.BlockSpec(arr.shape, lambda j, b, i: (0, 0)))
            out_shape.append(jax.ShapeDtypeStruct(arr.shape, F32))
        else:
            out_specs.append(pl.BlockSpec((arr.shape[0], blk(arr)), lambda j, b, i: (0, j)))
            out_shape.append(jax.ShapeDtypeStruct((arr.shape[0], full(arr)), F32))
    res = list(pl.pallas_call(
        body, name=name, grid=(cols, bl, s // ts), in_specs=_in_specs(toks, bats, pars, cots, ts, wb),
        out_specs=out_specs, out_shape=out_shape, compiler_params=_cparams("arbitrary", "arbitrary", "arbitrary"),
    )(*[a for a, _ in toks + bats + pars + cots]))
    tot = res.pop(0) if loss else None
    dtoks = [res.pop(0) if need[t] else None for t in range(nt)]
    dbats = [res.pop(0) for _ in range(nb)]
    dpars = [res.pop(0) for _ in range(npar)]
    return (tot, dtoks, dbats, dpars) if loss else (dtoks, dbats, dpars)


def _silu(x):
    return x * jax.nn.sigmoid(x)


def _rms(x, w):
    return x * lax.rsqrt(jnp.mean(x * x, axis=-1, keepdims=True) + EPS) * w


def _f_norm_mod(x, shift, scale, w):
    return (_rms(x, w) * (1.0 + scale) + shift,)


def _f_norm_mod_skip(x, shift, scale, w):
    return _rms(x, w) * (1.0 + scale) + shift, x


def _f_res_norm_mod(x, mix, gate, shift, scale, w):
    x2 = x + gate * mix
    return x2, _rms(x2, w) * (1.0 + scale) + shift


def _f_gates(p, a_log, dt_bias, *, heads):
    z = p + dt_bias
    g = -jnp.exp(a_log) * (jnp.maximum(z, 0.0) + jnp.log1p(jnp.exp(jnp.minimum(z, -z))))
    lane = lax.broadcasted_iota(jnp.int32, p.shape, 1)
    return (jnp.where(lane < heads, g, jax.nn.sigmoid(p)),)


def _f_gdn_out(o, z, w):
    return (_rms(o, w) * _silu(z),)


def _f_merge(ga, gb, ya, yb):
    return (jax.nn.sigmoid(ga) * ya + jax.nn.sigmoid(gb) * yb,)


def _f_swiglu(a, b):
    return (_silu(a) * b,)


def _f_loss(x2, ff, tgt, gate, shift, scale, w):
    y = _rms(x2 + gate * ff, w) * (1.0 + scale) + shift
    return (0.5 * jnp.mean(jnp.square(y - tgt), axis=-1, keepdims=True),)


def _shift_down(x, s):
    if s == 0:
        return x
    row = lax.broadcasted_iota(jnp.int32, x.shape, 0)
    return jnp.where(row >= s, pltpu.roll(x, s, 0), 0.0)


def _shift_up(x, s):
    if s == 0:
        return x
    n = x.shape[0]
    row = lax.broadcasted_iota(jnp.int32, x.shape, 0)
    return jnp.where(row < n - s, pltpu.roll(x, n - s, 0), 0.0)


def _conv(x, w):
    width = w.shape[0]
    acc = w[width - 1:width, :] * x
    for j in range(width - 1):
        acc = acc + w[j:j + 1, :] * _shift_down(x, width - 1 - j)
    return acc


def _conv_bwd(dy, x, w, dw_ref, first):
    width = w.shape[0]
    dx = w[width - 1:width, :] * dy
    for j in range(width - 1):
        dx = dx + w[j:j + 1, :] * _shift_up(dy, width - 1 - j)
    for j in range(width):
        row = jnp.sum(dy * _shift_down(x, width - 1 - j), axis=0, keepdims=True)
        _accumulate(dw_ref.at[j:j + 1, :], row, first)
    return dx


def _qkv_act(xc, is_v, scale):
    a = _silu(xc)
    nrm = a * lax.rsqrt(jnp.sum(a * a, axis=-1, keepdims=True) + EPS) * scale
    return jnp.where(is_v, a, nrm)


def _qkv_consts(j, heads):
    is_v = j >= 2 * heads
    scale = jnp.where(j < heads, HEAD ** -0.5, 1.0).astype(F32)
    return is_v, scale


def _qkv_fwd(p, w, heads, name):
    bl, s, w3 = p.shape

    def body(p_ref, w_ref, o_ref):
        is_v, scale = _qkv_consts(pl.program_id(0), heads)
        o_ref[...] = _qkv_act(_conv(p_ref[...], w_ref[...]), is_v, scale)

    blk = pl.BlockSpec((None, s, HEAD), lambda j, b: (b, 0, j))
    return pl.pallas_call(
        body, name=name, grid=(w3 // HEAD, bl), in_specs=[blk, pl.BlockSpec((w.shape[0], HEAD), lambda j, b: (0, j))],
        out_specs=blk, out_shape=jax.ShapeDtypeStruct(p.shape, F32), compiler_params=_cparams("parallel", "parallel"),
    )(p, w)


def _qkv_bwd(p, w, dout, heads, name):
    bl, s, w3 = p.shape

    def body(p_ref, w_ref, d_ref, dp_ref, dw_ref):
        is_v, scale = _qkv_consts(pl.program_id(0), heads)
        x, wv = p_ref[...], w_ref[...]
        _, vjp = jax.vjp(lambda xc: _qkv_act(xc, is_v, scale), _conv(x, wv))
        (dxc,) = vjp(d_ref[...])
        dp_ref[...] = _conv_bwd(dxc, x, wv, dw_ref, pl.program_id(1) == 0).astype(dp_ref.dtype)

    blk = pl.BlockSpec((None, s, HEAD), lambda j, b: (b, 0, j))
    wblk = pl.BlockSpec((w.shape[0], HEAD), lambda j, b: (0, j))
    return pl.pallas_call(
        body, name=name, grid=(w3 // HEAD, bl), in_specs=[blk, wblk, blk], out_specs=[blk, wblk],
        out_shape=[jax.ShapeDtypeStruct(p.shape, MXU_DTYPE), jax.ShapeDtypeStruct(w.shape, F32)],
        compiler_params=_cparams("arbitrary", "arbitrary"),
    )(p, w, dout)


def _sc_specs(p, w):
    bl, s, w3 = p.shape
    nblk = w3 // 3 // LANE
    sec = lambda k: pl.BlockSpec((None, s, LANE), functools.partial(lambda j, b, k: (b, 0, k * nblk + j), k=k))
    return nblk, [sec(0), sec(1), sec(2)], pl.BlockSpec((w.shape[0], LANE), lambda j, b: (0, j)), \
        pl.BlockSpec((None, s, LANE), lambda j, b: (b, 0, j))


def _sc_fwd(p, w, name):
    bl, s, w3 = p.shape
    nblk, secs, wblk, oblk = _sc_specs(p, w)

    def body(b_ref, c_ref, x_ref, w_ref, o_ref):
        o_ref[...] = (b_ref[...] * _conv(c_ref[...] * x_ref[...], w_ref[...])).astype(o_ref.dtype)

    return pl.pallas_call(
        body, name=name, grid=(nblk, bl), in_specs=secs + [wblk], out_specs=oblk,
        out_shape=jax.ShapeDtypeStruct((bl, s, w3 // 3), MXU_DTYPE), compiler_params=_cparams("parallel", "parallel"),
    )(p, p, p, w)


def _sc_bwd(p, w, dout, name):
    bl, s, w3 = p.shape
    nblk, secs, wblk, oblk = _sc_specs(p, w)

    def body(b_ref, c_ref, x_ref, w_ref, d_ref, db_ref, dc_ref, dx_ref, dw_ref):
        gb, gc, xin, wv, d = b_ref[...], c_ref[...], x_ref[...], w_ref[...], d_ref[...]
        u = gc * xin
        db_ref[...] = (d * _conv(u, wv)).astype(db_ref.dtype)
        du = _conv_bwd(d * gb, u, wv, dw_ref, pl.program_id(1) == 0)
        dc_ref[...] = (du * xin).astype(dc_ref.dtype)
        dx_ref[...] = (du * gc).astype(dx_ref.dtype)

    act = jax.ShapeDtypeStruct((bl, s, w3 // 3), MXU_DTYPE)
    return pl.pallas_call(
        body, name=name, grid=(nblk, bl), in_specs=secs + [wblk, oblk], out_specs=[oblk, oblk, oblk, wblk],
        out_shape=[act, act, act, jax.ShapeDtypeStruct(w.shape, F32)], compiler_params=_cparams("arbitrary", "arbitrary"),
    )(p, p, p, w, dout)


def _bdot(a, b, ca, cb):
    return lax.dot_general(a.astype(MXU_DTYPE), b.astype(MXU_DTYPE), (((ca,), (cb,)), ((), ())),
                           preferred_element_type=F32)


def _hdot(a, b):
    return lax.dot_general(a, b, (((1,), (0,)), ((), ())), precision=HIGHEST, preferred_element_type=F32)


def _lane_col(x, idx):
    lane = lax.broadcasted_iota(jnp.int32, x.shape, 1)
    return jnp.sum(jnp.where(lane == idx, x, 0.0), axis=1, keepdims=True)


def _chunk_masks():
    r = lax.broadcasted_iota(jnp.int32, (CHUNK, CHUNK), 0)
    c = lax.broadcasted_iota(jnp.int32, (CHUNK, CHUNK), 1)
    return r == c, r >= c, r > c


def _dot3(a, b):
    ah, bh = a.astype(MXU_DTYPE), b.astype(MXU_DTYPE)
    al, bl = (a - ah.astype(F32)).astype(MXU_DTYPE), (b - bh.astype(F32)).astype(MXU_DTYPE)
    dot = lambda x, y: lax.dot_general(x, y, (((1,), (0,)), ((), ())), preferred_element_type=F32)
    return dot(ah, bh) + (dot(ah, bl) + dot(al, bh))


def _tri_inv_steps(low, eye):
    x = -low
    p = jnp.where(eye, 1.0, 0.0) + x
    span = 2
    while span < CHUNK:
        x = _dot3(x, x)
        yield
        p = p + _dot3(p, x)
        yield
        span *= 2
    return p


def _round_robin(gens):
    out, live = [None] * len(gens), list(range(len(gens)))
    while live:
        still = []
        for i in live:
            try:
                next(gens[i])
                still.append(i)
            except StopIteration as stop:
                out[i] = stop.value
        live = still
    return out


def _gdn_pre(q, k, v, gc, beta, masks):
    eye, causal, strict = masks
    gc_row = jnp.sum(jnp.where(eye, gc, 0.0), axis=0, keepdims=True)
    decay = jnp.where(causal, jnp.exp(jnp.where(causal, gc - gc_row, 0.0)), 0.0)
    eg = jnp.exp(gc)
    gl = gc[CHUNK - 1:CHUNK, :]
    kb, vb = k * beta, v * beta
    low = jnp.where(strict, _bdot(kb, k, 1, 1) * decay, 0.0)
    qk = jnp.where(causal, _bdot(q, k, 1, 1) * decay, 0.0)
    rest = jnp.exp(gl - gc)
    return dict(decay=decay, eg=eg, gl=gl, kb=kb, vb=vb, kbe=kb * eg, low=low, qk=qk, qg=q * eg, rest=rest, kdec=k * rest)


def _gdn_specs(qkv, gbeta, heads, rev):
    bl, s, w3 = qkv.shape
    d, n = w3 // 3, s // CHUNK
    at = (lambda c: n - 1 - c) if rev else (lambda c: c)
    assert d == heads * HEAD
    sec = pl.BlockSpec((None, CHUNK, w3), lambda b, c: (b, at(c), 0))
    gspec = pl.BlockSpec((None, CHUNK, LANE), lambda b, c: (b, at(c), 0))
    sspec = pl.BlockSpec((None, None, heads, HEAD, HEAD), lambda b, c: (b, at(c), 0, 0, 0))
    tspec = pl.BlockSpec((None, None, heads, CHUNK, CHUNK), lambda b, c: (b, at(c), 0, 0, 0))
    return bl, s, d, n, sec, gspec, sspec, tspec


def _gdn_fwd(qkv, gbeta, heads, name):
    bl, s, d, n, sec, gspec, sspec, tspec = _gdn_specs(qkv, gbeta, heads, False)

    def body(x_ref, g_ref, o_ref, s_ref, t_ref, st_ref):
        @pl.when(pl.program_id(1) == 0)
        def _():
            st_ref[...] = jnp.zeros_like(st_ref)

        masks = _chunk_masks()
        eye, causal, _ = masks
        gblk = g_ref[...]
        gc_all = _hdot(jnp.where(causal, 1.0, 0.0), gblk)
        st_all = st_ref[...]

        def head(h):
            st = st_all[h]
            q, k, v = (x_ref[:, sec * d + h * HEAD:sec * d + (h + 1) * HEAD] for sec in range(3))
            pre = _gdn_pre(q, k, v, _lane_col(gc_all, h), _lane_col(gblk, heads + h), masks)
            yield
            t = yield from _tri_inv_steps(pre["low"], eye)
            u, w = _bdot(t, pre["vb"], 1, 0), _bdot(t, pre["kbe"], 1, 0)
            yield
            vnew = u - _bdot(w, st, 1, 0)
            yield
            out = _bdot(pre["qg"], st, 1, 0) + _bdot(pre["qk"], vnew, 1, 0)
            return out, t, st * jnp.exp(pre["gl"]) + _bdot(pre["kdec"], vnew, 0, 0)

        outs, ts, states = zip(*_round_robin([head(h) for h in range(heads)]))
        o_ref[...] = jnp.concatenate(outs, axis=1)
        s_ref[...] = st_all
        t_ref[...] = jnp.stack(ts)
        st_ref[...] = jnp.stack(states)

    return pl.pallas_call(
        body, name=name, grid=(bl, n), in_specs=[sec, gspec],
        out_specs=[pl.BlockSpec((None, CHUNK, d), lambda b, c: (b, c, 0)), sspec, tspec],
        out_shape=[jax.ShapeDtypeStruct((bl, s, d), F32), jax.ShapeDtypeStruct((bl, n, heads, HEAD, HEAD), F32),
                   jax.ShapeDtypeStruct((bl, n, heads, CHUNK, CHUNK), F32)],
        scratch_shapes=[pltpu.VMEM((heads, HEAD, HEAD), F32)], compiler_params=_cparams("arbitrary", "arbitrary"),
    )(qkv, gbeta)


def _gdn_bwd(qkv, gbeta, dout, s_all, t_all, heads, name):
    bl, s, d, n, sec, gspec, sspec, tspec = _gdn_specs(qkv, gbeta, heads, True)
    ospec = pl.BlockSpec((None, CHUNK, d), lambda b, c: (b, n - 1 - c, 0))

    def body(x_ref, g_ref, do_ref, s_ref, t_ref, dx_ref, dg_ref, ds_ref):
        @pl.when(pl.program_id(1) == 0)
        def _():
            ds_ref[...] = jnp.zeros_like(ds_ref)

        masks = _chunk_masks()
        eye, causal, strict = masks
        gblk = g_ref[...]
        gc_all = _hdot(jnp.where(causal, 1.0, 0.0), gblk)
        lane = lax.broadcasted_iota(jnp.int32, gblk.shape, 1)
        last_row = lax.broadcasted_iota(jnp.int32, (CHUNK, 1), 0) == CHUNK - 1
        rowsum = lambda a: jnp.sum(a, axis=1, keepdims=True)
        st_all, t_all_, ds_all = s_ref[...], t_ref[...], ds_ref[...]

        def head(h):
            sl = slice(h * HEAD, (h + 1) * HEAD)
            q, k, v = (x_ref[:, sec * d + h * HEAD:sec * d + (h + 1) * HEAD] for sec in range(3))
            do = do_ref[:, sl]
            beta = _lane_col(gblk, heads + h)
            st, t, dsn = st_all[h], t_all_[h], ds_all[h]
            pre = _gdn_pre(q, k, v, _lane_col(gc_all, h), beta, masks)
            decay, eg, kb, vb, kbe, low, qk, qg, kdec = (pre[x] for x in ("decay", "eg", "kb", "vb", "kbe", "low", "qk", "qg", "kdec"))
            egl = jnp.exp(pre["gl"])
            yield
            u, w = _bdot(t, vb, 1, 0), _bdot(t, kbe, 1, 0)
            yield
            vnew = u - _bdot(w, st, 1, 0)
            yield
            dkdec = _bdot(vnew, dsn, 1, 1)
            dvnew = _bdot(kdec, dsn, 1, 0) + _bdot(qk, do, 0, 0)
            dgl = jnp.sum(dsn * st, keepdims=True) * egl
            dqg = _bdot(do, st, 1, 1)
            dqk = jnp.where(causal, _bdot(do, vnew, 1, 1), 0.0)
            yield
            dw = -_bdot(dvnew, st, 1, 1)
            ds_new = dsn * egl + _bdot(qg, do, 0, 0) - _bdot(w, dvnew, 0, 0)
            yield
            dt = _bdot(dvnew, vb, 1, 1) + _bdot(dw, kbe, 1, 1)
            dvb, dkbe = _bdot(t, dvnew, 0, 0), _bdot(t, dw, 0, 0)
            yield
            inner = _bdot(dt, t, 1, 1)
            yield
            dlow = -jnp.where(strict, _bdot(t, inner, 0, 0), 0.0)
            da, db = dlow * decay, dqk * decay
            yield
            m = dlow * low + dqk * qk
            kdk = dkdec * kdec
            col_of_m = jnp.sum(jnp.where(eye, jnp.sum(m, axis=0, keepdims=True), 0.0), axis=1, keepdims=True)
            dgc = rowsum(m) - col_of_m + rowsum(dqg * qg) + rowsum(dkbe * kbe) - rowsum(kdk)
            dgc = dgc + jnp.where(last_row, dgl + jnp.sum(kdk, keepdims=True), 0.0)
            dkb = _bdot(da, k, 1, 0) + dkbe * eg
            yield
            dk = _bdot(da, kb, 0, 0) + _bdot(db, q, 0, 0) + dkdec * pre["rest"] + dkb * beta
            dq = _bdot(db, k, 1, 0) + dqg * eg
            dbeta = rowsum(dkb * k) + rowsum(dvb * v)
            return dq, dk, dvb * beta, jnp.where(lane == h, dgc, 0.0) + jnp.where(lane == heads + h, dbeta, 0.0), ds_new

        dqs, dks, dvs, dgs, dss = zip(*_round_robin([head(h) for h in range(heads)]))
        dx_ref[...] = jnp.concatenate(dqs + dks + dvs, axis=1)
        ds_ref[...] = jnp.stack(dss)
        dgb = dgs[0]
        for extra in dgs[1:]:
            dgb = dgb + extra
        upper = jnp.where(jnp.logical_or(eye, jnp.logical_not(causal)), 1.0, 0.0)
        dg_ref[...] = jnp.where(lane < heads, _hdot(upper, dgb), dgb)

    return pl.pallas_call(
        body, name=name, grid=(bl, n), in_specs=[sec, gspec, ospec, sspec, tspec], out_specs=[sec, gspec],
        out_shape=[jax.ShapeDtypeStruct(qkv.shape, F32), jax.ShapeDtypeStruct((bl, s, LANE), F32)],
        scratch_shapes=[pltpu.VMEM((heads, HEAD, HEAD), F32)], compiler_params=_cparams("arbitrary", "arbitrary"),
    )(qkv, gbeta, dout, s_all, t_all)


def _position():
    return lax.axis_index("x"), lax.axis_index("y"), lax.axis_index("c")


def _all_gather(x, *, name, hbm):
    space = pltpu.HBM if hbm else pltpu.VMEM

    def body(x_ref, out_ref, send_sems, recv_sems, local_sem):
        ax, ay, ac = _position()
        me, sibling = (ax, ay, ac), (ax, ay, 1 - ac)
        chips = [(1 - ax, ay), (ax, 1 - ay), (1 - ax, 1 - ay)]

        def slot(px, py, pc):
            return out_ref.at[4 * px + 2 * py + pc]

        def copy(k, block, to, src=None):
            return pltpu.make_async_remote_copy(
                src_ref=slot(*block) if src is None else src, dst_ref=slot(*block), send_sem=send_sems.at[k],
                recv_sem=recv_sems.at[k], device_id=to, device_id_type=MESH_IDS)

        mine = pltpu.make_async_copy(x_ref, slot(*me), local_sem)
        mine.start()
        first = [copy(0, me, sibling, src=x_ref)] + [copy(1 + j, me, (*chip, ac), src=x_ref) for j, chip in enumerate(chips)]
        for cp in first:
            cp.start()
        passed = [copy(4 + j, (*chip, ac), sibling) for j, chip in enumerate(chips)]
        for j, chip in enumerate(chips):
            copy(1 + j, (*chip, ac), me).wait_recv()
            passed[j].start()
        copy(0, sibling, me).wait_recv()
        for j, chip in enumerate(chips):
            copy(4 + j, (*chip, 1 - ac), me).wait_recv()
        for cp in first + passed:
            cp.wait_send()
        mine.wait()

    return pl.pallas_call(
        body, name=name, out_shape=jax.ShapeDtypeStruct((NDEV,) + x.shape, x.dtype),
        in_specs=[pl.BlockSpec(memory_space=space)], out_specs=pl.BlockSpec(memory_space=space),
        scratch_shapes=[pltpu.SemaphoreType.DMA((7,)), pltpu.SemaphoreType.DMA((7,)), pltpu.SemaphoreType.DMA],
    )(x)


ROW_ALIGN = 16


def _window_start(rows_per_dev, k):
    return rows_per_dev * k // ROW_ALIGN * ROW_ALIGN


def _exchange_in_chip(parts, name):
    n = len(parts)
    packed = sum(win for _, _, win, _ in parts)
    width, dtype = parts[0][0].shape[1], parts[0][0].dtype

    def body(*refs):
        g_refs, own_ref, recv_ref, send_sems, recv_sems, local_sems = refs[:n], *refs[n:]
        ax, ay, ac = _position()
        sibling = (ax, ay, 1 - ac)
        for q in range(4):
            for g_ref, (_, r, win, off) in zip(g_refs, parts):
                there = g_ref.at[pl.ds(pl.multiple_of(_window_start(r, 2 * q + 1 - ac), ROW_ALIGN), win)]
                here = g_ref.at[pl.ds(pl.multiple_of(_window_start(r, 2 * q + ac), ROW_ALIGN), win)]
                pltpu.make_async_remote_copy(src_ref=there, dst_ref=recv_ref.at[q, pl.ds(off, win)], send_sem=send_sems.at[q],
                                             recv_sem=recv_sems.at[q], device_id=sibling, device_id_type=MESH_IDS).start()
                pltpu.make_async_copy(here, own_ref.at[q, pl.ds(off, win)], local_sems.at[q]).start()
        for q in range(4):
            pltpu.make_async_remote_copy(src_ref=own_ref.at[q], dst_ref=recv_ref.at[q], send_sem=send_sems.at[q],
                                         recv_sem=recv_sems.at[q], device_id=sibling, device_id_type=MESH_IDS).wait()
            pltpu.make_async_copy(own_ref.at[q], own_ref.at[q], local_sems.at[q]).wait()

    hbm = pl.BlockSpec(memory_space=pltpu.HBM)
    out = jax.ShapeDtypeStruct((4, packed, width), dtype)
    return pl.pallas_call(
        body, name=name, out_shape=[out, out], in_specs=[hbm] * n, out_specs=[hbm, hbm],
        scratch_shapes=[pltpu.SemaphoreType.DMA((4,)), pltpu.SemaphoreType.DMA((4,)), pltpu.SemaphoreType.DMA((4,))],
    )(*[g for g, _, _, _ in parts])


def _exchange_chips(s1, name):
    def body(s_ref, recv_ref, send_sems, recv_sems):
        ax, ay, ac = _position()
        chips = [(1 - ax, ay), (ax, 1 - ay), (1 - ax, 1 - ay)]
        copies = [pltpu.make_async_remote_copy(
            src_ref=s_ref.at[2 * cx + cy], dst_ref=recv_ref.at[r], send_sem=send_sems.at[r], recv_sem=recv_sems.at[r],
            device_id=(cx, cy, ac), device_id_type=MESH_IDS) for r, (cx, cy) in enumerate(chips)]
        for cp in copies:
            cp.start()
        for cp in copies:
            cp.wait_recv()
        for cp in copies:
            cp.wait_send()

    hbm = pl.BlockSpec(memory_space=pltpu.HBM)
    return pl.pallas_call(
        body, name=name, out_shape=jax.ShapeDtypeStruct((3,) + s1.shape[1:], s1.dtype), in_specs=[hbm], out_specs=hbm,
        scratch_shapes=[pltpu.SemaphoreType.DMA((3,)), pltpu.SemaphoreType.DMA((3,))],
    )(s1)


def _sum_in_chip(own, recv, name):
    _, r, w = own.shape
    tr = _tile(r, (256, 128))

    def body(a_ref, b_ref, o_ref):
        o_ref[...] = (a_ref[...].astype(F32) + b_ref[...].astype(F32)).astype(o_ref.dtype)

    blk = pl.BlockSpec((None, tr, w), lambda q, i: (q, i, 0))
    return pl.pallas_call(body, name=name, grid=(4, r // tr), in_specs=[blk, blk], out_specs=blk,
                          out_shape=jax.ShapeDtypeStruct(own.shape, own.dtype),
                          compiler_params=_cparams("parallel", "parallel"))(own, recv)


def _sum_chips(s1, recv, chip, name):
    _, r, w = s1.shape
    tr = _tile(r, (256, 128))

    def body(c_ref, s_ref, r0_ref, r1_ref, r2_ref, o_ref):
        f = lambda ref: ref[...].astype(F32)
        o_ref[...] = ((f(s_ref) + f(r0_ref)) + f(r1_ref)) + f(r2_ref)

    rblk = lambda k: pl.BlockSpec((None, tr, w), functools.partial(lambda i, c, k: (k, i, 0), k=k))
    grid_spec = pltpu.PrefetchScalarGridSpec(
        num_scalar_prefetch=1, grid=(r // tr,),
        in_specs=[pl.BlockSpec((None, tr, w), lambda i, c: (c[0], i, 0)), rblk(0), rblk(1), rblk(2)],
        out_specs=pl.BlockSpec((tr, w), lambda i, c: (i, 0)))
    return pl.pallas_call(body, name=name, grid_spec=grid_spec, out_shape=jax.ShapeDtypeStruct((r, w), F32),
                          compiler_params=_cparams("parallel"))(chip, s1, recv, recv, recv)


def _silu_rows(x, name):
    def body(x_ref, o_ref):
        o_ref[...] = _silu(x_ref[...])

    return pl.pallas_call(body, name=name, out_shape=jax.ShapeDtypeStruct(x.shape, F32))(x)


def _row_sum(x, name):
    def body(x_ref, o_ref):
        acc = x_ref[0:1, :]
        for i in range(1, x.shape[0]):
            acc = acc + x_ref[i:i + 1, :]
        o_ref[...] = acc

    return pl.pallas_call(body, name=name, out_shape=jax.ShapeDtypeStruct((1, x.shape[1]), F32))(x)


def _adamw(w, g, m, v, name):
    cols = w.shape[-1]
    rows = w.size // cols
    tr = _tile(rows, (128,))

    def body(w_ref, g_ref, m_ref, v_ref, d_ref, mo_ref, vo_ref):
        grad = g_ref[...]
        m_new = ADAM_B1 * m_ref[...] + (1.0 - ADAM_B1) * grad
        v_new = ADAM_B2 * v_ref[...] + (1.0 - ADAM_B2) * jnp.square(grad)
        m_hat = m_new / (1.0 - ADAM_B1 ** ADAM_STEP)
        v_hat = v_new / (1.0 - ADAM_B2 ** ADAM_STEP)
        d_ref[...] = -ADAM_LR * (m_hat / (jnp.sqrt(v_hat) + ADAM_EPS) + ADAM_WD * w_ref[...])
        mo_ref[...] = m_new
        vo_ref[...] = v_new

    blk = pl.BlockSpec((tr, cols), lambda i: (i, 0))
    out = pl.pallas_call(
        body, name=name, grid=(rows // tr,), in_specs=[blk] * 4, out_specs=[blk] * 3,
        out_shape=[jax.ShapeDtypeStruct((rows, cols), F32)] * 3, compiler_params=_cparams("parallel"),
    )(*[t.reshape(rows, cols) for t in (w, g, m, v)])
    return [t.reshape(w.shape) for t in out]


def _pack(parts, width, row_mult, dtype):
    flat = jnp.concatenate([p.reshape(-1).astype(dtype) for p in parts])
    rows = -(-flat.shape[0] // (width * row_mult)) * row_mult
    return jnp.pad(flat, (0, rows * width - flat.shape[0])).reshape(rows, width)


def _unpack(flat, shapes):
    out, off = [], 0
    for shp in shapes:
        size = 1
        for dim in shp:
            size *= dim
        out.append(flat[:, off:off + size].reshape((flat.shape[0],) + tuple(shp)))
        off += size
    return out


def _devices_to_cols(a):
    _, r, c = a.shape
    return a.transpose(1, 0, 2).reshape(r, NDEV * c)


def kernel(x, c, w_ada, b_ada, norm1_w, w_in, gdn_conv_w, gdn_a_log, gdn_dt_bias, gdn_norm_w, w_gdn_proj, sc_conv_w, w_sc_out, w_o, norm2_w, w_ffn_in, w_ffn_out, w_ada_f, b_ada_f, normf_w, loss_target, m_w_ada, m_b_ada, m_norm1_w, m_w_in, m_gdn_conv_w, m_gdn_a_log, m_gdn_dt_bias, m_gdn_norm_w, m_w_gdn_proj, m_sc_conv_w, m_w_sc_out, m_w_o, m_norm2_w, m_w_ffn_in, m_w_ffn_out, m_w_ada_f, m_b_ada_f, m_normf_w, v_w_ada, v_b_ada, v_norm1_w, v_w_in, v_gdn_conv_w, v_gdn_a_log, v_gdn_dt_bias, v_gdn_norm_w, v_w_gdn_proj, v_sc_conv_w, v_w_sc_out, v_w_o, v_norm2_w, v_w_ffn_in, v_w_ffn_out, v_w_ada_f, v_b_ada_f, v_normf_w):
    bl, s, d = x.shape
    heads = gdn_a_log.shape[-1]
    dff = w_ffn_out.shape[1] * NDEV
    tok = bl * s
    ax, ay, ac = _position()
    dev = 4 * ax + 2 * ay + ac
    as_tok = lambda a: a.reshape(bl, s, a.shape[-1])
    as_mat = lambda a: a.reshape(tok, a.shape[-1])

    small = _all_gather(_pack([c, gdn_conv_w, sc_conv_w], LANE, 8, F32), name="gather_cond", hbm=False)
    c_all, conv_w, sc_w = _unpack(small.reshape(NDEV, -1), [(bl, d), gdn_conv_w.shape[1:], sc_conv_w.shape[1:]])
    c_act = _silu_rows(c_all.reshape(NDEV * bl, d), "cond_silu")
    conv_w, sc_w = _devices_to_cols(conv_w), _devices_to_cols(sc_w)
    n_ada, n_adaf = w_ada.shape[-1], w_ada_f.shape[-1]
    bias = jnp.broadcast_to(lax.dynamic_slice_in_dim(b_ada, dev * n_ada, n_ada, axis=1), (NDEV * bl, n_ada))
    biasf = jnp.broadcast_to(lax.dynamic_slice_in_dim(b_ada_f.reshape(1, -1), dev * n_adaf, n_adaf, axis=1), (NDEV * bl, n_adaf))
    mod_cols = _mm(c_act, w_ada[0], add=bias, name="ada_cols")
    modf_cols = _mm(c_act, w_ada_f, add=biasf, name="adaf_cols")
    mods = _all_gather(jnp.concatenate([mod_cols, modf_cols], axis=1), name="gather_mod", hbm=False)
    mod_all = mods[:, :, :n_ada].transpose(1, 0, 2).reshape(NDEV * bl, NDEV * n_ada)
    modf_all = mods[:, :, n_ada:].transpose(1, 0, 2).reshape(NDEV * bl, NDEV * n_adaf)
    my_rows = lambda a: lax.dynamic_slice_in_dim(a, dev * bl, bl, axis=0)
    sh1, sc1, g1, sh2, sc2, g2 = [t.reshape(bl, 1, d) for t in jnp.split(my_rows(mod_all), 6, axis=1)]
    shf, scf = [t.reshape(bl, 1, d) for t in jnp.split(my_rows(modf_all), 2, axis=1)]

    big = [w_gdn_proj[0], w_sc_out[0], w_o[0], w_ffn_in[0].T, w_ffn_out[0], w_in[0].T]
    rows = [t.shape[0] for t in big]
    offs = [sum(rows[:i]) for i in range(len(rows))]
    pad_rows = -sum(rows) % 128
    send = jnp.concatenate([t.astype(MXU_DTYPE) for t in big] + [jnp.zeros((pad_rows, d), MXU_DTYPE)], axis=0)
    gathered = _all_gather(send, name="gather_weights", hbm=True)
    wgp, wso, wo, wt_fi, wfo, wt_in = [gathered[:, offs[i]:offs[i] + rows[i], :].reshape(NDEV * rows[i], d) for i in range(6)]
    o_z, o_ab, o_sc, o_ga, o_gb = 3 * d, 4 * d, 4 * d + 2 * heads, 7 * d + 2 * heads, 8 * d + 2 * heads
    wt_qkv, wt_z = wt_in[:o_z], wt_in[o_z:o_ab]
    wt_ab = jnp.pad(wt_in[o_ab:o_sc], ((0, LANE - 2 * heads), (0, 0)))
    wt_scs = [wt_in[o_sc + k * d:o_sc + (k + 1) * d] for k in range(3)]
    wt_ga, wt_gb = wt_in[o_ga:o_gb], wt_in[o_gb:]
    wt_fa, wt_fb = wt_fi[:dff], wt_fi[dff:]

    n1w, n2w, nfw = norm1_w.reshape(1, d), norm2_w.reshape(1, d), normf_w.reshape(1, d)
    lanes = lambda a: jnp.pad(a.reshape(1, -1), ((0, 0), (0, LANE - a.size)))
    a_log, dt_bias, gnw = lanes(gdn_a_log), lanes(gdn_dt_bias), gdn_norm_w.reshape(1, HEAD)
    f_gates = functools.partial(_f_gates, heads=heads)
    (h1,) = _tok_fwd(_f_norm_mod, [x], [sh1, sc1], [n1w], [(d, MXU_DTYPE)], name="norm1", ts=256)
    h1m = as_mat(h1)
    p_qkv = as_tok(_mm(h1m, wt_qkv, tb=True, name="in_qkv"))
    p_z = as_tok(_mm(h1m, wt_z, tb=True, name="in_z"))
    p_ab = as_tok(_mm(h1m, wt_ab, tb=True, name="in_ab"))
    p_sc = as_tok(_mm(h1m, wt_in[o_sc:o_ga], tb=True, name="in_sc"))
    p_g = as_tok(_mm(h1m, wt_in[o_ga:], tb=True, name="in_gate"))
    qkv = _qkv_fwd(p_qkv, conv_w, heads, "qkv_conv")
    (gbeta,) = _tok_fwd(f_gates, [p_ab], [], [a_log, dt_bias], [(LANE, F32)], name="gates", ts=512)
    o, s_all, t_all = _gdn_fwd(qkv, gbeta, heads, "gdn")
    (og,) = _tok_fwd(_f_gdn_out, [o, p_z], [], [(gnw, None)], [(d, MXU_DTYPE)], name="gdn_out", ts=512, wb=HEAD, cols=heads)
    y_a = as_tok(_mm(as_mat(og), wgp, name="gdn_proj"))
    scp = _sc_fwd(p_sc, sc_w, "sc_conv")
    y_b = as_tok(_mm(as_mat(scp), wso, name="sc_out"))
    mcols = d // 512 if d % 512 == 0 else 1
    mwb = d // mcols
    merge_toks = [(p_g, 0), (p_g, mcols), y_a, y_b]
    (mrg,) = _tok_fwd(_f_merge, merge_toks, [], [], [(d, MXU_DTYPE)], name="merge", ts=256, wb=mwb, cols=mcols)
    mix = as_tok(_mm(as_mat(mrg), wo, name="mix_out"))
    x2, h2 = _tok_fwd(_f_res_norm_mod, [x, mix], [g1, sh2, sc2], [n2w], [(d, F32), (d, MXU_DTYPE)], name="norm2", ts=256)
    gu = as_tok(_mm(as_mat(h2), wt_fi, tb=True, name="ffn_in"))
    fwb = _tile(dff, (256, 128))
    fcols = dff // fwb
    (act,) = _tok_fwd(_f_swiglu, [(gu, 0), (gu, fcols)], [], [], [(dff, MXU_DTYPE)], name="swiglu", ts=512, wb=fwb, cols=fcols)
    ff = as_tok(_mm(as_mat(act), wfo, name="ffn_out"))

    loss_l, (dx2, dff_out, _), (dg2, dshf, dscf), (dnfw,) = _tok_bwd(
        _f_loss, [x2, ff, loss_target], [g2, shf, scf], [nfw], [], [True, True, False], name="loss", ts=256, loss=True)
    dffm = as_mat(dff_out)
    dact = as_tok(_mm(dffm, wfo, tb=True, name="d_ffn_out"))
    gmm = functools.partial(_mm, ta=True, out_dtype=MXU_DTYPE)
    gw_ffn_out = gmm(as_mat(act), dffm, name="g_ffn_out")
    (dgu_a, dgu_b), _, _ = _tok_bwd(_f_swiglu, [(gu, 0), (gu, fcols)], [], [], [dact], [True, True], name="d_swiglu",
                                    ts=512, wb=fwb, cols=fcols, tok_dtype=MXU_DTYPE)
    dh2 = _mm(as_mat(dgu_a), wt_fa, name="d_ffn_in_a")
    dh2 = as_tok(_mm(as_mat(dgu_b), wt_fb, add=dh2, name="d_ffn_in_b"))
    h2m = as_mat(h2)
    gwt_ffn_in = jnp.concatenate([gmm(as_mat(dgu_a), h2m, name="g_ffn_in_a"),
                                  gmm(as_mat(dgu_b), h2m, name="g_ffn_in_b")], axis=0)
    (dx_skip, dmix), (dg1, dsh2, dsc2), (dn2w,) = _tok_bwd(
        _f_res_norm_mod, [x, mix], [g1, sh2, sc2], [n2w], [dx2, dh2], [True, True], name="d_norm2", ts=256)
    dmixm = as_mat(dmix)
    dmrg = as_tok(_mm(dmixm, wo, tb=True, name="d_mix_out"))
    gw_o = gmm(as_mat(mrg), dmixm, name="g_mix_out")
    (dga, dgb, dya, dyb), _, _ = _tok_bwd(_f_merge, merge_toks, [], [], [dmrg], [True] * 4, name="d_merge", ts=256,
                                          wb=mwb, cols=mcols, tok_dtype=MXU_DTYPE)
    dyam, dybm = as_mat(dya), as_mat(dyb)
    dog = as_tok(_mm(dyam, wgp, tb=True, name="d_gdn_proj"))
    gw_gdn_proj = gmm(as_mat(og), dyam, name="g_gdn_proj")
    dscp = as_tok(_mm(dybm, wso, tb=True, name="d_sc_out"))
    gw_sc_out = gmm(as_mat(scp), dybm, name="g_sc_out")
    dscb, dscc, dscx, g_sc_w = _sc_bwd(p_sc, sc_w, dscp, "d_sc_conv")
    (do, dz), _, (g_gnw,) = _tok_bwd(_f_gdn_out, [o, p_z], [], [(gnw, None)], [dog], [True, True], name="d_gdn_out",
                                     ts=512, wb=HEAD, cols=heads, tok_dtype=[F32, MXU_DTYPE])
    dqkv, dgbeta = _gdn_bwd(qkv, gbeta, do, s_all, t_all, heads, "d_gdn")
    dp_qkv, g_conv_w = _qkv_bwd(p_qkv, conv_w, dqkv, heads, "d_qkv_conv")
    (dp_ab,), _, (g_a_log, g_dt_bias) = _tok_bwd(f_gates, [p_ab], [], [a_log, dt_bias], [dgbeta], [True], name="d_gates",
                                                 ts=512, tok_dtype=MXU_DTYPE)
    sections = [(dp_qkv, wt_qkv), (dz, wt_z), (dp_ab, wt_ab), (dscb, wt_scs[0]), (dscc, wt_scs[1]), (dscx, wt_scs[2]),
                (dga, wt_ga), (dgb, wt_gb)]
    dh1, gwt_in = None, []
    for k, (dp, wsec) in enumerate(sections):
        dh1 = _mm(as_mat(dp), wsec, add=dh1, name=f"d_in_{k}")
        gwt_in.append(gmm(as_mat(dp), h1m, name=f"g_in_{k}"))
    gwt_in[2] = gwt_in[2][:2 * heads]
    gwt_in = jnp.concatenate(gwt_in, axis=0)
    (grad_x,), (dsh1, dsc1), (dn1w,) = _tok_bwd(_f_norm_mod_skip, [x], [sh1, sc1], [n1w], [as_tok(dh1), dx_skip], [True],
                                                name="d_norm1", ts=256)

    grads_t = [gw_gdn_proj, gw_sc_out, gw_o, gwt_ffn_in, gw_ffn_out, gwt_in]
    wins = [r + max(r * k % ROW_ALIGN for k in range(NDEV)) for r in rows]
    wins = [-(-w // ROW_ALIGN) * ROW_ALIGN for w in wins]
    wins[-1] += -sum(wins) % 128
    need_rows = max(_window_start(rows[-1], k) for k in range(NDEV)) + wins[-1]
    grads_t[-1] = jnp.pad(gwt_in, ((0, need_rows - gwt_in.shape[0]), (0, 0)))
    packed_offs = [sum(wins[:i]) for i in range(len(wins))]
    own, recv1 = _exchange_in_chip(list(zip(grads_t, rows, wins, packed_offs)), "scatter_in_chip")
    s1 = _sum_in_chip(own, recv1, "sum_in_chip")
    recv2 = _exchange_chips(s1, "scatter_chips")
    reduced = _sum_chips(s1, recv2, (2 * ax + ay).reshape(1).astype(jnp.int32), "sum_chips")
    g_w_gdn_proj, g_w_sc_out, g_w_o, gt_w_ffn_in, g_w_ffn_out = [reduced[packed_offs[i]:packed_offs[i] + rows[i]] for i in range(5)]
    gt_w_in = lax.dynamic_slice_in_dim(reduced[packed_offs[5]:], rows[5] * dev - _window_start(rows[5], dev), rows[5], axis=0)
    g_w_gdn_proj, g_w_sc_out, g_w_o, g_w_ffn_out = (
        t.reshape(ref.shape) for t, ref in zip((g_w_gdn_proj, g_w_sc_out, g_w_o, g_w_ffn_out), (w_gdn_proj, w_sc_out, w_o, w_ffn_out)))
    g_w_in, g_w_ffn_in = gt_w_in.T.reshape(w_in.shape), gt_w_ffn_in.T.reshape(w_ffn_in.shape)

    dmod = jnp.concatenate([t.reshape(bl, d) for t in (dsh1, dsc1, dg1, dsh2, dsc2, dg2)], axis=1)
    dmodf = jnp.concatenate([t.reshape(bl, d) for t in (dshf, dscf)], axis=1)
    summed_parts = [dn1w, dn2w, dnfw, g_gnw, g_a_log, g_dt_bias, g_conv_w, g_sc_w, loss_l]
    partial = _all_gather(_pack([dmod, dmodf] + summed_parts, LANE, 8, F32), name="gather_small", hbm=False)
    partial = partial.reshape(NDEV, -1)
    n_rows = bl * (6 * d + 2 * d)
    dmod_all, dmodf_all = _unpack(partial[:, :n_rows], [(bl, 6 * d), (bl, 2 * d)])
    dmod_all, dmodf_all = dmod_all.reshape(NDEV * bl, 6 * d), dmodf_all.reshape(NDEV * bl, 2 * d)
    totals = _row_sum(partial[:, n_rows:], "sum_small")
    t_n1w, t_n2w, t_nfw, t_gnw, t_a_log, t_dt_bias, t_conv_w, t_sc_w, t_loss = [
        t[0] for t in _unpack(totals, [p.shape for p in summed_parts])]
    my_cols = lambda a, n: lax.dynamic_slice_in_dim(a, dev * n, n, axis=1)
    grads = {
        "w_ada": _mm(c_act, my_cols(dmod_all, n_ada), ta=True, name="g_ada").reshape(w_ada.shape),
        "b_ada": _row_sum(dmod_all, "g_ada_bias").reshape(b_ada.shape),
        "norm1_w": t_n1w.reshape(norm1_w.shape),
        "w_in": g_w_in,
        "gdn_conv_w": my_cols(t_conv_w, gdn_conv_w.shape[-1]).reshape(gdn_conv_w.shape),
        "gdn_a_log": t_a_log[:, :heads].reshape(gdn_a_log.shape),
        "gdn_dt_bias": t_dt_bias[:, :heads].reshape(gdn_dt_bias.shape),
        "gdn_norm_w": t_gnw.reshape(gdn_norm_w.shape),
        "w_gdn_proj": g_w_gdn_proj,
        "sc_conv_w": my_cols(t_sc_w, sc_conv_w.shape[-1]).reshape(sc_conv_w.shape),
        "w_sc_out": g_w_sc_out,
        "w_o": g_w_o,
        "norm2_w": t_n2w.reshape(norm2_w.shape),
        "w_ffn_in": g_w_ffn_in,
        "w_ffn_out": g_w_ffn_out,
        "w_ada_f": _mm(c_act, my_cols(dmodf_all, n_adaf), ta=True, name="g_adaf").reshape(w_ada_f.shape),
        "b_ada_f": _row_sum(dmodf_all, "g_adaf_bias").reshape(b_ada_f.shape),
        "normf_w": t_nfw.reshape(normf_w.shape),
    }
    weights = dict(w_ada=w_ada, b_ada=b_ada, norm1_w=norm1_w, w_in=w_in, gdn_conv_w=gdn_conv_w, gdn_a_log=gdn_a_log,
                   gdn_dt_bias=gdn_dt_bias, gdn_norm_w=gdn_norm_w, w_gdn_proj=w_gdn_proj, sc_conv_w=sc_conv_w,
                   w_sc_out=w_sc_out, w_o=w_o, norm2_w=norm2_w, w_ffn_in=w_ffn_in, w_ffn_out=w_ffn_out, w_ada_f=w_ada_f,
                   b_ada_f=b_ada_f, normf_w=normf_w)
    m_in = [m_w_ada, m_b_ada, m_norm1_w, m_w_in, m_gdn_conv_w, m_gdn_a_log, m_gdn_dt_bias, m_gdn_norm_w, m_w_gdn_proj,
            m_sc_conv_w, m_w_sc_out, m_w_o, m_norm2_w, m_w_ffn_in, m_w_ffn_out, m_w_ada_f, m_b_ada_f, m_normf_w]
    v_in = [v_w_ada, v_b_ada, v_norm1_w, v_w_in, v_gdn_conv_w, v_gdn_a_log, v_gdn_dt_bias, v_gdn_norm_w, v_w_gdn_proj,
            v_sc_conv_w, v_w_sc_out, v_w_o, v_norm2_w, v_w_ffn_in, v_w_ffn_out, v_w_ada_f, v_b_ada_f, v_normf_w]
    deltas, new_m, new_v = [], [], []
    for (wname, wt), mt, vt in zip(weights.items(), m_in, v_in):
        dl, mn, vn = _adamw(wt, grads[wname], mt, vt, "adamw_" + wname)
        deltas.append(dl)
        new_m.append(mn)
        new_v.append(vn)
    loss = t_loss[0, 0]
    return (loss, grad_x, *[grads[k] for k in weights], *deltas, *new_m, *new_v)
```

```python
import functools

import jax
import jax.numpy as jnp
from jax import lax
from jax.experimental import pallas as pl
from jax.experimental.pallas import tpu as pltpu

F32 = jnp.float32
MXU_DTYPE = jnp.bfloat16
NDEV = 8
CHUNK = 64
HEAD = 128
LANE = 128
EPS = 1e-6
ADAM_LR, ADAM_B1, ADAM_B2, ADAM_EPS, ADAM_WD, ADAM_STEP = 0.001, 0.9, 0.999, 1e-08, 0.01, 10
VMEM_LIMIT = 48 * 1024 * 1024
MESH_IDS = pl.DeviceIdType.MESH
HIGHEST = lax.Precision.HIGHEST


def _tile(n, cands=(512, 256, 128)):
    for c in cands:
        if n % c == 0:
            return c
    return n


def _cparams(*sem):
    return pltpu.CompilerParams(dimension_semantics=sem, vmem_limit_bytes=VMEM_LIMIT)


def _mm(a, b, *, ta=False, tb=False, add=None, out_dtype=F32, name):
    m, k = (a.shape[1], a.shape[0]) if ta else a.shape
    n = b.shape[0] if tb else b.shape[1]
    if ta:
        tm, tn = _tile(m), n if n <= 1024 else _tile(n)
        tk = k if k <= 2048 else _tile(k, (2048, 1024, 512))
    else:
        tm, tn = _tile(m, (1024, 512, 256, 128)), _tile(n)
        tk = k if k <= 1024 else _tile(k, (1024, 512))
    nk = k // tk
    dims = (((0 if ta else 1,), (1 if tb else 0,)), ((), ()))
    has_add = add is not None

    def body(*refs):
        a_ref, b_ref = refs[0], refs[1]
        add_ref = refs[2] if has_add else None
        o_ref = refs[3] if has_add else refs[2]
        part = lax.dot_general(a_ref[...].astype(MXU_DTYPE), b_ref[...].astype(MXU_DTYPE), dims,
                               preferred_element_type=F32)

        def finish(acc):
            if has_add:
                acc = acc + add_ref[...]
            o_ref[...] = acc.astype(o_ref.dtype)

        if nk == 1:
            finish(part)
        else:
            acc_ref = refs[-1]
            kk = pl.program_id(2)

            @pl.when(kk == 0)
            def _():
                acc_ref[...] = part

            @pl.when(kk > 0)
            def _():
                acc_ref[...] += part

            @pl.when(kk == nk - 1)
            def _():
                finish(acc_ref[...])

    a_spec = pl.BlockSpec((tk, tm), lambda i, j, kk: (kk, i)) if ta else pl.BlockSpec((tm, tk), lambda i, j, kk: (i, kk))
    b_spec = pl.BlockSpec((tn, tk), lambda i, j, kk: (j, kk)) if tb else pl.BlockSpec((tk, tn), lambda i, j, kk: (kk, j))
    o_spec = pl.BlockSpec((tm, tn), lambda i, j, kk: (i, j))
    in_specs = [a_spec, b_spec] + ([o_spec] if has_add else [])
    args = [a, b] + ([add] if has_add else [])
    return pl.pallas_call(
        body, name=name, grid=(m // tm, n // tn, nk), in_specs=in_specs, out_specs=o_spec,
        out_shape=jax.ShapeDtypeStruct((m, n), out_dtype),
        scratch_shapes=[pltpu.VMEM((tm, tn), F32)] if nk > 1 else [],
        compiler_params=_cparams("parallel", "parallel", "arbitrary"),
    )(*args)


def _with_off(xs):
    return [x if isinstance(x, tuple) else (x, 0) for x in xs]


def _spec(kind, arr, off, ts, wb):
    w = arr.shape[-1] if wb is None else wb
    col = (lambda j: 0) if wb is None else functools.partial(lambda j, o: o + j, o=off)
    if kind == "tok":
        return pl.BlockSpec((None, ts, w), lambda j, b, i: (b, i, col(j)))
    if kind == "bat":
        return pl.BlockSpec((None, 1, w), lambda j, b, i: (b, 0, col(j)))
    if off is None:
        return pl.BlockSpec(arr.shape, lambda j, b, i: (0, 0))
    return pl.BlockSpec((arr.shape[0], w), lambda j, b, i: (0, col(j)))


def _in_specs(toks, bats, pars, cots, ts, wb):
    return ([_spec("tok", a, o, ts, wb) for a, o in toks] + [_spec("bat", a, o, ts, wb) for a, o in bats]
            + [_spec("par", a, o, ts, wb) for a, o in pars] + [_spec("tok", a, o, ts, wb) for a, o in cots])


def _tok_fwd(fn, toks, bats, pars, outs, *, name, ts, wb=None, cols=1):
    toks, bats, pars = _with_off(toks), _with_off(bats), _with_off(pars)
    bl, s, _ = toks[0][0].shape
    ts = min(ts, s)
    n_in = len(toks) + len(bats) + len(pars)

    def body(*refs):
        res = fn(*[r[...].astype(F32) for r in refs[:n_in]])
        for r, val in zip(refs[n_in:], res):
            r[...] = val.astype(r.dtype)

    out_specs = [pl.BlockSpec((None, ts, w if wb is None else wb), lambda j, b, i: (b, i, j)) for w, _ in outs]
    return pl.pallas_call(
        body, name=name, grid=(cols, bl, s // ts), in_specs=_in_specs(toks, bats, pars, [], ts, wb),
        out_specs=out_specs, out_shape=[jax.ShapeDtypeStruct((bl, s, w), dt) for w, dt in outs],
        compiler_params=_cparams("parallel", "parallel", "parallel"),
    )(*[a for a, _ in toks + bats + pars])


def _accumulate(ref, val, first):
    @pl.when(first)
    def _():
        ref[...] = val

    @pl.when(jnp.logical_not(first))
    def _():
        ref[...] += val


def _tok_bwd(fn, toks, bats, pars, cots, need, *, name, ts, wb=None, cols=1, tok_dtype=F32, loss=False):
    toks, bats, pars, cots = _with_off(toks), _with_off(bats), _with_off(pars), _with_off(cots)
    bl, s, _ = toks[0][0].shape
    ts = min(ts, s)
    nt, nb, npar, nc = len(toks), len(bats), len(pars), len(cots)
    n_in = nt + nb + npar

    def body(*refs):
        j, b, i = pl.program_id(0), pl.program_id(1), pl.program_id(2)
        outs, vjp = jax.vjp(fn, *[r[...].astype(F32) for r in refs[:n_in]])
        o = n_in + nc
        if loss:
            ct = (jnp.ones_like(outs[0]),)
            tot = jnp.broadcast_to(jnp.sum(outs[0], keepdims=True), (1, LANE))
            _accumulate(refs[o], tot, jnp.logical_and(b == 0, i == 0))
            o += 1
        else:
            ct = tuple(r[...].astype(F32) for r in refs[n_in:n_in + nc])
        grads = vjp(ct)
        for t in range(nt):
            if need[t]:
                refs[o][...] = grads[t].astype(refs[o].dtype)
                o += 1
        for t in range(nb):
            _accumulate(refs[o], grads[nt + t], i == 0)
            o += 1
        for t in range(npar):
            first = jnp.logical_and(b == 0, i == 0)
            if pars[t][1] is None:
                first = jnp.logical_and(first, j == 0)
            _accumulate(refs[o], grads[nt + nb + t], first)
            o += 1

    full = lambda arr: arr.shape[-1] if wb is None else wb * cols
    blk = lambda arr: arr.shape[-1] if wb is None else wb
    out_specs, out_shape = [], []
    if loss:
        out_specs.append(pl.BlockSpec((1, LANE), lambda j, b, i: (0, 0)))
        out_shape.append(jax.ShapeDtypeStruct((1, LANE), F32))
    for t in range(nt):
        if need[t]:
            out_specs.append(pl.BlockSpec((None, ts, blk(toks[t][0])), lambda j, b, i: (b, i, j)))
            dt = tok_dtype[t] if isinstance(tok_dtype, (list, tuple)) else tok_dtype
            out_shape.append(jax.ShapeDtypeStruct((bl, s, full(toks[t][0])), dt))
    for arr, _ in bats:
        out_specs.append(pl.BlockSpec((None, 1, blk(arr)), lambda j, b, i: (b, 0, j)))
        out_shape.append(jax.ShapeDtypeStruct((bl, 1, full(arr)), F32))
    for arr, off in pars:
        if off is None:
            out_specs.append(pl.BlockSpec(arr.shape, lambda j, b, i: (0, 0)))
            out_shape.append(jax.ShapeDtypeStruct(arr.shape, F32))
        else:
            out_specs.append(pl.BlockSpec((arr.shape[0], blk(arr)), lambda j, b, i: (0, j)))
            out_shape.append(jax.ShapeDtypeStruct((arr.shape[0], full(arr)), F32))
    res = list(pl.pallas_call(
        body, name=name, grid=(cols, bl, s // ts), in_specs=_in_specs(toks, bats, pars, cots, ts, wb),
        out_specs=out_specs, out_shape=out_shape, compiler_params=_cparams("arbitrary", "arbitrary", "arbitrary"),
    )(*[a for a, _ in toks + bats + pars + cots]))
    tot = res.pop(0) if loss else None
    dtoks = [res.pop(0) if need[t] else None for t in range(nt)]
    dbats = [res.pop(0) for _ in range(nb)]
    dpars = [res.pop(0) for _ in range(npar)]
    return (tot, dtoks, dbats, dpars) if loss else (dtoks, dbats, dpars)


def _silu(x):
    return x * jax.nn.sigmoid(x)


def _rms(x, w):
    return x * lax.rsqrt(jnp.mean(x * x, axis=-1, keepdims=True) + EPS) * w


def _f_norm_mod(x, shift, scale, w):
    return (_rms(x, w) * (1.0 + scale) + shift,)


def _f_norm_mod_skip(x, shift, scale, w):
    return _rms(x, w) * (1.0 + scale) + shift, x


def _f_res_norm_mod(x, mix, gate, shift, scale, w):
    x2 = x + gate * mix
    return x2, _rms(x2, w) * (1.0 + scale) + shift


def _f_gates(p, a_log, dt_bias, *, heads):
    z = p + dt_bias
    g = -jnp.exp(a_log) * (jnp.maximum(z, 0.0) + jnp.log1p(jnp.exp(jnp.minimum(z, -z))))
    lane = lax.broadcasted_iota(jnp.int32, p.shape, 1)
    return (jnp.where(lane < heads, g, jax.nn.sigmoid(p)),)


def _f_gdn_out(o, z, w):
    return (_rms(o, w) * _silu(z),)


def _f_merge(ga, gb, ya, yb):
    return (jax.nn.sigmoid(ga) * ya + jax.nn.sigmoid(gb) * yb,)


def _f_swiglu(a, b):
    return (_silu(a) * b,)


def _f_loss(x2, ff, tgt, gate, shift, scale, w):
    y = _rms(x2 + gate * ff, w) * (1.0 + scale) + shift
    return (0.5 * jnp.mean(jnp.square(y - tgt), axis=-1, keepdims=True),)


def _shift_down(x, s):
    if s == 0:
        return x
    row = lax.broadcasted_iota(jnp.int32, x.shape, 0)
    return jnp.where(row >= s, pltpu.roll(x, s, 0), 0.0)


def _shift_up(x, s):
    if s == 0:
        return x
    n = x.shape[0]
    row = lax.broadcasted_iota(jnp.int32, x.shape, 0)
    return jnp.where(row < n - s, pltpu.roll(x, n - s, 0), 0.0)


def _conv(x, w):
    width = w.shape[0]
    acc = w[width - 1:width, :] * x
    for j in range(width - 1):
        acc = acc + w[j:j + 1, :] * _shift_down(x, width - 1 - j)
    return acc


def _conv_bwd(dy, x, w, dw_ref, first):
    width = w.shape[0]
    dx = w[width - 1:width, :] * dy
    for j in range(width - 1):
        dx = dx + w[j:j + 1, :] * _shift_up(dy, width - 1 - j)
    for j in range(width):
        row = jnp.sum(dy * _shift_down(x, width - 1 - j), axis=0, keepdims=True)
        _accumulate(dw_ref.at[j:j + 1, :], row, first)
    return dx


def _qkv_act(xc, is_v, scale):
    a = _silu(xc)
    nrm = a * lax.rsqrt(jnp.sum(a * a, axis=-1, keepdims=True) + EPS) * scale
    return jnp.where(is_v, a, nrm)


def _qkv_consts(j, heads):
    is_v = j >= 2 * heads
    scale = jnp.where(j < heads, HEAD ** -0.5, 1.0).astype(F32)
    return is_v, scale


def _qkv_fwd(p, w, heads, name):
    bl, s, w3 = p.shape

    def body(p_ref, w_ref, o_ref):
        is_v, scale = _qkv_consts(pl.program_id(0), heads)
        o_ref[...] = _qkv_act(_conv(p_ref[...], w_ref[...]), is_v, scale)

    blk = pl.BlockSpec((None, s, HEAD), lambda j, b: (b, 0, j))
    return pl.pallas_call(
        body, name=name, grid=(w3 // HEAD, bl), in_specs=[blk, pl.BlockSpec((w.shape[0], HEAD), lambda j, b: (0, j))],
        out_specs=blk, out_shape=jax.ShapeDtypeStruct(p.shape, F32), compiler_params=_cparams("parallel", "parallel"),
    )(p, w)


def _qkv_bwd(p, w, dout, heads, name):
    bl, s, w3 = p.shape

    def body(p_ref, w_ref, d_ref, dp_ref, dw_ref):
        is_v, scale = _qkv_consts(pl.program_id(0), heads)
        x, wv = p_ref[...], w_ref[...]
        _, vjp = jax.vjp(lambda xc: _qkv_act(xc, is_v, scale), _conv(x, wv))
        (dxc,) = vjp(d_ref[...])
        dp_ref[...] = _conv_bwd(dxc, x, wv, dw_ref, pl.program_id(1) == 0).astype(dp_ref.dtype)

    blk = pl.BlockSpec((None, s, HEAD), lambda j, b: (b, 0, j))
    wblk = pl.BlockSpec((w.shape[0], HEAD), lambda j, b: (0, j))
    return pl.pallas_call(
        body, name=name, grid=(w3 // HEAD, bl), in_specs=[blk, wblk, blk], out_specs=[blk, wblk],
        out_shape=[jax.ShapeDtypeStruct(p.shape, MXU_DTYPE), jax.ShapeDtypeStruct(w.shape, F32)],
        compiler_params=_cparams("arbitrary", "arbitrary"),
    )(p, w, dout)


def _sc_specs(p, w):
    bl, s, w3 = p.shape
    nblk = w3 // 3 // LANE
    sec = lambda k: pl.BlockSpec((None, s, LANE), functools.partial(lambda j, b, k: (b, 0, k * nblk + j), k=k))
    return nblk, [sec(0), sec(1), sec(2)], pl.BlockSpec((w.shape[0], LANE), lambda j, b: (0, j)), \
        pl.BlockSpec((None, s, LANE), lambda j, b: (b, 0, j))


def _sc_fwd(p, w, name):
    bl, s, w3 = p.shape
    nblk, secs, wblk, oblk = _sc_specs(p, w)

    def body(b_ref, c_ref, x_ref, w_ref, o_ref):
        o_ref[...] = (b_ref[...] * _conv(c_ref[...] * x_ref[...], w_ref[...])).astype(o_ref.dtype)

    return pl.pallas_call(
        body, name=name, grid=(nblk, bl), in_specs=secs + [wblk], out_specs=oblk,
        out_shape=jax.ShapeDtypeStruct((bl, s, w3 // 3), MXU_DTYPE), compiler_params=_cparams("parallel", "parallel"),
    )(p, p, p, w)


def _sc_bwd(p, w, dout, name):
    bl, s, w3 = p.shape
    nblk, secs, wblk, oblk = _sc_specs(p, w)

    def body(b_ref, c_ref, x_ref, w_ref, d_ref, db_ref, dc_ref, dx_ref, dw_ref):
        gb, gc, xin, wv, d = b_ref[...], c_ref[...], x_ref[...], w_ref[...], d_ref[...]
        u = gc * xin
        db_ref[...] = (d * _conv(u, wv)).astype(db_ref.dtype)
        du = _conv_bwd(d * gb, u, wv, dw_ref, pl.program_id(1) == 0)
        dc_ref[...] = (du * xin).astype(dc_ref.dtype)
        dx_ref[...] = (du * gc).astype(dx_ref.dtype)

    act = jax.ShapeDtypeStruct((bl, s, w3 // 3), MXU_DTYPE)
    return pl.pallas_call(
        body, name=name, grid=(nblk, bl), in_specs=secs + [wblk, oblk], out_specs=[oblk, oblk, oblk, wblk],
        out_shape=[act, act, act, jax.ShapeDtypeStruct(w.shape, F32)], compiler_params=_cparams("arbitrary", "arbitrary"),
    )(p, p, p, w, dout)


def _bdot(a, b, ca, cb):
    return lax.dot_general(a.astype(MXU_DTYPE), b.astype(MXU_DTYPE), (((ca,), (cb,)), ((), ())),
                           preferred_element_type=F32)


def _hdot(a, b):
    return lax.dot_general(a, b, (((1,), (0,)), ((), ())), precision=HIGHEST, preferred_element_type=F32)


def _lane_col(x, idx):
    lane = lax.broadcasted_iota(jnp.int32, x.shape, 1)
    return jnp.sum(jnp.where(lane == idx, x, 0.0), axis=1, keepdims=True)


def _chunk_masks():
    r = lax.broadcasted_iota(jnp.int32, (CHUNK, CHUNK), 0)
    c = lax.broadcasted_iota(jnp.int32, (CHUNK, CHUNK), 1)
    return r == c, r >= c, r > c


def _dot3(a, b):
    ah, bh = a.astype(MXU_DTYPE), b.astype(MXU_DTYPE)
    al, bl = (a - ah.astype(F32)).astype(MXU_DTYPE), (b - bh.astype(F32)).astype(MXU_DTYPE)
    dot = lambda x, y: lax.dot_general(x, y, (((1,), (0,)), ((), ())), preferred_element_type=F32)
    return dot(ah, bh) + (dot(ah, bl) + dot(al, bh))


def _tri_inv_steps(low, eye):
    x = -low
    p = jnp.where(eye, 1.0, 0.0) + x
    span = 2
    while span < CHUNK:
        x = _dot3(x, x)
        yield
        p = p + _dot3(p, x)
        yield
        span *= 2
    return p


def _round_robin(gens):
    out, live = [None] * len(gens), list(range(len(gens)))
    while live:
        still = []
        for i in live:
            try:
                next(gens[i])
                still.append(i)
            except StopIteration as stop:
                out[i] = stop.value
        live = still
    return out


def _gdn_pre(q, k, v, gc, beta, masks):
    eye, causal, strict = masks
    gc_row = jnp.sum(jnp.where(eye, gc, 0.0), axis=0, keepdims=True)
    decay = jnp.where(causal, jnp.exp(jnp.where(causal, gc - gc_row, 0.0)), 0.0)
    eg = jnp.exp(gc)
    gl = gc[CHUNK - 1:CHUNK, :]
    kb, vb = k * beta, v * beta
    low = jnp.where(strict, _bdot(kb, k, 1, 1) * decay, 0.0)
    qk = jnp.where(causal, _bdot(q, k, 1, 1) * decay, 0.0)
    rest = jnp.exp(gl - gc)
    return dict(decay=decay, eg=eg, gl=gl, kb=kb, vb=vb, kbe=kb * eg, low=low, qk=qk, qg=q * eg, rest=rest, kdec=k * rest)


def _gdn_specs(qkv, gbeta, heads, rev):
    bl, s, w3 = qkv.shape
    d, n = w3 // 3, s // CHUNK
    at = (lambda c: n - 1 - c) if rev else (lambda c: c)
    assert d == heads * HEAD
    sec = pl.BlockSpec((None, CHUNK, w3), lambda b, c: (b, at(c), 0))
    gspec = pl.BlockSpec((None, CHUNK, LANE), lambda b, c: (b, at(c), 0))
    sspec = pl.BlockSpec((None, None, heads, HEAD, HEAD), lambda b, c: (b, at(c), 0, 0, 0))
    tspec = pl.BlockSpec((None, None, heads, CHUNK, CHUNK), lambda b, c: (b, at(c), 0, 0, 0))
    return bl, s, d, n, sec, gspec, sspec, tspec


def _gdn_fwd(qkv, gbeta, heads, name):
    bl, s, d, n, sec, gspec, sspec, tspec = _gdn_specs(qkv, gbeta, heads, False)

    def body(x_ref, g_ref, o_ref, s_ref, t_ref, st_ref):
        @pl.when(pl.program_id(1) == 0)
        def _():
            st_ref[...] = jnp.zeros_like(st_ref)

        masks = _chunk_masks()
        eye, causal, _ = masks
        gblk = g_ref[...]
        gc_all = _hdot(jnp.where(causal, 1.0, 0.0), gblk)
        st_all = st_ref[...]

        def head(h):
            st = st_all[h]
            q, k, v = (x_ref[:, sec * d + h * HEAD:sec * d + (h + 1) * HEAD] for sec in range(3))
            pre = _gdn_pre(q, k, v, _lane_col(gc_all, h), _lane_col(gblk, heads + h), masks)
            yield
            t = yield from _tri_inv_steps(pre["low"], eye)
            u, w = _bdot(t, pre["vb"], 1, 0), _bdot(t, pre["kbe"], 1, 0)
            yield
            vnew = u - _bdot(w, st, 1, 0)
            yield
            out = _bdot(pre["qg"], st, 1, 0) + _bdot(pre["qk"], vnew, 1, 0)
            return out, t, st * jnp.exp(pre["gl"]) + _bdot(pre["kdec"], vnew, 0, 0)

        outs, ts, states = zip(*_round_robin([head(h) for h in range(heads)]))
        o_ref[...] = jnp.concatenate(outs, axis=1)
        s_ref[...] = st_all
        t_ref[...] = jnp.stack(ts)
        st_ref[...] = jnp.stack(states)

    return pl.pallas_call(
        body, name=name, grid=(bl, n), in_specs=[sec, gspec],
        out_specs=[pl.BlockSpec((None, CHUNK, d), lambda b, c: (b, c, 0)), sspec, tspec],
        out_shape=[jax.ShapeDtypeStruct((bl, s, d), F32), jax.ShapeDtypeStruct((bl, n, heads, HEAD, HEAD), F32),
                   jax.ShapeDtypeStruct((bl, n, heads, CHUNK, CHUNK), F32)],
        scratch_shapes=[pltpu.VMEM((heads, HEAD, HEAD), F32)], compiler_params=_cparams("arbitrary", "arbitrary"),
    )(qkv, gbeta)


def _gdn_bwd(qkv, gbeta, dout, s_all, t_all, heads, name):
    bl, s, d, n, sec, gspec, sspec, tspec = _gdn_specs(qkv, gbeta, heads, True)
    ospec = pl.BlockSpec((None, CHUNK, d), lambda b, c: (b, n - 1 - c, 0))

    def body(x_ref, g_ref, do_ref, s_ref, t_ref, dx_ref, dg_ref, ds_ref):
        @pl.when(pl.program_id(1) == 0)
        def _():
            ds_ref[...] = jnp.zeros_like(ds_ref)

        masks = _chunk_masks()
        eye, causal, strict = masks
        gblk = g_ref[...]
        gc_all = _hdot(jnp.where(causal, 1.0, 0.0), gblk)
        lane = lax.broadcasted_iota(jnp.int32, gblk.shape, 1)
        last_row = lax.broadcasted_iota(jnp.int32, (CHUNK, 1), 0) == CHUNK - 1
        rowsum = lambda a: jnp.sum(a, axis=1, keepdims=True)
        st_all, t_all_, ds_all = s_ref[...], t_ref[...], ds_ref[...]

        def head(h):
            sl = slice(h * HEAD, (h + 1) * HEAD)
            q, k, v = (x_ref[:, sec * d + h * HEAD:sec * d + (h + 1) * HEAD] for sec in range(3))
            do = do_ref[:, sl]
            beta = _lane_col(gblk, heads + h)
            st, t, dsn = st_all[h], t_all_[h], ds_all[h]
            pre = _gdn_pre(q, k, v, _lane_col(gc_all, h), beta, masks)
            decay, eg, kb, vb, kbe, low, qk, qg, kdec = (pre[x] for x in ("decay", "eg", "kb", "vb", "kbe", "low", "qk", "qg", "kdec"))
            egl = jnp.exp(pre["gl"])
            yield
            u, w = _bdot(t, vb, 1, 0), _bdot(t, kbe, 1, 0)
            yield
            vnew = u - _bdot(w, st, 1, 0)
            yield
            dkdec = _bdot(vnew, dsn, 1, 1)
            dvnew = _bdot(kdec, dsn, 1, 0) + _bdot(qk, do, 0, 0)
            dgl = jnp.sum(dsn * st, keepdims=True) * egl
            dqg = _bdot(do, st, 1, 1)
            dqk = jnp.where(causal, _bdot(do, vnew, 1, 1), 0.0)
            yield
            dw = -_bdot(dvnew, st, 1, 1)
            ds_new = dsn * egl + _bdot(qg, do, 0, 0) - _bdot(w, dvnew, 0, 0)
            yield
            dt = _bdot(dvnew, vb, 1, 1) + _bdot(dw, kbe, 1, 1)
            dvb, dkbe = _bdot(t, dvnew, 0, 0), _bdot(t, dw, 0, 0)
            yield
            inner = _bdot(dt, t, 1, 1)
            yield
            dlow = -jnp.where(strict, _bdot(t, inner, 0, 0), 0.0)
            da, db = dlow * decay, dqk * decay
            yield
            m = dlow * low + dqk * qk
            kdk = dkdec * kdec
            col_of_m = jnp.sum(jnp.where(eye, jnp.sum(m, axis=0, keepdims=True), 0.0), axis=1, keepdims=True)
            dgc = rowsum(m) - col_of_m + rowsum(dqg * qg) + rowsum(dkbe * kbe) - rowsum(kdk)
            dgc = dgc + jnp.where(last_row, dgl + jnp.sum(kdk, keepdims=True), 0.0)
            dkb = _bdot(da, k, 1, 0) + dkbe * eg
            yield
            dk = _bdot(da, kb, 0, 0) + _bdot(db, q, 0, 0) + dkdec * pre["rest"] + dkb * beta
            dq = _bdot(db, k, 1, 0) + dqg * eg
            dbeta = rowsum(dkb * k) + rowsum(dvb * v)
            return dq, dk, dvb * beta, jnp.where(lane == h, dgc, 0.0) + jnp.where(lane == heads + h, dbeta, 0.0), ds_new

        dqs, dks, dvs, dgs, dss = zip(*_round_robin([head(h) for h in range(heads)]))
        dx_ref[...] = jnp.concatenate(dqs + dks + dvs, axis=1)
        ds_ref[...] = jnp.stack(dss)
        dgb = dgs[0]
        for extra in dgs[1:]:
            dgb = dgb + extra
        upper = jnp.where(jnp.logical_or(eye, jnp.logical_not(causal)), 1.0, 0.0)
        dg_ref[...] = jnp.where(lane < heads, _hdot(upper, dgb), dgb)

    return pl.pallas_call(
        body, name=name, grid=(bl, n), in_specs=[sec, gspec, ospec, sspec, tspec], out_specs=[sec, gspec],
        out_shape=[jax.ShapeDtypeStruct(qkv.shape, F32), jax.ShapeDtypeStruct((bl, s, LANE), F32)],
        scratch_shapes=[pltpu.VMEM((heads, HEAD, HEAD), F32)], compiler_params=_cparams("arbitrary", "arbitrary"),
    )(qkv, gbeta, dout, s_all, t_all)


def _position():
    return lax.axis_index("x"), lax.axis_index("y"), lax.axis_index("c")


def _all_gather(x, *, name, hbm):
    space = pltpu.HBM if hbm else pltpu.VMEM

    def body(x_ref, out_ref, send_sems, recv_sems, local_sem):
        ax, ay, ac = _position()
        me, sibling = (ax, ay, ac), (ax, ay, 1 - ac)
        chips = [(1 - ax, ay), (ax, 1 - ay), (1 - ax, 1 - ay)]

        def slot(px, py, pc):
            return out_ref.at[4 * px + 2 * py + pc]

        def copy(k, block, to, src=None):
            return pltpu.make_async_remote_copy(
                src_ref=slot(*block) if src is None else src, dst_ref=slot(*block), send_sem=send_sems.at[k],
                recv_sem=recv_sems.at[k], device_id=to, device_id_type=MESH_IDS)

        mine = pltpu.make_async_copy(x_ref, slot(*me), local_sem)
        mine.start()
        first = [copy(0, me, sibling, src=x_ref)] + [copy(1 + j, me, (*chip, ac), src=x_ref) for j, chip in enumerate(chips)]
        for cp in first:
            cp.start()
        passed = [copy(4 + j, (*chip, ac), sibling) for j, chip in enumerate(chips)]
        for j, chip in enumerate(chips):
            copy(1 + j, (*chip, ac), me).wait_recv()
            passed[j].start()
        copy(0, sibling, me).wait_recv()
        for j, chip in enumerate(chips):
            copy(4 + j, (*chip, 1 - ac), me).wait_recv()
        for cp in first + passed:
            cp.wait_send()
        mine.wait()

    return pl.pallas_call(
        body, name=name, out_shape=jax.ShapeDtypeStruct((NDEV,) + x.shape, x.dtype),
        in_specs=[pl.BlockSpec(memory_space=space)], out_specs=pl.BlockSpec(memory_space=space),
        scratch_shapes=[pltpu.SemaphoreType.DMA((7,)), pltpu.SemaphoreType.DMA((7,)), pltpu.SemaphoreType.DMA],
    )(x)


ROW_ALIGN = 16


def _window_start(rows_per_dev, k):
    return rows_per_dev * k // ROW_ALIGN * ROW_ALIGN


def _exchange_in_chip(parts, name):
    n = len(parts)
    packed = sum(win for _, _, win, _ in parts)
    width, dtype = parts[0][0].shape[1], parts[0][0].dtype

    def body(*refs):
        g_refs, recv_ref, send_sems, recv_sems = refs[:n], *refs[n:]
        ax, ay, ac = _position()
        sibling = (ax, ay, 1 - ac)
        for q in range(4):
            for g_ref, (_, r, win, off) in zip(g_refs, parts):
                there = g_ref.at[pl.ds(pl.multiple_of(_window_start(r, 2 * q + 1 - ac), ROW_ALIGN), win)]
                pltpu.make_async_remote_copy(src_ref=there, dst_ref=recv_ref.at[q, pl.ds(off, win)], send_sem=send_sems.at[q],
                                             recv_sem=recv_sems.at[q], device_id=sibling, device_id_type=MESH_IDS).start()
        for q in range(4):
            pltpu.make_async_remote_copy(src_ref=recv_ref.at[q], dst_ref=recv_ref.at[q], send_sem=send_sems.at[q],
                                         recv_sem=recv_sems.at[q], device_id=sibling, device_id_type=MESH_IDS).wait()

    hbm = pl.BlockSpec(memory_space=pltpu.HBM)
    return pl.pallas_call(
        body, name=name, out_shape=jax.ShapeDtypeStruct((4, packed, width), dtype), in_specs=[hbm] * n, out_specs=hbm,
        scratch_shapes=[pltpu.SemaphoreType.DMA((4,)), pltpu.SemaphoreType.DMA((4,))],
    )(*[g for g, _, _, _ in parts])


def _exchange_chips(s1, name):
    def body(s_ref, recv_ref, send_sems, recv_sems):
        ax, ay, ac = _position()
        chips = [(1 - ax, ay), (ax, 1 - ay), (1 - ax, 1 - ay)]
        copies = [pltpu.make_async_remote_copy(
            src_ref=s_ref.at[2 * cx + cy], dst_ref=recv_ref.at[r], send_sem=send_sems.at[r], recv_sem=recv_sems.at[r],
            device_id=(cx, cy, ac), device_id_type=MESH_IDS) for r, (cx, cy) in enumerate(chips)]
        for cp in copies:
            cp.start()
        for cp in copies:
            cp.wait_recv()
        for cp in copies:
            cp.wait_send()

    hbm = pl.BlockSpec(memory_space=pltpu.HBM)
    return pl.pallas_call(
        body, name=name, out_shape=jax.ShapeDtypeStruct((3,) + s1.shape[1:], s1.dtype), in_specs=[hbm], out_specs=hbm,
        scratch_shapes=[pltpu.SemaphoreType.DMA((3,)), pltpu.SemaphoreType.DMA((3,))],
    )(s1)


def _sum_in_chip(own, recv, name):
    _, r, w = own.shape
    tr = _tile(r, (256, 128))

    def body(a_ref, b_ref, o_ref):
        o_ref[...] = (a_ref[...].astype(F32) + b_ref[...].astype(F32)).astype(o_ref.dtype)

    blk = pl.BlockSpec((None, tr, w), lambda q, i: (q, i, 0))
    return pl.pallas_call(body, name=name, grid=(4, r // tr), in_specs=[blk, blk], out_specs=blk,
                          out_shape=jax.ShapeDtypeStruct(own.shape, own.dtype),
                          compiler_params=_cparams("parallel", "parallel"))(own, recv)


def _sum_chips(s1, recv, chip, name):
    _, r, w = s1.shape
    tr = _tile(r, (256, 128))

    def body(c_ref, s_ref, r0_ref, r1_ref, r2_ref, o_ref):
        f = lambda ref: ref[...].astype(F32)
        o_ref[...] = ((f(s_ref) + f(r0_ref)) + f(r1_ref)) + f(r2_ref)

    rblk = lambda k: pl.BlockSpec((None, tr, w), functools.partial(lambda i, c, k: (k, i, 0), k=k))
    grid_spec = pltpu.PrefetchScalarGridSpec(
        num_scalar_prefetch=1, grid=(r // tr,),
        in_specs=[pl.BlockSpec((None, tr, w), lambda i, c: (c[0], i, 0)), rblk(0), rblk(1), rblk(2)],
        out_specs=pl.BlockSpec((tr, w), lambda i, c: (i, 0)))
    return pl.pallas_call(body, name=name, grid_spec=grid_spec, out_shape=jax.ShapeDtypeStruct((r, w), F32),
                          compiler_params=_cparams("parallel"))(chip, s1, recv, recv, recv)


def _silu_rows(x, name):
    def body(x_ref, o_ref):
        o_ref[...] = _silu(x_ref[...])

    return pl.pallas_call(body, name=name, out_shape=jax.ShapeDtypeStruct(x.shape, F32))(x)


def _row_sum(x, name):
    def body(x_ref, o_ref):
        acc = x_ref[0:1, :]
        for i in range(1, x.shape[0]):
            acc = acc + x_ref[i:i + 1, :]
        o_ref[...] = acc

    return pl.pallas_call(body, name=name, out_shape=jax.ShapeDtypeStruct((1, x.shape[1]), F32))(x)


def _adamw(w, g, m, v, name):
    cols = w.shape[-1]
    rows = w.size // cols
    tr = _tile(rows, (128,))

    def body(w_ref, g_ref, m_ref, v_ref, d_ref, mo_ref, vo_ref):
        grad = g_ref[...]
        m_new = ADAM_B1 * m_ref[...] + (1.0 - ADAM_B1) * grad
        v_new = ADAM_B2 * v_ref[...] + (1.0 - ADAM_B2) * jnp.square(grad)
        m_hat = m_new / (1.0 - ADAM_B1 ** ADAM_STEP)
        v_hat = v_new / (1.0 - ADAM_B2 ** ADAM_STEP)
        d_ref[...] = -ADAM_LR * (m_hat / (jnp.sqrt(v_hat) + ADAM_EPS) + ADAM_WD * w_ref[...])
        mo_ref[...] = m_new
        vo_ref[...] = v_new

    blk = pl.BlockSpec((tr, cols), lambda i: (i, 0))
    out = pl.pallas_call(
        body, name=name, grid=(rows // tr,), in_specs=[blk] * 4, out_specs=[blk] * 3,
        out_shape=[jax.ShapeDtypeStruct((rows, cols), F32)] * 3, compiler_params=_cparams("parallel"),
    )(*[t.reshape(rows, cols) for t in (w, g, m, v)])
    return [t.reshape(w.shape) for t in out]


def _pack(parts, width, row_mult, dtype):
    flat = jnp.concatenate([p.reshape(-1).astype(dtype) for p in parts])
    rows = -(-flat.shape[0] // (width * row_mult)) * row_mult
    return jnp.pad(flat, (0, rows * width - flat.shape[0])).reshape(rows, width)


def _unpack(flat, shapes):
    out, off = [], 0
    for shp in shapes:
        size = 1
        for dim in shp:
            size *= dim
        out.append(flat[:, off:off + size].reshape((flat.shape[0],) + tuple(shp)))
        off += size
    return out


def _devices_to_cols(a):
    _, r, c = a.shape
    return a.transpose(1, 0, 2).reshape(r, NDEV * c)


def kernel(x, c, w_ada, b_ada, norm1_w, w_in, gdn_conv_w, gdn_a_log, gdn_dt_bias, gdn_norm_w, w_gdn_proj, sc_conv_w, w_sc_out, w_o, norm2_w, w_ffn_in, w_ffn_out, w_ada_f, b_ada_f, normf_w, loss_target, m_w_ada, m_b_ada, m_norm1_w, m_w_in, m_gdn_conv_w, m_gdn_a_log, m_gdn_dt_bias, m_gdn_norm_w, m_w_gdn_proj, m_sc_conv_w, m_w_sc_out, m_w_o, m_norm2_w, m_w_ffn_in, m_w_ffn_out, m_w_ada_f, m_b_ada_f, m_normf_w, v_w_ada, v_b_ada, v_norm1_w, v_w_in, v_gdn_conv_w, v_gdn_a_log, v_gdn_dt_bias, v_gdn_norm_w, v_w_gdn_proj, v_sc_conv_w, v_w_sc_out, v_w_o, v_norm2_w, v_w_ffn_in, v_w_ffn_out, v_w_ada_f, v_b_ada_f, v_normf_w):
    bl, s, d = x.shape
    heads = gdn_a_log.shape[-1]
    dff = w_ffn_out.shape[1] * NDEV
    tok = bl * s
    ax, ay, ac = _position()
    dev = 4 * ax + 2 * ay + ac
    as_tok = lambda a: a.reshape(bl, s, a.shape[-1])
    as_mat = lambda a: a.reshape(tok, a.shape[-1])

    small = _all_gather(_pack([c, gdn_conv_w, sc_conv_w], LANE, 8, F32), name="gather_cond", hbm=False)
    c_all, conv_w, sc_w = _unpack(small.reshape(NDEV, -1), [(bl, d), gdn_conv_w.shape[1:], sc_conv_w.shape[1:]])
    c_act = _silu_rows(c_all.reshape(NDEV * bl, d), "cond_silu")
    conv_w, sc_w = _devices_to_cols(conv_w), _devices_to_cols(sc_w)
    n_ada, n_adaf = w_ada.shape[-1], w_ada_f.shape[-1]
    bias = jnp.broadcast_to(lax.dynamic_slice_in_dim(b_ada, dev * n_ada, n_ada, axis=1), (NDEV * bl, n_ada))
    biasf = jnp.broadcast_to(lax.dynamic_slice_in_dim(b_ada_f.reshape(1, -1), dev * n_adaf, n_adaf, axis=1), (NDEV * bl, n_adaf))
    mod_cols = _mm(c_act, w_ada[0], add=bias, name="ada_cols")
    modf_cols = _mm(c_act, w_ada_f, add=biasf, name="adaf_cols")
    mods = _all_gather(jnp.concatenate([mod_cols, modf_cols], axis=1), name="gather_mod", hbm=False)
    mod_all = mods[:, :, :n_ada].transpose(1, 0, 2).reshape(NDEV * bl, NDEV * n_ada)
    modf_all = mods[:, :, n_ada:].transpose(1, 0, 2).reshape(NDEV * bl, NDEV * n_adaf)
    my_rows = lambda a: lax.dynamic_slice_in_dim(a, dev * bl, bl, axis=0)
    sh1, sc1, g1, sh2, sc2, g2 = [t.reshape(bl, 1, d) for t in jnp.split(my_rows(mod_all), 6, axis=1)]
    shf, scf = [t.reshape(bl, 1, d) for t in jnp.split(my_rows(modf_all), 2, axis=1)]

    big = [w_gdn_proj[0], w_sc_out[0], w_o[0], w_ffn_in[0].T, w_ffn_out[0], w_in[0].T]
    rows = [t.shape[0] for t in big]
    offs = [sum(rows[:i]) for i in range(len(rows))]
    pad_rows = -sum(rows) % 128
    send = jnp.concatenate([t.astype(MXU_DTYPE) for t in big] + [jnp.zeros((pad_rows, d), MXU_DTYPE)], axis=0)
    gathered = _all_gather(send, name="gather_weights", hbm=True)
    wgp, wso, wo, wt_fi, wfo, wt_in = [gathered[:, offs[i]:offs[i] + rows[i], :].reshape(NDEV * rows[i], d) for i in range(6)]
    o_z, o_ab, o_sc, o_ga, o_gb = 3 * d, 4 * d, 4 * d + 2 * heads, 7 * d + 2 * heads, 8 * d + 2 * heads
    wt_qkv, wt_z = wt_in[:o_z], wt_in[o_z:o_ab]
    wt_ab = jnp.pad(wt_in[o_ab:o_sc], ((0, LANE - 2 * heads), (0, 0)))
    wt_scs = [wt_in[o_sc + k * d:o_sc + (k + 1) * d] for k in range(3)]
    wt_ga, wt_gb = wt_in[o_ga:o_gb], wt_in[o_gb:]
    wt_fa, wt_fb = wt_fi[:dff], wt_fi[dff:]

    n1w, n2w, nfw = norm1_w.reshape(1, d), norm2_w.reshape(1, d), normf_w.reshape(1, d)
    lanes = lambda a: jnp.pad(a.reshape(1, -1), ((0, 0), (0, LANE - a.size)))
    a_log, dt_bias, gnw = lanes(gdn_a_log), lanes(gdn_dt_bias), gdn_norm_w.reshape(1, HEAD)
    f_gates = functools.partial(_f_gates, heads=heads)
    (h1,) = _tok_fwd(_f_norm_mod, [x], [sh1, sc1], [n1w], [(d, MXU_DTYPE)], name="norm1", ts=256)
    h1m = as_mat(h1)
    p_qkv = as_tok(_mm(h1m, wt_qkv, tb=True, name="in_qkv"))
    p_z = as_tok(_mm(h1m, wt_z, tb=True, name="in_z"))
    p_ab = as_tok(_mm(h1m, wt_ab, tb=True, name="in_ab"))
    p_sc = as_tok(_mm(h1m, wt_in[o_sc:o_ga], tb=True, name="in_sc"))
    p_g = as_tok(_mm(h1m, wt_in[o_ga:], tb=True, name="in_gate"))
    qkv = _qkv_fwd(p_qkv, conv_w, heads, "qkv_conv")
    (gbeta,) = _tok_fwd(f_gates, [p_ab], [], [a_log, dt_bias], [(LANE, F32)], name="gates", ts=512)
    o, s_all, t_all = _gdn_fwd(qkv, gbeta, heads, "gdn")
    (og,) = _tok_fwd(_f_gdn_out, [o, p_z], [], [(gnw, None)], [(d, MXU_DTYPE)], name="gdn_out", ts=512, wb=HEAD, cols=heads)
    y_a = as_tok(_mm(as_mat(og), wgp, name="gdn_proj"))
    scp = _sc_fwd(p_sc, sc_w, "sc_conv")
    y_b = as_tok(_mm(as_mat(scp), wso, name="sc_out"))
    mcols = d // 512 if d % 512 == 0 else 1
    mwb = d // mcols
    merge_toks = [(p_g, 0), (p_g, mcols), y_a, y_b]
    (mrg,) = _tok_fwd(_f_merge, merge_toks, [], [], [(d, MXU_DTYPE)], name="merge", ts=256, wb=mwb, cols=mcols)
    mix = as_tok(_mm(as_mat(mrg), wo, name="mix_out"))
    x2, h2 = _tok_fwd(_f_res_norm_mod, [x, mix], [g1, sh2, sc2], [n2w], [(d, F32), (d, MXU_DTYPE)], name="norm2", ts=256)
    gu = as_tok(_mm(as_mat(h2), wt_fi, tb=True, name="ffn_in"))
    fwb = _tile(dff, (256, 128))
    fcols = dff // fwb
    (act,) = _tok_fwd(_f_swiglu, [(gu, 0), (gu, fcols)], [], [], [(dff, MXU_DTYPE)], name="swiglu", ts=512, wb=fwb, cols=fcols)
    ff = as_tok(_mm(as_mat(act), wfo, name="ffn_out"))

    loss_l, (dx2, dff_out, _), (dg2, dshf, dscf), (dnfw,) = _tok_bwd(
        _f_loss, [x2, ff, loss_target], [g2, shf, scf], [nfw], [], [True, True, False], name="loss", ts=256, loss=True,
        tok_dtype=[F32, MXU_DTYPE, None])
    dffm = as_mat(dff_out)
    dact = as_tok(_mm(dffm, wfo, tb=True, name="d_ffn_out"))
    gmm = functools.partial(_mm, ta=True, out_dtype=MXU_DTYPE)
    gw_ffn_out = gmm(as_mat(act), dffm, name="g_ffn_out")
    (dgu_a, dgu_b), _, _ = _tok_bwd(_f_swiglu, [(gu, 0), (gu, fcols)], [], [], [dact], [True, True], name="d_swiglu",
                                    ts=512, wb=fwb, cols=fcols, tok_dtype=MXU_DTYPE)
    dh2 = _mm(as_mat(dgu_a), wt_fa, name="d_ffn_in_a")
    dh2 = as_tok(_mm(as_mat(dgu_b), wt_fb, add=dh2, name="d_ffn_in_b"))
    h2m = as_mat(h2)
    gwt_ffn_in = jnp.concatenate([gmm(as_mat(dgu_a), h2m, name="g_ffn_in_a"),
                                  gmm(as_mat(dgu_b), h2m, name="g_ffn_in_b")], axis=0)
    (dx_skip, dmix), (dg1, dsh2, dsc2), (dn2w,) = _tok_bwd(
        _f_res_norm_mod, [x, mix], [g1, sh2, sc2], [n2w], [dx2, dh2], [True, True], name="d_norm2", ts=256,
        tok_dtype=[F32, MXU_DTYPE])
    dmixm = as_mat(dmix)
    dmrg = as_tok(_mm(dmixm, wo, tb=True, name="d_mix_out"))
    gw_o = gmm(as_mat(mrg), dmixm, name="g_mix_out")
    (dga, dgb, dya, dyb), _, _ = _tok_bwd(_f_merge, merge_toks, [], [], [dmrg], [True] * 4, name="d_merge", ts=256,
                                          wb=mwb, cols=mcols, tok_dtype=MXU_DTYPE)
    dyam, dybm = as_mat(dya), as_mat(dyb)
    dog = as_tok(_mm(dyam, wgp, tb=True, name="d_gdn_proj"))
    gw_gdn_proj = gmm(as_mat(og), dyam, name="g_gdn_proj")
    dscp = as_tok(_mm(dybm, wso, tb=True, name="d_sc_out"))
    gw_sc_out = gmm(as_mat(scp), dybm, name="g_sc_out")
    dscb, dscc, dscx, g_sc_w = _sc_bwd(p_sc, sc_w, dscp, "d_sc_conv")
    (do, dz), _, (g_gnw,) = _tok_bwd(_f_gdn_out, [o, p_z], [], [(gnw, None)], [dog], [True, True], name="d_gdn_out",
                                     ts=512, wb=HEAD, cols=heads, tok_dtype=[F32, MXU_DTYPE])
    dqkv, dgbeta = _gdn_bwd(qkv, gbeta, do, s_all, t_all, heads, "d_gdn")
    dp_qkv, g_conv_w = _qkv_bwd(p_qkv, conv_w, dqkv, heads, "d_qkv_conv")
    (dp_ab,), _, (g_a_log, g_dt_bias) = _tok_bwd(f_gates, [p_ab], [], [a_log, dt_bias], [dgbeta], [True], name="d_gates",
                                                 ts=512, tok_dtype=MXU_DTYPE)
    sections = [(dp_qkv, wt_qkv), (dz, wt_z), (dp_ab, wt_ab), (dscb, wt_scs[0]), (dscc, wt_scs[1]), (dscx, wt_scs[2]),
                (dga, wt_ga), (dgb, wt_gb)]
    dh1, gwt_in = None, []
    for k, (dp, wsec) in enumerate(sections):
        dh1 = _mm(as_mat(dp), wsec, add=dh1, name=f"d_in_{k}")
        gwt_in.append(gmm(as_mat(dp), h1m, name=f"g_in_{k}"))
    gwt_in[2] = gwt_in[2][:2 * heads]
    gwt_in = jnp.concatenate(gwt_in, axis=0)
    (grad_x,), (dsh1, dsc1), (dn1w,) = _tok_bwd(_f_norm_mod_skip, [x], [sh1, sc1], [n1w], [as_tok(dh1), dx_skip], [True],
                                                name="d_norm1", ts=256)

    grads_t = [gw_gdn_proj, gw_sc_out, gw_o, gwt_ffn_in, gw_ffn_out, gwt_in]
    wins = [r + max(r * k % ROW_ALIGN for k in range(NDEV)) for r in rows]
    wins = [-(-w // ROW_ALIGN) * ROW_ALIGN for w in wins]
    wins[-1] += -sum(wins) % 128
    need_rows = max(_window_start(rows[-1], k) for k in range(NDEV)) + wins[-1]
    grads_t[-1] = jnp.pad(gwt_in, ((0, need_rows - gwt_in.shape[0]), (0, 0)))
    packed_offs = [sum(wins[:i]) for i in range(len(wins))]
    recv1 = _exchange_in_chip(list(zip(grads_t, rows, wins, packed_offs)), "scatter_in_chip")
    own = jnp.stack([jnp.concatenate([lax.dynamic_slice_in_dim(g, _window_start(r, 2 * q + ac), w, axis=0)
                                      for g, r, w in zip(grads_t, rows, wins)], axis=0) for q in range(4)])
    s1 = _sum_in_chip(own, recv1, "sum_in_chip")
    recv2 = _exchange_chips(s1, "scatter_chips")
    reduced = _sum_chips(s1, recv2, (2 * ax + ay).reshape(1).astype(jnp.int32), "sum_chips")
    g_w_gdn_proj, g_w_sc_out, g_w_o, gt_w_ffn_in, g_w_ffn_out = [reduced[packed_offs[i]:packed_offs[i] + rows[i]] for i in range(5)]
    gt_w_in = lax.dynamic_slice_in_dim(reduced[packed_offs[5]:], rows[5] * dev - _window_start(rows[5], dev), rows[5], axis=0)
    g_w_gdn_proj, g_w_sc_out, g_w_o, g_w_ffn_out = (
        t.reshape(ref.shape) for t, ref in zip((g_w_gdn_proj, g_w_sc_out, g_w_o, g_w_ffn_out), (w_gdn_proj, w_sc_out, w_o, w_ffn_out)))
    g_w_in, g_w_ffn_in = gt_w_in.T.reshape(w_in.shape), gt_w_ffn_in.T.reshape(w_ffn_in.shape)

    dmod = jnp.concatenate([t.reshape(bl, d) for t in (dsh1, dsc1, dg1, dsh2, dsc2, dg2)], axis=1)
    dmodf = jnp.concatenate([t.reshape(bl, d) for t in (dshf, dscf)], axis=1)
    summed_parts = [dn1w, dn2w, dnfw, g_gnw, g_a_log, g_dt_bias, g_conv_w, g_sc_w, loss_l]
    partial = _all_gather(_pack([dmod, dmodf] + summed_parts, LANE, 8, F32), name="gather_small", hbm=False)
    partial = partial.reshape(NDEV, -1)
    n_rows = bl * (6 * d + 2 * d)
    dmod_all, dmodf_all = _unpack(partial[:, :n_rows], [(bl, 6 * d), (bl, 2 * d)])
    dmod_all, dmodf_all = dmod_all.reshape(NDEV * bl, 6 * d), dmodf_all.reshape(NDEV * bl, 2 * d)
    totals = _row_sum(partial[:, n_rows:], "sum_small")
    t_n1w, t_n2w, t_nfw, t_gnw, t_a_log, t_dt_bias, t_conv_w, t_sc_w, t_loss = [
        t[0] for t in _unpack(totals, [p.shape for p in summed_parts])]
    my_cols = lambda a, n: lax.dynamic_slice_in_dim(a, dev * n, n, axis=1)
    grads = {
        "w_ada": _mm(c_act, my_cols(dmod_all, n_ada), ta=True, name="g_ada").reshape(w_ada.shape),
        "b_ada": _row_sum(dmod_all, "g_ada_bias").reshape(b_ada.shape),
        "norm1_w": t_n1w.reshape(norm1_w.shape),
        "w_in": g_w_in,
        "gdn_conv_w": my_cols(t_conv_w, gdn_conv_w.shape[-1]).reshape(gdn_conv_w.shape),
        "gdn_a_log": t_a_log[:, :heads].reshape(gdn_a_log.shape),
        "gdn_dt_bias": t_dt_bias[:, :heads].reshape(gdn_dt_bias.shape),
        "gdn_norm_w": t_gnw.reshape(gdn_norm_w.shape),
        "w_gdn_proj": g_w_gdn_proj,
        "sc_conv_w": my_cols(t_sc_w, sc_conv_w.shape[-1]).reshape(sc_conv_w.shape),
        "w_sc_out": g_w_sc_out,
        "w_o": g_w_o,
        "norm2_w": t_n2w.reshape(norm2_w.shape),
        "w_ffn_in": g_w_ffn_in,
        "w_ffn_out": g_w_ffn_out,
        "w_ada_f": _mm(c_act, my_cols(dmodf_all, n_adaf), ta=True, name="g_adaf").reshape(w_ada_f.shape),
        "b_ada_f": _row_sum(dmodf_all, "g_adaf_bias").reshape(b_ada_f.shape),
        "normf_w": t_nfw.reshape(normf_w.shape),
    }
    weights = dict(w_ada=w_ada, b_ada=b_ada, norm1_w=norm1_w, w_in=w_in, gdn_conv_w=gdn_conv_w, gdn_a_log=gdn_a_log,
                   gdn_dt_bias=gdn_dt_bias, gdn_norm_w=gdn_norm_w, w_gdn_proj=w_gdn_proj, sc_conv_w=sc_conv_w,
                   w_sc_out=w_sc_out, w_o=w_o, norm2_w=norm2_w, w_ffn_in=w_ffn_in, w_ffn_out=w_ffn_out, w_ada_f=w_ada_f,
                   b_ada_f=b_ada_f, normf_w=normf_w)
    m_in = [m_w_ada, m_b_ada, m_norm1_w, m_w_in, m_gdn_conv_w, m_gdn_a_log, m_gdn_dt_bias, m_gdn_norm_w, m_w_gdn_proj,
            m_sc_conv_w, m_w_sc_out, m_w_o, m_norm2_w, m_w_ffn_in, m_w_ffn_out, m_w_ada_f, m_b_ada_f, m_normf_w]
    v_in = [v_w_ada, v_b_ada, v_norm1_w, v_w_in, v_gdn_conv_w, v_gdn_a_log, v_gdn_dt_bias, v_gdn_norm_w, v_w_gdn_proj,
            v_sc_conv_w, v_w_sc_out, v_w_o, v_norm2_w, v_w_ffn_in, v_w_ffn_out, v_w_ada_f, v_b_ada_f, v_normf_w]
    deltas, new_m, new_v = [], [], []
    for (wname, wt), mt, vt in zip(weights.items(), m_in, v_in):
        dl, mn, vn = _adamw(wt, grads[wname], mt, vt, "adamw_" + wname)
        deltas.append(dl)
        new_m.append(mn)
        new_v.append(vn)
    loss = t_loss[0, 0]
    return (loss, grad_x, *[grads[k] for k in weights], *deltas, *new_m, *new_v)
```

```python
import functools

import jax
import jax.numpy as jnp
from jax import lax
from jax.experimental import pallas as pl
from jax.experimental.pallas import tpu as pltpu

F32 = jnp.float32
MXU_DTYPE = jnp.bfloat16
NDEV = 8
CHUNK = 64
HEAD = 128
LANE = 128
EPS = 1e-6
ADAM_LR, ADAM_B1, ADAM_B2, ADAM_EPS, ADAM_WD, ADAM_STEP = 0.001, 0.9, 0.999, 1e-08, 0.01, 10
VMEM_LIMIT = 48 * 1024 * 1024
MESH_IDS = pl.DeviceIdType.MESH
HIGHEST = lax.Precision.HIGHEST


def _tile(n, cands=(512, 256, 128)):
    for c in cands:
        if n % c == 0:
            return c
    return n


def _cparams(*sem):
    return pltpu.CompilerParams(dimension_semantics=sem, vmem_limit_bytes=VMEM_LIMIT)


def _mm(a, b, *, ta=False, tb=False, add=None, out_dtype=F32, name):
    m, k = (a.shape[1], a.shape[0]) if ta else a.shape
    n = b.shape[0] if tb else b.shape[1]
    if ta:
        tm, tn = _tile(m), n if n <= 1024 else _tile(n)
        tk = k if k <= 2048 else _tile(k, (2048, 1024, 512))
    else:
        tm, tn = _tile(m, (1024, 512, 256, 128)), _tile(n)
        tk = k if k <= 1024 else _tile(k, (1024, 512))
    nk = k // tk
    dims = (((0 if ta else 1,), (1 if tb else 0,)), ((), ()))
    has_add = add is not None

    def body(*refs):
        a_ref, b_ref = refs[0], refs[1]
        add_ref = refs[2] if has_add else None
        o_ref = refs[3] if has_add else refs[2]
        part = lax.dot_general(a_ref[...].astype(MXU_DTYPE), b_ref[...].astype(MXU_DTYPE), dims,
                               preferred_element_type=F32)

        def finish(acc):
            if has_add:
                acc = acc + add_ref[...]
            o_ref[...] = acc.astype(o_ref.dtype)

        if nk == 1:
            finish(part)
        else:
            acc_ref = refs[-1]
            kk = pl.program_id(2)

            @pl.when(kk == 0)
            def _():
                acc_ref[...] = part

            @pl.when(kk > 0)
            def _():
                acc_ref[...] += part

            @pl.when(kk == nk - 1)
            def _():
                finish(acc_ref[...])

    a_spec = pl.BlockSpec((tk, tm), lambda i, j, kk: (kk, i)) if ta else pl.BlockSpec((tm, tk), lambda i, j, kk: (i, kk))
    b_spec = pl.BlockSpec((tn, tk), lambda i, j, kk: (j, kk)) if tb else pl.BlockSpec((tk, tn), lambda i, j, kk: (kk, j))
    o_spec = pl.BlockSpec((tm, tn), lambda i, j, kk: (i, j))
    in_specs = [a_spec, b_spec] + ([o_spec] if has_add else [])
    args = [a, b] + ([add] if has_add else [])
    return pl.pallas_call(
        body, name=name, grid=(m // tm, n // tn, nk), in_specs=in_specs, out_specs=o_spec,
        out_shape=jax.ShapeDtypeStruct((m, n), out_dtype),
        scratch_shapes=[pltpu.VMEM((tm, tn), F32)] if nk > 1 else [],
        compiler_params=_cparams("parallel", "parallel", "arbitrary"),
    )(*args)


def _with_off(xs):
    return [x if isinstance(x, tuple) else (x, 0) for x in xs]


def _spec(kind, arr, off, ts, wb):
    w = arr.shape[-1] if wb is None else wb
    col = (lambda j: 0) if wb is None else functools.partial(lambda j, o: o + j, o=off)
    if kind == "tok":
        return pl.BlockSpec((None, ts, w), lambda j, b, i: (b, i, col(j)))
    if kind == "bat":
        return pl.BlockSpec((None, 1, w), lambda j, b, i: (b, 0, col(j)))
    if off is None:
        return pl.BlockSpec(arr.shape, lambda j, b, i: (0, 0))
    return pl.BlockSpec((arr.shape[0], w), lambda j, b, i: (0, col(j)))


def _in_specs(toks, bats, pars, cots, ts, wb):
    return ([_spec("tok", a, o, ts, wb) for a, o in toks] + [_spec("bat", a, o, ts, wb) for a, o in bats]
            + [_spec("par", a, o, ts, wb) for a, o in pars] + [_spec("tok", a, o, ts, wb) for a, o in cots])


def _tok_fwd(fn, toks, bats, pars, outs, *, name, ts, wb=None, cols=1):
    toks, bats, pars = _with_off(toks), _with_off(bats), _with_off(pars)
    bl, s, _ = toks[0][0].shape
    ts = min(ts, s)
    n_in = len(toks) + len(bats) + len(pars)

    def body(*refs):
        res = fn(*[r[...].astype(F32) for r in refs[:n_in]])
        for r, val in zip(refs[n_in:], res):
            r[...] = val.astype(r.dtype)

    out_specs = [pl.BlockSpec((None, ts, w if wb is None else wb), lambda j, b, i: (b, i, j)) for w, _ in outs]
    return pl.pallas_call(
        body, name=name, grid=(cols, bl, s // ts), in_specs=_in_specs(toks, bats, pars, [], ts, wb),
        out_specs=out_specs, out_shape=[jax.ShapeDtypeStruct((bl, s, w), dt) for w, dt in outs],
        compiler_params=_cparams("parallel", "parallel", "parallel"),
    )(*[a for a, _ in toks + bats + pars])


def _accumulate(ref, val, first):
    @pl.when(first)
    def _():
        ref[...] = val

    @pl.when(jnp.logical_not(first))
    def _():
        ref[...] += val


def _tok_bwd(fn, toks, bats, pars, cots, need, *, name, ts, wb=None, cols=1, tok_dtype=F32, loss=False):
    toks, bats, pars, cots = _with_off(toks), _with_off(bats), _with_off(pars), _with_off(cots)
    bl, s, _ = toks[0][0].shape
    ts = min(ts, s)
    nt, nb, npar, nc = len(toks), len(bats), len(pars), len(cots)
    n_in = nt + nb + npar

    def body(*refs):
        j, b, i = pl.program_id(0), pl.program_id(1), pl.program_id(2)
        outs, vjp = jax.vjp(fn, *[r[...].astype(F32) for r in refs[:n_in]])
        o = n_in + nc
        if loss:
            ct = (jnp.ones_like(outs[0]),)
            tot = jnp.broadcast_to(jnp.sum(outs[0], keepdims=True), (1, LANE))
            _accumulate(refs[o], tot, jnp.logical_and(b == 0, i == 0))
            o += 1
        else:
            ct = tuple(r[...].astype(F32) for r in refs[n_in:n_in + nc])
        grads = vjp(ct)
        for t in range(nt):
            if need[t]:
                refs[o][...] = grads[t].astype(refs[o].dtype)
                o += 1
        for t in range(nb):
            _accumulate(refs[o], grads[nt + t], i == 0)
            o += 1
        for t in range(npar):
            first = jnp.logical_and(b == 0, i == 0)
            if pars[t][1] is None:
                first = jnp.logical_and(first, j == 0)
            _accumulate(refs[o], grads[nt + nb + t], first)
            o += 1

    full = lambda arr: arr.shape[-1] if wb is None else wb * cols
    blk = lambda arr: arr.shape[-1] if wb is None else wb
    out_specs, out_shape = [], []
    if loss:
        out_specs.append(pl.BlockSpec((1, LANE), lambda j, b, i: (0, 0)))
        out_shape.append(jax.ShapeDtypeStruct((1, LANE), F32))
    for t in range(nt):
        if need[t]:
            out_specs.append(pl.BlockSpec((None, ts, blk(toks[t][0])), lambda j, b, i: (b, i, j)))
            dt = tok_dtype[t] if isinstance(tok_dtype, (list, tuple)) else tok_dtype
            out_shape.append(jax.ShapeDtypeStruct((bl, s, full(toks[t][0])), dt))
    for arr, _ in bats:
        out_specs.append(pl.BlockSpec((None, 1, blk(arr)), lambda j, b, i: (b, 0, j)))
        out_shape.append(jax.ShapeDtypeStruct((bl, 1, full(arr)), F32))
    for arr, off in pars:
        if off is None:
            out_specs.append(pl.BlockSpec(arr.shape, lambda j, b, i: (0, 0)))
            out_shape.append(jax.ShapeDtypeStruct(arr.shape, F32))
        else:
            out_specs.append(pl.BlockSpec((arr.shape[0], blk(arr)), lambda j, b, i: (0, j)))
            out_shape.append(jax.ShapeDtypeStruct((arr.shape[0], full(arr)), F32))
    res = list(pl.pallas_call(
        body, name=name, grid=(cols, bl, s // ts), in_specs=_in_specs(toks, bats, pars, cots, ts, wb),
        out_specs=out_specs, out_shape=out_shape, compiler_params=_cparams("arbitrary", "arbitrary", "arbitrary"),
    )(*[a for a, _ in toks + bats + pars + cots]))
    tot = res.pop(0) if loss else None
    dtoks = [res.pop(0) if need[t] else None for t in range(nt)]
    dbats = [res.pop(0) for _ in range(nb)]
    dpars = [res.pop(0) for _ in range(npar)]
    return (tot, dtoks, dbats, dpars) if loss else (dtoks, dbats, dpars)


def _silu(x):
    return x * jax.nn.sigmoid(x)


def _rms(x, w):
    return x * lax.rsqrt(jnp.mean(x * x, axis=-1, keepdims=True) + EPS) * w


def _f_norm_mod(x, shift, scale, w):
    return (_rms(x, w) * (1.0 + scale) + shift,)


def _f_norm_mod_skip(x, shift, scale, w):
    return _rms(x, w) * (1.0 + scale) + shift, x


def _f_res_norm_mod(x, mix, gate, shift, scale, w):
    x2 = x + gate * mix
    return x2, _rms(x2, w) * (1.0 + scale) + shift


def _f_gates(p, a_log, dt_bias, *, heads):
    z = p + dt_bias
    g = -jnp.exp(a_log) * (jnp.maximum(z, 0.0) + jnp.log1p(jnp.exp(jnp.minimum(z, -z))))
    lane = lax.broadcasted_iota(jnp.int32, p.shape, 1)
    return (jnp.where(lane < heads, g, jax.nn.sigmoid(p)),)


def _f_gdn_out(o, z, w):
    return (_rms(o, w) * _silu(z),)


def _f_merge(ga, gb, ya, yb):
    return (jax.nn.sigmoid(ga) * ya + jax.nn.sigmoid(gb) * yb,)


def _f_swiglu(a, b):
    return (_silu(a) * b,)


def _f_loss(x2, ff, tgt, gate, shift, scale, w):
    y = _rms(x2 + gate * ff, w) * (1.0 + scale) + shift
    return (0.5 * jnp.mean(jnp.square(y - tgt), axis=-1, keepdims=True),)


def _shift_down(x, s):
    if s == 0:
        return x
    row = lax.broadcasted_iota(jnp.int32, x.shape, 0)
    return jnp.where(row >= s, pltpu.roll(x, s, 0), 0.0)


def _shift_up(x, s):
    if s == 0:
        return x
    n = x.shape[0]
    row = lax.broadcasted_iota(jnp.int32, x.shape, 0)
    return jnp.where(row < n - s, pltpu.roll(x, n - s, 0), 0.0)


def _conv(x, w):
    width = w.shape[0]
    acc = w[width - 1:width, :] * x
    for j in range(width - 1):
        acc = acc + w[j:j + 1, :] * _shift_down(x, width - 1 - j)
    return acc


def _conv_bwd(dy, x, w, dw_ref, first):
    width = w.shape[0]
    dx = w[width - 1:width, :] * dy
    for j in range(width - 1):
        dx = dx + w[j:j + 1, :] * _shift_up(dy, width - 1 - j)
    for j in range(width):
        row = jnp.sum(dy * _shift_down(x, width - 1 - j), axis=0, keepdims=True)
        _accumulate(dw_ref.at[j:j + 1, :], row, first)
    return dx


def _qkv_act(xc, is_v, scale):
    a = _silu(xc)
    nrm = a * lax.rsqrt(jnp.sum(a * a, axis=-1, keepdims=True) + EPS) * scale
    return jnp.where(is_v, a, nrm)


def _qkv_consts(j, heads):
    is_v = j >= 2 * heads
    scale = jnp.where(j < heads, HEAD ** -0.5, 1.0).astype(F32)
    return is_v, scale


def _qkv_fwd(p, w, heads, name):
    bl, s, w3 = p.shape

    def body(p_ref, w_ref, o_ref):
        is_v, scale = _qkv_consts(pl.program_id(0), heads)
        o_ref[...] = _qkv_act(_conv(p_ref[...], w_ref[...]), is_v, scale)

    blk = pl.BlockSpec((None, s, HEAD), lambda j, b: (b, 0, j))
    return pl.pallas_call(
        body, name=name, grid=(w3 // HEAD, bl), in_specs=[blk, pl.BlockSpec((w.shape[0], HEAD), lambda j, b: (0, j))],
        out_specs=blk, out_shape=jax.ShapeDtypeStruct(p.shape, F32), compiler_params=_cparams("parallel", "parallel"),
    )(p, w)


def _qkv_bwd(p, w, dout, heads, name, rider=None):
    bl, s, w3 = p.shape
    r_args, r_in, r_out, r_shape, r_sems = _ride_specs(rider)

    def body(*refs):
        (p_ref, w_ref, d_ref, dp_ref, dw_ref), hooks = _split_refs(refs, 3, 2, 0, rider)
        first, mid, last = _grid_marks(w3 // HEAD, bl)
        _hooks_before(hooks, first, mid)
        is_v, scale = _qkv_consts(pl.program_id(0), heads)
        x, wv = p_ref[...], w_ref[...]
        _, vjp = jax.vjp(lambda xc: _qkv_act(xc, is_v, scale), _conv(x, wv))
        (dxc,) = vjp(d_ref[...])
        dp_ref[...] = _conv_bwd(dxc, x, wv, dw_ref, pl.program_id(1) == 0).astype(dp_ref.dtype)
        _hooks_after(hooks, last)

    blk = pl.BlockSpec((None, s, HEAD), lambda j, b: (b, 0, j))
    wblk = pl.BlockSpec((w.shape[0], HEAD), lambda j, b: (0, j))
    return pl.pallas_call(
        body, name=name, grid=(w3 // HEAD, bl), in_specs=[blk, wblk, blk] + r_in, out_specs=[blk, wblk] + r_out,
        out_shape=[jax.ShapeDtypeStruct(p.shape, MXU_DTYPE), jax.ShapeDtypeStruct(w.shape, F32)] + r_shape,
        scratch_shapes=r_sems, compiler_params=_cparams("arbitrary", "arbitrary"),
    )(p, w, dout, *r_args)


def _sc_specs(p, w):
    bl, s, w3 = p.shape
    nblk = w3 // 3 // LANE
    sec = lambda k: pl.BlockSpec((None, s, LANE), functools.partial(lambda j, b, k: (b, 0, k * nblk + j), k=k))
    return nblk, [sec(0), sec(1), sec(2)], pl.BlockSpec((w.shape[0], LANE), lambda j, b: (0, j)), \
        pl.BlockSpec((None, s, LANE), lambda j, b: (b, 0, j))


def _sc_fwd(p, w, name):
    bl, s, w3 = p.shape
    nblk, secs, wblk, oblk = _sc_specs(p, w)

    def body(b_ref, c_ref, x_ref, w_ref, o_ref):
        o_ref[...] = (b_ref[...] * _conv(c_ref[...] * x_ref[...], w_ref[...])).astype(o_ref.dtype)

    return pl.pallas_call(
        body, name=name, grid=(nblk, bl), in_specs=secs + [wblk], out_specs=oblk,
        out_shape=jax.ShapeDtypeStruct((bl, s, w3 // 3), MXU_DTYPE), compiler_params=_cparams("parallel", "parallel"),
    )(p, p, p, w)


def _sc_bwd(p, w, dout, name):
    bl, s, w3 = p.shape
    nblk, secs, wblk, oblk = _sc_specs(p, w)

    def body(b_ref, c_ref, x_ref, w_ref, d_ref, db_ref, dc_ref, dx_ref, dw_ref):
        gb, gc, xin, wv, d = b_ref[...], c_ref[...], x_ref[...], w_ref[...], d_ref[...]
        u = gc * xin
        db_ref[...] = (d * _conv(u, wv)).astype(db_ref.dtype)
        du = _conv_bwd(d * gb, u, wv, dw_ref, pl.program_id(1) == 0)
        dc_ref[...] = (du * xin).astype(dc_ref.dtype)
        dx_ref[...] = (du * gc).astype(dx_ref.dtype)

    act = jax.ShapeDtypeStruct((bl, s, w3 // 3), MXU_DTYPE)
    return pl.pallas_call(
        body, name=name, grid=(nblk, bl), in_specs=secs + [wblk, oblk], out_specs=[oblk, oblk, oblk, wblk],
        out_shape=[act, act, act, jax.ShapeDtypeStruct(w.shape, F32)], compiler_params=_cparams("arbitrary", "arbitrary"),
    )(p, p, p, w, dout)


def _bdot(a, b, ca, cb):
    return lax.dot_general(a.astype(MXU_DTYPE), b.astype(MXU_DTYPE), (((ca,), (cb,)), ((), ())),
                           preferred_element_type=F32)


def _hdot(a, b):
    return lax.dot_general(a, b, (((1,), (0,)), ((), ())), precision=HIGHEST, preferred_element_type=F32)


def _lane_col(x, idx):
    lane = lax.broadcasted_iota(jnp.int32, x.shape, 1)
    return jnp.sum(jnp.where(lane == idx, x, 0.0), axis=1, keepdims=True)


def _chunk_masks():
    r = lax.broadcasted_iota(jnp.int32, (CHUNK, CHUNK), 0)
    c = lax.broadcasted_iota(jnp.int32, (CHUNK, CHUNK), 1)
    return r == c, r >= c, r > c


def _dot3(a, b):
    ah, bh = a.astype(MXU_DTYPE), b.astype(MXU_DTYPE)
    al, bl = (a - ah.astype(F32)).astype(MXU_DTYPE), (b - bh.astype(F32)).astype(MXU_DTYPE)
    dot = lambda x, y: lax.dot_general(x, y, (((1,), (0,)), ((), ())), preferred_element_type=F32)
    return dot(ah, bh) + (dot(ah, bl) + dot(al, bh))


def _tri_inv_steps(low, eye):
    x = -low
    p = jnp.where(eye, 1.0, 0.0) + x
    span = 2
    while span < CHUNK:
        x = _dot3(x, x)
        yield
        p = p + _dot3(p, x)
        yield
        span *= 2
    return p


def _round_robin(gens):
    out, live = [None] * len(gens), list(range(len(gens)))
    while live:
        still = []
        for i in live:
            try:
                next(gens[i])
                still.append(i)
            except StopIteration as stop:
                out[i] = stop.value
        live = still
    return out


def _gdn_pre(q, k, v, gc, beta, masks):
    eye, causal, strict = masks
    gc_row = jnp.sum(jnp.where(eye, gc, 0.0), axis=0, keepdims=True)
    decay = jnp.where(causal, jnp.exp(jnp.where(causal, gc - gc_row, 0.0)), 0.0)
    eg = jnp.exp(gc)
    gl = gc[CHUNK - 1:CHUNK, :]
    kb, vb = k * beta, v * beta
    low = jnp.where(strict, _bdot(kb, k, 1, 1) * decay, 0.0)
    qk = jnp.where(causal, _bdot(q, k, 1, 1) * decay, 0.0)
    rest = jnp.exp(gl - gc)
    return dict(decay=decay, eg=eg, gl=gl, kb=kb, vb=vb, kbe=kb * eg, low=low, qk=qk, qg=q * eg, rest=rest, kdec=k * rest)


def _gdn_specs(qkv, gbeta, heads, rev):
    bl, s, w3 = qkv.shape
    d, n = w3 // 3, s // CHUNK
    at = (lambda c: n - 1 - c) if rev else (lambda c: c)
    assert d == heads * HEAD
    sec = pl.BlockSpec((None, CHUNK, w3), lambda b, c: (b, at(c), 0))
    gspec = pl.BlockSpec((None, CHUNK, LANE), lambda b, c: (b, at(c), 0))
    sspec = pl.BlockSpec((None, None, heads, HEAD, HEAD), lambda b, c: (b, at(c), 0, 0, 0))
    tspec = pl.BlockSpec((None, None, heads, CHUNK, CHUNK), lambda b, c: (b, at(c), 0, 0, 0))
    return bl, s, d, n, sec, gspec, sspec, tspec


def _grid_marks(bl, n):
    b, c = pl.program_id(0), pl.program_id(1)
    first = jnp.logical_and(b == 0, c == 0)
    mid = jnp.logical_and(b == bl // 2, c == (0 if bl > 1 else n // 2))
    return first, mid, jnp.logical_and(b == bl - 1, c == n - 1)


def _gdn_fwd(qkv, gbeta, heads, name, rider=None):
    bl, s, d, n, sec, gspec, sspec, tspec = _gdn_specs(qkv, gbeta, heads, False)
    r_args, r_in, r_out, r_shape, r_sems = _ride_specs(rider)

    def body(*refs):
        (x_ref, g_ref, o_ref, s_ref, t_ref, st_ref), hooks = _split_refs(refs, 2, 3, 1, rider)
        first, mid, last = _grid_marks(bl, n)
        _hooks_before(hooks, first, mid)

        @pl.when(pl.program_id(1) == 0)
        def _():
            st_ref[...] = jnp.zeros_like(st_ref)

        masks = _chunk_masks()
        eye, causal, _ = masks
        gblk = g_ref[...]
        gc_all = _hdot(jnp.where(causal, 1.0, 0.0), gblk)
        st_all = st_ref[...]

        def head(h):
            st = st_all[h]
            q, k, v = (x_ref[:, sec * d + h * HEAD:sec * d + (h + 1) * HEAD] for sec in range(3))
            pre = _gdn_pre(q, k, v, _lane_col(gc_all, h), _lane_col(gblk, heads + h), masks)
            yield
            t = yield from _tri_inv_steps(pre["low"], eye)
            u, w = _bdot(t, pre["vb"], 1, 0), _bdot(t, pre["kbe"], 1, 0)
            yield
            vnew = u - _bdot(w, st, 1, 0)
            yield
            out = _bdot(pre["qg"], st, 1, 0) + _bdot(pre["qk"], vnew, 1, 0)
            return out, t, st * jnp.exp(pre["gl"]) + _bdot(pre["kdec"], vnew, 0, 0)

        outs, ts, states = zip(*_round_robin([head(h) for h in range(heads)]))
        o_ref[...] = jnp.concatenate(outs, axis=1)
        s_ref[...] = st_all
        t_ref[...] = jnp.stack(ts)
        st_ref[...] = jnp.stack(states)
        _hooks_after(hooks, last)

    return pl.pallas_call(
        body, name=name, grid=(bl, n), in_specs=[sec, gspec] + r_in,
        out_specs=[pl.BlockSpec((None, CHUNK, d), lambda b, c: (b, c, 0)), sspec, tspec] + r_out,
        out_shape=[jax.ShapeDtypeStruct((bl, s, d), F32), jax.ShapeDtypeStruct((bl, n, heads, HEAD, HEAD), F32),
                   jax.ShapeDtypeStruct((bl, n, heads, CHUNK, CHUNK), F32)] + r_shape,
        scratch_shapes=[pltpu.VMEM((heads, HEAD, HEAD), F32)] + r_sems, compiler_params=_cparams("arbitrary", "arbitrary"),
    )(qkv, gbeta, *r_args)


def _gdn_bwd(qkv, gbeta, dout, s_all, t_all, heads, name, rider=None):
    bl, s, d, n, sec, gspec, sspec, tspec = _gdn_specs(qkv, gbeta, heads, True)
    ospec = pl.BlockSpec((None, CHUNK, d), lambda b, c: (b, n - 1 - c, 0))
    r_args, r_in, r_out, r_shape, r_sems = _ride_specs(rider)

    def body(*refs):
        (x_ref, g_ref, do_ref, s_ref, t_ref, dx_ref, dg_ref, ds_ref), hooks = _split_refs(refs, 5, 2, 1, rider)
        first, mid, last = _grid_marks(bl, n)
        _hooks_before(hooks, first, mid)

        @pl.when(pl.program_id(1) == 0)
        def _():
            ds_ref[...] = jnp.zeros_like(ds_ref)

        masks = _chunk_masks()
        eye, causal, strict = masks
        gblk = g_ref[...]
        gc_all = _hdot(jnp.where(causal, 1.0, 0.0), gblk)
        lane = lax.broadcasted_iota(jnp.int32, gblk.shape, 1)
        last_row = lax.broadcasted_iota(jnp.int32, (CHUNK, 1), 0) == CHUNK - 1
        rowsum = lambda a: jnp.sum(a, axis=1, keepdims=True)
        st_all, t_all_, ds_all = s_ref[...], t_ref[...], ds_ref[...]

        def head(h):
            sl = slice(h * HEAD, (h + 1) * HEAD)
            q, k, v = (x_ref[:, sec * d + h * HEAD:sec * d + (h + 1) * HEAD] for sec in range(3))
            do = do_ref[:, sl]
            beta = _lane_col(gblk, heads + h)
            st, t, dsn = st_all[h], t_all_[h], ds_all[h]
            pre = _gdn_pre(q, k, v, _lane_col(gc_all, h), beta, masks)
            decay, eg, kb, vb, kbe, low, qk, qg, kdec = (pre[x] for x in ("decay", "eg", "kb", "vb", "kbe", "low", "qk", "qg", "kdec"))
            egl = jnp.exp(pre["gl"])
            yield
            u, w = _bdot(t, vb, 1, 0), _bdot(t, kbe, 1, 0)
            yield
            vnew = u - _bdot(w, st, 1, 0)
            yield
            dkdec = _bdot(vnew, dsn, 1, 1)
            dvnew = _bdot(kdec, dsn, 1, 0) + _bdot(qk, do, 0, 0)
            dgl = jnp.sum(dsn * st, keepdims=True) * egl
            dqg = _bdot(do, st, 1, 1)
            dqk = jnp.where(causal, _bdot(do, vnew, 1, 1), 0.0)
            yield
            dw = -_bdot(dvnew, st, 1, 1)
            ds_new = dsn * egl + _bdot(qg, do, 0, 0) - _bdot(w, dvnew, 0, 0)
            yield
            dt = _bdot(dvnew, vb, 1, 1) + _bdot(dw, kbe, 1, 1)
            dvb, dkbe = _bdot(t, dvnew, 0, 0), _bdot(t, dw, 0, 0)
            yield
            inner = _bdot(dt, t, 1, 1)
            yield
            dlow = -jnp.where(strict, _bdot(t, inner, 0, 0), 0.0)
            da, db = dlow * decay, dqk * decay
            yield
            m = dlow * low + dqk * qk
            kdk = dkdec * kdec
            col_of_m = jnp.sum(jnp.where(eye, jnp.sum(m, axis=0, keepdims=True), 0.0), axis=1, keepdims=True)
            dgc = rowsum(m) - col_of_m + rowsum(dqg * qg) + rowsum(dkbe * kbe) - rowsum(kdk)
            dgc = dgc + jnp.where(last_row, dgl + jnp.sum(kdk, keepdims=True), 0.0)
            dkb = _bdot(da, k, 1, 0) + dkbe * eg
            yield
            dk = _bdot(da, kb, 0, 0) + _bdot(db, q, 0, 0) + dkdec * pre["rest"] + dkb * beta
            dq = _bdot(db, k, 1, 0) + dqg * eg
            dbeta = rowsum(dkb * k) + rowsum(dvb * v)
            return dq, dk, dvb * beta, jnp.where(lane == h, dgc, 0.0) + jnp.where(lane == heads + h, dbeta, 0.0), ds_new

        dqs, dks, dvs, dgs, dss = zip(*_round_robin([head(h) for h in range(heads)]))
        dx_ref[...] = jnp.concatenate(dqs + dks + dvs, axis=1)
        ds_ref[...] = jnp.stack(dss)
        dgb = dgs[0]
        for extra in dgs[1:]:
            dgb = dgb + extra
        upper = jnp.where(jnp.logical_or(eye, jnp.logical_not(causal)), 1.0, 0.0)
        dg_ref[...] = jnp.where(lane < heads, _hdot(upper, dgb), dgb)
        _hooks_after(hooks, last)

    return pl.pallas_call(
        body, name=name, grid=(bl, n), in_specs=[sec, gspec, ospec, sspec, tspec] + r_in, out_specs=[sec, gspec] + r_out,
        out_shape=[jax.ShapeDtypeStruct(qkv.shape, F32), jax.ShapeDtypeStruct((bl, s, LANE), F32)] + r_shape,
        scratch_shapes=[pltpu.VMEM((heads, HEAD, HEAD), F32)] + r_sems, compiler_params=_cparams("arbitrary", "arbitrary"),
    )(qkv, gbeta, dout, s_all, t_all, *r_args)


def _position():
    return lax.axis_index("x"), lax.axis_index("y"), lax.axis_index("c")


def _all_gather(x, *, name, hbm):
    space = pltpu.HBM if hbm else pltpu.VMEM

    def body(x_ref, out_ref, send_sems, recv_sems, local_sem):
        ax, ay, ac = _position()
        me, sibling = (ax, ay, ac), (ax, ay, 1 - ac)
        chips = [(1 - ax, ay), (ax, 1 - ay), (1 - ax, 1 - ay)]

        def slot(px, py, pc):
            return out_ref.at[4 * px + 2 * py + pc]

        def copy(k, block, to, src=None):
            return pltpu.make_async_remote_copy(
                src_ref=slot(*block) if src is None else src, dst_ref=slot(*block), send_sem=send_sems.at[k],
                recv_sem=recv_sems.at[k], device_id=to, device_id_type=MESH_IDS)

        mine = pltpu.make_async_copy(x_ref, slot(*me), local_sem)
        mine.start()
        first = [copy(0, me, sibling, src=x_ref)] + [copy(1 + j, me, (*chip, ac), src=x_ref) for j, chip in enumerate(chips)]
        for cp in first:
            cp.start()
        passed = [copy(4 + j, (*chip, ac), sibling) for j, chip in enumerate(chips)]
        for j, chip in enumerate(chips):
            copy(1 + j, (*chip, ac), me).wait_recv()
            passed[j].start()
        copy(0, sibling, me).wait_recv()
        for j, chip in enumerate(chips):
            copy(4 + j, (*chip, 1 - ac), me).wait_recv()
        for cp in first + passed:
            cp.wait_send()
        mine.wait()

    return pl.pallas_call(
        body, name=name, out_shape=jax.ShapeDtypeStruct((NDEV,) + x.shape, x.dtype),
        in_specs=[pl.BlockSpec(memory_space=space)], out_specs=pl.BlockSpec(memory_space=space),
        scratch_shapes=[pltpu.SemaphoreType.DMA((7,)), pltpu.SemaphoreType.DMA((7,)), pltpu.SemaphoreType.DMA],
    )(x)


class _Rider:
    def __init__(self, arrays, out_shape, sems, hooks):
        self.arrays, self.out_shape, self.sems, self.hooks = arrays, out_shape, sems, hooks


def _split_refs(refs, n_in, n_out, n_scratch, rider):
    r_in = len(rider.arrays) if rider else 0
    o = n_in + r_in
    o2 = o + n_out + (1 if rider else 0)
    host = refs[:n_in] + refs[o:o + n_out] + refs[o2:o2 + n_scratch]
    if rider is None:
        return host, None
    return host, rider.hooks(refs[n_in:o], refs[o + n_out], *refs[o2 + n_scratch:])


def _hooks_before(hooks, first, mid):
    if hooks is not None:
        pl.when(first)(hooks[0])
        pl.when(mid)(hooks[1])


def _hooks_after(hooks, last):
    if hooks is not None:
        pl.when(last)(hooks[2])


def _ride_specs(rider):
    hbm = pl.BlockSpec(memory_space=pltpu.HBM)
    if rider is None:
        return [], [], [], [], []
    return list(rider.arrays), [hbm] * len(rider.arrays), [hbm], [rider.out_shape], list(rider.sems)


def _gather_rider(x):
    def hooks(in_refs, out_ref, send_sems, recv_sems):
        (x_ref,) = in_refs
        ax, ay, ac = _position()
        me, sibling = (ax, ay, ac), (ax, ay, 1 - ac)
        chips = [(1 - ax, ay), (ax, 1 - ay), (1 - ax, 1 - ay)]

        def copy(k, block, to, src=None):
            slot = out_ref.at[4 * block[0] + 2 * block[1] + block[2]]
            return pltpu.make_async_remote_copy(src_ref=slot if src is None else src, dst_ref=slot, send_sem=send_sems.at[k],
                                                recv_sem=recv_sems.at[k], device_id=to, device_id_type=MESH_IDS)

        def first():
            copy(0, me, sibling, src=x_ref).start()
            for j, chip in enumerate(chips):
                copy(1 + j, me, (*chip, ac), src=x_ref).start()

        def mid():
            for j, chip in enumerate(chips):
                copy(1 + j, (*chip, ac), me).wait_recv()
                copy(4 + j, (*chip, ac), sibling).start()

        def last():
            copy(0, sibling, me).wait_recv()
            for j, chip in enumerate(chips):
                copy(4 + j, (*chip, 1 - ac), me).wait_recv()
            copy(0, me, sibling, src=x_ref).wait_send()
            for j, chip in enumerate(chips):
                copy(1 + j, me, (*chip, ac), src=x_ref).wait_send()
                copy(4 + j, (*chip, ac), sibling).wait_send()

        return first, mid, last

    return _Rider([x], jax.ShapeDtypeStruct((NDEV,) + x.shape, x.dtype),
                  [pltpu.SemaphoreType.DMA((7,)), pltpu.SemaphoreType.DMA((7,))], hooks)


def _scatter_rider(parts):
    packed = sum(r for _, r in parts)
    width, dtype = parts[0][0].shape[1], parts[0][0].dtype

    def hooks(g_refs, recv_ref, send_sems, recv_sems):
        ax, ay, ac = _position()

        def peer(rel):
            flip = lambda a, bit: 1 - a if rel & bit else a
            return flip(ax, 4), flip(ay, 2), flip(ac, 1)

        def first():
            for rel in range(1, NDEV):
                px, py, pc = peer(rel)
                off = 0
                for g_ref, (_, r) in zip(g_refs, parts):
                    rows = g_ref.at[pl.ds(pl.multiple_of((4 * px + 2 * py + pc) * r, ROW_ALIGN), r)]
                    pltpu.make_async_remote_copy(
                        src_ref=rows, dst_ref=recv_ref.at[rel - 1, pl.ds(off, r)], send_sem=send_sems.at[rel - 1],
                        recv_sem=recv_sems.at[rel - 1], device_id=(px, py, pc), device_id_type=MESH_IDS).start()
                    off += r

        def last():
            for rel in range(1, NDEV):
                slot = recv_ref.at[rel - 1]
                pltpu.make_async_remote_copy(src_ref=slot, dst_ref=slot, send_sem=send_sems.at[rel - 1],
                                             recv_sem=recv_sems.at[rel - 1], device_id=peer(rel), device_id_type=MESH_IDS).wait()

        return first, lambda: None, last

    return _Rider([g for g, _ in parts], jax.ShapeDtypeStruct((NDEV - 1, packed, width), dtype),
                  [pltpu.SemaphoreType.DMA((NDEV - 1,)), pltpu.SemaphoreType.DMA((NDEV - 1,))], hooks)


def _sum_direct(own, recv, name):
    r, w = own.shape
    tr = max(t for t in range(ROW_ALIGN, 257, ROW_ALIGN) if r % t == 0)

    def body(own_ref, *refs):
        acc = own_ref[...].astype(F32)
        for ref in refs[:-1]:
            acc = acc + ref[...].astype(F32)
        refs[-1][...] = acc

    rblk = lambda k: pl.BlockSpec((None, tr, w), functools.partial(lambda i, k: (k, i, 0), k=k))
    blk = pl.BlockSpec((tr, w), lambda i: (i, 0))
    return pl.pallas_call(body, name=name, grid=(r // tr,), in_specs=[blk] + [rblk(k) for k in range(NDEV - 1)],
                          out_specs=blk, out_shape=jax.ShapeDtypeStruct((r, w), F32),
                          compiler_params=_cparams("parallel"))(own, *([recv] * (NDEV - 1)))


ROW_ALIGN = 16


def _window_start(rows_per_dev, k):
    return rows_per_dev * k // ROW_ALIGN * ROW_ALIGN


def _exchange_in_chip(parts, name):
    n = len(parts)
    packed = sum(win for _, _, win, _ in parts)
    width, dtype = parts[0][0].shape[1], parts[0][0].dtype

    def body(*refs):
        g_refs, recv_ref, send_sems, recv_sems = refs[:n], *refs[n:]
        ax, ay, ac = _position()
        sibling = (ax, ay, 1 - ac)
        for q in range(4):
            for g_ref, (_, r, win, off) in zip(g_refs, parts):
                there = g_ref.at[pl.ds(pl.multiple_of(_window_start(r, 2 * q + 1 - ac), ROW_ALIGN), win)]
                pltpu.make_async_remote_copy(src_ref=there, dst_ref=recv_ref.at[q, pl.ds(off, win)], send_sem=send_sems.at[q],
                                             recv_sem=recv_sems.at[q], device_id=sibling, device_id_type=MESH_IDS).start()
        for q in range(4):
            pltpu.make_async_remote_copy(src_ref=recv_ref.at[q], dst_ref=recv_ref.at[q], send_sem=send_sems.at[q],
                                         recv_sem=recv_sems.at[q], device_id=sibling, device_id_type=MESH_IDS).wait()

    hbm = pl.BlockSpec(memory_space=pltpu.HBM)
    return pl.pallas_call(
        body, name=name, out_shape=jax.ShapeDtypeStruct((4, packed, width), dtype), in_specs=[hbm] * n, out_specs=hbm,
        scratch_shapes=[pltpu.SemaphoreType.DMA((4,)), pltpu.SemaphoreType.DMA((4,))],
    )(*[g for g, _, _, _ in parts])


def _exchange_chips(s1, name):
    def body(s_ref, recv_ref, send_sems, recv_sems):
        ax, ay, ac = _position()
        chips = [(1 - ax, ay), (ax, 1 - ay), (1 - ax, 1 - ay)]
        copies = [pltpu.make_async_remote_copy(
            src_ref=s_ref.at[2 * cx + cy], dst_ref=recv_ref.at[r], send_sem=send_sems.at[r], recv_sem=recv_sems.at[r],
            device_id=(cx, cy, ac), device_id_type=MESH_IDS) for r, (cx, cy) in enumerate(chips)]
        for cp in copies:
            cp.start()
        for cp in copies:
            cp.wait_recv()
        for cp in copies:
            cp.wait_send()

    hbm = pl.BlockSpec(memory_space=pltpu.HBM)
    return pl.pallas_call(
        body, name=name, out_shape=jax.ShapeDtypeStruct((3,) + s1.shape[1:], s1.dtype), in_specs=[hbm], out_specs=hbm,
        scratch_shapes=[pltpu.SemaphoreType.DMA((3,)), pltpu.SemaphoreType.DMA((3,))],
    )(s1)


def _sum_in_chip(own, recv, name):
    _, r, w = own.shape
    tr = _tile(r, (256, 128))

    def body(a_ref, b_ref, o_ref):
        o_ref[...] = (a_ref[...].astype(F32) + b_ref[...].astype(F32)).astype(o_ref.dtype)

    blk = pl.BlockSpec((None, tr, w), lambda q, i: (q, i, 0))
    return pl.pallas_call(body, name=name, grid=(4, r // tr), in_specs=[blk, blk], out_specs=blk,
                          out_shape=jax.ShapeDtypeStruct(own.shape, own.dtype),
                          compiler_params=_cparams("parallel", "parallel"))(own, recv)


def _sum_chips(s1, recv, chip, name):
    _, r, w = s1.shape
    tr = _tile(r, (256, 128))

    def body(c_ref, s_ref, r0_ref, r1_ref, r2_ref, o_ref):
        f = lambda ref: ref[...].astype(F32)
        o_ref[...] = ((f(s_ref) + f(r0_ref)) + f(r1_ref)) + f(r2_ref)

    rblk = lambda k: pl.BlockSpec((None, tr, w), functools.partial(lambda i, c, k: (k, i, 0), k=k))
    grid_spec = pltpu.PrefetchScalarGridSpec(
        num_scalar_prefetch=1, grid=(r // tr,),
        in_specs=[pl.BlockSpec((None, tr, w), lambda i, c: (c[0], i, 0)), rblk(0), rblk(1), rblk(2)],
        out_specs=pl.BlockSpec((tr, w), lambda i, c: (i, 0)))
    return pl.pallas_call(body, name=name, grid_spec=grid_spec, out_shape=jax.ShapeDtypeStruct((r, w), F32),
                          compiler_params=_cparams("parallel"))(chip, s1, recv, recv, recv)


def _silu_rows(x, name):
    def body(x_ref, o_ref):
        o_ref[...] = _silu(x_ref[...])

    return pl.pallas_call(body, name=name, out_shape=jax.ShapeDtypeStruct(x.shape, F32))(x)


def _row_sum(x, name):
    def body(x_ref, o_ref):
        acc = x_ref[0:1, :]
        for i in range(1, x.shape[0]):
            acc = acc + x_ref[i:i + 1, :]
        o_ref[...] = acc

    return pl.pallas_call(body, name=name, out_shape=jax.ShapeDtypeStruct((1, x.shape[1]), F32))(x)


def _adamw(w, g, m, v, name):
    cols = w.shape[-1]
    rows = w.size // cols
    tr = _tile(rows, (128,))

    def body(w_ref, g_ref, m_ref, v_ref, d_ref, mo_ref, vo_ref):
        grad = g_ref[...]
        m_new = ADAM_B1 * m_ref[...] + (1.0 - ADAM_B1) * grad
        v_new = ADAM_B2 * v_ref[...] + (1.0 - ADAM_B2) * jnp.square(grad)
        m_hat = m_new / (1.0 - ADAM_B1 ** ADAM_STEP)
        v_hat = v_new / (1.0 - ADAM_B2 ** ADAM_STEP)
        d_ref[...] = -ADAM_LR * (m_hat / (jnp.sqrt(v_hat) + ADAM_EPS) + ADAM_WD * w_ref[...])
        mo_ref[...] = m_new
        vo_ref[...] = v_new

    blk = pl.BlockSpec((tr, cols), lambda i: (i, 0))
    out = pl.pallas_call(
        body, name=name, grid=(rows // tr,), in_specs=[blk] * 4, out_specs=[blk] * 3,
        out_shape=[jax.ShapeDtypeStruct((rows, cols), F32)] * 3, compiler_params=_cparams("parallel"),
    )(*[t.reshape(rows, cols) for t in (w, g, m, v)])
    return [t.reshape(w.shape) for t in out]


def _pack(parts, width, row_mult, dtype):
    flat = jnp.concatenate([p.reshape(-1).astype(dtype) for p in parts])
    rows = -(-flat.shape[0] // (width * row_mult)) * row_mult
    return jnp.pad(flat, (0, rows * width - flat.shape[0])).reshape(rows, width)


def _unpack(flat, shapes):
    out, off = [], 0
    for shp in shapes:
        size = 1
        for dim in shp:
            size *= dim
        out.append(flat[:, off:off + size].reshape((flat.shape[0],) + tuple(shp)))
        off += size
    return out


def _devices_to_cols(a):
    _, r, c = a.shape
    return a.transpose(1, 0, 2).reshape(r, NDEV * c)


def kernel(x, c, w_ada, b_ada, norm1_w, w_in, gdn_conv_w, gdn_a_log, gdn_dt_bias, gdn_norm_w, w_gdn_proj, sc_conv_w, w_sc_out, w_o, norm2_w, w_ffn_in, w_ffn_out, w_ada_f, b_ada_f, normf_w, loss_target, m_w_ada, m_b_ada, m_norm1_w, m_w_in, m_gdn_conv_w, m_gdn_a_log, m_gdn_dt_bias, m_gdn_norm_w, m_w_gdn_proj, m_sc_conv_w, m_w_sc_out, m_w_o, m_norm2_w, m_w_ffn_in, m_w_ffn_out, m_w_ada_f, m_b_ada_f, m_normf_w, v_w_ada, v_b_ada, v_norm1_w, v_w_in, v_gdn_conv_w, v_gdn_a_log, v_gdn_dt_bias, v_gdn_norm_w, v_w_gdn_proj, v_sc_conv_w, v_w_sc_out, v_w_o, v_norm2_w, v_w_ffn_in, v_w_ffn_out, v_w_ada_f, v_b_ada_f, v_normf_w):
    bl, s, d = x.shape
    heads = gdn_a_log.shape[-1]
    dff = w_ffn_out.shape[1] * NDEV
    tok = bl * s
    ax, ay, ac = _position()
    dev = 4 * ax + 2 * ay + ac
    as_tok = lambda a: a.reshape(bl, s, a.shape[-1])
    as_mat = lambda a: a.reshape(tok, a.shape[-1])

    small = _all_gather(_pack([c, gdn_conv_w, sc_conv_w], LANE, 8, F32), name="gather_cond", hbm=False)
    c_all, conv_w, sc_w = _unpack(small.reshape(NDEV, -1), [(bl, d), gdn_conv_w.shape[1:], sc_conv_w.shape[1:]])
    c_act = _silu_rows(c_all.reshape(NDEV * bl, d), "cond_silu")
    conv_w, sc_w = _devices_to_cols(conv_w), _devices_to_cols(sc_w)
    n_ada, n_adaf = w_ada.shape[-1], w_ada_f.shape[-1]
    bias = jnp.broadcast_to(lax.dynamic_slice_in_dim(b_ada, dev * n_ada, n_ada, axis=1), (NDEV * bl, n_ada))
    biasf = jnp.broadcast_to(lax.dynamic_slice_in_dim(b_ada_f.reshape(1, -1), dev * n_adaf, n_adaf, axis=1), (NDEV * bl, n_adaf))
    mod_cols = _mm(c_act, w_ada[0], add=bias, name="ada_cols")
    modf_cols = _mm(c_act, w_ada_f, add=biasf, name="adaf_cols")
    mods = _all_gather(jnp.concatenate([mod_cols, modf_cols], axis=1), name="gather_mod", hbm=False)
    mod_all = mods[:, :, :n_ada].transpose(1, 0, 2).reshape(NDEV * bl, NDEV * n_ada)
    modf_all = mods[:, :, n_ada:].transpose(1, 0, 2).reshape(NDEV * bl, NDEV * n_adaf)
    my_rows = lambda a: lax.dynamic_slice_in_dim(a, dev * bl, bl, axis=0)
    sh1, sc1, g1, sh2, sc2, g2 = [t.reshape(bl, 1, d) for t in jnp.split(my_rows(mod_all), 6, axis=1)]
    shf, scf = [t.reshape(bl, 1, d) for t in jnp.split(my_rows(modf_all), 2, axis=1)]

    late = [w_gdn_proj[0], w_sc_out[0], w_o[0], w_ffn_in[0].T, w_ffn_out[0]]
    rows = [t.shape[0] for t in late] + [w_in.shape[-1]]
    offs = [sum(rows[:i]) for i in range(5)]
    late_send = jnp.concatenate([t.astype(MXU_DTYPE) for t in late], axis=0)
    in_rows = -(-rows[5] // ROW_ALIGN) * ROW_ALIGN
    in_send = jnp.pad(w_in[0].T.astype(MXU_DTYPE), ((0, in_rows - rows[5]), (0, 0)))
    wt_in = _all_gather(in_send, name="gather_w_in", hbm=True)[:, :rows[5], :].reshape(NDEV * rows[5], d)
    o_z, o_ab, o_sc, o_ga, o_gb = 3 * d, 4 * d, 4 * d + 2 * heads, 7 * d + 2 * heads, 8 * d + 2 * heads
    wt_qkv, wt_z = wt_in[:o_z], wt_in[o_z:o_ab]
    wt_ab = jnp.pad(wt_in[o_ab:o_sc], ((0, LANE - 2 * heads), (0, 0)))
    wt_scs = [wt_in[o_sc + k * d:o_sc + (k + 1) * d] for k in range(3)]
    wt_ga, wt_gb = wt_in[o_ga:o_gb], wt_in[o_gb:]

    n1w, n2w, nfw = norm1_w.reshape(1, d), norm2_w.reshape(1, d), normf_w.reshape(1, d)
    lanes = lambda a: jnp.pad(a.reshape(1, -1), ((0, 0), (0, LANE - a.size)))
    a_log, dt_bias, gnw = lanes(gdn_a_log), lanes(gdn_dt_bias), gdn_norm_w.reshape(1, HEAD)
    f_gates = functools.partial(_f_gates, heads=heads)
    (h1,) = _tok_fwd(_f_norm_mod, [x], [sh1, sc1], [n1w], [(d, MXU_DTYPE)], name="norm1", ts=256)
    h1m = as_mat(h1)
    p_qkv = as_tok(_mm(h1m, wt_qkv, tb=True, name="in_qkv"))
    p_z = as_tok(_mm(h1m, wt_z, tb=True, name="in_z"))
    p_ab = as_tok(_mm(h1m, wt_ab, tb=True, name="in_ab"))
    p_sc = as_tok(_mm(h1m, wt_in[o_sc:o_ga], tb=True, name="in_sc"))
    p_g = as_tok(_mm(h1m, wt_in[o_ga:], tb=True, name="in_gate"))
    qkv = _qkv_fwd(p_qkv, conv_w, heads, "qkv_conv")
    (gbeta,) = _tok_fwd(f_gates, [p_ab], [], [a_log, dt_bias], [(LANE, F32)], name="gates", ts=512)
    o, s_all, t_all, gathered = _gdn_fwd(qkv, gbeta, heads, "gdn", rider=_gather_rider(late_send))
    gathered = lax.dynamic_update_slice_in_dim(gathered, late_send[None], dev, axis=0)
    wgp, wso, wo, wt_fi, wfo = [gathered[:, offs[i]:offs[i] + rows[i], :].reshape(NDEV * rows[i], d) for i in range(5)]
    wt_fa, wt_fb = wt_fi[:dff], wt_fi[dff:]
    (og,) = _tok_fwd(_f_gdn_out, [o, p_z], [], [(gnw, None)], [(d, MXU_DTYPE)], name="gdn_out", ts=512, wb=HEAD, cols=heads)
    y_a = as_tok(_mm(as_mat(og), wgp, name="gdn_proj"))
    scp = _sc_fwd(p_sc, sc_w, "sc_conv")
    y_b = as_tok(_mm(as_mat(scp), wso, name="sc_out"))
    mcols = d // 512 if d % 512 == 0 else 1
    mwb = d // mcols
    merge_toks = [(p_g, 0), (p_g, mcols), y_a, y_b]
    (mrg,) = _tok_fwd(_f_merge, merge_toks, [], [], [(d, MXU_DTYPE)], name="merge", ts=256, wb=mwb, cols=mcols)
    mix = as_tok(_mm(as_mat(mrg), wo, name="mix_out"))
    x2, h2 = _tok_fwd(_f_res_norm_mod, [x, mix], [g1, sh2, sc2], [n2w], [(d, F32), (d, MXU_DTYPE)], name="norm2", ts=256)
    gu = as_tok(_mm(as_mat(h2), wt_fi, tb=True, name="ffn_in"))
    fwb = _tile(dff, (256, 128))
    fcols = dff // fwb
    (act,) = _tok_fwd(_f_swiglu, [(gu, 0), (gu, fcols)], [], [], [(dff, MXU_DTYPE)], name="swiglu", ts=512, wb=fwb, cols=fcols)
    ff = as_tok(_mm(as_mat(act), wfo, name="ffn_out"))

    loss_l, (dx2, dff_out, _), (dg2, dshf, dscf), (dnfw,) = _tok_bwd(
        _f_loss, [x2, ff, loss_target], [g2, shf, scf], [nfw], [], [True, True, False], name="loss", ts=256, loss=True,
        tok_dtype=[F32, MXU_DTYPE, None])
    dffm = as_mat(dff_out)
    dact = as_tok(_mm(dffm, wfo, tb=True, name="d_ffn_out"))
    gmm = functools.partial(_mm, ta=True, out_dtype=MXU_DTYPE)
    gw_ffn_out = gmm(as_mat(act), dffm, name="g_ffn_out")
    (dgu_a, dgu_b), _, _ = _tok_bwd(_f_swiglu, [(gu, 0), (gu, fcols)], [], [], [dact], [True, True], name="d_swiglu",
                                    ts=512, wb=fwb, cols=fcols, tok_dtype=MXU_DTYPE)
    dh2 = _mm(as_mat(dgu_a), wt_fa, name="d_ffn_in_a")
    dh2 = as_tok(_mm(as_mat(dgu_b), wt_fb, add=dh2, name="d_ffn_in_b"))
    h2m = as_mat(h2)
    gwt_ffn_in = jnp.concatenate([gmm(as_mat(dgu_a), h2m, name="g_ffn_in_a"),
                                  gmm(as_mat(dgu_b), h2m, name="g_ffn_in_b")], axis=0)
    (dx_skip, dmix), (dg1, dsh2, dsc2), (dn2w,) = _tok_bwd(
        _f_res_norm_mod, [x, mix], [g1, sh2, sc2], [n2w], [dx2, dh2], [True, True], name="d_norm2", ts=256,
        tok_dtype=[F32, MXU_DTYPE])
    dmixm = as_mat(dmix)
    dmrg = as_tok(_mm(dmixm, wo, tb=True, name="d_mix_out"))
    gw_o = gmm(as_mat(mrg), dmixm, name="g_mix_out")
    (dga, dgb, dya, dyb), _, _ = _tok_bwd(_f_merge, merge_toks, [], [], [dmrg], [True] * 4, name="d_merge", ts=256,
                                          wb=mwb, cols=mcols, tok_dtype=MXU_DTYPE)
    dyam, dybm = as_mat(dya), as_mat(dyb)
    dog = as_tok(_mm(dyam, wgp, tb=True, name="d_gdn_proj"))
    gw_gdn_proj = gmm(as_mat(og), dyam, name="g_gdn_proj")
    dscp = as_tok(_mm(dybm, wso, tb=True, name="d_sc_out"))
    gw_sc_out = gmm(as_mat(scp), dybm, name="g_sc_out")
    dscb, dscc, dscx, g_sc_w = _sc_bwd(p_sc, sc_w, dscp, "d_sc_conv")
    (do, dz), _, (g_gnw,) = _tok_bwd(_f_gdn_out, [o, p_z], [], [(gnw, None)], [dog], [True, True], name="d_gdn_out",
                                     ts=512, wb=HEAD, cols=heads, tok_dtype=[F32, MXU_DTYPE])
    ffn_parts, mix_parts = [(gwt_ffn_in, rows[3]), (gw_ffn_out, rows[4])], [(gw_gdn_proj, rows[0]), (gw_sc_out, rows[1]), (gw_o, rows[2])]
    own_rows = lambda parts: jnp.concatenate([lax.dynamic_slice_in_dim(g, dev * r, r, axis=0) for g, r in parts], axis=0)
    dqkv, dgbeta, ffn_recv = _gdn_bwd(qkv, gbeta, do, s_all, t_all, heads, "d_gdn", rider=_scatter_rider(ffn_parts))
    dp_qkv, g_conv_w, mix_recv = _qkv_bwd(p_qkv, conv_w, dqkv, heads, "d_qkv_conv", rider=_scatter_rider(mix_parts))
    ffn_red = _sum_direct(own_rows(ffn_parts), ffn_recv, "sum_ffn")
    mix_red = _sum_direct(own_rows(mix_parts), mix_recv, "sum_mix")
    (dp_ab,), _, (g_a_log, g_dt_bias) = _tok_bwd(f_gates, [p_ab], [], [a_log, dt_bias], [dgbeta], [True], name="d_gates",
                                                 ts=512, tok_dtype=MXU_DTYPE)
    sections = [(dp_qkv, wt_qkv), (dz, wt_z), (dp_ab, wt_ab), (dscb, wt_scs[0]), (dscc, wt_scs[1]), (dscx, wt_scs[2]),
                (dga, wt_ga), (dgb, wt_gb)]
    dh1, gwt_in = None, []
    for k, (dp, wsec) in enumerate(sections):
        dh1 = _mm(as_mat(dp), wsec, add=dh1, name=f"d_in_{k}")
        gwt_in.append(gmm(as_mat(dp), h1m, name=f"g_in_{k}"))
    gwt_in[2] = gwt_in[2][:2 * heads]
    gwt_in = jnp.concatenate(gwt_in, axis=0)
    (grad_x,), (dsh1, dsc1), (dn1w,) = _tok_bwd(_f_norm_mod_skip, [x], [sh1, sc1], [n1w], [as_tok(dh1), dx_skip], [True],
                                                name="d_norm1", ts=256)

    r_in = rows[5]
    win = -(-(r_in + max(r_in * k % ROW_ALIGN for k in range(NDEV))) // 128) * 128
    need_rows = max(_window_start(r_in, k) for k in range(NDEV)) + win
    gwt_in = jnp.pad(gwt_in, ((0, need_rows - gwt_in.shape[0]), (0, 0)))
    recv1 = _exchange_in_chip([(gwt_in, r_in, win, 0)], "scatter_in_chip")
    own = jnp.stack([lax.dynamic_slice_in_dim(gwt_in, _window_start(r_in, 2 * q + ac), win, axis=0) for q in range(4)])
    s1 = _sum_in_chip(own, recv1, "sum_in_chip")
    recv2 = _exchange_chips(s1, "scatter_chips")
    reduced = _sum_chips(s1, recv2, (2 * ax + ay).reshape(1).astype(jnp.int32), "sum_chips")
    gt_w_in = lax.dynamic_slice_in_dim(reduced, r_in * dev - _window_start(r_in, dev), r_in, axis=0)
    g_w_in = gt_w_in.T.reshape(w_in.shape)
    g_w_ffn_in = ffn_red[:rows[3]].T.reshape(w_ffn_in.shape)
    g_w_ffn_out = ffn_red[rows[3]:].reshape(w_ffn_out.shape)
    g_w_gdn_proj, g_w_sc_out, g_w_o = (mix_red[offs[i]:offs[i] + rows[i]].reshape(ref.shape)
                                       for i, ref in enumerate((w_gdn_proj, w_sc_out, w_o)))

    dmod = jnp.concatenate([t.reshape(bl, d) for t in (dsh1, dsc1, dg1, dsh2, dsc2, dg2)], axis=1)
    dmodf = jnp.concatenate([t.reshape(bl, d) for t in (dshf, dscf)], axis=1)
    summed_parts = [dn1w, dn2w, dnfw, g_gnw, g_a_log, g_dt_bias, g_conv_w, g_sc_w, loss_l]
    partial = _all_gather(_pack([dmod, dmodf] + summed_parts, LANE, 8, F32), name="gather_small", hbm=False)
    partial = partial.reshape(NDEV, -1)
    n_rows = bl * (6 * d + 2 * d)
    dmod_all, dmodf_all = _unpack(partial[:, :n_rows], [(bl, 6 * d), (bl, 2 * d)])
    dmod_all, dmodf_all = dmod_all.reshape(NDEV * bl, 6 * d), dmodf_all.reshape(NDEV * bl, 2 * d)
    totals = _row_sum(partial[:, n_rows:], "sum_small")
    t_n1w, t_n2w, t_nfw, t_gnw, t_a_log, t_dt_bias, t_conv_w, t_sc_w, t_loss = [
        t[0] for t in _unpack(totals, [p.shape for p in summed_parts])]
    my_cols = lambda a, n: lax.dynamic_slice_in_dim(a, dev * n, n, axis=1)
    grads = {
        "w_ada": _mm(c_act, my_cols(dmod_all, n_ada), ta=True, name="g_ada").reshape(w_ada.shape),
        "b_ada": _row_sum(dmod_all, "g_ada_bias").reshape(b_ada.shape),
        "norm1_w": t_n1w.reshape(norm1_w.shape),
        "w_in": g_w_in,
        "gdn_conv_w": my_cols(t_conv_w, gdn_conv_w.shape[-1]).reshape(gdn_conv_w.shape),
        "gdn_a_log": t_a_log[:, :heads].reshape(gdn_a_log.shape),
        "gdn_dt_bias": t_dt_bias[:, :heads].reshape(gdn_dt_bias.shape),
        "gdn_norm_w": t_gnw.reshape(gdn_norm_w.shape),
        "w_gdn_proj": g_w_gdn_proj,
        "sc_conv_w": my_cols(t_sc_w, sc_conv_w.shape[-1]).reshape(sc_conv_w.shape),
        "w_sc_out": g_w_sc_out,
        "w_o": g_w_o,
        "norm2_w": t_n2w.reshape(norm2_w.shape),
        "w_ffn_in": g_w_ffn_in,
        "w_ffn_out": g_w_ffn_out,
        "w_ada_f": _mm(c_act, my_cols(dmodf_all, n_adaf), ta=True, name="g_adaf").reshape(w_ada_f.shape),
        "b_ada_f": _row_sum(dmodf_all, "g_adaf_bias").reshape(b_ada_f.shape),
        "normf_w": t_nfw.reshape(normf_w.shape),
    }
    weights = dict(w_ada=w_ada, b_ada=b_ada, norm1_w=norm1_w, w_in=w_in, gdn_conv_w=gdn_conv_w, gdn_a_log=gdn_a_log,
                   gdn_dt_bias=gdn_dt_bias, gdn_norm_w=gdn_norm_w, w_gdn_proj=w_gdn_proj, sc_conv_w=sc_conv_w,
                   w_sc_out=w_sc_out, w_o=w_o, norm2_w=norm2_w, w_ffn_in=w_ffn_in, w_ffn_out=w_ffn_out, w_ada_f=w_ada_f,
                   b_ada_f=b_ada_f, normf_w=normf_w)
    m_in = [m_w_ada, m_b_ada, m_norm1_w, m_w_in, m_gdn_conv_w, m_gdn_a_log, m_gdn_dt_bias, m_gdn_norm_w, m_w_gdn_proj,
            m_sc_conv_w, m_w_sc_out, m_w_o, m_norm2_w, m_w_ffn_in, m_w_ffn_out, m_w_ada_f, m_b_ada_f, m_normf_w]
    v_in = [v_w_ada, v_b_ada, v_norm1_w, v_w_in, v_gdn_conv_w, v_gdn_a_log, v_gdn_dt_bias, v_gdn_norm_w, v_w_gdn_proj,
            v_sc_conv_w, v_w_sc_out, v_w_o, v_norm2_w, v_w_ffn_in, v_w_ffn_out, v_w_ada_f, v_b_ada_f, v_normf_w]
    deltas, new_m, new_v = [], [], []
    for (wname, wt), mt, vt in zip(weights.items(), m_in, v_in):
        dl, mn, vn = _adamw(wt, grads[wname], mt, vt, "adamw_" + wname)
        deltas.append(dl)
        new_m.append(mn)
        new_v.append(vn)
    loss = t_loss[0, 0]
    return (loss, grad_x, *[grads[k] for k in weights], *deltas, *new_m, *new_v)
```

```python
import functools

import jax
import jax.numpy as jnp
from jax import lax
from jax.experimental import pallas as pl
from jax.experimental.pallas import tpu as pltpu

F32 = jnp.float32
MXU_DTYPE = jnp.bfloat16
NDEV = 8
CHUNK = 64
HEAD = 128
LANE = 128
EPS = 1e-6
ADAM_LR, ADAM_B1, ADAM_B2, ADAM_EPS, ADAM_WD, ADAM_STEP = 0.001, 0.9, 0.999, 1e-08, 0.01, 10
VMEM_LIMIT = 48 * 1024 * 1024
MESH_IDS = pl.DeviceIdType.MESH
HIGHEST = lax.Precision.HIGHEST


def _tile(n, cands=(512, 256, 128)):
    for c in cands:
        if n % c == 0:
            return c
    return n


def _cparams(*sem):
    return pltpu.CompilerParams(dimension_semantics=sem, vmem_limit_bytes=VMEM_LIMIT)


def _mm(a, b, *, ta=False, tb=False, add=None, out_dtype=F32, name, b_rows=None, out_rows=None, row_off=0, into=None):
    m, k = (a.shape[1], a.shape[0]) if ta else a.shape
    b_shape = b.shape if b_rows is None else (b_rows[1], b.shape[1])
    n = b_shape[0] if tb else b_shape[1]
    assert k == (b_shape[1] if tb else b_shape[0])
    if ta:
        tm, tn = _tile(m), n if n <= 1024 else _tile(n)
        tk = k if k <= 2048 else _tile(k, (2048, 1024, 512))
    else:
        tm, tn = _tile(m, (1024, 512, 256, 128)), _tile(n)
        tk = k if k <= 1024 else _tile(k, (1024, 512))
    nk = k // tk
    dims = (((0 if ta else 1,), (1 if tb else 0,)), ((), ()))
    has_add = add is not None

    def body(*refs):
        a_ref, b_ref = refs[0], refs[1]
        add_ref = refs[2] if has_add else None
        o_ref = refs[2 + has_add + (into is not None)]
        part = lax.dot_general(a_ref[...].astype(MXU_DTYPE), b_ref[...].astype(MXU_DTYPE), dims,
                               preferred_element_type=F32)

        def finish(acc):
            if has_add:
                acc = acc + add_ref[...]
            o_ref[...] = acc.astype(o_ref.dtype)

        if nk == 1:
            finish(part)
        else:
            acc_ref = refs[-1]
            kk = pl.program_id(2)

            @pl.when(kk == 0)
            def _():
                acc_ref[...] = part

            @pl.when(kk > 0)
            def _():
                acc_ref[...] += part

            @pl.when(kk == nk - 1)
            def _():
                finish(acc_ref[...])

    a_spec = pl.BlockSpec((tk, tm), lambda i, j, kk: (kk, i)) if ta else pl.BlockSpec((tm, tk), lambda i, j, kk: (i, kk))
    if b_rows is None:
        b_spec = pl.BlockSpec((tn, tk), lambda i, j, kk: (j, kk)) if tb else pl.BlockSpec((tk, tn), lambda i, j, kk: (kk, j))
    else:
        at = lambda t: pl.multiple_of(b_rows[0] + t, ROW_ALIGN)
        b_spec = (pl.BlockSpec((pl.Element(tn), pl.Element(tk)), lambda i, j, kk: (at(j * tn), kk * tk)) if tb else
                  pl.BlockSpec((pl.Element(tk), pl.Element(tn)), lambda i, j, kk: (at(kk * tk), j * tn)))
    add_spec = pl.BlockSpec((tm, tn), lambda i, j, kk: (i, j))
    assert row_off % tm == 0
    o_spec = pl.BlockSpec((tm, tn), lambda i, j, kk: (i + row_off // tm, j))
    in_specs = [a_spec, b_spec] + ([add_spec] if has_add else []) + ([pl.BlockSpec(memory_space=pl.ANY)] if into is not None else [])
    args = [a, b] + ([add] if has_add else []) + ([into] if into is not None else [])
    return pl.pallas_call(
        body, name=name, grid=(m // tm, n // tn, nk), in_specs=in_specs, out_specs=o_spec,
        out_shape=jax.ShapeDtypeStruct((out_rows or m, n), out_dtype),
        scratch_shapes=[pltpu.VMEM((tm, tn), F32)] if nk > 1 else [],
        input_output_aliases={len(args) - 1: 0} if into is not None else {},
        compiler_params=_cparams("parallel", "parallel", "arbitrary"),
    )(*args)


def _with_off(xs):
    return [x if isinstance(x, tuple) else (x, 0) for x in xs]


def _spec(kind, arr, off, ts, wb):
    w = arr.shape[-1] if wb is None else wb
    col = (lambda j: 0) if wb is None else functools.partial(lambda j, o: o + j, o=off)
    if kind == "tok":
        return pl.BlockSpec((None, ts, w), lambda j, b, i: (b, i, col(j)))
    if kind == "bat":
        return pl.BlockSpec((None, 1, w), lambda j, b, i: (b, 0, col(j)))
    if off is None:
        return pl.BlockSpec(arr.shape, lambda j, b, i: (0, 0))
    return pl.BlockSpec((arr.shape[0], w), lambda j, b, i: (0, col(j)))


def _in_specs(toks, bats, pars, cots, ts, wb):
    return ([_spec("tok", a, o, ts, wb) for a, o in toks] + [_spec("bat", a, o, ts, wb) for a, o in bats]
            + [_spec("par", a, o, ts, wb) for a, o in pars] + [_spec("tok", a, o, ts, wb) for a, o in cots])


def _tok_fwd(fn, toks, bats, pars, outs, *, name, ts, wb=None, cols=1):
    toks, bats, pars = _with_off(toks), _with_off(bats), _with_off(pars)
    bl, s, _ = toks[0][0].shape
    ts = min(ts, s)
    n_in = len(toks) + len(bats) + len(pars)

    def body(*refs):
        res = fn(*[r[...].astype(F32) for r in refs[:n_in]])
        for r, val in zip(refs[n_in:], res):
            r[...] = val.astype(r.dtype)

    out_specs = [pl.BlockSpec((None, ts, w if wb is None else wb), lambda j, b, i: (b, i, j)) for w, _ in outs]
    return pl.pallas_call(
        body, name=name, grid=(cols, bl, s // ts), in_specs=_in_specs(toks, bats, pars, [], ts, wb),
        out_specs=out_specs, out_shape=[jax.ShapeDtypeStruct((bl, s, w), dt) for w, dt in outs],
        compiler_params=_cparams("parallel", "parallel", "parallel"),
    )(*[a for a, _ in toks + bats + pars])


def _accumulate(ref, val, first):
    @pl.when(first)
    def _():
        ref[...] = val

    @pl.when(jnp.logical_not(first))
    def _():
        ref[...] += val


def _tok_bwd(fn, toks, bats, pars, cots, need, *, name, ts, wb=None, cols=1, tok_dtype=F32, loss=False):
    toks, bats, pars, cots = _with_off(toks), _with_off(bats), _with_off(pars), _with_off(cots)
    bl, s, _ = toks[0][0].shape
    ts = min(ts, s)
    nt, nb, npar, nc = len(toks), len(bats), len(pars), len(cots)
    n_in = nt + nb + npar

    def body(*refs):
        j, b, i = pl.program_id(0), pl.program_id(1), pl.program_id(2)
        outs, vjp = jax.vjp(fn, *[r[...].astype(F32) for r in refs[:n_in]])
        o = n_in + nc
        if loss:
            ct = (jnp.ones_like(outs[0]),)
            tot = jnp.broadcast_to(jnp.sum(outs[0], keepdims=True), (1, LANE))
            _accumulate(refs[o], tot, jnp.logical_and(b == 0, i == 0))
            o += 1
        else:
            ct = tuple(r[...].astype(F32) for r in refs[n_in:n_in + nc])
        grads = vjp(ct)
        for t in range(nt):
            if need[t]:
                refs[o][...] = grads[t].astype(refs[o].dtype)
                o += 1
        for t in range(nb):
            _accumulate(refs[o], grads[nt + t], i == 0)
            o += 1
        for t in range(npar):
            first = jnp.logical_and(b == 0, i == 0)
            if pars[t][1] is None:
                first = jnp.logical_and(first, j == 0)
            _accumulate(refs[o], grads[nt + nb + t], first)
            o += 1

    full = lambda arr: arr.shape[-1] if wb is None else wb * cols
    blk = lambda arr: arr.shape[-1] if wb is None else wb
    out_specs, out_shape = [], []
    if loss:
        out_specs.append(pl.BlockSpec((1, LANE), lambda j, b, i: (0, 0)))
        out_shape.append(jax.ShapeDtypeStruct((1, LANE), F32))
    for t in range(nt):
        if need[t]:
            out_specs.append(pl.BlockSpec((None, ts, blk(toks[t][0])), lambda j, b, i: (b, i, j)))
            dt = tok_dtype[t] if isinstance(tok_dtype, (list, tuple)) else tok_dtype
            out_shape.append(jax.ShapeDtypeStruct((bl, s, full(toks[t][0])), dt))
    for arr, _ in bats:
        out_specs.append(pl.BlockSpec((None, 1, blk(arr)), lambda j, b, i: (b, 0, j)))
        out_shape.append(jax.ShapeDtypeStruct((bl, 1, full(arr)), F32))
    for arr, off in pars:
        if off is None:
            out_specs.append(pl.BlockSpec(arr.shape, lambda j, b, i: (0, 0)))
            out_shape.append(jax.ShapeDtypeStruct(arr.shape, F32))
        else:
            out_specs.append(pl.BlockSpec((arr.shape[0], blk(arr)), lambda j, b, i: (0, j)))
            out_shape.append(jax.ShapeDtypeStruct((arr.shape[0], full(arr)), F32))
    res = list(pl.pallas_call(
        body, name=name, grid=(cols, bl, s // ts), in_specs=_in_specs(toks, bats, pars, cots, ts, wb),
        out_specs=out_specs, out_shape=out_shape, compiler_params=_cparams("arbitrary", "arbitrary", "arbitrary"),
    )(*[a for a, _ in toks + bats + pars + cots]))
    tot = res.pop(0) if loss else None
    dtoks = [res.pop(0) if need[t] else None for t in range(nt)]
    dbats = [res.pop(0) for _ in range(nb)]
    dpars = [res.pop(0) for _ in range(npar)]
    return (tot, dtoks, dbats, dpars) if loss else (dtoks, dbats, dpars)


def _silu(x):
    return x * jax.nn.sigmoid(x)


def _rms(x, w):
    return x * lax.rsqrt(jnp.mean(x * x, axis=-1, keepdims=True) + EPS) * w


def _f_norm_mod(x, shift, scale, w):
    return (_rms(x, w) * (1.0 + scale) + shift,)


def _f_norm_mod_skip(x, shift, scale, w):
    return _rms(x, w) * (1.0 + scale) + shift, x


def _f_res_norm_mod(x, mix, gate, shift, scale, w):
    x2 = x + gate * mix
    return x2, _rms(x2, w) * (1.0 + scale) + shift


def _f_gates(p, a_log, dt_bias, *, heads):
    z = p + dt_bias
    g = -jnp.exp(a_log) * (jnp.maximum(z, 0.0) + jnp.log1p(jnp.exp(jnp.minimum(z, -z))))
    lane = lax.broadcasted_iota(jnp.int32, p.shape, 1)
    return (jnp.where(lane < heads, g, jax.nn.sigmoid(p)),)


def _f_gdn_out(o, z, w):
    return (_rms(o, w) * _silu(z),)


def _f_merge(ga, gb, ya, yb):
    return (jax.nn.sigmoid(ga) * ya + jax.nn.sigmoid(gb) * yb,)


def _f_swiglu(a, b):
    return (_silu(a) * b,)


def _f_loss(x2, ff, tgt, gate, shift, scale, w):
    y = _rms(x2 + gate * ff, w) * (1.0 + scale) + shift
    return (0.5 * jnp.mean(jnp.square(y - tgt), axis=-1, keepdims=True),)


def _shift_down(x, s):
    if s == 0:
        return x
    row = lax.broadcasted_iota(jnp.int32, x.shape, 0)
    return jnp.where(row >= s, pltpu.roll(x, s, 0), 0.0)


def _shift_up(x, s):
    if s == 0:
        return x
    n = x.shape[0]
    row = lax.broadcasted_iota(jnp.int32, x.shape, 0)
    return jnp.where(row < n - s, pltpu.roll(x, n - s, 0), 0.0)


def _conv(x, w):
    width = w.shape[0]
    acc = w[width - 1:width, :] * x
    for j in range(width - 1):
        acc = acc + w[j:j + 1, :] * _shift_down(x, width - 1 - j)
    return acc


def _conv_bwd(dy, x, w, dw_ref, first):
    width = w.shape[0]
    dx = w[width - 1:width, :] * dy
    for j in range(width - 1):
        dx = dx + w[j:j + 1, :] * _shift_up(dy, width - 1 - j)
    for j in range(width):
        row = jnp.sum(dy * _shift_down(x, width - 1 - j), axis=0, keepdims=True)
        _accumulate(dw_ref.at[j:j + 1, :], row, first)
    return dx


def _qkv_act(xc, is_v, scale):
    a = _silu(xc)
    nrm = a * lax.rsqrt(jnp.sum(a * a, axis=-1, keepdims=True) + EPS) * scale
    return jnp.where(is_v, a, nrm)


def _qkv_consts(j, heads):
    is_v = j >= 2 * heads
    scale = jnp.where(j < heads, HEAD ** -0.5, 1.0).astype(F32)
    return is_v, scale


def _qkv_fwd(p, w, heads, name):
    bl, s, w3 = p.shape

    def body(p_ref, w_ref, o_ref):
        is_v, scale = _qkv_consts(pl.program_id(0), heads)
        o_ref[...] = _qkv_act(_conv(p_ref[...], w_ref[...]), is_v, scale)

    blk = pl.BlockSpec((None, s, HEAD), lambda j, b: (b, 0, j))
    return pl.pallas_call(
        body, name=name, grid=(w3 // HEAD, bl), in_specs=[blk, pl.BlockSpec((w.shape[0], HEAD), lambda j, b: (0, j))],
        out_specs=blk, out_shape=jax.ShapeDtypeStruct(p.shape, F32), compiler_params=_cparams("parallel", "parallel"),
    )(p, w)


def _qkv_bwd(p, w, dout, heads, name, rider=None):
    bl, s, w3 = p.shape
    r_args, r_in, r_out, r_shape, r_sems = _ride_specs(rider)

    def body(*refs):
        (p_ref, w_ref, d_ref, dp_ref, dw_ref), hooks = _split_refs(refs, 3, 2, 0, rider)
        first, mid, last = _grid_marks(w3 // HEAD, bl)
        _hooks_before(hooks, first, mid)
        is_v, scale = _qkv_consts(pl.program_id(0), heads)
        x, wv = p_ref[...], w_ref[...]
        _, vjp = jax.vjp(lambda xc: _qkv_act(xc, is_v, scale), _conv(x, wv))
        (dxc,) = vjp(d_ref[...])
        dp_ref[...] = _conv_bwd(dxc, x, wv, dw_ref, pl.program_id(1) == 0).astype(dp_ref.dtype)
        _hooks_after(hooks, last)

    blk = pl.BlockSpec((None, s, HEAD), lambda j, b: (b, 0, j))
    wblk = pl.BlockSpec((w.shape[0], HEAD), lambda j, b: (0, j))
    return pl.pallas_call(
        body, name=name, grid=(w3 // HEAD, bl), in_specs=[blk, wblk, blk] + r_in, out_specs=[blk, wblk] + r_out,
        out_shape=[jax.ShapeDtypeStruct(p.shape, MXU_DTYPE), jax.ShapeDtypeStruct(w.shape, F32)] + r_shape,
        scratch_shapes=r_sems, compiler_params=_cparams("arbitrary", "arbitrary"),
    )(p, w, dout, *r_args)


def _sc_specs(p, w):
    bl, s, w3 = p.shape
    nblk = w3 // 3 // LANE
    sec = lambda k: pl.BlockSpec((None, s, LANE), functools.partial(lambda j, b, k: (b, 0, k * nblk + j), k=k))
    return nblk, [sec(0), sec(1), sec(2)], pl.BlockSpec((w.shape[0], LANE), lambda j, b: (0, j)), \
        pl.BlockSpec((None, s, LANE), lambda j, b: (b, 0, j))


def _sc_fwd(p, w, name):
    bl, s, w3 = p.shape
    nblk, secs, wblk, oblk = _sc_specs(p, w)

    def body(b_ref, c_ref, x_ref, w_ref, o_ref):
        o_ref[...] = (b_ref[...] * _conv(c_ref[...] * x_ref[...], w_ref[...])).astype(o_ref.dtype)

    return pl.pallas_call(
        body, name=name, grid=(nblk, bl), in_specs=secs + [wblk], out_specs=oblk,
        out_shape=jax.ShapeDtypeStruct((bl, s, w3 // 3), MXU_DTYPE), compiler_params=_cparams("parallel", "parallel"),
    )(p, p, p, w)


def _sc_bwd(p, w, dout, name):
    bl, s, w3 = p.shape
    nblk, secs, wblk, oblk = _sc_specs(p, w)

    def body(b_ref, c_ref, x_ref, w_ref, d_ref, db_ref, dc_ref, dx_ref, dw_ref):
        gb, gc, xin, wv, d = b_ref[...], c_ref[...], x_ref[...], w_ref[...], d_ref[...]
        u = gc * xin
        db_ref[...] = (d * _conv(u, wv)).astype(db_ref.dtype)
        du = _conv_bwd(d * gb, u, wv, dw_ref, pl.program_id(1) == 0)
        dc_ref[...] = (du * xin).astype(dc_ref.dtype)
        dx_ref[...] = (du * gc).astype(dx_ref.dtype)

    act = jax.ShapeDtypeStruct((bl, s, w3 // 3), MXU_DTYPE)
    return pl.pallas_call(
        body, name=name, grid=(nblk, bl), in_specs=secs + [wblk, oblk], out_specs=[oblk, oblk, oblk, wblk],
        out_shape=[act, act, act, jax.ShapeDtypeStruct(w.shape, F32)], compiler_params=_cparams("arbitrary", "arbitrary"),
    )(p, p, p, w, dout)


def _bdot(a, b, ca, cb):
    return lax.dot_general(a.astype(MXU_DTYPE), b.astype(MXU_DTYPE), (((ca,), (cb,)), ((), ())),
                           preferred_element_type=F32)


def _hdot(a, b):
    return lax.dot_general(a, b, (((1,), (0,)), ((), ())), precision=HIGHEST, preferred_element_type=F32)


def _lane_col(x, idx):
    lane = lax.broadcasted_iota(jnp.int32, x.shape, 1)
    return jnp.sum(jnp.where(lane == idx, x, 0.0), axis=1, keepdims=True)


def _chunk_masks():
    r = lax.broadcasted_iota(jnp.int32, (CHUNK, CHUNK), 0)
    c = lax.broadcasted_iota(jnp.int32, (CHUNK, CHUNK), 1)
    return r == c, r >= c, r > c


def _dot3(a, b):
    ah, bh = a.astype(MXU_DTYPE), b.astype(MXU_DTYPE)
    al, bl = (a - ah.astype(F32)).astype(MXU_DTYPE), (b - bh.astype(F32)).astype(MXU_DTYPE)
    dot = lambda x, y: lax.dot_general(x, y, (((1,), (0,)), ((), ())), preferred_element_type=F32)
    return dot(ah, bh) + (dot(ah, bl) + dot(al, bh))


def _tri_inv_steps(low, eye):
    x = -low
    p = jnp.where(eye, 1.0, 0.0) + x
    span = 2
    while span < CHUNK:
        x = _dot3(x, x)
        yield
        p = p + _dot3(p, x)
        yield
        span *= 2
    return p


def _round_robin(gens):
    out, live = [None] * len(gens), list(range(len(gens)))
    while live:
        still = []
        for i in live:
            try:
                next(gens[i])
                still.append(i)
            except StopIteration as stop:
                out[i] = stop.value
        live = still
    return out


def _gdn_pre(q, k, v, gc, beta, masks):
    eye, causal, strict = masks
    gc_row = jnp.sum(jnp.where(eye, gc, 0.0), axis=0, keepdims=True)
    decay = jnp.where(causal, jnp.exp(jnp.where(causal, gc - gc_row, 0.0)), 0.0)
    eg = jnp.exp(gc)
    gl = gc[CHUNK - 1:CHUNK, :]
    kb, vb = k * beta, v * beta
    low = jnp.where(strict, _bdot(kb, k, 1, 1) * decay, 0.0)
    qk = jnp.where(causal, _bdot(q, k, 1, 1) * decay, 0.0)
    rest = jnp.exp(gl - gc)
    return dict(decay=decay, eg=eg, gl=gl, kb=kb, vb=vb, kbe=kb * eg, low=low, qk=qk, qg=q * eg, rest=rest, kdec=k * rest)


def _gdn_specs(qkv, gbeta, heads, rev):
    bl, s, w3 = qkv.shape
    d, n = w3 // 3, s // CHUNK
    at = (lambda c: n - 1 - c) if rev else (lambda c: c)
    assert d == heads * HEAD
    sec = pl.BlockSpec((None, CHUNK, w3), lambda b, c: (b, at(c), 0))
    gspec = pl.BlockSpec((None, CHUNK, LANE), lambda b, c: (b, at(c), 0))
    sspec = pl.BlockSpec((None, None, heads, HEAD, HEAD), lambda b, c: (b, at(c), 0, 0, 0))
    tspec = pl.BlockSpec((None, None, heads, CHUNK, CHUNK), lambda b, c: (b, at(c), 0, 0, 0))
    return bl, s, d, n, sec, gspec, sspec, tspec


def _grid_marks(bl, n):
    b, c = pl.program_id(0), pl.program_id(1)
    first = jnp.logical_and(b == 0, c == 0)
    mid = jnp.logical_and(b == bl // 2, c == (0 if bl > 1 else n // 2))
    return first, mid, jnp.logical_and(b == bl - 1, c == n - 1)


def _gdn_fwd(qkv, gbeta, heads, name, rider=None):
    bl, s, d, n, sec, gspec, sspec, tspec = _gdn_specs(qkv, gbeta, heads, False)
    r_args, r_in, r_out, r_shape, r_sems = _ride_specs(rider)

    def body(*refs):
        (x_ref, g_ref, o_ref, s_ref, t_ref, st_ref), hooks = _split_refs(refs, 2, 3, 1, rider)
        first, mid, last = _grid_marks(bl, n)
        _hooks_before(hooks, first, mid)

        @pl.when(pl.program_id(1) == 0)
        def _():
            st_ref[...] = jnp.zeros_like(st_ref)

        masks = _chunk_masks()
        eye, causal, _ = masks
        gblk = g_ref[...]
        gc_all = _hdot(jnp.where(causal, 1.0, 0.0), gblk)
        st_all = st_ref[...]

        def head(h):
            st = st_all[h]
            q, k, v = (x_ref[:, sec * d + h * HEAD:sec * d + (h + 1) * HEAD] for sec in range(3))
            pre = _gdn_pre(q, k, v, _lane_col(gc_all, h), _lane_col(gblk, heads + h), masks)
            yield
            t = yield from _tri_inv_steps(pre["low"], eye)
            u, w = _bdot(t, pre["vb"], 1, 0), _bdot(t, pre["kbe"], 1, 0)
            yield
            vnew = u - _bdot(w, st, 1, 0)
            yield
            out = _bdot(pre["qg"], st, 1, 0) + _bdot(pre["qk"], vnew, 1, 0)
            return out, t, st * jnp.exp(pre["gl"]) + _bdot(pre["kdec"], vnew, 0, 0)

        outs, ts, states = zip(*_round_robin([head(h) for h in range(heads)]))
        o_ref[...] = jnp.concatenate(outs, axis=1)
        s_ref[...] = st_all
        t_ref[...] = jnp.stack(ts)
        st_ref[...] = jnp.stack(states)
        _hooks_after(hooks, last)

    return pl.pallas_call(
        body, name=name, grid=(bl, n), in_specs=[sec, gspec] + r_in,
        out_specs=[pl.BlockSpec((None, CHUNK, d), lambda b, c: (b, c, 0)), sspec, tspec] + r_out,
        out_shape=[jax.ShapeDtypeStruct((bl, s, d), F32), jax.ShapeDtypeStruct((bl, n, heads, HEAD, HEAD), F32),
                   jax.ShapeDtypeStruct((bl, n, heads, CHUNK, CHUNK), F32)] + r_shape,
        scratch_shapes=[pltpu.VMEM((heads, HEAD, HEAD), F32)] + r_sems, compiler_params=_cparams("arbitrary", "arbitrary"),
    )(qkv, gbeta, *r_args)


def _gdn_bwd(qkv, gbeta, dout, s_all, t_all, heads, name, rider=None):
    bl, s, d, n, sec, gspec, sspec, tspec = _gdn_specs(qkv, gbeta, heads, True)
    ospec = pl.BlockSpec((None, CHUNK, d), lambda b, c: (b, n - 1 - c, 0))
    r_args, r_in, r_out, r_shape, r_sems = _ride_specs(rider)

    def body(*refs):
        (x_ref, g_ref, do_ref, s_ref, t_ref, dx_ref, dg_ref, ds_ref), hooks = _split_refs(refs, 5, 2, 1, rider)
        first, mid, last = _grid_marks(bl, n)
        _hooks_before(hooks, first, mid)

        @pl.when(pl.program_id(1) == 0)
        def _():
            ds_ref[...] = jnp.zeros_like(ds_ref)

        masks = _chunk_masks()
        eye, causal, strict = masks
        gblk = g_ref[...]
        gc_all = _hdot(jnp.where(causal, 1.0, 0.0), gblk)
        lane = lax.broadcasted_iota(jnp.int32, gblk.shape, 1)
        last_row = lax.broadcasted_iota(jnp.int32, (CHUNK, 1), 0) == CHUNK - 1
        rowsum = lambda a: jnp.sum(a, axis=1, keepdims=True)
        st_all, t_all_, ds_all = s_ref[...], t_ref[...], ds_ref[...]

        def head(h):
            sl = slice(h * HEAD, (h + 1) * HEAD)
            q, k, v = (x_ref[:, sec * d + h * HEAD:sec * d + (h + 1) * HEAD] for sec in range(3))
            do = do_ref[:, sl]
            beta = _lane_col(gblk, heads + h)
            st, t, dsn = st_all[h], t_all_[h], ds_all[h]
            pre = _gdn_pre(q, k, v, _lane_col(gc_all, h), beta, masks)
            decay, eg, kb, vb, kbe, low, qk, qg, kdec = (pre[x] for x in ("decay", "eg", "kb", "vb", "kbe", "low", "qk", "qg", "kdec"))
            egl = jnp.exp(pre["gl"])
            yield
            u, w = _bdot(t, vb, 1, 0), _bdot(t, kbe, 1, 0)
            yield
            vnew = u - _bdot(w, st, 1, 0)
            yield
            dkdec = _bdot(vnew, dsn, 1, 1)
            dvnew = _bdot(kdec, dsn, 1, 0) + _bdot(qk, do, 0, 0)
            dgl = jnp.sum(dsn * st, keepdims=True) * egl
            dqg = _bdot(do, st, 1, 1)
            dqk = jnp.where(causal, _bdot(do, vnew, 1, 1), 0.0)
            yield
            dw = -_bdot(dvnew, st, 1, 1)
            ds_new = dsn * egl + _bdot(qg, do, 0, 0) - _bdot(w, dvnew, 0, 0)
            yield
            dt = _bdot(dvnew, vb, 1, 1) + _bdot(dw, kbe, 1, 1)
            dvb, dkbe = _bdot(t, dvnew, 0, 0), _bdot(t, dw, 0, 0)
            yield
            inner = _bdot(dt, t, 1, 1)
            yield
            dlow = -jnp.where(strict, _bdot(t, inner, 0, 0), 0.0)
            da, db = dlow * decay, dqk * decay
            yield
            m = dlow * low + dqk * qk
            kdk = dkdec * kdec
            col_of_m = jnp.sum(jnp.where(eye, jnp.sum(m, axis=0, keepdims=True), 0.0), axis=1, keepdims=True)
            dgc = rowsum(m) - col_of_m + rowsum(dqg * qg) + rowsum(dkbe * kbe) - rowsum(kdk)
            dgc = dgc + jnp.where(last_row, dgl + jnp.sum(kdk, keepdims=True), 0.0)
            dkb = _bdot(da, k, 1, 0) + dkbe * eg
            yield
            dk = _bdot(da, kb, 0, 0) + _bdot(db, q, 0, 0) + dkdec * pre["rest"] + dkb * beta
            dq = _bdot(db, k, 1, 0) + dqg * eg
            dbeta = rowsum(dkb * k) + rowsum(dvb * v)
            return dq, dk, dvb * beta, jnp.where(lane == h, dgc, 0.0) + jnp.where(lane == heads + h, dbeta, 0.0), ds_new

        dqs, dks, dvs, dgs, dss = zip(*_round_robin([head(h) for h in range(heads)]))
        dx_ref[...] = jnp.concatenate(dqs + dks + dvs, axis=1)
        ds_ref[...] = jnp.stack(dss)
        dgb = dgs[0]
        for extra in dgs[1:]:
            dgb = dgb + extra
        upper = jnp.where(jnp.logical_or(eye, jnp.logical_not(causal)), 1.0, 0.0)
        dg_ref[...] = jnp.where(lane < heads, _hdot(upper, dgb), dgb)
        _hooks_after(hooks, last)

    return pl.pallas_call(
        body, name=name, grid=(bl, n), in_specs=[sec, gspec, ospec, sspec, tspec] + r_in, out_specs=[sec, gspec] + r_out,
        out_shape=[jax.ShapeDtypeStruct(qkv.shape, F32), jax.ShapeDtypeStruct((bl, s, LANE), F32)] + r_shape,
        scratch_shapes=[pltpu.VMEM((heads, HEAD, HEAD), F32)] + r_sems, compiler_params=_cparams("arbitrary", "arbitrary"),
    )(qkv, gbeta, dout, s_all, t_all, *r_args)


def _position():
    return lax.axis_index("x"), lax.axis_index("y"), lax.axis_index("c")


def _all_gather(x, *, name, hbm):
    space = pltpu.HBM if hbm else pltpu.VMEM

    def body(x_ref, out_ref, send_sems, recv_sems, local_sem):
        ax, ay, ac = _position()
        me, sibling = (ax, ay, ac), (ax, ay, 1 - ac)
        chips = [(1 - ax, ay), (ax, 1 - ay), (1 - ax, 1 - ay)]

        def slot(px, py, pc):
            return out_ref.at[4 * px + 2 * py + pc]

        def copy(k, block, to, src=None):
            return pltpu.make_async_remote_copy(
                src_ref=slot(*block) if src is None else src, dst_ref=slot(*block), send_sem=send_sems.at[k],
                recv_sem=recv_sems.at[k], device_id=to, device_id_type=MESH_IDS)

        mine = pltpu.make_async_copy(x_ref, slot(*me), local_sem)
        mine.start()
        first = [copy(0, me, sibling, src=x_ref)] + [copy(1 + j, me, (*chip, ac), src=x_ref) for j, chip in enumerate(chips)]
        for cp in first:
            cp.start()
        passed = [copy(4 + j, (*chip, ac), sibling) for j, chip in enumerate(chips)]
        for j, chip in enumerate(chips):
            copy(1 + j, (*chip, ac), me).wait_recv()
            passed[j].start()
        copy(0, sibling, me).wait_recv()
        for j, chip in enumerate(chips):
            copy(4 + j, (*chip, 1 - ac), me).wait_recv()
        for cp in first + passed:
            cp.wait_send()
        mine.wait()

    return pl.pallas_call(
        body, name=name, out_shape=jax.ShapeDtypeStruct((NDEV,) + x.shape, x.dtype),
        in_specs=[pl.BlockSpec(memory_space=space)], out_specs=pl.BlockSpec(memory_space=space),
        scratch_shapes=[pltpu.SemaphoreType.DMA((7,)), pltpu.SemaphoreType.DMA((7,)), pltpu.SemaphoreType.DMA],
    )(x)


class _Rider:
    def __init__(self, arrays, out_shapes, sems, hooks):
        self.arrays, self.out_shapes, self.sems, self.hooks = arrays, out_shapes, sems, hooks


def _split_refs(refs, n_in, n_out, n_scratch, rider):
    r_in = len(rider.arrays) if rider else 0
    r_out = len(rider.out_shapes) if rider else 0
    o = n_in + r_in
    o2 = o + n_out + r_out
    host = refs[:n_in] + refs[o:o + n_out] + refs[o2:o2 + n_scratch]
    if rider is None:
        return host, None
    return host, rider.hooks(refs[n_in:o], refs[o + n_out:o2], *refs[o2 + n_scratch:])


def _hooks_before(hooks, first, mid):
    if hooks is not None:
        pl.when(first)(hooks[0])
        pl.when(mid)(hooks[1])


def _hooks_after(hooks, last):
    if hooks is not None:
        pl.when(last)(hooks[2])


def _ride_specs(rider):
    hbm = pl.BlockSpec(memory_space=pltpu.HBM)
    if rider is None:
        return [], [], [], [], []
    n_out = len(rider.out_shapes)
    return list(rider.arrays), [hbm] * len(rider.arrays), [hbm] * n_out, list(rider.out_shapes), list(rider.sems)


def _gather_rider(xs):
    n = len(xs)

    def hooks(x_refs, out_refs, send_sems, recv_sems):
        ax, ay, ac = _position()
        me, sibling = (ax, ay, ac), (ax, ay, 1 - ac)
        chips = [(1 - ax, ay), (ax, 1 - ay), (1 - ax, 1 - ay)]

        def copies(k, block, to, own=False):
            out = []
            for i in range(n):
                slot = out_refs[i].at[4 * block[0] + 2 * block[1] + block[2]]
                out.append(pltpu.make_async_remote_copy(
                    src_ref=x_refs[i] if own else slot, dst_ref=slot, send_sem=send_sems.at[k, i], recv_sem=recv_sems.at[k, i],
                    device_id=to, device_id_type=MESH_IDS))
            return out

        def first():
            for cp in copies(0, me, sibling, own=True):
                cp.start()
            for j, chip in enumerate(chips):
                for cp in copies(1 + j, me, (*chip, ac), own=True):
                    cp.start()

        def mid():
            for j, chip in enumerate(chips):
                for arrived, onward in zip(copies(1 + j, (*chip, ac), me), copies(4 + j, (*chip, ac), sibling)):
                    arrived.wait_recv()
                    onward.start()

        def last():
            for cp in copies(0, sibling, me):
                cp.wait_recv()
            for j, chip in enumerate(chips):
                for cp in copies(4 + j, (*chip, 1 - ac), me):
                    cp.wait_recv()
            for cp in copies(0, me, sibling, own=True):
                cp.wait_send()
            for j, chip in enumerate(chips):
                for cp in copies(1 + j, me, (*chip, ac), own=True) + copies(4 + j, (*chip, ac), sibling):
                    cp.wait_send()

        return first, mid, last

    return _Rider(list(xs), [jax.ShapeDtypeStruct((NDEV,) + x.shape, x.dtype) for x in xs],
                  [pltpu.SemaphoreType.DMA((7, n)), pltpu.SemaphoreType.DMA((7, n))], hooks)


def _scatter_rider(parts):
    packed = sum(r for _, r in parts)
    width, dtype = parts[0][0].shape[1], parts[0][0].dtype

    def hooks(g_refs, out_refs, send_sems, recv_sems):
        (recv_ref,) = out_refs
        ax, ay, ac = _position()

        def peer(rel):
            flip = lambda a, bit: 1 - a if rel & bit else a
            return flip(ax, 4), flip(ay, 2), flip(ac, 1)

        def first():
            for rel in range(1, NDEV):
                px, py, pc = peer(rel)
                off = 0
                for g_ref, (_, r) in zip(g_refs, parts):
                    rows = g_ref.at[pl.ds(pl.multiple_of((4 * px + 2 * py + pc) * r, ROW_ALIGN), r)]
                    pltpu.make_async_remote_copy(
                        src_ref=rows, dst_ref=recv_ref.at[rel - 1, pl.ds(off, r)], send_sem=send_sems.at[rel - 1],
                        recv_sem=recv_sems.at[rel - 1], device_id=(px, py, pc), device_id_type=MESH_IDS).start()
                    off += r

        def last():
            for rel in range(1, NDEV):
                slot = recv_ref.at[rel - 1]
                pltpu.make_async_remote_copy(src_ref=slot, dst_ref=slot, send_sem=send_sems.at[rel - 1],
                                             recv_sem=recv_sems.at[rel - 1], device_id=peer(rel), device_id_type=MESH_IDS).wait()

        return first, lambda: None, last

    return _Rider([g for g, _ in parts], [jax.ShapeDtypeStruct((NDEV - 1, packed, width), dtype)],
                  [pltpu.SemaphoreType.DMA((NDEV - 1,)), pltpu.SemaphoreType.DMA((NDEV - 1,))], hooks)


def _sum_direct(own, recv, name):
    r, w = own.shape
    tr = max(t for t in range(ROW_ALIGN, 257, ROW_ALIGN) if r % t == 0)

    def body(own_ref, *refs):
        acc = own_ref[...].astype(F32)
        for ref in refs[:-1]:
            acc = acc + ref[...].astype(F32)
        refs[-1][...] = acc

    rblk = lambda k: pl.BlockSpec((None, tr, w), functools.partial(lambda i, k: (k, i, 0), k=k))
    blk = pl.BlockSpec((tr, w), lambda i: (i, 0))
    return pl.pallas_call(body, name=name, grid=(r // tr,), in_specs=[blk] + [rblk(k) for k in range(NDEV - 1)],
                          out_specs=blk, out_shape=jax.ShapeDtypeStruct((r, w), F32),
                          compiler_params=_cparams("parallel"))(own, *([recv] * (NDEV - 1)))


ROW_ALIGN = 16


def _window_start(rows_per_dev, k):
    return rows_per_dev * k // ROW_ALIGN * ROW_ALIGN


def _exchange_in_chip(parts, name):
    n = len(parts)
    packed = sum(win for _, _, win, _ in parts)
    width, dtype = parts[0][0].shape[1], parts[0][0].dtype

    def body(*refs):
        g_refs, recv_ref, send_sems, recv_sems = refs[:n], *refs[n:]
        ax, ay, ac = _position()
        sibling = (ax, ay, 1 - ac)
        for q in range(4):
            for g_ref, (_, r, win, off) in zip(g_refs, parts):
                there = g_ref.at[pl.ds(pl.multiple_of(_window_start(r, 2 * q + 1 - ac), ROW_ALIGN), win)]
                pltpu.make_async_remote_copy(src_ref=there, dst_ref=recv_ref.at[q, pl.ds(off, win)], send_sem=send_sems.at[q],
                                             recv_sem=recv_sems.at[q], device_id=sibling, device_id_type=MESH_IDS).start()
        for q in range(4):
            pltpu.make_async_remote_copy(src_ref=recv_ref.at[q], dst_ref=recv_ref.at[q], send_sem=send_sems.at[q],
                                         recv_sem=recv_sems.at[q], device_id=sibling, device_id_type=MESH_IDS).wait()

    hbm = pl.BlockSpec(memory_space=pltpu.HBM)
    return pl.pallas_call(
        body, name=name, out_shape=jax.ShapeDtypeStruct((4, packed, width), dtype), in_specs=[hbm] * n, out_specs=hbm,
        scratch_shapes=[pltpu.SemaphoreType.DMA((4,)), pltpu.SemaphoreType.DMA((4,))],
    )(*[g for g, _, _, _ in parts])


def _exchange_chips(s1, name):
    def body(s_ref, recv_ref, send_sems, recv_sems):
        ax, ay, ac = _position()
        chips = [(1 - ax, ay), (ax, 1 - ay), (1 - ax, 1 - ay)]
        copies = [pltpu.make_async_remote_copy(
            src_ref=s_ref.at[2 * cx + cy], dst_ref=recv_ref.at[r], send_sem=send_sems.at[r], recv_sem=recv_sems.at[r],
            device_id=(cx, cy, ac), device_id_type=MESH_IDS) for r, (cx, cy) in enumerate(chips)]
        for cp in copies:
            cp.start()
        for cp in copies:
            cp.wait_recv()
        for cp in copies:
            cp.wait_send()

    hbm = pl.BlockSpec(memory_space=pltpu.HBM)
    return pl.pallas_call(
        body, name=name, out_shape=jax.ShapeDtypeStruct((3,) + s1.shape[1:], s1.dtype), in_specs=[hbm], out_specs=hbm,
        scratch_shapes=[pltpu.SemaphoreType.DMA((3,)), pltpu.SemaphoreType.DMA((3,))],
    )(s1)


def _sum_in_chip(own, recv, name):
    _, r, w = own.shape
    tr = _tile(r, (256, 128))

    def body(a_ref, b_ref, o_ref):
        o_ref[...] = (a_ref[...].astype(F32) + b_ref[...].astype(F32)).astype(o_ref.dtype)

    blk = pl.BlockSpec((None, tr, w), lambda q, i: (q, i, 0))
    return pl.pallas_call(body, name=name, grid=(4, r // tr), in_specs=[blk, blk], out_specs=blk,
                          out_shape=jax.ShapeDtypeStruct(own.shape, own.dtype),
                          compiler_params=_cparams("parallel", "parallel"))(own, recv)


def _sum_chips(s1, recv, chip, name):
    _, r, w = s1.shape
    tr = _tile(r, (256, 128))

    def body(c_ref, s_ref, r0_ref, r1_ref, r2_ref, o_ref):
        f = lambda ref: ref[...].astype(F32)
        o_ref[...] = ((f(s_ref) + f(r0_ref)) + f(r1_ref)) + f(r2_ref)

    rblk = lambda k: pl.BlockSpec((None, tr, w), functools.partial(lambda i, c, k: (k, i, 0), k=k))
    grid_spec = pltpu.PrefetchScalarGridSpec(
        num_scalar_prefetch=1, grid=(r // tr,),
        in_specs=[pl.BlockSpec((None, tr, w), lambda i, c: (c[0], i, 0)), rblk(0), rblk(1), rblk(2)],
        out_specs=pl.BlockSpec((tr, w), lambda i, c: (i, 0)))
    return pl.pallas_call(body, name=name, grid_spec=grid_spec, out_shape=jax.ShapeDtypeStruct((r, w), F32),
                          compiler_params=_cparams("parallel"))(chip, s1, recv, recv, recv)


def _silu_rows(x, name):
    def body(x_ref, o_ref):
        o_ref[...] = _silu(x_ref[...])

    return pl.pallas_call(body, name=name, out_shape=jax.ShapeDtypeStruct(x.shape, F32))(x)


def _row_sum(x, name):
    def body(x_ref, o_ref):
        acc = x_ref[0:1, :]
        for i in range(1, x.shape[0]):
            acc = acc + x_ref[i:i + 1, :]
        o_ref[...] = acc

    return pl.pallas_call(body, name=name, out_shape=jax.ShapeDtypeStruct((1, x.shape[1]), F32))(x)


def _adamw(w, g, m, v, name):
    cols = w.shape[-1]
    rows = w.size // cols
    tr = _tile(rows, (128,))
    tc = LANE if (tr == rows and rows > 512 and cols % LANE == 0) else cols

    def body(w_ref, g_ref, m_ref, v_ref, d_ref, mo_ref, vo_ref):
        grad = g_ref[...]
        m_new = ADAM_B1 * m_ref[...] + (1.0 - ADAM_B1) * grad
        v_new = ADAM_B2 * v_ref[...] + (1.0 - ADAM_B2) * jnp.square(grad)
        m_hat = m_new / (1.0 - ADAM_B1 ** ADAM_STEP)
        v_hat = v_new / (1.0 - ADAM_B2 ** ADAM_STEP)
        d_ref[...] = -ADAM_LR * (m_hat / (jnp.sqrt(v_hat) + ADAM_EPS) + ADAM_WD * w_ref[...])
        mo_ref[...] = m_new
        vo_ref[...] = v_new

    blk = pl.BlockSpec((tr, tc), lambda i, j: (i, j))
    out = pl.pallas_call(
        body, name=name, grid=(rows // tr, cols // tc), in_specs=[blk] * 4, out_specs=[blk] * 3,
        out_shape=[jax.ShapeDtypeStruct((rows, cols), F32)] * 3, compiler_params=_cparams("parallel", "parallel"),
    )(*[t.reshape(rows, cols) for t in (w, g, m, v)])
    return [t.reshape(w.shape) for t in out]


def _pack(parts, width, row_mult, dtype):
    flat = jnp.concatenate([p.reshape(-1).astype(dtype) for p in parts])
    rows = -(-flat.shape[0] // (width * row_mult)) * row_mult
    return jnp.pad(flat, (0, rows * width - flat.shape[0])).reshape(rows, width)


def _unpack(flat, shapes):
    out, off = [], 0
    for shp in shapes:
        size = 1
        for dim in shp:
            size *= dim
        out.append(flat[:, off:off + size].reshape((flat.shape[0],) + tuple(shp)))
        off += size
    return out


def _devices_to_cols(a):
    _, r, c = a.shape
    return a.transpose(1, 0, 2).reshape(r, NDEV * c)


def kernel(x, c, w_ada, b_ada, norm1_w, w_in, gdn_conv_w, gdn_a_log, gdn_dt_bias, gdn_norm_w, w_gdn_proj, sc_conv_w, w_sc_out, w_o, norm2_w, w_ffn_in, w_ffn_out, w_ada_f, b_ada_f, normf_w, loss_target, m_w_ada, m_b_ada, m_norm1_w, m_w_in, m_gdn_conv_w, m_gdn_a_log, m_gdn_dt_bias, m_gdn_norm_w, m_w_gdn_proj, m_sc_conv_w, m_w_sc_out, m_w_o, m_norm2_w, m_w_ffn_in, m_w_ffn_out, m_w_ada_f, m_b_ada_f, m_normf_w, v_w_ada, v_b_ada, v_norm1_w, v_w_in, v_gdn_conv_w, v_gdn_a_log, v_gdn_dt_bias, v_gdn_norm_w, v_w_gdn_proj, v_sc_conv_w, v_w_sc_out, v_w_o, v_norm2_w, v_w_ffn_in, v_w_ffn_out, v_w_ada_f, v_b_ada_f, v_normf_w):
    bl, s, d = x.shape
    heads = gdn_a_log.shape[-1]
    dff = w_ffn_out.shape[1] * NDEV
    tok = bl * s
    ax, ay, ac = _position()
    dev = 4 * ax + 2 * ay + ac
    as_tok = lambda a: a.reshape(bl, s, a.shape[-1])
    as_mat = lambda a: a.reshape(tok, a.shape[-1])

    small = _all_gather(_pack([c, gdn_conv_w, sc_conv_w], LANE, 8, F32), name="gather_cond", hbm=False)
    c_all, conv_w, sc_w = _unpack(small.reshape(NDEV, -1), [(bl, d), gdn_conv_w.shape[1:], sc_conv_w.shape[1:]])
    c_act = _silu_rows(c_all.reshape(NDEV * bl, d), "cond_silu")
    conv_w, sc_w = _devices_to_cols(conv_w), _devices_to_cols(sc_w)
    n_ada, n_adaf = w_ada.shape[-1], w_ada_f.shape[-1]
    bias = jnp.broadcast_to(lax.dynamic_slice_in_dim(b_ada, dev * n_ada, n_ada, axis=1), (NDEV * bl, n_ada))
    biasf = jnp.broadcast_to(lax.dynamic_slice_in_dim(b_ada_f.reshape(1, -1), dev * n_adaf, n_adaf, axis=1), (NDEV * bl, n_adaf))
    mod_cols = _mm(c_act, w_ada[0], add=bias, name="ada_cols")
    modf_cols = _mm(c_act, w_ada_f, add=biasf, name="adaf_cols")
    mods = _all_gather(jnp.concatenate([mod_cols, modf_cols], axis=1), name="gather_mod", hbm=False)
    mod_all = mods[:, :, :n_ada].transpose(1, 0, 2).reshape(NDEV * bl, NDEV * n_ada)
    modf_all = mods[:, :, n_ada:].transpose(1, 0, 2).reshape(NDEV * bl, NDEV * n_adaf)
    my_rows = lambda a: lax.dynamic_slice_in_dim(a, dev * bl, bl, axis=0)
    sh1, sc1, g1, sh2, sc2, g2 = [t.reshape(bl, 1, d) for t in jnp.split(my_rows(mod_all), 6, axis=1)]
    shf, scf = [t.reshape(bl, 1, d) for t in jnp.split(my_rows(modf_all), 2, axis=1)]

    late = [t.astype(MXU_DTYPE) for t in (w_gdn_proj[0], w_sc_out[0], w_o[0], w_ffn_in[0].T, w_ffn_out[0])]
    rows = [t.shape[0] for t in late] + [w_in.shape[-1]]
    offs = [sum(rows[:i]) for i in range(5)]
    in_rows = -(-rows[5] // ROW_ALIGN) * ROW_ALIGN
    in_send = jnp.pad(w_in[0].T.astype(MXU_DTYPE), ((0, in_rows - rows[5]), (0, 0)))
    wt_in = _all_gather(in_send, name="gather_w_in", hbm=True)[:, :rows[5], :].reshape(NDEV * rows[5], d)
    o_z, o_ab, o_sc, o_ga, o_gb = 3 * d, 4 * d, 4 * d + 2 * heads, 7 * d + 2 * heads, 8 * d + 2 * heads
    s_qkv, s_z, s_sc, s_gate = (0, o_z), (o_z, d), (o_sc, 3 * d), (o_ga, 2 * d)
    wt_ab = jnp.pad(wt_in[o_ab:o_sc], ((0, LANE - 2 * heads), (0, 0)))

    n1w, n2w, nfw = norm1_w.reshape(1, d), norm2_w.reshape(1, d), normf_w.reshape(1, d)
    lanes = lambda a: jnp.pad(a.reshape(1, -1), ((0, 0), (0, LANE - a.size)))
    a_log, dt_bias, gnw = lanes(gdn_a_log), lanes(gdn_dt_bias), gdn_norm_w.reshape(1, HEAD)
    f_gates = functools.partial(_f_gates, heads=heads)
    (h1,) = _tok_fwd(_f_norm_mod, [x], [sh1, sc1], [n1w], [(d, MXU_DTYPE)], name="norm1", ts=256)
    h1m = as_mat(h1)
    p_qkv = as_tok(_mm(h1m, wt_in, tb=True, b_rows=s_qkv, name="in_qkv"))
    p_z = as_tok(_mm(h1m, wt_in, tb=True, b_rows=s_z, name="in_z"))
    p_ab = as_tok(_mm(h1m, wt_ab, tb=True, name="in_ab"))
    p_sc = as_tok(_mm(h1m, wt_in, tb=True, b_rows=s_sc, name="in_sc"))
    p_g = as_tok(_mm(h1m, wt_in, tb=True, b_rows=s_gate, name="in_gate"))
    qkv = _qkv_fwd(p_qkv, conv_w, heads, "qkv_conv")
    (gbeta,) = _tok_fwd(f_gates, [p_ab], [], [a_log, dt_bias], [(LANE, F32)], name="gates", ts=512)
    o, s_all, t_all, *gathered = _gdn_fwd(qkv, gbeta, heads, "gdn", rider=_gather_rider(late))
    wgp, wso, wo, wt_fi, wfo = [lax.dynamic_update_slice_in_dim(g, own[None], dev, axis=0).reshape(NDEV * own.shape[0], d)
                                for g, own in zip(gathered, late)]
    (og,) = _tok_fwd(_f_gdn_out, [o, p_z], [], [(gnw, None)], [(d, MXU_DTYPE)], name="gdn_out", ts=512, wb=HEAD, cols=heads)
    y_a = as_tok(_mm(as_mat(og), wgp, name="gdn_proj"))
    scp = _sc_fwd(p_sc, sc_w, "sc_conv")
    y_b = as_tok(_mm(as_mat(scp), wso, name="sc_out"))
    mcols = d // 512 if d % 512 == 0 else 1
    mwb = d // mcols
    merge_toks = [(p_g, 0), (p_g, mcols), y_a, y_b]
    (mrg,) = _tok_fwd(_f_merge, merge_toks, [], [], [(d, MXU_DTYPE)], name="merge", ts=256, wb=mwb, cols=mcols)
    mix = as_tok(_mm(as_mat(mrg), wo, name="mix_out"))
    x2, h2 = _tok_fwd(_f_res_norm_mod, [x, mix], [g1, sh2, sc2], [n2w], [(d, F32), (d, MXU_DTYPE)], name="norm2", ts=256)
    gu = as_tok(_mm(as_mat(h2), wt_fi, tb=True, name="ffn_in"))
    fwb = _tile(dff, (256, 128))
    fcols = dff // fwb
    (act,) = _tok_fwd(_f_swiglu, [(gu, 0), (gu, fcols)], [], [], [(dff, MXU_DTYPE)], name="swiglu", ts=512, wb=fwb, cols=fcols)
    ff = as_tok(_mm(as_mat(act), wfo, name="ffn_out"))

    loss_l, (dx2, dff_out, _), (dg2, dshf, dscf), (dnfw,) = _tok_bwd(
        _f_loss, [x2, ff, loss_target], [g2, shf, scf], [nfw], [], [True, True, False], name="loss", ts=256, loss=True,
        tok_dtype=[F32, MXU_DTYPE, None])
    dffm = as_mat(dff_out)
    dact = as_tok(_mm(dffm, wfo, tb=True, name="d_ffn_out"))
    gmm = functools.partial(_mm, ta=True, out_dtype=MXU_DTYPE)
    gw_ffn_out = gmm(as_mat(act), dffm, name="g_ffn_out")
    (dgu_a, dgu_b), _, _ = _tok_bwd(_f_swiglu, [(gu, 0), (gu, fcols)], [], [], [dact], [True, True], name="d_swiglu",
                                    ts=512, wb=fwb, cols=fcols, tok_dtype=MXU_DTYPE)
    dh2 = _mm(as_mat(dgu_a), wt_fi, b_rows=(0, dff), name="d_ffn_in_a")
    dh2 = as_tok(_mm(as_mat(dgu_b), wt_fi, b_rows=(dff, dff), add=dh2, name="d_ffn_in_b"))
    h2m = as_mat(h2)
    gwt_ffn_in = gmm(as_mat(dgu_a), h2m, out_rows=2 * dff, name="g_ffn_in_a")
    gwt_ffn_in = gmm(as_mat(dgu_b), h2m, out_rows=2 * dff, row_off=dff, into=gwt_ffn_in, name="g_ffn_in_b")
    (dx_skip, dmix), (dg1, dsh2, dsc2), (dn2w,) = _tok_bwd(
        _f_res_norm_mod, [x, mix], [g1, sh2, sc2], [n2w], [dx2, dh2], [True, True], name="d_norm2", ts=256,
        tok_dtype=[F32, MXU_DTYPE])
    dmixm = as_mat(dmix)
    dmrg = as_tok(_mm(dmixm, wo, tb=True, name="d_mix_out"))
    gw_o = gmm(as_mat(mrg), dmixm, name="g_mix_out")
    (dga, dgb, dya, dyb), _, _ = _tok_bwd(_f_merge, merge_toks, [], [], [dmrg], [True] * 4, name="d_merge", ts=256,
                                          wb=mwb, cols=mcols, tok_dtype=MXU_DTYPE)
    dyam, dybm = as_mat(dya), as_mat(dyb)
    dog = as_tok(_mm(dyam, wgp, tb=True, name="d_gdn_proj"))
    gw_gdn_proj = gmm(as_mat(og), dyam, name="g_gdn_proj")
    dscp = as_tok(_mm(dybm, wso, tb=True, name="d_sc_out"))
    gw_sc_out = gmm(as_mat(scp), dybm, name="g_sc_out")
    dscb, dscc, dscx, g_sc_w = _sc_bwd(p_sc, sc_w, dscp, "d_sc_conv")
    (do, dz), _, (g_gnw,) = _tok_bwd(_f_gdn_out, [o, p_z], [], [(gnw, None)], [dog], [True, True], name="d_gdn_out",
                                     ts=512, wb=HEAD, cols=heads, tok_dtype=[F32, MXU_DTYPE])
    ffn_parts, mix_parts = [(gwt_ffn_in, rows[3]), (gw_ffn_out, rows[4])], [(gw_gdn_proj, rows[0]), (gw_sc_out, rows[1]), (gw_o, rows[2])]
    own_rows = lambda parts: jnp.concatenate([lax.dynamic_slice_in_dim(g, dev * r, r, axis=0) for g, r in parts], axis=0)
    dqkv, dgbeta, ffn_recv = _gdn_bwd(qkv, gbeta, do, s_all, t_all, heads, "d_gdn", rider=_scatter_rider(ffn_parts))
    dp_qkv, g_conv_w, mix_recv = _qkv_bwd(p_qkv, conv_w, dqkv, heads, "d_qkv_conv", rider=_scatter_rider(mix_parts))
    ffn_red = _sum_direct(own_rows(ffn_parts), ffn_recv, "sum_ffn")
    mix_red = _sum_direct(own_rows(mix_parts), mix_recv, "sum_mix")
    (dp_ab,), _, (g_a_log, g_dt_bias) = _tok_bwd(f_gates, [p_ab], [], [a_log, dt_bias], [dgbeta], [True], name="d_gates",
                                                 ts=512, tok_dtype=MXU_DTYPE)
    sections = [(dp_qkv, s_qkv), (dz, s_z), (dp_ab, None), (dscb, (o_sc, d)), (dscc, (o_sc + d, d)), (dscx, (o_sc + 2 * d, d)),
                (dga, (o_ga, d)), (dgb, (o_gb, d))]
    dh1, gwt_in = None, []
    for k, (dp, sec) in enumerate(sections):
        dh1 = _mm(as_mat(dp), wt_ab if sec is None else wt_in, b_rows=sec, add=dh1, name=f"d_in_{k}")
        gwt_in.append(gmm(as_mat(dp), h1m, name=f"g_in_{k}"))
    gwt_in[2] = gwt_in[2][:2 * heads]
    gwt_in = jnp.concatenate(gwt_in, axis=0)
    (grad_x,), (dsh1, dsc1), (dn1w,) = _tok_bwd(_f_norm_mod_skip, [x], [sh1, sc1], [n1w], [as_tok(dh1), dx_skip], [True],
                                                name="d_norm1", ts=256)

    r_in = rows[5]
    win = -(-(r_in + max(r_in * k % ROW_ALIGN for k in range(NDEV))) // 128) * 128
    need_rows = max(_window_start(r_in, k) for k in range(NDEV)) + win
    gwt_in = jnp.pad(gwt_in, ((0, need_rows - gwt_in.shape[0]), (0, 0)))
    recv1 = _exchange_in_chip([(gwt_in, r_in, win, 0)], "scatter_in_chip")
    own = jnp.stack([lax.dynamic_slice_in_dim(gwt_in, _window_start(r_in, 2 * q + ac), win, axis=0) for q in range(4)])
    s1 = _sum_in_chip(own, recv1, "sum_in_chip")
    recv2 = _exchange_chips(s1, "scatter_chips")
    reduced = _sum_chips(s1, recv2, (2 * ax + ay).reshape(1).astype(jnp.int32), "sum_chips")
    gt_w_in = lax.dynamic_slice_in_dim(reduced, r_in * dev - _window_start(r_in, dev), r_in, axis=0)
    g_w_in = gt_w_in.T.reshape(w_in.shape)
    gt_w_ffn_in = ffn_red[:rows[3]]
    g_w_ffn_in = gt_w_ffn_in.T.reshape(w_ffn_in.shape)
    g_w_ffn_out = ffn_red[rows[3]:].reshape(w_ffn_out.shape)
    g_w_gdn_proj, g_w_sc_out, g_w_o = (mix_red[offs[i]:offs[i] + rows[i]].reshape(ref.shape)
                                       for i, ref in enumerate((w_gdn_proj, w_sc_out, w_o)))

    dmod = jnp.concatenate([t.reshape(bl, d) for t in (dsh1, dsc1, dg1, dsh2, dsc2, dg2)], axis=1)
    dmodf = jnp.concatenate([t.reshape(bl, d) for t in (dshf, dscf)], axis=1)
    summed_parts = [dn1w, dn2w, dnfw, g_gnw, g_a_log, g_dt_bias, g_conv_w, g_sc_w, loss_l]
    partial = _all_gather(_pack([dmod, dmodf] + summed_parts, LANE, 8, F32), name="gather_small", hbm=False)
    partial = partial.reshape(NDEV, -1)
    n_rows = bl * (6 * d + 2 * d)
    dmod_all, dmodf_all = _unpack(partial[:, :n_rows], [(bl, 6 * d), (bl, 2 * d)])
    dmod_all, dmodf_all = dmod_all.reshape(NDEV * bl, 6 * d), dmodf_all.reshape(NDEV * bl, 2 * d)
    totals = _row_sum(partial[:, n_rows:], "sum_small")
    t_n1w, t_n2w, t_nfw, t_gnw, t_a_log, t_dt_bias, t_conv_w, t_sc_w, t_loss = [
        t[0] for t in _unpack(totals, [p.shape for p in summed_parts])]
    my_cols = lambda a, n: lax.dynamic_slice_in_dim(a, dev * n, n, axis=1)
    grads = {
        "w_ada": _mm(c_act, my_cols(dmod_all, n_ada), ta=True, name="g_ada").reshape(w_ada.shape),
        "b_ada": _row_sum(dmod_all, "g_ada_bias").reshape(b_ada.shape),
        "norm1_w": t_n1w.reshape(norm1_w.shape),
        "w_in": g_w_in,
        "gdn_conv_w": my_cols(t_conv_w, gdn_conv_w.shape[-1]).reshape(gdn_conv_w.shape),
        "gdn_a_log": t_a_log[:, :heads].reshape(gdn_a_log.shape),
        "gdn_dt_bias": t_dt_bias[:, :heads].reshape(gdn_dt_bias.shape),
        "gdn_norm_w": t_gnw.reshape(gdn_norm_w.shape),
        "w_gdn_proj": g_w_gdn_proj,
        "sc_conv_w": my_cols(t_sc_w, sc_conv_w.shape[-1]).reshape(sc_conv_w.shape),
        "w_sc_out": g_w_sc_out,
        "w_o": g_w_o,
        "norm2_w": t_n2w.reshape(norm2_w.shape),
        "w_ffn_in": g_w_ffn_in,
        "w_ffn_out": g_w_ffn_out,
        "w_ada_f": _mm(c_act, my_cols(dmodf_all, n_adaf), ta=True, name="g_adaf").reshape(w_ada_f.shape),
        "b_ada_f": _row_sum(dmodf_all, "g_adaf_bias").reshape(b_ada_f.shape),
        "normf_w": t_nfw.reshape(normf_w.shape),
    }
    weights = dict(w_ada=w_ada, b_ada=b_ada, norm1_w=norm1_w, w_in=w_in, gdn_conv_w=gdn_conv_w, gdn_a_log=gdn_a_log,
                   gdn_dt_bias=gdn_dt_bias, gdn_norm_w=gdn_norm_w, w_gdn_proj=w_gdn_proj, sc_conv_w=sc_conv_w,
                   w_sc_out=w_sc_out, w_o=w_o, norm2_w=norm2_w, w_ffn_in=w_ffn_in, w_ffn_out=w_ffn_out, w_ada_f=w_ada_f,
                   b_ada_f=b_ada_f, normf_w=normf_w)
    m_in = [m_w_ada, m_b_ada, m_norm1_w, m_w_in, m_gdn_conv_w, m_gdn_a_log, m_gdn_dt_bias, m_gdn_norm_w, m_w_gdn_proj,
            m_sc_conv_w, m_w_sc_out, m_w_o, m_norm2_w, m_w_ffn_in, m_w_ffn_out, m_w_ada_f, m_b_ada_f, m_normf_w]
    v_in = [v_w_ada, v_b_ada, v_norm1_w, v_w_in, v_gdn_conv_w, v_gdn_a_log, v_gdn_dt_bias, v_gdn_norm_w, v_w_gdn_proj,
            v_sc_conv_w, v_w_sc_out, v_w_o, v_norm2_w, v_w_ffn_in, v_w_ffn_out, v_w_ada_f, v_b_ada_f, v_normf_w]
    deltas, new_m, new_v = [], [], []
    grads_t = {"w_in": gt_w_in, "w_ffn_in": gt_w_ffn_in}
    for (wname, wt), mt, vt in zip(weights.items(), m_in, v_in):
        if wname in grads_t:
            back = lambda a, wt=wt: a.T.reshape(wt.shape)
            dl, mn, vn = (back(a) for a in _adamw(wt[0].T, grads_t[wname], mt[0].T, vt[0].T, "adamw_" + wname))
        else:
            dl, mn, vn = _adamw(wt, grads[wname], mt, vt, "adamw_" + wname)
        deltas.append(dl)
        new_m.append(mn)
        new_v.append(vn)
    loss = t_loss[0, 0]
    return (loss, grad_x, *[grads[k] for k in weights], *deltas, *new_m, *new_v)
```

```python
import functools

import jax
import jax.numpy as jnp
from jax import lax
from jax.experimental import pallas as pl
from jax.experimental.pallas import tpu as pltpu

F32 = jnp.float32
MXU_DTYPE = jnp.bfloat16
NDEV = 8
CHUNK = 64
HEAD = 128
LANE = 128
EPS = 1e-6
ADAM_LR, ADAM_B1, ADAM_B2, ADAM_EPS, ADAM_WD, ADAM_STEP = 0.001, 0.9, 0.999, 1e-08, 0.01, 10
VMEM_LIMIT = 48 * 1024 * 1024
MESH_IDS = pl.DeviceIdType.MESH
HIGHEST = lax.Precision.HIGHEST


def _tile(n, cands=(512, 256, 128)):
    for c in cands:
        if n % c == 0:
            return c
    return n


def _cparams(*sem):
    return pltpu.CompilerParams(dimension_semantics=sem, vmem_limit_bytes=VMEM_LIMIT)


def _mm(a, b, *, ta=False, tb=False, add=None, out_dtype=F32, name, b_rows=None, out_rows=None, row_off=0, into=None):
    m, k = (a.shape[1], a.shape[0]) if ta else a.shape
    b_shape = b.shape if b_rows is None else (b_rows[1], b.shape[1])
    n = b_shape[0] if tb else b_shape[1]
    assert k == (b_shape[1] if tb else b_shape[0])
    if ta:
        tm, tn = _tile(m), n if n <= 1024 else _tile(n)
        tk = k if k <= 4096 else _tile(k, (4096, 2048, 1024, 512))
        if tm * tk > 1024 * 2048:
            tk = _tile(k, (2048, 1024, 512))
    else:
        tk = k if k <= 1024 else _tile(k, (1024, 512))
        tn = _tile(n, (1024 if tk <= 1024 else 512, 512, 256, 128))
        tm = _tile(m, (2048 if (tn <= 512 and tk <= 1024) else 1024, 1024, 512, 256, 128))
    nk = k // tk
    dims = (((0 if ta else 1,), (1 if tb else 0,)), ((), ()))
    has_add = add is not None

    def body(*refs):
        a_ref, b_ref = refs[0], refs[1]
        add_ref = refs[2] if has_add else None
        o_ref = refs[2 + has_add + (into is not None)]
        part = lax.dot_general(a_ref[...].astype(MXU_DTYPE), b_ref[...].astype(MXU_DTYPE), dims,
                               preferred_element_type=F32)

        def finish(acc):
            if has_add:
                acc = acc + add_ref[...]
            o_ref[...] = acc.astype(o_ref.dtype)

        if nk == 1:
            finish(part)
        else:
            acc_ref = refs[-1]
            kk = pl.program_id(2)

            @pl.when(kk == 0)
            def _():
                acc_ref[...] = part

            @pl.when(kk > 0)
            def _():
                acc_ref[...] += part

            @pl.when(kk == nk - 1)
            def _():
                finish(acc_ref[...])

    a_spec = pl.BlockSpec((tk, tm), lambda i, j, kk: (kk, i)) if ta else pl.BlockSpec((tm, tk), lambda i, j, kk: (i, kk))
    if b_rows is None:
        b_spec = pl.BlockSpec((tn, tk), lambda i, j, kk: (j, kk)) if tb else pl.BlockSpec((tk, tn), lambda i, j, kk: (kk, j))
    else:
        at = lambda t: pl.multiple_of(b_rows[0] + t, ROW_ALIGN)
        b_spec = (pl.BlockSpec((pl.Element(tn), pl.Element(tk)), lambda i, j, kk: (at(j * tn), kk * tk)) if tb else
                  pl.BlockSpec((pl.Element(tk), pl.Element(tn)), lambda i, j, kk: (at(kk * tk), j * tn)))
    add_spec = pl.BlockSpec((tm, tn), lambda i, j, kk: (i, j))
    assert row_off % tm == 0
    o_spec = pl.BlockSpec((tm, tn), lambda i, j, kk: (i + row_off // tm, j))
    in_specs = [a_spec, b_spec] + ([add_spec] if has_add else []) + ([pl.BlockSpec(memory_space=pl.ANY)] if into is not None else [])
    args = [a, b] + ([add] if has_add else []) + ([into] if into is not None else [])
    return pl.pallas_call(
        body, name=name, grid=(m // tm, n // tn, nk), in_specs=in_specs, out_specs=o_spec,
        out_shape=jax.ShapeDtypeStruct((out_rows or m, n), out_dtype),
        scratch_shapes=[pltpu.VMEM((tm, tn), F32)] if nk > 1 else [],
        input_output_aliases={len(args) - 1: 0} if into is not None else {},
        compiler_params=_cparams("parallel", "parallel", "arbitrary"),
    )(*args)


def _with_off(xs):
    return [x if isinstance(x, tuple) else (x, 0) for x in xs]


def _spec(kind, arr, off, ts, wb):
    w = arr.shape[-1] if wb is None else wb
    col = (lambda j: 0) if wb is None else functools.partial(lambda j, o: o + j, o=off)
    if kind == "tok":
        return pl.BlockSpec((None, ts, w), lambda j, b, i: (b, i, col(j)))
    if kind == "bat":
        return pl.BlockSpec((None, 1, w), lambda j, b, i: (b, 0, col(j)))
    if off is None:
        return pl.BlockSpec(arr.shape, lambda j, b, i: (0, 0))
    return pl.BlockSpec((arr.shape[0], w), lambda j, b, i: (0, col(j)))


def _in_specs(toks, bats, pars, cots, ts, wb):
    return ([_spec("tok", a, o, ts, wb) for a, o in toks] + [_spec("bat", a, o, ts, wb) for a, o in bats]
            + [_spec("par", a, o, ts, wb) for a, o in pars] + [_spec("tok", a, o, ts, wb) for a, o in cots])


def _tok_fwd(fn, toks, bats, pars, outs, *, name, ts, wb=None, cols=1):
    toks, bats, pars = _with_off(toks), _with_off(bats), _with_off(pars)
    bl, s, _ = toks[0][0].shape
    ts = min(ts, s)
    n_in = len(toks) + len(bats) + len(pars)

    def body(*refs):
        res = fn(*[r[...].astype(F32) for r in refs[:n_in]])
        for r, val in zip(refs[n_in:], res):
            r[...] = val.astype(r.dtype)

    out_specs = [pl.BlockSpec((None, ts, w if wb is None else wb), lambda j, b, i: (b, i, j)) for w, _ in outs]
    return pl.pallas_call(
        body, name=name, grid=(cols, bl, s // ts), in_specs=_in_specs(toks, bats, pars, [], ts, wb),
        out_specs=out_specs, out_shape=[jax.ShapeDtypeStruct((bl, s, w), dt) for w, dt in outs],
        compiler_params=_cparams("parallel", "parallel", "parallel"),
    )(*[a for a, _ in toks + bats + pars])


def _accumulate(ref, val, first):
    @pl.when(first)
    def _():
        ref[...] = val

    @pl.when(jnp.logical_not(first))
    def _():
        ref[...] += val


def _tok_bwd(fn, toks, bats, pars, cots, need, *, name, ts, wb=None, cols=1, tok_dtype=F32, loss=False):
    toks, bats, pars, cots = _with_off(toks), _with_off(bats), _with_off(pars), _with_off(cots)
    bl, s, _ = toks[0][0].shape
    ts = min(ts, s)
    nt, nb, npar, nc = len(toks), len(bats), len(pars), len(cots)
    n_in = nt + nb + npar

    def body(*refs):
        j, b, i = pl.program_id(0), pl.program_id(1), pl.program_id(2)
        outs, vjp = jax.vjp(fn, *[r[...].astype(F32) for r in refs[:n_in]])
        o = n_in + nc
        if loss:
            ct = (jnp.ones_like(outs[0]),)
            tot = jnp.broadcast_to(jnp.sum(outs[0], keepdims=True), (1, LANE))
            _accumulate(refs[o], tot, jnp.logical_and(b == 0, i == 0))
            o += 1
        else:
            ct = tuple(r[...].astype(F32) for r in refs[n_in:n_in + nc])
        grads = vjp(ct)
        for t in range(nt):
            if need[t]:
                refs[o][...] = grads[t].astype(refs[o].dtype)
                o += 1
        for t in range(nb):
            _accumulate(refs[o], grads[nt + t], i == 0)
            o += 1
        for t in range(npar):
            first = jnp.logical_and(b == 0, i == 0)
            if pars[t][1] is None:
                first = jnp.logical_and(first, j == 0)
            _accumulate(refs[o], grads[nt + nb + t], first)
            o += 1

    full = lambda arr: arr.shape[-1] if wb is None else wb * cols
    blk = lambda arr: arr.shape[-1] if wb is None else wb
    out_specs, out_shape = [], []
    if loss:
        out_specs.append(pl.BlockSpec((1, LANE), lambda j, b, i: (0, 0)))
        out_shape.append(jax.ShapeDtypeStruct((1, LANE), F32))
    for t in range(nt):
        if need[t]:
            out_specs.append(pl.BlockSpec((None, ts, blk(toks[t][0])), lambda j, b, i: (b, i, j)))
            dt = tok_dtype[t] if isinstance(tok_dtype, (list, tuple)) else tok_dtype
            out_shape.append(jax.ShapeDtypeStruct((bl, s, full(toks[t][0])), dt))
    for arr, _ in bats:
        out_specs.append(pl.BlockSpec((None, 1, blk(arr)), lambda j, b, i: (b, 0, j)))
        out_shape.append(jax.ShapeDtypeStruct((bl, 1, full(arr)), F32))
    for arr, off in pars:
        if off is None:
            out_specs.append(pl.BlockSpec(arr.shape, lambda j, b, i: (0, 0)))
            out_shape.append(jax.ShapeDtypeStruct(arr.shape, F32))
        else:
            out_specs.append(pl.BlockSpec((arr.shape[0], blk(arr)), lambda j, b, i: (0, j)))
            out_shape.append(jax.ShapeDtypeStruct((arr.shape[0], full(arr)), F32))
    res = list(pl.pallas_call(
        body, name=name, grid=(cols, bl, s // ts), in_specs=_in_specs(toks, bats, pars, cots, ts, wb),
        out_specs=out_specs, out_shape=out_shape, compiler_params=_cparams("arbitrary", "arbitrary", "arbitrary"),
    )(*[a for a, _ in toks + bats + pars + cots]))
    tot = res.pop(0) if loss else None
    dtoks = [res.pop(0) if need[t] else None for t in range(nt)]
    dbats = [res.pop(0) for _ in range(nb)]
    dpars = [res.pop(0) for _ in range(npar)]
    return (tot, dtoks, dbats, dpars) if loss else (dtoks, dbats, dpars)


def _silu(x):
    return x * jax.nn.sigmoid(x)


def _rms(x, w):
    return x * lax.rsqrt(jnp.mean(x * x, axis=-1, keepdims=True) + EPS) * w


def _f_norm_mod(x, shift, scale, w):
    return (_rms(x, w) * (1.0 + scale) + shift,)


def _f_norm_mod_skip(x, shift, scale, w):
    return _rms(x, w) * (1.0 + scale) + shift, x


def _f_res_norm_mod(x, mix, gate, shift, scale, w):
    x2 = x + gate * mix
    return x2, _rms(x2, w) * (1.0 + scale) + shift


def _f_gates(p, a_log, dt_bias, *, heads):
    z = p + dt_bias
    g = -jnp.exp(a_log) * (jnp.maximum(z, 0.0) + jnp.log1p(jnp.exp(jnp.minimum(z, -z))))
    lane = lax.broadcasted_iota(jnp.int32, p.shape, 1)
    return (jnp.where(lane < heads, g, jax.nn.sigmoid(p)),)


def _f_gdn_out(o, z, w):
    return (_rms(o, w) * _silu(z),)


def _f_merge(ga, gb, ya, yb):
    return (jax.nn.sigmoid(ga) * ya + jax.nn.sigmoid(gb) * yb,)


def _f_swiglu(a, b):
    return (_silu(a) * b,)


def _f_loss(x2, ff, tgt, gate, shift, scale, w):
    y = _rms(x2 + gate * ff, w) * (1.0 + scale) + shift
    return (0.5 * jnp.mean(jnp.square(y - tgt), axis=-1, keepdims=True),)


def _shift_down(x, s):
    if s == 0:
        return x
    row = lax.broadcasted_iota(jnp.int32, x.shape, 0)
    return jnp.where(row >= s, pltpu.roll(x, s, 0), 0.0)


def _shift_up(x, s):
    if s == 0:
        return x
    n = x.shape[0]
    row = lax.broadcasted_iota(jnp.int32, x.shape, 0)
    return jnp.where(row < n - s, pltpu.roll(x, n - s, 0), 0.0)


def _conv(x, w):
    width = w.shape[0]
    acc = w[width - 1:width, :] * x
    for j in range(width - 1):
        acc = acc + w[j:j + 1, :] * _shift_down(x, width - 1 - j)
    return acc


def _conv_bwd(dy, x, w, dw_ref, first):
    width = w.shape[0]
    dx = w[width - 1:width, :] * dy
    for j in range(width - 1):
        dx = dx + w[j:j + 1, :] * _shift_up(dy, width - 1 - j)
    for j in range(width):
        row = jnp.sum(dy * _shift_down(x, width - 1 - j), axis=0, keepdims=True)
        _accumulate(dw_ref.at[j:j + 1, :], row, first)
    return dx


def _qkv_act(xc, is_v, scale):
    a = _silu(xc)
    nrm = a * lax.rsqrt(jnp.sum(a * a, axis=-1, keepdims=True) + EPS) * scale
    return jnp.where(is_v, a, nrm)


def _qkv_consts(j, heads):
    is_v = j >= 2 * heads
    scale = jnp.where(j < heads, HEAD ** -0.5, 1.0).astype(F32)
    return is_v, scale


def _qkv_fwd(p, w, heads, name):
    bl, s, w3 = p.shape

    def body(p_ref, w_ref, o_ref):
        is_v, scale = _qkv_consts(pl.program_id(0), heads)
        o_ref[...] = _qkv_act(_conv(p_ref[...], w_ref[...]), is_v, scale)

    blk = pl.BlockSpec((None, s, HEAD), lambda j, b: (b, 0, j))
    return pl.pallas_call(
        body, name=name, grid=(w3 // HEAD, bl), in_specs=[blk, pl.BlockSpec((w.shape[0], HEAD), lambda j, b: (0, j))],
        out_specs=blk, out_shape=jax.ShapeDtypeStruct(p.shape, F32), compiler_params=_cparams("parallel", "parallel"),
    )(p, w)


def _qkv_bwd(p, w, dout, heads, name, rider=None):
    bl, s, w3 = p.shape
    r_args, r_in, r_out, r_shape, r_sems = _ride_specs(rider)

    def body(*refs):
        (p_ref, w_ref, d_ref, dp_ref, dw_ref), hooks = _split_refs(refs, 3, 2, 0, rider)
        first, mid, last = _grid_marks(w3 // HEAD, bl)
        _hooks_before(hooks, first, mid)
        is_v, scale = _qkv_consts(pl.program_id(0), heads)
        x, wv = p_ref[...], w_ref[...]
        _, vjp = jax.vjp(lambda xc: _qkv_act(xc, is_v, scale), _conv(x, wv))
        (dxc,) = vjp(d_ref[...])
        dp_ref[...] = _conv_bwd(dxc, x, wv, dw_ref, pl.program_id(1) == 0).astype(dp_ref.dtype)
        _hooks_after(hooks, last)

    blk = pl.BlockSpec((None, s, HEAD), lambda j, b: (b, 0, j))
    wblk = pl.BlockSpec((w.shape[0], HEAD), lambda j, b: (0, j))
    return pl.pallas_call(
        body, name=name, grid=(w3 // HEAD, bl), in_specs=[blk, wblk, blk] + r_in, out_specs=[blk, wblk] + r_out,
        out_shape=[jax.ShapeDtypeStruct(p.shape, MXU_DTYPE), jax.ShapeDtypeStruct(w.shape, F32)] + r_shape,
        scratch_shapes=r_sems, compiler_params=_cparams("arbitrary", "arbitrary"),
    )(p, w, dout, *r_args)


def _sc_specs(p, w):
    bl, s, w3 = p.shape
    nblk = w3 // 3 // LANE
    sec = lambda k: pl.BlockSpec((None, s, LANE), functools.partial(lambda j, b, k: (b, 0, k * nblk + j), k=k))
    return nblk, [sec(0), sec(1), sec(2)], pl.BlockSpec((w.shape[0], LANE), lambda j, b: (0, j)), \
        pl.BlockSpec((None, s, LANE), lambda j, b: (b, 0, j))


def _sc_fwd(p, w, name):
    bl, s, w3 = p.shape
    nblk, secs, wblk, oblk = _sc_specs(p, w)

    def body(b_ref, c_ref, x_ref, w_ref, o_ref):
        o_ref[...] = (b_ref[...] * _conv(c_ref[...] * x_ref[...], w_ref[...])).astype(o_ref.dtype)

    return pl.pallas_call(
        body, name=name, grid=(nblk, bl), in_specs=secs + [wblk], out_specs=oblk,
        out_shape=jax.ShapeDtypeStruct((bl, s, w3 // 3), MXU_DTYPE), compiler_params=_cparams("parallel", "parallel"),
    )(p, p, p, w)


def _sc_bwd(p, w, dout, name):
    bl, s, w3 = p.shape
    nblk, secs, wblk, oblk = _sc_specs(p, w)

    def body(b_ref, c_ref, x_ref, w_ref, d_ref, db_ref, dc_ref, dx_ref, dw_ref):
        gb, gc, xin, wv, d = b_ref[...], c_ref[...], x_ref[...], w_ref[...], d_ref[...]
        u = gc * xin
        db_ref[...] = (d * _conv(u, wv)).astype(db_ref.dtype)
        du = _conv_bwd(d * gb, u, wv, dw_ref, pl.program_id(1) == 0)
        dc_ref[...] = (du * xin).astype(dc_ref.dtype)
        dx_ref[...] = (du * gc).astype(dx_ref.dtype)

    act = jax.ShapeDtypeStruct((bl, s, w3 // 3), MXU_DTYPE)
    return pl.pallas_call(
        body, name=name, grid=(nblk, bl), in_specs=secs + [wblk, oblk], out_specs=[oblk, oblk, oblk, wblk],
        out_shape=[act, act, act, jax.ShapeDtypeStruct(w.shape, F32)], compiler_params=_cparams("arbitrary", "arbitrary"),
    )(p, p, p, w, dout)


def _bdot(a, b, ca, cb):
    return lax.dot_general(a.astype(MXU_DTYPE), b.astype(MXU_DTYPE), (((ca,), (cb,)), ((), ())),
                           preferred_element_type=F32)


def _hdot(a, b):
    return lax.dot_general(a, b, (((1,), (0,)), ((), ())), precision=HIGHEST, preferred_element_type=F32)


def _lane_col(x, idx):
    lane = lax.broadcasted_iota(jnp.int32, x.shape, 1)
    return jnp.sum(jnp.where(lane == idx, x, 0.0), axis=1, keepdims=True)


def _chunk_masks():
    r = lax.broadcasted_iota(jnp.int32, (CHUNK, CHUNK), 0)
    c = lax.broadcasted_iota(jnp.int32, (CHUNK, CHUNK), 1)
    return r == c, r >= c, r > c


def _dot3(a, b):
    ah, bh = a.astype(MXU_DTYPE), b.astype(MXU_DTYPE)
    al, bl = (a - ah.astype(F32)).astype(MXU_DTYPE), (b - bh.astype(F32)).astype(MXU_DTYPE)
    dot = lambda x, y: lax.dot_general(x, y, (((1,), (0,)), ((), ())), preferred_element_type=F32)
    return dot(ah, bh) + (dot(ah, bl) + dot(al, bh))


def _tri_inv_steps(low, eye):
    x = -low
    p = jnp.where(eye, 1.0, 0.0) + x
    span = 2
    while span < CHUNK:
        x = _dot3(x, x)
        yield
        p = p + _dot3(p, x)
        yield
        span *= 2
    return p


def _round_robin(gens):
    out, live = [None] * len(gens), list(range(len(gens)))
    while live:
        still = []
        for i in live:
            try:
                next(gens[i])
                still.append(i)
            except StopIteration as stop:
                out[i] = stop.value
        live = still
    return out


def _gdn_pre(q, k, v, gc, beta, masks):
    eye, causal, strict = masks
    gc_row = jnp.sum(jnp.where(eye, gc, 0.0), axis=0, keepdims=True)
    decay = jnp.where(causal, jnp.exp(jnp.where(causal, gc - gc_row, 0.0)), 0.0)
    eg = jnp.exp(gc)
    gl = gc[CHUNK - 1:CHUNK, :]
    kb, vb = k * beta, v * beta
    low = jnp.where(strict, _bdot(kb, k, 1, 1) * decay, 0.0)
    qk = jnp.where(causal, _bdot(q, k, 1, 1) * decay, 0.0)
    rest = jnp.exp(gl - gc)
    return dict(decay=decay, eg=eg, gl=gl, kb=kb, vb=vb, kbe=kb * eg, low=low, qk=qk, qg=q * eg, rest=rest, kdec=k * rest)


def _gdn_specs(qkv, gbeta, heads, rev):
    bl, s, w3 = qkv.shape
    d, n = w3 // 3, s // CHUNK
    at = (lambda c: n - 1 - c) if rev else (lambda c: c)
    assert d == heads * HEAD
    sec = pl.BlockSpec((None, CHUNK, w3), lambda b, c: (b, at(c), 0))
    gspec = pl.BlockSpec((None, CHUNK, LANE), lambda b, c: (b, at(c), 0))
    sspec = pl.BlockSpec((None, None, heads, HEAD, HEAD), lambda b, c: (b, at(c), 0, 0, 0))
    tspec = pl.BlockSpec((None, None, heads, CHUNK, CHUNK), lambda b, c: (b, at(c), 0, 0, 0))
    return bl, s, d, n, sec, gspec, sspec, tspec


def _grid_marks(bl, n):
    b, c = pl.program_id(0), pl.program_id(1)
    first = jnp.logical_and(b == 0, c == 0)
    mid = jnp.logical_and(b == bl // 2, c == (0 if bl > 1 else n // 2))
    return first, mid, jnp.logical_and(b == bl - 1, c == n - 1)


def _gdn_fwd(qkv, gbeta, heads, name, rider=None):
    bl, s, d, n, sec, gspec, sspec, tspec = _gdn_specs(qkv, gbeta, heads, False)
    r_args, r_in, r_out, r_shape, r_sems = _ride_specs(rider)

    def body(*refs):
        (x_ref, g_ref, o_ref, s_ref, t_ref, st_ref), hooks = _split_refs(refs, 2, 3, 1, rider)
        first, mid, last = _grid_marks(bl, n)
        _hooks_before(hooks, first, mid)

        @pl.when(pl.program_id(1) == 0)
        def _():
            st_ref[...] = jnp.zeros_like(st_ref)

        masks = _chunk_masks()
        eye, causal, _ = masks
        gblk = g_ref[...]
        gc_all = _hdot(jnp.where(causal, 1.0, 0.0), gblk)
        st_all = st_ref[...]

        def head(h):
            st = st_all[h]
            q, k, v = (x_ref[:, sec * d + h * HEAD:sec * d + (h + 1) * HEAD] for sec in range(3))
            pre = _gdn_pre(q, k, v, _lane_col(gc_all, h), _lane_col(gblk, heads + h), masks)
            yield
            t = yield from _tri_inv_steps(pre["low"], eye)
            u, w = _bdot(t, pre["vb"], 1, 0), _bdot(t, pre["kbe"], 1, 0)
            yield
            vnew = u - _bdot(w, st, 1, 0)
            yield
            out = _bdot(pre["qg"], st, 1, 0) + _bdot(pre["qk"], vnew, 1, 0)
            return out, t, st * jnp.exp(pre["gl"]) + _bdot(pre["kdec"], vnew, 0, 0)

        outs, ts, states = zip(*_round_robin([head(h) for h in range(heads)]))
        o_ref[...] = jnp.concatenate(outs, axis=1)
        s_ref[...] = st_all
        t_ref[...] = jnp.stack(ts)
        st_ref[...] = jnp.stack(states)
        _hooks_after(hooks, last)

    return pl.pallas_call(
        body, name=name, grid=(bl, n), in_specs=[sec, gspec] + r_in,
        out_specs=[pl.BlockSpec((None, CHUNK, d), lambda b, c: (b, c, 0)), sspec, tspec] + r_out,
        out_shape=[jax.ShapeDtypeStruct((bl, s, d), F32), jax.ShapeDtypeStruct((bl, n, heads, HEAD, HEAD), F32),
                   jax.ShapeDtypeStruct((bl, n, heads, CHUNK, CHUNK), F32)] + r_shape,
        scratch_shapes=[pltpu.VMEM((heads, HEAD, HEAD), F32)] + r_sems, compiler_params=_cparams("arbitrary", "arbitrary"),
    )(qkv, gbeta, *r_args)


def _gdn_bwd(qkv, gbeta, dout, s_all, t_all, heads, name, rider=None):
    bl, s, d, n, sec, gspec, sspec, tspec = _gdn_specs(qkv, gbeta, heads, True)
    ospec = pl.BlockSpec((None, CHUNK, d), lambda b, c: (b, n - 1 - c, 0))
    r_args, r_in, r_out, r_shape, r_sems = _ride_specs(rider)

    def body(*refs):
        (x_ref, g_ref, do_ref, s_ref, t_ref, dx_ref, dg_ref, ds_ref), hooks = _split_refs(refs, 5, 2, 1, rider)
        first, mid, last = _grid_marks(bl, n)
        _hooks_before(hooks, first, mid)

        @pl.when(pl.program_id(1) == 0)
        def _():
            ds_ref[...] = jnp.zeros_like(ds_ref)

        masks = _chunk_masks()
        eye, causal, strict = masks
        gblk = g_ref[...]
        gc_all = _hdot(jnp.where(causal, 1.0, 0.0), gblk)
        lane = lax.broadcasted_iota(jnp.int32, gblk.shape, 1)
        last_row = lax.broadcasted_iota(jnp.int32, (CHUNK, 1), 0) == CHUNK - 1
        rowsum = lambda a: jnp.sum(a, axis=1, keepdims=True)
        st_all, t_all_, ds_all = s_ref[...], t_ref[...], ds_ref[...]

        def head(h):
            sl = slice(h * HEAD, (h + 1) * HEAD)
            q, k, v = (x_ref[:, sec * d + h * HEAD:sec * d + (h + 1) * HEAD] for sec in range(3))
            do = do_ref[:, sl]
            beta = _lane_col(gblk, heads + h)
            st, t, dsn = st_all[h], t_all_[h], ds_all[h]
            pre = _gdn_pre(q, k, v, _lane_col(gc_all, h), beta, masks)
            decay, eg, kb, vb, kbe, low, qk, qg, kdec = (pre[x] for x in ("decay", "eg", "kb", "vb", "kbe", "low", "qk", "qg", "kdec"))
            egl = jnp.exp(pre["gl"])
            yield
            u, w = _bdot(t, vb, 1, 0), _bdot(t, kbe, 1, 0)
            yield
            vnew = u - _bdot(w, st, 1, 0)
            yield
            dkdec = _bdot(vnew, dsn, 1, 1)
            dvnew = _bdot(kdec, dsn, 1, 0) + _bdot(qk, do, 0, 0)
            dgl = jnp.sum(dsn * st, keepdims=True) * egl
            dqg = _bdot(do, st, 1, 1)
            dqk = jnp.where(causal, _bdot(do, vnew, 1, 1), 0.0)
            yield
            dw = -_bdot(dvnew, st, 1, 1)
            ds_new = dsn * egl + _bdot(qg, do, 0, 0) - _bdot(w, dvnew, 0, 0)
            yield
            dt = _bdot(dvnew, vb, 1, 1) + _bdot(dw, kbe, 1, 1)
            dvb, dkbe = _bdot(t, dvnew, 0, 0), _bdot(t, dw, 0, 0)
            yield
            inner = _bdot(dt, t, 1, 1)
            yield
            dlow = -jnp.where(strict, _bdot(t, inner, 0, 0), 0.0)
            da, db = dlow * decay, dqk * decay
            yield
            m = dlow * low + dqk * qk
            kdk = dkdec * kdec
            col_of_m = jnp.sum(jnp.where(eye, jnp.sum(m, axis=0, keepdims=True), 0.0), axis=1, keepdims=True)
            dgc = rowsum(m) - col_of_m + rowsum(dqg * qg) + rowsum(dkbe * kbe) - rowsum(kdk)
            dgc = dgc + jnp.where(last_row, dgl + jnp.sum(kdk, keepdims=True), 0.0)
            dkb = _bdot(da, k, 1, 0) + dkbe * eg
            yield
            dk = _bdot(da, kb, 0, 0) + _bdot(db, q, 0, 0) + dkdec * pre["rest"] + dkb * beta
            dq = _bdot(db, k, 1, 0) + dqg * eg
            dbeta = rowsum(dkb * k) + rowsum(dvb * v)
            return dq, dk, dvb * beta, jnp.where(lane == h, dgc, 0.0) + jnp.where(lane == heads + h, dbeta, 0.0), ds_new

        dqs, dks, dvs, dgs, dss = zip(*_round_robin([head(h) for h in range(heads)]))
        dx_ref[...] = jnp.concatenate(dqs + dks + dvs, axis=1)
        ds_ref[...] = jnp.stack(dss)
        dgb = dgs[0]
        for extra in dgs[1:]:
            dgb = dgb + extra
        upper = jnp.where(jnp.logical_or(eye, jnp.logical_not(causal)), 1.0, 0.0)
        dg_ref[...] = jnp.where(lane < heads, _hdot(upper, dgb), dgb)
        _hooks_after(hooks, last)

    return pl.pallas_call(
        body, name=name, grid=(bl, n), in_specs=[sec, gspec, ospec, sspec, tspec] + r_in, out_specs=[sec, gspec] + r_out,
        out_shape=[jax.ShapeDtypeStruct(qkv.shape, F32), jax.ShapeDtypeStruct((bl, s, LANE), F32)] + r_shape,
        scratch_shapes=[pltpu.VMEM((heads, HEAD, HEAD), F32)] + r_sems, compiler_params=_cparams("arbitrary", "arbitrary"),
    )(qkv, gbeta, dout, s_all, t_all, *r_args)


def _position():
    return lax.axis_index("x"), lax.axis_index("y"), lax.axis_index("c")


def _all_gather(x, *, name, hbm):
    space = pltpu.HBM if hbm else pltpu.VMEM

    def body(x_ref, out_ref, send_sems, recv_sems, local_sem):
        ax, ay, ac = _position()
        me, sibling = (ax, ay, ac), (ax, ay, 1 - ac)
        chips = [(1 - ax, ay), (ax, 1 - ay), (1 - ax, 1 - ay)]

        def slot(px, py, pc):
            return out_ref.at[4 * px + 2 * py + pc]

        def copy(k, block, to, src=None):
            return pltpu.make_async_remote_copy(
                src_ref=slot(*block) if src is None else src, dst_ref=slot(*block), send_sem=send_sems.at[k],
                recv_sem=recv_sems.at[k], device_id=to, device_id_type=MESH_IDS)

        mine = pltpu.make_async_copy(x_ref, slot(*me), local_sem)
        mine.start()
        first = [copy(0, me, sibling, src=x_ref)] + [copy(1 + j, me, (*chip, ac), src=x_ref) for j, chip in enumerate(chips)]
        for cp in first:
            cp.start()
        passed = [copy(4 + j, (*chip, ac), sibling) for j, chip in enumerate(chips)]
        for j, chip in enumerate(chips):
            copy(1 + j, (*chip, ac), me).wait_recv()
            passed[j].start()
        copy(0, sibling, me).wait_recv()
        for j, chip in enumerate(chips):
            copy(4 + j, (*chip, 1 - ac), me).wait_recv()
        for cp in first + passed:
            cp.wait_send()
        mine.wait()

    return pl.pallas_call(
        body, name=name, out_shape=jax.ShapeDtypeStruct((NDEV,) + x.shape, x.dtype),
        in_specs=[pl.BlockSpec(memory_space=space)], out_specs=pl.BlockSpec(memory_space=space),
        scratch_shapes=[pltpu.SemaphoreType.DMA((7,)), pltpu.SemaphoreType.DMA((7,)), pltpu.SemaphoreType.DMA],
    )(x)


class _Rider:
    def __init__(self, arrays, out_shapes, sems, hooks):
        self.arrays, self.out_shapes, self.sems, self.hooks = arrays, out_shapes, sems, hooks


def _split_refs(refs, n_in, n_out, n_scratch, rider):
    r_in = len(rider.arrays) if rider else 0
    r_out = len(rider.out_shapes) if rider else 0
    o = n_in + r_in
    o2 = o + n_out + r_out
    host = refs[:n_in] + refs[o:o + n_out] + refs[o2:o2 + n_scratch]
    if rider is None:
        return host, None
    return host, rider.hooks(refs[n_in:o], refs[o + n_out:o2], *refs[o2 + n_scratch:])


def _hooks_before(hooks, first, mid):
    if hooks is not None:
        pl.when(first)(hooks[0])
        pl.when(mid)(hooks[1])


def _hooks_after(hooks, last):
    if hooks is not None:
        pl.when(last)(hooks[2])


def _ride_specs(rider):
    hbm = pl.BlockSpec(memory_space=pltpu.HBM)
    if rider is None:
        return [], [], [], [], []
    n_out = len(rider.out_shapes)
    return list(rider.arrays), [hbm] * len(rider.arrays), [hbm] * n_out, list(rider.out_shapes), list(rider.sems)


def _gather_rider(xs):
    n = len(xs)

    def hooks(x_refs, out_refs, send_sems, recv_sems):
        ax, ay, ac = _position()
        me, sibling = (ax, ay, ac), (ax, ay, 1 - ac)
        chips = [(1 - ax, ay), (ax, 1 - ay), (1 - ax, 1 - ay)]

        def copies(k, block, to, own=False):
            out = []
            for i in range(n):
                slot = out_refs[i].at[4 * block[0] + 2 * block[1] + block[2]]
                out.append(pltpu.make_async_remote_copy(
                    src_ref=x_refs[i] if own else slot, dst_ref=slot, send_sem=send_sems.at[k, i], recv_sem=recv_sems.at[k, i],
                    device_id=to, device_id_type=MESH_IDS))
            return out

        def first():
            for cp in copies(0, me, sibling, own=True):
                cp.start()
            for j, chip in enumerate(chips):
                for cp in copies(1 + j, me, (*chip, ac), own=True):
                    cp.start()

        def mid():
            for j, chip in enumerate(chips):
                for arrived, onward in zip(copies(1 + j, (*chip, ac), me), copies(4 + j, (*chip, ac), sibling)):
                    arrived.wait_recv()
                    onward.start()

        def last():
            for cp in copies(0, sibling, me):
                cp.wait_recv()
            for j, chip in enumerate(chips):
                for cp in copies(4 + j, (*chip, 1 - ac), me):
                    cp.wait_recv()
            for cp in copies(0, me, sibling, own=True):
                cp.wait_send()
            for j, chip in enumerate(chips):
                for cp in copies(1 + j, me, (*chip, ac), own=True) + copies(4 + j, (*chip, ac), sibling):
                    cp.wait_send()

        return first, mid, last

    return _Rider(list(xs), [jax.ShapeDtypeStruct((NDEV,) + x.shape, x.dtype) for x in xs],
                  [pltpu.SemaphoreType.DMA((7, n)), pltpu.SemaphoreType.DMA((7, n))], hooks)


def _scatter_rider(parts):
    packed = sum(r for _, r in parts)
    width, dtype = parts[0][0].shape[1], parts[0][0].dtype

    def hooks(g_refs, out_refs, send_sems, recv_sems):
        (recv_ref,) = out_refs
        ax, ay, ac = _position()

        def peer(rel):
            flip = lambda a, bit: 1 - a if rel & bit else a
            return flip(ax, 4), flip(ay, 2), flip(ac, 1)

        def first():
            for rel in range(1, NDEV):
                px, py, pc = peer(rel)
                off = 0
                for g_ref, (_, r) in zip(g_refs, parts):
                    rows = g_ref.at[pl.ds(pl.multiple_of((4 * px + 2 * py + pc) * r, ROW_ALIGN), r)]
                    pltpu.make_async_remote_copy(
                        src_ref=rows, dst_ref=recv_ref.at[rel - 1, pl.ds(off, r)], send_sem=send_sems.at[rel - 1],
                        recv_sem=recv_sems.at[rel - 1], device_id=(px, py, pc), device_id_type=MESH_IDS).start()
                    off += r

        def last():
            for rel in range(1, NDEV):
                slot = recv_ref.at[rel - 1]
                pltpu.make_async_remote_copy(src_ref=slot, dst_ref=slot, send_sem=send_sems.at[rel - 1],
                                             recv_sem=recv_sems.at[rel - 1], device_id=peer(rel), device_id_type=MESH_IDS).wait()

        return first, lambda: None, last

    return _Rider([g for g, _ in parts], [jax.ShapeDtypeStruct((NDEV - 1, packed, width), dtype)],
                  [pltpu.SemaphoreType.DMA((NDEV - 1,)), pltpu.SemaphoreType.DMA((NDEV - 1,))], hooks)


def _sum_direct(own, recv, name):
    r, w = own.shape
    tr = max(t for t in range(ROW_ALIGN, 257, ROW_ALIGN) if r % t == 0)

    def body(own_ref, *refs):
        acc = own_ref[...].astype(F32)
        for ref in refs[:-1]:
            acc = acc + ref[...].astype(F32)
        refs[-1][...] = acc

    rblk = lambda k: pl.BlockSpec((None, tr, w), functools.partial(lambda i, k: (k, i, 0), k=k))
    blk = pl.BlockSpec((tr, w), lambda i: (i, 0))
    return pl.pallas_call(body, name=name, grid=(r // tr,), in_specs=[blk] + [rblk(k) for k in range(NDEV - 1)],
                          out_specs=blk, out_shape=jax.ShapeDtypeStruct((r, w), F32),
                          compiler_params=_cparams("parallel"))(own, *([recv] * (NDEV - 1)))


ROW_ALIGN = 16


def _window_start(rows_per_dev, k):
    return rows_per_dev * k // ROW_ALIGN * ROW_ALIGN


def _exchange_in_chip(parts, name):
    n = len(parts)
    packed = sum(win for _, _, win, _ in parts)
    width, dtype = parts[0][0].shape[1], parts[0][0].dtype

    def body(*refs):
        g_refs, recv_ref, send_sems, recv_sems = refs[:n], *refs[n:]
        ax, ay, ac = _position()
        sibling = (ax, ay, 1 - ac)
        for q in range(4):
            for g_ref, (_, r, win, off) in zip(g_refs, parts):
                there = g_ref.at[pl.ds(pl.multiple_of(_window_start(r, 2 * q + 1 - ac), ROW_ALIGN), win)]
                pltpu.make_async_remote_copy(src_ref=there, dst_ref=recv_ref.at[q, pl.ds(off, win)], send_sem=send_sems.at[q],
                                             recv_sem=recv_sems.at[q], device_id=sibling, device_id_type=MESH_IDS).start()
        for q in range(4):
            pltpu.make_async_remote_copy(src_ref=recv_ref.at[q], dst_ref=recv_ref.at[q], send_sem=send_sems.at[q],
                                         recv_sem=recv_sems.at[q], device_id=sibling, device_id_type=MESH_IDS).wait()

    hbm = pl.BlockSpec(memory_space=pltpu.HBM)
    return pl.pallas_call(
        body, name=name, out_shape=jax.ShapeDtypeStruct((4, packed, width), dtype), in_specs=[hbm] * n, out_specs=hbm,
        scratch_shapes=[pltpu.SemaphoreType.DMA((4,)), pltpu.SemaphoreType.DMA((4,))],
    )(*[g for g, _, _, _ in parts])


def _exchange_chips(s1, name):
    def body(s_ref, recv_ref, send_sems, recv_sems):
        ax, ay, ac = _position()
        chips = [(1 - ax, ay), (ax, 1 - ay), (1 - ax, 1 - ay)]
        copies = [pltpu.make_async_remote_copy(
            src_ref=s_ref.at[2 * cx + cy], dst_ref=recv_ref.at[r], send_sem=send_sems.at[r], recv_sem=recv_sems.at[r],
            device_id=(cx, cy, ac), device_id_type=MESH_IDS) for r, (cx, cy) in enumerate(chips)]
        for cp in copies:
            cp.start()
        for cp in copies:
            cp.wait_recv()
        for cp in copies:
            cp.wait_send()

    hbm = pl.BlockSpec(memory_space=pltpu.HBM)
    return pl.pallas_call(
        body, name=name, out_shape=jax.ShapeDtypeStruct((3,) + s1.shape[1:], s1.dtype), in_specs=[hbm], out_specs=hbm,
        scratch_shapes=[pltpu.SemaphoreType.DMA((3,)), pltpu.SemaphoreType.DMA((3,))],
    )(s1)


def _sum_in_chip(own, recv, name):
    _, r, w = own.shape
    tr = _tile(r, (256, 128))

    def body(a_ref, b_ref, o_ref):
        o_ref[...] = (a_ref[...].astype(F32) + b_ref[...].astype(F32)).astype(o_ref.dtype)

    blk = pl.BlockSpec((None, tr, w), lambda q, i: (q, i, 0))
    return pl.pallas_call(body, name=name, grid=(4, r // tr), in_specs=[blk, blk], out_specs=blk,
                          out_shape=jax.ShapeDtypeStruct(own.shape, own.dtype),
                          compiler_params=_cparams("parallel", "parallel"))(own, recv)


def _sum_chips(s1, recv, chip, name):
    _, r, w = s1.shape
    tr = _tile(r, (256, 128))

    def body(c_ref, s_ref, r0_ref, r1_ref, r2_ref, o_ref):
        f = lambda ref: ref[...].astype(F32)
        o_ref[...] = ((f(s_ref) + f(r0_ref)) + f(r1_ref)) + f(r2_ref)

    rblk = lambda k: pl.BlockSpec((None, tr, w), functools.partial(lambda i, c, k: (k, i, 0), k=k))
    grid_spec = pltpu.PrefetchScalarGridSpec(
        num_scalar_prefetch=1, grid=(r // tr,),
        in_specs=[pl.BlockSpec((None, tr, w), lambda i, c: (c[0], i, 0)), rblk(0), rblk(1), rblk(2)],
        out_specs=pl.BlockSpec((tr, w), lambda i, c: (i, 0)))
    return pl.pallas_call(body, name=name, grid_spec=grid_spec, out_shape=jax.ShapeDtypeStruct((r, w), F32),
                          compiler_params=_cparams("parallel"))(chip, s1, recv, recv, recv)


def _silu_rows(x, name):
    def body(x_ref, o_ref):
        o_ref[...] = _silu(x_ref[...])

    return pl.pallas_call(body, name=name, out_shape=jax.ShapeDtypeStruct(x.shape, F32))(x)


def _row_sum(x, name):
    def body(x_ref, o_ref):
        acc = x_ref[0:1, :]
        for i in range(1, x.shape[0]):
            acc = acc + x_ref[i:i + 1, :]
        o_ref[...] = acc

    return pl.pallas_call(body, name=name, out_shape=jax.ShapeDtypeStruct((1, x.shape[1]), F32))(x)


def _adamw(w, g, m, v, name):
    cols = w.shape[-1]
    rows = w.size // cols
    tr = _tile(rows, (128,))
    tc = LANE if (tr == rows and rows > 512 and cols % LANE == 0) else cols

    def body(w_ref, g_ref, m_ref, v_ref, d_ref, mo_ref, vo_ref):
        grad = g_ref[...]
        m_new = ADAM_B1 * m_ref[...] + (1.0 - ADAM_B1) * grad
        v_new = ADAM_B2 * v_ref[...] + (1.0 - ADAM_B2) * jnp.square(grad)
        m_hat = m_new / (1.0 - ADAM_B1 ** ADAM_STEP)
        v_hat = v_new / (1.0 - ADAM_B2 ** ADAM_STEP)
        d_ref[...] = -ADAM_LR * (m_hat / (jnp.sqrt(v_hat) + ADAM_EPS) + ADAM_WD * w_ref[...])
        mo_ref[...] = m_new
        vo_ref[...] = v_new

    blk = pl.BlockSpec((tr, tc), lambda i, j: (i, j))
    out = pl.pallas_call(
        body, name=name, grid=(rows // tr, cols // tc), in_specs=[blk] * 4, out_specs=[blk] * 3,
        out_shape=[jax.ShapeDtypeStruct((rows, cols), F32)] * 3, compiler_params=_cparams("parallel", "parallel"),
    )(*[t.reshape(rows, cols) for t in (w, g, m, v)])
    return [t.reshape(w.shape) for t in out]


def _pack(parts, width, row_mult, dtype):
    flat = jnp.concatenate([p.reshape(-1).astype(dtype) for p in parts])
    rows = -(-flat.shape[0] // (width * row_mult)) * row_mult
    return jnp.pad(flat, (0, rows * width - flat.shape[0])).reshape(rows, width)


def _unpack(flat, shapes):
    out, off = [], 0
    for shp in shapes:
        size = 1
        for dim in shp:
            size *= dim
        out.append(flat[:, off:off + size].reshape((flat.shape[0],) + tuple(shp)))
        off += size
    return out


def _devices_to_cols(a):
    _, r, c = a.shape
    return a.transpose(1, 0, 2).reshape(r, NDEV * c)


def kernel(x, c, w_ada, b_ada, norm1_w, w_in, gdn_conv_w, gdn_a_log, gdn_dt_bias, gdn_norm_w, w_gdn_proj, sc_conv_w, w_sc_out, w_o, norm2_w, w_ffn_in, w_ffn_out, w_ada_f, b_ada_f, normf_w, loss_target, m_w_ada, m_b_ada, m_norm1_w, m_w_in, m_gdn_conv_w, m_gdn_a_log, m_gdn_dt_bias, m_gdn_norm_w, m_w_gdn_proj, m_sc_conv_w, m_w_sc_out, m_w_o, m_norm2_w, m_w_ffn_in, m_w_ffn_out, m_w_ada_f, m_b_ada_f, m_normf_w, v_w_ada, v_b_ada, v_norm1_w, v_w_in, v_gdn_conv_w, v_gdn_a_log, v_gdn_dt_bias, v_gdn_norm_w, v_w_gdn_proj, v_sc_conv_w, v_w_sc_out, v_w_o, v_norm2_w, v_w_ffn_in, v_w_ffn_out, v_w_ada_f, v_b_ada_f, v_normf_w):
    bl, s, d = x.shape
    heads = gdn_a_log.shape[-1]
    dff = w_ffn_out.shape[1] * NDEV
    tok = bl * s
    ax, ay, ac = _position()
    dev = 4 * ax + 2 * ay + ac
    as_tok = lambda a: a.reshape(bl, s, a.shape[-1])
    as_mat = lambda a: a.reshape(tok, a.shape[-1])

    small = _all_gather(_pack([c, gdn_conv_w, sc_conv_w], LANE, 8, F32), name="gather_cond", hbm=False)
    c_all, conv_w, sc_w = _unpack(small.reshape(NDEV, -1), [(bl, d), gdn_conv_w.shape[1:], sc_conv_w.shape[1:]])
    c_act = _silu_rows(c_all.reshape(NDEV * bl, d), "cond_silu")
    conv_w, sc_w = _devices_to_cols(conv_w), _devices_to_cols(sc_w)
    n_ada, n_adaf = w_ada.shape[-1], w_ada_f.shape[-1]
    bias = jnp.broadcast_to(lax.dynamic_slice_in_dim(b_ada, dev * n_ada, n_ada, axis=1), (NDEV * bl, n_ada))
    biasf = jnp.broadcast_to(lax.dynamic_slice_in_dim(b_ada_f.reshape(1, -1), dev * n_adaf, n_adaf, axis=1), (NDEV * bl, n_adaf))
    mod_cols = _mm(c_act, w_ada[0], add=bias, name="ada_cols")
    modf_cols = _mm(c_act, w_ada_f, add=biasf, name="adaf_cols")
    mods = _all_gather(jnp.concatenate([mod_cols, modf_cols], axis=1), name="gather_mod", hbm=False)
    mod_all = mods[:, :, :n_ada].transpose(1, 0, 2).reshape(NDEV * bl, NDEV * n_ada)
    modf_all = mods[:, :, n_ada:].transpose(1, 0, 2).reshape(NDEV * bl, NDEV * n_adaf)
    my_rows = lambda a: lax.dynamic_slice_in_dim(a, dev * bl, bl, axis=0)
    sh1, sc1, g1, sh2, sc2, g2 = [t.reshape(bl, 1, d) for t in jnp.split(my_rows(mod_all), 6, axis=1)]
    shf, scf = [t.reshape(bl, 1, d) for t in jnp.split(my_rows(modf_all), 2, axis=1)]

    late = [t.astype(MXU_DTYPE) for t in (w_gdn_proj[0], w_sc_out[0], w_o[0], w_ffn_in[0].T, w_ffn_out[0])]
    rows = [t.shape[0] for t in late] + [w_in.shape[-1]]
    offs = [sum(rows[:i]) for i in range(5)]
    in_rows = -(-rows[5] // ROW_ALIGN) * ROW_ALIGN
    in_send = jnp.pad(w_in[0].T.astype(MXU_DTYPE), ((0, in_rows - rows[5]), (0, 0)))
    wt_in = _all_gather(in_send, name="gather_w_in", hbm=True)[:, :rows[5], :].reshape(NDEV * rows[5], d)
    o_z, o_ab, o_sc, o_ga, o_gb = 3 * d, 4 * d, 4 * d + 2 * heads, 7 * d + 2 * heads, 8 * d + 2 * heads
    s_qkv, s_z, s_sc, s_gate = (0, o_z), (o_z, d), (o_sc, 3 * d), (o_ga, 2 * d)
    wt_ab = jnp.pad(wt_in[o_ab:o_sc], ((0, LANE - 2 * heads), (0, 0)))

    n1w, n2w, nfw = norm1_w.reshape(1, d), norm2_w.reshape(1, d), normf_w.reshape(1, d)
    lanes = lambda a: jnp.pad(a.reshape(1, -1), ((0, 0), (0, LANE - a.size)))
    a_log, dt_bias, gnw = lanes(gdn_a_log), lanes(gdn_dt_bias), gdn_norm_w.reshape(1, HEAD)
    f_gates = functools.partial(_f_gates, heads=heads)
    (h1,) = _tok_fwd(_f_norm_mod, [x], [sh1, sc1], [n1w], [(d, MXU_DTYPE)], name="norm1", ts=512)
    h1m = as_mat(h1)
    p_qkv = as_tok(_mm(h1m, wt_in, tb=True, b_rows=s_qkv, name="in_qkv"))
    p_z = as_tok(_mm(h1m, wt_in, tb=True, b_rows=s_z, name="in_z"))
    p_ab = as_tok(_mm(h1m, wt_ab, tb=True, name="in_ab"))
    p_sc = as_tok(_mm(h1m, wt_in, tb=True, b_rows=s_sc, name="in_sc"))
    p_g = as_tok(_mm(h1m, wt_in, tb=True, b_rows=s_gate, name="in_gate"))
    qkv = _qkv_fwd(p_qkv, conv_w, heads, "qkv_conv")
    (gbeta,) = _tok_fwd(f_gates, [p_ab], [], [a_log, dt_bias], [(LANE, F32)], name="gates", ts=512)
    o, s_all, t_all, *gathered = _gdn_fwd(qkv, gbeta, heads, "gdn", rider=_gather_rider(late))
    wgp, wso, wo, wt_fi, wfo = [lax.dynamic_update_slice_in_dim(g, own[None], dev, axis=0).reshape(NDEV * own.shape[0], d)
                                for g, own in zip(gathered, late)]
    (og,) = _tok_fwd(_f_gdn_out, [o, p_z], [], [(gnw, None)], [(d, MXU_DTYPE)], name="gdn_out", ts=2048, wb=HEAD, cols=heads)
    y_a = as_tok(_mm(as_mat(og), wgp, name="gdn_proj"))
    scp = _sc_fwd(p_sc, sc_w, "sc_conv")
    y_b = as_tok(_mm(as_mat(scp), wso, name="sc_out"))
    mcols = d // 512 if d % 512 == 0 else 1
    mwb = d // mcols
    merge_toks = [(p_g, 0), (p_g, mcols), y_a, y_b]
    (mrg,) = _tok_fwd(_f_merge, merge_toks, [], [], [(d, MXU_DTYPE)], name="merge", ts=1024, wb=mwb, cols=mcols)
    mix = as_tok(_mm(as_mat(mrg), wo, name="mix_out"))
    x2, h2 = _tok_fwd(_f_res_norm_mod, [x, mix], [g1, sh2, sc2], [n2w], [(d, F32), (d, MXU_DTYPE)], name="norm2", ts=512)
    gu = as_tok(_mm(as_mat(h2), wt_fi, tb=True, name="ffn_in"))
    fwb = _tile(dff, (256, 128))
    fcols = dff // fwb
    (act,) = _tok_fwd(_f_swiglu, [(gu, 0), (gu, fcols)], [], [], [(dff, MXU_DTYPE)], name="swiglu", ts=2048, wb=fwb, cols=fcols)
    ff = as_tok(_mm(as_mat(act), wfo, name="ffn_out"))

    loss_l, (dx2, dff_out, _), (dg2, dshf, dscf), (dnfw,) = _tok_bwd(
        _f_loss, [x2, ff, loss_target], [g2, shf, scf], [nfw], [], [True, True, False], name="loss", ts=512, loss=True,
        tok_dtype=[F32, MXU_DTYPE, None])
    dffm = as_mat(dff_out)
    dact = as_tok(_mm(dffm, wfo, tb=True, name="d_ffn_out"))
    gmm = functools.partial(_mm, ta=True, out_dtype=MXU_DTYPE)
    gw_ffn_out = gmm(as_mat(act), dffm, name="g_ffn_out")
    (dgu_a, dgu_b), _, _ = _tok_bwd(_f_swiglu, [(gu, 0), (gu, fcols)], [], [], [dact], [True, True], name="d_swiglu",
                                    ts=2048, wb=fwb, cols=fcols, tok_dtype=MXU_DTYPE)
    dh2 = _mm(as_mat(dgu_a), wt_fi, b_rows=(0, dff), name="d_ffn_in_a")
    dh2 = as_tok(_mm(as_mat(dgu_b), wt_fi, b_rows=(dff, dff), add=dh2, name="d_ffn_in_b"))
    h2m = as_mat(h2)
    gwt_ffn_in = gmm(as_mat(dgu_a), h2m, out_rows=2 * dff, name="g_ffn_in_a")
    gwt_ffn_in = gmm(as_mat(dgu_b), h2m, out_rows=2 * dff, row_off=dff, into=gwt_ffn_in, name="g_ffn_in_b")
    (dx_skip, dmix), (dg1, dsh2, dsc2), (dn2w,) = _tok_bwd(
        _f_res_norm_mod, [x, mix], [g1, sh2, sc2], [n2w], [dx2, dh2], [True, True], name="d_norm2", ts=256,
        tok_dtype=[F32, MXU_DTYPE])
    dmixm = as_mat(dmix)
    dmrg = as_tok(_mm(dmixm, wo, tb=True, name="d_mix_out"))
    gw_o = gmm(as_mat(mrg), dmixm, name="g_mix_out")
    (dga, dgb, dya, dyb), _, _ = _tok_bwd(_f_merge, merge_toks, [], [], [dmrg], [True] * 4, name="d_merge", ts=512,
                                          wb=mwb, cols=mcols, tok_dtype=MXU_DTYPE)
    dyam, dybm = as_mat(dya), as_mat(dyb)
    dog = as_tok(_mm(dyam, wgp, tb=True, name="d_gdn_proj"))
    gw_gdn_proj = gmm(as_mat(og), dyam, name="g_gdn_proj")
    dscp = as_tok(_mm(dybm, wso, tb=True, name="d_sc_out"))
    gw_sc_out = gmm(as_mat(scp), dybm, name="g_sc_out")
    dscb, dscc, dscx, g_sc_w = _sc_bwd(p_sc, sc_w, dscp, "d_sc_conv")
    (do, dz), _, (g_gnw,) = _tok_bwd(_f_gdn_out, [o, p_z], [], [(gnw, None)], [dog], [True, True], name="d_gdn_out",
                                     ts=2048, wb=HEAD, cols=heads, tok_dtype=[F32, MXU_DTYPE])
    ffn_parts, mix_parts = [(gwt_ffn_in, rows[3]), (gw_ffn_out, rows[4])], [(gw_gdn_proj, rows[0]), (gw_sc_out, rows[1]), (gw_o, rows[2])]
    own_rows = lambda parts: jnp.concatenate([lax.dynamic_slice_in_dim(g, dev * r, r, axis=0) for g, r in parts], axis=0)
    dqkv, dgbeta, ffn_recv = _gdn_bwd(qkv, gbeta, do, s_all, t_all, heads, "d_gdn", rider=_scatter_rider(ffn_parts))
    dp_qkv, g_conv_w, mix_recv = _qkv_bwd(p_qkv, conv_w, dqkv, heads, "d_qkv_conv", rider=_scatter_rider(mix_parts))
    ffn_red = _sum_direct(own_rows(ffn_parts), ffn_recv, "sum_ffn")
    mix_red = _sum_direct(own_rows(mix_parts), mix_recv, "sum_mix")
    (dp_ab,), _, (g_a_log, g_dt_bias) = _tok_bwd(f_gates, [p_ab], [], [a_log, dt_bias], [dgbeta], [True], name="d_gates",
                                                 ts=512, tok_dtype=MXU_DTYPE)
    sections = [(dp_qkv, s_qkv), (dz, s_z), (dp_ab, None), (dscb, (o_sc, d)), (dscc, (o_sc + d, d)), (dscx, (o_sc + 2 * d, d)),
                (dga, (o_ga, d)), (dgb, (o_gb, d))]
    dh1, gwt_in = None, []
    for k, (dp, sec) in enumerate(sections):
        dh1 = _mm(as_mat(dp), wt_ab if sec is None else wt_in, b_rows=sec, add=dh1, name=f"d_in_{k}")
        gwt_in.append(gmm(as_mat(dp), h1m, name=f"g_in_{k}"))
    gwt_in[2] = gwt_in[2][:2 * heads]
    gwt_in = jnp.concatenate(gwt_in, axis=0)
    (grad_x,), (dsh1, dsc1), (dn1w,) = _tok_bwd(_f_norm_mod_skip, [x], [sh1, sc1], [n1w], [as_tok(dh1), dx_skip], [True],
                                                name="d_norm1", ts=256)

    r_in = rows[5]
    win = -(-(r_in + max(r_in * k % ROW_ALIGN for k in range(NDEV))) // 128) * 128
    need_rows = max(_window_start(r_in, k) for k in range(NDEV)) + win
    gwt_in = jnp.pad(gwt_in, ((0, need_rows - gwt_in.shape[0]), (0, 0)))
    recv1 = _exchange_in_chip([(gwt_in, r_in, win, 0)], "scatter_in_chip")
    own = jnp.stack([lax.dynamic_slice_in_dim(gwt_in, _window_start(r_in, 2 * q + ac), win, axis=0) for q in range(4)])
    s1 = _sum_in_chip(own, recv1, "sum_in_chip")
    recv2 = _exchange_chips(s1, "scatter_chips")
    reduced = _sum_chips(s1, recv2, (2 * ax + ay).reshape(1).astype(jnp.int32), "sum_chips")
    gt_w_in = lax.dynamic_slice_in_dim(reduced, r_in * dev - _window_start(r_in, dev), r_in, axis=0)
    g_w_in = gt_w_in.T.reshape(w_in.shape)
    gt_w_ffn_in = ffn_red[:rows[3]]
    g_w_ffn_in = gt_w_ffn_in.T.reshape(w_ffn_in.shape)
    g_w_ffn_out = ffn_red[rows[3]:].reshape(w_ffn_out.shape)
    g_w_gdn_proj, g_w_sc_out, g_w_o = (mix_red[offs[i]:offs[i] + rows[i]].reshape(ref.shape)
                                       for i, ref in enumerate((w_gdn_proj, w_sc_out, w_o)))

    dmod = jnp.concatenate([t.reshape(bl, d) for t in (dsh1, dsc1, dg1, dsh2, dsc2, dg2)], axis=1)
    dmodf = jnp.concatenate([t.reshape(bl, d) for t in (dshf, dscf)], axis=1)
    summed_parts = [dn1w, dn2w, dnfw, g_gnw, g_a_log, g_dt_bias, g_conv_w, g_sc_w, loss_l]
    partial = _all_gather(_pack([dmod, dmodf] + summed_parts, LANE, 8, F32), name="gather_small", hbm=False)
    partial = partial.reshape(NDEV, -1)
    n_rows = bl * (6 * d + 2 * d)
    dmod_all, dmodf_all = _unpack(partial[:, :n_rows], [(bl, 6 * d), (bl, 2 * d)])
    dmod_all, dmodf_all = dmod_all.reshape(NDEV * bl, 6 * d), dmodf_all.reshape(NDEV * bl, 2 * d)
    totals = _row_sum(partial[:, n_rows:], "sum_small")
    t_n1w, t_n2w, t_nfw, t_gnw, t_a_log, t_dt_bias, t_conv_w, t_sc_w, t_loss = [
        t[0] for t in _unpack(totals, [p.shape for p in summed_parts])]
    my_cols = lambda a, n: lax.dynamic_slice_in_dim(a, dev * n, n, axis=1)
    grads = {
        "w_ada": _mm(c_act, my_cols(dmod_all, n_ada), ta=True, name="g_ada").reshape(w_ada.shape),
        "b_ada": _row_sum(dmod_all, "g_ada_bias").reshape(b_ada.shape),
        "norm1_w": t_n1w.reshape(norm1_w.shape),
        "w_in": g_w_in,
        "gdn_conv_w": my_cols(t_conv_w, gdn_conv_w.shape[-1]).reshape(gdn_conv_w.shape),
        "gdn_a_log": t_a_log[:, :heads].reshape(gdn_a_log.shape),
        "gdn_dt_bias": t_dt_bias[:, :heads].reshape(gdn_dt_bias.shape),
        "gdn_norm_w": t_gnw.reshape(gdn_norm_w.shape),
        "w_gdn_proj": g_w_gdn_proj,
        "sc_conv_w": my_cols(t_sc_w, sc_conv_w.shape[-1]).reshape(sc_conv_w.shape),
        "w_sc_out": g_w_sc_out,
        "w_o": g_w_o,
        "norm2_w": t_n2w.reshape(norm2_w.shape),
        "w_ffn_in": g_w_ffn_in,
        "w_ffn_out": g_w_ffn_out,
        "w_ada_f": _mm(c_act, my_cols(dmodf_all, n_adaf), ta=True, name="g_adaf").reshape(w_ada_f.shape),
        "b_ada_f": _row_sum(dmodf_all, "g_adaf_bias").reshape(b_ada_f.shape),
        "normf_w": t_nfw.reshape(normf_w.shape),
    }
    weights = dict(w_ada=w_ada, b_ada=b_ada, norm1_w=norm1_w, w_in=w_in, gdn_conv_w=gdn_conv_w, gdn_a_log=gdn_a_log,
                   gdn_dt_bias=gdn_dt_bias, gdn_norm_w=gdn_norm_w, w_gdn_proj=w_gdn_proj, sc_conv_w=sc_conv_w,
                   w_sc_out=w_sc_out, w_o=w_o, norm2_w=norm2_w, w_ffn_in=w_ffn_in, w_ffn_out=w_ffn_out, w_ada_f=w_ada_f,
                   b_ada_f=b_ada_f, normf_w=normf_w)
    m_in = [m_w_ada, m_b_ada, m_norm1_w, m_w_in, m_gdn_conv_w, m_gdn_a_log, m_gdn_dt_bias, m_gdn_norm_w, m_w_gdn_proj,
            m_sc_conv_w, m_w_sc_out, m_w_o, m_norm2_w, m_w_ffn_in, m_w_ffn_out, m_w_ada_f, m_b_ada_f, m_normf_w]
    v_in = [v_w_ada, v_b_ada, v_norm1_w, v_w_in, v_gdn_conv_w, v_gdn_a_log, v_gdn_dt_bias, v_gdn_norm_w, v_w_gdn_proj,
            v_sc_conv_w, v_w_sc_out, v_w_o, v_norm2_w, v_w_ffn_in, v_w_ffn_out, v_w_ada_f, v_b_ada_f, v_normf_w]
    deltas, new_m, new_v = [], [], []
    grads_t = {"w_in": gt_w_in, "w_ffn_in": gt_w_ffn_in}
    for (wname, wt), mt, vt in zip(weights.items(), m_in, v_in):
        if wname in grads_t:
            back = lambda a, wt=wt: a.T.reshape(wt.shape)
            dl, mn, vn = (back(a) for a in _adamw(wt[0].T, grads_t[wname], mt[0].T, vt[0].T, "adamw_" + wname))
        else:
            dl, mn, vn = _adamw(wt, grads[wname], mt, vt, "adamw_" + wname)
        deltas.append(dl)
        new_m.append(mn)
        new_v.append(vn)
    loss = t_loss[0, 0]
    return (loss, grad_x, *[grads[k] for k in weights], *deltas, *new_m, *new_v)
```

```python
import functools

import jax
import jax.numpy as jnp
from jax import lax
from jax.experimental import pallas as pl
from jax.experimental.pallas import tpu as pltpu

F32 = jnp.float32
MXU_DTYPE = jnp.bfloat16
NDEV = 8
CHUNK = 64
HEAD = 128
LANE = 128
EPS = 1e-6
ADAM_LR, ADAM_B1, ADAM_B2, ADAM_EPS, ADAM_WD, ADAM_STEP = 0.001, 0.9, 0.999, 1e-08, 0.01, 10
VMEM_LIMIT = 48 * 1024 * 1024
MESH_IDS = pl.DeviceIdType.MESH
HIGHEST = lax.Precision.HIGHEST


def _tile(n, cands=(512, 256, 128)):
    for c in cands:
        if n % c == 0:
            return c
    return n


def _cparams(*sem):
    return pltpu.CompilerParams(dimension_semantics=sem, vmem_limit_bytes=VMEM_LIMIT)


def _mm(a, b, *, ta=False, tb=False, add=None, out_dtype=F32, name, b_rows=None, out_rows=None, row_off=0, into=None):
    m, k = (a.shape[1], a.shape[0]) if ta else a.shape
    b_shape = b.shape if b_rows is None else (b_rows[1], b.shape[1])
    n = b_shape[0] if tb else b_shape[1]
    assert k == (b_shape[1] if tb else b_shape[0])
    if ta:
        tm, tn = _tile(m), n if n <= 1024 else _tile(n)
        tk = k if k <= 4096 else _tile(k, (4096, 2048, 1024, 512))
        if tm * tk > 1024 * 2048:
            tk = _tile(k, (2048, 1024, 512))
    else:
        tk = k if k <= 1024 else _tile(k, (1024, 512))
        tn = _tile(n, (1024 if tk <= 1024 else 512, 512, 256, 128))
        tm = _tile(m, (2048 if (tn <= 512 and tk <= 1024) else 1024, 1024, 512, 256, 128))
    nk = k // tk
    dims = (((0 if ta else 1,), (1 if tb else 0,)), ((), ()))
    has_add = add is not None

    def body(*refs):
        a_ref, b_ref = refs[0], refs[1]
        add_ref = refs[2] if has_add else None
        o_ref = refs[2 + has_add + (into is not None)]
        part = lax.dot_general(a_ref[...].astype(MXU_DTYPE), b_ref[...].astype(MXU_DTYPE), dims,
                               preferred_element_type=F32)

        def finish(acc):
            if has_add:
                acc = acc + add_ref[...]
            o_ref[...] = acc.astype(o_ref.dtype)

        if nk == 1:
            finish(part)
        else:
            acc_ref = refs[-1]
            kk = pl.program_id(2)

            @pl.when(kk == 0)
            def _():
                acc_ref[...] = part

            @pl.when(kk > 0)
            def _():
                acc_ref[...] += part

            @pl.when(kk == nk - 1)
            def _():
                finish(acc_ref[...])

    a_spec = pl.BlockSpec((tk, tm), lambda i, j, kk: (kk, i)) if ta else pl.BlockSpec((tm, tk), lambda i, j, kk: (i, kk))
    if b_rows is None:
        b_spec = pl.BlockSpec((tn, tk), lambda i, j, kk: (j, kk)) if tb else pl.BlockSpec((tk, tn), lambda i, j, kk: (kk, j))
    else:
        at = lambda t: pl.multiple_of(b_rows[0] + t, ROW_ALIGN)
        b_spec = (pl.BlockSpec((pl.Element(tn), pl.Element(tk)), lambda i, j, kk: (at(j * tn), kk * tk)) if tb else
                  pl.BlockSpec((pl.Element(tk), pl.Element(tn)), lambda i, j, kk: (at(kk * tk), j * tn)))
    add_spec = pl.BlockSpec((tm, tn), lambda i, j, kk: (i, j))
    assert row_off % tm == 0
    o_spec = pl.BlockSpec((tm, tn), lambda i, j, kk: (i + row_off // tm, j))
    in_specs = [a_spec, b_spec] + ([add_spec] if has_add else []) + ([pl.BlockSpec(memory_space=pl.ANY)] if into is not None else [])
    args = [a, b] + ([add] if has_add else []) + ([into] if into is not None else [])
    return pl.pallas_call(
        body, name=name, grid=(m // tm, n // tn, nk), in_specs=in_specs, out_specs=o_spec,
        out_shape=jax.ShapeDtypeStruct((out_rows or m, n), out_dtype),
        scratch_shapes=[pltpu.VMEM((tm, tn), F32)] if nk > 1 else [],
        input_output_aliases={len(args) - 1: 0} if into is not None else {},
        compiler_params=_cparams("parallel", "parallel", "arbitrary"),
    )(*args)


def _with_off(xs):
    return [x if isinstance(x, tuple) else (x, 0) for x in xs]


def _spec(kind, arr, off, ts, wb):
    w = arr.shape[-1] if wb is None else wb
    col = (lambda j: 0) if wb is None else functools.partial(lambda j, o: o + j, o=off)
    if kind == "tok":
        return pl.BlockSpec((None, ts, w), lambda j, b, i: (b, i, col(j)))
    if kind == "bat":
        return pl.BlockSpec((None, 1, w), lambda j, b, i: (b, 0, col(j)))
    if off is None:
        return pl.BlockSpec(arr.shape, lambda j, b, i: (0, 0))
    return pl.BlockSpec((arr.shape[0], w), lambda j, b, i: (0, col(j)))


def _in_specs(toks, bats, pars, cots, ts, wb):
    return ([_spec("tok", a, o, ts, wb) for a, o in toks] + [_spec("bat", a, o, ts, wb) for a, o in bats]
            + [_spec("par", a, o, ts, wb) for a, o in pars] + [_spec("tok", a, o, ts, wb) for a, o in cots])


def _tok_fwd(fn, toks, bats, pars, outs, *, name, ts, wb=None, cols=1):
    toks, bats, pars = _with_off(toks), _with_off(bats), _with_off(pars)
    bl, s, _ = toks[0][0].shape
    ts = min(ts, s)
    n_in = len(toks) + len(bats) + len(pars)

    def body(*refs):
        res = fn(*[r[...].astype(F32) for r in refs[:n_in]])
        for r, val in zip(refs[n_in:], res):
            r[...] = val.astype(r.dtype)

    out_specs = [pl.BlockSpec((None, ts, w if wb is None else wb), lambda j, b, i: (b, i, j)) for w, _ in outs]
    return pl.pallas_call(
        body, name=name, grid=(cols, bl, s // ts), in_specs=_in_specs(toks, bats, pars, [], ts, wb),
        out_specs=out_specs, out_shape=[jax.ShapeDtypeStruct((bl, s, w), dt) for w, dt in outs],
        compiler_params=_cparams("parallel", "parallel", "parallel"),
    )(*[a for a, _ in toks + bats + pars])


def _accumulate(ref, val, first):
    @pl.when(first)
    def _():
        ref[...] = val

    @pl.when(jnp.logical_not(first))
    def _():
        ref[...] += val


def _tok_bwd(fn, toks, bats, pars, cots, need, *, name, ts, wb=None, cols=1, tok_dtype=F32, loss=False):
    toks, bats, pars, cots = _with_off(toks), _with_off(bats), _with_off(pars), _with_off(cots)
    bl, s, _ = toks[0][0].shape
    ts = min(ts, s)
    nt, nb, npar, nc = len(toks), len(bats), len(pars), len(cots)
    n_in = nt + nb + npar

    def body(*refs):
        j, b, i = pl.program_id(0), pl.program_id(1), pl.program_id(2)
        outs, vjp = jax.vjp(fn, *[r[...].astype(F32) for r in refs[:n_in]])
        o = n_in + nc
        if loss:
            ct = (jnp.ones_like(outs[0]),)
            tot = jnp.broadcast_to(jnp.sum(outs[0], keepdims=True), (1, LANE))
            _accumulate(refs[o], tot, jnp.logical_and(b == 0, i == 0))
            o += 1
        else:
            ct = tuple(r[...].astype(F32) for r in refs[n_in:n_in + nc])
        grads = vjp(ct)
        for t in range(nt):
            if need[t]:
                refs[o][...] = grads[t].astype(refs[o].dtype)
                o += 1
        for t in range(nb):
            _accumulate(refs[o], grads[nt + t], i == 0)
            o += 1
        for t in range(npar):
            first = jnp.logical_and(b == 0, i == 0)
            if pars[t][1] is None:
                first = jnp.logical_and(first, j == 0)
            _accumulate(refs[o], grads[nt + nb + t], first)
            o += 1

    full = lambda arr: arr.shape[-1] if wb is None else wb * cols
    blk = lambda arr: arr.shape[-1] if wb is None else wb
    out_specs, out_shape = [], []
    if loss:
        out_specs.append(pl.BlockSpec((1, LANE), lambda j, b, i: (0, 0)))
        out_shape.append(jax.ShapeDtypeStruct((1, LANE), F32))
    for t in range(nt):
        if need[t]:
            out_specs.append(pl.BlockSpec((None, ts, blk(toks[t][0])), lambda j, b, i: (b, i, j)))
            dt = tok_dtype[t] if isinstance(tok_dtype, (list, tuple)) else tok_dtype
            out_shape.append(jax.ShapeDtypeStruct((bl, s, full(toks[t][0])), dt))
    for arr, _ in bats:
        out_specs.append(pl.BlockSpec((None, 1, blk(arr)), lambda j, b, i: (b, 0, j)))
        out_shape.append(jax.ShapeDtypeStruct((bl, 1, full(arr)), F32))
    for arr, off in pars:
        if off is None:
            out_specs.append(pl.BlockSpec(arr.shape, lambda j, b, i: (0, 0)))
            out_shape.append(jax.ShapeDtypeStruct(arr.shape, F32))
        else:
            out_specs.append(pl.BlockSpec((arr.shape[0], blk(arr)), lambda j, b, i: (0, j)))
            out_shape.append(jax.ShapeDtypeStruct((arr.shape[0], full(arr)), F32))
    res = list(pl.pallas_call(
        body, name=name, grid=(cols, bl, s // ts), in_specs=_in_specs(toks, bats, pars, cots, ts, wb),
        out_specs=out_specs, out_shape=out_shape, compiler_params=_cparams("arbitrary", "arbitrary", "arbitrary"),
    )(*[a for a, _ in toks + bats + pars + cots]))
    tot = res.pop(0) if loss else None
    dtoks = [res.pop(0) if need[t] else None for t in range(nt)]
    dbats = [res.pop(0) for _ in range(nb)]
    dpars = [res.pop(0) for _ in range(npar)]
    return (tot, dtoks, dbats, dpars) if loss else (dtoks, dbats, dpars)


def _silu(x):
    return x * jax.nn.sigmoid(x)


def _rms(x, w):
    return x * lax.rsqrt(jnp.mean(x * x, axis=-1, keepdims=True) + EPS) * w


def _f_norm_mod(x, shift, scale, w):
    return (_rms(x, w) * (1.0 + scale) + shift,)


def _f_norm_mod_skip(x, shift, scale, w):
    return _rms(x, w) * (1.0 + scale) + shift, x


def _f_res_norm_mod(x, mix, gate, shift, scale, w):
    x2 = x + gate * mix
    return x2, _rms(x2, w) * (1.0 + scale) + shift


def _f_gates(p, a_log, dt_bias, *, heads):
    z = p + dt_bias
    g = -jnp.exp(a_log) * (jnp.maximum(z, 0.0) + jnp.log1p(jnp.exp(jnp.minimum(z, -z))))
    lane = lax.broadcasted_iota(jnp.int32, p.shape, 1)
    return (jnp.where(lane < heads, g, jax.nn.sigmoid(p)),)


def _f_gdn_out(o, z, w):
    return (_rms(o, w) * _silu(z),)


def _f_merge(ga, gb, ya, yb):
    return (jax.nn.sigmoid(ga) * ya + jax.nn.sigmoid(gb) * yb,)


def _f_swiglu(a, b):
    return (_silu(a) * b,)


def _f_loss(x2, ff, tgt, gate, shift, scale, w):
    y = _rms(x2 + gate * ff, w) * (1.0 + scale) + shift
    return (0.5 * jnp.mean(jnp.square(y - tgt), axis=-1, keepdims=True),)


def _shift_down(x, s):
    if s == 0:
        return x
    row = lax.broadcasted_iota(jnp.int32, x.shape, 0)
    return jnp.where(row >= s, pltpu.roll(x, s, 0), 0.0)


def _shift_up(x, s):
    if s == 0:
        return x
    n = x.shape[0]
    row = lax.broadcasted_iota(jnp.int32, x.shape, 0)
    return jnp.where(row < n - s, pltpu.roll(x, n - s, 0), 0.0)


def _conv(x, w):
    width = w.shape[0]
    acc = w[width - 1:width, :] * x
    for j in range(width - 1):
        acc = acc + w[j:j + 1, :] * _shift_down(x, width - 1 - j)
    return acc


def _conv_bwd(dy, x, w, dw_ref, first):
    width = w.shape[0]
    dx = w[width - 1:width, :] * dy
    for j in range(width - 1):
        dx = dx + w[j:j + 1, :] * _shift_up(dy, width - 1 - j)
    for j in range(width):
        row = jnp.sum(dy * _shift_down(x, width - 1 - j), axis=0, keepdims=True)
        _accumulate(dw_ref.at[j:j + 1, :], row, first)
    return dx


def _qkv_act(xc, is_v, scale):
    a = _silu(xc)
    nrm = a * lax.rsqrt(jnp.sum(a * a, axis=-1, keepdims=True) + EPS) * scale
    return jnp.where(is_v, a, nrm)


def _qkv_consts(j, heads):
    is_v = j >= 2 * heads
    scale = jnp.where(j < heads, HEAD ** -0.5, 1.0).astype(F32)
    return is_v, scale


def _qkv_fwd(p, w, heads, name, rider=None):
    bl, s, w3 = p.shape
    r_args, r_in, r_out, r_shape, r_sems = _ride_specs(rider)

    def body(*refs):
        (p_ref, w_ref, o_ref), hooks = _split_refs(refs, 2, 1, 0, rider)
        first, mid, last = _grid_marks(w3 // HEAD, bl)
        _hooks_before(hooks, first, mid)
        is_v, scale = _qkv_consts(pl.program_id(0), heads)
        o_ref[...] = _qkv_act(_conv(p_ref[...], w_ref[...]), is_v, scale)
        _hooks_after(hooks, last)

    blk = pl.BlockSpec((None, s, HEAD), lambda j, b: (b, 0, j))
    return pl.pallas_call(
        body, name=name, grid=(w3 // HEAD, bl), in_specs=[blk, pl.BlockSpec((w.shape[0], HEAD), lambda j, b: (0, j))] + r_in,
        out_specs=[blk] + r_out, out_shape=[jax.ShapeDtypeStruct(p.shape, F32)] + r_shape, scratch_shapes=r_sems,
        compiler_params=_cparams("arbitrary", "arbitrary"),
    )(p, w, *r_args)


def _qkv_bwd(p, w, dout, heads, name, rider=None):
    bl, s, w3 = p.shape
    r_args, r_in, r_out, r_shape, r_sems = _ride_specs(rider)

    def body(*refs):
        (p_ref, w_ref, d_ref, dp_ref, dw_ref), hooks = _split_refs(refs, 3, 2, 0, rider)
        first, mid, last = _grid_marks(w3 // HEAD, bl)
        _hooks_before(hooks, first, mid)
        is_v, scale = _qkv_consts(pl.program_id(0), heads)
        x, wv = p_ref[...], w_ref[...]
        _, vjp = jax.vjp(lambda xc: _qkv_act(xc, is_v, scale), _conv(x, wv))
        (dxc,) = vjp(d_ref[...])
        dp_ref[...] = _conv_bwd(dxc, x, wv, dw_ref, pl.program_id(1) == 0).astype(dp_ref.dtype)
        _hooks_after(hooks, last)

    blk = pl.BlockSpec((None, s, HEAD), lambda j, b: (b, 0, j))
    wblk = pl.BlockSpec((w.shape[0], HEAD), lambda j, b: (0, j))
    return pl.pallas_call(
        body, name=name, grid=(w3 // HEAD, bl), in_specs=[blk, wblk, blk] + r_in, out_specs=[blk, wblk] + r_out,
        out_shape=[jax.ShapeDtypeStruct(p.shape, MXU_DTYPE), jax.ShapeDtypeStruct(w.shape, F32)] + r_shape,
        scratch_shapes=r_sems, compiler_params=_cparams("arbitrary", "arbitrary"),
    )(p, w, dout, *r_args)


def _sc_specs(p, w):
    bl, s, w3 = p.shape
    nblk = w3 // 3 // LANE
    sec = lambda k: pl.BlockSpec((None, s, LANE), functools.partial(lambda j, b, k: (b, 0, k * nblk + j), k=k))
    return nblk, [sec(0), sec(1), sec(2)], pl.BlockSpec((w.shape[0], LANE), lambda j, b: (0, j)), \
        pl.BlockSpec((None, s, LANE), lambda j, b: (b, 0, j))


def _sc_fwd(p, w, name):
    bl, s, w3 = p.shape
    nblk, secs, wblk, oblk = _sc_specs(p, w)

    def body(b_ref, c_ref, x_ref, w_ref, o_ref):
        o_ref[...] = (b_ref[...] * _conv(c_ref[...] * x_ref[...], w_ref[...])).astype(o_ref.dtype)

    return pl.pallas_call(
        body, name=name, grid=(nblk, bl), in_specs=secs + [wblk], out_specs=oblk,
        out_shape=jax.ShapeDtypeStruct((bl, s, w3 // 3), MXU_DTYPE), compiler_params=_cparams("parallel", "parallel"),
    )(p, p, p, w)


def _sc_bwd(p, w, dout, name):
    bl, s, w3 = p.shape
    nblk, secs, wblk, oblk = _sc_specs(p, w)

    def body(b_ref, c_ref, x_ref, w_ref, d_ref, db_ref, dc_ref, dx_ref, dw_ref):
        gb, gc, xin, wv, d = b_ref[...], c_ref[...], x_ref[...], w_ref[...], d_ref[...]
        u = gc * xin
        db_ref[...] = (d * _conv(u, wv)).astype(db_ref.dtype)
        du = _conv_bwd(d * gb, u, wv, dw_ref, pl.program_id(1) == 0)
        dc_ref[...] = (du * xin).astype(dc_ref.dtype)
        dx_ref[...] = (du * gc).astype(dx_ref.dtype)

    act = jax.ShapeDtypeStruct((bl, s, w3 // 3), MXU_DTYPE)
    return pl.pallas_call(
        body, name=name, grid=(nblk, bl), in_specs=secs + [wblk, oblk], out_specs=[oblk, oblk, oblk, wblk],
        out_shape=[act, act, act, jax.ShapeDtypeStruct(w.shape, F32)], compiler_params=_cparams("arbitrary", "arbitrary"),
    )(p, p, p, w, dout)


def _bdot(a, b, ca, cb):
    return lax.dot_general(a.astype(MXU_DTYPE), b.astype(MXU_DTYPE), (((ca,), (cb,)), ((), ())),
                           preferred_element_type=F32)


def _hdot(a, b):
    return lax.dot_general(a, b, (((1,), (0,)), ((), ())), precision=HIGHEST, preferred_element_type=F32)


def _lane_col(x, idx):
    lane = lax.broadcasted_iota(jnp.int32, x.shape, 1)
    return jnp.sum(jnp.where(lane == idx, x, 0.0), axis=1, keepdims=True)


def _chunk_masks():
    r = lax.broadcasted_iota(jnp.int32, (CHUNK, CHUNK), 0)
    c = lax.broadcasted_iota(jnp.int32, (CHUNK, CHUNK), 1)
    return r == c, r >= c, r > c


def _dot3(a, b):
    ah, bh = a.astype(MXU_DTYPE), b.astype(MXU_DTYPE)
    al, bl = (a - ah.astype(F32)).astype(MXU_DTYPE), (b - bh.astype(F32)).astype(MXU_DTYPE)
    dot = lambda x, y: lax.dot_general(x, y, (((1,), (0,)), ((), ())), preferred_element_type=F32)
    return dot(ah, bh) + (dot(ah, bl) + dot(al, bh))


def _tri_inv_steps(low, eye):
    x = -low
    p = jnp.where(eye, 1.0, 0.0) + x
    span = 2
    while span < CHUNK:
        x = _dot3(x, x)
        yield
        p = p + _dot3(p, x)
        yield
        span *= 2
    return p


def _round_robin(gens):
    out, live = [None] * len(gens), list(range(len(gens)))
    while live:
        still = []
        for i in live:
            try:
                next(gens[i])
                still.append(i)
            except StopIteration as stop:
                out[i] = stop.value
        live = still
    return out


def _gdn_pre(q, k, v, gc, beta, masks):
    eye, causal, strict = masks
    gc_row = jnp.sum(jnp.where(eye, gc, 0.0), axis=0, keepdims=True)
    decay = jnp.where(causal, jnp.exp(jnp.where(causal, gc - gc_row, 0.0)), 0.0)
    eg = jnp.exp(gc)
    gl = gc[CHUNK - 1:CHUNK, :]
    kb, vb = k * beta, v * beta
    low = jnp.where(strict, _bdot(kb, k, 1, 1) * decay, 0.0)
    qk = jnp.where(causal, _bdot(q, k, 1, 1) * decay, 0.0)
    rest = jnp.exp(gl - gc)
    return dict(decay=decay, eg=eg, gl=gl, kb=kb, vb=vb, kbe=kb * eg, low=low, qk=qk, qg=q * eg, rest=rest, kdec=k * rest)


def _gdn_specs(qkv, gbeta, heads, rev):
    bl, s, w3 = qkv.shape
    d, n = w3 // 3, s // CHUNK
    at = (lambda c: n - 1 - c) if rev else (lambda c: c)
    assert d == heads * HEAD
    sec = pl.BlockSpec((None, CHUNK, w3), lambda b, c: (b, at(c), 0))
    gspec = pl.BlockSpec((None, CHUNK, LANE), lambda b, c: (b, at(c), 0))
    sspec = pl.BlockSpec((None, None, heads, HEAD, HEAD), lambda b, c: (b, at(c), 0, 0, 0))
    tspec = pl.BlockSpec((None, None, heads, CHUNK, CHUNK), lambda b, c: (b, at(c), 0, 0, 0))
    return bl, s, d, n, sec, gspec, sspec, tspec


def _grid_marks(bl, n):
    b, c = pl.program_id(0), pl.program_id(1)
    late = min(bl * n - 1, bl * n * 5 // 6)
    first = jnp.logical_and(b == 0, c == 0)
    mid = jnp.logical_and(b == late // n, c == late % n)
    return first, mid, jnp.logical_and(b == bl - 1, c == n - 1)


def _gdn_fwd(qkv, gbeta, heads, name, rider=None):
    bl, s, d, n, sec, gspec, sspec, tspec = _gdn_specs(qkv, gbeta, heads, False)
    r_args, r_in, r_out, r_shape, r_sems = _ride_specs(rider)

    def body(*refs):
        (x_ref, g_ref, o_ref, s_ref, t_ref, st_ref), hooks = _split_refs(refs, 2, 3, 1, rider)
        first, mid, last = _grid_marks(bl, n)
        _hooks_before(hooks, first, mid)

        @pl.when(pl.program_id(1) == 0)
        def _():
            st_ref[...] = jnp.zeros_like(st_ref)

        masks = _chunk_masks()
        eye, causal, _ = masks
        gblk = g_ref[...]
        gc_all = _hdot(jnp.where(causal, 1.0, 0.0), gblk)
        st_all = st_ref[...]

        def head(h):
            st = st_all[h]
            q, k, v = (x_ref[:, sec * d + h * HEAD:sec * d + (h + 1) * HEAD] for sec in range(3))
            pre = _gdn_pre(q, k, v, _lane_col(gc_all, h), _lane_col(gblk, heads + h), masks)
            yield
            t = yield from _tri_inv_steps(pre["low"], eye)
            u, w = _bdot(t, pre["vb"], 1, 0), _bdot(t, pre["kbe"], 1, 0)
            yield
            vnew = u - _bdot(w, st, 1, 0)
            yield
            out = _bdot(pre["qg"], st, 1, 0) + _bdot(pre["qk"], vnew, 1, 0)
            return out, t, st * jnp.exp(pre["gl"]) + _bdot(pre["kdec"], vnew, 0, 0)

        outs, ts, states = zip(*_round_robin([head(h) for h in range(heads)]))
        o_ref[...] = jnp.concatenate(outs, axis=1)
        s_ref[...] = st_all
        t_ref[...] = jnp.stack(ts)
        st_ref[...] = jnp.stack(states)
        _hooks_after(hooks, last)

    return pl.pallas_call(
        body, name=name, grid=(bl, n), in_specs=[sec, gspec] + r_in,
        out_specs=[pl.BlockSpec((None, CHUNK, d), lambda b, c: (b, c, 0)), sspec, tspec] + r_out,
        out_shape=[jax.ShapeDtypeStruct((bl, s, d), F32), jax.ShapeDtypeStruct((bl, n, heads, HEAD, HEAD), F32),
                   jax.ShapeDtypeStruct((bl, n, heads, CHUNK, CHUNK), F32)] + r_shape,
        scratch_shapes=[pltpu.VMEM((heads, HEAD, HEAD), F32)] + r_sems, compiler_params=_cparams("arbitrary", "arbitrary"),
    )(qkv, gbeta, *r_args)


def _gdn_bwd(qkv, gbeta, dout, s_all, t_all, heads, name, rider=None):
    bl, s, d, n, sec, gspec, sspec, tspec = _gdn_specs(qkv, gbeta, heads, True)
    ospec = pl.BlockSpec((None, CHUNK, d), lambda b, c: (b, n - 1 - c, 0))
    r_args, r_in, r_out, r_shape, r_sems = _ride_specs(rider)

    def body(*refs):
        (x_ref, g_ref, do_ref, s_ref, t_ref, dx_ref, dg_ref, ds_ref), hooks = _split_refs(refs, 5, 2, 1, rider)
        first, mid, last = _grid_marks(bl, n)
        _hooks_before(hooks, first, mid)

        @pl.when(pl.program_id(1) == 0)
        def _():
            ds_ref[...] = jnp.zeros_like(ds_ref)

        masks = _chunk_masks()
        eye, causal, strict = masks
        gblk = g_ref[...]
        gc_all = _hdot(jnp.where(causal, 1.0, 0.0), gblk)
        lane = lax.broadcasted_iota(jnp.int32, gblk.shape, 1)
        last_row = lax.broadcasted_iota(jnp.int32, (CHUNK, 1), 0) == CHUNK - 1
        rowsum = lambda a: jnp.sum(a, axis=1, keepdims=True)
        st_all, t_all_, ds_all = s_ref[...], t_ref[...], ds_ref[...]

        def head(h):
            sl = slice(h * HEAD, (h + 1) * HEAD)
            q, k, v = (x_ref[:, sec * d + h * HEAD:sec * d + (h + 1) * HEAD] for sec in range(3))
            do = do_ref[:, sl]
            beta = _lane_col(gblk, heads + h)
            st, t, dsn = st_all[h], t_all_[h], ds_all[h]
            pre = _gdn_pre(q, k, v, _lane_col(gc_all, h), beta, masks)
            decay, eg, kb, vb, kbe, low, qk, qg, kdec = (pre[x] for x in ("decay", "eg", "kb", "vb", "kbe", "low", "qk", "qg", "kdec"))
            egl = jnp.exp(pre["gl"])
            yield
            u, w = _bdot(t, vb, 1, 0), _bdot(t, kbe, 1, 0)
            yield
            vnew = u - _bdot(w, st, 1, 0)
            yield
            dkdec = _bdot(vnew, dsn, 1, 1)
            dvnew = _bdot(kdec, dsn, 1, 0) + _bdot(qk, do, 0, 0)
            dgl = jnp.sum(dsn * st, keepdims=True) * egl
            dqg = _bdot(do, st, 1, 1)
            dqk = jnp.where(causal, _bdot(do, vnew, 1, 1), 0.0)
            yield
            dw = -_bdot(dvnew, st, 1, 1)
            ds_new = dsn * egl + _bdot(qg, do, 0, 0) - _bdot(w, dvnew, 0, 0)
            yield
            dt = _bdot(dvnew, vb, 1, 1) + _bdot(dw, kbe, 1, 1)
            dvb, dkbe = _bdot(t, dvnew, 0, 0), _bdot(t, dw, 0, 0)
            yield
            inner = _bdot(dt, t, 1, 1)
            yield
            dlow = -jnp.where(strict, _bdot(t, inner, 0, 0), 0.0)
            da, db = dlow * decay, dqk * decay
            yield
            m = dlow * low + dqk * qk
            kdk = dkdec * kdec
            col_of_m = jnp.sum(jnp.where(eye, jnp.sum(m, axis=0, keepdims=True), 0.0), axis=1, keepdims=True)
            dgc = rowsum(m) - col_of_m + rowsum(dqg * qg) + rowsum(dkbe * kbe) - rowsum(kdk)
            dgc = dgc + jnp.where(last_row, dgl + jnp.sum(kdk, keepdims=True), 0.0)
            dkb = _bdot(da, k, 1, 0) + dkbe * eg
            yield
            dk = _bdot(da, kb, 0, 0) + _bdot(db, q, 0, 0) + dkdec * pre["rest"] + dkb * beta
            dq = _bdot(db, k, 1, 0) + dqg * eg
            dbeta = rowsum(dkb * k) + rowsum(dvb * v)
            return dq, dk, dvb * beta, jnp.where(lane == h, dgc, 0.0) + jnp.where(lane == heads + h, dbeta, 0.0), ds_new

        dqs, dks, dvs, dgs, dss = zip(*_round_robin([head(h) for h in range(heads)]))
        dx_ref[...] = jnp.concatenate(dqs + dks + dvs, axis=1)
        ds_ref[...] = jnp.stack(dss)
        dgb = dgs[0]
        for extra in dgs[1:]:
            dgb = dgb + extra
        upper = jnp.where(jnp.logical_or(eye, jnp.logical_not(causal)), 1.0, 0.0)
        dg_ref[...] = jnp.where(lane < heads, _hdot(upper, dgb), dgb)
        _hooks_after(hooks, last)

    return pl.pallas_call(
        body, name=name, grid=(bl, n), in_specs=[sec, gspec, ospec, sspec, tspec] + r_in, out_specs=[sec, gspec] + r_out,
        out_shape=[jax.ShapeDtypeStruct(qkv.shape, F32), jax.ShapeDtypeStruct((bl, s, LANE), F32)] + r_shape,
        scratch_shapes=[pltpu.VMEM((heads, HEAD, HEAD), F32)] + r_sems, compiler_params=_cparams("arbitrary", "arbitrary"),
    )(qkv, gbeta, dout, s_all, t_all, *r_args)


def _position():
    return lax.axis_index("x"), lax.axis_index("y"), lax.axis_index("c")


def _all_gather(x, *, name, hbm):
    space = pltpu.HBM if hbm else pltpu.VMEM

    def body(x_ref, out_ref, send_sems, recv_sems, local_sem):
        ax, ay, ac = _position()
        me, sibling = (ax, ay, ac), (ax, ay, 1 - ac)
        chips = [(1 - ax, ay), (ax, 1 - ay), (1 - ax, 1 - ay)]

        def slot(px, py, pc):
            return out_ref.at[4 * px + 2 * py + pc]

        def copy(k, block, to, src=None):
            return pltpu.make_async_remote_copy(
                src_ref=slot(*block) if src is None else src, dst_ref=slot(*block), send_sem=send_sems.at[k],
                recv_sem=recv_sems.at[k], device_id=to, device_id_type=MESH_IDS)

        mine = pltpu.make_async_copy(x_ref, slot(*me), local_sem)
        mine.start()
        first = [copy(0, me, sibling, src=x_ref)] + [copy(1 + j, me, (*chip, ac), src=x_ref) for j, chip in enumerate(chips)]
        for cp in first:
            cp.start()
        passed = [copy(4 + j, (*chip, ac), sibling) for j, chip in enumerate(chips)]
        for j, chip in enumerate(chips):
            copy(1 + j, (*chip, ac), me).wait_recv()
            passed[j].start()
        copy(0, sibling, me).wait_recv()
        for j, chip in enumerate(chips):
            copy(4 + j, (*chip, 1 - ac), me).wait_recv()
        for cp in first + passed:
            cp.wait_send()
        mine.wait()

    return pl.pallas_call(
        body, name=name, out_shape=jax.ShapeDtypeStruct((NDEV,) + x.shape, x.dtype),
        in_specs=[pl.BlockSpec(memory_space=space)], out_specs=pl.BlockSpec(memory_space=space),
        scratch_shapes=[pltpu.SemaphoreType.DMA((7,)), pltpu.SemaphoreType.DMA((7,)), pltpu.SemaphoreType.DMA],
    )(x)


class _Rider:
    def __init__(self, arrays, out_shapes, sems, hooks):
        self.arrays, self.out_shapes, self.sems, self.hooks = arrays, out_shapes, sems, hooks


def _split_refs(refs, n_in, n_out, n_scratch, rider):
    r_in = len(rider.arrays) if rider else 0
    r_out = len(rider.out_shapes) if rider else 0
    o = n_in + r_in
    o2 = o + n_out + r_out
    host = refs[:n_in] + refs[o:o + n_out] + refs[o2:o2 + n_scratch]
    if rider is None:
        return host, None
    return host, rider.hooks(refs[n_in:o], refs[o + n_out:o2], *refs[o2 + n_scratch:])


def _hooks_before(hooks, first, mid):
    if hooks is not None:
        pl.when(first)(hooks[0])
        pl.when(mid)(hooks[1])


def _hooks_after(hooks, last):
    if hooks is not None:
        pl.when(last)(hooks[2])


def _ride_specs(rider):
    hbm = pl.BlockSpec(memory_space=pltpu.HBM)
    if rider is None:
        return [], [], [], [], []
    n_out = len(rider.out_shapes)
    return list(rider.arrays), [hbm] * len(rider.arrays), [hbm] * n_out, list(rider.out_shapes), list(rider.sems)


def _gather_rider(xs):
    n = len(xs)

    def hooks(x_refs, out_refs, send_sems, recv_sems):
        ax, ay, ac = _position()
        me, sibling = (ax, ay, ac), (ax, ay, 1 - ac)
        chips = [(1 - ax, ay), (ax, 1 - ay), (1 - ax, 1 - ay)]

        def copies(k, block, to, own=False):
            out = []
            for i in range(n):
                slot = out_refs[i].at[4 * block[0] + 2 * block[1] + block[2]]
                out.append(pltpu.make_async_remote_copy(
                    src_ref=x_refs[i] if own else slot, dst_ref=slot, send_sem=send_sems.at[k, i], recv_sem=recv_sems.at[k, i],
                    device_id=to, device_id_type=MESH_IDS))
            return out

        def first():
            for cp in copies(0, me, sibling, own=True):
                cp.start()
            for j, chip in enumerate(chips):
                for cp in copies(1 + j, me, (*chip, ac), own=True):
                    cp.start()

        def mid():
            for j, chip in enumerate(chips):
                for arrived, onward in zip(copies(1 + j, (*chip, ac), me), copies(4 + j, (*chip, ac), sibling)):
                    arrived.wait_recv()
                    onward.start()

        def last():
            for cp in copies(0, sibling, me):
                cp.wait_recv()
            for j, chip in enumerate(chips):
                for cp in copies(4 + j, (*chip, 1 - ac), me):
                    cp.wait_recv()
            for cp in copies(0, me, sibling, own=True):
                cp.wait_send()
            for j, chip in enumerate(chips):
                for cp in copies(1 + j, me, (*chip, ac), own=True) + copies(4 + j, (*chip, ac), sibling):
                    cp.wait_send()

        return first, mid, last

    return _Rider(list(xs), [jax.ShapeDtypeStruct((NDEV,) + x.shape, x.dtype) for x in xs],
                  [pltpu.SemaphoreType.DMA((7, n)), pltpu.SemaphoreType.DMA((7, n))], hooks)


def _scatter_rider(parts):
    packed = sum(r for _, r in parts)
    width, dtype = parts[0][0].shape[1], parts[0][0].dtype

    def hooks(g_refs, out_refs, send_sems, recv_sems):
        (recv_ref,) = out_refs
        ax, ay, ac = _position()

        def peer(rel):
            flip = lambda a, bit: 1 - a if rel & bit else a
            return flip(ax, 4), flip(ay, 2), flip(ac, 1)

        def first():
            for rel in range(1, NDEV):
                px, py, pc = peer(rel)
                off = 0
                for g_ref, (_, r) in zip(g_refs, parts):
                    rows = g_ref.at[pl.ds(pl.multiple_of((4 * px + 2 * py + pc) * r, ROW_ALIGN), r)]
                    pltpu.make_async_remote_copy(
                        src_ref=rows, dst_ref=recv_ref.at[rel - 1, pl.ds(off, r)], send_sem=send_sems.at[rel - 1],
                        recv_sem=recv_sems.at[rel - 1], device_id=(px, py, pc), device_id_type=MESH_IDS).start()
                    off += r

        def last():
            for rel in range(1, NDEV):
                slot = recv_ref.at[rel - 1]
                pltpu.make_async_remote_copy(src_ref=slot, dst_ref=slot, send_sem=send_sems.at[rel - 1],
                                             recv_sem=recv_sems.at[rel - 1], device_id=peer(rel), device_id_type=MESH_IDS).wait()

        return first, lambda: None, last

    return _Rider([g for g, _ in parts], [jax.ShapeDtypeStruct((NDEV - 1, packed, width), dtype)],
                  [pltpu.SemaphoreType.DMA((NDEV - 1,)), pltpu.SemaphoreType.DMA((NDEV - 1,))], hooks)


def _sum_direct(own, recv, name):
    r, w = own.shape
    tr = max(t for t in range(ROW_ALIGN, 257, ROW_ALIGN) if r % t == 0)

    def body(own_ref, *refs):
        acc = own_ref[...].astype(F32)
        for ref in refs[:-1]:
            acc = acc + ref[...].astype(F32)
        refs[-1][...] = acc

    rblk = lambda k: pl.BlockSpec((None, tr, w), functools.partial(lambda i, k: (k, i, 0), k=k))
    blk = pl.BlockSpec((tr, w), lambda i: (i, 0))
    return pl.pallas_call(body, name=name, grid=(r // tr,), in_specs=[blk] + [rblk(k) for k in range(NDEV - 1)],
                          out_specs=blk, out_shape=jax.ShapeDtypeStruct((r, w), F32),
                          compiler_params=_cparams("parallel"))(own, *([recv] * (NDEV - 1)))


ROW_ALIGN = 16


def _window_start(rows_per_dev, k):
    return rows_per_dev * k // ROW_ALIGN * ROW_ALIGN


def _exchange_in_chip(parts, name):
    n = len(parts)
    packed = sum(win for _, _, win, _ in parts)
    width, dtype = parts[0][0].shape[1], parts[0][0].dtype

    def body(*refs):
        g_refs, recv_ref, send_sems, recv_sems = refs[:n], *refs[n:]
        ax, ay, ac = _position()
        sibling = (ax, ay, 1 - ac)
        for q in range(4):
            for g_ref, (_, r, win, off) in zip(g_refs, parts):
                there = g_ref.at[pl.ds(pl.multiple_of(_window_start(r, 2 * q + 1 - ac), ROW_ALIGN), win)]
                pltpu.make_async_remote_copy(src_ref=there, dst_ref=recv_ref.at[q, pl.ds(off, win)], send_sem=send_sems.at[q],
                                             recv_sem=recv_sems.at[q], device_id=sibling, device_id_type=MESH_IDS).start()
        for q in range(4):
            pltpu.make_async_remote_copy(src_ref=recv_ref.at[q], dst_ref=recv_ref.at[q], send_sem=send_sems.at[q],
                                         recv_sem=recv_sems.at[q], device_id=sibling, device_id_type=MESH_IDS).wait()

    hbm = pl.BlockSpec(memory_space=pltpu.HBM)
    return pl.pallas_call(
        body, name=name, out_shape=jax.ShapeDtypeStruct((4, packed, width), dtype), in_specs=[hbm] * n, out_specs=hbm,
        scratch_shapes=[pltpu.SemaphoreType.DMA((4,)), pltpu.SemaphoreType.DMA((4,))],
    )(*[g for g, _, _, _ in parts])


def _exchange_chips(s1, name):
    def body(s_ref, recv_ref, send_sems, recv_sems):
        ax, ay, ac = _position()
        chips = [(1 - ax, ay), (ax, 1 - ay), (1 - ax, 1 - ay)]
        copies = [pltpu.make_async_remote_copy(
            src_ref=s_ref.at[2 * cx + cy], dst_ref=recv_ref.at[r], send_sem=send_sems.at[r], recv_sem=recv_sems.at[r],
            device_id=(cx, cy, ac), device_id_type=MESH_IDS) for r, (cx, cy) in enumerate(chips)]
        for cp in copies:
            cp.start()
        for cp in copies:
            cp.wait_recv()
        for cp in copies:
            cp.wait_send()

    hbm = pl.BlockSpec(memory_space=pltpu.HBM)
    return pl.pallas_call(
        body, name=name, out_shape=jax.ShapeDtypeStruct((3,) + s1.shape[1:], s1.dtype), in_specs=[hbm], out_specs=hbm,
        scratch_shapes=[pltpu.SemaphoreType.DMA((3,)), pltpu.SemaphoreType.DMA((3,))],
    )(s1)


def _sum_in_chip(own, recv, name):
    _, r, w = own.shape
    tr = _tile(r, (256, 128))

    def body(a_ref, b_ref, o_ref):
        o_ref[...] = (a_ref[...].astype(F32) + b_ref[...].astype(F32)).astype(o_ref.dtype)

    blk = pl.BlockSpec((None, tr, w), lambda q, i: (q, i, 0))
    return pl.pallas_call(body, name=name, grid=(4, r // tr), in_specs=[blk, blk], out_specs=blk,
                          out_shape=jax.ShapeDtypeStruct(own.shape, own.dtype),
                          compiler_params=_cparams("parallel", "parallel"))(own, recv)


def _sum_chips(s1, recv, chip, name):
    _, r, w = s1.shape
    tr = _tile(r, (256, 128))

    def body(c_ref, s_ref, r0_ref, r1_ref, r2_ref, o_ref):
        f = lambda ref: ref[...].astype(F32)
        o_ref[...] = ((f(s_ref) + f(r0_ref)) + f(r1_ref)) + f(r2_ref)

    rblk = lambda k: pl.BlockSpec((None, tr, w), functools.partial(lambda i, c, k: (k, i, 0), k=k))
    grid_spec = pltpu.PrefetchScalarGridSpec(
        num_scalar_prefetch=1, grid=(r // tr,),
        in_specs=[pl.BlockSpec((None, tr, w), lambda i, c: (c[0], i, 0)), rblk(0), rblk(1), rblk(2)],
        out_specs=pl.BlockSpec((tr, w), lambda i, c: (i, 0)))
    return pl.pallas_call(body, name=name, grid_spec=grid_spec, out_shape=jax.ShapeDtypeStruct((r, w), F32),
                          compiler_params=_cparams("parallel"))(chip, s1, recv, recv, recv)


def _silu_rows(x, name):
    def body(x_ref, o_ref):
        o_ref[...] = _silu(x_ref[...])

    return pl.pallas_call(body, name=name, out_shape=jax.ShapeDtypeStruct(x.shape, F32))(x)


def _row_sum(x, name):
    def body(x_ref, o_ref):
        acc = x_ref[0:1, :]
        for i in range(1, x.shape[0]):
            acc = acc + x_ref[i:i + 1, :]
        o_ref[...] = acc

    return pl.pallas_call(body, name=name, out_shape=jax.ShapeDtypeStruct((1, x.shape[1]), F32))(x)


def _adamw(w, g, m, v, name):
    cols = w.shape[-1]
    rows = w.size // cols
    tr = _tile(rows, (128,))
    tc = LANE if (tr == rows and rows > 512 and cols % LANE == 0) else cols

    def body(w_ref, g_ref, m_ref, v_ref, d_ref, mo_ref, vo_ref):
        grad = g_ref[...]
        m_new = ADAM_B1 * m_ref[...] + (1.0 - ADAM_B1) * grad
        v_new = ADAM_B2 * v_ref[...] + (1.0 - ADAM_B2) * jnp.square(grad)
        m_hat = m_new / (1.0 - ADAM_B1 ** ADAM_STEP)
        v_hat = v_new / (1.0 - ADAM_B2 ** ADAM_STEP)
        d_ref[...] = -ADAM_LR * (m_hat / (jnp.sqrt(v_hat) + ADAM_EPS) + ADAM_WD * w_ref[...])
        mo_ref[...] = m_new
        vo_ref[...] = v_new

    blk = pl.BlockSpec((tr, tc), lambda i, j: (i, j))
    out = pl.pallas_call(
        body, name=name, grid=(rows // tr, cols // tc), in_specs=[blk] * 4, out_specs=[blk] * 3,
        out_shape=[jax.ShapeDtypeStruct((rows, cols), F32)] * 3, compiler_params=_cparams("parallel", "parallel"),
    )(*[t.reshape(rows, cols) for t in (w, g, m, v)])
    return [t.reshape(w.shape) for t in out]


def _pack(parts, width, row_mult, dtype):
    flat = jnp.concatenate([p.reshape(-1).astype(dtype) for p in parts])
    rows = -(-flat.shape[0] // (width * row_mult)) * row_mult
    return jnp.pad(flat, (0, rows * width - flat.shape[0])).reshape(rows, width)


def _unpack(flat, shapes):
    out, off = [], 0
    for shp in shapes:
        size = 1
        for dim in shp:
            size *= dim
        out.append(flat[:, off:off + size].reshape((flat.shape[0],) + tuple(shp)))
        off += size
    return out


def _devices_to_cols(a):
    _, r, c = a.shape
    return a.transpose(1, 0, 2).reshape(r, NDEV * c)


def kernel(x, c, w_ada, b_ada, norm1_w, w_in, gdn_conv_w, gdn_a_log, gdn_dt_bias, gdn_norm_w, w_gdn_proj, sc_conv_w, w_sc_out, w_o, norm2_w, w_ffn_in, w_ffn_out, w_ada_f, b_ada_f, normf_w, loss_target, m_w_ada, m_b_ada, m_norm1_w, m_w_in, m_gdn_conv_w, m_gdn_a_log, m_gdn_dt_bias, m_gdn_norm_w, m_w_gdn_proj, m_sc_conv_w, m_w_sc_out, m_w_o, m_norm2_w, m_w_ffn_in, m_w_ffn_out, m_w_ada_f, m_b_ada_f, m_normf_w, v_w_ada, v_b_ada, v_norm1_w, v_w_in, v_gdn_conv_w, v_gdn_a_log, v_gdn_dt_bias, v_gdn_norm_w, v_w_gdn_proj, v_sc_conv_w, v_w_sc_out, v_w_o, v_norm2_w, v_w_ffn_in, v_w_ffn_out, v_w_ada_f, v_b_ada_f, v_normf_w):
    bl, s, d = x.shape
    heads = gdn_a_log.shape[-1]
    dff = w_ffn_out.shape[1] * NDEV
    tok = bl * s
    ax, ay, ac = _position()
    dev = 4 * ax + 2 * ay + ac
    as_tok = lambda a: a.reshape(bl, s, a.shape[-1])
    as_mat = lambda a: a.reshape(tok, a.shape[-1])

    small = _all_gather(_pack([c, gdn_conv_w, sc_conv_w], LANE, 8, F32), name="gather_cond", hbm=False)
    c_all, conv_w, sc_w = _unpack(small.reshape(NDEV, -1), [(bl, d), gdn_conv_w.shape[1:], sc_conv_w.shape[1:]])
    c_act = _silu_rows(c_all.reshape(NDEV * bl, d), "cond_silu")
    conv_w, sc_w = _devices_to_cols(conv_w), _devices_to_cols(sc_w)
    n_ada, n_adaf = w_ada.shape[-1], w_ada_f.shape[-1]
    bias = jnp.broadcast_to(lax.dynamic_slice_in_dim(b_ada, dev * n_ada, n_ada, axis=1), (NDEV * bl, n_ada))
    biasf = jnp.broadcast_to(lax.dynamic_slice_in_dim(b_ada_f.reshape(1, -1), dev * n_adaf, n_adaf, axis=1), (NDEV * bl, n_adaf))
    mod_cols = _mm(c_act, w_ada[0], add=bias, name="ada_cols")
    modf_cols = _mm(c_act, w_ada_f, add=biasf, name="adaf_cols")
    mods = _all_gather(jnp.concatenate([mod_cols, modf_cols], axis=1), name="gather_mod", hbm=False)
    mod_all = mods[:, :, :n_ada].transpose(1, 0, 2).reshape(NDEV * bl, NDEV * n_ada)
    modf_all = mods[:, :, n_ada:].transpose(1, 0, 2).reshape(NDEV * bl, NDEV * n_adaf)
    my_rows = lambda a: lax.dynamic_slice_in_dim(a, dev * bl, bl, axis=0)
    sh1, sc1, g1, sh2, sc2, g2 = [t.reshape(bl, 1, d) for t in jnp.split(my_rows(mod_all), 6, axis=1)]
    shf, scf = [t.reshape(bl, 1, d) for t in jnp.split(my_rows(modf_all), 2, axis=1)]

    late = [t.astype(MXU_DTYPE) for t in (w_gdn_proj[0], w_sc_out[0], w_o[0], w_ffn_in[0].T, w_ffn_out[0])]
    rows = [t.shape[0] for t in late] + [w_in.shape[-1]]
    offs = [sum(rows[:i]) for i in range(5)]
    in_rows = -(-rows[5] // ROW_ALIGN) * ROW_ALIGN
    in_send = jnp.pad(w_in[0].T.astype(MXU_DTYPE), ((0, in_rows - rows[5]), (0, 0)))
    wt_in = _all_gather(in_send, name="gather_w_in", hbm=True)[:, :rows[5], :].reshape(NDEV * rows[5], d)
    o_z, o_ab, o_sc, o_ga, o_gb = 3 * d, 4 * d, 4 * d + 2 * heads, 7 * d + 2 * heads, 8 * d + 2 * heads
    s_qkv, s_z, s_sc, s_gate = (0, o_z), (o_z, d), (o_sc, 3 * d), (o_ga, 2 * d)
    wt_ab = jnp.pad(wt_in[o_ab:o_sc], ((0, LANE - 2 * heads), (0, 0)))

    n1w, n2w, nfw = norm1_w.reshape(1, d), norm2_w.reshape(1, d), normf_w.reshape(1, d)
    lanes = lambda a: jnp.pad(a.reshape(1, -1), ((0, 0), (0, LANE - a.size)))
    a_log, dt_bias, gnw = lanes(gdn_a_log), lanes(gdn_dt_bias), gdn_norm_w.reshape(1, HEAD)
    f_gates = functools.partial(_f_gates, heads=heads)
    (h1,) = _tok_fwd(_f_norm_mod, [x], [sh1, sc1], [n1w], [(d, MXU_DTYPE)], name="norm1", ts=512)
    h1m = as_mat(h1)
    p_qkv = as_tok(_mm(h1m, wt_in, tb=True, b_rows=s_qkv, name="in_qkv"))
    p_z = as_tok(_mm(h1m, wt_in, tb=True, b_rows=s_z, name="in_z"))
    p_ab = as_tok(_mm(h1m, wt_ab, tb=True, name="in_ab"))
    p_sc = as_tok(_mm(h1m, wt_in, tb=True, b_rows=s_sc, name="in_sc"))
    p_g = as_tok(_mm(h1m, wt_in, tb=True, b_rows=s_gate, name="in_gate"))
    qkv, *gathered = _qkv_fwd(p_qkv, conv_w, heads, "qkv_conv", rider=_gather_rider(late[:3]))
    (gbeta,) = _tok_fwd(f_gates, [p_ab], [], [a_log, dt_bias], [(LANE, F32)], name="gates", ts=512)
    o, s_all, t_all, *gathered_ffn = _gdn_fwd(qkv, gbeta, heads, "gdn", rider=_gather_rider(late[3:]))
    gathered += gathered_ffn
    wgp, wso, wo, wt_fi, wfo = [lax.dynamic_update_slice_in_dim(g, own[None], dev, axis=0).reshape(NDEV * own.shape[0], d)
                                for g, own in zip(gathered, late)]
    (og,) = _tok_fwd(_f_gdn_out, [o, p_z], [], [(gnw, None)], [(d, MXU_DTYPE)], name="gdn_out", ts=2048, wb=HEAD, cols=heads)
    y_a = as_tok(_mm(as_mat(og), wgp, name="gdn_proj"))
    scp = _sc_fwd(p_sc, sc_w, "sc_conv")
    y_b = as_tok(_mm(as_mat(scp), wso, name="sc_out"))
    mcols = d // 512 if d % 512 == 0 else 1
    mwb = d // mcols
    merge_toks = [(p_g, 0), (p_g, mcols), y_a, y_b]
    (mrg,) = _tok_fwd(_f_merge, merge_toks, [], [], [(d, MXU_DTYPE)], name="merge", ts=1024, wb=mwb, cols=mcols)
    mix = as_tok(_mm(as_mat(mrg), wo, name="mix_out"))
    x2, h2 = _tok_fwd(_f_res_norm_mod, [x, mix], [g1, sh2, sc2], [n2w], [(d, F32), (d, MXU_DTYPE)], name="norm2", ts=512)
    gu = as_tok(_mm(as_mat(h2), wt_fi, tb=True, name="ffn_in"))
    fwb = _tile(dff, (256, 128))
    fcols = dff // fwb
    (act,) = _tok_fwd(_f_swiglu, [(gu, 0), (gu, fcols)], [], [], [(dff, MXU_DTYPE)], name="swiglu", ts=2048, wb=fwb, cols=fcols)
    ff = as_tok(_mm(as_mat(act), wfo, name="ffn_out"))

    loss_l, (dx2, dff_out, _), (dg2, dshf, dscf), (dnfw,) = _tok_bwd(
        _f_loss, [x2, ff, loss_target], [g2, shf, scf], [nfw], [], [True, True, False], name="loss", ts=512, loss=True,
        tok_dtype=[F32, MXU_DTYPE, None])
    dffm = as_mat(dff_out)
    dact = as_tok(_mm(dffm, wfo, tb=True, name="d_ffn_out"))
    gmm = functools.partial(_mm, ta=True, out_dtype=MXU_DTYPE)
    gw_ffn_out = gmm(as_mat(act), dffm, name="g_ffn_out")
    (dgu_a, dgu_b), _, _ = _tok_bwd(_f_swiglu, [(gu, 0), (gu, fcols)], [], [], [dact], [True, True], name="d_swiglu",
                                    ts=2048, wb=fwb, cols=fcols, tok_dtype=MXU_DTYPE)
    dh2 = _mm(as_mat(dgu_a), wt_fi, b_rows=(0, dff), name="d_ffn_in_a")
    dh2 = as_tok(_mm(as_mat(dgu_b), wt_fi, b_rows=(dff, dff), add=dh2, name="d_ffn_in_b"))
    h2m = as_mat(h2)
    gwt_ffn_in = gmm(as_mat(dgu_a), h2m, out_rows=2 * dff, name="g_ffn_in_a")
    gwt_ffn_in = gmm(as_mat(dgu_b), h2m, out_rows=2 * dff, row_off=dff, into=gwt_ffn_in, name="g_ffn_in_b")
    (dx_skip, dmix), (dg1, dsh2, dsc2), (dn2w,) = _tok_bwd(
        _f_res_norm_mod, [x, mix], [g1, sh2, sc2], [n2w], [dx2, dh2], [True, True], name="d_norm2", ts=256,
        tok_dtype=[F32, MXU_DTYPE])
    dmixm = as_mat(dmix)
    dmrg = as_tok(_mm(dmixm, wo, tb=True, name="d_mix_out"))
    gw_o = gmm(as_mat(mrg), dmixm, name="g_mix_out")
    (dga, dgb, dya, dyb), _, _ = _tok_bwd(_f_merge, merge_toks, [], [], [dmrg], [True] * 4, name="d_merge", ts=512,
                                          wb=mwb, cols=mcols, tok_dtype=MXU_DTYPE)
    dyam, dybm = as_mat(dya), as_mat(dyb)
    dog = as_tok(_mm(dyam, wgp, tb=True, name="d_gdn_proj"))
    gw_gdn_proj = gmm(as_mat(og), dyam, name="g_gdn_proj")
    dscp = as_tok(_mm(dybm, wso, tb=True, name="d_sc_out"))
    gw_sc_out = gmm(as_mat(scp), dybm, name="g_sc_out")
    dscb, dscc, dscx, g_sc_w = _sc_bwd(p_sc, sc_w, dscp, "d_sc_conv")
    (do, dz), _, (g_gnw,) = _tok_bwd(_f_gdn_out, [o, p_z], [], [(gnw, None)], [dog], [True, True], name="d_gdn_out",
                                     ts=2048, wb=HEAD, cols=heads, tok_dtype=[F32, MXU_DTYPE])
    ffn_parts, mix_parts = [(gwt_ffn_in, rows[3]), (gw_ffn_out, rows[4])], [(gw_gdn_proj, rows[0]), (gw_sc_out, rows[1]), (gw_o, rows[2])]
    own_rows = lambda parts: jnp.concatenate([lax.dynamic_slice_in_dim(g, dev * r, r, axis=0) for g, r in parts], axis=0)
    dqkv, dgbeta, ffn_recv = _gdn_bwd(qkv, gbeta, do, s_all, t_all, heads, "d_gdn", rider=_scatter_rider(ffn_parts))
    dp_qkv, g_conv_w, mix_recv = _qkv_bwd(p_qkv, conv_w, dqkv, heads, "d_qkv_conv", rider=_scatter_rider(mix_parts))
    ffn_red = _sum_direct(own_rows(ffn_parts), ffn_recv, "sum_ffn")
    mix_red = _sum_direct(own_rows(mix_parts), mix_recv, "sum_mix")
    (dp_ab,), _, (g_a_log, g_dt_bias) = _tok_bwd(f_gates, [p_ab], [], [a_log, dt_bias], [dgbeta], [True], name="d_gates",
                                                 ts=512, tok_dtype=MXU_DTYPE)
    sections = [(dp_qkv, s_qkv), (dz, s_z), (dp_ab, None), (dscb, (o_sc, d)), (dscc, (o_sc + d, d)), (dscx, (o_sc + 2 * d, d)),
                (dga, (o_ga, d)), (dgb, (o_gb, d))]
    dh1, gwt_in = None, []
    for k, (dp, sec) in enumerate(sections):
        dh1 = _mm(as_mat(dp), wt_ab if sec is None else wt_in, b_rows=sec, add=dh1, name=f"d_in_{k}")
        gwt_in.append(gmm(as_mat(dp), h1m, name=f"g_in_{k}"))
    gwt_in[2] = gwt_in[2][:2 * heads]
    gwt_in = jnp.concatenate(gwt_in, axis=0)
    (grad_x,), (dsh1, dsc1), (dn1w,) = _tok_bwd(_f_norm_mod_skip, [x], [sh1, sc1], [n1w], [as_tok(dh1), dx_skip], [True],
                                                name="d_norm1", ts=256)

    r_in = rows[5]
    win = -(-(r_in + max(r_in * k % ROW_ALIGN for k in range(NDEV))) // 128) * 128
    need_rows = max(_window_start(r_in, k) for k in range(NDEV)) + win
    gwt_in = jnp.pad(gwt_in, ((0, need_rows - gwt_in.shape[0]), (0, 0)))
    recv1 = _exchange_in_chip([(gwt_in, r_in, win, 0)], "scatter_in_chip")
    own = jnp.stack([lax.dynamic_slice_in_dim(gwt_in, _window_start(r_in, 2 * q + ac), win, axis=0) for q in range(4)])
    s1 = _sum_in_chip(own, recv1, "sum_in_chip")
    recv2 = _exchange_chips(s1, "scatter_chips")
    reduced = _sum_chips(s1, recv2, (2 * ax + ay).reshape(1).astype(jnp.int32), "sum_chips")
    gt_w_in = lax.dynamic_slice_in_dim(reduced, r_in * dev - _window_start(r_in, dev), r_in, axis=0)
    g_w_in = gt_w_in.T.reshape(w_in.shape)
    gt_w_ffn_in = ffn_red[:rows[3]]
    g_w_ffn_in = gt_w_ffn_in.T.reshape(w_ffn_in.shape)
    g_w_ffn_out = ffn_red[rows[3]:].reshape(w_ffn_out.shape)
    g_w_gdn_proj, g_w_sc_out, g_w_o = (mix_red[offs[i]:offs[i] + rows[i]].reshape(ref.shape)
                                       for i, ref in enumerate((w_gdn_proj, w_sc_out, w_o)))

    dmod = jnp.concatenate([t.reshape(bl, d) for t in (dsh1, dsc1, dg1, dsh2, dsc2, dg2)], axis=1)
    dmodf = jnp.concatenate([t.reshape(bl, d) for t in (dshf, dscf)], axis=1)
    summed_parts = [dn1w, dn2w, dnfw, g_gnw, g_a_log, g_dt_bias, g_conv_w, g_sc_w, loss_l]
    partial = _all_gather(_pack([dmod, dmodf] + summed_parts, LANE, 8, F32), name="gather_small", hbm=False)
    partial = partial.reshape(NDEV, -1)
    n_rows = bl * (6 * d + 2 * d)
    dmod_all, dmodf_all = _unpack(partial[:, :n_rows], [(bl, 6 * d), (bl, 2 * d)])
    dmod_all, dmodf_all = dmod_all.reshape(NDEV * bl, 6 * d), dmodf_all.reshape(NDEV * bl, 2 * d)
    totals = _row_sum(partial[:, n_rows:], "sum_small")
    t_n1w, t_n2w, t_nfw, t_gnw, t_a_log, t_dt_bias, t_conv_w, t_sc_w, t_loss = [
        t[0] for t in _unpack(totals, [p.shape for p in summed_parts])]
    my_cols = lambda a, n: lax.dynamic_slice_in_dim(a, dev * n, n, axis=1)
    grads = {
        "w_ada": _mm(c_act, my_cols(dmod_all, n_ada), ta=True, name="g_ada").reshape(w_ada.shape),
        "b_ada": _row_sum(dmod_all, "g_ada_bias").reshape(b_ada.shape),
        "norm1_w": t_n1w.reshape(norm1_w.shape),
        "w_in": g_w_in,
        "gdn_conv_w": my_cols(t_conv_w, gdn_conv_w.shape[-1]).reshape(gdn_conv_w.shape),
        "gdn_a_log": t_a_log[:, :heads].reshape(gdn_a_log.shape),
        "gdn_dt_bias": t_dt_bias[:, :heads].reshape(gdn_dt_bias.shape),
        "gdn_norm_w": t_gnw.reshape(gdn_norm_w.shape),
        "w_gdn_proj": g_w_gdn_proj,
        "sc_conv_w": my_cols(t_sc_w, sc_conv_w.shape[-1]).reshape(sc_conv_w.shape),
        "w_sc_out": g_w_sc_out,
        "w_o": g_w_o,
        "norm2_w": t_n2w.reshape(norm2_w.shape),
        "w_ffn_in": g_w_ffn_in,
        "w_ffn_out": g_w_ffn_out,
        "w_ada_f": _mm(c_act, my_cols(dmodf_all, n_adaf), ta=True, name="g_adaf").reshape(w_ada_f.shape),
        "b_ada_f": _row_sum(dmodf_all, "g_adaf_bias").reshape(b_ada_f.shape),
        "normf_w": t_nfw.reshape(normf_w.shape),
    }
    weights = dict(w_ada=w_ada, b_ada=b_ada, norm1_w=norm1_w, w_in=w_in, gdn_conv_w=gdn_conv_w, gdn_a_log=gdn_a_log,
                   gdn_dt_bias=gdn_dt_bias, gdn_norm_w=gdn_norm_w, w_gdn_proj=w_gdn_proj, sc_conv_w=sc_conv_w,
                   w_sc_out=w_sc_out, w_o=w_o, norm2_w=norm2_w, w_ffn_in=w_ffn_in, w_ffn_out=w_ffn_out, w_ada_f=w_ada_f,
                   b_ada_f=b_ada_f, normf_w=normf_w)
    m_in = [m_w_ada, m_b_ada, m_norm1_w, m_w_in, m_gdn_conv_w, m_gdn_a_log, m_gdn_dt_bias, m_gdn_norm_w, m_w_gdn_proj,
            m_sc_conv_w, m_w_sc_out, m_w_o, m_norm2_w, m_w_ffn_in, m_w_ffn_out, m_w_ada_f, m_b_ada_f, m_normf_w]
    v_in = [v_w_ada, v_b_ada, v_norm1_w, v_w_in, v_gdn_conv_w, v_gdn_a_log, v_gdn_dt_bias, v_gdn_norm_w, v_w_gdn_proj,
            v_sc_conv_w, v_w_sc_out, v_w_o, v_norm2_w, v_w_ffn_in, v_w_ffn_out, v_w_ada_f, v_b_ada_f, v_normf_w]
    deltas, new_m, new_v = [], [], []
    grads_t = {"w_in": gt_w_in, "w_ffn_in": gt_w_ffn_in}
    for (wname, wt), mt, vt in zip(weights.items(), m_in, v_in):
        if wname in grads_t:
            back = lambda a, wt=wt: a.T.reshape(wt.shape)
            dl, mn, vn = (back(a) for a in _adamw(wt[0].T, grads_t[wname], mt[0].T, vt[0].T, "adamw_" + wname))
        else:
            dl, mn, vn = _adamw(wt, grads[wname], mt, vt, "adamw_" + wname)
        deltas.append(dl)
        new_m.append(mn)
        new_v.append(vn)
    loss = t_loss[0, 0]
    return (loss, grad_x, *[grads[k] for k in weights], *deltas, *new_m, *new_v)
```

```python
import functools

import jax
import jax.numpy as jnp
from jax import lax
from jax.experimental import pallas as pl
from jax.experimental.pallas import tpu as pltpu
from jax.experimental.pallas import tpu_sc as plsc

F32 = jnp.float32
MXU_DTYPE = jnp.bfloat16
NDEV = 8
CHUNK = 64
HEAD = 128
LANE = 128
EPS = 1e-6
ADAM_LR, ADAM_B1, ADAM_B2, ADAM_EPS, ADAM_WD, ADAM_STEP = 0.001, 0.9, 0.999, 1e-08, 0.01, 10
VMEM_LIMIT = 48 * 1024 * 1024
MESH_IDS = pl.DeviceIdType.MESH
HIGHEST = lax.Precision.HIGHEST


def _tile(n, cands=(512, 256, 128)):
    for c in cands:
        if n % c == 0:
            return c
    return n


def _cparams(*sem):
    return pltpu.CompilerParams(dimension_semantics=sem, vmem_limit_bytes=VMEM_LIMIT)


def _mm(a, b, *, ta=False, tb=False, add=None, out_dtype=F32, name, b_rows=None, out_rows=None, row_off=0, into=None):
    m, k = (a.shape[1], a.shape[0]) if ta else a.shape
    b_shape = b.shape if b_rows is None else (b_rows[1], b.shape[1])
    n = b_shape[0] if tb else b_shape[1]
    assert k == (b_shape[1] if tb else b_shape[0])
    if ta:
        tm, tn = _tile(m), n if n <= 1024 else _tile(n)
        tk = k if k <= 4096 else _tile(k, (4096, 2048, 1024, 512))
        if tm * tk > 1024 * 2048:
            tk = _tile(k, (2048, 1024, 512))
    else:
        tk = k if k <= 1024 else _tile(k, (1024, 512))
        tn = _tile(n, (1024 if tk <= 1024 else 512, 512, 256, 128))
        tm = _tile(m, (2048 if (tn <= 512 and tk <= 1024) else 1024, 1024, 512, 256, 128))
    nk = k // tk
    dims = (((0 if ta else 1,), (1 if tb else 0,)), ((), ()))
    has_add = add is not None

    def body(*refs):
        a_ref, b_ref = refs[0], refs[1]
        add_ref = refs[2] if has_add else None
        o_ref = refs[2 + has_add + (into is not None)]
        part = lax.dot_general(a_ref[...].astype(MXU_DTYPE), b_ref[...].astype(MXU_DTYPE), dims,
                               preferred_element_type=F32)

        def finish(acc):
            if has_add:
                acc = acc + add_ref[...]
            o_ref[...] = acc.astype(o_ref.dtype)

        if nk == 1:
            finish(part)
        else:
            acc_ref = refs[-1]
            kk = pl.program_id(2)

            @pl.when(kk == 0)
            def _():
                acc_ref[...] = part

            @pl.when(kk > 0)
            def _():
                acc_ref[...] += part

            @pl.when(kk == nk - 1)
            def _():
                finish(acc_ref[...])

    a_spec = pl.BlockSpec((tk, tm), lambda i, j, kk: (kk, i)) if ta else pl.BlockSpec((tm, tk), lambda i, j, kk: (i, kk))
    if b_rows is None:
        b_spec = pl.BlockSpec((tn, tk), lambda i, j, kk: (j, kk)) if tb else pl.BlockSpec((tk, tn), lambda i, j, kk: (kk, j))
    else:
        at = lambda t: pl.multiple_of(b_rows[0] + t, ROW_ALIGN)
        b_spec = (pl.BlockSpec((pl.Element(tn), pl.Element(tk)), lambda i, j, kk: (at(j * tn), kk * tk)) if tb else
                  pl.BlockSpec((pl.Element(tk), pl.Element(tn)), lambda i, j, kk: (at(kk * tk), j * tn)))
    add_spec = pl.BlockSpec((tm, tn), lambda i, j, kk: (i, j))
    assert row_off % tm == 0
    o_spec = pl.BlockSpec((tm, tn), lambda i, j, kk: (i + row_off // tm, j))
    in_specs = [a_spec, b_spec] + ([add_spec] if has_add else []) + ([pl.BlockSpec(memory_space=pl.ANY)] if into is not None else [])
    args = [a, b] + ([add] if has_add else []) + ([into] if into is not None else [])
    return pl.pallas_call(
        body, name=name, grid=(m // tm, n // tn, nk), in_specs=in_specs, out_specs=o_spec,
        out_shape=jax.ShapeDtypeStruct((out_rows or m, n), out_dtype),
        scratch_shapes=[pltpu.VMEM((tm, tn), F32)] if nk > 1 else [],
        input_output_aliases={len(args) - 1: 0} if into is not None else {},
        compiler_params=_cparams("parallel", "parallel", "arbitrary"),
    )(*args)


def _with_off(xs):
    return [x if isinstance(x, tuple) else (x, 0) for x in xs]


def _spec(kind, arr, off, ts, wb):
    w = arr.shape[-1] if wb is None else wb
    col = (lambda j: 0) if wb is None else functools.partial(lambda j, o: o + j, o=off)
    if kind == "tok":
        return pl.BlockSpec((None, ts, w), lambda j, b, i: (b, i, col(j)))
    if kind == "bat":
        return pl.BlockSpec((None, 1, w), lambda j, b, i: (b, 0, col(j)))
    if off is None:
        return pl.BlockSpec(arr.shape, lambda j, b, i: (0, 0))
    return pl.BlockSpec((arr.shape[0], w), lambda j, b, i: (0, col(j)))


def _in_specs(toks, bats, pars, cots, ts, wb):
    return ([_spec("tok", a, o, ts, wb) for a, o in toks] + [_spec("bat", a, o, ts, wb) for a, o in bats]
            + [_spec("par", a, o, ts, wb) for a, o in pars] + [_spec("tok", a, o, ts, wb) for a, o in cots])


def _tok_fwd(fn, toks, bats, pars, outs, *, name, ts, wb=None, cols=1):
    toks, bats, pars = _with_off(toks), _with_off(bats), _with_off(pars)
    bl, s, _ = toks[0][0].shape
    ts = min(ts, s)
    n_in = len(toks) + len(bats) + len(pars)

    def body(*refs):
        res = fn(*[r[...].astype(F32) for r in refs[:n_in]])
        for r, val in zip(refs[n_in:], res):
            r[...] = val.astype(r.dtype)

    out_specs = [pl.BlockSpec((None, ts, w if wb is None else wb), lambda j, b, i: (b, i, j)) for w, _ in outs]
    return pl.pallas_call(
        body, name=name, grid=(cols, bl, s // ts), in_specs=_in_specs(toks, bats, pars, [], ts, wb),
        out_specs=out_specs, out_shape=[jax.ShapeDtypeStruct((bl, s, w), dt) for w, dt in outs],
        compiler_params=_cparams("parallel", "parallel", "parallel"),
    )(*[a for a, _ in toks + bats + pars])


def _accumulate(ref, val, first):
    @pl.when(first)
    def _():
        ref[...] = val

    @pl.when(jnp.logical_not(first))
    def _():
        ref[...] += val


def _tok_bwd(fn, toks, bats, pars, cots, need, *, name, ts, wb=None, cols=1, tok_dtype=F32, loss=False):
    toks, bats, pars, cots = _with_off(toks), _with_off(bats), _with_off(pars), _with_off(cots)
    bl, s, _ = toks[0][0].shape
    ts = min(ts, s)
    nt, nb, npar, nc = len(toks), len(bats), len(pars), len(cots)
    n_in = nt + nb + npar

    def body(*refs):
        j, b, i = pl.program_id(0), pl.program_id(1), pl.program_id(2)
        outs, vjp = jax.vjp(fn, *[r[...].astype(F32) for r in refs[:n_in]])
        o = n_in + nc
        if loss:
            ct = (jnp.ones_like(outs[0]),)
            tot = jnp.broadcast_to(jnp.sum(outs[0], keepdims=True), (1, LANE))
            _accumulate(refs[o], tot, jnp.logical_and(b == 0, i == 0))
            o += 1
        else:
            ct = tuple(r[...].astype(F32) for r in refs[n_in:n_in + nc])
        grads = vjp(ct)
        for t in range(nt):
            if need[t]:
                refs[o][...] = grads[t].astype(refs[o].dtype)
                o += 1
        for t in range(nb):
            _accumulate(refs[o], grads[nt + t], i == 0)
            o += 1
        for t in range(npar):
            first = jnp.logical_and(b == 0, i == 0)
            if pars[t][1] is None:
                first = jnp.logical_and(first, j == 0)
            _accumulate(refs[o], grads[nt + nb + t], first)
            o += 1

    full = lambda arr: arr.shape[-1] if wb is None else wb * cols
    blk = lambda arr: arr.shape[-1] if wb is None else wb
    out_specs, out_shape = [], []
    if loss:
        out_specs.append(pl.BlockSpec((1, LANE), lambda j, b, i: (0, 0)))
        out_shape.append(jax.ShapeDtypeStruct((1, LANE), F32))
    for t in range(nt):
        if need[t]:
            out_specs.append(pl.BlockSpec((None, ts, blk(toks[t][0])), lambda j, b, i: (b, i, j)))
            dt = tok_dtype[t] if isinstance(tok_dtype, (list, tuple)) else tok_dtype
            out_shape.append(jax.ShapeDtypeStruct((bl, s, full(toks[t][0])), dt))
    for arr, _ in bats:
        out_specs.append(pl.BlockSpec((None, 1, blk(arr)), lambda j, b, i: (b, 0, j)))
        out_shape.append(jax.ShapeDtypeStruct((bl, 1, full(arr)), F32))
    for arr, off in pars:
        if off is None:
            out_specs.append(pl.BlockSpec(arr.shape, lambda j, b, i: (0, 0)))
            out_shape.append(jax.ShapeDtypeStruct(arr.shape, F32))
        else:
            out_specs.append(pl.BlockSpec((arr.shape[0], blk(arr)), lambda j, b, i: (0, j)))
            out_shape.append(jax.ShapeDtypeStruct((arr.shape[0], full(arr)), F32))
    res = list(pl.pallas_call(
        body, name=name, grid=(cols, bl, s // ts), in_specs=_in_specs(toks, bats, pars, cots, ts, wb),
        out_specs=out_specs, out_shape=out_shape, compiler_params=_cparams("arbitrary", "arbitrary", "arbitrary"),
    )(*[a for a, _ in toks + bats + pars + cots]))
    tot = res.pop(0) if loss else None
    dtoks = [res.pop(0) if need[t] else None for t in range(nt)]
    dbats = [res.pop(0) for _ in range(nb)]
    dpars = [res.pop(0) for _ in range(npar)]
    return (tot, dtoks, dbats, dpars) if loss else (dtoks, dbats, dpars)


def _silu(x):
    return x * jax.nn.sigmoid(x)


def _rms(x, w):
    return x * lax.rsqrt(jnp.mean(x * x, axis=-1, keepdims=True) + EPS) * w


def _f_norm_mod(x, shift, scale, w):
    return (_rms(x, w) * (1.0 + scale) + shift,)


def _f_norm_mod_skip(x, shift, scale, w):
    return _rms(x, w) * (1.0 + scale) + shift, x


def _f_res_norm_mod(x, mix, gate, shift, scale, w):
    x2 = x + gate * mix
    return x2, _rms(x2, w) * (1.0 + scale) + shift


def _f_gates(p, a_log, dt_bias, *, heads):
    z = p + dt_bias
    g = -jnp.exp(a_log) * (jnp.maximum(z, 0.0) + jnp.log1p(jnp.exp(jnp.minimum(z, -z))))
    lane = lax.broadcasted_iota(jnp.int32, p.shape, 1)
    return (jnp.where(lane < heads, g, jax.nn.sigmoid(p)),)


def _f_gdn_out(o, z, w):
    return (_rms(o, w) * _silu(z),)


def _f_merge(ga, gb, ya, yb):
    return (jax.nn.sigmoid(ga) * ya + jax.nn.sigmoid(gb) * yb,)


def _f_swiglu(a, b):
    return (_silu(a) * b,)


def _f_loss(x2, ff, tgt, gate, shift, scale, w):
    y = _rms(x2 + gate * ff, w) * (1.0 + scale) + shift
    return (0.5 * jnp.mean(jnp.square(y - tgt), axis=-1, keepdims=True),)


def _shift_down(x, s):
    if s == 0:
        return x
    row = lax.broadcasted_iota(jnp.int32, x.shape, 0)
    return jnp.where(row >= s, pltpu.roll(x, s, 0), 0.0)


def _shift_up(x, s):
    if s == 0:
        return x
    n = x.shape[0]
    row = lax.broadcasted_iota(jnp.int32, x.shape, 0)
    return jnp.where(row < n - s, pltpu.roll(x, n - s, 0), 0.0)


def _conv(x, w):
    width = w.shape[0]
    acc = w[width - 1:width, :] * x
    for j in range(width - 1):
        acc = acc + w[j:j + 1, :] * _shift_down(x, width - 1 - j)
    return acc


def _conv_bwd(dy, x, w, dw_ref, first):
    width = w.shape[0]
    dx = w[width - 1:width, :] * dy
    for j in range(width - 1):
        dx = dx + w[j:j + 1, :] * _shift_up(dy, width - 1 - j)
    for j in range(width):
        row = jnp.sum(dy * _shift_down(x, width - 1 - j), axis=0, keepdims=True)
        _accumulate(dw_ref.at[j:j + 1, :], row, first)
    return dx


def _qkv_act(xc, is_v, scale):
    a = _silu(xc)
    nrm = a * lax.rsqrt(jnp.sum(a * a, axis=-1, keepdims=True) + EPS) * scale
    return jnp.where(is_v, a, nrm)


def _qkv_consts(j, heads):
    is_v = j >= 2 * heads
    scale = jnp.where(j < heads, HEAD ** -0.5, 1.0).astype(F32)
    return is_v, scale


def _qkv_fwd(p, w, heads, name, rider=None):
    bl, s, w3 = p.shape
    r_args, r_in, r_out, r_shape, r_sems = _ride_specs(rider)

    def body(*refs):
        (p_ref, w_ref, o_ref), hooks = _split_refs(refs, 2, 1, 0, rider)
        first, mid, last = _grid_marks(w3 // HEAD, bl)
        _hooks_before(hooks, first, mid)
        is_v, scale = _qkv_consts(pl.program_id(0), heads)
        o_ref[...] = _qkv_act(_conv(p_ref[...], w_ref[...]), is_v, scale)
        _hooks_after(hooks, last)

    blk = pl.BlockSpec((None, s, HEAD), lambda j, b: (b, 0, j))
    return pl.pallas_call(
        body, name=name, grid=(w3 // HEAD, bl), in_specs=[blk, pl.BlockSpec((w.shape[0], HEAD), lambda j, b: (0, j))] + r_in,
        out_specs=[blk] + r_out, out_shape=[jax.ShapeDtypeStruct(p.shape, F32)] + r_shape, scratch_shapes=r_sems,
        compiler_params=_cparams("arbitrary", "arbitrary"),
    )(p, w, *r_args)


def _qkv_bwd(p, w, dout, heads, name, rider=None):
    bl, s, w3 = p.shape
    r_args, r_in, r_out, r_shape, r_sems = _ride_specs(rider)

    def body(*refs):
        (p_ref, w_ref, d_ref, dp_ref, dw_ref), hooks = _split_refs(refs, 3, 2, 0, rider)
        first, mid, last = _grid_marks(w3 // HEAD, bl)
        _hooks_before(hooks, first, mid)
        is_v, scale = _qkv_consts(pl.program_id(0), heads)
        x, wv = p_ref[...], w_ref[...]
        _, vjp = jax.vjp(lambda xc: _qkv_act(xc, is_v, scale), _conv(x, wv))
        (dxc,) = vjp(d_ref[...])
        dp_ref[...] = _conv_bwd(dxc, x, wv, dw_ref, pl.program_id(1) == 0).astype(dp_ref.dtype)
        _hooks_after(hooks, last)

    blk = pl.BlockSpec((None, s, HEAD), lambda j, b: (b, 0, j))
    wblk = pl.BlockSpec((w.shape[0], HEAD), lambda j, b: (0, j))
    return pl.pallas_call(
        body, name=name, grid=(w3 // HEAD, bl), in_specs=[blk, wblk, blk] + r_in, out_specs=[blk, wblk] + r_out,
        out_shape=[jax.ShapeDtypeStruct(p.shape, MXU_DTYPE), jax.ShapeDtypeStruct(w.shape, F32)] + r_shape,
        scratch_shapes=r_sems, compiler_params=_cparams("arbitrary", "arbitrary"),
    )(p, w, dout, *r_args)


def _sc_specs(p, w):
    bl, s, w3 = p.shape
    nblk = w3 // 3 // LANE
    sec = lambda k: pl.BlockSpec((None, s, LANE), functools.partial(lambda j, b, k: (b, 0, k * nblk + j), k=k))
    return nblk, [sec(0), sec(1), sec(2)], pl.BlockSpec((w.shape[0], LANE), lambda j, b: (0, j)), \
        pl.BlockSpec((None, s, LANE), lambda j, b: (b, 0, j))


def _sc_fwd(p, w, name):
    bl, s, w3 = p.shape
    nblk, secs, wblk, oblk = _sc_specs(p, w)

    def body(b_ref, c_ref, x_ref, w_ref, o_ref):
        o_ref[...] = (b_ref[...] * _conv(c_ref[...] * x_ref[...], w_ref[...])).astype(o_ref.dtype)

    return pl.pallas_call(
        body, name=name, grid=(nblk, bl), in_specs=secs + [wblk], out_specs=oblk,
        out_shape=jax.ShapeDtypeStruct((bl, s, w3 // 3), MXU_DTYPE), compiler_params=_cparams("parallel", "parallel"),
    )(p, p, p, w)


def _sc_bwd(p, w, dout, name):
    bl, s, w3 = p.shape
    nblk, secs, wblk, oblk = _sc_specs(p, w)

    def body(b_ref, c_ref, x_ref, w_ref, d_ref, db_ref, dc_ref, dx_ref, dw_ref):
        gb, gc, xin, wv, d = b_ref[...], c_ref[...], x_ref[...], w_ref[...], d_ref[...]
        u = gc * xin
        db_ref[...] = (d * _conv(u, wv)).astype(db_ref.dtype)
        du = _conv_bwd(d * gb, u, wv, dw_ref, pl.program_id(1) == 0)
        dc_ref[...] = (du * xin).astype(dc_ref.dtype)
        dx_ref[...] = (du * gc).astype(dx_ref.dtype)

    act = jax.ShapeDtypeStruct((bl, s, w3 // 3), MXU_DTYPE)
    return pl.pallas_call(
        body, name=name, grid=(nblk, bl), in_specs=secs + [wblk, oblk], out_specs=[oblk, oblk, oblk, wblk],
        out_shape=[act, act, act, jax.ShapeDtypeStruct(w.shape, F32)], compiler_params=_cparams("arbitrary", "arbitrary"),
    )(p, p, p, w, dout)


def _bdot(a, b, ca, cb):
    return lax.dot_general(a.astype(MXU_DTYPE), b.astype(MXU_DTYPE), (((ca,), (cb,)), ((), ())),
                           preferred_element_type=F32)


def _hdot(a, b):
    return lax.dot_general(a, b, (((1,), (0,)), ((), ())), precision=HIGHEST, preferred_element_type=F32)


def _lane_col(x, idx):
    lane = lax.broadcasted_iota(jnp.int32, x.shape, 1)
    return jnp.sum(jnp.where(lane == idx, x, 0.0), axis=1, keepdims=True)


def _chunk_masks():
    r = lax.broadcasted_iota(jnp.int32, (CHUNK, CHUNK), 0)
    c = lax.broadcasted_iota(jnp.int32, (CHUNK, CHUNK), 1)
    return r == c, r >= c, r > c


def _dot3(a, b):
    ah, bh = a.astype(MXU_DTYPE), b.astype(MXU_DTYPE)
    al, bl = (a - ah.astype(F32)).astype(MXU_DTYPE), (b - bh.astype(F32)).astype(MXU_DTYPE)
    dot = lambda x, y: lax.dot_general(x, y, (((1,), (0,)), ((), ())), preferred_element_type=F32)
    return dot(ah, bh) + (dot(ah, bl) + dot(al, bh))


def _tri_inv_steps(low, eye):
    x = -low
    p = jnp.where(eye, 1.0, 0.0) + x
    span = 2
    while span < CHUNK:
        x = _dot3(x, x)
        yield
        p = p + _dot3(p, x)
        yield
        span *= 2
    return p


def _round_robin(gens):
    out, live = [None] * len(gens), list(range(len(gens)))
    while live:
        still = []
        for i in live:
            try:
                next(gens[i])
                still.append(i)
            except StopIteration as stop:
                out[i] = stop.value
        live = still
    return out


def _gdn_pre(q, k, v, gc, beta, masks):
    eye, causal, strict = masks
    gc_row = jnp.sum(jnp.where(eye, gc, 0.0), axis=0, keepdims=True)
    decay = jnp.where(causal, jnp.exp(jnp.where(causal, gc - gc_row, 0.0)), 0.0)
    eg = jnp.exp(gc)
    gl = gc[CHUNK - 1:CHUNK, :]
    kb, vb = k * beta, v * beta
    low = jnp.where(strict, _bdot(kb, k, 1, 1) * decay, 0.0)
    qk = jnp.where(causal, _bdot(q, k, 1, 1) * decay, 0.0)
    rest = jnp.exp(gl - gc)
    return dict(decay=decay, eg=eg, gl=gl, kb=kb, vb=vb, kbe=kb * eg, low=low, qk=qk, qg=q * eg, rest=rest, kdec=k * rest)


def _gdn_specs(qkv, gbeta, heads, rev):
    bl, s, w3 = qkv.shape
    d, n = w3 // 3, s // CHUNK
    at = (lambda c: n - 1 - c) if rev else (lambda c: c)
    assert d == heads * HEAD
    sec = pl.BlockSpec((None, CHUNK, w3), lambda b, c: (b, at(c), 0))
    gspec = pl.BlockSpec((None, CHUNK, LANE), lambda b, c: (b, at(c), 0))
    sspec = pl.BlockSpec((None, None, heads, HEAD, HEAD), lambda b, c: (b, at(c), 0, 0, 0))
    tspec = pl.BlockSpec((None, None, heads, CHUNK, CHUNK), lambda b, c: (b, at(c), 0, 0, 0))
    return bl, s, d, n, sec, gspec, sspec, tspec


def _grid_marks(bl, n):
    b, c = pl.program_id(0), pl.program_id(1)
    late = min(bl * n - 1, bl * n * 5 // 6)
    first = jnp.logical_and(b == 0, c == 0)
    mid = jnp.logical_and(b == late // n, c == late % n)
    return first, mid, jnp.logical_and(b == bl - 1, c == n - 1)


def _gdn_fwd(qkv, gbeta, heads, name, rider=None):
    bl, s, d, n, sec, gspec, sspec, tspec = _gdn_specs(qkv, gbeta, heads, False)
    r_args, r_in, r_out, r_shape, r_sems = _ride_specs(rider)

    def body(*refs):
        (x_ref, g_ref, o_ref, s_ref, t_ref, st_ref), hooks = _split_refs(refs, 2, 3, 1, rider)
        first, mid, last = _grid_marks(bl, n)
        _hooks_before(hooks, first, mid)

        @pl.when(pl.program_id(1) == 0)
        def _():
            st_ref[...] = jnp.zeros_like(st_ref)

        masks = _chunk_masks()
        eye, causal, _ = masks
        gblk = g_ref[...]
        gc_all = _hdot(jnp.where(causal, 1.0, 0.0), gblk)
        st_all = st_ref[...]

        def head(h):
            st = st_all[h]
            q, k, v = (x_ref[:, sec * d + h * HEAD:sec * d + (h + 1) * HEAD] for sec in range(3))
            pre = _gdn_pre(q, k, v, _lane_col(gc_all, h), _lane_col(gblk, heads + h), masks)
            yield
            t = yield from _tri_inv_steps(pre["low"], eye)
            u, w = _bdot(t, pre["vb"], 1, 0), _bdot(t, pre["kbe"], 1, 0)
            yield
            vnew = u - _bdot(w, st, 1, 0)
            yield
            out = _bdot(pre["qg"], st, 1, 0) + _bdot(pre["qk"], vnew, 1, 0)
            return out, t, st * jnp.exp(pre["gl"]) + _bdot(pre["kdec"], vnew, 0, 0)

        outs, ts, states = zip(*_round_robin([head(h) for h in range(heads)]))
        o_ref[...] = jnp.concatenate(outs, axis=1)
        s_ref[...] = st_all
        t_ref[...] = jnp.stack(ts)
        st_ref[...] = jnp.stack(states)
        _hooks_after(hooks, last)

    return pl.pallas_call(
        body, name=name, grid=(bl, n), in_specs=[sec, gspec] + r_in,
        out_specs=[pl.BlockSpec((None, CHUNK, d), lambda b, c: (b, c, 0)), sspec, tspec] + r_out,
        out_shape=[jax.ShapeDtypeStruct((bl, s, d), F32), jax.ShapeDtypeStruct((bl, n, heads, HEAD, HEAD), F32),
                   jax.ShapeDtypeStruct((bl, n, heads, CHUNK, CHUNK), F32)] + r_shape,
        scratch_shapes=[pltpu.VMEM((heads, HEAD, HEAD), F32)] + r_sems, compiler_params=_cparams("arbitrary", "arbitrary"),
    )(qkv, gbeta, *r_args)


def _gdn_bwd(qkv, gbeta, dout, s_all, t_all, heads, name, rider=None):
    bl, s, d, n, sec, gspec, sspec, tspec = _gdn_specs(qkv, gbeta, heads, True)
    ospec = pl.BlockSpec((None, CHUNK, d), lambda b, c: (b, n - 1 - c, 0))
    r_args, r_in, r_out, r_shape, r_sems = _ride_specs(rider)

    def body(*refs):
        (x_ref, g_ref, do_ref, s_ref, t_ref, dx_ref, dg_ref, ds_ref), hooks = _split_refs(refs, 5, 2, 1, rider)
        first, mid, last = _grid_marks(bl, n)
        _hooks_before(hooks, first, mid)

        @pl.when(pl.program_id(1) == 0)
        def _():
            ds_ref[...] = jnp.zeros_like(ds_ref)

        masks = _chunk_masks()
        eye, causal, strict = masks
        gblk = g_ref[...]
        gc_all = _hdot(jnp.where(causal, 1.0, 0.0), gblk)
        lane = lax.broadcasted_iota(jnp.int32, gblk.shape, 1)
        last_row = lax.broadcasted_iota(jnp.int32, (CHUNK, 1), 0) == CHUNK - 1
        rowsum = lambda a: jnp.sum(a, axis=1, keepdims=True)
        st_all, t_all_, ds_all = s_ref[...], t_ref[...], ds_ref[...]

        def head(h):
            sl = slice(h * HEAD, (h + 1) * HEAD)
            q, k, v = (x_ref[:, sec * d + h * HEAD:sec * d + (h + 1) * HEAD] for sec in range(3))
            do = do_ref[:, sl]
            beta = _lane_col(gblk, heads + h)
            st, t, dsn = st_all[h], t_all_[h], ds_all[h]
            pre = _gdn_pre(q, k, v, _lane_col(gc_all, h), beta, masks)
            decay, eg, kb, vb, kbe, low, qk, qg, kdec = (pre[x] for x in ("decay", "eg", "kb", "vb", "kbe", "low", "qk", "qg", "kdec"))
            egl = jnp.exp(pre["gl"])
            yield
            u, w = _bdot(t, vb, 1, 0), _bdot(t, kbe, 1, 0)
            yield
            vnew = u - _bdot(w, st, 1, 0)
            yield
            dkdec = _bdot(vnew, dsn, 1, 1)
            dvnew = _bdot(kdec, dsn, 1, 0) + _bdot(qk, do, 0, 0)
            dgl = jnp.sum(dsn * st, keepdims=True) * egl
            dqg = _bdot(do, st, 1, 1)
            dqk = jnp.where(causal, _bdot(do, vnew, 1, 1), 0.0)
            yield
            dw = -_bdot(dvnew, st, 1, 1)
            ds_new = dsn * egl + _bdot(qg, do, 0, 0) - _bdot(w, dvnew, 0, 0)
            yield
            dt = _bdot(dvnew, vb, 1, 1) + _bdot(dw, kbe, 1, 1)
            dvb, dkbe = _bdot(t, dvnew, 0, 0), _bdot(t, dw, 0, 0)
            yield
            inner = _bdot(dt, t, 1, 1)
            yield
            dlow = -jnp.where(strict, _bdot(t, inner, 0, 0), 0.0)
            da, db = dlow * decay, dqk * decay
            yield
            m = dlow * low + dqk * qk
            kdk = dkdec * kdec
            col_of_m = jnp.sum(jnp.where(eye, jnp.sum(m, axis=0, keepdims=True), 0.0), axis=1, keepdims=True)
            dgc = rowsum(m) - col_of_m + rowsum(dqg * qg) + rowsum(dkbe * kbe) - rowsum(kdk)
            dgc = dgc + jnp.where(last_row, dgl + jnp.sum(kdk, keepdims=True), 0.0)
            dkb = _bdot(da, k, 1, 0) + dkbe * eg
            yield
            dk = _bdot(da, kb, 0, 0) + _bdot(db, q, 0, 0) + dkdec * pre["rest"] + dkb * beta
            dq = _bdot(db, k, 1, 0) + dqg * eg
            dbeta = rowsum(dkb * k) + rowsum(dvb * v)
            return dq, dk, dvb * beta, jnp.where(lane == h, dgc, 0.0) + jnp.where(lane == heads + h, dbeta, 0.0), ds_new

        dqs, dks, dvs, dgs, dss = zip(*_round_robin([head(h) for h in range(heads)]))
        dx_ref[...] = jnp.concatenate(dqs + dks + dvs, axis=1)
        ds_ref[...] = jnp.stack(dss)
        dgb = dgs[0]
        for extra in dgs[1:]:
            dgb = dgb + extra
        upper = jnp.where(jnp.logical_or(eye, jnp.logical_not(causal)), 1.0, 0.0)
        dg_ref[...] = jnp.where(lane < heads, _hdot(upper, dgb), dgb)
        _hooks_after(hooks, last)

    return pl.pallas_call(
        body, name=name, grid=(bl, n), in_specs=[sec, gspec, ospec, sspec, tspec] + r_in, out_specs=[sec, gspec] + r_out,
        out_shape=[jax.ShapeDtypeStruct(qkv.shape, F32), jax.ShapeDtypeStruct((bl, s, LANE), F32)] + r_shape,
        scratch_shapes=[pltpu.VMEM((heads, HEAD, HEAD), F32)] + r_sems, compiler_params=_cparams("arbitrary", "arbitrary"),
    )(qkv, gbeta, dout, s_all, t_all, *r_args)


def _position():
    return lax.axis_index("x"), lax.axis_index("y"), lax.axis_index("c")


def _all_gather(x, *, name, hbm):
    space = pltpu.HBM if hbm else pltpu.VMEM

    def body(x_ref, out_ref, send_sems, recv_sems, local_sem):
        ax, ay, ac = _position()
        me, sibling = (ax, ay, ac), (ax, ay, 1 - ac)
        chips = [(1 - ax, ay), (ax, 1 - ay), (1 - ax, 1 - ay)]

        def slot(px, py, pc):
            return out_ref.at[4 * px + 2 * py + pc]

        def copy(k, block, to, src=None):
            return pltpu.make_async_remote_copy(
                src_ref=slot(*block) if src is None else src, dst_ref=slot(*block), send_sem=send_sems.at[k],
                recv_sem=recv_sems.at[k], device_id=to, device_id_type=MESH_IDS)

        mine = pltpu.make_async_copy(x_ref, slot(*me), local_sem)
        mine.start()
        first = [copy(0, me, sibling, src=x_ref)] + [copy(1 + j, me, (*chip, ac), src=x_ref) for j, chip in enumerate(chips)]
        for cp in first:
            cp.start()
        passed = [copy(4 + j, (*chip, ac), sibling) for j, chip in enumerate(chips)]
        for j, chip in enumerate(chips):
            copy(1 + j, (*chip, ac), me).wait_recv()
            passed[j].start()
        copy(0, sibling, me).wait_recv()
        for j, chip in enumerate(chips):
            copy(4 + j, (*chip, 1 - ac), me).wait_recv()
        for cp in first + passed:
            cp.wait_send()
        mine.wait()

    return pl.pallas_call(
        body, name=name, out_shape=jax.ShapeDtypeStruct((NDEV,) + x.shape, x.dtype),
        in_specs=[pl.BlockSpec(memory_space=space)], out_specs=pl.BlockSpec(memory_space=space),
        scratch_shapes=[pltpu.SemaphoreType.DMA((7,)), pltpu.SemaphoreType.DMA((7,)), pltpu.SemaphoreType.DMA],
    )(x)


class _Rider:
    def __init__(self, arrays, out_shapes, sems, hooks):
        self.arrays, self.out_shapes, self.sems, self.hooks = arrays, out_shapes, sems, hooks


def _split_refs(refs, n_in, n_out, n_scratch, rider):
    r_in = len(rider.arrays) if rider else 0
    r_out = len(rider.out_shapes) if rider else 0
    o = n_in + r_in
    o2 = o + n_out + r_out
    host = refs[:n_in] + refs[o:o + n_out] + refs[o2:o2 + n_scratch]
    if rider is None:
        return host, None
    return host, rider.hooks(refs[n_in:o], refs[o + n_out:o2], *refs[o2 + n_scratch:])


def _hooks_before(hooks, first, mid):
    if hooks is not None:
        pl.when(first)(hooks[0])
        pl.when(mid)(hooks[1])


def _hooks_after(hooks, last):
    if hooks is not None:
        pl.when(last)(hooks[2])


def _ride_specs(rider):
    hbm = pl.BlockSpec(memory_space=pltpu.HBM)
    if rider is None:
        return [], [], [], [], []
    n_out = len(rider.out_shapes)
    return list(rider.arrays), [hbm] * len(rider.arrays), [hbm] * n_out, list(rider.out_shapes), list(rider.sems)


def _gather_rider(xs):
    n = len(xs)

    def hooks(x_refs, out_refs, send_sems, recv_sems):
        ax, ay, ac = _position()
        me, sibling = (ax, ay, ac), (ax, ay, 1 - ac)
        chips = [(1 - ax, ay), (ax, 1 - ay), (1 - ax, 1 - ay)]

        def copies(k, block, to, own=False):
            out = []
            for i in range(n):
                slot = out_refs[i].at[4 * block[0] + 2 * block[1] + block[2]]
                out.append(pltpu.make_async_remote_copy(
                    src_ref=x_refs[i] if own else slot, dst_ref=slot, send_sem=send_sems.at[k, i], recv_sem=recv_sems.at[k, i],
                    device_id=to, device_id_type=MESH_IDS))
            return out

        def first():
            for cp in copies(0, me, sibling, own=True):
                cp.start()
            for j, chip in enumerate(chips):
                for cp in copies(1 + j, me, (*chip, ac), own=True):
                    cp.start()

        def mid():
            for j, chip in enumerate(chips):
                for arrived, onward in zip(copies(1 + j, (*chip, ac), me), copies(4 + j, (*chip, ac), sibling)):
                    arrived.wait_recv()
                    onward.start()

        def last():
            for cp in copies(0, sibling, me):
                cp.wait_recv()
            for j, chip in enumerate(chips):
                for cp in copies(4 + j, (*chip, 1 - ac), me):
                    cp.wait_recv()
            for cp in copies(0, me, sibling, own=True):
                cp.wait_send()
            for j, chip in enumerate(chips):
                for cp in copies(1 + j, me, (*chip, ac), own=True) + copies(4 + j, (*chip, ac), sibling):
                    cp.wait_send()

        return first, mid, last

    return _Rider(list(xs), [jax.ShapeDtypeStruct((NDEV,) + x.shape, x.dtype) for x in xs],
                  [pltpu.SemaphoreType.DMA((7, n)), pltpu.SemaphoreType.DMA((7, n))], hooks)


def _scatter_rider(parts):
    packed = sum(r for _, r in parts)
    width, dtype = parts[0][0].shape[1], parts[0][0].dtype

    def hooks(g_refs, out_refs, send_sems, recv_sems):
        (recv_ref,) = out_refs
        ax, ay, ac = _position()

        def peer(rel):
            flip = lambda a, bit: 1 - a if rel & bit else a
            return flip(ax, 4), flip(ay, 2), flip(ac, 1)

        def first():
            for rel in range(1, NDEV):
                px, py, pc = peer(rel)
                off = 0
                for g_ref, (_, r) in zip(g_refs, parts):
                    rows = g_ref.at[pl.ds(pl.multiple_of((4 * px + 2 * py + pc) * r, ROW_ALIGN), r)]
                    pltpu.make_async_remote_copy(
                        src_ref=rows, dst_ref=recv_ref.at[rel - 1, pl.ds(off, r)], send_sem=send_sems.at[rel - 1],
                        recv_sem=recv_sems.at[rel - 1], device_id=(px, py, pc), device_id_type=MESH_IDS).start()
                    off += r

        def last():
            for rel in range(1, NDEV):
                slot = recv_ref.at[rel - 1]
                pltpu.make_async_remote_copy(src_ref=slot, dst_ref=slot, send_sem=send_sems.at[rel - 1],
                                             recv_sem=recv_sems.at[rel - 1], device_id=peer(rel), device_id_type=MESH_IDS).wait()

        return first, lambda: None, last

    return _Rider([g for g, _ in parts], [jax.ShapeDtypeStruct((NDEV - 1, packed, width), dtype)],
                  [pltpu.SemaphoreType.DMA((NDEV - 1,)), pltpu.SemaphoreType.DMA((NDEV - 1,))], hooks)


def _sum_direct(own, recv, name):
    r, w = own.shape
    tr = max(t for t in range(ROW_ALIGN, 257, ROW_ALIGN) if r % t == 0)

    def body(own_ref, *refs):
        acc = own_ref[...].astype(F32)
        for ref in refs[:-1]:
            acc = acc + ref[...].astype(F32)
        refs[-1][...] = acc

    rblk = lambda k: pl.BlockSpec((None, tr, w), functools.partial(lambda i, k: (k, i, 0), k=k))
    blk = pl.BlockSpec((tr, w), lambda i: (i, 0))
    return pl.pallas_call(body, name=name, grid=(r // tr,), in_specs=[blk] + [rblk(k) for k in range(NDEV - 1)],
                          out_specs=blk, out_shape=jax.ShapeDtypeStruct((r, w), F32),
                          compiler_params=_cparams("parallel"))(own, *([recv] * (NDEV - 1)))


ROW_ALIGN = 16


def _window_start(rows_per_dev, k):
    return rows_per_dev * k // ROW_ALIGN * ROW_ALIGN


def _exchange_in_chip(parts, name):
    n = len(parts)
    packed = sum(win for _, _, win, _ in parts)
    width, dtype = parts[0][0].shape[1], parts[0][0].dtype

    def body(*refs):
        g_refs, recv_ref, send_sems, recv_sems = refs[:n], *refs[n:]
        ax, ay, ac = _position()
        sibling = (ax, ay, 1 - ac)
        for q in range(4):
            for g_ref, (_, r, win, off) in zip(g_refs, parts):
                there = g_ref.at[pl.ds(pl.multiple_of(_window_start(r, 2 * q + 1 - ac), ROW_ALIGN), win)]
                pltpu.make_async_remote_copy(src_ref=there, dst_ref=recv_ref.at[q, pl.ds(off, win)], send_sem=send_sems.at[q],
                                             recv_sem=recv_sems.at[q], device_id=sibling, device_id_type=MESH_IDS).start()
        for q in range(4):
            pltpu.make_async_remote_copy(src_ref=recv_ref.at[q], dst_ref=recv_ref.at[q], send_sem=send_sems.at[q],
                                         recv_sem=recv_sems.at[q], device_id=sibling, device_id_type=MESH_IDS).wait()

    hbm = pl.BlockSpec(memory_space=pltpu.HBM)
    return pl.pallas_call(
        body, name=name, out_shape=jax.ShapeDtypeStruct((4, packed, width), dtype), in_specs=[hbm] * n, out_specs=hbm,
        scratch_shapes=[pltpu.SemaphoreType.DMA((4,)), pltpu.SemaphoreType.DMA((4,))],
    )(*[g for g, _, _, _ in parts])


def _exchange_chips(s1, name):
    def body(s_ref, recv_ref, send_sems, recv_sems):
        ax, ay, ac = _position()
        chips = [(1 - ax, ay), (ax, 1 - ay), (1 - ax, 1 - ay)]
        copies = [pltpu.make_async_remote_copy(
            src_ref=s_ref.at[2 * cx + cy], dst_ref=recv_ref.at[r], send_sem=send_sems.at[r], recv_sem=recv_sems.at[r],
            device_id=(cx, cy, ac), device_id_type=MESH_IDS) for r, (cx, cy) in enumerate(chips)]
        for cp in copies:
            cp.start()
        for cp in copies:
            cp.wait_recv()
        for cp in copies:
            cp.wait_send()

    hbm = pl.BlockSpec(memory_space=pltpu.HBM)
    return pl.pallas_call(
        body, name=name, out_shape=jax.ShapeDtypeStruct((3,) + s1.shape[1:], s1.dtype), in_specs=[hbm], out_specs=hbm,
        scratch_shapes=[pltpu.SemaphoreType.DMA((3,)), pltpu.SemaphoreType.DMA((3,))],
    )(s1)


def _exchange_chips_async(s1, name):
    hbm = pltpu.MemorySpace.HBM
    src = jax.new_ref(s1, memory_space=hbm)
    got = jax.empty_ref(jax.ShapeDtypeStruct((3,) + s1.shape[1:], s1.dtype), memory_space=hbm)

    @pl.kernel(mesh=plsc.ScalarSubcoreMesh(axis_name="sequencer", num_cores=1), name=name,
               scratch_types=(pltpu.SemaphoreType.DMA((3,)), pltpu.SemaphoreType.DMA((3,))),
               compiler_params=pltpu.CompilerParams(collective_id=0))
    def launch(send_sems, recv_sems):
        ax, ay, ac = _position()
        chips = [(1 - ax, ay), (ax, 1 - ay), (1 - ax, 1 - ay)]
        barrier = pltpu.get_barrier_semaphore()
        for cx, cy in chips:
            pl.semaphore_signal(barrier, inc=1, device_id=(cx, cy, ac), device_id_type=MESH_IDS)
        pl.semaphore_wait(barrier, 3)
        copies = [pltpu.make_async_remote_copy(
            src_ref=src.at[2 * cx + cy], dst_ref=got.at[r], send_sem=send_sems.at[r], recv_sem=recv_sems.at[r],
            device_id=(cx, cy, ac), device_id_type=MESH_IDS) for r, (cx, cy) in enumerate(chips)]
        for cp in copies:
            cp.start()
        for cp in copies:
            cp.wait_recv()
        for cp in copies:
            cp.wait_send()

    launch()
    return got[...]


def _sum_in_chip(own, recv, name):
    _, r, w = own.shape
    tr = _tile(r, (256, 128))

    def body(a_ref, b_ref, o_ref):
        o_ref[...] = (a_ref[...].astype(F32) + b_ref[...].astype(F32)).astype(o_ref.dtype)

    blk = pl.BlockSpec((None, tr, w), lambda q, i: (q, i, 0))
    return pl.pallas_call(body, name=name, grid=(4, r // tr), in_specs=[blk, blk], out_specs=blk,
                          out_shape=jax.ShapeDtypeStruct(own.shape, own.dtype),
                          compiler_params=_cparams("parallel", "parallel"))(own, recv)


def _sum_chips(s1, recv, chip, name):
    _, r, w = s1.shape
    tr = _tile(r, (256, 128))

    def body(c_ref, s_ref, r0_ref, r1_ref, r2_ref, o_ref):
        f = lambda ref: ref[...].astype(F32)
        o_ref[...] = ((f(s_ref) + f(r0_ref)) + f(r1_ref)) + f(r2_ref)

    rblk = lambda k: pl.BlockSpec((None, tr, w), functools.partial(lambda i, c, k: (k, i, 0), k=k))
    grid_spec = pltpu.PrefetchScalarGridSpec(
        num_scalar_prefetch=1, grid=(r // tr,),
        in_specs=[pl.BlockSpec((None, tr, w), lambda i, c: (c[0], i, 0)), rblk(0), rblk(1), rblk(2)],
        out_specs=pl.BlockSpec((tr, w), lambda i, c: (i, 0)))
    return pl.pallas_call(body, name=name, grid_spec=grid_spec, out_shape=jax.ShapeDtypeStruct((r, w), F32),
                          compiler_params=_cparams("parallel"))(chip, s1, recv, recv, recv)


def _silu_rows(x, name):
    def body(x_ref, o_ref):
        o_ref[...] = _silu(x_ref[...])

    return pl.pallas_call(body, name=name, out_shape=jax.ShapeDtypeStruct(x.shape, F32))(x)


def _row_sum(x, name):
    def body(x_ref, o_ref):
        acc = x_ref[0:1, :]
        for i in range(1, x.shape[0]):
            acc = acc + x_ref[i:i + 1, :]
        o_ref[...] = acc

    return pl.pallas_call(body, name=name, out_shape=jax.ShapeDtypeStruct((1, x.shape[1]), F32))(x)


def _adamw(w, g, m, v, name):
    cols = w.shape[-1]
    rows = w.size // cols
    tr = _tile(rows, (128,))
    tc = LANE if (tr == rows and rows > 512 and cols % LANE == 0) else cols

    def body(w_ref, g_ref, m_ref, v_ref, d_ref, mo_ref, vo_ref):
        grad = g_ref[...]
        m_new = ADAM_B1 * m_ref[...] + (1.0 - ADAM_B1) * grad
        v_new = ADAM_B2 * v_ref[...] + (1.0 - ADAM_B2) * jnp.square(grad)
        m_hat = m_new / (1.0 - ADAM_B1 ** ADAM_STEP)
        v_hat = v_new / (1.0 - ADAM_B2 ** ADAM_STEP)
        d_ref[...] = -ADAM_LR * (m_hat / (jnp.sqrt(v_hat) + ADAM_EPS) + ADAM_WD * w_ref[...])
        mo_ref[...] = m_new
        vo_ref[...] = v_new

    blk = pl.BlockSpec((tr, tc), lambda i, j: (i, j))
    out = pl.pallas_call(
        body, name=name, grid=(rows // tr, cols // tc), in_specs=[blk] * 4, out_specs=[blk] * 3,
        out_shape=[jax.ShapeDtypeStruct((rows, cols), F32)] * 3, compiler_params=_cparams("parallel", "parallel"),
    )(*[t.reshape(rows, cols) for t in (w, g, m, v)])
    return [t.reshape(w.shape) for t in out]


def _pack(parts, width, row_mult, dtype):
    flat = jnp.concatenate([p.reshape(-1).astype(dtype) for p in parts])
    rows = -(-flat.shape[0] // (width * row_mult)) * row_mult
    return jnp.pad(flat, (0, rows * width - flat.shape[0])).reshape(rows, width)


def _unpack(flat, shapes):
    out, off = [], 0
    for shp in shapes:
        size = 1
        for dim in shp:
            size *= dim
        out.append(flat[:, off:off + size].reshape((flat.shape[0],) + tuple(shp)))
        off += size
    return out


def _devices_to_cols(a):
    _, r, c = a.shape
    return a.transpose(1, 0, 2).reshape(r, NDEV * c)


def kernel(x, c, w_ada, b_ada, norm1_w, w_in, gdn_conv_w, gdn_a_log, gdn_dt_bias, gdn_norm_w, w_gdn_proj, sc_conv_w, w_sc_out, w_o, norm2_w, w_ffn_in, w_ffn_out, w_ada_f, b_ada_f, normf_w, loss_target, m_w_ada, m_b_ada, m_norm1_w, m_w_in, m_gdn_conv_w, m_gdn_a_log, m_gdn_dt_bias, m_gdn_norm_w, m_w_gdn_proj, m_sc_conv_w, m_w_sc_out, m_w_o, m_norm2_w, m_w_ffn_in, m_w_ffn_out, m_w_ada_f, m_b_ada_f, m_normf_w, v_w_ada, v_b_ada, v_norm1_w, v_w_in, v_gdn_conv_w, v_gdn_a_log, v_gdn_dt_bias, v_gdn_norm_w, v_w_gdn_proj, v_sc_conv_w, v_w_sc_out, v_w_o, v_norm2_w, v_w_ffn_in, v_w_ffn_out, v_w_ada_f, v_b_ada_f, v_normf_w):
    bl, s, d = x.shape
    heads = gdn_a_log.shape[-1]
    dff = w_ffn_out.shape[1] * NDEV
    tok = bl * s
    ax, ay, ac = _position()
    dev = 4 * ax + 2 * ay + ac
    as_tok = lambda a: a.reshape(bl, s, a.shape[-1])
    as_mat = lambda a: a.reshape(tok, a.shape[-1])

    small = _all_gather(_pack([c, gdn_conv_w, sc_conv_w], LANE, 8, F32), name="gather_cond", hbm=False)
    c_all, conv_w, sc_w = _unpack(small.reshape(NDEV, -1), [(bl, d), gdn_conv_w.shape[1:], sc_conv_w.shape[1:]])
    c_act = _silu_rows(c_all.reshape(NDEV * bl, d), "cond_silu")
    conv_w, sc_w = _devices_to_cols(conv_w), _devices_to_cols(sc_w)
    n_ada, n_adaf = w_ada.shape[-1], w_ada_f.shape[-1]
    bias = jnp.broadcast_to(lax.dynamic_slice_in_dim(b_ada, dev * n_ada, n_ada, axis=1), (NDEV * bl, n_ada))
    biasf = jnp.broadcast_to(lax.dynamic_slice_in_dim(b_ada_f.reshape(1, -1), dev * n_adaf, n_adaf, axis=1), (NDEV * bl, n_adaf))
    mod_cols = _mm(c_act, w_ada[0], add=bias, name="ada_cols")
    modf_cols = _mm(c_act, w_ada_f, add=biasf, name="adaf_cols")
    mods = _all_gather(jnp.concatenate([mod_cols, modf_cols], axis=1), name="gather_mod", hbm=False)
    mod_all = mods[:, :, :n_ada].transpose(1, 0, 2).reshape(NDEV * bl, NDEV * n_ada)
    modf_all = mods[:, :, n_ada:].transpose(1, 0, 2).reshape(NDEV * bl, NDEV * n_adaf)
    my_rows = lambda a: lax.dynamic_slice_in_dim(a, dev * bl, bl, axis=0)
    sh1, sc1, g1, sh2, sc2, g2 = [t.reshape(bl, 1, d) for t in jnp.split(my_rows(mod_all), 6, axis=1)]
    shf, scf = [t.reshape(bl, 1, d) for t in jnp.split(my_rows(modf_all), 2, axis=1)]

    late = [t.astype(MXU_DTYPE) for t in (w_gdn_proj[0], w_sc_out[0], w_o[0], w_ffn_in[0].T, w_ffn_out[0])]
    rows = [t.shape[0] for t in late] + [w_in.shape[-1]]
    offs = [sum(rows[:i]) for i in range(5)]
    in_rows = -(-rows[5] // ROW_ALIGN) * ROW_ALIGN
    in_send = jnp.pad(w_in[0].T.astype(MXU_DTYPE), ((0, in_rows - rows[5]), (0, 0)))
    wt_in = _all_gather(in_send, name="gather_w_in", hbm=True)[:, :rows[5], :].reshape(NDEV * rows[5], d)
    o_z, o_ab, o_sc, o_ga, o_gb = 3 * d, 4 * d, 4 * d + 2 * heads, 7 * d + 2 * heads, 8 * d + 2 * heads
    s_qkv, s_z, s_sc, s_gate = (0, o_z), (o_z, d), (o_sc, 3 * d), (o_ga, 2 * d)
    wt_ab = jnp.pad(wt_in[o_ab:o_sc], ((0, LANE - 2 * heads), (0, 0)))

    n1w, n2w, nfw = norm1_w.reshape(1, d), norm2_w.reshape(1, d), normf_w.reshape(1, d)
    lanes = lambda a: jnp.pad(a.reshape(1, -1), ((0, 0), (0, LANE - a.size)))
    a_log, dt_bias, gnw = lanes(gdn_a_log), lanes(gdn_dt_bias), gdn_norm_w.reshape(1, HEAD)
    f_gates = functools.partial(_f_gates, heads=heads)
    (h1,) = _tok_fwd(_f_norm_mod, [x], [sh1, sc1], [n1w], [(d, MXU_DTYPE)], name="norm1", ts=512)
    h1m = as_mat(h1)
    p_qkv = as_tok(_mm(h1m, wt_in, tb=True, b_rows=s_qkv, name="in_qkv"))
    p_z = as_tok(_mm(h1m, wt_in, tb=True, b_rows=s_z, name="in_z"))
    p_ab = as_tok(_mm(h1m, wt_ab, tb=True, name="in_ab"))
    p_sc = as_tok(_mm(h1m, wt_in, tb=True, b_rows=s_sc, name="in_sc"))
    p_g = as_tok(_mm(h1m, wt_in, tb=True, b_rows=s_gate, name="in_gate"))
    qkv, *gathered = _qkv_fwd(p_qkv, conv_w, heads, "qkv_conv", rider=_gather_rider(late[:3]))
    (gbeta,) = _tok_fwd(f_gates, [p_ab], [], [a_log, dt_bias], [(LANE, F32)], name="gates", ts=512)
    o, s_all, t_all, *gathered_ffn = _gdn_fwd(qkv, gbeta, heads, "gdn", rider=_gather_rider(late[3:]))
    gathered += gathered_ffn
    wgp, wso, wo, wt_fi, wfo = [lax.dynamic_update_slice_in_dim(g, own[None], dev, axis=0).reshape(NDEV * own.shape[0], d)
                                for g, own in zip(gathered, late)]
    (og,) = _tok_fwd(_f_gdn_out, [o, p_z], [], [(gnw, None)], [(d, MXU_DTYPE)], name="gdn_out", ts=2048, wb=HEAD, cols=heads)
    y_a = as_tok(_mm(as_mat(og), wgp, name="gdn_proj"))
    scp = _sc_fwd(p_sc, sc_w, "sc_conv")
    y_b = as_tok(_mm(as_mat(scp), wso, name="sc_out"))
    mcols = d // 512 if d % 512 == 0 else 1
    mwb = d // mcols
    merge_toks = [(p_g, 0), (p_g, mcols), y_a, y_b]
    (mrg,) = _tok_fwd(_f_merge, merge_toks, [], [], [(d, MXU_DTYPE)], name="merge", ts=1024, wb=mwb, cols=mcols)
    mix = as_tok(_mm(as_mat(mrg), wo, name="mix_out"))
    x2, h2 = _tok_fwd(_f_res_norm_mod, [x, mix], [g1, sh2, sc2], [n2w], [(d, F32), (d, MXU_DTYPE)], name="norm2", ts=512)
    gu = as_tok(_mm(as_mat(h2), wt_fi, tb=True, name="ffn_in"))
    fwb = _tile(dff, (256, 128))
    fcols = dff // fwb
    (act,) = _tok_fwd(_f_swiglu, [(gu, 0), (gu, fcols)], [], [], [(dff, MXU_DTYPE)], name="swiglu", ts=2048, wb=fwb, cols=fcols)
    ff = as_tok(_mm(as_mat(act), wfo, name="ffn_out"))

    loss_l, (dx2, dff_out, _), (dg2, dshf, dscf), (dnfw,) = _tok_bwd(
        _f_loss, [x2, ff, loss_target], [g2, shf, scf], [nfw], [], [True, True, False], name="loss", ts=512, loss=True,
        tok_dtype=[F32, MXU_DTYPE, None])
    dffm = as_mat(dff_out)
    dact = as_tok(_mm(dffm, wfo, tb=True, name="d_ffn_out"))
    gmm = functools.partial(_mm, ta=True, out_dtype=MXU_DTYPE)
    gw_ffn_out = gmm(as_mat(act), dffm, name="g_ffn_out")
    (dgu_a, dgu_b), _, _ = _tok_bwd(_f_swiglu, [(gu, 0), (gu, fcols)], [], [], [dact], [True, True], name="d_swiglu",
                                    ts=2048, wb=fwb, cols=fcols, tok_dtype=MXU_DTYPE)
    dh2 = _mm(as_mat(dgu_a), wt_fi, b_rows=(0, dff), name="d_ffn_in_a")
    dh2 = as_tok(_mm(as_mat(dgu_b), wt_fi, b_rows=(dff, dff), add=dh2, name="d_ffn_in_b"))
    h2m = as_mat(h2)
    gwt_ffn_in = gmm(as_mat(dgu_a), h2m, out_rows=2 * dff, name="g_ffn_in_a")
    gwt_ffn_in = gmm(as_mat(dgu_b), h2m, out_rows=2 * dff, row_off=dff, into=gwt_ffn_in, name="g_ffn_in_b")
    (dx_skip, dmix), (dg1, dsh2, dsc2), (dn2w,) = _tok_bwd(
        _f_res_norm_mod, [x, mix], [g1, sh2, sc2], [n2w], [dx2, dh2], [True, True], name="d_norm2", ts=256,
        tok_dtype=[F32, MXU_DTYPE])
    dmixm = as_mat(dmix)
    dmrg = as_tok(_mm(dmixm, wo, tb=True, name="d_mix_out"))
    gw_o = gmm(as_mat(mrg), dmixm, name="g_mix_out")
    (dga, dgb, dya, dyb), _, _ = _tok_bwd(_f_merge, merge_toks, [], [], [dmrg], [True] * 4, name="d_merge", ts=512,
                                          wb=mwb, cols=mcols, tok_dtype=MXU_DTYPE)
    dyam, dybm = as_mat(dya), as_mat(dyb)
    dog = as_tok(_mm(dyam, wgp, tb=True, name="d_gdn_proj"))
    gw_gdn_proj = gmm(as_mat(og), dyam, name="g_gdn_proj")
    dscp = as_tok(_mm(dybm, wso, tb=True, name="d_sc_out"))
    gw_sc_out = gmm(as_mat(scp), dybm, name="g_sc_out")
    dscb, dscc, dscx, g_sc_w = _sc_bwd(p_sc, sc_w, dscp, "d_sc_conv")
    (do, dz), _, (g_gnw,) = _tok_bwd(_f_gdn_out, [o, p_z], [], [(gnw, None)], [dog], [True, True], name="d_gdn_out",
                                     ts=2048, wb=HEAD, cols=heads, tok_dtype=[F32, MXU_DTYPE])
    ffn_parts, mix_parts = [(gwt_ffn_in, rows[3]), (gw_ffn_out, rows[4])], [(gw_gdn_proj, rows[0]), (gw_sc_out, rows[1]), (gw_o, rows[2])]
    own_rows = lambda parts: jnp.concatenate([lax.dynamic_slice_in_dim(g, dev * r, r, axis=0) for g, r in parts], axis=0)
    dqkv, dgbeta, ffn_recv = _gdn_bwd(qkv, gbeta, do, s_all, t_all, heads, "d_gdn", rider=_scatter_rider(ffn_parts))
    dp_qkv, g_conv_w, mix_recv = _qkv_bwd(p_qkv, conv_w, dqkv, heads, "d_qkv_conv", rider=_scatter_rider(mix_parts))
    ffn_red = _sum_direct(own_rows(ffn_parts), ffn_recv, "sum_ffn")
    mix_red = _sum_direct(own_rows(mix_parts), mix_recv, "sum_mix")
    (dp_ab,), _, (g_a_log, g_dt_bias) = _tok_bwd(f_gates, [p_ab], [], [a_log, dt_bias], [dgbeta], [True], name="d_gates",
                                                 ts=512, tok_dtype=MXU_DTYPE)
    sections = [(dp_qkv, s_qkv), (dz, s_z), (dp_ab, None), (dscb, (o_sc, d)), (dscc, (o_sc + d, d)), (dscx, (o_sc + 2 * d, d)),
                (dga, (o_ga, d)), (dgb, (o_gb, d))]
    gwt_in = [gmm(as_mat(dp), h1m, name=f"g_in_{k}") for k, (dp, _) in enumerate(sections)]
    gwt_in[2] = gwt_in[2][:2 * heads]
    gwt_in = jnp.concatenate(gwt_in, axis=0)

    r_in = rows[5]
    win = -(-(r_in + max(r_in * k % ROW_ALIGN for k in range(NDEV))) // 128) * 128
    need_rows = max(_window_start(r_in, k) for k in range(NDEV)) + win
    gwt_in = jnp.pad(gwt_in, ((0, need_rows - gwt_in.shape[0]), (0, 0)))
    recv1 = _exchange_in_chip([(gwt_in, r_in, win, 0)], "scatter_in_chip")
    own = jnp.stack([lax.dynamic_slice_in_dim(gwt_in, _window_start(r_in, 2 * q + ac), win, axis=0) for q in range(4)])
    s1 = _sum_in_chip(own, recv1, "sum_in_chip")
    recv2 = _exchange_chips_async(s1, "scatter_chips")

    dh1 = None
    for k, (dp, sec) in enumerate(sections):
        dh1 = _mm(as_mat(dp), wt_ab if sec is None else wt_in, b_rows=sec, add=dh1, name=f"d_in_{k}")
    (grad_x,), (dsh1, dsc1), (dn1w,) = _tok_bwd(_f_norm_mod_skip, [x], [sh1, sc1], [n1w], [as_tok(dh1), dx_skip], [True],
                                                name="d_norm1", ts=256)
    reduced = _sum_chips(s1, recv2, (2 * ax + ay).reshape(1).astype(jnp.int32), "sum_chips")
    gt_w_in = lax.dynamic_slice_in_dim(reduced, r_in * dev - _window_start(r_in, dev), r_in, axis=0)
    g_w_in = gt_w_in.T.reshape(w_in.shape)
    gt_w_ffn_in = ffn_red[:rows[3]]
    g_w_ffn_in = gt_w_ffn_in.T.reshape(w_ffn_in.shape)
    g_w_ffn_out = ffn_red[rows[3]:].reshape(w_ffn_out.shape)
    g_w_gdn_proj, g_w_sc_out, g_w_o = (mix_red[offs[i]:offs[i] + rows[i]].reshape(ref.shape)
                                       for i, ref in enumerate((w_gdn_proj, w_sc_out, w_o)))

    dmod = jnp.concatenate([t.reshape(bl, d) for t in (dsh1, dsc1, dg1, dsh2, dsc2, dg2)], axis=1)
    dmodf = jnp.concatenate([t.reshape(bl, d) for t in (dshf, dscf)], axis=1)
    summed_parts = [dn1w, dn2w, dnfw, g_gnw, g_a_log, g_dt_bias, g_conv_w, g_sc_w, loss_l]
    partial = _all_gather(_pack([dmod, dmodf] + summed_parts, LANE, 8, F32), name="gather_small", hbm=False)
    partial = partial.reshape(NDEV, -1)
    n_rows = bl * (6 * d + 2 * d)
    dmod_all, dmodf_all = _unpack(partial[:, :n_rows], [(bl, 6 * d), (bl, 2 * d)])
    dmod_all, dmodf_all = dmod_all.reshape(NDEV * bl, 6 * d), dmodf_all.reshape(NDEV * bl, 2 * d)
    totals = _row_sum(partial[:, n_rows:], "sum_small")
    t_n1w, t_n2w, t_nfw, t_gnw, t_a_log, t_dt_bias, t_conv_w, t_sc_w, t_loss = [
        t[0] for t in _unpack(totals, [p.shape for p in summed_parts])]
    my_cols = lambda a, n: lax.dynamic_slice_in_dim(a, dev * n, n, axis=1)
    grads = {
        "w_ada": _mm(c_act, my_cols(dmod_all, n_ada), ta=True, name="g_ada").reshape(w_ada.shape),
        "b_ada": _row_sum(dmod_all, "g_ada_bias").reshape(b_ada.shape),
        "norm1_w": t_n1w.reshape(norm1_w.shape),
        "w_in": g_w_in,
        "gdn_conv_w": my_cols(t_conv_w, gdn_conv_w.shape[-1]).reshape(gdn_conv_w.shape),
        "gdn_a_log": t_a_log[:, :heads].reshape(gdn_a_log.shape),
        "gdn_dt_bias": t_dt_bias[:, :heads].reshape(gdn_dt_bias.shape),
        "gdn_norm_w": t_gnw.reshape(gdn_norm_w.shape),
        "w_gdn_proj": g_w_gdn_proj,
        "sc_conv_w": my_cols(t_sc_w, sc_conv_w.shape[-1]).reshape(sc_conv_w.shape),
        "w_sc_out": g_w_sc_out,
        "w_o": g_w_o,
        "norm2_w": t_n2w.reshape(norm2_w.shape),
        "w_ffn_in": g_w_ffn_in,
        "w_ffn_out": g_w_ffn_out,
        "w_ada_f": _mm(c_act, my_cols(dmodf_all, n_adaf), ta=True, name="g_adaf").reshape(w_ada_f.shape),
        "b_ada_f": _row_sum(dmodf_all, "g_adaf_bias").reshape(b_ada_f.shape),
        "normf_w": t_nfw.reshape(normf_w.shape),
    }
    weights = dict(w_ada=w_ada, b_ada=b_ada, norm1_w=norm1_w, w_in=w_in, gdn_conv_w=gdn_conv_w, gdn_a_log=gdn_a_log,
                   gdn_dt_bias=gdn_dt_bias, gdn_norm_w=gdn_norm_w, w_gdn_proj=w_gdn_proj, sc_conv_w=sc_conv_w,
                   w_sc_out=w_sc_out, w_o=w_o, norm2_w=norm2_w, w_ffn_in=w_ffn_in, w_ffn_out=w_ffn_out, w_ada_f=w_ada_f,
                   b_ada_f=b_ada_f, normf_w=normf_w)
    m_in = [m_w_ada, m_b_ada, m_norm1_w, m_w_in, m_gdn_conv_w, m_gdn_a_log, m_gdn_dt_bias, m_gdn_norm_w, m_w_gdn_proj,
            m_sc_conv_w, m_w_sc_out, m_w_o, m_norm2_w, m_w_ffn_in, m_w_ffn_out, m_w_ada_f, m_b_ada_f, m_normf_w]
    v_in = [v_w_ada, v_b_ada, v_norm1_w, v_w_in, v_gdn_conv_w, v_gdn_a_log, v_gdn_dt_bias, v_gdn_norm_w, v_w_gdn_proj,
            v_sc_conv_w, v_w_sc_out, v_w_o, v_norm2_w, v_w_ffn_in, v_w_ffn_out, v_w_ada_f, v_b_ada_f, v_normf_w]
    deltas, new_m, new_v = [], [], []
    grads_t = {"w_in": gt_w_in, "w_ffn_in": gt_w_ffn_in}
    for (wname, wt), mt, vt in zip(weights.items(), m_in, v_in):
        if wname in grads_t:
            back = lambda a, wt=wt: a.T.reshape(wt.shape)
            dl, mn, vn = (back(a) for a in _adamw(wt[0].T, grads_t[wname], mt[0].T, vt[0].T, "adamw_" + wname))
        else:
            dl, mn, vn = _adamw(wt, grads[wname], mt, vt, "adamw_" + wname)
        deltas.append(dl)
        new_m.append(mn)
        new_v.append(vn)
    loss = t_loss[0, 0]
    return (loss, grad_x, *[grads[k] for k in weights], *deltas, *new_m, *new_v)
```

```python
import functools

import jax
import jax.numpy as jnp
from jax import lax
from jax.experimental import pallas as pl
from jax.experimental.pallas import tpu as pltpu
from jax.experimental.pallas import tpu_sc as plsc

F32 = jnp.float32
MXU_DTYPE = jnp.bfloat16
NDEV = 8
CHUNK = 64
HEAD = 128
LANE = 128
EPS = 1e-6
ADAM_LR, ADAM_B1, ADAM_B2, ADAM_EPS, ADAM_WD, ADAM_STEP = 0.001, 0.9, 0.999, 1e-08, 0.01, 10
VMEM_LIMIT = 48 * 1024 * 1024
MESH_IDS = pl.DeviceIdType.MESH
HIGHEST = lax.Precision.HIGHEST


def _tile(n, cands=(512, 256, 128)):
    for c in cands:
        if n % c == 0:
            return c
    return n


def _cparams(*sem):
    return pltpu.CompilerParams(dimension_semantics=sem, vmem_limit_bytes=VMEM_LIMIT)


def _mm(a, b, *, ta=False, tb=False, add=None, out_dtype=F32, name, b_rows=None, out_rows=None, row_off=0, into=None):
    m, k = (a.shape[1], a.shape[0]) if ta else a.shape
    b_shape = b.shape if b_rows is None else (b_rows[1], b.shape[1])
    n = b_shape[0] if tb else b_shape[1]
    assert k == (b_shape[1] if tb else b_shape[0])
    if ta:
        tm, tn = _tile(m), n if n <= 1024 else _tile(n)
        tk = k if k <= 4096 else _tile(k, (4096, 2048, 1024, 512))
        if tm * tk > 1024 * 2048:
            tk = _tile(k, (2048, 1024, 512))
    else:
        tk = k if k <= 1024 else _tile(k, (1024, 512))
        tn = _tile(n, (1024 if tk <= 1024 else 512, 512, 256, 128))
        tm = _tile(m, (2048 if (tn <= 512 and tk <= 1024) else 1024, 1024, 512, 256, 128))
    nk = k // tk
    dims = (((0 if ta else 1,), (1 if tb else 0,)), ((), ()))
    has_add = add is not None

    def body(*refs):
        a_ref, b_ref = refs[0], refs[1]
        add_ref = refs[2] if has_add else None
        o_ref = refs[2 + has_add + (into is not None)]
        part = lax.dot_general(a_ref[...].astype(MXU_DTYPE), b_ref[...].astype(MXU_DTYPE), dims,
                               preferred_element_type=F32)

        def finish(acc):
            if has_add:
                acc = acc + add_ref[...]
            o_ref[...] = acc.astype(o_ref.dtype)

        if nk == 1:
            finish(part)
        else:
            acc_ref = refs[-1]
            kk = pl.program_id(2)

            @pl.when(kk == 0)
            def _():
                acc_ref[...] = part

            @pl.when(kk > 0)
            def _():
                acc_ref[...] += part

            @pl.when(kk == nk - 1)
            def _():
                finish(acc_ref[...])

    a_spec = pl.BlockSpec((tk, tm), lambda i, j, kk: (kk, i)) if ta else pl.BlockSpec((tm, tk), lambda i, j, kk: (i, kk))
    if b_rows is None:
        b_spec = pl.BlockSpec((tn, tk), lambda i, j, kk: (j, kk)) if tb else pl.BlockSpec((tk, tn), lambda i, j, kk: (kk, j))
    else:
        at = lambda t: pl.multiple_of(b_rows[0] + t, ROW_ALIGN)
        b_spec = (pl.BlockSpec((pl.Element(tn), pl.Element(tk)), lambda i, j, kk: (at(j * tn), kk * tk)) if tb else
                  pl.BlockSpec((pl.Element(tk), pl.Element(tn)), lambda i, j, kk: (at(kk * tk), j * tn)))
    add_spec = pl.BlockSpec((tm, tn), lambda i, j, kk: (i, j))
    assert row_off % tm == 0
    o_spec = pl.BlockSpec((tm, tn), lambda i, j, kk: (i + row_off // tm, j))
    in_specs = [a_spec, b_spec] + ([add_spec] if has_add else []) + ([pl.BlockSpec(memory_space=pl.ANY)] if into is not None else [])
    args = [a, b] + ([add] if has_add else []) + ([into] if into is not None else [])
    return pl.pallas_call(
        body, name=name, grid=(m // tm, n // tn, nk), in_specs=in_specs, out_specs=o_spec,
        out_shape=jax.ShapeDtypeStruct((out_rows or m, n), out_dtype),
        scratch_shapes=[pltpu.VMEM((tm, tn), F32)] if nk > 1 else [],
        input_output_aliases={len(args) - 1: 0} if into is not None else {},
        compiler_params=_cparams("parallel", "parallel", "arbitrary"),
    )(*args)


def _with_off(xs):
    return [x if isinstance(x, tuple) else (x, 0) for x in xs]


def _spec(kind, arr, off, ts, wb):
    w = arr.shape[-1] if wb is None else wb
    col = (lambda j: 0) if wb is None else functools.partial(lambda j, o: o + j, o=off)
    if kind == "tok":
        return pl.BlockSpec((None, ts, w), lambda j, b, i: (b, i, col(j)))
    if kind == "bat":
        return pl.BlockSpec((None, 1, w), lambda j, b, i: (b, 0, col(j)))
    if off is None:
        return pl.BlockSpec(arr.shape, lambda j, b, i: (0, 0))
    return pl.BlockSpec((arr.shape[0], w), lambda j, b, i: (0, col(j)))


def _in_specs(toks, bats, pars, cots, ts, wb):
    return ([_spec("tok", a, o, ts, wb) for a, o in toks] + [_spec("bat", a, o, ts, wb) for a, o in bats]
            + [_spec("par", a, o, ts, wb) for a, o in pars] + [_spec("tok", a, o, ts, wb) for a, o in cots])


def _tok_fwd(fn, toks, bats, pars, outs, *, name, ts, wb=None, cols=1):
    toks, bats, pars = _with_off(toks), _with_off(bats), _with_off(pars)
    bl, s, _ = toks[0][0].shape
    ts = min(ts, s)
    n_in = len(toks) + len(bats) + len(pars)

    def body(*refs):
        res = fn(*[r[...].astype(F32) for r in refs[:n_in]])
        for r, val in zip(refs[n_in:], res):
            r[...] = val.astype(r.dtype)

    out_specs = [pl.BlockSpec((None, ts, w if wb is None else wb), lambda j, b, i: (b, i, j)) for w, _ in outs]
    return pl.pallas_call(
        body, name=name, grid=(cols, bl, s // ts), in_specs=_in_specs(toks, bats, pars, [], ts, wb),
        out_specs=out_specs, out_shape=[jax.ShapeDtypeStruct((bl, s, w), dt) for w, dt in outs],
        compiler_params=_cparams("parallel", "parallel", "parallel"),
    )(*[a for a, _ in toks + bats + pars])


def _accumulate(ref, val, first):
    @pl.when(first)
    def _():
        ref[...] = val

    @pl.when(jnp.logical_not(first))
    def _():
        ref[...] += val


def _tok_bwd(fn, toks, bats, pars, cots, need, *, name, ts, wb=None, cols=1, tok_dtype=F32, loss=False):
    toks, bats, pars, cots = _with_off(toks), _with_off(bats), _with_off(pars), _with_off(cots)
    bl, s, _ = toks[0][0].shape
    ts = min(ts, s)
    nt, nb, npar, nc = len(toks), len(bats), len(pars), len(cots)
    n_in = nt + nb + npar

    def body(*refs):
        j, b, i = pl.program_id(0), pl.program_id(1), pl.program_id(2)
        outs, vjp = jax.vjp(fn, *[r[...].astype(F32) for r in refs[:n_in]])
        o = n_in + nc
        if loss:
            ct = (jnp.ones_like(outs[0]),)
            tot = jnp.broadcast_to(jnp.sum(outs[0], keepdims=True), (1, LANE))
            _accumulate(refs[o], tot, jnp.logical_and(b == 0, i == 0))
            o += 1
        else:
            ct = tuple(r[...].astype(F32) for r in refs[n_in:n_in + nc])
        grads = vjp(ct)
        for t in range(nt):
            if need[t]:
                refs[o][...] = grads[t].astype(refs[o].dtype)
                o += 1
        for t in range(nb):
            _accumulate(refs[o], grads[nt + t], i == 0)
            o += 1
        for t in range(npar):
            first = jnp.logical_and(b == 0, i == 0)
            if pars[t][1] is None:
                first = jnp.logical_and(first, j == 0)
            _accumulate(refs[o], grads[nt + nb + t], first)
            o += 1

    full = lambda arr: arr.shape[-1] if wb is None else wb * cols
    blk = lambda arr: arr.shape[-1] if wb is None else wb
    out_specs, out_shape = [], []
    if loss:
        out_specs.append(pl.BlockSpec((1, LANE), lambda j, b, i: (0, 0)))
        out_shape.append(jax.ShapeDtypeStruct((1, LANE), F32))
    for t in range(nt):
        if need[t]:
            out_specs.append(pl.BlockSpec((None, ts, blk(toks[t][0])), lambda j, b, i: (b, i, j)))
            dt = tok_dtype[t] if isinstance(tok_dtype, (list, tuple)) else tok_dtype
            out_shape.append(jax.ShapeDtypeStruct((bl, s, full(toks[t][0])), dt))
    for arr, _ in bats:
        out_specs.append(pl.BlockSpec((None, 1, blk(arr)), lambda j, b, i: (b, 0, j)))
        out_shape.append(jax.ShapeDtypeStruct((bl, 1, full(arr)), F32))
    for arr, off in pars:
        if off is None:
            out_specs.append(pl.BlockSpec(arr.shape, lambda j, b, i: (0, 0)))
            out_shape.append(jax.ShapeDtypeStruct(arr.shape, F32))
        else:
            out_specs.append(pl.BlockSpec((arr.shape[0], blk(arr)), lambda j, b, i: (0, j)))
            out_shape.append(jax.ShapeDtypeStruct((arr.shape[0], full(arr)), F32))
    res = list(pl.pallas_call(
        body, name=name, grid=(cols, bl, s // ts), in_specs=_in_specs(toks, bats, pars, cots, ts, wb),
        out_specs=out_specs, out_shape=out_shape, compiler_params=_cparams("arbitrary", "arbitrary", "arbitrary"),
    )(*[a for a, _ in toks + bats + pars + cots]))
    tot = res.pop(0) if loss else None
    dtoks = [res.pop(0) if need[t] else None for t in range(nt)]
    dbats = [res.pop(0) for _ in range(nb)]
    dpars = [res.pop(0) for _ in range(npar)]
    return (tot, dtoks, dbats, dpars) if loss else (dtoks, dbats, dpars)


def _silu(x):
    return x * jax.nn.sigmoid(x)


def _rms(x, w):
    return x * lax.rsqrt(jnp.mean(x * x, axis=-1, keepdims=True) + EPS) * w


def _f_norm_mod(x, shift, scale, w):
    return (_rms(x, w) * (1.0 + scale) + shift,)


def _f_norm_mod_skip(x, shift, scale, w):
    return _rms(x, w) * (1.0 + scale) + shift, x


def _f_res_norm_mod(x, mix, gate, shift, scale, w):
    x2 = x + gate * mix
    return x2, _rms(x2, w) * (1.0 + scale) + shift


def _f_gates(p, a_log, dt_bias, *, heads):
    z = p + dt_bias
    g = -jnp.exp(a_log) * (jnp.maximum(z, 0.0) + jnp.log1p(jnp.exp(jnp.minimum(z, -z))))
    lane = lax.broadcasted_iota(jnp.int32, p.shape, 1)
    return (jnp.where(lane < heads, g, jax.nn.sigmoid(p)),)


def _f_gdn_out(o, z, w):
    return (_rms(o, w) * _silu(z),)


def _f_merge(ga, gb, ya, yb):
    return (jax.nn.sigmoid(ga) * ya + jax.nn.sigmoid(gb) * yb,)


def _f_swiglu(a, b):
    return (_silu(a) * b,)


def _f_loss(x2, ff, tgt, gate, shift, scale, w):
    y = _rms(x2 + gate * ff, w) * (1.0 + scale) + shift
    return (0.5 * jnp.mean(jnp.square(y - tgt), axis=-1, keepdims=True),)


def _shift_down(x, s):
    if s == 0:
        return x
    row = lax.broadcasted_iota(jnp.int32, x.shape, 0)
    return jnp.where(row >= s, pltpu.roll(x, s, 0), 0.0)


def _shift_up(x, s):
    if s == 0:
        return x
    n = x.shape[0]
    row = lax.broadcasted_iota(jnp.int32, x.shape, 0)
    return jnp.where(row < n - s, pltpu.roll(x, n - s, 0), 0.0)


def _conv(x, w):
    width = w.shape[0]
    acc = w[width - 1:width, :] * x
    for j in range(width - 1):
        acc = acc + w[j:j + 1, :] * _shift_down(x, width - 1 - j)
    return acc


def _conv_bwd(dy, x, w, dw_ref, first):
    width = w.shape[0]
    dx = w[width - 1:width, :] * dy
    for j in range(width - 1):
        dx = dx + w[j:j + 1, :] * _shift_up(dy, width - 1 - j)
    for j in range(width):
        row = jnp.sum(dy * _shift_down(x, width - 1 - j), axis=0, keepdims=True)
        _accumulate(dw_ref.at[j:j + 1, :], row, first)
    return dx


def _qkv_act(xc, is_v, scale):
    a = _silu(xc)
    nrm = a * lax.rsqrt(jnp.sum(a * a, axis=-1, keepdims=True) + EPS) * scale
    return jnp.where(is_v, a, nrm)


def _qkv_consts(j, heads):
    is_v = j >= 2 * heads
    scale = jnp.where(j < heads, HEAD ** -0.5, 1.0).astype(F32)
    return is_v, scale


def _qkv_fwd(p, w, heads, name, rider=None):
    bl, s, w3 = p.shape
    r_args, r_in, r_out, r_shape, r_sems = _ride_specs(rider)

    def body(*refs):
        (p_ref, w_ref, o_ref), hooks = _split_refs(refs, 2, 1, 0, rider)
        first, mid, last = _grid_marks(w3 // HEAD, bl)
        _hooks_before(hooks, first, mid)
        is_v, scale = _qkv_consts(pl.program_id(0), heads)
        o_ref[...] = _qkv_act(_conv(p_ref[...], w_ref[...]), is_v, scale)
        _hooks_after(hooks, last)

    blk = pl.BlockSpec((None, s, HEAD), lambda j, b: (b, 0, j))
    return pl.pallas_call(
        body, name=name, grid=(w3 // HEAD, bl), in_specs=[blk, pl.BlockSpec((w.shape[0], HEAD), lambda j, b: (0, j))] + r_in,
        out_specs=[blk] + r_out, out_shape=[jax.ShapeDtypeStruct(p.shape, F32)] + r_shape, scratch_shapes=r_sems,
        compiler_params=_cparams("arbitrary", "arbitrary"),
    )(p, w, *r_args)


def _qkv_bwd(p, w, dout, heads, name, rider=None):
    bl, s, w3 = p.shape
    r_args, r_in, r_out, r_shape, r_sems = _ride_specs(rider)

    def body(*refs):
        (p_ref, w_ref, d_ref, dp_ref, dw_ref), hooks = _split_refs(refs, 3, 2, 0, rider)
        first, mid, last = _grid_marks(w3 // HEAD, bl)
        _hooks_before(hooks, first, mid)
        is_v, scale = _qkv_consts(pl.program_id(0), heads)
        x, wv = p_ref[...], w_ref[...]
        _, vjp = jax.vjp(lambda xc: _qkv_act(xc, is_v, scale), _conv(x, wv))
        (dxc,) = vjp(d_ref[...])
        dp_ref[...] = _conv_bwd(dxc, x, wv, dw_ref, pl.program_id(1) == 0).astype(dp_ref.dtype)
        _hooks_after(hooks, last)

    blk = pl.BlockSpec((None, s, HEAD), lambda j, b: (b, 0, j))
    wblk = pl.BlockSpec((w.shape[0], HEAD), lambda j, b: (0, j))
    return pl.pallas_call(
        body, name=name, grid=(w3 // HEAD, bl), in_specs=[blk, wblk, blk] + r_in, out_specs=[blk, wblk] + r_out,
        out_shape=[jax.ShapeDtypeStruct(p.shape, MXU_DTYPE), jax.ShapeDtypeStruct(w.shape, F32)] + r_shape,
        scratch_shapes=r_sems, compiler_params=_cparams("arbitrary", "arbitrary"),
    )(p, w, dout, *r_args)


def _sc_specs(p, w):
    bl, s, w3 = p.shape
    nblk = w3 // 3 // LANE
    sec = lambda k: pl.BlockSpec((None, s, LANE), functools.partial(lambda j, b, k: (b, 0, k * nblk + j), k=k))
    return nblk, [sec(0), sec(1), sec(2)], pl.BlockSpec((w.shape[0], LANE), lambda j, b: (0, j)), \
        pl.BlockSpec((None, s, LANE), lambda j, b: (b, 0, j))


def _sc_fwd(p, w, name):
    bl, s, w3 = p.shape
    nblk, secs, wblk, oblk = _sc_specs(p, w)

    def body(b_ref, c_ref, x_ref, w_ref, o_ref):
        o_ref[...] = (b_ref[...] * _conv(c_ref[...] * x_ref[...], w_ref[...])).astype(o_ref.dtype)

    return pl.pallas_call(
        body, name=name, grid=(nblk, bl), in_specs=secs + [wblk], out_specs=oblk,
        out_shape=jax.ShapeDtypeStruct((bl, s, w3 // 3), MXU_DTYPE), compiler_params=_cparams("parallel", "parallel"),
    )(p, p, p, w)


def _sc_bwd(p, w, dout, name):
    bl, s, w3 = p.shape
    nblk, secs, wblk, oblk = _sc_specs(p, w)

    def body(b_ref, c_ref, x_ref, w_ref, d_ref, db_ref, dc_ref, dx_ref, dw_ref):
        gb, gc, xin, wv, d = b_ref[...], c_ref[...], x_ref[...], w_ref[...], d_ref[...]
        u = gc * xin
        db_ref[...] = (d * _conv(u, wv)).astype(db_ref.dtype)
        du = _conv_bwd(d * gb, u, wv, dw_ref, pl.program_id(1) == 0)
        dc_ref[...] = (du * xin).astype(dc_ref.dtype)
        dx_ref[...] = (du * gc).astype(dx_ref.dtype)

    act = jax.ShapeDtypeStruct((bl, s, w3 // 3), MXU_DTYPE)
    return pl.pallas_call(
        body, name=name, grid=(nblk, bl), in_specs=secs + [wblk, oblk], out_specs=[oblk, oblk, oblk, wblk],
        out_shape=[act, act, act, jax.ShapeDtypeStruct(w.shape, F32)], compiler_params=_cparams("arbitrary", "arbitrary"),
    )(p, p, p, w, dout)


def _bdot(a, b, ca, cb):
    return lax.dot_general(a.astype(MXU_DTYPE), b.astype(MXU_DTYPE), (((ca,), (cb,)), ((), ())),
                           preferred_element_type=F32)


def _hdot(a, b):
    return lax.dot_general(a, b, (((1,), (0,)), ((), ())), precision=HIGHEST, preferred_element_type=F32)


def _lane_col(x, idx):
    lane = lax.broadcasted_iota(jnp.int32, x.shape, 1)
    return jnp.sum(jnp.where(lane == idx, x, 0.0), axis=1, keepdims=True)


def _chunk_masks():
    r = lax.broadcasted_iota(jnp.int32, (CHUNK, CHUNK), 0)
    c = lax.broadcasted_iota(jnp.int32, (CHUNK, CHUNK), 1)
    return r == c, r >= c, r > c


def _dot3(a, b):
    ah, bh = a.astype(MXU_DTYPE), b.astype(MXU_DTYPE)
    al, bl = (a - ah.astype(F32)).astype(MXU_DTYPE), (b - bh.astype(F32)).astype(MXU_DTYPE)
    dot = lambda x, y: lax.dot_general(x, y, (((1,), (0,)), ((), ())), preferred_element_type=F32)
    return dot(ah, bh) + (dot(ah, bl) + dot(al, bh))


def _tri_inv_steps(low, eye):
    x = -low
    p = jnp.where(eye, 1.0, 0.0) + x
    span = 2
    while span < CHUNK:
        x = _dot3(x, x)
        yield
        p = p + _dot3(p, x)
        yield
        span *= 2
    return p


def _round_robin(gens):
    out, live = [None] * len(gens), list(range(len(gens)))
    while live:
        still = []
        for i in live:
            try:
                next(gens[i])
                still.append(i)
            except StopIteration as stop:
                out[i] = stop.value
        live = still
    return out


def _gdn_pre(q, k, v, gc, beta, masks):
    eye, causal, strict = masks
    gc_row = jnp.sum(jnp.where(eye, gc, 0.0), axis=0, keepdims=True)
    decay = jnp.where(causal, jnp.exp(jnp.where(causal, gc - gc_row, 0.0)), 0.0)
    eg = jnp.exp(gc)
    gl = gc[CHUNK - 1:CHUNK, :]
    kb, vb = k * beta, v * beta
    low = jnp.where(strict, _bdot(kb, k, 1, 1) * decay, 0.0)
    qk = jnp.where(causal, _bdot(q, k, 1, 1) * decay, 0.0)
    rest = jnp.exp(gl - gc)
    return dict(decay=decay, eg=eg, gl=gl, kb=kb, vb=vb, kbe=kb * eg, low=low, qk=qk, qg=q * eg, rest=rest, kdec=k * rest)


def _gdn_specs(qkv, gbeta, heads, rev):
    bl, s, w3 = qkv.shape
    d, n = w3 // 3, s // CHUNK
    at = (lambda c: n - 1 - c) if rev else (lambda c: c)
    assert d == heads * HEAD
    sec = pl.BlockSpec((None, CHUNK, w3), lambda b, c: (b, at(c), 0))
    gspec = pl.BlockSpec((None, CHUNK, LANE), lambda b, c: (b, at(c), 0))
    sspec = pl.BlockSpec((None, None, heads, HEAD, HEAD), lambda b, c: (b, at(c), 0, 0, 0))
    tspec = pl.BlockSpec((None, None, heads, CHUNK, CHUNK), lambda b, c: (b, at(c), 0, 0, 0))
    return bl, s, d, n, sec, gspec, sspec, tspec


def _grid_marks(bl, n):
    b, c = pl.program_id(0), pl.program_id(1)
    late = min(bl * n - 1, bl * n * 5 // 6)
    first = jnp.logical_and(b == 0, c == 0)
    mid = jnp.logical_and(b == late // n, c == late % n)
    return first, mid, jnp.logical_and(b == bl - 1, c == n - 1)


def _gdn_fwd(qkv, gbeta, heads, name, rider=None):
    bl, s, d, n, sec, gspec, sspec, tspec = _gdn_specs(qkv, gbeta, heads, False)
    r_args, r_in, r_out, r_shape, r_sems = _ride_specs(rider)

    def body(*refs):
        (x_ref, g_ref, o_ref, s_ref, t_ref, st_ref), hooks = _split_refs(refs, 2, 3, 1, rider)
        first, mid, last = _grid_marks(bl, n)
        _hooks_before(hooks, first, mid)

        @pl.when(pl.program_id(1) == 0)
        def _():
            st_ref[...] = jnp.zeros_like(st_ref)

        masks = _chunk_masks()
        eye, causal, _ = masks
        gblk = g_ref[...]
        gc_all = _hdot(jnp.where(causal, 1.0, 0.0), gblk)
        st_all = st_ref[...]

        def head(h):
            st = st_all[h]
            q, k, v = (x_ref[:, sec * d + h * HEAD:sec * d + (h + 1) * HEAD] for sec in range(3))
            pre = _gdn_pre(q, k, v, _lane_col(gc_all, h), _lane_col(gblk, heads + h), masks)
            yield
            t = yield from _tri_inv_steps(pre["low"], eye)
            u, w = _bdot(t, pre["vb"], 1, 0), _bdot(t, pre["kbe"], 1, 0)
            yield
            vnew = u - _bdot(w, st, 1, 0)
            yield
            out = _bdot(pre["qg"], st, 1, 0) + _bdot(pre["qk"], vnew, 1, 0)
            return out, t, st * jnp.exp(pre["gl"]) + _bdot(pre["kdec"], vnew, 0, 0)

        outs, ts, states = zip(*_round_robin([head(h) for h in range(heads)]))
        o_ref[...] = jnp.concatenate(outs, axis=1)
        s_ref[...] = st_all
        t_ref[...] = jnp.stack(ts)
        st_ref[...] = jnp.stack(states)
        _hooks_after(hooks, last)

    return pl.pallas_call(
        body, name=name, grid=(bl, n), in_specs=[sec, gspec] + r_in,
        out_specs=[pl.BlockSpec((None, CHUNK, d), lambda b, c: (b, c, 0)), sspec, tspec] + r_out,
        out_shape=[jax.ShapeDtypeStruct((bl, s, d), F32), jax.ShapeDtypeStruct((bl, n, heads, HEAD, HEAD), F32),
                   jax.ShapeDtypeStruct((bl, n, heads, CHUNK, CHUNK), F32)] + r_shape,
        scratch_shapes=[pltpu.VMEM((heads, HEAD, HEAD), F32)] + r_sems, compiler_params=_cparams("arbitrary", "arbitrary"),
    )(qkv, gbeta, *r_args)


def _gdn_bwd(qkv, gbeta, dout, s_all, t_all, heads, name, rider=None):
    bl, s, d, n, sec, gspec, sspec, tspec = _gdn_specs(qkv, gbeta, heads, True)
    ospec = pl.BlockSpec((None, CHUNK, d), lambda b, c: (b, n - 1 - c, 0))
    r_args, r_in, r_out, r_shape, r_sems = _ride_specs(rider)

    def body(*refs):
        (x_ref, g_ref, do_ref, s_ref, t_ref, dx_ref, dg_ref, ds_ref), hooks = _split_refs(refs, 5, 2, 1, rider)
        first, mid, last = _grid_marks(bl, n)
        _hooks_before(hooks, first, mid)

        @pl.when(pl.program_id(1) == 0)
        def _():
            ds_ref[...] = jnp.zeros_like(ds_ref)

        masks = _chunk_masks()
        eye, causal, strict = masks
        gblk = g_ref[...]
        gc_all = _hdot(jnp.where(causal, 1.0, 0.0), gblk)
        lane = lax.broadcasted_iota(jnp.int32, gblk.shape, 1)
        last_row = lax.broadcasted_iota(jnp.int32, (CHUNK, 1), 0) == CHUNK - 1
        rowsum = lambda a: jnp.sum(a, axis=1, keepdims=True)
        st_all, t_all_, ds_all = s_ref[...], t_ref[...], ds_ref[...]

        def head(h):
            sl = slice(h * HEAD, (h + 1) * HEAD)
            q, k, v = (x_ref[:, sec * d + h * HEAD:sec * d + (h + 1) * HEAD] for sec in range(3))
            do = do_ref[:, sl]
            beta = _lane_col(gblk, heads + h)
            st, t, dsn = st_all[h], t_all_[h], ds_all[h]
            pre = _gdn_pre(q, k, v, _lane_col(gc_all, h), beta, masks)
            decay, eg, kb, vb, kbe, low, qk, qg, kdec = (pre[x] for x in ("decay", "eg", "kb", "vb", "kbe", "low", "qk", "qg", "kdec"))
            egl = jnp.exp(pre["gl"])
            yield
            u, w = _bdot(t, vb, 1, 0), _bdot(t, kbe, 1, 0)
            yield
            vnew = u - _bdot(w, st, 1, 0)
            yield
            dkdec = _bdot(vnew, dsn, 1, 1)
            dvnew = _bdot(kdec, dsn, 1, 0) + _bdot(qk, do, 0, 0)
            dgl = jnp.sum(dsn * st, keepdims=True) * egl
            dqg = _bdot(do, st, 1, 1)
            dqk = jnp.where(causal, _bdot(do, vnew, 1, 1), 0.0)
            yield
            dw = -_bdot(dvnew, st, 1, 1)
            ds_new = dsn * egl + _bdot(qg, do, 0, 0) - _bdot(w, dvnew, 0, 0)
            yield
            dt = _bdot(dvnew, vb, 1, 1) + _bdot(dw, kbe, 1, 1)
            dvb, dkbe = _bdot(t, dvnew, 0, 0), _bdot(t, dw, 0, 0)
            yield
            inner = _bdot(dt, t, 1, 1)
            yield
            dlow = -jnp.where(strict, _bdot(t, inner, 0, 0), 0.0)
            da, db = dlow * decay, dqk * decay
            yield
            m = dlow * low + dqk * qk
            kdk = dkdec * kdec
            col_of_m = jnp.sum(jnp.where(eye, jnp.sum(m, axis=0, keepdims=True), 0.0), axis=1, keepdims=True)
            dgc = rowsum(m) - col_of_m + rowsum(dqg * qg) + rowsum(dkbe * kbe) - rowsum(kdk)
            dgc = dgc + jnp.where(last_row, dgl + jnp.sum(kdk, keepdims=True), 0.0)
            dkb = _bdot(da, k, 1, 0) + dkbe * eg
            yield
            dk = _bdot(da, kb, 0, 0) + _bdot(db, q, 0, 0) + dkdec * pre["rest"] + dkb * beta
            dq = _bdot(db, k, 1, 0) + dqg * eg
            dbeta = rowsum(dkb * k) + rowsum(dvb * v)
            return dq, dk, dvb * beta, jnp.where(lane == h, dgc, 0.0) + jnp.where(lane == heads + h, dbeta, 0.0), ds_new

        dqs, dks, dvs, dgs, dss = zip(*_round_robin([head(h) for h in range(heads)]))
        dx_ref[...] = jnp.concatenate(dqs + dks + dvs, axis=1)
        ds_ref[...] = jnp.stack(dss)
        dgb = dgs[0]
        for extra in dgs[1:]:
            dgb = dgb + extra
        upper = jnp.where(jnp.logical_or(eye, jnp.logical_not(causal)), 1.0, 0.0)
        dg_ref[...] = jnp.where(lane < heads, _hdot(upper, dgb), dgb)
        _hooks_after(hooks, last)

    return pl.pallas_call(
        body, name=name, grid=(bl, n), in_specs=[sec, gspec, ospec, sspec, tspec] + r_in, out_specs=[sec, gspec] + r_out,
        out_shape=[jax.ShapeDtypeStruct(qkv.shape, F32), jax.ShapeDtypeStruct((bl, s, LANE), F32)] + r_shape,
        scratch_shapes=[pltpu.VMEM((heads, HEAD, HEAD), F32)] + r_sems, compiler_params=_cparams("arbitrary", "arbitrary"),
    )(qkv, gbeta, dout, s_all, t_all, *r_args)


def _position():
    return lax.axis_index("x"), lax.axis_index("y"), lax.axis_index("c")


def _all_gather(x, *, name, hbm):
    space = pltpu.HBM if hbm else pltpu.VMEM

    def body(x_ref, out_ref, send_sems, recv_sems, local_sem):
        ax, ay, ac = _position()
        me, sibling = (ax, ay, ac), (ax, ay, 1 - ac)
        chips = [(1 - ax, ay), (ax, 1 - ay), (1 - ax, 1 - ay)]

        def slot(px, py, pc):
            return out_ref.at[4 * px + 2 * py + pc]

        def copy(k, block, to, src=None):
            return pltpu.make_async_remote_copy(
                src_ref=slot(*block) if src is None else src, dst_ref=slot(*block), send_sem=send_sems.at[k],
                recv_sem=recv_sems.at[k], device_id=to, device_id_type=MESH_IDS)

        mine = pltpu.make_async_copy(x_ref, slot(*me), local_sem)
        mine.start()
        first = [copy(0, me, sibling, src=x_ref)] + [copy(1 + j, me, (*chip, ac), src=x_ref) for j, chip in enumerate(chips)]
        for cp in first:
            cp.start()
        passed = [copy(4 + j, (*chip, ac), sibling) for j, chip in enumerate(chips)]
        for j, chip in enumerate(chips):
            copy(1 + j, (*chip, ac), me).wait_recv()
            passed[j].start()
        copy(0, sibling, me).wait_recv()
        for j, chip in enumerate(chips):
            copy(4 + j, (*chip, 1 - ac), me).wait_recv()
        for cp in first + passed:
            cp.wait_send()
        mine.wait()

    return pl.pallas_call(
        body, name=name, out_shape=jax.ShapeDtypeStruct((NDEV,) + x.shape, x.dtype),
        in_specs=[pl.BlockSpec(memory_space=space)], out_specs=pl.BlockSpec(memory_space=space),
        scratch_shapes=[pltpu.SemaphoreType.DMA((7,)), pltpu.SemaphoreType.DMA((7,)), pltpu.SemaphoreType.DMA],
    )(x)


class _Rider:
    def __init__(self, arrays, out_shapes, sems, hooks):
        self.arrays, self.out_shapes, self.sems, self.hooks = arrays, out_shapes, sems, hooks


def _split_refs(refs, n_in, n_out, n_scratch, rider):
    r_in = len(rider.arrays) if rider else 0
    r_out = len(rider.out_shapes) if rider else 0
    o = n_in + r_in
    o2 = o + n_out + r_out
    host = refs[:n_in] + refs[o:o + n_out] + refs[o2:o2 + n_scratch]
    if rider is None:
        return host, None
    return host, rider.hooks(refs[n_in:o], refs[o + n_out:o2], *refs[o2 + n_scratch:])


def _hooks_before(hooks, first, mid):
    if hooks is not None:
        pl.when(first)(hooks[0])
        pl.when(mid)(hooks[1])


def _hooks_after(hooks, last):
    if hooks is not None:
        pl.when(last)(hooks[2])


def _ride_specs(rider):
    hbm = pl.BlockSpec(memory_space=pltpu.HBM)
    if rider is None:
        return [], [], [], [], []
    n_out = len(rider.out_shapes)
    return list(rider.arrays), [hbm] * len(rider.arrays), [hbm] * n_out, list(rider.out_shapes), list(rider.sems)


def _gather_rider(xs):
    n = len(xs)

    def hooks(x_refs, out_refs, send_sems, recv_sems):
        ax, ay, ac = _position()
        me, sibling = (ax, ay, ac), (ax, ay, 1 - ac)
        chips = [(1 - ax, ay), (ax, 1 - ay), (1 - ax, 1 - ay)]

        def copies(k, block, to, own=False):
            out = []
            for i in range(n):
                slot = out_refs[i].at[4 * block[0] + 2 * block[1] + block[2]]
                out.append(pltpu.make_async_remote_copy(
                    src_ref=x_refs[i] if own else slot, dst_ref=slot, send_sem=send_sems.at[k, i], recv_sem=recv_sems.at[k, i],
                    device_id=to, device_id_type=MESH_IDS))
            return out

        def first():
            for cp in copies(0, me, sibling, own=True):
                cp.start()
            for j, chip in enumerate(chips):
                for cp in copies(1 + j, me, (*chip, ac), own=True):
                    cp.start()

        def mid():
            for j, chip in enumerate(chips):
                for arrived, onward in zip(copies(1 + j, (*chip, ac), me), copies(4 + j, (*chip, ac), sibling)):
                    arrived.wait_recv()
                    onward.start()

        def last():
            for cp in copies(0, sibling, me):
                cp.wait_recv()
            for j, chip in enumerate(chips):
                for cp in copies(4 + j, (*chip, 1 - ac), me):
                    cp.wait_recv()
            for cp in copies(0, me, sibling, own=True):
                cp.wait_send()
            for j, chip in enumerate(chips):
                for cp in copies(1 + j, me, (*chip, ac), own=True) + copies(4 + j, (*chip, ac), sibling):
                    cp.wait_send()

        return first, mid, last

    return _Rider(list(xs), [jax.ShapeDtypeStruct((NDEV,) + x.shape, x.dtype) for x in xs],
                  [pltpu.SemaphoreType.DMA((7, n)), pltpu.SemaphoreType.DMA((7, n))], hooks)


def _scatter_rider(parts):
    packed = sum(r for _, r in parts)
    width, dtype = parts[0][0].shape[1], parts[0][0].dtype

    def hooks(g_refs, out_refs, send_sems, recv_sems):
        (recv_ref,) = out_refs
        ax, ay, ac = _position()

        def peer(rel):
            flip = lambda a, bit: 1 - a if rel & bit else a
            return flip(ax, 4), flip(ay, 2), flip(ac, 1)

        def first():
            for rel in range(1, NDEV):
                px, py, pc = peer(rel)
                off = 0
                for g_ref, (_, r) in zip(g_refs, parts):
                    rows = g_ref.at[pl.ds(pl.multiple_of((4 * px + 2 * py + pc) * r, ROW_ALIGN), r)]
                    pltpu.make_async_remote_copy(
                        src_ref=rows, dst_ref=recv_ref.at[rel - 1, pl.ds(off, r)], send_sem=send_sems.at[rel - 1],
                        recv_sem=recv_sems.at[rel - 1], device_id=(px, py, pc), device_id_type=MESH_IDS).start()
                    off += r

        def last():
            for rel in range(1, NDEV):
                slot = recv_ref.at[rel - 1]
                pltpu.make_async_remote_copy(src_ref=slot, dst_ref=slot, send_sem=send_sems.at[rel - 1],
                                             recv_sem=recv_sems.at[rel - 1], device_id=peer(rel), device_id_type=MESH_IDS).wait()

        return first, lambda: None, last

    return _Rider([g for g, _ in parts], [jax.ShapeDtypeStruct((NDEV - 1, packed, width), dtype)],
                  [pltpu.SemaphoreType.DMA((NDEV - 1,)), pltpu.SemaphoreType.DMA((NDEV - 1,))], hooks)


def _sum_direct(own, recv, name):
    r, w = own.shape
    tr = max(t for t in range(ROW_ALIGN, 257, ROW_ALIGN) if r % t == 0)

    def body(own_ref, *refs):
        acc = own_ref[...].astype(F32)
        for ref in refs[:-1]:
            acc = acc + ref[...].astype(F32)
        refs[-1][...] = acc

    rblk = lambda k: pl.BlockSpec((None, tr, w), functools.partial(lambda i, k: (k, i, 0), k=k))
    blk = pl.BlockSpec((tr, w), lambda i: (i, 0))
    return pl.pallas_call(body, name=name, grid=(r // tr,), in_specs=[blk] + [rblk(k) for k in range(NDEV - 1)],
                          out_specs=blk, out_shape=jax.ShapeDtypeStruct((r, w), F32),
                          compiler_params=_cparams("parallel"))(own, *([recv] * (NDEV - 1)))


ROW_ALIGN = 16


def _window_start(rows_per_dev, k):
    return rows_per_dev * k // ROW_ALIGN * ROW_ALIGN


def _exchange_in_chip(parts, name):
    n = len(parts)
    packed = sum(win for _, _, win, _ in parts)
    width, dtype = parts[0][0].shape[1], parts[0][0].dtype

    def body(*refs):
        g_refs, recv_ref, send_sems, recv_sems = refs[:n], *refs[n:]
        ax, ay, ac = _position()
        sibling = (ax, ay, 1 - ac)
        for q in range(4):
            for g_ref, (_, r, win, off) in zip(g_refs, parts):
                there = g_ref.at[pl.ds(pl.multiple_of(_window_start(r, 2 * q + 1 - ac), ROW_ALIGN), win)]
                pltpu.make_async_remote_copy(src_ref=there, dst_ref=recv_ref.at[q, pl.ds(off, win)], send_sem=send_sems.at[q],
                                             recv_sem=recv_sems.at[q], device_id=sibling, device_id_type=MESH_IDS).start()
        for q in range(4):
            pltpu.make_async_remote_copy(src_ref=recv_ref.at[q], dst_ref=recv_ref.at[q], send_sem=send_sems.at[q],
                                         recv_sem=recv_sems.at[q], device_id=sibling, device_id_type=MESH_IDS).wait()

    hbm = pl.BlockSpec(memory_space=pltpu.HBM)
    return pl.pallas_call(
        body, name=name, out_shape=jax.ShapeDtypeStruct((4, packed, width), dtype), in_specs=[hbm] * n, out_specs=hbm,
        scratch_shapes=[pltpu.SemaphoreType.DMA((4,)), pltpu.SemaphoreType.DMA((4,))],
    )(*[g for g, _, _, _ in parts])


def _exchange_chips(s1, name):
    def body(s_ref, recv_ref, send_sems, recv_sems):
        ax, ay, ac = _position()
        chips = [(1 - ax, ay), (ax, 1 - ay), (1 - ax, 1 - ay)]
        copies = [pltpu.make_async_remote_copy(
            src_ref=s_ref.at[2 * cx + cy], dst_ref=recv_ref.at[r], send_sem=send_sems.at[r], recv_sem=recv_sems.at[r],
            device_id=(cx, cy, ac), device_id_type=MESH_IDS) for r, (cx, cy) in enumerate(chips)]
        for cp in copies:
            cp.start()
        for cp in copies:
            cp.wait_recv()
        for cp in copies:
            cp.wait_send()

    hbm = pl.BlockSpec(memory_space=pltpu.HBM)
    return pl.pallas_call(
        body, name=name, out_shape=jax.ShapeDtypeStruct((3,) + s1.shape[1:], s1.dtype), in_specs=[hbm], out_specs=hbm,
        scratch_shapes=[pltpu.SemaphoreType.DMA((3,)), pltpu.SemaphoreType.DMA((3,))],
    )(s1)


def _on_sequencer(body, ins, out_shapes, sems, *, name, collective_id):
    hbm = pltpu.MemorySpace.HBM
    in_refs = [jax.new_ref(a, memory_space=hbm) for a in ins]
    out_refs = [jax.empty_ref(s, memory_space=hbm) for s in out_shapes]

    @pl.kernel(mesh=plsc.ScalarSubcoreMesh(axis_name="sequencer", num_cores=1), name=name, scratch_types=tuple(sems),
               compiler_params=pltpu.CompilerParams(collective_id=collective_id))
    def launch(*sem_refs):
        body(in_refs, out_refs, *sem_refs)

    launch()
    return [r[...] for r in out_refs]


def _handshake(peers):
    barrier = pltpu.get_barrier_semaphore()
    for peer in peers:
        pl.semaphore_signal(barrier, inc=1, device_id=peer, device_id_type=MESH_IDS)
    pl.semaphore_wait(barrier, len(peers))


def _exchange_chips_async(s1, name, collective_id):
    def body(in_refs, out_refs, send_sems, recv_sems):
        (src,), (got,) = in_refs, out_refs
        ax, ay, ac = _position()
        chips = [(1 - ax, ay), (ax, 1 - ay), (1 - ax, 1 - ay)]
        _handshake([(cx, cy, ac) for cx, cy in chips])
        copies = [pltpu.make_async_remote_copy(
            src_ref=src.at[2 * cx + cy], dst_ref=got.at[r], send_sem=send_sems.at[r], recv_sem=recv_sems.at[r],
            device_id=(cx, cy, ac), device_id_type=MESH_IDS) for r, (cx, cy) in enumerate(chips)]
        for cp in copies:
            cp.start()
        for cp in copies:
            cp.wait_recv()
        for cp in copies:
            cp.wait_send()

    return _on_sequencer(body, [s1], [jax.ShapeDtypeStruct((3,) + s1.shape[1:], s1.dtype)],
                         [pltpu.SemaphoreType.DMA((3,)), pltpu.SemaphoreType.DMA((3,))], name=name, collective_id=collective_id)[0]


def _gather_async(xs, name, collective_id):
    rider = _gather_rider(xs)

    def body(in_refs, out_refs, send_sems, recv_sems):
        ax, ay, ac = _position()
        _handshake([(ax, ay, 1 - ac), (1 - ax, ay, ac), (ax, 1 - ay, ac), (1 - ax, 1 - ay, ac)])
        for hook in rider.hooks(in_refs, out_refs, send_sems, recv_sems):
            hook()

    return _on_sequencer(body, rider.arrays, rider.out_shapes, rider.sems, name=name, collective_id=collective_id)


def _scatter_async(parts, name, collective_id):
    rider = _scatter_rider(parts)

    def body(in_refs, out_refs, send_sems, recv_sems):
        ax, ay, ac = _position()
        flip = lambda a, on: 1 - a if on else a
        _handshake([(flip(ax, rel & 4), flip(ay, rel & 2), flip(ac, rel & 1)) for rel in range(1, NDEV)])
        for hook in rider.hooks(in_refs, out_refs, send_sems, recv_sems):
            hook()

    return _on_sequencer(body, rider.arrays, rider.out_shapes, rider.sems, name=name, collective_id=collective_id)[0]


def _sum_in_chip(own, recv, name):
    _, r, w = own.shape
    tr = _tile(r, (256, 128))

    def body(a_ref, b_ref, o_ref):
        o_ref[...] = (a_ref[...].astype(F32) + b_ref[...].astype(F32)).astype(o_ref.dtype)

    blk = pl.BlockSpec((None, tr, w), lambda q, i: (q, i, 0))
    return pl.pallas_call(body, name=name, grid=(4, r // tr), in_specs=[blk, blk], out_specs=blk,
                          out_shape=jax.ShapeDtypeStruct(own.shape, own.dtype),
                          compiler_params=_cparams("parallel", "parallel"))(own, recv)


def _sum_chips(s1, recv, chip, name):
    _, r, w = s1.shape
    tr = _tile(r, (256, 128))

    def body(c_ref, s_ref, r0_ref, r1_ref, r2_ref, o_ref):
        f = lambda ref: ref[...].astype(F32)
        o_ref[...] = ((f(s_ref) + f(r0_ref)) + f(r1_ref)) + f(r2_ref)

    rblk = lambda k: pl.BlockSpec((None, tr, w), functools.partial(lambda i, c, k: (k, i, 0), k=k))
    grid_spec = pltpu.PrefetchScalarGridSpec(
        num_scalar_prefetch=1, grid=(r // tr,),
        in_specs=[pl.BlockSpec((None, tr, w), lambda i, c: (c[0], i, 0)), rblk(0), rblk(1), rblk(2)],
        out_specs=pl.BlockSpec((tr, w), lambda i, c: (i, 0)))
    return pl.pallas_call(body, name=name, grid_spec=grid_spec, out_shape=jax.ShapeDtypeStruct((r, w), F32),
                          compiler_params=_cparams("parallel"))(chip, s1, recv, recv, recv)


def _silu_rows(x, name):
    def body(x_ref, o_ref):
        o_ref[...] = _silu(x_ref[...])

    return pl.pallas_call(body, name=name, out_shape=jax.ShapeDtypeStruct(x.shape, F32))(x)


def _row_sum(x, name):
    def body(x_ref, o_ref):
        acc = x_ref[0:1, :]
        for i in range(1, x.shape[0]):
            acc = acc + x_ref[i:i + 1, :]
        o_ref[...] = acc

    return pl.pallas_call(body, name=name, out_shape=jax.ShapeDtypeStruct((1, x.shape[1]), F32))(x)


def _adamw(w, g, m, v, name):
    cols = w.shape[-1]
    rows = w.size // cols
    tr = _tile(rows, (128,))
    tc = LANE if (tr == rows and rows > 512 and cols % LANE == 0) else cols

    def body(w_ref, g_ref, m_ref, v_ref, d_ref, mo_ref, vo_ref):
        grad = g_ref[...]
        m_new = ADAM_B1 * m_ref[...] + (1.0 - ADAM_B1) * grad
        v_new = ADAM_B2 * v_ref[...] + (1.0 - ADAM_B2) * jnp.square(grad)
        m_hat = m_new / (1.0 - ADAM_B1 ** ADAM_STEP)
        v_hat = v_new / (1.0 - ADAM_B2 ** ADAM_STEP)
        d_ref[...] = -ADAM_LR * (m_hat / (jnp.sqrt(v_hat) + ADAM_EPS) + ADAM_WD * w_ref[...])
        mo_ref[...] = m_new
        vo_ref[...] = v_new

    blk = pl.BlockSpec((tr, tc), lambda i, j: (i, j))
    out = pl.pallas_call(
        body, name=name, grid=(rows // tr, cols // tc), in_specs=[blk] * 4, out_specs=[blk] * 3,
        out_shape=[jax.ShapeDtypeStruct((rows, cols), F32)] * 3, compiler_params=_cparams("parallel", "parallel"),
    )(*[t.reshape(rows, cols) for t in (w, g, m, v)])
    return [t.reshape(w.shape) for t in out]


def _pack(parts, width, row_mult, dtype):
    flat = jnp.concatenate([p.reshape(-1).astype(dtype) for p in parts])
    rows = -(-flat.shape[0] // (width * row_mult)) * row_mult
    return jnp.pad(flat, (0, rows * width - flat.shape[0])).reshape(rows, width)


def _unpack(flat, shapes):
    out, off = [], 0
    for shp in shapes:
        size = 1
        for dim in shp:
            size *= dim
        out.append(flat[:, off:off + size].reshape((flat.shape[0],) + tuple(shp)))
        off += size
    return out


def _devices_to_cols(a):
    _, r, c = a.shape
    return a.transpose(1, 0, 2).reshape(r, NDEV * c)


def kernel(x, c, w_ada, b_ada, norm1_w, w_in, gdn_conv_w, gdn_a_log, gdn_dt_bias, gdn_norm_w, w_gdn_proj, sc_conv_w, w_sc_out, w_o, norm2_w, w_ffn_in, w_ffn_out, w_ada_f, b_ada_f, normf_w, loss_target, m_w_ada, m_b_ada, m_norm1_w, m_w_in, m_gdn_conv_w, m_gdn_a_log, m_gdn_dt_bias, m_gdn_norm_w, m_w_gdn_proj, m_sc_conv_w, m_w_sc_out, m_w_o, m_norm2_w, m_w_ffn_in, m_w_ffn_out, m_w_ada_f, m_b_ada_f, m_normf_w, v_w_ada, v_b_ada, v_norm1_w, v_w_in, v_gdn_conv_w, v_gdn_a_log, v_gdn_dt_bias, v_gdn_norm_w, v_w_gdn_proj, v_sc_conv_w, v_w_sc_out, v_w_o, v_norm2_w, v_w_ffn_in, v_w_ffn_out, v_w_ada_f, v_b_ada_f, v_normf_w):
    bl, s, d = x.shape
    heads = gdn_a_log.shape[-1]
    dff = w_ffn_out.shape[1] * NDEV
    tok = bl * s
    ax, ay, ac = _position()
    dev = 4 * ax + 2 * ay + ac
    as_tok = lambda a: a.reshape(bl, s, a.shape[-1])
    as_mat = lambda a: a.reshape(tok, a.shape[-1])

    small = _all_gather(_pack([c, gdn_conv_w, sc_conv_w], LANE, 8, F32), name="gather_cond", hbm=False)
    c_all, conv_w, sc_w = _unpack(small.reshape(NDEV, -1), [(bl, d), gdn_conv_w.shape[1:], sc_conv_w.shape[1:]])
    c_act = _silu_rows(c_all.reshape(NDEV * bl, d), "cond_silu")
    conv_w, sc_w = _devices_to_cols(conv_w), _devices_to_cols(sc_w)
    n_ada, n_adaf = w_ada.shape[-1], w_ada_f.shape[-1]
    bias = jnp.broadcast_to(lax.dynamic_slice_in_dim(b_ada, dev * n_ada, n_ada, axis=1), (NDEV * bl, n_ada))
    biasf = jnp.broadcast_to(lax.dynamic_slice_in_dim(b_ada_f.reshape(1, -1), dev * n_adaf, n_adaf, axis=1), (NDEV * bl, n_adaf))
    mod_cols = _mm(c_act, w_ada[0], add=bias, name="ada_cols")
    modf_cols = _mm(c_act, w_ada_f, add=biasf, name="adaf_cols")
    mods = _all_gather(jnp.concatenate([mod_cols, modf_cols], axis=1), name="gather_mod", hbm=False)
    mod_all = mods[:, :, :n_ada].transpose(1, 0, 2).reshape(NDEV * bl, NDEV * n_ada)
    modf_all = mods[:, :, n_ada:].transpose(1, 0, 2).reshape(NDEV * bl, NDEV * n_adaf)
    my_rows = lambda a: lax.dynamic_slice_in_dim(a, dev * bl, bl, axis=0)
    sh1, sc1, g1, sh2, sc2, g2 = [t.reshape(bl, 1, d) for t in jnp.split(my_rows(mod_all), 6, axis=1)]
    shf, scf = [t.reshape(bl, 1, d) for t in jnp.split(my_rows(modf_all), 2, axis=1)]

    late = [t.astype(MXU_DTYPE) for t in (w_gdn_proj[0], w_sc_out[0], w_o[0], w_ffn_in[0].T, w_ffn_out[0])]
    rows = [t.shape[0] for t in late] + [w_in.shape[-1]]
    offs = [sum(rows[:i]) for i in range(5)]
    in_rows = -(-rows[5] // ROW_ALIGN) * ROW_ALIGN
    in_send = jnp.pad(w_in[0].T.astype(MXU_DTYPE), ((0, in_rows - rows[5]), (0, 0)))
    with_own = lambda g, own: lax.dynamic_update_slice_in_dim(g, own[None], dev, axis=0)
    (wt_in,) = _gather_async([in_send], "gather_w_in", 1)
    wt_in = with_own(wt_in, in_send)[:, :rows[5], :].reshape(NDEV * rows[5], d)
    gathered = _gather_async(late[:3], "gather_mixer", 2) + _gather_async(late[3:], "gather_ffn", 3)
    wgp, wso, wo, wt_fi, wfo = [with_own(g, own).reshape(NDEV * own.shape[0], d) for g, own in zip(gathered, late)]
    o_z, o_ab, o_sc, o_ga, o_gb = 3 * d, 4 * d, 4 * d + 2 * heads, 7 * d + 2 * heads, 8 * d + 2 * heads
    s_qkv, s_z, s_sc, s_gate = (0, o_z), (o_z, d), (o_sc, 3 * d), (o_ga, 2 * d)
    wt_ab = jnp.pad(wt_in[o_ab:o_sc], ((0, LANE - 2 * heads), (0, 0)))

    n1w, n2w, nfw = norm1_w.reshape(1, d), norm2_w.reshape(1, d), normf_w.reshape(1, d)
    lanes = lambda a: jnp.pad(a.reshape(1, -1), ((0, 0), (0, LANE - a.size)))
    a_log, dt_bias, gnw = lanes(gdn_a_log), lanes(gdn_dt_bias), gdn_norm_w.reshape(1, HEAD)
    f_gates = functools.partial(_f_gates, heads=heads)
    (h1,) = _tok_fwd(_f_norm_mod, [x], [sh1, sc1], [n1w], [(d, MXU_DTYPE)], name="norm1", ts=512)
    h1m = as_mat(h1)
    p_qkv = as_tok(_mm(h1m, wt_in, tb=True, b_rows=s_qkv, name="in_qkv"))
    p_z = as_tok(_mm(h1m, wt_in, tb=True, b_rows=s_z, name="in_z"))
    p_ab = as_tok(_mm(h1m, wt_ab, tb=True, name="in_ab"))
    p_sc = as_tok(_mm(h1m, wt_in, tb=True, b_rows=s_sc, name="in_sc"))
    p_g = as_tok(_mm(h1m, wt_in, tb=True, b_rows=s_gate, name="in_gate"))
    (qkv,) = _qkv_fwd(p_qkv, conv_w, heads, "qkv_conv")
    (gbeta,) = _tok_fwd(f_gates, [p_ab], [], [a_log, dt_bias], [(LANE, F32)], name="gates", ts=512)
    o, s_all, t_all = _gdn_fwd(qkv, gbeta, heads, "gdn")
    (og,) = _tok_fwd(_f_gdn_out, [o, p_z], [], [(gnw, None)], [(d, MXU_DTYPE)], name="gdn_out", ts=2048, wb=HEAD, cols=heads)
    y_a = as_tok(_mm(as_mat(og), wgp, name="gdn_proj"))
    scp = _sc_fwd(p_sc, sc_w, "sc_conv")
    y_b = as_tok(_mm(as_mat(scp), wso, name="sc_out"))
    mcols = d // 512 if d % 512 == 0 else 1
    mwb = d // mcols
    merge_toks = [(p_g, 0), (p_g, mcols), y_a, y_b]
    (mrg,) = _tok_fwd(_f_merge, merge_toks, [], [], [(d, MXU_DTYPE)], name="merge", ts=1024, wb=mwb, cols=mcols)
    mix = as_tok(_mm(as_mat(mrg), wo, name="mix_out"))
    x2, h2 = _tok_fwd(_f_res_norm_mod, [x, mix], [g1, sh2, sc2], [n2w], [(d, F32), (d, MXU_DTYPE)], name="norm2", ts=512)
    gu = as_tok(_mm(as_mat(h2), wt_fi, tb=True, name="ffn_in"))
    fwb = _tile(dff, (256, 128))
    fcols = dff // fwb
    (act,) = _tok_fwd(_f_swiglu, [(gu, 0), (gu, fcols)], [], [], [(dff, MXU_DTYPE)], name="swiglu", ts=2048, wb=fwb, cols=fcols)
    ff = as_tok(_mm(as_mat(act), wfo, name="ffn_out"))

    loss_l, (dx2, dff_out, _), (dg2, dshf, dscf), (dnfw,) = _tok_bwd(
        _f_loss, [x2, ff, loss_target], [g2, shf, scf], [nfw], [], [True, True, False], name="loss", ts=512, loss=True,
        tok_dtype=[F32, MXU_DTYPE, None])
    dffm = as_mat(dff_out)
    dact = as_tok(_mm(dffm, wfo, tb=True, name="d_ffn_out"))
    gmm = functools.partial(_mm, ta=True, out_dtype=MXU_DTYPE)
    gw_ffn_out = gmm(as_mat(act), dffm, name="g_ffn_out")
    (dgu_a, dgu_b), _, _ = _tok_bwd(_f_swiglu, [(gu, 0), (gu, fcols)], [], [], [dact], [True, True], name="d_swiglu",
                                    ts=2048, wb=fwb, cols=fcols, tok_dtype=MXU_DTYPE)
    dh2 = _mm(as_mat(dgu_a), wt_fi, b_rows=(0, dff), name="d_ffn_in_a")
    dh2 = as_tok(_mm(as_mat(dgu_b), wt_fi, b_rows=(dff, dff), add=dh2, name="d_ffn_in_b"))
    h2m = as_mat(h2)
    gwt_ffn_in = gmm(as_mat(dgu_a), h2m, out_rows=2 * dff, name="g_ffn_in_a")
    gwt_ffn_in = gmm(as_mat(dgu_b), h2m, out_rows=2 * dff, row_off=dff, into=gwt_ffn_in, name="g_ffn_in_b")
    (dx_skip, dmix), (dg1, dsh2, dsc2), (dn2w,) = _tok_bwd(
        _f_res_norm_mod, [x, mix], [g1, sh2, sc2], [n2w], [dx2, dh2], [True, True], name="d_norm2", ts=256,
        tok_dtype=[F32, MXU_DTYPE])
    dmixm = as_mat(dmix)
    dmrg = as_tok(_mm(dmixm, wo, tb=True, name="d_mix_out"))
    gw_o = gmm(as_mat(mrg), dmixm, name="g_mix_out")
    (dga, dgb, dya, dyb), _, _ = _tok_bwd(_f_merge, merge_toks, [], [], [dmrg], [True] * 4, name="d_merge", ts=512,
                                          wb=mwb, cols=mcols, tok_dtype=MXU_DTYPE)
    dyam, dybm = as_mat(dya), as_mat(dyb)
    dog = as_tok(_mm(dyam, wgp, tb=True, name="d_gdn_proj"))
    gw_gdn_proj = gmm(as_mat(og), dyam, name="g_gdn_proj")
    dscp = as_tok(_mm(dybm, wso, tb=True, name="d_sc_out"))
    gw_sc_out = gmm(as_mat(scp), dybm, name="g_sc_out")
    dscb, dscc, dscx, g_sc_w = _sc_bwd(p_sc, sc_w, dscp, "d_sc_conv")
    (do, dz), _, (g_gnw,) = _tok_bwd(_f_gdn_out, [o, p_z], [], [(gnw, None)], [dog], [True, True], name="d_gdn_out",
                                     ts=2048, wb=HEAD, cols=heads, tok_dtype=[F32, MXU_DTYPE])
    ffn_parts, mix_parts = [(gwt_ffn_in, rows[3]), (gw_ffn_out, rows[4])], [(gw_gdn_proj, rows[0]), (gw_sc_out, rows[1]), (gw_o, rows[2])]
    own_rows = lambda parts: jnp.concatenate([lax.dynamic_slice_in_dim(g, dev * r, r, axis=0) for g, r in parts], axis=0)
    ffn_recv = _scatter_async(ffn_parts, "scatter_ffn", 4)
    mix_recv = _scatter_async(mix_parts, "scatter_mixer", 5)
    dqkv, dgbeta = _gdn_bwd(qkv, gbeta, do, s_all, t_all, heads, "d_gdn")
    dp_qkv, g_conv_w = _qkv_bwd(p_qkv, conv_w, dqkv, heads, "d_qkv_conv")
    ffn_red = _sum_direct(own_rows(ffn_parts), ffn_recv, "sum_ffn")
    mix_red = _sum_direct(own_rows(mix_parts), mix_recv, "sum_mix")
    (dp_ab,), _, (g_a_log, g_dt_bias) = _tok_bwd(f_gates, [p_ab], [], [a_log, dt_bias], [dgbeta], [True], name="d_gates",
                                                 ts=512, tok_dtype=MXU_DTYPE)
    sections = [(dp_qkv, s_qkv), (dz, s_z), (dp_ab, None), (dscb, (o_sc, d)), (dscc, (o_sc + d, d)), (dscx, (o_sc + 2 * d, d)),
                (dga, (o_ga, d)), (dgb, (o_gb, d))]
    gwt_in = [gmm(as_mat(dp), h1m, name=f"g_in_{k}") for k, (dp, _) in enumerate(sections)]
    gwt_in[2] = gwt_in[2][:2 * heads]
    gwt_in = jnp.concatenate(gwt_in, axis=0)

    r_in = rows[5]
    win = -(-(r_in + max(r_in * k % ROW_ALIGN for k in range(NDEV))) // 128) * 128
    need_rows = max(_window_start(r_in, k) for k in range(NDEV)) + win
    gwt_in = jnp.pad(gwt_in, ((0, need_rows - gwt_in.shape[0]), (0, 0)))
    recv1 = _exchange_in_chip([(gwt_in, r_in, win, 0)], "scatter_in_chip")
    own = jnp.stack([lax.dynamic_slice_in_dim(gwt_in, _window_start(r_in, 2 * q + ac), win, axis=0) for q in range(4)])
    s1 = _sum_in_chip(own, recv1, "sum_in_chip")
    recv2 = _exchange_chips_async(s1, "scatter_chips", 6)

    dh1 = None
    for k, (dp, sec) in enumerate(sections):
        dh1 = _mm(as_mat(dp), wt_ab if sec is None else wt_in, b_rows=sec, add=dh1, name=f"d_in_{k}")
    (grad_x,), (dsh1, dsc1), (dn1w,) = _tok_bwd(_f_norm_mod_skip, [x], [sh1, sc1], [n1w], [as_tok(dh1), dx_skip], [True],
                                                name="d_norm1", ts=256)
    reduced = _sum_chips(s1, recv2, (2 * ax + ay).reshape(1).astype(jnp.int32), "sum_chips")
    gt_w_in = lax.dynamic_slice_in_dim(reduced, r_in * dev - _window_start(r_in, dev), r_in, axis=0)
    g_w_in = gt_w_in.T.reshape(w_in.shape)
    gt_w_ffn_in = ffn_red[:rows[3]]
    g_w_ffn_in = gt_w_ffn_in.T.reshape(w_ffn_in.shape)
    g_w_ffn_out = ffn_red[rows[3]:].reshape(w_ffn_out.shape)
    g_w_gdn_proj, g_w_sc_out, g_w_o = (mix_red[offs[i]:offs[i] + rows[i]].reshape(ref.shape)
                                       for i, ref in enumerate((w_gdn_proj, w_sc_out, w_o)))

    dmod = jnp.concatenate([t.reshape(bl, d) for t in (dsh1, dsc1, dg1, dsh2, dsc2, dg2)], axis=1)
    dmodf = jnp.concatenate([t.reshape(bl, d) for t in (dshf, dscf)], axis=1)
    summed_parts = [dn1w, dn2w, dnfw, g_gnw, g_a_log, g_dt_bias, g_conv_w, g_sc_w, loss_l]
    partial = _all_gather(_pack([dmod, dmodf] + summed_parts, LANE, 8, F32), name="gather_small", hbm=False)
    partial = partial.reshape(NDEV, -1)
    n_rows = bl * (6 * d + 2 * d)
    dmod_all, dmodf_all = _unpack(partial[:, :n_rows], [(bl, 6 * d), (bl, 2 * d)])
    dmod_all, dmodf_all = dmod_all.reshape(NDEV * bl, 6 * d), dmodf_all.reshape(NDEV * bl, 2 * d)
    totals = _row_sum(partial[:, n_rows:], "sum_small")
    t_n1w, t_n2w, t_nfw, t_gnw, t_a_log, t_dt_bias, t_conv_w, t_sc_w, t_loss = [
        t[0] for t in _unpack(totals, [p.shape for p in summed_parts])]
    my_cols = lambda a, n: lax.dynamic_slice_in_dim(a, dev * n, n, axis=1)
    grads = {
        "w_ada": _mm(c_act, my_cols(dmod_all, n_ada), ta=True, name="g_ada").reshape(w_ada.shape),
        "b_ada": _row_sum(dmod_all, "g_ada_bias").reshape(b_ada.shape),
        "norm1_w": t_n1w.reshape(norm1_w.shape),
        "w_in": g_w_in,
        "gdn_conv_w": my_cols(t_conv_w, gdn_conv_w.shape[-1]).reshape(gdn_conv_w.shape),
        "gdn_a_log": t_a_log[:, :heads].reshape(gdn_a_log.shape),
        "gdn_dt_bias": t_dt_bias[:, :heads].reshape(gdn_dt_bias.shape),
        "gdn_norm_w": t_gnw.reshape(gdn_norm_w.shape),
        "w_gdn_proj": g_w_gdn_proj,
        "sc_conv_w": my_cols(t_sc_w, sc_conv_w.shape[-1]).reshape(sc_conv_w.shape),
        "w_sc_out": g_w_sc_out,
        "w_o": g_w_o,
        "norm2_w": t_n2w.reshape(norm2_w.shape),
        "w_ffn_in": g_w_ffn_in,
        "w_ffn_out": g_w_ffn_out,
        "w_ada_f": _mm(c_act, my_cols(dmodf_all, n_adaf), ta=True, name="g_adaf").reshape(w_ada_f.shape),
        "b_ada_f": _row_sum(dmodf_all, "g_adaf_bias").reshape(b_ada_f.shape),
        "normf_w": t_nfw.reshape(normf_w.shape),
    }
    weights = dict(w_ada=w_ada, b_ada=b_ada, norm1_w=norm1_w, w_in=w_in, gdn_conv_w=gdn_conv_w, gdn_a_log=gdn_a_log,
                   gdn_dt_bias=gdn_dt_bias, gdn_norm_w=gdn_norm_w, w_gdn_proj=w_gdn_proj, sc_conv_w=sc_conv_w,
                   w_sc_out=w_sc_out, w_o=w_o, norm2_w=norm2_w, w_ffn_in=w_ffn_in, w_ffn_out=w_ffn_out, w_ada_f=w_ada_f,
                   b_ada_f=b_ada_f, normf_w=normf_w)
    m_in = [m_w_ada, m_b_ada, m_norm1_w, m_w_in, m_gdn_conv_w, m_gdn_a_log, m_gdn_dt_bias, m_gdn_norm_w, m_w_gdn_proj,
            m_sc_conv_w, m_w_sc_out, m_w_o, m_norm2_w, m_w_ffn_in, m_w_ffn_out, m_w_ada_f, m_b_ada_f, m_normf_w]
    v_in = [v_w_ada, v_b_ada, v_norm1_w, v_w_in, v_gdn_conv_w, v_gdn_a_log, v_gdn_dt_bias, v_gdn_norm_w, v_w_gdn_proj,
            v_sc_conv_w, v_w_sc_out, v_w_o, v_norm2_w, v_w_ffn_in, v_w_ffn_out, v_w_ada_f, v_b_ada_f, v_normf_w]
    deltas, new_m, new_v = [], [], []
    grads_t = {"w_in": gt_w_in, "w_ffn_in": gt_w_ffn_in}
    for (wname, wt), mt, vt in zip(weights.items(), m_in, v_in):
        if wname in grads_t:
            back = lambda a, wt=wt: a.T.reshape(wt.shape)
            dl, mn, vn = (back(a) for a in _adamw(wt[0].T, grads_t[wname], mt[0].T, vt[0].T, "adamw_" + wname))
        else:
            dl, mn, vn = _adamw(wt, grads[wname], mt, vt, "adamw_" + wname)
        deltas.append(dl)
        new_m.append(mn)
        new_v.append(vn)
    loss = t_loss[0, 0]
    return (loss, grad_x, *[grads[k] for k in weights], *deltas, *new_m, *new_v)
```

```python
import functools

import jax
import jax.numpy as jnp
from jax import lax
from jax.experimental import pallas as pl
from jax.experimental.pallas import tpu as pltpu
from jax.experimental.pallas import tpu_sc as plsc

F32 = jnp.float32
MXU_DTYPE = jnp.bfloat16
NDEV = 8
CHUNK = 64
HEAD = 128
LANE = 128
EPS = 1e-6
ADAM_LR, ADAM_B1, ADAM_B2, ADAM_EPS, ADAM_WD, ADAM_STEP = 0.001, 0.9, 0.999, 1e-08, 0.01, 10
VMEM_LIMIT = 48 * 1024 * 1024
MESH_IDS = pl.DeviceIdType.MESH
HIGHEST = lax.Precision.HIGHEST


def _tile(n, cands=(512, 256, 128)):
    for c in cands:
        if n % c == 0:
            return c
    return n


def _cparams(*sem):
    return pltpu.CompilerParams(dimension_semantics=sem, vmem_limit_bytes=VMEM_LIMIT)


def _mm(a, b, *, ta=False, tb=False, add=None, out_dtype=F32, name, b_rows=None, out_rows=None, row_off=0, into=None):
    m, k = (a.shape[1], a.shape[0]) if ta else a.shape
    b_shape = b.shape if b_rows is None else (b_rows[1], b.shape[1])
    n = b_shape[0] if tb else b_shape[1]
    assert k == (b_shape[1] if tb else b_shape[0])
    if ta:
        tm, tn = _tile(m), n if n <= 1024 else _tile(n)
        tk = k if k <= 4096 else _tile(k, (4096, 2048, 1024, 512))
        if tm * tk > 1024 * 2048:
            tk = _tile(k, (2048, 1024, 512))
    else:
        tk = k if k <= 1024 else _tile(k, (1024, 512))
        tn = _tile(n, (1024 if tk <= 1024 else 512, 512, 256, 128))
        tm = _tile(m, (2048 if (tn <= 512 and tk <= 1024) else 1024, 1024, 512, 256, 128))
    nk = k // tk
    dims = (((0 if ta else 1,), (1 if tb else 0,)), ((), ()))
    has_add = add is not None

    def body(*refs):
        a_ref, b_ref = refs[0], refs[1]
        add_ref = refs[2] if has_add else None
        o_ref = refs[2 + has_add + (into is not None)]
        part = lax.dot_general(a_ref[...].astype(MXU_DTYPE), b_ref[...].astype(MXU_DTYPE), dims,
                               preferred_element_type=F32)

        def finish(acc):
            if has_add:
                acc = acc + add_ref[...]
            o_ref[...] = acc.astype(o_ref.dtype)

        if nk == 1:
            finish(part)
        else:
            acc_ref = refs[-1]
            kk = pl.program_id(2)

            @pl.when(kk == 0)
            def _():
                acc_ref[...] = part

            @pl.when(kk > 0)
            def _():
                acc_ref[...] += part

            @pl.when(kk == nk - 1)
            def _():
                finish(acc_ref[...])

    a_spec = pl.BlockSpec((tk, tm), lambda i, j, kk: (kk, i)) if ta else pl.BlockSpec((tm, tk), lambda i, j, kk: (i, kk))
    if b_rows is None:
        b_spec = pl.BlockSpec((tn, tk), lambda i, j, kk: (j, kk)) if tb else pl.BlockSpec((tk, tn), lambda i, j, kk: (kk, j))
    else:
        at = lambda t: pl.multiple_of(b_rows[0] + t, ROW_ALIGN)
        b_spec = (pl.BlockSpec((pl.Element(tn), pl.Element(tk)), lambda i, j, kk: (at(j * tn), kk * tk)) if tb else
                  pl.BlockSpec((pl.Element(tk), pl.Element(tn)), lambda i, j, kk: (at(kk * tk), j * tn)))
    add_spec = pl.BlockSpec((tm, tn), lambda i, j, kk: (i, j))
    assert row_off % tm == 0
    o_spec = pl.BlockSpec((tm, tn), lambda i, j, kk: (i + row_off // tm, j))
    in_specs = [a_spec, b_spec] + ([add_spec] if has_add else []) + ([pl.BlockSpec(memory_space=pl.ANY)] if into is not None else [])
    args = [a, b] + ([add] if has_add else []) + ([into] if into is not None else [])
    return pl.pallas_call(
        body, name=name, grid=(m // tm, n // tn, nk), in_specs=in_specs, out_specs=o_spec,
        out_shape=jax.ShapeDtypeStruct((out_rows or m, n), out_dtype),
        scratch_shapes=[pltpu.VMEM((tm, tn), F32)] if nk > 1 else [],
        input_output_aliases={len(args) - 1: 0} if into is not None else {},
        compiler_params=_cparams("parallel", "parallel", "arbitrary"),
    )(*args)


def _with_off(xs):
    return [x if isinstance(x, tuple) else (x, 0) for x in xs]


def _spec(kind, arr, off, ts, wb):
    w = arr.shape[-1] if wb is None else wb
    col = (lambda j: 0) if wb is None else functools.partial(lambda j, o: o + j, o=off)
    if kind == "tok":
        return pl.BlockSpec((None, ts, w), lambda j, b, i: (b, i, col(j)))
    if kind == "bat":
        return pl.BlockSpec((None, 1, w), lambda j, b, i: (b, 0, col(j)))
    if off is None:
        return pl.BlockSpec(arr.shape, lambda j, b, i: (0, 0))
    return pl.BlockSpec((arr.shape[0], w), lambda j, b, i: (0, col(j)))


def _in_specs(toks, bats, pars, cots, ts, wb):
    return ([_spec("tok", a, o, ts, wb) for a, o in toks] + [_spec("bat", a, o, ts, wb) for a, o in bats]
            + [_spec("par", a, o, ts, wb) for a, o in pars] + [_spec("tok", a, o, ts, wb) for a, o in cots])


def _tok_fwd(fn, toks, bats, pars, outs, *, name, ts, wb=None, cols=1):
    toks, bats, pars = _with_off(toks), _with_off(bats), _with_off(pars)
    bl, s, _ = toks[0][0].shape
    ts = min(ts, s)
    n_in = len(toks) + len(bats) + len(pars)

    def body(*refs):
        res = fn(*[r[...].astype(F32) for r in refs[:n_in]])
        for r, val in zip(refs[n_in:], res):
            r[...] = val.astype(r.dtype)

    out_specs = [pl.BlockSpec((None, ts, w if wb is None else wb), lambda j, b, i: (b, i, j)) for w, _ in outs]
    return pl.pallas_call(
        body, name=name, grid=(cols, bl, s // ts), in_specs=_in_specs(toks, bats, pars, [], ts, wb),
        out_specs=out_specs, out_shape=[jax.ShapeDtypeStruct((bl, s, w), dt) for w, dt in outs],
        compiler_params=_cparams("parallel", "parallel", "parallel"),
    )(*[a for a, _ in toks + bats + pars])


def _accumulate(ref, val, first):
    @pl.when(first)
    def _():
        ref[...] = val

    @pl.when(jnp.logical_not(first))
    def _():
        ref[...] += val


def _tok_bwd(fn, toks, bats, pars, cots, need, *, name, ts, wb=None, cols=1, tok_dtype=F32, loss=False):
    toks, bats, pars, cots = _with_off(toks), _with_off(bats), _with_off(pars), _with_off(cots)
    bl, s, _ = toks[0][0].shape
    ts = min(ts, s)
    nt, nb, npar, nc = len(toks), len(bats), len(pars), len(cots)
    n_in = nt + nb + npar

    def body(*refs):
        j, b, i = pl.program_id(0), pl.program_id(1), pl.program_id(2)
        outs, vjp = jax.vjp(fn, *[r[...].astype(F32) for r in refs[:n_in]])
        o = n_in + nc
        if loss:
            ct = (jnp.ones_like(outs[0]),)
            tot = jnp.broadcast_to(jnp.sum(outs[0], keepdims=True), (1, LANE))
            _accumulate(refs[o], tot, jnp.logical_and(b == 0, i == 0))
            o += 1
        else:
            ct = tuple(r[...].astype(F32) for r in refs[n_in:n_in + nc])
        grads = vjp(ct)
        for t in range(nt):
            if need[t]:
                refs[o][...] = grads[t].astype(refs[o].dtype)
                o += 1
        for t in range(nb):
            _accumulate(refs[o], grads[nt + t], i == 0)
            o += 1
        for t in range(npar):
            first = jnp.logical_and(b == 0, i == 0)
            if pars[t][1] is None:
                first = jnp.logical_and(first, j == 0)
            _accumulate(refs[o], grads[nt + nb + t], first)
            o += 1

    full = lambda arr: arr.shape[-1] if wb is None else wb * cols
    blk = lambda arr: arr.shape[-1] if wb is None else wb
    out_specs, out_shape = [], []
    if loss:
        out_specs.append(pl.BlockSpec((1, LANE), lambda j, b, i: (0, 0)))
        out_shape.append(jax.ShapeDtypeStruct((1, LANE), F32))
    for t in range(nt):
        if need[t]:
            out_specs.append(pl.BlockSpec((None, ts, blk(toks[t][0])), lambda j, b, i: (b, i, j)))
            dt = tok_dtype[t] if isinstance(tok_dtype, (list, tuple)) else tok_dtype
            out_shape.append(jax.ShapeDtypeStruct((bl, s, full(toks[t][0])), dt))
    for arr, _ in bats:
        out_specs.append(pl.BlockSpec((None, 1, blk(arr)), lambda j, b, i: (b, 0, j)))
        out_shape.append(jax.ShapeDtypeStruct((bl, 1, full(arr)), F32))
    for arr, off in pars:
        if off is None:
            out_specs.append(pl.BlockSpec(arr.shape, lambda j, b, i: (0, 0)))
            out_shape.append(jax.ShapeDtypeStruct(arr.shape, F32))
        else:
            out_specs.append(pl.BlockSpec((arr.shape[0], blk(arr)), lambda j, b, i: (0, j)))
            out_shape.append(jax.ShapeDtypeStruct((arr.shape[0], full(arr)), F32))
    res = list(pl.pallas_call(
        body, name=name, grid=(cols, bl, s // ts), in_specs=_in_specs(toks, bats, pars, cots, ts, wb),
        out_specs=out_specs, out_shape=out_shape, compiler_params=_cparams("arbitrary", "arbitrary", "arbitrary"),
    )(*[a for a, _ in toks + bats + pars + cots]))
    tot = res.pop(0) if loss else None
    dtoks = [res.pop(0) if need[t] else None for t in range(nt)]
    dbats = [res.pop(0) for _ in range(nb)]
    dpars = [res.pop(0) for _ in range(npar)]
    return (tot, dtoks, dbats, dpars) if loss else (dtoks, dbats, dpars)


def _silu(x):
    return x * jax.nn.sigmoid(x)


def _rms(x, w):
    return x * lax.rsqrt(jnp.mean(x * x, axis=-1, keepdims=True) + EPS) * w


def _f_norm_mod(x, shift, scale, w):
    return (_rms(x, w) * (1.0 + scale) + shift,)


def _f_norm_mod_skip(x, shift, scale, w):
    return _rms(x, w) * (1.0 + scale) + shift, x


def _f_res_norm_mod(x, mix, gate, shift, scale, w):
    x2 = x + gate * mix
    return x2, _rms(x2, w) * (1.0 + scale) + shift


def _f_gates(p, a_log, dt_bias, *, heads):
    z = p + dt_bias
    g = -jnp.exp(a_log) * (jnp.maximum(z, 0.0) + jnp.log1p(jnp.exp(jnp.minimum(z, -z))))
    lane = lax.broadcasted_iota(jnp.int32, p.shape, 1)
    return (jnp.where(lane < heads, g, jax.nn.sigmoid(p)),)


def _f_gdn_out(o, z, w):
    return (_rms(o, w) * _silu(z),)


def _f_merge(ga, gb, ya, yb):
    return (jax.nn.sigmoid(ga) * ya + jax.nn.sigmoid(gb) * yb,)


def _f_swiglu(a, b):
    return (_silu(a) * b,)


def _f_loss(x2, ff, tgt, gate, shift, scale, w):
    y = _rms(x2 + gate * ff, w) * (1.0 + scale) + shift
    return (0.5 * jnp.mean(jnp.square(y - tgt), axis=-1, keepdims=True),)


def _shift_down(x, s):
    if s == 0:
        return x
    row = lax.broadcasted_iota(jnp.int32, x.shape, 0)
    return jnp.where(row >= s, pltpu.roll(x, s, 0), 0.0)


def _shift_up(x, s):
    if s == 0:
        return x
    n = x.shape[0]
    row = lax.broadcasted_iota(jnp.int32, x.shape, 0)
    return jnp.where(row < n - s, pltpu.roll(x, n - s, 0), 0.0)


def _conv(x, w):
    width = w.shape[0]
    acc = w[width - 1:width, :] * x
    for j in range(width - 1):
        acc = acc + w[j:j + 1, :] * _shift_down(x, width - 1 - j)
    return acc


def _conv_bwd(dy, x, w, dw_ref, first):
    width = w.shape[0]
    dx = w[width - 1:width, :] * dy
    for j in range(width - 1):
        dx = dx + w[j:j + 1, :] * _shift_up(dy, width - 1 - j)
    for j in range(width):
        row = jnp.sum(dy * _shift_down(x, width - 1 - j), axis=0, keepdims=True)
        _accumulate(dw_ref.at[j:j + 1, :], row, first)
    return dx


def _qk_act(xc, scale):
    a = _silu(xc)
    return a * lax.rsqrt(jnp.sum(a * a, axis=-1, keepdims=True) + EPS) * scale


def _qk_scale(j, heads):
    return jnp.where(j < heads, HEAD ** -0.5, 1.0).astype(F32)


def _qkv_fwd(p, w, heads, name):
    bl, s, w3 = p.shape

    def body(p_ref, w_ref, o_ref):
        j = pl.program_id(0)
        xc = _conv(p_ref[...], w_ref[...])

        @pl.when(j < 2 * heads)
        def _():
            o_ref[...] = _qk_act(xc, _qk_scale(j, heads))

        @pl.when(j >= 2 * heads)
        def _():
            o_ref[...] = _silu(xc)

    blk = pl.BlockSpec((None, s, HEAD), lambda j, b: (b, 0, j))
    return pl.pallas_call(
        body, name=name, grid=(w3 // HEAD, bl), in_specs=[blk, pl.BlockSpec((w.shape[0], HEAD), lambda j, b: (0, j))],
        out_specs=blk, out_shape=jax.ShapeDtypeStruct(p.shape, F32), compiler_params=_cparams("parallel", "parallel"),
    )(p, w)


def _qkv_bwd(p, w, dout, heads, name):
    bl, s, w3 = p.shape

    def body(p_ref, w_ref, d_ref, dp_ref, dw_ref, dxc_ref):
        j = pl.program_id(0)
        x, wv = p_ref[...], w_ref[...]
        xc = _conv(x, wv)

        @pl.when(j < 2 * heads)
        def _():
            scale = _qk_scale(j, heads)
            dxc_ref[...] = jax.vjp(lambda t: _qk_act(t, scale), xc)[1](d_ref[...])[0]

        @pl.when(j >= 2 * heads)
        def _():
            dxc_ref[...] = jax.vjp(_silu, xc)[1](d_ref[...])[0]

        dp_ref[...] = _conv_bwd(dxc_ref[...], x, wv, dw_ref, pl.program_id(1) == 0).astype(dp_ref.dtype)

    blk = pl.BlockSpec((None, s, HEAD), lambda j, b: (b, 0, j))
    wblk = pl.BlockSpec((w.shape[0], HEAD), lambda j, b: (0, j))
    return pl.pallas_call(
        body, name=name, grid=(w3 // HEAD, bl), in_specs=[blk, wblk, blk], out_specs=[blk, wblk],
        out_shape=[jax.ShapeDtypeStruct(p.shape, MXU_DTYPE), jax.ShapeDtypeStruct(w.shape, F32)],
        scratch_shapes=[pltpu.VMEM((s, HEAD), F32)], compiler_params=_cparams("arbitrary", "arbitrary"),
    )(p, w, dout)


def _sc_specs(p, w):
    bl, s, w3 = p.shape
    nblk = w3 // 3 // LANE
    sec = lambda k: pl.BlockSpec((None, s, LANE), functools.partial(lambda j, b, k: (b, 0, k * nblk + j), k=k))
    return nblk, [sec(0), sec(1), sec(2)], pl.BlockSpec((w.shape[0], LANE), lambda j, b: (0, j)), \
        pl.BlockSpec((None, s, LANE), lambda j, b: (b, 0, j))


def _sc_fwd(p, w, name):
    bl, s, w3 = p.shape
    nblk, secs, wblk, oblk = _sc_specs(p, w)

    def body(b_ref, c_ref, x_ref, w_ref, o_ref):
        o_ref[...] = (b_ref[...] * _conv(c_ref[...] * x_ref[...], w_ref[...])).astype(o_ref.dtype)

    return pl.pallas_call(
        body, name=name, grid=(nblk, bl), in_specs=secs + [wblk], out_specs=oblk,
        out_shape=jax.ShapeDtypeStruct((bl, s, w3 // 3), MXU_DTYPE), compiler_params=_cparams("parallel", "parallel"),
    )(p, p, p, w)


def _sc_bwd(p, w, dout, name):
    bl, s, w3 = p.shape
    nblk, secs, wblk, oblk = _sc_specs(p, w)

    def body(b_ref, c_ref, x_ref, w_ref, d_ref, db_ref, dc_ref, dx_ref, dw_ref):
        gb, gc, xin, wv, d = b_ref[...], c_ref[...], x_ref[...], w_ref[...], d_ref[...]
        u = gc * xin
        db_ref[...] = (d * _conv(u, wv)).astype(db_ref.dtype)
        du = _conv_bwd(d * gb, u, wv, dw_ref, pl.program_id(1) == 0)
        dc_ref[...] = (du * xin).astype(dc_ref.dtype)
        dx_ref[...] = (du * gc).astype(dx_ref.dtype)

    act = jax.ShapeDtypeStruct((bl, s, w3 // 3), MXU_DTYPE)
    return pl.pallas_call(
        body, name=name, grid=(nblk, bl), in_specs=secs + [wblk, oblk], out_specs=[oblk, oblk, oblk, wblk],
        out_shape=[act, act, act, jax.ShapeDtypeStruct(w.shape, F32)], compiler_params=_cparams("arbitrary", "arbitrary"),
    )(p, p, p, w, dout)


def _bdot(a, b, ca, cb):
    return lax.dot_general(a.astype(MXU_DTYPE), b.astype(MXU_DTYPE), (((ca,), (cb,)), ((), ())),
                           preferred_element_type=F32)


def _hdot(a, b):
    return lax.dot_general(a, b, (((1,), (0,)), ((), ())), precision=HIGHEST, preferred_element_type=F32)


def _lane_col(x, idx):
    lane = lax.broadcasted_iota(jnp.int32, x.shape, 1)
    return jnp.sum(jnp.where(lane == idx, x, 0.0), axis=1, keepdims=True)


def _chunk_masks():
    r = lax.broadcasted_iota(jnp.int32, (CHUNK, CHUNK), 0)
    c = lax.broadcasted_iota(jnp.int32, (CHUNK, CHUNK), 1)
    return r == c, r >= c, r > c


def _dot3(a, b):
    ah, bh = a.astype(MXU_DTYPE), b.astype(MXU_DTYPE)
    al, bl = (a - ah.astype(F32)).astype(MXU_DTYPE), (b - bh.astype(F32)).astype(MXU_DTYPE)
    dot = lambda x, y: lax.dot_general(x, y, (((1,), (0,)), ((), ())), preferred_element_type=F32)
    return dot(ah, bh) + (dot(ah, bl) + dot(al, bh))


def _tri_inv_steps(low, eye):
    x = -low
    p = jnp.where(eye, 1.0, 0.0) + x
    span = 2
    while span < CHUNK:
        x = _dot3(x, x)
        yield
        p = p + _dot3(p, x)
        yield
        span *= 2
    return p


def _round_robin(gens):
    out, live = [None] * len(gens), list(range(len(gens)))
    while live:
        still = []
        for i in live:
            try:
                next(gens[i])
                still.append(i)
            except StopIteration as stop:
                out[i] = stop.value
        live = still
    return out


def _gdn_pre(q, k, v, gc, beta, masks):
    eye, causal, strict = masks
    gc_row = jnp.sum(jnp.where(eye, gc, 0.0), axis=0, keepdims=True)
    decay = jnp.where(causal, jnp.exp(jnp.where(causal, gc - gc_row, 0.0)), 0.0)
    eg = jnp.exp(gc)
    gl = gc[CHUNK - 1:CHUNK, :]
    kb, vb = k * beta, v * beta
    low = jnp.where(strict, _bdot(kb, k, 1, 1) * decay, 0.0)
    qk = jnp.where(causal, _bdot(q, k, 1, 1) * decay, 0.0)
    rest = jnp.exp(gl - gc)
    return dict(decay=decay, eg=eg, gl=gl, kb=kb, vb=vb, kbe=kb * eg, low=low, qk=qk, qg=q * eg, rest=rest, kdec=k * rest)


def _gdn_specs(qkv, gbeta, heads, rev):
    bl, s, w3 = qkv.shape
    d, n = w3 // 3, s // CHUNK
    at = (lambda c: n - 1 - c) if rev else (lambda c: c)
    assert d == heads * HEAD
    sec = pl.BlockSpec((None, CHUNK, w3), lambda b, c: (b, at(c), 0))
    gspec = pl.BlockSpec((None, CHUNK, LANE), lambda b, c: (b, at(c), 0))
    sspec = pl.BlockSpec((None, None, heads, HEAD, HEAD), lambda b, c: (b, at(c), 0, 0, 0))
    tspec = pl.BlockSpec((None, None, heads, CHUNK, CHUNK), lambda b, c: (b, at(c), 0, 0, 0))
    return bl, s, d, n, sec, gspec, sspec, tspec


def _grid_marks(bl, n):
    b, c = pl.program_id(0), pl.program_id(1)
    late = min(bl * n - 1, bl * n * 5 // 6)
    first = jnp.logical_and(b == 0, c == 0)
    mid = jnp.logical_and(b == late // n, c == late % n)
    return first, mid, jnp.logical_and(b == bl - 1, c == n - 1)


def _gdn_fwd(qkv, gbeta, heads, name, rider=None):
    bl, s, d, n, sec, gspec, sspec, tspec = _gdn_specs(qkv, gbeta, heads, False)
    r_args, r_in, r_out, r_shape, r_sems = _ride_specs(rider)

    def body(*refs):
        (x_ref, g_ref, o_ref, s_ref, t_ref, st_ref), hooks = _split_refs(refs, 2, 3, 1, rider)
        first, mid, last = _grid_marks(bl, n)
        _hooks_before(hooks, first, mid)

        @pl.when(pl.program_id(1) == 0)
        def _():
            st_ref[...] = jnp.zeros_like(st_ref)

        masks = _chunk_masks()
        eye, causal, _ = masks
        gblk = g_ref[...]
        gc_all = _hdot(jnp.where(causal, 1.0, 0.0), gblk)
        st_all = st_ref[...]

        def head(h):
            st = st_all[h]
            q, k, v = (x_ref[:, sec * d + h * HEAD:sec * d + (h + 1) * HEAD] for sec in range(3))
            pre = _gdn_pre(q, k, v, _lane_col(gc_all, h), _lane_col(gblk, heads + h), masks)
            yield
            t = yield from _tri_inv_steps(pre["low"], eye)
            u, w = _bdot(t, pre["vb"], 1, 0), _bdot(t, pre["kbe"], 1, 0)
            yield
            vnew = u - _bdot(w, st, 1, 0)
            yield
            out = _bdot(pre["qg"], st, 1, 0) + _bdot(pre["qk"], vnew, 1, 0)
            return out, t, st * jnp.exp(pre["gl"]) + _bdot(pre["kdec"], vnew, 0, 0)

        outs, ts, states = zip(*_round_robin([head(h) for h in range(heads)]))
        o_ref[...] = jnp.concatenate(outs, axis=1)
        s_ref[...] = st_all
        t_ref[...] = jnp.stack(ts)
        st_ref[...] = jnp.stack(states)
        _hooks_after(hooks, last)

    return pl.pallas_call(
        body, name=name, grid=(bl, n), in_specs=[sec, gspec] + r_in,
        out_specs=[pl.BlockSpec((None, CHUNK, d), lambda b, c: (b, c, 0)), sspec, tspec] + r_out,
        out_shape=[jax.ShapeDtypeStruct((bl, s, d), F32), jax.ShapeDtypeStruct((bl, n, heads, HEAD, HEAD), F32),
                   jax.ShapeDtypeStruct((bl, n, heads, CHUNK, CHUNK), F32)] + r_shape,
        scratch_shapes=[pltpu.VMEM((heads, HEAD, HEAD), F32)] + r_sems, compiler_params=_cparams("arbitrary", "arbitrary"),
    )(qkv, gbeta, *r_args)


def _gdn_bwd(qkv, gbeta, dout, s_all, t_all, heads, name, rider=None):
    bl, s, d, n, sec, gspec, sspec, tspec = _gdn_specs(qkv, gbeta, heads, True)
    ospec = pl.BlockSpec((None, CHUNK, d), lambda b, c: (b, n - 1 - c, 0))
    r_args, r_in, r_out, r_shape, r_sems = _ride_specs(rider)

    def body(*refs):
        (x_ref, g_ref, do_ref, s_ref, t_ref, dx_ref, dg_ref, ds_ref), hooks = _split_refs(refs, 5, 2, 1, rider)
        first, mid, last = _grid_marks(bl, n)
        _hooks_before(hooks, first, mid)

        @pl.when(pl.program_id(1) == 0)
        def _():
            ds_ref[...] = jnp.zeros_like(ds_ref)

        masks = _chunk_masks()
        eye, causal, strict = masks
        gblk = g_ref[...]
        gc_all = _hdot(jnp.where(causal, 1.0, 0.0), gblk)
        lane = lax.broadcasted_iota(jnp.int32, gblk.shape, 1)
        last_row = lax.broadcasted_iota(jnp.int32, (CHUNK, 1), 0) == CHUNK - 1
        rowsum = lambda a: jnp.sum(a, axis=1, keepdims=True)
        st_all, t_all_, ds_all = s_ref[...], t_ref[...], ds_ref[...]

        def head(h):
            sl = slice(h * HEAD, (h + 1) * HEAD)
            q, k, v = (x_ref[:, sec * d + h * HEAD:sec * d + (h + 1) * HEAD] for sec in range(3))
            do = do_ref[:, sl]
            beta = _lane_col(gblk, heads + h)
            st, t, dsn = st_all[h], t_all_[h], ds_all[h]
            pre = _gdn_pre(q, k, v, _lane_col(gc_all, h), beta, masks)
            decay, eg, kb, vb, kbe, low, qk, qg, kdec = (pre[x] for x in ("decay", "eg", "kb", "vb", "kbe", "low", "qk", "qg", "kdec"))
            egl = jnp.exp(pre["gl"])
            yield
            u, w = _bdot(t, vb, 1, 0), _bdot(t, kbe, 1, 0)
            yield
            vnew = u - _bdot(w, st, 1, 0)
            yield
            dkdec = _bdot(vnew, dsn, 1, 1)
            dvnew = _bdot(kdec, dsn, 1, 0) + _bdot(qk, do, 0, 0)
            dgl = jnp.sum(dsn * st, keepdims=True) * egl
            dqg = _bdot(do, st, 1, 1)
            dqk = jnp.where(causal, _bdot(do, vnew, 1, 1), 0.0)
            yield
            dw = -_bdot(dvnew, st, 1, 1)
            ds_new = dsn * egl + _bdot(qg, do, 0, 0) - _bdot(w, dvnew, 0, 0)
            yield
            dt = _bdot(dvnew, vb, 1, 1) + _bdot(dw, kbe, 1, 1)
            dvb, dkbe = _bdot(t, dvnew, 0, 0), _bdot(t, dw, 0, 0)
            yield
            inner = _bdot(dt, t, 1, 1)
            yield
            dlow = -jnp.where(strict, _bdot(t, inner, 0, 0), 0.0)
            da, db = dlow * decay, dqk * decay
            yield
            m = dlow * low + dqk * qk
            kdk = dkdec * kdec
            col_of_m = jnp.sum(jnp.where(eye, jnp.sum(m, axis=0, keepdims=True), 0.0), axis=1, keepdims=True)
            dgc = rowsum(m) - col_of_m + rowsum(dqg * qg) + rowsum(dkbe * kbe) - rowsum(kdk)
            dgc = dgc + jnp.where(last_row, dgl + jnp.sum(kdk, keepdims=True), 0.0)
            dkb = _bdot(da, k, 1, 0) + dkbe * eg
            yield
            dk = _bdot(da, kb, 0, 0) + _bdot(db, q, 0, 0) + dkdec * pre["rest"] + dkb * beta
            dq = _bdot(db, k, 1, 0) + dqg * eg
            dbeta = rowsum(dkb * k) + rowsum(dvb * v)
            return dq, dk, dvb * beta, jnp.where(lane == h, dgc, 0.0) + jnp.where(lane == heads + h, dbeta, 0.0), ds_new

        dqs, dks, dvs, dgs, dss = zip(*_round_robin([head(h) for h in range(heads)]))
        dx_ref[...] = jnp.concatenate(dqs + dks + dvs, axis=1)
        ds_ref[...] = jnp.stack(dss)
        dgb = dgs[0]
        for extra in dgs[1:]:
            dgb = dgb + extra
        upper = jnp.where(jnp.logical_or(eye, jnp.logical_not(causal)), 1.0, 0.0)
        dg_ref[...] = jnp.where(lane < heads, _hdot(upper, dgb), dgb)
        _hooks_after(hooks, last)

    return pl.pallas_call(
        body, name=name, grid=(bl, n), in_specs=[sec, gspec, ospec, sspec, tspec] + r_in, out_specs=[sec, gspec] + r_out,
        out_shape=[jax.ShapeDtypeStruct(qkv.shape, F32), jax.ShapeDtypeStruct((bl, s, LANE), F32)] + r_shape,
        scratch_shapes=[pltpu.VMEM((heads, HEAD, HEAD), F32)] + r_sems, compiler_params=_cparams("arbitrary", "arbitrary"),
    )(qkv, gbeta, dout, s_all, t_all, *r_args)


def _position():
    return lax.axis_index("x"), lax.axis_index("y"), lax.axis_index("c")


def _all_gather(x, *, name, hbm):
    space = pltpu.HBM if hbm else pltpu.VMEM

    def body(x_ref, out_ref, send_sems, recv_sems, local_sem):
        ax, ay, ac = _position()
        me, sibling = (ax, ay, ac), (ax, ay, 1 - ac)
        chips = [(1 - ax, ay), (ax, 1 - ay), (1 - ax, 1 - ay)]

        def slot(px, py, pc):
            return out_ref.at[4 * px + 2 * py + pc]

        def copy(k, block, to, src=None):
            return pltpu.make_async_remote_copy(
                src_ref=slot(*block) if src is None else src, dst_ref=slot(*block), send_sem=send_sems.at[k],
                recv_sem=recv_sems.at[k], device_id=to, device_id_type=MESH_IDS)

        mine = pltpu.make_async_copy(x_ref, slot(*me), local_sem)
        mine.start()
        first = [copy(0, me, sibling, src=x_ref)] + [copy(1 + j, me, (*chip, ac), src=x_ref) for j, chip in enumerate(chips)]
        for cp in first:
            cp.start()
        passed = [copy(4 + j, (*chip, ac), sibling) for j, chip in enumerate(chips)]
        for j, chip in enumerate(chips):
            copy(1 + j, (*chip, ac), me).wait_recv()
            passed[j].start()
        copy(0, sibling, me).wait_recv()
        for j, chip in enumerate(chips):
            copy(4 + j, (*chip, 1 - ac), me).wait_recv()
        for cp in first + passed:
            cp.wait_send()
        mine.wait()

    return pl.pallas_call(
        body, name=name, out_shape=jax.ShapeDtypeStruct((NDEV,) + x.shape, x.dtype),
        in_specs=[pl.BlockSpec(memory_space=space)], out_specs=pl.BlockSpec(memory_space=space),
        scratch_shapes=[pltpu.SemaphoreType.DMA((7,)), pltpu.SemaphoreType.DMA((7,)), pltpu.SemaphoreType.DMA],
    )(x)


class _Rider:
    def __init__(self, arrays, out_shapes, sems, hooks):
        self.arrays, self.out_shapes, self.sems, self.hooks = arrays, out_shapes, sems, hooks


def _split_refs(refs, n_in, n_out, n_scratch, rider):
    r_in = len(rider.arrays) if rider else 0
    r_out = len(rider.out_shapes) if rider else 0
    o = n_in + r_in
    o2 = o + n_out + r_out
    host = refs[:n_in] + refs[o:o + n_out] + refs[o2:o2 + n_scratch]
    if rider is None:
        return host, None
    return host, rider.hooks(refs[n_in:o], refs[o + n_out:o2], *refs[o2 + n_scratch:])


def _hooks_before(hooks, first, mid):
    if hooks is not None:
        pl.when(first)(hooks[0])
        pl.when(mid)(hooks[1])


def _hooks_after(hooks, last):
    if hooks is not None:
        pl.when(last)(hooks[2])


def _ride_specs(rider):
    hbm = pl.BlockSpec(memory_space=pltpu.HBM)
    if rider is None:
        return [], [], [], [], []
    n_out = len(rider.out_shapes)
    return list(rider.arrays), [hbm] * len(rider.arrays), [hbm] * n_out, list(rider.out_shapes), list(rider.sems)


def _gather_rider(xs):
    n = len(xs)

    def hooks(x_refs, out_refs, send_sems, recv_sems):
        ax, ay, ac = _position()
        me, sibling = (ax, ay, ac), (ax, ay, 1 - ac)
        chips = [(1 - ax, ay), (ax, 1 - ay), (1 - ax, 1 - ay)]

        def copies(k, block, to, own=False):
            out = []
            for i in range(n):
                slot = out_refs[i].at[4 * block[0] + 2 * block[1] + block[2]]
                out.append(pltpu.make_async_remote_copy(
                    src_ref=x_refs[i] if own else slot, dst_ref=slot, send_sem=send_sems.at[k, i], recv_sem=recv_sems.at[k, i],
                    device_id=to, device_id_type=MESH_IDS))
            return out

        def first():
            for cp in copies(0, me, sibling, own=True):
                cp.start()
            for j, chip in enumerate(chips):
                for cp in copies(1 + j, me, (*chip, ac), own=True):
                    cp.start()

        def mid():
            for j, chip in enumerate(chips):
                for arrived, onward in zip(copies(1 + j, (*chip, ac), me), copies(4 + j, (*chip, ac), sibling)):
                    arrived.wait_recv()
                    onward.start()

        def last():
            for cp in copies(0, sibling, me):
                cp.wait_recv()
            for j, chip in enumerate(chips):
                for cp in copies(4 + j, (*chip, 1 - ac), me):
                    cp.wait_recv()
            for cp in copies(0, me, sibling, own=True):
                cp.wait_send()
            for j, chip in enumerate(chips):
                for cp in copies(1 + j, me, (*chip, ac), own=True) + copies(4 + j, (*chip, ac), sibling):
                    cp.wait_send()

        return first, mid, last

    return _Rider(list(xs), [jax.ShapeDtypeStruct((NDEV,) + x.shape, x.dtype) for x in xs],
                  [pltpu.SemaphoreType.DMA((7, n)), pltpu.SemaphoreType.DMA((7, n))], hooks)


def _scatter_rider(parts):
    packed = sum(r for _, r in parts)
    width, dtype = parts[0][0].shape[1], parts[0][0].dtype

    def hooks(g_refs, out_refs, send_sems, recv_sems):
        (recv_ref,) = out_refs
        ax, ay, ac = _position()

        def peer(rel):
            flip = lambda a, bit: 1 - a if rel & bit else a
            return flip(ax, 4), flip(ay, 2), flip(ac, 1)

        def first():
            for rel in range(1, NDEV):
                px, py, pc = peer(rel)
                off = 0
                for g_ref, (_, r) in zip(g_refs, parts):
                    rows = g_ref.at[pl.ds(pl.multiple_of((4 * px + 2 * py + pc) * r, ROW_ALIGN), r)]
                    pltpu.make_async_remote_copy(
                        src_ref=rows, dst_ref=recv_ref.at[rel - 1, pl.ds(off, r)], send_sem=send_sems.at[rel - 1],
                        recv_sem=recv_sems.at[rel - 1], device_id=(px, py, pc), device_id_type=MESH_IDS).start()
                    off += r

        def last():
            for rel in range(1, NDEV):
                slot = recv_ref.at[rel - 1]
                pltpu.make_async_remote_copy(src_ref=slot, dst_ref=slot, send_sem=send_sems.at[rel - 1],
                                             recv_sem=recv_sems.at[rel - 1], device_id=peer(rel), device_id_type=MESH_IDS).wait()

        return first, lambda: None, last

    return _Rider([g for g, _ in parts], [jax.ShapeDtypeStruct((NDEV - 1, packed, width), dtype)],
                  [pltpu.SemaphoreType.DMA((NDEV - 1,)), pltpu.SemaphoreType.DMA((NDEV - 1,))], hooks)


def _sum_direct(own, recv, name):
    r, w = own.shape
    tr = max(t for t in range(ROW_ALIGN, 257, ROW_ALIGN) if r % t == 0)

    def body(own_ref, *refs):
        acc = own_ref[...].astype(F32)
        for ref in refs[:-1]:
            acc = acc + ref[...].astype(F32)
        refs[-1][...] = acc

    rblk = lambda k: pl.BlockSpec((None, tr, w), functools.partial(lambda i, k: (k, i, 0), k=k))
    blk = pl.BlockSpec((tr, w), lambda i: (i, 0))
    return pl.pallas_call(body, name=name, grid=(r // tr,), in_specs=[blk] + [rblk(k) for k in range(NDEV - 1)],
                          out_specs=blk, out_shape=jax.ShapeDtypeStruct((r, w), F32),
                          compiler_params=_cparams("parallel"))(own, *([recv] * (NDEV - 1)))


ROW_ALIGN = 16


def _window_start(rows_per_dev, k):
    return rows_per_dev * k // ROW_ALIGN * ROW_ALIGN


def _exchange_in_chip(parts, name, collective_id):
    packed = sum(win for _, _, win, _ in parts)
    width, dtype = parts[0][0].shape[1], parts[0][0].dtype

    def body(g_refs, out_refs, send_sems, recv_sems):
        (recv_ref,) = out_refs
        ax, ay, ac = _position()
        sibling = (ax, ay, 1 - ac)
        _handshake([sibling])
        for q in range(4):
            for g_ref, (_, r, win, off) in zip(g_refs, parts):
                there = g_ref.at[pl.ds(pl.multiple_of(_window_start(r, 2 * q + 1 - ac), ROW_ALIGN), win)]
                pltpu.make_async_remote_copy(src_ref=there, dst_ref=recv_ref.at[q, pl.ds(off, win)], send_sem=send_sems.at[q],
                                             recv_sem=recv_sems.at[q], device_id=sibling, device_id_type=MESH_IDS).start()
        for q in range(4):
            pltpu.make_async_remote_copy(src_ref=recv_ref.at[q], dst_ref=recv_ref.at[q], send_sem=send_sems.at[q],
                                         recv_sem=recv_sems.at[q], device_id=sibling, device_id_type=MESH_IDS).wait()

    return _on_sequencer(body, [g for g, _, _, _ in parts], [jax.ShapeDtypeStruct((4, packed, width), dtype)],
                         [pltpu.SemaphoreType.DMA((4,)), pltpu.SemaphoreType.DMA((4,))], name=name, collective_id=collective_id)[0]


def _exchange_chips(s1, name):
    def body(s_ref, recv_ref, send_sems, recv_sems):
        ax, ay, ac = _position()
        chips = [(1 - ax, ay), (ax, 1 - ay), (1 - ax, 1 - ay)]
        copies = [pltpu.make_async_remote_copy(
            src_ref=s_ref.at[2 * cx + cy], dst_ref=recv_ref.at[r], send_sem=send_sems.at[r], recv_sem=recv_sems.at[r],
            device_id=(cx, cy, ac), device_id_type=MESH_IDS) for r, (cx, cy) in enumerate(chips)]
        for cp in copies:
            cp.start()
        for cp in copies:
            cp.wait_recv()
        for cp in copies:
            cp.wait_send()

    hbm = pl.BlockSpec(memory_space=pltpu.HBM)
    return pl.pallas_call(
        body, name=name, out_shape=jax.ShapeDtypeStruct((3,) + s1.shape[1:], s1.dtype), in_specs=[hbm], out_specs=hbm,
        scratch_shapes=[pltpu.SemaphoreType.DMA((3,)), pltpu.SemaphoreType.DMA((3,))],
    )(s1)


def _on_sequencer(body, ins, out_shapes, sems, *, name, collective_id):
    hbm = pltpu.MemorySpace.HBM
    in_refs = [jax.new_ref(a, memory_space=hbm) for a in ins]
    out_refs = [jax.empty_ref(s, memory_space=hbm) for s in out_shapes]

    @pl.kernel(mesh=plsc.ScalarSubcoreMesh(axis_name="sequencer", num_cores=1), name=name, scratch_types=tuple(sems),
               compiler_params=pltpu.CompilerParams(collective_id=collective_id))
    def launch(*sem_refs):
        body(in_refs, out_refs, *sem_refs)

    launch()
    return [r[...] for r in out_refs]


def _handshake(peers):
    barrier = pltpu.get_barrier_semaphore()
    for peer in peers:
        pl.semaphore_signal(barrier, inc=1, device_id=peer, device_id_type=MESH_IDS)
    pl.semaphore_wait(barrier, len(peers))


def _exchange_chips_async(s1, name, collective_id):
    def body(in_refs, out_refs, send_sems, recv_sems):
        (src,), (got,) = in_refs, out_refs
        ax, ay, ac = _position()
        chips = [(1 - ax, ay), (ax, 1 - ay), (1 - ax, 1 - ay)]
        _handshake([(cx, cy, ac) for cx, cy in chips])
        copies = [pltpu.make_async_remote_copy(
            src_ref=src.at[2 * cx + cy], dst_ref=got.at[r], send_sem=send_sems.at[r], recv_sem=recv_sems.at[r],
            device_id=(cx, cy, ac), device_id_type=MESH_IDS) for r, (cx, cy) in enumerate(chips)]
        for cp in copies:
            cp.start()
        for cp in copies:
            cp.wait_recv()
        for cp in copies:
            cp.wait_send()

    return _on_sequencer(body, [s1], [jax.ShapeDtypeStruct((3,) + s1.shape[1:], s1.dtype)],
                         [pltpu.SemaphoreType.DMA((3,)), pltpu.SemaphoreType.DMA((3,))], name=name, collective_id=collective_id)[0]


def _gather_async(xs, name, collective_id):
    rider = _gather_rider(xs)

    def body(in_refs, out_refs, send_sems, recv_sems):
        ax, ay, ac = _position()
        _handshake([(ax, ay, 1 - ac), (1 - ax, ay, ac), (ax, 1 - ay, ac), (1 - ax, 1 - ay, ac)])
        for hook in rider.hooks(in_refs, out_refs, send_sems, recv_sems):
            hook()

    return _on_sequencer(body, rider.arrays, rider.out_shapes, rider.sems, name=name, collective_id=collective_id)


def _scatter_async(parts, name, collective_id):
    rider = _scatter_rider(parts)

    def body(in_refs, out_refs, send_sems, recv_sems):
        ax, ay, ac = _position()
        flip = lambda a, on: 1 - a if on else a
        _handshake([(flip(ax, rel & 4), flip(ay, rel & 2), flip(ac, rel & 1)) for rel in range(1, NDEV)])
        for hook in rider.hooks(in_refs, out_refs, send_sems, recv_sems):
            hook()

    return _on_sequencer(body, rider.arrays, rider.out_shapes, rider.sems, name=name, collective_id=collective_id)[0]


def _sum_in_chip(own, recv, name):
    _, r, w = own.shape
    tr = _tile(r, (256, 128))

    def body(a_ref, b_ref, o_ref):
        o_ref[...] = (a_ref[...].astype(F32) + b_ref[...].astype(F32)).astype(o_ref.dtype)

    blk = pl.BlockSpec((None, tr, w), lambda q, i: (q, i, 0))
    return pl.pallas_call(body, name=name, grid=(4, r // tr), in_specs=[blk, blk], out_specs=blk,
                          out_shape=jax.ShapeDtypeStruct(own.shape, own.dtype),
                          compiler_params=_cparams("parallel", "parallel"))(own, recv)


def _sum_chips(s1, recv, chip, name):
    _, r, w = s1.shape
    tr = _tile(r, (256, 128))

    def body(c_ref, s_ref, r0_ref, r1_ref, r2_ref, o_ref):
        f = lambda ref: ref[...].astype(F32)
        o_ref[...] = ((f(s_ref) + f(r0_ref)) + f(r1_ref)) + f(r2_ref)

    rblk = lambda k: pl.BlockSpec((None, tr, w), functools.partial(lambda i, c, k: (k, i, 0), k=k))
    grid_spec = pltpu.PrefetchScalarGridSpec(
        num_scalar_prefetch=1, grid=(r // tr,),
        in_specs=[pl.BlockSpec((None, tr, w), lambda i, c: (c[0], i, 0)), rblk(0), rblk(1), rblk(2)],
        out_specs=pl.BlockSpec((tr, w), lambda i, c: (i, 0)))
    return pl.pallas_call(body, name=name, grid_spec=grid_spec, out_shape=jax.ShapeDtypeStruct((r, w), F32),
                          compiler_params=_cparams("parallel"))(chip, s1, recv, recv, recv)


def _silu_rows(x, name):
    def body(x_ref, o_ref):
        o_ref[...] = _silu(x_ref[...])

    return pl.pallas_call(body, name=name, out_shape=jax.ShapeDtypeStruct(x.shape, F32))(x)


def _row_sum(x, name):
    def body(x_ref, o_ref):
        acc = x_ref[0:1, :]
        for i in range(1, x.shape[0]):
            acc = acc + x_ref[i:i + 1, :]
        o_ref[...] = acc

    return pl.pallas_call(body, name=name, out_shape=jax.ShapeDtypeStruct((1, x.shape[1]), F32))(x)


def _adamw(w, g, m, v, name):
    cols = w.shape[-1]
    rows = w.size // cols
    tr = _tile(rows, (128,))
    tc = LANE if (tr == rows and rows > 512 and cols % LANE == 0) else cols

    def body(w_ref, g_ref, m_ref, v_ref, d_ref, mo_ref, vo_ref):
        grad = g_ref[...]
        m_new = ADAM_B1 * m_ref[...] + (1.0 - ADAM_B1) * grad
        v_new = ADAM_B2 * v_ref[...] + (1.0 - ADAM_B2) * jnp.square(grad)
        m_hat = m_new / (1.0 - ADAM_B1 ** ADAM_STEP)
        v_hat = v_new / (1.0 - ADAM_B2 ** ADAM_STEP)
        d_ref[...] = -ADAM_LR * (m_hat / (jnp.sqrt(v_hat) + ADAM_EPS) + ADAM_WD * w_ref[...])
        mo_ref[...] = m_new
        vo_ref[...] = v_new

    blk = pl.BlockSpec((tr, tc), lambda i, j: (i, j))
    out = pl.pallas_call(
        body, name=name, grid=(rows // tr, cols // tc), in_specs=[blk] * 4, out_specs=[blk] * 3,
        out_shape=[jax.ShapeDtypeStruct((rows, cols), F32)] * 3, compiler_params=_cparams("parallel", "parallel"),
    )(*[t.reshape(rows, cols) for t in (w, g, m, v)])
    return [t.reshape(w.shape) for t in out]


def _pack(parts, width, row_mult, dtype):
    flat = jnp.concatenate([p.reshape(-1).astype(dtype) for p in parts])
    rows = -(-flat.shape[0] // (width * row_mult)) * row_mult
    return jnp.pad(flat, (0, rows * width - flat.shape[0])).reshape(rows, width)


def _unpack(flat, shapes):
    out, off = [], 0
    for shp in shapes:
        size = 1
        for dim in shp:
            size *= dim
        out.append(flat[:, off:off + size].reshape((flat.shape[0],) + tuple(shp)))
        off += size
    return out


def _devices_to_cols(a):
    _, r, c = a.shape
    return a.transpose(1, 0, 2).reshape(r, NDEV * c)


def kernel(x, c, w_ada, b_ada, norm1_w, w_in, gdn_conv_w, gdn_a_log, gdn_dt_bias, gdn_norm_w, w_gdn_proj, sc_conv_w, w_sc_out, w_o, norm2_w, w_ffn_in, w_ffn_out, w_ada_f, b_ada_f, normf_w, loss_target, m_w_ada, m_b_ada, m_norm1_w, m_w_in, m_gdn_conv_w, m_gdn_a_log, m_gdn_dt_bias, m_gdn_norm_w, m_w_gdn_proj, m_sc_conv_w, m_w_sc_out, m_w_o, m_norm2_w, m_w_ffn_in, m_w_ffn_out, m_w_ada_f, m_b_ada_f, m_normf_w, v_w_ada, v_b_ada, v_norm1_w, v_w_in, v_gdn_conv_w, v_gdn_a_log, v_gdn_dt_bias, v_gdn_norm_w, v_w_gdn_proj, v_sc_conv_w, v_w_sc_out, v_w_o, v_norm2_w, v_w_ffn_in, v_w_ffn_out, v_w_ada_f, v_b_ada_f, v_normf_w):
    bl, s, d = x.shape
    heads = gdn_a_log.shape[-1]
    dff = w_ffn_out.shape[1] * NDEV
    tok = bl * s
    ax, ay, ac = _position()
    dev = 4 * ax + 2 * ay + ac
    as_tok = lambda a: a.reshape(bl, s, a.shape[-1])
    as_mat = lambda a: a.reshape(tok, a.shape[-1])

    small = _all_gather(_pack([c, gdn_conv_w, sc_conv_w], LANE, 8, F32), name="gather_cond", hbm=False)
    c_all, conv_w, sc_w = _unpack(small.reshape(NDEV, -1), [(bl, d), gdn_conv_w.shape[1:], sc_conv_w.shape[1:]])
    c_act = _silu_rows(c_all.reshape(NDEV * bl, d), "cond_silu")
    conv_w, sc_w = _devices_to_cols(conv_w), _devices_to_cols(sc_w)
    n_ada, n_adaf = w_ada.shape[-1], w_ada_f.shape[-1]
    bias = jnp.broadcast_to(lax.dynamic_slice_in_dim(b_ada, dev * n_ada, n_ada, axis=1), (NDEV * bl, n_ada))
    biasf = jnp.broadcast_to(lax.dynamic_slice_in_dim(b_ada_f.reshape(1, -1), dev * n_adaf, n_adaf, axis=1), (NDEV * bl, n_adaf))
    mod_cols = _mm(c_act, w_ada[0], add=bias, name="ada_cols")
    modf_cols = _mm(c_act, w_ada_f, add=biasf, name="adaf_cols")
    mods = _all_gather(jnp.concatenate([mod_cols, modf_cols], axis=1), name="gather_mod", hbm=False)
    mod_all = mods[:, :, :n_ada].transpose(1, 0, 2).reshape(NDEV * bl, NDEV * n_ada)
    modf_all = mods[:, :, n_ada:].transpose(1, 0, 2).reshape(NDEV * bl, NDEV * n_adaf)
    my_rows = lambda a: lax.dynamic_slice_in_dim(a, dev * bl, bl, axis=0)
    sh1, sc1, g1, sh2, sc2, g2 = [t.reshape(bl, 1, d) for t in jnp.split(my_rows(mod_all), 6, axis=1)]
    shf, scf = [t.reshape(bl, 1, d) for t in jnp.split(my_rows(modf_all), 2, axis=1)]

    late = [t.astype(MXU_DTYPE) for t in (w_gdn_proj[0], w_sc_out[0], w_o[0], w_ffn_in[0].T, w_ffn_out[0])]
    rows = [t.shape[0] for t in late] + [w_in.shape[-1]]
    offs = [sum(rows[:i]) for i in range(5)]
    in_rows = -(-rows[5] // ROW_ALIGN) * ROW_ALIGN
    in_send = jnp.pad(w_in[0].T.astype(MXU_DTYPE), ((0, in_rows - rows[5]), (0, 0)))
    with_own = lambda g, own: lax.dynamic_update_slice_in_dim(g, own[None], dev, axis=0)
    (wt_in,) = _gather_async([in_send], "gather_w_in", 1)
    wt_in = with_own(wt_in, in_send)[:, :rows[5], :].reshape(NDEV * rows[5], d)
    gathered = _gather_async(late[:3], "gather_mixer", 2) + _gather_async(late[3:], "gather_ffn", 3)
    wgp, wso, wo, wt_fi, wfo = [with_own(g, own).reshape(NDEV * own.shape[0], d) for g, own in zip(gathered, late)]
    o_z, o_ab, o_sc, o_ga, o_gb = 3 * d, 4 * d, 4 * d + 2 * heads, 7 * d + 2 * heads, 8 * d + 2 * heads
    s_qkv, s_z, s_sc, s_gate = (0, o_z), (o_z, d), (o_sc, 3 * d), (o_ga, 2 * d)
    wt_ab = jnp.pad(wt_in[o_ab:o_sc], ((0, LANE - 2 * heads), (0, 0)))

    n1w, n2w, nfw = norm1_w.reshape(1, d), norm2_w.reshape(1, d), normf_w.reshape(1, d)
    lanes = lambda a: jnp.pad(a.reshape(1, -1), ((0, 0), (0, LANE - a.size)))
    a_log, dt_bias, gnw = lanes(gdn_a_log), lanes(gdn_dt_bias), gdn_norm_w.reshape(1, HEAD)
    f_gates = functools.partial(_f_gates, heads=heads)
    (h1,) = _tok_fwd(_f_norm_mod, [x], [sh1, sc1], [n1w], [(d, MXU_DTYPE)], name="norm1", ts=512)
    h1m = as_mat(h1)
    p_qkv = as_tok(_mm(h1m, wt_in, tb=True, b_rows=s_qkv, name="in_qkv"))
    p_z = as_tok(_mm(h1m, wt_in, tb=True, b_rows=s_z, name="in_z"))
    p_ab = as_tok(_mm(h1m, wt_ab, tb=True, name="in_ab"))
    p_sc = as_tok(_mm(h1m, wt_in, tb=True, b_rows=s_sc, name="in_sc"))
    p_g = as_tok(_mm(h1m, wt_in, tb=True, b_rows=s_gate, name="in_gate"))
    qkv = _qkv_fwd(p_qkv, conv_w, heads, "qkv_conv")
    (gbeta,) = _tok_fwd(f_gates, [p_ab], [], [a_log, dt_bias], [(LANE, F32)], name="gates", ts=512)
    o, s_all, t_all = _gdn_fwd(qkv, gbeta, heads, "gdn")
    (og,) = _tok_fwd(_f_gdn_out, [o, p_z], [], [(gnw, None)], [(d, MXU_DTYPE)], name="gdn_out", ts=2048, wb=HEAD, cols=heads)
    y_a = as_tok(_mm(as_mat(og), wgp, name="gdn_proj"))
    scp = _sc_fwd(p_sc, sc_w, "sc_conv")
    y_b = as_tok(_mm(as_mat(scp), wso, name="sc_out"))
    mcols = d // 512 if d % 512 == 0 else 1
    mwb = d // mcols
    merge_toks = [(p_g, 0), (p_g, mcols), y_a, y_b]
    (mrg,) = _tok_fwd(_f_merge, merge_toks, [], [], [(d, MXU_DTYPE)], name="merge", ts=1024, wb=mwb, cols=mcols)
    mix = as_tok(_mm(as_mat(mrg), wo, name="mix_out"))
    x2, h2 = _tok_fwd(_f_res_norm_mod, [x, mix], [g1, sh2, sc2], [n2w], [(d, F32), (d, MXU_DTYPE)], name="norm2", ts=512)
    gu = as_tok(_mm(as_mat(h2), wt_fi, tb=True, out_dtype=MXU_DTYPE, name="ffn_in"))
    fwb = _tile(dff, (256, 128))
    fcols = dff // fwb
    (act,) = _tok_fwd(_f_swiglu, [(gu, 0), (gu, fcols)], [], [], [(dff, MXU_DTYPE)], name="swiglu", ts=2048, wb=fwb, cols=fcols)
    ff = as_tok(_mm(as_mat(act), wfo, name="ffn_out"))

    loss_l, (dx2, dff_out, _), (dg2, dshf, dscf), (dnfw,) = _tok_bwd(
        _f_loss, [x2, ff, loss_target], [g2, shf, scf], [nfw], [], [True, True, False], name="loss", ts=512, loss=True,
        tok_dtype=[F32, MXU_DTYPE, None])
    dffm = as_mat(dff_out)
    dact = as_tok(_mm(dffm, wfo, tb=True, name="d_ffn_out"))
    gmm = functools.partial(_mm, ta=True, out_dtype=MXU_DTYPE)
    gw_ffn_out = gmm(as_mat(act), dffm, name="g_ffn_out")
    (dgu_a, dgu_b), _, _ = _tok_bwd(_f_swiglu, [(gu, 0), (gu, fcols)], [], [], [dact], [True, True], name="d_swiglu",
                                    ts=2048, wb=fwb, cols=fcols, tok_dtype=MXU_DTYPE)
    dh2 = _mm(as_mat(dgu_a), wt_fi, b_rows=(0, dff), name="d_ffn_in_a")
    dh2 = as_tok(_mm(as_mat(dgu_b), wt_fi, b_rows=(dff, dff), add=dh2, name="d_ffn_in_b"))
    h2m = as_mat(h2)
    gwt_ffn_in = gmm(as_mat(dgu_a), h2m, out_rows=2 * dff, name="g_ffn_in_a")
    gwt_ffn_in = gmm(as_mat(dgu_b), h2m, out_rows=2 * dff, row_off=dff, into=gwt_ffn_in, name="g_ffn_in_b")
    (dx_skip, dmix), (dg1, dsh2, dsc2), (dn2w,) = _tok_bwd(
        _f_res_norm_mod, [x, mix], [g1, sh2, sc2], [n2w], [dx2, dh2], [True, True], name="d_norm2", ts=256,
        tok_dtype=[F32, MXU_DTYPE])
    dmixm = as_mat(dmix)
    dmrg = as_tok(_mm(dmixm, wo, tb=True, name="d_mix_out"))
    gw_o = gmm(as_mat(mrg), dmixm, name="g_mix_out")
    (dga, dgb, dya, dyb), _, _ = _tok_bwd(_f_merge, merge_toks, [], [], [dmrg], [True] * 4, name="d_merge", ts=512,
                                          wb=mwb, cols=mcols, tok_dtype=MXU_DTYPE)
    dyam, dybm = as_mat(dya), as_mat(dyb)
    dog = as_tok(_mm(dyam, wgp, tb=True, name="d_gdn_proj"))
    gw_gdn_proj = gmm(as_mat(og), dyam, name="g_gdn_proj")
    dscp = as_tok(_mm(dybm, wso, tb=True, name="d_sc_out"))
    gw_sc_out = gmm(as_mat(scp), dybm, name="g_sc_out")
    dscb, dscc, dscx, g_sc_w = _sc_bwd(p_sc, sc_w, dscp, "d_sc_conv")
    (do, dz), _, (g_gnw,) = _tok_bwd(_f_gdn_out, [o, p_z], [], [(gnw, None)], [dog], [True, True], name="d_gdn_out",
                                     ts=2048, wb=HEAD, cols=heads, tok_dtype=[F32, MXU_DTYPE])
    ffn_parts, mix_parts = [(gwt_ffn_in, rows[3]), (gw_ffn_out, rows[4])], [(gw_gdn_proj, rows[0]), (gw_sc_out, rows[1]), (gw_o, rows[2])]
    own_rows = lambda parts: jnp.concatenate([lax.dynamic_slice_in_dim(g, dev * r, r, axis=0) for g, r in parts], axis=0)
    ffn_recv = _scatter_async(ffn_parts, "scatter_ffn", 4)
    mix_recv = _scatter_async(mix_parts, "scatter_mixer", 5)
    dqkv, dgbeta = _gdn_bwd(qkv, gbeta, do, s_all, t_all, heads, "d_gdn")
    dp_qkv, g_conv_w = _qkv_bwd(p_qkv, conv_w, dqkv, heads, "d_qkv_conv")
    ffn_red = _sum_direct(own_rows(ffn_parts), ffn_recv, "sum_ffn")
    mix_red = _sum_direct(own_rows(mix_parts), mix_recv, "sum_mix")
    (dp_ab,), _, (g_a_log, g_dt_bias) = _tok_bwd(f_gates, [p_ab], [], [a_log, dt_bias], [dgbeta], [True], name="d_gates",
                                                 ts=512, tok_dtype=MXU_DTYPE)
    sections = [(dp_qkv, s_qkv), (dz, s_z), (dp_ab, None), (dscb, (o_sc, d)), (dscc, (o_sc + d, d)), (dscx, (o_sc + 2 * d, d)),
                (dga, (o_ga, d)), (dgb, (o_gb, d))]
    gwt_in = [gmm(as_mat(dp), h1m, name=f"g_in_{k}") for k, (dp, _) in enumerate(sections)]
    gwt_in[2] = gwt_in[2][:2 * heads]
    gwt_in = jnp.concatenate(gwt_in, axis=0)

    r_in = rows[5]
    win = -(-(r_in + max(r_in * k % ROW_ALIGN for k in range(NDEV))) // 128) * 128
    need_rows = max(_window_start(r_in, k) for k in range(NDEV)) + win
    gwt_in = jnp.pad(gwt_in, ((0, need_rows - gwt_in.shape[0]), (0, 0)))
    recv1 = _exchange_in_chip([(gwt_in, r_in, win, 0)], "scatter_in_chip", 7)
    own = jnp.stack([lax.dynamic_slice_in_dim(gwt_in, _window_start(r_in, 2 * q + ac), win, axis=0) for q in range(4)])
    s1 = _sum_in_chip(own, recv1, "sum_in_chip")
    recv2 = _exchange_chips_async(s1, "scatter_chips", 6)

    dh1 = None
    for k, (dp, sec) in enumerate(sections):
        dh1 = _mm(as_mat(dp), wt_ab if sec is None else wt_in, b_rows=sec, add=dh1, name=f"d_in_{k}")
    (grad_x,), (dsh1, dsc1), (dn1w,) = _tok_bwd(_f_norm_mod_skip, [x], [sh1, sc1], [n1w], [as_tok(dh1), dx_skip], [True],
                                                name="d_norm1", ts=256)
    reduced = _sum_chips(s1, recv2, (2 * ax + ay).reshape(1).astype(jnp.int32), "sum_chips")
    gt_w_in = lax.dynamic_slice_in_dim(reduced, r_in * dev - _window_start(r_in, dev), r_in, axis=0)
    g_w_in = gt_w_in.T.reshape(w_in.shape)
    gt_w_ffn_in = ffn_red[:rows[3]]
    g_w_ffn_in = gt_w_ffn_in.T.reshape(w_ffn_in.shape)
    g_w_ffn_out = ffn_red[rows[3]:].reshape(w_ffn_out.shape)
    g_w_gdn_proj, g_w_sc_out, g_w_o = (mix_red[offs[i]:offs[i] + rows[i]].reshape(ref.shape)
                                       for i, ref in enumerate((w_gdn_proj, w_sc_out, w_o)))

    dmod = jnp.concatenate([t.reshape(bl, d) for t in (dsh1, dsc1, dg1, dsh2, dsc2, dg2)], axis=1)
    dmodf = jnp.concatenate([t.reshape(bl, d) for t in (dshf, dscf)], axis=1)
    summed_parts = [dn1w, dn2w, dnfw, g_gnw, g_a_log, g_dt_bias, g_conv_w, g_sc_w, loss_l]
    partial = _all_gather(_pack([dmod, dmodf] + summed_parts, LANE, 8, F32), name="gather_small", hbm=False)
    partial = partial.reshape(NDEV, -1)
    n_rows = bl * (6 * d + 2 * d)
    dmod_all, dmodf_all = _unpack(partial[:, :n_rows], [(bl, 6 * d), (bl, 2 * d)])
    dmod_all, dmodf_all = dmod_all.reshape(NDEV * bl, 6 * d), dmodf_all.reshape(NDEV * bl, 2 * d)
    totals = _row_sum(partial[:, n_rows:], "sum_small")
    t_n1w, t_n2w, t_nfw, t_gnw, t_a_log, t_dt_bias, t_conv_w, t_sc_w, t_loss = [
        t[0] for t in _unpack(totals, [p.shape for p in summed_parts])]
    my_cols = lambda a, n: lax.dynamic_slice_in_dim(a, dev * n, n, axis=1)
    grads = {
        "w_ada": _mm(c_act, my_cols(dmod_all, n_ada), ta=True, name="g_ada").reshape(w_ada.shape),
        "b_ada": _row_sum(dmod_all, "g_ada_bias").reshape(b_ada.shape),
        "norm1_w": t_n1w.reshape(norm1_w.shape),
        "w_in": g_w_in,
        "gdn_conv_w": my_cols(t_conv_w, gdn_conv_w.shape[-1]).reshape(gdn_conv_w.shape),
        "gdn_a_log": t_a_log[:, :heads].reshape(gdn_a_log.shape),
        "gdn_dt_bias": t_dt_bias[:, :heads].reshape(gdn_dt_bias.shape),
        "gdn_norm_w": t_gnw.reshape(gdn_norm_w.shape),
        "w_gdn_proj": g_w_gdn_proj,
        "sc_conv_w": my_cols(t_sc_w, sc_conv_w.shape[-1]).reshape(sc_conv_w.shape),
        "w_sc_out": g_w_sc_out,
        "w_o": g_w_o,
        "norm2_w": t_n2w.reshape(norm2_w.shape),
        "w_ffn_in": g_w_ffn_in,
        "w_ffn_out": g_w_ffn_out,
        "w_ada_f": _mm(c_act, my_cols(dmodf_all, n_adaf), ta=True, name="g_adaf").reshape(w_ada_f.shape),
        "b_ada_f": _row_sum(dmodf_all, "g_adaf_bias").reshape(b_ada_f.shape),
        "normf_w": t_nfw.reshape(normf_w.shape),
    }
    weights = dict(w_ada=w_ada, b_ada=b_ada, norm1_w=norm1_w, w_in=w_in, gdn_conv_w=gdn_conv_w, gdn_a_log=gdn_a_log,
                   gdn_dt_bias=gdn_dt_bias, gdn_norm_w=gdn_norm_w, w_gdn_proj=w_gdn_proj, sc_conv_w=sc_conv_w,
                   w_sc_out=w_sc_out, w_o=w_o, norm2_w=norm2_w, w_ffn_in=w_ffn_in, w_ffn_out=w_ffn_out, w_ada_f=w_ada_f,
                   b_ada_f=b_ada_f, normf_w=normf_w)
    m_in = [m_w_ada, m_b_ada, m_norm1_w, m_w_in, m_gdn_conv_w, m_gdn_a_log, m_gdn_dt_bias, m_gdn_norm_w, m_w_gdn_proj,
            m_sc_conv_w, m_w_sc_out, m_w_o, m_norm2_w, m_w_ffn_in, m_w_ffn_out, m_w_ada_f, m_b_ada_f, m_normf_w]
    v_in = [v_w_ada, v_b_ada, v_norm1_w, v_w_in, v_gdn_conv_w, v_gdn_a_log, v_gdn_dt_bias, v_gdn_norm_w, v_w_gdn_proj,
            v_sc_conv_w, v_w_sc_out, v_w_o, v_norm2_w, v_w_ffn_in, v_w_ffn_out, v_w_ada_f, v_b_ada_f, v_normf_w]
    deltas, new_m, new_v = [], [], []
    grads_t = {"w_in": gt_w_in, "w_ffn_in": gt_w_ffn_in}
    for (wname, wt), mt, vt in zip(weights.items(), m_in, v_in):
        if wname in grads_t:
            back = lambda a, wt=wt: a.T.reshape(wt.shape)
            dl, mn, vn = (back(a) for a in _adamw(wt[0].T, grads_t[wname], mt[0].T, vt[0].T, "adamw_" + wname))
        else:
            dl, mn, vn = _adamw(wt, grads[wname], mt, vt, "adamw_" + wname)
        deltas.append(dl)
        new_m.append(mn)
        new_v.append(vn)
    loss = t_loss[0, 0]
    return (loss, grad_x, *[grads[k] for k in weights], *deltas, *new_m, *new_v)
```

```python
import functools

import jax
import jax.numpy as jnp
from jax import lax
from jax.experimental import pallas as pl
from jax.experimental.pallas import tpu as pltpu
from jax.experimental.pallas import tpu_sc as plsc

F32 = jnp.float32
MXU_DTYPE = jnp.bfloat16
NDEV = 8
CHUNK = 64
HEAD = 128
LANE = 128
EPS = 1e-6
ADAM_LR, ADAM_B1, ADAM_B2, ADAM_EPS, ADAM_WD, ADAM_STEP = 0.001, 0.9, 0.999, 1e-08, 0.01, 10
VMEM_LIMIT = 48 * 1024 * 1024
MESH_IDS = pl.DeviceIdType.MESH
HIGHEST = lax.Precision.HIGHEST


def _tile(n, cands=(512, 256, 128)):
    for c in cands:
        if n % c == 0:
            return c
    return n


def _cparams(*sem):
    return pltpu.CompilerParams(dimension_semantics=sem, vmem_limit_bytes=VMEM_LIMIT)


def _mm(a, b, *, ta=False, tb=False, add=None, out_dtype=F32, name, b_rows=None, out_rows=None, row_off=0, into=None):
    m, k = (a.shape[1], a.shape[0]) if ta else a.shape
    b_shape = b.shape if b_rows is None else (b_rows[1], b.shape[1])
    n = b_shape[0] if tb else b_shape[1]
    assert k == (b_shape[1] if tb else b_shape[0])
    if ta:
        tm, tn = _tile(m), n if n <= 1024 else _tile(n)
        tk = k if k <= 4096 else _tile(k, (4096, 2048, 1024, 512))
        if tm * tk > 1024 * 2048:
            tk = _tile(k, (2048, 1024, 512))
    else:
        tk = k if k <= 1024 else _tile(k, (1024, 512))
        tn = _tile(n, (1024 if tk <= 1024 else 512, 512, 256, 128))
        tm = _tile(m, (2048 if (tn <= 512 and tk <= 1024) else 1024, 1024, 512, 256, 128))
    nk = k // tk
    dims = (((0 if ta else 1,), (1 if tb else 0,)), ((), ()))
    has_add = add is not None

    def body(*refs):
        a_ref, b_ref = refs[0], refs[1]
        add_ref = refs[2] if has_add else None
        o_ref = refs[2 + has_add + (into is not None)]
        part = lax.dot_general(a_ref[...].astype(MXU_DTYPE), b_ref[...].astype(MXU_DTYPE), dims,
                               preferred_element_type=F32)

        def finish(acc):
            if has_add:
                acc = acc + add_ref[...]
            o_ref[...] = acc.astype(o_ref.dtype)

        if nk == 1:
            finish(part)
        else:
            acc_ref = refs[-1]
            kk = pl.program_id(2)

            @pl.when(kk == 0)
            def _():
                acc_ref[...] = part

            @pl.when(kk > 0)
            def _():
                acc_ref[...] += part

            @pl.when(kk == nk - 1)
            def _():
                finish(acc_ref[...])

    a_spec = pl.BlockSpec((tk, tm), lambda i, j, kk: (kk, i)) if ta else pl.BlockSpec((tm, tk), lambda i, j, kk: (i, kk))
    if b_rows is None:
        b_spec = pl.BlockSpec((tn, tk), lambda i, j, kk: (j, kk)) if tb else pl.BlockSpec((tk, tn), lambda i, j, kk: (kk, j))
    else:
        at = lambda t: pl.multiple_of(b_rows[0] + t, ROW_ALIGN)
        b_spec = (pl.BlockSpec((pl.Element(tn), pl.Element(tk)), lambda i, j, kk: (at(j * tn), kk * tk)) if tb else
                  pl.BlockSpec((pl.Element(tk), pl.Element(tn)), lambda i, j, kk: (at(kk * tk), j * tn)))
    add_spec = pl.BlockSpec((tm, tn), lambda i, j, kk: (i, j))
    assert row_off % tm == 0
    o_spec = pl.BlockSpec((tm, tn), lambda i, j, kk: (i + row_off // tm, j))
    in_specs = [a_spec, b_spec] + ([add_spec] if has_add else []) + ([pl.BlockSpec(memory_space=pl.ANY)] if into is not None else [])
    args = [a, b] + ([add] if has_add else []) + ([into] if into is not None else [])
    return pl.pallas_call(
        body, name=name, grid=(m // tm, n // tn, nk), in_specs=in_specs, out_specs=o_spec,
        out_shape=jax.ShapeDtypeStruct((out_rows or m, n), out_dtype),
        scratch_shapes=[pltpu.VMEM((tm, tn), F32)] if nk > 1 else [],
        input_output_aliases={len(args) - 1: 0} if into is not None else {},
        compiler_params=_cparams("parallel", "parallel", "arbitrary"),
    )(*args)


def _with_off(xs):
    return [x if isinstance(x, tuple) else (x, 0) for x in xs]


def _spec(kind, arr, off, ts, wb):
    w = arr.shape[-1] if wb is None else wb
    col = (lambda j: 0) if wb is None else functools.partial(lambda j, o: o + j, o=off)
    if kind == "tok":
        return pl.BlockSpec((None, ts, w), lambda j, b, i: (b, i, col(j)))
    if kind == "bat":
        return pl.BlockSpec((None, 1, w), lambda j, b, i: (b, 0, col(j)))
    if off is None:
        return pl.BlockSpec(arr.shape, lambda j, b, i: (0, 0))
    return pl.BlockSpec((arr.shape[0], w), lambda j, b, i: (0, col(j)))


def _in_specs(toks, bats, pars, cots, ts, wb):
    return ([_spec("tok", a, o, ts, wb) for a, o in toks] + [_spec("bat", a, o, ts, wb) for a, o in bats]
            + [_spec("par", a, o, ts, wb) for a, o in pars] + [_spec("tok", a, o, ts, wb) for a, o in cots])


def _tok_fwd(fn, toks, bats, pars, outs, *, name, ts, wb=None, cols=1):
    toks, bats, pars = _with_off(toks), _with_off(bats), _with_off(pars)
    bl, s, _ = toks[0][0].shape
    ts = min(ts, s)
    n_in = len(toks) + len(bats) + len(pars)

    def body(*refs):
        res = fn(*[r[...].astype(F32) for r in refs[:n_in]])
        for r, val in zip(refs[n_in:], res):
            r[...] = val.astype(r.dtype)

    out_specs = [pl.BlockSpec((None, ts, w if wb is None else wb), lambda j, b, i: (b, i, j)) for w, _ in outs]
    return pl.pallas_call(
        body, name=name, grid=(cols, bl, s // ts), in_specs=_in_specs(toks, bats, pars, [], ts, wb),
        out_specs=out_specs, out_shape=[jax.ShapeDtypeStruct((bl, s, w), dt) for w, dt in outs],
        compiler_params=_cparams("parallel", "parallel", "parallel"),
    )(*[a for a, _ in toks + bats + pars])


def _accumulate(ref, val, first):
    @pl.when(first)
    def _():
        ref[...] = val

    @pl.when(jnp.logical_not(first))
    def _():
        ref[...] += val


def _tok_bwd(fn, toks, bats, pars, cots, need, *, name, ts, wb=None, cols=1, tok_dtype=F32, loss=False):
    toks, bats, pars, cots = _with_off(toks), _with_off(bats), _with_off(pars), _with_off(cots)
    bl, s, _ = toks[0][0].shape
    ts = min(ts, s)
    nt, nb, npar, nc = len(toks), len(bats), len(pars), len(cots)
    n_in = nt + nb + npar

    def body(*refs):
        j, b, i = pl.program_id(0), pl.program_id(1), pl.program_id(2)
        outs, vjp = jax.vjp(fn, *[r[...].astype(F32) for r in refs[:n_in]])
        o = n_in + nc
        if loss:
            ct = (jnp.ones_like(outs[0]),)
            tot = jnp.broadcast_to(jnp.sum(outs[0], keepdims=True), (1, LANE))
            _accumulate(refs[o], tot, jnp.logical_and(b == 0, i == 0))
            o += 1
        else:
            ct = tuple(r[...].astype(F32) for r in refs[n_in:n_in + nc])
        grads = vjp(ct)
        for t in range(nt):
            if need[t]:
                refs[o][...] = grads[t].astype(refs[o].dtype)
                o += 1
        for t in range(nb):
            _accumulate(refs[o], grads[nt + t], i == 0)
            o += 1
        for t in range(npar):
            first = jnp.logical_and(b == 0, i == 0)
            if pars[t][1] is None:
                first = jnp.logical_and(first, j == 0)
            _accumulate(refs[o], grads[nt + nb + t], first)
            o += 1

    full = lambda arr: arr.shape[-1] if wb is None else wb * cols
    blk = lambda arr: arr.shape[-1] if wb is None else wb
    out_specs, out_shape = [], []
    if loss:
        out_specs.append(pl.BlockSpec((1, LANE), lambda j, b, i: (0, 0)))
        out_shape.append(jax.ShapeDtypeStruct((1, LANE), F32))
    for t in range(nt):
        if need[t]:
            out_specs.append(pl.BlockSpec((None, ts, blk(toks[t][0])), lambda j, b, i: (b, i, j)))
            dt = tok_dtype[t] if isinstance(tok_dtype, (list, tuple)) else tok_dtype
            out_shape.append(jax.ShapeDtypeStruct((bl, s, full(toks[t][0])), dt))
    for arr, _ in bats:
        out_specs.append(pl.BlockSpec((None, 1, blk(arr)), lambda j, b, i: (b, 0, j)))
        out_shape.append(jax.ShapeDtypeStruct((bl, 1, full(arr)), F32))
    for arr, off in pars:
        if off is None:
            out_specs.append(pl.BlockSpec(arr.shape, lambda j, b, i: (0, 0)))
            out_shape.append(jax.ShapeDtypeStruct(arr.shape, F32))
        else:
            out_specs.append(pl.BlockSpec((arr.shape[0], blk(arr)), lambda j, b, i: (0, j)))
            out_shape.append(jax.ShapeDtypeStruct((arr.shape[0], full(arr)), F32))
    res = list(pl.pallas_call(
        body, name=name, grid=(cols, bl, s // ts), in_specs=_in_specs(toks, bats, pars, cots, ts, wb),
        out_specs=out_specs, out_shape=out_shape, compiler_params=_cparams("arbitrary", "arbitrary", "arbitrary"),
    )(*[a for a, _ in toks + bats + pars + cots]))
    tot = res.pop(0) if loss else None
    dtoks = [res.pop(0) if need[t] else None for t in range(nt)]
    dbats = [res.pop(0) for _ in range(nb)]
    dpars = [res.pop(0) for _ in range(npar)]
    return (tot, dtoks, dbats, dpars) if loss else (dtoks, dbats, dpars)


def _silu(x):
    return x * jax.nn.sigmoid(x)


def _rms(x, w):
    return x * lax.rsqrt(jnp.mean(x * x, axis=-1, keepdims=True) + EPS) * w


def _f_norm_mod(x, shift, scale, w):
    return (_rms(x, w) * (1.0 + scale) + shift,)


def _f_norm_mod_skip(x, shift, scale, w):
    return _rms(x, w) * (1.0 + scale) + shift, x


def _f_res_norm_mod(x, mix, gate, shift, scale, w):
    x2 = x + gate * mix
    return x2, _rms(x2, w) * (1.0 + scale) + shift


def _f_gates(p, a_log, dt_bias, *, heads):
    z = p + dt_bias
    g = -jnp.exp(a_log) * (jnp.maximum(z, 0.0) + jnp.log1p(jnp.exp(jnp.minimum(z, -z))))
    lane = lax.broadcasted_iota(jnp.int32, p.shape, 1)
    return (jnp.where(lane < heads, g, jax.nn.sigmoid(p)),)


def _f_gdn_out(o, z, w):
    return (_rms(o, w) * _silu(z),)


def _f_merge(ga, gb, ya, yb):
    return (jax.nn.sigmoid(ga) * ya + jax.nn.sigmoid(gb) * yb,)


def _f_swiglu(a, b):
    return (_silu(a) * b,)


def _f_loss(x2, ff, tgt, gate, shift, scale, w):
    y = _rms(x2 + gate * ff, w) * (1.0 + scale) + shift
    return (0.5 * jnp.mean(jnp.square(y - tgt), axis=-1, keepdims=True),)


def _shift_down(x, s):
    if s == 0:
        return x
    row = lax.broadcasted_iota(jnp.int32, x.shape, 0)
    return jnp.where(row >= s, pltpu.roll(x, s, 0), 0.0)


def _shift_up(x, s):
    if s == 0:
        return x
    n = x.shape[0]
    row = lax.broadcasted_iota(jnp.int32, x.shape, 0)
    return jnp.where(row < n - s, pltpu.roll(x, n - s, 0), 0.0)


def _conv(x, w):
    width = w.shape[0]
    acc = w[width - 1:width, :] * x
    for j in range(width - 1):
        acc = acc + w[j:j + 1, :] * _shift_down(x, width - 1 - j)
    return acc


def _conv_bwd(dy, x, w, dw_ref, first):
    width = w.shape[0]
    dx = w[width - 1:width, :] * dy
    for j in range(width - 1):
        dx = dx + w[j:j + 1, :] * _shift_up(dy, width - 1 - j)
    for j in range(width):
        row = jnp.sum(dy * _shift_down(x, width - 1 - j), axis=0, keepdims=True)
        _accumulate(dw_ref.at[j:j + 1, :], row, first)
    return dx


def _qkv_act(xc, is_v, scale):
    a = _silu(xc)
    nrm = a * lax.rsqrt(jnp.sum(a * a, axis=-1, keepdims=True) + EPS) * scale
    return jnp.where(is_v, a, nrm)


def _qkv_consts(j, heads):
    is_v = j >= 2 * heads
    scale = jnp.where(j < heads, HEAD ** -0.5, 1.0).astype(F32)
    return is_v, scale


def _qkv_fwd(p, w, heads, name):
    bl, s, w3 = p.shape

    def body(p_ref, w_ref, o_ref):
        is_v, scale = _qkv_consts(pl.program_id(0), heads)
        o_ref[...] = _qkv_act(_conv(p_ref[...], w_ref[...]), is_v, scale)

    blk = pl.BlockSpec((None, s, HEAD), lambda j, b: (b, 0, j))
    return pl.pallas_call(
        body, name=name, grid=(w3 // HEAD, bl), in_specs=[blk, pl.BlockSpec((w.shape[0], HEAD), lambda j, b: (0, j))],
        out_specs=blk, out_shape=jax.ShapeDtypeStruct(p.shape, F32), compiler_params=_cparams("parallel", "parallel"),
    )(p, w)


def _qkv_bwd(p, w, dout, heads, name):
    bl, s, w3 = p.shape

    def body(p_ref, w_ref, d_ref, dp_ref, dw_ref):
        is_v, scale = _qkv_consts(pl.program_id(0), heads)
        x, wv = p_ref[...], w_ref[...]
        _, vjp = jax.vjp(lambda xc: _qkv_act(xc, is_v, scale), _conv(x, wv))
        (dxc,) = vjp(d_ref[...])
        dp_ref[...] = _conv_bwd(dxc, x, wv, dw_ref, pl.program_id(1) == 0).astype(dp_ref.dtype)

    blk = pl.BlockSpec((None, s, HEAD), lambda j, b: (b, 0, j))
    wblk = pl.BlockSpec((w.shape[0], HEAD), lambda j, b: (0, j))
    return pl.pallas_call(
        body, name=name, grid=(w3 // HEAD, bl), in_specs=[blk, wblk, blk], out_specs=[blk, wblk],
        out_shape=[jax.ShapeDtypeStruct(p.shape, MXU_DTYPE), jax.ShapeDtypeStruct(w.shape, F32)],
        compiler_params=_cparams("arbitrary", "arbitrary"),
    )(p, w, dout)


def _sc_specs(p, w):
    bl, s, w3 = p.shape
    nblk = w3 // 3 // LANE
    sec = lambda k: pl.BlockSpec((None, s, LANE), functools.partial(lambda j, b, k: (b, 0, k * nblk + j), k=k))
    return nblk, [sec(0), sec(1), sec(2)], pl.BlockSpec((w.shape[0], LANE), lambda j, b: (0, j)), \
        pl.BlockSpec((None, s, LANE), lambda j, b: (b, 0, j))


def _sc_fwd(p, w, name):
    bl, s, w3 = p.shape
    nblk, secs, wblk, oblk = _sc_specs(p, w)

    def body(b_ref, c_ref, x_ref, w_ref, o_ref):
        o_ref[...] = (b_ref[...] * _conv(c_ref[...] * x_ref[...], w_ref[...])).astype(o_ref.dtype)

    return pl.pallas_call(
        body, name=name, grid=(nblk, bl), in_specs=secs + [wblk], out_specs=oblk,
        out_shape=jax.ShapeDtypeStruct((bl, s, w3 // 3), MXU_DTYPE), compiler_params=_cparams("parallel", "parallel"),
    )(p, p, p, w)


def _sc_bwd(p, w, dout, name):
    bl, s, w3 = p.shape
    nblk, secs, wblk, oblk = _sc_specs(p, w)

    def body(b_ref, c_ref, x_ref, w_ref, d_ref, db_ref, dc_ref, dx_ref, dw_ref):
        gb, gc, xin, wv, d = b_ref[...], c_ref[...], x_ref[...], w_ref[...], d_ref[...]
        u = gc * xin
        db_ref[...] = (d * _conv(u, wv)).astype(db_ref.dtype)
        du = _conv_bwd(d * gb, u, wv, dw_ref, pl.program_id(1) == 0)
        dc_ref[...] = (du * xin).astype(dc_ref.dtype)
        dx_ref[...] = (du * gc).astype(dx_ref.dtype)

    act = jax.ShapeDtypeStruct((bl, s, w3 // 3), MXU_DTYPE)
    return pl.pallas_call(
        body, name=name, grid=(nblk, bl), in_specs=secs + [wblk, oblk], out_specs=[oblk, oblk, oblk, wblk],
        out_shape=[act, act, act, jax.ShapeDtypeStruct(w.shape, F32)], compiler_params=_cparams("arbitrary", "arbitrary"),
    )(p, p, p, w, dout)


def _bdot(a, b, ca, cb):
    return lax.dot_general(a.astype(MXU_DTYPE), b.astype(MXU_DTYPE), (((ca,), (cb,)), ((), ())),
                           preferred_element_type=F32)


def _hdot(a, b):
    return lax.dot_general(a, b, (((1,), (0,)), ((), ())), precision=HIGHEST, preferred_element_type=F32)


def _lane_col(x, idx):
    lane = lax.broadcasted_iota(jnp.int32, x.shape, 1)
    return jnp.sum(jnp.where(lane == idx, x, 0.0), axis=1, keepdims=True)


def _chunk_masks():
    r = lax.broadcasted_iota(jnp.int32, (CHUNK, CHUNK), 0)
    c = lax.broadcasted_iota(jnp.int32, (CHUNK, CHUNK), 1)
    return r == c, r >= c, r > c


def _dot3(a, b):
    ah, bh = a.astype(MXU_DTYPE), b.astype(MXU_DTYPE)
    al, bl = (a - ah.astype(F32)).astype(MXU_DTYPE), (b - bh.astype(F32)).astype(MXU_DTYPE)
    dot = lambda x, y: lax.dot_general(x, y, (((1,), (0,)), ((), ())), preferred_element_type=F32)
    return dot(ah, bh) + (dot(ah, bl) + dot(al, bh))


def _tri_inv_steps(low, eye):
    x = -low
    p = jnp.where(eye, 1.0, 0.0) + x
    span = 2
    while span < CHUNK:
        x = _dot3(x, x)
        yield
        p = p + _dot3(p, x)
        yield
        span *= 2
    return p


def _round_robin(gens):
    out, live = [None] * len(gens), list(range(len(gens)))
    while live:
        still = []
        for i in live:
            try:
                next(gens[i])
                still.append(i)
            except StopIteration as stop:
                out[i] = stop.value
        live = still
    return out


def _gdn_pre(q, k, v, gc, beta, masks):
    eye, causal, strict = masks
    gc_row = jnp.sum(jnp.where(eye, gc, 0.0), axis=0, keepdims=True)
    decay = jnp.where(causal, jnp.exp(jnp.where(causal, gc - gc_row, 0.0)), 0.0)
    eg = jnp.exp(gc)
    gl = gc[CHUNK - 1:CHUNK, :]
    kb, vb = k * beta, v * beta
    low = jnp.where(strict, _bdot(kb, k, 1, 1) * decay, 0.0)
    qk = jnp.where(causal, _bdot(q, k, 1, 1) * decay, 0.0)
    rest = jnp.exp(gl - gc)
    return dict(decay=decay, eg=eg, gl=gl, kb=kb, vb=vb, kbe=kb * eg, low=low, qk=qk, qg=q * eg, rest=rest, kdec=k * rest)


def _gdn_specs(qkv, gbeta, heads, rev):
    bl, s, w3 = qkv.shape
    d, n = w3 // 3, s // CHUNK
    at = (lambda c: n - 1 - c) if rev else (lambda c: c)
    assert d == heads * HEAD
    sec = pl.BlockSpec((None, CHUNK, w3), lambda b, c: (b, at(c), 0))
    gspec = pl.BlockSpec((None, CHUNK, LANE), lambda b, c: (b, at(c), 0))
    sspec = pl.BlockSpec((None, None, heads, HEAD, HEAD), lambda b, c: (b, at(c), 0, 0, 0))
    tspec = pl.BlockSpec((None, None, heads, CHUNK, CHUNK), lambda b, c: (b, at(c), 0, 0, 0))
    return bl, s, d, n, sec, gspec, sspec, tspec


def _gdn_fwd(qkv, gbeta, heads, name):
    bl, s, d, n, sec, gspec, sspec, tspec = _gdn_specs(qkv, gbeta, heads, False)

    def body(x_ref, g_ref, o_ref, s_ref, t_ref, st_ref):
        @pl.when(pl.program_id(1) == 0)
        def _():
            st_ref[...] = jnp.zeros_like(st_ref)

        masks = _chunk_masks()
        eye, causal, _ = masks
        gblk = g_ref[...]
        gc_all = _hdot(jnp.where(causal, 1.0, 0.0), gblk)
        st_all = st_ref[...]

        def head(h):
            st = st_all[h]
            q, k, v = (x_ref[:, sec * d + h * HEAD:sec * d + (h + 1) * HEAD] for sec in range(3))
            pre = _gdn_pre(q, k, v, _lane_col(gc_all, h), _lane_col(gblk, heads + h), masks)
            yield
            t = yield from _tri_inv_steps(pre["low"], eye)
            u, w = _bdot(t, pre["vb"], 1, 0), _bdot(t, pre["kbe"], 1, 0)
            yield
            vnew = u - _bdot(w, st, 1, 0)
            yield
            out = _bdot(pre["qg"], st, 1, 0) + _bdot(pre["qk"], vnew, 1, 0)
            return out, t, st * jnp.exp(pre["gl"]) + _bdot(pre["kdec"], vnew, 0, 0)

        outs, ts, states = zip(*_round_robin([head(h) for h in range(heads)]))
        o_ref[...] = jnp.concatenate(outs, axis=1)
        s_ref[...] = st_all
        t_ref[...] = jnp.stack(ts)
        st_ref[...] = jnp.stack(states)

    return pl.pallas_call(
        body, name=name, grid=(bl, n), in_specs=[sec, gspec],
        out_specs=[pl.BlockSpec((None, CHUNK, d), lambda b, c: (b, c, 0)), sspec, tspec],
        out_shape=[jax.ShapeDtypeStruct((bl, s, d), F32), jax.ShapeDtypeStruct((bl, n, heads, HEAD, HEAD), F32),
                   jax.ShapeDtypeStruct((bl, n, heads, CHUNK, CHUNK), F32)],
        scratch_shapes=[pltpu.VMEM((heads, HEAD, HEAD), F32)], compiler_params=_cparams("parallel", "arbitrary"),
    )(qkv, gbeta)


def _gdn_bwd(qkv, gbeta, dout, s_all, t_all, heads, name):
    bl, s, d, n, sec, gspec, sspec, tspec = _gdn_specs(qkv, gbeta, heads, True)
    ospec = pl.BlockSpec((None, CHUNK, d), lambda b, c: (b, n - 1 - c, 0))

    def body(x_ref, g_ref, do_ref, s_ref, t_ref, dx_ref, dg_ref, ds_ref):
        @pl.when(pl.program_id(1) == 0)
        def _():
            ds_ref[...] = jnp.zeros_like(ds_ref)

        masks = _chunk_masks()
        eye, causal, strict = masks
        gblk = g_ref[...]
        gc_all = _hdot(jnp.where(causal, 1.0, 0.0), gblk)
        lane = lax.broadcasted_iota(jnp.int32, gblk.shape, 1)
        last_row = lax.broadcasted_iota(jnp.int32, (CHUNK, 1), 0) == CHUNK - 1
        rowsum = lambda a: jnp.sum(a, axis=1, keepdims=True)
        st_all, t_all_, ds_all = s_ref[...], t_ref[...], ds_ref[...]

        def head(h):
            sl = slice(h * HEAD, (h + 1) * HEAD)
            q, k, v = (x_ref[:, sec * d + h * HEAD:sec * d + (h + 1) * HEAD] for sec in range(3))
            do = do_ref[:, sl]
            beta = _lane_col(gblk, heads + h)
            st, t, dsn = st_all[h], t_all_[h], ds_all[h]
            pre = _gdn_pre(q, k, v, _lane_col(gc_all, h), beta, masks)
            decay, eg, kb, vb, kbe, low, qk, qg, kdec = (pre[x] for x in ("decay", "eg", "kb", "vb", "kbe", "low", "qk", "qg", "kdec"))
            egl = jnp.exp(pre["gl"])
            yield
            u, w = _bdot(t, vb, 1, 0), _bdot(t, kbe, 1, 0)
            yield
            vnew = u - _bdot(w, st, 1, 0)
            yield
            dkdec = _bdot(vnew, dsn, 1, 1)
            dvnew = _bdot(kdec, dsn, 1, 0) + _bdot(qk, do, 0, 0)
            dgl = jnp.sum(dsn * st, keepdims=True) * egl
            dqg = _bdot(do, st, 1, 1)
            dqk = jnp.where(causal, _bdot(do, vnew, 1, 1), 0.0)
            yield
            dw = -_bdot(dvnew, st, 1, 1)
            ds_new = dsn * egl + _bdot(qg, do, 0, 0) - _bdot(w, dvnew, 0, 0)
            yield
            dt = _bdot(dvnew, vb, 1, 1) + _bdot(dw, kbe, 1, 1)
            dvb, dkbe = _bdot(t, dvnew, 0, 0), _bdot(t, dw, 0, 0)
            yield
            inner = _bdot(dt, t, 1, 1)
            yield
            dlow = -jnp.where(strict, _bdot(t, inner, 0, 0), 0.0)
            da, db = dlow * decay, dqk * decay
            yield
            m = dlow * low + dqk * qk
            kdk = dkdec * kdec
            col_of_m = jnp.sum(jnp.where(eye, jnp.sum(m, axis=0, keepdims=True), 0.0), axis=1, keepdims=True)
            dgc = rowsum(m) - col_of_m + rowsum(dqg * qg) + rowsum(dkbe * kbe) - rowsum(kdk)
            dgc = dgc + jnp.where(last_row, dgl + jnp.sum(kdk, keepdims=True), 0.0)
            dkb = _bdot(da, k, 1, 0) + dkbe * eg
            yield
            dk = _bdot(da, kb, 0, 0) + _bdot(db, q, 0, 0) + dkdec * pre["rest"] + dkb * beta
            dq = _bdot(db, k, 1, 0) + dqg * eg
            dbeta = rowsum(dkb * k) + rowsum(dvb * v)
            return dq, dk, dvb * beta, jnp.where(lane == h, dgc, 0.0) + jnp.where(lane == heads + h, dbeta, 0.0), ds_new

        dqs, dks, dvs, dgs, dss = zip(*_round_robin([head(h) for h in range(heads)]))
        dx_ref[...] = jnp.concatenate(dqs + dks + dvs, axis=1)
        ds_ref[...] = jnp.stack(dss)
        dgb = dgs[0]
        for extra in dgs[1:]:
            dgb = dgb + extra
        upper = jnp.where(jnp.logical_or(eye, jnp.logical_not(causal)), 1.0, 0.0)
        dg_ref[...] = jnp.where(lane < heads, _hdot(upper, dgb), dgb)

    return pl.pallas_call(
        body, name=name, grid=(bl, n), in_specs=[sec, gspec, ospec, sspec, tspec], out_specs=[sec, gspec],
        out_shape=[jax.ShapeDtypeStruct(qkv.shape, F32), jax.ShapeDtypeStruct((bl, s, LANE), F32)],
        scratch_shapes=[pltpu.VMEM((heads, HEAD, HEAD), F32)], compiler_params=_cparams("parallel", "arbitrary"),
    )(qkv, gbeta, dout, s_all, t_all)


def _position():
    return lax.axis_index("x"), lax.axis_index("y"), lax.axis_index("c")


def _all_gather(x, *, name, hbm):
    space = pltpu.HBM if hbm else pltpu.VMEM

    def body(x_ref, out_ref, send_sems, recv_sems, local_sem):
        ax, ay, ac = _position()
        me, sibling = (ax, ay, ac), (ax, ay, 1 - ac)
        chips = [(1 - ax, ay), (ax, 1 - ay), (1 - ax, 1 - ay)]

        def slot(px, py, pc):
            return out_ref.at[4 * px + 2 * py + pc]

        def copy(k, block, to, src=None):
            return pltpu.make_async_remote_copy(
                src_ref=slot(*block) if src is None else src, dst_ref=slot(*block), send_sem=send_sems.at[k],
                recv_sem=recv_sems.at[k], device_id=to, device_id_type=MESH_IDS)

        mine = pltpu.make_async_copy(x_ref, slot(*me), local_sem)
        mine.start()
        first = [copy(0, me, sibling, src=x_ref)] + [copy(1 + j, me, (*chip, ac), src=x_ref) for j, chip in enumerate(chips)]
        for cp in first:
            cp.start()
        passed = [copy(4 + j, (*chip, ac), sibling) for j, chip in enumerate(chips)]
        for j, chip in enumerate(chips):
            copy(1 + j, (*chip, ac), me).wait_recv()
            passed[j].start()
        copy(0, sibling, me).wait_recv()
        for j, chip in enumerate(chips):
            copy(4 + j, (*chip, 1 - ac), me).wait_recv()
        for cp in first + passed:
            cp.wait_send()
        mine.wait()

    return pl.pallas_call(
        body, name=name, out_shape=jax.ShapeDtypeStruct((NDEV,) + x.shape, x.dtype),
        in_specs=[pl.BlockSpec(memory_space=space)], out_specs=pl.BlockSpec(memory_space=space),
        scratch_shapes=[pltpu.SemaphoreType.DMA((7,)), pltpu.SemaphoreType.DMA((7,)), pltpu.SemaphoreType.DMA],
    )(x)


class _Rider:
    def __init__(self, arrays, out_shapes, sems, hooks):
        self.arrays, self.out_shapes, self.sems, self.hooks = arrays, out_shapes, sems, hooks


def _gather_rider(xs):
    n = len(xs)

    def hooks(x_refs, out_refs, send_sems, recv_sems):
        ax, ay, ac = _position()
        me, sibling = (ax, ay, ac), (ax, ay, 1 - ac)
        chips = [(1 - ax, ay), (ax, 1 - ay), (1 - ax, 1 - ay)]

        def copies(k, block, to, own=False):
            out = []
            for i in range(n):
                slot = out_refs[i].at[4 * block[0] + 2 * block[1] + block[2]]
                out.append(pltpu.make_async_remote_copy(
                    src_ref=x_refs[i] if own else slot, dst_ref=slot, send_sem=send_sems.at[k, i], recv_sem=recv_sems.at[k, i],
                    device_id=to, device_id_type=MESH_IDS))
            return out

        def first():
            for cp in copies(0, me, sibling, own=True):
                cp.start()
            for j, chip in enumerate(chips):
                for cp in copies(1 + j, me, (*chip, ac), own=True):
                    cp.start()

        def mid():
            for j, chip in enumerate(chips):
                for arrived, onward in zip(copies(1 + j, (*chip, ac), me), copies(4 + j, (*chip, ac), sibling)):
                    arrived.wait_recv()
                    onward.start()

        def last():
            for cp in copies(0, sibling, me):
                cp.wait_recv()
            for j, chip in enumerate(chips):
                for cp in copies(4 + j, (*chip, 1 - ac), me):
                    cp.wait_recv()
            for cp in copies(0, me, sibling, own=True):
                cp.wait_send()
            for j, chip in enumerate(chips):
                for cp in copies(1 + j, me, (*chip, ac), own=True) + copies(4 + j, (*chip, ac), sibling):
                    cp.wait_send()

        return first, mid, last

    return _Rider(list(xs), [jax.ShapeDtypeStruct((NDEV,) + x.shape, x.dtype) for x in xs],
                  [pltpu.SemaphoreType.DMA((7, n)), pltpu.SemaphoreType.DMA((7, n))], hooks)


def _scatter_rider(parts):
    packed = sum(r for _, r in parts)
    width, dtype = parts[0][0].shape[1], parts[0][0].dtype

    def hooks(g_refs, out_refs, send_sems, recv_sems):
        (recv_ref,) = out_refs
        ax, ay, ac = _position()

        def peer(rel):
            flip = lambda a, bit: 1 - a if rel & bit else a
            return flip(ax, 4), flip(ay, 2), flip(ac, 1)

        def first():
            for rel in range(1, NDEV):
                px, py, pc = peer(rel)
                off = 0
                for g_ref, (_, r) in zip(g_refs, parts):
                    rows = g_ref.at[pl.ds(pl.multiple_of((4 * px + 2 * py + pc) * r, ROW_ALIGN), r)]
                    pltpu.make_async_remote_copy(
                        src_ref=rows, dst_ref=recv_ref.at[rel - 1, pl.ds(off, r)], send_sem=send_sems.at[rel - 1],
                        recv_sem=recv_sems.at[rel - 1], device_id=(px, py, pc), device_id_type=MESH_IDS).start()
                    off += r

        def last():
            for rel in range(1, NDEV):
                slot = recv_ref.at[rel - 1]
                pltpu.make_async_remote_copy(src_ref=slot, dst_ref=slot, send_sem=send_sems.at[rel - 1],
                                             recv_sem=recv_sems.at[rel - 1], device_id=peer(rel), device_id_type=MESH_IDS).wait()

        return first, lambda: None, last

    return _Rider([g for g, _ in parts], [jax.ShapeDtypeStruct((NDEV - 1, packed, width), dtype)],
                  [pltpu.SemaphoreType.DMA((NDEV - 1,)), pltpu.SemaphoreType.DMA((NDEV - 1,))], hooks)


def _sum_direct(own, recv, name):
    r, w = own.shape
    tr = max(t for t in range(ROW_ALIGN, 257, ROW_ALIGN) if r % t == 0)

    def body(own_ref, *refs):
        acc = own_ref[...].astype(F32)
        for ref in refs[:-1]:
            acc = acc + ref[...].astype(F32)
        refs[-1][...] = acc

    rblk = lambda k: pl.BlockSpec((None, tr, w), functools.partial(lambda i, k: (k, i, 0), k=k))
    blk = pl.BlockSpec((tr, w), lambda i: (i, 0))
    return pl.pallas_call(body, name=name, grid=(r // tr,), in_specs=[blk] + [rblk(k) for k in range(NDEV - 1)],
                          out_specs=blk, out_shape=jax.ShapeDtypeStruct((r, w), F32),
                          compiler_params=_cparams("parallel"))(own, *([recv] * (NDEV - 1)))


ROW_ALIGN = 16


def _window_start(rows_per_dev, k):
    return rows_per_dev * k // ROW_ALIGN * ROW_ALIGN


def _exchange_in_chip(parts, name, collective_id):
    packed = sum(win for _, _, win, _ in parts)
    width, dtype = parts[0][0].shape[1], parts[0][0].dtype

    def body(g_refs, out_refs, send_sems, recv_sems):
        (recv_ref,) = out_refs
        ax, ay, ac = _position()
        sibling = (ax, ay, 1 - ac)
        _handshake([sibling])
        for q in range(4):
            for g_ref, (_, r, win, off) in zip(g_refs, parts):
                there = g_ref.at[pl.ds(pl.multiple_of(_window_start(r, 2 * q + 1 - ac), ROW_ALIGN), win)]
                pltpu.make_async_remote_copy(src_ref=there, dst_ref=recv_ref.at[q, pl.ds(off, win)], send_sem=send_sems.at[q],
                                             recv_sem=recv_sems.at[q], device_id=sibling, device_id_type=MESH_IDS).start()
        for q in range(4):
            pltpu.make_async_remote_copy(src_ref=recv_ref.at[q], dst_ref=recv_ref.at[q], send_sem=send_sems.at[q],
                                         recv_sem=recv_sems.at[q], device_id=sibling, device_id_type=MESH_IDS).wait()

    return _on_sequencer(body, [g for g, _, _, _ in parts], [jax.ShapeDtypeStruct((4, packed, width), dtype)],
                         [pltpu.SemaphoreType.DMA((4,)), pltpu.SemaphoreType.DMA((4,))], name=name, collective_id=collective_id)[0]


def _on_sequencer(body, ins, out_shapes, sems, *, name, collective_id):
    hbm = pltpu.MemorySpace.HBM
    in_refs = [jax.new_ref(a, memory_space=hbm) for a in ins]
    out_refs = [jax.empty_ref(s, memory_space=hbm) for s in out_shapes]

    @pl.kernel(mesh=plsc.ScalarSubcoreMesh(axis_name="sequencer", num_cores=1), name=name, scratch_types=tuple(sems),
               compiler_params=pltpu.CompilerParams(collective_id=collective_id))
    def launch(*sem_refs):
        body(in_refs, out_refs, *sem_refs)

    launch()
    return [r[...] for r in out_refs]


def _handshake(peers):
    barrier = pltpu.get_barrier_semaphore()
    for peer in peers:
        pl.semaphore_signal(barrier, inc=1, device_id=peer, device_id_type=MESH_IDS)
    pl.semaphore_wait(barrier, len(peers))


def _exchange_chips_async(s1, name, collective_id):
    def body(in_refs, out_refs, send_sems, recv_sems):
        (src,), (got,) = in_refs, out_refs
        ax, ay, ac = _position()
        chips = [(1 - ax, ay), (ax, 1 - ay), (1 - ax, 1 - ay)]
        _handshake([(cx, cy, ac) for cx, cy in chips])
        copies = [pltpu.make_async_remote_copy(
            src_ref=src.at[2 * cx + cy], dst_ref=got.at[r], send_sem=send_sems.at[r], recv_sem=recv_sems.at[r],
            device_id=(cx, cy, ac), device_id_type=MESH_IDS) for r, (cx, cy) in enumerate(chips)]
        for cp in copies:
            cp.start()
        for cp in copies:
            cp.wait_recv()
        for cp in copies:
            cp.wait_send()

    return _on_sequencer(body, [s1], [jax.ShapeDtypeStruct((3,) + s1.shape[1:], s1.dtype)],
                         [pltpu.SemaphoreType.DMA((3,)), pltpu.SemaphoreType.DMA((3,))], name=name, collective_id=collective_id)[0]


def _gather_async(xs, name, collective_id):
    rider = _gather_rider(xs)

    def body(in_refs, out_refs, send_sems, recv_sems):
        ax, ay, ac = _position()
        _handshake([(ax, ay, 1 - ac), (1 - ax, ay, ac), (ax, 1 - ay, ac), (1 - ax, 1 - ay, ac)])
        for hook in rider.hooks(in_refs, out_refs, send_sems, recv_sems):
            hook()

    return _on_sequencer(body, rider.arrays, rider.out_shapes, rider.sems, name=name, collective_id=collective_id)


def _scatter_async(parts, name, collective_id):
    rider = _scatter_rider(parts)

    def body(in_refs, out_refs, send_sems, recv_sems):
        ax, ay, ac = _position()
        flip = lambda a, on: 1 - a if on else a
        _handshake([(flip(ax, rel & 4), flip(ay, rel & 2), flip(ac, rel & 1)) for rel in range(1, NDEV)])
        for hook in rider.hooks(in_refs, out_refs, send_sems, recv_sems):
            hook()

    return _on_sequencer(body, rider.arrays, rider.out_shapes, rider.sems, name=name, collective_id=collective_id)[0]


def _sum_in_chip(own, recv, name):
    _, r, w = own.shape
    tr = _tile(r, (256, 128))

    def body(a_ref, b_ref, o_ref):
        o_ref[...] = (a_ref[...].astype(F32) + b_ref[...].astype(F32)).astype(o_ref.dtype)

    blk = pl.BlockSpec((None, tr, w), lambda q, i: (q, i, 0))
    return pl.pallas_call(body, name=name, grid=(4, r // tr), in_specs=[blk, blk], out_specs=blk,
                          out_shape=jax.ShapeDtypeStruct(own.shape, own.dtype),
                          compiler_params=_cparams("parallel", "parallel"))(own, recv)


def _sum_chips(s1, recv, chip, name):
    _, r, w = s1.shape
    tr = _tile(r, (256, 128))

    def body(c_ref, s_ref, r0_ref, r1_ref, r2_ref, o_ref):
        f = lambda ref: ref[...].astype(F32)
        o_ref[...] = ((f(s_ref) + f(r0_ref)) + f(r1_ref)) + f(r2_ref)

    rblk = lambda k: pl.BlockSpec((None, tr, w), functools.partial(lambda i, c, k: (k, i, 0), k=k))
    grid_spec = pltpu.PrefetchScalarGridSpec(
        num_scalar_prefetch=1, grid=(r // tr,),
        in_specs=[pl.BlockSpec((None, tr, w), lambda i, c: (c[0], i, 0)), rblk(0), rblk(1), rblk(2)],
        out_specs=pl.BlockSpec((tr, w), lambda i, c: (i, 0)))
    return pl.pallas_call(body, name=name, grid_spec=grid_spec, out_shape=jax.ShapeDtypeStruct((r, w), F32),
                          compiler_params=_cparams("parallel"))(chip, s1, recv, recv, recv)


def _silu_rows(x, name):
    def body(x_ref, o_ref):
        o_ref[...] = _silu(x_ref[...])

    return pl.pallas_call(body, name=name, out_shape=jax.ShapeDtypeStruct(x.shape, F32))(x)


def _row_sum(x, name):
    def body(x_ref, o_ref):
        acc = x_ref[0:1, :]
        for i in range(1, x.shape[0]):
            acc = acc + x_ref[i:i + 1, :]
        o_ref[...] = acc

    return pl.pallas_call(body, name=name, out_shape=jax.ShapeDtypeStruct((1, x.shape[1]), F32))(x)


def _adamw(w, g, m, v, name):
    cols = w.shape[-1]
    rows = w.size // cols
    tr = _tile(rows, (128,))
    tc = LANE if (tr == rows and rows > 512 and cols % LANE == 0) else cols

    def body(w_ref, g_ref, m_ref, v_ref, d_ref, mo_ref, vo_ref):
        grad = g_ref[...]
        m_new = ADAM_B1 * m_ref[...] + (1.0 - ADAM_B1) * grad
        v_new = ADAM_B2 * v_ref[...] + (1.0 - ADAM_B2) * jnp.square(grad)
        m_hat = m_new / (1.0 - ADAM_B1 ** ADAM_STEP)
        v_hat = v_new / (1.0 - ADAM_B2 ** ADAM_STEP)
        d_ref[...] = -ADAM_LR * (m_hat / (jnp.sqrt(v_hat) + ADAM_EPS) + ADAM_WD * w_ref[...])
        mo_ref[...] = m_new
        vo_ref[...] = v_new

    blk = pl.BlockSpec((tr, tc), lambda i, j: (i, j))
    out = pl.pallas_call(
        body, name=name, grid=(rows // tr, cols // tc), in_specs=[blk] * 4, out_specs=[blk] * 3,
        out_shape=[jax.ShapeDtypeStruct((rows, cols), F32)] * 3, compiler_params=_cparams("parallel", "parallel"),
    )(*[t.reshape(rows, cols) for t in (w, g, m, v)])
    return [t.reshape(w.shape) for t in out]


def _pack(parts, width, row_mult, dtype):
    flat = jnp.concatenate([p.reshape(-1).astype(dtype) for p in parts])
    rows = -(-flat.shape[0] // (width * row_mult)) * row_mult
    return jnp.pad(flat, (0, rows * width - flat.shape[0])).reshape(rows, width)


def _unpack(flat, shapes):
    out, off = [], 0
    for shp in shapes:
        size = 1
        for dim in shp:
            size *= dim
        out.append(flat[:, off:off + size].reshape((flat.shape[0],) + tuple(shp)))
        off += size
    return out


def _devices_to_cols(a):
    _, r, c = a.shape
    return a.transpose(1, 0, 2).reshape(r, NDEV * c)


def kernel(x, c, w_ada, b_ada, norm1_w, w_in, gdn_conv_w, gdn_a_log, gdn_dt_bias, gdn_norm_w, w_gdn_proj, sc_conv_w, w_sc_out, w_o, norm2_w, w_ffn_in, w_ffn_out, w_ada_f, b_ada_f, normf_w, loss_target, m_w_ada, m_b_ada, m_norm1_w, m_w_in, m_gdn_conv_w, m_gdn_a_log, m_gdn_dt_bias, m_gdn_norm_w, m_w_gdn_proj, m_sc_conv_w, m_w_sc_out, m_w_o, m_norm2_w, m_w_ffn_in, m_w_ffn_out, m_w_ada_f, m_b_ada_f, m_normf_w, v_w_ada, v_b_ada, v_norm1_w, v_w_in, v_gdn_conv_w, v_gdn_a_log, v_gdn_dt_bias, v_gdn_norm_w, v_w_gdn_proj, v_sc_conv_w, v_w_sc_out, v_w_o, v_norm2_w, v_w_ffn_in, v_w_ffn_out, v_w_ada_f, v_b_ada_f, v_normf_w):
    bl, s, d = x.shape
    heads = gdn_a_log.shape[-1]
    dff = w_ffn_out.shape[1] * NDEV
    tok = bl * s
    ax, ay, ac = _position()
    dev = 4 * ax + 2 * ay + ac
    as_tok = lambda a: a.reshape(bl, s, a.shape[-1])
    as_mat = lambda a: a.reshape(tok, a.shape[-1])

    small = _all_gather(_pack([c, gdn_conv_w, sc_conv_w], LANE, 8, F32), name="gather_cond", hbm=False)
    c_all, conv_w, sc_w = _unpack(small.reshape(NDEV, -1), [(bl, d), gdn_conv_w.shape[1:], sc_conv_w.shape[1:]])
    c_act = _silu_rows(c_all.reshape(NDEV * bl, d), "cond_silu")
    conv_w, sc_w = _devices_to_cols(conv_w), _devices_to_cols(sc_w)
    n_ada, n_adaf = w_ada.shape[-1], w_ada_f.shape[-1]
    bias = jnp.broadcast_to(lax.dynamic_slice_in_dim(b_ada, dev * n_ada, n_ada, axis=1), (NDEV * bl, n_ada))
    biasf = jnp.broadcast_to(lax.dynamic_slice_in_dim(b_ada_f.reshape(1, -1), dev * n_adaf, n_adaf, axis=1), (NDEV * bl, n_adaf))
    mod_cols = _mm(c_act, w_ada[0], add=bias, name="ada_cols")
    modf_cols = _mm(c_act, w_ada_f, add=biasf, name="adaf_cols")
    mods = _all_gather(jnp.concatenate([mod_cols, modf_cols], axis=1), name="gather_mod", hbm=False)
    mod_all = mods[:, :, :n_ada].transpose(1, 0, 2).reshape(NDEV * bl, NDEV * n_ada)
    modf_all = mods[:, :, n_ada:].transpose(1, 0, 2).reshape(NDEV * bl, NDEV * n_adaf)
    my_rows = lambda a: lax.dynamic_slice_in_dim(a, dev * bl, bl, axis=0)
    sh1, sc1, g1, sh2, sc2, g2 = [t.reshape(bl, 1, d) for t in jnp.split(my_rows(mod_all), 6, axis=1)]
    shf, scf = [t.reshape(bl, 1, d) for t in jnp.split(my_rows(modf_all), 2, axis=1)]

    late = [t.astype(MXU_DTYPE) for t in (w_gdn_proj[0], w_sc_out[0], w_o[0], w_ffn_in[0].T, w_ffn_out[0])]
    rows = [t.shape[0] for t in late] + [w_in.shape[-1]]
    offs = [sum(rows[:i]) for i in range(5)]
    in_rows = -(-rows[5] // ROW_ALIGN) * ROW_ALIGN
    in_send = jnp.pad(w_in[0].T.astype(MXU_DTYPE), ((0, in_rows - rows[5]), (0, 0)))
    with_own = lambda g, own: lax.dynamic_update_slice_in_dim(g, own[None], dev, axis=0)
    (wt_in,) = _gather_async([in_send], "gather_w_in", 1)
    wt_in = with_own(wt_in, in_send)[:, :rows[5], :].reshape(NDEV * rows[5], d)
    gathered = _gather_async(late[:3], "gather_mixer", 2) + _gather_async(late[3:], "gather_ffn", 3)
    wgp, wso, wo, wt_fi, wfo = [with_own(g, own).reshape(NDEV * own.shape[0], d) for g, own in zip(gathered, late)]
    o_z, o_ab, o_sc, o_ga, o_gb = 3 * d, 4 * d, 4 * d + 2 * heads, 7 * d + 2 * heads, 8 * d + 2 * heads
    s_qkv, s_z, s_sc, s_gate = (0, o_z), (o_z, d), (o_sc, 3 * d), (o_ga, 2 * d)
    wt_ab = jnp.pad(wt_in[o_ab:o_sc], ((0, LANE - 2 * heads), (0, 0)))

    n1w, n2w, nfw = norm1_w.reshape(1, d), norm2_w.reshape(1, d), normf_w.reshape(1, d)
    lanes = lambda a: jnp.pad(a.reshape(1, -1), ((0, 0), (0, LANE - a.size)))
    a_log, dt_bias, gnw = lanes(gdn_a_log), lanes(gdn_dt_bias), gdn_norm_w.reshape(1, HEAD)
    f_gates = functools.partial(_f_gates, heads=heads)
    (h1,) = _tok_fwd(_f_norm_mod, [x], [sh1, sc1], [n1w], [(d, MXU_DTYPE)], name="norm1", ts=512)
    h1m = as_mat(h1)
    p_qkv = as_tok(_mm(h1m, wt_in, tb=True, b_rows=s_qkv, name="in_qkv"))
    p_z = as_tok(_mm(h1m, wt_in, tb=True, b_rows=s_z, name="in_z"))
    p_ab = as_tok(_mm(h1m, wt_ab, tb=True, name="in_ab"))
    p_sc = as_tok(_mm(h1m, wt_in, tb=True, b_rows=s_sc, name="in_sc"))
    p_g = as_tok(_mm(h1m, wt_in, tb=True, b_rows=s_gate, name="in_gate"))
    qkv = _qkv_fwd(p_qkv, conv_w, heads, "qkv_conv")
    (gbeta,) = _tok_fwd(f_gates, [p_ab], [], [a_log, dt_bias], [(LANE, F32)], name="gates", ts=512)
    o, s_all, t_all = _gdn_fwd(qkv, gbeta, heads, "gdn")
    (og,) = _tok_fwd(_f_gdn_out, [o, p_z], [], [(gnw, None)], [(d, MXU_DTYPE)], name="gdn_out", ts=2048, wb=HEAD, cols=heads)
    y_a = as_tok(_mm(as_mat(og), wgp, name="gdn_proj"))
    scp = _sc_fwd(p_sc, sc_w, "sc_conv")
    y_b = as_tok(_mm(as_mat(scp), wso, name="sc_out"))
    mcols = d // 512 if d % 512 == 0 else 1
    mwb = d // mcols
    merge_toks = [(p_g, 0), (p_g, mcols), y_a, y_b]
    (mrg,) = _tok_fwd(_f_merge, merge_toks, [], [], [(d, MXU_DTYPE)], name="merge", ts=1024, wb=mwb, cols=mcols)
    mix = as_tok(_mm(as_mat(mrg), wo, name="mix_out"))
    x2, h2 = _tok_fwd(_f_res_norm_mod, [x, mix], [g1, sh2, sc2], [n2w], [(d, F32), (d, MXU_DTYPE)], name="norm2", ts=512)
    gu = as_tok(_mm(as_mat(h2), wt_fi, tb=True, out_dtype=MXU_DTYPE, name="ffn_in"))
    fwb = _tile(dff, (256, 128))
    fcols = dff // fwb
    (act,) = _tok_fwd(_f_swiglu, [(gu, 0), (gu, fcols)], [], [], [(dff, MXU_DTYPE)], name="swiglu", ts=2048, wb=fwb, cols=fcols)
    ff = as_tok(_mm(as_mat(act), wfo, name="ffn_out"))

    loss_l, (dx2, dff_out, _), (dg2, dshf, dscf), (dnfw,) = _tok_bwd(
        _f_loss, [x2, ff, loss_target], [g2, shf, scf], [nfw], [], [True, True, False], name="loss", ts=512, loss=True,
        tok_dtype=[F32, MXU_DTYPE, None])
    dffm = as_mat(dff_out)
    dact = as_tok(_mm(dffm, wfo, tb=True, name="d_ffn_out"))
    gmm = functools.partial(_mm, ta=True, out_dtype=MXU_DTYPE)
    gw_ffn_out = gmm(as_mat(act), dffm, name="g_ffn_out")
    (dgu_a, dgu_b), _, _ = _tok_bwd(_f_swiglu, [(gu, 0), (gu, fcols)], [], [], [dact], [True, True], name="d_swiglu",
                                    ts=2048, wb=fwb, cols=fcols, tok_dtype=MXU_DTYPE)
    dh2 = _mm(as_mat(dgu_a), wt_fi, b_rows=(0, dff), name="d_ffn_in_a")
    dh2 = as_tok(_mm(as_mat(dgu_b), wt_fi, b_rows=(dff, dff), add=dh2, name="d_ffn_in_b"))
    h2m = as_mat(h2)
    gwt_ffn_in = gmm(as_mat(dgu_a), h2m, out_rows=2 * dff, name="g_ffn_in_a")
    gwt_ffn_in = gmm(as_mat(dgu_b), h2m, out_rows=2 * dff, row_off=dff, into=gwt_ffn_in, name="g_ffn_in_b")
    (dx_skip, dmix), (dg1, dsh2, dsc2), (dn2w,) = _tok_bwd(
        _f_res_norm_mod, [x, mix], [g1, sh2, sc2], [n2w], [dx2, dh2], [True, True], name="d_norm2", ts=256,
        tok_dtype=[F32, MXU_DTYPE])
    dmixm = as_mat(dmix)
    dmrg = as_tok(_mm(dmixm, wo, tb=True, name="d_mix_out"))
    gw_o = gmm(as_mat(mrg), dmixm, name="g_mix_out")
    (dga, dgb, dya, dyb), _, _ = _tok_bwd(_f_merge, merge_toks, [], [], [dmrg], [True] * 4, name="d_merge", ts=512,
                                          wb=mwb, cols=mcols, tok_dtype=MXU_DTYPE)
    dyam, dybm = as_mat(dya), as_mat(dyb)
    dog = as_tok(_mm(dyam, wgp, tb=True, name="d_gdn_proj"))
    gw_gdn_proj = gmm(as_mat(og), dyam, name="g_gdn_proj")
    dscp = as_tok(_mm(dybm, wso, tb=True, name="d_sc_out"))
    gw_sc_out = gmm(as_mat(scp), dybm, name="g_sc_out")
    dscb, dscc, dscx, g_sc_w = _sc_bwd(p_sc, sc_w, dscp, "d_sc_conv")
    (do, dz), _, (g_gnw,) = _tok_bwd(_f_gdn_out, [o, p_z], [], [(gnw, None)], [dog], [True, True], name="d_gdn_out",
                                     ts=2048, wb=HEAD, cols=heads, tok_dtype=[F32, MXU_DTYPE])
    ffn_parts, mix_parts = [(gwt_ffn_in, rows[3]), (gw_ffn_out, rows[4])], [(gw_gdn_proj, rows[0]), (gw_sc_out, rows[1]), (gw_o, rows[2])]
    own_rows = lambda parts: jnp.concatenate([lax.dynamic_slice_in_dim(g, dev * r, r, axis=0) for g, r in parts], axis=0)
    ffn_recv = _scatter_async(ffn_parts, "scatter_ffn", 4)
    mix_recv = _scatter_async(mix_parts, "scatter_mixer", 5)
    dqkv, dgbeta = _gdn_bwd(qkv, gbeta, do, s_all, t_all, heads, "d_gdn")
    dp_qkv, g_conv_w = _qkv_bwd(p_qkv, conv_w, dqkv, heads, "d_qkv_conv")
    ffn_red = _sum_direct(own_rows(ffn_parts), ffn_recv, "sum_ffn")
    mix_red = _sum_direct(own_rows(mix_parts), mix_recv, "sum_mix")
    (dp_ab,), _, (g_a_log, g_dt_bias) = _tok_bwd(f_gates, [p_ab], [], [a_log, dt_bias], [dgbeta], [True], name="d_gates",
                                                 ts=512, tok_dtype=MXU_DTYPE)
    sections = [(dp_qkv, s_qkv), (dz, s_z), (dp_ab, None), (dscb, (o_sc, d)), (dscc, (o_sc + d, d)), (dscx, (o_sc + 2 * d, d)),
                (dga, (o_ga, d)), (dgb, (o_gb, d))]
    gwt_in = [gmm(as_mat(dp), h1m, name=f"g_in_{k}") for k, (dp, _) in enumerate(sections)]
    gwt_in[2] = gwt_in[2][:2 * heads]
    gwt_in = jnp.concatenate(gwt_in, axis=0)

    r_in = rows[5]
    win = -(-(r_in + max(r_in * k % ROW_ALIGN for k in range(NDEV))) // 128) * 128
    need_rows = max(_window_start(r_in, k) for k in range(NDEV)) + win
    gwt_in = jnp.pad(gwt_in, ((0, need_rows - gwt_in.shape[0]), (0, 0)))
    recv1 = _exchange_in_chip([(gwt_in, r_in, win, 0)], "scatter_in_chip", 7)
    own = jnp.stack([lax.dynamic_slice_in_dim(gwt_in, _window_start(r_in, 2 * q + ac), win, axis=0) for q in range(4)])
    s1 = _sum_in_chip(own, recv1, "sum_in_chip")
    recv2 = _exchange_chips_async(s1, "scatter_chips", 6)

    dh1 = None
    for k, (dp, sec) in enumerate(sections):
        dh1 = _mm(as_mat(dp), wt_ab if sec is None else wt_in, b_rows=sec, add=dh1, name=f"d_in_{k}")
    (grad_x,), (dsh1, dsc1), (dn1w,) = _tok_bwd(_f_norm_mod_skip, [x], [sh1, sc1], [n1w], [as_tok(dh1), dx_skip], [True],
                                                name="d_norm1", ts=256)
    reduced = _sum_chips(s1, recv2, (2 * ax + ay).reshape(1).astype(jnp.int32), "sum_chips")
    gt_w_in = lax.dynamic_slice_in_dim(reduced, r_in * dev - _window_start(r_in, dev), r_in, axis=0)
    g_w_in = gt_w_in.T.reshape(w_in.shape)
    gt_w_ffn_in = ffn_red[:rows[3]]
    g_w_ffn_in = gt_w_ffn_in.T.reshape(w_ffn_in.shape)
    g_w_ffn_out = ffn_red[rows[3]:].reshape(w_ffn_out.shape)
    g_w_gdn_proj, g_w_sc_out, g_w_o = (mix_red[offs[i]:offs[i] + rows[i]].reshape(ref.shape)
                                       for i, ref in enumerate((w_gdn_proj, w_sc_out, w_o)))

    dmod = jnp.concatenate([t.reshape(bl, d) for t in (dsh1, dsc1, dg1, dsh2, dsc2, dg2)], axis=1)
    dmodf = jnp.concatenate([t.reshape(bl, d) for t in (dshf, dscf)], axis=1)
    summed_parts = [dn1w, dn2w, dnfw, g_gnw, g_a_log, g_dt_bias, g_conv_w, g_sc_w, loss_l]
    partial = _all_gather(_pack([dmod, dmodf] + summed_parts, LANE, 8, F32), name="gather_small", hbm=False)
    partial = partial.reshape(NDEV, -1)
    n_rows = bl * (6 * d + 2 * d)
    dmod_all, dmodf_all = _unpack(partial[:, :n_rows], [(bl, 6 * d), (bl, 2 * d)])
    dmod_all, dmodf_all = dmod_all.reshape(NDEV * bl, 6 * d), dmodf_all.reshape(NDEV * bl, 2 * d)
    totals = _row_sum(partial[:, n_rows:], "sum_small")
    t_n1w, t_n2w, t_nfw, t_gnw, t_a_log, t_dt_bias, t_conv_w, t_sc_w, t_loss = [
        t[0] for t in _unpack(totals, [p.shape for p in summed_parts])]
    my_cols = lambda a, n: lax.dynamic_slice_in_dim(a, dev * n, n, axis=1)
    grads = {
        "w_ada": _mm(c_act, my_cols(dmod_all, n_ada), ta=True, name="g_ada").reshape(w_ada.shape),
        "b_ada": _row_sum(dmod_all, "g_ada_bias").reshape(b_ada.shape),
        "norm1_w": t_n1w.reshape(norm1_w.shape),
        "w_in": g_w_in,
        "gdn_conv_w": my_cols(t_conv_w, gdn_conv_w.shape[-1]).reshape(gdn_conv_w.shape),
        "gdn_a_log": t_a_log[:, :heads].reshape(gdn_a_log.shape),
        "gdn_dt_bias": t_dt_bias[:, :heads].reshape(gdn_dt_bias.shape),
        "gdn_norm_w": t_gnw.reshape(gdn_norm_w.shape),
        "w_gdn_proj": g_w_gdn_proj,
        "sc_conv_w": my_cols(t_sc_w, sc_conv_w.shape[-1]).reshape(sc_conv_w.shape),
        "w_sc_out": g_w_sc_out,
        "w_o": g_w_o,
        "norm2_w": t_n2w.reshape(norm2_w.shape),
        "w_ffn_in": g_w_ffn_in,
        "w_ffn_out": g_w_ffn_out,
        "w_ada_f": _mm(c_act, my_cols(dmodf_all, n_adaf), ta=True, name="g_adaf").reshape(w_ada_f.shape),
        "b_ada_f": _row_sum(dmodf_all, "g_adaf_bias").reshape(b_ada_f.shape),
        "normf_w": t_nfw.reshape(normf_w.shape),
    }
    weights = dict(w_ada=w_ada, b_ada=b_ada, norm1_w=norm1_w, w_in=w_in, gdn_conv_w=gdn_conv_w, gdn_a_log=gdn_a_log,
                   gdn_dt_bias=gdn_dt_bias, gdn_norm_w=gdn_norm_w, w_gdn_proj=w_gdn_proj, sc_conv_w=sc_conv_w,
                   w_sc_out=w_sc_out, w_o=w_o, norm2_w=norm2_w, w_ffn_in=w_ffn_in, w_ffn_out=w_ffn_out, w_ada_f=w_ada_f,
                   b_ada_f=b_ada_f, normf_w=normf_w)
    m_in = [m_w_ada, m_b_ada, m_norm1_w, m_w_in, m_gdn_conv_w, m_gdn_a_log, m_gdn_dt_bias, m_gdn_norm_w, m_w_gdn_proj,
            m_sc_conv_w, m_w_sc_out, m_w_o, m_norm2_w, m_w_ffn_in, m_w_ffn_out, m_w_ada_f, m_b_ada_f, m_normf_w]
    v_in = [v_w_ada, v_b_ada, v_norm1_w, v_w_in, v_gdn_conv_w, v_gdn_a_log, v_gdn_dt_bias, v_gdn_norm_w, v_w_gdn_proj,
            v_sc_conv_w, v_w_sc_out, v_w_o, v_norm2_w, v_w_ffn_in, v_w_ffn_out, v_w_ada_f, v_b_ada_f, v_normf_w]
    deltas, new_m, new_v = [], [], []
    grads_t = {"w_in": gt_w_in, "w_ffn_in": gt_w_ffn_in}
    for (wname, wt), mt, vt in zip(weights.items(), m_in, v_in):
        if wname in grads_t:
            back = lambda a, wt=wt: a.T.reshape(wt.shape)
            dl, mn, vn = (back(a) for a in _adamw(wt[0].T, grads_t[wname], mt[0].T, vt[0].T, "adamw_" + wname))
        else:
            dl, mn, vn = _adamw(wt, grads[wname], mt, vt, "adamw_" + wname)
        deltas.append(dl)
        new_m.append(mn)
        new_v.append(vn)
    loss = t_loss[0, 0]
    return (loss, grad_x, *[grads[k] for k in weights], *deltas, *new_m, *new_v)
```

```python
import functools

import jax
import jax.numpy as jnp
from jax import lax
from jax.experimental import pallas as pl
from jax.experimental.pallas import tpu as pltpu
from jax.experimental.pallas import tpu_sc as plsc

F32 = jnp.float32
MXU_DTYPE = jnp.bfloat16
NDEV = 8
CHUNK = 64
HEAD = 128
LANE = 128
EPS = 1e-6
ADAM_LR, ADAM_B1, ADAM_B2, ADAM_EPS, ADAM_WD, ADAM_STEP = 0.001, 0.9, 0.999, 1e-08, 0.01, 10
VMEM_LIMIT = 48 * 1024 * 1024
MESH_IDS = pl.DeviceIdType.MESH
HIGHEST = lax.Precision.HIGHEST


def _tile(n, cands=(512, 256, 128)):
    for c in cands:
        if n % c == 0:
            return c
    return n


def _cparams(*sem):
    return pltpu.CompilerParams(dimension_semantics=sem, vmem_limit_bytes=VMEM_LIMIT)


def _mm(a, b, *, ta=False, tb=False, add=None, out_dtype=F32, name, b_rows=None, out_rows=None, row_off=0, into=None):
    m, k = (a.shape[1], a.shape[0]) if ta else a.shape
    b_shape = b.shape if b_rows is None else (b_rows[1], b.shape[1])
    n = b_shape[0] if tb else b_shape[1]
    assert k == (b_shape[1] if tb else b_shape[0])
    if ta:
        tm, tn = _tile(m), n if n <= 1024 else _tile(n)
        tk = k if k <= 4096 else _tile(k, (4096, 2048, 1024, 512))
        if tm * tk > 1024 * 2048:
            tk = _tile(k, (2048, 1024, 512))
    else:
        tk = k if k <= 1024 else _tile(k, (1024, 512))
        tn = _tile(n, (1024 if tk <= 1024 else 512, 512, 256, 128))
        tm = _tile(m, (2048 if (tn <= 512 and tk <= 1024) else 1024, 1024, 512, 256, 128))
    nk = k // tk
    dims = (((0 if ta else 1,), (1 if tb else 0,)), ((), ()))
    has_add = add is not None

    def body(*refs):
        a_ref, b_ref = refs[0], refs[1]
        add_ref = refs[2] if has_add else None
        o_ref = refs[2 + has_add + (into is not None)]
        part = lax.dot_general(a_ref[...].astype(MXU_DTYPE), b_ref[...].astype(MXU_DTYPE), dims,
                               preferred_element_type=F32)

        def finish(acc):
            if has_add:
                acc = acc + add_ref[...]
            o_ref[...] = acc.astype(o_ref.dtype)

        if nk == 1:
            finish(part)
        else:
            acc_ref = refs[-1]
            kk = pl.program_id(2)

            @pl.when(kk == 0)
            def _():
                acc_ref[...] = part

            @pl.when(kk > 0)
            def _():
                acc_ref[...] += part

            @pl.when(kk == nk - 1)
            def _():
                finish(acc_ref[...])

    a_spec = pl.BlockSpec((tk, tm), lambda i, j, kk: (kk, i)) if ta else pl.BlockSpec((tm, tk), lambda i, j, kk: (i, kk))
    if b_rows is None:
        b_spec = pl.BlockSpec((tn, tk), lambda i, j, kk: (j, kk)) if tb else pl.BlockSpec((tk, tn), lambda i, j, kk: (kk, j))
    else:
        at = lambda t: pl.multiple_of(b_rows[0] + t, ROW_ALIGN)
        b_spec = (pl.BlockSpec((pl.Element(tn), pl.Element(tk)), lambda i, j, kk: (at(j * tn), kk * tk)) if tb else
                  pl.BlockSpec((pl.Element(tk), pl.Element(tn)), lambda i, j, kk: (at(kk * tk), j * tn)))
    add_spec = pl.BlockSpec((tm, tn), lambda i, j, kk: (i, j))
    assert row_off % tm == 0
    o_spec = pl.BlockSpec((tm, tn), lambda i, j, kk: (i + row_off // tm, j))
    in_specs = [a_spec, b_spec] + ([add_spec] if has_add else []) + ([pl.BlockSpec(memory_space=pl.ANY)] if into is not None else [])
    args = [a, b] + ([add] if has_add else []) + ([into] if into is not None else [])
    return pl.pallas_call(
        body, name=name, grid=(m // tm, n // tn, nk), in_specs=in_specs, out_specs=o_spec,
        out_shape=jax.ShapeDtypeStruct((out_rows or m, n), out_dtype),
        scratch_shapes=[pltpu.VMEM((tm, tn), F32)] if nk > 1 else [],
        input_output_aliases={len(args) - 1: 0} if into is not None else {},
        compiler_params=_cparams("parallel", "parallel", "arbitrary"),
    )(*args)


def _swiglu_tiles(m, half):
    tn = _tile(half, (512, 256, 128))
    return _tile(m, (2048 if tn <= 256 else 1024, 1024, 512, 256, 128)), tn


def _ffn_in_swiglu(h, wt, half, name):
    m, k = h.shape
    tm, tn = _swiglu_tiles(m, half)
    nj = half // tn
    dims = (((1,), (1,)), ((), ()))

    def body(h_ref, wa_ref, wb_ref, act_ref, a_ref, b_ref):
        lhs = h_ref[...].astype(MXU_DTYPE)
        a = lax.dot_general(lhs, wa_ref[...].astype(MXU_DTYPE), dims, preferred_element_type=F32)
        b = lax.dot_general(lhs, wb_ref[...].astype(MXU_DTYPE), dims, preferred_element_type=F32)
        act_ref[...] = (_silu(a) * b).astype(act_ref.dtype)
        a_ref[...] = a.astype(a_ref.dtype)
        b_ref[...] = b.astype(b_ref.dtype)

    out = jax.ShapeDtypeStruct((m, half), MXU_DTYPE)
    oblk = pl.BlockSpec((tm, tn), lambda i, j: (i, j))
    return pl.pallas_call(
        body, name=name, grid=(m // tm, nj),
        in_specs=[pl.BlockSpec((tm, k), lambda i, j: (i, 0)), pl.BlockSpec((tn, k), lambda i, j: (j, 0)),
                  pl.BlockSpec((tn, k), lambda i, j: (j + nj, 0))],
        out_specs=[oblk, oblk, oblk], out_shape=[out, out, out], compiler_params=_cparams("parallel", "parallel"),
    )(h, wt, wt)


def _ffn_out_bwd_swiglu(dff, w, a, b, name):
    m, k = dff.shape
    half = w.shape[0]
    tm, tn = _swiglu_tiles(m, half)

    def body(d_ref, w_ref, a_ref, b_ref, da_ref, db_ref):
        dact = lax.dot_general(d_ref[...].astype(MXU_DTYPE), w_ref[...].astype(MXU_DTYPE), (((1,), (1,)), ((), ())),
                               preferred_element_type=F32)
        av, bv = a_ref[...].astype(F32), b_ref[...].astype(F32)
        sig = jax.nn.sigmoid(av)
        da_ref[...] = (dact * bv * (sig * (1.0 + av * (1.0 - sig)))).astype(da_ref.dtype)
        db_ref[...] = (dact * (av * sig)).astype(db_ref.dtype)

    out = jax.ShapeDtypeStruct((m, half), MXU_DTYPE)
    oblk = pl.BlockSpec((tm, tn), lambda i, j: (i, j))
    return pl.pallas_call(
        body, name=name, grid=(m // tm, half // tn),
        in_specs=[pl.BlockSpec((tm, k), lambda i, j: (i, 0)), pl.BlockSpec((tn, k), lambda i, j: (j, 0)), oblk, oblk],
        out_specs=[oblk, oblk], out_shape=[out, out], compiler_params=_cparams("parallel", "parallel"),
    )(dff, w, a, b)


def _with_off(xs):
    return [x if isinstance(x, tuple) else (x, 0) for x in xs]


def _spec(kind, arr, off, ts, wb):
    w = arr.shape[-1] if wb is None else wb
    col = (lambda j: 0) if wb is None else functools.partial(lambda j, o: o + j, o=off)
    if kind == "tok":
        return pl.BlockSpec((None, ts, w), lambda j, b, i: (b, i, col(j)))
    if kind == "bat":
        return pl.BlockSpec((None, 1, w), lambda j, b, i: (b, 0, col(j)))
    if off is None:
        return pl.BlockSpec(arr.shape, lambda j, b, i: (0, 0))
    return pl.BlockSpec((arr.shape[0], w), lambda j, b, i: (0, col(j)))


def _in_specs(toks, bats, pars, cots, ts, wb):
    return ([_spec("tok", a, o, ts, wb) for a, o in toks] + [_spec("bat", a, o, ts, wb) for a, o in bats]
            + [_spec("par", a, o, ts, wb) for a, o in pars] + [_spec("tok", a, o, ts, wb) for a, o in cots])


def _tok_fwd(fn, toks, bats, pars, outs, *, name, ts, wb=None, cols=1):
    toks, bats, pars = _with_off(toks), _with_off(bats), _with_off(pars)
    bl, s, _ = toks[0][0].shape
    ts = min(ts, s)
    n_in = len(toks) + len(bats) + len(pars)

    def body(*refs):
        res = fn(*[r[...].astype(F32) for r in refs[:n_in]])
        for r, val in zip(refs[n_in:], res):
            r[...] = val.astype(r.dtype)

    out_specs = [pl.BlockSpec((None, ts, w if wb is None else wb), lambda j, b, i: (b, i, j)) for w, _ in outs]
    return pl.pallas_call(
        body, name=name, grid=(cols, bl, s // ts), in_specs=_in_specs(toks, bats, pars, [], ts, wb),
        out_specs=out_specs, out_shape=[jax.ShapeDtypeStruct((bl, s, w), dt) for w, dt in outs],
        compiler_params=_cparams("parallel", "parallel", "parallel"),
    )(*[a for a, _ in toks + bats + pars])


def _accumulate(ref, val, first):
    @pl.when(first)
    def _():
        ref[...] = val

    @pl.when(jnp.logical_not(first))
    def _():
        ref[...] += val


def _tok_bwd(fn, toks, bats, pars, cots, need, *, name, ts, wb=None, cols=1, tok_dtype=F32, loss=False):
    toks, bats, pars, cots = _with_off(toks), _with_off(bats), _with_off(pars), _with_off(cots)
    bl, s, _ = toks[0][0].shape
    ts = min(ts, s)
    nt, nb, npar, nc = len(toks), len(bats), len(pars), len(cots)
    n_in = nt + nb + npar

    def body(*refs):
        j, b, i = pl.program_id(0), pl.program_id(1), pl.program_id(2)
        outs, vjp = jax.vjp(fn, *[r[...].astype(F32) for r in refs[:n_in]])
        o = n_in + nc
        if loss:
            ct = (jnp.ones_like(outs[0]),)
            tot = jnp.broadcast_to(jnp.sum(outs[0], keepdims=True), (1, LANE))
            _accumulate(refs[o], tot, jnp.logical_and(b == 0, i == 0))
            o += 1
        else:
            ct = tuple(r[...].astype(F32) for r in refs[n_in:n_in + nc])
        grads = vjp(ct)
        for t in range(nt):
            if need[t]:
                refs[o][...] = grads[t].astype(refs[o].dtype)
                o += 1
        for t in range(nb):
            _accumulate(refs[o], grads[nt + t], i == 0)
            o += 1
        for t in range(npar):
            first = jnp.logical_and(b == 0, i == 0)
            if pars[t][1] is None:
                first = jnp.logical_and(first, j == 0)
            _accumulate(refs[o], grads[nt + nb + t], first)
            o += 1

    full = lambda arr: arr.shape[-1] if wb is None else wb * cols
    blk = lambda arr: arr.shape[-1] if wb is None else wb
    out_specs, out_shape = [], []
    if loss:
        out_specs.append(pl.BlockSpec((1, LANE), lambda j, b, i: (0, 0)))
        out_shape.append(jax.ShapeDtypeStruct((1, LANE), F32))
    for t in range(nt):
        if need[t]:
            out_specs.append(pl.BlockSpec((None, ts, blk(toks[t][0])), lambda j, b, i: (b, i, j)))
            dt = tok_dtype[t] if isinstance(tok_dtype, (list, tuple)) else tok_dtype
            out_shape.append(jax.ShapeDtypeStruct((bl, s, full(toks[t][0])), dt))
    for arr, _ in bats:
        out_specs.append(pl.BlockSpec((None, 1, blk(arr)), lambda j, b, i: (b, 0, j)))
        out_shape.append(jax.ShapeDtypeStruct((bl, 1, full(arr)), F32))
    for arr, off in pars:
        if off is None:
            out_specs.append(pl.BlockSpec(arr.shape, lambda j, b, i: (0, 0)))
            out_shape.append(jax.ShapeDtypeStruct(arr.shape, F32))
        else:
            out_specs.append(pl.BlockSpec((arr.shape[0], blk(arr)), lambda j, b, i: (0, j)))
            out_shape.append(jax.ShapeDtypeStruct((arr.shape[0], full(arr)), F32))
    res = list(pl.pallas_call(
        body, name=name, grid=(cols, bl, s // ts), in_specs=_in_specs(toks, bats, pars, cots, ts, wb),
        out_specs=out_specs, out_shape=out_shape, compiler_params=_cparams("arbitrary", "arbitrary", "arbitrary"),
    )(*[a for a, _ in toks + bats + pars + cots]))
    tot = res.pop(0) if loss else None
    dtoks = [res.pop(0) if need[t] else None for t in range(nt)]
    dbats = [res.pop(0) for _ in range(nb)]
    dpars = [res.pop(0) for _ in range(npar)]
    return (tot, dtoks, dbats, dpars) if loss else (dtoks, dbats, dpars)


def _silu(x):
    return x * jax.nn.sigmoid(x)


def _rms(x, w):
    return x * lax.rsqrt(jnp.mean(x * x, axis=-1, keepdims=True) + EPS) * w


def _f_norm_mod(x, shift, scale, w):
    return (_rms(x, w) * (1.0 + scale) + shift,)


def _f_norm_mod_skip(x, shift, scale, w):
    return _rms(x, w) * (1.0 + scale) + shift, x


def _f_res_norm_mod(x, mix, gate, shift, scale, w):
    x2 = x + gate * mix
    return x2, _rms(x2, w) * (1.0 + scale) + shift


def _f_gates(p, a_log, dt_bias, *, heads):
    z = p + dt_bias
    g = -jnp.exp(a_log) * (jnp.maximum(z, 0.0) + jnp.log1p(jnp.exp(jnp.minimum(z, -z))))
    lane = lax.broadcasted_iota(jnp.int32, p.shape, 1)
    return (jnp.where(lane < heads, g, jax.nn.sigmoid(p)),)


def _f_gdn_out(o, z, w):
    return (_rms(o, w) * _silu(z),)


def _f_merge(ga, gb, ya, yb):
    return (jax.nn.sigmoid(ga) * ya + jax.nn.sigmoid(gb) * yb,)


def _f_loss(x2, ff, tgt, gate, shift, scale, w):
    y = _rms(x2 + gate * ff, w) * (1.0 + scale) + shift
    return (0.5 * jnp.mean(jnp.square(y - tgt), axis=-1, keepdims=True),)


def _shift_down(x, s):
    if s == 0:
        return x
    row = lax.broadcasted_iota(jnp.int32, x.shape, 0)
    return jnp.where(row >= s, pltpu.roll(x, s, 0), 0.0)


def _shift_up(x, s):
    if s == 0:
        return x
    n = x.shape[0]
    row = lax.broadcasted_iota(jnp.int32, x.shape, 0)
    return jnp.where(row < n - s, pltpu.roll(x, n - s, 0), 0.0)


def _conv(x, w):
    width = w.shape[0]
    acc = w[width - 1:width, :] * x
    for j in range(width - 1):
        acc = acc + w[j:j + 1, :] * _shift_down(x, width - 1 - j)
    return acc


def _conv_bwd(dy, x, w, dw_ref, first):
    width = w.shape[0]
    dx = w[width - 1:width, :] * dy
    for j in range(width - 1):
        dx = dx + w[j:j + 1, :] * _shift_up(dy, width - 1 - j)
    for j in range(width):
        row = jnp.sum(dy * _shift_down(x, width - 1 - j), axis=0, keepdims=True)
        _accumulate(dw_ref.at[j:j + 1, :], row, first)
    return dx


def _qkv_act(xc, is_v, scale):
    a = _silu(xc)
    nrm = a * lax.rsqrt(jnp.sum(a * a, axis=-1, keepdims=True) + EPS) * scale
    return jnp.where(is_v, a, nrm)


def _qkv_consts(j, heads):
    is_v = j >= 2 * heads
    scale = jnp.where(j < heads, HEAD ** -0.5, 1.0).astype(F32)
    return is_v, scale


def _qkv_fwd(p, w, heads, name):
    bl, s, w3 = p.shape

    def body(p_ref, w_ref, o_ref):
        is_v, scale = _qkv_consts(pl.program_id(0), heads)
        o_ref[...] = _qkv_act(_conv(p_ref[...], w_ref[...]), is_v, scale)

    blk = pl.BlockSpec((None, s, HEAD), lambda j, b: (b, 0, j))
    return pl.pallas_call(
        body, name=name, grid=(w3 // HEAD, bl), in_specs=[blk, pl.BlockSpec((w.shape[0], HEAD), lambda j, b: (0, j))],
        out_specs=blk, out_shape=jax.ShapeDtypeStruct(p.shape, F32), compiler_params=_cparams("parallel", "parallel"),
    )(p, w)


def _qkv_bwd(p, w, dout, heads, name):
    bl, s, w3 = p.shape

    def body(p_ref, w_ref, d_ref, dp_ref, dw_ref):
        is_v, scale = _qkv_consts(pl.program_id(0), heads)
        x, wv = p_ref[...], w_ref[...]
        _, vjp = jax.vjp(lambda xc: _qkv_act(xc, is_v, scale), _conv(x, wv))
        (dxc,) = vjp(d_ref[...])
        dp_ref[...] = _conv_bwd(dxc, x, wv, dw_ref, pl.program_id(1) == 0).astype(dp_ref.dtype)

    blk = pl.BlockSpec((None, s, HEAD), lambda j, b: (b, 0, j))
    wblk = pl.BlockSpec((w.shape[0], HEAD), lambda j, b: (0, j))
    return pl.pallas_call(
        body, name=name, grid=(w3 // HEAD, bl), in_specs=[blk, wblk, blk], out_specs=[blk, wblk],
        out_shape=[jax.ShapeDtypeStruct(p.shape, MXU_DTYPE), jax.ShapeDtypeStruct(w.shape, F32)],
        compiler_params=_cparams("arbitrary", "arbitrary"),
    )(p, w, dout)


def _sc_specs(p, w):
    bl, s, w3 = p.shape
    nblk = w3 // 3 // LANE
    sec = lambda k: pl.BlockSpec((None, s, LANE), functools.partial(lambda j, b, k: (b, 0, k * nblk + j), k=k))
    return nblk, [sec(0), sec(1), sec(2)], pl.BlockSpec((w.shape[0], LANE), lambda j, b: (0, j)), \
        pl.BlockSpec((None, s, LANE), lambda j, b: (b, 0, j))


def _sc_fwd(p, w, name):
    bl, s, w3 = p.shape
    nblk, secs, wblk, oblk = _sc_specs(p, w)

    def body(b_ref, c_ref, x_ref, w_ref, o_ref):
        o_ref[...] = (b_ref[...] * _conv(c_ref[...] * x_ref[...], w_ref[...])).astype(o_ref.dtype)

    return pl.pallas_call(
        body, name=name, grid=(nblk, bl), in_specs=secs + [wblk], out_specs=oblk,
        out_shape=jax.ShapeDtypeStruct((bl, s, w3 // 3), MXU_DTYPE), compiler_params=_cparams("parallel", "parallel"),
    )(p, p, p, w)


def _sc_bwd(p, w, dout, name):
    bl, s, w3 = p.shape
    nblk, secs, wblk, oblk = _sc_specs(p, w)

    def body(b_ref, c_ref, x_ref, w_ref, d_ref, db_ref, dc_ref, dx_ref, dw_ref):
        gb, gc, xin, wv, d = b_ref[...], c_ref[...], x_ref[...], w_ref[...], d_ref[...]
        u = gc * xin
        db_ref[...] = (d * _conv(u, wv)).astype(db_ref.dtype)
        du = _conv_bwd(d * gb, u, wv, dw_ref, pl.program_id(1) == 0)
        dc_ref[...] = (du * xin).astype(dc_ref.dtype)
        dx_ref[...] = (du * gc).astype(dx_ref.dtype)

    act = jax.ShapeDtypeStruct((bl, s, w3 // 3), MXU_DTYPE)
    return pl.pallas_call(
        body, name=name, grid=(nblk, bl), in_specs=secs + [wblk, oblk], out_specs=[oblk, oblk, oblk, wblk],
        out_shape=[act, act, act, jax.ShapeDtypeStruct(w.shape, F32)], compiler_params=_cparams("arbitrary", "arbitrary"),
    )(p, p, p, w, dout)


def _bdot(a, b, ca, cb):
    return lax.dot_general(a.astype(MXU_DTYPE), b.astype(MXU_DTYPE), (((ca,), (cb,)), ((), ())),
                           preferred_element_type=F32)


def _hdot(a, b):
    return lax.dot_general(a, b, (((1,), (0,)), ((), ())), precision=HIGHEST, preferred_element_type=F32)


def _lane_col(x, idx):
    lane = lax.broadcasted_iota(jnp.int32, x.shape, 1)
    return jnp.sum(jnp.where(lane == idx, x, 0.0), axis=1, keepdims=True)


def _chunk_masks():
    r = lax.broadcasted_iota(jnp.int32, (CHUNK, CHUNK), 0)
    c = lax.broadcasted_iota(jnp.int32, (CHUNK, CHUNK), 1)
    return r == c, r >= c, r > c


def _dot3(a, b):
    ah, bh = a.astype(MXU_DTYPE), b.astype(MXU_DTYPE)
    al, bl = (a - ah.astype(F32)).astype(MXU_DTYPE), (b - bh.astype(F32)).astype(MXU_DTYPE)
    dot = lambda x, y: lax.dot_general(x, y, (((1,), (0,)), ((), ())), preferred_element_type=F32)
    return dot(ah, bh) + (dot(ah, bl) + dot(al, bh))


def _tri_inv_steps(low, eye):
    x = -low
    p = jnp.where(eye, 1.0, 0.0) + x
    span = 2
    while span < CHUNK:
        x = _dot3(x, x)
        yield
        p = p + _dot3(p, x)
        yield
        span *= 2
    return p


def _round_robin(gens):
    out, live = [None] * len(gens), list(range(len(gens)))
    while live:
        still = []
        for i in live:
            try:
                next(gens[i])
                still.append(i)
            except StopIteration as stop:
                out[i] = stop.value
        live = still
    return out


def _gdn_pre(q, k, v, gc, beta, masks):
    eye, causal, strict = masks
    gc_row = jnp.sum(jnp.where(eye, gc, 0.0), axis=0, keepdims=True)
    decay = jnp.where(causal, jnp.exp(jnp.where(causal, gc - gc_row, 0.0)), 0.0)
    eg = jnp.exp(gc)
    gl = gc[CHUNK - 1:CHUNK, :]
    kb, vb = k * beta, v * beta
    low = jnp.where(strict, _bdot(kb, k, 1, 1) * decay, 0.0)
    qk = jnp.where(causal, _bdot(q, k, 1, 1) * decay, 0.0)
    rest = jnp.exp(gl - gc)
    return dict(decay=decay, eg=eg, gl=gl, kb=kb, vb=vb, kbe=kb * eg, low=low, qk=qk, qg=q * eg, rest=rest, kdec=k * rest)


def _gdn_specs(qkv, gbeta, heads, rev):
    bl, s, w3 = qkv.shape
    d, n = w3 // 3, s // CHUNK
    at = (lambda c: n - 1 - c) if rev else (lambda c: c)
    assert d == heads * HEAD
    sec = pl.BlockSpec((None, CHUNK, w3), lambda b, c: (b, at(c), 0))
    gspec = pl.BlockSpec((None, CHUNK, LANE), lambda b, c: (b, at(c), 0))
    sspec = pl.BlockSpec((None, None, heads, HEAD, HEAD), lambda b, c: (b, at(c), 0, 0, 0))
    tspec = pl.BlockSpec((None, None, heads, CHUNK, CHUNK), lambda b, c: (b, at(c), 0, 0, 0))
    return bl, s, d, n, sec, gspec, sspec, tspec


def _gdn_fwd(qkv, gbeta, heads, name):
    bl, s, d, n, sec, gspec, sspec, tspec = _gdn_specs(qkv, gbeta, heads, False)

    def body(x_ref, g_ref, o_ref, s_ref, t_ref, st_ref):
        @pl.when(pl.program_id(1) == 0)
        def _():
            st_ref[...] = jnp.zeros_like(st_ref)

        masks = _chunk_masks()
        eye, causal, _ = masks
        gblk = g_ref[...]
        gc_all = _hdot(jnp.where(causal, 1.0, 0.0), gblk)
        st_all = st_ref[...]

        def head(h):
            st = st_all[h]
            q, k, v = (x_ref[:, sec * d + h * HEAD:sec * d + (h + 1) * HEAD] for sec in range(3))
            pre = _gdn_pre(q, k, v, _lane_col(gc_all, h), _lane_col(gblk, heads + h), masks)
            yield
            t = yield from _tri_inv_steps(pre["low"], eye)
            u, w = _bdot(t, pre["vb"], 1, 0), _bdot(t, pre["kbe"], 1, 0)
            yield
            vnew = u - _bdot(w, st, 1, 0)
            yield
            out = _bdot(pre["qg"], st, 1, 0) + _bdot(pre["qk"], vnew, 1, 0)
            return out, t, st * jnp.exp(pre["gl"]) + _bdot(pre["kdec"], vnew, 0, 0)

        outs, ts, states = zip(*_round_robin([head(h) for h in range(heads)]))
        o_ref[...] = jnp.concatenate(outs, axis=1)
        s_ref[...] = st_all
        t_ref[...] = jnp.stack(ts)
        st_ref[...] = jnp.stack(states)

    return pl.pallas_call(
        body, name=name, grid=(bl, n), in_specs=[sec, gspec],
        out_specs=[pl.BlockSpec((None, CHUNK, d), lambda b, c: (b, c, 0)), sspec, tspec],
        out_shape=[jax.ShapeDtypeStruct((bl, s, d), F32), jax.ShapeDtypeStruct((bl, n, heads, HEAD, HEAD), F32),
                   jax.ShapeDtypeStruct((bl, n, heads, CHUNK, CHUNK), F32)],
        scratch_shapes=[pltpu.VMEM((heads, HEAD, HEAD), F32)], compiler_params=_cparams("parallel", "arbitrary"),
    )(qkv, gbeta)


def _gdn_bwd(qkv, gbeta, dout, s_all, t_all, heads, name):
    bl, s, d, n, sec, gspec, sspec, tspec = _gdn_specs(qkv, gbeta, heads, True)
    ospec = pl.BlockSpec((None, CHUNK, d), lambda b, c: (b, n - 1 - c, 0))

    def body(x_ref, g_ref, do_ref, s_ref, t_ref, dx_ref, dg_ref, ds_ref):
        @pl.when(pl.program_id(1) == 0)
        def _():
            ds_ref[...] = jnp.zeros_like(ds_ref)

        masks = _chunk_masks()
        eye, causal, strict = masks
        gblk = g_ref[...]
        gc_all = _hdot(jnp.where(causal, 1.0, 0.0), gblk)
        lane = lax.broadcasted_iota(jnp.int32, gblk.shape, 1)
        last_row = lax.broadcasted_iota(jnp.int32, (CHUNK, 1), 0) == CHUNK - 1
        rowsum = lambda a: jnp.sum(a, axis=1, keepdims=True)
        st_all, t_all_, ds_all = s_ref[...], t_ref[...], ds_ref[...]

        def head(h):
            sl = slice(h * HEAD, (h + 1) * HEAD)
            q, k, v = (x_ref[:, sec * d + h * HEAD:sec * d + (h + 1) * HEAD] for sec in range(3))
            do = do_ref[:, sl]
            beta = _lane_col(gblk, heads + h)
            st, t, dsn = st_all[h], t_all_[h], ds_all[h]
            pre = _gdn_pre(q, k, v, _lane_col(gc_all, h), beta, masks)
            decay, eg, kb, vb, kbe, low, qk, qg, kdec = (pre[x] for x in ("decay", "eg", "kb", "vb", "kbe", "low", "qk", "qg", "kdec"))
            egl = jnp.exp(pre["gl"])
            yield
            u, w = _bdot(t, vb, 1, 0), _bdot(t, kbe, 1, 0)
            yield
            vnew = u - _bdot(w, st, 1, 0)
            yield
            dkdec = _bdot(vnew, dsn, 1, 1)
            dvnew = _bdot(kdec, dsn, 1, 0) + _bdot(qk, do, 0, 0)
            dgl = jnp.sum(dsn * st, keepdims=True) * egl
            dqg = _bdot(do, st, 1, 1)
            dqk = jnp.where(causal, _bdot(do, vnew, 1, 1), 0.0)
            yield
            dw = -_bdot(dvnew, st, 1, 1)
            ds_new = dsn * egl + _bdot(qg, do, 0, 0) - _bdot(w, dvnew, 0, 0)
            yield
            dt = _bdot(dvnew, vb, 1, 1) + _bdot(dw, kbe, 1, 1)
            dvb, dkbe = _bdot(t, dvnew, 0, 0), _bdot(t, dw, 0, 0)
            yield
            inner = _bdot(dt, t, 1, 1)
            yield
            dlow = -jnp.where(strict, _bdot(t, inner, 0, 0), 0.0)
            da, db = dlow * decay, dqk * decay
            yield
            m = dlow * low + dqk * qk
            kdk = dkdec * kdec
            col_of_m = jnp.sum(jnp.where(eye, jnp.sum(m, axis=0, keepdims=True), 0.0), axis=1, keepdims=True)
            dgc = rowsum(m) - col_of_m + rowsum(dqg * qg) + rowsum(dkbe * kbe) - rowsum(kdk)
            dgc = dgc + jnp.where(last_row, dgl + jnp.sum(kdk, keepdims=True), 0.0)
            dkb = _bdot(da, k, 1, 0) + dkbe * eg
            yield
            dk = _bdot(da, kb, 0, 0) + _bdot(db, q, 0, 0) + dkdec * pre["rest"] + dkb * beta
            dq = _bdot(db, k, 1, 0) + dqg * eg
            dbeta = rowsum(dkb * k) + rowsum(dvb * v)
            return dq, dk, dvb * beta, jnp.where(lane == h, dgc, 0.0) + jnp.where(lane == heads + h, dbeta, 0.0), ds_new

        dqs, dks, dvs, dgs, dss = zip(*_round_robin([head(h) for h in range(heads)]))
        dx_ref[...] = jnp.concatenate(dqs + dks + dvs, axis=1)
        ds_ref[...] = jnp.stack(dss)
        dgb = dgs[0]
        for extra in dgs[1:]:
            dgb = dgb + extra
        upper = jnp.where(jnp.logical_or(eye, jnp.logical_not(causal)), 1.0, 0.0)
        dg_ref[...] = jnp.where(lane < heads, _hdot(upper, dgb), dgb)

    return pl.pallas_call(
        body, name=name, grid=(bl, n), in_specs=[sec, gspec, ospec, sspec, tspec], out_specs=[sec, gspec],
        out_shape=[jax.ShapeDtypeStruct(qkv.shape, F32), jax.ShapeDtypeStruct((bl, s, LANE), F32)],
        scratch_shapes=[pltpu.VMEM((heads, HEAD, HEAD), F32)], compiler_params=_cparams("parallel", "arbitrary"),
    )(qkv, gbeta, dout, s_all, t_all)


def _position():
    return lax.axis_index("x"), lax.axis_index("y"), lax.axis_index("c")


def _all_gather(x, *, name, hbm):
    space = pltpu.HBM if hbm else pltpu.VMEM

    def body(x_ref, out_ref, send_sems, recv_sems, local_sem):
        ax, ay, ac = _position()
        me, sibling = (ax, ay, ac), (ax, ay, 1 - ac)
        chips = [(1 - ax, ay), (ax, 1 - ay), (1 - ax, 1 - ay)]

        def slot(px, py, pc):
            return out_ref.at[4 * px + 2 * py + pc]

        def copy(k, block, to, src=None):
            return pltpu.make_async_remote_copy(
                src_ref=slot(*block) if src is None else src, dst_ref=slot(*block), send_sem=send_sems.at[k],
                recv_sem=recv_sems.at[k], device_id=to, device_id_type=MESH_IDS)

        mine = pltpu.make_async_copy(x_ref, slot(*me), local_sem)
        mine.start()
        first = [copy(0, me, sibling, src=x_ref)] + [copy(1 + j, me, (*chip, ac), src=x_ref) for j, chip in enumerate(chips)]
        for cp in first:
            cp.start()
        passed = [copy(4 + j, (*chip, ac), sibling) for j, chip in enumerate(chips)]
        for j, chip in enumerate(chips):
            copy(1 + j, (*chip, ac), me).wait_recv()
            passed[j].start()
        copy(0, sibling, me).wait_recv()
        for j, chip in enumerate(chips):
            copy(4 + j, (*chip, 1 - ac), me).wait_recv()
        for cp in first + passed:
            cp.wait_send()
        mine.wait()

    return pl.pallas_call(
        body, name=name, out_shape=jax.ShapeDtypeStruct((NDEV,) + x.shape, x.dtype),
        in_specs=[pl.BlockSpec(memory_space=space)], out_specs=pl.BlockSpec(memory_space=space),
        scratch_shapes=[pltpu.SemaphoreType.DMA((7,)), pltpu.SemaphoreType.DMA((7,)), pltpu.SemaphoreType.DMA],
    )(x)


class _Rider:
    def __init__(self, arrays, out_shapes, sems, hooks):
        self.arrays, self.out_shapes, self.sems, self.hooks = arrays, out_shapes, sems, hooks


def _gather_rider(xs):
    n = len(xs)

    def hooks(x_refs, out_refs, send_sems, recv_sems):
        ax, ay, ac = _position()
        me, sibling = (ax, ay, ac), (ax, ay, 1 - ac)
        chips = [(1 - ax, ay), (ax, 1 - ay), (1 - ax, 1 - ay)]

        def copies(k, block, to, own=False):
            out = []
            for i in range(n):
                slot = out_refs[i].at[4 * block[0] + 2 * block[1] + block[2]]
                out.append(pltpu.make_async_remote_copy(
                    src_ref=x_refs[i] if own else slot, dst_ref=slot, send_sem=send_sems.at[k, i], recv_sem=recv_sems.at[k, i],
                    device_id=to, device_id_type=MESH_IDS))
            return out

        def first():
            for cp in copies(0, me, sibling, own=True):
                cp.start()
            for j, chip in enumerate(chips):
                for cp in copies(1 + j, me, (*chip, ac), own=True):
                    cp.start()

        def mid():
            for j, chip in enumerate(chips):
                for arrived, onward in zip(copies(1 + j, (*chip, ac), me), copies(4 + j, (*chip, ac), sibling)):
                    arrived.wait_recv()
                    onward.start()

        def last():
            for cp in copies(0, sibling, me):
                cp.wait_recv()
            for j, chip in enumerate(chips):
                for cp in copies(4 + j, (*chip, 1 - ac), me):
                    cp.wait_recv()
            for cp in copies(0, me, sibling, own=True):
                cp.wait_send()
            for j, chip in enumerate(chips):
                for cp in copies(1 + j, me, (*chip, ac), own=True) + copies(4 + j, (*chip, ac), sibling):
                    cp.wait_send()

        return first, mid, last

    return _Rider(list(xs), [jax.ShapeDtypeStruct((NDEV,) + x.shape, x.dtype) for x in xs],
                  [pltpu.SemaphoreType.DMA((7, n)), pltpu.SemaphoreType.DMA((7, n))], hooks)


def _scatter_rider(parts):
    packed = sum(r for _, r in parts)
    width, dtype = parts[0][0].shape[1], parts[0][0].dtype

    def hooks(g_refs, out_refs, send_sems, recv_sems):
        (recv_ref,) = out_refs
        ax, ay, ac = _position()

        def peer(rel):
            flip = lambda a, bit: 1 - a if rel & bit else a
            return flip(ax, 4), flip(ay, 2), flip(ac, 1)

        def first():
            for rel in range(1, NDEV):
                px, py, pc = peer(rel)
                off = 0
                for g_ref, (_, r) in zip(g_refs, parts):
                    rows = g_ref.at[pl.ds(pl.multiple_of((4 * px + 2 * py + pc) * r, ROW_ALIGN), r)]
                    pltpu.make_async_remote_copy(
                        src_ref=rows, dst_ref=recv_ref.at[rel - 1, pl.ds(off, r)], send_sem=send_sems.at[rel - 1],
                        recv_sem=recv_sems.at[rel - 1], device_id=(px, py, pc), device_id_type=MESH_IDS).start()
                    off += r

        def last():
            for rel in range(1, NDEV):
                slot = recv_ref.at[rel - 1]
                pltpu.make_async_remote_copy(src_ref=slot, dst_ref=slot, send_sem=send_sems.at[rel - 1],
                                             recv_sem=recv_sems.at[rel - 1], device_id=peer(rel), device_id_type=MESH_IDS).wait()

        return first, lambda: None, last

    return _Rider([g for g, _ in parts], [jax.ShapeDtypeStruct((NDEV - 1, packed, width), dtype)],
                  [pltpu.SemaphoreType.DMA((NDEV - 1,)), pltpu.SemaphoreType.DMA((NDEV - 1,))], hooks)


def _sum_direct(own, recv, name):
    r, w = own.shape
    tr = max(t for t in range(ROW_ALIGN, 257, ROW_ALIGN) if r % t == 0)

    def body(own_ref, *refs):
        acc = own_ref[...].astype(F32)
        for ref in refs[:-1]:
            acc = acc + ref[...].astype(F32)
        refs[-1][...] = acc

    rblk = lambda k: pl.BlockSpec((None, tr, w), functools.partial(lambda i, k: (k, i, 0), k=k))
    blk = pl.BlockSpec((tr, w), lambda i: (i, 0))
    return pl.pallas_call(body, name=name, grid=(r // tr,), in_specs=[blk] + [rblk(k) for k in range(NDEV - 1)],
                          out_specs=blk, out_shape=jax.ShapeDtypeStruct((r, w), F32),
                          compiler_params=_cparams("parallel"))(own, *([recv] * (NDEV - 1)))


ROW_ALIGN = 16


def _window_start(rows_per_dev, k):
    return rows_per_dev * k // ROW_ALIGN * ROW_ALIGN


def _exchange_in_chip(parts, name, collective_id):
    packed = sum(win for _, _, win, _ in parts)
    width, dtype = parts[0][0].shape[1], parts[0][0].dtype

    def body(g_refs, out_refs, send_sems, recv_sems):
        (recv_ref,) = out_refs
        ax, ay, ac = _position()
        sibling = (ax, ay, 1 - ac)
        _handshake([sibling])
        for q in range(4):
            for g_ref, (_, r, win, off) in zip(g_refs, parts):
                there = g_ref.at[pl.ds(pl.multiple_of(_window_start(r, 2 * q + 1 - ac), ROW_ALIGN), win)]
                pltpu.make_async_remote_copy(src_ref=there, dst_ref=recv_ref.at[q, pl.ds(off, win)], send_sem=send_sems.at[q],
                                             recv_sem=recv_sems.at[q], device_id=sibling, device_id_type=MESH_IDS).start()
        for q in range(4):
            pltpu.make_async_remote_copy(src_ref=recv_ref.at[q], dst_ref=recv_ref.at[q], send_sem=send_sems.at[q],
                                         recv_sem=recv_sems.at[q], device_id=sibling, device_id_type=MESH_IDS).wait()

    return _on_sequencer(body, [g for g, _, _, _ in parts], [jax.ShapeDtypeStruct((4, packed, width), dtype)],
                         [pltpu.SemaphoreType.DMA((4,)), pltpu.SemaphoreType.DMA((4,))], name=name, collective_id=collective_id)[0]


def _on_sequencer(body, ins, out_shapes, sems, *, name, collective_id):
    hbm = pltpu.MemorySpace.HBM
    in_refs = [jax.new_ref(a, memory_space=hbm) for a in ins]
    out_refs = [jax.empty_ref(s, memory_space=hbm) for s in out_shapes]

    @pl.kernel(mesh=plsc.ScalarSubcoreMesh(axis_name="sequencer", num_cores=1), name=name, scratch_types=tuple(sems),
               compiler_params=pltpu.CompilerParams(collective_id=collective_id))
    def launch(*sem_refs):
        body(in_refs, out_refs, *sem_refs)

    launch()
    return [r[...] for r in out_refs]


def _handshake(peers):
    barrier = pltpu.get_barrier_semaphore()
    for peer in peers:
        pl.semaphore_signal(barrier, inc=1, device_id=peer, device_id_type=MESH_IDS)
    pl.semaphore_wait(barrier, len(peers))


def _exchange_chips_async(s1, name, collective_id):
    def body(in_refs, out_refs, send_sems, recv_sems):
        (src,), (got,) = in_refs, out_refs
        ax, ay, ac = _position()
        chips = [(1 - ax, ay), (ax, 1 - ay), (1 - ax, 1 - ay)]
        _handshake([(cx, cy, ac) for cx, cy in chips])
        copies = [pltpu.make_async_remote_copy(
            src_ref=src.at[2 * cx + cy], dst_ref=got.at[r], send_sem=send_sems.at[r], recv_sem=recv_sems.at[r],
            device_id=(cx, cy, ac), device_id_type=MESH_IDS) for r, (cx, cy) in enumerate(chips)]
        for cp in copies:
            cp.start()
        for cp in copies:
            cp.wait_recv()
        for cp in copies:
            cp.wait_send()

    return _on_sequencer(body, [s1], [jax.ShapeDtypeStruct((3,) + s1.shape[1:], s1.dtype)],
                         [pltpu.SemaphoreType.DMA((3,)), pltpu.SemaphoreType.DMA((3,))], name=name, collective_id=collective_id)[0]


def _gather_async(xs, name, collective_id):
    rider = _gather_rider(xs)

    def body(in_refs, out_refs, send_sems, recv_sems):
        ax, ay, ac = _position()
        _handshake([(ax, ay, 1 - ac), (1 - ax, ay, ac), (ax, 1 - ay, ac), (1 - ax, 1 - ay, ac)])
        for hook in rider.hooks(in_refs, out_refs, send_sems, recv_sems):
            hook()

    return _on_sequencer(body, rider.arrays, rider.out_shapes, rider.sems, name=name, collective_id=collective_id)


def _scatter_async(parts, name, collective_id):
    rider = _scatter_rider(parts)

    def body(in_refs, out_refs, send_sems, recv_sems):
        ax, ay, ac = _position()
        flip = lambda a, on: 1 - a if on else a
        _handshake([(flip(ax, rel & 4), flip(ay, rel & 2), flip(ac, rel & 1)) for rel in range(1, NDEV)])
        for hook in rider.hooks(in_refs, out_refs, send_sems, recv_sems):
            hook()

    return _on_sequencer(body, rider.arrays, rider.out_shapes, rider.sems, name=name, collective_id=collective_id)[0]


def _sum_in_chip(own, recv, name):
    _, r, w = own.shape
    tr = _tile(r, (256, 128))

    def body(a_ref, b_ref, o_ref):
        o_ref[...] = (a_ref[...].astype(F32) + b_ref[...].astype(F32)).astype(o_ref.dtype)

    blk = pl.BlockSpec((None, tr, w), lambda q, i: (q, i, 0))
    return pl.pallas_call(body, name=name, grid=(4, r // tr), in_specs=[blk, blk], out_specs=blk,
                          out_shape=jax.ShapeDtypeStruct(own.shape, own.dtype),
                          compiler_params=_cparams("parallel", "parallel"))(own, recv)


def _sum_chips(s1, recv, chip, name):
    _, r, w = s1.shape
    tr = _tile(r, (256, 128))

    def body(c_ref, s_ref, r0_ref, r1_ref, r2_ref, o_ref):
        f = lambda ref: ref[...].astype(F32)
        o_ref[...] = ((f(s_ref) + f(r0_ref)) + f(r1_ref)) + f(r2_ref)

    rblk = lambda k: pl.BlockSpec((None, tr, w), functools.partial(lambda i, c, k: (k, i, 0), k=k))
    grid_spec = pltpu.PrefetchScalarGridSpec(
        num_scalar_prefetch=1, grid=(r // tr,),
        in_specs=[pl.BlockSpec((None, tr, w), lambda i, c: (c[0], i, 0)), rblk(0), rblk(1), rblk(2)],
        out_specs=pl.BlockSpec((tr, w), lambda i, c: (i, 0)))
    return pl.pallas_call(body, name=name, grid_spec=grid_spec, out_shape=jax.ShapeDtypeStruct((r, w), F32),
                          compiler_params=_cparams("parallel"))(chip, s1, recv, recv, recv)


def _silu_rows(x, name):
    def body(x_ref, o_ref):
        o_ref[...] = _silu(x_ref[...])

    return pl.pallas_call(body, name=name, out_shape=jax.ShapeDtypeStruct(x.shape, F32))(x)


def _row_sum(x, name):
    def body(x_ref, o_ref):
        acc = x_ref[0:1, :]
        for i in range(1, x.shape[0]):
            acc = acc + x_ref[i:i + 1, :]
        o_ref[...] = acc

    return pl.pallas_call(body, name=name, out_shape=jax.ShapeDtypeStruct((1, x.shape[1]), F32))(x)


def _adamw(w, g, m, v, name):
    cols = w.shape[-1]
    rows = w.size // cols
    tr = _tile(rows, (128,))
    tc = LANE if (tr == rows and rows > 512 and cols % LANE == 0) else cols

    def body(w_ref, g_ref, m_ref, v_ref, d_ref, mo_ref, vo_ref):
        grad = g_ref[...]
        m_new = ADAM_B1 * m_ref[...] + (1.0 - ADAM_B1) * grad
        v_new = ADAM_B2 * v_ref[...] + (1.0 - ADAM_B2) * jnp.square(grad)
        m_hat = m_new / (1.0 - ADAM_B1 ** ADAM_STEP)
        v_hat = v_new / (1.0 - ADAM_B2 ** ADAM_STEP)
        d_ref[...] = -ADAM_LR * (m_hat / (jnp.sqrt(v_hat) + ADAM_EPS) + ADAM_WD * w_ref[...])
        mo_ref[...] = m_new
        vo_ref[...] = v_new

    blk = pl.BlockSpec((tr, tc), lambda i, j: (i, j))
    out = pl.pallas_call(
        body, name=name, grid=(rows // tr, cols // tc), in_specs=[blk] * 4, out_specs=[blk] * 3,
        out_shape=[jax.ShapeDtypeStruct((rows, cols), F32)] * 3, compiler_params=_cparams("parallel", "parallel"),
    )(*[t.reshape(rows, cols) for t in (w, g, m, v)])
    return [t.reshape(w.shape) for t in out]


def _pack(parts, width, row_mult, dtype):
    flat = jnp.concatenate([p.reshape(-1).astype(dtype) for p in parts])
    rows = -(-flat.shape[0] // (width * row_mult)) * row_mult
    return jnp.pad(flat, (0, rows * width - flat.shape[0])).reshape(rows, width)


def _unpack(flat, shapes):
    out, off = [], 0
    for shp in shapes:
        size = 1
        for dim in shp:
            size *= dim
        out.append(flat[:, off:off + size].reshape((flat.shape[0],) + tuple(shp)))
        off += size
    return out


def _devices_to_cols(a):
    _, r, c = a.shape
    return a.transpose(1, 0, 2).reshape(r, NDEV * c)


def kernel(x, c, w_ada, b_ada, norm1_w, w_in, gdn_conv_w, gdn_a_log, gdn_dt_bias, gdn_norm_w, w_gdn_proj, sc_conv_w, w_sc_out, w_o, norm2_w, w_ffn_in, w_ffn_out, w_ada_f, b_ada_f, normf_w, loss_target, m_w_ada, m_b_ada, m_norm1_w, m_w_in, m_gdn_conv_w, m_gdn_a_log, m_gdn_dt_bias, m_gdn_norm_w, m_w_gdn_proj, m_sc_conv_w, m_w_sc_out, m_w_o, m_norm2_w, m_w_ffn_in, m_w_ffn_out, m_w_ada_f, m_b_ada_f, m_normf_w, v_w_ada, v_b_ada, v_norm1_w, v_w_in, v_gdn_conv_w, v_gdn_a_log, v_gdn_dt_bias, v_gdn_norm_w, v_w_gdn_proj, v_sc_conv_w, v_w_sc_out, v_w_o, v_norm2_w, v_w_ffn_in, v_w_ffn_out, v_w_ada_f, v_b_ada_f, v_normf_w):
    bl, s, d = x.shape
    heads = gdn_a_log.shape[-1]
    dff = w_ffn_out.shape[1] * NDEV
    tok = bl * s
    ax, ay, ac = _position()
    dev = 4 * ax + 2 * ay + ac
    as_tok = lambda a: a.reshape(bl, s, a.shape[-1])
    as_mat = lambda a: a.reshape(tok, a.shape[-1])

    small = _all_gather(_pack([c, gdn_conv_w, sc_conv_w], LANE, 8, F32), name="gather_cond", hbm=False)
    c_all, conv_w, sc_w = _unpack(small.reshape(NDEV, -1), [(bl, d), gdn_conv_w.shape[1:], sc_conv_w.shape[1:]])
    c_act = _silu_rows(c_all.reshape(NDEV * bl, d), "cond_silu")
    conv_w, sc_w = _devices_to_cols(conv_w), _devices_to_cols(sc_w)
    n_ada, n_adaf = w_ada.shape[-1], w_ada_f.shape[-1]
    bias = jnp.broadcast_to(lax.dynamic_slice_in_dim(b_ada, dev * n_ada, n_ada, axis=1), (NDEV * bl, n_ada))
    biasf = jnp.broadcast_to(lax.dynamic_slice_in_dim(b_ada_f.reshape(1, -1), dev * n_adaf, n_adaf, axis=1), (NDEV * bl, n_adaf))
    mod_cols = _mm(c_act, w_ada[0], add=bias, name="ada_cols")
    modf_cols = _mm(c_act, w_ada_f, add=biasf, name="adaf_cols")
    mods = _all_gather(jnp.concatenate([mod_cols, modf_cols], axis=1), name="gather_mod", hbm=False)
    mod_all = mods[:, :, :n_ada].transpose(1, 0, 2).reshape(NDEV * bl, NDEV * n_ada)
    modf_all = mods[:, :, n_ada:].transpose(1, 0, 2).reshape(NDEV * bl, NDEV * n_adaf)
    my_rows = lambda a: lax.dynamic_slice_in_dim(a, dev * bl, bl, axis=0)
    sh1, sc1, g1, sh2, sc2, g2 = [t.reshape(bl, 1, d) for t in jnp.split(my_rows(mod_all), 6, axis=1)]
    shf, scf = [t.reshape(bl, 1, d) for t in jnp.split(my_rows(modf_all), 2, axis=1)]

    late = [t.astype(MXU_DTYPE) for t in (w_gdn_proj[0], w_sc_out[0], w_o[0], w_ffn_in[0].T, w_ffn_out[0])]
    rows = [t.shape[0] for t in late] + [w_in.shape[-1]]
    offs = [sum(rows[:i]) for i in range(5)]
    in_rows = -(-rows[5] // ROW_ALIGN) * ROW_ALIGN
    in_send = jnp.pad(w_in[0].T.astype(MXU_DTYPE), ((0, in_rows - rows[5]), (0, 0)))
    with_own = lambda g, own: lax.dynamic_update_slice_in_dim(g, own[None], dev, axis=0)
    (wt_in,) = _gather_async([in_send], "gather_w_in", 1)
    wt_in = with_own(wt_in, in_send)[:, :rows[5], :].reshape(NDEV * rows[5], d)
    gathered = _gather_async(late[:3], "gather_mixer", 2) + _gather_async(late[3:], "gather_ffn", 3)
    wgp, wso, wo, wt_fi, wfo = [with_own(g, own).reshape(NDEV * own.shape[0], d) for g, own in zip(gathered, late)]
    o_z, o_ab, o_sc, o_ga, o_gb = 3 * d, 4 * d, 4 * d + 2 * heads, 7 * d + 2 * heads, 8 * d + 2 * heads
    s_qkv, s_z, s_sc, s_gate = (0, o_z), (o_z, d), (o_sc, 3 * d), (o_ga, 2 * d)
    wt_ab = jnp.pad(wt_in[o_ab:o_sc], ((0, LANE - 2 * heads), (0, 0)))

    n1w, n2w, nfw = norm1_w.reshape(1, d), norm2_w.reshape(1, d), normf_w.reshape(1, d)
    lanes = lambda a: jnp.pad(a.reshape(1, -1), ((0, 0), (0, LANE - a.size)))
    a_log, dt_bias, gnw = lanes(gdn_a_log), lanes(gdn_dt_bias), gdn_norm_w.reshape(1, HEAD)
    f_gates = functools.partial(_f_gates, heads=heads)
    (h1,) = _tok_fwd(_f_norm_mod, [x], [sh1, sc1], [n1w], [(d, MXU_DTYPE)], name="norm1", ts=512)
    h1m = as_mat(h1)
    p_qkv = as_tok(_mm(h1m, wt_in, tb=True, b_rows=s_qkv, name="in_qkv"))
    p_z = as_tok(_mm(h1m, wt_in, tb=True, b_rows=s_z, name="in_z"))
    p_ab = as_tok(_mm(h1m, wt_ab, tb=True, name="in_ab"))
    p_sc = as_tok(_mm(h1m, wt_in, tb=True, b_rows=s_sc, name="in_sc"))
    p_g = as_tok(_mm(h1m, wt_in, tb=True, b_rows=s_gate, name="in_gate"))
    qkv = _qkv_fwd(p_qkv, conv_w, heads, "qkv_conv")
    (gbeta,) = _tok_fwd(f_gates, [p_ab], [], [a_log, dt_bias], [(LANE, F32)], name="gates", ts=512)
    o, s_all, t_all = _gdn_fwd(qkv, gbeta, heads, "gdn")
    (og,) = _tok_fwd(_f_gdn_out, [o, p_z], [], [(gnw, None)], [(d, MXU_DTYPE)], name="gdn_out", ts=2048, wb=HEAD, cols=heads)
    y_a = as_tok(_mm(as_mat(og), wgp, name="gdn_proj"))
    scp = _sc_fwd(p_sc, sc_w, "sc_conv")
    y_b = as_tok(_mm(as_mat(scp), wso, name="sc_out"))
    mcols = d // 512 if d % 512 == 0 else 1
    mwb = d // mcols
    merge_toks = [(p_g, 0), (p_g, mcols), y_a, y_b]
    (mrg,) = _tok_fwd(_f_merge, merge_toks, [], [], [(d, MXU_DTYPE)], name="merge", ts=1024, wb=mwb, cols=mcols)
    mix = as_tok(_mm(as_mat(mrg), wo, name="mix_out"))
    x2, h2 = _tok_fwd(_f_res_norm_mod, [x, mix], [g1, sh2, sc2], [n2w], [(d, F32), (d, MXU_DTYPE)], name="norm2", ts=512)
    act, gu_a, gu_b = _ffn_in_swiglu(as_mat(h2), wt_fi, dff, "ffn_in")
    ff = as_tok(_mm(act, wfo, name="ffn_out"))

    loss_l, (dx2, dff_out, _), (dg2, dshf, dscf), (dnfw,) = _tok_bwd(
        _f_loss, [x2, ff, loss_target], [g2, shf, scf], [nfw], [], [True, True, False], name="loss", ts=512, loss=True,
        tok_dtype=[F32, MXU_DTYPE, None])
    dffm = as_mat(dff_out)
    dgu_a, dgu_b = _ffn_out_bwd_swiglu(dffm, wfo, gu_a, gu_b, "d_ffn_out")
    gmm = functools.partial(_mm, ta=True, out_dtype=MXU_DTYPE)
    gw_ffn_out = gmm(act, dffm, name="g_ffn_out")
    dh2 = _mm(dgu_a, wt_fi, b_rows=(0, dff), name="d_ffn_in_a")
    dh2 = as_tok(_mm(dgu_b, wt_fi, b_rows=(dff, dff), add=dh2, name="d_ffn_in_b"))
    h2m = as_mat(h2)
    gwt_ffn_in = gmm(dgu_a, h2m, out_rows=2 * dff, name="g_ffn_in_a")
    gwt_ffn_in = gmm(dgu_b, h2m, out_rows=2 * dff, row_off=dff, into=gwt_ffn_in, name="g_ffn_in_b")
    (dx_skip, dmix), (dg1, dsh2, dsc2), (dn2w,) = _tok_bwd(
        _f_res_norm_mod, [x, mix], [g1, sh2, sc2], [n2w], [dx2, dh2], [True, True], name="d_norm2", ts=256,
        tok_dtype=[F32, MXU_DTYPE])
    dmixm = as_mat(dmix)
    dmrg = as_tok(_mm(dmixm, wo, tb=True, name="d_mix_out"))
    gw_o = gmm(as_mat(mrg), dmixm, name="g_mix_out")
    (dga, dgb, dya, dyb), _, _ = _tok_bwd(_f_merge, merge_toks, [], [], [dmrg], [True] * 4, name="d_merge", ts=512,
                                          wb=mwb, cols=mcols, tok_dtype=MXU_DTYPE)
    dyam, dybm = as_mat(dya), as_mat(dyb)
    dog = as_tok(_mm(dyam, wgp, tb=True, name="d_gdn_proj"))
    gw_gdn_proj = gmm(as_mat(og), dyam, name="g_gdn_proj")
    dscp = as_tok(_mm(dybm, wso, tb=True, name="d_sc_out"))
    gw_sc_out = gmm(as_mat(scp), dybm, name="g_sc_out")
    dscb, dscc, dscx, g_sc_w = _sc_bwd(p_sc, sc_w, dscp, "d_sc_conv")
    (do, dz), _, (g_gnw,) = _tok_bwd(_f_gdn_out, [o, p_z], [], [(gnw, None)], [dog], [True, True], name="d_gdn_out",
                                     ts=2048, wb=HEAD, cols=heads, tok_dtype=[F32, MXU_DTYPE])
    ffn_parts, mix_parts = [(gwt_ffn_in, rows[3]), (gw_ffn_out, rows[4])], [(gw_gdn_proj, rows[0]), (gw_sc_out, rows[1]), (gw_o, rows[2])]
    own_rows = lambda parts: jnp.concatenate([lax.dynamic_slice_in_dim(g, dev * r, r, axis=0) for g, r in parts], axis=0)
    ffn_recv = _scatter_async(ffn_parts, "scatter_ffn", 4)
    mix_recv = _scatter_async(mix_parts, "scatter_mixer", 5)
    dqkv, dgbeta = _gdn_bwd(qkv, gbeta, do, s_all, t_all, heads, "d_gdn")
    dp_qkv, g_conv_w = _qkv_bwd(p_qkv, conv_w, dqkv, heads, "d_qkv_conv")
    ffn_red = _sum_direct(own_rows(ffn_parts), ffn_recv, "sum_ffn")
    mix_red = _sum_direct(own_rows(mix_parts), mix_recv, "sum_mix")
    (dp_ab,), _, (g_a_log, g_dt_bias) = _tok_bwd(f_gates, [p_ab], [], [a_log, dt_bias], [dgbeta], [True], name="d_gates",
                                                 ts=512, tok_dtype=MXU_DTYPE)
    sections = [(dp_qkv, s_qkv), (dz, s_z), (dp_ab, None), (dscb, (o_sc, d)), (dscc, (o_sc + d, d)), (dscx, (o_sc + 2 * d, d)),
                (dga, (o_ga, d)), (dgb, (o_gb, d))]
    gwt_in = [gmm(as_mat(dp), h1m, name=f"g_in_{k}") for k, (dp, _) in enumerate(sections)]
    gwt_in[2] = gwt_in[2][:2 * heads]

    r_in = rows[5]
    win = -(-(r_in + max(r_in * k % ROW_ALIGN for k in range(NDEV))) // 128) * 128
    need_rows = max(_window_start(r_in, k) for k in range(NDEV)) + win
    gwt_in = jnp.concatenate(gwt_in + [jnp.zeros((need_rows - NDEV * r_in, d), MXU_DTYPE)], axis=0)
    recv1 = _exchange_in_chip([(gwt_in, r_in, win, 0)], "scatter_in_chip", 7)
    own = jnp.stack([lax.dynamic_slice_in_dim(gwt_in, _window_start(r_in, 2 * q + ac), win, axis=0) for q in range(4)])
    s1 = _sum_in_chip(own, recv1, "sum_in_chip")
    recv2 = _exchange_chips_async(s1, "scatter_chips", 6)

    dh1 = None
    for k, (dp, sec) in enumerate(sections):
        dh1 = _mm(as_mat(dp), wt_ab if sec is None else wt_in, b_rows=sec, add=dh1, name=f"d_in_{k}")
    (grad_x,), (dsh1, dsc1), (dn1w,) = _tok_bwd(_f_norm_mod_skip, [x], [sh1, sc1], [n1w], [as_tok(dh1), dx_skip], [True],
                                                name="d_norm1", ts=256)
    reduced = _sum_chips(s1, recv2, (2 * ax + ay).reshape(1).astype(jnp.int32), "sum_chips")
    gt_w_in = lax.dynamic_slice_in_dim(reduced, r_in * dev - _window_start(r_in, dev), r_in, axis=0)
    g_w_in = gt_w_in.T.reshape(w_in.shape)
    gt_w_ffn_in = ffn_red[:rows[3]]
    g_w_ffn_in = gt_w_ffn_in.T.reshape(w_ffn_in.shape)
    g_w_ffn_out = ffn_red[rows[3]:].reshape(w_ffn_out.shape)
    g_w_gdn_proj, g_w_sc_out, g_w_o = (mix_red[offs[i]:offs[i] + rows[i]].reshape(ref.shape)
                                       for i, ref in enumerate((w_gdn_proj, w_sc_out, w_o)))

    dmod = jnp.concatenate([t.reshape(bl, d) for t in (dsh1, dsc1, dg1, dsh2, dsc2, dg2)], axis=1)
    dmodf = jnp.concatenate([t.reshape(bl, d) for t in (dshf, dscf)], axis=1)
    summed_parts = [dn1w, dn2w, dnfw, g_gnw, g_a_log, g_dt_bias, g_conv_w, g_sc_w, loss_l]
    partial = _all_gather(_pack([dmod, dmodf] + summed_parts, LANE, 8, F32), name="gather_small", hbm=False)
    partial = partial.reshape(NDEV, -1)
    n_rows = bl * (6 * d + 2 * d)
    dmod_all, dmodf_all = _unpack(partial[:, :n_rows], [(bl, 6 * d), (bl, 2 * d)])
    dmod_all, dmodf_all = dmod_all.reshape(NDEV * bl, 6 * d), dmodf_all.reshape(NDEV * bl, 2 * d)
    totals = _row_sum(partial[:, n_rows:], "sum_small")
    t_n1w, t_n2w, t_nfw, t_gnw, t_a_log, t_dt_bias, t_conv_w, t_sc_w, t_loss = [
        t[0] for t in _unpack(totals, [p.shape for p in summed_parts])]
    my_cols = lambda a, n: lax.dynamic_slice_in_dim(a, dev * n, n, axis=1)
    grads = {
        "w_ada": _mm(c_act, my_cols(dmod_all, n_ada), ta=True, name="g_ada").reshape(w_ada.shape),
        "b_ada": _row_sum(dmod_all, "g_ada_bias").reshape(b_ada.shape),
        "norm1_w": t_n1w.reshape(norm1_w.shape),
        "w_in": g_w_in,
        "gdn_conv_w": my_cols(t_conv_w, gdn_conv_w.shape[-1]).reshape(gdn_conv_w.shape),
        "gdn_a_log": t_a_log[:, :heads].reshape(gdn_a_log.shape),
        "gdn_dt_bias": t_dt_bias[:, :heads].reshape(gdn_dt_bias.shape),
        "gdn_norm_w": t_gnw.reshape(gdn_norm_w.shape),
        "w_gdn_proj": g_w_gdn_proj,
        "sc_conv_w": my_cols(t_sc_w, sc_conv_w.shape[-1]).reshape(sc_conv_w.shape),
        "w_sc_out": g_w_sc_out,
        "w_o": g_w_o,
        "norm2_w": t_n2w.reshape(norm2_w.shape),
        "w_ffn_in": g_w_ffn_in,
        "w_ffn_out": g_w_ffn_out,
        "w_ada_f": _mm(c_act, my_cols(dmodf_all, n_adaf), ta=True, name="g_adaf").reshape(w_ada_f.shape),
        "b_ada_f": _row_sum(dmodf_all, "g_adaf_bias").reshape(b_ada_f.shape),
        "normf_w": t_nfw.reshape(normf_w.shape),
    }
    weights = dict(w_ada=w_ada, b_ada=b_ada, norm1_w=norm1_w, w_in=w_in, gdn_conv_w=gdn_conv_w, gdn_a_log=gdn_a_log,
                   gdn_dt_bias=gdn_dt_bias, gdn_norm_w=gdn_norm_w, w_gdn_proj=w_gdn_proj, sc_conv_w=sc_conv_w,
                   w_sc_out=w_sc_out, w_o=w_o, norm2_w=norm2_w, w_ffn_in=w_ffn_in, w_ffn_out=w_ffn_out, w_ada_f=w_ada_f,
                   b_ada_f=b_ada_f, normf_w=normf_w)
    m_in = [m_w_ada, m_b_ada, m_norm1_w, m_w_in, m_gdn_conv_w, m_gdn_a_log, m_gdn_dt_bias, m_gdn_norm_w, m_w_gdn_proj,
            m_sc_conv_w, m_w_sc_out, m_w_o, m_norm2_w, m_w_ffn_in, m_w_ffn_out, m_w_ada_f, m_b_ada_f, m_normf_w]
    v_in = [v_w_ada, v_b_ada, v_norm1_w, v_w_in, v_gdn_conv_w, v_gdn_a_log, v_gdn_dt_bias, v_gdn_norm_w, v_w_gdn_proj,
            v_sc_conv_w, v_w_sc_out, v_w_o, v_norm2_w, v_w_ffn_in, v_w_ffn_out, v_w_ada_f, v_b_ada_f, v_normf_w]
    deltas, new_m, new_v = [], [], []
    grads_t = {"w_in": gt_w_in, "w_ffn_in": gt_w_ffn_in}
    for (wname, wt), mt, vt in zip(weights.items(), m_in, v_in):
        if wname in grads_t:
            back = lambda a, wt=wt: a.T.reshape(wt.shape)
            dl, mn, vn = (back(a) for a in _adamw(wt[0].T, grads_t[wname], mt[0].T, vt[0].T, "adamw_" + wname))
        else:
            dl, mn, vn = _adamw(wt, grads[wname], mt, vt, "adamw_" + wname)
        deltas.append(dl)
        new_m.append(mn)
        new_v.append(vn)
    loss = t_loss[0, 0]
    return (loss, grad_x, *[grads[k] for k in weights], *deltas, *new_m, *new_v)
```

```python
import functools

import jax
import jax.numpy as jnp
from jax import lax
from jax.experimental import pallas as pl
from jax.experimental.pallas import tpu as pltpu
from jax.experimental.pallas import tpu_sc as plsc

F32 = jnp.float32
MXU_DTYPE = jnp.bfloat16
NDEV = 8
CHUNK = 64
HEAD = 128
LANE = 128
EPS = 1e-6
ADAM_LR, ADAM_B1, ADAM_B2, ADAM_EPS, ADAM_WD, ADAM_STEP = 0.001, 0.9, 0.999, 1e-08, 0.01, 10
VMEM_LIMIT = 48 * 1024 * 1024
MESH_IDS = pl.DeviceIdType.MESH
HIGHEST = lax.Precision.HIGHEST


def _tile(n, cands=(512, 256, 128)):
    for c in cands:
        if n % c == 0:
            return c
    return n


def _cparams(*sem):
    return pltpu.CompilerParams(dimension_semantics=sem, vmem_limit_bytes=VMEM_LIMIT)


def _mm(a, b, *, ta=False, tb=False, add=None, out_dtype=F32, name, b_rows=None, out_rows=None, row_off=0, into=None):
    m, k = (a.shape[1], a.shape[0]) if ta else a.shape
    b_shape = b.shape if b_rows is None else (b_rows[1], b.shape[1])
    n = b_shape[0] if tb else b_shape[1]
    assert k == (b_shape[1] if tb else b_shape[0])
    if ta:
        tm, tn = _tile(m), n if n <= 1024 else _tile(n)
        tk = k if k <= 4096 else _tile(k, (4096, 2048, 1024, 512))
        if tm * tk > 1024 * 2048:
            tk = _tile(k, (2048, 1024, 512))
    else:
        tk = k if k <= 1024 else _tile(k, (1024, 512))
        tn = _tile(n, (1024 if tk <= 1024 else 512, 512, 256, 128))
        tm = _tile(m, (2048 if (tn <= 512 and tk <= 1024) else 1024, 1024, 512, 256, 128))
    nk = k // tk
    dims = (((0 if ta else 1,), (1 if tb else 0,)), ((), ()))
    has_add = add is not None

    def body(*refs):
        a_ref, b_ref = refs[0], refs[1]
        add_ref = refs[2] if has_add else None
        o_ref = refs[2 + has_add + (into is not None)]
        part = lax.dot_general(a_ref[...].astype(MXU_DTYPE), b_ref[...].astype(MXU_DTYPE), dims,
                               preferred_element_type=F32)

        def finish(acc):
            if has_add:
                acc = acc + add_ref[...]
            o_ref[...] = acc.astype(o_ref.dtype)

        if nk == 1:
            finish(part)
        else:
            acc_ref = refs[-1]
            kk = pl.program_id(2)

            @pl.when(kk == 0)
            def _():
                acc_ref[...] = part

            @pl.when(kk > 0)
            def _():
                acc_ref[...] += part

            @pl.when(kk == nk - 1)
            def _():
                finish(acc_ref[...])

    a_spec = pl.BlockSpec((tk, tm), lambda i, j, kk: (kk, i)) if ta else pl.BlockSpec((tm, tk), lambda i, j, kk: (i, kk))
    if b_rows is None:
        b_spec = pl.BlockSpec((tn, tk), lambda i, j, kk: (j, kk)) if tb else pl.BlockSpec((tk, tn), lambda i, j, kk: (kk, j))
    else:
        at = lambda t: pl.multiple_of(b_rows[0] + t, ROW_ALIGN)
        b_spec = (pl.BlockSpec((pl.Element(tn), pl.Element(tk)), lambda i, j, kk: (at(j * tn), kk * tk)) if tb else
                  pl.BlockSpec((pl.Element(tk), pl.Element(tn)), lambda i, j, kk: (at(kk * tk), j * tn)))
    add_spec = pl.BlockSpec((tm, tn), lambda i, j, kk: (i, j))
    assert row_off % tm == 0
    o_spec = pl.BlockSpec((tm, tn), lambda i, j, kk: (i + row_off // tm, j))
    in_specs = [a_spec, b_spec] + ([add_spec] if has_add else []) + ([pl.BlockSpec(memory_space=pl.ANY)] if into is not None else [])
    args = [a, b] + ([add] if has_add else []) + ([into] if into is not None else [])
    return pl.pallas_call(
        body, name=name, grid=(m // tm, n // tn, nk), in_specs=in_specs, out_specs=o_spec,
        out_shape=jax.ShapeDtypeStruct((out_rows or m, n), out_dtype),
        scratch_shapes=[pltpu.VMEM((tm, tn), F32)] if nk > 1 else [],
        input_output_aliases={len(args) - 1: 0} if into is not None else {},
        compiler_params=_cparams("parallel", "parallel", "arbitrary"),
    )(*args)


def _swiglu_tiles(m, half):
    tn = _tile(half, (512, 256, 128))
    return _tile(m, (2048 if tn <= 256 else 1024, 1024, 512, 256, 128)), tn


def _ffn_in_swiglu(h, wt, half, name):
    m, k = h.shape
    tm, tn = _swiglu_tiles(m, half)
    nj = half // tn
    dims = (((1,), (1,)), ((), ()))

    def body(h_ref, wa_ref, wb_ref, act_ref, a_ref, b_ref):
        lhs = h_ref[...].astype(MXU_DTYPE)
        a = lax.dot_general(lhs, wa_ref[...].astype(MXU_DTYPE), dims, preferred_element_type=F32)
        b = lax.dot_general(lhs, wb_ref[...].astype(MXU_DTYPE), dims, preferred_element_type=F32)
        act_ref[...] = (_silu(a) * b).astype(act_ref.dtype)
        a_ref[...] = a.astype(a_ref.dtype)
        b_ref[...] = b.astype(b_ref.dtype)

    out = jax.ShapeDtypeStruct((m, half), MXU_DTYPE)
    oblk = pl.BlockSpec((tm, tn), lambda i, j: (i, j))
    return pl.pallas_call(
        body, name=name, grid=(m // tm, nj),
        in_specs=[pl.BlockSpec((tm, k), lambda i, j: (i, 0)), pl.BlockSpec((tn, k), lambda i, j: (j, 0)),
                  pl.BlockSpec((tn, k), lambda i, j: (j + nj, 0))],
        out_specs=[oblk, oblk, oblk], out_shape=[out, out, out], compiler_params=_cparams("parallel", "parallel"),
    )(h, wt, wt)


def _ffn_out_bwd_swiglu(dff, w, a, b, name):
    m, k = dff.shape
    half = w.shape[0]
    tm, tn = _swiglu_tiles(m, half)

    def body(d_ref, w_ref, a_ref, b_ref, da_ref, db_ref):
        dact = lax.dot_general(d_ref[...].astype(MXU_DTYPE), w_ref[...].astype(MXU_DTYPE), (((1,), (1,)), ((), ())),
                               preferred_element_type=F32)
        av, bv = a_ref[...].astype(F32), b_ref[...].astype(F32)
        sig = jax.nn.sigmoid(av)
        da_ref[...] = (dact * bv * (sig * (1.0 + av * (1.0 - sig)))).astype(da_ref.dtype)
        db_ref[...] = (dact * (av * sig)).astype(db_ref.dtype)

    out = jax.ShapeDtypeStruct((m, half), MXU_DTYPE)
    oblk = pl.BlockSpec((tm, tn), lambda i, j: (i, j))
    return pl.pallas_call(
        body, name=name, grid=(m // tm, half // tn),
        in_specs=[pl.BlockSpec((tm, k), lambda i, j: (i, 0)), pl.BlockSpec((tn, k), lambda i, j: (j, 0)), oblk, oblk],
        out_specs=[oblk, oblk], out_shape=[out, out], compiler_params=_cparams("parallel", "parallel"),
    )(dff, w, a, b)


def _with_off(xs):
    return [x if isinstance(x, tuple) else (x, 0) for x in xs]


def _spec(kind, arr, off, ts, wb):
    w = arr.shape[-1] if wb is None else wb
    col = (lambda j: 0) if wb is None else functools.partial(lambda j, o: o + j, o=off)
    if kind == "tok":
        return pl.BlockSpec((None, ts, w), lambda j, b, i: (b, i, col(j)))
    if kind == "bat":
        return pl.BlockSpec((None, 1, w), lambda j, b, i: (b, 0, col(j)))
    if off is None:
        return pl.BlockSpec(arr.shape, lambda j, b, i: (0, 0))
    return pl.BlockSpec((arr.shape[0], w), lambda j, b, i: (0, col(j)))


class _Product:
    def __init__(self, a, b, *, tb=False, b_rows=None, add=None):
        self.a, self.b, self.tb, self.b_rows, self.add = a, b, tb, b_rows, add
        rows = b.shape[0] if b_rows is None else b_rows[1]
        self.shape = a.shape[:2] + (rows if tb else b.shape[1],)

    def inputs(self, ts):
        a_spec = pl.BlockSpec((None, ts, self.a.shape[2]), lambda j, b, i: (b, i, 0))
        if self.b_rows is None:
            b_spec = pl.BlockSpec(self.b.shape, lambda j, b, i: (0, 0))
        else:
            start, count = self.b_rows
            b_spec = pl.BlockSpec((pl.Element(count), pl.Element(self.b.shape[1])), lambda j, b, i: (start, 0))
        extra = [] if self.add is None else [(self.add, pl.BlockSpec((None, ts, self.shape[2]), lambda j, b, i: (b, i, 0)))]
        return [(self.a, a_spec), (self.b, b_spec)] + extra

    def value(self, refs):
        dims = (((1,), (1 if self.tb else 0,)), ((), ()))
        val = lax.dot_general(refs[0][...].astype(MXU_DTYPE), refs[1][...].astype(MXU_DTYPE), dims, preferred_element_type=F32)
        return val if self.add is None else val + refs[2][...].astype(F32)


def _inputs(groups, kinds, ts, wb):
    loaded = [(a, _spec(kind, a, o, ts, wb)) for g, kind in zip(groups, kinds) for a, o in g if not isinstance(a, _Product)]
    made = [pair for g in groups for a, _ in g if isinstance(a, _Product) for pair in a.inputs(ts)]
    return [a for a, _ in loaded + made], [sp for _, sp in loaded + made]


def _values(refs, groups):
    n_loaded = sum(1 for g in groups for a, _ in g if not isinstance(a, _Product))
    loaded, pos, out = iter(refs[:n_loaded]), n_loaded, []
    for g in groups:
        vals = []
        for a, _ in g:
            if isinstance(a, _Product):
                k = 2 if a.add is None else 3
                vals.append(a.value(refs[pos:pos + k]))
                pos += k
            else:
                vals.append(next(loaded)[...].astype(F32))
        out.append(vals)
    return out, pos


def _tok_fwd(fn, toks, bats, pars, outs, *, name, ts, wb=None, cols=1):
    groups = [_with_off(toks), _with_off(bats), _with_off(pars)]
    bl, s, _ = groups[0][0][0].shape
    ts = min(ts, s)
    args, in_specs = _inputs(groups, ("tok", "bat", "par"), ts, wb)

    def body(*refs):
        vals, n_in = _values(refs, groups)
        res = fn(*[v for g in vals for v in g])
        for r, val in zip(refs[n_in:], res):
            r[...] = val.astype(r.dtype)

    out_specs = [pl.BlockSpec((None, ts, w if wb is None else wb), lambda j, b, i: (b, i, j)) for w, _ in outs]
    return pl.pallas_call(
        body, name=name, grid=(cols, bl, s // ts), in_specs=in_specs,
        out_specs=out_specs, out_shape=[jax.ShapeDtypeStruct((bl, s, w), dt) for w, dt in outs],
        compiler_params=_cparams("parallel", "parallel", "parallel"),
    )(*args)


def _accumulate(ref, val, first):
    @pl.when(first)
    def _():
        ref[...] = val

    @pl.when(jnp.logical_not(first))
    def _():
        ref[...] += val


def _tok_bwd(fn, toks, bats, pars, cots, need, *, name, ts, wb=None, cols=1, tok_dtype=F32, loss=False):
    toks, bats, pars, cots = _with_off(toks), _with_off(bats), _with_off(pars), _with_off(cots)
    groups = [toks, bats, pars, cots]
    bl, s, _ = toks[0][0].shape
    ts = min(ts, s)
    nt, nb, npar = len(toks), len(bats), len(pars)
    args, in_specs = _inputs(groups, ("tok", "bat", "par", "tok"), ts, wb)

    def body(*refs):
        j, b, i = pl.program_id(0), pl.program_id(1), pl.program_id(2)
        (tok_vals, bat_vals, par_vals, cot_vals), o = _values(refs, groups)
        outs, vjp = jax.vjp(fn, *tok_vals, *bat_vals, *par_vals)
        if loss:
            ct = (jnp.ones_like(outs[0]),)
            tot = jnp.broadcast_to(jnp.sum(outs[0], keepdims=True), (1, LANE))
            _accumulate(refs[o], tot, jnp.logical_and(b == 0, i == 0))
            o += 1
        else:
            ct = tuple(cot_vals)
        grads = vjp(ct)
        for t in range(nt):
            if need[t]:
                refs[o][...] = grads[t].astype(refs[o].dtype)
                o += 1
        for t in range(nb):
            _accumulate(refs[o], grads[nt + t], i == 0)
            o += 1
        for t in range(npar):
            first = jnp.logical_and(b == 0, i == 0)
            if pars[t][1] is None:
                first = jnp.logical_and(first, j == 0)
            _accumulate(refs[o], grads[nt + nb + t], first)
            o += 1

    full = lambda arr: arr.shape[-1] if wb is None else wb * cols
    blk = lambda arr: arr.shape[-1] if wb is None else wb
    out_specs, out_shape = [], []
    if loss:
        out_specs.append(pl.BlockSpec((1, LANE), lambda j, b, i: (0, 0)))
        out_shape.append(jax.ShapeDtypeStruct((1, LANE), F32))
    for t in range(nt):
        if need[t]:
            out_specs.append(pl.BlockSpec((None, ts, blk(toks[t][0])), lambda j, b, i: (b, i, j)))
            dt = tok_dtype[t] if isinstance(tok_dtype, (list, tuple)) else tok_dtype
            out_shape.append(jax.ShapeDtypeStruct((bl, s, full(toks[t][0])), dt))
    for arr, _ in bats:
        out_specs.append(pl.BlockSpec((None, 1, blk(arr)), lambda j, b, i: (b, 0, j)))
        out_shape.append(jax.ShapeDtypeStruct((bl, 1, full(arr)), F32))
    for arr, off in pars:
        if off is None:
            out_specs.append(pl.BlockSpec(arr.shape, lambda j, b, i: (0, 0)))
            out_shape.append(jax.ShapeDtypeStruct(arr.shape, F32))
        else:
            out_specs.append(pl.BlockSpec((arr.shape[0], blk(arr)), lambda j, b, i: (0, j)))
            out_shape.append(jax.ShapeDtypeStruct((arr.shape[0], full(arr)), F32))
    res = list(pl.pallas_call(
        body, name=name, grid=(cols, bl, s // ts), in_specs=in_specs,
        out_specs=out_specs, out_shape=out_shape, compiler_params=_cparams("arbitrary", "arbitrary", "arbitrary"),
    )(*args))
    tot = res.pop(0) if loss else None
    dtoks = [res.pop(0) if need[t] else None for t in range(nt)]
    dbats = [res.pop(0) for _ in range(nb)]
    dpars = [res.pop(0) for _ in range(npar)]
    return (tot, dtoks, dbats, dpars) if loss else (dtoks, dbats, dpars)


def _silu(x):
    return x * jax.nn.sigmoid(x)


def _rms(x, w):
    return x * lax.rsqrt(jnp.mean(x * x, axis=-1, keepdims=True) + EPS) * w


def _f_norm_mod(x, shift, scale, w):
    return (_rms(x, w) * (1.0 + scale) + shift,)


def _f_norm_mod_skip(x, shift, scale, w):
    return _rms(x, w) * (1.0 + scale) + shift, x


def _f_res_norm_mod(x, mix, gate, shift, scale, w):
    x2 = x + gate * mix
    return x2, _rms(x2, w) * (1.0 + scale) + shift


def _f_res_norm_mod_keep(x, mix, gate, shift, scale, w):
    return (*_f_res_norm_mod(x, mix, gate, shift, scale, w), mix)


def _f_gates(p, a_log, dt_bias, *, heads):
    z = p + dt_bias
    g = -jnp.exp(a_log) * (jnp.maximum(z, 0.0) + jnp.log1p(jnp.exp(jnp.minimum(z, -z))))
    lane = lax.broadcasted_iota(jnp.int32, p.shape, 1)
    return (jnp.where(lane < heads, g, jax.nn.sigmoid(p)),)


def _f_gdn_out(o, z, w):
    return (_rms(o, w) * _silu(z),)


def _f_merge(ga, gb, ya, yb):
    return (jax.nn.sigmoid(ga) * ya + jax.nn.sigmoid(gb) * yb,)


def _f_merge_keep(ga, gb, ya, yb):
    return (*_f_merge(ga, gb, ya, yb), yb)


def _f_loss(x2, ff, tgt, gate, shift, scale, w):
    y = _rms(x2 + gate * ff, w) * (1.0 + scale) + shift
    return (0.5 * jnp.mean(jnp.square(y - tgt), axis=-1, keepdims=True),)


def _shift_down(x, s):
    if s == 0:
        return x
    row = lax.broadcasted_iota(jnp.int32, x.shape, 0)
    return jnp.where(row >= s, pltpu.roll(x, s, 0), 0.0)


def _shift_up(x, s):
    if s == 0:
        return x
    n = x.shape[0]
    row = lax.broadcasted_iota(jnp.int32, x.shape, 0)
    return jnp.where(row < n - s, pltpu.roll(x, n - s, 0), 0.0)


def _conv(x, w):
    width = w.shape[0]
    acc = w[width - 1:width, :] * x
    for j in range(width - 1):
        acc = acc + w[j:j + 1, :] * _shift_down(x, width - 1 - j)
    return acc


def _conv_bwd(dy, x, w, dw_ref, first):
    width = w.shape[0]
    dx = w[width - 1:width, :] * dy
    for j in range(width - 1):
        dx = dx + w[j:j + 1, :] * _shift_up(dy, width - 1 - j)
    for j in range(width):
        row = jnp.sum(dy * _shift_down(x, width - 1 - j), axis=0, keepdims=True)
        _accumulate(dw_ref.at[j:j + 1, :], row, first)
    return dx


def _qkv_act(xc, is_v, scale):
    a = _silu(xc)
    nrm = a * lax.rsqrt(jnp.sum(a * a, axis=-1, keepdims=True) + EPS) * scale
    return jnp.where(is_v, a, nrm)


def _qkv_consts(j, heads):
    is_v = j >= 2 * heads
    scale = jnp.where(j < heads, HEAD ** -0.5, 1.0).astype(F32)
    return is_v, scale


def _qkv_fwd(p, w, heads, name):
    bl, s, w3 = p.shape

    def body(p_ref, w_ref, o_ref):
        is_v, scale = _qkv_consts(pl.program_id(0), heads)
        o_ref[...] = _qkv_act(_conv(p_ref[...], w_ref[...]), is_v, scale)

    blk = pl.BlockSpec((None, s, HEAD), lambda j, b: (b, 0, j))
    return pl.pallas_call(
        body, name=name, grid=(w3 // HEAD, bl), in_specs=[blk, pl.BlockSpec((w.shape[0], HEAD), lambda j, b: (0, j))],
        out_specs=blk, out_shape=jax.ShapeDtypeStruct(p.shape, F32), compiler_params=_cparams("parallel", "parallel"),
    )(p, w)


def _qkv_bwd(p, w, dout, heads, name):
    bl, s, w3 = p.shape

    def body(p_ref, w_ref, d_ref, dp_ref, dw_ref):
        is_v, scale = _qkv_consts(pl.program_id(0), heads)
        x, wv = p_ref[...], w_ref[...]
        _, vjp = jax.vjp(lambda xc: _qkv_act(xc, is_v, scale), _conv(x, wv))
        (dxc,) = vjp(d_ref[...])
        dp_ref[...] = _conv_bwd(dxc, x, wv, dw_ref, pl.program_id(1) == 0).astype(dp_ref.dtype)

    blk = pl.BlockSpec((None, s, HEAD), lambda j, b: (b, 0, j))
    wblk = pl.BlockSpec((w.shape[0], HEAD), lambda j, b: (0, j))
    return pl.pallas_call(
        body, name=name, grid=(w3 // HEAD, bl), in_specs=[blk, wblk, blk], out_specs=[blk, wblk],
        out_shape=[jax.ShapeDtypeStruct(p.shape, MXU_DTYPE), jax.ShapeDtypeStruct(w.shape, F32)],
        compiler_params=_cparams("arbitrary", "arbitrary"),
    )(p, w, dout)


def _sc_specs(p, w):
    bl, s, w3 = p.shape
    nblk = w3 // 3 // LANE
    sec = lambda k: pl.BlockSpec((None, s, LANE), functools.partial(lambda j, b, k: (b, 0, k * nblk + j), k=k))
    return nblk, [sec(0), sec(1), sec(2)], pl.BlockSpec((w.shape[0], LANE), lambda j, b: (0, j)), \
        pl.BlockSpec((None, s, LANE), lambda j, b: (b, 0, j))


def _sc_fwd(p, w, name):
    bl, s, w3 = p.shape
    nblk, secs, wblk, oblk = _sc_specs(p, w)

    def body(b_ref, c_ref, x_ref, w_ref, o_ref):
        o_ref[...] = (b_ref[...] * _conv(c_ref[...] * x_ref[...], w_ref[...])).astype(o_ref.dtype)

    return pl.pallas_call(
        body, name=name, grid=(nblk, bl), in_specs=secs + [wblk], out_specs=oblk,
        out_shape=jax.ShapeDtypeStruct((bl, s, w3 // 3), MXU_DTYPE), compiler_params=_cparams("parallel", "parallel"),
    )(p, p, p, w)


def _sc_bwd(p, w, dout, name):
    bl, s, w3 = p.shape
    nblk, secs, wblk, oblk = _sc_specs(p, w)

    def body(b_ref, c_ref, x_ref, w_ref, d_ref, db_ref, dc_ref, dx_ref, dw_ref):
        gb, gc, xin, wv, d = b_ref[...], c_ref[...], x_ref[...], w_ref[...], d_ref[...]
        u = gc * xin
        db_ref[...] = (d * _conv(u, wv)).astype(db_ref.dtype)
        du = _conv_bwd(d * gb, u, wv, dw_ref, pl.program_id(1) == 0)
        dc_ref[...] = (du * xin).astype(dc_ref.dtype)
        dx_ref[...] = (du * gc).astype(dx_ref.dtype)

    act = jax.ShapeDtypeStruct((bl, s, w3 // 3), MXU_DTYPE)
    return pl.pallas_call(
        body, name=name, grid=(nblk, bl), in_specs=secs + [wblk, oblk], out_specs=[oblk, oblk, oblk, wblk],
        out_shape=[act, act, act, jax.ShapeDtypeStruct(w.shape, F32)], compiler_params=_cparams("arbitrary", "arbitrary"),
    )(p, p, p, w, dout)


def _bdot(a, b, ca, cb):
    return lax.dot_general(a.astype(MXU_DTYPE), b.astype(MXU_DTYPE), (((ca,), (cb,)), ((), ())),
                           preferred_element_type=F32)


def _hdot(a, b):
    return lax.dot_general(a, b, (((1,), (0,)), ((), ())), precision=HIGHEST, preferred_element_type=F32)


def _lane_col(x, idx):
    lane = lax.broadcasted_iota(jnp.int32, x.shape, 1)
    return jnp.sum(jnp.where(lane == idx, x, 0.0), axis=1, keepdims=True)


def _chunk_masks():
    r = lax.broadcasted_iota(jnp.int32, (CHUNK, CHUNK), 0)
    c = lax.broadcasted_iota(jnp.int32, (CHUNK, CHUNK), 1)
    return r == c, r >= c, r > c


def _dot3(a, b):
    ah, bh = a.astype(MXU_DTYPE), b.astype(MXU_DTYPE)
    al, bl = (a - ah.astype(F32)).astype(MXU_DTYPE), (b - bh.astype(F32)).astype(MXU_DTYPE)
    dot = lambda x, y: lax.dot_general(x, y, (((1,), (0,)), ((), ())), preferred_element_type=F32)
    return dot(ah, bh) + (dot(ah, bl) + dot(al, bh))


def _tri_inv_steps(low, eye):
    x = -low
    p = jnp.where(eye, 1.0, 0.0) + x
    span = 2
    while span < CHUNK:
        x = _dot3(x, x)
        yield
        p = p + _dot3(p, x)
        yield
        span *= 2
    return p


def _round_robin(gens):
    out, live = [None] * len(gens), list(range(len(gens)))
    while live:
        still = []
        for i in live:
            try:
                next(gens[i])
                still.append(i)
            except StopIteration as stop:
                out[i] = stop.value
        live = still
    return out


def _gdn_pre(q, k, v, gc, beta, masks):
    eye, causal, strict = masks
    gc_row = jnp.sum(jnp.where(eye, gc, 0.0), axis=0, keepdims=True)
    decay = jnp.where(causal, jnp.exp(jnp.where(causal, gc - gc_row, 0.0)), 0.0)
    eg = jnp.exp(gc)
    gl = gc[CHUNK - 1:CHUNK, :]
    kb, vb = k * beta, v * beta
    low = jnp.where(strict, _bdot(kb, k, 1, 1) * decay, 0.0)
    qk = jnp.where(causal, _bdot(q, k, 1, 1) * decay, 0.0)
    rest = jnp.exp(gl - gc)
    return dict(decay=decay, eg=eg, gl=gl, kb=kb, vb=vb, kbe=kb * eg, low=low, qk=qk, qg=q * eg, rest=rest, kdec=k * rest)


def _gdn_specs(qkv, gbeta, heads, rev):
    bl, s, w3 = qkv.shape
    d, n = w3 // 3, s // CHUNK
    at = (lambda c: n - 1 - c) if rev else (lambda c: c)
    assert d == heads * HEAD
    sec = pl.BlockSpec((None, CHUNK, w3), lambda b, c: (b, at(c), 0))
    gspec = pl.BlockSpec((None, CHUNK, LANE), lambda b, c: (b, at(c), 0))
    sspec = pl.BlockSpec((None, None, heads, HEAD, HEAD), lambda b, c: (b, at(c), 0, 0, 0))
    tspec = pl.BlockSpec((None, None, heads, CHUNK, CHUNK), lambda b, c: (b, at(c), 0, 0, 0))
    return bl, s, d, n, sec, gspec, sspec, tspec


def _gdn_fwd(qkv, gbeta, heads, name):
    bl, s, d, n, sec, gspec, sspec, tspec = _gdn_specs(qkv, gbeta, heads, False)

    def body(x_ref, g_ref, o_ref, s_ref, t_ref, st_ref):
        @pl.when(pl.program_id(1) == 0)
        def _():
            st_ref[...] = jnp.zeros_like(st_ref)

        masks = _chunk_masks()
        eye, causal, _ = masks
        gblk = g_ref[...]
        gc_all = _hdot(jnp.where(causal, 1.0, 0.0), gblk)
        st_all = st_ref[...]

        def head(h):
            st = st_all[h]
            q, k, v = (x_ref[:, sec * d + h * HEAD:sec * d + (h + 1) * HEAD] for sec in range(3))
            pre = _gdn_pre(q, k, v, _lane_col(gc_all, h), _lane_col(gblk, heads + h), masks)
            yield
            t = yield from _tri_inv_steps(pre["low"], eye)
            u, w = _bdot(t, pre["vb"], 1, 0), _bdot(t, pre["kbe"], 1, 0)
            yield
            vnew = u - _bdot(w, st, 1, 0)
            yield
            out = _bdot(pre["qg"], st, 1, 0) + _bdot(pre["qk"], vnew, 1, 0)
            return out, t, st * jnp.exp(pre["gl"]) + _bdot(pre["kdec"], vnew, 0, 0)

        outs, ts, states = zip(*_round_robin([head(h) for h in range(heads)]))
        o_ref[...] = jnp.concatenate(outs, axis=1)
        s_ref[...] = st_all
        t_ref[...] = jnp.stack(ts)
        st_ref[...] = jnp.stack(states)

    return pl.pallas_call(
        body, name=name, grid=(bl, n), in_specs=[sec, gspec],
        out_specs=[pl.BlockSpec((None, CHUNK, d), lambda b, c: (b, c, 0)), sspec, tspec],
        out_shape=[jax.ShapeDtypeStruct((bl, s, d), F32), jax.ShapeDtypeStruct((bl, n, heads, HEAD, HEAD), F32),
                   jax.ShapeDtypeStruct((bl, n, heads, CHUNK, CHUNK), F32)],
        scratch_shapes=[pltpu.VMEM((heads, HEAD, HEAD), F32)], compiler_params=_cparams("parallel", "arbitrary"),
    )(qkv, gbeta)


def _gdn_bwd(qkv, gbeta, dout, s_all, t_all, heads, name):
    bl, s, d, n, sec, gspec, sspec, tspec = _gdn_specs(qkv, gbeta, heads, True)
    ospec = pl.BlockSpec((None, CHUNK, d), lambda b, c: (b, n - 1 - c, 0))

    def body(x_ref, g_ref, do_ref, s_ref, t_ref, dx_ref, dg_ref, ds_ref):
        @pl.when(pl.program_id(1) == 0)
        def _():
            ds_ref[...] = jnp.zeros_like(ds_ref)

        masks = _chunk_masks()
        eye, causal, strict = masks
        gblk = g_ref[...]
        gc_all = _hdot(jnp.where(causal, 1.0, 0.0), gblk)
        lane = lax.broadcasted_iota(jnp.int32, gblk.shape, 1)
        last_row = lax.broadcasted_iota(jnp.int32, (CHUNK, 1), 0) == CHUNK - 1
        rowsum = lambda a: jnp.sum(a, axis=1, keepdims=True)
        st_all, t_all_, ds_all = s_ref[...], t_ref[...], ds_ref[...]

        def head(h):
            sl = slice(h * HEAD, (h + 1) * HEAD)
            q, k, v = (x_ref[:, sec * d + h * HEAD:sec * d + (h + 1) * HEAD] for sec in range(3))
            do = do_ref[:, sl]
            beta = _lane_col(gblk, heads + h)
            st, t, dsn = st_all[h], t_all_[h], ds_all[h]
            pre = _gdn_pre(q, k, v, _lane_col(gc_all, h), beta, masks)
            decay, eg, kb, vb, kbe, low, qk, qg, kdec = (pre[x] for x in ("decay", "eg", "kb", "vb", "kbe", "low", "qk", "qg", "kdec"))
            egl = jnp.exp(pre["gl"])
            yield
            u, w = _bdot(t, vb, 1, 0), _bdot(t, kbe, 1, 0)
            yield
            vnew = u - _bdot(w, st, 1, 0)
            yield
            dkdec = _bdot(vnew, dsn, 1, 1)
            dvnew = _bdot(kdec, dsn, 1, 0) + _bdot(qk, do, 0, 0)
            dgl = jnp.sum(dsn * st, keepdims=True) * egl
            dqg = _bdot(do, st, 1, 1)
            dqk = jnp.where(causal, _bdot(do, vnew, 1, 1), 0.0)
            yield
            dw = -_bdot(dvnew, st, 1, 1)
            ds_new = dsn * egl + _bdot(qg, do, 0, 0) - _bdot(w, dvnew, 0, 0)
            yield
            dt = _bdot(dvnew, vb, 1, 1) + _bdot(dw, kbe, 1, 1)
            dvb, dkbe = _bdot(t, dvnew, 0, 0), _bdot(t, dw, 0, 0)
            yield
            inner = _bdot(dt, t, 1, 1)
            yield
            dlow = -jnp.where(strict, _bdot(t, inner, 0, 0), 0.0)
            da, db = dlow * decay, dqk * decay
            yield
            m = dlow * low + dqk * qk
            kdk = dkdec * kdec
            col_of_m = jnp.sum(jnp.where(eye, jnp.sum(m, axis=0, keepdims=True), 0.0), axis=1, keepdims=True)
            dgc = rowsum(m) - col_of_m + rowsum(dqg * qg) + rowsum(dkbe * kbe) - rowsum(kdk)
            dgc = dgc + jnp.where(last_row, dgl + jnp.sum(kdk, keepdims=True), 0.0)
            dkb = _bdot(da, k, 1, 0) + dkbe * eg
            yield
            dk = _bdot(da, kb, 0, 0) + _bdot(db, q, 0, 0) + dkdec * pre["rest"] + dkb * beta
            dq = _bdot(db, k, 1, 0) + dqg * eg
            dbeta = rowsum(dkb * k) + rowsum(dvb * v)
            return dq, dk, dvb * beta, jnp.where(lane == h, dgc, 0.0) + jnp.where(lane == heads + h, dbeta, 0.0), ds_new

        dqs, dks, dvs, dgs, dss = zip(*_round_robin([head(h) for h in range(heads)]))
        dx_ref[...] = jnp.concatenate(dqs + dks + dvs, axis=1)
        ds_ref[...] = jnp.stack(dss)
        dgb = dgs[0]
        for extra in dgs[1:]:
            dgb = dgb + extra
        upper = jnp.where(jnp.logical_or(eye, jnp.logical_not(causal)), 1.0, 0.0)
        dg_ref[...] = jnp.where(lane < heads, _hdot(upper, dgb), dgb)

    return pl.pallas_call(
        body, name=name, grid=(bl, n), in_specs=[sec, gspec, ospec, sspec, tspec], out_specs=[sec, gspec],
        out_shape=[jax.ShapeDtypeStruct(qkv.shape, F32), jax.ShapeDtypeStruct((bl, s, LANE), F32)],
        scratch_shapes=[pltpu.VMEM((heads, HEAD, HEAD), F32)], compiler_params=_cparams("parallel", "arbitrary"),
    )(qkv, gbeta, dout, s_all, t_all)


def _position():
    return lax.axis_index("x"), lax.axis_index("y"), lax.axis_index("c")


def _all_gather(x, *, name, hbm):
    space = pltpu.HBM if hbm else pltpu.VMEM

    def body(x_ref, out_ref, send_sems, recv_sems, local_sem):
        ax, ay, ac = _position()
        me, sibling = (ax, ay, ac), (ax, ay, 1 - ac)
        chips = [(1 - ax, ay), (ax, 1 - ay), (1 - ax, 1 - ay)]

        def slot(px, py, pc):
            return out_ref.at[4 * px + 2 * py + pc]

        def copy(k, block, to, src=None):
            return pltpu.make_async_remote_copy(
                src_ref=slot(*block) if src is None else src, dst_ref=slot(*block), send_sem=send_sems.at[k],
                recv_sem=recv_sems.at[k], device_id=to, device_id_type=MESH_IDS)

        mine = pltpu.make_async_copy(x_ref, slot(*me), local_sem)
        mine.start()
        first = [copy(0, me, sibling, src=x_ref)] + [copy(1 + j, me, (*chip, ac), src=x_ref) for j, chip in enumerate(chips)]
        for cp in first:
            cp.start()
        passed = [copy(4 + j, (*chip, ac), sibling) for j, chip in enumerate(chips)]
        for j, chip in enumerate(chips):
            copy(1 + j, (*chip, ac), me).wait_recv()
            passed[j].start()
        copy(0, sibling, me).wait_recv()
        for j, chip in enumerate(chips):
            copy(4 + j, (*chip, 1 - ac), me).wait_recv()
        for cp in first + passed:
            cp.wait_send()
        mine.wait()

    return pl.pallas_call(
        body, name=name, out_shape=jax.ShapeDtypeStruct((NDEV,) + x.shape, x.dtype),
        in_specs=[pl.BlockSpec(memory_space=space)], out_specs=pl.BlockSpec(memory_space=space),
        scratch_shapes=[pltpu.SemaphoreType.DMA((7,)), pltpu.SemaphoreType.DMA((7,)), pltpu.SemaphoreType.DMA],
    )(x)


class _Rider:
    def __init__(self, arrays, out_shapes, sems, hooks):
        self.arrays, self.out_shapes, self.sems, self.hooks = arrays, out_shapes, sems, hooks


def _gather_rider(xs):
    n = len(xs)

    def hooks(x_refs, out_refs, send_sems, recv_sems):
        ax, ay, ac = _position()
        me, sibling = (ax, ay, ac), (ax, ay, 1 - ac)
        chips = [(1 - ax, ay), (ax, 1 - ay), (1 - ax, 1 - ay)]

        def copies(k, block, to, own=False):
            out = []
            for i in range(n):
                slot = out_refs[i].at[4 * block[0] + 2 * block[1] + block[2]]
                out.append(pltpu.make_async_remote_copy(
                    src_ref=x_refs[i] if own else slot, dst_ref=slot, send_sem=send_sems.at[k, i], recv_sem=recv_sems.at[k, i],
                    device_id=to, device_id_type=MESH_IDS))
            return out

        def first():
            for cp in copies(0, me, sibling, own=True):
                cp.start()
            for j, chip in enumerate(chips):
                for cp in copies(1 + j, me, (*chip, ac), own=True):
                    cp.start()

        def mid():
            for j, chip in enumerate(chips):
                for arrived, onward in zip(copies(1 + j, (*chip, ac), me), copies(4 + j, (*chip, ac), sibling)):
                    arrived.wait_recv()
                    onward.start()

        def last():
            for cp in copies(0, sibling, me):
                cp.wait_recv()
            for j, chip in enumerate(chips):
                for cp in copies(4 + j, (*chip, 1 - ac), me):
                    cp.wait_recv()
            for cp in copies(0, me, sibling, own=True):
                cp.wait_send()
            for j, chip in enumerate(chips):
                for cp in copies(1 + j, me, (*chip, ac), own=True) + copies(4 + j, (*chip, ac), sibling):
                    cp.wait_send()

        return first, mid, last

    return _Rider(list(xs), [jax.ShapeDtypeStruct((NDEV,) + x.shape, x.dtype) for x in xs],
                  [pltpu.SemaphoreType.DMA((7, n)), pltpu.SemaphoreType.DMA((7, n))], hooks)


def _scatter_rider(parts):
    packed = sum(r for _, r in parts)
    width, dtype = parts[0][0].shape[1], parts[0][0].dtype

    def hooks(g_refs, out_refs, send_sems, recv_sems):
        (recv_ref,) = out_refs
        ax, ay, ac = _position()

        def peer(rel):
            flip = lambda a, bit: 1 - a if rel & bit else a
            return flip(ax, 4), flip(ay, 2), flip(ac, 1)

        def first():
            for rel in range(1, NDEV):
                px, py, pc = peer(rel)
                off = 0
                for g_ref, (_, r) in zip(g_refs, parts):
                    rows = g_ref.at[pl.ds(pl.multiple_of((4 * px + 2 * py + pc) * r, ROW_ALIGN), r)]
                    pltpu.make_async_remote_copy(
                        src_ref=rows, dst_ref=recv_ref.at[rel - 1, pl.ds(off, r)], send_sem=send_sems.at[rel - 1],
                        recv_sem=recv_sems.at[rel - 1], device_id=(px, py, pc), device_id_type=MESH_IDS).start()
                    off += r

        def last():
            for rel in range(1, NDEV):
                slot = recv_ref.at[rel - 1]
                pltpu.make_async_remote_copy(src_ref=slot, dst_ref=slot, send_sem=send_sems.at[rel - 1],
                                             recv_sem=recv_sems.at[rel - 1], device_id=peer(rel), device_id_type=MESH_IDS).wait()

        return first, lambda: None, last

    return _Rider([g for g, _ in parts], [jax.ShapeDtypeStruct((NDEV - 1, packed, width), dtype)],
                  [pltpu.SemaphoreType.DMA((NDEV - 1,)), pltpu.SemaphoreType.DMA((NDEV - 1,))], hooks)


def _sum_direct(own, recv, name):
    r, w = own.shape
    tr = max(t for t in range(ROW_ALIGN, 257, ROW_ALIGN) if r % t == 0)

    def body(own_ref, *refs):
        acc = own_ref[...].astype(F32)
        for ref in refs[:-1]:
            acc = acc + ref[...].astype(F32)
        refs[-1][...] = acc

    rblk = lambda k: pl.BlockSpec((None, tr, w), functools.partial(lambda i, k: (k, i, 0), k=k))
    blk = pl.BlockSpec((tr, w), lambda i: (i, 0))
    return pl.pallas_call(body, name=name, grid=(r // tr,), in_specs=[blk] + [rblk(k) for k in range(NDEV - 1)],
                          out_specs=blk, out_shape=jax.ShapeDtypeStruct((r, w), F32),
                          compiler_params=_cparams("parallel"))(own, *([recv] * (NDEV - 1)))


ROW_ALIGN = 16


def _window_start(rows_per_dev, k):
    return rows_per_dev * k // ROW_ALIGN * ROW_ALIGN


def _exchange_in_chip(parts, name, collective_id):
    packed = sum(win for _, _, win, _ in parts)
    width, dtype = parts[0][0].shape[1], parts[0][0].dtype

    def body(g_refs, out_refs, send_sems, recv_sems):
        (recv_ref,) = out_refs
        ax, ay, ac = _position()
        sibling = (ax, ay, 1 - ac)
        _handshake([sibling])
        for q in range(4):
            for g_ref, (_, r, win, off) in zip(g_refs, parts):
                there = g_ref.at[pl.ds(pl.multiple_of(_window_start(r, 2 * q + 1 - ac), ROW_ALIGN), win)]
                pltpu.make_async_remote_copy(src_ref=there, dst_ref=recv_ref.at[q, pl.ds(off, win)], send_sem=send_sems.at[q],
                                             recv_sem=recv_sems.at[q], device_id=sibling, device_id_type=MESH_IDS).start()
        for q in range(4):
            pltpu.make_async_remote_copy(src_ref=recv_ref.at[q], dst_ref=recv_ref.at[q], send_sem=send_sems.at[q],
                                         recv_sem=recv_sems.at[q], device_id=sibling, device_id_type=MESH_IDS).wait()

    return _on_sequencer(body, [g for g, _, _, _ in parts], [jax.ShapeDtypeStruct((4, packed, width), dtype)],
                         [pltpu.SemaphoreType.DMA((4,)), pltpu.SemaphoreType.DMA((4,))], name=name, collective_id=collective_id)[0]


def _on_sequencer(body, ins, out_shapes, sems, *, name, collective_id):
    hbm = pltpu.MemorySpace.HBM
    in_refs = [jax.new_ref(a, memory_space=hbm) for a in ins]
    out_refs = [jax.empty_ref(s, memory_space=hbm) for s in out_shapes]

    @pl.kernel(mesh=plsc.ScalarSubcoreMesh(axis_name="sequencer", num_cores=1), name=name, scratch_types=tuple(sems),
               compiler_params=pltpu.CompilerParams(collective_id=collective_id))
    def launch(*sem_refs):
        body(in_refs, out_refs, *sem_refs)

    launch()
    return [r[...] for r in out_refs]


def _handshake(peers):
    barrier = pltpu.get_barrier_semaphore()
    for peer in peers:
        pl.semaphore_signal(barrier, inc=1, device_id=peer, device_id_type=MESH_IDS)
    pl.semaphore_wait(barrier, len(peers))


def _exchange_chips_async(s1, name, collective_id):
    def body(in_refs, out_refs, send_sems, recv_sems):
        (src,), (got,) = in_refs, out_refs
        ax, ay, ac = _position()
        chips = [(1 - ax, ay), (ax, 1 - ay), (1 - ax, 1 - ay)]
        _handshake([(cx, cy, ac) for cx, cy in chips])
        copies = [pltpu.make_async_remote_copy(
            src_ref=src.at[2 * cx + cy], dst_ref=got.at[r], send_sem=send_sems.at[r], recv_sem=recv_sems.at[r],
            device_id=(cx, cy, ac), device_id_type=MESH_IDS) for r, (cx, cy) in enumerate(chips)]
        for cp in copies:
            cp.start()
        for cp in copies:
            cp.wait_recv()
        for cp in copies:
            cp.wait_send()

    return _on_sequencer(body, [s1], [jax.ShapeDtypeStruct((3,) + s1.shape[1:], s1.dtype)],
                         [pltpu.SemaphoreType.DMA((3,)), pltpu.SemaphoreType.DMA((3,))], name=name, collective_id=collective_id)[0]


def _gather_async(xs, name, collective_id):
    rider = _gather_rider(xs)

    def body(in_refs, out_refs, send_sems, recv_sems):
        ax, ay, ac = _position()
        _handshake([(ax, ay, 1 - ac), (1 - ax, ay, ac), (ax, 1 - ay, ac), (1 - ax, 1 - ay, ac)])
        for hook in rider.hooks(in_refs, out_refs, send_sems, recv_sems):
            hook()

    return _on_sequencer(body, rider.arrays, rider.out_shapes, rider.sems, name=name, collective_id=collective_id)


def _scatter_async(parts, name, collective_id):
    rider = _scatter_rider(parts)

    def body(in_refs, out_refs, send_sems, recv_sems):
        ax, ay, ac = _position()
        flip = lambda a, on: 1 - a if on else a
        _handshake([(flip(ax, rel & 4), flip(ay, rel & 2), flip(ac, rel & 1)) for rel in range(1, NDEV)])
        for hook in rider.hooks(in_refs, out_refs, send_sems, recv_sems):
            hook()

    return _on_sequencer(body, rider.arrays, rider.out_shapes, rider.sems, name=name, collective_id=collective_id)[0]


def _sum_in_chip(own, recv, name):
    _, r, w = own.shape
    tr = _tile(r, (256, 128))

    def body(a_ref, b_ref, o_ref):
        o_ref[...] = (a_ref[...].astype(F32) + b_ref[...].astype(F32)).astype(o_ref.dtype)

    blk = pl.BlockSpec((None, tr, w), lambda q, i: (q, i, 0))
    return pl.pallas_call(body, name=name, grid=(4, r // tr), in_specs=[blk, blk], out_specs=blk,
                          out_shape=jax.ShapeDtypeStruct(own.shape, own.dtype),
                          compiler_params=_cparams("parallel", "parallel"))(own, recv)


def _sum_chips(s1, recv, chip, name):
    _, r, w = s1.shape
    tr = _tile(r, (256, 128))

    def body(c_ref, s_ref, r0_ref, r1_ref, r2_ref, o_ref):
        f = lambda ref: ref[...].astype(F32)
        o_ref[...] = ((f(s_ref) + f(r0_ref)) + f(r1_ref)) + f(r2_ref)

    rblk = lambda k: pl.BlockSpec((None, tr, w), functools.partial(lambda i, c, k: (k, i, 0), k=k))
    grid_spec = pltpu.PrefetchScalarGridSpec(
        num_scalar_prefetch=1, grid=(r // tr,),
        in_specs=[pl.BlockSpec((None, tr, w), lambda i, c: (c[0], i, 0)), rblk(0), rblk(1), rblk(2)],
        out_specs=pl.BlockSpec((tr, w), lambda i, c: (i, 0)))
    return pl.pallas_call(body, name=name, grid_spec=grid_spec, out_shape=jax.ShapeDtypeStruct((r, w), F32),
                          compiler_params=_cparams("parallel"))(chip, s1, recv, recv, recv)


def _silu_rows(x, name):
    def body(x_ref, o_ref):
        o_ref[...] = _silu(x_ref[...])

    return pl.pallas_call(body, name=name, out_shape=jax.ShapeDtypeStruct(x.shape, F32))(x)


def _row_sum(x, name):
    def body(x_ref, o_ref):
        acc = x_ref[0:1, :]
        for i in range(1, x.shape[0]):
            acc = acc + x_ref[i:i + 1, :]
        o_ref[...] = acc

    return pl.pallas_call(body, name=name, out_shape=jax.ShapeDtypeStruct((1, x.shape[1]), F32))(x)


def _adamw(w, g, m, v, name):
    cols = w.shape[-1]
    rows = w.size // cols
    tr = _tile(rows, (128,))
    tc = LANE if (tr == rows and rows > 512 and cols % LANE == 0) else cols

    def body(w_ref, g_ref, m_ref, v_ref, d_ref, mo_ref, vo_ref):
        grad = g_ref[...]
        m_new = ADAM_B1 * m_ref[...] + (1.0 - ADAM_B1) * grad
        v_new = ADAM_B2 * v_ref[...] + (1.0 - ADAM_B2) * jnp.square(grad)
        m_hat = m_new / (1.0 - ADAM_B1 ** ADAM_STEP)
        v_hat = v_new / (1.0 - ADAM_B2 ** ADAM_STEP)
        d_ref[...] = -ADAM_LR * (m_hat / (jnp.sqrt(v_hat) + ADAM_EPS) + ADAM_WD * w_ref[...])
        mo_ref[...] = m_new
        vo_ref[...] = v_new

    blk = pl.BlockSpec((tr, tc), lambda i, j: (i, j))
    out = pl.pallas_call(
        body, name=name, grid=(rows // tr, cols // tc), in_specs=[blk] * 4, out_specs=[blk] * 3,
        out_shape=[jax.ShapeDtypeStruct((rows, cols), F32)] * 3, compiler_params=_cparams("parallel", "parallel"),
    )(*[t.reshape(rows, cols) for t in (w, g, m, v)])
    return [t.reshape(w.shape) for t in out]


def _pack(parts, width, row_mult, dtype):
    flat = jnp.concatenate([p.reshape(-1).astype(dtype) for p in parts])
    rows = -(-flat.shape[0] // (width * row_mult)) * row_mult
    return jnp.pad(flat, (0, rows * width - flat.shape[0])).reshape(rows, width)


def _unpack(flat, shapes):
    out, off = [], 0
    for shp in shapes:
        size = 1
        for dim in shp:
            size *= dim
        out.append(flat[:, off:off + size].reshape((flat.shape[0],) + tuple(shp)))
        off += size
    return out


def _devices_to_cols(a):
    _, r, c = a.shape
    return a.transpose(1, 0, 2).reshape(r, NDEV * c)


def kernel(x, c, w_ada, b_ada, norm1_w, w_in, gdn_conv_w, gdn_a_log, gdn_dt_bias, gdn_norm_w, w_gdn_proj, sc_conv_w, w_sc_out, w_o, norm2_w, w_ffn_in, w_ffn_out, w_ada_f, b_ada_f, normf_w, loss_target, m_w_ada, m_b_ada, m_norm1_w, m_w_in, m_gdn_conv_w, m_gdn_a_log, m_gdn_dt_bias, m_gdn_norm_w, m_w_gdn_proj, m_sc_conv_w, m_w_sc_out, m_w_o, m_norm2_w, m_w_ffn_in, m_w_ffn_out, m_w_ada_f, m_b_ada_f, m_normf_w, v_w_ada, v_b_ada, v_norm1_w, v_w_in, v_gdn_conv_w, v_gdn_a_log, v_gdn_dt_bias, v_gdn_norm_w, v_w_gdn_proj, v_sc_conv_w, v_w_sc_out, v_w_o, v_norm2_w, v_w_ffn_in, v_w_ffn_out, v_w_ada_f, v_b_ada_f, v_normf_w):
    bl, s, d = x.shape
    heads = gdn_a_log.shape[-1]
    dff = w_ffn_out.shape[1] * NDEV
    tok = bl * s
    ax, ay, ac = _position()
    dev = 4 * ax + 2 * ay + ac
    as_tok = lambda a: a.reshape(bl, s, a.shape[-1])
    as_mat = lambda a: a.reshape(tok, a.shape[-1])

    small = _all_gather(_pack([c, gdn_conv_w, sc_conv_w], LANE, 8, F32), name="gather_cond", hbm=False)
    c_all, conv_w, sc_w = _unpack(small.reshape(NDEV, -1), [(bl, d), gdn_conv_w.shape[1:], sc_conv_w.shape[1:]])
    c_act = _silu_rows(c_all.reshape(NDEV * bl, d), "cond_silu")
    conv_w, sc_w = _devices_to_cols(conv_w), _devices_to_cols(sc_w)
    n_ada, n_adaf = w_ada.shape[-1], w_ada_f.shape[-1]
    bias = jnp.broadcast_to(lax.dynamic_slice_in_dim(b_ada, dev * n_ada, n_ada, axis=1), (NDEV * bl, n_ada))
    biasf = jnp.broadcast_to(lax.dynamic_slice_in_dim(b_ada_f.reshape(1, -1), dev * n_adaf, n_adaf, axis=1), (NDEV * bl, n_adaf))
    mod_cols = _mm(c_act, w_ada[0], add=bias, name="ada_cols")
    modf_cols = _mm(c_act, w_ada_f, add=biasf, name="adaf_cols")
    mods = _all_gather(jnp.concatenate([mod_cols, modf_cols], axis=1), name="gather_mod", hbm=False)
    mod_all = mods[:, :, :n_ada].transpose(1, 0, 2).reshape(NDEV * bl, NDEV * n_ada)
    modf_all = mods[:, :, n_ada:].transpose(1, 0, 2).reshape(NDEV * bl, NDEV * n_adaf)
    my_rows = lambda a: lax.dynamic_slice_in_dim(a, dev * bl, bl, axis=0)
    sh1, sc1, g1, sh2, sc2, g2 = [t.reshape(bl, 1, d) for t in jnp.split(my_rows(mod_all), 6, axis=1)]
    shf, scf = [t.reshape(bl, 1, d) for t in jnp.split(my_rows(modf_all), 2, axis=1)]

    late = [t.astype(MXU_DTYPE) for t in (w_gdn_proj[0], w_sc_out[0], w_o[0], w_ffn_in[0].T, w_ffn_out[0])]
    rows = [t.shape[0] for t in late] + [w_in.shape[-1]]
    offs = [sum(rows[:i]) for i in range(5)]
    in_rows = -(-rows[5] // ROW_ALIGN) * ROW_ALIGN
    in_send = jnp.pad(w_in[0].T.astype(MXU_DTYPE), ((0, in_rows - rows[5]), (0, 0)))
    with_own = lambda g, own: lax.dynamic_update_slice_in_dim(g, own[None], dev, axis=0)
    (wt_in,) = _gather_async([in_send], "gather_w_in", 1)
    wt_in = with_own(wt_in, in_send)[:, :rows[5], :].reshape(NDEV * rows[5], d)
    gathered = _gather_async(late[:3], "gather_mixer", 2) + _gather_async(late[3:], "gather_ffn", 3)
    wgp, wso, wo, wt_fi, wfo = [with_own(g, own).reshape(NDEV * own.shape[0], d) for g, own in zip(gathered, late)]
    o_z, o_ab, o_sc, o_ga, o_gb = 3 * d, 4 * d, 4 * d + 2 * heads, 7 * d + 2 * heads, 8 * d + 2 * heads
    s_qkv, s_z, s_sc, s_gate = (0, o_z), (o_z, d), (o_sc, 3 * d), (o_ga, 2 * d)
    wt_ab = jnp.pad(wt_in[o_ab:o_sc], ((0, LANE - 2 * heads), (0, 0)))

    n1w, n2w, nfw = norm1_w.reshape(1, d), norm2_w.reshape(1, d), normf_w.reshape(1, d)
    lanes = lambda a: jnp.pad(a.reshape(1, -1), ((0, 0), (0, LANE - a.size)))
    a_log, dt_bias, gnw = lanes(gdn_a_log), lanes(gdn_dt_bias), gdn_norm_w.reshape(1, HEAD)
    f_gates = functools.partial(_f_gates, heads=heads)
    (h1,) = _tok_fwd(_f_norm_mod, [x], [sh1, sc1], [n1w], [(d, MXU_DTYPE)], name="norm1", ts=512)
    h1m = as_mat(h1)
    p_qkv = as_tok(_mm(h1m, wt_in, tb=True, b_rows=s_qkv, name="in_qkv"))
    p_z = as_tok(_mm(h1m, wt_in, tb=True, b_rows=s_z, name="in_z"))
    p_ab = as_tok(_mm(h1m, wt_ab, tb=True, name="in_ab"))
    p_sc = as_tok(_mm(h1m, wt_in, tb=True, b_rows=s_sc, name="in_sc"))
    p_g = as_tok(_mm(h1m, wt_in, tb=True, b_rows=s_gate, name="in_gate"))
    qkv = _qkv_fwd(p_qkv, conv_w, heads, "qkv_conv")
    (gbeta,) = _tok_fwd(f_gates, [p_ab], [], [a_log, dt_bias], [(LANE, F32)], name="gates", ts=512)
    o, s_all, t_all = _gdn_fwd(qkv, gbeta, heads, "gdn")
    (og,) = _tok_fwd(_f_gdn_out, [o, p_z], [], [(gnw, None)], [(d, MXU_DTYPE)], name="gdn_out", ts=2048, wb=HEAD, cols=heads)
    y_a = as_tok(_mm(as_mat(og), wgp, name="gdn_proj"))
    scp = _sc_fwd(p_sc, sc_w, "sc_conv")
    mrg, y_b = _tok_fwd(_f_merge_keep, [(p_g, 0), (p_g, 1), y_a, _Product(scp, wso)], [], [], [(d, MXU_DTYPE), (d, F32)],
                        name="merge", ts=256, wb=d)
    merge_toks = [(p_g, 0), (p_g, 1), y_a, y_b]
    x2, h2, mix = _tok_fwd(_f_res_norm_mod_keep, [x, _Product(mrg, wo)], [g1, sh2, sc2], [n2w],
                           [(d, F32), (d, MXU_DTYPE), (d, F32)], name="norm2", ts=512)
    act, gu_a, gu_b = _ffn_in_swiglu(as_mat(h2), wt_fi, dff, "ffn_in")

    loss_l, (dx2, dff_out, _), (dg2, dshf, dscf), (dnfw,) = _tok_bwd(
        _f_loss, [x2, _Product(as_tok(act), wfo), loss_target], [g2, shf, scf], [nfw], [], [True, True, False], name="loss",
        ts=256, loss=True, tok_dtype=[F32, MXU_DTYPE, None])
    dffm = as_mat(dff_out)
    dgu_a, dgu_b = _ffn_out_bwd_swiglu(dffm, wfo, gu_a, gu_b, "d_ffn_out")
    gmm = functools.partial(_mm, ta=True, out_dtype=MXU_DTYPE)
    gw_ffn_out = gmm(act, dffm, name="g_ffn_out")
    dh2 = _Product(as_tok(dgu_b), wt_fi, b_rows=(dff, dff), add=as_tok(_mm(dgu_a, wt_fi, b_rows=(0, dff), name="d_ffn_in_a")))
    h2m = as_mat(h2)
    gwt_ffn_in = gmm(dgu_a, h2m, out_rows=2 * dff, name="g_ffn_in_a")
    gwt_ffn_in = gmm(dgu_b, h2m, out_rows=2 * dff, row_off=dff, into=gwt_ffn_in, name="g_ffn_in_b")
    (dx_skip, dmix), (dg1, dsh2, dsc2), (dn2w,) = _tok_bwd(
        _f_res_norm_mod, [x, mix], [g1, sh2, sc2], [n2w], [dx2, dh2], [True, True], name="d_norm2", ts=256,
        tok_dtype=[F32, MXU_DTYPE])
    gw_o = gmm(as_mat(mrg), as_mat(dmix), name="g_mix_out")
    (dga, dgb, dya, dyb), _, _ = _tok_bwd(_f_merge, merge_toks, [], [], [_Product(dmix, wo, tb=True)], [True] * 4,
                                          name="d_merge", ts=256, wb=d, tok_dtype=MXU_DTYPE)
    dyam, dybm = as_mat(dya), as_mat(dyb)
    dog = as_tok(_mm(dyam, wgp, tb=True, name="d_gdn_proj"))
    gw_gdn_proj = gmm(as_mat(og), dyam, name="g_gdn_proj")
    dscp = as_tok(_mm(dybm, wso, tb=True, name="d_sc_out"))
    gw_sc_out = gmm(as_mat(scp), dybm, name="g_sc_out")
    dscb, dscc, dscx, g_sc_w = _sc_bwd(p_sc, sc_w, dscp, "d_sc_conv")
    (do, dz), _, (g_gnw,) = _tok_bwd(_f_gdn_out, [o, p_z], [], [(gnw, None)], [dog], [True, True], name="d_gdn_out",
                                     ts=2048, wb=HEAD, cols=heads, tok_dtype=[F32, MXU_DTYPE])
    ffn_parts, mix_parts = [(gwt_ffn_in, rows[3]), (gw_ffn_out, rows[4])], [(gw_gdn_proj, rows[0]), (gw_sc_out, rows[1]), (gw_o, rows[2])]
    own_rows = lambda parts: jnp.concatenate([lax.dynamic_slice_in_dim(g, dev * r, r, axis=0) for g, r in parts], axis=0)
    ffn_recv = _scatter_async(ffn_parts, "scatter_ffn", 4)
    mix_recv = _scatter_async(mix_parts, "scatter_mixer", 5)
    dqkv, dgbeta = _gdn_bwd(qkv, gbeta, do, s_all, t_all, heads, "d_gdn")
    dp_qkv, g_conv_w = _qkv_bwd(p_qkv, conv_w, dqkv, heads, "d_qkv_conv")
    ffn_red = _sum_direct(own_rows(ffn_parts), ffn_recv, "sum_ffn")
    mix_red = _sum_direct(own_rows(mix_parts), mix_recv, "sum_mix")
    (dp_ab,), _, (g_a_log, g_dt_bias) = _tok_bwd(f_gates, [p_ab], [], [a_log, dt_bias], [dgbeta], [True], name="d_gates",
                                                 ts=512, tok_dtype=MXU_DTYPE)
    sections = [(dp_qkv, s_qkv), (dz, s_z), (dp_ab, None), (dscb, (o_sc, d)), (dscc, (o_sc + d, d)), (dscx, (o_sc + 2 * d, d)),
                (dga, (o_ga, d)), (dgb, (o_gb, d))]
    gwt_in = [gmm(as_mat(dp), h1m, name=f"g_in_{k}") for k, (dp, _) in enumerate(sections)]
    gwt_in[2] = gwt_in[2][:2 * heads]

    r_in = rows[5]
    win = -(-(r_in + max(r_in * k % ROW_ALIGN for k in range(NDEV))) // 128) * 128
    need_rows = max(_window_start(r_in, k) for k in range(NDEV)) + win
    gwt_in = jnp.concatenate(gwt_in + [jnp.zeros((need_rows - NDEV * r_in, d), MXU_DTYPE)], axis=0)
    recv1 = _exchange_in_chip([(gwt_in, r_in, win, 0)], "scatter_in_chip", 7)
    own = jnp.stack([lax.dynamic_slice_in_dim(gwt_in, _window_start(r_in, 2 * q + ac), win, axis=0) for q in range(4)])
    s1 = _sum_in_chip(own, recv1, "sum_in_chip")
    recv2 = _exchange_chips_async(s1, "scatter_chips", 6)

    dh1 = None
    for k, (dp, sec) in enumerate(sections[:-1]):
        dh1 = _mm(as_mat(dp), wt_ab if sec is None else wt_in, b_rows=sec, add=dh1, name=f"d_in_{k}")
    dh1 = _Product(sections[-1][0], wt_in, b_rows=sections[-1][1], add=as_tok(dh1))
    (grad_x,), (dsh1, dsc1), (dn1w,) = _tok_bwd(_f_norm_mod_skip, [x], [sh1, sc1], [n1w], [dh1, dx_skip], [True],
                                                name="d_norm1", ts=256)
    reduced = _sum_chips(s1, recv2, (2 * ax + ay).reshape(1).astype(jnp.int32), "sum_chips")
    gt_w_in = lax.dynamic_slice_in_dim(reduced, r_in * dev - _window_start(r_in, dev), r_in, axis=0)
    g_w_in = gt_w_in.T.reshape(w_in.shape)
    gt_w_ffn_in = ffn_red[:rows[3]]
    g_w_ffn_in = gt_w_ffn_in.T.reshape(w_ffn_in.shape)
    g_w_ffn_out = ffn_red[rows[3]:].reshape(w_ffn_out.shape)
    g_w_gdn_proj, g_w_sc_out, g_w_o = (mix_red[offs[i]:offs[i] + rows[i]].reshape(ref.shape)
                                       for i, ref in enumerate((w_gdn_proj, w_sc_out, w_o)))

    dmod = jnp.concatenate([t.reshape(bl, d) for t in (dsh1, dsc1, dg1, dsh2, dsc2, dg2)], axis=1)
    dmodf = jnp.concatenate([t.reshape(bl, d) for t in (dshf, dscf)], axis=1)
    summed_parts = [dn1w, dn2w, dnfw, g_gnw, g_a_log, g_dt_bias, g_conv_w, g_sc_w, loss_l]
    partial = _all_gather(_pack([dmod, dmodf] + summed_parts, LANE, 8, F32), name="gather_small", hbm=False)
    partial = partial.reshape(NDEV, -1)
    n_rows = bl * (6 * d + 2 * d)
    dmod_all, dmodf_all = _unpack(partial[:, :n_rows], [(bl, 6 * d), (bl, 2 * d)])
    dmod_all, dmodf_all = dmod_all.reshape(NDEV * bl, 6 * d), dmodf_all.reshape(NDEV * bl, 2 * d)
    totals = _row_sum(partial[:, n_rows:], "sum_small")
    t_n1w, t_n2w, t_nfw, t_gnw, t_a_log, t_dt_bias, t_conv_w, t_sc_w, t_loss = [
        t[0] for t in _unpack(totals, [p.shape for p in summed_parts])]
    my_cols = lambda a, n: lax.dynamic_slice_in_dim(a, dev * n, n, axis=1)
    grads = {
        "w_ada": _mm(c_act, my_cols(dmod_all, n_ada), ta=True, name="g_ada").reshape(w_ada.shape),
        "b_ada": _row_sum(dmod_all, "g_ada_bias").reshape(b_ada.shape),
        "norm1_w": t_n1w.reshape(norm1_w.shape),
        "w_in": g_w_in,
        "gdn_conv_w": my_cols(t_conv_w, gdn_conv_w.shape[-1]).reshape(gdn_conv_w.shape),
        "gdn_a_log": t_a_log[:, :heads].reshape(gdn_a_log.shape),
        "gdn_dt_bias": t_dt_bias[:, :heads].reshape(gdn_dt_bias.shape),
        "gdn_norm_w": t_gnw.reshape(gdn_norm_w.shape),
        "w_gdn_proj": g_w_gdn_proj,
        "sc_conv_w": my_cols(t_sc_w, sc_conv_w.shape[-1]).reshape(sc_conv_w.shape),
        "w_sc_out": g_w_sc_out,
        "w_o": g_w_o,
        "norm2_w": t_n2w.reshape(norm2_w.shape),
        "w_ffn_in": g_w_ffn_in,
        "w_ffn_out": g_w_ffn_out,
        "w_ada_f": _mm(c_act, my_cols(dmodf_all, n_adaf), ta=True, name="g_adaf").reshape(w_ada_f.shape),
        "b_ada_f": _row_sum(dmodf_all, "g_adaf_bias").reshape(b_ada_f.shape),
        "normf_w": t_nfw.reshape(normf_w.shape),
    }
    weights = dict(w_ada=w_ada, b_ada=b_ada, norm1_w=norm1_w, w_in=w_in, gdn_conv_w=gdn_conv_w, gdn_a_log=gdn_a_log,
                   gdn_dt_bias=gdn_dt_bias, gdn_norm_w=gdn_norm_w, w_gdn_proj=w_gdn_proj, sc_conv_w=sc_conv_w,
                   w_sc_out=w_sc_out, w_o=w_o, norm2_w=norm2_w, w_ffn_in=w_ffn_in, w_ffn_out=w_ffn_out, w_ada_f=w_ada_f,
                   b_ada_f=b_ada_f, normf_w=normf_w)
    m_in = [m_w_ada, m_b_ada, m_norm1_w, m_w_in, m_gdn_conv_w, m_gdn_a_log, m_gdn_dt_bias, m_gdn_norm_w, m_w_gdn_proj,
            m_sc_conv_w, m_w_sc_out, m_w_o, m_norm2_w, m_w_ffn_in, m_w_ffn_out, m_w_ada_f, m_b_ada_f, m_normf_w]
    v_in = [v_w_ada, v_b_ada, v_norm1_w, v_w_in, v_gdn_conv_w, v_gdn_a_log, v_gdn_dt_bias, v_gdn_norm_w, v_w_gdn_proj,
            v_sc_conv_w, v_w_sc_out, v_w_o, v_norm2_w, v_w_ffn_in, v_w_ffn_out, v_w_ada_f, v_b_ada_f, v_normf_w]
    deltas, new_m, new_v = [], [], []
    grads_t = {"w_in": gt_w_in, "w_ffn_in": gt_w_ffn_in}
    for (wname, wt), mt, vt in zip(weights.items(), m_in, v_in):
        if wname in grads_t:
            back = lambda a, wt=wt: a.T.reshape(wt.shape)
            dl, mn, vn = (back(a) for a in _adamw(wt[0].T, grads_t[wname], mt[0].T, vt[0].T, "adamw_" + wname))
        else:
            dl, mn, vn = _adamw(wt, grads[wname], mt, vt, "adamw_" + wname)
        deltas.append(dl)
        new_m.append(mn)
        new_v.append(vn)
    loss = t_loss[0, 0]
    return (loss, grad_x, *[grads[k] for k in weights], *deltas, *new_m, *new_v)
```

```python
import functools

import jax
import jax.numpy as jnp
from jax import lax
from jax.experimental import pallas as pl
from jax.experimental.pallas import tpu as pltpu
from jax.experimental.pallas import tpu_sc as plsc

F32 = jnp.float32
MXU_DTYPE = jnp.bfloat16
NDEV = 8
CHUNK = 64
HEAD = 128
LANE = 128
EPS = 1e-6
ADAM_LR, ADAM_B1, ADAM_B2, ADAM_EPS, ADAM_WD, ADAM_STEP = 0.001, 0.9, 0.999, 1e-08, 0.01, 10
VMEM_LIMIT = 48 * 1024 * 1024
MESH_IDS = pl.DeviceIdType.MESH
HIGHEST = lax.Precision.HIGHEST


def _tile(n, cands=(512, 256, 128)):
    for c in cands:
        if n % c == 0:
            return c
    return n


def _cparams(*sem):
    return pltpu.CompilerParams(dimension_semantics=sem, vmem_limit_bytes=VMEM_LIMIT)


def _mm(a, b, *, ta=False, tb=False, add=None, out_dtype=F32, name, b_rows=None, out_rows=None, row_off=0, into=None):
    m, k = (a.shape[1], a.shape[0]) if ta else a.shape
    b_shape = b.shape if b_rows is None else (b_rows[1], b.shape[1])
    n = b_shape[0] if tb else b_shape[1]
    assert k == (b_shape[1] if tb else b_shape[0])
    if ta:
        tm, tn = _tile(m), n if n <= 1024 else _tile(n)
        tk = k if k <= 4096 else _tile(k, (4096, 2048, 1024, 512))
        if tm * tk > 1024 * 2048:
            tk = _tile(k, (2048, 1024, 512))
    else:
        tk = k if k <= 1024 else _tile(k, (1024, 512))
        tn = _tile(n, (1024 if tk <= 1024 else 512, 512, 256, 128))
        tm = _tile(m, (2048 if (tn <= 512 and tk <= 1024) else 1024, 1024, 512, 256, 128))
    nk = k // tk
    dims = (((0 if ta else 1,), (1 if tb else 0,)), ((), ()))
    has_add = add is not None

    def body(*refs):
        a_ref, b_ref = refs[0], refs[1]
        add_ref = refs[2] if has_add else None
        o_ref = refs[2 + has_add + (into is not None)]
        part = lax.dot_general(a_ref[...].astype(MXU_DTYPE), b_ref[...].astype(MXU_DTYPE), dims,
                               preferred_element_type=F32)

        def finish(acc):
            if has_add:
                acc = acc + add_ref[...]
            o_ref[...] = acc.astype(o_ref.dtype)

        if nk == 1:
            finish(part)
        else:
            acc_ref = refs[-1]
            kk = pl.program_id(2)

            @pl.when(kk == 0)
            def _():
                acc_ref[...] = part

            @pl.when(kk > 0)
            def _():
                acc_ref[...] += part

            @pl.when(kk == nk - 1)
            def _():
                finish(acc_ref[...])

    a_spec = pl.BlockSpec((tk, tm), lambda i, j, kk: (kk, i)) if ta else pl.BlockSpec((tm, tk), lambda i, j, kk: (i, kk))
    if b_rows is None:
        b_spec = pl.BlockSpec((tn, tk), lambda i, j, kk: (j, kk)) if tb else pl.BlockSpec((tk, tn), lambda i, j, kk: (kk, j))
    else:
        at = lambda t: pl.multiple_of(b_rows[0] + t, ROW_ALIGN)
        b_spec = (pl.BlockSpec((pl.Element(tn), pl.Element(tk)), lambda i, j, kk: (at(j * tn), kk * tk)) if tb else
                  pl.BlockSpec((pl.Element(tk), pl.Element(tn)), lambda i, j, kk: (at(kk * tk), j * tn)))
    add_spec = pl.BlockSpec((tm, tn), lambda i, j, kk: (i, j))
    assert row_off % tm == 0
    o_spec = pl.BlockSpec((tm, tn), lambda i, j, kk: (i + row_off // tm, j))
    in_specs = [a_spec, b_spec] + ([add_spec] if has_add else []) + ([pl.BlockSpec(memory_space=pl.ANY)] if into is not None else [])
    args = [a, b] + ([add] if has_add else []) + ([into] if into is not None else [])
    return pl.pallas_call(
        body, name=name, grid=(m // tm, n // tn, nk), in_specs=in_specs, out_specs=o_spec,
        out_shape=jax.ShapeDtypeStruct((out_rows or m, n), out_dtype),
        scratch_shapes=[pltpu.VMEM((tm, tn), F32)] if nk > 1 else [],
        input_output_aliases={len(args) - 1: 0} if into is not None else {},
        compiler_params=_cparams("parallel", "parallel", "arbitrary"),
    )(*args)


def _swiglu_tiles(m, half):
    tn = _tile(half, (512, 256, 128))
    return _tile(m, (2048 if tn <= 256 else 1024, 1024, 512, 256, 128)), tn


def _ffn_in_swiglu(h, wt, half, name):
    m, k = h.shape
    tm, tn = _swiglu_tiles(m, half)
    nj = half // tn
    dims = (((1,), (1,)), ((), ()))

    def body(h_ref, wa_ref, wb_ref, act_ref, a_ref, b_ref):
        lhs = h_ref[...].astype(MXU_DTYPE)
        a = lax.dot_general(lhs, wa_ref[...].astype(MXU_DTYPE), dims, preferred_element_type=F32)
        b = lax.dot_general(lhs, wb_ref[...].astype(MXU_DTYPE), dims, preferred_element_type=F32)
        act_ref[...] = (_silu(a) * b).astype(act_ref.dtype)
        a_ref[...] = a.astype(a_ref.dtype)
        b_ref[...] = b.astype(b_ref.dtype)

    out = jax.ShapeDtypeStruct((m, half), MXU_DTYPE)
    oblk = pl.BlockSpec((tm, tn), lambda i, j: (i, j))
    return pl.pallas_call(
        body, name=name, grid=(m // tm, nj),
        in_specs=[pl.BlockSpec((tm, k), lambda i, j: (i, 0)), pl.BlockSpec((tn, k), lambda i, j: (j, 0)),
                  pl.BlockSpec((tn, k), lambda i, j: (j + nj, 0))],
        out_specs=[oblk, oblk, oblk], out_shape=[out, out, out], compiler_params=_cparams("parallel", "parallel"),
    )(h, wt, wt)


def _ffn_out_bwd_swiglu(dff, w, a, b, name):
    m, k = dff.shape
    half = w.shape[0]
    tm, tn = _swiglu_tiles(m, half)

    def body(d_ref, w_ref, a_ref, b_ref, da_ref, db_ref):
        dact = lax.dot_general(d_ref[...].astype(MXU_DTYPE), w_ref[...].astype(MXU_DTYPE), (((1,), (1,)), ((), ())),
                               preferred_element_type=F32)
        av, bv = a_ref[...].astype(F32), b_ref[...].astype(F32)
        sig = jax.nn.sigmoid(av)
        da_ref[...] = (dact * bv * (sig * (1.0 + av * (1.0 - sig)))).astype(da_ref.dtype)
        db_ref[...] = (dact * (av * sig)).astype(db_ref.dtype)

    out = jax.ShapeDtypeStruct((m, half), MXU_DTYPE)
    oblk = pl.BlockSpec((tm, tn), lambda i, j: (i, j))
    return pl.pallas_call(
        body, name=name, grid=(m // tm, half // tn),
        in_specs=[pl.BlockSpec((tm, k), lambda i, j: (i, 0)), pl.BlockSpec((tn, k), lambda i, j: (j, 0)), oblk, oblk],
        out_specs=[oblk, oblk], out_shape=[out, out], compiler_params=_cparams("parallel", "parallel"),
    )(dff, w, a, b)


def _with_off(xs):
    return [x if isinstance(x, tuple) else (x, 0) for x in xs]


def _spec(kind, arr, off, ts, wb):
    w = arr.shape[-1] if wb is None else wb
    col = (lambda j: 0) if wb is None else functools.partial(lambda j, o: o + j, o=off)
    if kind == "tok":
        return pl.BlockSpec((None, ts, w), lambda j, b, i: (b, i, col(j)))
    if kind == "bat":
        return pl.BlockSpec((None, 1, w), lambda j, b, i: (b, 0, col(j)))
    if off is None:
        return pl.BlockSpec(arr.shape, lambda j, b, i: (0, 0))
    return pl.BlockSpec((arr.shape[0], w), lambda j, b, i: (0, col(j)))


class _Product:
    def __init__(self, a, b, *, tb=False, b_rows=None, add=None):
        self.a, self.b, self.tb, self.b_rows, self.add = a, b, tb, b_rows, add
        rows = b.shape[0] if b_rows is None else b_rows[1]
        self.shape = a.shape[:2] + (rows if tb else b.shape[1],)

    def inputs(self, ts):
        a_spec = pl.BlockSpec((None, ts, self.a.shape[2]), lambda j, b, i: (b, i, 0))
        if self.b_rows is None:
            b_spec = pl.BlockSpec(self.b.shape, lambda j, b, i: (0, 0))
        else:
            start, count = self.b_rows
            b_spec = pl.BlockSpec((pl.Element(count), pl.Element(self.b.shape[1])), lambda j, b, i: (start, 0))
        extra = [] if self.add is None else [(self.add, pl.BlockSpec((None, ts, self.shape[2]), lambda j, b, i: (b, i, 0)))]
        return [(self.a, a_spec), (self.b, b_spec)] + extra

    def value(self, refs):
        dims = (((1,), (1 if self.tb else 0,)), ((), ()))
        val = lax.dot_general(refs[0][...].astype(MXU_DTYPE), refs[1][...].astype(MXU_DTYPE), dims, preferred_element_type=F32)
        return val if self.add is None else val + refs[2][...].astype(F32)


def _inputs(groups, kinds, ts, wb):
    loaded = [(a, _spec(kind, a, o, ts, wb)) for g, kind in zip(groups, kinds) for a, o in g if not isinstance(a, _Product)]
    made = [pair for g in groups for a, _ in g if isinstance(a, _Product) for pair in a.inputs(ts)]
    return [a for a, _ in loaded + made], [sp for _, sp in loaded + made]


def _values(refs, groups):
    n_loaded = sum(1 for g in groups for a, _ in g if not isinstance(a, _Product))
    loaded, pos, out = iter(refs[:n_loaded]), n_loaded, []
    for g in groups:
        vals = []
        for a, _ in g:
            if isinstance(a, _Product):
                k = 2 if a.add is None else 3
                vals.append(a.value(refs[pos:pos + k]))
                pos += k
            else:
                vals.append(next(loaded)[...].astype(F32))
        out.append(vals)
    return out, pos


def _tok_fwd(fn, toks, bats, pars, outs, *, name, ts, wb=None, cols=1):
    groups = [_with_off(toks), _with_off(bats), _with_off(pars)]
    bl, s, _ = groups[0][0][0].shape
    ts = min(ts, s)
    args, in_specs = _inputs(groups, ("tok", "bat", "par"), ts, wb)

    def body(*refs):
        vals, n_in = _values(refs, groups)
        res = fn(*[v for g in vals for v in g])
        for r, val in zip(refs[n_in:], res):
            r[...] = val.astype(r.dtype)

    out_specs = [pl.BlockSpec((None, ts, w if wb is None else wb), lambda j, b, i: (b, i, j)) for w, _ in outs]
    return pl.pallas_call(
        body, name=name, grid=(cols, bl, s // ts), in_specs=in_specs,
        out_specs=out_specs, out_shape=[jax.ShapeDtypeStruct((bl, s, w), dt) for w, dt in outs],
        compiler_params=_cparams("parallel", "parallel", "parallel"),
    )(*args)


def _accumulate(ref, val, first):
    @pl.when(first)
    def _():
        ref[...] = val

    @pl.when(jnp.logical_not(first))
    def _():
        ref[...] += val


def _tok_bwd(fn, toks, bats, pars, cots, need, *, name, ts, wb=None, cols=1, tok_dtype=F32, loss=False, after=()):
    toks, bats, pars, cots = _with_off(toks), _with_off(bats), _with_off(pars), _with_off(cots)
    groups = [toks, bats, pars, cots]
    bl, s, _ = toks[0][0].shape
    ts = min(ts, s)
    nt, nb, npar = len(toks), len(bats), len(pars)
    args, in_specs = _inputs(groups, ("tok", "bat", "par", "tok"), ts, wb)
    args, in_specs = args + list(after), in_specs + [pl.BlockSpec(memory_space=pl.ANY)] * len(after)

    def body(*refs):
        j, b, i = pl.program_id(0), pl.program_id(1), pl.program_id(2)
        (tok_vals, bat_vals, par_vals, cot_vals), o = _values(refs, groups)
        o += len(after)
        outs, vjp = jax.vjp(fn, *tok_vals, *bat_vals, *par_vals)
        if loss:
            ct = (jnp.ones_like(outs[0]),)
            tot = jnp.broadcast_to(jnp.sum(outs[0], keepdims=True), (1, LANE))
            _accumulate(refs[o], tot, jnp.logical_and(b == 0, i == 0))
            o += 1
        else:
            ct = tuple(cot_vals)
        grads = vjp(ct)
        for t in range(nt):
            if need[t]:
                refs[o][...] = grads[t].astype(refs[o].dtype)
                o += 1
        for t in range(nb):
            _accumulate(refs[o], grads[nt + t], i == 0)
            o += 1
        for t in range(npar):
            first = jnp.logical_and(b == 0, i == 0)
            if pars[t][1] is None:
                first = jnp.logical_and(first, j == 0)
            _accumulate(refs[o], grads[nt + nb + t], first)
            o += 1

    full = lambda arr: arr.shape[-1] if wb is None else wb * cols
    blk = lambda arr: arr.shape[-1] if wb is None else wb
    out_specs, out_shape = [], []
    if loss:
        out_specs.append(pl.BlockSpec((1, LANE), lambda j, b, i: (0, 0)))
        out_shape.append(jax.ShapeDtypeStruct((1, LANE), F32))
    for t in range(nt):
        if need[t]:
            out_specs.append(pl.BlockSpec((None, ts, blk(toks[t][0])), lambda j, b, i: (b, i, j)))
            dt = tok_dtype[t] if isinstance(tok_dtype, (list, tuple)) else tok_dtype
            out_shape.append(jax.ShapeDtypeStruct((bl, s, full(toks[t][0])), dt))
    for arr, _ in bats:
        out_specs.append(pl.BlockSpec((None, 1, blk(arr)), lambda j, b, i: (b, 0, j)))
        out_shape.append(jax.ShapeDtypeStruct((bl, 1, full(arr)), F32))
    for arr, off in pars:
        if off is None:
            out_specs.append(pl.BlockSpec(arr.shape, lambda j, b, i: (0, 0)))
            out_shape.append(jax.ShapeDtypeStruct(arr.shape, F32))
        else:
            out_specs.append(pl.BlockSpec((arr.shape[0], blk(arr)), lambda j, b, i: (0, j)))
            out_shape.append(jax.ShapeDtypeStruct((arr.shape[0], full(arr)), F32))
    res = list(pl.pallas_call(
        body, name=name, grid=(cols, bl, s // ts), in_specs=in_specs,
        out_specs=out_specs, out_shape=out_shape, compiler_params=_cparams("arbitrary", "arbitrary", "arbitrary"),
    )(*args))
    tot = res.pop(0) if loss else None
    dtoks = [res.pop(0) if need[t] else None for t in range(nt)]
    dbats = [res.pop(0) for _ in range(nb)]
    dpars = [res.pop(0) for _ in range(npar)]
    return (tot, dtoks, dbats, dpars) if loss else (dtoks, dbats, dpars)


def _silu(x):
    return x * jax.nn.sigmoid(x)


def _rms(x, w):
    return x * lax.rsqrt(jnp.mean(x * x, axis=-1, keepdims=True) + EPS) * w


def _f_norm_mod(x, shift, scale, w):
    return (_rms(x, w) * (1.0 + scale) + shift,)


def _f_norm_mod_skip(x, shift, scale, w):
    return _rms(x, w) * (1.0 + scale) + shift, x


def _f_res_norm_mod(x, mix, gate, shift, scale, w):
    x2 = x + gate * mix
    return x2, _rms(x2, w) * (1.0 + scale) + shift


def _f_res_norm_mod_keep(x, mix, gate, shift, scale, w):
    return (*_f_res_norm_mod(x, mix, gate, shift, scale, w), mix)


def _f_gates(p, a_log, dt_bias, *, heads):
    z = p + dt_bias
    g = -jnp.exp(a_log) * (jnp.maximum(z, 0.0) + jnp.log1p(jnp.exp(jnp.minimum(z, -z))))
    lane = lax.broadcasted_iota(jnp.int32, p.shape, 1)
    return (jnp.where(lane < heads, g, jax.nn.sigmoid(p)),)


def _f_gdn_out(o, z, w):
    return (_rms(o, w) * _silu(z),)


def _f_merge(ga, gb, ya, yb):
    return (jax.nn.sigmoid(ga) * ya + jax.nn.sigmoid(gb) * yb,)


def _f_merge_keep(ga, gb, ya, yb):
    return (*_f_merge(ga, gb, ya, yb), yb)


def _f_loss(x2, ff, tgt, gate, shift, scale, w):
    y = _rms(x2 + gate * ff, w) * (1.0 + scale) + shift
    return (0.5 * jnp.mean(jnp.square(y - tgt), axis=-1, keepdims=True),)


def _shift_down(x, s):
    if s == 0:
        return x
    row = lax.broadcasted_iota(jnp.int32, x.shape, 0)
    return jnp.where(row >= s, pltpu.roll(x, s, 0), 0.0)


def _shift_up(x, s):
    if s == 0:
        return x
    n = x.shape[0]
    row = lax.broadcasted_iota(jnp.int32, x.shape, 0)
    return jnp.where(row < n - s, pltpu.roll(x, n - s, 0), 0.0)


def _conv(x, w):
    width = w.shape[0]
    acc = w[width - 1:width, :] * x
    for j in range(width - 1):
        acc = acc + w[j:j + 1, :] * _shift_down(x, width - 1 - j)
    return acc


def _conv_bwd(dy, x, w, dw_ref, first):
    width = w.shape[0]
    dx = w[width - 1:width, :] * dy
    for j in range(width - 1):
        dx = dx + w[j:j + 1, :] * _shift_up(dy, width - 1 - j)
    for j in range(width):
        row = jnp.sum(dy * _shift_down(x, width - 1 - j), axis=0, keepdims=True)
        _accumulate(dw_ref.at[j:j + 1, :], row, first)
    return dx


def _qkv_act(xc, is_v, scale):
    a = _silu(xc)
    nrm = a * lax.rsqrt(jnp.sum(a * a, axis=-1, keepdims=True) + EPS) * scale
    return jnp.where(is_v, a, nrm)


def _qkv_consts(j, heads):
    is_v = j >= 2 * heads
    scale = jnp.where(j < heads, HEAD ** -0.5, 1.0).astype(F32)
    return is_v, scale


def _qkv_fwd(p, w, heads, name):
    bl, s, w3 = p.shape

    def body(p_ref, w_ref, o_ref):
        is_v, scale = _qkv_consts(pl.program_id(0), heads)
        o_ref[...] = _qkv_act(_conv(p_ref[...], w_ref[...]), is_v, scale)

    blk = pl.BlockSpec((None, s, HEAD), lambda j, b: (b, 0, j))
    return pl.pallas_call(
        body, name=name, grid=(w3 // HEAD, bl), in_specs=[blk, pl.BlockSpec((w.shape[0], HEAD), lambda j, b: (0, j))],
        out_specs=blk, out_shape=jax.ShapeDtypeStruct(p.shape, F32), compiler_params=_cparams("parallel", "parallel"),
    )(p, w)


def _qkv_bwd(p, w, dout, heads, name):
    bl, s, w3 = p.shape

    def body(p_ref, w_ref, d_ref, dp_ref, dw_ref):
        is_v, scale = _qkv_consts(pl.program_id(0), heads)
        x, wv = p_ref[...], w_ref[...]
        _, vjp = jax.vjp(lambda xc: _qkv_act(xc, is_v, scale), _conv(x, wv))
        (dxc,) = vjp(d_ref[...])
        dp_ref[...] = _conv_bwd(dxc, x, wv, dw_ref, pl.program_id(1) == 0).astype(dp_ref.dtype)

    blk = pl.BlockSpec((None, s, HEAD), lambda j, b: (b, 0, j))
    wblk = pl.BlockSpec((w.shape[0], HEAD), lambda j, b: (0, j))
    return pl.pallas_call(
        body, name=name, grid=(w3 // HEAD, bl), in_specs=[blk, wblk, blk], out_specs=[blk, wblk],
        out_shape=[jax.ShapeDtypeStruct(p.shape, MXU_DTYPE), jax.ShapeDtypeStruct(w.shape, F32)],
        compiler_params=_cparams("arbitrary", "arbitrary"),
    )(p, w, dout)


def _sc_specs(p, w):
    bl, s, w3 = p.shape
    nblk = w3 // 3 // LANE
    sec = lambda k: pl.BlockSpec((None, s, LANE), functools.partial(lambda j, b, k: (b, 0, k * nblk + j), k=k))
    return nblk, [sec(0), sec(1), sec(2)], pl.BlockSpec((w.shape[0], LANE), lambda j, b: (0, j)), \
        pl.BlockSpec((None, s, LANE), lambda j, b: (b, 0, j))


def _sc_fwd(p, w, name):
    bl, s, w3 = p.shape
    nblk, secs, wblk, oblk = _sc_specs(p, w)

    def body(b_ref, c_ref, x_ref, w_ref, o_ref):
        o_ref[...] = (b_ref[...] * _conv(c_ref[...] * x_ref[...], w_ref[...])).astype(o_ref.dtype)

    return pl.pallas_call(
        body, name=name, grid=(nblk, bl), in_specs=secs + [wblk], out_specs=oblk,
        out_shape=jax.ShapeDtypeStruct((bl, s, w3 // 3), MXU_DTYPE), compiler_params=_cparams("parallel", "parallel"),
    )(p, p, p, w)


def _sc_bwd(p, w, dout, name):
    bl, s, w3 = p.shape
    nblk, secs, wblk, oblk = _sc_specs(p, w)

    def body(b_ref, c_ref, x_ref, w_ref, d_ref, db_ref, dc_ref, dx_ref, dw_ref):
        gb, gc, xin, wv, d = b_ref[...], c_ref[...], x_ref[...], w_ref[...], d_ref[...]
        u = gc * xin
        db_ref[...] = (d * _conv(u, wv)).astype(db_ref.dtype)
        du = _conv_bwd(d * gb, u, wv, dw_ref, pl.program_id(1) == 0)
        dc_ref[...] = (du * xin).astype(dc_ref.dtype)
        dx_ref[...] = (du * gc).astype(dx_ref.dtype)

    act = jax.ShapeDtypeStruct((bl, s, w3 // 3), MXU_DTYPE)
    return pl.pallas_call(
        body, name=name, grid=(nblk, bl), in_specs=secs + [wblk, oblk], out_specs=[oblk, oblk, oblk, wblk],
        out_shape=[act, act, act, jax.ShapeDtypeStruct(w.shape, F32)], compiler_params=_cparams("arbitrary", "arbitrary"),
    )(p, p, p, w, dout)


def _bdot(a, b, ca, cb):
    return lax.dot_general(a.astype(MXU_DTYPE), b.astype(MXU_DTYPE), (((ca,), (cb,)), ((), ())),
                           preferred_element_type=F32)


def _hdot(a, b):
    return lax.dot_general(a, b, (((1,), (0,)), ((), ())), precision=HIGHEST, preferred_element_type=F32)


def _lane_col(x, idx):
    lane = lax.broadcasted_iota(jnp.int32, x.shape, 1)
    return jnp.sum(jnp.where(lane == idx, x, 0.0), axis=1, keepdims=True)


def _chunk_masks():
    r = lax.broadcasted_iota(jnp.int32, (CHUNK, CHUNK), 0)
    c = lax.broadcasted_iota(jnp.int32, (CHUNK, CHUNK), 1)
    return r == c, r >= c, r > c


def _dot3(a, b):
    ah, bh = a.astype(MXU_DTYPE), b.astype(MXU_DTYPE)
    al, bl = (a - ah.astype(F32)).astype(MXU_DTYPE), (b - bh.astype(F32)).astype(MXU_DTYPE)
    dot = lambda x, y: lax.dot_general(x, y, (((1,), (0,)), ((), ())), preferred_element_type=F32)
    return dot(ah, bh) + (dot(ah, bl) + dot(al, bh))


def _tri_inv_steps(low, eye):
    x = -low
    p = jnp.where(eye, 1.0, 0.0) + x
    span = 2
    while span < CHUNK:
        x = _dot3(x, x)
        yield
        p = p + _dot3(p, x)
        yield
        span *= 2
    return p


def _round_robin(gens):
    out, live = [None] * len(gens), list(range(len(gens)))
    while live:
        still = []
        for i in live:
            try:
                next(gens[i])
                still.append(i)
            except StopIteration as stop:
                out[i] = stop.value
        live = still
    return out


def _gdn_pre(q, k, v, gc, beta, masks):
    eye, causal, strict = masks
    gc_row = jnp.sum(jnp.where(eye, gc, 0.0), axis=0, keepdims=True)
    decay = jnp.where(causal, jnp.exp(jnp.where(causal, gc - gc_row, 0.0)), 0.0)
    eg = jnp.exp(gc)
    gl = gc[CHUNK - 1:CHUNK, :]
    kb, vb = k * beta, v * beta
    low = jnp.where(strict, _bdot(kb, k, 1, 1) * decay, 0.0)
    qk = jnp.where(causal, _bdot(q, k, 1, 1) * decay, 0.0)
    rest = jnp.exp(gl - gc)
    return dict(decay=decay, eg=eg, gl=gl, kb=kb, vb=vb, kbe=kb * eg, low=low, qk=qk, qg=q * eg, rest=rest, kdec=k * rest)


def _gdn_specs(qkv, gbeta, heads, rev):
    bl, s, w3 = qkv.shape
    d, n = w3 // 3, s // CHUNK
    at = (lambda c: n - 1 - c) if rev else (lambda c: c)
    assert d == heads * HEAD
    sec = pl.BlockSpec((None, CHUNK, w3), lambda b, c: (b, at(c), 0))
    gspec = pl.BlockSpec((None, CHUNK, LANE), lambda b, c: (b, at(c), 0))
    sspec = pl.BlockSpec((None, None, heads, HEAD, HEAD), lambda b, c: (b, at(c), 0, 0, 0))
    tspec = pl.BlockSpec((None, None, heads, CHUNK, CHUNK), lambda b, c: (b, at(c), 0, 0, 0))
    return bl, s, d, n, sec, gspec, sspec, tspec


def _gdn_fwd(qkv, gbeta, heads, name):
    bl, s, d, n, sec, gspec, sspec, tspec = _gdn_specs(qkv, gbeta, heads, False)

    def body(x_ref, g_ref, o_ref, s_ref, t_ref, st_ref):
        @pl.when(pl.program_id(1) == 0)
        def _():
            st_ref[...] = jnp.zeros_like(st_ref)

        masks = _chunk_masks()
        eye, causal, _ = masks
        gblk = g_ref[...]
        gc_all = _hdot(jnp.where(causal, 1.0, 0.0), gblk)
        st_all = st_ref[...]

        def head(h):
            st = st_all[h]
            q, k, v = (x_ref[:, sec * d + h * HEAD:sec * d + (h + 1) * HEAD] for sec in range(3))
            pre = _gdn_pre(q, k, v, _lane_col(gc_all, h), _lane_col(gblk, heads + h), masks)
            yield
            t = yield from _tri_inv_steps(pre["low"], eye)
            u, w = _bdot(t, pre["vb"], 1, 0), _bdot(t, pre["kbe"], 1, 0)
            yield
            vnew = u - _bdot(w, st, 1, 0)
            yield
            out = _bdot(pre["qg"], st, 1, 0) + _bdot(pre["qk"], vnew, 1, 0)
            return out, t, st * jnp.exp(pre["gl"]) + _bdot(pre["kdec"], vnew, 0, 0)

        outs, ts, states = zip(*_round_robin([head(h) for h in range(heads)]))
        o_ref[...] = jnp.concatenate(outs, axis=1)
        s_ref[...] = st_all
        t_ref[...] = jnp.stack(ts)
        st_ref[...] = jnp.stack(states)

    return pl.pallas_call(
        body, name=name, grid=(bl, n), in_specs=[sec, gspec],
        out_specs=[pl.BlockSpec((None, CHUNK, d), lambda b, c: (b, c, 0)), sspec, tspec],
        out_shape=[jax.ShapeDtypeStruct((bl, s, d), F32), jax.ShapeDtypeStruct((bl, n, heads, HEAD, HEAD), F32),
                   jax.ShapeDtypeStruct((bl, n, heads, CHUNK, CHUNK), F32)],
        scratch_shapes=[pltpu.VMEM((heads, HEAD, HEAD), F32)], compiler_params=_cparams("parallel", "arbitrary"),
    )(qkv, gbeta)


def _gdn_bwd(qkv, gbeta, dout, s_all, t_all, heads, name):
    bl, s, d, n, sec, gspec, sspec, tspec = _gdn_specs(qkv, gbeta, heads, True)
    ospec = pl.BlockSpec((None, CHUNK, d), lambda b, c: (b, n - 1 - c, 0))

    def body(x_ref, g_ref, do_ref, s_ref, t_ref, dx_ref, dg_ref, ds_ref):
        @pl.when(pl.program_id(1) == 0)
        def _():
            ds_ref[...] = jnp.zeros_like(ds_ref)

        masks = _chunk_masks()
        eye, causal, strict = masks
        gblk = g_ref[...]
        gc_all = _hdot(jnp.where(causal, 1.0, 0.0), gblk)
        lane = lax.broadcasted_iota(jnp.int32, gblk.shape, 1)
        last_row = lax.broadcasted_iota(jnp.int32, (CHUNK, 1), 0) == CHUNK - 1
        rowsum = lambda a: jnp.sum(a, axis=1, keepdims=True)
        st_all, t_all_, ds_all = s_ref[...], t_ref[...], ds_ref[...]

        def head(h):
            sl = slice(h * HEAD, (h + 1) * HEAD)
            q, k, v = (x_ref[:, sec * d + h * HEAD:sec * d + (h + 1) * HEAD] for sec in range(3))
            do = do_ref[:, sl]
            beta = _lane_col(gblk, heads + h)
            st, t, dsn = st_all[h], t_all_[h], ds_all[h]
            pre = _gdn_pre(q, k, v, _lane_col(gc_all, h), beta, masks)
            decay, eg, kb, vb, kbe, low, qk, qg, kdec = (pre[x] for x in ("decay", "eg", "kb", "vb", "kbe", "low", "qk", "qg", "kdec"))
            egl = jnp.exp(pre["gl"])
            yield
            u, w = _bdot(t, vb, 1, 0), _bdot(t, kbe, 1, 0)
            yield
            vnew = u - _bdot(w, st, 1, 0)
            yield
            dkdec = _bdot(vnew, dsn, 1, 1)
            dvnew = _bdot(kdec, dsn, 1, 0) + _bdot(qk, do, 0, 0)
            dgl = jnp.sum(dsn * st, keepdims=True) * egl
            dqg = _bdot(do, st, 1, 1)
            dqk = jnp.where(causal, _bdot(do, vnew, 1, 1), 0.0)
            yield
            dw = -_bdot(dvnew, st, 1, 1)
            ds_new = dsn * egl + _bdot(qg, do, 0, 0) - _bdot(w, dvnew, 0, 0)
            yield
            dt = _bdot(dvnew, vb, 1, 1) + _bdot(dw, kbe, 1, 1)
            dvb, dkbe = _bdot(t, dvnew, 0, 0), _bdot(t, dw, 0, 0)
            yield
            inner = _bdot(dt, t, 1, 1)
            yield
            dlow = -jnp.where(strict, _bdot(t, inner, 0, 0), 0.0)
            da, db = dlow * decay, dqk * decay
            yield
            m = dlow * low + dqk * qk
            kdk = dkdec * kdec
            col_of_m = jnp.sum(jnp.where(eye, jnp.sum(m, axis=0, keepdims=True), 0.0), axis=1, keepdims=True)
            dgc = rowsum(m) - col_of_m + rowsum(dqg * qg) + rowsum(dkbe * kbe) - rowsum(kdk)
            dgc = dgc + jnp.where(last_row, dgl + jnp.sum(kdk, keepdims=True), 0.0)
            dkb = _bdot(da, k, 1, 0) + dkbe * eg
            yield
            dk = _bdot(da, kb, 0, 0) + _bdot(db, q, 0, 0) + dkdec * pre["rest"] + dkb * beta
            dq = _bdot(db, k, 1, 0) + dqg * eg
            dbeta = rowsum(dkb * k) + rowsum(dvb * v)
            return dq, dk, dvb * beta, jnp.where(lane == h, dgc, 0.0) + jnp.where(lane == heads + h, dbeta, 0.0), ds_new

        dqs, dks, dvs, dgs, dss = zip(*_round_robin([head(h) for h in range(heads)]))
        dx_ref[...] = jnp.concatenate(dqs + dks + dvs, axis=1)
        ds_ref[...] = jnp.stack(dss)
        dgb = dgs[0]
        for extra in dgs[1:]:
            dgb = dgb + extra
        upper = jnp.where(jnp.logical_or(eye, jnp.logical_not(causal)), 1.0, 0.0)
        dg_ref[...] = jnp.where(lane < heads, _hdot(upper, dgb), dgb)

    return pl.pallas_call(
        body, name=name, grid=(bl, n), in_specs=[sec, gspec, ospec, sspec, tspec], out_specs=[sec, gspec],
        out_shape=[jax.ShapeDtypeStruct(qkv.shape, F32), jax.ShapeDtypeStruct((bl, s, LANE), F32)],
        scratch_shapes=[pltpu.VMEM((heads, HEAD, HEAD), F32)], compiler_params=_cparams("parallel", "arbitrary"),
    )(qkv, gbeta, dout, s_all, t_all)


def _position():
    return lax.axis_index("x"), lax.axis_index("y"), lax.axis_index("c")


def _all_gather(x, *, name, hbm):
    space = pltpu.HBM if hbm else pltpu.VMEM

    def body(x_ref, out_ref, send_sems, recv_sems, local_sem):
        ax, ay, ac = _position()
        me, sibling = (ax, ay, ac), (ax, ay, 1 - ac)
        chips = [(1 - ax, ay), (ax, 1 - ay), (1 - ax, 1 - ay)]

        def slot(px, py, pc):
            return out_ref.at[4 * px + 2 * py + pc]

        def copy(k, block, to, src=None):
            return pltpu.make_async_remote_copy(
                src_ref=slot(*block) if src is None else src, dst_ref=slot(*block), send_sem=send_sems.at[k],
                recv_sem=recv_sems.at[k], device_id=to, device_id_type=MESH_IDS)

        mine = pltpu.make_async_copy(x_ref, slot(*me), local_sem)
        mine.start()
        first = [copy(0, me, sibling, src=x_ref)] + [copy(1 + j, me, (*chip, ac), src=x_ref) for j, chip in enumerate(chips)]
        for cp in first:
            cp.start()
        passed = [copy(4 + j, (*chip, ac), sibling) for j, chip in enumerate(chips)]
        for j, chip in enumerate(chips):
            copy(1 + j, (*chip, ac), me).wait_recv()
            passed[j].start()
        copy(0, sibling, me).wait_recv()
        for j, chip in enumerate(chips):
            copy(4 + j, (*chip, 1 - ac), me).wait_recv()
        for cp in first + passed:
            cp.wait_send()
        mine.wait()

    return pl.pallas_call(
        body, name=name, out_shape=jax.ShapeDtypeStruct((NDEV,) + x.shape, x.dtype),
        in_specs=[pl.BlockSpec(memory_space=space)], out_specs=pl.BlockSpec(memory_space=space),
        scratch_shapes=[pltpu.SemaphoreType.DMA((7,)), pltpu.SemaphoreType.DMA((7,)), pltpu.SemaphoreType.DMA],
    )(x)


class _Rider:
    def __init__(self, arrays, out_shapes, sems, hooks):
        self.arrays, self.out_shapes, self.sems, self.hooks = arrays, out_shapes, sems, hooks


def _gather_rider(xs):
    n = len(xs)

    def hooks(x_refs, out_refs, send_sems, recv_sems):
        ax, ay, ac = _position()
        me, sibling = (ax, ay, ac), (ax, ay, 1 - ac)
        chips = [(1 - ax, ay), (ax, 1 - ay), (1 - ax, 1 - ay)]

        def copies(k, block, to, own=False):
            out = []
            for i in range(n):
                slot = out_refs[i].at[4 * block[0] + 2 * block[1] + block[2]]
                out.append(pltpu.make_async_remote_copy(
                    src_ref=x_refs[i] if own else slot, dst_ref=slot, send_sem=send_sems.at[k, i], recv_sem=recv_sems.at[k, i],
                    device_id=to, device_id_type=MESH_IDS))
            return out

        def first():
            for cp in copies(0, me, sibling, own=True):
                cp.start()
            for j, chip in enumerate(chips):
                for cp in copies(1 + j, me, (*chip, ac), own=True):
                    cp.start()

        def mid():
            for j, chip in enumerate(chips):
                for arrived, onward in zip(copies(1 + j, (*chip, ac), me), copies(4 + j, (*chip, ac), sibling)):
                    arrived.wait_recv()
                    onward.start()

        def last():
            for cp in copies(0, sibling, me):
                cp.wait_recv()
            for j, chip in enumerate(chips):
                for cp in copies(4 + j, (*chip, 1 - ac), me):
                    cp.wait_recv()
            for cp in copies(0, me, sibling, own=True):
                cp.wait_send()
            for j, chip in enumerate(chips):
                for cp in copies(1 + j, me, (*chip, ac), own=True) + copies(4 + j, (*chip, ac), sibling):
                    cp.wait_send()

        return first, mid, last

    return _Rider(list(xs), [jax.ShapeDtypeStruct((NDEV,) + x.shape, x.dtype) for x in xs],
                  [pltpu.SemaphoreType.DMA((7, n)), pltpu.SemaphoreType.DMA((7, n))], hooks)


def _scatter_rider(parts):
    packed = sum(r for _, r in parts)
    width, dtype = parts[0][0].shape[1], parts[0][0].dtype

    def hooks(g_refs, out_refs, send_sems, recv_sems):
        (recv_ref,) = out_refs
        ax, ay, ac = _position()

        def peer(rel):
            flip = lambda a, bit: 1 - a if rel & bit else a
            return flip(ax, 4), flip(ay, 2), flip(ac, 1)

        def first():
            for rel in range(1, NDEV):
                px, py, pc = peer(rel)
                off = 0
                for g_ref, (_, r) in zip(g_refs, parts):
                    rows = g_ref.at[pl.ds(pl.multiple_of((4 * px + 2 * py + pc) * r, ROW_ALIGN), r)]
                    pltpu.make_async_remote_copy(
                        src_ref=rows, dst_ref=recv_ref.at[rel - 1, pl.ds(off, r)], send_sem=send_sems.at[rel - 1],
                        recv_sem=recv_sems.at[rel - 1], device_id=(px, py, pc), device_id_type=MESH_IDS).start()
                    off += r

        def last():
            for rel in range(1, NDEV):
                slot = recv_ref.at[rel - 1]
                pltpu.make_async_remote_copy(src_ref=slot, dst_ref=slot, send_sem=send_sems.at[rel - 1],
                                             recv_sem=recv_sems.at[rel - 1], device_id=peer(rel), device_id_type=MESH_IDS).wait()

        return first, lambda: None, last

    return _Rider([g for g, _ in parts], [jax.ShapeDtypeStruct((NDEV - 1, packed, width), dtype)],
                  [pltpu.SemaphoreType.DMA((NDEV - 1,)), pltpu.SemaphoreType.DMA((NDEV - 1,))], hooks)


def _sum_direct(own, recv, name):
    r, w = own.shape
    tr = max(t for t in range(ROW_ALIGN, 257, ROW_ALIGN) if r % t == 0)

    def body(own_ref, *refs):
        acc = own_ref[...].astype(F32)
        for ref in refs[:-1]:
            acc = acc + ref[...].astype(F32)
        refs[-1][...] = acc

    rblk = lambda k: pl.BlockSpec((None, tr, w), functools.partial(lambda i, k: (k, i, 0), k=k))
    blk = pl.BlockSpec((tr, w), lambda i: (i, 0))
    return pl.pallas_call(body, name=name, grid=(r // tr,), in_specs=[blk] + [rblk(k) for k in range(NDEV - 1)],
                          out_specs=blk, out_shape=jax.ShapeDtypeStruct((r, w), F32),
                          compiler_params=_cparams("parallel"))(own, *([recv] * (NDEV - 1)))


ROW_ALIGN = 16


def _window_start(rows_per_dev, k):
    return rows_per_dev * k // ROW_ALIGN * ROW_ALIGN


def _exchange_in_chip(parts, name, collective_id):
    packed = sum(win for _, _, win, _ in parts)
    width, dtype = parts[0][0].shape[1], parts[0][0].dtype

    def body(g_refs, out_refs, send_sems, recv_sems):
        (recv_ref,) = out_refs
        ax, ay, ac = _position()
        sibling = (ax, ay, 1 - ac)
        _handshake([sibling])
        for q in range(4):
            for g_ref, (_, r, win, off) in zip(g_refs, parts):
                there = g_ref.at[pl.ds(pl.multiple_of(_window_start(r, 2 * q + 1 - ac), ROW_ALIGN), win)]
                pltpu.make_async_remote_copy(src_ref=there, dst_ref=recv_ref.at[q, pl.ds(off, win)], send_sem=send_sems.at[q],
                                             recv_sem=recv_sems.at[q], device_id=sibling, device_id_type=MESH_IDS).start()
        for q in range(4):
            pltpu.make_async_remote_copy(src_ref=recv_ref.at[q], dst_ref=recv_ref.at[q], send_sem=send_sems.at[q],
                                         recv_sem=recv_sems.at[q], device_id=sibling, device_id_type=MESH_IDS).wait()

    return _on_sequencer(body, [g for g, _, _, _ in parts], [jax.ShapeDtypeStruct((4, packed, width), dtype)],
                         [pltpu.SemaphoreType.DMA((4,)), pltpu.SemaphoreType.DMA((4,))], name=name, collective_id=collective_id)[0]


def _on_sequencer(body, ins, out_shapes, sems, *, name, collective_id):
    hbm = pltpu.MemorySpace.HBM
    in_refs = [jax.new_ref(a, memory_space=hbm) for a in ins]
    out_refs = [jax.empty_ref(s, memory_space=hbm) for s in out_shapes]

    @pl.kernel(mesh=plsc.ScalarSubcoreMesh(axis_name="sequencer", num_cores=1), name=name, scratch_types=tuple(sems),
               compiler_params=pltpu.CompilerParams(collective_id=collective_id))
    def launch(*sem_refs):
        body(in_refs, out_refs, *sem_refs)

    launch()
    return [r[...] for r in out_refs]


def _handshake(peers):
    barrier = pltpu.get_barrier_semaphore()
    for peer in peers:
        pl.semaphore_signal(barrier, inc=1, device_id=peer, device_id_type=MESH_IDS)
    pl.semaphore_wait(barrier, len(peers))


def _exchange_chips_async(s1, name, collective_id):
    def body(in_refs, out_refs, send_sems, recv_sems):
        (src,), (got,) = in_refs, out_refs
        ax, ay, ac = _position()
        chips = [(1 - ax, ay), (ax, 1 - ay), (1 - ax, 1 - ay)]
        _handshake([(cx, cy, ac) for cx, cy in chips])
        copies = [pltpu.make_async_remote_copy(
            src_ref=src.at[2 * cx + cy], dst_ref=got.at[r], send_sem=send_sems.at[r], recv_sem=recv_sems.at[r],
            device_id=(cx, cy, ac), device_id_type=MESH_IDS) for r, (cx, cy) in enumerate(chips)]
        for cp in copies:
            cp.start()
        for cp in copies:
            cp.wait_recv()
        for cp in copies:
            cp.wait_send()

    return _on_sequencer(body, [s1], [jax.ShapeDtypeStruct((3,) + s1.shape[1:], s1.dtype)],
                         [pltpu.SemaphoreType.DMA((3,)), pltpu.SemaphoreType.DMA((3,))], name=name, collective_id=collective_id)[0]


def _gather_async(xs, name, collective_id):
    rider = _gather_rider(xs)

    def body(in_refs, out_refs, send_sems, recv_sems):
        ax, ay, ac = _position()
        _handshake([(ax, ay, 1 - ac), (1 - ax, ay, ac), (ax, 1 - ay, ac), (1 - ax, 1 - ay, ac)])
        for hook in rider.hooks(in_refs, out_refs, send_sems, recv_sems):
            hook()

    return _on_sequencer(body, rider.arrays, rider.out_shapes, rider.sems, name=name, collective_id=collective_id)


def _scatter_async(parts, name, collective_id):
    rider = _scatter_rider(parts)

    def body(in_refs, out_refs, send_sems, recv_sems):
        ax, ay, ac = _position()
        flip = lambda a, on: 1 - a if on else a
        _handshake([(flip(ax, rel & 4), flip(ay, rel & 2), flip(ac, rel & 1)) for rel in range(1, NDEV)])
        for hook in rider.hooks(in_refs, out_refs, send_sems, recv_sems):
            hook()

    return _on_sequencer(body, rider.arrays, rider.out_shapes, rider.sems, name=name, collective_id=collective_id)[0]


def _sum_in_chip(own, recv, name):
    _, r, w = own.shape
    tr = _tile(r, (256, 128))

    def body(a_ref, b_ref, o_ref):
        o_ref[...] = (a_ref[...].astype(F32) + b_ref[...].astype(F32)).astype(o_ref.dtype)

    blk = pl.BlockSpec((None, tr, w), lambda q, i: (q, i, 0))
    return pl.pallas_call(body, name=name, grid=(4, r // tr), in_specs=[blk, blk], out_specs=blk,
                          out_shape=jax.ShapeDtypeStruct(own.shape, own.dtype),
                          compiler_params=_cparams("parallel", "parallel"))(own, recv)


def _sum_chips(s1, recv, chip, name):
    _, r, w = s1.shape
    tr = _tile(r, (256, 128))

    def body(c_ref, s_ref, r0_ref, r1_ref, r2_ref, o_ref):
        f = lambda ref: ref[...].astype(F32)
        o_ref[...] = ((f(s_ref) + f(r0_ref)) + f(r1_ref)) + f(r2_ref)

    rblk = lambda k: pl.BlockSpec((None, tr, w), functools.partial(lambda i, c, k: (k, i, 0), k=k))
    grid_spec = pltpu.PrefetchScalarGridSpec(
        num_scalar_prefetch=1, grid=(r // tr,),
        in_specs=[pl.BlockSpec((None, tr, w), lambda i, c: (c[0], i, 0)), rblk(0), rblk(1), rblk(2)],
        out_specs=pl.BlockSpec((tr, w), lambda i, c: (i, 0)))
    return pl.pallas_call(body, name=name, grid_spec=grid_spec, out_shape=jax.ShapeDtypeStruct((r, w), F32),
                          compiler_params=_cparams("parallel"))(chip, s1, recv, recv, recv)


def _silu_rows(x, name):
    def body(x_ref, o_ref):
        o_ref[...] = _silu(x_ref[...])

    return pl.pallas_call(body, name=name, out_shape=jax.ShapeDtypeStruct(x.shape, F32))(x)


def _row_sum(x, name):
    def body(x_ref, o_ref):
        acc = x_ref[0:1, :]
        for i in range(1, x.shape[0]):
            acc = acc + x_ref[i:i + 1, :]
        o_ref[...] = acc

    return pl.pallas_call(body, name=name, out_shape=jax.ShapeDtypeStruct((1, x.shape[1]), F32))(x)


def _adamw(w, g, m, v, name):
    cols = w.shape[-1]
    rows = w.size // cols
    tr = _tile(rows, (128,))
    tc = LANE if (tr == rows and rows > 512 and cols % LANE == 0) else cols

    def body(w_ref, g_ref, m_ref, v_ref, d_ref, mo_ref, vo_ref):
        grad = g_ref[...]
        m_new = ADAM_B1 * m_ref[...] + (1.0 - ADAM_B1) * grad
        v_new = ADAM_B2 * v_ref[...] + (1.0 - ADAM_B2) * jnp.square(grad)
        m_hat = m_new / (1.0 - ADAM_B1 ** ADAM_STEP)
        v_hat = v_new / (1.0 - ADAM_B2 ** ADAM_STEP)
        d_ref[...] = -ADAM_LR * (m_hat / (jnp.sqrt(v_hat) + ADAM_EPS) + ADAM_WD * w_ref[...])
        mo_ref[...] = m_new
        vo_ref[...] = v_new

    blk = pl.BlockSpec((tr, tc), lambda i, j: (i, j))
    out = pl.pallas_call(
        body, name=name, grid=(rows // tr, cols // tc), in_specs=[blk] * 4, out_specs=[blk] * 3,
        out_shape=[jax.ShapeDtypeStruct((rows, cols), F32)] * 3, compiler_params=_cparams("parallel", "parallel"),
    )(*[t.reshape(rows, cols) for t in (w, g, m, v)])
    return [t.reshape(w.shape) for t in out]


def _pack(parts, width, row_mult, dtype):
    flat = jnp.concatenate([p.reshape(-1).astype(dtype) for p in parts])
    rows = -(-flat.shape[0] // (width * row_mult)) * row_mult
    return jnp.pad(flat, (0, rows * width - flat.shape[0])).reshape(rows, width)


def _unpack(flat, shapes):
    out, off = [], 0
    for shp in shapes:
        size = 1
        for dim in shp:
            size *= dim
        out.append(flat[:, off:off + size].reshape((flat.shape[0],) + tuple(shp)))
        off += size
    return out


def _devices_to_cols(a):
    _, r, c = a.shape
    return a.transpose(1, 0, 2).reshape(r, NDEV * c)


def kernel(x, c, w_ada, b_ada, norm1_w, w_in, gdn_conv_w, gdn_a_log, gdn_dt_bias, gdn_norm_w, w_gdn_proj, sc_conv_w, w_sc_out, w_o, norm2_w, w_ffn_in, w_ffn_out, w_ada_f, b_ada_f, normf_w, loss_target, m_w_ada, m_b_ada, m_norm1_w, m_w_in, m_gdn_conv_w, m_gdn_a_log, m_gdn_dt_bias, m_gdn_norm_w, m_w_gdn_proj, m_sc_conv_w, m_w_sc_out, m_w_o, m_norm2_w, m_w_ffn_in, m_w_ffn_out, m_w_ada_f, m_b_ada_f, m_normf_w, v_w_ada, v_b_ada, v_norm1_w, v_w_in, v_gdn_conv_w, v_gdn_a_log, v_gdn_dt_bias, v_gdn_norm_w, v_w_gdn_proj, v_sc_conv_w, v_w_sc_out, v_w_o, v_norm2_w, v_w_ffn_in, v_w_ffn_out, v_w_ada_f, v_b_ada_f, v_normf_w):
    bl, s, d = x.shape
    heads = gdn_a_log.shape[-1]
    dff = w_ffn_out.shape[1] * NDEV
    tok = bl * s
    ax, ay, ac = _position()
    dev = 4 * ax + 2 * ay + ac
    as_tok = lambda a: a.reshape(bl, s, a.shape[-1])
    as_mat = lambda a: a.reshape(tok, a.shape[-1])

    small = _all_gather(_pack([c, gdn_conv_w, sc_conv_w], LANE, 8, F32), name="gather_cond", hbm=False)
    c_all, conv_w, sc_w = _unpack(small.reshape(NDEV, -1), [(bl, d), gdn_conv_w.shape[1:], sc_conv_w.shape[1:]])
    c_act = _silu_rows(c_all.reshape(NDEV * bl, d), "cond_silu")
    conv_w, sc_w = _devices_to_cols(conv_w), _devices_to_cols(sc_w)
    n_ada, n_adaf = w_ada.shape[-1], w_ada_f.shape[-1]
    bias = jnp.broadcast_to(lax.dynamic_slice_in_dim(b_ada, dev * n_ada, n_ada, axis=1), (NDEV * bl, n_ada))
    biasf = jnp.broadcast_to(lax.dynamic_slice_in_dim(b_ada_f.reshape(1, -1), dev * n_adaf, n_adaf, axis=1), (NDEV * bl, n_adaf))
    mod_cols = _mm(c_act, w_ada[0], add=bias, name="ada_cols")
    modf_cols = _mm(c_act, w_ada_f, add=biasf, name="adaf_cols")
    mods = _all_gather(jnp.concatenate([mod_cols, modf_cols], axis=1), name="gather_mod", hbm=False)
    mod_all = mods[:, :, :n_ada].transpose(1, 0, 2).reshape(NDEV * bl, NDEV * n_ada)
    modf_all = mods[:, :, n_ada:].transpose(1, 0, 2).reshape(NDEV * bl, NDEV * n_adaf)
    my_rows = lambda a: lax.dynamic_slice_in_dim(a, dev * bl, bl, axis=0)
    sh1, sc1, g1, sh2, sc2, g2 = [t.reshape(bl, 1, d) for t in jnp.split(my_rows(mod_all), 6, axis=1)]
    shf, scf = [t.reshape(bl, 1, d) for t in jnp.split(my_rows(modf_all), 2, axis=1)]

    late = [t.astype(MXU_DTYPE) for t in (w_gdn_proj[0], w_sc_out[0], w_o[0], w_ffn_in[0].T, w_ffn_out[0])]
    rows = [t.shape[0] for t in late] + [w_in.shape[-1]]
    offs = [sum(rows[:i]) for i in range(5)]
    in_rows = -(-rows[5] // ROW_ALIGN) * ROW_ALIGN
    in_send = jnp.pad(w_in[0].T.astype(MXU_DTYPE), ((0, in_rows - rows[5]), (0, 0)))
    with_own = lambda g, own: lax.dynamic_update_slice_in_dim(g, own[None], dev, axis=0)
    (wt_in,) = _gather_async([in_send], "gather_w_in", 1)
    wt_in = with_own(wt_in, in_send)[:, :rows[5], :].reshape(NDEV * rows[5], d)
    gathered = _gather_async(late[:3], "gather_mixer", 2) + _gather_async(late[3:], "gather_ffn", 3)
    wgp, wso, wo, wt_fi, wfo = [with_own(g, own).reshape(NDEV * own.shape[0], d) for g, own in zip(gathered, late)]
    o_z, o_ab, o_sc, o_ga, o_gb = 3 * d, 4 * d, 4 * d + 2 * heads, 7 * d + 2 * heads, 8 * d + 2 * heads
    s_qkv, s_z, s_sc, s_gate = (0, o_z), (o_z, d), (o_sc, 3 * d), (o_ga, 2 * d)
    wt_ab = jnp.pad(wt_in[o_ab:o_sc], ((0, LANE - 2 * heads), (0, 0)))

    n1w, n2w, nfw = norm1_w.reshape(1, d), norm2_w.reshape(1, d), normf_w.reshape(1, d)
    lanes = lambda a: jnp.pad(a.reshape(1, -1), ((0, 0), (0, LANE - a.size)))
    a_log, dt_bias, gnw = lanes(gdn_a_log), lanes(gdn_dt_bias), gdn_norm_w.reshape(1, HEAD)
    f_gates = functools.partial(_f_gates, heads=heads)
    (h1,) = _tok_fwd(_f_norm_mod, [x], [sh1, sc1], [n1w], [(d, MXU_DTYPE)], name="norm1", ts=512)
    h1m = as_mat(h1)
    p_qkv = as_tok(_mm(h1m, wt_in, tb=True, b_rows=s_qkv, name="in_qkv"))
    p_z = as_tok(_mm(h1m, wt_in, tb=True, b_rows=s_z, name="in_z"))
    p_ab = as_tok(_mm(h1m, wt_ab, tb=True, name="in_ab"))
    p_sc = as_tok(_mm(h1m, wt_in, tb=True, b_rows=s_sc, name="in_sc"))
    p_g = as_tok(_mm(h1m, wt_in, tb=True, b_rows=s_gate, name="in_gate"))
    qkv = _qkv_fwd(p_qkv, conv_w, heads, "qkv_conv")
    (gbeta,) = _tok_fwd(f_gates, [p_ab], [], [a_log, dt_bias], [(LANE, F32)], name="gates", ts=512)
    o, s_all, t_all = _gdn_fwd(qkv, gbeta, heads, "gdn")
    (og,) = _tok_fwd(_f_gdn_out, [o, p_z], [], [(gnw, None)], [(d, MXU_DTYPE)], name="gdn_out", ts=2048, wb=HEAD, cols=heads)
    y_a = as_tok(_mm(as_mat(og), wgp, name="gdn_proj"))
    scp = _sc_fwd(p_sc, sc_w, "sc_conv")
    mrg, y_b = _tok_fwd(_f_merge_keep, [(p_g, 0), (p_g, 1), y_a, _Product(scp, wso)], [], [], [(d, MXU_DTYPE), (d, F32)],
                        name="merge", ts=256, wb=d)
    merge_toks = [(p_g, 0), (p_g, 1), y_a, y_b]
    x2, h2, mix = _tok_fwd(_f_res_norm_mod_keep, [x, _Product(mrg, wo)], [g1, sh2, sc2], [n2w],
                           [(d, F32), (d, MXU_DTYPE), (d, F32)], name="norm2", ts=512)
    act, gu_a, gu_b = _ffn_in_swiglu(as_mat(h2), wt_fi, dff, "ffn_in")

    loss_l, (dx2, dff_out, _), (dg2, dshf, dscf), (dnfw,) = _tok_bwd(
        _f_loss, [x2, _Product(as_tok(act), wfo), loss_target], [g2, shf, scf], [nfw], [], [True, True, False], name="loss",
        ts=256, loss=True, tok_dtype=[F32, MXU_DTYPE, None])
    dffm = as_mat(dff_out)
    dgu_a, dgu_b = _ffn_out_bwd_swiglu(dffm, wfo, gu_a, gu_b, "d_ffn_out")
    gmm = functools.partial(_mm, ta=True, out_dtype=MXU_DTYPE)
    gw_ffn_out = gmm(act, dffm, name="g_ffn_out")
    dh2 = _Product(as_tok(dgu_b), wt_fi, b_rows=(dff, dff), add=as_tok(_mm(dgu_a, wt_fi, b_rows=(0, dff), name="d_ffn_in_a")))
    h2m = as_mat(h2)
    gwt_ffn_in = gmm(dgu_a, h2m, out_rows=2 * dff, name="g_ffn_in_a")
    gwt_ffn_in = gmm(dgu_b, h2m, out_rows=2 * dff, row_off=dff, into=gwt_ffn_in, name="g_ffn_in_b")
    ffn_parts = [(gwt_ffn_in, rows[3]), (gw_ffn_out, rows[4])]
    ffn_recv = _scatter_async(ffn_parts, "scatter_ffn", 4)
    (dx_skip, dmix), (dg1, dsh2, dsc2), (dn2w,) = _tok_bwd(
        _f_res_norm_mod, [x, mix], [g1, sh2, sc2], [n2w], [dx2, dh2], [True, True], name="d_norm2", ts=256,
        tok_dtype=[F32, MXU_DTYPE], after=[gwt_ffn_in, gw_ffn_out])
    gw_o = gmm(as_mat(mrg), as_mat(dmix), name="g_mix_out")
    (dga, dgb, dya, dyb), _, _ = _tok_bwd(_f_merge, merge_toks, [], [], [_Product(dmix, wo, tb=True)], [True] * 4,
                                          name="d_merge", ts=256, wb=d, tok_dtype=MXU_DTYPE)
    dyam, dybm = as_mat(dya), as_mat(dyb)
    dog = as_tok(_mm(dyam, wgp, tb=True, name="d_gdn_proj"))
    gw_gdn_proj = gmm(as_mat(og), dyam, name="g_gdn_proj")
    dscp = as_tok(_mm(dybm, wso, tb=True, name="d_sc_out"))
    gw_sc_out = gmm(as_mat(scp), dybm, name="g_sc_out")
    dscb, dscc, dscx, g_sc_w = _sc_bwd(p_sc, sc_w, dscp, "d_sc_conv")
    mix_parts = [(gw_gdn_proj, rows[0]), (gw_sc_out, rows[1]), (gw_o, rows[2])]
    mix_recv = _scatter_async(mix_parts, "scatter_mixer", 5)
    (do, dz), _, (g_gnw,) = _tok_bwd(_f_gdn_out, [o, p_z], [], [(gnw, None)], [dog], [True, True], name="d_gdn_out",
                                     ts=2048, wb=HEAD, cols=heads, tok_dtype=[F32, MXU_DTYPE],
                                     after=[gw_gdn_proj, gw_sc_out, gw_o])
    own_rows = lambda parts: jnp.concatenate([lax.dynamic_slice_in_dim(g, dev * r, r, axis=0) for g, r in parts], axis=0)
    dqkv, dgbeta = _gdn_bwd(qkv, gbeta, do, s_all, t_all, heads, "d_gdn")
    dp_qkv, g_conv_w = _qkv_bwd(p_qkv, conv_w, dqkv, heads, "d_qkv_conv")
    ffn_red = _sum_direct(own_rows(ffn_parts), ffn_recv, "sum_ffn")
    mix_red = _sum_direct(own_rows(mix_parts), mix_recv, "sum_mix")
    (dp_ab,), _, (g_a_log, g_dt_bias) = _tok_bwd(f_gates, [p_ab], [], [a_log, dt_bias], [dgbeta], [True], name="d_gates",
                                                 ts=512, tok_dtype=MXU_DTYPE)
    sections = [(dp_qkv, s_qkv), (dz, s_z), (dp_ab, None), (dscb, (o_sc, d)), (dscc, (o_sc + d, d)), (dscx, (o_sc + 2 * d, d)),
                (dga, (o_ga, d)), (dgb, (o_gb, d))]
    gwt_in = [gmm(as_mat(dp), h1m, name=f"g_in_{k}") for k, (dp, _) in enumerate(sections)]
    gwt_in[2] = gwt_in[2][:2 * heads]

    r_in = rows[5]
    win = -(-(r_in + max(r_in * k % ROW_ALIGN for k in range(NDEV))) // 128) * 128
    need_rows = max(_window_start(r_in, k) for k in range(NDEV)) + win
    gwt_in = jnp.concatenate(gwt_in + [jnp.zeros((need_rows - NDEV * r_in, d), MXU_DTYPE)], axis=0)
    recv1 = _exchange_in_chip([(gwt_in, r_in, win, 0)], "scatter_in_chip", 7)
    own = jnp.stack([lax.dynamic_slice_in_dim(gwt_in, _window_start(r_in, 2 * q + ac), win, axis=0) for q in range(4)])
    s1 = _sum_in_chip(own, recv1, "sum_in_chip")
    recv2 = _exchange_chips_async(s1, "scatter_chips", 6)

    dh1 = None
    for k, (dp, sec) in enumerate(sections[:-1]):
        dh1 = _mm(as_mat(dp), wt_ab if sec is None else wt_in, b_rows=sec, add=dh1, name=f"d_in_{k}")
    dh1 = _Product(sections[-1][0], wt_in, b_rows=sections[-1][1], add=as_tok(dh1))
    (grad_x,), (dsh1, dsc1), (dn1w,) = _tok_bwd(_f_norm_mod_skip, [x], [sh1, sc1], [n1w], [dh1, dx_skip], [True],
                                                name="d_norm1", ts=256)
    reduced = _sum_chips(s1, recv2, (2 * ax + ay).reshape(1).astype(jnp.int32), "sum_chips")
    gt_w_in = lax.dynamic_slice_in_dim(reduced, r_in * dev - _window_start(r_in, dev), r_in, axis=0)
    g_w_in = gt_w_in.T.reshape(w_in.shape)
    gt_w_ffn_in = ffn_red[:rows[3]]
    g_w_ffn_in = gt_w_ffn_in.T.reshape(w_ffn_in.shape)
    g_w_ffn_out = ffn_red[rows[3]:].reshape(w_ffn_out.shape)
    g_w_gdn_proj, g_w_sc_out, g_w_o = (mix_red[offs[i]:offs[i] + rows[i]].reshape(ref.shape)
                                       for i, ref in enumerate((w_gdn_proj, w_sc_out, w_o)))

    dmod = jnp.concatenate([t.reshape(bl, d) for t in (dsh1, dsc1, dg1, dsh2, dsc2, dg2)], axis=1)
    dmodf = jnp.concatenate([t.reshape(bl, d) for t in (dshf, dscf)], axis=1)
    summed_parts = [dn1w, dn2w, dnfw, g_gnw, g_a_log, g_dt_bias, g_conv_w, g_sc_w, loss_l]
    partial = _all_gather(_pack([dmod, dmodf] + summed_parts, LANE, 8, F32), name="gather_small", hbm=False)
    partial = partial.reshape(NDEV, -1)
    n_rows = bl * (6 * d + 2 * d)
    dmod_all, dmodf_all = _unpack(partial[:, :n_rows], [(bl, 6 * d), (bl, 2 * d)])
    dmod_all, dmodf_all = dmod_all.reshape(NDEV * bl, 6 * d), dmodf_all.reshape(NDEV * bl, 2 * d)
    totals = _row_sum(partial[:, n_rows:], "sum_small")
    t_n1w, t_n2w, t_nfw, t_gnw, t_a_log, t_dt_bias, t_conv_w, t_sc_w, t_loss = [
        t[0] for t in _unpack(totals, [p.shape for p in summed_parts])]
    my_cols = lambda a, n: lax.dynamic_slice_in_dim(a, dev * n, n, axis=1)
    grads = {
        "w_ada": _mm(c_act, my_cols(dmod_all, n_ada), ta=True, name="g_ada").reshape(w_ada.shape),
        "b_ada": _row_sum(dmod_all, "g_ada_bias").reshape(b_ada.shape),
        "norm1_w": t_n1w.reshape(norm1_w.shape),
        "w_in": g_w_in,
        "gdn_conv_w": my_cols(t_conv_w, gdn_conv_w.shape[-1]).reshape(gdn_conv_w.shape),
        "gdn_a_log": t_a_log[:, :heads].reshape(gdn_a_log.shape),
        "gdn_dt_bias": t_dt_bias[:, :heads].reshape(gdn_dt_bias.shape),
        "gdn_norm_w": t_gnw.reshape(gdn_norm_w.shape),
        "w_gdn_proj": g_w_gdn_proj,
        "sc_conv_w": my_cols(t_sc_w, sc_conv_w.shape[-1]).reshape(sc_conv_w.shape),
        "w_sc_out": g_w_sc_out,
        "w_o": g_w_o,
        "norm2_w": t_n2w.reshape(norm2_w.shape),
        "w_ffn_in": g_w_ffn_in,
        "w_ffn_out": g_w_ffn_out,
        "w_ada_f": _mm(c_act, my_cols(dmodf_all, n_adaf), ta=True, name="g_adaf").reshape(w_ada_f.shape),
        "b_ada_f": _row_sum(dmodf_all, "g_adaf_bias").reshape(b_ada_f.shape),
        "normf_w": t_nfw.reshape(normf_w.shape),
    }
    weights = dict(w_ada=w_ada, b_ada=b_ada, norm1_w=norm1_w, w_in=w_in, gdn_conv_w=gdn_conv_w, gdn_a_log=gdn_a_log,
                   gdn_dt_bias=gdn_dt_bias, gdn_norm_w=gdn_norm_w, w_gdn_proj=w_gdn_proj, sc_conv_w=sc_conv_w,
                   w_sc_out=w_sc_out, w_o=w_o, norm2_w=norm2_w, w_ffn_in=w_ffn_in, w_ffn_out=w_ffn_out, w_ada_f=w_ada_f,
                   b_ada_f=b_ada_f, normf_w=normf_w)
    m_in = [m_w_ada, m_b_ada, m_norm1_w, m_w_in, m_gdn_conv_w, m_gdn_a_log, m_gdn_dt_bias, m_gdn_norm_w, m_w_gdn_proj,
            m_sc_conv_w, m_w_sc_out, m_w_o, m_norm2_w, m_w_ffn_in, m_w_ffn_out, m_w_ada_f, m_b_ada_f, m_normf_w]
    v_in = [v_w_ada, v_b_ada, v_norm1_w, v_w_in, v_gdn_conv_w, v_gdn_a_log, v_gdn_dt_bias, v_gdn_norm_w, v_w_gdn_proj,
            v_sc_conv_w, v_w_sc_out, v_w_o, v_norm2_w, v_w_ffn_in, v_w_ffn_out, v_w_ada_f, v_b_ada_f, v_normf_w]
    deltas, new_m, new_v = [], [], []
    grads_t = {"w_in": gt_w_in, "w_ffn_in": gt_w_ffn_in}
    for (wname, wt), mt, vt in zip(weights.items(), m_in, v_in):
        if wname in grads_t:
            back = lambda a, wt=wt: a.T.reshape(wt.shape)
            dl, mn, vn = (back(a) for a in _adamw(wt[0].T, grads_t[wname], mt[0].T, vt[0].T, "adamw_" + wname))
        else:
            dl, mn, vn = _adamw(wt, grads[wname], mt, vt, "adamw_" + wname)
        deltas.append(dl)
        new_m.append(mn)
        new_v.append(vn)
    loss = t_loss[0, 0]
    return (loss, grad_x, *[grads[k] for k in weights], *deltas, *new_m, *new_v)
```

```python
import functools

import jax
import jax.numpy as jnp
from jax import lax
from jax.experimental import pallas as pl
from jax.experimental.pallas import tpu as pltpu
from jax.experimental.pallas import tpu_sc as plsc

F32 = jnp.float32
MXU_DTYPE = jnp.bfloat16
NDEV = 8
CHUNK = 64
HEAD = 128
LANE = 128
EPS = 1e-6
ADAM_LR, ADAM_B1, ADAM_B2, ADAM_EPS, ADAM_WD, ADAM_STEP = 0.001, 0.9, 0.999, 1e-08, 0.01, 10
VMEM_LIMIT = 48 * 1024 * 1024
MESH_IDS = pl.DeviceIdType.MESH
HIGHEST = lax.Precision.HIGHEST


def _tile(n, cands=(512, 256, 128)):
    for c in cands:
        if n % c == 0:
            return c
    return n


def _cparams(*sem):
    return pltpu.CompilerParams(dimension_semantics=sem, vmem_limit_bytes=VMEM_LIMIT)


def _mm(a, b, *, ta=False, tb=False, add=None, out_dtype=F32, name, b_rows=None, out_rows=None, row_off=0, into=None):
    m, k = (a.shape[1], a.shape[0]) if ta else a.shape
    b_shape = b.shape if b_rows is None else (b_rows[1], b.shape[1])
    n = b_shape[0] if tb else b_shape[1]
    assert k == (b_shape[1] if tb else b_shape[0])
    if ta:
        tm, tn = _tile(m), n if n <= 1024 else _tile(n)
        tk = k if k <= 4096 else _tile(k, (4096, 2048, 1024, 512))
        if tm * tk > 1024 * 2048:
            tk = _tile(k, (2048, 1024, 512))
    else:
        tk = k if k <= 1024 else _tile(k, (1024, 512))
        tn = _tile(n, (1024 if tk <= 1024 else 512, 512, 256, 128))
        tm = _tile(m, (2048 if (tn <= 512 and tk <= 1024) else 1024, 1024, 512, 256, 128))
    nk = k // tk
    dims = (((0 if ta else 1,), (1 if tb else 0,)), ((), ()))
    has_add = add is not None

    def body(*refs):
        a_ref, b_ref = refs[0], refs[1]
        add_ref = refs[2] if has_add else None
        o_ref = refs[2 + has_add + (into is not None)]
        part = lax.dot_general(a_ref[...].astype(MXU_DTYPE), b_ref[...].astype(MXU_DTYPE), dims,
                               preferred_element_type=F32)

        def finish(acc):
            if has_add:
                acc = acc + add_ref[...]
            o_ref[...] = acc.astype(o_ref.dtype)

        if nk == 1:
            finish(part)
        else:
            acc_ref = refs[-1]
            kk = pl.program_id(2)

            @pl.when(kk == 0)
            def _():
                acc_ref[...] = part

            @pl.when(kk > 0)
            def _():
                acc_ref[...] += part

            @pl.when(kk == nk - 1)
            def _():
                finish(acc_ref[...])

    a_spec = pl.BlockSpec((tk, tm), lambda i, j, kk: (kk, i)) if ta else pl.BlockSpec((tm, tk), lambda i, j, kk: (i, kk))
    if b_rows is None:
        b_spec = pl.BlockSpec((tn, tk), lambda i, j, kk: (j, kk)) if tb else pl.BlockSpec((tk, tn), lambda i, j, kk: (kk, j))
    else:
        at = lambda t: pl.multiple_of(b_rows[0] + t, ROW_ALIGN)
        b_spec = (pl.BlockSpec((pl.Element(tn), pl.Element(tk)), lambda i, j, kk: (at(j * tn), kk * tk)) if tb else
                  pl.BlockSpec((pl.Element(tk), pl.Element(tn)), lambda i, j, kk: (at(kk * tk), j * tn)))
    add_spec = pl.BlockSpec((tm, tn), lambda i, j, kk: (i, j))
    assert row_off % tm == 0
    o_spec = pl.BlockSpec((tm, tn), lambda i, j, kk: (i + row_off // tm, j))
    in_specs = [a_spec, b_spec] + ([add_spec] if has_add else []) + ([pl.BlockSpec(memory_space=pl.ANY)] if into is not None else [])
    args = [a, b] + ([add] if has_add else []) + ([into] if into is not None else [])
    return pl.pallas_call(
        body, name=name, grid=(m // tm, n // tn, nk), in_specs=in_specs, out_specs=o_spec,
        out_shape=jax.ShapeDtypeStruct((out_rows or m, n), out_dtype),
        scratch_shapes=[pltpu.VMEM((tm, tn), F32)] if nk > 1 else [],
        input_output_aliases={len(args) - 1: 0} if into is not None else {},
        compiler_params=_cparams("parallel", "parallel", "arbitrary"),
    )(*args)


def _swiglu_tiles(m, half):
    tn = _tile(half, (512, 256, 128))
    return _tile(m, (2048 if tn <= 256 else 1024, 1024, 512, 256, 128)), tn


def _ffn_in_swiglu(h, wt, half, name):
    m, k = h.shape
    tm, tn = _swiglu_tiles(m, half)
    nj = half // tn
    dims = (((1,), (1,)), ((), ()))

    def body(h_ref, wa_ref, wb_ref, act_ref, a_ref, b_ref):
        lhs = h_ref[...].astype(MXU_DTYPE)
        a = lax.dot_general(lhs, wa_ref[...].astype(MXU_DTYPE), dims, preferred_element_type=F32)
        b = lax.dot_general(lhs, wb_ref[...].astype(MXU_DTYPE), dims, preferred_element_type=F32)
        act_ref[...] = (_silu(a) * b).astype(act_ref.dtype)
        a_ref[...] = a.astype(a_ref.dtype)
        b_ref[...] = b.astype(b_ref.dtype)

    out = jax.ShapeDtypeStruct((m, half), MXU_DTYPE)
    oblk = pl.BlockSpec((tm, tn), lambda i, j: (i, j))
    return pl.pallas_call(
        body, name=name, grid=(m // tm, nj),
        in_specs=[pl.BlockSpec((tm, k), lambda i, j: (i, 0)), pl.BlockSpec((tn, k), lambda i, j: (j, 0)),
                  pl.BlockSpec((tn, k), lambda i, j: (j + nj, 0))],
        out_specs=[oblk, oblk, oblk], out_shape=[out, out, out], compiler_params=_cparams("parallel", "parallel"),
    )(h, wt, wt)


def _ffn_out_bwd_swiglu(dff, w, a, b, name):
    m, k = dff.shape
    half = w.shape[0]
    tm, tn = _swiglu_tiles(m, half)

    def body(d_ref, w_ref, a_ref, b_ref, da_ref, db_ref):
        dact = lax.dot_general(d_ref[...].astype(MXU_DTYPE), w_ref[...].astype(MXU_DTYPE), (((1,), (1,)), ((), ())),
                               preferred_element_type=F32)
        av, bv = a_ref[...].astype(F32), b_ref[...].astype(F32)
        sig = jax.nn.sigmoid(av)
        da_ref[...] = (dact * bv * (sig * (1.0 + av * (1.0 - sig)))).astype(da_ref.dtype)
        db_ref[...] = (dact * (av * sig)).astype(db_ref.dtype)

    out = jax.ShapeDtypeStruct((m, half), MXU_DTYPE)
    oblk = pl.BlockSpec((tm, tn), lambda i, j: (i, j))
    return pl.pallas_call(
        body, name=name, grid=(m // tm, half // tn),
        in_specs=[pl.BlockSpec((tm, k), lambda i, j: (i, 0)), pl.BlockSpec((tn, k), lambda i, j: (j, 0)), oblk, oblk],
        out_specs=[oblk, oblk], out_shape=[out, out], compiler_params=_cparams("parallel", "parallel"),
    )(dff, w, a, b)


def _with_off(xs):
    return [x if isinstance(x, tuple) else (x, 0) for x in xs]


def _spec(kind, arr, off, ts, wb):
    w = arr.shape[-1] if wb is None else wb
    col = (lambda j: 0) if wb is None else functools.partial(lambda j, o: o + j, o=off)
    if kind == "tok":
        return pl.BlockSpec((None, ts, w), lambda j, b, i: (b, i, col(j)))
    if kind == "bat":
        return pl.BlockSpec((None, 1, w), lambda j, b, i: (b, 0, col(j)))
    if off is None:
        return pl.BlockSpec(arr.shape, lambda j, b, i: (0, 0))
    return pl.BlockSpec((arr.shape[0], w), lambda j, b, i: (0, col(j)))


class _Product:
    def __init__(self, a, b, *, tb=False, b_rows=None, add=None):
        self.a, self.b, self.tb, self.b_rows, self.add = a, b, tb, b_rows, add
        rows = b.shape[0] if b_rows is None else b_rows[1]
        self.shape = a.shape[:2] + (rows if tb else b.shape[1],)

    def inputs(self, ts):
        a_spec = pl.BlockSpec((None, ts, self.a.shape[2]), lambda j, b, i: (b, i, 0))
        if self.b_rows is None:
            b_spec = pl.BlockSpec(self.b.shape, lambda j, b, i: (0, 0))
        else:
            start, count = self.b_rows
            b_spec = pl.BlockSpec((pl.Element(count), pl.Element(self.b.shape[1])), lambda j, b, i: (start, 0))
        extra = [] if self.add is None else [(self.add, pl.BlockSpec((None, ts, self.shape[2]), lambda j, b, i: (b, i, 0)))]
        return [(self.a, a_spec), (self.b, b_spec)] + extra

    def value(self, refs):
        dims = (((1,), (1 if self.tb else 0,)), ((), ()))
        val = lax.dot_general(refs[0][...].astype(MXU_DTYPE), refs[1][...].astype(MXU_DTYPE), dims, preferred_element_type=F32)
        return val if self.add is None else val + refs[2][...].astype(F32)


def _inputs(groups, kinds, ts, wb):
    loaded = [(a, _spec(kind, a, o, ts, wb)) for g, kind in zip(groups, kinds) for a, o in g if not isinstance(a, _Product)]
    made = [pair for g in groups for a, _ in g if isinstance(a, _Product) for pair in a.inputs(ts)]
    return [a for a, _ in loaded + made], [sp for _, sp in loaded + made]


def _values(refs, groups):
    n_loaded = sum(1 for g in groups for a, _ in g if not isinstance(a, _Product))
    loaded, pos, out = iter(refs[:n_loaded]), n_loaded, []
    for g in groups:
        vals = []
        for a, _ in g:
            if isinstance(a, _Product):
                k = 2 if a.add is None else 3
                vals.append(a.value(refs[pos:pos + k]))
                pos += k
            else:
                vals.append(next(loaded)[...].astype(F32))
        out.append(vals)
    return out, pos


def _tok_fwd(fn, toks, bats, pars, outs, *, name, ts, wb=None, cols=1):
    groups = [_with_off(toks), _with_off(bats), _with_off(pars)]
    bl, s, _ = groups[0][0][0].shape
    ts = min(ts, s)
    args, in_specs = _inputs(groups, ("tok", "bat", "par"), ts, wb)

    def body(*refs):
        vals, n_in = _values(refs, groups)
        res = fn(*[v for g in vals for v in g])
        for r, val in zip(refs[n_in:], res):
            r[...] = val.astype(r.dtype)

    out_specs = [pl.BlockSpec((None, ts, w if wb is None else wb), lambda j, b, i: (b, i, j)) for w, _ in outs]
    return pl.pallas_call(
        body, name=name, grid=(cols, bl, s // ts), in_specs=in_specs,
        out_specs=out_specs, out_shape=[jax.ShapeDtypeStruct((bl, s, w), dt) for w, dt in outs],
        compiler_params=_cparams("parallel", "parallel", "parallel"),
    )(*args)


def _accumulate(ref, val, first):
    @pl.when(first)
    def _():
        ref[...] = val

    @pl.when(jnp.logical_not(first))
    def _():
        ref[...] += val


def _tok_bwd(fn, toks, bats, pars, cots, need, *, name, ts, wb=None, cols=1, tok_dtype=F32, loss=False, after=()):
    toks, bats, pars, cots = _with_off(toks), _with_off(bats), _with_off(pars), _with_off(cots)
    groups = [toks, bats, pars, cots]
    bl, s, _ = toks[0][0].shape
    ts = min(ts, s)
    nt, nb, npar = len(toks), len(bats), len(pars)
    args, in_specs = _inputs(groups, ("tok", "bat", "par", "tok"), ts, wb)
    args, in_specs = args + list(after), in_specs + [pl.BlockSpec(memory_space=pl.ANY)] * len(after)

    def body(*refs):
        j, b, i = pl.program_id(0), pl.program_id(1), pl.program_id(2)
        (tok_vals, bat_vals, par_vals, cot_vals), o = _values(refs, groups)
        o += len(after)
        outs, vjp = jax.vjp(fn, *tok_vals, *bat_vals, *par_vals)
        if loss:
            ct = (jnp.ones_like(outs[0]),)
            tot = jnp.broadcast_to(jnp.sum(outs[0], keepdims=True), (1, LANE))
            _accumulate(refs[o], tot, jnp.logical_and(b == 0, i == 0))
            o += 1
        else:
            ct = tuple(cot_vals)
        grads = vjp(ct)
        for t in range(nt):
            if need[t]:
                refs[o][...] = grads[t].astype(refs[o].dtype)
                o += 1
        for t in range(nb):
            _accumulate(refs[o], grads[nt + t], i == 0)
            o += 1
        for t in range(npar):
            first = jnp.logical_and(b == 0, i == 0)
            if pars[t][1] is None:
                first = jnp.logical_and(first, j == 0)
            _accumulate(refs[o], grads[nt + nb + t], first)
            o += 1

    full = lambda arr: arr.shape[-1] if wb is None else wb * cols
    blk = lambda arr: arr.shape[-1] if wb is None else wb
    out_specs, out_shape = [], []
    if loss:
        out_specs.append(pl.BlockSpec((1, LANE), lambda j, b, i: (0, 0)))
        out_shape.append(jax.ShapeDtypeStruct((1, LANE), F32))
    for t in range(nt):
        if need[t]:
            out_specs.append(pl.BlockSpec((None, ts, blk(toks[t][0])), lambda j, b, i: (b, i, j)))
            dt = tok_dtype[t] if isinstance(tok_dtype, (list, tuple)) else tok_dtype
            out_shape.append(jax.ShapeDtypeStruct((bl, s, full(toks[t][0])), dt))
    for arr, _ in bats:
        out_specs.append(pl.BlockSpec((None, 1, blk(arr)), lambda j, b, i: (b, 0, j)))
        out_shape.append(jax.ShapeDtypeStruct((bl, 1, full(arr)), F32))
    for arr, off in pars:
        if off is None:
            out_specs.append(pl.BlockSpec(arr.shape, lambda j, b, i: (0, 0)))
            out_shape.append(jax.ShapeDtypeStruct(arr.shape, F32))
        else:
            out_specs.append(pl.BlockSpec((arr.shape[0], blk(arr)), lambda j, b, i: (0, j)))
            out_shape.append(jax.ShapeDtypeStruct((arr.shape[0], full(arr)), F32))
    res = list(pl.pallas_call(
        body, name=name, grid=(cols, bl, s // ts), in_specs=in_specs,
        out_specs=out_specs, out_shape=out_shape, compiler_params=_cparams("arbitrary", "arbitrary", "arbitrary"),
    )(*args))
    tot = res.pop(0) if loss else None
    dtoks = [res.pop(0) if need[t] else None for t in range(nt)]
    dbats = [res.pop(0) for _ in range(nb)]
    dpars = [res.pop(0) for _ in range(npar)]
    return (tot, dtoks, dbats, dpars) if loss else (dtoks, dbats, dpars)


def _silu(x):
    return x * jax.nn.sigmoid(x)


def _rms(x, w):
    return x * lax.rsqrt(jnp.mean(x * x, axis=-1, keepdims=True) + EPS) * w


def _f_norm_mod(x, shift, scale, w):
    return (_rms(x, w) * (1.0 + scale) + shift,)


def _f_norm_mod_skip(x, shift, scale, w):
    return _rms(x, w) * (1.0 + scale) + shift, x


def _f_res_norm_mod(x, mix, gate, shift, scale, w):
    x2 = x + gate * mix
    return x2, _rms(x2, w) * (1.0 + scale) + shift


def _f_res_norm_mod_keep(x, mix, gate, shift, scale, w):
    return (*_f_res_norm_mod(x, mix, gate, shift, scale, w), mix)


def _f_gates(p, a_log, dt_bias, *, heads):
    z = p + dt_bias
    g = -jnp.exp(a_log) * (jnp.maximum(z, 0.0) + jnp.log1p(jnp.exp(jnp.minimum(z, -z))))
    lane = lax.broadcasted_iota(jnp.int32, p.shape, 1)
    return (jnp.where(lane < heads, g, jax.nn.sigmoid(p)),)


def _f_gdn_out(o, z, w):
    return (_rms(o, w) * _silu(z),)


def _f_merge(ga, gb, ya, yb):
    return (jax.nn.sigmoid(ga) * ya + jax.nn.sigmoid(gb) * yb,)


def _f_merge_keep(ga, gb, ya, yb):
    return (*_f_merge(ga, gb, ya, yb), yb)


def _f_loss(x2, ff, tgt, gate, shift, scale, w):
    y = _rms(x2 + gate * ff, w) * (1.0 + scale) + shift
    return (0.5 * jnp.mean(jnp.square(y - tgt), axis=-1, keepdims=True),)


def _shift_down(x, s):
    if s == 0:
        return x
    row = lax.broadcasted_iota(jnp.int32, x.shape, 0)
    return jnp.where(row >= s, pltpu.roll(x, s, 0), 0.0)


def _shift_up(x, s):
    if s == 0:
        return x
    n = x.shape[0]
    row = lax.broadcasted_iota(jnp.int32, x.shape, 0)
    return jnp.where(row < n - s, pltpu.roll(x, n - s, 0), 0.0)


def _conv(x, w):
    width = w.shape[0]
    acc = w[width - 1:width, :] * x
    for j in range(width - 1):
        acc = acc + w[j:j + 1, :] * _shift_down(x, width - 1 - j)
    return acc


def _conv_bwd(dy, x, w, dw_ref, first):
    width = w.shape[0]
    dx = w[width - 1:width, :] * dy
    for j in range(width - 1):
        dx = dx + w[j:j + 1, :] * _shift_up(dy, width - 1 - j)
    for j in range(width):
        row = jnp.sum(dy * _shift_down(x, width - 1 - j), axis=0, keepdims=True)
        _accumulate(dw_ref.at[j:j + 1, :], row, first)
    return dx


def _qkv_act(xc, is_v, scale):
    a = _silu(xc)
    nrm = a * lax.rsqrt(jnp.sum(a * a, axis=-1, keepdims=True) + EPS) * scale
    return jnp.where(is_v, a, nrm)


def _qkv_consts(j, heads):
    is_v = j >= 2 * heads
    scale = jnp.where(j < heads, HEAD ** -0.5, 1.0).astype(F32)
    return is_v, scale


def _qkv_fwd(p, w, heads, name):
    bl, s, w3 = p.shape

    def body(p_ref, w_ref, o_ref):
        is_v, scale = _qkv_consts(pl.program_id(0), heads)
        o_ref[...] = _qkv_act(_conv(p_ref[...], w_ref[...]), is_v, scale)

    blk = pl.BlockSpec((None, s, HEAD), lambda j, b: (b, 0, j))
    return pl.pallas_call(
        body, name=name, grid=(w3 // HEAD, bl), in_specs=[blk, pl.BlockSpec((w.shape[0], HEAD), lambda j, b: (0, j))],
        out_specs=blk, out_shape=jax.ShapeDtypeStruct(p.shape, F32), compiler_params=_cparams("parallel", "parallel"),
    )(p, w)


def _qkv_bwd(p, w, dout, heads, name):
    bl, s, w3 = p.shape

    def body(p_ref, w_ref, d_ref, dp_ref, dw_ref):
        is_v, scale = _qkv_consts(pl.program_id(0), heads)
        x, wv = p_ref[...], w_ref[...]
        _, vjp = jax.vjp(lambda xc: _qkv_act(xc, is_v, scale), _conv(x, wv))
        (dxc,) = vjp(d_ref[...])
        dp_ref[...] = _conv_bwd(dxc, x, wv, dw_ref, pl.program_id(1) == 0).astype(dp_ref.dtype)

    blk = pl.BlockSpec((None, s, HEAD), lambda j, b: (b, 0, j))
    wblk = pl.BlockSpec((w.shape[0], HEAD), lambda j, b: (0, j))
    return pl.pallas_call(
        body, name=name, grid=(w3 // HEAD, bl), in_specs=[blk, wblk, blk], out_specs=[blk, wblk],
        out_shape=[jax.ShapeDtypeStruct(p.shape, MXU_DTYPE), jax.ShapeDtypeStruct(w.shape, F32)],
        compiler_params=_cparams("arbitrary", "arbitrary"),
    )(p, w, dout)


def _sc_specs(p, w):
    bl, s, w3 = p.shape
    nblk = w3 // 3 // LANE
    sec = lambda k: pl.BlockSpec((None, s, LANE), functools.partial(lambda j, b, k: (b, 0, k * nblk + j), k=k))
    return nblk, [sec(0), sec(1), sec(2)], pl.BlockSpec((w.shape[0], LANE), lambda j, b: (0, j)), \
        pl.BlockSpec((None, s, LANE), lambda j, b: (b, 0, j))


def _sc_fwd(p, w, name):
    bl, s, w3 = p.shape
    nblk, secs, wblk, oblk = _sc_specs(p, w)

    def body(b_ref, c_ref, x_ref, w_ref, o_ref):
        o_ref[...] = (b_ref[...] * _conv(c_ref[...] * x_ref[...], w_ref[...])).astype(o_ref.dtype)

    return pl.pallas_call(
        body, name=name, grid=(nblk, bl), in_specs=secs + [wblk], out_specs=oblk,
        out_shape=jax.ShapeDtypeStruct((bl, s, w3 // 3), MXU_DTYPE), compiler_params=_cparams("parallel", "parallel"),
    )(p, p, p, w)


def _sc_bwd(p, w, dout, name):
    bl, s, w3 = p.shape
    nblk, secs, wblk, oblk = _sc_specs(p, w)

    def body(b_ref, c_ref, x_ref, w_ref, d_ref, db_ref, dc_ref, dx_ref, dw_ref):
        gb, gc, xin, wv, d = b_ref[...], c_ref[...], x_ref[...], w_ref[...], d_ref[...]
        u = gc * xin
        db_ref[...] = (d * _conv(u, wv)).astype(db_ref.dtype)
        du = _conv_bwd(d * gb, u, wv, dw_ref, pl.program_id(1) == 0)
        dc_ref[...] = (du * xin).astype(dc_ref.dtype)
        dx_ref[...] = (du * gc).astype(dx_ref.dtype)

    act = jax.ShapeDtypeStruct((bl, s, w3 // 3), MXU_DTYPE)
    return pl.pallas_call(
        body, name=name, grid=(nblk, bl), in_specs=secs + [wblk, oblk], out_specs=[oblk, oblk, oblk, wblk],
        out_shape=[act, act, act, jax.ShapeDtypeStruct(w.shape, F32)], compiler_params=_cparams("arbitrary", "arbitrary"),
    )(p, p, p, w, dout)


def _bdot(a, b, ca, cb):
    return lax.dot_general(a.astype(MXU_DTYPE), b.astype(MXU_DTYPE), (((ca,), (cb,)), ((), ())),
                           preferred_element_type=F32)


def _hdot(a, b):
    return lax.dot_general(a, b, (((1,), (0,)), ((), ())), precision=HIGHEST, preferred_element_type=F32)


def _lane_col(x, idx):
    lane = lax.broadcasted_iota(jnp.int32, x.shape, 1)
    return jnp.sum(jnp.where(lane == idx, x, 0.0), axis=1, keepdims=True)


def _chunk_masks():
    r = lax.broadcasted_iota(jnp.int32, (CHUNK, CHUNK), 0)
    c = lax.broadcasted_iota(jnp.int32, (CHUNK, CHUNK), 1)
    return r == c, r >= c, r > c


def _dot3(a, b):
    ah, bh = a.astype(MXU_DTYPE), b.astype(MXU_DTYPE)
    al, bl = (a - ah.astype(F32)).astype(MXU_DTYPE), (b - bh.astype(F32)).astype(MXU_DTYPE)
    dot = lambda x, y: lax.dot_general(x, y, (((1,), (0,)), ((), ())), preferred_element_type=F32)
    return dot(ah, bh) + (dot(ah, bl) + dot(al, bh))


def _tri_inv_steps(low, eye):
    x = -low
    p = jnp.where(eye, 1.0, 0.0) + x
    span = 2
    while span < CHUNK:
        x = _dot3(x, x)
        yield
        p = p + _dot3(p, x)
        yield
        span *= 2
    return p


def _round_robin(gens):
    out, live = [None] * len(gens), list(range(len(gens)))
    while live:
        still = []
        for i in live:
            try:
                next(gens[i])
                still.append(i)
            except StopIteration as stop:
                out[i] = stop.value
        live = still
    return out


def _gdn_pre(q, k, v, gc, beta, masks):
    eye, causal, strict = masks
    gc_row = jnp.sum(jnp.where(eye, gc, 0.0), axis=0, keepdims=True)
    decay = jnp.where(causal, jnp.exp(jnp.where(causal, gc - gc_row, 0.0)), 0.0)
    eg = jnp.exp(gc)
    gl = gc[CHUNK - 1:CHUNK, :]
    kb, vb = k * beta, v * beta
    low = jnp.where(strict, _bdot(kb, k, 1, 1) * decay, 0.0)
    qk = jnp.where(causal, _bdot(q, k, 1, 1) * decay, 0.0)
    rest = jnp.exp(gl - gc)
    return dict(decay=decay, eg=eg, gl=gl, kb=kb, vb=vb, kbe=kb * eg, low=low, qk=qk, qg=q * eg, rest=rest, kdec=k * rest)


def _gdn_specs(qkv, gbeta, heads, rev):
    bl, s, w3 = qkv.shape
    d, n = w3 // 3, s // CHUNK
    at = (lambda c: n - 1 - c) if rev else (lambda c: c)
    assert d == heads * HEAD
    sec = pl.BlockSpec((None, CHUNK, w3), lambda b, c: (b, at(c), 0))
    gspec = pl.BlockSpec((None, CHUNK, LANE), lambda b, c: (b, at(c), 0))
    sspec = pl.BlockSpec((None, None, heads, HEAD, HEAD), lambda b, c: (b, at(c), 0, 0, 0))
    tspec = pl.BlockSpec((None, None, heads, CHUNK, CHUNK), lambda b, c: (b, at(c), 0, 0, 0))
    return bl, s, d, n, sec, gspec, sspec, tspec


def _gdn_fwd(qkv, gbeta, heads, name):
    bl, s, d, n, sec, gspec, sspec, tspec = _gdn_specs(qkv, gbeta, heads, False)

    def body(x_ref, g_ref, o_ref, s_ref, t_ref, st_ref):
        @pl.when(pl.program_id(1) == 0)
        def _():
            st_ref[...] = jnp.zeros_like(st_ref)

        masks = _chunk_masks()
        eye, causal, _ = masks
        gblk = g_ref[...]
        gc_all = _hdot(jnp.where(causal, 1.0, 0.0), gblk)
        st_all = st_ref[...]

        def head(h):
            st = st_all[h]
            q, k, v = (x_ref[:, sec * d + h * HEAD:sec * d + (h + 1) * HEAD] for sec in range(3))
            pre = _gdn_pre(q, k, v, _lane_col(gc_all, h), _lane_col(gblk, heads + h), masks)
            yield
            t = yield from _tri_inv_steps(pre["low"], eye)
            u, w = _bdot(t, pre["vb"], 1, 0), _bdot(t, pre["kbe"], 1, 0)
            yield
            vnew = u - _bdot(w, st, 1, 0)
            yield
            out = _bdot(pre["qg"], st, 1, 0) + _bdot(pre["qk"], vnew, 1, 0)
            return out, t, st * jnp.exp(pre["gl"]) + _bdot(pre["kdec"], vnew, 0, 0)

        outs, ts, states = zip(*_round_robin([head(h) for h in range(heads)]))
        o_ref[...] = jnp.concatenate(outs, axis=1)
        s_ref[...] = st_all
        t_ref[...] = jnp.stack(ts)
        st_ref[...] = jnp.stack(states)

    return pl.pallas_call(
        body, name=name, grid=(bl, n), in_specs=[sec, gspec],
        out_specs=[pl.BlockSpec((None, CHUNK, d), lambda b, c: (b, c, 0)), sspec, tspec],
        out_shape=[jax.ShapeDtypeStruct((bl, s, d), F32), jax.ShapeDtypeStruct((bl, n, heads, HEAD, HEAD), F32),
                   jax.ShapeDtypeStruct((bl, n, heads, CHUNK, CHUNK), F32)],
        scratch_shapes=[pltpu.VMEM((heads, HEAD, HEAD), F32)], compiler_params=_cparams("parallel", "arbitrary"),
    )(qkv, gbeta)


def _gdn_bwd(qkv, gbeta, dout, s_all, t_all, heads, name):
    bl, s, d, n, sec, gspec, sspec, tspec = _gdn_specs(qkv, gbeta, heads, True)
    ospec = pl.BlockSpec((None, CHUNK, d), lambda b, c: (b, n - 1 - c, 0))

    def body(x_ref, g_ref, do_ref, s_ref, t_ref, dx_ref, dg_ref, ds_ref):
        @pl.when(pl.program_id(1) == 0)
        def _():
            ds_ref[...] = jnp.zeros_like(ds_ref)

        masks = _chunk_masks()
        eye, causal, strict = masks
        gblk = g_ref[...]
        gc_all = _hdot(jnp.where(causal, 1.0, 0.0), gblk)
        lane = lax.broadcasted_iota(jnp.int32, gblk.shape, 1)
        last_row = lax.broadcasted_iota(jnp.int32, (CHUNK, 1), 0) == CHUNK - 1
        rowsum = lambda a: jnp.sum(a, axis=1, keepdims=True)
        st_all, t_all_, ds_all = s_ref[...], t_ref[...], ds_ref[...]

        def head(h):
            sl = slice(h * HEAD, (h + 1) * HEAD)
            q, k, v = (x_ref[:, sec * d + h * HEAD:sec * d + (h + 1) * HEAD] for sec in range(3))
            do = do_ref[:, sl]
            beta = _lane_col(gblk, heads + h)
            st, t, dsn = st_all[h], t_all_[h], ds_all[h]
            pre = _gdn_pre(q, k, v, _lane_col(gc_all, h), beta, masks)
            decay, eg, kb, vb, kbe, low, qk, qg, kdec = (pre[x] for x in ("decay", "eg", "kb", "vb", "kbe", "low", "qk", "qg", "kdec"))
            egl = jnp.exp(pre["gl"])
            yield
            u, w = _bdot(t, vb, 1, 0), _bdot(t, kbe, 1, 0)
            yield
            vnew = u - _bdot(w, st, 1, 0)
            yield
            dkdec = _bdot(vnew, dsn, 1, 1)
            dvnew = _bdot(kdec, dsn, 1, 0) + _bdot(qk, do, 0, 0)
            dgl = jnp.sum(dsn * st, keepdims=True) * egl
            dqg = _bdot(do, st, 1, 1)
            dqk = jnp.where(causal, _bdot(do, vnew, 1, 1), 0.0)
            yield
            dw = -_bdot(dvnew, st, 1, 1)
            ds_new = dsn * egl + _bdot(qg, do, 0, 0) - _bdot(w, dvnew, 0, 0)
            yield
            dt = _bdot(dvnew, vb, 1, 1) + _bdot(dw, kbe, 1, 1)
            dvb, dkbe = _bdot(t, dvnew, 0, 0), _bdot(t, dw, 0, 0)
            yield
            inner = _bdot(dt, t, 1, 1)
            yield
            dlow = -jnp.where(strict, _bdot(t, inner, 0, 0), 0.0)
            da, db = dlow * decay, dqk * decay
            yield
            m = dlow * low + dqk * qk
            kdk = dkdec * kdec
            col_of_m = jnp.sum(jnp.where(eye, jnp.sum(m, axis=0, keepdims=True), 0.0), axis=1, keepdims=True)
            dgc = rowsum(m) - col_of_m + rowsum(dqg * qg) + rowsum(dkbe * kbe) - rowsum(kdk)
            dgc = dgc + jnp.where(last_row, dgl + jnp.sum(kdk, keepdims=True), 0.0)
            dkb = _bdot(da, k, 1, 0) + dkbe * eg
            yield
            dk = _bdot(da, kb, 0, 0) + _bdot(db, q, 0, 0) + dkdec * pre["rest"] + dkb * beta
            dq = _bdot(db, k, 1, 0) + dqg * eg
            dbeta = rowsum(dkb * k) + rowsum(dvb * v)
            return dq, dk, dvb * beta, jnp.where(lane == h, dgc, 0.0) + jnp.where(lane == heads + h, dbeta, 0.0), ds_new

        dqs, dks, dvs, dgs, dss = zip(*_round_robin([head(h) for h in range(heads)]))
        dx_ref[...] = jnp.concatenate(dqs + dks + dvs, axis=1)
        ds_ref[...] = jnp.stack(dss)
        dgb = dgs[0]
        for extra in dgs[1:]:
            dgb = dgb + extra
        upper = jnp.where(jnp.logical_or(eye, jnp.logical_not(causal)), 1.0, 0.0)
        dg_ref[...] = jnp.where(lane < heads, _hdot(upper, dgb), dgb)

    return pl.pallas_call(
        body, name=name, grid=(bl, n), in_specs=[sec, gspec, ospec, sspec, tspec], out_specs=[sec, gspec],
        out_shape=[jax.ShapeDtypeStruct(qkv.shape, F32), jax.ShapeDtypeStruct((bl, s, LANE), F32)],
        scratch_shapes=[pltpu.VMEM((heads, HEAD, HEAD), F32)], compiler_params=_cparams("parallel", "arbitrary"),
    )(qkv, gbeta, dout, s_all, t_all)


def _position():
    return lax.axis_index("x"), lax.axis_index("y"), lax.axis_index("c")


def _all_gather(x, *, name, hbm):
    space = pltpu.HBM if hbm else pltpu.VMEM

    def body(x_ref, out_ref, send_sems, recv_sems, local_sem):
        ax, ay, ac = _position()
        me, sibling = (ax, ay, ac), (ax, ay, 1 - ac)
        chips = [(1 - ax, ay), (ax, 1 - ay), (1 - ax, 1 - ay)]

        def slot(px, py, pc):
            return out_ref.at[4 * px + 2 * py + pc]

        def copy(k, block, to, src=None):
            return pltpu.make_async_remote_copy(
                src_ref=slot(*block) if src is None else src, dst_ref=slot(*block), send_sem=send_sems.at[k],
                recv_sem=recv_sems.at[k], device_id=to, device_id_type=MESH_IDS)

        mine = pltpu.make_async_copy(x_ref, slot(*me), local_sem)
        mine.start()
        first = [copy(0, me, sibling, src=x_ref)] + [copy(1 + j, me, (*chip, ac), src=x_ref) for j, chip in enumerate(chips)]
        for cp in first:
            cp.start()
        passed = [copy(4 + j, (*chip, ac), sibling) for j, chip in enumerate(chips)]
        for j, chip in enumerate(chips):
            copy(1 + j, (*chip, ac), me).wait_recv()
            passed[j].start()
        copy(0, sibling, me).wait_recv()
        for j, chip in enumerate(chips):
            copy(4 + j, (*chip, 1 - ac), me).wait_recv()
        for cp in first + passed:
            cp.wait_send()
        mine.wait()

    return pl.pallas_call(
        body, name=name, out_shape=jax.ShapeDtypeStruct((NDEV,) + x.shape, x.dtype),
        in_specs=[pl.BlockSpec(memory_space=space)], out_specs=pl.BlockSpec(memory_space=space),
        scratch_shapes=[pltpu.SemaphoreType.DMA((7,)), pltpu.SemaphoreType.DMA((7,)), pltpu.SemaphoreType.DMA],
    )(x)


class _Rider:
    def __init__(self, arrays, out_shapes, sems, hooks):
        self.arrays, self.out_shapes, self.sems, self.hooks = arrays, out_shapes, sems, hooks


def _gather_rider(xs):
    n = len(xs)

    def hooks(x_refs, out_refs, send_sems, recv_sems):
        ax, ay, ac = _position()
        me, sibling = (ax, ay, ac), (ax, ay, 1 - ac)
        chips = [(1 - ax, ay), (ax, 1 - ay), (1 - ax, 1 - ay)]

        def copies(k, block, to, own=False):
            out = []
            for i in range(n):
                slot = out_refs[i].at[4 * block[0] + 2 * block[1] + block[2]]
                out.append(pltpu.make_async_remote_copy(
                    src_ref=x_refs[i] if own else slot, dst_ref=slot, send_sem=send_sems.at[k, i], recv_sem=recv_sems.at[k, i],
                    device_id=to, device_id_type=MESH_IDS))
            return out

        def first():
            for cp in copies(0, me, sibling, own=True):
                cp.start()
            for j, chip in enumerate(chips):
                for cp in copies(1 + j, me, (*chip, ac), own=True):
                    cp.start()

        def mid():
            for j, chip in enumerate(chips):
                for arrived, onward in zip(copies(1 + j, (*chip, ac), me), copies(4 + j, (*chip, ac), sibling)):
                    arrived.wait_recv()
                    onward.start()

        def last():
            for cp in copies(0, sibling, me):
                cp.wait_recv()
            for j, chip in enumerate(chips):
                for cp in copies(4 + j, (*chip, 1 - ac), me):
                    cp.wait_recv()
            for cp in copies(0, me, sibling, own=True):
                cp.wait_send()
            for j, chip in enumerate(chips):
                for cp in copies(1 + j, me, (*chip, ac), own=True) + copies(4 + j, (*chip, ac), sibling):
                    cp.wait_send()

        return first, mid, last

    return _Rider(list(xs), [jax.ShapeDtypeStruct((NDEV,) + x.shape, x.dtype) for x in xs],
                  [pltpu.SemaphoreType.DMA((7, n)), pltpu.SemaphoreType.DMA((7, n))], hooks)


def _scatter_rider(parts):
    packed = sum(r for _, r in parts)
    width, dtype = parts[0][0].shape[1], parts[0][0].dtype

    def hooks(g_refs, out_refs, send_sems, recv_sems):
        (recv_ref,) = out_refs
        ax, ay, ac = _position()

        def peer(rel):
            flip = lambda a, bit: 1 - a if rel & bit else a
            return flip(ax, 4), flip(ay, 2), flip(ac, 1)

        def first():
            for rel in range(1, NDEV):
                px, py, pc = peer(rel)
                off = 0
                for g_ref, (_, r) in zip(g_refs, parts):
                    rows = g_ref.at[pl.ds(pl.multiple_of((4 * px + 2 * py + pc) * r, ROW_ALIGN), r)]
                    pltpu.make_async_remote_copy(
                        src_ref=rows, dst_ref=recv_ref.at[rel - 1, pl.ds(off, r)], send_sem=send_sems.at[rel - 1],
                        recv_sem=recv_sems.at[rel - 1], device_id=(px, py, pc), device_id_type=MESH_IDS).start()
                    off += r

        def last():
            for rel in range(1, NDEV):
                slot = recv_ref.at[rel - 1]
                pltpu.make_async_remote_copy(src_ref=slot, dst_ref=slot, send_sem=send_sems.at[rel - 1],
                                             recv_sem=recv_sems.at[rel - 1], device_id=peer(rel), device_id_type=MESH_IDS).wait()

        return first, lambda: None, last

    return _Rider([g for g, _ in parts], [jax.ShapeDtypeStruct((NDEV - 1, packed, width), dtype)],
                  [pltpu.SemaphoreType.DMA((NDEV - 1,)), pltpu.SemaphoreType.DMA((NDEV - 1,))], hooks)


def _sum_direct(own, recv, name):
    r, w = own.shape
    tr = max(t for t in range(ROW_ALIGN, 257, ROW_ALIGN) if r % t == 0)

    def body(own_ref, *refs):
        acc = own_ref[...].astype(F32)
        for ref in refs[:-1]:
            acc = acc + ref[...].astype(F32)
        refs[-1][...] = acc

    rblk = lambda k: pl.BlockSpec((None, tr, w), functools.partial(lambda i, k: (k, i, 0), k=k))
    blk = pl.BlockSpec((tr, w), lambda i: (i, 0))
    return pl.pallas_call(body, name=name, grid=(r // tr,), in_specs=[blk] + [rblk(k) for k in range(NDEV - 1)],
                          out_specs=blk, out_shape=jax.ShapeDtypeStruct((r, w), F32),
                          compiler_params=_cparams("parallel"))(own, *([recv] * (NDEV - 1)))


ROW_ALIGN = 16


def _window_start(rows_per_dev, k):
    return rows_per_dev * k // ROW_ALIGN * ROW_ALIGN


def _exchange_in_chip(parts, name, collective_id):
    packed = sum(win for _, _, win, _ in parts)
    width, dtype = parts[0][0].shape[1], parts[0][0].dtype

    def body(g_refs, out_refs, send_sems, recv_sems):
        (recv_ref,) = out_refs
        ax, ay, ac = _position()
        sibling = (ax, ay, 1 - ac)
        _handshake([sibling])
        for q in range(4):
            for g_ref, (_, r, win, off) in zip(g_refs, parts):
                there = g_ref.at[pl.ds(pl.multiple_of(_window_start(r, 2 * q + 1 - ac), ROW_ALIGN), win)]
                pltpu.make_async_remote_copy(src_ref=there, dst_ref=recv_ref.at[q, pl.ds(off, win)], send_sem=send_sems.at[q],
                                             recv_sem=recv_sems.at[q], device_id=sibling, device_id_type=MESH_IDS).start()
        for q in range(4):
            pltpu.make_async_remote_copy(src_ref=recv_ref.at[q], dst_ref=recv_ref.at[q], send_sem=send_sems.at[q],
                                         recv_sem=recv_sems.at[q], device_id=sibling, device_id_type=MESH_IDS).wait()

    return _on_sequencer(body, [g for g, _, _, _ in parts], [jax.ShapeDtypeStruct((4, packed, width), dtype)],
                         [pltpu.SemaphoreType.DMA((4,)), pltpu.SemaphoreType.DMA((4,))], name=name, collective_id=collective_id)[0]


def _on_sequencer(body, ins, out_shapes, sems, *, name, collective_id):
    hbm = pltpu.MemorySpace.HBM
    in_refs = [jax.new_ref(a, memory_space=hbm) for a in ins]
    out_refs = [jax.empty_ref(s, memory_space=hbm) for s in out_shapes]

    @pl.kernel(mesh=plsc.ScalarSubcoreMesh(axis_name="sequencer", num_cores=1), name=name, scratch_types=tuple(sems),
               compiler_params=pltpu.CompilerParams(collective_id=collective_id))
    def launch(*sem_refs):
        body(in_refs, out_refs, *sem_refs)

    launch()
    return [r[...] for r in out_refs]


def _handshake(peers):
    barrier = pltpu.get_barrier_semaphore()
    for peer in peers:
        pl.semaphore_signal(barrier, inc=1, device_id=peer, device_id_type=MESH_IDS)
    pl.semaphore_wait(barrier, len(peers))


def _exchange_chips_async(s1, name, collective_id):
    def body(in_refs, out_refs, send_sems, recv_sems):
        (src,), (got,) = in_refs, out_refs
        ax, ay, ac = _position()
        chips = [(1 - ax, ay), (ax, 1 - ay), (1 - ax, 1 - ay)]
        _handshake([(cx, cy, ac) for cx, cy in chips])
        copies = [pltpu.make_async_remote_copy(
            src_ref=src.at[2 * cx + cy], dst_ref=got.at[r], send_sem=send_sems.at[r], recv_sem=recv_sems.at[r],
            device_id=(cx, cy, ac), device_id_type=MESH_IDS) for r, (cx, cy) in enumerate(chips)]
        for cp in copies:
            cp.start()
        for cp in copies:
            cp.wait_recv()
        for cp in copies:
            cp.wait_send()

    return _on_sequencer(body, [s1], [jax.ShapeDtypeStruct((3,) + s1.shape[1:], s1.dtype)],
                         [pltpu.SemaphoreType.DMA((3,)), pltpu.SemaphoreType.DMA((3,))], name=name, collective_id=collective_id)[0]


def _gather_async(xs, name, collective_id):
    rider = _gather_rider(xs)

    def body(in_refs, out_refs, send_sems, recv_sems):
        ax, ay, ac = _position()
        _handshake([(ax, ay, 1 - ac), (1 - ax, ay, ac), (ax, 1 - ay, ac), (1 - ax, 1 - ay, ac)])
        for hook in rider.hooks(in_refs, out_refs, send_sems, recv_sems):
            hook()

    return _on_sequencer(body, rider.arrays, rider.out_shapes, rider.sems, name=name, collective_id=collective_id)


def _scatter_async(parts, name, collective_id):
    rider = _scatter_rider(parts)

    def body(in_refs, out_refs, send_sems, recv_sems):
        ax, ay, ac = _position()
        flip = lambda a, on: 1 - a if on else a
        _handshake([(flip(ax, rel & 4), flip(ay, rel & 2), flip(ac, rel & 1)) for rel in range(1, NDEV)])
        for hook in rider.hooks(in_refs, out_refs, send_sems, recv_sems):
            hook()

    return _on_sequencer(body, rider.arrays, rider.out_shapes, rider.sems, name=name, collective_id=collective_id)[0]


def _sum_in_chip(own, recv, name):
    _, r, w = own.shape
    tr = _tile(r, (256, 128))

    def body(a_ref, b_ref, o_ref):
        o_ref[...] = (a_ref[...].astype(F32) + b_ref[...].astype(F32)).astype(o_ref.dtype)

    blk = pl.BlockSpec((None, tr, w), lambda q, i: (q, i, 0))
    return pl.pallas_call(body, name=name, grid=(4, r // tr), in_specs=[blk, blk], out_specs=blk,
                          out_shape=jax.ShapeDtypeStruct(own.shape, own.dtype),
                          compiler_params=_cparams("parallel", "parallel"))(own, recv)


def _sum_chips(s1, recv, chip, name):
    _, r, w = s1.shape
    tr = _tile(r, (256, 128))

    def body(c_ref, s_ref, r0_ref, r1_ref, r2_ref, o_ref):
        f = lambda ref: ref[...].astype(F32)
        o_ref[...] = ((f(s_ref) + f(r0_ref)) + f(r1_ref)) + f(r2_ref)

    rblk = lambda k: pl.BlockSpec((None, tr, w), functools.partial(lambda i, c, k: (k, i, 0), k=k))
    grid_spec = pltpu.PrefetchScalarGridSpec(
        num_scalar_prefetch=1, grid=(r // tr,),
        in_specs=[pl.BlockSpec((None, tr, w), lambda i, c: (c[0], i, 0)), rblk(0), rblk(1), rblk(2)],
        out_specs=pl.BlockSpec((tr, w), lambda i, c: (i, 0)))
    return pl.pallas_call(body, name=name, grid_spec=grid_spec, out_shape=jax.ShapeDtypeStruct((r, w), F32),
                          compiler_params=_cparams("parallel"))(chip, s1, recv, recv, recv)


def _silu_rows(x, name):
    def body(x_ref, o_ref):
        o_ref[...] = _silu(x_ref[...])

    return pl.pallas_call(body, name=name, out_shape=jax.ShapeDtypeStruct(x.shape, F32))(x)


def _row_sum(x, name):
    def body(x_ref, o_ref):
        acc = x_ref[0:1, :]
        for i in range(1, x.shape[0]):
            acc = acc + x_ref[i:i + 1, :]
        o_ref[...] = acc

    return pl.pallas_call(body, name=name, out_shape=jax.ShapeDtypeStruct((1, x.shape[1]), F32))(x)


def _adamw(w, g, m, v, name):
    cols = w.shape[-1]
    rows = w.size // cols
    tr = _tile(rows, (128,))
    tc = LANE if (tr == rows and rows > 512 and cols % LANE == 0) else cols

    def body(w_ref, g_ref, m_ref, v_ref, d_ref, mo_ref, vo_ref):
        grad = g_ref[...]
        m_new = ADAM_B1 * m_ref[...] + (1.0 - ADAM_B1) * grad
        v_new = ADAM_B2 * v_ref[...] + (1.0 - ADAM_B2) * jnp.square(grad)
        m_hat = m_new / (1.0 - ADAM_B1 ** ADAM_STEP)
        v_hat = v_new / (1.0 - ADAM_B2 ** ADAM_STEP)
        d_ref[...] = -ADAM_LR * (m_hat / (jnp.sqrt(v_hat) + ADAM_EPS) + ADAM_WD * w_ref[...])
        mo_ref[...] = m_new
        vo_ref[...] = v_new

    blk = pl.BlockSpec((tr, tc), lambda i, j: (i, j))
    out = pl.pallas_call(
        body, name=name, grid=(rows // tr, cols // tc), in_specs=[blk] * 4, out_specs=[blk] * 3,
        out_shape=[jax.ShapeDtypeStruct((rows, cols), F32)] * 3, compiler_params=_cparams("parallel", "parallel"),
    )(*[t.reshape(rows, cols) for t in (w, g, m, v)])
    return [t.reshape(w.shape) for t in out]


def _pack(parts, width, row_mult, dtype):
    flat = jnp.concatenate([p.reshape(-1).astype(dtype) for p in parts])
    rows = -(-flat.shape[0] // (width * row_mult)) * row_mult
    return jnp.pad(flat, (0, rows * width - flat.shape[0])).reshape(rows, width)


def _unpack(flat, shapes):
    out, off = [], 0
    for shp in shapes:
        size = 1
        for dim in shp:
            size *= dim
        out.append(flat[:, off:off + size].reshape((flat.shape[0],) + tuple(shp)))
        off += size
    return out


def _devices_to_cols(a):
    _, r, c = a.shape
    return a.transpose(1, 0, 2).reshape(r, NDEV * c)


def kernel(x, c, w_ada, b_ada, norm1_w, w_in, gdn_conv_w, gdn_a_log, gdn_dt_bias, gdn_norm_w, w_gdn_proj, sc_conv_w, w_sc_out, w_o, norm2_w, w_ffn_in, w_ffn_out, w_ada_f, b_ada_f, normf_w, loss_target, m_w_ada, m_b_ada, m_norm1_w, m_w_in, m_gdn_conv_w, m_gdn_a_log, m_gdn_dt_bias, m_gdn_norm_w, m_w_gdn_proj, m_sc_conv_w, m_w_sc_out, m_w_o, m_norm2_w, m_w_ffn_in, m_w_ffn_out, m_w_ada_f, m_b_ada_f, m_normf_w, v_w_ada, v_b_ada, v_norm1_w, v_w_in, v_gdn_conv_w, v_gdn_a_log, v_gdn_dt_bias, v_gdn_norm_w, v_w_gdn_proj, v_sc_conv_w, v_w_sc_out, v_w_o, v_norm2_w, v_w_ffn_in, v_w_ffn_out, v_w_ada_f, v_b_ada_f, v_normf_w):
    bl, s, d = x.shape
    heads = gdn_a_log.shape[-1]
    dff = w_ffn_out.shape[1] * NDEV
    tok = bl * s
    ax, ay, ac = _position()
    dev = 4 * ax + 2 * ay + ac
    as_tok = lambda a: a.reshape(bl, s, a.shape[-1])
    as_mat = lambda a: a.reshape(tok, a.shape[-1])

    small = _all_gather(_pack([c, gdn_conv_w, sc_conv_w], LANE, 8, F32), name="gather_cond", hbm=False)
    c_all, conv_w, sc_w = _unpack(small.reshape(NDEV, -1), [(bl, d), gdn_conv_w.shape[1:], sc_conv_w.shape[1:]])
    c_act = _silu_rows(c_all.reshape(NDEV * bl, d), "cond_silu")
    conv_w, sc_w = _devices_to_cols(conv_w), _devices_to_cols(sc_w)
    n_ada, n_adaf = w_ada.shape[-1], w_ada_f.shape[-1]
    bias = jnp.broadcast_to(lax.dynamic_slice_in_dim(b_ada, dev * n_ada, n_ada, axis=1), (NDEV * bl, n_ada))
    biasf = jnp.broadcast_to(lax.dynamic_slice_in_dim(b_ada_f.reshape(1, -1), dev * n_adaf, n_adaf, axis=1), (NDEV * bl, n_adaf))
    mod_cols = _mm(c_act, w_ada[0], add=bias, name="ada_cols")
    modf_cols = _mm(c_act, w_ada_f, add=biasf, name="adaf_cols")
    mods = _all_gather(jnp.concatenate([mod_cols, modf_cols], axis=1), name="gather_mod", hbm=False)
    mod_all = mods[:, :, :n_ada].transpose(1, 0, 2).reshape(NDEV * bl, NDEV * n_ada)
    modf_all = mods[:, :, n_ada:].transpose(1, 0, 2).reshape(NDEV * bl, NDEV * n_adaf)
    my_rows = lambda a: lax.dynamic_slice_in_dim(a, dev * bl, bl, axis=0)
    sh1, sc1, g1, sh2, sc2, g2 = [t.reshape(bl, 1, d) for t in jnp.split(my_rows(mod_all), 6, axis=1)]
    shf, scf = [t.reshape(bl, 1, d) for t in jnp.split(my_rows(modf_all), 2, axis=1)]

    late = [t.astype(MXU_DTYPE) for t in (w_gdn_proj[0], w_sc_out[0], w_o[0], w_ffn_in[0].T, w_ffn_out[0])]
    rows = [t.shape[0] for t in late] + [w_in.shape[-1]]
    offs = [sum(rows[:i]) for i in range(5)]
    in_rows = -(-rows[5] // ROW_ALIGN) * ROW_ALIGN
    in_send = jnp.pad(w_in[0].T.astype(MXU_DTYPE), ((0, in_rows - rows[5]), (0, 0)))
    with_own = lambda g, own: lax.dynamic_update_slice_in_dim(g, own[None], dev, axis=0)
    (wt_in,) = _gather_async([in_send], "gather_w_in", 1)
    wt_in = with_own(wt_in, in_send)[:, :rows[5], :].reshape(NDEV * rows[5], d)
    gathered = _gather_async(late[:3], "gather_mixer", 2) + _gather_async(late[3:], "gather_ffn", 3)
    wgp, wso, wo, wt_fi, wfo = [with_own(g, own).reshape(NDEV * own.shape[0], d) for g, own in zip(gathered, late)]
    o_z, o_ab, o_sc, o_ga, o_gb = 3 * d, 4 * d, 4 * d + 2 * heads, 7 * d + 2 * heads, 8 * d + 2 * heads
    s_qkv, s_z, s_sc, s_gate = (0, o_z), (o_z, d), (o_sc, 3 * d), (o_ga, 2 * d)
    wt_ab = jnp.pad(wt_in[o_ab:o_sc], ((0, LANE - 2 * heads), (0, 0)))

    n1w, n2w, nfw = norm1_w.reshape(1, d), norm2_w.reshape(1, d), normf_w.reshape(1, d)
    lanes = lambda a: jnp.pad(a.reshape(1, -1), ((0, 0), (0, LANE - a.size)))
    a_log, dt_bias, gnw = lanes(gdn_a_log), lanes(gdn_dt_bias), gdn_norm_w.reshape(1, HEAD)
    f_gates = functools.partial(_f_gates, heads=heads)
    (h1,) = _tok_fwd(_f_norm_mod, [x], [sh1, sc1], [n1w], [(d, MXU_DTYPE)], name="norm1", ts=512)
    h1m = as_mat(h1)
    p_qkv = as_tok(_mm(h1m, wt_in, tb=True, b_rows=s_qkv, name="in_qkv"))
    p_z = as_tok(_mm(h1m, wt_in, tb=True, b_rows=s_z, name="in_z"))
    p_ab = as_tok(_mm(h1m, wt_ab, tb=True, name="in_ab"))
    p_sc = as_tok(_mm(h1m, wt_in, tb=True, b_rows=s_sc, name="in_sc"))
    p_g = as_tok(_mm(h1m, wt_in, tb=True, b_rows=s_gate, name="in_gate"))
    qkv = _qkv_fwd(p_qkv, conv_w, heads, "qkv_conv")
    (gbeta,) = _tok_fwd(f_gates, [p_ab], [], [a_log, dt_bias], [(LANE, F32)], name="gates", ts=512)
    o, s_all, t_all = _gdn_fwd(qkv, gbeta, heads, "gdn")
    (og,) = _tok_fwd(_f_gdn_out, [o, p_z], [], [(gnw, None)], [(d, MXU_DTYPE)], name="gdn_out", ts=2048, wb=HEAD, cols=heads)
    y_a = as_tok(_mm(as_mat(og), wgp, name="gdn_proj"))
    scp = _sc_fwd(p_sc, sc_w, "sc_conv")
    mrg, y_b = _tok_fwd(_f_merge_keep, [(p_g, 0), (p_g, 1), y_a, _Product(scp, wso)], [], [], [(d, MXU_DTYPE), (d, F32)],
                        name="merge", ts=256, wb=d)
    merge_toks = [(p_g, 0), (p_g, 1), y_a, y_b]
    x2, h2, mix = _tok_fwd(_f_res_norm_mod_keep, [x, _Product(mrg, wo)], [g1, sh2, sc2], [n2w],
                           [(d, F32), (d, MXU_DTYPE), (d, F32)], name="norm2", ts=512)
    act, gu_a, gu_b = _ffn_in_swiglu(as_mat(h2), wt_fi, dff, "ffn_in")

    loss_l, (dx2, dff_out, _), (dg2, dshf, dscf), (dnfw,) = _tok_bwd(
        _f_loss, [x2, _Product(as_tok(act), wfo), loss_target], [g2, shf, scf], [nfw], [], [True, True, False], name="loss",
        ts=256, loss=True, tok_dtype=[F32, MXU_DTYPE, None])
    dffm = as_mat(dff_out)
    dgu_a, dgu_b = _ffn_out_bwd_swiglu(dffm, wfo, gu_a, gu_b, "d_ffn_out")
    gmm = functools.partial(_mm, ta=True, out_dtype=MXU_DTYPE)
    gw_ffn_out = gmm(act, dffm, name="g_ffn_out")
    dh2 = _Product(as_tok(dgu_b), wt_fi, b_rows=(dff, dff), add=as_tok(_mm(dgu_a, wt_fi, b_rows=(0, dff), name="d_ffn_in_a")))
    h2m = as_mat(h2)
    gwt_ffn_in = gmm(dgu_a, h2m, out_rows=2 * dff, name="g_ffn_in_a")
    gwt_ffn_in = gmm(dgu_b, h2m, out_rows=2 * dff, row_off=dff, into=gwt_ffn_in, name="g_ffn_in_b")
    ffn_parts = [(gwt_ffn_in, rows[3]), (gw_ffn_out, rows[4])]
    ffn_recv = _scatter_async(ffn_parts, "scatter_ffn", 4)
    (dx_skip, dmix), (dg1, dsh2, dsc2), (dn2w,) = _tok_bwd(
        _f_res_norm_mod, [x, mix], [g1, sh2, sc2], [n2w], [dx2, dh2], [True, True], name="d_norm2", ts=256,
        tok_dtype=[F32, MXU_DTYPE], after=[gwt_ffn_in, gw_ffn_out])
    gw_o = gmm(as_mat(mrg), as_mat(dmix), name="g_mix_out")
    (dga, dgb, dya, dyb), _, _ = _tok_bwd(_f_merge, merge_toks, [], [], [_Product(dmix, wo, tb=True)], [True] * 4,
                                          name="d_merge", ts=256, wb=d, tok_dtype=MXU_DTYPE)
    dyam, dybm = as_mat(dya), as_mat(dyb)
    dog = as_tok(_mm(dyam, wgp, tb=True, name="d_gdn_proj"))
    gw_gdn_proj = gmm(as_mat(og), dyam, name="g_gdn_proj")
    dscp = as_tok(_mm(dybm, wso, tb=True, name="d_sc_out"))
    gw_sc_out = gmm(as_mat(scp), dybm, name="g_sc_out")
    dscb, dscc, dscx, g_sc_w = _sc_bwd(p_sc, sc_w, dscp, "d_sc_conv")
    mix_parts = [(gw_gdn_proj, rows[0]), (gw_sc_out, rows[1]), (gw_o, rows[2])]
    mix_recv = _scatter_async(mix_parts, "scatter_mixer", 5)
    (do, dz), _, (g_gnw,) = _tok_bwd(_f_gdn_out, [o, p_z], [], [(gnw, None)], [dog], [True, True], name="d_gdn_out",
                                     ts=2048, wb=HEAD, cols=heads, tok_dtype=[F32, MXU_DTYPE],
                                     after=[gw_gdn_proj, gw_sc_out, gw_o])
    own_rows = lambda parts: jnp.concatenate([lax.dynamic_slice_in_dim(g, dev * r, r, axis=0) for g, r in parts], axis=0)
    dqkv, dgbeta = _gdn_bwd(qkv, gbeta, do, s_all, t_all, heads, "d_gdn")
    dp_qkv, g_conv_w = _qkv_bwd(p_qkv, conv_w, dqkv, heads, "d_qkv_conv")
    ffn_red = _sum_direct(own_rows(ffn_parts), ffn_recv, "sum_ffn")
    mix_red = _sum_direct(own_rows(mix_parts), mix_recv, "sum_mix")
    (dp_ab,), _, (g_a_log, g_dt_bias) = _tok_bwd(f_gates, [p_ab], [], [a_log, dt_bias], [dgbeta], [True], name="d_gates",
                                                 ts=512, tok_dtype=MXU_DTYPE, after=[ffn_red, mix_red])
    sections = [(dp_qkv, s_qkv), (dz, s_z), (dp_ab, None), (dscb, (o_sc, d)), (dscc, (o_sc + d, d)), (dscx, (o_sc + 2 * d, d)),
                (dga, (o_ga, d)), (dgb, (o_gb, d))]
    gwt_in = [gmm(as_mat(dp), h1m, name=f"g_in_{k}") for k, (dp, _) in enumerate(sections)]
    gwt_in[2] = gwt_in[2][:2 * heads]

    r_in = rows[5]
    win = -(-(r_in + max(r_in * k % ROW_ALIGN for k in range(NDEV))) // 128) * 128
    need_rows = max(_window_start(r_in, k) for k in range(NDEV)) + win
    gwt_in = jnp.concatenate(gwt_in + [jnp.zeros((need_rows - NDEV * r_in, d), MXU_DTYPE)], axis=0)
    recv1 = _exchange_in_chip([(gwt_in, r_in, win, 0)], "scatter_in_chip", 7)
    own = jnp.stack([lax.dynamic_slice_in_dim(gwt_in, _window_start(r_in, 2 * q + ac), win, axis=0) for q in range(4)])
    s1 = _sum_in_chip(own, recv1, "sum_in_chip")
    recv2 = _exchange_chips_async(s1, "scatter_chips", 6)

    dh1 = None
    for k, (dp, sec) in enumerate(sections[:-1]):
        dh1 = _mm(as_mat(dp), wt_ab if sec is None else wt_in, b_rows=sec, add=dh1, name=f"d_in_{k}")
    dh1 = _Product(sections[-1][0], wt_in, b_rows=sections[-1][1], add=as_tok(dh1))
    (grad_x,), (dsh1, dsc1), (dn1w,) = _tok_bwd(_f_norm_mod_skip, [x], [sh1, sc1], [n1w], [dh1, dx_skip], [True],
                                                name="d_norm1", ts=256)
    reduced = _sum_chips(s1, recv2, (2 * ax + ay).reshape(1).astype(jnp.int32), "sum_chips")
    gt_w_in = lax.dynamic_slice_in_dim(reduced, r_in * dev - _window_start(r_in, dev), r_in, axis=0)
    g_w_in = gt_w_in.T.reshape(w_in.shape)
    gt_w_ffn_in = ffn_red[:rows[3]]
    g_w_ffn_in = gt_w_ffn_in.T.reshape(w_ffn_in.shape)
    g_w_ffn_out = ffn_red[rows[3]:].reshape(w_ffn_out.shape)
    g_w_gdn_proj, g_w_sc_out, g_w_o = (mix_red[offs[i]:offs[i] + rows[i]].reshape(ref.shape)
                                       for i, ref in enumerate((w_gdn_proj, w_sc_out, w_o)))

    dmod = jnp.concatenate([t.reshape(bl, d) for t in (dsh1, dsc1, dg1, dsh2, dsc2, dg2)], axis=1)
    dmodf = jnp.concatenate([t.reshape(bl, d) for t in (dshf, dscf)], axis=1)
    summed_parts = [dn1w, dn2w, dnfw, g_gnw, g_a_log, g_dt_bias, g_conv_w, g_sc_w, loss_l]
    partial = _all_gather(_pack([dmod, dmodf] + summed_parts, LANE, 8, F32), name="gather_small", hbm=False)
    partial = partial.reshape(NDEV, -1)
    n_rows = bl * (6 * d + 2 * d)
    dmod_all, dmodf_all = _unpack(partial[:, :n_rows], [(bl, 6 * d), (bl, 2 * d)])
    dmod_all, dmodf_all = dmod_all.reshape(NDEV * bl, 6 * d), dmodf_all.reshape(NDEV * bl, 2 * d)
    totals = _row_sum(partial[:, n_rows:], "sum_small")
    t_n1w, t_n2w, t_nfw, t_gnw, t_a_log, t_dt_bias, t_conv_w, t_sc_w, t_loss = [
        t[0] for t in _unpack(totals, [p.shape for p in summed_parts])]
    my_cols = lambda a, n: lax.dynamic_slice_in_dim(a, dev * n, n, axis=1)
    grads = {
        "w_ada": _mm(c_act, my_cols(dmod_all, n_ada), ta=True, name="g_ada").reshape(w_ada.shape),
        "b_ada": _row_sum(dmod_all, "g_ada_bias").reshape(b_ada.shape),
        "norm1_w": t_n1w.reshape(norm1_w.shape),
        "w_in": g_w_in,
        "gdn_conv_w": my_cols(t_conv_w, gdn_conv_w.shape[-1]).reshape(gdn_conv_w.shape),
        "gdn_a_log": t_a_log[:, :heads].reshape(gdn_a_log.shape),
        "gdn_dt_bias": t_dt_bias[:, :heads].reshape(gdn_dt_bias.shape),
        "gdn_norm_w": t_gnw.reshape(gdn_norm_w.shape),
        "w_gdn_proj": g_w_gdn_proj,
        "sc_conv_w": my_cols(t_sc_w, sc_conv_w.shape[-1]).reshape(sc_conv_w.shape),
        "w_sc_out": g_w_sc_out,
        "w_o": g_w_o,
        "norm2_w": t_n2w.reshape(norm2_w.shape),
        "w_ffn_in": g_w_ffn_in,
        "w_ffn_out": g_w_ffn_out,
        "w_ada_f": _mm(c_act, my_cols(dmodf_all, n_adaf), ta=True, name="g_adaf").reshape(w_ada_f.shape),
        "b_ada_f": _row_sum(dmodf_all, "g_adaf_bias").reshape(b_ada_f.shape),
        "normf_w": t_nfw.reshape(normf_w.shape),
    }
    weights = dict(w_ada=w_ada, b_ada=b_ada, norm1_w=norm1_w, w_in=w_in, gdn_conv_w=gdn_conv_w, gdn_a_log=gdn_a_log,
                   gdn_dt_bias=gdn_dt_bias, gdn_norm_w=gdn_norm_w, w_gdn_proj=w_gdn_proj, sc_conv_w=sc_conv_w,
                   w_sc_out=w_sc_out, w_o=w_o, norm2_w=norm2_w, w_ffn_in=w_ffn_in, w_ffn_out=w_ffn_out, w_ada_f=w_ada_f,
                   b_ada_f=b_ada_f, normf_w=normf_w)
    m_in = [m_w_ada, m_b_ada, m_norm1_w, m_w_in, m_gdn_conv_w, m_gdn_a_log, m_gdn_dt_bias, m_gdn_norm_w, m_w_gdn_proj,
            m_sc_conv_w, m_w_sc_out, m_w_o, m_norm2_w, m_w_ffn_in, m_w_ffn_out, m_w_ada_f, m_b_ada_f, m_normf_w]
    v_in = [v_w_ada, v_b_ada, v_norm1_w, v_w_in, v_gdn_conv_w, v_gdn_a_log, v_gdn_dt_bias, v_gdn_norm_w, v_w_gdn_proj,
            v_sc_conv_w, v_w_sc_out, v_w_o, v_norm2_w, v_w_ffn_in, v_w_ffn_out, v_w_ada_f, v_b_ada_f, v_normf_w]
    deltas, new_m, new_v = [], [], []
    grads_t = {"w_in": gt_w_in, "w_ffn_in": gt_w_ffn_in}
    for (wname, wt), mt, vt in zip(weights.items(), m_in, v_in):
        if wname in grads_t:
            back = lambda a, wt=wt: a.T.reshape(wt.shape)
            dl, mn, vn = (back(a) for a in _adamw(wt[0].T, grads_t[wname], mt[0].T, vt[0].T, "adamw_" + wname))
        else:
            dl, mn, vn = _adamw(wt, grads[wname], mt, vt, "adamw_" + wname)
        deltas.append(dl)
        new_m.append(mn)
        new_v.append(vn)
    loss = t_loss[0, 0]
    return (loss, grad_x, *[grads[k] for k in weights], *deltas, *new_m, *new_v)
```

```python
import functools

import jax
import jax.numpy as jnp
from jax import lax
from jax.experimental import pallas as pl
from jax.experimental.pallas import tpu as pltpu
from jax.experimental.pallas import tpu_sc as plsc

F32 = jnp.float32
MXU_DTYPE = jnp.bfloat16
NDEV = 8
CHUNK = 64
HEAD = 128
LANE = 128
EPS = 1e-6
ADAM_LR, ADAM_B1, ADAM_B2, ADAM_EPS, ADAM_WD, ADAM_STEP = 0.001, 0.9, 0.999, 1e-08, 0.01, 10
VMEM_LIMIT = 48 * 1024 * 1024
MESH_IDS = pl.DeviceIdType.MESH
HIGHEST = lax.Precision.HIGHEST


def _tile(n, cands=(512, 256, 128)):
    for c in cands:
        if n % c == 0:
            return c
    return n


def _cparams(*sem):
    return pltpu.CompilerParams(dimension_semantics=sem, vmem_limit_bytes=VMEM_LIMIT)


def _mm(a, b, *, ta=False, tb=False, add=None, out_dtype=F32, name, b_rows=None, out_rows=None, row_off=0, into=None):
    m, k = (a.shape[1], a.shape[0]) if ta else a.shape
    b_shape = b.shape if b_rows is None else (b_rows[1], b.shape[1])
    n = b_shape[0] if tb else b_shape[1]
    assert k == (b_shape[1] if tb else b_shape[0])
    if ta:
        tm, tn = _tile(m), n if n <= 1024 else _tile(n)
        tk = k if k <= 4096 else _tile(k, (4096, 2048, 1024, 512))
        if tm * tk > 1024 * 2048:
            tk = _tile(k, (2048, 1024, 512))
    else:
        tk = k if k <= 1024 else _tile(k, (1024, 512))
        tn = _tile(n, (1024 if tk <= 1024 else 512, 512, 256, 128))
        tm = _tile(m, (2048 if (tn <= 512 and tk <= 1024) else 1024, 1024, 512, 256, 128))
    nk = k // tk
    dims = (((0 if ta else 1,), (1 if tb else 0,)), ((), ()))
    has_add = add is not None

    def body(*refs):
        a_ref, b_ref = refs[0], refs[1]
        add_ref = refs[2] if has_add else None
        o_ref = refs[2 + has_add + (into is not None)]
        part = lax.dot_general(a_ref[...].astype(MXU_DTYPE), b_ref[...].astype(MXU_DTYPE), dims,
                               preferred_element_type=F32)

        def finish(acc):
            if has_add:
                acc = acc + add_ref[...]
            o_ref[...] = acc.astype(o_ref.dtype)

        if nk == 1:
            finish(part)
        else:
            acc_ref = refs[-1]
            kk = pl.program_id(2)

            @pl.when(kk == 0)
            def _():
                acc_ref[...] = part

            @pl.when(kk > 0)
            def _():
                acc_ref[...] += part

            @pl.when(kk == nk - 1)
            def _():
                finish(acc_ref[...])

    a_spec = pl.BlockSpec((tk, tm), lambda i, j, kk: (kk, i)) if ta else pl.BlockSpec((tm, tk), lambda i, j, kk: (i, kk))
    if b_rows is None:
        b_spec = pl.BlockSpec((tn, tk), lambda i, j, kk: (j, kk)) if tb else pl.BlockSpec((tk, tn), lambda i, j, kk: (kk, j))
    else:
        at = lambda t: pl.multiple_of(b_rows[0] + t, ROW_ALIGN)
        b_spec = (pl.BlockSpec((pl.Element(tn), pl.Element(tk)), lambda i, j, kk: (at(j * tn), kk * tk)) if tb else
                  pl.BlockSpec((pl.Element(tk), pl.Element(tn)), lambda i, j, kk: (at(kk * tk), j * tn)))
    add_spec = pl.BlockSpec((tm, tn), lambda i, j, kk: (i, j))
    assert row_off % tm == 0
    o_spec = pl.BlockSpec((tm, tn), lambda i, j, kk: (i + row_off // tm, j))
    in_specs = [a_spec, b_spec] + ([add_spec] if has_add else []) + ([pl.BlockSpec(memory_space=pl.ANY)] if into is not None else [])
    args = [a, b] + ([add] if has_add else []) + ([into] if into is not None else [])
    return pl.pallas_call(
        body, name=name, grid=(m // tm, n // tn, nk), in_specs=in_specs, out_specs=o_spec,
        out_shape=jax.ShapeDtypeStruct((out_rows or m, n), out_dtype),
        scratch_shapes=[pltpu.VMEM((tm, tn), F32)] if nk > 1 else [],
        input_output_aliases={len(args) - 1: 0} if into is not None else {},
        compiler_params=_cparams("parallel", "parallel", "arbitrary"),
    )(*args)


def _mm_chain(parts, b, row_of_tile, *, add, name, tk=1024, tm=512):
    m, n = parts[0].shape[0], b.shape[1]
    tiles = [p.shape[1] // tk for p in parts]
    first = [sum(tiles[:s]) for s in range(len(parts))]
    nk = sum(tiles)

    def body(*refs):
        a_refs, b_ref, add_ref, o_ref, acc_ref = refs[:len(parts)], *refs[len(parts):]
        kk = pl.program_id(1)

        @pl.when(kk == 0)
        def _():
            acc_ref[...] = add_ref[...]

        for a_ref, lo, cnt in zip(a_refs, first, tiles):
            @pl.when(jnp.logical_and(kk >= lo, kk < lo + cnt))
            def _(a_ref=a_ref):
                acc_ref[...] += lax.dot_general(a_ref[...].astype(MXU_DTYPE), b_ref[...].astype(MXU_DTYPE),
                                                (((1,), (0,)), ((), ())), preferred_element_type=F32)

        @pl.when(kk == nk - 1)
        def _():
            o_ref[...] = acc_ref[...]

    a_specs = [pl.BlockSpec((tm, tk), functools.partial(lambda i, kk, lo, cnt: (i, jnp.clip(kk - lo, 0, cnt - 1)), lo=lo, cnt=cnt))
               for lo, cnt in zip(first, tiles)]
    b_spec = pl.BlockSpec((pl.Element(tk), pl.Element(n)), lambda i, kk: (pl.multiple_of(row_of_tile(kk), ROW_ALIGN), 0))
    o_spec = pl.BlockSpec((tm, n), lambda i, kk: (i, 0))
    return pl.pallas_call(
        body, name=name, grid=(m // tm, nk), in_specs=a_specs + [b_spec, o_spec], out_specs=o_spec,
        out_shape=jax.ShapeDtypeStruct((m, n), F32), scratch_shapes=[pltpu.VMEM((tm, n), F32)],
        compiler_params=_cparams("parallel", "arbitrary"),
    )(*parts, b, add)


def _swiglu_tiles(m, half):
    tn = _tile(half, (512, 256, 128))
    return _tile(m, (2048 if tn <= 256 else 1024, 1024, 512, 256, 128)), tn


def _ffn_in_swiglu(h, wt, half, name):
    m, k = h.shape
    tm, tn = _swiglu_tiles(m, half)
    nj = half // tn
    dims = (((1,), (1,)), ((), ()))

    def body(h_ref, wa_ref, wb_ref, act_ref, a_ref, b_ref):
        lhs = h_ref[...].astype(MXU_DTYPE)
        a = lax.dot_general(lhs, wa_ref[...].astype(MXU_DTYPE), dims, preferred_element_type=F32)
        b = lax.dot_general(lhs, wb_ref[...].astype(MXU_DTYPE), dims, preferred_element_type=F32)
        act_ref[...] = (_silu(a) * b).astype(act_ref.dtype)
        a_ref[...] = a.astype(a_ref.dtype)
        b_ref[...] = b.astype(b_ref.dtype)

    out = jax.ShapeDtypeStruct((m, half), MXU_DTYPE)
    oblk = pl.BlockSpec((tm, tn), lambda i, j: (i, j))
    return pl.pallas_call(
        body, name=name, grid=(m // tm, nj),
        in_specs=[pl.BlockSpec((tm, k), lambda i, j: (i, 0)), pl.BlockSpec((tn, k), lambda i, j: (j, 0)),
                  pl.BlockSpec((tn, k), lambda i, j: (j + nj, 0))],
        out_specs=[oblk, oblk, oblk], out_shape=[out, out, out], compiler_params=_cparams("parallel", "parallel"),
    )(h, wt, wt)


def _ffn_out_bwd_swiglu(dff, w, a, b, name):
    m, k = dff.shape
    half = w.shape[0]
    tm, tn = _swiglu_tiles(m, half)

    def body(d_ref, w_ref, a_ref, b_ref, da_ref, db_ref):
        dact = lax.dot_general(d_ref[...].astype(MXU_DTYPE), w_ref[...].astype(MXU_DTYPE), (((1,), (1,)), ((), ())),
                               preferred_element_type=F32)
        av, bv = a_ref[...].astype(F32), b_ref[...].astype(F32)
        sig = jax.nn.sigmoid(av)
        da_ref[...] = (dact * bv * (sig * (1.0 + av * (1.0 - sig)))).astype(da_ref.dtype)
        db_ref[...] = (dact * (av * sig)).astype(db_ref.dtype)

    out = jax.ShapeDtypeStruct((m, half), MXU_DTYPE)
    oblk = pl.BlockSpec((tm, tn), lambda i, j: (i, j))
    return pl.pallas_call(
        body, name=name, grid=(m // tm, half // tn),
        in_specs=[pl.BlockSpec((tm, k), lambda i, j: (i, 0)), pl.BlockSpec((tn, k), lambda i, j: (j, 0)), oblk, oblk],
        out_specs=[oblk, oblk], out_shape=[out, out], compiler_params=_cparams("parallel", "parallel"),
    )(dff, w, a, b)


def _with_off(xs):
    return [x if isinstance(x, tuple) else (x, 0) for x in xs]


def _spec(kind, arr, off, ts, wb):
    w = arr.shape[-1] if wb is None else wb
    col = (lambda j: 0) if wb is None else functools.partial(lambda j, o: o + j, o=off)
    if kind == "tok":
        return pl.BlockSpec((None, ts, w), lambda j, b, i: (b, i, col(j)))
    if kind == "bat":
        return pl.BlockSpec((None, 1, w), lambda j, b, i: (b, 0, col(j)))
    if off is None:
        return pl.BlockSpec(arr.shape, lambda j, b, i: (0, 0))
    return pl.BlockSpec((arr.shape[0], w), lambda j, b, i: (0, col(j)))


class _Product:
    def __init__(self, a, b, *, tb=False, b_rows=None, add=None):
        self.a, self.b, self.tb, self.b_rows, self.add = a, b, tb, b_rows, add
        rows = b.shape[0] if b_rows is None else b_rows[1]
        self.shape = a.shape[:2] + (rows if tb else b.shape[1],)

    def inputs(self, ts):
        a_spec = pl.BlockSpec((None, ts, self.a.shape[2]), lambda j, b, i: (b, i, 0))
        if self.b_rows is None:
            b_spec = pl.BlockSpec(self.b.shape, lambda j, b, i: (0, 0))
        else:
            start, count = self.b_rows
            b_spec = pl.BlockSpec((pl.Element(count), pl.Element(self.b.shape[1])), lambda j, b, i: (start, 0))
        extra = [] if self.add is None else [(self.add, pl.BlockSpec((None, ts, self.shape[2]), lambda j, b, i: (b, i, 0)))]
        return [(self.a, a_spec), (self.b, b_spec)] + extra

    def value(self, refs):
        dims = (((1,), (1 if self.tb else 0,)), ((), ()))
        val = lax.dot_general(refs[0][...].astype(MXU_DTYPE), refs[1][...].astype(MXU_DTYPE), dims, preferred_element_type=F32)
        return val if self.add is None else val + refs[2][...].astype(F32)


def _inputs(groups, kinds, ts, wb):
    loaded = [(a, _spec(kind, a, o, ts, wb)) for g, kind in zip(groups, kinds) for a, o in g if not isinstance(a, _Product)]
    made = [pair for g in groups for a, _ in g if isinstance(a, _Product) for pair in a.inputs(ts)]
    return [a for a, _ in loaded + made], [sp for _, sp in loaded + made]


def _values(refs, groups):
    n_loaded = sum(1 for g in groups for a, _ in g if not isinstance(a, _Product))
    loaded, pos, out = iter(refs[:n_loaded]), n_loaded, []
    for g in groups:
        vals = []
        for a, _ in g:
            if isinstance(a, _Product):
                k = 2 if a.add is None else 3
                vals.append(a.value(refs[pos:pos + k]))
                pos += k
            else:
                vals.append(next(loaded)[...].astype(F32))
        out.append(vals)
    return out, pos


def _tok_fwd(fn, toks, bats, pars, outs, *, name, ts, wb=None, cols=1):
    groups = [_with_off(toks), _with_off(bats), _with_off(pars)]
    bl, s, _ = groups[0][0][0].shape
    ts = min(ts, s)
    args, in_specs = _inputs(groups, ("tok", "bat", "par"), ts, wb)

    def body(*refs):
        vals, n_in = _values(refs, groups)
        res = fn(*[v for g in vals for v in g])
        for r, val in zip(refs[n_in:], res):
            r[...] = val.astype(r.dtype)

    out_specs = [pl.BlockSpec((None, ts, w if wb is None else wb), lambda j, b, i: (b, i, j)) for w, _ in outs]
    return pl.pallas_call(
        body, name=name, grid=(cols, bl, s // ts), in_specs=in_specs,
        out_specs=out_specs, out_shape=[jax.ShapeDtypeStruct((bl, s, w), dt) for w, dt in outs],
        compiler_params=_cparams("parallel", "parallel", "parallel"),
    )(*args)


def _accumulate(ref, val, first):
    @pl.when(first)
    def _():
        ref[...] = val

    @pl.when(jnp.logical_not(first))
    def _():
        ref[...] += val


def _tok_bwd(fn, toks, bats, pars, cots, need, *, name, ts, wb=None, cols=1, tok_dtype=F32, loss=False, after=()):
    toks, bats, pars, cots = _with_off(toks), _with_off(bats), _with_off(pars), _with_off(cots)
    groups = [toks, bats, pars, cots]
    bl, s, _ = toks[0][0].shape
    ts = min(ts, s)
    nt, nb, npar = len(toks), len(bats), len(pars)
    args, in_specs = _inputs(groups, ("tok", "bat", "par", "tok"), ts, wb)
    args, in_specs = args + list(after), in_specs + [pl.BlockSpec(memory_space=pl.ANY)] * len(after)

    def body(*refs):
        j, b, i = pl.program_id(0), pl.program_id(1), pl.program_id(2)
        (tok_vals, bat_vals, par_vals, cot_vals), o = _values(refs, groups)
        o += len(after)
        outs, vjp = jax.vjp(fn, *tok_vals, *bat_vals, *par_vals)
        if loss:
            ct = (jnp.ones_like(outs[0]),)
            tot = jnp.broadcast_to(jnp.sum(outs[0], keepdims=True), (1, LANE))
            _accumulate(refs[o], tot, jnp.logical_and(b == 0, i == 0))
            o += 1
        else:
            ct = tuple(cot_vals)
        grads = vjp(ct)
        for t in range(nt):
            if need[t]:
                refs[o][...] = grads[t].astype(refs[o].dtype)
                o += 1
        for t in range(nb):
            _accumulate(refs[o], grads[nt + t], i == 0)
            o += 1
        for t in range(npar):
            first = jnp.logical_and(b == 0, i == 0)
            if pars[t][1] is None:
                first = jnp.logical_and(first, j == 0)
            _accumulate(refs[o], grads[nt + nb + t], first)
            o += 1

    full = lambda arr: arr.shape[-1] if wb is None else wb * cols
    blk = lambda arr: arr.shape[-1] if wb is None else wb
    out_specs, out_shape = [], []
    if loss:
        out_specs.append(pl.BlockSpec((1, LANE), lambda j, b, i: (0, 0)))
        out_shape.append(jax.ShapeDtypeStruct((1, LANE), F32))
    for t in range(nt):
        if need[t]:
            out_specs.append(pl.BlockSpec((None, ts, blk(toks[t][0])), lambda j, b, i: (b, i, j)))
            dt = tok_dtype[t] if isinstance(tok_dtype, (list, tuple)) else tok_dtype
            out_shape.append(jax.ShapeDtypeStruct((bl, s, full(toks[t][0])), dt))
    for arr, _ in bats:
        out_specs.append(pl.BlockSpec((None, 1, blk(arr)), lambda j, b, i: (b, 0, j)))
        out_shape.append(jax.ShapeDtypeStruct((bl, 1, full(arr)), F32))
    for arr, off in pars:
        if off is None:
            out_specs.append(pl.BlockSpec(arr.shape, lambda j, b, i: (0, 0)))
            out_shape.append(jax.ShapeDtypeStruct(arr.shape, F32))
        else:
            out_specs.append(pl.BlockSpec((arr.shape[0], blk(arr)), lambda j, b, i: (0, j)))
            out_shape.append(jax.ShapeDtypeStruct((arr.shape[0], full(arr)), F32))
    res = list(pl.pallas_call(
        body, name=name, grid=(cols, bl, s // ts), in_specs=in_specs,
        out_specs=out_specs, out_shape=out_shape, compiler_params=_cparams("arbitrary", "arbitrary", "arbitrary"),
    )(*args))
    tot = res.pop(0) if loss else None
    dtoks = [res.pop(0) if need[t] else None for t in range(nt)]
    dbats = [res.pop(0) for _ in range(nb)]
    dpars = [res.pop(0) for _ in range(npar)]
    return (tot, dtoks, dbats, dpars) if loss else (dtoks, dbats, dpars)


def _silu(x):
    return x * jax.nn.sigmoid(x)


def _rms(x, w):
    return x * lax.rsqrt(jnp.mean(x * x, axis=-1, keepdims=True) + EPS) * w


def _f_norm_mod(x, shift, scale, w):
    return (_rms(x, w) * (1.0 + scale) + shift,)


def _f_norm_mod_skip(x, shift, scale, w):
    return _rms(x, w) * (1.0 + scale) + shift, x


def _f_res_norm_mod(x, mix, gate, shift, scale, w):
    x2 = x + gate * mix
    return x2, _rms(x2, w) * (1.0 + scale) + shift


def _f_res_norm_mod_keep(x, mix, gate, shift, scale, w):
    return (*_f_res_norm_mod(x, mix, gate, shift, scale, w), mix)


def _f_gates(p, a_log, dt_bias, *, heads):
    z = p + dt_bias
    g = -jnp.exp(a_log) * (jnp.maximum(z, 0.0) + jnp.log1p(jnp.exp(jnp.minimum(z, -z))))
    lane = lax.broadcasted_iota(jnp.int32, p.shape, 1)
    return (jnp.where(lane < heads, g, jax.nn.sigmoid(p)),)


def _f_gdn_out(o, z, w):
    return (_rms(o, w) * _silu(z),)


def _f_merge(ga, gb, ya, yb):
    return (jax.nn.sigmoid(ga) * ya + jax.nn.sigmoid(gb) * yb,)


def _f_merge_keep(ga, gb, ya, yb):
    return (*_f_merge(ga, gb, ya, yb), yb)


def _f_loss(x2, ff, tgt, gate, shift, scale, w):
    y = _rms(x2 + gate * ff, w) * (1.0 + scale) + shift
    return (0.5 * jnp.mean(jnp.square(y - tgt), axis=-1, keepdims=True),)


def _shift_down(x, s):
    if s == 0:
        return x
    row = lax.broadcasted_iota(jnp.int32, x.shape, 0)
    return jnp.where(row >= s, pltpu.roll(x, s, 0), 0.0)


def _shift_up(x, s):
    if s == 0:
        return x
    n = x.shape[0]
    row = lax.broadcasted_iota(jnp.int32, x.shape, 0)
    return jnp.where(row < n - s, pltpu.roll(x, n - s, 0), 0.0)


def _conv(x, w):
    width = w.shape[0]
    acc = w[width - 1:width, :] * x
    for j in range(width - 1):
        acc = acc + w[j:j + 1, :] * _shift_down(x, width - 1 - j)
    return acc


def _conv_bwd(dy, x, w, dw_ref, first):
    width = w.shape[0]
    dx = w[width - 1:width, :] * dy
    for j in range(width - 1):
        dx = dx + w[j:j + 1, :] * _shift_up(dy, width - 1 - j)
    for j in range(width):
        row = jnp.sum(dy * _shift_down(x, width - 1 - j), axis=0, keepdims=True)
        _accumulate(dw_ref.at[j:j + 1, :], row, first)
    return dx


def _qkv_act(xc, is_v, scale):
    a = _silu(xc)
    nrm = a * lax.rsqrt(jnp.sum(a * a, axis=-1, keepdims=True) + EPS) * scale
    return jnp.where(is_v, a, nrm)


def _qkv_consts(j, heads):
    is_v = j >= 2 * heads
    scale = jnp.where(j < heads, HEAD ** -0.5, 1.0).astype(F32)
    return is_v, scale


def _qkv_fwd(p, w, heads, name):
    bl, s, w3 = p.shape

    def body(p_ref, w_ref, o_ref):
        is_v, scale = _qkv_consts(pl.program_id(0), heads)
        o_ref[...] = _qkv_act(_conv(p_ref[...], w_ref[...]), is_v, scale)

    blk = pl.BlockSpec((None, s, HEAD), lambda j, b: (b, 0, j))
    return pl.pallas_call(
        body, name=name, grid=(w3 // HEAD, bl), in_specs=[blk, pl.BlockSpec((w.shape[0], HEAD), lambda j, b: (0, j))],
        out_specs=blk, out_shape=jax.ShapeDtypeStruct(p.shape, F32), compiler_params=_cparams("parallel", "parallel"),
    )(p, w)


def _qkv_bwd(p, w, dout, heads, name):
    bl, s, w3 = p.shape

    def body(p_ref, w_ref, d_ref, dp_ref, dw_ref):
        is_v, scale = _qkv_consts(pl.program_id(0), heads)
        x, wv = p_ref[...], w_ref[...]
        _, vjp = jax.vjp(lambda xc: _qkv_act(xc, is_v, scale), _conv(x, wv))
        (dxc,) = vjp(d_ref[...])
        dp_ref[...] = _conv_bwd(dxc, x, wv, dw_ref, pl.program_id(1) == 0).astype(dp_ref.dtype)

    blk = pl.BlockSpec((None, s, HEAD), lambda j, b: (b, 0, j))
    wblk = pl.BlockSpec((w.shape[0], HEAD), lambda j, b: (0, j))
    return pl.pallas_call(
        body, name=name, grid=(w3 // HEAD, bl), in_specs=[blk, wblk, blk], out_specs=[blk, wblk],
        out_shape=[jax.ShapeDtypeStruct(p.shape, MXU_DTYPE), jax.ShapeDtypeStruct(w.shape, F32)],
        compiler_params=_cparams("arbitrary", "arbitrary"),
    )(p, w, dout)


def _sc_specs(p, w):
    bl, s, w3 = p.shape
    nblk = w3 // 3 // LANE
    sec = lambda k: pl.BlockSpec((None, s, LANE), functools.partial(lambda j, b, k: (b, 0, k * nblk + j), k=k))
    return nblk, [sec(0), sec(1), sec(2)], pl.BlockSpec((w.shape[0], LANE), lambda j, b: (0, j)), \
        pl.BlockSpec((None, s, LANE), lambda j, b: (b, 0, j))


def _sc_fwd(p, w, name):
    bl, s, w3 = p.shape
    nblk, secs, wblk, oblk = _sc_specs(p, w)

    def body(b_ref, c_ref, x_ref, w_ref, o_ref):
        o_ref[...] = (b_ref[...] * _conv(c_ref[...] * x_ref[...], w_ref[...])).astype(o_ref.dtype)

    return pl.pallas_call(
        body, name=name, grid=(nblk, bl), in_specs=secs + [wblk], out_specs=oblk,
        out_shape=jax.ShapeDtypeStruct((bl, s, w3 // 3), MXU_DTYPE), compiler_params=_cparams("parallel", "parallel"),
    )(p, p, p, w)


def _sc_bwd(p, w, dout, name):
    bl, s, w3 = p.shape
    nblk, secs, wblk, oblk = _sc_specs(p, w)

    def body(b_ref, c_ref, x_ref, w_ref, d_ref, db_ref, dc_ref, dx_ref, dw_ref):
        gb, gc, xin, wv, d = b_ref[...], c_ref[...], x_ref[...], w_ref[...], d_ref[...]
        u = gc * xin
        db_ref[...] = (d * _conv(u, wv)).astype(db_ref.dtype)
        du = _conv_bwd(d * gb, u, wv, dw_ref, pl.program_id(1) == 0)
        dc_ref[...] = (du * xin).astype(dc_ref.dtype)
        dx_ref[...] = (du * gc).astype(dx_ref.dtype)

    act = jax.ShapeDtypeStruct((bl, s, w3 // 3), MXU_DTYPE)
    return pl.pallas_call(
        body, name=name, grid=(nblk, bl), in_specs=secs + [wblk, oblk], out_specs=[oblk, oblk, oblk, wblk],
        out_shape=[act, act, act, jax.ShapeDtypeStruct(w.shape, F32)], compiler_params=_cparams("arbitrary", "arbitrary"),
    )(p, p, p, w, dout)


def _bdot(a, b, ca, cb):
    return lax.dot_general(a.astype(MXU_DTYPE), b.astype(MXU_DTYPE), (((ca,), (cb,)), ((), ())),
                           preferred_element_type=F32)


def _hdot(a, b):
    return lax.dot_general(a, b, (((1,), (0,)), ((), ())), precision=HIGHEST, preferred_element_type=F32)


def _lane_col(x, idx):
    lane = lax.broadcasted_iota(jnp.int32, x.shape, 1)
    return jnp.sum(jnp.where(lane == idx, x, 0.0), axis=1, keepdims=True)


def _chunk_masks():
    r = lax.broadcasted_iota(jnp.int32, (CHUNK, CHUNK), 0)
    c = lax.broadcasted_iota(jnp.int32, (CHUNK, CHUNK), 1)
    return r == c, r >= c, r > c


def _dot3(a, b):
    ah, bh = a.astype(MXU_DTYPE), b.astype(MXU_DTYPE)
    al, bl = (a - ah.astype(F32)).astype(MXU_DTYPE), (b - bh.astype(F32)).astype(MXU_DTYPE)
    dot = lambda x, y: lax.dot_general(x, y, (((1,), (0,)), ((), ())), preferred_element_type=F32)
    return dot(ah, bh) + (dot(ah, bl) + dot(al, bh))


def _tri_inv_steps(low, eye):
    x = -low
    p = jnp.where(eye, 1.0, 0.0) + x
    span = 2
    while span < CHUNK:
        x = _dot3(x, x)
        yield
        p = p + _dot3(p, x)
        yield
        span *= 2
    return p


def _round_robin(gens):
    out, live = [None] * len(gens), list(range(len(gens)))
    while live:
        still = []
        for i in live:
            try:
                next(gens[i])
                still.append(i)
            except StopIteration as stop:
                out[i] = stop.value
        live = still
    return out


def _gdn_pre(q, k, v, gc, beta, masks):
    eye, causal, strict = masks
    gc_row = jnp.sum(jnp.where(eye, gc, 0.0), axis=0, keepdims=True)
    decay = jnp.where(causal, jnp.exp(jnp.where(causal, gc - gc_row, 0.0)), 0.0)
    eg = jnp.exp(gc)
    gl = gc[CHUNK - 1:CHUNK, :]
    kb, vb = k * beta, v * beta
    both = _bdot(jnp.concatenate([kb, q], axis=0), k, 1, 1)
    low = jnp.where(strict, both[:CHUNK] * decay, 0.0)
    qk = jnp.where(causal, both[CHUNK:] * decay, 0.0)
    rest = jnp.exp(gl - gc)
    return dict(decay=decay, eg=eg, gl=gl, kb=kb, vb=vb, kbe=kb * eg, low=low, qk=qk, qg=q * eg, rest=rest, kdec=k * rest)


def _gdn_specs(qkv, gbeta, heads, rev):
    bl, s, w3 = qkv.shape
    d, n = w3 // 3, s // CHUNK
    at = (lambda c: n - 1 - c) if rev else (lambda c: c)
    assert d == heads * HEAD
    sec = pl.BlockSpec((None, CHUNK, w3), lambda b, c: (b, at(c), 0))
    gspec = pl.BlockSpec((None, CHUNK, LANE), lambda b, c: (b, at(c), 0))
    sspec = pl.BlockSpec((None, None, heads, HEAD, HEAD), lambda b, c: (b, at(c), 0, 0, 0))
    tspec = pl.BlockSpec((None, None, heads, CHUNK, CHUNK), lambda b, c: (b, at(c), 0, 0, 0))
    return bl, s, d, n, sec, gspec, sspec, tspec


def _gdn_fwd(qkv, gbeta, heads, name):
    bl, s, d, n, sec, gspec, sspec, tspec = _gdn_specs(qkv, gbeta, heads, False)

    def body(x_ref, g_ref, o_ref, s_ref, t_ref, st_ref):
        @pl.when(pl.program_id(1) == 0)
        def _():
            st_ref[...] = jnp.zeros_like(st_ref)

        masks = _chunk_masks()
        eye, causal, _ = masks
        gblk = g_ref[...]
        gc_all = _hdot(jnp.where(causal, 1.0, 0.0), gblk)
        st_all = st_ref[...]

        def head(h):
            st = st_all[h]
            q, k, v = (x_ref[:, sec * d + h * HEAD:sec * d + (h + 1) * HEAD] for sec in range(3))
            pre = _gdn_pre(q, k, v, _lane_col(gc_all, h), _lane_col(gblk, heads + h), masks)
            yield
            t = yield from _tri_inv_steps(pre["low"], eye)
            uw = _bdot(t, jnp.concatenate([pre["vb"], pre["kbe"]], axis=1), 1, 0)
            u, w = uw[:, :HEAD], uw[:, HEAD:]
            yield
            vnew = u - _bdot(w, st, 1, 0)
            yield
            out = _bdot(pre["qg"], st, 1, 0) + _bdot(pre["qk"], vnew, 1, 0)
            return out, t, st * jnp.exp(pre["gl"]) + _bdot(pre["kdec"], vnew, 0, 0)

        outs, ts, states = zip(*_round_robin([head(h) for h in range(heads)]))
        o_ref[...] = jnp.concatenate(outs, axis=1)
        s_ref[...] = st_all
        t_ref[...] = jnp.stack(ts)
        st_ref[...] = jnp.stack(states)

    return pl.pallas_call(
        body, name=name, grid=(bl, n), in_specs=[sec, gspec],
        out_specs=[pl.BlockSpec((None, CHUNK, d), lambda b, c: (b, c, 0)), sspec, tspec],
        out_shape=[jax.ShapeDtypeStruct((bl, s, d), F32), jax.ShapeDtypeStruct((bl, n, heads, HEAD, HEAD), F32),
                   jax.ShapeDtypeStruct((bl, n, heads, CHUNK, CHUNK), F32)],
        scratch_shapes=[pltpu.VMEM((heads, HEAD, HEAD), F32)], compiler_params=_cparams("parallel", "arbitrary"),
    )(qkv, gbeta)


def _gdn_bwd(qkv, gbeta, dout, s_all, t_all, heads, name):
    bl, s, d, n, sec, gspec, sspec, tspec = _gdn_specs(qkv, gbeta, heads, True)
    ospec = pl.BlockSpec((None, CHUNK, d), lambda b, c: (b, n - 1 - c, 0))

    def body(x_ref, g_ref, do_ref, s_ref, t_ref, dx_ref, dg_ref, ds_ref):
        @pl.when(pl.program_id(1) == 0)
        def _():
            ds_ref[...] = jnp.zeros_like(ds_ref)

        masks = _chunk_masks()
        eye, causal, strict = masks
        gblk = g_ref[...]
        gc_all = _hdot(jnp.where(causal, 1.0, 0.0), gblk)
        lane = lax.broadcasted_iota(jnp.int32, gblk.shape, 1)
        last_row = lax.broadcasted_iota(jnp.int32, (CHUNK, 1), 0) == CHUNK - 1
        rowsum = lambda a: jnp.sum(a, axis=1, keepdims=True)
        st_all, t_all_, ds_all = s_ref[...], t_ref[...], ds_ref[...]

        def head(h):
            sl = slice(h * HEAD, (h + 1) * HEAD)
            q, k, v = (x_ref[:, sec * d + h * HEAD:sec * d + (h + 1) * HEAD] for sec in range(3))
            do = do_ref[:, sl]
            beta = _lane_col(gblk, heads + h)
            st, t, dsn = st_all[h], t_all_[h], ds_all[h]
            pre = _gdn_pre(q, k, v, _lane_col(gc_all, h), beta, masks)
            decay, eg, kb, vb, kbe, low, qk, qg, kdec = (pre[x] for x in ("decay", "eg", "kb", "vb", "kbe", "low", "qk", "qg", "kdec"))
            egl = jnp.exp(pre["gl"])
            yield
            uw = _bdot(t, jnp.concatenate([vb, kbe], axis=1), 1, 0)
            u, w = uw[:, :HEAD], uw[:, HEAD:]
            yield
            vnew = u - _bdot(w, st, 1, 0)
            yield
            stack, side = functools.partial(jnp.concatenate, axis=0), functools.partial(jnp.concatenate, axis=1)
            dkdec = _bdot(vnew, dsn, 1, 1)
            dvnew = _bdot(kdec, dsn, 1, 0) + _bdot(qk, do, 0, 0)
            dgl = jnp.sum(dsn * st, keepdims=True) * egl
            dqk = jnp.where(causal, _bdot(do, vnew, 1, 1), 0.0)
            yield
            by_state = _bdot(stack([do, dvnew]), st, 1, 1)
            dqg, dw = by_state[:CHUNK], -by_state[CHUNK:]
            ds_new = dsn * egl + _bdot(stack([qg, -w]), stack([do, dvnew]), 0, 0)
            yield
            dt = _bdot(side([dvnew, dw]), side([vb, kbe]), 1, 1)
            by_t = _bdot(t, side([dvnew, dw]), 0, 0)
            dvb, dkbe = by_t[:, :HEAD], by_t[:, HEAD:]
            yield
            inner = _bdot(dt, t, 1, 1)
            yield
            dlow = -jnp.where(strict, _bdot(t, inner, 0, 0), 0.0)
            da, db = dlow * decay, dqk * decay
            yield
            m = dlow * low + dqk * qk
            kdk = dkdec * kdec
            col_of_m = jnp.sum(jnp.where(eye, jnp.sum(m, axis=0, keepdims=True), 0.0), axis=1, keepdims=True)
            dgc = rowsum(m) - col_of_m + rowsum(dqg * qg) + rowsum(dkbe * kbe) - rowsum(kdk)
            dgc = dgc + jnp.where(last_row, dgl + jnp.sum(kdk, keepdims=True), 0.0)
            by_k = _bdot(stack([da, db]), k, 1, 0)
            dkb = by_k[:CHUNK] + dkbe * eg
            yield
            dk = _bdot(stack([da, db]), stack([kb, q]), 0, 0) + dkdec * pre["rest"] + dkb * beta
            dq = by_k[CHUNK:] + dqg * eg
            dbeta = rowsum(dkb * k) + rowsum(dvb * v)
            return dq, dk, dvb * beta, jnp.where(lane == h, dgc, 0.0) + jnp.where(lane == heads + h, dbeta, 0.0), ds_new

        dqs, dks, dvs, dgs, dss = zip(*_round_robin([head(h) for h in range(heads)]))
        dx_ref[...] = jnp.concatenate(dqs + dks + dvs, axis=1)
        ds_ref[...] = jnp.stack(dss)
        dgb = dgs[0]
        for extra in dgs[1:]:
            dgb = dgb + extra
        upper = jnp.where(jnp.logical_or(eye, jnp.logical_not(causal)), 1.0, 0.0)
        dg_ref[...] = jnp.where(lane < heads, _hdot(upper, dgb), dgb)

    return pl.pallas_call(
        body, name=name, grid=(bl, n), in_specs=[sec, gspec, ospec, sspec, tspec], out_specs=[sec, gspec],
        out_shape=[jax.ShapeDtypeStruct(qkv.shape, F32), jax.ShapeDtypeStruct((bl, s, LANE), F32)],
        scratch_shapes=[pltpu.VMEM((heads, HEAD, HEAD), F32)], compiler_params=_cparams("parallel", "arbitrary"),
    )(qkv, gbeta, dout, s_all, t_all)


def _position():
    return lax.axis_index("x"), lax.axis_index("y"), lax.axis_index("c")


def _all_gather(x, *, name, hbm):
    space = pltpu.HBM if hbm else pltpu.VMEM

    def body(x_ref, out_ref, send_sems, recv_sems, local_sem):
        ax, ay, ac = _position()
        me, sibling = (ax, ay, ac), (ax, ay, 1 - ac)
        chips = [(1 - ax, ay), (ax, 1 - ay), (1 - ax, 1 - ay)]

        def slot(px, py, pc):
            return out_ref.at[4 * px + 2 * py + pc]

        def copy(k, block, to, src=None):
            return pltpu.make_async_remote_copy(
                src_ref=slot(*block) if src is None else src, dst_ref=slot(*block), send_sem=send_sems.at[k],
                recv_sem=recv_sems.at[k], device_id=to, device_id_type=MESH_IDS)

        mine = pltpu.make_async_copy(x_ref, slot(*me), local_sem)
        mine.start()
        first = [copy(0, me, sibling, src=x_ref)] + [copy(1 + j, me, (*chip, ac), src=x_ref) for j, chip in enumerate(chips)]
        for cp in first:
            cp.start()
        passed = [copy(4 + j, (*chip, ac), sibling) for j, chip in enumerate(chips)]
        for j, chip in enumerate(chips):
            copy(1 + j, (*chip, ac), me).wait_recv()
            passed[j].start()
        copy(0, sibling, me).wait_recv()
        for j, chip in enumerate(chips):
            copy(4 + j, (*chip, 1 - ac), me).wait_recv()
        for cp in first + passed:
            cp.wait_send()
        mine.wait()

    return pl.pallas_call(
        body, name=name, out_shape=jax.ShapeDtypeStruct((NDEV,) + x.shape, x.dtype),
        in_specs=[pl.BlockSpec(memory_space=space)], out_specs=pl.BlockSpec(memory_space=space),
        scratch_shapes=[pltpu.SemaphoreType.DMA((7,)), pltpu.SemaphoreType.DMA((7,)), pltpu.SemaphoreType.DMA],
    )(x)


class _Rider:
    def __init__(self, arrays, out_shapes, sems, hooks):
        self.arrays, self.out_shapes, self.sems, self.hooks = arrays, out_shapes, sems, hooks


def _gather_rider(xs):
    n = len(xs)

    def hooks(x_refs, out_refs, send_sems, recv_sems):
        ax, ay, ac = _position()
        me, sibling = (ax, ay, ac), (ax, ay, 1 - ac)
        chips = [(1 - ax, ay), (ax, 1 - ay), (1 - ax, 1 - ay)]

        def copies(k, block, to, own=False):
            out = []
            for i in range(n):
                slot = out_refs[i].at[4 * block[0] + 2 * block[1] + block[2]]
                out.append(pltpu.make_async_remote_copy(
                    src_ref=x_refs[i] if own else slot, dst_ref=slot, send_sem=send_sems.at[k, i], recv_sem=recv_sems.at[k, i],
                    device_id=to, device_id_type=MESH_IDS))
            return out

        def first():
            for cp in copies(0, me, sibling, own=True):
                cp.start()
            for j, chip in enumerate(chips):
                for cp in copies(1 + j, me, (*chip, ac), own=True):
                    cp.start()

        def mid():
            for j, chip in enumerate(chips):
                for arrived, onward in zip(copies(1 + j, (*chip, ac), me), copies(4 + j, (*chip, ac), sibling)):
                    arrived.wait_recv()
                    onward.start()

        def last():
            for cp in copies(0, sibling, me):
                cp.wait_recv()
            for j, chip in enumerate(chips):
                for cp in copies(4 + j, (*chip, 1 - ac), me):
                    cp.wait_recv()
            for cp in copies(0, me, sibling, own=True):
                cp.wait_send()
            for j, chip in enumerate(chips):
                for cp in copies(1 + j, me, (*chip, ac), own=True) + copies(4 + j, (*chip, ac), sibling):
                    cp.wait_send()

        return first, mid, last

    return _Rider(list(xs), [jax.ShapeDtypeStruct((NDEV,) + x.shape, x.dtype) for x in xs],
                  [pltpu.SemaphoreType.DMA((7, n)), pltpu.SemaphoreType.DMA((7, n))], hooks)


def _scatter_rider(parts):
    packed = sum(r for _, r in parts)
    width, dtype = parts[0][0].shape[1], parts[0][0].dtype

    def hooks(g_refs, out_refs, send_sems, recv_sems):
        (recv_ref,) = out_refs
        ax, ay, ac = _position()

        def peer(rel):
            flip = lambda a, bit: 1 - a if rel & bit else a
            return flip(ax, 4), flip(ay, 2), flip(ac, 1)

        def first():
            for rel in range(1, NDEV):
                px, py, pc = peer(rel)
                off = 0
                for g_ref, (_, r) in zip(g_refs, parts):
                    rows = g_ref.at[pl.ds(pl.multiple_of((4 * px + 2 * py + pc) * r, ROW_ALIGN), r)]
                    pltpu.make_async_remote_copy(
                        src_ref=rows, dst_ref=recv_ref.at[rel - 1, pl.ds(off, r)], send_sem=send_sems.at[rel - 1],
                        recv_sem=recv_sems.at[rel - 1], device_id=(px, py, pc), device_id_type=MESH_IDS).start()
                    off += r

        def last():
            for rel in range(1, NDEV):
                slot = recv_ref.at[rel - 1]
                pltpu.make_async_remote_copy(src_ref=slot, dst_ref=slot, send_sem=send_sems.at[rel - 1],
                                             recv_sem=recv_sems.at[rel - 1], device_id=peer(rel), device_id_type=MESH_IDS).wait()

        return first, lambda: None, last

    return _Rider([g for g, _ in parts], [jax.ShapeDtypeStruct((NDEV - 1, packed, width), dtype)],
                  [pltpu.SemaphoreType.DMA((NDEV - 1,)), pltpu.SemaphoreType.DMA((NDEV - 1,))], hooks)


def _sum_direct(own, recv, name):
    r, w = own.shape
    tr = max(t for t in range(ROW_ALIGN, 257, ROW_ALIGN) if r % t == 0)

    def body(own_ref, *refs):
        acc = own_ref[...].astype(F32)
        for ref in refs[:-1]:
            acc = acc + ref[...].astype(F32)
        refs[-1][...] = acc

    rblk = lambda k: pl.BlockSpec((None, tr, w), functools.partial(lambda i, k: (k, i, 0), k=k))
    blk = pl.BlockSpec((tr, w), lambda i: (i, 0))
    return pl.pallas_call(body, name=name, grid=(r // tr,), in_specs=[blk] + [rblk(k) for k in range(NDEV - 1)],
                          out_specs=blk, out_shape=jax.ShapeDtypeStruct((r, w), F32),
                          compiler_params=_cparams("parallel"))(own, *([recv] * (NDEV - 1)))


ROW_ALIGN = 16


def _window_start(rows_per_dev, k):
    return rows_per_dev * k // ROW_ALIGN * ROW_ALIGN


def _exchange_in_chip(parts, name, collective_id):
    packed = sum(win for _, _, win, _ in parts)
    width, dtype = parts[0][0].shape[1], parts[0][0].dtype

    def body(g_refs, out_refs, send_sems, recv_sems):
        (recv_ref,) = out_refs
        ax, ay, ac = _position()
        sibling = (ax, ay, 1 - ac)
        _handshake([sibling])
        for q in range(4):
            for g_ref, (_, r, win, off) in zip(g_refs, parts):
                there = g_ref.at[pl.ds(pl.multiple_of(_window_start(r, 2 * q + 1 - ac), ROW_ALIGN), win)]
                pltpu.make_async_remote_copy(src_ref=there, dst_ref=recv_ref.at[q, pl.ds(off, win)], send_sem=send_sems.at[q],
                                             recv_sem=recv_sems.at[q], device_id=sibling, device_id_type=MESH_IDS).start()
        for q in range(4):
            pltpu.make_async_remote_copy(src_ref=recv_ref.at[q], dst_ref=recv_ref.at[q], send_sem=send_sems.at[q],
                                         recv_sem=recv_sems.at[q], device_id=sibling, device_id_type=MESH_IDS).wait()

    return _on_sequencer(body, [g for g, _, _, _ in parts], [jax.ShapeDtypeStruct((4, packed, width), dtype)],
                         [pltpu.SemaphoreType.DMA((4,)), pltpu.SemaphoreType.DMA((4,))], name=name, collective_id=collective_id)[0]


def _on_sequencer(body, ins, out_shapes, sems, *, name, collective_id):
    hbm = pltpu.MemorySpace.HBM
    in_refs = [jax.new_ref(a, memory_space=hbm) for a in ins]
    out_refs = [jax.empty_ref(s, memory_space=hbm) for s in out_shapes]

    @pl.kernel(mesh=plsc.ScalarSubcoreMesh(axis_name="sequencer", num_cores=1), name=name, scratch_types=tuple(sems),
               compiler_params=pltpu.CompilerParams(collective_id=collective_id))
    def launch(*sem_refs):
        body(in_refs, out_refs, *sem_refs)

    launch()
    return [r[...] for r in out_refs]


def _handshake(peers):
    barrier = pltpu.get_barrier_semaphore()
    for peer in peers:
        pl.semaphore_signal(barrier, inc=1, device_id=peer, device_id_type=MESH_IDS)
    pl.semaphore_wait(barrier, len(peers))


def _exchange_chips_async(s1, name, collective_id):
    def body(in_refs, out_refs, send_sems, recv_sems):
        (src,), (got,) = in_refs, out_refs
        ax, ay, ac = _position()
        chips = [(1 - ax, ay), (ax, 1 - ay), (1 - ax, 1 - ay)]
        _handshake([(cx, cy, ac) for cx, cy in chips])
        copies = [pltpu.make_async_remote_copy(
            src_ref=src.at[2 * cx + cy], dst_ref=got.at[r], send_sem=send_sems.at[r], recv_sem=recv_sems.at[r],
            device_id=(cx, cy, ac), device_id_type=MESH_IDS) for r, (cx, cy) in enumerate(chips)]
        for cp in copies:
            cp.start()
        for cp in copies:
            cp.wait_recv()
        for cp in copies:
            cp.wait_send()

    return _on_sequencer(body, [s1], [jax.ShapeDtypeStruct((3,) + s1.shape[1:], s1.dtype)],
                         [pltpu.SemaphoreType.DMA((3,)), pltpu.SemaphoreType.DMA((3,))], name=name, collective_id=collective_id)[0]


def _gather_async(xs, name, collective_id):
    rider = _gather_rider(xs)

    def body(in_refs, out_refs, send_sems, recv_sems):
        ax, ay, ac = _position()
        _handshake([(ax, ay, 1 - ac), (1 - ax, ay, ac), (ax, 1 - ay, ac), (1 - ax, 1 - ay, ac)])
        for hook in rider.hooks(in_refs, out_refs, send_sems, recv_sems):
            hook()

    return _on_sequencer(body, rider.arrays, rider.out_shapes, rider.sems, name=name, collective_id=collective_id)


def _scatter_async(parts, name, collective_id):
    rider = _scatter_rider(parts)

    def body(in_refs, out_refs, send_sems, recv_sems):
        ax, ay, ac = _position()
        flip = lambda a, on: 1 - a if on else a
        _handshake([(flip(ax, rel & 4), flip(ay, rel & 2), flip(ac, rel & 1)) for rel in range(1, NDEV)])
        for hook in rider.hooks(in_refs, out_refs, send_sems, recv_sems):
            hook()

    return _on_sequencer(body, rider.arrays, rider.out_shapes, rider.sems, name=name, collective_id=collective_id)[0]


def _sum_in_chip(own, recv, name):
    _, r, w = own.shape
    tr = _tile(r, (256, 128))

    def body(a_ref, b_ref, o_ref):
        o_ref[...] = (a_ref[...].astype(F32) + b_ref[...].astype(F32)).astype(o_ref.dtype)

    blk = pl.BlockSpec((None, tr, w), lambda q, i: (q, i, 0))
    return pl.pallas_call(body, name=name, grid=(4, r // tr), in_specs=[blk, blk], out_specs=blk,
                          out_shape=jax.ShapeDtypeStruct(own.shape, own.dtype),
                          compiler_params=_cparams("parallel", "parallel"))(own, recv)


def _sum_chips(s1, recv, chip, name):
    _, r, w = s1.shape
    tr = _tile(r, (256, 128))

    def body(c_ref, s_ref, r0_ref, r1_ref, r2_ref, o_ref):
        f = lambda ref: ref[...].astype(F32)
        o_ref[...] = ((f(s_ref) + f(r0_ref)) + f(r1_ref)) + f(r2_ref)

    rblk = lambda k: pl.BlockSpec((None, tr, w), functools.partial(lambda i, c, k: (k, i, 0), k=k))
    grid_spec = pltpu.PrefetchScalarGridSpec(
        num_scalar_prefetch=1, grid=(r // tr,),
        in_specs=[pl.BlockSpec((None, tr, w), lambda i, c: (c[0], i, 0)), rblk(0), rblk(1), rblk(2)],
        out_specs=pl.BlockSpec((tr, w), lambda i, c: (i, 0)))
    return pl.pallas_call(body, name=name, grid_spec=grid_spec, out_shape=jax.ShapeDtypeStruct((r, w), F32),
                          compiler_params=_cparams("parallel"))(chip, s1, recv, recv, recv)


def _silu_rows(x, name):
    def body(x_ref, o_ref):
        o_ref[...] = _silu(x_ref[...])

    return pl.pallas_call(body, name=name, out_shape=jax.ShapeDtypeStruct(x.shape, F32))(x)


def _row_sum(x, name):
    def body(x_ref, o_ref):
        acc = x_ref[0:1, :]
        for i in range(1, x.shape[0]):
            acc = acc + x_ref[i:i + 1, :]
        o_ref[...] = acc

    return pl.pallas_call(body, name=name, out_shape=jax.ShapeDtypeStruct((1, x.shape[1]), F32))(x)


def _adamw(w, g, m, v, name):
    cols = w.shape[-1]
    rows = w.size // cols
    tr = _tile(rows, (128,))
    tc = LANE if (tr == rows and rows > 512 and cols % LANE == 0) else cols

    def body(w_ref, g_ref, m_ref, v_ref, d_ref, mo_ref, vo_ref):
        grad = g_ref[...]
        m_new = ADAM_B1 * m_ref[...] + (1.0 - ADAM_B1) * grad
        v_new = ADAM_B2 * v_ref[...] + (1.0 - ADAM_B2) * jnp.square(grad)
        m_hat = m_new / (1.0 - ADAM_B1 ** ADAM_STEP)
        v_hat = v_new / (1.0 - ADAM_B2 ** ADAM_STEP)
        d_ref[...] = -ADAM_LR * (m_hat / (jnp.sqrt(v_hat) + ADAM_EPS) + ADAM_WD * w_ref[...])
        mo_ref[...] = m_new
        vo_ref[...] = v_new

    blk = pl.BlockSpec((tr, tc), lambda i, j: (i, j))
    out = pl.pallas_call(
        body, name=name, grid=(rows // tr, cols // tc), in_specs=[blk] * 4, out_specs=[blk] * 3,
        out_shape=[jax.ShapeDtypeStruct((rows, cols), F32)] * 3, compiler_params=_cparams("parallel", "parallel"),
    )(*[t.reshape(rows, cols) for t in (w, g, m, v)])
    return [t.reshape(w.shape) for t in out]


def _pack(parts, width, row_mult, dtype):
    flat = jnp.concatenate([p.reshape(-1).astype(dtype) for p in parts])
    rows = -(-flat.shape[0] // (width * row_mult)) * row_mult
    return jnp.pad(flat, (0, rows * width - flat.shape[0])).reshape(rows, width)


def _unpack(flat, shapes):
    out, off = [], 0
    for shp in shapes:
        size = 1
        for dim in shp:
            size *= dim
        out.append(flat[:, off:off + size].reshape((flat.shape[0],) + tuple(shp)))
        off += size
    return out


def _devices_to_cols(a):
    _, r, c = a.shape
    return a.transpose(1, 0, 2).reshape(r, NDEV * c)


def kernel(x, c, w_ada, b_ada, norm1_w, w_in, gdn_conv_w, gdn_a_log, gdn_dt_bias, gdn_norm_w, w_gdn_proj, sc_conv_w, w_sc_out, w_o, norm2_w, w_ffn_in, w_ffn_out, w_ada_f, b_ada_f, normf_w, loss_target, m_w_ada, m_b_ada, m_norm1_w, m_w_in, m_gdn_conv_w, m_gdn_a_log, m_gdn_dt_bias, m_gdn_norm_w, m_w_gdn_proj, m_sc_conv_w, m_w_sc_out, m_w_o, m_norm2_w, m_w_ffn_in, m_w_ffn_out, m_w_ada_f, m_b_ada_f, m_normf_w, v_w_ada, v_b_ada, v_norm1_w, v_w_in, v_gdn_conv_w, v_gdn_a_log, v_gdn_dt_bias, v_gdn_norm_w, v_w_gdn_proj, v_sc_conv_w, v_w_sc_out, v_w_o, v_norm2_w, v_w_ffn_in, v_w_ffn_out, v_w_ada_f, v_b_ada_f, v_normf_w):
    bl, s, d = x.shape
    heads = gdn_a_log.shape[-1]
    dff = w_ffn_out.shape[1] * NDEV
    tok = bl * s
    ax, ay, ac = _position()
    dev = 4 * ax + 2 * ay + ac
    as_tok = lambda a: a.reshape(bl, s, a.shape[-1])
    as_mat = lambda a: a.reshape(tok, a.shape[-1])

    small = _all_gather(_pack([c, gdn_conv_w, sc_conv_w], LANE, 8, F32), name="gather_cond", hbm=False)
    c_all, conv_w, sc_w = _unpack(small.reshape(NDEV, -1), [(bl, d), gdn_conv_w.shape[1:], sc_conv_w.shape[1:]])
    c_act = _silu_rows(c_all.reshape(NDEV * bl, d), "cond_silu")
    conv_w, sc_w = _devices_to_cols(conv_w), _devices_to_cols(sc_w)
    n_ada, n_adaf = w_ada.shape[-1], w_ada_f.shape[-1]
    bias = jnp.broadcast_to(lax.dynamic_slice_in_dim(b_ada, dev * n_ada, n_ada, axis=1), (NDEV * bl, n_ada))
    biasf = jnp.broadcast_to(lax.dynamic_slice_in_dim(b_ada_f.reshape(1, -1), dev * n_adaf, n_adaf, axis=1), (NDEV * bl, n_adaf))
    mod_cols = _mm(c_act, w_ada[0], add=bias, name="ada_cols")
    modf_cols = _mm(c_act, w_ada_f, add=biasf, name="adaf_cols")
    mods = _all_gather(jnp.concatenate([mod_cols, modf_cols], axis=1), name="gather_mod", hbm=False)
    mod_all = mods[:, :, :n_ada].transpose(1, 0, 2).reshape(NDEV * bl, NDEV * n_ada)
    modf_all = mods[:, :, n_ada:].transpose(1, 0, 2).reshape(NDEV * bl, NDEV * n_adaf)
    my_rows = lambda a: lax.dynamic_slice_in_dim(a, dev * bl, bl, axis=0)
    sh1, sc1, g1, sh2, sc2, g2 = [t.reshape(bl, 1, d) for t in jnp.split(my_rows(mod_all), 6, axis=1)]
    shf, scf = [t.reshape(bl, 1, d) for t in jnp.split(my_rows(modf_all), 2, axis=1)]

    late = [t.astype(MXU_DTYPE) for t in (w_gdn_proj[0], w_sc_out[0], w_o[0], w_ffn_in[0].T, w_ffn_out[0])]
    rows = [t.shape[0] for t in late] + [w_in.shape[-1]]
    offs = [sum(rows[:i]) for i in range(5)]
    in_rows = -(-rows[5] // ROW_ALIGN) * ROW_ALIGN
    in_send = jnp.pad(w_in[0].T.astype(MXU_DTYPE), ((0, in_rows - rows[5]), (0, 0)))
    with_own = lambda g, own: lax.dynamic_update_slice_in_dim(g, own[None], dev, axis=0)
    (wt_in,) = _gather_async([in_send], "gather_w_in", 1)
    wt_in = with_own(wt_in, in_send)[:, :rows[5], :].reshape(NDEV * rows[5], d)
    gathered = _gather_async(late[:3], "gather_mixer", 2) + _gather_async(late[3:], "gather_ffn", 3)
    wgp, wso, wo, wt_fi, wfo = [with_own(g, own).reshape(NDEV * own.shape[0], d) for g, own in zip(gathered, late)]
    o_z, o_ab, o_sc, o_ga, o_gb = 3 * d, 4 * d, 4 * d + 2 * heads, 7 * d + 2 * heads, 8 * d + 2 * heads
    s_qkv, s_z, s_sc, s_gate = (0, o_z), (o_z, d), (o_sc, 3 * d), (o_ga, 2 * d)
    wt_ab = jnp.pad(wt_in[o_ab:o_sc], ((0, LANE - 2 * heads), (0, 0)))

    n1w, n2w, nfw = norm1_w.reshape(1, d), norm2_w.reshape(1, d), normf_w.reshape(1, d)
    lanes = lambda a: jnp.pad(a.reshape(1, -1), ((0, 0), (0, LANE - a.size)))
    a_log, dt_bias, gnw = lanes(gdn_a_log), lanes(gdn_dt_bias), gdn_norm_w.reshape(1, HEAD)
    f_gates = functools.partial(_f_gates, heads=heads)
    (h1,) = _tok_fwd(_f_norm_mod, [x], [sh1, sc1], [n1w], [(d, MXU_DTYPE)], name="norm1", ts=512)
    h1m = as_mat(h1)
    p_qkv = as_tok(_mm(h1m, wt_in, tb=True, b_rows=s_qkv, name="in_qkv"))
    p_z = as_tok(_mm(h1m, wt_in, tb=True, b_rows=s_z, name="in_z"))
    p_ab = as_tok(_mm(h1m, wt_ab, tb=True, name="in_ab"))
    p_sc = as_tok(_mm(h1m, wt_in, tb=True, b_rows=s_sc, name="in_sc"))
    p_g = as_tok(_mm(h1m, wt_in, tb=True, b_rows=s_gate, name="in_gate"))
    qkv = _qkv_fwd(p_qkv, conv_w, heads, "qkv_conv")
    (gbeta,) = _tok_fwd(f_gates, [p_ab], [], [a_log, dt_bias], [(LANE, F32)], name="gates", ts=512)
    o, s_all, t_all = _gdn_fwd(qkv, gbeta, heads, "gdn")
    (og,) = _tok_fwd(_f_gdn_out, [o, p_z], [], [(gnw, None)], [(d, MXU_DTYPE)], name="gdn_out", ts=2048, wb=HEAD, cols=heads)
    y_a = as_tok(_mm(as_mat(og), wgp, name="gdn_proj"))
    scp = _sc_fwd(p_sc, sc_w, "sc_conv")
    mrg, y_b = _tok_fwd(_f_merge_keep, [(p_g, 0), (p_g, 1), y_a, _Product(scp, wso)], [], [], [(d, MXU_DTYPE), (d, F32)],
                        name="merge", ts=256, wb=d)
    merge_toks = [(p_g, 0), (p_g, 1), y_a, y_b]
    x2, h2, mix = _tok_fwd(_f_res_norm_mod_keep, [x, _Product(mrg, wo)], [g1, sh2, sc2], [n2w],
                           [(d, F32), (d, MXU_DTYPE), (d, F32)], name="norm2", ts=512)
    act, gu_a, gu_b = _ffn_in_swiglu(as_mat(h2), wt_fi, dff, "ffn_in")

    loss_l, (dx2, dff_out, _), (dg2, dshf, dscf), (dnfw,) = _tok_bwd(
        _f_loss, [x2, _Product(as_tok(act), wfo), loss_target], [g2, shf, scf], [nfw], [], [True, True, False], name="loss",
        ts=256, loss=True, tok_dtype=[F32, MXU_DTYPE, None])
    dffm = as_mat(dff_out)
    dgu_a, dgu_b = _ffn_out_bwd_swiglu(dffm, wfo, gu_a, gu_b, "d_ffn_out")
    gmm = functools.partial(_mm, ta=True, out_dtype=MXU_DTYPE)
    gw_ffn_out = gmm(act, dffm, name="g_ffn_out")
    dh2 = _Product(as_tok(dgu_b), wt_fi, b_rows=(dff, dff), add=as_tok(_mm(dgu_a, wt_fi, b_rows=(0, dff), name="d_ffn_in_a")))
    h2m = as_mat(h2)
    gwt_ffn_in = gmm(dgu_a, h2m, out_rows=2 * dff, name="g_ffn_in_a")
    gwt_ffn_in = gmm(dgu_b, h2m, out_rows=2 * dff, row_off=dff, into=gwt_ffn_in, name="g_ffn_in_b")
    ffn_parts = [(gwt_ffn_in, rows[3]), (gw_ffn_out, rows[4])]
    ffn_recv = _scatter_async(ffn_parts, "scatter_ffn", 4)
    (dx_skip, dmix), (dg1, dsh2, dsc2), (dn2w,) = _tok_bwd(
        _f_res_norm_mod, [x, mix], [g1, sh2, sc2], [n2w], [dx2, dh2], [True, True], name="d_norm2", ts=256,
        tok_dtype=[F32, MXU_DTYPE], after=[gwt_ffn_in, gw_ffn_out])
    gw_o = gmm(as_mat(mrg), as_mat(dmix), name="g_mix_out")
    (dga, dgb, dya, dyb), _, _ = _tok_bwd(_f_merge, merge_toks, [], [], [_Product(dmix, wo, tb=True)], [True] * 4,
                                          name="d_merge", ts=256, wb=d, tok_dtype=MXU_DTYPE)
    dyam, dybm = as_mat(dya), as_mat(dyb)
    dog = as_tok(_mm(dyam, wgp, tb=True, name="d_gdn_proj"))
    gw_gdn_proj = gmm(as_mat(og), dyam, name="g_gdn_proj")
    dscp = as_tok(_mm(dybm, wso, tb=True, name="d_sc_out"))
    gw_sc_out = gmm(as_mat(scp), dybm, name="g_sc_out")
    dscb, dscc, dscx, g_sc_w = _sc_bwd(p_sc, sc_w, dscp, "d_sc_conv")
    mix_parts = [(gw_gdn_proj, rows[0]), (gw_sc_out, rows[1]), (gw_o, rows[2])]
    mix_recv = _scatter_async(mix_parts, "scatter_mixer", 5)
    (do, dz), _, (g_gnw,) = _tok_bwd(_f_gdn_out, [o, p_z], [], [(gnw, None)], [dog], [True, True], name="d_gdn_out",
                                     ts=2048, wb=HEAD, cols=heads, tok_dtype=[F32, MXU_DTYPE],
                                     after=[gw_gdn_proj, gw_sc_out, gw_o])
    own_rows = lambda parts: jnp.concatenate([lax.dynamic_slice_in_dim(g, dev * r, r, axis=0) for g, r in parts], axis=0)
    dqkv, dgbeta = _gdn_bwd(qkv, gbeta, do, s_all, t_all, heads, "d_gdn")
    dp_qkv, g_conv_w = _qkv_bwd(p_qkv, conv_w, dqkv, heads, "d_qkv_conv")
    ffn_red = _sum_direct(own_rows(ffn_parts), ffn_recv, "sum_ffn")
    mix_red = _sum_direct(own_rows(mix_parts), mix_recv, "sum_mix")
    (dp_ab,), _, (g_a_log, g_dt_bias) = _tok_bwd(f_gates, [p_ab], [], [a_log, dt_bias], [dgbeta], [True], name="d_gates",
                                                 ts=512, tok_dtype=MXU_DTYPE, after=[ffn_red, mix_red])
    sections = [(dp_qkv, s_qkv), (dz, s_z), (dp_ab, None), (dscb, (o_sc, d)), (dscc, (o_sc + d, d)), (dscx, (o_sc + 2 * d, d)),
                (dga, (o_ga, d)), (dgb, (o_gb, d))]
    gwt_in = [gmm(as_mat(dp), h1m, name=f"g_in_{k}") for k, (dp, _) in enumerate(sections)]
    gwt_in[2] = gwt_in[2][:2 * heads]

    r_in = rows[5]
    win = -(-(r_in + max(r_in * k % ROW_ALIGN for k in range(NDEV))) // 128) * 128
    need_rows = max(_window_start(r_in, k) for k in range(NDEV)) + win
    gwt_in = jnp.concatenate(gwt_in + [jnp.zeros((need_rows - NDEV * r_in, d), MXU_DTYPE)], axis=0)
    recv1 = _exchange_in_chip([(gwt_in, r_in, win, 0)], "scatter_in_chip", 7)
    own = jnp.stack([lax.dynamic_slice_in_dim(gwt_in, _window_start(r_in, 2 * q + ac), win, axis=0) for q in range(4)])
    s1 = _sum_in_chip(own, recv1, "sum_in_chip")
    recv2 = _exchange_chips_async(s1, "scatter_chips", 6)

    dh1 = _mm(as_mat(dp_ab), wt_ab, name="d_in_ab")
    wide = [as_mat(dp) for dp, sec in sections[:-1] if sec is not None]
    tk_in = min(d, 1024)
    dh1 = _mm_chain(wide, wt_in, lambda t: tk_in * t + jnp.where(t * tk_in >= o_ab, 2 * heads, 0), add=dh1, name="d_in", tk=tk_in)
    dh1 = _Product(sections[-1][0], wt_in, b_rows=sections[-1][1], add=as_tok(dh1))
    (grad_x,), (dsh1, dsc1), (dn1w,) = _tok_bwd(_f_norm_mod_skip, [x], [sh1, sc1], [n1w], [dh1, dx_skip], [True],
                                                name="d_norm1", ts=256)
    reduced = _sum_chips(s1, recv2, (2 * ax + ay).reshape(1).astype(jnp.int32), "sum_chips")
    gt_w_in = lax.dynamic_slice_in_dim(reduced, r_in * dev - _window_start(r_in, dev), r_in, axis=0)
    g_w_in = gt_w_in.T.reshape(w_in.shape)
    gt_w_ffn_in = ffn_red[:rows[3]]
    g_w_ffn_in = gt_w_ffn_in.T.reshape(w_ffn_in.shape)
    g_w_ffn_out = ffn_red[rows[3]:].reshape(w_ffn_out.shape)
    g_w_gdn_proj, g_w_sc_out, g_w_o = (mix_red[offs[i]:offs[i] + rows[i]].reshape(ref.shape)
                                       for i, ref in enumerate((w_gdn_proj, w_sc_out, w_o)))

    dmod = jnp.concatenate([t.reshape(bl, d) for t in (dsh1, dsc1, dg1, dsh2, dsc2, dg2)], axis=1)
    dmodf = jnp.concatenate([t.reshape(bl, d) for t in (dshf, dscf)], axis=1)
    summed_parts = [dn1w, dn2w, dnfw, g_gnw, g_a_log, g_dt_bias, g_conv_w, g_sc_w, loss_l]
    partial = _all_gather(_pack([dmod, dmodf] + summed_parts, LANE, 8, F32), name="gather_small", hbm=False)
    partial = partial.reshape(NDEV, -1)
    n_rows = bl * (6 * d + 2 * d)
    dmod_all, dmodf_all = _unpack(partial[:, :n_rows], [(bl, 6 * d), (bl, 2 * d)])
    dmod_all, dmodf_all = dmod_all.reshape(NDEV * bl, 6 * d), dmodf_all.reshape(NDEV * bl, 2 * d)
    totals = _row_sum(partial[:, n_rows:], "sum_small")
    t_n1w, t_n2w, t_nfw, t_gnw, t_a_log, t_dt_bias, t_conv_w, t_sc_w, t_loss = [
        t[0] for t in _unpack(totals, [p.shape for p in summed_parts])]
    my_cols = lambda a, n: lax.dynamic_slice_in_dim(a, dev * n, n, axis=1)
    grads = {
        "w_ada": _mm(c_act, my_cols(dmod_all, n_ada), ta=True, name="g_ada").reshape(w_ada.shape),
        "b_ada": _row_sum(dmod_all, "g_ada_bias").reshape(b_ada.shape),
        "norm1_w": t_n1w.reshape(norm1_w.shape),
        "w_in": g_w_in,
        "gdn_conv_w": my_cols(t_conv_w, gdn_conv_w.shape[-1]).reshape(gdn_conv_w.shape),
        "gdn_a_log": t_a_log[:, :heads].reshape(gdn_a_log.shape),
        "gdn_dt_bias": t_dt_bias[:, :heads].reshape(gdn_dt_bias.shape),
        "gdn_norm_w": t_gnw.reshape(gdn_norm_w.shape),
        "w_gdn_proj": g_w_gdn_proj,
        "sc_conv_w": my_cols(t_sc_w, sc_conv_w.shape[-1]).reshape(sc_conv_w.shape),
        "w_sc_out": g_w_sc_out,
        "w_o": g_w_o,
        "norm2_w": t_n2w.reshape(norm2_w.shape),
        "w_ffn_in": g_w_ffn_in,
        "w_ffn_out": g_w_ffn_out,
        "w_ada_f": _mm(c_act, my_cols(dmodf_all, n_adaf), ta=True, name="g_adaf").reshape(w_ada_f.shape),
        "b_ada_f": _row_sum(dmodf_all, "g_adaf_bias").reshape(b_ada_f.shape),
        "normf_w": t_nfw.reshape(normf_w.shape),
    }
    weights = dict(w_ada=w_ada, b_ada=b_ada, norm1_w=norm1_w, w_in=w_in, gdn_conv_w=gdn_conv_w, gdn_a_log=gdn_a_log,
                   gdn_dt_bias=gdn_dt_bias, gdn_norm_w=gdn_norm_w, w_gdn_proj=w_gdn_proj, sc_conv_w=sc_conv_w,
                   w_sc_out=w_sc_out, w_o=w_o, norm2_w=norm2_w, w_ffn_in=w_ffn_in, w_ffn_out=w_ffn_out, w_ada_f=w_ada_f,
                   b_ada_f=b_ada_f, normf_w=normf_w)
    m_in = [m_w_ada, m_b_ada, m_norm1_w, m_w_in, m_gdn_conv_w, m_gdn_a_log, m_gdn_dt_bias, m_gdn_norm_w, m_w_gdn_proj,
            m_sc_conv_w, m_w_sc_out, m_w_o, m_norm2_w, m_w_ffn_in, m_w_ffn_out, m_w_ada_f, m_b_ada_f, m_normf_w]
    v_in = [v_w_ada, v_b_ada, v_norm1_w, v_w_in, v_gdn_conv_w, v_gdn_a_log, v_gdn_dt_bias, v_gdn_norm_w, v_w_gdn_proj,
            v_sc_conv_w, v_w_sc_out, v_w_o, v_norm2_w, v_w_ffn_in, v_w_ffn_out, v_w_ada_f, v_b_ada_f, v_normf_w]
    deltas, new_m, new_v = [], [], []
    grads_t = {"w_in": gt_w_in, "w_ffn_in": gt_w_ffn_in}
    for (wname, wt), mt, vt in zip(weights.items(), m_in, v_in):
        if wname in grads_t:
            back = lambda a, wt=wt: a.T.reshape(wt.shape)
            dl, mn, vn = (back(a) for a in _adamw(wt[0].T, grads_t[wname], mt[0].T, vt[0].T, "adamw_" + wname))
        else:
            dl, mn, vn = _adamw(wt, grads[wname], mt, vt, "adamw_" + wname)
        deltas.append(dl)
        new_m.append(mn)
        new_v.append(vn)
    loss = t_loss[0, 0]
    return (loss, grad_x, *[grads[k] for k in weights], *deltas, *new_m, *new_v)
```

```python
import functools

import jax
import jax.numpy as jnp
from jax import lax
from jax.experimental import pallas as pl
from jax.experimental.pallas import tpu as pltpu
from jax.experimental.pallas import tpu_sc as plsc

F32 = jnp.float32
MXU_DTYPE = jnp.bfloat16
NDEV = 8
CHUNK = 64
HEAD = 128
LANE = 128
EPS = 1e-6
ADAM_LR, ADAM_B1, ADAM_B2, ADAM_EPS, ADAM_WD, ADAM_STEP = 0.001, 0.9, 0.999, 1e-08, 0.01, 10
VMEM_LIMIT = 48 * 1024 * 1024
MESH_IDS = pl.DeviceIdType.MESH
HIGHEST = lax.Precision.HIGHEST


def _tile(n, cands=(512, 256, 128)):
    for c in cands:
        if n % c == 0:
            return c
    return n


def _cparams(*sem):
    return pltpu.CompilerParams(dimension_semantics=sem, vmem_limit_bytes=VMEM_LIMIT)


def _mm(a, b, *, ta=False, tb=False, add=None, out_dtype=F32, name, b_rows=None, out_rows=None, row_off=0, into=None):
    m, k = (a.shape[1], a.shape[0]) if ta else a.shape
    b_shape = b.shape if b_rows is None else (b_rows[1], b.shape[1])
    n = b_shape[0] if tb else b_shape[1]
    assert k == (b_shape[1] if tb else b_shape[0])
    if ta:
        tm, tn = _tile(m), n if n <= 1024 else _tile(n)
        tk = k if k <= 4096 else _tile(k, (4096, 2048, 1024, 512))
        if tm * tk > 1024 * 2048:
            tk = _tile(k, (2048, 1024, 512))
    else:
        tk = k if k <= 1024 else _tile(k, (1024, 512))
        tn = _tile(n, (1024 if tk <= 1024 else 512, 512, 256, 128))
        tm = _tile(m, (2048 if (tn <= 512 and tk <= 1024) else 1024, 1024, 512, 256, 128))
    nk = k // tk
    dims = (((0 if ta else 1,), (1 if tb else 0,)), ((), ()))
    has_add = add is not None

    def body(*refs):
        a_ref, b_ref = refs[0], refs[1]
        add_ref = refs[2] if has_add else None
        o_ref = refs[2 + has_add + (into is not None)]
        part = lax.dot_general(a_ref[...].astype(MXU_DTYPE), b_ref[...].astype(MXU_DTYPE), dims,
                               preferred_element_type=F32)

        def finish(acc):
            if has_add:
                acc = acc + add_ref[...]
            o_ref[...] = acc.astype(o_ref.dtype)

        if nk == 1:
            finish(part)
        else:
            acc_ref = refs[-1]
            kk = pl.program_id(2)

            @pl.when(kk == 0)
            def _():
                acc_ref[...] = part

            @pl.when(kk > 0)
            def _():
                acc_ref[...] += part

            @pl.when(kk == nk - 1)
            def _():
                finish(acc_ref[...])

    a_spec = pl.BlockSpec((tk, tm), lambda i, j, kk: (kk, i)) if ta else pl.BlockSpec((tm, tk), lambda i, j, kk: (i, kk))
    if b_rows is None:
        b_spec = pl.BlockSpec((tn, tk), lambda i, j, kk: (j, kk)) if tb else pl.BlockSpec((tk, tn), lambda i, j, kk: (kk, j))
    else:
        at = lambda t: pl.multiple_of(b_rows[0] + t, ROW_ALIGN)
        b_spec = (pl.BlockSpec((pl.Element(tn), pl.Element(tk)), lambda i, j, kk: (at(j * tn), kk * tk)) if tb else
                  pl.BlockSpec((pl.Element(tk), pl.Element(tn)), lambda i, j, kk: (at(kk * tk), j * tn)))
    add_spec = pl.BlockSpec((tm, tn), lambda i, j, kk: (i, j))
    assert row_off % tm == 0
    o_spec = pl.BlockSpec((tm, tn), lambda i, j, kk: (i + row_off // tm, j))
    in_specs = [a_spec, b_spec] + ([add_spec] if has_add else []) + ([pl.BlockSpec(memory_space=pl.ANY)] if into is not None else [])
    args = [a, b] + ([add] if has_add else []) + ([into] if into is not None else [])
    return pl.pallas_call(
        body, name=name, grid=(m // tm, n // tn, nk), in_specs=in_specs, out_specs=o_spec,
        out_shape=jax.ShapeDtypeStruct((out_rows or m, n), out_dtype),
        scratch_shapes=[pltpu.VMEM((tm, tn), F32)] if nk > 1 else [],
        input_output_aliases={len(args) - 1: 0} if into is not None else {},
        compiler_params=_cparams("parallel", "parallel", "arbitrary"),
    )(*args)


def _mm_chain(parts, b, row_of_tile, *, add, name, tk=1024, tm=512):
    m, n = parts[0].shape[0], b.shape[1]
    tiles = [p.shape[1] // tk for p in parts]
    first = [sum(tiles[:s]) for s in range(len(parts))]
    nk = sum(tiles)

    def body(*refs):
        a_refs, b_ref, add_ref, o_ref, acc_ref = refs[:len(parts)], *refs[len(parts):]
        kk = pl.program_id(1)

        @pl.when(kk == 0)
        def _():
            acc_ref[...] = add_ref[...]

        for a_ref, lo, cnt in zip(a_refs, first, tiles):
            @pl.when(jnp.logical_and(kk >= lo, kk < lo + cnt))
            def _(a_ref=a_ref):
                acc_ref[...] += lax.dot_general(a_ref[...].astype(MXU_DTYPE), b_ref[...].astype(MXU_DTYPE),
                                                (((1,), (0,)), ((), ())), preferred_element_type=F32)

        @pl.when(kk == nk - 1)
        def _():
            o_ref[...] = acc_ref[...]

    a_specs = [pl.BlockSpec((tm, tk), functools.partial(lambda i, kk, lo, cnt: (i, jnp.clip(kk - lo, 0, cnt - 1)), lo=lo, cnt=cnt))
               for lo, cnt in zip(first, tiles)]
    b_spec = pl.BlockSpec((pl.Element(tk), pl.Element(n)), lambda i, kk: (pl.multiple_of(row_of_tile(kk), ROW_ALIGN), 0))
    o_spec = pl.BlockSpec((tm, n), lambda i, kk: (i, 0))
    return pl.pallas_call(
        body, name=name, grid=(m // tm, nk), in_specs=a_specs + [b_spec, o_spec], out_specs=o_spec,
        out_shape=jax.ShapeDtypeStruct((m, n), F32), scratch_shapes=[pltpu.VMEM((tm, n), F32)],
        compiler_params=_cparams("parallel", "arbitrary"),
    )(*parts, b, add)


def _swiglu_tiles(m, half):
    tn = _tile(half, (512, 256, 128))
    return _tile(m, (2048 if tn <= 256 else 1024, 1024, 512, 256, 128)), tn


def _ffn_in_swiglu(h, wt, half, name):
    m, k = h.shape
    tm, tn = _swiglu_tiles(m, half)
    nj = half // tn
    dims = (((1,), (1,)), ((), ()))

    def body(h_ref, wa_ref, wb_ref, act_ref, a_ref, b_ref):
        lhs = h_ref[...].astype(MXU_DTYPE)
        a = lax.dot_general(lhs, wa_ref[...].astype(MXU_DTYPE), dims, preferred_element_type=F32)
        b = lax.dot_general(lhs, wb_ref[...].astype(MXU_DTYPE), dims, preferred_element_type=F32)
        act_ref[...] = (_silu(a) * b).astype(act_ref.dtype)
        a_ref[...] = a.astype(a_ref.dtype)
        b_ref[...] = b.astype(b_ref.dtype)

    out = jax.ShapeDtypeStruct((m, half), MXU_DTYPE)
    oblk = pl.BlockSpec((tm, tn), lambda i, j: (i, j))
    return pl.pallas_call(
        body, name=name, grid=(m // tm, nj),
        in_specs=[pl.BlockSpec((tm, k), lambda i, j: (i, 0)), pl.BlockSpec((tn, k), lambda i, j: (j, 0)),
                  pl.BlockSpec((tn, k), lambda i, j: (j + nj, 0))],
        out_specs=[oblk, oblk, oblk], out_shape=[out, out, out], compiler_params=_cparams("parallel", "parallel"),
    )(h, wt, wt)


def _ffn_out_bwd_swiglu(dff, w, a, b, name):
    m, k = dff.shape
    half = w.shape[0]
    tm, tn = _swiglu_tiles(m, half)

    def body(d_ref, w_ref, a_ref, b_ref, da_ref, db_ref):
        dact = lax.dot_general(d_ref[...].astype(MXU_DTYPE), w_ref[...].astype(MXU_DTYPE), (((1,), (1,)), ((), ())),
                               preferred_element_type=F32)
        av, bv = a_ref[...].astype(F32), b_ref[...].astype(F32)
        sig = jax.nn.sigmoid(av)
        da_ref[...] = (dact * bv * (sig * (1.0 + av * (1.0 - sig)))).astype(da_ref.dtype)
        db_ref[...] = (dact * (av * sig)).astype(db_ref.dtype)

    out = jax.ShapeDtypeStruct((m, half), MXU_DTYPE)
    oblk = pl.BlockSpec((tm, tn), lambda i, j: (i, j))
    return pl.pallas_call(
        body, name=name, grid=(m // tm, half // tn),
        in_specs=[pl.BlockSpec((tm, k), lambda i, j: (i, 0)), pl.BlockSpec((tn, k), lambda i, j: (j, 0)), oblk, oblk],
        out_specs=[oblk, oblk], out_shape=[out, out], compiler_params=_cparams("parallel", "parallel"),
    )(dff, w, a, b)


def _with_off(xs):
    return [x if isinstance(x, tuple) else (x, 0) for x in xs]


def _spec(kind, arr, off, ts, wb):
    w = arr.shape[-1] if wb is None else wb
    col = (lambda j: 0) if wb is None else functools.partial(lambda j, o: o + j, o=off)
    if kind == "tok":
        return pl.BlockSpec((None, ts, w), lambda j, b, i: (b, i, col(j)))
    if kind == "bat":
        return pl.BlockSpec((None, 1, w), lambda j, b, i: (b, 0, col(j)))
    if off is None:
        return pl.BlockSpec(arr.shape, lambda j, b, i: (0, 0))
    return pl.BlockSpec((arr.shape[0], w), lambda j, b, i: (0, col(j)))


class _Product:
    def __init__(self, a, b, *, tb=False, b_rows=None, add=None):
        self.a, self.b, self.tb, self.b_rows, self.add = a, b, tb, b_rows, add
        rows = b.shape[0] if b_rows is None else b_rows[1]
        self.shape = a.shape[:2] + (rows if tb else b.shape[1],)

    def inputs(self, ts):
        a_spec = pl.BlockSpec((None, ts, self.a.shape[2]), lambda j, b, i: (b, i, 0))
        if self.b_rows is None:
            b_spec = pl.BlockSpec(self.b.shape, lambda j, b, i: (0, 0))
        else:
            start, count = self.b_rows
            b_spec = pl.BlockSpec((pl.Element(count), pl.Element(self.b.shape[1])), lambda j, b, i: (start, 0))
        extra = [] if self.add is None else [(self.add, pl.BlockSpec((None, ts, self.shape[2]), lambda j, b, i: (b, i, 0)))]
        return [(self.a, a_spec), (self.b, b_spec)] + extra

    def value(self, refs):
        dims = (((1,), (1 if self.tb else 0,)), ((), ()))
        val = lax.dot_general(refs[0][...].astype(MXU_DTYPE), refs[1][...].astype(MXU_DTYPE), dims, preferred_element_type=F32)
        return val if self.add is None else val + refs[2][...].astype(F32)


def _inputs(groups, kinds, ts, wb):
    loaded = [(a, _spec(kind, a, o, ts, wb)) for g, kind in zip(groups, kinds) for a, o in g if not isinstance(a, _Product)]
    made = [pair for g in groups for a, _ in g if isinstance(a, _Product) for pair in a.inputs(ts)]
    return [a for a, _ in loaded + made], [sp for _, sp in loaded + made]


def _values(refs, groups):
    n_loaded = sum(1 for g in groups for a, _ in g if not isinstance(a, _Product))
    loaded, pos, out = iter(refs[:n_loaded]), n_loaded, []
    for g in groups:
        vals = []
        for a, _ in g:
            if isinstance(a, _Product):
                k = 2 if a.add is None else 3
                vals.append(a.value(refs[pos:pos + k]))
                pos += k
            else:
                vals.append(next(loaded)[...].astype(F32))
        out.append(vals)
    return out, pos


def _tok_fwd(fn, toks, bats, pars, outs, *, name, ts, wb=None, cols=1):
    groups = [_with_off(toks), _with_off(bats), _with_off(pars)]
    bl, s, _ = groups[0][0][0].shape
    ts = min(ts, s)
    args, in_specs = _inputs(groups, ("tok", "bat", "par"), ts, wb)

    def body(*refs):
        vals, n_in = _values(refs, groups)
        res = fn(*[v for g in vals for v in g])
        for r, val in zip(refs[n_in:], res):
            r[...] = val.astype(r.dtype)

    out_specs = [pl.BlockSpec((None, ts, w if wb is None else wb), lambda j, b, i: (b, i, j)) for w, _ in outs]
    return pl.pallas_call(
        body, name=name, grid=(cols, bl, s // ts), in_specs=in_specs,
        out_specs=out_specs, out_shape=[jax.ShapeDtypeStruct((bl, s, w), dt) for w, dt in outs],
        compiler_params=_cparams("parallel", "parallel", "parallel"),
    )(*args)


def _accumulate(ref, val, first):
    @pl.when(first)
    def _():
        ref[...] = val

    @pl.when(jnp.logical_not(first))
    def _():
        ref[...] += val


def _tok_bwd(fn, toks, bats, pars, cots, need, *, name, ts, wb=None, cols=1, tok_dtype=F32, loss=False, after=()):
    toks, bats, pars, cots = _with_off(toks), _with_off(bats), _with_off(pars), _with_off(cots)
    groups = [toks, bats, pars, cots]
    bl, s, _ = toks[0][0].shape
    ts = min(ts, s)
    nt, nb, npar = len(toks), len(bats), len(pars)
    args, in_specs = _inputs(groups, ("tok", "bat", "par", "tok"), ts, wb)
    args, in_specs = args + list(after), in_specs + [pl.BlockSpec(memory_space=pl.ANY)] * len(after)

    def body(*refs):
        j, b, i = pl.program_id(0), pl.program_id(1), pl.program_id(2)
        (tok_vals, bat_vals, par_vals, cot_vals), o = _values(refs, groups)
        o += len(after)
        outs, vjp = jax.vjp(fn, *tok_vals, *bat_vals, *par_vals)
        if loss:
            ct = (jnp.ones_like(outs[0]),)
            tot = jnp.broadcast_to(jnp.sum(outs[0], keepdims=True), (1, LANE))
            _accumulate(refs[o], tot, jnp.logical_and(b == 0, i == 0))
            o += 1
        else:
            ct = tuple(cot_vals)
        grads = vjp(ct)
        for t in range(nt):
            if need[t]:
                refs[o][...] = grads[t].astype(refs[o].dtype)
                o += 1
        for t in range(nb):
            _accumulate(refs[o], grads[nt + t], i == 0)
            o += 1
        for t in range(npar):
            first = jnp.logical_and(b == 0, i == 0)
            if pars[t][1] is None:
                first = jnp.logical_and(first, j == 0)
            _accumulate(refs[o], grads[nt + nb + t], first)
            o += 1

    full = lambda arr: arr.shape[-1] if wb is None else wb * cols
    blk = lambda arr: arr.shape[-1] if wb is None else wb
    out_specs, out_shape = [], []
    if loss:
        out_specs.append(pl.BlockSpec((1, LANE), lambda j, b, i: (0, 0)))
        out_shape.append(jax.ShapeDtypeStruct((1, LANE), F32))
    for t in range(nt):
        if need[t]:
            out_specs.append(pl.BlockSpec((None, ts, blk(toks[t][0])), lambda j, b, i: (b, i, j)))
            dt = tok_dtype[t] if isinstance(tok_dtype, (list, tuple)) else tok_dtype
            out_shape.append(jax.ShapeDtypeStruct((bl, s, full(toks[t][0])), dt))
    for arr, _ in bats:
        out_specs.append(pl.BlockSpec((None, 1, blk(arr)), lambda j, b, i: (b, 0, j)))
        out_shape.append(jax.ShapeDtypeStruct((bl, 1, full(arr)), F32))
    for arr, off in pars:
        if off is None:
            out_specs.append(pl.BlockSpec(arr.shape, lambda j, b, i: (0, 0)))
            out_shape.append(jax.ShapeDtypeStruct(arr.shape, F32))
        else:
            out_specs.append(pl.BlockSpec((arr.shape[0], blk(arr)), lambda j, b, i: (0, j)))
            out_shape.append(jax.ShapeDtypeStruct((arr.shape[0], full(arr)), F32))
    res = list(pl.pallas_call(
        body, name=name, grid=(cols, bl, s // ts), in_specs=in_specs,
        out_specs=out_specs, out_shape=out_shape, compiler_params=_cparams("arbitrary", "arbitrary", "arbitrary"),
    )(*args))
    tot = res.pop(0) if loss else None
    dtoks = [res.pop(0) if need[t] else None for t in range(nt)]
    dbats = [res.pop(0) for _ in range(nb)]
    dpars = [res.pop(0) for _ in range(npar)]
    return (tot, dtoks, dbats, dpars) if loss else (dtoks, dbats, dpars)


def _silu(x):
    return x * jax.nn.sigmoid(x)


def _rms(x, w):
    return x * lax.rsqrt(jnp.mean(x * x, axis=-1, keepdims=True) + EPS) * w


def _f_norm_mod(x, shift, scale, w):
    return (_rms(x, w) * (1.0 + scale) + shift,)


def _f_norm_mod_skip(x, shift, scale, w):
    return _rms(x, w) * (1.0 + scale) + shift, x


def _f_res_norm_mod(x, mix, gate, shift, scale, w):
    x2 = x + gate * mix
    return x2, _rms(x2, w) * (1.0 + scale) + shift


def _f_res_norm_mod_keep(x, mix, gate, shift, scale, w):
    return (*_f_res_norm_mod(x, mix, gate, shift, scale, w), mix)


def _f_gates(p, a_log, dt_bias, *, heads):
    z = p + dt_bias
    g = -jnp.exp(a_log) * (jnp.maximum(z, 0.0) + jnp.log1p(jnp.exp(jnp.minimum(z, -z))))
    lane = lax.broadcasted_iota(jnp.int32, p.shape, 1)
    return (jnp.where(lane < heads, g, jax.nn.sigmoid(p)),)


def _f_gdn_out(o, z, w):
    return (_rms(o, w) * _silu(z),)


def _f_merge(ga, gb, ya, yb):
    return (jax.nn.sigmoid(ga) * ya + jax.nn.sigmoid(gb) * yb,)


def _f_merge_keep(ga, gb, ya, yb):
    return (*_f_merge(ga, gb, ya, yb), yb)


def _f_loss(x2, ff, tgt, gate, shift, scale, w):
    y = _rms(x2 + gate * ff, w) * (1.0 + scale) + shift
    return (0.5 * jnp.mean(jnp.square(y - tgt), axis=-1, keepdims=True),)


def _shift_down(x, s):
    if s == 0:
        return x
    row = lax.broadcasted_iota(jnp.int32, x.shape, 0)
    return jnp.where(row >= s, pltpu.roll(x, s, 0), 0.0)


def _shift_up(x, s):
    if s == 0:
        return x
    n = x.shape[0]
    row = lax.broadcasted_iota(jnp.int32, x.shape, 0)
    return jnp.where(row < n - s, pltpu.roll(x, n - s, 0), 0.0)


def _conv(x, w):
    width = w.shape[0]
    acc = w[width - 1:width, :] * x
    for j in range(width - 1):
        acc = acc + w[j:j + 1, :] * _shift_down(x, width - 1 - j)
    return acc


def _conv_bwd(dy, x, w, dw_ref, first):
    width = w.shape[0]
    dx = w[width - 1:width, :] * dy
    for j in range(width - 1):
        dx = dx + w[j:j + 1, :] * _shift_up(dy, width - 1 - j)
    for j in range(width):
        row = jnp.sum(dy * _shift_down(x, width - 1 - j), axis=0, keepdims=True)
        _accumulate(dw_ref.at[j:j + 1, :], row, first)
    return dx


def _qkv_act(xc, is_v, scale):
    a = _silu(xc)
    nrm = a * lax.rsqrt(jnp.sum(a * a, axis=-1, keepdims=True) + EPS) * scale
    return jnp.where(is_v, a, nrm)


def _qkv_consts(j, heads):
    is_v = j >= 2 * heads
    scale = jnp.where(j < heads, HEAD ** -0.5, 1.0).astype(F32)
    return is_v, scale


def _qkv_fwd(p, w, heads, name):
    bl, s, w3 = p.shape

    def body(p_ref, w_ref, o_ref):
        is_v, scale = _qkv_consts(pl.program_id(0), heads)
        o_ref[...] = _qkv_act(_conv(p_ref[...], w_ref[...]), is_v, scale)

    blk = pl.BlockSpec((None, s, HEAD), lambda j, b: (b, 0, j))
    return pl.pallas_call(
        body, name=name, grid=(w3 // HEAD, bl), in_specs=[blk, pl.BlockSpec((w.shape[0], HEAD), lambda j, b: (0, j))],
        out_specs=blk, out_shape=jax.ShapeDtypeStruct(p.shape, F32), compiler_params=_cparams("parallel", "parallel"),
    )(p, w)


def _qkv_bwd(p, w, dout, heads, name):
    bl, s, w3 = p.shape

    def body(p_ref, w_ref, d_ref, dp_ref, dw_ref):
        is_v, scale = _qkv_consts(pl.program_id(0), heads)
        x, wv = p_ref[...], w_ref[...]
        _, vjp = jax.vjp(lambda xc: _qkv_act(xc, is_v, scale), _conv(x, wv))
        (dxc,) = vjp(d_ref[...])
        dp_ref[...] = _conv_bwd(dxc, x, wv, dw_ref, pl.program_id(1) == 0).astype(dp_ref.dtype)

    blk = pl.BlockSpec((None, s, HEAD), lambda j, b: (b, 0, j))
    wblk = pl.BlockSpec((w.shape[0], HEAD), lambda j, b: (0, j))
    return pl.pallas_call(
        body, name=name, grid=(w3 // HEAD, bl), in_specs=[blk, wblk, blk], out_specs=[blk, wblk],
        out_shape=[jax.ShapeDtypeStruct(p.shape, MXU_DTYPE), jax.ShapeDtypeStruct(w.shape, F32)],
        compiler_params=_cparams("arbitrary", "arbitrary"),
    )(p, w, dout)


def _sc_specs(p, w):
    bl, s, w3 = p.shape
    nblk = w3 // 3 // LANE
    sec = lambda k: pl.BlockSpec((None, s, LANE), functools.partial(lambda j, b, k: (b, 0, k * nblk + j), k=k))
    return nblk, [sec(0), sec(1), sec(2)], pl.BlockSpec((w.shape[0], LANE), lambda j, b: (0, j)), \
        pl.BlockSpec((None, s, LANE), lambda j, b: (b, 0, j))


def _sc_fwd(p, w, name):
    bl, s, w3 = p.shape
    nblk, secs, wblk, oblk = _sc_specs(p, w)

    def body(b_ref, c_ref, x_ref, w_ref, o_ref):
        o_ref[...] = (b_ref[...] * _conv(c_ref[...] * x_ref[...], w_ref[...])).astype(o_ref.dtype)

    return pl.pallas_call(
        body, name=name, grid=(nblk, bl), in_specs=secs + [wblk], out_specs=oblk,
        out_shape=jax.ShapeDtypeStruct((bl, s, w3 // 3), MXU_DTYPE), compiler_params=_cparams("parallel", "parallel"),
    )(p, p, p, w)


def _sc_bwd(p, w, dout, name):
    bl, s, w3 = p.shape
    nblk, secs, wblk, oblk = _sc_specs(p, w)

    def body(b_ref, c_ref, x_ref, w_ref, d_ref, db_ref, dc_ref, dx_ref, dw_ref):
        gb, gc, xin, wv, d = b_ref[...], c_ref[...], x_ref[...], w_ref[...], d_ref[...]
        u = gc * xin
        db_ref[...] = (d * _conv(u, wv)).astype(db_ref.dtype)
        du = _conv_bwd(d * gb, u, wv, dw_ref, pl.program_id(1) == 0)
        dc_ref[...] = (du * xin).astype(dc_ref.dtype)
        dx_ref[...] = (du * gc).astype(dx_ref.dtype)

    act = jax.ShapeDtypeStruct((bl, s, w3 // 3), MXU_DTYPE)
    return pl.pallas_call(
        body, name=name, grid=(nblk, bl), in_specs=secs + [wblk, oblk], out_specs=[oblk, oblk, oblk, wblk],
        out_shape=[act, act, act, jax.ShapeDtypeStruct(w.shape, F32)], compiler_params=_cparams("arbitrary", "arbitrary"),
    )(p, p, p, w, dout)


def _bdot(a, b, ca, cb):
    return lax.dot_general(a.astype(MXU_DTYPE), b.astype(MXU_DTYPE), (((ca,), (cb,)), ((), ())),
                           preferred_element_type=F32)


def _hdot(a, b):
    return lax.dot_general(a, b, (((1,), (0,)), ((), ())), precision=HIGHEST, preferred_element_type=F32)


def _lane_col(x, idx):
    lane = lax.broadcasted_iota(jnp.int32, x.shape, 1)
    return jnp.sum(jnp.where(lane == idx, x, 0.0), axis=1, keepdims=True)


def _chunk_masks():
    r = lax.broadcasted_iota(jnp.int32, (CHUNK, CHUNK), 0)
    c = lax.broadcasted_iota(jnp.int32, (CHUNK, CHUNK), 1)
    return r == c, r >= c, r > c


def _dot3(a, b):
    ah, bh = a.astype(MXU_DTYPE), b.astype(MXU_DTYPE)
    al, bl = (a - ah.astype(F32)).astype(MXU_DTYPE), (b - bh.astype(F32)).astype(MXU_DTYPE)
    dot = lambda x, y: lax.dot_general(x, y, (((1,), (0,)), ((), ())), preferred_element_type=F32)
    return dot(ah, bh) + (dot(ah, bl) + dot(al, bh))


def _tri_inv_steps(low, eye):
    x = -low
    p = jnp.where(eye, 1.0, 0.0) + x
    span = 2
    while span < CHUNK:
        x = _dot3(x, x)
        yield
        p = p + _dot3(p, x)
        yield
        span *= 2
    return p


def _round_robin(gens):
    out, live = [None] * len(gens), list(range(len(gens)))
    while live:
        still = []
        for i in live:
            try:
                next(gens[i])
                still.append(i)
            except StopIteration as stop:
                out[i] = stop.value
        live = still
    return out


def _gdn_pre(q, k, v, gc, beta, masks):
    eye, causal, strict = masks
    gc_row = jnp.sum(jnp.where(eye, gc, 0.0), axis=0, keepdims=True)
    decay = jnp.where(causal, jnp.exp(jnp.where(causal, gc - gc_row, 0.0)), 0.0)
    eg = jnp.exp(gc)
    gl = gc[CHUNK - 1:CHUNK, :]
    kb, vb = k * beta, v * beta
    both = _bdot(jnp.concatenate([kb, q], axis=0), k, 1, 1)
    low = jnp.where(strict, both[:CHUNK] * decay, 0.0)
    qk = jnp.where(causal, both[CHUNK:] * decay, 0.0)
    rest = jnp.exp(gl - gc)
    return dict(decay=decay, eg=eg, gl=gl, kb=kb, vb=vb, kbe=kb * eg, low=low, qk=qk, qg=q * eg, rest=rest, kdec=k * rest)


def _gdn_specs(qkv, gbeta, heads, rev):
    bl, s, w3 = qkv.shape
    d, n = w3 // 3, s // CHUNK
    at = (lambda c: n - 1 - c) if rev else (lambda c: c)
    assert d == heads * HEAD
    sec = pl.BlockSpec((None, CHUNK, w3), lambda b, c: (b, at(c), 0))
    gspec = pl.BlockSpec((None, CHUNK, LANE), lambda b, c: (b, at(c), 0))
    sspec = pl.BlockSpec((None, None, heads, HEAD, HEAD), lambda b, c: (b, at(c), 0, 0, 0))
    tspec = pl.BlockSpec((None, None, heads, CHUNK, CHUNK), lambda b, c: (b, at(c), 0, 0, 0))
    return bl, s, d, n, sec, gspec, sspec, tspec


def _gdn_fwd(qkv, gbeta, heads, name):
    bl, s, d, n, sec, gspec, sspec, tspec = _gdn_specs(qkv, gbeta, heads, False)

    def body(x_ref, g_ref, o_ref, s_ref, t_ref, st_ref):
        @pl.when(pl.program_id(1) == 0)
        def _():
            st_ref[...] = jnp.zeros_like(st_ref)

        masks = _chunk_masks()
        eye, causal, _ = masks
        gblk = g_ref[...]
        gc_all = _hdot(jnp.where(causal, 1.0, 0.0), gblk)
        st_all = st_ref[...]

        def head(h):
            st = st_all[h]
            q, k, v = (x_ref[:, sec * d + h * HEAD:sec * d + (h + 1) * HEAD] for sec in range(3))
            pre = _gdn_pre(q, k, v, _lane_col(gc_all, h), _lane_col(gblk, heads + h), masks)
            yield
            t = yield from _tri_inv_steps(pre["low"], eye)
            uw = _bdot(t, jnp.concatenate([pre["vb"], pre["kbe"]], axis=1), 1, 0)
            u, w = uw[:, :HEAD], uw[:, HEAD:]
            yield
            vnew = u - _bdot(w, st, 1, 0)
            yield
            out = _bdot(pre["qg"], st, 1, 0) + _bdot(pre["qk"], vnew, 1, 0)
            return out, t, st * jnp.exp(pre["gl"]) + _bdot(pre["kdec"], vnew, 0, 0)

        outs, ts, states = zip(*_round_robin([head(h) for h in range(heads)]))
        o_ref[...] = jnp.concatenate(outs, axis=1)
        s_ref[...] = st_all
        t_ref[...] = jnp.stack(ts)
        st_ref[...] = jnp.stack(states)

    return pl.pallas_call(
        body, name=name, grid=(bl, n), in_specs=[sec, gspec],
        out_specs=[pl.BlockSpec((None, CHUNK, d), lambda b, c: (b, c, 0)), sspec, tspec],
        out_shape=[jax.ShapeDtypeStruct((bl, s, d), F32), jax.ShapeDtypeStruct((bl, n, heads, HEAD, HEAD), F32),
                   jax.ShapeDtypeStruct((bl, n, heads, CHUNK, CHUNK), F32)],
        scratch_shapes=[pltpu.VMEM((heads, HEAD, HEAD), F32)], compiler_params=_cparams("parallel", "arbitrary"),
    )(qkv, gbeta)


def _gdn_bwd(qkv, gbeta, dout, s_all, t_all, heads, name):
    bl, s, d, n, sec, gspec, sspec, tspec = _gdn_specs(qkv, gbeta, heads, True)
    ospec = pl.BlockSpec((None, CHUNK, d), lambda b, c: (b, n - 1 - c, 0))

    def body(x_ref, g_ref, do_ref, s_ref, t_ref, dx_ref, dg_ref, ds_ref):
        @pl.when(pl.program_id(1) == 0)
        def _():
            ds_ref[...] = jnp.zeros_like(ds_ref)

        masks = _chunk_masks()
        eye, causal, strict = masks
        gblk = g_ref[...]
        gc_all = _hdot(jnp.where(causal, 1.0, 0.0), gblk)
        lane = lax.broadcasted_iota(jnp.int32, gblk.shape, 1)
        last_row = lax.broadcasted_iota(jnp.int32, (CHUNK, 1), 0) == CHUNK - 1
        rowsum = lambda a: jnp.sum(a, axis=1, keepdims=True)
        st_all, t_all_, ds_all = s_ref[...], t_ref[...], ds_ref[...]

        def head(h):
            sl = slice(h * HEAD, (h + 1) * HEAD)
            q, k, v = (x_ref[:, sec * d + h * HEAD:sec * d + (h + 1) * HEAD] for sec in range(3))
            do = do_ref[:, sl]
            beta = _lane_col(gblk, heads + h)
            st, t, dsn = st_all[h], t_all_[h], ds_all[h]
            pre = _gdn_pre(q, k, v, _lane_col(gc_all, h), beta, masks)
            decay, eg, kb, vb, kbe, low, qk, qg, kdec = (pre[x] for x in ("decay", "eg", "kb", "vb", "kbe", "low", "qk", "qg", "kdec"))
            egl = jnp.exp(pre["gl"])
            yield
            uw = _bdot(t, jnp.concatenate([vb, kbe], axis=1), 1, 0)
            u, w = uw[:, :HEAD], uw[:, HEAD:]
            yield
            vnew = u - _bdot(w, st, 1, 0)
            yield
            stack, side = functools.partial(jnp.concatenate, axis=0), functools.partial(jnp.concatenate, axis=1)
            dkdec = _bdot(vnew, dsn, 1, 1)
            dvnew = _bdot(kdec, dsn, 1, 0) + _bdot(qk, do, 0, 0)
            dgl = jnp.sum(dsn * st, keepdims=True) * egl
            dqk = jnp.where(causal, _bdot(do, vnew, 1, 1), 0.0)
            yield
            by_state = _bdot(stack([do, dvnew]), st, 1, 1)
            dqg, dw = by_state[:CHUNK], -by_state[CHUNK:]
            ds_new = dsn * egl + _bdot(stack([qg, -w]), stack([do, dvnew]), 0, 0)
            yield
            dt = _bdot(side([dvnew, dw]), side([vb, kbe]), 1, 1)
            by_t = _bdot(t, side([dvnew, dw]), 0, 0)
            dvb, dkbe = by_t[:, :HEAD], by_t[:, HEAD:]
            yield
            inner = _bdot(dt, t, 1, 1)
            yield
            dlow = -jnp.where(strict, _bdot(t, inner, 0, 0), 0.0)
            da, db = dlow * decay, dqk * decay
            yield
            m = dlow * low + dqk * qk
            kdk = dkdec * kdec
            col_of_m = jnp.sum(jnp.where(eye, jnp.sum(m, axis=0, keepdims=True), 0.0), axis=1, keepdims=True)
            dgc = rowsum(m) - col_of_m + rowsum(dqg * qg) + rowsum(dkbe * kbe) - rowsum(kdk)
            dgc = dgc + jnp.where(last_row, dgl + jnp.sum(kdk, keepdims=True), 0.0)
            by_k = _bdot(stack([da, db]), k, 1, 0)
            dkb = by_k[:CHUNK] + dkbe * eg
            yield
            dk = _bdot(stack([da, db]), stack([kb, q]), 0, 0) + dkdec * pre["rest"] + dkb * beta
            dq = by_k[CHUNK:] + dqg * eg
            dbeta = rowsum(dkb * k) + rowsum(dvb * v)
            return dq, dk, dvb * beta, jnp.where(lane == h, dgc, 0.0) + jnp.where(lane == heads + h, dbeta, 0.0), ds_new

        dqs, dks, dvs, dgs, dss = zip(*_round_robin([head(h) for h in range(heads)]))
        dx_ref[...] = jnp.concatenate(dqs + dks + dvs, axis=1)
        ds_ref[...] = jnp.stack(dss)
        dgb = dgs[0]
        for extra in dgs[1:]:
            dgb = dgb + extra
        upper = jnp.where(jnp.logical_or(eye, jnp.logical_not(causal)), 1.0, 0.0)
        dg_ref[...] = jnp.where(lane < heads, _hdot(upper, dgb), dgb)

    return pl.pallas_call(
        body, name=name, grid=(bl, n), in_specs=[sec, gspec, ospec, sspec, tspec], out_specs=[sec, gspec],
        out_shape=[jax.ShapeDtypeStruct(qkv.shape, F32), jax.ShapeDtypeStruct((bl, s, LANE), F32)],
        scratch_shapes=[pltpu.VMEM((heads, HEAD, HEAD), F32)], compiler_params=_cparams("parallel", "arbitrary"),
    )(qkv, gbeta, dout, s_all, t_all)


def _position():
    return lax.axis_index("x"), lax.axis_index("y"), lax.axis_index("c")


def _all_gather(x, *, name, hbm):
    space = pltpu.HBM if hbm else pltpu.VMEM

    def body(x_ref, out_ref, send_sems, recv_sems, local_sem):
        ax, ay, ac = _position()
        me, sibling = (ax, ay, ac), (ax, ay, 1 - ac)
        chips = [(1 - ax, ay), (ax, 1 - ay), (1 - ax, 1 - ay)]

        def slot(px, py, pc):
            return out_ref.at[4 * px + 2 * py + pc]

        def copy(k, block, to, src=None):
            return pltpu.make_async_remote_copy(
                src_ref=slot(*block) if src is None else src, dst_ref=slot(*block), send_sem=send_sems.at[k],
                recv_sem=recv_sems.at[k], device_id=to, device_id_type=MESH_IDS)

        mine = pltpu.make_async_copy(x_ref, slot(*me), local_sem)
        mine.start()
        first = [copy(0, me, sibling, src=x_ref)] + [copy(1 + j, me, (*chip, ac), src=x_ref) for j, chip in enumerate(chips)]
        for cp in first:
            cp.start()
        passed = [copy(4 + j, (*chip, ac), sibling) for j, chip in enumerate(chips)]
        for j, chip in enumerate(chips):
            copy(1 + j, (*chip, ac), me).wait_recv()
            passed[j].start()
        copy(0, sibling, me).wait_recv()
        for j, chip in enumerate(chips):
            copy(4 + j, (*chip, 1 - ac), me).wait_recv()
        for cp in first + passed:
            cp.wait_send()
        mine.wait()

    return pl.pallas_call(
        body, name=name, out_shape=jax.ShapeDtypeStruct((NDEV,) + x.shape, x.dtype),
        in_specs=[pl.BlockSpec(memory_space=space)], out_specs=pl.BlockSpec(memory_space=space),
        scratch_shapes=[pltpu.SemaphoreType.DMA((7,)), pltpu.SemaphoreType.DMA((7,)), pltpu.SemaphoreType.DMA],
    )(x)


class _Rider:
    def __init__(self, arrays, out_shapes, sems, hooks):
        self.arrays, self.out_shapes, self.sems, self.hooks = arrays, out_shapes, sems, hooks


def _gather_rider(xs):
    n = len(xs)

    def hooks(x_refs, out_refs, send_sems, recv_sems):
        ax, ay, ac = _position()
        me, sibling = (ax, ay, ac), (ax, ay, 1 - ac)
        chips = [(1 - ax, ay), (ax, 1 - ay), (1 - ax, 1 - ay)]

        def copies(k, block, to, own=False):
            out = []
            for i in range(n):
                slot = out_refs[i].at[4 * block[0] + 2 * block[1] + block[2]]
                out.append(pltpu.make_async_remote_copy(
                    src_ref=x_refs[i] if own else slot, dst_ref=slot, send_sem=send_sems.at[k, i], recv_sem=recv_sems.at[k, i],
                    device_id=to, device_id_type=MESH_IDS))
            return out

        def first():
            for cp in copies(0, me, sibling, own=True):
                cp.start()
            for j, chip in enumerate(chips):
                for cp in copies(1 + j, me, (*chip, ac), own=True):
                    cp.start()

        def mid():
            for j, chip in enumerate(chips):
                for arrived, onward in zip(copies(1 + j, (*chip, ac), me), copies(4 + j, (*chip, ac), sibling)):
                    arrived.wait_recv()
                    onward.start()

        def last():
            for cp in copies(0, sibling, me):
                cp.wait_recv()
            for j, chip in enumerate(chips):
                for cp in copies(4 + j, (*chip, 1 - ac), me):
                    cp.wait_recv()
            for cp in copies(0, me, sibling, own=True):
                cp.wait_send()
            for j, chip in enumerate(chips):
                for cp in copies(1 + j, me, (*chip, ac), own=True) + copies(4 + j, (*chip, ac), sibling):
                    cp.wait_send()

        return first, mid, last

    return _Rider(list(xs), [jax.ShapeDtypeStruct((NDEV,) + x.shape, x.dtype) for x in xs],
                  [pltpu.SemaphoreType.DMA((7, n)), pltpu.SemaphoreType.DMA((7, n))], hooks)


def _scatter_rider(parts):
    packed = sum(r for _, r in parts)
    width, dtype = parts[0][0].shape[1], parts[0][0].dtype

    def hooks(g_refs, out_refs, send_sems, recv_sems):
        (recv_ref,) = out_refs
        ax, ay, ac = _position()

        def peer(rel):
            flip = lambda a, bit: 1 - a if rel & bit else a
            return flip(ax, 4), flip(ay, 2), flip(ac, 1)

        def first():
            for rel in range(1, NDEV):
                px, py, pc = peer(rel)
                off = 0
                for g_ref, (_, r) in zip(g_refs, parts):
                    rows = g_ref.at[pl.ds(pl.multiple_of((4 * px + 2 * py + pc) * r, ROW_ALIGN), r)]
                    pltpu.make_async_remote_copy(
                        src_ref=rows, dst_ref=recv_ref.at[rel - 1, pl.ds(off, r)], send_sem=send_sems.at[rel - 1],
                        recv_sem=recv_sems.at[rel - 1], device_id=(px, py, pc), device_id_type=MESH_IDS).start()
                    off += r

        def last():
            for rel in range(1, NDEV):
                slot = recv_ref.at[rel - 1]
                pltpu.make_async_remote_copy(src_ref=slot, dst_ref=slot, send_sem=send_sems.at[rel - 1],
                                             recv_sem=recv_sems.at[rel - 1], device_id=peer(rel), device_id_type=MESH_IDS).wait()

        return first, lambda: None, last

    return _Rider([g for g, _ in parts], [jax.ShapeDtypeStruct((NDEV - 1, packed, width), dtype)],
                  [pltpu.SemaphoreType.DMA((NDEV - 1,)), pltpu.SemaphoreType.DMA((NDEV - 1,))], hooks)


def _sum_direct(own, recv, name):
    r, w = own.shape
    tr = max(t for t in range(ROW_ALIGN, 257, ROW_ALIGN) if r % t == 0)

    def body(own_ref, *refs):
        acc = own_ref[...].astype(F32)
        for ref in refs[:-1]:
            acc = acc + ref[...].astype(F32)
        refs[-1][...] = acc

    rblk = lambda k: pl.BlockSpec((None, tr, w), functools.partial(lambda i, k: (k, i, 0), k=k))
    blk = pl.BlockSpec((tr, w), lambda i: (i, 0))
    return pl.pallas_call(body, name=name, grid=(r // tr,), in_specs=[blk] + [rblk(k) for k in range(NDEV - 1)],
                          out_specs=blk, out_shape=jax.ShapeDtypeStruct((r, w), F32),
                          compiler_params=_cparams("parallel"))(own, *([recv] * (NDEV - 1)))


ROW_ALIGN = 16


def _window_start(rows_per_dev, k):
    return rows_per_dev * k // ROW_ALIGN * ROW_ALIGN


def _exchange_in_chip(parts, name, collective_id):
    packed = sum(win for _, _, win, _ in parts)
    width, dtype = parts[0][0].shape[1], parts[0][0].dtype

    def body(g_refs, out_refs, send_sems, recv_sems):
        (recv_ref,) = out_refs
        ax, ay, ac = _position()
        sibling = (ax, ay, 1 - ac)
        _handshake([sibling])
        for q in range(4):
            for g_ref, (_, r, win, off) in zip(g_refs, parts):
                there = g_ref.at[pl.ds(pl.multiple_of(_window_start(r, 2 * q + 1 - ac), ROW_ALIGN), win)]
                pltpu.make_async_remote_copy(src_ref=there, dst_ref=recv_ref.at[q, pl.ds(off, win)], send_sem=send_sems.at[q],
                                             recv_sem=recv_sems.at[q], device_id=sibling, device_id_type=MESH_IDS).start()
        for q in range(4):
            pltpu.make_async_remote_copy(src_ref=recv_ref.at[q], dst_ref=recv_ref.at[q], send_sem=send_sems.at[q],
                                         recv_sem=recv_sems.at[q], device_id=sibling, device_id_type=MESH_IDS).wait()

    return _on_sequencer(body, [g for g, _, _, _ in parts], [jax.ShapeDtypeStruct((4, packed, width), dtype)],
                         [pltpu.SemaphoreType.DMA((4,)), pltpu.SemaphoreType.DMA((4,))], name=name, collective_id=collective_id)[0]


def _on_sequencer(body, ins, out_shapes, sems, *, name, collective_id):
    hbm = pltpu.MemorySpace.HBM
    in_refs = [jax.new_ref(a, memory_space=hbm) for a in ins]
    out_refs = [jax.empty_ref(s, memory_space=hbm) for s in out_shapes]

    @pl.kernel(mesh=plsc.ScalarSubcoreMesh(axis_name="sequencer", num_cores=1), name=name, scratch_types=tuple(sems),
               compiler_params=pltpu.CompilerParams(collective_id=collective_id))
    def launch(*sem_refs):
        body(in_refs, out_refs, *sem_refs)

    launch()
    return [r[...] for r in out_refs]


def _handshake(peers):
    barrier = pltpu.get_barrier_semaphore()
    for peer in peers:
        pl.semaphore_signal(barrier, inc=1, device_id=peer, device_id_type=MESH_IDS)
    pl.semaphore_wait(barrier, len(peers))


def _exchange_chips_async(s1, name, collective_id):
    def body(in_refs, out_refs, send_sems, recv_sems):
        (src,), (got,) = in_refs, out_refs
        ax, ay, ac = _position()
        chips = [(1 - ax, ay), (ax, 1 - ay), (1 - ax, 1 - ay)]
        _handshake([(cx, cy, ac) for cx, cy in chips])
        copies = [pltpu.make_async_remote_copy(
            src_ref=src.at[2 * cx + cy], dst_ref=got.at[r], send_sem=send_sems.at[r], recv_sem=recv_sems.at[r],
            device_id=(cx, cy, ac), device_id_type=MESH_IDS) for r, (cx, cy) in enumerate(chips)]
        for cp in copies:
            cp.start()
        for cp in copies:
            cp.wait_recv()
        for cp in copies:
            cp.wait_send()

    return _on_sequencer(body, [s1], [jax.ShapeDtypeStruct((3,) + s1.shape[1:], s1.dtype)],
                         [pltpu.SemaphoreType.DMA((3,)), pltpu.SemaphoreType.DMA((3,))], name=name, collective_id=collective_id)[0]


def _gather_async(xs, name, collective_id):
    rider = _gather_rider(xs)

    def body(in_refs, out_refs, send_sems, recv_sems):
        ax, ay, ac = _position()
        _handshake([(ax, ay, 1 - ac), (1 - ax, ay, ac), (ax, 1 - ay, ac), (1 - ax, 1 - ay, ac)])
        for hook in rider.hooks(in_refs, out_refs, send_sems, recv_sems):
            hook()

    return _on_sequencer(body, rider.arrays, rider.out_shapes, rider.sems, name=name, collective_id=collective_id)


def _scatter_async(parts, name, collective_id):
    rider = _scatter_rider(parts)

    def body(in_refs, out_refs, send_sems, recv_sems):
        ax, ay, ac = _position()
        flip = lambda a, on: 1 - a if on else a
        _handshake([(flip(ax, rel & 4), flip(ay, rel & 2), flip(ac, rel & 1)) for rel in range(1, NDEV)])
        for hook in rider.hooks(in_refs, out_refs, send_sems, recv_sems):
            hook()

    return _on_sequencer(body, rider.arrays, rider.out_shapes, rider.sems, name=name, collective_id=collective_id)[0]


def _sum_in_chip(own, recv, name):
    _, r, w = own.shape
    tr = _tile(r, (256, 128))

    def body(a_ref, b_ref, o_ref):
        o_ref[...] = (a_ref[...].astype(F32) + b_ref[...].astype(F32)).astype(o_ref.dtype)

    blk = pl.BlockSpec((None, tr, w), lambda q, i: (q, i, 0))
    return pl.pallas_call(body, name=name, grid=(4, r // tr), in_specs=[blk, blk], out_specs=blk,
                          out_shape=jax.ShapeDtypeStruct(own.shape, own.dtype),
                          compiler_params=_cparams("parallel", "parallel"))(own, recv)


def _sum_chips(s1, recv, chip, name):
    _, r, w = s1.shape
    tr = _tile(r, (256, 128))

    def body(c_ref, s_ref, r0_ref, r1_ref, r2_ref, o_ref):
        f = lambda ref: ref[...].astype(F32)
        o_ref[...] = ((f(s_ref) + f(r0_ref)) + f(r1_ref)) + f(r2_ref)

    rblk = lambda k: pl.BlockSpec((None, tr, w), functools.partial(lambda i, c, k: (k, i, 0), k=k))
    grid_spec = pltpu.PrefetchScalarGridSpec(
        num_scalar_prefetch=1, grid=(r // tr,),
        in_specs=[pl.BlockSpec((None, tr, w), lambda i, c: (c[0], i, 0)), rblk(0), rblk(1), rblk(2)],
        out_specs=pl.BlockSpec((tr, w), lambda i, c: (i, 0)))
    return pl.pallas_call(body, name=name, grid_spec=grid_spec, out_shape=jax.ShapeDtypeStruct((r, w), F32),
                          compiler_params=_cparams("parallel"))(chip, s1, recv, recv, recv)


def _silu_rows(x, name):
    def body(x_ref, o_ref):
        o_ref[...] = _silu(x_ref[...])

    return pl.pallas_call(body, name=name, out_shape=jax.ShapeDtypeStruct(x.shape, F32))(x)


def _row_sum(x, name):
    def body(x_ref, o_ref):
        acc = x_ref[0:1, :]
        for i in range(1, x.shape[0]):
            acc = acc + x_ref[i:i + 1, :]
        o_ref[...] = acc

    return pl.pallas_call(body, name=name, out_shape=jax.ShapeDtypeStruct((1, x.shape[1]), F32))(x)


def _adamw(w, g, m, v, name):
    cols = w.shape[-1]
    rows = w.size // cols
    tr = _tile(rows, (128,))
    tc = LANE if (tr == rows and rows > 512 and cols % LANE == 0) else cols

    def body(w_ref, g_ref, m_ref, v_ref, d_ref, mo_ref, vo_ref):
        grad = g_ref[...]
        m_new = ADAM_B1 * m_ref[...] + (1.0 - ADAM_B1) * grad
        v_new = ADAM_B2 * v_ref[...] + (1.0 - ADAM_B2) * jnp.square(grad)
        m_hat = m_new / (1.0 - ADAM_B1 ** ADAM_STEP)
        v_hat = v_new / (1.0 - ADAM_B2 ** ADAM_STEP)
        d_ref[...] = -ADAM_LR * (m_hat / (jnp.sqrt(v_hat) + ADAM_EPS) + ADAM_WD * w_ref[...])
        mo_ref[...] = m_new
        vo_ref[...] = v_new

    blk = pl.BlockSpec((tr, tc), lambda i, j: (i, j))
    out = pl.pallas_call(
        body, name=name, grid=(rows // tr, cols // tc), in_specs=[blk] * 4, out_specs=[blk] * 3,
        out_shape=[jax.ShapeDtypeStruct((rows, cols), F32)] * 3, compiler_params=_cparams("parallel", "parallel"),
    )(*[t.reshape(rows, cols) for t in (w, g, m, v)])
    return [t.reshape(w.shape) for t in out]


def _pack(parts, width, row_mult, dtype):
    flat = jnp.concatenate([p.reshape(-1).astype(dtype) for p in parts])
    rows = -(-flat.shape[0] // (width * row_mult)) * row_mult
    return jnp.pad(flat, (0, rows * width - flat.shape[0])).reshape(rows, width)


def _unpack(flat, shapes):
    out, off = [], 0
    for shp in shapes:
        size = 1
        for dim in shp:
            size *= dim
        out.append(flat[:, off:off + size].reshape((flat.shape[0],) + tuple(shp)))
        off += size
    return out


def _devices_to_cols(a):
    _, r, c = a.shape
    return a.transpose(1, 0, 2).reshape(r, NDEV * c)


def kernel(x, c, w_ada, b_ada, norm1_w, w_in, gdn_conv_w, gdn_a_log, gdn_dt_bias, gdn_norm_w, w_gdn_proj, sc_conv_w, w_sc_out, w_o, norm2_w, w_ffn_in, w_ffn_out, w_ada_f, b_ada_f, normf_w, loss_target, m_w_ada, m_b_ada, m_norm1_w, m_w_in, m_gdn_conv_w, m_gdn_a_log, m_gdn_dt_bias, m_gdn_norm_w, m_w_gdn_proj, m_sc_conv_w, m_w_sc_out, m_w_o, m_norm2_w, m_w_ffn_in, m_w_ffn_out, m_w_ada_f, m_b_ada_f, m_normf_w, v_w_ada, v_b_ada, v_norm1_w, v_w_in, v_gdn_conv_w, v_gdn_a_log, v_gdn_dt_bias, v_gdn_norm_w, v_w_gdn_proj, v_sc_conv_w, v_w_sc_out, v_w_o, v_norm2_w, v_w_ffn_in, v_w_ffn_out, v_w_ada_f, v_b_ada_f, v_normf_w):
    bl, s, d = x.shape
    heads = gdn_a_log.shape[-1]
    dff = w_ffn_out.shape[1] * NDEV
    tok = bl * s
    ax, ay, ac = _position()
    dev = 4 * ax + 2 * ay + ac
    as_tok = lambda a: a.reshape(bl, s, a.shape[-1])
    as_mat = lambda a: a.reshape(tok, a.shape[-1])

    small = _all_gather(_pack([c, gdn_conv_w, sc_conv_w], LANE, 8, F32), name="gather_cond", hbm=False)
    c_all, conv_w, sc_w = _unpack(small.reshape(NDEV, -1), [(bl, d), gdn_conv_w.shape[1:], sc_conv_w.shape[1:]])
    c_act = _silu_rows(c_all.reshape(NDEV * bl, d), "cond_silu")
    conv_w, sc_w = _devices_to_cols(conv_w), _devices_to_cols(sc_w)
    n_ada, n_adaf = w_ada.shape[-1], w_ada_f.shape[-1]
    bias = jnp.broadcast_to(lax.dynamic_slice_in_dim(b_ada, dev * n_ada, n_ada, axis=1), (NDEV * bl, n_ada))
    biasf = jnp.broadcast_to(lax.dynamic_slice_in_dim(b_ada_f.reshape(1, -1), dev * n_adaf, n_adaf, axis=1), (NDEV * bl, n_adaf))
    mod_cols = _mm(c_act, w_ada[0], add=bias, name="ada_cols")
    modf_cols = _mm(c_act, w_ada_f, add=biasf, name="adaf_cols")
    mods = _all_gather(jnp.concatenate([mod_cols, modf_cols], axis=1), name="gather_mod", hbm=False)
    mod_all = mods[:, :, :n_ada].transpose(1, 0, 2).reshape(NDEV * bl, NDEV * n_ada)
    modf_all = mods[:, :, n_ada:].transpose(1, 0, 2).reshape(NDEV * bl, NDEV * n_adaf)
    my_rows = lambda a: lax.dynamic_slice_in_dim(a, dev * bl, bl, axis=0)
    sh1, sc1, g1, sh2, sc2, g2 = [t.reshape(bl, 1, d) for t in jnp.split(my_rows(mod_all), 6, axis=1)]
    shf, scf = [t.reshape(bl, 1, d) for t in jnp.split(my_rows(modf_all), 2, axis=1)]

    late = [t.astype(MXU_DTYPE) for t in (w_gdn_proj[0], w_sc_out[0], w_o[0], w_ffn_in[0].T, w_ffn_out[0])]
    rows = [t.shape[0] for t in late] + [w_in.shape[-1]]
    offs = [sum(rows[:i]) for i in range(5)]
    in_send = w_in[0].T.astype(MXU_DTYPE)
    with_own = lambda g, own: lax.dynamic_update_slice_in_dim(g, own[None], dev, axis=0)
    (wt_in,) = _gather_async([in_send], "gather_w_in", 1)
    wt_in = with_own(wt_in, in_send).reshape(NDEV * rows[5], d)
    gathered = _gather_async(late[:3], "gather_mixer", 2) + _gather_async(late[3:], "gather_ffn", 3)
    wgp, wso, wo, wt_fi, wfo = [with_own(g, own).reshape(NDEV * own.shape[0], d) for g, own in zip(gathered, late)]
    o_z, o_ab, o_sc, o_ga, o_gb = 3 * d, 4 * d, 4 * d + 2 * heads, 7 * d + 2 * heads, 8 * d + 2 * heads
    s_qkv, s_z, s_sc, s_gate = (0, o_z), (o_z, d), (o_sc, 3 * d), (o_ga, 2 * d)
    wt_ab = jnp.pad(wt_in[o_ab:o_sc], ((0, LANE - 2 * heads), (0, 0)))

    n1w, n2w, nfw = norm1_w.reshape(1, d), norm2_w.reshape(1, d), normf_w.reshape(1, d)
    lanes = lambda a: jnp.pad(a.reshape(1, -1), ((0, 0), (0, LANE - a.size)))
    a_log, dt_bias, gnw = lanes(gdn_a_log), lanes(gdn_dt_bias), gdn_norm_w.reshape(1, HEAD)
    f_gates = functools.partial(_f_gates, heads=heads)
    (h1,) = _tok_fwd(_f_norm_mod, [x], [sh1, sc1], [n1w], [(d, MXU_DTYPE)], name="norm1", ts=512)
    h1m = as_mat(h1)
    p_qkv = as_tok(_mm(h1m, wt_in, tb=True, b_rows=s_qkv, name="in_qkv"))
    p_z = as_tok(_mm(h1m, wt_in, tb=True, b_rows=s_z, name="in_z"))
    p_ab = as_tok(_mm(h1m, wt_ab, tb=True, name="in_ab"))
    p_sc = as_tok(_mm(h1m, wt_in, tb=True, b_rows=s_sc, name="in_sc"))
    p_g = as_tok(_mm(h1m, wt_in, tb=True, b_rows=s_gate, name="in_gate"))
    qkv = _qkv_fwd(p_qkv, conv_w, heads, "qkv_conv")
    (gbeta,) = _tok_fwd(f_gates, [p_ab], [], [a_log, dt_bias], [(LANE, F32)], name="gates", ts=512)
    o, s_all, t_all = _gdn_fwd(qkv, gbeta, heads, "gdn")
    (og,) = _tok_fwd(_f_gdn_out, [o, p_z], [], [(gnw, None)], [(d, MXU_DTYPE)], name="gdn_out", ts=2048, wb=HEAD, cols=heads)
    y_a = as_tok(_mm(as_mat(og), wgp, name="gdn_proj"))
    scp = _sc_fwd(p_sc, sc_w, "sc_conv")
    mrg, y_b = _tok_fwd(_f_merge_keep, [(p_g, 0), (p_g, 1), y_a, _Product(scp, wso)], [], [], [(d, MXU_DTYPE), (d, F32)],
                        name="merge", ts=256, wb=d)
    merge_toks = [(p_g, 0), (p_g, 1), y_a, y_b]
    x2, h2, mix = _tok_fwd(_f_res_norm_mod_keep, [x, _Product(mrg, wo)], [g1, sh2, sc2], [n2w],
                           [(d, F32), (d, MXU_DTYPE), (d, F32)], name="norm2", ts=512)
    act, gu_a, gu_b = _ffn_in_swiglu(as_mat(h2), wt_fi, dff, "ffn_in")

    loss_l, (dx2, dff_out, _), (dg2, dshf, dscf), (dnfw,) = _tok_bwd(
        _f_loss, [x2, _Product(as_tok(act), wfo), loss_target], [g2, shf, scf], [nfw], [], [True, True, False], name="loss",
        ts=256, loss=True, tok_dtype=[F32, MXU_DTYPE, None])
    dffm = as_mat(dff_out)
    dgu_a, dgu_b = _ffn_out_bwd_swiglu(dffm, wfo, gu_a, gu_b, "d_ffn_out")
    gmm = functools.partial(_mm, ta=True, out_dtype=MXU_DTYPE)
    gw_ffn_out = gmm(act, dffm, name="g_ffn_out")
    dh2 = _Product(as_tok(dgu_b), wt_fi, b_rows=(dff, dff), add=as_tok(_mm(dgu_a, wt_fi, b_rows=(0, dff), name="d_ffn_in_a")))
    h2m = as_mat(h2)
    gwt_ffn_in = gmm(dgu_a, h2m, out_rows=2 * dff, name="g_ffn_in_a")
    gwt_ffn_in = gmm(dgu_b, h2m, out_rows=2 * dff, row_off=dff, into=gwt_ffn_in, name="g_ffn_in_b")
    ffn_parts = [(gwt_ffn_in, rows[3]), (gw_ffn_out, rows[4])]
    ffn_recv = _scatter_async(ffn_parts, "scatter_ffn", 4)
    (dx_skip, dmix), (dg1, dsh2, dsc2), (dn2w,) = _tok_bwd(
        _f_res_norm_mod, [x, mix], [g1, sh2, sc2], [n2w], [dx2, dh2], [True, True], name="d_norm2", ts=256,
        tok_dtype=[F32, MXU_DTYPE], after=[gwt_ffn_in, gw_ffn_out])
    gw_o = gmm(as_mat(mrg), as_mat(dmix), name="g_mix_out")
    (dga, dgb, dya, dyb), _, _ = _tok_bwd(_f_merge, merge_toks, [], [], [_Product(dmix, wo, tb=True)], [True] * 4,
                                          name="d_merge", ts=256, wb=d, tok_dtype=MXU_DTYPE)
    dyam, dybm = as_mat(dya), as_mat(dyb)
    dog = as_tok(_mm(dyam, wgp, tb=True, name="d_gdn_proj"))
    gw_gdn_proj = gmm(as_mat(og), dyam, name="g_gdn_proj")
    dscp = as_tok(_mm(dybm, wso, tb=True, name="d_sc_out"))
    gw_sc_out = gmm(as_mat(scp), dybm, name="g_sc_out")
    dscb, dscc, dscx, g_sc_w = _sc_bwd(p_sc, sc_w, dscp, "d_sc_conv")
    mix_parts = [(gw_gdn_proj, rows[0]), (gw_sc_out, rows[1]), (gw_o, rows[2])]
    mix_recv = _scatter_async(mix_parts, "scatter_mixer", 5)
    (do, dz), _, (g_gnw,) = _tok_bwd(_f_gdn_out, [o, p_z], [], [(gnw, None)], [dog], [True, True], name="d_gdn_out",
                                     ts=2048, wb=HEAD, cols=heads, tok_dtype=[F32, MXU_DTYPE],
                                     after=[gw_gdn_proj, gw_sc_out, gw_o])
    own_rows = lambda parts: jnp.concatenate([lax.dynamic_slice_in_dim(g, dev * r, r, axis=0) for g, r in parts], axis=0)
    dqkv, dgbeta = _gdn_bwd(qkv, gbeta, do, s_all, t_all, heads, "d_gdn")
    dp_qkv, g_conv_w = _qkv_bwd(p_qkv, conv_w, dqkv, heads, "d_qkv_conv")
    ffn_red = _sum_direct(own_rows(ffn_parts), ffn_recv, "sum_ffn")
    mix_red = _sum_direct(own_rows(mix_parts), mix_recv, "sum_mix")
    (dp_ab,), _, (g_a_log, g_dt_bias) = _tok_bwd(f_gates, [p_ab], [], [a_log, dt_bias], [dgbeta], [True], name="d_gates",
                                                 ts=512, tok_dtype=MXU_DTYPE, after=[ffn_red, mix_red])
    sections = [(dp_qkv, s_qkv), (dz, s_z), (dp_ab, None), (dscb, (o_sc, d)), (dscc, (o_sc + d, d)), (dscx, (o_sc + 2 * d, d)),
                (dga, (o_ga, d)), (dgb, (o_gb, d))]
    gwt_in = [gmm(as_mat(dp), h1m, name=f"g_in_{k}") for k, (dp, _) in enumerate(sections)]
    gwt_in[2] = gwt_in[2][:2 * heads]

    r_in = rows[5]
    win = -(-(r_in + max(r_in * k % ROW_ALIGN for k in range(NDEV))) // 128) * 128
    need_rows = max(_window_start(r_in, k) for k in range(NDEV)) + win
    gwt_in = jnp.concatenate(gwt_in + [jnp.zeros((need_rows - NDEV * r_in, d), MXU_DTYPE)], axis=0)
    recv1 = _exchange_in_chip([(gwt_in, r_in, win, 0)], "scatter_in_chip", 7)
    own = jnp.stack([lax.dynamic_slice_in_dim(gwt_in, _window_start(r_in, 2 * q + ac), win, axis=0) for q in range(4)])
    s1 = _sum_in_chip(own, recv1, "sum_in_chip")
    recv2 = _exchange_chips_async(s1, "scatter_chips", 6)

    dh1 = _mm(as_mat(dp_ab), wt_ab, name="d_in_ab")
    wide = [as_mat(dp) for dp, sec in sections[:-1] if sec is not None]
    tk_in = min(d, 1024)
    dh1 = _mm_chain(wide, wt_in, lambda t: tk_in * t + jnp.where(t * tk_in >= o_ab, 2 * heads, 0), add=dh1, name="d_in", tk=tk_in)
    dh1 = _Product(sections[-1][0], wt_in, b_rows=sections[-1][1], add=as_tok(dh1))
    (grad_x,), (dsh1, dsc1), (dn1w,) = _tok_bwd(_f_norm_mod_skip, [x], [sh1, sc1], [n1w], [dh1, dx_skip], [True],
                                                name="d_norm1", ts=256)
    reduced = _sum_chips(s1, recv2, (2 * ax + ay).reshape(1).astype(jnp.int32), "sum_chips")
    gt_w_in = lax.dynamic_slice_in_dim(reduced, r_in * dev - _window_start(r_in, dev), r_in, axis=0)
    g_w_in = gt_w_in.T.reshape(w_in.shape)
    gt_w_ffn_in = ffn_red[:rows[3]]
    g_w_ffn_in = gt_w_ffn_in.T.reshape(w_ffn_in.shape)
    g_w_ffn_out = ffn_red[rows[3]:].reshape(w_ffn_out.shape)
    g_w_gdn_proj, g_w_sc_out, g_w_o = (mix_red[offs[i]:offs[i] + rows[i]].reshape(ref.shape)
                                       for i, ref in enumerate((w_gdn_proj, w_sc_out, w_o)))

    dmod = jnp.concatenate([t.reshape(bl, d) for t in (dsh1, dsc1, dg1, dsh2, dsc2, dg2)], axis=1)
    dmodf = jnp.concatenate([t.reshape(bl, d) for t in (dshf, dscf)], axis=1)
    summed_parts = [dn1w, dn2w, dnfw, g_gnw, g_a_log, g_dt_bias, g_conv_w, g_sc_w, loss_l]
    partial = _all_gather(_pack([dmod, dmodf] + summed_parts, LANE, 8, F32), name="gather_small", hbm=False)
    partial = partial.reshape(NDEV, -1)
    n_rows = bl * (6 * d + 2 * d)
    dmod_all, dmodf_all = _unpack(partial[:, :n_rows], [(bl, 6 * d), (bl, 2 * d)])
    dmod_all, dmodf_all = dmod_all.reshape(NDEV * bl, 6 * d), dmodf_all.reshape(NDEV * bl, 2 * d)
    totals = _row_sum(partial[:, n_rows:], "sum_small")
    t_n1w, t_n2w, t_nfw, t_gnw, t_a_log, t_dt_bias, t_conv_w, t_sc_w, t_loss = [
        t[0] for t in _unpack(totals, [p.shape for p in summed_parts])]
    my_cols = lambda a, n: lax.dynamic_slice_in_dim(a, dev * n, n, axis=1)
    grads = {
        "w_ada": _mm(c_act, my_cols(dmod_all, n_ada), ta=True, name="g_ada").reshape(w_ada.shape),
        "b_ada": _row_sum(dmod_all, "g_ada_bias").reshape(b_ada.shape),
        "norm1_w": t_n1w.reshape(norm1_w.shape),
        "w_in": g_w_in,
        "gdn_conv_w": my_cols(t_conv_w, gdn_conv_w.shape[-1]).reshape(gdn_conv_w.shape),
        "gdn_a_log": t_a_log[:, :heads].reshape(gdn_a_log.shape),
        "gdn_dt_bias": t_dt_bias[:, :heads].reshape(gdn_dt_bias.shape),
        "gdn_norm_w": t_gnw.reshape(gdn_norm_w.shape),
        "w_gdn_proj": g_w_gdn_proj,
        "sc_conv_w": my_cols(t_sc_w, sc_conv_w.shape[-1]).reshape(sc_conv_w.shape),
        "w_sc_out": g_w_sc_out,
        "w_o": g_w_o,
        "norm2_w": t_n2w.reshape(norm2_w.shape),
        "w_ffn_in": g_w_ffn_in,
        "w_ffn_out": g_w_ffn_out,
        "w_ada_f": _mm(c_act, my_cols(dmodf_all, n_adaf), ta=True, name="g_adaf").reshape(w_ada_f.shape),
        "b_ada_f": _row_sum(dmodf_all, "g_adaf_bias").reshape(b_ada_f.shape),
        "normf_w": t_nfw.reshape(normf_w.shape),
    }
    weights = dict(w_ada=w_ada, b_ada=b_ada, norm1_w=norm1_w, w_in=w_in, gdn_conv_w=gdn_conv_w, gdn_a_log=gdn_a_log,
                   gdn_dt_bias=gdn_dt_bias, gdn_norm_w=gdn_norm_w, w_gdn_proj=w_gdn_proj, sc_conv_w=sc_conv_w,
                   w_sc_out=w_sc_out, w_o=w_o, norm2_w=norm2_w, w_ffn_in=w_ffn_in, w_ffn_out=w_ffn_out, w_ada_f=w_ada_f,
                   b_ada_f=b_ada_f, normf_w=normf_w)
    m_in = [m_w_ada, m_b_ada, m_norm1_w, m_w_in, m_gdn_conv_w, m_gdn_a_log, m_gdn_dt_bias, m_gdn_norm_w, m_w_gdn_proj,
            m_sc_conv_w, m_w_sc_out, m_w_o, m_norm2_w, m_w_ffn_in, m_w_ffn_out, m_w_ada_f, m_b_ada_f, m_normf_w]
    v_in = [v_w_ada, v_b_ada, v_norm1_w, v_w_in, v_gdn_conv_w, v_gdn_a_log, v_gdn_dt_bias, v_gdn_norm_w, v_w_gdn_proj,
            v_sc_conv_w, v_w_sc_out, v_w_o, v_norm2_w, v_w_ffn_in, v_w_ffn_out, v_w_ada_f, v_b_ada_f, v_normf_w]
    deltas, new_m, new_v = [], [], []
    grads_t = {"w_in": gt_w_in, "w_ffn_in": gt_w_ffn_in}
    for (wname, wt), mt, vt in zip(weights.items(), m_in, v_in):
        if wname in grads_t:
            back = lambda a, wt=wt: a.T.reshape(wt.shape)
            dl, mn, vn = (back(a) for a in _adamw(wt[0].T, grads_t[wname], mt[0].T, vt[0].T, "adamw_" + wname))
        else:
            dl, mn, vn = _adamw(wt, grads[wname], mt, vt, "adamw_" + wname)
        deltas.append(dl)
        new_m.append(mn)
        new_v.append(vn)
    loss = t_loss[0, 0]
    return (loss, grad_x, *[grads[k] for k in weights], *deltas, *new_m, *new_v)
```

```python
import functools

import jax
import jax.numpy as jnp
from jax import lax
from jax.experimental import pallas as pl
from jax.experimental.pallas import tpu as pltpu
from jax.experimental.pallas import tpu_sc as plsc

F32 = jnp.float32
MXU_DTYPE = jnp.bfloat16
NDEV = 8
CHUNK = 64
HEAD = 128
LANE = 128
EPS = 1e-6
ADAM_LR, ADAM_B1, ADAM_B2, ADAM_EPS, ADAM_WD, ADAM_STEP = 0.001, 0.9, 0.999, 1e-08, 0.01, 10
VMEM_LIMIT = 48 * 1024 * 1024
MESH_IDS = pl.DeviceIdType.MESH
HIGHEST = lax.Precision.HIGHEST


def _tile(n, cands=(512, 256, 128)):
    for c in cands:
        if n % c == 0:
            return c
    return n


def _cparams(*sem):
    return pltpu.CompilerParams(dimension_semantics=sem, vmem_limit_bytes=VMEM_LIMIT)


def _mm(a, b, *, ta=False, tb=False, add=None, out_dtype=F32, name, b_rows=None, out_rows=None, row_off=0, into=None):
    m, k = (a.shape[1], a.shape[0]) if ta else a.shape
    b_shape = b.shape if b_rows is None else (b_rows[1], b.shape[1])
    n = b_shape[0] if tb else b_shape[1]
    assert k == (b_shape[1] if tb else b_shape[0])
    if ta:
        tm, tn = _tile(m), n if n <= 1024 else _tile(n)
        tk = k if k <= 4096 else _tile(k, (4096, 2048, 1024, 512))
        if tm * tk > 1024 * 2048:
            tk = _tile(k, (2048, 1024, 512))
    else:
        tk = k if k <= 1024 else _tile(k, (1024, 512))
        tn = _tile(n, (1024 if tk <= 1024 else 512, 512, 256, 128))
        tm = _tile(m, (2048 if (tn <= 512 and tk <= 1024) else 1024, 1024, 512, 256, 128))
    nk = k // tk
    dims = (((0 if ta else 1,), (1 if tb else 0,)), ((), ()))
    has_add = add is not None

    def body(*refs):
        a_ref, b_ref = refs[0], refs[1]
        add_ref = refs[2] if has_add else None
        o_ref = refs[2 + has_add + (into is not None)]
        part = lax.dot_general(a_ref[...].astype(MXU_DTYPE), b_ref[...].astype(MXU_DTYPE), dims,
                               preferred_element_type=F32)

        def finish(acc):
            if has_add:
                acc = acc + add_ref[...]
            o_ref[...] = acc.astype(o_ref.dtype)

        if nk == 1:
            finish(part)
        else:
            acc_ref = refs[-1]
            kk = pl.program_id(2)

            @pl.when(kk == 0)
            def _():
                acc_ref[...] = part

            @pl.when(kk > 0)
            def _():
                acc_ref[...] += part

            @pl.when(kk == nk - 1)
            def _():
                finish(acc_ref[...])

    a_spec = pl.BlockSpec((tk, tm), lambda i, j, kk: (kk, i)) if ta else pl.BlockSpec((tm, tk), lambda i, j, kk: (i, kk))
    if b_rows is None:
        b_spec = pl.BlockSpec((tn, tk), lambda i, j, kk: (j, kk)) if tb else pl.BlockSpec((tk, tn), lambda i, j, kk: (kk, j))
    else:
        at = lambda t: pl.multiple_of(b_rows[0] + t, ROW_ALIGN)
        b_spec = (pl.BlockSpec((pl.Element(tn), pl.Element(tk)), lambda i, j, kk: (at(j * tn), kk * tk)) if tb else
                  pl.BlockSpec((pl.Element(tk), pl.Element(tn)), lambda i, j, kk: (at(kk * tk), j * tn)))
    add_spec = pl.BlockSpec((tm, tn), lambda i, j, kk: (i, j))
    assert row_off % tm == 0
    o_spec = pl.BlockSpec((tm, tn), lambda i, j, kk: (i + row_off // tm, j))
    in_specs = [a_spec, b_spec] + ([add_spec] if has_add else []) + ([pl.BlockSpec(memory_space=pl.ANY)] if into is not None else [])
    args = [a, b] + ([add] if has_add else []) + ([into] if into is not None else [])
    return pl.pallas_call(
        body, name=name, grid=(m // tm, n // tn, nk), in_specs=in_specs, out_specs=o_spec,
        out_shape=jax.ShapeDtypeStruct((out_rows or m, n), out_dtype),
        scratch_shapes=[pltpu.VMEM((tm, tn), F32)] if nk > 1 else [],
        input_output_aliases={len(args) - 1: 0} if into is not None else {},
        compiler_params=_cparams("parallel", "parallel", "arbitrary"),
    )(*args)


def _mm_chain(parts, b, row_of_tile, *, add, name, tk=1024, tm=1024):
    m, n = parts[0].shape[-2], b.shape[1]
    tm = min(tm, m)
    tiles = [p.shape[0] if p.ndim == 3 else p.shape[1] // tk for p in parts]
    first = [sum(tiles[:s]) for s in range(len(parts))]
    nk = sum(tiles)

    def body(*refs):
        a_refs, b_ref, add_ref, o_ref, acc_ref = refs[:len(parts)], *refs[len(parts):]
        kk = pl.program_id(1)

        @pl.when(kk == 0)
        def _():
            acc_ref[...] = add_ref[...]

        for a_ref, lo, cnt in zip(a_refs, first, tiles):
            @pl.when(jnp.logical_and(kk >= lo, kk < lo + cnt))
            def _(a_ref=a_ref):
                acc_ref[...] += lax.dot_general(a_ref[...].astype(MXU_DTYPE), b_ref[...].astype(MXU_DTYPE),
                                                (((1,), (0,)), ((), ())), preferred_element_type=F32)

        @pl.when(kk == nk - 1)
        def _():
            o_ref[...] = acc_ref[...]

    tile_of = lambda kk, lo, cnt: jnp.clip(kk - lo, 0, cnt - 1)
    a_specs = [pl.BlockSpec((None, tm, tk), functools.partial(lambda i, kk, lo, cnt: (tile_of(kk, lo, cnt), i, 0), lo=lo, cnt=cnt))
               if p.ndim == 3 else
               pl.BlockSpec((tm, tk), functools.partial(lambda i, kk, lo, cnt: (i, tile_of(kk, lo, cnt)), lo=lo, cnt=cnt))
               for p, lo, cnt in zip(parts, first, tiles)]
    b_spec = pl.BlockSpec((pl.Element(tk), pl.Element(n)), lambda i, kk: (pl.multiple_of(row_of_tile(kk), ROW_ALIGN), 0))
    o_spec = pl.BlockSpec((tm, n), lambda i, kk: (i, 0))
    return pl.pallas_call(
        body, name=name, grid=(m // tm, nk), in_specs=a_specs + [b_spec, o_spec], out_specs=o_spec,
        out_shape=jax.ShapeDtypeStruct((m, n), F32), scratch_shapes=[pltpu.VMEM((tm, n), F32)],
        compiler_params=_cparams("parallel", "arbitrary"),
    )(*parts, b, add)


def _swiglu_tiles(m, half):
    tn = _tile(half, (512, 256, 128))
    return _tile(m, (2048 if tn <= 256 else 1024, 1024, 512, 256, 128)), tn


def _ffn_in_swiglu(h, wt, half, name):
    m, k = h.shape
    tm, tn = _swiglu_tiles(m, half)
    nj = half // tn
    dims = (((1,), (1,)), ((), ()))

    def body(h_ref, wa_ref, wb_ref, act_ref, a_ref, b_ref):
        lhs = h_ref[...].astype(MXU_DTYPE)
        a = lax.dot_general(lhs, wa_ref[...].astype(MXU_DTYPE), dims, preferred_element_type=F32)
        b = lax.dot_general(lhs, wb_ref[...].astype(MXU_DTYPE), dims, preferred_element_type=F32)
        act_ref[...] = (_silu(a) * b).astype(act_ref.dtype)
        a_ref[...] = a.astype(a_ref.dtype)
        b_ref[...] = b.astype(b_ref.dtype)

    out = jax.ShapeDtypeStruct((m, half), MXU_DTYPE)
    oblk = pl.BlockSpec((tm, tn), lambda i, j: (i, j))
    return pl.pallas_call(
        body, name=name, grid=(m // tm, nj),
        in_specs=[pl.BlockSpec((tm, k), lambda i, j: (i, 0)), pl.BlockSpec((tn, k), lambda i, j: (j, 0)),
                  pl.BlockSpec((tn, k), lambda i, j: (j + nj, 0))],
        out_specs=[oblk, oblk, oblk], out_shape=[out, out, out], compiler_params=_cparams("parallel", "parallel"),
    )(h, wt, wt)


def _ffn_out_bwd_swiglu(dff, w, a, b, name):
    m, k = dff.shape
    half = w.shape[0]
    tm, tn = _swiglu_tiles(m, half)

    def body(d_ref, w_ref, a_ref, b_ref, da_ref, db_ref):
        dact = lax.dot_general(d_ref[...].astype(MXU_DTYPE), w_ref[...].astype(MXU_DTYPE), (((1,), (1,)), ((), ())),
                               preferred_element_type=F32)
        av, bv = a_ref[...].astype(F32), b_ref[...].astype(F32)
        sig = jax.nn.sigmoid(av)
        da_ref[...] = (dact * bv * (sig * (1.0 + av * (1.0 - sig)))).astype(da_ref.dtype)
        db_ref[...] = (dact * (av * sig)).astype(db_ref.dtype)

    out = jax.ShapeDtypeStruct((m, half), MXU_DTYPE)
    oblk = pl.BlockSpec((tm, tn), lambda i, j: (i, j))
    return pl.pallas_call(
        body, name=name, grid=(m // tm, half // tn),
        in_specs=[pl.BlockSpec((tm, k), lambda i, j: (i, 0)), pl.BlockSpec((tn, k), lambda i, j: (j, 0)), oblk, oblk],
        out_specs=[oblk, oblk], out_shape=[out, out], compiler_params=_cparams("parallel", "parallel"),
    )(dff, w, a, b)


def _with_off(xs):
    return [x if isinstance(x, tuple) else (x, 0) for x in xs]


def _spec(kind, arr, off, ts, wb):
    w = arr.shape[-1] if wb is None else wb
    col = (lambda j: 0) if wb is None else functools.partial(lambda j, o: o + j, o=off)
    if kind == "tok":
        return pl.BlockSpec((None, ts, w), lambda j, b, i: (b, i, col(j)))
    if kind == "bat":
        return pl.BlockSpec((None, 1, w), lambda j, b, i: (b, 0, col(j)))
    if off is None:
        return pl.BlockSpec(arr.shape, lambda j, b, i: (0, 0))
    return pl.BlockSpec((arr.shape[0], w), lambda j, b, i: (0, col(j)))


class _Product:
    def __init__(self, a, b, *, tb=False, b_rows=None, add=None):
        self.a, self.b, self.tb, self.b_rows, self.add = a, b, tb, b_rows, add
        rows = b.shape[0] if b_rows is None else b_rows[1]
        self.shape = a.shape[:2] + (rows if tb else b.shape[1],)

    def inputs(self, ts):
        a_spec = pl.BlockSpec((None, ts, self.a.shape[2]), lambda j, b, i: (b, i, 0))
        if self.b_rows is None:
            b_spec = pl.BlockSpec(self.b.shape, lambda j, b, i: (0, 0))
        else:
            start, count = self.b_rows
            b_spec = pl.BlockSpec((pl.Element(count), pl.Element(self.b.shape[1])), lambda j, b, i: (start, 0))
        extra = [] if self.add is None else [(self.add, pl.BlockSpec((None, ts, self.shape[2]), lambda j, b, i: (b, i, 0)))]
        return [(self.a, a_spec), (self.b, b_spec)] + extra

    def value(self, refs):
        dims = (((1,), (1 if self.tb else 0,)), ((), ()))
        val = lax.dot_general(refs[0][...].astype(MXU_DTYPE), refs[1][...].astype(MXU_DTYPE), dims, preferred_element_type=F32)
        return val if self.add is None else val + refs[2][...].astype(F32)


def _inputs(groups, kinds, ts, wb):
    loaded = [(a, _spec(kind, a, o, ts, wb)) for g, kind in zip(groups, kinds) for a, o in g if not isinstance(a, _Product)]
    made = [pair for g in groups for a, _ in g if isinstance(a, _Product) for pair in a.inputs(ts)]
    return [a for a, _ in loaded + made], [sp for _, sp in loaded + made]


def _values(refs, groups):
    n_loaded = sum(1 for g in groups for a, _ in g if not isinstance(a, _Product))
    loaded, pos, out = iter(refs[:n_loaded]), n_loaded, []
    for g in groups:
        vals = []
        for a, _ in g:
            if isinstance(a, _Product):
                k = 2 if a.add is None else 3
                vals.append(a.value(refs[pos:pos + k]))
                pos += k
            else:
                vals.append(next(loaded)[...].astype(F32))
        out.append(vals)
    return out, pos


def _tok_fwd(fn, toks, bats, pars, outs, *, name, ts, wb=None, cols=1):
    groups = [_with_off(toks), _with_off(bats), _with_off(pars)]
    bl, s, _ = groups[0][0][0].shape
    ts = min(ts, s)
    args, in_specs = _inputs(groups, ("tok", "bat", "par"), ts, wb)

    def body(*refs):
        vals, n_in = _values(refs, groups)
        res = fn(*[v for g in vals for v in g])
        for r, val in zip(refs[n_in:], res):
            r[...] = val.astype(r.dtype)

    out_specs = [pl.BlockSpec((None, ts, w if wb is None else wb), lambda j, b, i: (b, i, j)) for w, _ in outs]
    return pl.pallas_call(
        body, name=name, grid=(cols, bl, s // ts), in_specs=in_specs,
        out_specs=out_specs, out_shape=[jax.ShapeDtypeStruct((bl, s, w), dt) for w, dt in outs],
        compiler_params=_cparams("parallel", "parallel", "parallel"),
    )(*args)


def _accumulate(ref, val, first):
    @pl.when(first)
    def _():
        ref[...] = val

    @pl.when(jnp.logical_not(first))
    def _():
        ref[...] += val


def _tok_bwd(fn, toks, bats, pars, cots, need, *, name, ts, wb=None, cols=1, tok_dtype=F32, loss=False, after=()):
    toks, bats, pars, cots = _with_off(toks), _with_off(bats), _with_off(pars), _with_off(cots)
    groups = [toks, bats, pars, cots]
    bl, s, _ = toks[0][0].shape
    ts = min(ts, s)
    nt, nb, npar = len(toks), len(bats), len(pars)
    args, in_specs = _inputs(groups, ("tok", "bat", "par", "tok"), ts, wb)
    args, in_specs = args + list(after), in_specs + [pl.BlockSpec(memory_space=pl.ANY)] * len(after)

    def body(*refs):
        j, b, i = pl.program_id(0), pl.program_id(1), pl.program_id(2)
        (tok_vals, bat_vals, par_vals, cot_vals), o = _values(refs, groups)
        o += len(after)
        outs, vjp = jax.vjp(fn, *tok_vals, *bat_vals, *par_vals)
        if loss:
            ct = (jnp.ones_like(outs[0]),)
            tot = jnp.broadcast_to(jnp.sum(outs[0], keepdims=True), (1, LANE))
            _accumulate(refs[o], tot, jnp.logical_and(b == 0, i == 0))
            o += 1
        else:
            ct = tuple(cot_vals)
        grads = vjp(ct)
        for t in range(nt):
            if need[t]:
                refs[o][...] = grads[t].astype(refs[o].dtype)
                o += 1
        for t in range(nb):
            _accumulate(refs[o], grads[nt + t], i == 0)
            o += 1
        for t in range(npar):
            first = jnp.logical_and(b == 0, i == 0)
            if pars[t][1] is None:
                first = jnp.logical_and(first, j == 0)
            _accumulate(refs[o], grads[nt + nb + t], first)
            o += 1

    full = lambda arr: arr.shape[-1] if wb is None else wb * cols
    blk = lambda arr: arr.shape[-1] if wb is None else wb
    out_specs, out_shape = [], []
    if loss:
        out_specs.append(pl.BlockSpec((1, LANE), lambda j, b, i: (0, 0)))
        out_shape.append(jax.ShapeDtypeStruct((1, LANE), F32))
    for t in range(nt):
        if need[t]:
            out_specs.append(pl.BlockSpec((None, ts, blk(toks[t][0])), lambda j, b, i: (b, i, j)))
            dt = tok_dtype[t] if isinstance(tok_dtype, (list, tuple)) else tok_dtype
            out_shape.append(jax.ShapeDtypeStruct((bl, s, full(toks[t][0])), dt))
    for arr, _ in bats:
        out_specs.append(pl.BlockSpec((None, 1, blk(arr)), lambda j, b, i: (b, 0, j)))
        out_shape.append(jax.ShapeDtypeStruct((bl, 1, full(arr)), F32))
    for arr, off in pars:
        if off is None:
            out_specs.append(pl.BlockSpec(arr.shape, lambda j, b, i: (0, 0)))
            out_shape.append(jax.ShapeDtypeStruct(arr.shape, F32))
        else:
            out_specs.append(pl.BlockSpec((arr.shape[0], blk(arr)), lambda j, b, i: (0, j)))
            out_shape.append(jax.ShapeDtypeStruct((arr.shape[0], full(arr)), F32))
    res = list(pl.pallas_call(
        body, name=name, grid=(cols, bl, s // ts), in_specs=in_specs,
        out_specs=out_specs, out_shape=out_shape, compiler_params=_cparams("arbitrary", "arbitrary", "arbitrary"),
    )(*args))
    tot = res.pop(0) if loss else None
    dtoks = [res.pop(0) if need[t] else None for t in range(nt)]
    dbats = [res.pop(0) for _ in range(nb)]
    dpars = [res.pop(0) for _ in range(npar)]
    return (tot, dtoks, dbats, dpars) if loss else (dtoks, dbats, dpars)


def _silu(x):
    return x * jax.nn.sigmoid(x)


def _rms(x, w):
    return x * lax.rsqrt(jnp.mean(x * x, axis=-1, keepdims=True) + EPS) * w


def _f_norm_mod(x, shift, scale, w):
    return (_rms(x, w) * (1.0 + scale) + shift,)


def _f_norm_mod_skip(x, shift, scale, w):
    return _rms(x, w) * (1.0 + scale) + shift, x


def _f_res_norm_mod(x, mix, gate, shift, scale, w):
    x2 = x + gate * mix
    return x2, _rms(x2, w) * (1.0 + scale) + shift


def _f_res_norm_mod_keep(x, mix, gate, shift, scale, w):
    return (*_f_res_norm_mod(x, mix, gate, shift, scale, w), mix)


def _f_gates(p, a_log, dt_bias, *, heads):
    z = p + dt_bias
    g = -jnp.exp(a_log) * (jnp.maximum(z, 0.0) + jnp.log1p(jnp.exp(jnp.minimum(z, -z))))
    lane = lax.broadcasted_iota(jnp.int32, p.shape, 1)
    return (jnp.where(lane < heads, g, jax.nn.sigmoid(p)),)


def _f_gdn_out(o, z, w):
    return (_rms(o, w) * _silu(z),)


def _f_merge(ga, gb, ya, yb):
    return (jax.nn.sigmoid(ga) * ya + jax.nn.sigmoid(gb) * yb,)


def _f_merge_keep(ga, gb, ya, yb):
    return (*_f_merge(ga, gb, ya, yb), yb)


def _f_loss(x2, ff, tgt, gate, shift, scale, w):
    y = _rms(x2 + gate * ff, w) * (1.0 + scale) + shift
    return (0.5 * jnp.mean(jnp.square(y - tgt), axis=-1, keepdims=True),)


def _shift_down(x, s):
    if s == 0:
        return x
    row = lax.broadcasted_iota(jnp.int32, x.shape, 0)
    return jnp.where(row >= s, pltpu.roll(x, s, 0), 0.0)


def _shift_up(x, s):
    if s == 0:
        return x
    n = x.shape[0]
    row = lax.broadcasted_iota(jnp.int32, x.shape, 0)
    return jnp.where(row < n - s, pltpu.roll(x, n - s, 0), 0.0)


def _conv(x, w):
    width = w.shape[0]
    acc = w[width - 1:width, :] * x
    for j in range(width - 1):
        acc = acc + w[j:j + 1, :] * _shift_down(x, width - 1 - j)
    return acc


def _conv_bwd(dy, x, w, dw_ref, first):
    width = w.shape[0]
    dx = w[width - 1:width, :] * dy
    for j in range(width - 1):
        dx = dx + w[j:j + 1, :] * _shift_up(dy, width - 1 - j)
    for j in range(width):
        row = jnp.sum(dy * _shift_down(x, width - 1 - j), axis=0, keepdims=True)
        _accumulate(dw_ref.at[j:j + 1, :], row, first)
    return dx


def _qkv_act(xc, is_v, scale):
    a = _silu(xc)
    nrm = a * lax.rsqrt(jnp.sum(a * a, axis=-1, keepdims=True) + EPS) * scale
    return jnp.where(is_v, a, nrm)


def _qkv_consts(j, heads):
    is_v = j >= 2 * heads
    scale = jnp.where(j < heads, HEAD ** -0.5, 1.0).astype(F32)
    return is_v, scale


def _qkv_fwd(p, w, heads, name):
    bl, s, w3 = p.shape

    def body(p_ref, w_ref, o_ref):
        is_v, scale = _qkv_consts(pl.program_id(0), heads)
        o_ref[...] = _qkv_act(_conv(p_ref[...], w_ref[...]), is_v, scale)

    blk = pl.BlockSpec((None, s, HEAD), lambda j, b: (b, 0, j))
    return pl.pallas_call(
        body, name=name, grid=(w3 // HEAD, bl), in_specs=[blk, pl.BlockSpec((w.shape[0], HEAD), lambda j, b: (0, j))],
        out_specs=blk, out_shape=jax.ShapeDtypeStruct(p.shape, F32), compiler_params=_cparams("parallel", "parallel"),
    )(p, w)


def _qkv_bwd(p, w, dout, heads, name):
    bl, s, w3 = p.shape

    def body(p_ref, w_ref, d_ref, dp_ref, dw_ref):
        is_v, scale = _qkv_consts(pl.program_id(0), heads)
        x, wv = p_ref[...], w_ref[...]
        _, vjp = jax.vjp(lambda xc: _qkv_act(xc, is_v, scale), _conv(x, wv))
        (dxc,) = vjp(d_ref[...])
        dp_ref[...] = _conv_bwd(dxc, x, wv, dw_ref, pl.program_id(1) == 0).astype(dp_ref.dtype)

    blk = pl.BlockSpec((None, s, HEAD), lambda j, b: (b, 0, j))
    wblk = pl.BlockSpec((w.shape[0], HEAD), lambda j, b: (0, j))
    return pl.pallas_call(
        body, name=name, grid=(w3 // HEAD, bl), in_specs=[blk, wblk, blk], out_specs=[blk, wblk],
        out_shape=[jax.ShapeDtypeStruct(p.shape, MXU_DTYPE), jax.ShapeDtypeStruct(w.shape, F32)],
        compiler_params=_cparams("arbitrary", "arbitrary"),
    )(p, w, dout)


def _sc_specs(p, w):
    bl, s, w3 = p.shape
    nblk = w3 // 3 // LANE
    sec = lambda k: pl.BlockSpec((None, s, LANE), functools.partial(lambda j, b, k: (b, 0, k * nblk + j), k=k))
    return nblk, [sec(0), sec(1), sec(2)], pl.BlockSpec((w.shape[0], LANE), lambda j, b: (0, j)), \
        pl.BlockSpec((None, s, LANE), lambda j, b: (b, 0, j))


def _sc_fwd(p, w, name):
    bl, s, w3 = p.shape
    nblk, secs, wblk, oblk = _sc_specs(p, w)

    def body(b_ref, c_ref, x_ref, w_ref, o_ref):
        o_ref[...] = (b_ref[...] * _conv(c_ref[...] * x_ref[...], w_ref[...])).astype(o_ref.dtype)

    return pl.pallas_call(
        body, name=name, grid=(nblk, bl), in_specs=secs + [wblk], out_specs=oblk,
        out_shape=jax.ShapeDtypeStruct((bl, s, w3 // 3), MXU_DTYPE), compiler_params=_cparams("parallel", "parallel"),
    )(p, p, p, w)


def _sc_bwd(p, w, dout, name):
    bl, s, w3 = p.shape
    nblk, secs, wblk, oblk = _sc_specs(p, w)

    def body(b_ref, c_ref, x_ref, w_ref, d_ref, dp_ref, dw_ref):
        gb, gc, xin, wv, d = b_ref[...], c_ref[...], x_ref[...], w_ref[...], d_ref[...]
        u = gc * xin
        dp_ref[0] = (d * _conv(u, wv)).astype(dp_ref.dtype)
        du = _conv_bwd(d * gb, u, wv, dw_ref, pl.program_id(1) == 0)
        dp_ref[1] = (du * xin).astype(dp_ref.dtype)
        dp_ref[2] = (du * gc).astype(dp_ref.dtype)

    return pl.pallas_call(
        body, name=name, grid=(nblk, bl), in_specs=secs + [wblk, oblk],
        out_specs=[pl.BlockSpec((3, None, s, LANE), lambda j, b: (0, b, 0, j)), wblk],
        out_shape=[jax.ShapeDtypeStruct((3, bl, s, w3 // 3), MXU_DTYPE), jax.ShapeDtypeStruct(w.shape, F32)],
        compiler_params=_cparams("arbitrary", "arbitrary"),
    )(p, p, p, w, dout)


def _bdot(a, b, ca, cb):
    return lax.dot_general(a.astype(MXU_DTYPE), b.astype(MXU_DTYPE), (((ca,), (cb,)), ((), ())),
                           preferred_element_type=F32)


def _hdot(a, b):
    return lax.dot_general(a, b, (((1,), (0,)), ((), ())), precision=HIGHEST, preferred_element_type=F32)


def _lane_col(x, idx):
    lane = lax.broadcasted_iota(jnp.int32, x.shape, 1)
    return jnp.sum(jnp.where(lane == idx, x, 0.0), axis=1, keepdims=True)


def _chunk_masks():
    r = lax.broadcasted_iota(jnp.int32, (CHUNK, CHUNK), 0)
    c = lax.broadcasted_iota(jnp.int32, (CHUNK, CHUNK), 1)
    return r == c, r >= c, r > c


def _dot3(a, b):
    ah, bh = a.astype(MXU_DTYPE), b.astype(MXU_DTYPE)
    al, bl = (a - ah.astype(F32)).astype(MXU_DTYPE), (b - bh.astype(F32)).astype(MXU_DTYPE)
    dot = lambda x, y: lax.dot_general(x, y, (((1,), (0,)), ((), ())), preferred_element_type=F32)
    return dot(ah, bh) + (dot(ah, bl) + dot(al, bh))


def _tri_inv_steps(low, eye):
    x = -low
    p = jnp.where(eye, 1.0, 0.0) + x
    span = 2
    while span < CHUNK:
        x = _dot3(x, x)
        yield
        p = p + _dot3(p, x)
        yield
        span *= 2
    return p


def _round_robin(gens):
    out, live = [None] * len(gens), list(range(len(gens)))
    while live:
        still = []
        for i in live:
            try:
                next(gens[i])
                still.append(i)
            except StopIteration as stop:
                out[i] = stop.value
        live = still
    return out


def _gdn_pre(q, k, v, gc, beta, masks):
    eye, causal, strict = masks
    gc_row = jnp.sum(jnp.where(eye, gc, 0.0), axis=0, keepdims=True)
    decay = jnp.where(causal, jnp.exp(jnp.where(causal, gc - gc_row, 0.0)), 0.0)
    eg = jnp.exp(gc)
    gl = gc[CHUNK - 1:CHUNK, :]
    kb, vb = k * beta, v * beta
    both = _bdot(jnp.concatenate([kb, q], axis=0), k, 1, 1)
    low = jnp.where(strict, both[:CHUNK] * decay, 0.0)
    qk = jnp.where(causal, both[CHUNK:] * decay, 0.0)
    rest = jnp.exp(gl - gc)
    return dict(decay=decay, eg=eg, gl=gl, kb=kb, vb=vb, kbe=kb * eg, low=low, qk=qk, qg=q * eg, rest=rest, kdec=k * rest)


def _gdn_specs(qkv, gbeta, heads, rev):
    bl, s, w3 = qkv.shape
    d, n = w3 // 3, s // CHUNK
    at = (lambda c: n - 1 - c) if rev else (lambda c: c)
    assert d == heads * HEAD
    sec = pl.BlockSpec((None, CHUNK, w3), lambda b, c: (b, at(c), 0))
    gspec = pl.BlockSpec((None, CHUNK, LANE), lambda b, c: (b, at(c), 0))
    sspec = pl.BlockSpec((None, None, heads, HEAD, HEAD), lambda b, c: (b, at(c), 0, 0, 0))
    tspec = pl.BlockSpec((None, None, heads, CHUNK, CHUNK), lambda b, c: (b, at(c), 0, 0, 0))
    return bl, s, d, n, sec, gspec, sspec, tspec


def _gdn_fwd(qkv, gbeta, heads, name):
    bl, s, d, n, sec, gspec, sspec, tspec = _gdn_specs(qkv, gbeta, heads, False)

    def body(x_ref, g_ref, o_ref, s_ref, t_ref, st_ref):
        @pl.when(pl.program_id(1) == 0)
        def _():
            st_ref[...] = jnp.zeros_like(st_ref)

        masks = _chunk_masks()
        eye, causal, _ = masks
        gblk = g_ref[...]
        gc_all = _hdot(jnp.where(causal, 1.0, 0.0), gblk)
        st_all = st_ref[...]

        def head(h):
            st = st_all[h]
            q, k, v = (x_ref[:, sec * d + h * HEAD:sec * d + (h + 1) * HEAD] for sec in range(3))
            pre = _gdn_pre(q, k, v, _lane_col(gc_all, h), _lane_col(gblk, heads + h), masks)
            yield
            t = yield from _tri_inv_steps(pre["low"], eye)
            uw = _bdot(t, jnp.concatenate([pre["vb"], pre["kbe"]], axis=1), 1, 0)
            u, w = uw[:, :HEAD], uw[:, HEAD:]
            yield
            vnew = u - _bdot(w, st, 1, 0)
            yield
            out = _bdot(pre["qg"], st, 1, 0) + _bdot(pre["qk"], vnew, 1, 0)
            return out, t, st * jnp.exp(pre["gl"]) + _bdot(pre["kdec"], vnew, 0, 0)

        outs, ts, states = zip(*_round_robin([head(h) for h in range(heads)]))
        o_ref[...] = jnp.concatenate(outs, axis=1)
        s_ref[...] = st_all
        t_ref[...] = jnp.stack(ts)
        st_ref[...] = jnp.stack(states)

    return pl.pallas_call(
        body, name=name, grid=(bl, n), in_specs=[sec, gspec],
        out_specs=[pl.BlockSpec((None, CHUNK, d), lambda b, c: (b, c, 0)), sspec, tspec],
        out_shape=[jax.ShapeDtypeStruct((bl, s, d), F32), jax.ShapeDtypeStruct((bl, n, heads, HEAD, HEAD), F32),
                   jax.ShapeDtypeStruct((bl, n, heads, CHUNK, CHUNK), F32)],
        scratch_shapes=[pltpu.VMEM((heads, HEAD, HEAD), F32)], compiler_params=_cparams("parallel", "arbitrary"),
    )(qkv, gbeta)


def _gdn_bwd(qkv, gbeta, dout, s_all, t_all, heads, name):
    bl, s, d, n, sec, gspec, sspec, tspec = _gdn_specs(qkv, gbeta, heads, True)
    ospec = pl.BlockSpec((None, CHUNK, d), lambda b, c: (b, n - 1 - c, 0))

    def body(x_ref, g_ref, do_ref, s_ref, t_ref, dx_ref, dg_ref, ds_ref):
        @pl.when(pl.program_id(1) == 0)
        def _():
            ds_ref[...] = jnp.zeros_like(ds_ref)

        masks = _chunk_masks()
        eye, causal, strict = masks
        gblk = g_ref[...]
        gc_all = _hdot(jnp.where(causal, 1.0, 0.0), gblk)
        lane = lax.broadcasted_iota(jnp.int32, gblk.shape, 1)
        last_row = lax.broadcasted_iota(jnp.int32, (CHUNK, 1), 0) == CHUNK - 1
        rowsum = lambda a: jnp.sum(a, axis=1, keepdims=True)
        st_all, t_all_, ds_all = s_ref[...], t_ref[...], ds_ref[...]

        def head(h):
            sl = slice(h * HEAD, (h + 1) * HEAD)
            q, k, v = (x_ref[:, sec * d + h * HEAD:sec * d + (h + 1) * HEAD] for sec in range(3))
            do = do_ref[:, sl]
            beta = _lane_col(gblk, heads + h)
            st, t, dsn = st_all[h], t_all_[h], ds_all[h]
            pre = _gdn_pre(q, k, v, _lane_col(gc_all, h), beta, masks)
            decay, eg, kb, vb, kbe, low, qk, qg, kdec = (pre[x] for x in ("decay", "eg", "kb", "vb", "kbe", "low", "qk", "qg", "kdec"))
            egl = jnp.exp(pre["gl"])
            yield
            uw = _bdot(t, jnp.concatenate([vb, kbe], axis=1), 1, 0)
            u, w = uw[:, :HEAD], uw[:, HEAD:]
            yield
            vnew = u - _bdot(w, st, 1, 0)
            yield
            stack, side = functools.partial(jnp.concatenate, axis=0), functools.partial(jnp.concatenate, axis=1)
            dkdec = _bdot(vnew, dsn, 1, 1)
            dvnew = _bdot(kdec, dsn, 1, 0) + _bdot(qk, do, 0, 0)
            dgl = jnp.sum(dsn * st, keepdims=True) * egl
            dqk = jnp.where(causal, _bdot(do, vnew, 1, 1), 0.0)
            yield
            by_state = _bdot(stack([do, dvnew]), st, 1, 1)
            dqg, dw = by_state[:CHUNK], -by_state[CHUNK:]
            ds_new = dsn * egl + _bdot(stack([qg, -w]), stack([do, dvnew]), 0, 0)
            yield
            dt = _bdot(side([dvnew, dw]), side([vb, kbe]), 1, 1)
            by_t = _bdot(t, side([dvnew, dw]), 0, 0)
            dvb, dkbe = by_t[:, :HEAD], by_t[:, HEAD:]
            yield
            inner = _bdot(dt, t, 1, 1)
            yield
            dlow = -jnp.where(strict, _bdot(t, inner, 0, 0), 0.0)
            da, db = dlow * decay, dqk * decay
            yield
            m = dlow * low + dqk * qk
            kdk = dkdec * kdec
            col_of_m = jnp.sum(jnp.where(eye, jnp.sum(m, axis=0, keepdims=True), 0.0), axis=1, keepdims=True)
            dgc = rowsum(m) - col_of_m + rowsum(dqg * qg) + rowsum(dkbe * kbe) - rowsum(kdk)
            dgc = dgc + jnp.where(last_row, dgl + jnp.sum(kdk, keepdims=True), 0.0)
            by_k = _bdot(stack([da, db]), k, 1, 0)
            dkb = by_k[:CHUNK] + dkbe * eg
            yield
            dk = _bdot(stack([da, db]), stack([kb, q]), 0, 0) + dkdec * pre["rest"] + dkb * beta
            dq = by_k[CHUNK:] + dqg * eg
            dbeta = rowsum(dkb * k) + rowsum(dvb * v)
            return dq, dk, dvb * beta, jnp.where(lane == h, dgc, 0.0) + jnp.where(lane == heads + h, dbeta, 0.0), ds_new

        dqs, dks, dvs, dgs, dss = zip(*_round_robin([head(h) for h in range(heads)]))
        dx_ref[...] = jnp.concatenate(dqs + dks + dvs, axis=1)
        ds_ref[...] = jnp.stack(dss)
        dgb = dgs[0]
        for extra in dgs[1:]:
            dgb = dgb + extra
        upper = jnp.where(jnp.logical_or(eye, jnp.logical_not(causal)), 1.0, 0.0)
        dg_ref[...] = jnp.where(lane < heads, _hdot(upper, dgb), dgb)

    return pl.pallas_call(
        body, name=name, grid=(bl, n), in_specs=[sec, gspec, ospec, sspec, tspec], out_specs=[sec, gspec],
        out_shape=[jax.ShapeDtypeStruct(qkv.shape, F32), jax.ShapeDtypeStruct((bl, s, LANE), F32)],
        scratch_shapes=[pltpu.VMEM((heads, HEAD, HEAD), F32)], compiler_params=_cparams("parallel", "arbitrary"),
    )(qkv, gbeta, dout, s_all, t_all)


def _position():
    return lax.axis_index("x"), lax.axis_index("y"), lax.axis_index("c")


def _all_gather(x, *, name, hbm):
    space = pltpu.HBM if hbm else pltpu.VMEM

    def body(x_ref, out_ref, send_sems, recv_sems, local_sem):
        ax, ay, ac = _position()
        me, sibling = (ax, ay, ac), (ax, ay, 1 - ac)
        chips = [(1 - ax, ay), (ax, 1 - ay), (1 - ax, 1 - ay)]

        def slot(px, py, pc):
            return out_ref.at[4 * px + 2 * py + pc]

        def copy(k, block, to, src=None):
            return pltpu.make_async_remote_copy(
                src_ref=slot(*block) if src is None else src, dst_ref=slot(*block), send_sem=send_sems.at[k],
                recv_sem=recv_sems.at[k], device_id=to, device_id_type=MESH_IDS)

        mine = pltpu.make_async_copy(x_ref, slot(*me), local_sem)
        mine.start()
        first = [copy(0, me, sibling, src=x_ref)] + [copy(1 + j, me, (*chip, ac), src=x_ref) for j, chip in enumerate(chips)]
        for cp in first:
            cp.start()
        passed = [copy(4 + j, (*chip, ac), sibling) for j, chip in enumerate(chips)]
        for j, chip in enumerate(chips):
            copy(1 + j, (*chip, ac), me).wait_recv()
            passed[j].start()
        copy(0, sibling, me).wait_recv()
        for j, chip in enumerate(chips):
            copy(4 + j, (*chip, 1 - ac), me).wait_recv()
        for cp in first + passed:
            cp.wait_send()
        mine.wait()

    return pl.pallas_call(
        body, name=name, out_shape=jax.ShapeDtypeStruct((NDEV,) + x.shape, x.dtype),
        in_specs=[pl.BlockSpec(memory_space=space)], out_specs=pl.BlockSpec(memory_space=space),
        scratch_shapes=[pltpu.SemaphoreType.DMA((7,)), pltpu.SemaphoreType.DMA((7,)), pltpu.SemaphoreType.DMA],
    )(x)


class _Rider:
    def __init__(self, arrays, out_shapes, sems, hooks):
        self.arrays, self.out_shapes, self.sems, self.hooks = arrays, out_shapes, sems, hooks


def _gather_rider(xs):
    n = len(xs)

    def hooks(x_refs, out_refs, send_sems, recv_sems):
        ax, ay, ac = _position()
        me, sibling = (ax, ay, ac), (ax, ay, 1 - ac)
        chips = [(1 - ax, ay), (ax, 1 - ay), (1 - ax, 1 - ay)]

        def copies(k, block, to, own=False):
            out = []
            for i in range(n):
                slot = out_refs[i].at[4 * block[0] + 2 * block[1] + block[2]]
                out.append(pltpu.make_async_remote_copy(
                    src_ref=x_refs[i] if own else slot, dst_ref=slot, send_sem=send_sems.at[k, i], recv_sem=recv_sems.at[k, i],
                    device_id=to, device_id_type=MESH_IDS))
            return out

        def first():
            for cp in copies(0, me, sibling, own=True):
                cp.start()
            for j, chip in enumerate(chips):
                for cp in copies(1 + j, me, (*chip, ac), own=True):
                    cp.start()

        def mid():
            for j, chip in enumerate(chips):
                for arrived, onward in zip(copies(1 + j, (*chip, ac), me), copies(4 + j, (*chip, ac), sibling)):
                    arrived.wait_recv()
                    onward.start()

        def last():
            for cp in copies(0, sibling, me):
                cp.wait_recv()
            for j, chip in enumerate(chips):
                for cp in copies(4 + j, (*chip, 1 - ac), me):
                    cp.wait_recv()
            for cp in copies(0, me, sibling, own=True):
                cp.wait_send()
            for j, chip in enumerate(chips):
                for cp in copies(1 + j, me, (*chip, ac), own=True) + copies(4 + j, (*chip, ac), sibling):
                    cp.wait_send()

        return first, mid, last

    return _Rider(list(xs), [jax.ShapeDtypeStruct((NDEV,) + x.shape, x.dtype) for x in xs],
                  [pltpu.SemaphoreType.DMA((7, n)), pltpu.SemaphoreType.DMA((7, n))], hooks)


def _scatter_rider(parts):
    packed = sum(r for _, r in parts)
    width, dtype = parts[0][0].shape[1], parts[0][0].dtype

    def hooks(g_refs, out_refs, send_sems, recv_sems):
        (recv_ref,) = out_refs
        ax, ay, ac = _position()

        def peer(rel):
            flip = lambda a, bit: 1 - a if rel & bit else a
            return flip(ax, 4), flip(ay, 2), flip(ac, 1)

        def first():
            for rel in range(1, NDEV):
                px, py, pc = peer(rel)
                off = 0
                for g_ref, (_, r) in zip(g_refs, parts):
                    rows = g_ref.at[pl.ds(pl.multiple_of((4 * px + 2 * py + pc) * r, ROW_ALIGN), r)]
                    pltpu.make_async_remote_copy(
                        src_ref=rows, dst_ref=recv_ref.at[rel - 1, pl.ds(off, r)], send_sem=send_sems.at[rel - 1],
                        recv_sem=recv_sems.at[rel - 1], device_id=(px, py, pc), device_id_type=MESH_IDS).start()
                    off += r

        def last():
            for rel in range(1, NDEV):
                slot = recv_ref.at[rel - 1]
                pltpu.make_async_remote_copy(src_ref=slot, dst_ref=slot, send_sem=send_sems.at[rel - 1],
                                             recv_sem=recv_sems.at[rel - 1], device_id=peer(rel), device_id_type=MESH_IDS).wait()

        return first, lambda: None, last

    return _Rider([g for g, _ in parts], [jax.ShapeDtypeStruct((NDEV - 1, packed, width), dtype)],
                  [pltpu.SemaphoreType.DMA((NDEV - 1,)), pltpu.SemaphoreType.DMA((NDEV - 1,))], hooks)


def _sum_direct(own, recv, name):
    r, w = own.shape
    tr = max(t for t in range(ROW_ALIGN, 257, ROW_ALIGN) if r % t == 0)

    def body(own_ref, *refs):
        acc = own_ref[...].astype(F32)
        for ref in refs[:-1]:
            acc = acc + ref[...].astype(F32)
        refs[-1][...] = acc

    rblk = lambda k: pl.BlockSpec((None, tr, w), functools.partial(lambda i, k: (k, i, 0), k=k))
    blk = pl.BlockSpec((tr, w), lambda i: (i, 0))
    return pl.pallas_call(body, name=name, grid=(r // tr,), in_specs=[blk] + [rblk(k) for k in range(NDEV - 1)],
                          out_specs=blk, out_shape=jax.ShapeDtypeStruct((r, w), F32),
                          compiler_params=_cparams("parallel"))(own, *([recv] * (NDEV - 1)))


ROW_ALIGN = 16


def _window_start(rows_per_dev, k):
    return rows_per_dev * k // ROW_ALIGN * ROW_ALIGN


def _exchange_in_chip(parts, name, collective_id):
    packed = sum(win for _, _, win, _ in parts)
    width, dtype = parts[0][0].shape[1], parts[0][0].dtype

    def body(g_refs, out_refs, send_sems, recv_sems):
        (recv_ref,) = out_refs
        ax, ay, ac = _position()
        sibling = (ax, ay, 1 - ac)
        _handshake([sibling])
        for q in range(4):
            for g_ref, (_, r, win, off) in zip(g_refs, parts):
                there = g_ref.at[pl.ds(pl.multiple_of(_window_start(r, 2 * q + 1 - ac), ROW_ALIGN), win)]
                pltpu.make_async_remote_copy(src_ref=there, dst_ref=recv_ref.at[q, pl.ds(off, win)], send_sem=send_sems.at[q],
                                             recv_sem=recv_sems.at[q], device_id=sibling, device_id_type=MESH_IDS).start()
        for q in range(4):
            pltpu.make_async_remote_copy(src_ref=recv_ref.at[q], dst_ref=recv_ref.at[q], send_sem=send_sems.at[q],
                                         recv_sem=recv_sems.at[q], device_id=sibling, device_id_type=MESH_IDS).wait()

    return _on_sequencer(body, [g for g, _, _, _ in parts], [jax.ShapeDtypeStruct((4, packed, width), dtype)],
                         [pltpu.SemaphoreType.DMA((4,)), pltpu.SemaphoreType.DMA((4,))], name=name, collective_id=collective_id)[0]


def _on_sequencer(body, ins, out_shapes, sems, *, name, collective_id):
    hbm = pltpu.MemorySpace.HBM
    in_refs = [jax.new_ref(a, memory_space=hbm) for a in ins]
    out_refs = [jax.empty_ref(s, memory_space=hbm) for s in out_shapes]

    @pl.kernel(mesh=plsc.ScalarSubcoreMesh(axis_name="sequencer", num_cores=1), name=name, scratch_types=tuple(sems),
               compiler_params=pltpu.CompilerParams(collective_id=collective_id))
    def launch(*sem_refs):
        body(in_refs, out_refs, *sem_refs)

    launch()
    return [r[...] for r in out_refs]


def _handshake(peers):
    barrier = pltpu.get_barrier_semaphore()
    for peer in peers:
        pl.semaphore_signal(barrier, inc=1, device_id=peer, device_id_type=MESH_IDS)
    pl.semaphore_wait(barrier, len(peers))


def _exchange_chips_async(s1, name, collective_id):
    def body(in_refs, out_refs, send_sems, recv_sems):
        (src,), (got,) = in_refs, out_refs
        ax, ay, ac = _position()
        chips = [(1 - ax, ay), (ax, 1 - ay), (1 - ax, 1 - ay)]
        _handshake([(cx, cy, ac) for cx, cy in chips])
        copies = [pltpu.make_async_remote_copy(
            src_ref=src.at[2 * cx + cy], dst_ref=got.at[r], send_sem=send_sems.at[r], recv_sem=recv_sems.at[r],
            device_id=(cx, cy, ac), device_id_type=MESH_IDS) for r, (cx, cy) in enumerate(chips)]
        for cp in copies:
            cp.start()
        for cp in copies:
            cp.wait_recv()
        for cp in copies:
            cp.wait_send()

    return _on_sequencer(body, [s1], [jax.ShapeDtypeStruct((3,) + s1.shape[1:], s1.dtype)],
                         [pltpu.SemaphoreType.DMA((3,)), pltpu.SemaphoreType.DMA((3,))], name=name, collective_id=collective_id)[0]


def _gather_async(xs, name, collective_id):
    rider = _gather_rider(xs)

    def body(in_refs, out_refs, send_sems, recv_sems):
        ax, ay, ac = _position()
        _handshake([(ax, ay, 1 - ac), (1 - ax, ay, ac), (ax, 1 - ay, ac), (1 - ax, 1 - ay, ac)])
        for hook in rider.hooks(in_refs, out_refs, send_sems, recv_sems):
            hook()

    return _on_sequencer(body, rider.arrays, rider.out_shapes, rider.sems, name=name, collective_id=collective_id)


def _scatter_async(parts, name, collective_id):
    rider = _scatter_rider(parts)

    def body(in_refs, out_refs, send_sems, recv_sems):
        ax, ay, ac = _position()
        flip = lambda a, on: 1 - a if on else a
        _handshake([(flip(ax, rel & 4), flip(ay, rel & 2), flip(ac, rel & 1)) for rel in range(1, NDEV)])
        for hook in rider.hooks(in_refs, out_refs, send_sems, recv_sems):
            hook()

    return _on_sequencer(body, rider.arrays, rider.out_shapes, rider.sems, name=name, collective_id=collective_id)[0]


def _sum_in_chip(own, recv, name):
    _, r, w = own.shape
    tr = _tile(r, (256, 128))

    def body(a_ref, b_ref, o_ref):
        o_ref[...] = (a_ref[...].astype(F32) + b_ref[...].astype(F32)).astype(o_ref.dtype)

    blk = pl.BlockSpec((None, tr, w), lambda q, i: (q, i, 0))
    return pl.pallas_call(body, name=name, grid=(4, r // tr), in_specs=[blk, blk], out_specs=blk,
                          out_shape=jax.ShapeDtypeStruct(own.shape, own.dtype),
                          compiler_params=_cparams("parallel", "parallel"))(own, recv)


def _sum_chips(s1, recv, chip, name):
    _, r, w = s1.shape
    tr = _tile(r, (256, 128))

    def body(c_ref, s_ref, r0_ref, r1_ref, r2_ref, o_ref):
        f = lambda ref: ref[...].astype(F32)
        o_ref[...] = ((f(s_ref) + f(r0_ref)) + f(r1_ref)) + f(r2_ref)

    rblk = lambda k: pl.BlockSpec((None, tr, w), functools.partial(lambda i, c, k: (k, i, 0), k=k))
    grid_spec = pltpu.PrefetchScalarGridSpec(
        num_scalar_prefetch=1, grid=(r // tr,),
        in_specs=[pl.BlockSpec((None, tr, w), lambda i, c: (c[0], i, 0)), rblk(0), rblk(1), rblk(2)],
        out_specs=pl.BlockSpec((tr, w), lambda i, c: (i, 0)))
    return pl.pallas_call(body, name=name, grid_spec=grid_spec, out_shape=jax.ShapeDtypeStruct((r, w), F32),
                          compiler_params=_cparams("parallel"))(chip, s1, recv, recv, recv)


def _silu_rows(x, name):
    def body(x_ref, o_ref):
        o_ref[...] = _silu(x_ref[...])

    return pl.pallas_call(body, name=name, out_shape=jax.ShapeDtypeStruct(x.shape, F32))(x)


def _row_sum(x, name):
    def body(x_ref, o_ref):
        acc = x_ref[0:1, :]
        for i in range(1, x.shape[0]):
            acc = acc + x_ref[i:i + 1, :]
        o_ref[...] = acc

    return pl.pallas_call(body, name=name, out_shape=jax.ShapeDtypeStruct((1, x.shape[1]), F32))(x)


def _adamw(w, g, m, v, name):
    cols = w.shape[-1]
    rows = w.size // cols
    tr = _tile(rows, (128,))
    tc = LANE if (tr == rows and rows > 512 and cols % LANE == 0) else cols

    def body(w_ref, g_ref, m_ref, v_ref, d_ref, mo_ref, vo_ref):
        grad = g_ref[...]
        m_new = ADAM_B1 * m_ref[...] + (1.0 - ADAM_B1) * grad
        v_new = ADAM_B2 * v_ref[...] + (1.0 - ADAM_B2) * jnp.square(grad)
        m_hat = m_new / (1.0 - ADAM_B1 ** ADAM_STEP)
        v_hat = v_new / (1.0 - ADAM_B2 ** ADAM_STEP)
        d_ref[...] = -ADAM_LR * (m_hat / (jnp.sqrt(v_hat) + ADAM_EPS) + ADAM_WD * w_ref[...])
        mo_ref[...] = m_new
        vo_ref[...] = v_new

    blk = pl.BlockSpec((tr, tc), lambda i, j: (i, j))
    out = pl.pallas_call(
        body, name=name, grid=(rows // tr, cols // tc), in_specs=[blk] * 4, out_specs=[blk] * 3,
        out_shape=[jax.ShapeDtypeStruct((rows, cols), F32)] * 3, compiler_params=_cparams("parallel", "parallel"),
    )(*[t.reshape(rows, cols) for t in (w, g, m, v)])
    return [t.reshape(w.shape) for t in out]


def _pack(parts, width, row_mult, dtype):
    flat = jnp.concatenate([p.reshape(-1).astype(dtype) for p in parts])
    rows = -(-flat.shape[0] // (width * row_mult)) * row_mult
    return jnp.pad(flat, (0, rows * width - flat.shape[0])).reshape(rows, width)


def _unpack(flat, shapes):
    out, off = [], 0
    for shp in shapes:
        size = 1
        for dim in shp:
            size *= dim
        out.append(flat[:, off:off + size].reshape((flat.shape[0],) + tuple(shp)))
        off += size
    return out


def _devices_to_cols(a):
    _, r, c = a.shape
    return a.transpose(1, 0, 2).reshape(r, NDEV * c)


def kernel(x, c, w_ada, b_ada, norm1_w, w_in, gdn_conv_w, gdn_a_log, gdn_dt_bias, gdn_norm_w, w_gdn_proj, sc_conv_w, w_sc_out, w_o, norm2_w, w_ffn_in, w_ffn_out, w_ada_f, b_ada_f, normf_w, loss_target, m_w_ada, m_b_ada, m_norm1_w, m_w_in, m_gdn_conv_w, m_gdn_a_log, m_gdn_dt_bias, m_gdn_norm_w, m_w_gdn_proj, m_sc_conv_w, m_w_sc_out, m_w_o, m_norm2_w, m_w_ffn_in, m_w_ffn_out, m_w_ada_f, m_b_ada_f, m_normf_w, v_w_ada, v_b_ada, v_norm1_w, v_w_in, v_gdn_conv_w, v_gdn_a_log, v_gdn_dt_bias, v_gdn_norm_w, v_w_gdn_proj, v_sc_conv_w, v_w_sc_out, v_w_o, v_norm2_w, v_w_ffn_in, v_w_ffn_out, v_w_ada_f, v_b_ada_f, v_normf_w):
    bl, s, d = x.shape
    heads = gdn_a_log.shape[-1]
    dff = w_ffn_out.shape[1] * NDEV
    tok = bl * s
    ax, ay, ac = _position()
    dev = 4 * ax + 2 * ay + ac
    as_tok = lambda a: a.reshape(bl, s, a.shape[-1])
    as_mat = lambda a: a.reshape(tok, a.shape[-1])

    small = _all_gather(_pack([c, gdn_conv_w, sc_conv_w], LANE, 8, F32), name="gather_cond", hbm=False)
    c_all, conv_w, sc_w = _unpack(small.reshape(NDEV, -1), [(bl, d), gdn_conv_w.shape[1:], sc_conv_w.shape[1:]])
    c_act = _silu_rows(c_all.reshape(NDEV * bl, d), "cond_silu")
    conv_w, sc_w = _devices_to_cols(conv_w), _devices_to_cols(sc_w)
    n_ada, n_adaf = w_ada.shape[-1], w_ada_f.shape[-1]
    bias = jnp.broadcast_to(lax.dynamic_slice_in_dim(b_ada, dev * n_ada, n_ada, axis=1), (NDEV * bl, n_ada))
    biasf = jnp.broadcast_to(lax.dynamic_slice_in_dim(b_ada_f.reshape(1, -1), dev * n_adaf, n_adaf, axis=1), (NDEV * bl, n_adaf))
    mod_cols = _mm(c_act, w_ada[0], add=bias, name="ada_cols")
    modf_cols = _mm(c_act, w_ada_f, add=biasf, name="adaf_cols")
    mods = _all_gather(jnp.concatenate([mod_cols, modf_cols], axis=1), name="gather_mod", hbm=False)
    mod_all = mods[:, :, :n_ada].transpose(1, 0, 2).reshape(NDEV * bl, NDEV * n_ada)
    modf_all = mods[:, :, n_ada:].transpose(1, 0, 2).reshape(NDEV * bl, NDEV * n_adaf)
    my_rows = lambda a: lax.dynamic_slice_in_dim(a, dev * bl, bl, axis=0)
    sh1, sc1, g1, sh2, sc2, g2 = [t.reshape(bl, 1, d) for t in jnp.split(my_rows(mod_all), 6, axis=1)]
    shf, scf = [t.reshape(bl, 1, d) for t in jnp.split(my_rows(modf_all), 2, axis=1)]

    late = [t.astype(MXU_DTYPE) for t in (w_gdn_proj[0], w_sc_out[0], w_o[0], w_ffn_in[0].T, w_ffn_out[0])]
    rows = [t.shape[0] for t in late] + [w_in.shape[-1]]
    offs = [sum(rows[:i]) for i in range(5)]
    in_send = w_in[0].T.astype(MXU_DTYPE)
    with_own = lambda g, own: lax.dynamic_update_slice_in_dim(g, own[None], dev, axis=0)
    (wt_in,) = _gather_async([in_send], "gather_w_in", 1)
    wt_in = with_own(wt_in, in_send).reshape(NDEV * rows[5], d)
    gathered = _gather_async(late[:3], "gather_mixer", 2) + _gather_async(late[3:], "gather_ffn", 3)
    wgp, wso, wo, wt_fi, wfo = [with_own(g, own).reshape(NDEV * own.shape[0], d) for g, own in zip(gathered, late)]
    o_z, o_ab, o_sc, o_ga, o_gb = 3 * d, 4 * d, 4 * d + 2 * heads, 7 * d + 2 * heads, 8 * d + 2 * heads
    s_qkv, s_z, s_sc, s_gate = (0, o_z), (o_z, d), (o_sc, 3 * d), (o_ga, 2 * d)
    wt_ab = jnp.pad(wt_in[o_ab:o_sc], ((0, LANE - 2 * heads), (0, 0)))

    n1w, n2w, nfw = norm1_w.reshape(1, d), norm2_w.reshape(1, d), normf_w.reshape(1, d)
    lanes = lambda a: jnp.pad(a.reshape(1, -1), ((0, 0), (0, LANE - a.size)))
    a_log, dt_bias, gnw = lanes(gdn_a_log), lanes(gdn_dt_bias), gdn_norm_w.reshape(1, HEAD)
    f_gates = functools.partial(_f_gates, heads=heads)
    (h1,) = _tok_fwd(_f_norm_mod, [x], [sh1, sc1], [n1w], [(d, MXU_DTYPE)], name="norm1", ts=512)
    h1m = as_mat(h1)
    p_qkv = as_tok(_mm(h1m, wt_in, tb=True, b_rows=s_qkv, name="in_qkv"))
    p_z = as_tok(_mm(h1m, wt_in, tb=True, b_rows=s_z, name="in_z"))
    p_ab = as_tok(_mm(h1m, wt_ab, tb=True, name="in_ab"))
    p_sc = as_tok(_mm(h1m, wt_in, tb=True, b_rows=s_sc, name="in_sc"))
    p_g = as_tok(_mm(h1m, wt_in, tb=True, b_rows=s_gate, name="in_gate"))
    qkv = _qkv_fwd(p_qkv, conv_w, heads, "qkv_conv")
    (gbeta,) = _tok_fwd(f_gates, [p_ab], [], [a_log, dt_bias], [(LANE, F32)], name="gates", ts=512)
    o, s_all, t_all = _gdn_fwd(qkv, gbeta, heads, "gdn")
    (og,) = _tok_fwd(_f_gdn_out, [o, p_z], [], [(gnw, None)], [(d, MXU_DTYPE)], name="gdn_out", ts=2048, wb=HEAD, cols=heads)
    y_a = as_tok(_mm(as_mat(og), wgp, name="gdn_proj"))
    scp = _sc_fwd(p_sc, sc_w, "sc_conv")
    mrg, y_b = _tok_fwd(_f_merge_keep, [(p_g, 0), (p_g, 1), y_a, _Product(scp, wso)], [], [], [(d, MXU_DTYPE), (d, F32)],
                        name="merge", ts=256, wb=d)
    merge_toks = [(p_g, 0), (p_g, 1), y_a, y_b]
    x2, h2, mix = _tok_fwd(_f_res_norm_mod_keep, [x, _Product(mrg, wo)], [g1, sh2, sc2], [n2w],
                           [(d, F32), (d, MXU_DTYPE), (d, F32)], name="norm2", ts=512)
    act, gu_a, gu_b = _ffn_in_swiglu(as_mat(h2), wt_fi, dff, "ffn_in")

    loss_l, (dx2, dff_out, _), (dg2, dshf, dscf), (dnfw,) = _tok_bwd(
        _f_loss, [x2, _Product(as_tok(act), wfo), loss_target], [g2, shf, scf], [nfw], [], [True, True, False], name="loss",
        ts=256, loss=True, tok_dtype=[F32, MXU_DTYPE, None])
    dffm = as_mat(dff_out)
    dgu_a, dgu_b = _ffn_out_bwd_swiglu(dffm, wfo, gu_a, gu_b, "d_ffn_out")
    gmm = functools.partial(_mm, ta=True, out_dtype=MXU_DTYPE)
    gw_ffn_out = gmm(act, dffm, name="g_ffn_out")
    dh2 = _Product(as_tok(dgu_b), wt_fi, b_rows=(dff, dff), add=as_tok(_mm(dgu_a, wt_fi, b_rows=(0, dff), name="d_ffn_in_a")))
    h2m = as_mat(h2)
    gwt_ffn_in = gmm(dgu_a, h2m, out_rows=2 * dff, name="g_ffn_in_a")
    gwt_ffn_in = gmm(dgu_b, h2m, out_rows=2 * dff, row_off=dff, into=gwt_ffn_in, name="g_ffn_in_b")
    ffn_parts = [(gwt_ffn_in, rows[3]), (gw_ffn_out, rows[4])]
    ffn_recv = _scatter_async(ffn_parts, "scatter_ffn", 4)
    (dx_skip, dmix), (dg1, dsh2, dsc2), (dn2w,) = _tok_bwd(
        _f_res_norm_mod, [x, mix], [g1, sh2, sc2], [n2w], [dx2, dh2], [True, True], name="d_norm2", ts=256,
        tok_dtype=[F32, MXU_DTYPE], after=[gwt_ffn_in, gw_ffn_out])
    gw_o = gmm(as_mat(mrg), as_mat(dmix), name="g_mix_out")
    (dga, dgb, dya, dyb), _, _ = _tok_bwd(_f_merge, merge_toks, [], [], [_Product(dmix, wo, tb=True)], [True] * 4,
                                          name="d_merge", ts=256, wb=d, tok_dtype=MXU_DTYPE)
    dyam, dybm = as_mat(dya), as_mat(dyb)
    dog = as_tok(_mm(dyam, wgp, tb=True, name="d_gdn_proj"))
    gw_gdn_proj = gmm(as_mat(og), dyam, name="g_gdn_proj")
    dscp = as_tok(_mm(dybm, wso, tb=True, name="d_sc_out"))
    gw_sc_out = gmm(as_mat(scp), dybm, name="g_sc_out")
    dsc, g_sc_w = _sc_bwd(p_sc, sc_w, dscp, "d_sc_conv")
    mix_parts = [(gw_gdn_proj, rows[0]), (gw_sc_out, rows[1]), (gw_o, rows[2])]
    mix_recv = _scatter_async(mix_parts, "scatter_mixer", 5)
    (do, dz), _, (g_gnw,) = _tok_bwd(_f_gdn_out, [o, p_z], [], [(gnw, None)], [dog], [True, True], name="d_gdn_out",
                                     ts=2048, wb=HEAD, cols=heads, tok_dtype=[F32, MXU_DTYPE],
                                     after=[gw_gdn_proj, gw_sc_out, gw_o])
    own_rows = lambda parts: jnp.concatenate([lax.dynamic_slice_in_dim(g, dev * r, r, axis=0) for g, r in parts], axis=0)
    dqkv, dgbeta = _gdn_bwd(qkv, gbeta, do, s_all, t_all, heads, "d_gdn")
    dp_qkv, g_conv_w = _qkv_bwd(p_qkv, conv_w, dqkv, heads, "d_qkv_conv")
    ffn_red = _sum_direct(own_rows(ffn_parts), ffn_recv, "sum_ffn")
    mix_red = _sum_direct(own_rows(mix_parts), mix_recv, "sum_mix")
    (dp_ab,), _, (g_a_log, g_dt_bias) = _tok_bwd(f_gates, [p_ab], [], [a_log, dt_bias], [dgbeta], [True], name="d_gates",
                                                 ts=512, tok_dtype=MXU_DTYPE, after=[ffn_red, mix_red])
    sections = [(dp_qkv, s_qkv), (dz, s_z), (dp_ab, None), (dsc[0], (o_sc, d)), (dsc[1], (o_sc + d, d)), (dsc[2], (o_sc + 2 * d, d)),
                (dga, (o_ga, d)), (dgb, (o_gb, d))]
    gwt_in = [gmm(as_mat(dp), h1m, name=f"g_in_{k}") for k, (dp, _) in enumerate(sections)]
    gwt_in[2] = gwt_in[2][:2 * heads]

    r_in = rows[5]
    win = -(-(r_in + max(r_in * k % ROW_ALIGN for k in range(NDEV))) // 128) * 128
    need_rows = max(_window_start(r_in, k) for k in range(NDEV)) + win
    gwt_in = jnp.concatenate(gwt_in + [jnp.zeros((need_rows - NDEV * r_in, d), MXU_DTYPE)], axis=0)
    recv1 = _exchange_in_chip([(gwt_in, r_in, win, 0)], "scatter_in_chip", 7)
    own = jnp.stack([lax.dynamic_slice_in_dim(gwt_in, _window_start(r_in, 2 * q + ac), win, axis=0) for q in range(4)])
    s1 = _sum_in_chip(own, recv1, "sum_in_chip")
    recv2 = _exchange_chips_async(s1, "scatter_chips", 6)

    dh1 = _mm(as_mat(dp_ab), wt_ab, name="d_in_ab")
    wide = [as_mat(dp_qkv), as_mat(dz), dsc.reshape(3, tok, d), as_mat(dga)]
    tk_in = d
    assert d <= 1024
    dh1 = _mm_chain(wide, wt_in, lambda t: tk_in * t + jnp.where(t * tk_in >= o_ab, 2 * heads, 0), add=dh1, name="d_in", tk=tk_in)
    dh1 = _Product(sections[-1][0], wt_in, b_rows=sections[-1][1], add=as_tok(dh1))
    (grad_x,), (dsh1, dsc1), (dn1w,) = _tok_bwd(_f_norm_mod_skip, [x], [sh1, sc1], [n1w], [dh1, dx_skip], [True],
                                                name="d_norm1", ts=256)
    reduced = _sum_chips(s1, recv2, (2 * ax + ay).reshape(1).astype(jnp.int32), "sum_chips")
    gt_w_in = lax.dynamic_slice_in_dim(reduced, r_in * dev - _window_start(r_in, dev), r_in, axis=0)
    g_w_in = gt_w_in.T.reshape(w_in.shape)
    gt_w_ffn_in = ffn_red[:rows[3]]
    g_w_ffn_in = gt_w_ffn_in.T.reshape(w_ffn_in.shape)
    g_w_ffn_out = ffn_red[rows[3]:].reshape(w_ffn_out.shape)
    g_w_gdn_proj, g_w_sc_out, g_w_o = (mix_red[offs[i]:offs[i] + rows[i]].reshape(ref.shape)
                                       for i, ref in enumerate((w_gdn_proj, w_sc_out, w_o)))

    dmod = jnp.concatenate([t.reshape(bl, d) for t in (dsh1, dsc1, dg1, dsh2, dsc2, dg2)], axis=1)
    dmodf = jnp.concatenate([t.reshape(bl, d) for t in (dshf, dscf)], axis=1)
    summed_parts = [dn1w, dn2w, dnfw, g_gnw, g_a_log, g_dt_bias, g_conv_w, g_sc_w, loss_l]
    partial = _all_gather(_pack([dmod, dmodf] + summed_parts, LANE, 8, F32), name="gather_small", hbm=False)
    partial = partial.reshape(NDEV, -1)
    n_rows = bl * (6 * d + 2 * d)
    dmod_all, dmodf_all = _unpack(partial[:, :n_rows], [(bl, 6 * d), (bl, 2 * d)])
    dmod_all, dmodf_all = dmod_all.reshape(NDEV * bl, 6 * d), dmodf_all.reshape(NDEV * bl, 2 * d)
    totals = _row_sum(partial[:, n_rows:], "sum_small")
    t_n1w, t_n2w, t_nfw, t_gnw, t_a_log, t_dt_bias, t_conv_w, t_sc_w, t_loss = [
        t[0] for t in _unpack(totals, [p.shape for p in summed_parts])]
    my_cols = lambda a, n: lax.dynamic_slice_in_dim(a, dev * n, n, axis=1)
    grads = {
        "w_ada": _mm(c_act, my_cols(dmod_all, n_ada), ta=True, name="g_ada").reshape(w_ada.shape),
        "b_ada": _row_sum(dmod_all, "g_ada_bias").reshape(b_ada.shape),
        "norm1_w": t_n1w.reshape(norm1_w.shape),
        "w_in": g_w_in,
        "gdn_conv_w": my_cols(t_conv_w, gdn_conv_w.shape[-1]).reshape(gdn_conv_w.shape),
        "gdn_a_log": t_a_log[:, :heads].reshape(gdn_a_log.shape),
        "gdn_dt_bias": t_dt_bias[:, :heads].reshape(gdn_dt_bias.shape),
        "gdn_norm_w": t_gnw.reshape(gdn_norm_w.shape),
        "w_gdn_proj": g_w_gdn_proj,
        "sc_conv_w": my_cols(t_sc_w, sc_conv_w.shape[-1]).reshape(sc_conv_w.shape),
        "w_sc_out": g_w_sc_out,
        "w_o": g_w_o,
        "norm2_w": t_n2w.reshape(norm2_w.shape),
        "w_ffn_in": g_w_ffn_in,
        "w_ffn_out": g_w_ffn_out,
        "w_ada_f": _mm(c_act, my_cols(dmodf_all, n_adaf), ta=True, name="g_adaf").reshape(w_ada_f.shape),
        "b_ada_f": _row_sum(dmodf_all, "g_adaf_bias").reshape(b_ada_f.shape),
        "normf_w": t_nfw.reshape(normf_w.shape),
    }
    weights = dict(w_ada=w_ada, b_ada=b_ada, norm1_w=norm1_w, w_in=w_in, gdn_conv_w=gdn_conv_w, gdn_a_log=gdn_a_log,
                   gdn_dt_bias=gdn_dt_bias, gdn_norm_w=gdn_norm_w, w_gdn_proj=w_gdn_proj, sc_conv_w=sc_conv_w,
                   w_sc_out=w_sc_out, w_o=w_o, norm2_w=norm2_w, w_ffn_in=w_ffn_in, w_ffn_out=w_ffn_out, w_ada_f=w_ada_f,
                   b_ada_f=b_ada_f, normf_w=normf_w)
    m_in = [m_w_ada, m_b_ada, m_norm1_w, m_w_in, m_gdn_conv_w, m_gdn_a_log, m_gdn_dt_bias, m_gdn_norm_w, m_w_gdn_proj,
            m_sc_conv_w, m_w_sc_out, m_w_o, m_norm2_w, m_w_ffn_in, m_w_ffn_out, m_w_ada_f, m_b_ada_f, m_normf_w]
    v_in = [v_w_ada, v_b_ada, v_norm1_w, v_w_in, v_gdn_conv_w, v_gdn_a_log, v_gdn_dt_bias, v_gdn_norm_w, v_w_gdn_proj,
            v_sc_conv_w, v_w_sc_out, v_w_o, v_norm2_w, v_w_ffn_in, v_w_ffn_out, v_w_ada_f, v_b_ada_f, v_normf_w]
    deltas, new_m, new_v = [], [], []
    grads_t = {"w_in": gt_w_in, "w_ffn_in": gt_w_ffn_in}
    for (wname, wt), mt, vt in zip(weights.items(), m_in, v_in):
        if wname in grads_t:
            back = lambda a, wt=wt: a.T.reshape(wt.shape)
            dl, mn, vn = (back(a) for a in _adamw(wt[0].T, grads_t[wname], mt[0].T, vt[0].T, "adamw_" + wname))
        else:
            dl, mn, vn = _adamw(wt, grads[wname], mt, vt, "adamw_" + wname)
        deltas.append(dl)
        new_m.append(mn)
        new_v.append(vn)
    loss = t_loss[0, 0]
    return (loss, grad_x, *[grads[k] for k in weights], *deltas, *new_m, *new_v)
```

```python
import functools

import jax
import jax.numpy as jnp
from jax import lax
from jax.experimental import pallas as pl
from jax.experimental.pallas import tpu as pltpu
from jax.experimental.pallas import tpu_sc as plsc

F32 = jnp.float32
MXU_DTYPE = jnp.bfloat16
NDEV = 8
CHUNK = 64
HEAD = 128
LANE = 128
EPS = 1e-6
ADAM_LR, ADAM_B1, ADAM_B2, ADAM_EPS, ADAM_WD, ADAM_STEP = 0.001, 0.9, 0.999, 1e-08, 0.01, 10
VMEM_LIMIT = 48 * 1024 * 1024
MESH_IDS = pl.DeviceIdType.MESH
HIGHEST = lax.Precision.HIGHEST


def _tile(n, cands=(512, 256, 128)):
    for c in cands:
        if n % c == 0:
            return c
    return n


def _cparams(*sem):
    return pltpu.CompilerParams(dimension_semantics=sem, vmem_limit_bytes=VMEM_LIMIT)


def _mm(a, b, *, ta=False, tb=False, add=None, out_dtype=F32, name, b_rows=None, out_rows=None, row_off=0, into=None):
    m, k = (a.shape[1], a.shape[0]) if ta else a.shape
    b_shape = b.shape if b_rows is None else (b_rows[1], b.shape[1])
    n = b_shape[0] if tb else b_shape[1]
    assert k == (b_shape[1] if tb else b_shape[0])
    if ta:
        tm, tn = _tile(m), n if n <= 1024 else _tile(n)
        tk = k if k <= 4096 else _tile(k, (4096, 2048, 1024, 512))
        if tm * tk > 1024 * 2048:
            tk = _tile(k, (2048, 1024, 512))
    else:
        tk = k if k <= 1024 else _tile(k, (1024, 512))
        tn = _tile(n, (1024 if tk <= 1024 else 512, 512, 256, 128))
        tm = _tile(m, (2048 if (tn <= 512 and tk <= 1024) else 1024, 1024, 512, 256, 128))
    nk = k // tk
    dims = (((0 if ta else 1,), (1 if tb else 0,)), ((), ()))
    has_add = add is not None

    def body(*refs):
        a_ref, b_ref = refs[0], refs[1]
        add_ref = refs[2] if has_add else None
        o_ref = refs[2 + has_add + (into is not None)]
        part = lax.dot_general(a_ref[...].astype(MXU_DTYPE), b_ref[...].astype(MXU_DTYPE), dims,
                               preferred_element_type=F32)

        def finish(acc):
            if has_add:
                acc = acc + add_ref[...]
            o_ref[...] = acc.astype(o_ref.dtype)

        if nk == 1:
            finish(part)
        else:
            acc_ref = refs[-1]
            kk = pl.program_id(2)

            @pl.when(kk == 0)
            def _():
                acc_ref[...] = part

            @pl.when(kk > 0)
            def _():
                acc_ref[...] += part

            @pl.when(kk == nk - 1)
            def _():
                finish(acc_ref[...])

    a_spec = pl.BlockSpec((tk, tm), lambda i, j, kk: (kk, i)) if ta else pl.BlockSpec((tm, tk), lambda i, j, kk: (i, kk))
    if b_rows is None:
        b_spec = pl.BlockSpec((tn, tk), lambda i, j, kk: (j, kk)) if tb else pl.BlockSpec((tk, tn), lambda i, j, kk: (kk, j))
    else:
        at = lambda t: pl.multiple_of(b_rows[0] + t, ROW_ALIGN)
        b_spec = (pl.BlockSpec((pl.Element(tn), pl.Element(tk)), lambda i, j, kk: (at(j * tn), kk * tk)) if tb else
                  pl.BlockSpec((pl.Element(tk), pl.Element(tn)), lambda i, j, kk: (at(kk * tk), j * tn)))
    add_spec = pl.BlockSpec((tm, tn), lambda i, j, kk: (i, j))
    assert row_off % tm == 0
    o_spec = pl.BlockSpec((tm, tn), lambda i, j, kk: (i + row_off // tm, j))
    in_specs = [a_spec, b_spec] + ([add_spec] if has_add else []) + ([pl.BlockSpec(memory_space=pl.ANY)] if into is not None else [])
    args = [a, b] + ([add] if has_add else []) + ([into] if into is not None else [])
    return pl.pallas_call(
        body, name=name, grid=(m // tm, n // tn, nk), in_specs=in_specs, out_specs=o_spec,
        out_shape=jax.ShapeDtypeStruct((out_rows or m, n), out_dtype),
        scratch_shapes=[pltpu.VMEM((tm, tn), F32)] if nk > 1 else [],
        input_output_aliases={len(args) - 1: 0} if into is not None else {},
        compiler_params=_cparams("parallel", "parallel", "arbitrary"),
    )(*args)


def _mm_chain(parts, b, row_of_tile, *, add, name, tk=1024, tm=1024):
    m, n = parts[0].shape[-2], b.shape[1]
    tm = min(tm, m)
    tiles = [p.shape[0] if p.ndim == 3 else p.shape[1] // tk for p in parts]
    first = [sum(tiles[:s]) for s in range(len(parts))]
    nk = sum(tiles)

    def body(*refs):
        a_refs, b_ref, add_ref, o_ref, acc_ref = refs[:len(parts)], *refs[len(parts):]
        kk = pl.program_id(1)

        @pl.when(kk == 0)
        def _():
            acc_ref[...] = add_ref[...]

        for a_ref, lo, cnt in zip(a_refs, first, tiles):
            @pl.when(jnp.logical_and(kk >= lo, kk < lo + cnt))
            def _(a_ref=a_ref):
                acc_ref[...] += lax.dot_general(a_ref[...].astype(MXU_DTYPE), b_ref[...].astype(MXU_DTYPE),
                                                (((1,), (0,)), ((), ())), preferred_element_type=F32)

        @pl.when(kk == nk - 1)
        def _():
            o_ref[...] = acc_ref[...]

    tile_of = lambda kk, lo, cnt: jnp.clip(kk - lo, 0, cnt - 1)
    a_specs = [pl.BlockSpec((None, tm, tk), functools.partial(lambda i, kk, lo, cnt: (tile_of(kk, lo, cnt), i, 0), lo=lo, cnt=cnt))
               if p.ndim == 3 else
               pl.BlockSpec((tm, tk), functools.partial(lambda i, kk, lo, cnt: (i, tile_of(kk, lo, cnt)), lo=lo, cnt=cnt))
               for p, lo, cnt in zip(parts, first, tiles)]
    b_spec = pl.BlockSpec((pl.Element(tk), pl.Element(n)), lambda i, kk: (pl.multiple_of(row_of_tile(kk), ROW_ALIGN), 0))
    o_spec = pl.BlockSpec((tm, n), lambda i, kk: (i, 0))
    return pl.pallas_call(
        body, name=name, grid=(m // tm, nk), in_specs=a_specs + [b_spec, o_spec], out_specs=o_spec,
        out_shape=jax.ShapeDtypeStruct((m, n), F32), scratch_shapes=[pltpu.VMEM((tm, n), F32)],
        compiler_params=_cparams("parallel", "arbitrary"),
    )(*parts, b, add)


def _gmm_chain(parts, h, out_rows, row_of_tile, *, name, tm=1024, tk=1024):
    t, n = h.shape
    tiles = [p.shape[0] if p.ndim == 3 else p.shape[1] // tm for p in parts]
    first = [sum(tiles[:s]) for s in range(len(parts))]
    nk = t // tk

    def body(*refs):
        a_refs, h_ref, o_ref, acc_ref = refs[:len(parts)], *refs[len(parts):]
        r, kk = pl.program_id(0), pl.program_id(1)
        for a_ref, lo, cnt in zip(a_refs, first, tiles):
            @pl.when(jnp.logical_and(r >= lo, r < lo + cnt))
            def _(a_ref=a_ref):
                part = lax.dot_general(a_ref[...].astype(MXU_DTYPE), h_ref[...].astype(MXU_DTYPE), (((0,), (0,)), ((), ())),
                                       preferred_element_type=F32)
                _accumulate(acc_ref, part, kk == 0)

        @pl.when(kk == nk - 1)
        def _():
            o_ref[...] = acc_ref[...].astype(o_ref.dtype)

    tile_of = lambda r, lo, cnt: jnp.clip(r - lo, 0, cnt - 1)
    a_specs = [pl.BlockSpec((None, tk, tm), functools.partial(lambda r, kk, lo, cnt: (tile_of(r, lo, cnt), kk, 0), lo=lo, cnt=cnt))
               if p.ndim == 3 else
               pl.BlockSpec((tk, tm), functools.partial(lambda r, kk, lo, cnt: (kk, tile_of(r, lo, cnt)), lo=lo, cnt=cnt))
               for p, lo, cnt in zip(parts, first, tiles)]
    return pl.pallas_call(
        body, name=name, grid=(sum(tiles), nk), in_specs=a_specs + [pl.BlockSpec((tk, n), lambda r, kk: (kk, 0))],
        out_specs=pl.BlockSpec((pl.Element(tm), pl.Element(n)), lambda r, kk: (pl.multiple_of(row_of_tile(r), ROW_ALIGN), 0)),
        out_shape=jax.ShapeDtypeStruct((out_rows, n), MXU_DTYPE), scratch_shapes=[pltpu.VMEM((tm, n), F32)],
        compiler_params=_cparams("arbitrary", "arbitrary"),
    )(*parts, h)


def _swiglu_tiles(m, half):
    tn = _tile(half, (512, 256, 128))
    return _tile(m, (2048 if tn <= 256 else 1024, 1024, 512, 256, 128)), tn


def _ffn_in_swiglu(h, wt, half, name):
    m, k = h.shape
    tm, tn = _swiglu_tiles(m, half)
    nj = half // tn
    dims = (((1,), (1,)), ((), ()))

    def body(h_ref, wa_ref, wb_ref, act_ref, a_ref, b_ref):
        lhs = h_ref[...].astype(MXU_DTYPE)
        a = lax.dot_general(lhs, wa_ref[...].astype(MXU_DTYPE), dims, preferred_element_type=F32)
        b = lax.dot_general(lhs, wb_ref[...].astype(MXU_DTYPE), dims, preferred_element_type=F32)
        act_ref[...] = (_silu(a) * b).astype(act_ref.dtype)
        a_ref[...] = a.astype(a_ref.dtype)
        b_ref[...] = b.astype(b_ref.dtype)

    out = jax.ShapeDtypeStruct((m, half), MXU_DTYPE)
    oblk = pl.BlockSpec((tm, tn), lambda i, j: (i, j))
    return pl.pallas_call(
        body, name=name, grid=(m // tm, nj),
        in_specs=[pl.BlockSpec((tm, k), lambda i, j: (i, 0)), pl.BlockSpec((tn, k), lambda i, j: (j, 0)),
                  pl.BlockSpec((tn, k), lambda i, j: (j + nj, 0))],
        out_specs=[oblk, oblk, oblk], out_shape=[out, out, out], compiler_params=_cparams("parallel", "parallel"),
    )(h, wt, wt)


def _ffn_out_bwd_swiglu(dff, w, a, b, name):
    m, k = dff.shape
    half = w.shape[0]
    tm, tn = _swiglu_tiles(m, half)

    def body(d_ref, w_ref, a_ref, b_ref, da_ref, db_ref):
        dact = lax.dot_general(d_ref[...].astype(MXU_DTYPE), w_ref[...].astype(MXU_DTYPE), (((1,), (1,)), ((), ())),
                               preferred_element_type=F32)
        av, bv = a_ref[...].astype(F32), b_ref[...].astype(F32)
        sig = jax.nn.sigmoid(av)
        da_ref[...] = (dact * bv * (sig * (1.0 + av * (1.0 - sig)))).astype(da_ref.dtype)
        db_ref[...] = (dact * (av * sig)).astype(db_ref.dtype)

    out = jax.ShapeDtypeStruct((m, half), MXU_DTYPE)
    oblk = pl.BlockSpec((tm, tn), lambda i, j: (i, j))
    return pl.pallas_call(
        body, name=name, grid=(m // tm, half // tn),
        in_specs=[pl.BlockSpec((tm, k), lambda i, j: (i, 0)), pl.BlockSpec((tn, k), lambda i, j: (j, 0)), oblk, oblk],
        out_specs=[oblk, oblk], out_shape=[out, out], compiler_params=_cparams("parallel", "parallel"),
    )(dff, w, a, b)


def _with_off(xs):
    return [x if isinstance(x, tuple) else (x, 0) for x in xs]


def _spec(kind, arr, off, ts, wb):
    w = arr.shape[-1] if wb is None else wb
    col = (lambda j: 0) if wb is None else functools.partial(lambda j, o: o + j, o=off)
    if kind == "tok":
        return pl.BlockSpec((None, ts, w), lambda j, b, i: (b, i, col(j)))
    if kind == "bat":
        return pl.BlockSpec((None, 1, w), lambda j, b, i: (b, 0, col(j)))
    if off is None:
        return pl.BlockSpec(arr.shape, lambda j, b, i: (0, 0))
    return pl.BlockSpec((arr.shape[0], w), lambda j, b, i: (0, col(j)))


class _Product:
    def __init__(self, a, b, *, tb=False, b_rows=None, add=None):
        self.a, self.b, self.tb, self.b_rows, self.add = a, b, tb, b_rows, add
        rows = b.shape[0] if b_rows is None else b_rows[1]
        self.shape = a.shape[:2] + (rows if tb else b.shape[1],)

    def inputs(self, ts):
        a_spec = pl.BlockSpec((None, ts, self.a.shape[2]), lambda j, b, i: (b, i, 0))
        if self.b_rows is None:
            b_spec = pl.BlockSpec(self.b.shape, lambda j, b, i: (0, 0))
        else:
            start, count = self.b_rows
            b_spec = pl.BlockSpec((pl.Element(count), pl.Element(self.b.shape[1])), lambda j, b, i: (start, 0))
        extra = [] if self.add is None else [(self.add, pl.BlockSpec((None, ts, self.shape[2]), lambda j, b, i: (b, i, 0)))]
        return [(self.a, a_spec), (self.b, b_spec)] + extra

    def value(self, refs):
        dims = (((1,), (1 if self.tb else 0,)), ((), ()))
        val = lax.dot_general(refs[0][...].astype(MXU_DTYPE), refs[1][...].astype(MXU_DTYPE), dims, preferred_element_type=F32)
        return val if self.add is None else val + refs[2][...].astype(F32)


def _inputs(groups, kinds, ts, wb):
    loaded = [(a, _spec(kind, a, o, ts, wb)) for g, kind in zip(groups, kinds) for a, o in g if not isinstance(a, _Product)]
    made = [pair for g in groups for a, _ in g if isinstance(a, _Product) for pair in a.inputs(ts)]
    return [a for a, _ in loaded + made], [sp for _, sp in loaded + made]


def _values(refs, groups):
    n_loaded = sum(1 for g in groups for a, _ in g if not isinstance(a, _Product))
    loaded, pos, out = iter(refs[:n_loaded]), n_loaded, []
    for g in groups:
        vals = []
        for a, _ in g:
            if isinstance(a, _Product):
                k = 2 if a.add is None else 3
                vals.append(a.value(refs[pos:pos + k]))
                pos += k
            else:
                vals.append(next(loaded)[...].astype(F32))
        out.append(vals)
    return out, pos


def _tok_fwd(fn, toks, bats, pars, outs, *, name, ts, wb=None, cols=1):
    groups = [_with_off(toks), _with_off(bats), _with_off(pars)]
    bl, s, _ = groups[0][0][0].shape
    ts = min(ts, s)
    args, in_specs = _inputs(groups, ("tok", "bat", "par"), ts, wb)

    def body(*refs):
        vals, n_in = _values(refs, groups)
        res = fn(*[v for g in vals for v in g])
        for r, val in zip(refs[n_in:], res):
            r[...] = val.astype(r.dtype)

    out_specs = [pl.BlockSpec((None, ts, w if wb is None else wb), lambda j, b, i: (b, i, j)) for w, _ in outs]
    return pl.pallas_call(
        body, name=name, grid=(cols, bl, s // ts), in_specs=in_specs,
        out_specs=out_specs, out_shape=[jax.ShapeDtypeStruct((bl, s, w), dt) for w, dt in outs],
        compiler_params=_cparams("parallel", "parallel", "parallel"),
    )(*args)


def _accumulate(ref, val, first):
    @pl.when(first)
    def _():
        ref[...] = val

    @pl.when(jnp.logical_not(first))
    def _():
        ref[...] += val


def _tok_bwd(fn, toks, bats, pars, cots, need, *, name, ts, wb=None, cols=1, tok_dtype=F32, loss=False, after=()):
    toks, bats, pars, cots = _with_off(toks), _with_off(bats), _with_off(pars), _with_off(cots)
    groups = [toks, bats, pars, cots]
    bl, s, _ = toks[0][0].shape
    ts = min(ts, s)
    nt, nb, npar = len(toks), len(bats), len(pars)
    args, in_specs = _inputs(groups, ("tok", "bat", "par", "tok"), ts, wb)
    args, in_specs = args + list(after), in_specs + [pl.BlockSpec(memory_space=pl.ANY)] * len(after)

    def body(*refs):
        j, b, i = pl.program_id(0), pl.program_id(1), pl.program_id(2)
        (tok_vals, bat_vals, par_vals, cot_vals), o = _values(refs, groups)
        o += len(after)
        outs, vjp = jax.vjp(fn, *tok_vals, *bat_vals, *par_vals)
        if loss:
            ct = (jnp.ones_like(outs[0]),)
            tot = jnp.broadcast_to(jnp.sum(outs[0], keepdims=True), (1, LANE))
            _accumulate(refs[o], tot, jnp.logical_and(b == 0, i == 0))
            o += 1
        else:
            ct = tuple(cot_vals)
        grads = vjp(ct)
        for t in range(nt):
            if need[t]:
                refs[o][...] = grads[t].astype(refs[o].dtype)
                o += 1
        for t in range(nb):
            _accumulate(refs[o], grads[nt + t], i == 0)
            o += 1
        for t in range(npar):
            first = jnp.logical_and(b == 0, i == 0)
            if pars[t][1] is None:
                first = jnp.logical_and(first, j == 0)
            _accumulate(refs[o], grads[nt + nb + t], first)
            o += 1

    full = lambda arr: arr.shape[-1] if wb is None else wb * cols
    blk = lambda arr: arr.shape[-1] if wb is None else wb
    out_specs, out_shape = [], []
    if loss:
        out_specs.append(pl.BlockSpec((1, LANE), lambda j, b, i: (0, 0)))
        out_shape.append(jax.ShapeDtypeStruct((1, LANE), F32))
    for t in range(nt):
        if need[t]:
            out_specs.append(pl.BlockSpec((None, ts, blk(toks[t][0])), lambda j, b, i: (b, i, j)))
            dt = tok_dtype[t] if isinstance(tok_dtype, (list, tuple)) else tok_dtype
            out_shape.append(jax.ShapeDtypeStruct((bl, s, full(toks[t][0])), dt))
    for arr, _ in bats:
        out_specs.append(pl.BlockSpec((None, 1, blk(arr)), lambda j, b, i: (b, 0, j)))
        out_shape.append(jax.ShapeDtypeStruct((bl, 1, full(arr)), F32))
    for arr, off in pars:
        if off is None:
            out_specs.append(pl.BlockSpec(arr.shape, lambda j, b, i: (0, 0)))
            out_shape.append(jax.ShapeDtypeStruct(arr.shape, F32))
        else:
            out_specs.append(pl.BlockSpec((arr.shape[0], blk(arr)), lambda j, b, i: (0, j)))
            out_shape.append(jax.ShapeDtypeStruct((arr.shape[0], full(arr)), F32))
    res = list(pl.pallas_call(
        body, name=name, grid=(cols, bl, s // ts), in_specs=in_specs,
        out_specs=out_specs, out_shape=out_shape, compiler_params=_cparams("arbitrary", "arbitrary", "arbitrary"),
    )(*args))
    tot = res.pop(0) if loss else None
    dtoks = [res.pop(0) if need[t] else None for t in range(nt)]
    dbats = [res.pop(0) for _ in range(nb)]
    dpars = [res.pop(0) for _ in range(npar)]
    return (tot, dtoks, dbats, dpars) if loss else (dtoks, dbats, dpars)


def _silu(x):
    return x * jax.nn.sigmoid(x)


def _rms(x, w):
    return x * lax.rsqrt(jnp.mean(x * x, axis=-1, keepdims=True) + EPS) * w


def _f_norm_mod(x, shift, scale, w):
    return (_rms(x, w) * (1.0 + scale) + shift,)


def _f_norm_mod_skip(x, shift, scale, w):
    return _rms(x, w) * (1.0 + scale) + shift, x


def _f_res_norm_mod(x, mix, gate, shift, scale, w):
    x2 = x + gate * mix
    return x2, _rms(x2, w) * (1.0 + scale) + shift


def _f_res_norm_mod_keep(x, mix, gate, shift, scale, w):
    return (*_f_res_norm_mod(x, mix, gate, shift, scale, w), mix)


def _f_gates(p, a_log, dt_bias, *, heads):
    z = p + dt_bias
    g = -jnp.exp(a_log) * (jnp.maximum(z, 0.0) + jnp.log1p(jnp.exp(jnp.minimum(z, -z))))
    lane = lax.broadcasted_iota(jnp.int32, p.shape, 1)
    return (jnp.where(lane < heads, g, jax.nn.sigmoid(p)),)


def _f_gdn_out(o, z, w):
    return (_rms(o, w) * _silu(z),)


def _f_merge(ga, gb, ya, yb):
    return (jax.nn.sigmoid(ga) * ya + jax.nn.sigmoid(gb) * yb,)


def _f_merge_keep(ga, gb, ya, yb):
    return (*_f_merge(ga, gb, ya, yb), yb)


def _f_loss(x2, ff, tgt, gate, shift, scale, w):
    y = _rms(x2 + gate * ff, w) * (1.0 + scale) + shift
    return (0.5 * jnp.mean(jnp.square(y - tgt), axis=-1, keepdims=True),)


def _shift_down(x, s):
    if s == 0:
        return x
    row = lax.broadcasted_iota(jnp.int32, x.shape, 0)
    return jnp.where(row >= s, pltpu.roll(x, s, 0), 0.0)


def _shift_up(x, s):
    if s == 0:
        return x
    n = x.shape[0]
    row = lax.broadcasted_iota(jnp.int32, x.shape, 0)
    return jnp.where(row < n - s, pltpu.roll(x, n - s, 0), 0.0)


def _conv(x, w):
    width = w.shape[0]
    acc = w[width - 1:width, :] * x
    for j in range(width - 1):
        acc = acc + w[j:j + 1, :] * _shift_down(x, width - 1 - j)
    return acc


def _conv_bwd(dy, x, w, dw_ref, first):
    width = w.shape[0]
    dx = w[width - 1:width, :] * dy
    for j in range(width - 1):
        dx = dx + w[j:j + 1, :] * _shift_up(dy, width - 1 - j)
    for j in range(width):
        row = jnp.sum(dy * _shift_down(x, width - 1 - j), axis=0, keepdims=True)
        _accumulate(dw_ref.at[j:j + 1, :], row, first)
    return dx


def _qkv_act(xc, is_v, scale):
    a = _silu(xc)
    nrm = a * lax.rsqrt(jnp.sum(a * a, axis=-1, keepdims=True) + EPS) * scale
    return jnp.where(is_v, a, nrm)


def _qkv_consts(j, heads):
    is_v = j >= 2 * heads
    scale = jnp.where(j < heads, HEAD ** -0.5, 1.0).astype(F32)
    return is_v, scale


def _qkv_fwd(p, w, heads, name):
    bl, s, w3 = p.shape

    def body(p_ref, w_ref, o_ref):
        is_v, scale = _qkv_consts(pl.program_id(0), heads)
        o_ref[...] = _qkv_act(_conv(p_ref[...], w_ref[...]), is_v, scale)

    blk = pl.BlockSpec((None, s, HEAD), lambda j, b: (b, 0, j))
    return pl.pallas_call(
        body, name=name, grid=(w3 // HEAD, bl), in_specs=[blk, pl.BlockSpec((w.shape[0], HEAD), lambda j, b: (0, j))],
        out_specs=blk, out_shape=jax.ShapeDtypeStruct(p.shape, F32), compiler_params=_cparams("parallel", "parallel"),
    )(p, w)


def _qkv_bwd(p, w, dout, heads, name):
    bl, s, w3 = p.shape

    def body(p_ref, w_ref, d_ref, dp_ref, dw_ref):
        is_v, scale = _qkv_consts(pl.program_id(0), heads)
        x, wv = p_ref[...], w_ref[...]
        _, vjp = jax.vjp(lambda xc: _qkv_act(xc, is_v, scale), _conv(x, wv))
        (dxc,) = vjp(d_ref[...])
        dp_ref[...] = _conv_bwd(dxc, x, wv, dw_ref, pl.program_id(1) == 0).astype(dp_ref.dtype)

    blk = pl.BlockSpec((None, s, HEAD), lambda j, b: (b, 0, j))
    wblk = pl.BlockSpec((w.shape[0], HEAD), lambda j, b: (0, j))
    return pl.pallas_call(
        body, name=name, grid=(w3 // HEAD, bl), in_specs=[blk, wblk, blk], out_specs=[blk, wblk],
        out_shape=[jax.ShapeDtypeStruct(p.shape, MXU_DTYPE), jax.ShapeDtypeStruct(w.shape, F32)],
        compiler_params=_cparams("arbitrary", "arbitrary"),
    )(p, w, dout)


def _sc_specs(p, w):
    bl, s, w3 = p.shape
    nblk = w3 // 3 // LANE
    sec = lambda k: pl.BlockSpec((None, s, LANE), functools.partial(lambda j, b, k: (b, 0, k * nblk + j), k=k))
    return nblk, [sec(0), sec(1), sec(2)], pl.BlockSpec((w.shape[0], LANE), lambda j, b: (0, j)), \
        pl.BlockSpec((None, s, LANE), lambda j, b: (b, 0, j))


def _sc_fwd(p, w, name):
    bl, s, w3 = p.shape
    nblk, secs, wblk, oblk = _sc_specs(p, w)

    def body(b_ref, c_ref, x_ref, w_ref, o_ref):
        o_ref[...] = (b_ref[...] * _conv(c_ref[...] * x_ref[...], w_ref[...])).astype(o_ref.dtype)

    return pl.pallas_call(
        body, name=name, grid=(nblk, bl), in_specs=secs + [wblk], out_specs=oblk,
        out_shape=jax.ShapeDtypeStruct((bl, s, w3 // 3), MXU_DTYPE), compiler_params=_cparams("parallel", "parallel"),
    )(p, p, p, w)


def _sc_bwd(p, w, dout, name):
    bl, s, w3 = p.shape
    nblk, secs, wblk, oblk = _sc_specs(p, w)

    def body(b_ref, c_ref, x_ref, w_ref, d_ref, dp_ref, dw_ref):
        gb, gc, xin, wv, d = b_ref[...], c_ref[...], x_ref[...], w_ref[...], d_ref[...]
        u = gc * xin
        dp_ref[0] = (d * _conv(u, wv)).astype(dp_ref.dtype)
        du = _conv_bwd(d * gb, u, wv, dw_ref, pl.program_id(1) == 0)
        dp_ref[1] = (du * xin).astype(dp_ref.dtype)
        dp_ref[2] = (du * gc).astype(dp_ref.dtype)

    return pl.pallas_call(
        body, name=name, grid=(nblk, bl), in_specs=secs + [wblk, oblk],
        out_specs=[pl.BlockSpec((3, None, s, LANE), lambda j, b: (0, b, 0, j)), wblk],
        out_shape=[jax.ShapeDtypeStruct((3, bl, s, w3 // 3), MXU_DTYPE), jax.ShapeDtypeStruct(w.shape, F32)],
        compiler_params=_cparams("arbitrary", "arbitrary"),
    )(p, p, p, w, dout)


def _bdot(a, b, ca, cb):
    return lax.dot_general(a.astype(MXU_DTYPE), b.astype(MXU_DTYPE), (((ca,), (cb,)), ((), ())),
                           preferred_element_type=F32)


def _hdot(a, b):
    return lax.dot_general(a, b, (((1,), (0,)), ((), ())), precision=HIGHEST, preferred_element_type=F32)


def _lane_col(x, idx):
    lane = lax.broadcasted_iota(jnp.int32, x.shape, 1)
    return jnp.sum(jnp.where(lane == idx, x, 0.0), axis=1, keepdims=True)


def _chunk_masks():
    r = lax.broadcasted_iota(jnp.int32, (CHUNK, CHUNK), 0)
    c = lax.broadcasted_iota(jnp.int32, (CHUNK, CHUNK), 1)
    return r == c, r >= c, r > c


def _dot3(a, b):
    ah, bh = a.astype(MXU_DTYPE), b.astype(MXU_DTYPE)
    al, bl = (a - ah.astype(F32)).astype(MXU_DTYPE), (b - bh.astype(F32)).astype(MXU_DTYPE)
    dot = lambda x, y: lax.dot_general(x, y, (((1,), (0,)), ((), ())), preferred_element_type=F32)
    return dot(ah, bh) + (dot(ah, bl) + dot(al, bh))


def _tri_inv_steps(low, eye):
    x = -low
    p = jnp.where(eye, 1.0, 0.0) + x
    span = 2
    while span < CHUNK:
        x = _dot3(x, x)
        yield
        p = p + _dot3(p, x)
        yield
        span *= 2
    return p


def _round_robin(gens):
    out, live = [None] * len(gens), list(range(len(gens)))
    while live:
        still = []
        for i in live:
            try:
                next(gens[i])
                still.append(i)
            except StopIteration as stop:
                out[i] = stop.value
        live = still
    return out


def _gdn_pre(q, k, v, gc, beta, masks):
    eye, causal, strict = masks
    gc_row = jnp.sum(jnp.where(eye, gc, 0.0), axis=0, keepdims=True)
    decay = jnp.where(causal, jnp.exp(jnp.where(causal, gc - gc_row, 0.0)), 0.0)
    eg = jnp.exp(gc)
    gl = gc[CHUNK - 1:CHUNK, :]
    kb, vb = k * beta, v * beta
    both = _bdot(jnp.concatenate([kb, q], axis=0), k, 1, 1)
    low = jnp.where(strict, both[:CHUNK] * decay, 0.0)
    qk = jnp.where(causal, both[CHUNK:] * decay, 0.0)
    rest = jnp.exp(gl - gc)
    return dict(decay=decay, eg=eg, gl=gl, kb=kb, vb=vb, kbe=kb * eg, low=low, qk=qk, qg=q * eg, rest=rest, kdec=k * rest)


def _gdn_specs(qkv, gbeta, heads, rev):
    bl, s, w3 = qkv.shape
    d, n = w3 // 3, s // CHUNK
    at = (lambda c: n - 1 - c) if rev else (lambda c: c)
    assert d == heads * HEAD
    sec = pl.BlockSpec((None, CHUNK, w3), lambda b, c: (b, at(c), 0))
    gspec = pl.BlockSpec((None, CHUNK, LANE), lambda b, c: (b, at(c), 0))
    sspec = pl.BlockSpec((None, None, heads, HEAD, HEAD), lambda b, c: (b, at(c), 0, 0, 0))
    tspec = pl.BlockSpec((None, None, heads, CHUNK, CHUNK), lambda b, c: (b, at(c), 0, 0, 0))
    return bl, s, d, n, sec, gspec, sspec, tspec


def _gdn_fwd(qkv, gbeta, heads, name):
    bl, s, d, n, sec, gspec, sspec, tspec = _gdn_specs(qkv, gbeta, heads, False)

    def body(x_ref, g_ref, o_ref, s_ref, t_ref, st_ref):
        @pl.when(pl.program_id(1) == 0)
        def _():
            st_ref[...] = jnp.zeros_like(st_ref)

        masks = _chunk_masks()
        eye, causal, _ = masks
        gblk = g_ref[...]
        gc_all = _hdot(jnp.where(causal, 1.0, 0.0), gblk)
        st_all = st_ref[...]

        def head(h):
            st = st_all[h]
            q, k, v = (x_ref[:, sec * d + h * HEAD:sec * d + (h + 1) * HEAD] for sec in range(3))
            pre = _gdn_pre(q, k, v, _lane_col(gc_all, h), _lane_col(gblk, heads + h), masks)
            yield
            t = yield from _tri_inv_steps(pre["low"], eye)
            uw = _bdot(t, jnp.concatenate([pre["vb"], pre["kbe"]], axis=1), 1, 0)
            u, w = uw[:, :HEAD], uw[:, HEAD:]
            yield
            vnew = u - _bdot(w, st, 1, 0)
            yield
            out = _bdot(pre["qg"], st, 1, 0) + _bdot(pre["qk"], vnew, 1, 0)
            return out, t, st * jnp.exp(pre["gl"]) + _bdot(pre["kdec"], vnew, 0, 0)

        outs, ts, states = zip(*_round_robin([head(h) for h in range(heads)]))
        o_ref[...] = jnp.concatenate(outs, axis=1)
        s_ref[...] = st_all
        t_ref[...] = jnp.stack(ts)
        st_ref[...] = jnp.stack(states)

    return pl.pallas_call(
        body, name=name, grid=(bl, n), in_specs=[sec, gspec],
        out_specs=[pl.BlockSpec((None, CHUNK, d), lambda b, c: (b, c, 0)), sspec, tspec],
        out_shape=[jax.ShapeDtypeStruct((bl, s, d), F32), jax.ShapeDtypeStruct((bl, n, heads, HEAD, HEAD), F32),
                   jax.ShapeDtypeStruct((bl, n, heads, CHUNK, CHUNK), F32)],
        scratch_shapes=[pltpu.VMEM((heads, HEAD, HEAD), F32)], compiler_params=_cparams("parallel", "arbitrary"),
    )(qkv, gbeta)


def _gdn_bwd(qkv, gbeta, dout, s_all, t_all, heads, name):
    bl, s, d, n, sec, gspec, sspec, tspec = _gdn_specs(qkv, gbeta, heads, True)
    ospec = pl.BlockSpec((None, CHUNK, d), lambda b, c: (b, n - 1 - c, 0))

    def body(x_ref, g_ref, do_ref, s_ref, t_ref, dx_ref, dg_ref, ds_ref):
        @pl.when(pl.program_id(1) == 0)
        def _():
            ds_ref[...] = jnp.zeros_like(ds_ref)

        masks = _chunk_masks()
        eye, causal, strict = masks
        gblk = g_ref[...]
        gc_all = _hdot(jnp.where(causal, 1.0, 0.0), gblk)
        lane = lax.broadcasted_iota(jnp.int32, gblk.shape, 1)
        last_row = lax.broadcasted_iota(jnp.int32, (CHUNK, 1), 0) == CHUNK - 1
        rowsum = lambda a: jnp.sum(a, axis=1, keepdims=True)
        st_all, t_all_, ds_all = s_ref[...], t_ref[...], ds_ref[...]

        def head(h):
            sl = slice(h * HEAD, (h + 1) * HEAD)
            q, k, v = (x_ref[:, sec * d + h * HEAD:sec * d + (h + 1) * HEAD] for sec in range(3))
            do = do_ref[:, sl]
            beta = _lane_col(gblk, heads + h)
            st, t, dsn = st_all[h], t_all_[h], ds_all[h]
            pre = _gdn_pre(q, k, v, _lane_col(gc_all, h), beta, masks)
            decay, eg, kb, vb, kbe, low, qk, qg, kdec = (pre[x] for x in ("decay", "eg", "kb", "vb", "kbe", "low", "qk", "qg", "kdec"))
            egl = jnp.exp(pre["gl"])
            yield
            uw = _bdot(t, jnp.concatenate([vb, kbe], axis=1), 1, 0)
            u, w = uw[:, :HEAD], uw[:, HEAD:]
            yield
            vnew = u - _bdot(w, st, 1, 0)
            yield
            stack, side = functools.partial(jnp.concatenate, axis=0), functools.partial(jnp.concatenate, axis=1)
            dkdec = _bdot(vnew, dsn, 1, 1)
            dvnew = _bdot(kdec, dsn, 1, 0) + _bdot(qk, do, 0, 0)
            dgl = jnp.sum(dsn * st, keepdims=True) * egl
            dqk = jnp.where(causal, _bdot(do, vnew, 1, 1), 0.0)
            yield
            by_state = _bdot(stack([do, dvnew]), st, 1, 1)
            dqg, dw = by_state[:CHUNK], -by_state[CHUNK:]
            ds_new = dsn * egl + _bdot(stack([qg, -w]), stack([do, dvnew]), 0, 0)
            yield
            dt = _bdot(side([dvnew, dw]), side([vb, kbe]), 1, 1)
            by_t = _bdot(t, side([dvnew, dw]), 0, 0)
            dvb, dkbe = by_t[:, :HEAD], by_t[:, HEAD:]
            yield
            inner = _bdot(dt, t, 1, 1)
            yield
            dlow = -jnp.where(strict, _bdot(t, inner, 0, 0), 0.0)
            da, db = dlow * decay, dqk * decay
            yield
            m = dlow * low + dqk * qk
            kdk = dkdec * kdec
            col_of_m = jnp.sum(jnp.where(eye, jnp.sum(m, axis=0, keepdims=True), 0.0), axis=1, keepdims=True)
            dgc = rowsum(m) - col_of_m + rowsum(dqg * qg) + rowsum(dkbe * kbe) - rowsum(kdk)
            dgc = dgc + jnp.where(last_row, dgl + jnp.sum(kdk, keepdims=True), 0.0)
            by_k = _bdot(stack([da, db]), k, 1, 0)
            dkb = by_k[:CHUNK] + dkbe * eg
            yield
            dk = _bdot(stack([da, db]), stack([kb, q]), 0, 0) + dkdec * pre["rest"] + dkb * beta
            dq = by_k[CHUNK:] + dqg * eg
            dbeta = rowsum(dkb * k) + rowsum(dvb * v)
            return dq, dk, dvb * beta, jnp.where(lane == h, dgc, 0.0) + jnp.where(lane == heads + h, dbeta, 0.0), ds_new

        dqs, dks, dvs, dgs, dss = zip(*_round_robin([head(h) for h in range(heads)]))
        dx_ref[...] = jnp.concatenate(dqs + dks + dvs, axis=1)
        ds_ref[...] = jnp.stack(dss)
        dgb = dgs[0]
        for extra in dgs[1:]:
            dgb = dgb + extra
        upper = jnp.where(jnp.logical_or(eye, jnp.logical_not(causal)), 1.0, 0.0)
        dg_ref[...] = jnp.where(lane < heads, _hdot(upper, dgb), dgb)

    return pl.pallas_call(
        body, name=name, grid=(bl, n), in_specs=[sec, gspec, ospec, sspec, tspec], out_specs=[sec, gspec],
        out_shape=[jax.ShapeDtypeStruct(qkv.shape, F32), jax.ShapeDtypeStruct((bl, s, LANE), F32)],
        scratch_shapes=[pltpu.VMEM((heads, HEAD, HEAD), F32)], compiler_params=_cparams("parallel", "arbitrary"),
    )(qkv, gbeta, dout, s_all, t_all)


def _position():
    return lax.axis_index("x"), lax.axis_index("y"), lax.axis_index("c")


def _all_gather(x, *, name, hbm):
    space = pltpu.HBM if hbm else pltpu.VMEM

    def body(x_ref, out_ref, send_sems, recv_sems, local_sem):
        ax, ay, ac = _position()
        me, sibling = (ax, ay, ac), (ax, ay, 1 - ac)
        chips = [(1 - ax, ay), (ax, 1 - ay), (1 - ax, 1 - ay)]

        def slot(px, py, pc):
            return out_ref.at[4 * px + 2 * py + pc]

        def copy(k, block, to, src=None):
            return pltpu.make_async_remote_copy(
                src_ref=slot(*block) if src is None else src, dst_ref=slot(*block), send_sem=send_sems.at[k],
                recv_sem=recv_sems.at[k], device_id=to, device_id_type=MESH_IDS)

        mine = pltpu.make_async_copy(x_ref, slot(*me), local_sem)
        mine.start()
        first = [copy(0, me, sibling, src=x_ref)] + [copy(1 + j, me, (*chip, ac), src=x_ref) for j, chip in enumerate(chips)]
        for cp in first:
            cp.start()
        passed = [copy(4 + j, (*chip, ac), sibling) for j, chip in enumerate(chips)]
        for j, chip in enumerate(chips):
            copy(1 + j, (*chip, ac), me).wait_recv()
            passed[j].start()
        copy(0, sibling, me).wait_recv()
        for j, chip in enumerate(chips):
            copy(4 + j, (*chip, 1 - ac), me).wait_recv()
        for cp in first + passed:
            cp.wait_send()
        mine.wait()

    return pl.pallas_call(
        body, name=name, out_shape=jax.ShapeDtypeStruct((NDEV,) + x.shape, x.dtype),
        in_specs=[pl.BlockSpec(memory_space=space)], out_specs=pl.BlockSpec(memory_space=space),
        scratch_shapes=[pltpu.SemaphoreType.DMA((7,)), pltpu.SemaphoreType.DMA((7,)), pltpu.SemaphoreType.DMA],
    )(x)


class _Rider:
    def __init__(self, arrays, out_shapes, sems, hooks):
        self.arrays, self.out_shapes, self.sems, self.hooks = arrays, out_shapes, sems, hooks


def _gather_rider(xs):
    n = len(xs)

    def hooks(x_refs, out_refs, send_sems, recv_sems):
        ax, ay, ac = _position()
        me, sibling = (ax, ay, ac), (ax, ay, 1 - ac)
        chips = [(1 - ax, ay), (ax, 1 - ay), (1 - ax, 1 - ay)]

        def copies(k, block, to, own=False):
            out = []
            for i in range(n):
                slot = out_refs[i].at[4 * block[0] + 2 * block[1] + block[2]]
                out.append(pltpu.make_async_remote_copy(
                    src_ref=x_refs[i] if own else slot, dst_ref=slot, send_sem=send_sems.at[k, i], recv_sem=recv_sems.at[k, i],
                    device_id=to, device_id_type=MESH_IDS))
            return out

        def first():
            for cp in copies(0, me, sibling, own=True):
                cp.start()
            for j, chip in enumerate(chips):
                for cp in copies(1 + j, me, (*chip, ac), own=True):
                    cp.start()

        def mid():
            for j, chip in enumerate(chips):
                for arrived, onward in zip(copies(1 + j, (*chip, ac), me), copies(4 + j, (*chip, ac), sibling)):
                    arrived.wait_recv()
                    onward.start()

        def last():
            for cp in copies(0, sibling, me):
                cp.wait_recv()
            for j, chip in enumerate(chips):
                for cp in copies(4 + j, (*chip, 1 - ac), me):
                    cp.wait_recv()
            for cp in copies(0, me, sibling, own=True):
                cp.wait_send()
            for j, chip in enumerate(chips):
                for cp in copies(1 + j, me, (*chip, ac), own=True) + copies(4 + j, (*chip, ac), sibling):
                    cp.wait_send()

        return first, mid, last

    return _Rider(list(xs), [jax.ShapeDtypeStruct((NDEV,) + x.shape, x.dtype) for x in xs],
                  [pltpu.SemaphoreType.DMA((7, n)), pltpu.SemaphoreType.DMA((7, n))], hooks)


def _scatter_rider(parts):
    packed = sum(r for _, r in parts)
    width, dtype = parts[0][0].shape[1], parts[0][0].dtype

    def hooks(g_refs, out_refs, send_sems, recv_sems):
        (recv_ref,) = out_refs
        ax, ay, ac = _position()

        def peer(rel):
            flip = lambda a, bit: 1 - a if rel & bit else a
            return flip(ax, 4), flip(ay, 2), flip(ac, 1)

        def first():
            for rel in range(1, NDEV):
                px, py, pc = peer(rel)
                off = 0
                for g_ref, (_, r) in zip(g_refs, parts):
                    rows = g_ref.at[pl.ds(pl.multiple_of((4 * px + 2 * py + pc) * r, ROW_ALIGN), r)]
                    pltpu.make_async_remote_copy(
                        src_ref=rows, dst_ref=recv_ref.at[rel - 1, pl.ds(off, r)], send_sem=send_sems.at[rel - 1],
                        recv_sem=recv_sems.at[rel - 1], device_id=(px, py, pc), device_id_type=MESH_IDS).start()
                    off += r

        def last():
            for rel in range(1, NDEV):
                slot = recv_ref.at[rel - 1]
                pltpu.make_async_remote_copy(src_ref=slot, dst_ref=slot, send_sem=send_sems.at[rel - 1],
                                             recv_sem=recv_sems.at[rel - 1], device_id=peer(rel), device_id_type=MESH_IDS).wait()

        return first, lambda: None, last

    return _Rider([g for g, _ in parts], [jax.ShapeDtypeStruct((NDEV - 1, packed, width), dtype)],
                  [pltpu.SemaphoreType.DMA((NDEV - 1,)), pltpu.SemaphoreType.DMA((NDEV - 1,))], hooks)


def _sum_direct(own, recv, name):
    r, w = own.shape
    tr = max(t for t in range(ROW_ALIGN, 257, ROW_ALIGN) if r % t == 0)

    def body(own_ref, *refs):
        acc = own_ref[...].astype(F32)
        for ref in refs[:-1]:
            acc = acc + ref[...].astype(F32)
        refs[-1][...] = acc

    rblk = lambda k: pl.BlockSpec((None, tr, w), functools.partial(lambda i, k: (k, i, 0), k=k))
    blk = pl.BlockSpec((tr, w), lambda i: (i, 0))
    return pl.pallas_call(body, name=name, grid=(r // tr,), in_specs=[blk] + [rblk(k) for k in range(NDEV - 1)],
                          out_specs=blk, out_shape=jax.ShapeDtypeStruct((r, w), F32),
                          compiler_params=_cparams("parallel"))(own, *([recv] * (NDEV - 1)))


ROW_ALIGN = 16


def _window_start(rows_per_dev, k):
    return rows_per_dev * k // ROW_ALIGN * ROW_ALIGN


def _exchange_in_chip(parts, name, collective_id):
    packed = sum(win for _, _, win, _ in parts)
    width, dtype = parts[0][0].shape[1], parts[0][0].dtype

    def body(g_refs, out_refs, send_sems, recv_sems):
        (recv_ref,) = out_refs
        ax, ay, ac = _position()
        sibling = (ax, ay, 1 - ac)
        _handshake([sibling])
        for q in range(4):
            for g_ref, (_, r, win, off) in zip(g_refs, parts):
                there = g_ref.at[pl.ds(pl.multiple_of(_window_start(r, 2 * q + 1 - ac), ROW_ALIGN), win)]
                pltpu.make_async_remote_copy(src_ref=there, dst_ref=recv_ref.at[q, pl.ds(off, win)], send_sem=send_sems.at[q],
                                             recv_sem=recv_sems.at[q], device_id=sibling, device_id_type=MESH_IDS).start()
        for q in range(4):
            pltpu.make_async_remote_copy(src_ref=recv_ref.at[q], dst_ref=recv_ref.at[q], send_sem=send_sems.at[q],
                                         recv_sem=recv_sems.at[q], device_id=sibling, device_id_type=MESH_IDS).wait()

    return _on_sequencer(body, [g for g, _, _, _ in parts], [jax.ShapeDtypeStruct((4, packed, width), dtype)],
                         [pltpu.SemaphoreType.DMA((4,)), pltpu.SemaphoreType.DMA((4,))], name=name, collective_id=collective_id)[0]


def _on_sequencer(body, ins, out_shapes, sems, *, name, collective_id):
    hbm = pltpu.MemorySpace.HBM
    in_refs = [jax.new_ref(a, memory_space=hbm) for a in ins]
    out_refs = [jax.empty_ref(s, memory_space=hbm) for s in out_shapes]

    @pl.kernel(mesh=plsc.ScalarSubcoreMesh(axis_name="sequencer", num_cores=1), name=name, scratch_types=tuple(sems),
               compiler_params=pltpu.CompilerParams(collective_id=collective_id))
    def launch(*sem_refs):
        body(in_refs, out_refs, *sem_refs)

    launch()
    return [r[...] for r in out_refs]


def _handshake(peers):
    barrier = pltpu.get_barrier_semaphore()
    for peer in peers:
        pl.semaphore_signal(barrier, inc=1, device_id=peer, device_id_type=MESH_IDS)
    pl.semaphore_wait(barrier, len(peers))


def _exchange_chips_async(s1, name, collective_id):
    def body(in_refs, out_refs, send_sems, recv_sems):
        (src,), (got,) = in_refs, out_refs
        ax, ay, ac = _position()
        chips = [(1 - ax, ay), (ax, 1 - ay), (1 - ax, 1 - ay)]
        _handshake([(cx, cy, ac) for cx, cy in chips])
        copies = [pltpu.make_async_remote_copy(
            src_ref=src.at[2 * cx + cy], dst_ref=got.at[r], send_sem=send_sems.at[r], recv_sem=recv_sems.at[r],
            device_id=(cx, cy, ac), device_id_type=MESH_IDS) for r, (cx, cy) in enumerate(chips)]
        for cp in copies:
            cp.start()
        for cp in copies:
            cp.wait_recv()
        for cp in copies:
            cp.wait_send()

    return _on_sequencer(body, [s1], [jax.ShapeDtypeStruct((3,) + s1.shape[1:], s1.dtype)],
                         [pltpu.SemaphoreType.DMA((3,)), pltpu.SemaphoreType.DMA((3,))], name=name, collective_id=collective_id)[0]


def _gather_async(xs, name, collective_id):
    rider = _gather_rider(xs)

    def body(in_refs, out_refs, send_sems, recv_sems):
        ax, ay, ac = _position()
        _handshake([(ax, ay, 1 - ac), (1 - ax, ay, ac), (ax, 1 - ay, ac), (1 - ax, 1 - ay, ac)])
        for hook in rider.hooks(in_refs, out_refs, send_sems, recv_sems):
            hook()

    return _on_sequencer(body, rider.arrays, rider.out_shapes, rider.sems, name=name, collective_id=collective_id)


def _scatter_async(parts, name, collective_id):
    rider = _scatter_rider(parts)

    def body(in_refs, out_refs, send_sems, recv_sems):
        ax, ay, ac = _position()
        flip = lambda a, on: 1 - a if on else a
        _handshake([(flip(ax, rel & 4), flip(ay, rel & 2), flip(ac, rel & 1)) for rel in range(1, NDEV)])
        for hook in rider.hooks(in_refs, out_refs, send_sems, recv_sems):
            hook()

    return _on_sequencer(body, rider.arrays, rider.out_shapes, rider.sems, name=name, collective_id=collective_id)[0]


def _sum_in_chip(own, recv, name):
    _, r, w = own.shape
    tr = _tile(r, (256, 128))

    def body(a_ref, b_ref, o_ref):
        o_ref[...] = (a_ref[...].astype(F32) + b_ref[...].astype(F32)).astype(o_ref.dtype)

    blk = pl.BlockSpec((None, tr, w), lambda q, i: (q, i, 0))
    return pl.pallas_call(body, name=name, grid=(4, r // tr), in_specs=[blk, blk], out_specs=blk,
                          out_shape=jax.ShapeDtypeStruct(own.shape, own.dtype),
                          compiler_params=_cparams("parallel", "parallel"))(own, recv)


def _sum_chips(s1, recv, chip, name):
    _, r, w = s1.shape
    tr = _tile(r, (256, 128))

    def body(c_ref, s_ref, r0_ref, r1_ref, r2_ref, o_ref):
        f = lambda ref: ref[...].astype(F32)
        o_ref[...] = ((f(s_ref) + f(r0_ref)) + f(r1_ref)) + f(r2_ref)

    rblk = lambda k: pl.BlockSpec((None, tr, w), functools.partial(lambda i, c, k: (k, i, 0), k=k))
    grid_spec = pltpu.PrefetchScalarGridSpec(
        num_scalar_prefetch=1, grid=(r // tr,),
        in_specs=[pl.BlockSpec((None, tr, w), lambda i, c: (c[0], i, 0)), rblk(0), rblk(1), rblk(2)],
        out_specs=pl.BlockSpec((tr, w), lambda i, c: (i, 0)))
    return pl.pallas_call(body, name=name, grid_spec=grid_spec, out_shape=jax.ShapeDtypeStruct((r, w), F32),
                          compiler_params=_cparams("parallel"))(chip, s1, recv, recv, recv)


def _silu_rows(x, name):
    def body(x_ref, o_ref):
        o_ref[...] = _silu(x_ref[...])

    return pl.pallas_call(body, name=name, out_shape=jax.ShapeDtypeStruct(x.shape, F32))(x)


def _row_sum(x, name):
    def body(x_ref, o_ref):
        acc = x_ref[0:1, :]
        for i in range(1, x.shape[0]):
            acc = acc + x_ref[i:i + 1, :]
        o_ref[...] = acc

    return pl.pallas_call(body, name=name, out_shape=jax.ShapeDtypeStruct((1, x.shape[1]), F32))(x)


def _adamw(w, g, m, v, name):
    cols = w.shape[-1]
    rows = w.size // cols
    tr = _tile(rows, (128,))
    tc = LANE if (tr == rows and rows > 512 and cols % LANE == 0) else cols

    def body(w_ref, g_ref, m_ref, v_ref, d_ref, mo_ref, vo_ref):
        grad = g_ref[...]
        m_new = ADAM_B1 * m_ref[...] + (1.0 - ADAM_B1) * grad
        v_new = ADAM_B2 * v_ref[...] + (1.0 - ADAM_B2) * jnp.square(grad)
        m_hat = m_new / (1.0 - ADAM_B1 ** ADAM_STEP)
        v_hat = v_new / (1.0 - ADAM_B2 ** ADAM_STEP)
        d_ref[...] = -ADAM_LR * (m_hat / (jnp.sqrt(v_hat) + ADAM_EPS) + ADAM_WD * w_ref[...])
        mo_ref[...] = m_new
        vo_ref[...] = v_new

    blk = pl.BlockSpec((tr, tc), lambda i, j: (i, j))
    out = pl.pallas_call(
        body, name=name, grid=(rows // tr, cols // tc), in_specs=[blk] * 4, out_specs=[blk] * 3,
        out_shape=[jax.ShapeDtypeStruct((rows, cols), F32)] * 3, compiler_params=_cparams("parallel", "parallel"),
    )(*[t.reshape(rows, cols) for t in (w, g, m, v)])
    return [t.reshape(w.shape) for t in out]


def _pack(parts, width, row_mult, dtype):
    flat = jnp.concatenate([p.reshape(-1).astype(dtype) for p in parts])
    rows = -(-flat.shape[0] // (width * row_mult)) * row_mult
    return jnp.pad(flat, (0, rows * width - flat.shape[0])).reshape(rows, width)


def _unpack(flat, shapes):
    out, off = [], 0
    for shp in shapes:
        size = 1
        for dim in shp:
            size *= dim
        out.append(flat[:, off:off + size].reshape((flat.shape[0],) + tuple(shp)))
        off += size
    return out


def _devices_to_cols(a):
    _, r, c = a.shape
    return a.transpose(1, 0, 2).reshape(r, NDEV * c)


def kernel(x, c, w_ada, b_ada, norm1_w, w_in, gdn_conv_w, gdn_a_log, gdn_dt_bias, gdn_norm_w, w_gdn_proj, sc_conv_w, w_sc_out, w_o, norm2_w, w_ffn_in, w_ffn_out, w_ada_f, b_ada_f, normf_w, loss_target, m_w_ada, m_b_ada, m_norm1_w, m_w_in, m_gdn_conv_w, m_gdn_a_log, m_gdn_dt_bias, m_gdn_norm_w, m_w_gdn_proj, m_sc_conv_w, m_w_sc_out, m_w_o, m_norm2_w, m_w_ffn_in, m_w_ffn_out, m_w_ada_f, m_b_ada_f, m_normf_w, v_w_ada, v_b_ada, v_norm1_w, v_w_in, v_gdn_conv_w, v_gdn_a_log, v_gdn_dt_bias, v_gdn_norm_w, v_w_gdn_proj, v_sc_conv_w, v_w_sc_out, v_w_o, v_norm2_w, v_w_ffn_in, v_w_ffn_out, v_w_ada_f, v_b_ada_f, v_normf_w):
    bl, s, d = x.shape
    heads = gdn_a_log.shape[-1]
    dff = w_ffn_out.shape[1] * NDEV
    tok = bl * s
    ax, ay, ac = _position()
    dev = 4 * ax + 2 * ay + ac
    as_tok = lambda a: a.reshape(bl, s, a.shape[-1])
    as_mat = lambda a: a.reshape(tok, a.shape[-1])

    small = _all_gather(_pack([c, gdn_conv_w, sc_conv_w], LANE, 8, F32), name="gather_cond", hbm=False)
    c_all, conv_w, sc_w = _unpack(small.reshape(NDEV, -1), [(bl, d), gdn_conv_w.shape[1:], sc_conv_w.shape[1:]])
    c_act = _silu_rows(c_all.reshape(NDEV * bl, d), "cond_silu")
    conv_w, sc_w = _devices_to_cols(conv_w), _devices_to_cols(sc_w)
    n_ada, n_adaf = w_ada.shape[-1], w_ada_f.shape[-1]
    bias = jnp.broadcast_to(lax.dynamic_slice_in_dim(b_ada, dev * n_ada, n_ada, axis=1), (NDEV * bl, n_ada))
    biasf = jnp.broadcast_to(lax.dynamic_slice_in_dim(b_ada_f.reshape(1, -1), dev * n_adaf, n_adaf, axis=1), (NDEV * bl, n_adaf))
    mod_cols = _mm(c_act, w_ada[0], add=bias, name="ada_cols")
    modf_cols = _mm(c_act, w_ada_f, add=biasf, name="adaf_cols")
    mods = _all_gather(jnp.concatenate([mod_cols, modf_cols], axis=1), name="gather_mod", hbm=False)
    mod_all = mods[:, :, :n_ada].transpose(1, 0, 2).reshape(NDEV * bl, NDEV * n_ada)
    modf_all = mods[:, :, n_ada:].transpose(1, 0, 2).reshape(NDEV * bl, NDEV * n_adaf)
    my_rows = lambda a: lax.dynamic_slice_in_dim(a, dev * bl, bl, axis=0)
    sh1, sc1, g1, sh2, sc2, g2 = [t.reshape(bl, 1, d) for t in jnp.split(my_rows(mod_all), 6, axis=1)]
    shf, scf = [t.reshape(bl, 1, d) for t in jnp.split(my_rows(modf_all), 2, axis=1)]

    late = [t.astype(MXU_DTYPE) for t in (w_gdn_proj[0], w_sc_out[0], w_o[0], w_ffn_in[0].T, w_ffn_out[0])]
    rows = [t.shape[0] for t in late] + [w_in.shape[-1]]
    offs = [sum(rows[:i]) for i in range(5)]
    in_send = w_in[0].T.astype(MXU_DTYPE)
    with_own = lambda g, own: lax.dynamic_update_slice_in_dim(g, own[None], dev, axis=0)
    (wt_in,) = _gather_async([in_send], "gather_w_in", 1)
    wt_in = with_own(wt_in, in_send).reshape(NDEV * rows[5], d)
    gathered = _gather_async(late[:3], "gather_mixer", 2) + _gather_async(late[3:], "gather_ffn", 3)
    wgp, wso, wo, wt_fi, wfo = [with_own(g, own).reshape(NDEV * own.shape[0], d) for g, own in zip(gathered, late)]
    o_z, o_ab, o_sc, o_ga, o_gb = 3 * d, 4 * d, 4 * d + 2 * heads, 7 * d + 2 * heads, 8 * d + 2 * heads
    s_qkv, s_z, s_sc, s_gate = (0, o_z), (o_z, d), (o_sc, 3 * d), (o_ga, 2 * d)
    wt_ab = jnp.pad(wt_in[o_ab:o_sc], ((0, LANE - 2 * heads), (0, 0)))

    n1w, n2w, nfw = norm1_w.reshape(1, d), norm2_w.reshape(1, d), normf_w.reshape(1, d)
    lanes = lambda a: jnp.pad(a.reshape(1, -1), ((0, 0), (0, LANE - a.size)))
    a_log, dt_bias, gnw = lanes(gdn_a_log), lanes(gdn_dt_bias), gdn_norm_w.reshape(1, HEAD)
    f_gates = functools.partial(_f_gates, heads=heads)
    (h1,) = _tok_fwd(_f_norm_mod, [x], [sh1, sc1], [n1w], [(d, MXU_DTYPE)], name="norm1", ts=512)
    h1m = as_mat(h1)
    p_qkv = as_tok(_mm(h1m, wt_in, tb=True, b_rows=s_qkv, name="in_qkv"))
    p_z = as_tok(_mm(h1m, wt_in, tb=True, b_rows=s_z, name="in_z"))
    p_ab = as_tok(_mm(h1m, wt_ab, tb=True, name="in_ab"))
    p_sc = as_tok(_mm(h1m, wt_in, tb=True, b_rows=s_sc, name="in_sc"))
    p_g = as_tok(_mm(h1m, wt_in, tb=True, b_rows=s_gate, name="in_gate"))
    qkv = _qkv_fwd(p_qkv, conv_w, heads, "qkv_conv")
    (gbeta,) = _tok_fwd(f_gates, [p_ab], [], [a_log, dt_bias], [(LANE, F32)], name="gates", ts=512)
    o, s_all, t_all = _gdn_fwd(qkv, gbeta, heads, "gdn")
    (og,) = _tok_fwd(_f_gdn_out, [o, p_z], [], [(gnw, None)], [(d, MXU_DTYPE)], name="gdn_out", ts=2048, wb=HEAD, cols=heads)
    y_a = as_tok(_mm(as_mat(og), wgp, name="gdn_proj"))
    scp = _sc_fwd(p_sc, sc_w, "sc_conv")
    mrg, y_b = _tok_fwd(_f_merge_keep, [(p_g, 0), (p_g, 1), y_a, _Product(scp, wso)], [], [], [(d, MXU_DTYPE), (d, F32)],
                        name="merge", ts=256, wb=d)
    merge_toks = [(p_g, 0), (p_g, 1), y_a, y_b]
    x2, h2, mix = _tok_fwd(_f_res_norm_mod_keep, [x, _Product(mrg, wo)], [g1, sh2, sc2], [n2w],
                           [(d, F32), (d, MXU_DTYPE), (d, F32)], name="norm2", ts=512)
    act, gu_a, gu_b = _ffn_in_swiglu(as_mat(h2), wt_fi, dff, "ffn_in")

    loss_l, (dx2, dff_out, _), (dg2, dshf, dscf), (dnfw,) = _tok_bwd(
        _f_loss, [x2, _Product(as_tok(act), wfo), loss_target], [g2, shf, scf], [nfw], [], [True, True, False], name="loss",
        ts=256, loss=True, tok_dtype=[F32, MXU_DTYPE, None])
    dffm = as_mat(dff_out)
    dgu_a, dgu_b = _ffn_out_bwd_swiglu(dffm, wfo, gu_a, gu_b, "d_ffn_out")
    gmm = functools.partial(_mm, ta=True, out_dtype=MXU_DTYPE)
    gw_ffn_out = gmm(act, dffm, name="g_ffn_out")
    dh2 = _Product(as_tok(dgu_b), wt_fi, b_rows=(dff, dff), add=as_tok(_mm(dgu_a, wt_fi, b_rows=(0, dff), name="d_ffn_in_a")))
    h2m = as_mat(h2)
    gwt_ffn_in = gmm(dgu_a, h2m, out_rows=2 * dff, name="g_ffn_in_a")
    gwt_ffn_in = gmm(dgu_b, h2m, out_rows=2 * dff, row_off=dff, into=gwt_ffn_in, name="g_ffn_in_b")
    ffn_parts = [(gwt_ffn_in, rows[3]), (gw_ffn_out, rows[4])]
    ffn_recv = _scatter_async(ffn_parts, "scatter_ffn", 4)
    (dx_skip, dmix), (dg1, dsh2, dsc2), (dn2w,) = _tok_bwd(
        _f_res_norm_mod, [x, mix], [g1, sh2, sc2], [n2w], [dx2, dh2], [True, True], name="d_norm2", ts=256,
        tok_dtype=[F32, MXU_DTYPE], after=[gwt_ffn_in, gw_ffn_out])
    gw_o = gmm(as_mat(mrg), as_mat(dmix), name="g_mix_out")
    (dga, dgb, dya, dyb), _, _ = _tok_bwd(_f_merge, merge_toks, [], [], [_Product(dmix, wo, tb=True)], [True] * 4,
                                          name="d_merge", ts=256, wb=d, tok_dtype=MXU_DTYPE)
    dyam, dybm = as_mat(dya), as_mat(dyb)
    dog = as_tok(_mm(dyam, wgp, tb=True, name="d_gdn_proj"))
    gw_gdn_proj = gmm(as_mat(og), dyam, name="g_gdn_proj")
    dscp = as_tok(_mm(dybm, wso, tb=True, name="d_sc_out"))
    gw_sc_out = gmm(as_mat(scp), dybm, name="g_sc_out")
    dsc, g_sc_w = _sc_bwd(p_sc, sc_w, dscp, "d_sc_conv")
    mix_parts = [(gw_gdn_proj, rows[0]), (gw_sc_out, rows[1]), (gw_o, rows[2])]
    mix_recv = _scatter_async(mix_parts, "scatter_mixer", 5)
    (do, dz), _, (g_gnw,) = _tok_bwd(_f_gdn_out, [o, p_z], [], [(gnw, None)], [dog], [True, True], name="d_gdn_out",
                                     ts=2048, wb=HEAD, cols=heads, tok_dtype=[F32, MXU_DTYPE],
                                     after=[gw_gdn_proj, gw_sc_out, gw_o])
    own_rows = lambda parts: jnp.concatenate([lax.dynamic_slice_in_dim(g, dev * r, r, axis=0) for g, r in parts], axis=0)
    dqkv, dgbeta = _gdn_bwd(qkv, gbeta, do, s_all, t_all, heads, "d_gdn")
    dp_qkv, g_conv_w = _qkv_bwd(p_qkv, conv_w, dqkv, heads, "d_qkv_conv")
    ffn_red = _sum_direct(own_rows(ffn_parts), ffn_recv, "sum_ffn")
    mix_red = _sum_direct(own_rows(mix_parts), mix_recv, "sum_mix")
    (dp_ab,), _, (g_a_log, g_dt_bias) = _tok_bwd(f_gates, [p_ab], [], [a_log, dt_bias], [dgbeta], [True], name="d_gates",
                                                 ts=512, tok_dtype=MXU_DTYPE, after=[ffn_red, mix_red])
    r_in = rows[5]
    win = -(-(r_in + max(r_in * k % ROW_ALIGN for k in range(NDEV))) // 128) * 128
    need_rows = max(_window_start(r_in, k) for k in range(NDEV)) + win
    assert d <= 1024
    wide = [as_mat(dp_qkv), as_mat(dz), dsc.reshape(3, tok, d), as_mat(dga), as_mat(dgb)]
    row_of = lambda t: d * t + jnp.where(t * d >= o_ab, 2 * heads, 0)
    gwt_in = _gmm_chain(wide, h1m, need_rows, row_of, name="g_in", tm=d, tk=min(1024, tok))
    gwt_in = gmm(as_mat(dp_ab)[:, :2 * heads], h1m, out_rows=need_rows, row_off=o_ab, into=gwt_in, name="g_in_ab")
    recv1 = _exchange_in_chip([(gwt_in, r_in, win, 0)], "scatter_in_chip", 7)
    own = jnp.stack([lax.dynamic_slice_in_dim(gwt_in, _window_start(r_in, 2 * q + ac), win, axis=0) for q in range(4)])
    s1 = _sum_in_chip(own, recv1, "sum_in_chip")
    recv2 = _exchange_chips_async(s1, "scatter_chips", 6)

    dh1 = _mm(as_mat(dp_ab), wt_ab, name="d_in_ab")
    dh1 = _mm_chain(wide[:-1], wt_in, row_of, add=dh1, name="d_in", tk=d)
    dh1 = _Product(dgb, wt_in, b_rows=(o_gb, d), add=as_tok(dh1))
    (grad_x,), (dsh1, dsc1), (dn1w,) = _tok_bwd(_f_norm_mod_skip, [x], [sh1, sc1], [n1w], [dh1, dx_skip], [True],
                                                name="d_norm1", ts=256)
    reduced = _sum_chips(s1, recv2, (2 * ax + ay).reshape(1).astype(jnp.int32), "sum_chips")
    gt_w_in = lax.dynamic_slice_in_dim(reduced, r_in * dev - _window_start(r_in, dev), r_in, axis=0)
    g_w_in = gt_w_in.T.reshape(w_in.shape)
    gt_w_ffn_in = ffn_red[:rows[3]]
    g_w_ffn_in = gt_w_ffn_in.T.reshape(w_ffn_in.shape)
    g_w_ffn_out = ffn_red[rows[3]:].reshape(w_ffn_out.shape)
    g_w_gdn_proj, g_w_sc_out, g_w_o = (mix_red[offs[i]:offs[i] + rows[i]].reshape(ref.shape)
                                       for i, ref in enumerate((w_gdn_proj, w_sc_out, w_o)))

    dmod = jnp.concatenate([t.reshape(bl, d) for t in (dsh1, dsc1, dg1, dsh2, dsc2, dg2)], axis=1)
    dmodf = jnp.concatenate([t.reshape(bl, d) for t in (dshf, dscf)], axis=1)
    summed_parts = [dn1w, dn2w, dnfw, g_gnw, g_a_log, g_dt_bias, g_conv_w, g_sc_w, loss_l]
    partial = _all_gather(_pack([dmod, dmodf] + summed_parts, LANE, 8, F32), name="gather_small", hbm=False)
    partial = partial.reshape(NDEV, -1)
    n_rows = bl * (6 * d + 2 * d)
    dmod_all, dmodf_all = _unpack(partial[:, :n_rows], [(bl, 6 * d), (bl, 2 * d)])
    dmod_all, dmodf_all = dmod_all.reshape(NDEV * bl, 6 * d), dmodf_all.reshape(NDEV * bl, 2 * d)
    totals = _row_sum(partial[:, n_rows:], "sum_small")
    t_n1w, t_n2w, t_nfw, t_gnw, t_a_log, t_dt_bias, t_conv_w, t_sc_w, t_loss = [
        t[0] for t in _unpack(totals, [p.shape for p in summed_parts])]
    my_cols = lambda a, n: lax.dynamic_slice_in_dim(a, dev * n, n, axis=1)
    grads = {
        "w_ada": _mm(c_act, my_cols(dmod_all, n_ada), ta=True, name="g_ada").reshape(w_ada.shape),
        "b_ada": _row_sum(dmod_all, "g_ada_bias").reshape(b_ada.shape),
        "norm1_w": t_n1w.reshape(norm1_w.shape),
        "w_in": g_w_in,
        "gdn_conv_w": my_cols(t_conv_w, gdn_conv_w.shape[-1]).reshape(gdn_conv_w.shape),
        "gdn_a_log": t_a_log[:, :heads].reshape(gdn_a_log.shape),
        "gdn_dt_bias": t_dt_bias[:, :heads].reshape(gdn_dt_bias.shape),
        "gdn_norm_w": t_gnw.reshape(gdn_norm_w.shape),
        "w_gdn_proj": g_w_gdn_proj,
        "sc_conv_w": my_cols(t_sc_w, sc_conv_w.shape[-1]).reshape(sc_conv_w.shape),
        "w_sc_out": g_w_sc_out,
        "w_o": g_w_o,
        "norm2_w": t_n2w.reshape(norm2_w.shape),
        "w_ffn_in": g_w_ffn_in,
        "w_ffn_out": g_w_ffn_out,
        "w_ada_f": _mm(c_act, my_cols(dmodf_all, n_adaf), ta=True, name="g_adaf").reshape(w_ada_f.shape),
        "b_ada_f": _row_sum(dmodf_all, "g_adaf_bias").reshape(b_ada_f.shape),
        "normf_w": t_nfw.reshape(normf_w.shape),
    }
    weights = dict(w_ada=w_ada, b_ada=b_ada, norm1_w=norm1_w, w_in=w_in, gdn_conv_w=gdn_conv_w, gdn_a_log=gdn_a_log,
                   gdn_dt_bias=gdn_dt_bias, gdn_norm_w=gdn_norm_w, w_gdn_proj=w_gdn_proj, sc_conv_w=sc_conv_w,
                   w_sc_out=w_sc_out, w_o=w_o, norm2_w=norm2_w, w_ffn_in=w_ffn_in, w_ffn_out=w_ffn_out, w_ada_f=w_ada_f,
                   b_ada_f=b_ada_f, normf_w=normf_w)
    m_in = [m_w_ada, m_b_ada, m_norm1_w, m_w_in, m_gdn_conv_w, m_gdn_a_log, m_gdn_dt_bias, m_gdn_norm_w, m_w_gdn_proj,
            m_sc_conv_w, m_w_sc_out, m_w_o, m_norm2_w, m_w_ffn_in, m_w_ffn_out, m_w_ada_f, m_b_ada_f, m_normf_w]
    v_in = [v_w_ada, v_b_ada, v_norm1_w, v_w_in, v_gdn_conv_w, v_gdn_a_log, v_gdn_dt_bias, v_gdn_norm_w, v_w_gdn_proj,
            v_sc_conv_w, v_w_sc_out, v_w_o, v_norm2_w, v_w_ffn_in, v_w_ffn_out, v_w_ada_f, v_b_ada_f, v_normf_w]
    deltas, new_m, new_v = [], [], []
    grads_t = {"w_in": gt_w_in, "w_ffn_in": gt_w_ffn_in}
    for (wname, wt), mt, vt in zip(weights.items(), m_in, v_in):
        if wname in grads_t:
            back = lambda a, wt=wt: a.T.reshape(wt.shape)
            dl, mn, vn = (back(a) for a in _adamw(wt[0].T, grads_t[wname], mt[0].T, vt[0].T, "adamw_" + wname))
        else:
            dl, mn, vn = _adamw(wt, grads[wname], mt, vt, "adamw_" + wname)
        deltas.append(dl)
        new_m.append(mn)
        new_v.append(vn)
    loss = t_loss[0, 0]
    return (loss, grad_x, *[grads[k] for k in weights], *deltas, *new_m, *new_v)
```

```python
import functools

import jax
import jax.numpy as jnp
from jax import lax
from jax.experimental import pallas as pl
from jax.experimental.pallas import tpu as pltpu
from jax.experimental.pallas import tpu_sc as plsc

F32 = jnp.float32
MXU_DTYPE = jnp.bfloat16
NDEV = 8
CHUNK = 64
HEAD = 128
LANE = 128
EPS = 1e-6
ADAM_LR, ADAM_B1, ADAM_B2, ADAM_EPS, ADAM_WD, ADAM_STEP = 0.001, 0.9, 0.999, 1e-08, 0.01, 10
VMEM_LIMIT = 48 * 1024 * 1024
MESH_IDS = pl.DeviceIdType.MESH
HIGHEST = lax.Precision.HIGHEST


def _tile(n, cands=(512, 256, 128)):
    for c in cands:
        if n % c == 0:
            return c
    return n


def _cparams(*sem):
    return pltpu.CompilerParams(dimension_semantics=sem, vmem_limit_bytes=VMEM_LIMIT)


def _mm(a, b, *, ta=False, tb=False, add=None, out_dtype=F32, name, b_rows=None, out_rows=None, row_off=0, into=None,
        a_index=None):
    m, k = (a.shape[-1], a.shape[-2]) if ta else a.shape[-2:]
    b_shape = b.shape if b_rows is None else (b_rows[1], b.shape[1])
    n = b_shape[0] if tb else b_shape[1]
    assert k == (b_shape[1] if tb else b_shape[0])
    if ta:
        tm, tn = _tile(m), n if n <= 1024 else _tile(n)
        tk = k if k <= 4096 else _tile(k, (4096, 2048, 1024, 512))
        if tm * tk > 1024 * 2048:
            tk = _tile(k, (2048, 1024, 512))
    else:
        tk = k if k <= 1024 else _tile(k, (1024, 512))
        tn = _tile(n, (1024 if tk <= 1024 else 512, 512, 256, 128))
        tm = _tile(m, (2048 if (tn <= 512 and tk <= 1024) else 1024, 1024, 512, 256, 128))
    nk = k // tk
    dims = (((0 if ta else 1,), (1 if tb else 0,)), ((), ()))
    has_add = add is not None

    def body(*refs):
        a_ref, b_ref = refs[0], refs[1]
        add_ref = refs[2] if has_add else None
        o_ref = refs[2 + has_add + (into is not None)]
        part = lax.dot_general(a_ref[...].astype(MXU_DTYPE), b_ref[...].astype(MXU_DTYPE), dims,
                               preferred_element_type=F32)

        def finish(acc):
            if has_add:
                acc = acc + add_ref[...]
            o_ref[...] = acc.astype(o_ref.dtype)

        if nk == 1:
            finish(part)
        else:
            acc_ref = refs[-1]
            kk = pl.program_id(2)

            @pl.when(kk == 0)
            def _():
                acc_ref[...] = part

            @pl.when(kk > 0)
            def _():
                acc_ref[...] += part

            @pl.when(kk == nk - 1)
            def _():
                finish(acc_ref[...])

    a_blk, a_at = ((tk, tm), lambda i, j, kk: (kk, i)) if ta else ((tm, tk), lambda i, j, kk: (i, kk))
    a_spec = (pl.BlockSpec(a_blk, a_at) if a_index is None else
              pl.BlockSpec((None,) + a_blk, lambda i, j, kk: (a_index,) + a_at(i, j, kk)))
    if b_rows is None:
        b_spec = pl.BlockSpec((tn, tk), lambda i, j, kk: (j, kk)) if tb else pl.BlockSpec((tk, tn), lambda i, j, kk: (kk, j))
    else:
        at = lambda t: pl.multiple_of(b_rows[0] + t, ROW_ALIGN)
        b_spec = (pl.BlockSpec((pl.Element(tn), pl.Element(tk)), lambda i, j, kk: (at(j * tn), kk * tk)) if tb else
                  pl.BlockSpec((pl.Element(tk), pl.Element(tn)), lambda i, j, kk: (at(kk * tk), j * tn)))
    add_spec = pl.BlockSpec((tm, tn), lambda i, j, kk: (i, j))
    assert row_off % tm == 0
    o_spec = pl.BlockSpec((tm, tn), lambda i, j, kk: (i + row_off // tm, j))
    in_specs = [a_spec, b_spec] + ([add_spec] if has_add else []) + ([pl.BlockSpec(memory_space=pl.ANY)] if into is not None else [])
    args = [a, b] + ([add] if has_add else []) + ([into] if into is not None else [])
    return pl.pallas_call(
        body, name=name, grid=(m // tm, n // tn, nk), in_specs=in_specs, out_specs=o_spec,
        out_shape=jax.ShapeDtypeStruct((out_rows or m, n), out_dtype),
        scratch_shapes=[pltpu.VMEM((tm, tn), F32)] if nk > 1 else [],
        input_output_aliases={len(args) - 1: 0} if into is not None else {},
        compiler_params=_cparams("parallel", "parallel", "arbitrary"),
    )(*args)


def _mm_chain(parts, b, row_of_tile, *, add, name, tk=1024, tm=1024):
    m, n = parts[0].shape[-2], b.shape[1]
    tm = min(tm, m)
    tiles = [p.shape[0] if p.ndim == 3 else p.shape[1] // tk for p in parts]
    first = [sum(tiles[:s]) for s in range(len(parts))]
    nk = sum(tiles)

    def body(*refs):
        a_refs, b_ref, add_ref, o_ref, acc_ref = refs[:len(parts)], *refs[len(parts):]
        kk = pl.program_id(1)

        @pl.when(kk == 0)
        def _():
            acc_ref[...] = add_ref[...]

        for a_ref, lo, cnt in zip(a_refs, first, tiles):
            @pl.when(jnp.logical_and(kk >= lo, kk < lo + cnt))
            def _(a_ref=a_ref):
                acc_ref[...] += lax.dot_general(a_ref[...].astype(MXU_DTYPE), b_ref[...].astype(MXU_DTYPE),
                                                (((1,), (0,)), ((), ())), preferred_element_type=F32)

        @pl.when(kk == nk - 1)
        def _():
            o_ref[...] = acc_ref[...]

    tile_of = lambda kk, lo, cnt: jnp.clip(kk - lo, 0, cnt - 1)
    a_specs = [pl.BlockSpec((None, tm, tk), functools.partial(lambda i, kk, lo, cnt: (tile_of(kk, lo, cnt), i, 0), lo=lo, cnt=cnt))
               if p.ndim == 3 else
               pl.BlockSpec((tm, tk), functools.partial(lambda i, kk, lo, cnt: (i, tile_of(kk, lo, cnt)), lo=lo, cnt=cnt))
               for p, lo, cnt in zip(parts, first, tiles)]
    b_spec = pl.BlockSpec((pl.Element(tk), pl.Element(n)), lambda i, kk: (pl.multiple_of(row_of_tile(kk), ROW_ALIGN), 0))
    o_spec = pl.BlockSpec((tm, n), lambda i, kk: (i, 0))
    return pl.pallas_call(
        body, name=name, grid=(m // tm, nk), in_specs=a_specs + [b_spec, o_spec], out_specs=o_spec,
        out_shape=jax.ShapeDtypeStruct((m, n), F32), scratch_shapes=[pltpu.VMEM((tm, n), F32)],
        compiler_params=_cparams("parallel", "arbitrary"),
    )(*parts, b, add)


def _swiglu_tiles(m, half):
    tn = _tile(half, (512, 256, 128))
    return _tile(m, (2048 if tn <= 256 else 1024, 1024, 512, 256, 128)), tn


def _ffn_in_swiglu(h, wt, half, name):
    m, k = h.shape
    tm, tn = _swiglu_tiles(m, half)
    nj = half // tn
    dims = (((1,), (1,)), ((), ()))

    def body(h_ref, wa_ref, wb_ref, act_ref, a_ref, b_ref):
        lhs = h_ref[...].astype(MXU_DTYPE)
        a = lax.dot_general(lhs, wa_ref[...].astype(MXU_DTYPE), dims, preferred_element_type=F32)
        b = lax.dot_general(lhs, wb_ref[...].astype(MXU_DTYPE), dims, preferred_element_type=F32)
        act_ref[...] = (_silu(a) * b).astype(act_ref.dtype)
        a_ref[...] = a.astype(a_ref.dtype)
        b_ref[...] = b.astype(b_ref.dtype)

    out = jax.ShapeDtypeStruct((m, half), MXU_DTYPE)
    oblk = pl.BlockSpec((tm, tn), lambda i, j: (i, j))
    return pl.pallas_call(
        body, name=name, grid=(m // tm, nj),
        in_specs=[pl.BlockSpec((tm, k), lambda i, j: (i, 0)), pl.BlockSpec((tn, k), lambda i, j: (j, 0)),
                  pl.BlockSpec((tn, k), lambda i, j: (j + nj, 0))],
        out_specs=[oblk, oblk, oblk], out_shape=[out, out, out], compiler_params=_cparams("parallel", "parallel"),
    )(h, wt, wt)


def _ffn_out_bwd_swiglu(dff, w, a, b, name):
    m, k = dff.shape
    half = w.shape[0]
    tm, tn = _swiglu_tiles(m, half)

    def body(d_ref, w_ref, a_ref, b_ref, da_ref, db_ref):
        dact = lax.dot_general(d_ref[...].astype(MXU_DTYPE), w_ref[...].astype(MXU_DTYPE), (((1,), (1,)), ((), ())),
                               preferred_element_type=F32)
        av, bv = a_ref[...].astype(F32), b_ref[...].astype(F32)
        sig = jax.nn.sigmoid(av)
        da_ref[...] = (dact * bv * (sig * (1.0 + av * (1.0 - sig)))).astype(da_ref.dtype)
        db_ref[...] = (dact * (av * sig)).astype(db_ref.dtype)

    out = jax.ShapeDtypeStruct((m, half), MXU_DTYPE)
    oblk = pl.BlockSpec((tm, tn), lambda i, j: (i, j))
    return pl.pallas_call(
        body, name=name, grid=(m // tm, half // tn),
        in_specs=[pl.BlockSpec((tm, k), lambda i, j: (i, 0)), pl.BlockSpec((tn, k), lambda i, j: (j, 0)), oblk, oblk],
        out_specs=[oblk, oblk], out_shape=[out, out], compiler_params=_cparams("parallel", "parallel"),
    )(dff, w, a, b)


def _with_off(xs):
    return [x if isinstance(x, tuple) else (x, 0) for x in xs]


def _spec(kind, arr, off, ts, wb):
    w = arr.shape[-1] if wb is None else wb
    col = (lambda j: 0) if wb is None else functools.partial(lambda j, o: o + j, o=off)
    if kind == "tok":
        return pl.BlockSpec((None, ts, w), lambda j, b, i: (b, i, col(j)))
    if kind == "bat":
        return pl.BlockSpec((None, 1, w), lambda j, b, i: (b, 0, col(j)))
    if off is None:
        return pl.BlockSpec(arr.shape, lambda j, b, i: (0, 0))
    return pl.BlockSpec((arr.shape[0], w), lambda j, b, i: (0, col(j)))


class _Product:
    def __init__(self, a, b, *, tb=False, b_rows=None, add=None):
        self.a, self.b, self.tb, self.b_rows, self.add = a, b, tb, b_rows, add
        rows = b.shape[0] if b_rows is None else b_rows[1]
        self.shape = a.shape[:2] + (rows if tb else b.shape[1],)

    def inputs(self, ts):
        a_spec = pl.BlockSpec((None, ts, self.a.shape[2]), lambda j, b, i: (b, i, 0))
        if self.b_rows is None:
            b_spec = pl.BlockSpec(self.b.shape, lambda j, b, i: (0, 0))
        else:
            start, count = self.b_rows
            b_spec = pl.BlockSpec((pl.Element(count), pl.Element(self.b.shape[1])), lambda j, b, i: (start, 0))
        extra = [] if self.add is None else [(self.add, pl.BlockSpec((None, ts, self.shape[2]), lambda j, b, i: (b, i, 0)))]
        return [(self.a, a_spec), (self.b, b_spec)] + extra

    def value(self, refs):
        dims = (((1,), (1 if self.tb else 0,)), ((), ()))
        val = lax.dot_general(refs[0][...].astype(MXU_DTYPE), refs[1][...].astype(MXU_DTYPE), dims, preferred_element_type=F32)
        return val if self.add is None else val + refs[2][...].astype(F32)


def _inputs(groups, kinds, ts, wb):
    loaded = [(a, _spec(kind, a, o, ts, wb)) for g, kind in zip(groups, kinds) for a, o in g if not isinstance(a, _Product)]
    made = [pair for g in groups for a, _ in g if isinstance(a, _Product) for pair in a.inputs(ts)]
    return [a for a, _ in loaded + made], [sp for _, sp in loaded + made]


def _values(refs, groups):
    n_loaded = sum(1 for g in groups for a, _ in g if not isinstance(a, _Product))
    loaded, pos, out = iter(refs[:n_loaded]), n_loaded, []
    for g in groups:
        vals = []
        for a, _ in g:
            if isinstance(a, _Product):
                k = 2 if a.add is None else 3
                vals.append(a.value(refs[pos:pos + k]))
                pos += k
            else:
                vals.append(next(loaded)[...].astype(F32))
        out.append(vals)
    return out, pos


def _tok_fwd(fn, toks, bats, pars, outs, *, name, ts, wb=None, cols=1):
    groups = [_with_off(toks), _with_off(bats), _with_off(pars)]
    bl, s, _ = groups[0][0][0].shape
    ts = min(ts, s)
    args, in_specs = _inputs(groups, ("tok", "bat", "par"), ts, wb)

    def body(*refs):
        vals, n_in = _values(refs, groups)
        res = fn(*[v for g in vals for v in g])
        for r, val in zip(refs[n_in:], res):
            r[...] = val.astype(r.dtype)

    out_specs = [pl.BlockSpec((None, ts, w if wb is None else wb), lambda j, b, i: (b, i, j)) for w, _ in outs]
    return pl.pallas_call(
        body, name=name, grid=(cols, bl, s // ts), in_specs=in_specs,
        out_specs=out_specs, out_shape=[jax.ShapeDtypeStruct((bl, s, w), dt) for w, dt in outs],
        compiler_params=_cparams("parallel", "parallel", "parallel"),
    )(*args)


def _accumulate(ref, val, first):
    @pl.when(first)
    def _():
        ref[...] = val

    @pl.when(jnp.logical_not(first))
    def _():
        ref[...] += val


def _tok_bwd(fn, toks, bats, pars, cots, need, *, name, ts, wb=None, cols=1, tok_dtype=F32, loss=False, after=()):
    toks, bats, pars, cots = _with_off(toks), _with_off(bats), _with_off(pars), _with_off(cots)
    groups = [toks, bats, pars, cots]
    bl, s, _ = toks[0][0].shape
    ts = min(ts, s)
    nt, nb, npar = len(toks), len(bats), len(pars)
    args, in_specs = _inputs(groups, ("tok", "bat", "par", "tok"), ts, wb)
    args, in_specs = args + list(after), in_specs + [pl.BlockSpec(memory_space=pl.ANY)] * len(after)

    def body(*refs):
        j, b, i = pl.program_id(0), pl.program_id(1), pl.program_id(2)
        (tok_vals, bat_vals, par_vals, cot_vals), o = _values(refs, groups)
        o += len(after)
        outs, vjp = jax.vjp(fn, *tok_vals, *bat_vals, *par_vals)
        if loss:
            ct = (jnp.ones_like(outs[0]),)
            tot = jnp.broadcast_to(jnp.sum(outs[0], keepdims=True), (1, LANE))
            _accumulate(refs[o], tot, jnp.logical_and(b == 0, i == 0))
            o += 1
        else:
            ct = tuple(cot_vals)
        grads = vjp(ct)
        for t in range(nt):
            if need[t]:
                refs[o][...] = grads[t].astype(refs[o].dtype)
                o += 1
        for t in range(nb):
            _accumulate(refs[o], grads[nt + t], i == 0)
            o += 1
        for t in range(npar):
            first = jnp.logical_and(b == 0, i == 0)
            if pars[t][1] is None:
                first = jnp.logical_and(first, j == 0)
            _accumulate(refs[o], grads[nt + nb + t], first)
            o += 1

    full = lambda arr: arr.shape[-1] if wb is None else wb * cols
    blk = lambda arr: arr.shape[-1] if wb is None else wb
    out_specs, out_shape = [], []
    if loss:
        out_specs.append(pl.BlockSpec((1, LANE), lambda j, b, i: (0, 0)))
        out_shape.append(jax.ShapeDtypeStruct((1, LANE), F32))
    for t in range(nt):
        if need[t]:
            out_specs.append(pl.BlockSpec((None, ts, blk(toks[t][0])), lambda j, b, i: (b, i, j)))
            dt = tok_dtype[t] if isinstance(tok_dtype, (list, tuple)) else tok_dtype
            out_shape.append(jax.ShapeDtypeStruct((bl, s, full(toks[t][0])), dt))
    for arr, _ in bats:
        out_specs.append(pl.BlockSpec((None, 1, blk(arr)), lambda j, b, i: (b, 0, j)))
        out_shape.append(jax.ShapeDtypeStruct((bl, 1, full(arr)), F32))
    for arr, off in pars:
        if off is None:
            out_specs.append(pl.BlockSpec(arr.shape, lambda j, b, i: (0, 0)))
            out_shape.append(jax.ShapeDtypeStruct(arr.shape, F32))
        else:
            out_specs.append(pl.BlockSpec((arr.shape[0], blk(arr)), lambda j, b, i: (0, j)))
            out_shape.append(jax.ShapeDtypeStruct((arr.shape[0], full(arr)), F32))
    res = list(pl.pallas_call(
        body, name=name, grid=(cols, bl, s // ts), in_specs=in_specs,
        out_specs=out_specs, out_shape=out_shape, compiler_params=_cparams("arbitrary", "arbitrary", "arbitrary"),
    )(*args))
    tot = res.pop(0) if loss else None
    dtoks = [res.pop(0) if need[t] else None for t in range(nt)]
    dbats = [res.pop(0) for _ in range(nb)]
    dpars = [res.pop(0) for _ in range(npar)]
    return (tot, dtoks, dbats, dpars) if loss else (dtoks, dbats, dpars)


def _silu(x):
    return x * jax.nn.sigmoid(x)


def _rms(x, w):
    return x * lax.rsqrt(jnp.mean(x * x, axis=-1, keepdims=True) + EPS) * w


def _f_norm_mod(x, shift, scale, w):
    return (_rms(x, w) * (1.0 + scale) + shift,)


def _f_norm_mod_skip(x, shift, scale, w):
    return _rms(x, w) * (1.0 + scale) + shift, x


def _f_res_norm_mod(x, mix, gate, shift, scale, w):
    x2 = x + gate * mix
    return x2, _rms(x2, w) * (1.0 + scale) + shift


def _f_res_norm_mod_keep(x, mix, gate, shift, scale, w):
    return (*_f_res_norm_mod(x, mix, gate, shift, scale, w), mix)


def _f_gates(p, a_log, dt_bias, *, heads):
    z = p + dt_bias
    g = -jnp.exp(a_log) * (jnp.maximum(z, 0.0) + jnp.log1p(jnp.exp(jnp.minimum(z, -z))))
    lane = lax.broadcasted_iota(jnp.int32, p.shape, 1)
    return (jnp.where(lane < heads, g, jax.nn.sigmoid(p)),)


def _f_gdn_out(o, z, w):
    return (_rms(o, w) * _silu(z),)


def _f_merge(ga, gb, ya, yb):
    return (jax.nn.sigmoid(ga) * ya + jax.nn.sigmoid(gb) * yb,)


def _f_merge_keep(ga, gb, ya, yb):
    return (*_f_merge(ga, gb, ya, yb), yb)


def _f_loss(x2, ff, tgt, gate, shift, scale, w):
    y = _rms(x2 + gate * ff, w) * (1.0 + scale) + shift
    return (0.5 * jnp.mean(jnp.square(y - tgt), axis=-1, keepdims=True),)


def _shift_down(x, s):
    if s == 0:
        return x
    row = lax.broadcasted_iota(jnp.int32, x.shape, 0)
    return jnp.where(row >= s, pltpu.roll(x, s, 0), 0.0)


def _shift_up(x, s):
    if s == 0:
        return x
    n = x.shape[0]
    row = lax.broadcasted_iota(jnp.int32, x.shape, 0)
    return jnp.where(row < n - s, pltpu.roll(x, n - s, 0), 0.0)


def _conv(x, w):
    width = w.shape[0]
    acc = w[width - 1:width, :] * x
    for j in range(width - 1):
        acc = acc + w[j:j + 1, :] * _shift_down(x, width - 1 - j)
    return acc


def _conv_bwd(dy, x, w, dw_ref, first):
    width = w.shape[0]
    dx = w[width - 1:width, :] * dy
    for j in range(width - 1):
        dx = dx + w[j:j + 1, :] * _shift_up(dy, width - 1 - j)
    for j in range(width):
        row = jnp.sum(dy * _shift_down(x, width - 1 - j), axis=0, keepdims=True)
        _accumulate(dw_ref.at[j:j + 1, :], row, first)
    return dx


def _qkv_act(xc, is_v, scale):
    a = _silu(xc)
    nrm = a * lax.rsqrt(jnp.sum(a * a, axis=-1, keepdims=True) + EPS) * scale
    return jnp.where(is_v, a, nrm)


def _qkv_consts(j, heads):
    is_v = j >= 2 * heads
    scale = jnp.where(j < heads, HEAD ** -0.5, 1.0).astype(F32)
    return is_v, scale


def _qkv_fwd(p, w, heads, name):
    bl, s, w3 = p.shape

    def body(p_ref, w_ref, o_ref):
        is_v, scale = _qkv_consts(pl.program_id(0), heads)
        o_ref[...] = _qkv_act(_conv(p_ref[...], w_ref[...]), is_v, scale)

    blk = pl.BlockSpec((None, s, HEAD), lambda j, b: (b, 0, j))
    return pl.pallas_call(
        body, name=name, grid=(w3 // HEAD, bl), in_specs=[blk, pl.BlockSpec((w.shape[0], HEAD), lambda j, b: (0, j))],
        out_specs=blk, out_shape=jax.ShapeDtypeStruct(p.shape, F32), compiler_params=_cparams("parallel", "parallel"),
    )(p, w)


def _qkv_bwd(p, w, dout, heads, name):
    bl, s, w3 = p.shape

    def body(p_ref, w_ref, d_ref, dp_ref, dw_ref):
        is_v, scale = _qkv_consts(pl.program_id(0), heads)
        x, wv = p_ref[...], w_ref[...]
        _, vjp = jax.vjp(lambda xc: _qkv_act(xc, is_v, scale), _conv(x, wv))
        (dxc,) = vjp(d_ref[...])
        dp_ref[...] = _conv_bwd(dxc, x, wv, dw_ref, pl.program_id(1) == 0).astype(dp_ref.dtype)

    blk = pl.BlockSpec((None, s, HEAD), lambda j, b: (b, 0, j))
    wblk = pl.BlockSpec((w.shape[0], HEAD), lambda j, b: (0, j))
    return pl.pallas_call(
        body, name=name, grid=(w3 // HEAD, bl), in_specs=[blk, wblk, blk], out_specs=[blk, wblk],
        out_shape=[jax.ShapeDtypeStruct(p.shape, MXU_DTYPE), jax.ShapeDtypeStruct(w.shape, F32)],
        compiler_params=_cparams("arbitrary", "arbitrary"),
    )(p, w, dout)


def _sc_specs(p, w):
    bl, s, w3 = p.shape
    nblk = w3 // 3 // LANE
    sec = lambda k: pl.BlockSpec((None, s, LANE), functools.partial(lambda j, b, k: (b, 0, k * nblk + j), k=k))
    return nblk, [sec(0), sec(1), sec(2)], pl.BlockSpec((w.shape[0], LANE), lambda j, b: (0, j)), \
        pl.BlockSpec((None, s, LANE), lambda j, b: (b, 0, j))


def _sc_fwd(p, w, name):
    bl, s, w3 = p.shape
    nblk, secs, wblk, oblk = _sc_specs(p, w)

    def body(b_ref, c_ref, x_ref, w_ref, o_ref):
        o_ref[...] = (b_ref[...] * _conv(c_ref[...] * x_ref[...], w_ref[...])).astype(o_ref.dtype)

    return pl.pallas_call(
        body, name=name, grid=(nblk, bl), in_specs=secs + [wblk], out_specs=oblk,
        out_shape=jax.ShapeDtypeStruct((bl, s, w3 // 3), MXU_DTYPE), compiler_params=_cparams("parallel", "parallel"),
    )(p, p, p, w)


def _sc_bwd(p, w, dout, name):
    bl, s, w3 = p.shape
    nblk, secs, wblk, oblk = _sc_specs(p, w)

    def body(b_ref, c_ref, x_ref, w_ref, d_ref, dp_ref, dw_ref):
        gb, gc, xin, wv, d = b_ref[...], c_ref[...], x_ref[...], w_ref[...], d_ref[...]
        u = gc * xin
        dp_ref[0] = (d * _conv(u, wv)).astype(dp_ref.dtype)
        du = _conv_bwd(d * gb, u, wv, dw_ref, pl.program_id(1) == 0)
        dp_ref[1] = (du * xin).astype(dp_ref.dtype)
        dp_ref[2] = (du * gc).astype(dp_ref.dtype)

    return pl.pallas_call(
        body, name=name, grid=(nblk, bl), in_specs=secs + [wblk, oblk],
        out_specs=[pl.BlockSpec((3, None, s, LANE), lambda j, b: (0, b, 0, j)), wblk],
        out_shape=[jax.ShapeDtypeStruct((3, bl, s, w3 // 3), MXU_DTYPE), jax.ShapeDtypeStruct(w.shape, F32)],
        compiler_params=_cparams("arbitrary", "arbitrary"),
    )(p, p, p, w, dout)


def _bdot(a, b, ca, cb):
    return lax.dot_general(a.astype(MXU_DTYPE), b.astype(MXU_DTYPE), (((ca,), (cb,)), ((), ())),
                           preferred_element_type=F32)


def _hdot(a, b):
    return lax.dot_general(a, b, (((1,), (0,)), ((), ())), precision=HIGHEST, preferred_element_type=F32)


def _lane_col(x, idx):
    lane = lax.broadcasted_iota(jnp.int32, x.shape, 1)
    return jnp.sum(jnp.where(lane == idx, x, 0.0), axis=1, keepdims=True)


def _chunk_masks():
    r = lax.broadcasted_iota(jnp.int32, (CHUNK, CHUNK), 0)
    c = lax.broadcasted_iota(jnp.int32, (CHUNK, CHUNK), 1)
    return r == c, r >= c, r > c


def _dot3(a, b):
    ah, bh = a.astype(MXU_DTYPE), b.astype(MXU_DTYPE)
    al, bl = (a - ah.astype(F32)).astype(MXU_DTYPE), (b - bh.astype(F32)).astype(MXU_DTYPE)
    dot = lambda x, y: lax.dot_general(x, y, (((1,), (0,)), ((), ())), preferred_element_type=F32)
    return dot(ah, bh) + (dot(ah, bl) + dot(al, bh))


def _tri_inv_steps(low, eye):
    x = -low
    p = jnp.where(eye, 1.0, 0.0) + x
    span = 2
    while span < CHUNK:
        x = _dot3(x, x)
        yield
        p = p + _dot3(p, x)
        yield
        span *= 2
    return p


def _round_robin(gens):
    out, live = [None] * len(gens), list(range(len(gens)))
    while live:
        still = []
        for i in live:
            try:
                next(gens[i])
                still.append(i)
            except StopIteration as stop:
                out[i] = stop.value
        live = still
    return out


def _gdn_pre(q, k, v, gc, beta, masks):
    eye, causal, strict = masks
    gc_row = jnp.sum(jnp.where(eye, gc, 0.0), axis=0, keepdims=True)
    decay = jnp.where(causal, jnp.exp(jnp.where(causal, gc - gc_row, 0.0)), 0.0)
    eg = jnp.exp(gc)
    gl = gc[CHUNK - 1:CHUNK, :]
    kb, vb = k * beta, v * beta
    both = _bdot(jnp.concatenate([kb, q], axis=0), k, 1, 1)
    low = jnp.where(strict, both[:CHUNK] * decay, 0.0)
    qk = jnp.where(causal, both[CHUNK:] * decay, 0.0)
    rest = jnp.exp(gl - gc)
    return dict(decay=decay, eg=eg, gl=gl, kb=kb, vb=vb, kbe=kb * eg, low=low, qk=qk, qg=q * eg, rest=rest, kdec=k * rest)


def _gdn_specs(qkv, gbeta, heads, rev):
    bl, s, w3 = qkv.shape
    d, n = w3 // 3, s // CHUNK
    at = (lambda c: n - 1 - c) if rev else (lambda c: c)
    assert d == heads * HEAD
    sec = pl.BlockSpec((None, CHUNK, w3), lambda b, c: (b, at(c), 0))
    gspec = pl.BlockSpec((None, CHUNK, LANE), lambda b, c: (b, at(c), 0))
    sspec = pl.BlockSpec((None, None, heads, HEAD, HEAD), lambda b, c: (b, at(c), 0, 0, 0))
    tspec = pl.BlockSpec((None, None, heads, CHUNK, CHUNK), lambda b, c: (b, at(c), 0, 0, 0))
    return bl, s, d, n, sec, gspec, sspec, tspec


def _gdn_fwd(qkv, gbeta, heads, name):
    bl, s, d, n, sec, gspec, sspec, tspec = _gdn_specs(qkv, gbeta, heads, False)

    def body(x_ref, g_ref, o_ref, s_ref, t_ref, st_ref):
        @pl.when(pl.program_id(1) == 0)
        def _():
            st_ref[...] = jnp.zeros_like(st_ref)

        masks = _chunk_masks()
        eye, causal, _ = masks
        gblk = g_ref[...]
        gc_all = _hdot(jnp.where(causal, 1.0, 0.0), gblk)
        st_all = st_ref[...]

        def head(h):
            st = st_all[h]
            q, k, v = (x_ref[:, sec * d + h * HEAD:sec * d + (h + 1) * HEAD] for sec in range(3))
            pre = _gdn_pre(q, k, v, _lane_col(gc_all, h), _lane_col(gblk, heads + h), masks)
            yield
            t = yield from _tri_inv_steps(pre["low"], eye)
            uw = _bdot(t, jnp.concatenate([pre["vb"], pre["kbe"]], axis=1), 1, 0)
            u, w = uw[:, :HEAD], uw[:, HEAD:]
            yield
            vnew = u - _bdot(w, st, 1, 0)
            yield
            out = _bdot(pre["qg"], st, 1, 0) + _bdot(pre["qk"], vnew, 1, 0)
            return out, t, st * jnp.exp(pre["gl"]) + _bdot(pre["kdec"], vnew, 0, 0)

        outs, ts, states = zip(*_round_robin([head(h) for h in range(heads)]))
        o_ref[...] = jnp.concatenate(outs, axis=1)
        s_ref[...] = st_all
        t_ref[...] = jnp.stack(ts)
        st_ref[...] = jnp.stack(states)

    return pl.pallas_call(
        body, name=name, grid=(bl, n), in_specs=[sec, gspec],
        out_specs=[pl.BlockSpec((None, CHUNK, d), lambda b, c: (b, c, 0)), sspec, tspec],
        out_shape=[jax.ShapeDtypeStruct((bl, s, d), F32), jax.ShapeDtypeStruct((bl, n, heads, HEAD, HEAD), F32),
                   jax.ShapeDtypeStruct((bl, n, heads, CHUNK, CHUNK), F32)],
        scratch_shapes=[pltpu.VMEM((heads, HEAD, HEAD), F32)], compiler_params=_cparams("parallel", "arbitrary"),
    )(qkv, gbeta)


def _gdn_bwd(qkv, gbeta, dout, s_all, t_all, heads, name):
    bl, s, d, n, sec, gspec, sspec, tspec = _gdn_specs(qkv, gbeta, heads, True)
    ospec = pl.BlockSpec((None, CHUNK, d), lambda b, c: (b, n - 1 - c, 0))

    def body(x_ref, g_ref, do_ref, s_ref, t_ref, dx_ref, dg_ref, ds_ref):
        @pl.when(pl.program_id(1) == 0)
        def _():
            ds_ref[...] = jnp.zeros_like(ds_ref)

        masks = _chunk_masks()
        eye, causal, strict = masks
        gblk = g_ref[...]
        gc_all = _hdot(jnp.where(causal, 1.0, 0.0), gblk)
        lane = lax.broadcasted_iota(jnp.int32, gblk.shape, 1)
        last_row = lax.broadcasted_iota(jnp.int32, (CHUNK, 1), 0) == CHUNK - 1
        rowsum = lambda a: jnp.sum(a, axis=1, keepdims=True)
        st_all, t_all_, ds_all = s_ref[...], t_ref[...], ds_ref[...]

        def head(h):
            sl = slice(h * HEAD, (h + 1) * HEAD)
            q, k, v = (x_ref[:, sec * d + h * HEAD:sec * d + (h + 1) * HEAD] for sec in range(3))
            do = do_ref[:, sl]
            beta = _lane_col(gblk, heads + h)
            st, t, dsn = st_all[h], t_all_[h], ds_all[h]
            pre = _gdn_pre(q, k, v, _lane_col(gc_all, h), beta, masks)
            decay, eg, kb, vb, kbe, low, qk, qg, kdec = (pre[x] for x in ("decay", "eg", "kb", "vb", "kbe", "low", "qk", "qg", "kdec"))
            egl = jnp.exp(pre["gl"])
            yield
            uw = _bdot(t, jnp.concatenate([vb, kbe], axis=1), 1, 0)
            u, w = uw[:, :HEAD], uw[:, HEAD:]
            yield
            vnew = u - _bdot(w, st, 1, 0)
            yield
            stack, side = functools.partial(jnp.concatenate, axis=0), functools.partial(jnp.concatenate, axis=1)
            dkdec = _bdot(vnew, dsn, 1, 1)
            dvnew = _bdot(kdec, dsn, 1, 0) + _bdot(qk, do, 0, 0)
            dgl = jnp.sum(dsn * st, keepdims=True) * egl
            dqk = jnp.where(causal, _bdot(do, vnew, 1, 1), 0.0)
            yield
            by_state = _bdot(stack([do, dvnew]), st, 1, 1)
            dqg, dw = by_state[:CHUNK], -by_state[CHUNK:]
            ds_new = dsn * egl + _bdot(stack([qg, -w]), stack([do, dvnew]), 0, 0)
            yield
            dt = _bdot(side([dvnew, dw]), side([vb, kbe]), 1, 1)
            by_t = _bdot(t, side([dvnew, dw]), 0, 0)
            dvb, dkbe = by_t[:, :HEAD], by_t[:, HEAD:]
            yield
            inner = _bdot(dt, t, 1, 1)
            yield
            dlow = -jnp.where(strict, _bdot(t, inner, 0, 0), 0.0)
            da, db = dlow * decay, dqk * decay
            yield
            m = dlow * low + dqk * qk
            kdk = dkdec * kdec
            col_of_m = jnp.sum(jnp.where(eye, jnp.sum(m, axis=0, keepdims=True), 0.0), axis=1, keepdims=True)
            dgc = rowsum(m) - col_of_m + rowsum(dqg * qg) + rowsum(dkbe * kbe) - rowsum(kdk)
            dgc = dgc + jnp.where(last_row, dgl + jnp.sum(kdk, keepdims=True), 0.0)
            by_k = _bdot(stack([da, db]), k, 1, 0)
            dkb = by_k[:CHUNK] + dkbe * eg
            yield
            dk = _bdot(stack([da, db]), stack([kb, q]), 0, 0) + dkdec * pre["rest"] + dkb * beta
            dq = by_k[CHUNK:] + dqg * eg
            dbeta = rowsum(dkb * k) + rowsum(dvb * v)
            return dq, dk, dvb * beta, jnp.where(lane == h, dgc, 0.0) + jnp.where(lane == heads + h, dbeta, 0.0), ds_new

        dqs, dks, dvs, dgs, dss = zip(*_round_robin([head(h) for h in range(heads)]))
        dx_ref[...] = jnp.concatenate(dqs + dks + dvs, axis=1)
        ds_ref[...] = jnp.stack(dss)
        dgb = dgs[0]
        for extra in dgs[1:]:
            dgb = dgb + extra
        upper = jnp.where(jnp.logical_or(eye, jnp.logical_not(causal)), 1.0, 0.0)
        dg_ref[...] = jnp.where(lane < heads, _hdot(upper, dgb), dgb)

    return pl.pallas_call(
        body, name=name, grid=(bl, n), in_specs=[sec, gspec, ospec, sspec, tspec], out_specs=[sec, gspec],
        out_shape=[jax.ShapeDtypeStruct(qkv.shape, F32), jax.ShapeDtypeStruct((bl, s, LANE), F32)],
        scratch_shapes=[pltpu.VMEM((heads, HEAD, HEAD), F32)], compiler_params=_cparams("parallel", "arbitrary"),
    )(qkv, gbeta, dout, s_all, t_all)


def _position():
    return lax.axis_index("x"), lax.axis_index("y"), lax.axis_index("c")


def _all_gather(x, *, name, hbm):
    space = pltpu.HBM if hbm else pltpu.VMEM

    def body(x_ref, out_ref, send_sems, recv_sems, local_sem):
        ax, ay, ac = _position()
        me, sibling = (ax, ay, ac), (ax, ay, 1 - ac)
        chips = [(1 - ax, ay), (ax, 1 - ay), (1 - ax, 1 - ay)]

        def slot(px, py, pc):
            return out_ref.at[4 * px + 2 * py + pc]

        def copy(k, block, to, src=None):
            return pltpu.make_async_remote_copy(
                src_ref=slot(*block) if src is None else src, dst_ref=slot(*block), send_sem=send_sems.at[k],
                recv_sem=recv_sems.at[k], device_id=to, device_id_type=MESH_IDS)

        mine = pltpu.make_async_copy(x_ref, slot(*me), local_sem)
        mine.start()
        first = [copy(0, me, sibling, src=x_ref)] + [copy(1 + j, me, (*chip, ac), src=x_ref) for j, chip in enumerate(chips)]
        for cp in first:
            cp.start()
        passed = [copy(4 + j, (*chip, ac), sibling) for j, chip in enumerate(chips)]
        for j, chip in enumerate(chips):
            copy(1 + j, (*chip, ac), me).wait_recv()
            passed[j].start()
        copy(0, sibling, me).wait_recv()
        for j, chip in enumerate(chips):
            copy(4 + j, (*chip, 1 - ac), me).wait_recv()
        for cp in first + passed:
            cp.wait_send()
        mine.wait()

    return pl.pallas_call(
        body, name=name, out_shape=jax.ShapeDtypeStruct((NDEV,) + x.shape, x.dtype),
        in_specs=[pl.BlockSpec(memory_space=space)], out_specs=pl.BlockSpec(memory_space=space),
        scratch_shapes=[pltpu.SemaphoreType.DMA((7,)), pltpu.SemaphoreType.DMA((7,)), pltpu.SemaphoreType.DMA],
    )(x)


class _Rider:
    def __init__(self, arrays, out_shapes, sems, hooks):
        self.arrays, self.out_shapes, self.sems, self.hooks = arrays, out_shapes, sems, hooks


def _gather_rider(xs):
    n = len(xs)

    def hooks(x_refs, out_refs, send_sems, recv_sems):
        ax, ay, ac = _position()
        me, sibling = (ax, ay, ac), (ax, ay, 1 - ac)
        chips = [(1 - ax, ay), (ax, 1 - ay), (1 - ax, 1 - ay)]

        def copies(k, block, to, own=False):
            out = []
            for i in range(n):
                slot = out_refs[i].at[4 * block[0] + 2 * block[1] + block[2]]
                out.append(pltpu.make_async_remote_copy(
                    src_ref=x_refs[i] if own else slot, dst_ref=slot, send_sem=send_sems.at[k, i], recv_sem=recv_sems.at[k, i],
                    device_id=to, device_id_type=MESH_IDS))
            return out

        def first():
            for cp in copies(0, me, sibling, own=True):
                cp.start()
            for j, chip in enumerate(chips):
                for cp in copies(1 + j, me, (*chip, ac), own=True):
                    cp.start()

        def mid():
            for j, chip in enumerate(chips):
                for arrived, onward in zip(copies(1 + j, (*chip, ac), me), copies(4 + j, (*chip, ac), sibling)):
                    arrived.wait_recv()
                    onward.start()

        def last():
            for cp in copies(0, sibling, me):
                cp.wait_recv()
            for j, chip in enumerate(chips):
                for cp in copies(4 + j, (*chip, 1 - ac), me):
                    cp.wait_recv()
            for cp in copies(0, me, sibling, own=True):
                cp.wait_send()
            for j, chip in enumerate(chips):
                for cp in copies(1 + j, me, (*chip, ac), own=True) + copies(4 + j, (*chip, ac), sibling):
                    cp.wait_send()

        return first, mid, last

    return _Rider(list(xs), [jax.ShapeDtypeStruct((NDEV,) + x.shape, x.dtype) for x in xs],
                  [pltpu.SemaphoreType.DMA((7, n)), pltpu.SemaphoreType.DMA((7, n))], hooks)


def _scatter_rider(parts):
    packed = sum(r for _, r in parts)
    width, dtype = parts[0][0].shape[1], parts[0][0].dtype

    def hooks(g_refs, out_refs, send_sems, recv_sems):
        (recv_ref,) = out_refs
        ax, ay, ac = _position()

        def peer(rel):
            flip = lambda a, bit: 1 - a if rel & bit else a
            return flip(ax, 4), flip(ay, 2), flip(ac, 1)

        def first():
            for rel in range(1, NDEV):
                px, py, pc = peer(rel)
                off = 0
                for g_ref, (_, r) in zip(g_refs, parts):
                    rows = g_ref.at[pl.ds(pl.multiple_of((4 * px + 2 * py + pc) * r, ROW_ALIGN), r)]
                    pltpu.make_async_remote_copy(
                        src_ref=rows, dst_ref=recv_ref.at[rel - 1, pl.ds(off, r)], send_sem=send_sems.at[rel - 1],
                        recv_sem=recv_sems.at[rel - 1], device_id=(px, py, pc), device_id_type=MESH_IDS).start()
                    off += r

        def last():
            for rel in range(1, NDEV):
                slot = recv_ref.at[rel - 1]
                pltpu.make_async_remote_copy(src_ref=slot, dst_ref=slot, send_sem=send_sems.at[rel - 1],
                                             recv_sem=recv_sems.at[rel - 1], device_id=peer(rel), device_id_type=MESH_IDS).wait()

        return first, lambda: None, last

    return _Rider([g for g, _ in parts], [jax.ShapeDtypeStruct((NDEV - 1, packed, width), dtype)],
                  [pltpu.SemaphoreType.DMA((NDEV - 1,)), pltpu.SemaphoreType.DMA((NDEV - 1,))], hooks)


def _sum_direct(own, recv, name):
    r, w = own.shape
    tr = max(t for t in range(ROW_ALIGN, 257, ROW_ALIGN) if r % t == 0)

    def body(own_ref, *refs):
        acc = own_ref[...].astype(F32)
        for ref in refs[:-1]:
            acc = acc + ref[...].astype(F32)
        refs[-1][...] = acc

    rblk = lambda k: pl.BlockSpec((None, tr, w), functools.partial(lambda i, k: (k, i, 0), k=k))
    blk = pl.BlockSpec((tr, w), lambda i: (i, 0))
    return pl.pallas_call(body, name=name, grid=(r // tr,), in_specs=[blk] + [rblk(k) for k in range(NDEV - 1)],
                          out_specs=blk, out_shape=jax.ShapeDtypeStruct((r, w), F32),
                          compiler_params=_cparams("parallel"))(own, *([recv] * (NDEV - 1)))


ROW_ALIGN = 16


def _window_start(rows_per_dev, k):
    return rows_per_dev * k // ROW_ALIGN * ROW_ALIGN


def _exchange_in_chip(parts, name, collective_id):
    packed = sum(win for _, _, win, _ in parts)
    width, dtype = parts[0][0].shape[1], parts[0][0].dtype

    def body(g_refs, out_refs, send_sems, recv_sems):
        (recv_ref,) = out_refs
        ax, ay, ac = _position()
        sibling = (ax, ay, 1 - ac)
        _handshake([sibling])
        for q in range(4):
            for g_ref, (_, r, win, off) in zip(g_refs, parts):
                there = g_ref.at[pl.ds(pl.multiple_of(_window_start(r, 2 * q + 1 - ac), ROW_ALIGN), win)]
                pltpu.make_async_remote_copy(src_ref=there, dst_ref=recv_ref.at[q, pl.ds(off, win)], send_sem=send_sems.at[q],
                                             recv_sem=recv_sems.at[q], device_id=sibling, device_id_type=MESH_IDS).start()
        for q in range(4):
            pltpu.make_async_remote_copy(src_ref=recv_ref.at[q], dst_ref=recv_ref.at[q], send_sem=send_sems.at[q],
                                         recv_sem=recv_sems.at[q], device_id=sibling, device_id_type=MESH_IDS).wait()

    return _on_sequencer(body, [g for g, _, _, _ in parts], [jax.ShapeDtypeStruct((4, packed, width), dtype)],
                         [pltpu.SemaphoreType.DMA((4,)), pltpu.SemaphoreType.DMA((4,))], name=name, collective_id=collective_id)[0]


def _on_sequencer(body, ins, out_shapes, sems, *, name, collective_id):
    hbm = pltpu.MemorySpace.HBM
    in_refs = [jax.new_ref(a, memory_space=hbm) for a in ins]
    out_refs = [jax.empty_ref(s, memory_space=hbm) for s in out_shapes]

    @pl.kernel(mesh=plsc.ScalarSubcoreMesh(axis_name="sequencer", num_cores=1), name=name, scratch_types=tuple(sems),
               compiler_params=pltpu.CompilerParams(collective_id=collective_id))
    def launch(*sem_refs):
        body(in_refs, out_refs, *sem_refs)

    launch()
    return [r[...] for r in out_refs]


def _handshake(peers):
    barrier = pltpu.get_barrier_semaphore()
    for peer in peers:
        pl.semaphore_signal(barrier, inc=1, device_id=peer, device_id_type=MESH_IDS)
    pl.semaphore_wait(barrier, len(peers))


def _exchange_chips_async(s1, name, collective_id):
    def body(in_refs, out_refs, send_sems, recv_sems):
        (src,), (got,) = in_refs, out_refs
        ax, ay, ac = _position()
        chips = [(1 - ax, ay), (ax, 1 - ay), (1 - ax, 1 - ay)]
        _handshake([(cx, cy, ac) for cx, cy in chips])
        copies = [pltpu.make_async_remote_copy(
            src_ref=src.at[2 * cx + cy], dst_ref=got.at[r], send_sem=send_sems.at[r], recv_sem=recv_sems.at[r],
            device_id=(cx, cy, ac), device_id_type=MESH_IDS) for r, (cx, cy) in enumerate(chips)]
        for cp in copies:
            cp.start()
        for cp in copies:
            cp.wait_recv()
        for cp in copies:
            cp.wait_send()

    return _on_sequencer(body, [s1], [jax.ShapeDtypeStruct((3,) + s1.shape[1:], s1.dtype)],
                         [pltpu.SemaphoreType.DMA((3,)), pltpu.SemaphoreType.DMA((3,))], name=name, collective_id=collective_id)[0]


def _gather_async(xs, name, collective_id):
    rider = _gather_rider(xs)

    def body(in_refs, out_refs, send_sems, recv_sems):
        ax, ay, ac = _position()
        _handshake([(ax, ay, 1 - ac), (1 - ax, ay, ac), (ax, 1 - ay, ac), (1 - ax, 1 - ay, ac)])
        for hook in rider.hooks(in_refs, out_refs, send_sems, recv_sems):
            hook()

    return _on_sequencer(body, rider.arrays, rider.out_shapes, rider.sems, name=name, collective_id=collective_id)


def _scatter_async(parts, name, collective_id):
    rider = _scatter_rider(parts)

    def body(in_refs, out_refs, send_sems, recv_sems):
        ax, ay, ac = _position()
        flip = lambda a, on: 1 - a if on else a
        _handshake([(flip(ax, rel & 4), flip(ay, rel & 2), flip(ac, rel & 1)) for rel in range(1, NDEV)])
        for hook in rider.hooks(in_refs, out_refs, send_sems, recv_sems):
            hook()

    return _on_sequencer(body, rider.arrays, rider.out_shapes, rider.sems, name=name, collective_id=collective_id)[0]


def _sum_in_chip(own, recv, name):
    _, r, w = own.shape
    tr = _tile(r, (256, 128))

    def body(a_ref, b_ref, o_ref):
        o_ref[...] = (a_ref[...].astype(F32) + b_ref[...].astype(F32)).astype(o_ref.dtype)

    blk = pl.BlockSpec((None, tr, w), lambda q, i: (q, i, 0))
    return pl.pallas_call(body, name=name, grid=(4, r // tr), in_specs=[blk, blk], out_specs=blk,
                          out_shape=jax.ShapeDtypeStruct(own.shape, own.dtype),
                          compiler_params=_cparams("parallel", "parallel"))(own, recv)


def _sum_chips(s1, recv, chip, name):
    _, r, w = s1.shape
    tr = _tile(r, (256, 128))

    def body(c_ref, s_ref, r0_ref, r1_ref, r2_ref, o_ref):
        f = lambda ref: ref[...].astype(F32)
        o_ref[...] = ((f(s_ref) + f(r0_ref)) + f(r1_ref)) + f(r2_ref)

    rblk = lambda k: pl.BlockSpec((None, tr, w), functools.partial(lambda i, c, k: (k, i, 0), k=k))
    grid_spec = pltpu.PrefetchScalarGridSpec(
        num_scalar_prefetch=1, grid=(r // tr,),
        in_specs=[pl.BlockSpec((None, tr, w), lambda i, c: (c[0], i, 0)), rblk(0), rblk(1), rblk(2)],
        out_specs=pl.BlockSpec((tr, w), lambda i, c: (i, 0)))
    return pl.pallas_call(body, name=name, grid_spec=grid_spec, out_shape=jax.ShapeDtypeStruct((r, w), F32),
                          compiler_params=_cparams("parallel"))(chip, s1, recv, recv, recv)


def _silu_rows(x, name):
    def body(x_ref, o_ref):
        o_ref[...] = _silu(x_ref[...])

    return pl.pallas_call(body, name=name, out_shape=jax.ShapeDtypeStruct(x.shape, F32))(x)


def _row_sum(x, name):
    def body(x_ref, o_ref):
        acc = x_ref[0:1, :]
        for i in range(1, x.shape[0]):
            acc = acc + x_ref[i:i + 1, :]
        o_ref[...] = acc

    return pl.pallas_call(body, name=name, out_shape=jax.ShapeDtypeStruct((1, x.shape[1]), F32))(x)


def _adamw(w, g, m, v, name):
    cols = w.shape[-1]
    rows = w.size // cols
    tr = _tile(rows, (128,))
    tc = LANE if (tr == rows and rows > 512 and cols % LANE == 0) else cols

    def body(w_ref, g_ref, m_ref, v_ref, d_ref, mo_ref, vo_ref):
        grad = g_ref[...]
        m_new = ADAM_B1 * m_ref[...] + (1.0 - ADAM_B1) * grad
        v_new = ADAM_B2 * v_ref[...] + (1.0 - ADAM_B2) * jnp.square(grad)
        m_hat = m_new / (1.0 - ADAM_B1 ** ADAM_STEP)
        v_hat = v_new / (1.0 - ADAM_B2 ** ADAM_STEP)
        d_ref[...] = -ADAM_LR * (m_hat / (jnp.sqrt(v_hat) + ADAM_EPS) + ADAM_WD * w_ref[...])
        mo_ref[...] = m_new
        vo_ref[...] = v_new

    blk = pl.BlockSpec((tr, tc), lambda i, j: (i, j))
    out = pl.pallas_call(
        body, name=name, grid=(rows // tr, cols // tc), in_specs=[blk] * 4, out_specs=[blk] * 3,
        out_shape=[jax.ShapeDtypeStruct((rows, cols), F32)] * 3, compiler_params=_cparams("parallel", "parallel"),
    )(*[t.reshape(rows, cols) for t in (w, g, m, v)])
    return [t.reshape(w.shape) for t in out]


def _pack(parts, width, row_mult, dtype):
    flat = jnp.concatenate([p.reshape(-1).astype(dtype) for p in parts])
    rows = -(-flat.shape[0] // (width * row_mult)) * row_mult
    return jnp.pad(flat, (0, rows * width - flat.shape[0])).reshape(rows, width)


def _unpack(flat, shapes):
    out, off = [], 0
    for shp in shapes:
        size = 1
        for dim in shp:
            size *= dim
        out.append(flat[:, off:off + size].reshape((flat.shape[0],) + tuple(shp)))
        off += size
    return out


def _devices_to_cols(a):
    _, r, c = a.shape
    return a.transpose(1, 0, 2).reshape(r, NDEV * c)


def kernel(x, c, w_ada, b_ada, norm1_w, w_in, gdn_conv_w, gdn_a_log, gdn_dt_bias, gdn_norm_w, w_gdn_proj, sc_conv_w, w_sc_out, w_o, norm2_w, w_ffn_in, w_ffn_out, w_ada_f, b_ada_f, normf_w, loss_target, m_w_ada, m_b_ada, m_norm1_w, m_w_in, m_gdn_conv_w, m_gdn_a_log, m_gdn_dt_bias, m_gdn_norm_w, m_w_gdn_proj, m_sc_conv_w, m_w_sc_out, m_w_o, m_norm2_w, m_w_ffn_in, m_w_ffn_out, m_w_ada_f, m_b_ada_f, m_normf_w, v_w_ada, v_b_ada, v_norm1_w, v_w_in, v_gdn_conv_w, v_gdn_a_log, v_gdn_dt_bias, v_gdn_norm_w, v_w_gdn_proj, v_sc_conv_w, v_w_sc_out, v_w_o, v_norm2_w, v_w_ffn_in, v_w_ffn_out, v_w_ada_f, v_b_ada_f, v_normf_w):
    bl, s, d = x.shape
    heads = gdn_a_log.shape[-1]
    dff = w_ffn_out.shape[1] * NDEV
    tok = bl * s
    ax, ay, ac = _position()
    dev = 4 * ax + 2 * ay + ac
    as_tok = lambda a: a.reshape(bl, s, a.shape[-1])
    as_mat = lambda a: a.reshape(tok, a.shape[-1])

    small = _all_gather(_pack([c, gdn_conv_w, sc_conv_w], LANE, 8, F32), name="gather_cond", hbm=False)
    c_all, conv_w, sc_w = _unpack(small.reshape(NDEV, -1), [(bl, d), gdn_conv_w.shape[1:], sc_conv_w.shape[1:]])
    c_act = _silu_rows(c_all.reshape(NDEV * bl, d), "cond_silu")
    conv_w, sc_w = _devices_to_cols(conv_w), _devices_to_cols(sc_w)
    n_ada, n_adaf = w_ada.shape[-1], w_ada_f.shape[-1]
    bias = jnp.broadcast_to(lax.dynamic_slice_in_dim(b_ada, dev * n_ada, n_ada, axis=1), (NDEV * bl, n_ada))
    biasf = jnp.broadcast_to(lax.dynamic_slice_in_dim(b_ada_f.reshape(1, -1), dev * n_adaf, n_adaf, axis=1), (NDEV * bl, n_adaf))
    mod_cols = _mm(c_act, w_ada[0], add=bias, name="ada_cols")
    modf_cols = _mm(c_act, w_ada_f, add=biasf, name="adaf_cols")
    mods = _all_gather(jnp.concatenate([mod_cols, modf_cols], axis=1), name="gather_mod", hbm=False)
    mod_all = mods[:, :, :n_ada].transpose(1, 0, 2).reshape(NDEV * bl, NDEV * n_ada)
    modf_all = mods[:, :, n_ada:].transpose(1, 0, 2).reshape(NDEV * bl, NDEV * n_adaf)
    my_rows = lambda a: lax.dynamic_slice_in_dim(a, dev * bl, bl, axis=0)
    sh1, sc1, g1, sh2, sc2, g2 = [t.reshape(bl, 1, d) for t in jnp.split(my_rows(mod_all), 6, axis=1)]
    shf, scf = [t.reshape(bl, 1, d) for t in jnp.split(my_rows(modf_all), 2, axis=1)]

    late = [t.astype(MXU_DTYPE) for t in (w_gdn_proj[0], w_sc_out[0], w_o[0], w_ffn_in[0].T, w_ffn_out[0])]
    rows = [t.shape[0] for t in late] + [w_in.shape[-1]]
    offs = [sum(rows[:i]) for i in range(5)]
    in_send = w_in[0].T.astype(MXU_DTYPE)
    with_own = lambda g, own: lax.dynamic_update_slice_in_dim(g, own[None], dev, axis=0)
    (wt_in,) = _gather_async([in_send], "gather_w_in", 1)
    wt_in = with_own(wt_in, in_send).reshape(NDEV * rows[5], d)
    gathered = _gather_async(late[:3], "gather_mixer", 2) + _gather_async(late[3:], "gather_ffn", 3)
    wgp, wso, wo, wt_fi, wfo = [with_own(g, own).reshape(NDEV * own.shape[0], d) for g, own in zip(gathered, late)]
    o_z, o_ab, o_sc, o_ga, o_gb = 3 * d, 4 * d, 4 * d + 2 * heads, 7 * d + 2 * heads, 8 * d + 2 * heads
    s_qkv, s_z, s_sc, s_gate = (0, o_z), (o_z, d), (o_sc, 3 * d), (o_ga, 2 * d)
    wt_ab = jnp.pad(wt_in[o_ab:o_sc], ((0, LANE - 2 * heads), (0, 0)))

    n1w, n2w, nfw = norm1_w.reshape(1, d), norm2_w.reshape(1, d), normf_w.reshape(1, d)
    lanes = lambda a: jnp.pad(a.reshape(1, -1), ((0, 0), (0, LANE - a.size)))
    a_log, dt_bias, gnw = lanes(gdn_a_log), lanes(gdn_dt_bias), gdn_norm_w.reshape(1, HEAD)
    f_gates = functools.partial(_f_gates, heads=heads)
    (h1,) = _tok_fwd(_f_norm_mod, [x], [sh1, sc1], [n1w], [(d, MXU_DTYPE)], name="norm1", ts=512)
    h1m = as_mat(h1)
    p_qkv = as_tok(_mm(h1m, wt_in, tb=True, b_rows=s_qkv, name="in_qkv"))
    p_z = as_tok(_mm(h1m, wt_in, tb=True, b_rows=s_z, name="in_z"))
    p_ab = as_tok(_mm(h1m, wt_ab, tb=True, name="in_ab"))
    p_sc = as_tok(_mm(h1m, wt_in, tb=True, b_rows=s_sc, name="in_sc"))
    p_g = as_tok(_mm(h1m, wt_in, tb=True, b_rows=s_gate, name="in_gate"))
    qkv = _qkv_fwd(p_qkv, conv_w, heads, "qkv_conv")
    (gbeta,) = _tok_fwd(f_gates, [p_ab], [], [a_log, dt_bias], [(LANE, F32)], name="gates", ts=512)
    o, s_all, t_all = _gdn_fwd(qkv, gbeta, heads, "gdn")
    (og,) = _tok_fwd(_f_gdn_out, [o, p_z], [], [(gnw, None)], [(d, MXU_DTYPE)], name="gdn_out", ts=2048, wb=HEAD, cols=heads)
    y_a = as_tok(_mm(as_mat(og), wgp, name="gdn_proj"))
    scp = _sc_fwd(p_sc, sc_w, "sc_conv")
    mrg, y_b = _tok_fwd(_f_merge_keep, [(p_g, 0), (p_g, 1), y_a, _Product(scp, wso)], [], [], [(d, MXU_DTYPE), (d, F32)],
                        name="merge", ts=256, wb=d)
    merge_toks = [(p_g, 0), (p_g, 1), y_a, y_b]
    x2, h2, mix = _tok_fwd(_f_res_norm_mod_keep, [x, _Product(mrg, wo)], [g1, sh2, sc2], [n2w],
                           [(d, F32), (d, MXU_DTYPE), (d, F32)], name="norm2", ts=512)
    act, gu_a, gu_b = _ffn_in_swiglu(as_mat(h2), wt_fi, dff, "ffn_in")

    loss_l, (dx2, dff_out, _), (dg2, dshf, dscf), (dnfw,) = _tok_bwd(
        _f_loss, [x2, _Product(as_tok(act), wfo), loss_target], [g2, shf, scf], [nfw], [], [True, True, False], name="loss",
        ts=256, loss=True, tok_dtype=[F32, MXU_DTYPE, None])
    dffm = as_mat(dff_out)
    dgu_a, dgu_b = _ffn_out_bwd_swiglu(dffm, wfo, gu_a, gu_b, "d_ffn_out")
    gmm = functools.partial(_mm, ta=True, out_dtype=MXU_DTYPE)
    gw_ffn_out = gmm(act, dffm, name="g_ffn_out")
    dh2 = _Product(as_tok(dgu_b), wt_fi, b_rows=(dff, dff), add=as_tok(_mm(dgu_a, wt_fi, b_rows=(0, dff), name="d_ffn_in_a")))
    h2m = as_mat(h2)
    gwt_ffn_in = gmm(dgu_a, h2m, out_rows=2 * dff, name="g_ffn_in_a")
    gwt_ffn_in = gmm(dgu_b, h2m, out_rows=2 * dff, row_off=dff, into=gwt_ffn_in, name="g_ffn_in_b")
    ffn_parts = [(gwt_ffn_in, rows[3]), (gw_ffn_out, rows[4])]
    ffn_recv = _scatter_async(ffn_parts, "scatter_ffn", 4)
    (dx_skip, dmix), (dg1, dsh2, dsc2), (dn2w,) = _tok_bwd(
        _f_res_norm_mod, [x, mix], [g1, sh2, sc2], [n2w], [dx2, dh2], [True, True], name="d_norm2", ts=256,
        tok_dtype=[F32, MXU_DTYPE], after=[gwt_ffn_in, gw_ffn_out])
    gw_o = gmm(as_mat(mrg), as_mat(dmix), name="g_mix_out")
    (dga, dgb, dya, dyb), _, _ = _tok_bwd(_f_merge, merge_toks, [], [], [_Product(dmix, wo, tb=True)], [True] * 4,
                                          name="d_merge", ts=256, wb=d, tok_dtype=MXU_DTYPE)
    dyam, dybm = as_mat(dya), as_mat(dyb)
    dog = as_tok(_mm(dyam, wgp, tb=True, name="d_gdn_proj"))
    gw_gdn_proj = gmm(as_mat(og), dyam, name="g_gdn_proj")
    dscp = as_tok(_mm(dybm, wso, tb=True, name="d_sc_out"))
    gw_sc_out = gmm(as_mat(scp), dybm, name="g_sc_out")
    dsc, g_sc_w = _sc_bwd(p_sc, sc_w, dscp, "d_sc_conv")
    mix_parts = [(gw_gdn_proj, rows[0]), (gw_sc_out, rows[1]), (gw_o, rows[2])]
    mix_recv = _scatter_async(mix_parts, "scatter_mixer", 5)
    (do, dz), _, (g_gnw,) = _tok_bwd(_f_gdn_out, [o, p_z], [], [(gnw, None)], [dog], [True, True], name="d_gdn_out",
                                     ts=2048, wb=HEAD, cols=heads, tok_dtype=[F32, MXU_DTYPE],
                                     after=[gw_gdn_proj, gw_sc_out, gw_o])
    own_rows = lambda parts: jnp.concatenate([lax.dynamic_slice_in_dim(g, dev * r, r, axis=0) for g, r in parts], axis=0)
    dqkv, dgbeta = _gdn_bwd(qkv, gbeta, do, s_all, t_all, heads, "d_gdn")
    dp_qkv, g_conv_w = _qkv_bwd(p_qkv, conv_w, dqkv, heads, "d_qkv_conv")
    ffn_red = _sum_direct(own_rows(ffn_parts), ffn_recv, "sum_ffn")
    mix_red = _sum_direct(own_rows(mix_parts), mix_recv, "sum_mix")
    (dp_ab,), _, (g_a_log, g_dt_bias) = _tok_bwd(f_gates, [p_ab], [], [a_log, dt_bias], [dgbeta], [True], name="d_gates",
                                                 ts=512, tok_dtype=MXU_DTYPE, after=[ffn_red, mix_red])
    r_in = rows[5]
    win = -(-(r_in + max(r_in * k % ROW_ALIGN for k in range(NDEV))) // 128) * 128
    need_rows = max(_window_start(r_in, k) for k in range(NDEV)) + win
    dsc_m = dsc.reshape(3, tok, d)
    gwt_in = ([gmm(as_mat(dp_qkv), h1m, name="g_in_qkv"), gmm(as_mat(dz), h1m, name="g_in_z"),
               gmm(as_mat(dp_ab), h1m, name="g_in_ab")[:2 * heads]]
              + [gmm(dsc_m, h1m, a_index=k, name=f"g_in_sc{k}") for k in range(3)]
              + [gmm(as_mat(dga), h1m, name="g_in_ga"), gmm(as_mat(dgb), h1m, name="g_in_gb")])
    gwt_in = jnp.concatenate(gwt_in + [jnp.zeros((need_rows - NDEV * r_in, d), MXU_DTYPE)], axis=0)
    assert d <= 1024
    wide = [as_mat(dp_qkv), as_mat(dz), dsc_m, as_mat(dga)]
    row_of = lambda t: d * t + jnp.where(t * d >= o_ab, 2 * heads, 0)
    recv1 = _exchange_in_chip([(gwt_in, r_in, win, 0)], "scatter_in_chip", 7)
    own = jnp.stack([lax.dynamic_slice_in_dim(gwt_in, _window_start(r_in, 2 * q + ac), win, axis=0) for q in range(4)])
    s1 = _sum_in_chip(own, recv1, "sum_in_chip")
    recv2 = _exchange_chips_async(s1, "scatter_chips", 6)

    dh1 = _mm(as_mat(dp_ab), wt_ab, name="d_in_ab")
    dh1 = _mm_chain(wide, wt_in, row_of, add=dh1, name="d_in", tk=d)
    dh1 = _Product(dgb, wt_in, b_rows=(o_gb, d), add=as_tok(dh1))
    (grad_x,), (dsh1, dsc1), (dn1w,) = _tok_bwd(_f_norm_mod_skip, [x], [sh1, sc1], [n1w], [dh1, dx_skip], [True],
                                                name="d_norm1", ts=256)
    reduced = _sum_chips(s1, recv2, (2 * ax + ay).reshape(1).astype(jnp.int32), "sum_chips")
    gt_w_in = lax.dynamic_slice_in_dim(reduced, r_in * dev - _window_start(r_in, dev), r_in, axis=0)
    g_w_in = gt_w_in.T.reshape(w_in.shape)
    gt_w_ffn_in = ffn_red[:rows[3]]
    g_w_ffn_in = gt_w_ffn_in.T.reshape(w_ffn_in.shape)
    g_w_ffn_out = ffn_red[rows[3]:].reshape(w_ffn_out.shape)
    g_w_gdn_proj, g_w_sc_out, g_w_o = (mix_red[offs[i]:offs[i] + rows[i]].reshape(ref.shape)
                                       for i, ref in enumerate((w_gdn_proj, w_sc_out, w_o)))

    dmod = jnp.concatenate([t.reshape(bl, d) for t in (dsh1, dsc1, dg1, dsh2, dsc2, dg2)], axis=1)
    dmodf = jnp.concatenate([t.reshape(bl, d) for t in (dshf, dscf)], axis=1)
    summed_parts = [dn1w, dn2w, dnfw, g_gnw, g_a_log, g_dt_bias, g_conv_w, g_sc_w, loss_l]
    partial = _all_gather(_pack([dmod, dmodf] + summed_parts, LANE, 8, F32), name="gather_small", hbm=False)
    partial = partial.reshape(NDEV, -1)
    n_rows = bl * (6 * d + 2 * d)
    dmod_all, dmodf_all = _unpack(partial[:, :n_rows], [(bl, 6 * d), (bl, 2 * d)])
    dmod_all, dmodf_all = dmod_all.reshape(NDEV * bl, 6 * d), dmodf_all.reshape(NDEV * bl, 2 * d)
    totals = _row_sum(partial[:, n_rows:], "sum_small")
    t_n1w, t_n2w, t_nfw, t_gnw, t_a_log, t_dt_bias, t_conv_w, t_sc_w, t_loss = [
        t[0] for t in _unpack(totals, [p.shape for p in summed_parts])]
    my_cols = lambda a, n: lax.dynamic_slice_in_dim(a, dev * n, n, axis=1)
    grads = {
        "w_ada": _mm(c_act, my_cols(dmod_all, n_ada), ta=True, name="g_ada").reshape(w_ada.shape),
        "b_ada": _row_sum(dmod_all, "g_ada_bias").reshape(b_ada.shape),
        "norm1_w": t_n1w.reshape(norm1_w.shape),
        "w_in": g_w_in,
        "gdn_conv_w": my_cols(t_conv_w, gdn_conv_w.shape[-1]).reshape(gdn_conv_w.shape),
        "gdn_a_log": t_a_log[:, :heads].reshape(gdn_a_log.shape),
        "gdn_dt_bias": t_dt_bias[:, :heads].reshape(gdn_dt_bias.shape),
        "gdn_norm_w": t_gnw.reshape(gdn_norm_w.shape),
        "w_gdn_proj": g_w_gdn_proj,
        "sc_conv_w": my_cols(t_sc_w, sc_conv_w.shape[-1]).reshape(sc_conv_w.shape),
        "w_sc_out": g_w_sc_out,
        "w_o": g_w_o,
        "norm2_w": t_n2w.reshape(norm2_w.shape),
        "w_ffn_in": g_w_ffn_in,
        "w_ffn_out": g_w_ffn_out,
        "w_ada_f": _mm(c_act, my_cols(dmodf_all, n_adaf), ta=True, name="g_adaf").reshape(w_ada_f.shape),
        "b_ada_f": _row_sum(dmodf_all, "g_adaf_bias").reshape(b_ada_f.shape),
        "normf_w": t_nfw.reshape(normf_w.shape),
    }
    weights = dict(w_ada=w_ada, b_ada=b_ada, norm1_w=norm1_w, w_in=w_in, gdn_conv_w=gdn_conv_w, gdn_a_log=gdn_a_log,
                   gdn_dt_bias=gdn_dt_bias, gdn_norm_w=gdn_norm_w, w_gdn_proj=w_gdn_proj, sc_conv_w=sc_conv_w,
                   w_sc_out=w_sc_out, w_o=w_o, norm2_w=norm2_w, w_ffn_in=w_ffn_in, w_ffn_out=w_ffn_out, w_ada_f=w_ada_f,
                   b_ada_f=b_ada_f, normf_w=normf_w)
    m_in = [m_w_ada, m_b_ada, m_norm1_w, m_w_in, m_gdn_conv_w, m_gdn_a_log, m_gdn_dt_bias, m_gdn_norm_w, m_w_gdn_proj,
            m_sc_conv_w, m_w_sc_out, m_w_o, m_norm2_w, m_w_ffn_in, m_w_ffn_out, m_w_ada_f, m_b_ada_f, m_normf_w]
    v_in = [v_w_ada, v_b_ada, v_norm1_w, v_w_in, v_gdn_conv_w, v_gdn_a_log, v_gdn_dt_bias, v_gdn_norm_w, v_w_gdn_proj,
            v_sc_conv_w, v_w_sc_out, v_w_o, v_norm2_w, v_w_ffn_in, v_w_ffn_out, v_w_ada_f, v_b_ada_f, v_normf_w]
    deltas, new_m, new_v = [], [], []
    grads_t = {"w_in": gt_w_in, "w_ffn_in": gt_w_ffn_in}
    for (wname, wt), mt, vt in zip(weights.items(), m_in, v_in):
        if wname in grads_t:
            back = lambda a, wt=wt: a.T.reshape(wt.shape)
            dl, mn, vn = (back(a) for a in _adamw(wt[0].T, grads_t[wname], mt[0].T, vt[0].T, "adamw_" + wname))
        else:
            dl, mn, vn = _adamw(wt, grads[wname], mt, vt, "adamw_" + wname)
        deltas.append(dl)
        new_m.append(mn)
        new_v.append(vn)
    loss = t_loss[0, 0]
    return (loss, grad_x, *[grads[k] for k in weights], *deltas, *new_m, *new_v)
```

```python
import functools

import jax
import jax.numpy as jnp
from jax import lax
from jax.experimental import pallas as pl
from jax.experimental.pallas import tpu as pltpu
from jax.experimental.pallas import tpu_sc as plsc

F32 = jnp.float32
MXU_DTYPE = jnp.bfloat16
NDEV = 8
CHUNK = 64
HEAD = 128
LANE = 128
EPS = 1e-6
ADAM_LR, ADAM_B1, ADAM_B2, ADAM_EPS, ADAM_WD, ADAM_STEP = 0.001, 0.9, 0.999, 1e-08, 0.01, 10
VMEM_LIMIT = 48 * 1024 * 1024
MESH_IDS = pl.DeviceIdType.MESH
HIGHEST = lax.Precision.HIGHEST


def _tile(n, cands=(512, 256, 128)):
    for c in cands:
        if n % c == 0:
            return c
    return n


def _cparams(*sem):
    return pltpu.CompilerParams(dimension_semantics=sem, vmem_limit_bytes=VMEM_LIMIT)


def _mm(a, b, *, ta=False, tb=False, add=None, out_dtype=F32, name, b_rows=None, out_rows=None, row_off=0, into=None,
        a_index=None):
    m, k = (a.shape[-1], a.shape[-2]) if ta else a.shape[-2:]
    b_shape = b.shape if b_rows is None else (b_rows[1], b.shape[1])
    n = b_shape[0] if tb else b_shape[1]
    assert k == (b_shape[1] if tb else b_shape[0])
    if ta:
        tm, tn = _tile(m), n if n <= 1024 else _tile(n)
        tk = k if k <= 4096 else _tile(k, (4096, 2048, 1024, 512))
        if tm * tk > 1024 * 2048:
            tk = _tile(k, (2048, 1024, 512))
    else:
        tk = k if k <= 1024 else _tile(k, (1024, 512))
        tn = _tile(n, (1024 if tk <= 1024 else 512, 512, 256, 128))
        tm = _tile(m, (2048 if (tn <= 512 and tk <= 1024) else 1024, 1024, 512, 256, 128))
    nk = k // tk
    dims = (((0 if ta else 1,), (1 if tb else 0,)), ((), ()))
    has_add = add is not None

    def body(*refs):
        a_ref, b_ref = refs[0], refs[1]
        add_ref = refs[2] if has_add else None
        o_ref = refs[2 + has_add + (into is not None)]
        part = lax.dot_general(a_ref[...].astype(MXU_DTYPE), b_ref[...].astype(MXU_DTYPE), dims,
                               preferred_element_type=F32)

        def finish(acc):
            if has_add:
                acc = acc + add_ref[...]
            o_ref[...] = acc.astype(o_ref.dtype)

        if nk == 1:
            finish(part)
        else:
            acc_ref = refs[-1]
            kk = pl.program_id(2)

            @pl.when(kk == 0)
            def _():
                acc_ref[...] = part

            @pl.when(kk > 0)
            def _():
                acc_ref[...] += part

            @pl.when(kk == nk - 1)
            def _():
                finish(acc_ref[...])

    a_blk, a_at = ((tk, tm), lambda i, j, kk: (kk, i)) if ta else ((tm, tk), lambda i, j, kk: (i, kk))
    a_spec = (pl.BlockSpec(a_blk, a_at) if a_index is None else
              pl.BlockSpec((None,) + a_blk, lambda i, j, kk: (a_index,) + a_at(i, j, kk)))
    if b_rows is None:
        b_spec = pl.BlockSpec((tn, tk), lambda i, j, kk: (j, kk)) if tb else pl.BlockSpec((tk, tn), lambda i, j, kk: (kk, j))
    else:
        at = lambda t: pl.multiple_of(b_rows[0] + t, ROW_ALIGN)
        b_spec = (pl.BlockSpec((pl.Element(tn), pl.Element(tk)), lambda i, j, kk: (at(j * tn), kk * tk)) if tb else
                  pl.BlockSpec((pl.Element(tk), pl.Element(tn)), lambda i, j, kk: (at(kk * tk), j * tn)))
    add_spec = pl.BlockSpec((tm, tn), lambda i, j, kk: (i, j))
    assert row_off % tm == 0
    o_spec = pl.BlockSpec((tm, tn), lambda i, j, kk: (i + row_off // tm, j))
    in_specs = [a_spec, b_spec] + ([add_spec] if has_add else []) + ([pl.BlockSpec(memory_space=pl.ANY)] if into is not None else [])
    args = [a, b] + ([add] if has_add else []) + ([into] if into is not None else [])
    return pl.pallas_call(
        body, name=name, grid=(m // tm, n // tn, nk), in_specs=in_specs, out_specs=o_spec,
        out_shape=jax.ShapeDtypeStruct((out_rows or m, n), out_dtype),
        scratch_shapes=[pltpu.VMEM((tm, tn), F32)] if nk > 1 else [],
        input_output_aliases={len(args) - 1: 0} if into is not None else {},
        compiler_params=_cparams("parallel", "parallel", "arbitrary"),
    )(*args)


def _mm_chain(parts, b, row_of_tile, *, add, name, tk=1024, tm=1024):
    m, n = parts[0].shape[-2], b.shape[1]
    tm = min(tm, m)
    tiles = [p.shape[0] if p.ndim == 3 else p.shape[1] // tk for p in parts]
    first = [sum(tiles[:s]) for s in range(len(parts))]
    nk = sum(tiles)

    def body(*refs):
        a_refs, b_ref, add_ref, o_ref, acc_ref = refs[:len(parts)], *refs[len(parts):]
        kk = pl.program_id(1)

        @pl.when(kk == 0)
        def _():
            acc_ref[...] = add_ref[...]

        for a_ref, lo, cnt in zip(a_refs, first, tiles):
            @pl.when(jnp.logical_and(kk >= lo, kk < lo + cnt))
            def _(a_ref=a_ref):
                acc_ref[...] += lax.dot_general(a_ref[...].astype(MXU_DTYPE), b_ref[...].astype(MXU_DTYPE),
                                                (((1,), (0,)), ((), ())), preferred_element_type=F32)

        @pl.when(kk == nk - 1)
        def _():
            o_ref[...] = acc_ref[...]

    tile_of = lambda kk, lo, cnt: jnp.clip(kk - lo, 0, cnt - 1)
    a_specs = [pl.BlockSpec((None, tm, tk), functools.partial(lambda i, kk, lo, cnt: (tile_of(kk, lo, cnt), i, 0), lo=lo, cnt=cnt))
               if p.ndim == 3 else
               pl.BlockSpec((tm, tk), functools.partial(lambda i, kk, lo, cnt: (i, tile_of(kk, lo, cnt)), lo=lo, cnt=cnt))
               for p, lo, cnt in zip(parts, first, tiles)]
    b_spec = pl.BlockSpec((pl.Element(tk), pl.Element(n)), lambda i, kk: (pl.multiple_of(row_of_tile(kk), ROW_ALIGN), 0))
    o_spec = pl.BlockSpec((tm, n), lambda i, kk: (i, 0))
    return pl.pallas_call(
        body, name=name, grid=(m // tm, nk), in_specs=a_specs + [b_spec, o_spec], out_specs=o_spec,
        out_shape=jax.ShapeDtypeStruct((m, n), F32), scratch_shapes=[pltpu.VMEM((tm, n), F32)],
        compiler_params=_cparams("parallel", "arbitrary"),
    )(*parts, b, add)


def _swiglu_tiles(m, half):
    tn = _tile(half, (512, 256, 128))
    return _tile(m, (2048 if tn <= 256 else 1024, 1024, 512, 256, 128)), tn


def _ffn_in_swiglu(h, wt, half, name):
    m, k = h.shape
    tm, tn = _swiglu_tiles(m, half)
    nj = half // tn
    dims = (((1,), (1,)), ((), ()))

    def body(h_ref, wa_ref, wb_ref, act_ref, a_ref, b_ref):
        lhs = h_ref[...].astype(MXU_DTYPE)
        a = lax.dot_general(lhs, wa_ref[...].astype(MXU_DTYPE), dims, preferred_element_type=F32)
        b = lax.dot_general(lhs, wb_ref[...].astype(MXU_DTYPE), dims, preferred_element_type=F32)
        act_ref[...] = (_silu(a) * b).astype(act_ref.dtype)
        a_ref[...] = a.astype(a_ref.dtype)
        b_ref[...] = b.astype(b_ref.dtype)

    out = jax.ShapeDtypeStruct((m, half), MXU_DTYPE)
    oblk = pl.BlockSpec((tm, tn), lambda i, j: (i, j))
    return pl.pallas_call(
        body, name=name, grid=(m // tm, nj),
        in_specs=[pl.BlockSpec((tm, k), lambda i, j: (i, 0)), pl.BlockSpec((tn, k), lambda i, j: (j, 0)),
                  pl.BlockSpec((tn, k), lambda i, j: (j + nj, 0))],
        out_specs=[oblk, oblk, oblk], out_shape=[out, out, out], compiler_params=_cparams("parallel", "parallel"),
    )(h, wt, wt)


def _ffn_out_bwd_swiglu(dff, w, a, b, name):
    m, k = dff.shape
    half = w.shape[0]
    tm, tn = _swiglu_tiles(m, half)

    def body(d_ref, w_ref, a_ref, b_ref, da_ref, db_ref):
        dact = lax.dot_general(d_ref[...].astype(MXU_DTYPE), w_ref[...].astype(MXU_DTYPE), (((1,), (1,)), ((), ())),
                               preferred_element_type=F32)
        av, bv = a_ref[...].astype(F32), b_ref[...].astype(F32)
        sig = jax.nn.sigmoid(av)
        da_ref[...] = (dact * bv * (sig * (1.0 + av * (1.0 - sig)))).astype(da_ref.dtype)
        db_ref[...] = (dact * (av * sig)).astype(db_ref.dtype)

    out = jax.ShapeDtypeStruct((m, half), MXU_DTYPE)
    oblk = pl.BlockSpec((tm, tn), lambda i, j: (i, j))
    return pl.pallas_call(
        body, name=name, grid=(m // tm, half // tn),
        in_specs=[pl.BlockSpec((tm, k), lambda i, j: (i, 0)), pl.BlockSpec((tn, k), lambda i, j: (j, 0)), oblk, oblk],
        out_specs=[oblk, oblk], out_shape=[out, out], compiler_params=_cparams("parallel", "parallel"),
    )(dff, w, a, b)


def _with_off(xs):
    return [x if isinstance(x, tuple) else (x, 0) for x in xs]


def _spec(kind, arr, off, ts, wb):
    w = arr.shape[-1] if wb is None else wb
    col = (lambda j: 0) if wb is None else functools.partial(lambda j, o: o + j, o=off)
    if kind == "tok":
        return pl.BlockSpec((None, ts, w), lambda j, b, i: (b, i, col(j)))
    if kind == "bat":
        return pl.BlockSpec((None, 1, w), lambda j, b, i: (b, 0, col(j)))
    if off is None:
        return pl.BlockSpec(arr.shape, lambda j, b, i: (0, 0))
    return pl.BlockSpec((arr.shape[0], w), lambda j, b, i: (0, col(j)))


class _Product:
    def __init__(self, a, b, *, tb=False, b_rows=None, add=None):
        self.a, self.b, self.tb, self.b_rows, self.add = a, b, tb, b_rows, add
        rows = b.shape[0] if b_rows is None else b_rows[1]
        self.shape = a.shape[:2] + (rows if tb else b.shape[1],)

    def inputs(self, ts):
        a_spec = pl.BlockSpec((None, ts, self.a.shape[2]), lambda j, b, i: (b, i, 0))
        if self.b_rows is None:
            b_spec = pl.BlockSpec(self.b.shape, lambda j, b, i: (0, 0))
        else:
            start, count = self.b_rows
            b_spec = pl.BlockSpec((pl.Element(count), pl.Element(self.b.shape[1])), lambda j, b, i: (start, 0))
        extra = [] if self.add is None else [(self.add, pl.BlockSpec((None, ts, self.shape[2]), lambda j, b, i: (b, i, 0)))]
        return [(self.a, a_spec), (self.b, b_spec)] + extra

    def value(self, refs):
        dims = (((1,), (1 if self.tb else 0,)), ((), ()))
        val = lax.dot_general(refs[0][...].astype(MXU_DTYPE), refs[1][...].astype(MXU_DTYPE), dims, preferred_element_type=F32)
        return val if self.add is None else val + refs[2][...].astype(F32)


def _inputs(groups, kinds, ts, wb):
    loaded = [(a, _spec(kind, a, o, ts, wb)) for g, kind in zip(groups, kinds) for a, o in g if not isinstance(a, _Product)]
    made = [pair for g in groups for a, _ in g if isinstance(a, _Product) for pair in a.inputs(ts)]
    return [a for a, _ in loaded + made], [sp for _, sp in loaded + made]


def _values(refs, groups):
    n_loaded = sum(1 for g in groups for a, _ in g if not isinstance(a, _Product))
    loaded, pos, out = iter(refs[:n_loaded]), n_loaded, []
    for g in groups:
        vals = []
        for a, _ in g:
            if isinstance(a, _Product):
                k = 2 if a.add is None else 3
                vals.append(a.value(refs[pos:pos + k]))
                pos += k
            else:
                vals.append(next(loaded)[...].astype(F32))
        out.append(vals)
    return out, pos


def _tok_fwd(fn, toks, bats, pars, outs, *, name, ts, wb=None, cols=1):
    groups = [_with_off(toks), _with_off(bats), _with_off(pars)]
    bl, s, _ = groups[0][0][0].shape
    ts = min(ts, s)
    args, in_specs = _inputs(groups, ("tok", "bat", "par"), ts, wb)

    def body(*refs):
        vals, n_in = _values(refs, groups)
        res = fn(*[v for g in vals for v in g])
        for r, val in zip(refs[n_in:], res):
            r[...] = val.astype(r.dtype)

    out_specs = [pl.BlockSpec((None, ts, w if wb is None else wb), lambda j, b, i: (b, i, j)) for w, _ in outs]
    return pl.pallas_call(
        body, name=name, grid=(cols, bl, s // ts), in_specs=in_specs,
        out_specs=out_specs, out_shape=[jax.ShapeDtypeStruct((bl, s, w), dt) for w, dt in outs],
        compiler_params=_cparams("parallel", "parallel", "parallel"),
    )(*args)


def _accumulate(ref, val, first):
    @pl.when(first)
    def _():
        ref[...] = val

    @pl.when(jnp.logical_not(first))
    def _():
        ref[...] += val


def _tok_bwd(fn, toks, bats, pars, cots, need, *, name, ts, wb=None, cols=1, tok_dtype=F32, loss=False, after=()):
    toks, bats, pars, cots = _with_off(toks), _with_off(bats), _with_off(pars), _with_off(cots)
    groups = [toks, bats, pars, cots]
    bl, s, _ = toks[0][0].shape
    ts = min(ts, s)
    nt, nb, npar = len(toks), len(bats), len(pars)
    args, in_specs = _inputs(groups, ("tok", "bat", "par", "tok"), ts, wb)
    args, in_specs = args + list(after), in_specs + [pl.BlockSpec(memory_space=pl.ANY)] * len(after)

    def body(*refs):
        j, b, i = pl.program_id(0), pl.program_id(1), pl.program_id(2)
        (tok_vals, bat_vals, par_vals, cot_vals), o = _values(refs, groups)
        o += len(after)
        outs, vjp = jax.vjp(fn, *tok_vals, *bat_vals, *par_vals)
        if loss:
            ct = (jnp.ones_like(outs[0]),)
            tot = jnp.broadcast_to(jnp.sum(outs[0], keepdims=True), (1, LANE))
            _accumulate(refs[o], tot, jnp.logical_and(b == 0, i == 0))
            o += 1
        else:
            ct = tuple(cot_vals)
        grads = vjp(ct)
        for t in range(nt):
            if need[t]:
                refs[o][...] = grads[t].astype(refs[o].dtype)
                o += 1
        for t in range(nb):
            _accumulate(refs[o], grads[nt + t], i == 0)
            o += 1
        for t in range(npar):
            first = jnp.logical_and(b == 0, i == 0)
            if pars[t][1] is None:
                first = jnp.logical_and(first, j == 0)
            _accumulate(refs[o], grads[nt + nb + t], first)
            o += 1

    full = lambda arr: arr.shape[-1] if wb is None else wb * cols
    blk = lambda arr: arr.shape[-1] if wb is None else wb
    out_specs, out_shape = [], []
    if loss:
        out_specs.append(pl.BlockSpec((1, LANE), lambda j, b, i: (0, 0)))
        out_shape.append(jax.ShapeDtypeStruct((1, LANE), F32))
    for t in range(nt):
        if need[t]:
            out_specs.append(pl.BlockSpec((None, ts, blk(toks[t][0])), lambda j, b, i: (b, i, j)))
            dt = tok_dtype[t] if isinstance(tok_dtype, (list, tuple)) else tok_dtype
            out_shape.append(jax.ShapeDtypeStruct((bl, s, full(toks[t][0])), dt))
    for arr, _ in bats:
        out_specs.append(pl.BlockSpec((None, 1, blk(arr)), lambda j, b, i: (b, 0, j)))
        out_shape.append(jax.ShapeDtypeStruct((bl, 1, full(arr)), F32))
    for arr, off in pars:
        if off is None:
            out_specs.append(pl.BlockSpec(arr.shape, lambda j, b, i: (0, 0)))
            out_shape.append(jax.ShapeDtypeStruct(arr.shape, F32))
        else:
            out_specs.append(pl.BlockSpec((arr.shape[0], blk(arr)), lambda j, b, i: (0, j)))
            out_shape.append(jax.ShapeDtypeStruct((arr.shape[0], full(arr)), F32))
    res = list(pl.pallas_call(
        body, name=name, grid=(cols, bl, s // ts), in_specs=in_specs,
        out_specs=out_specs, out_shape=out_shape, compiler_params=_cparams("arbitrary", "arbitrary", "arbitrary"),
    )(*args))
    tot = res.pop(0) if loss else None
    dtoks = [res.pop(0) if need[t] else None for t in range(nt)]
    dbats = [res.pop(0) for _ in range(nb)]
    dpars = [res.pop(0) for _ in range(npar)]
    return (tot, dtoks, dbats, dpars) if loss else (dtoks, dbats, dpars)


def _silu(x):
    return x * jax.nn.sigmoid(x)


def _rms(x, w):
    return x * lax.rsqrt(jnp.mean(x * x, axis=-1, keepdims=True) + EPS) * w


def _f_norm_mod(x, shift, scale, w):
    return (_rms(x, w) * (1.0 + scale) + shift,)


def _f_norm_mod_skip(x, shift, scale, w):
    return _rms(x, w) * (1.0 + scale) + shift, x


def _f_res_norm_mod(x, mix, gate, shift, scale, w):
    x2 = x + gate * mix
    return x2, _rms(x2, w) * (1.0 + scale) + shift


def _f_res_norm_mod_keep(x, mix, gate, shift, scale, w):
    return (*_f_res_norm_mod(x, mix, gate, shift, scale, w), mix)


def _f_gates(p, a_log, dt_bias, *, heads):
    z = p + dt_bias
    g = -jnp.exp(a_log) * (jnp.maximum(z, 0.0) + jnp.log1p(jnp.exp(jnp.minimum(z, -z))))
    lane = lax.broadcasted_iota(jnp.int32, p.shape, 1)
    return (jnp.where(lane < heads, g, jax.nn.sigmoid(p)),)


def _f_gdn_out(o, z, w):
    return (_rms(o, w) * _silu(z),)


def _f_merge(ga, gb, ya, yb):
    return (jax.nn.sigmoid(ga) * ya + jax.nn.sigmoid(gb) * yb,)


def _f_merge_keep(ga, gb, ya, yb):
    return (*_f_merge(ga, gb, ya, yb), yb)


def _f_loss(x2, ff, tgt, gate, shift, scale, w):
    y = _rms(x2 + gate * ff, w) * (1.0 + scale) + shift
    return (0.5 * jnp.mean(jnp.square(y - tgt), axis=-1, keepdims=True),)


def _shift_down(x, s):
    if s == 0:
        return x
    row = lax.broadcasted_iota(jnp.int32, x.shape, 0)
    return jnp.where(row >= s, pltpu.roll(x, s, 0), 0.0)


def _shift_up(x, s):
    if s == 0:
        return x
    n = x.shape[0]
    row = lax.broadcasted_iota(jnp.int32, x.shape, 0)
    return jnp.where(row < n - s, pltpu.roll(x, n - s, 0), 0.0)


def _conv(x, w):
    width = w.shape[0]
    acc = w[width - 1:width, :] * x
    for j in range(width - 1):
        acc = acc + w[j:j + 1, :] * _shift_down(x, width - 1 - j)
    return acc


def _conv_bwd(dy, x, w, dw_ref, first):
    width = w.shape[0]
    dx = w[width - 1:width, :] * dy
    for j in range(width - 1):
        dx = dx + w[j:j + 1, :] * _shift_up(dy, width - 1 - j)
    for j in range(width):
        row = jnp.sum(dy * _shift_down(x, width - 1 - j), axis=0, keepdims=True)
        _accumulate(dw_ref.at[j:j + 1, :], row, first)
    return dx


def _qkv_act(xc, is_v, scale):
    a = _silu(xc)
    nrm = a * lax.rsqrt(jnp.sum(a * a, axis=-1, keepdims=True) + EPS) * scale
    return jnp.where(is_v, a, nrm)


def _qkv_consts(j, heads):
    is_v = j >= 2 * heads
    scale = jnp.where(j < heads, HEAD ** -0.5, 1.0).astype(F32)
    return is_v, scale


def _qkv_fwd(p, w, heads, name):
    bl, s, w3 = p.shape

    def body(p_ref, w_ref, o_ref):
        is_v, scale = _qkv_consts(pl.program_id(0), heads)
        o_ref[...] = _qkv_act(_conv(p_ref[...], w_ref[...]), is_v, scale)

    blk = pl.BlockSpec((None, s, HEAD), lambda j, b: (b, 0, j))
    return pl.pallas_call(
        body, name=name, grid=(w3 // HEAD, bl), in_specs=[blk, pl.BlockSpec((w.shape[0], HEAD), lambda j, b: (0, j))],
        out_specs=blk, out_shape=jax.ShapeDtypeStruct(p.shape, F32), compiler_params=_cparams("parallel", "parallel"),
    )(p, w)


def _qkv_bwd(p, w, dout, heads, name):
    bl, s, w3 = p.shape

    def body(p_ref, w_ref, d_ref, dp_ref, dw_ref):
        is_v, scale = _qkv_consts(pl.program_id(0), heads)
        x, wv = p_ref[...], w_ref[...]
        _, vjp = jax.vjp(lambda xc: _qkv_act(xc, is_v, scale), _conv(x, wv))
        (dxc,) = vjp(d_ref[...])
        dp_ref[...] = _conv_bwd(dxc, x, wv, dw_ref, pl.program_id(1) == 0).astype(dp_ref.dtype)

    blk = pl.BlockSpec((None, s, HEAD), lambda j, b: (b, 0, j))
    wblk = pl.BlockSpec((w.shape[0], HEAD), lambda j, b: (0, j))
    return pl.pallas_call(
        body, name=name, grid=(w3 // HEAD, bl), in_specs=[blk, wblk, blk], out_specs=[blk, wblk],
        out_shape=[jax.ShapeDtypeStruct(p.shape, MXU_DTYPE), jax.ShapeDtypeStruct(w.shape, F32)],
        compiler_params=_cparams("arbitrary", "arbitrary"),
    )(p, w, dout)


def _sc_specs(p, w):
    bl, s, w3 = p.shape
    nblk = w3 // 3 // LANE
    sec = lambda k: pl.BlockSpec((None, s, LANE), functools.partial(lambda j, b, k: (b, 0, k * nblk + j), k=k))
    return nblk, [sec(0), sec(1), sec(2)], pl.BlockSpec((w.shape[0], LANE), lambda j, b: (0, j)), \
        pl.BlockSpec((None, s, LANE), lambda j, b: (b, 0, j))


def _sc_fwd(p, w, name):
    bl, s, w3 = p.shape
    nblk, secs, wblk, oblk = _sc_specs(p, w)

    def body(b_ref, c_ref, x_ref, w_ref, o_ref):
        o_ref[...] = (b_ref[...] * _conv(c_ref[...] * x_ref[...], w_ref[...])).astype(o_ref.dtype)

    return pl.pallas_call(
        body, name=name, grid=(nblk, bl), in_specs=secs + [wblk], out_specs=oblk,
        out_shape=jax.ShapeDtypeStruct((bl, s, w3 // 3), MXU_DTYPE), compiler_params=_cparams("parallel", "parallel"),
    )(p, p, p, w)


def _sc_bwd(p, w, dout, name):
    bl, s, w3 = p.shape
    nblk, secs, wblk, oblk = _sc_specs(p, w)

    def body(b_ref, c_ref, x_ref, w_ref, d_ref, dp_ref, dw_ref):
        gb, gc, xin, wv, d = b_ref[...], c_ref[...], x_ref[...], w_ref[...], d_ref[...]
        u = gc * xin
        dp_ref[0] = (d * _conv(u, wv)).astype(dp_ref.dtype)
        du = _conv_bwd(d * gb, u, wv, dw_ref, pl.program_id(1) == 0)
        dp_ref[1] = (du * xin).astype(dp_ref.dtype)
        dp_ref[2] = (du * gc).astype(dp_ref.dtype)

    return pl.pallas_call(
        body, name=name, grid=(nblk, bl), in_specs=secs + [wblk, oblk],
        out_specs=[pl.BlockSpec((3, None, s, LANE), lambda j, b: (0, b, 0, j)), wblk],
        out_shape=[jax.ShapeDtypeStruct((3, bl, s, w3 // 3), MXU_DTYPE), jax.ShapeDtypeStruct(w.shape, F32)],
        compiler_params=_cparams("arbitrary", "arbitrary"),
    )(p, p, p, w, dout)


def _bdot(a, b, ca, cb):
    return lax.dot_general(a.astype(MXU_DTYPE), b.astype(MXU_DTYPE), (((ca,), (cb,)), ((), ())),
                           preferred_element_type=F32)


def _hdot(a, b):
    return lax.dot_general(a, b, (((1,), (0,)), ((), ())), precision=HIGHEST, preferred_element_type=F32)


def _lane_col(x, idx):
    lane = lax.broadcasted_iota(jnp.int32, x.shape, 1)
    return jnp.sum(jnp.where(lane == idx, x, 0.0), axis=1, keepdims=True)


def _chunk_masks():
    r = lax.broadcasted_iota(jnp.int32, (CHUNK, CHUNK), 0)
    c = lax.broadcasted_iota(jnp.int32, (CHUNK, CHUNK), 1)
    return r == c, r >= c, r > c


def _dot3(a, b):
    ah, bh = a.astype(MXU_DTYPE), b.astype(MXU_DTYPE)
    al, bl = (a - ah.astype(F32)).astype(MXU_DTYPE), (b - bh.astype(F32)).astype(MXU_DTYPE)
    dot = lambda x, y: lax.dot_general(x, y, (((1,), (0,)), ((), ())), preferred_element_type=F32)
    return dot(ah, bh) + (dot(ah, bl) + dot(al, bh))


def _tri_inv_steps(low, eye):
    x = -low
    p = jnp.where(eye, 1.0, 0.0) + x
    span = 2
    while span < CHUNK:
        x = _dot3(x, x)
        yield
        p = p + _dot3(p, x)
        yield
        span *= 2
    return p


def _round_robin(gens):
    out, live = [None] * len(gens), list(range(len(gens)))
    while live:
        still = []
        for i in live:
            try:
                next(gens[i])
                still.append(i)
            except StopIteration as stop:
                out[i] = stop.value
        live = still
    return out


def _gdn_pre(q, k, v, gc, beta, masks):
    eye, causal, strict = masks
    gc_row = jnp.sum(jnp.where(eye, gc, 0.0), axis=0, keepdims=True)
    decay = jnp.where(causal, jnp.exp(jnp.where(causal, gc - gc_row, 0.0)), 0.0)
    eg = jnp.exp(gc)
    gl = gc[CHUNK - 1:CHUNK, :]
    kb, vb = k * beta, v * beta
    both = _bdot(jnp.concatenate([kb, q], axis=0), k, 1, 1)
    low = jnp.where(strict, both[:CHUNK] * decay, 0.0)
    qk = jnp.where(causal, both[CHUNK:] * decay, 0.0)
    rest = jnp.exp(gl - gc)
    return dict(decay=decay, eg=eg, gl=gl, kb=kb, vb=vb, kbe=kb * eg, low=low, qk=qk, qg=q * eg, rest=rest, kdec=k * rest)


GROUP = 4


def _gdn_specs(qkv, gbeta, heads, rev):
    bl, s, w3 = qkv.shape
    d, n = w3 // 3, s // CHUNK
    group = GROUP if n % GROUP == 0 else 1
    steps = n // group
    at = (lambda c: steps - 1 - c) if rev else (lambda c: c)
    assert d == heads * HEAD
    rows = group * CHUNK
    sec = pl.BlockSpec((None, rows, w3), lambda b, c: (b, at(c), 0))
    gspec = pl.BlockSpec((None, rows, LANE), lambda b, c: (b, at(c), 0))
    ospec = pl.BlockSpec((None, rows, d), lambda b, c: (b, at(c), 0))
    sspec = pl.BlockSpec((None, group, heads, HEAD, HEAD), lambda b, c: (b, at(c), 0, 0, 0))
    tspec = pl.BlockSpec((None, group, heads, CHUNK, CHUNK), lambda b, c: (b, at(c), 0, 0, 0))
    return bl, s, d, n, group, sec, gspec, ospec, sspec, tspec


def _gdn_fwd(qkv, gbeta, heads, name):
    bl, s, d, n, group, sec, gspec, ospec, sspec, tspec = _gdn_specs(qkv, gbeta, heads, False)
    rows = lambda sub: slice(sub * CHUNK, (sub + 1) * CHUNK)
    pairs = [(h, sub) for h in range(heads) for sub in range(group)]

    def body(x_ref, g_ref, o_ref, s_ref, t_ref, st_ref):
        @pl.when(pl.program_id(1) == 0)
        def _():
            st_ref[...] = jnp.zeros_like(st_ref)

        masks = _chunk_masks()
        eye, causal, _ = masks
        gblks = [g_ref[rows(sub), :] for sub in range(group)]
        gcs = [_hdot(jnp.where(causal, 1.0, 0.0), gb) for gb in gblks]
        st_all = st_ref[...]

        def free(h, sub):
            q, k, v = (x_ref[rows(sub), sec * d + h * HEAD:sec * d + (h + 1) * HEAD] for sec in range(3))
            pre = _gdn_pre(q, k, v, _lane_col(gcs[sub], h), _lane_col(gblks[sub], heads + h), masks)
            yield
            t = yield from _tri_inv_steps(pre["low"], eye)
            uw = _bdot(t, jnp.concatenate([pre["vb"], pre["kbe"]], axis=1), 1, 0)
            return pre, t, uw[:, :HEAD], uw[:, HEAD:]

        pieces = dict(zip(pairs, _round_robin([free(h, sub) for h, sub in pairs])))

        def carry(h):
            st, outs, starts = st_all[h], [], []
            for sub in range(group):
                pre, _, u, w = pieces[h, sub]
                starts.append(st)
                vnew = u - _bdot(w, st, 1, 0)
                yield
                outs.append(_bdot(pre["qg"], st, 1, 0) + _bdot(pre["qk"], vnew, 1, 0))
                st = st * jnp.exp(pre["gl"]) + _bdot(pre["kdec"], vnew, 0, 0)
                yield
            return outs, starts, st

        carried = _round_robin([carry(h) for h in range(heads)])
        per_sub = lambda pick: [[pick(h, sub) for h in range(heads)] for sub in range(group)]
        o_ref[...] = jnp.concatenate([jnp.concatenate(r, axis=1) for r in per_sub(lambda h, sub: carried[h][0][sub])], axis=0)
        s_ref[...] = jnp.stack([jnp.stack(r) for r in per_sub(lambda h, sub: carried[h][1][sub])])
        t_ref[...] = jnp.stack([jnp.stack(r) for r in per_sub(lambda h, sub: pieces[h, sub][1])])
        st_ref[...] = jnp.stack([carried[h][2] for h in range(heads)])

    return pl.pallas_call(
        body, name=name, grid=(bl, n // group), in_specs=[sec, gspec], out_specs=[ospec, sspec, tspec],
        out_shape=[jax.ShapeDtypeStruct((bl, s, d), F32), jax.ShapeDtypeStruct((bl, n, heads, HEAD, HEAD), F32),
                   jax.ShapeDtypeStruct((bl, n, heads, CHUNK, CHUNK), F32)],
        scratch_shapes=[pltpu.VMEM((heads, HEAD, HEAD), F32)], compiler_params=_cparams("parallel", "arbitrary"),
    )(qkv, gbeta)


def _gdn_bwd(qkv, gbeta, dout, s_all, t_all, heads, name):
    bl, s, d, n, group, sec, gspec, ospec, sspec, tspec = _gdn_specs(qkv, gbeta, heads, True)
    rows = lambda sub: slice(sub * CHUNK, (sub + 1) * CHUNK)
    pairs = [(h, sub) for h in range(heads) for sub in range(group)]
    stack, side = functools.partial(jnp.concatenate, axis=0), functools.partial(jnp.concatenate, axis=1)

    def body(x_ref, g_ref, do_ref, s_ref, t_ref, dx_ref, dg_ref, ds_ref):
        @pl.when(pl.program_id(1) == 0)
        def _():
            ds_ref[...] = jnp.zeros_like(ds_ref)

        masks = _chunk_masks()
        eye, causal, strict = masks
        gblks = [g_ref[rows(sub), :] for sub in range(group)]
        gcs = [_hdot(jnp.where(causal, 1.0, 0.0), gb) for gb in gblks]
        lane = lax.broadcasted_iota(jnp.int32, (CHUNK, LANE), 1)
        last_row = lax.broadcasted_iota(jnp.int32, (CHUNK, 1), 0) == CHUNK - 1
        rowsum = lambda a: jnp.sum(a, axis=1, keepdims=True)
        st_all, t_all_, ds_all = s_ref[...], t_ref[...], ds_ref[...]

        def free(h, sub):
            q, k, v = (x_ref[rows(sub), sec * d + h * HEAD:sec * d + (h + 1) * HEAD] for sec in range(3))
            do = do_ref[rows(sub), h * HEAD:(h + 1) * HEAD]
            beta = _lane_col(gblks[sub], heads + h)
            st, t = st_all[sub, h], t_all_[sub, h]
            pre = _gdn_pre(q, k, v, _lane_col(gcs[sub], h), beta, masks)
            yield
            uw = _bdot(t, side([pre["vb"], pre["kbe"]]), 1, 0)
            u, w = uw[:, :HEAD], uw[:, HEAD:]
            yield
            vnew = u - _bdot(w, st, 1, 0)
            yield
            dqk = jnp.where(causal, _bdot(do, vnew, 1, 1), 0.0)
            dqg = _bdot(do, st, 1, 1)
            return dict(q=q, k=k, v=v, do=do, beta=beta, st=st, t=t, pre=pre, w=w, vnew=vnew, dqk=dqk, dqg=dqg)

        pieces = dict(zip(pairs, _round_robin([free(h, sub) for h, sub in pairs])))

        def carry(h):
            dsn, outs = ds_all[h], {}
            for sub in reversed(range(group)):
                pc = pieces[h, sub]
                pre, st, do = pc["pre"], pc["st"], pc["do"]
                egl = jnp.exp(pre["gl"])
                dkdec = _bdot(pc["vnew"], dsn, 1, 1)
                dvnew = _bdot(pre["kdec"], dsn, 1, 0) + _bdot(pre["qk"], do, 0, 0)
                dgl = jnp.sum(dsn * st, keepdims=True) * egl
                yield
                dw = -_bdot(dvnew, st, 1, 1)
                dsn = dsn * egl + _bdot(stack([pre["qg"], -pc["w"]]), stack([do, dvnew]), 0, 0)
                outs[sub] = (dkdec, dvnew, dgl, dw)
                yield
            return outs, dsn

        carried = _round_robin([carry(h) for h in range(heads)])

        def rest(h, sub):
            pc = pieces[h, sub]
            dkdec, dvnew, dgl, dw = carried[h][0][sub]
            q, k, v, beta, t, pre, dqk, dqg = (pc[x] for x in ("q", "k", "v", "beta", "t", "pre", "dqk", "dqg"))
            decay, eg, kb, vb, kbe, low, qk, qg, kdec = (pre[x] for x in ("decay", "eg", "kb", "vb", "kbe", "low", "qk", "qg", "kdec"))
            dt = _bdot(side([dvnew, dw]), side([vb, kbe]), 1, 1)
            by_t = _bdot(t, side([dvnew, dw]), 0, 0)
            dvb, dkbe = by_t[:, :HEAD], by_t[:, HEAD:]
            yield
            inner = _bdot(dt, t, 1, 1)
            yield
            dlow = -jnp.where(strict, _bdot(t, inner, 0, 0), 0.0)
            da, db = dlow * decay, dqk * decay
            yield
            m = dlow * low + dqk * qk
            kdk = dkdec * kdec
            col_of_m = jnp.sum(jnp.where(eye, jnp.sum(m, axis=0, keepdims=True), 0.0), axis=1, keepdims=True)
            dgc = rowsum(m) - col_of_m + rowsum(dqg * qg) + rowsum(dkbe * kbe) - rowsum(kdk)
            dgc = dgc + jnp.where(last_row, dgl + jnp.sum(kdk, keepdims=True), 0.0)
            by_k = _bdot(stack([da, db]), k, 1, 0)
            dkb = by_k[:CHUNK] + dkbe * eg
            yield
            dk = _bdot(stack([da, db]), stack([kb, q]), 0, 0) + dkdec * pre["rest"] + dkb * beta
            dq = by_k[CHUNK:] + dqg * eg
            dbeta = rowsum(dkb * k) + rowsum(dvb * v)
            return dq, dk, dvb * beta, jnp.where(lane == h, dgc, 0.0) + jnp.where(lane == heads + h, dbeta, 0.0)

        done = dict(zip(pairs, _round_robin([rest(h, sub) for h, sub in pairs])))
        dx_ref[...] = stack([side([done[h, sub][i] for i in range(3) for h in range(heads)]) for sub in range(group)])
        ds_ref[...] = jnp.stack([carried[h][1] for h in range(heads)])
        upper = jnp.where(jnp.logical_or(eye, jnp.logical_not(causal)), 1.0, 0.0)
        dgs = []
        for sub in range(group):
            dgb = done[0, sub][3]
            for h in range(1, heads):
                dgb = dgb + done[h, sub][3]
            dgs.append(jnp.where(lane < heads, _hdot(upper, dgb), dgb))
        dg_ref[...] = stack(dgs)

    return pl.pallas_call(
        body, name=name, grid=(bl, n // group), in_specs=[sec, gspec, ospec, sspec, tspec], out_specs=[sec, gspec],
        out_shape=[jax.ShapeDtypeStruct(qkv.shape, F32), jax.ShapeDtypeStruct((bl, s, LANE), F32)],
        scratch_shapes=[pltpu.VMEM((heads, HEAD, HEAD), F32)], compiler_params=_cparams("parallel", "arbitrary"),
    )(qkv, gbeta, dout, s_all, t_all)


def _position():
    return lax.axis_index("x"), lax.axis_index("y"), lax.axis_index("c")


def _all_gather(x, *, name):
    space = pltpu.VMEM

    def body(x_ref, out_ref, send_sems, recv_sems, local_sem):
        ax, ay, ac = _position()
        me, sibling = (ax, ay, ac), (ax, ay, 1 - ac)
        chips = [(1 - ax, ay), (ax, 1 - ay), (1 - ax, 1 - ay)]

        def slot(px, py, pc):
            return out_ref.at[4 * px + 2 * py + pc]

        def copy(k, block, to, src=None):
            return pltpu.make_async_remote_copy(
                src_ref=slot(*block) if src is None else src, dst_ref=slot(*block), send_sem=send_sems.at[k],
                recv_sem=recv_sems.at[k], device_id=to, device_id_type=MESH_IDS)

        mine = pltpu.make_async_copy(x_ref, slot(*me), local_sem)
        mine.start()
        first = [copy(0, me, sibling, src=x_ref)] + [copy(1 + j, me, (*chip, ac), src=x_ref) for j, chip in enumerate(chips)]
        for cp in first:
            cp.start()
        passed = [copy(4 + j, (*chip, ac), sibling) for j, chip in enumerate(chips)]
        for j, chip in enumerate(chips):
            copy(1 + j, (*chip, ac), me).wait_recv()
            passed[j].start()
        copy(0, sibling, me).wait_recv()
        for j, chip in enumerate(chips):
            copy(4 + j, (*chip, 1 - ac), me).wait_recv()
        for cp in first + passed:
            cp.wait_send()
        mine.wait()

    return pl.pallas_call(
        body, name=name, out_shape=jax.ShapeDtypeStruct((NDEV,) + x.shape, x.dtype),
        in_specs=[pl.BlockSpec(memory_space=space)], out_specs=pl.BlockSpec(memory_space=space),
        scratch_shapes=[pltpu.SemaphoreType.DMA((7,)), pltpu.SemaphoreType.DMA((7,)), pltpu.SemaphoreType.DMA],
    )(x)


class _Rider:
    def __init__(self, arrays, out_shapes, sems, hooks):
        self.arrays, self.out_shapes, self.sems, self.hooks = arrays, out_shapes, sems, hooks


def _gather_rider(xs):
    n = len(xs)

    def hooks(x_refs, out_refs, send_sems, recv_sems):
        ax, ay, ac = _position()
        me, sibling = (ax, ay, ac), (ax, ay, 1 - ac)
        chips = [(1 - ax, ay), (ax, 1 - ay), (1 - ax, 1 - ay)]

        def copies(k, block, to, own=False):
            out = []
            for i in range(n):
                slot = out_refs[i].at[4 * block[0] + 2 * block[1] + block[2]]
                out.append(pltpu.make_async_remote_copy(
                    src_ref=x_refs[i] if own else slot, dst_ref=slot, send_sem=send_sems.at[k, i], recv_sem=recv_sems.at[k, i],
                    device_id=to, device_id_type=MESH_IDS))
            return out

        def first():
            for cp in copies(0, me, sibling, own=True):
                cp.start()
            for j, chip in enumerate(chips):
                for cp in copies(1 + j, me, (*chip, ac), own=True):
                    cp.start()

        def mid():
            for j, chip in enumerate(chips):
                for arrived, onward in zip(copies(1 + j, (*chip, ac), me), copies(4 + j, (*chip, ac), sibling)):
                    arrived.wait_recv()
                    onward.start()

        def last():
            for cp in copies(0, sibling, me):
                cp.wait_recv()
            for j, chip in enumerate(chips):
                for cp in copies(4 + j, (*chip, 1 - ac), me):
                    cp.wait_recv()
            for cp in copies(0, me, sibling, own=True):
                cp.wait_send()
            for j, chip in enumerate(chips):
                for cp in copies(1 + j, me, (*chip, ac), own=True) + copies(4 + j, (*chip, ac), sibling):
                    cp.wait_send()

        return first, mid, last

    return _Rider(list(xs), [jax.ShapeDtypeStruct((NDEV,) + x.shape, x.dtype) for x in xs],
                  [pltpu.SemaphoreType.DMA((7, n)), pltpu.SemaphoreType.DMA((7, n))], hooks)


def _scatter_rider(parts):
    packed = sum(r for _, r in parts)
    width, dtype = parts[0][0].shape[1], parts[0][0].dtype

    def hooks(g_refs, out_refs, send_sems, recv_sems):
        (recv_ref,) = out_refs
        ax, ay, ac = _position()

        def peer(rel):
            flip = lambda a, bit: 1 - a if rel & bit else a
            return flip(ax, 4), flip(ay, 2), flip(ac, 1)

        def first():
            for rel in range(1, NDEV):
                px, py, pc = peer(rel)
                off = 0
                for g_ref, (_, r) in zip(g_refs, parts):
                    rows = g_ref.at[pl.ds(pl.multiple_of((4 * px + 2 * py + pc) * r, ROW_ALIGN), r)]
                    pltpu.make_async_remote_copy(
                        src_ref=rows, dst_ref=recv_ref.at[rel - 1, pl.ds(off, r)], send_sem=send_sems.at[rel - 1],
                        recv_sem=recv_sems.at[rel - 1], device_id=(px, py, pc), device_id_type=MESH_IDS).start()
                    off += r

        def last():
            for rel in range(1, NDEV):
                slot = recv_ref.at[rel - 1]
                pltpu.make_async_remote_copy(src_ref=slot, dst_ref=slot, send_sem=send_sems.at[rel - 1],
                                             recv_sem=recv_sems.at[rel - 1], device_id=peer(rel), device_id_type=MESH_IDS).wait()

        return first, lambda: None, last

    return _Rider([g for g, _ in parts], [jax.ShapeDtypeStruct((NDEV - 1, packed, width), dtype)],
                  [pltpu.SemaphoreType.DMA((NDEV - 1,)), pltpu.SemaphoreType.DMA((NDEV - 1,))], hooks)


def _sum_direct(own, recv, name):
    r, w = own.shape
    tr = max(t for t in range(ROW_ALIGN, 257, ROW_ALIGN) if r % t == 0)

    def body(own_ref, *refs):
        acc = own_ref[...].astype(F32)
        for ref in refs[:-1]:
            acc = acc + ref[...].astype(F32)
        refs[-1][...] = acc

    rblk = lambda k: pl.BlockSpec((None, tr, w), functools.partial(lambda i, k: (k, i, 0), k=k))
    blk = pl.BlockSpec((tr, w), lambda i: (i, 0))
    return pl.pallas_call(body, name=name, grid=(r // tr,), in_specs=[blk] + [rblk(k) for k in range(NDEV - 1)],
                          out_specs=blk, out_shape=jax.ShapeDtypeStruct((r, w), F32),
                          compiler_params=_cparams("parallel"))(own, *([recv] * (NDEV - 1)))


ROW_ALIGN = 16


def _window_start(rows_per_dev, k):
    return rows_per_dev * k // ROW_ALIGN * ROW_ALIGN


def _exchange_in_chip(parts, name, collective_id):
    packed = sum(win for _, _, win, _ in parts)
    width, dtype = parts[0][0].shape[1], parts[0][0].dtype

    def body(g_refs, out_refs, send_sems, recv_sems):
        (recv_ref,) = out_refs
        ax, ay, ac = _position()
        sibling = (ax, ay, 1 - ac)
        _handshake([sibling])
        for q in range(4):
            for g_ref, (_, r, win, off) in zip(g_refs, parts):
                there = g_ref.at[pl.ds(pl.multiple_of(_window_start(r, 2 * q + 1 - ac), ROW_ALIGN), win)]
                pltpu.make_async_remote_copy(src_ref=there, dst_ref=recv_ref.at[q, pl.ds(off, win)], send_sem=send_sems.at[q],
                                             recv_sem=recv_sems.at[q], device_id=sibling, device_id_type=MESH_IDS).start()
        for q in range(4):
            pltpu.make_async_remote_copy(src_ref=recv_ref.at[q], dst_ref=recv_ref.at[q], send_sem=send_sems.at[q],
                                         recv_sem=recv_sems.at[q], device_id=sibling, device_id_type=MESH_IDS).wait()

    return _on_sequencer(body, [g for g, _, _, _ in parts], [jax.ShapeDtypeStruct((4, packed, width), dtype)],
                         [pltpu.SemaphoreType.DMA((4,)), pltpu.SemaphoreType.DMA((4,))], name=name, collective_id=collective_id)[0]


def _on_sequencer(body, ins, out_shapes, sems, *, name, collective_id):
    hbm = pltpu.MemorySpace.HBM
    in_refs = [jax.new_ref(a, memory_space=hbm) for a in ins]
    out_refs = [jax.empty_ref(s, memory_space=hbm) for s in out_shapes]

    @pl.kernel(mesh=plsc.ScalarSubcoreMesh(axis_name="sequencer", num_cores=1), name=name, scratch_types=tuple(sems),
               compiler_params=pltpu.CompilerParams(collective_id=collective_id))
    def launch(*sem_refs):
        body(in_refs, out_refs, *sem_refs)

    launch()
    return [r[...] for r in out_refs]


def _handshake(peers):
    barrier = pltpu.get_barrier_semaphore()
    for peer in peers:
        pl.semaphore_signal(barrier, inc=1, device_id=peer, device_id_type=MESH_IDS)
    pl.semaphore_wait(barrier, len(peers))


def _exchange_chips_async(s1, name, collective_id):
    def body(in_refs, out_refs, send_sems, recv_sems):
        (src,), (got,) = in_refs, out_refs
        ax, ay, ac = _position()
        chips = [(1 - ax, ay), (ax, 1 - ay), (1 - ax, 1 - ay)]
        _handshake([(cx, cy, ac) for cx, cy in chips])
        copies = [pltpu.make_async_remote_copy(
            src_ref=src.at[2 * cx + cy], dst_ref=got.at[r], send_sem=send_sems.at[r], recv_sem=recv_sems.at[r],
            device_id=(cx, cy, ac), device_id_type=MESH_IDS) for r, (cx, cy) in enumerate(chips)]
        for cp in copies:
            cp.start()
        for cp in copies:
            cp.wait_recv()
        for cp in copies:
            cp.wait_send()

    return _on_sequencer(body, [s1], [jax.ShapeDtypeStruct((3,) + s1.shape[1:], s1.dtype)],
                         [pltpu.SemaphoreType.DMA((3,)), pltpu.SemaphoreType.DMA((3,))], name=name, collective_id=collective_id)[0]


def _gather_async(xs, name, collective_id):
    rider = _gather_rider(xs)

    def body(in_refs, out_refs, send_sems, recv_sems):
        ax, ay, ac = _position()
        _handshake([(ax, ay, 1 - ac), (1 - ax, ay, ac), (ax, 1 - ay, ac), (1 - ax, 1 - ay, ac)])
        for hook in rider.hooks(in_refs, out_refs, send_sems, recv_sems):
            hook()

    return _on_sequencer(body, rider.arrays, rider.out_shapes, rider.sems, name=name, collective_id=collective_id)


def _scatter_async(parts, name, collective_id):
    rider = _scatter_rider(parts)

    def body(in_refs, out_refs, send_sems, recv_sems):
        ax, ay, ac = _position()
        flip = lambda a, on: 1 - a if on else a
        _handshake([(flip(ax, rel & 4), flip(ay, rel & 2), flip(ac, rel & 1)) for rel in range(1, NDEV)])
        for hook in rider.hooks(in_refs, out_refs, send_sems, recv_sems):
            hook()

    return _on_sequencer(body, rider.arrays, rider.out_shapes, rider.sems, name=name, collective_id=collective_id)[0]


def _sum_in_chip(own, recv, name):
    _, r, w = own.shape
    tr = _tile(r, (256, 128))

    def body(a_ref, b_ref, o_ref):
        o_ref[...] = (a_ref[...].astype(F32) + b_ref[...].astype(F32)).astype(o_ref.dtype)

    blk = pl.BlockSpec((None, tr, w), lambda q, i: (q, i, 0))
    return pl.pallas_call(body, name=name, grid=(4, r // tr), in_specs=[blk, blk], out_specs=blk,
                          out_shape=jax.ShapeDtypeStruct(own.shape, own.dtype),
                          compiler_params=_cparams("parallel", "parallel"))(own, recv)


def _sum_chips(s1, recv, chip, name):
    _, r, w = s1.shape
    tr = _tile(r, (256, 128))

    def body(c_ref, s_ref, r0_ref, r1_ref, r2_ref, o_ref):
        f = lambda ref: ref[...].astype(F32)
        o_ref[...] = ((f(s_ref) + f(r0_ref)) + f(r1_ref)) + f(r2_ref)

    rblk = lambda k: pl.BlockSpec((None, tr, w), functools.partial(lambda i, c, k: (k, i, 0), k=k))
    grid_spec = pltpu.PrefetchScalarGridSpec(
        num_scalar_prefetch=1, grid=(r // tr,),
        in_specs=[pl.BlockSpec((None, tr, w), lambda i, c: (c[0], i, 0)), rblk(0), rblk(1), rblk(2)],
        out_specs=pl.BlockSpec((tr, w), lambda i, c: (i, 0)))
    return pl.pallas_call(body, name=name, grid_spec=grid_spec, out_shape=jax.ShapeDtypeStruct((r, w), F32),
                          compiler_params=_cparams("parallel"))(chip, s1, recv, recv, recv)


def _silu_rows(x, name):
    def body(x_ref, o_ref):
        o_ref[...] = _silu(x_ref[...])

    return pl.pallas_call(body, name=name, out_shape=jax.ShapeDtypeStruct(x.shape, F32))(x)


def _row_sum(x, name):
    def body(x_ref, o_ref):
        acc = x_ref[0:1, :]
        for i in range(1, x.shape[0]):
            acc = acc + x_ref[i:i + 1, :]
        o_ref[...] = acc

    return pl.pallas_call(body, name=name, out_shape=jax.ShapeDtypeStruct((1, x.shape[1]), F32))(x)


def _adamw(w, g, m, v, name):
    cols = w.shape[-1]
    rows = w.size // cols
    tr = _tile(rows, (128,))
    tc = LANE if (tr == rows and rows > 512 and cols % LANE == 0) else cols

    def body(w_ref, g_ref, m_ref, v_ref, d_ref, mo_ref, vo_ref):
        grad = g_ref[...]
        m_new = ADAM_B1 * m_ref[...] + (1.0 - ADAM_B1) * grad
        v_new = ADAM_B2 * v_ref[...] + (1.0 - ADAM_B2) * jnp.square(grad)
        m_hat = m_new / (1.0 - ADAM_B1 ** ADAM_STEP)
        v_hat = v_new / (1.0 - ADAM_B2 ** ADAM_STEP)
        d_ref[...] = -ADAM_LR * (m_hat / (jnp.sqrt(v_hat) + ADAM_EPS) + ADAM_WD * w_ref[...])
        mo_ref[...] = m_new
        vo_ref[...] = v_new

    blk = pl.BlockSpec((tr, tc), lambda i, j: (i, j))
    out = pl.pallas_call(
        body, name=name, grid=(rows // tr, cols // tc), in_specs=[blk] * 4, out_specs=[blk] * 3,
        out_shape=[jax.ShapeDtypeStruct((rows, cols), F32)] * 3, compiler_params=_cparams("parallel", "parallel"),
    )(*[t.reshape(rows, cols) for t in (w, g, m, v)])
    return [t.reshape(w.shape) for t in out]


def _pack(parts, width, row_mult, dtype):
    flat = jnp.concatenate([p.reshape(-1).astype(dtype) for p in parts])
    rows = -(-flat.shape[0] // (width * row_mult)) * row_mult
    return jnp.pad(flat, (0, rows * width - flat.shape[0])).reshape(rows, width)


def _unpack(flat, shapes):
    out, off = [], 0
    for shp in shapes:
        size = 1
        for dim in shp:
            size *= dim
        out.append(flat[:, off:off + size].reshape((flat.shape[0],) + tuple(shp)))
        off += size
    return out


def _devices_to_cols(a):
    _, r, c = a.shape
    return a.transpose(1, 0, 2).reshape(r, NDEV * c)


def kernel(x, c, w_ada, b_ada, norm1_w, w_in, gdn_conv_w, gdn_a_log, gdn_dt_bias, gdn_norm_w, w_gdn_proj, sc_conv_w, w_sc_out, w_o, norm2_w, w_ffn_in, w_ffn_out, w_ada_f, b_ada_f, normf_w, loss_target, m_w_ada, m_b_ada, m_norm1_w, m_w_in, m_gdn_conv_w, m_gdn_a_log, m_gdn_dt_bias, m_gdn_norm_w, m_w_gdn_proj, m_sc_conv_w, m_w_sc_out, m_w_o, m_norm2_w, m_w_ffn_in, m_w_ffn_out, m_w_ada_f, m_b_ada_f, m_normf_w, v_w_ada, v_b_ada, v_norm1_w, v_w_in, v_gdn_conv_w, v_gdn_a_log, v_gdn_dt_bias, v_gdn_norm_w, v_w_gdn_proj, v_sc_conv_w, v_w_sc_out, v_w_o, v_norm2_w, v_w_ffn_in, v_w_ffn_out, v_w_ada_f, v_b_ada_f, v_normf_w):
    bl, s, d = x.shape
    heads = gdn_a_log.shape[-1]
    dff = w_ffn_out.shape[1] * NDEV
    tok = bl * s
    ax, ay, ac = _position()
    dev = 4 * ax + 2 * ay + ac
    as_tok = lambda a: a.reshape(bl, s, a.shape[-1])
    as_mat = lambda a: a.reshape(tok, a.shape[-1])

    small = _all_gather(_pack([c, gdn_conv_w, sc_conv_w], LANE, 8, F32), name="gather_cond")
    c_all, conv_w, sc_w = _unpack(small.reshape(NDEV, -1), [(bl, d), gdn_conv_w.shape[1:], sc_conv_w.shape[1:]])
    c_act = _silu_rows(c_all.reshape(NDEV * bl, d), "cond_silu")
    conv_w, sc_w = _devices_to_cols(conv_w), _devices_to_cols(sc_w)
    n_ada, n_adaf = w_ada.shape[-1], w_ada_f.shape[-1]
    bias = jnp.broadcast_to(lax.dynamic_slice_in_dim(b_ada, dev * n_ada, n_ada, axis=1), (NDEV * bl, n_ada))
    biasf = jnp.broadcast_to(lax.dynamic_slice_in_dim(b_ada_f.reshape(1, -1), dev * n_adaf, n_adaf, axis=1), (NDEV * bl, n_adaf))
    mod_cols = _mm(c_act, w_ada[0], add=bias, name="ada_cols")
    modf_cols = _mm(c_act, w_ada_f, add=biasf, name="adaf_cols")
    mods = _all_gather(jnp.concatenate([mod_cols, modf_cols], axis=1), name="gather_mod")
    mod_all = mods[:, :, :n_ada].transpose(1, 0, 2).reshape(NDEV * bl, NDEV * n_ada)
    modf_all = mods[:, :, n_ada:].transpose(1, 0, 2).reshape(NDEV * bl, NDEV * n_adaf)
    my_rows = lambda a: lax.dynamic_slice_in_dim(a, dev * bl, bl, axis=0)
    sh1, sc1, g1, sh2, sc2, g2 = [t.reshape(bl, 1, d) for t in jnp.split(my_rows(mod_all), 6, axis=1)]
    shf, scf = [t.reshape(bl, 1, d) for t in jnp.split(my_rows(modf_all), 2, axis=1)]

    late = [t.astype(MXU_DTYPE) for t in (w_gdn_proj[0], w_sc_out[0], w_o[0], w_ffn_in[0].T, w_ffn_out[0])]
    rows = [t.shape[0] for t in late] + [w_in.shape[-1]]
    offs = [sum(rows[:i]) for i in range(5)]
    in_send = w_in[0].T.astype(MXU_DTYPE)
    with_own = lambda g, own: lax.dynamic_update_slice_in_dim(g, own[None], dev, axis=0)
    (wt_in,) = _gather_async([in_send], "gather_w_in", 1)
    wt_in = with_own(wt_in, in_send).reshape(NDEV * rows[5], d)
    gathered = _gather_async(late[:3], "gather_mixer", 2) + _gather_async(late[3:], "gather_ffn", 3)
    wgp, wso, wo, wt_fi, wfo = [with_own(g, own).reshape(NDEV * own.shape[0], d) for g, own in zip(gathered, late)]
    o_z, o_ab, o_sc, o_ga, o_gb = 3 * d, 4 * d, 4 * d + 2 * heads, 7 * d + 2 * heads, 8 * d + 2 * heads
    s_qkv, s_z, s_sc, s_gate = (0, o_z), (o_z, d), (o_sc, 3 * d), (o_ga, 2 * d)
    wt_ab = jnp.pad(wt_in[o_ab:o_sc], ((0, LANE - 2 * heads), (0, 0)))

    n1w, n2w, nfw = norm1_w.reshape(1, d), norm2_w.reshape(1, d), normf_w.reshape(1, d)
    lanes = lambda a: jnp.pad(a.reshape(1, -1), ((0, 0), (0, LANE - a.size)))
    a_log, dt_bias, gnw = lanes(gdn_a_log), lanes(gdn_dt_bias), gdn_norm_w.reshape(1, HEAD)
    f_gates = functools.partial(_f_gates, heads=heads)
    (h1,) = _tok_fwd(_f_norm_mod, [x], [sh1, sc1], [n1w], [(d, MXU_DTYPE)], name="norm1", ts=512)
    h1m = as_mat(h1)
    p_qkv = as_tok(_mm(h1m, wt_in, tb=True, b_rows=s_qkv, name="in_qkv"))
    p_z = as_tok(_mm(h1m, wt_in, tb=True, b_rows=s_z, name="in_z"))
    p_ab = as_tok(_mm(h1m, wt_ab, tb=True, name="in_ab"))
    p_sc = as_tok(_mm(h1m, wt_in, tb=True, b_rows=s_sc, name="in_sc"))
    p_g = as_tok(_mm(h1m, wt_in, tb=True, b_rows=s_gate, name="in_gate"))
    qkv = _qkv_fwd(p_qkv, conv_w, heads, "qkv_conv")
    (gbeta,) = _tok_fwd(f_gates, [p_ab], [], [a_log, dt_bias], [(LANE, F32)], name="gates", ts=512)
    o, s_all, t_all = _gdn_fwd(qkv, gbeta, heads, "gdn")
    (og,) = _tok_fwd(_f_gdn_out, [o, p_z], [], [(gnw, None)], [(d, MXU_DTYPE)], name="gdn_out", ts=2048, wb=HEAD, cols=heads)
    y_a = as_tok(_mm(as_mat(og), wgp, name="gdn_proj"))
    scp = _sc_fwd(p_sc, sc_w, "sc_conv")
    mrg, y_b = _tok_fwd(_f_merge_keep, [(p_g, 0), (p_g, 1), y_a, _Product(scp, wso)], [], [], [(d, MXU_DTYPE), (d, F32)],
                        name="merge", ts=256, wb=d)
    merge_toks = [(p_g, 0), (p_g, 1), y_a, y_b]
    x2, h2, mix = _tok_fwd(_f_res_norm_mod_keep, [x, _Product(mrg, wo)], [g1, sh2, sc2], [n2w],
                           [(d, F32), (d, MXU_DTYPE), (d, F32)], name="norm2", ts=512)
    act, gu_a, gu_b = _ffn_in_swiglu(as_mat(h2), wt_fi, dff, "ffn_in")

    loss_l, (dx2, dff_out, _), (dg2, dshf, dscf), (dnfw,) = _tok_bwd(
        _f_loss, [x2, _Product(as_tok(act), wfo), loss_target], [g2, shf, scf], [nfw], [], [True, True, False], name="loss",
        ts=256, loss=True, tok_dtype=[F32, MXU_DTYPE, None])
    dffm = as_mat(dff_out)
    dgu_a, dgu_b = _ffn_out_bwd_swiglu(dffm, wfo, gu_a, gu_b, "d_ffn_out")
    gmm = functools.partial(_mm, ta=True, out_dtype=MXU_DTYPE)
    gw_ffn_out = gmm(act, dffm, name="g_ffn_out")
    dh2 = _Product(as_tok(dgu_b), wt_fi, b_rows=(dff, dff), add=as_tok(_mm(dgu_a, wt_fi, b_rows=(0, dff), name="d_ffn_in_a")))
    h2m = as_mat(h2)
    gwt_ffn_in = gmm(dgu_a, h2m, out_rows=2 * dff, name="g_ffn_in_a")
    gwt_ffn_in = gmm(dgu_b, h2m, out_rows=2 * dff, row_off=dff, into=gwt_ffn_in, name="g_ffn_in_b")
    ffn_parts = [(gwt_ffn_in, rows[3]), (gw_ffn_out, rows[4])]
    ffn_recv = _scatter_async(ffn_parts, "scatter_ffn", 4)
    (dx_skip, dmix), (dg1, dsh2, dsc2), (dn2w,) = _tok_bwd(
        _f_res_norm_mod, [x, mix], [g1, sh2, sc2], [n2w], [dx2, dh2], [True, True], name="d_norm2", ts=256,
        tok_dtype=[F32, MXU_DTYPE], after=[gwt_ffn_in, gw_ffn_out])
    gw_o = gmm(as_mat(mrg), as_mat(dmix), name="g_mix_out")
    (dga, dgb, dya, dyb), _, _ = _tok_bwd(_f_merge, merge_toks, [], [], [_Product(dmix, wo, tb=True)], [True] * 4,
                                          name="d_merge", ts=256, wb=d, tok_dtype=MXU_DTYPE)
    dyam, dybm = as_mat(dya), as_mat(dyb)
    dog = as_tok(_mm(dyam, wgp, tb=True, name="d_gdn_proj"))
    gw_gdn_proj = gmm(as_mat(og), dyam, name="g_gdn_proj")
    dscp = as_tok(_mm(dybm, wso, tb=True, name="d_sc_out"))
    gw_sc_out = gmm(as_mat(scp), dybm, name="g_sc_out")
    dsc, g_sc_w = _sc_bwd(p_sc, sc_w, dscp, "d_sc_conv")
    mix_parts = [(gw_gdn_proj, rows[0]), (gw_sc_out, rows[1]), (gw_o, rows[2])]
    mix_recv = _scatter_async(mix_parts, "scatter_mixer", 5)
    (do, dz), _, (g_gnw,) = _tok_bwd(_f_gdn_out, [o, p_z], [], [(gnw, None)], [dog], [True, True], name="d_gdn_out",
                                     ts=2048, wb=HEAD, cols=heads, tok_dtype=[F32, MXU_DTYPE],
                                     after=[gw_gdn_proj, gw_sc_out, gw_o])
    own_rows = lambda parts: jnp.concatenate([lax.dynamic_slice_in_dim(g, dev * r, r, axis=0) for g, r in parts], axis=0)
    dqkv, dgbeta = _gdn_bwd(qkv, gbeta, do, s_all, t_all, heads, "d_gdn")
    dp_qkv, g_conv_w = _qkv_bwd(p_qkv, conv_w, dqkv, heads, "d_qkv_conv")
    ffn_red = _sum_direct(own_rows(ffn_parts), ffn_recv, "sum_ffn")
    mix_red = _sum_direct(own_rows(mix_parts), mix_recv, "sum_mix")
    (dp_ab,), _, (g_a_log, g_dt_bias) = _tok_bwd(f_gates, [p_ab], [], [a_log, dt_bias], [dgbeta], [True], name="d_gates",
                                                 ts=512, tok_dtype=MXU_DTYPE, after=[ffn_red, mix_red])
    r_in = rows[5]
    win = -(-(r_in + max(r_in * k % ROW_ALIGN for k in range(NDEV))) // 128) * 128
    need_rows = max(_window_start(r_in, k) for k in range(NDEV)) + win
    dsc_m = dsc.reshape(3, tok, d)
    gwt_in = ([gmm(as_mat(dp_qkv), h1m, name="g_in_qkv"), gmm(as_mat(dz), h1m, name="g_in_z"),
               gmm(as_mat(dp_ab), h1m, name="g_in_ab")[:2 * heads]]
              + [gmm(dsc_m, h1m, a_index=k, name=f"g_in_sc{k}") for k in range(3)]
              + [gmm(as_mat(dga), h1m, name="g_in_ga"), gmm(as_mat(dgb), h1m, name="g_in_gb")])
    gwt_in = jnp.concatenate(gwt_in + [jnp.zeros((need_rows - NDEV * r_in, d), MXU_DTYPE)], axis=0)
    assert d <= 1024
    wide = [as_mat(dp_qkv), as_mat(dz), dsc_m, as_mat(dga)]
    row_of = lambda t: d * t + jnp.where(t * d >= o_ab, 2 * heads, 0)
    recv1 = _exchange_in_chip([(gwt_in, r_in, win, 0)], "scatter_in_chip", 7)
    own = jnp.stack([lax.dynamic_slice_in_dim(gwt_in, _window_start(r_in, 2 * q + ac), win, axis=0) for q in range(4)])
    s1 = _sum_in_chip(own, recv1, "sum_in_chip")
    recv2 = _exchange_chips_async(s1, "scatter_chips", 6)

    dh1 = _mm(as_mat(dp_ab), wt_ab, name="d_in_ab")
    dh1 = _mm_chain(wide, wt_in, row_of, add=dh1, name="d_in", tk=d)
    dh1 = _Product(dgb, wt_in, b_rows=(o_gb, d), add=as_tok(dh1))
    (grad_x,), (dsh1, dsc1), (dn1w,) = _tok_bwd(_f_norm_mod_skip, [x], [sh1, sc1], [n1w], [dh1, dx_skip], [True],
                                                name="d_norm1", ts=256)
    reduced = _sum_chips(s1, recv2, (2 * ax + ay).reshape(1).astype(jnp.int32), "sum_chips")
    gt_w_in = lax.dynamic_slice_in_dim(reduced, r_in * dev - _window_start(r_in, dev), r_in, axis=0)
    g_w_in = gt_w_in.T.reshape(w_in.shape)
    gt_w_ffn_in = ffn_red[:rows[3]]
    g_w_ffn_in = gt_w_ffn_in.T.reshape(w_ffn_in.shape)
    g_w_ffn_out = ffn_red[rows[3]:].reshape(w_ffn_out.shape)
    g_w_gdn_proj, g_w_sc_out, g_w_o = (mix_red[offs[i]:offs[i] + rows[i]].reshape(ref.shape)
                                       for i, ref in enumerate((w_gdn_proj, w_sc_out, w_o)))

    dmod = jnp.concatenate([t.reshape(bl, d) for t in (dsh1, dsc1, dg1, dsh2, dsc2, dg2)], axis=1)
    dmodf = jnp.concatenate([t.reshape(bl, d) for t in (dshf, dscf)], axis=1)
    summed_parts = [dn1w, dn2w, dnfw, g_gnw, g_a_log, g_dt_bias, g_conv_w, g_sc_w, loss_l]
    partial = _all_gather(_pack([dmod, dmodf] + summed_parts, LANE, 8, F32), name="gather_small")
    partial = partial.reshape(NDEV, -1)
    n_rows = bl * (6 * d + 2 * d)
    dmod_all, dmodf_all = _unpack(partial[:, :n_rows], [(bl, 6 * d), (bl, 2 * d)])
    dmod_all, dmodf_all = dmod_all.reshape(NDEV * bl, 6 * d), dmodf_all.reshape(NDEV * bl, 2 * d)
    totals = _row_sum(partial[:, n_rows:], "sum_small")
    t_n1w, t_n2w, t_nfw, t_gnw, t_a_log, t_dt_bias, t_conv_w, t_sc_w, t_loss = [
        t[0] for t in _unpack(totals, [p.shape for p in summed_parts])]
    my_cols = lambda a, n: lax.dynamic_slice_in_dim(a, dev * n, n, axis=1)
    grads = {
        "w_ada": _mm(c_act, my_cols(dmod_all, n_ada), ta=True, name="g_ada").reshape(w_ada.shape),
        "b_ada": _row_sum(dmod_all, "g_ada_bias").reshape(b_ada.shape),
        "norm1_w": t_n1w.reshape(norm1_w.shape),
        "w_in": g_w_in,
        "gdn_conv_w": my_cols(t_conv_w, gdn_conv_w.shape[-1]).reshape(gdn_conv_w.shape),
        "gdn_a_log": t_a_log[:, :heads].reshape(gdn_a_log.shape),
        "gdn_dt_bias": t_dt_bias[:, :heads].reshape(gdn_dt_bias.shape),
        "gdn_norm_w": t_gnw.reshape(gdn_norm_w.shape),
        "w_gdn_proj": g_w_gdn_proj,
        "sc_conv_w": my_cols(t_sc_w, sc_conv_w.shape[-1]).reshape(sc_conv_w.shape),
        "w_sc_out": g_w_sc_out,
        "w_o": g_w_o,
        "norm2_w": t_n2w.reshape(norm2_w.shape),
        "w_ffn_in": g_w_ffn_in,
        "w_ffn_out": g_w_ffn_out,
        "w_ada_f": _mm(c_act, my_cols(dmodf_all, n_adaf), ta=True, name="g_adaf").reshape(w_ada_f.shape),
        "b_ada_f": _row_sum(dmodf_all, "g_adaf_bias").reshape(b_ada_f.shape),
        "normf_w": t_nfw.reshape(normf_w.shape),
    }
    weights = dict(w_ada=w_ada, b_ada=b_ada, norm1_w=norm1_w, w_in=w_in, gdn_conv_w=gdn_conv_w, gdn_a_log=gdn_a_log,
                   gdn_dt_bias=gdn_dt_bias, gdn_norm_w=gdn_norm_w, w_gdn_proj=w_gdn_proj, sc_conv_w=sc_conv_w,
                   w_sc_out=w_sc_out, w_o=w_o, norm2_w=norm2_w, w_ffn_in=w_ffn_in, w_ffn_out=w_ffn_out, w_ada_f=w_ada_f,
                   b_ada_f=b_ada_f, normf_w=normf_w)
    m_in = [m_w_ada, m_b_ada, m_norm1_w, m_w_in, m_gdn_conv_w, m_gdn_a_log, m_gdn_dt_bias, m_gdn_norm_w, m_w_gdn_proj,
            m_sc_conv_w, m_w_sc_out, m_w_o, m_norm2_w, m_w_ffn_in, m_w_ffn_out, m_w_ada_f, m_b_ada_f, m_normf_w]
    v_in = [v_w_ada, v_b_ada, v_norm1_w, v_w_in, v_gdn_conv_w, v_gdn_a_log, v_gdn_dt_bias, v_gdn_norm_w, v_w_gdn_proj,
            v_sc_conv_w, v_w_sc_out, v_w_o, v_norm2_w, v_w_ffn_in, v_w_ffn_out, v_w_ada_f, v_b_ada_f, v_normf_w]
    deltas, new_m, new_v = [], [], []
    grads_t = {"w_in": gt_w_in, "w_ffn_in": gt_w_ffn_in}
    for (wname, wt), mt, vt in zip(weights.items(), m_in, v_in):
        if wname in grads_t:
            back = lambda a, wt=wt: a.T.reshape(wt.shape)
            dl, mn, vn = (back(a) for a in _adamw(wt[0].T, grads_t[wname], mt[0].T, vt[0].T, "adamw_" + wname))
        else:
            dl, mn, vn = _adamw(wt, grads[wname], mt, vt, "adamw_" + wname)
        deltas.append(dl)
        new_m.append(mn)
        new_v.append(vn)
    loss = t_loss[0, 0]
    return (loss, grad_x, *[grads[k] for k in weights], *deltas, *new_m, *new_v)
```

```python
import functools

import jax
import jax.numpy as jnp
from jax import lax
from jax.experimental import pallas as pl
from jax.experimental.pallas import tpu as pltpu
from jax.experimental.pallas import tpu_sc as plsc

F32 = jnp.float32
MXU_DTYPE = jnp.bfloat16
NDEV = 8
CHUNK = 64
HEAD = 128
LANE = 128
EPS = 1e-6
ADAM_LR, ADAM_B1, ADAM_B2, ADAM_EPS, ADAM_WD, ADAM_STEP = 0.001, 0.9, 0.999, 1e-08, 0.01, 10
VMEM_LIMIT = 48 * 1024 * 1024
MESH_IDS = pl.DeviceIdType.MESH
HIGHEST = lax.Precision.HIGHEST


def _tile(n, cands=(512, 256, 128)):
    for c in cands:
        if n % c == 0:
            return c
    return n


def _cparams(*sem):
    return pltpu.CompilerParams(dimension_semantics=sem, vmem_limit_bytes=VMEM_LIMIT)


def _mm(a, b, *, ta=False, tb=False, add=None, out_dtype=F32, name, b_rows=None, out_rows=None, row_off=0, into=None,
        a_index=None):
    m, k = (a.shape[-1], a.shape[-2]) if ta else a.shape[-2:]
    b_shape = b.shape if b_rows is None else (b_rows[1], b.shape[1])
    n = b_shape[0] if tb else b_shape[1]
    assert k == (b_shape[1] if tb else b_shape[0])
    if ta:
        tm, tn = _tile(m), n if n <= 1024 else _tile(n)
        tk = k if k <= 4096 else _tile(k, (4096, 2048, 1024, 512))
        if tm * tk > 1024 * 2048:
            tk = _tile(k, (2048, 1024, 512))
    else:
        tk = k if k <= 1024 else _tile(k, (1024, 512))
        tn = _tile(n, (1024 if tk <= 1024 else 512, 512, 256, 128))
        tm = _tile(m, (2048 if (tn <= 512 and tk <= 1024) else 1024, 1024, 512, 256, 128))
        if (m // tm) * (n // tn) * (k // tk) < 8 and tm % 1024 == 0:
            tm //= 2
    nk = k // tk
    dims = (((0 if ta else 1,), (1 if tb else 0,)), ((), ()))
    has_add = add is not None

    def body(*refs):
        a_ref, b_ref = refs[0], refs[1]
        add_ref = refs[2] if has_add else None
        o_ref = refs[2 + has_add + (into is not None)]
        part = lax.dot_general(a_ref[...].astype(MXU_DTYPE), b_ref[...].astype(MXU_DTYPE), dims,
                               preferred_element_type=F32)

        def finish(acc):
            if has_add:
                acc = acc + add_ref[...]
            o_ref[...] = acc.astype(o_ref.dtype)

        if nk == 1:
            finish(part)
        else:
            acc_ref = refs[-1]
            kk = pl.program_id(2)

            @pl.when(kk == 0)
            def _():
                acc_ref[...] = part

            @pl.when(kk > 0)
            def _():
                acc_ref[...] += part

            @pl.when(kk == nk - 1)
            def _():
                finish(acc_ref[...])

    a_blk, a_at = ((tk, tm), lambda i, j, kk: (kk, i)) if ta else ((tm, tk), lambda i, j, kk: (i, kk))
    a_spec = (pl.BlockSpec(a_blk, a_at) if a_index is None else
              pl.BlockSpec((None,) + a_blk, lambda i, j, kk: (a_index,) + a_at(i, j, kk)))
    if b_rows is None:
        b_spec = pl.BlockSpec((tn, tk), lambda i, j, kk: (j, kk)) if tb else pl.BlockSpec((tk, tn), lambda i, j, kk: (kk, j))
    else:
        at = lambda t: pl.multiple_of(b_rows[0] + t, ROW_ALIGN)
        b_spec = (pl.BlockSpec((pl.Element(tn), pl.Element(tk)), lambda i, j, kk: (at(j * tn), kk * tk)) if tb else
                  pl.BlockSpec((pl.Element(tk), pl.Element(tn)), lambda i, j, kk: (at(kk * tk), j * tn)))
    add_spec = pl.BlockSpec((tm, tn), lambda i, j, kk: (i, j))
    assert row_off % tm == 0
    o_spec = pl.BlockSpec((tm, tn), lambda i, j, kk: (i + row_off // tm, j))
    in_specs = [a_spec, b_spec] + ([add_spec] if has_add else []) + ([pl.BlockSpec(memory_space=pl.ANY)] if into is not None else [])
    args = [a, b] + ([add] if has_add else []) + ([into] if into is not None else [])
    return pl.pallas_call(
        body, name=name, grid=(m // tm, n // tn, nk), in_specs=in_specs, out_specs=o_spec,
        out_shape=jax.ShapeDtypeStruct((out_rows or m, n), out_dtype),
        scratch_shapes=[pltpu.VMEM((tm, tn), F32)] if nk > 1 else [],
        input_output_aliases={len(args) - 1: 0} if into is not None else {},
        compiler_params=_cparams("parallel", "parallel", "arbitrary"),
    )(*args)


def _mm_chain(parts, b, row_of_tile, *, add, name, tk=1024, tm=1024):
    m, n = parts[0].shape[-2], b.shape[1]
    tm = min(tm, m)
    tiles = [p.shape[0] if p.ndim == 3 else p.shape[1] // tk for p in parts]
    first = [sum(tiles[:s]) for s in range(len(parts))]
    nk = sum(tiles)

    def body(*refs):
        a_refs, b_ref, add_ref, o_ref, acc_ref = refs[:len(parts)], *refs[len(parts):]
        kk = pl.program_id(1)

        @pl.when(kk == 0)
        def _():
            acc_ref[...] = add_ref[...]

        for a_ref, lo, cnt in zip(a_refs, first, tiles):
            @pl.when(jnp.logical_and(kk >= lo, kk < lo + cnt))
            def _(a_ref=a_ref):
                acc_ref[...] += lax.dot_general(a_ref[...].astype(MXU_DTYPE), b_ref[...].astype(MXU_DTYPE),
                                                (((1,), (0,)), ((), ())), preferred_element_type=F32)

        @pl.when(kk == nk - 1)
        def _():
            o_ref[...] = acc_ref[...]

    tile_of = lambda kk, lo, cnt: jnp.clip(kk - lo, 0, cnt - 1)
    a_specs = [pl.BlockSpec((None, tm, tk), functools.partial(lambda i, kk, lo, cnt: (tile_of(kk, lo, cnt), i, 0), lo=lo, cnt=cnt))
               if p.ndim == 3 else
               pl.BlockSpec((tm, tk), functools.partial(lambda i, kk, lo, cnt: (i, tile_of(kk, lo, cnt)), lo=lo, cnt=cnt))
               for p, lo, cnt in zip(parts, first, tiles)]
    b_spec = pl.BlockSpec((pl.Element(tk), pl.Element(n)), lambda i, kk: (pl.multiple_of(row_of_tile(kk), ROW_ALIGN), 0))
    o_spec = pl.BlockSpec((tm, n), lambda i, kk: (i, 0))
    return pl.pallas_call(
        body, name=name, grid=(m // tm, nk), in_specs=a_specs + [b_spec, o_spec], out_specs=o_spec,
        out_shape=jax.ShapeDtypeStruct((m, n), F32), scratch_shapes=[pltpu.VMEM((tm, n), F32)],
        compiler_params=_cparams("parallel", "arbitrary"),
    )(*parts, b, add)


def _swiglu_tiles(m, half):
    tn = _tile(half, (512, 256, 128))
    return _tile(m, (2048 if tn <= 256 else 1024, 1024, 512, 256, 128)), tn


def _ffn_in_swiglu(h, wt, half, name):
    m, k = h.shape
    tm, tn = _swiglu_tiles(m, half)
    nj = half // tn
    dims = (((1,), (1,)), ((), ()))

    def body(h_ref, wa_ref, wb_ref, act_ref, a_ref, b_ref):
        lhs = h_ref[...].astype(MXU_DTYPE)
        a = lax.dot_general(lhs, wa_ref[...].astype(MXU_DTYPE), dims, preferred_element_type=F32)
        b = lax.dot_general(lhs, wb_ref[...].astype(MXU_DTYPE), dims, preferred_element_type=F32)
        act_ref[...] = (_silu(a) * b).astype(act_ref.dtype)
        a_ref[...] = a.astype(a_ref.dtype)
        b_ref[...] = b.astype(b_ref.dtype)

    out = jax.ShapeDtypeStruct((m, half), MXU_DTYPE)
    oblk = pl.BlockSpec((tm, tn), lambda i, j: (i, j))
    return pl.pallas_call(
        body, name=name, grid=(m // tm, nj),
        in_specs=[pl.BlockSpec((tm, k), lambda i, j: (i, 0)), pl.BlockSpec((tn, k), lambda i, j: (j, 0)),
                  pl.BlockSpec((tn, k), lambda i, j: (j + nj, 0))],
        out_specs=[oblk, oblk, oblk], out_shape=[out, out, out], compiler_params=_cparams("parallel", "parallel"),
    )(h, wt, wt)


def _ffn_out_bwd_swiglu(dff, w, a, b, name):
    m, k = dff.shape
    half = w.shape[0]
    tm, tn = _swiglu_tiles(m, half)

    def body(d_ref, w_ref, a_ref, b_ref, da_ref, db_ref):
        dact = lax.dot_general(d_ref[...].astype(MXU_DTYPE), w_ref[...].astype(MXU_DTYPE), (((1,), (1,)), ((), ())),
                               preferred_element_type=F32)
        av, bv = a_ref[...].astype(F32), b_ref[...].astype(F32)
        sig = jax.nn.sigmoid(av)
        da_ref[...] = (dact * bv * (sig * (1.0 + av * (1.0 - sig)))).astype(da_ref.dtype)
        db_ref[...] = (dact * (av * sig)).astype(db_ref.dtype)

    out = jax.ShapeDtypeStruct((m, half), MXU_DTYPE)
    oblk = pl.BlockSpec((tm, tn), lambda i, j: (i, j))
    return pl.pallas_call(
        body, name=name, grid=(m // tm, half // tn),
        in_specs=[pl.BlockSpec((tm, k), lambda i, j: (i, 0)), pl.BlockSpec((tn, k), lambda i, j: (j, 0)), oblk, oblk],
        out_specs=[oblk, oblk], out_shape=[out, out], compiler_params=_cparams("parallel", "parallel"),
    )(dff, w, a, b)


def _with_off(xs):
    return [x if isinstance(x, tuple) else (x, 0) for x in xs]


def _spec(kind, arr, off, ts, wb):
    w = arr.shape[-1] if wb is None else wb
    col = (lambda j: 0) if wb is None else functools.partial(lambda j, o: o + j, o=off)
    if kind == "tok":
        return pl.BlockSpec((None, ts, w), lambda j, b, i: (b, i, col(j)))
    if kind == "bat":
        return pl.BlockSpec((None, 1, w), lambda j, b, i: (b, 0, col(j)))
    if off is None:
        return pl.BlockSpec(arr.shape, lambda j, b, i: (0, 0))
    return pl.BlockSpec((arr.shape[0], w), lambda j, b, i: (0, col(j)))


class _Product:
    def __init__(self, a, b, *, tb=False, b_rows=None, add=None):
        self.a, self.b, self.tb, self.b_rows, self.add = a, b, tb, b_rows, add
        rows = b.shape[0] if b_rows is None else b_rows[1]
        self.shape = a.shape[:2] + (rows if tb else b.shape[1],)

    def inputs(self, ts):
        a_spec = pl.BlockSpec((None, ts, self.a.shape[2]), lambda j, b, i: (b, i, 0))
        if self.b_rows is None:
            b_spec = pl.BlockSpec(self.b.shape, lambda j, b, i: (0, 0))
        else:
            start, count = self.b_rows
            b_spec = pl.BlockSpec((pl.Element(count), pl.Element(self.b.shape[1])), lambda j, b, i: (start, 0))
        extra = [] if self.add is None else [(self.add, pl.BlockSpec((None, ts, self.shape[2]), lambda j, b, i: (b, i, 0)))]
        return [(self.a, a_spec), (self.b, b_spec)] + extra

    def value(self, refs):
        dims = (((1,), (1 if self.tb else 0,)), ((), ()))
        val = lax.dot_general(refs[0][...].astype(MXU_DTYPE), refs[1][...].astype(MXU_DTYPE), dims, preferred_element_type=F32)
        return val if self.add is None else val + refs[2][...].astype(F32)


def _inputs(groups, kinds, ts, wb):
    loaded = [(a, _spec(kind, a, o, ts, wb)) for g, kind in zip(groups, kinds) for a, o in g if not isinstance(a, _Product)]
    made = [pair for g in groups for a, _ in g if isinstance(a, _Product) for pair in a.inputs(ts)]
    return [a for a, _ in loaded + made], [sp for _, sp in loaded + made]


def _values(refs, groups):
    n_loaded = sum(1 for g in groups for a, _ in g if not isinstance(a, _Product))
    loaded, pos, out = iter(refs[:n_loaded]), n_loaded, []
    for g in groups:
        vals = []
        for a, _ in g:
            if isinstance(a, _Product):
                k = 2 if a.add is None else 3
                vals.append(a.value(refs[pos:pos + k]))
                pos += k
            else:
                vals.append(next(loaded)[...].astype(F32))
        out.append(vals)
    return out, pos


def _tok_fwd(fn, toks, bats, pars, outs, *, name, ts, wb=None, cols=1):
    groups = [_with_off(toks), _with_off(bats), _with_off(pars)]
    bl, s, _ = groups[0][0][0].shape
    ts = min(ts, s)
    args, in_specs = _inputs(groups, ("tok", "bat", "par"), ts, wb)

    def body(*refs):
        vals, n_in = _values(refs, groups)
        res = fn(*[v for g in vals for v in g])
        for r, val in zip(refs[n_in:], res):
            r[...] = val.astype(r.dtype)

    out_specs = [pl.BlockSpec((None, ts, w if wb is None else wb), lambda j, b, i: (b, i, j)) for w, _ in outs]
    return pl.pallas_call(
        body, name=name, grid=(cols, bl, s // ts), in_specs=in_specs,
        out_specs=out_specs, out_shape=[jax.ShapeDtypeStruct((bl, s, w), dt) for w, dt in outs],
        compiler_params=_cparams("parallel", "parallel", "parallel"),
    )(*args)


def _accumulate(ref, val, first):
    @pl.when(first)
    def _():
        ref[...] = val

    @pl.when(jnp.logical_not(first))
    def _():
        ref[...] += val


def _tok_bwd(fn, toks, bats, pars, cots, need, *, name, ts, wb=None, cols=1, tok_dtype=F32, loss=False, after=()):
    toks, bats, pars, cots = _with_off(toks), _with_off(bats), _with_off(pars), _with_off(cots)
    groups = [toks, bats, pars, cots]
    bl, s, _ = toks[0][0].shape
    ts = min(ts, s)
    nt, nb, npar = len(toks), len(bats), len(pars)
    args, in_specs = _inputs(groups, ("tok", "bat", "par", "tok"), ts, wb)
    args, in_specs = args + list(after), in_specs + [pl.BlockSpec(memory_space=pl.ANY)] * len(after)

    def body(*refs):
        j, b, i = pl.program_id(0), pl.program_id(1), pl.program_id(2)
        (tok_vals, bat_vals, par_vals, cot_vals), o = _values(refs, groups)
        o += len(after)
        outs, vjp = jax.vjp(fn, *tok_vals, *bat_vals, *par_vals)
        if loss:
            ct = (jnp.ones_like(outs[0]),)
            tot = jnp.broadcast_to(jnp.sum(outs[0], keepdims=True), (1, LANE))
            _accumulate(refs[o], tot, jnp.logical_and(b == 0, i == 0))
            o += 1
        else:
            ct = tuple(cot_vals)
        grads = vjp(ct)
        for t in range(nt):
            if need[t]:
                refs[o][...] = grads[t].astype(refs[o].dtype)
                o += 1
        for t in range(nb):
            _accumulate(refs[o], grads[nt + t], i == 0)
            o += 1
        for t in range(npar):
            first = jnp.logical_and(b == 0, i == 0)
            if pars[t][1] is None:
                first = jnp.logical_and(first, j == 0)
            _accumulate(refs[o], grads[nt + nb + t], first)
            o += 1

    full = lambda arr: arr.shape[-1] if wb is None else wb * cols
    blk = lambda arr: arr.shape[-1] if wb is None else wb
    out_specs, out_shape = [], []
    if loss:
        out_specs.append(pl.BlockSpec((1, LANE), lambda j, b, i: (0, 0)))
        out_shape.append(jax.ShapeDtypeStruct((1, LANE), F32))
    for t in range(nt):
        if need[t]:
            out_specs.append(pl.BlockSpec((None, ts, blk(toks[t][0])), lambda j, b, i: (b, i, j)))
            dt = tok_dtype[t] if isinstance(tok_dtype, (list, tuple)) else tok_dtype
            out_shape.append(jax.ShapeDtypeStruct((bl, s, full(toks[t][0])), dt))
    for arr, _ in bats:
        out_specs.append(pl.BlockSpec((None, 1, blk(arr)), lambda j, b, i: (b, 0, j)))
        out_shape.append(jax.ShapeDtypeStruct((bl, 1, full(arr)), F32))
    for arr, off in pars:
        if off is None:
            out_specs.append(pl.BlockSpec(arr.shape, lambda j, b, i: (0, 0)))
            out_shape.append(jax.ShapeDtypeStruct(arr.shape, F32))
        else:
            out_specs.append(pl.BlockSpec((arr.shape[0], blk(arr)), lambda j, b, i: (0, j)))
            out_shape.append(jax.ShapeDtypeStruct((arr.shape[0], full(arr)), F32))
    res = list(pl.pallas_call(
        body, name=name, grid=(cols, bl, s // ts), in_specs=in_specs,
        out_specs=out_specs, out_shape=out_shape, compiler_params=_cparams("arbitrary", "arbitrary", "arbitrary"),
    )(*args))
    tot = res.pop(0) if loss else None
    dtoks = [res.pop(0) if need[t] else None for t in range(nt)]
    dbats = [res.pop(0) for _ in range(nb)]
    dpars = [res.pop(0) for _ in range(npar)]
    return (tot, dtoks, dbats, dpars) if loss else (dtoks, dbats, dpars)


def _silu(x):
    return x * jax.nn.sigmoid(x)


def _rms(x, w):
    return x * lax.rsqrt(jnp.mean(x * x, axis=-1, keepdims=True) + EPS) * w


def _f_norm_mod(x, shift, scale, w):
    return (_rms(x, w) * (1.0 + scale) + shift,)


def _f_norm_mod_skip(x, shift, scale, w):
    return _rms(x, w) * (1.0 + scale) + shift, x


def _f_res_norm_mod(x, mix, gate, shift, scale, w):
    x2 = x + gate * mix
    return x2, _rms(x2, w) * (1.0 + scale) + shift


def _f_res_norm_mod_keep(x, mix, gate, shift, scale, w):
    return (*_f_res_norm_mod(x, mix, gate, shift, scale, w), mix)


def _f_gates(p, a_log, dt_bias, *, heads):
    z = p + dt_bias
    g = -jnp.exp(a_log) * (jnp.maximum(z, 0.0) + jnp.log1p(jnp.exp(jnp.minimum(z, -z))))
    lane = lax.broadcasted_iota(jnp.int32, p.shape, 1)
    return (jnp.where(lane < heads, g, jax.nn.sigmoid(p)),)


def _f_gdn_out(o, z, w):
    return (_rms(o, w) * _silu(z),)


def _f_merge(ga, gb, ya, yb):
    return (jax.nn.sigmoid(ga) * ya + jax.nn.sigmoid(gb) * yb,)


def _f_merge_keep(ga, gb, ya, yb):
    return (*_f_merge(ga, gb, ya, yb), yb)


def _f_loss(x2, ff, tgt, gate, shift, scale, w):
    y = _rms(x2 + gate * ff, w) * (1.0 + scale) + shift
    return (0.5 * jnp.mean(jnp.square(y - tgt), axis=-1, keepdims=True),)


def _shift_down(x, s):
    if s == 0:
        return x
    row = lax.broadcasted_iota(jnp.int32, x.shape, 0)
    return jnp.where(row >= s, pltpu.roll(x, s, 0), 0.0)


def _shift_up(x, s):
    if s == 0:
        return x
    n = x.shape[0]
    row = lax.broadcasted_iota(jnp.int32, x.shape, 0)
    return jnp.where(row < n - s, pltpu.roll(x, n - s, 0), 0.0)


def _conv(x, w):
    width = w.shape[0]
    acc = w[width - 1:width, :] * x
    for j in range(width - 1):
        acc = acc + w[j:j + 1, :] * _shift_down(x, width - 1 - j)
    return acc


def _conv_bwd(dy, x, w, dw_ref, first):
    width = w.shape[0]
    dx = w[width - 1:width, :] * dy
    for j in range(width - 1):
        dx = dx + w[j:j + 1, :] * _shift_up(dy, width - 1 - j)
    for j in range(width):
        row = jnp.sum(dy * _shift_down(x, width - 1 - j), axis=0, keepdims=True)
        _accumulate(dw_ref.at[j:j + 1, :], row, first)
    return dx


def _qkv_act(xc, is_v, scale):
    a = _silu(xc)
    nrm = a * lax.rsqrt(jnp.sum(a * a, axis=-1, keepdims=True) + EPS) * scale
    return jnp.where(is_v, a, nrm)


def _qkv_consts(j, heads):
    is_v = j >= 2 * heads
    scale = jnp.where(j < heads, HEAD ** -0.5, 1.0).astype(F32)
    return is_v, scale


def _qkv_fwd(p, w, heads, name):
    bl, s, w3 = p.shape

    def body(p_ref, w_ref, o_ref):
        is_v, scale = _qkv_consts(pl.program_id(0), heads)
        o_ref[...] = _qkv_act(_conv(p_ref[...], w_ref[...]), is_v, scale)

    blk = pl.BlockSpec((None, s, HEAD), lambda j, b: (b, 0, j))
    return pl.pallas_call(
        body, name=name, grid=(w3 // HEAD, bl), in_specs=[blk, pl.BlockSpec((w.shape[0], HEAD), lambda j, b: (0, j))],
        out_specs=blk, out_shape=jax.ShapeDtypeStruct(p.shape, F32), compiler_params=_cparams("parallel", "parallel"),
    )(p, w)


def _qkv_bwd(p, w, dout, heads, name):
    bl, s, w3 = p.shape

    def body(p_ref, w_ref, d_ref, dp_ref, dw_ref):
        is_v, scale = _qkv_consts(pl.program_id(0), heads)
        x, wv = p_ref[...], w_ref[...]
        _, vjp = jax.vjp(lambda xc: _qkv_act(xc, is_v, scale), _conv(x, wv))
        (dxc,) = vjp(d_ref[...])
        dp_ref[...] = _conv_bwd(dxc, x, wv, dw_ref, pl.program_id(1) == 0).astype(dp_ref.dtype)

    blk = pl.BlockSpec((None, s, HEAD), lambda j, b: (b, 0, j))
    wblk = pl.BlockSpec((w.shape[0], HEAD), lambda j, b: (0, j))
    return pl.pallas_call(
        body, name=name, grid=(w3 // HEAD, bl), in_specs=[blk, wblk, blk], out_specs=[blk, wblk],
        out_shape=[jax.ShapeDtypeStruct(p.shape, MXU_DTYPE), jax.ShapeDtypeStruct(w.shape, F32)],
        compiler_params=_cparams("arbitrary", "arbitrary"),
    )(p, w, dout)


def _sc_specs(p, w):
    bl, s, w3 = p.shape
    nblk = w3 // 3 // LANE
    sec = lambda k: pl.BlockSpec((None, s, LANE), functools.partial(lambda j, b, k: (b, 0, k * nblk + j), k=k))
    return nblk, [sec(0), sec(1), sec(2)], pl.BlockSpec((w.shape[0], LANE), lambda j, b: (0, j)), \
        pl.BlockSpec((None, s, LANE), lambda j, b: (b, 0, j))


def _sc_fwd(p, w, name):
    bl, s, w3 = p.shape
    nblk, secs, wblk, oblk = _sc_specs(p, w)

    def body(b_ref, c_ref, x_ref, w_ref, o_ref):
        o_ref[...] = (b_ref[...] * _conv(c_ref[...] * x_ref[...], w_ref[...])).astype(o_ref.dtype)

    return pl.pallas_call(
        body, name=name, grid=(nblk, bl), in_specs=secs + [wblk], out_specs=oblk,
        out_shape=jax.ShapeDtypeStruct((bl, s, w3 // 3), MXU_DTYPE), compiler_params=_cparams("parallel", "parallel"),
    )(p, p, p, w)


def _sc_bwd(p, w, dout, name):
    bl, s, w3 = p.shape
    nblk, secs, wblk, oblk = _sc_specs(p, w)

    def body(b_ref, c_ref, x_ref, w_ref, d_ref, dp_ref, dw_ref):
        gb, gc, xin, wv, d = b_ref[...], c_ref[...], x_ref[...], w_ref[...], d_ref[...]
        u = gc * xin
        dp_ref[0] = (d * _conv(u, wv)).astype(dp_ref.dtype)
        du = _conv_bwd(d * gb, u, wv, dw_ref, pl.program_id(1) == 0)
        dp_ref[1] = (du * xin).astype(dp_ref.dtype)
        dp_ref[2] = (du * gc).astype(dp_ref.dtype)

    return pl.pallas_call(
        body, name=name, grid=(nblk, bl), in_specs=secs + [wblk, oblk],
        out_specs=[pl.BlockSpec((3, None, s, LANE), lambda j, b: (0, b, 0, j)), wblk],
        out_shape=[jax.ShapeDtypeStruct((3, bl, s, w3 // 3), MXU_DTYPE), jax.ShapeDtypeStruct(w.shape, F32)],
        compiler_params=_cparams("arbitrary", "arbitrary"),
    )(p, p, p, w, dout)


def _bdot(a, b, ca, cb):
    return lax.dot_general(a.astype(MXU_DTYPE), b.astype(MXU_DTYPE), (((ca,), (cb,)), ((), ())),
                           preferred_element_type=F32)


def _hdot(a, b):
    return lax.dot_general(a, b, (((1,), (0,)), ((), ())), precision=HIGHEST, preferred_element_type=F32)


def _lane_col(x, idx):
    lane = lax.broadcasted_iota(jnp.int32, x.shape, 1)
    return jnp.sum(jnp.where(lane == idx, x, 0.0), axis=1, keepdims=True)


def _chunk_masks():
    r = lax.broadcasted_iota(jnp.int32, (CHUNK, CHUNK), 0)
    c = lax.broadcasted_iota(jnp.int32, (CHUNK, CHUNK), 1)
    return r == c, r >= c, r > c


def _dot3(a, b):
    ah, bh = a.astype(MXU_DTYPE), b.astype(MXU_DTYPE)
    al, bl = (a - ah.astype(F32)).astype(MXU_DTYPE), (b - bh.astype(F32)).astype(MXU_DTYPE)
    dot = lambda x, y: lax.dot_general(x, y, (((1,), (0,)), ((), ())), preferred_element_type=F32)
    return dot(ah, bh) + (dot(ah, bl) + dot(al, bh))


def _tri_inv_steps(low, eye):
    x = -low
    p = jnp.where(eye, 1.0, 0.0) + x
    span = 2
    while span < CHUNK:
        x = _dot3(x, x)
        yield
        p = p + _dot3(p, x)
        yield
        span *= 2
    return p


def _round_robin(gens):
    out, live = [None] * len(gens), list(range(len(gens)))
    while live:
        still = []
        for i in live:
            try:
                next(gens[i])
                still.append(i)
            except StopIteration as stop:
                out[i] = stop.value
        live = still
    return out


def _gdn_pre(q, k, v, gc, beta, masks):
    eye, causal, strict = masks
    gc_row = jnp.sum(jnp.where(eye, gc, 0.0), axis=0, keepdims=True)
    decay = jnp.where(causal, jnp.exp(jnp.where(causal, gc - gc_row, 0.0)), 0.0)
    eg = jnp.exp(gc)
    gl = gc[CHUNK - 1:CHUNK, :]
    kb, vb = k * beta, v * beta
    both = _bdot(jnp.concatenate([kb, q], axis=0), k, 1, 1)
    low = jnp.where(strict, both[:CHUNK] * decay, 0.0)
    qk = jnp.where(causal, both[CHUNK:] * decay, 0.0)
    rest = jnp.exp(gl - gc)
    return dict(decay=decay, eg=eg, gl=gl, kb=kb, vb=vb, kbe=kb * eg, low=low, qk=qk, qg=q * eg, rest=rest, kdec=k * rest)


GROUP = 4


def _gdn_specs(qkv, gbeta, heads, rev):
    bl, s, w3 = qkv.shape
    d, n = w3 // 3, s // CHUNK
    group = GROUP if n % GROUP == 0 else 1
    steps = n // group
    at = (lambda c: steps - 1 - c) if rev else (lambda c: c)
    assert d == heads * HEAD
    rows = group * CHUNK
    sec = pl.BlockSpec((None, rows, w3), lambda b, c: (b, at(c), 0))
    gspec = pl.BlockSpec((None, rows, LANE), lambda b, c: (b, at(c), 0))
    ospec = pl.BlockSpec((None, rows, d), lambda b, c: (b, at(c), 0))
    sspec = pl.BlockSpec((None, group, heads, HEAD, HEAD), lambda b, c: (b, at(c), 0, 0, 0))
    tspec = pl.BlockSpec((None, group, heads, CHUNK, CHUNK), lambda b, c: (b, at(c), 0, 0, 0))
    return bl, s, d, n, group, sec, gspec, ospec, sspec, tspec


def _gdn_fwd(qkv, gbeta, heads, name):
    bl, s, d, n, group, sec, gspec, ospec, sspec, tspec = _gdn_specs(qkv, gbeta, heads, False)
    rows = lambda sub: slice(sub * CHUNK, (sub + 1) * CHUNK)
    pairs = [(h, sub) for h in range(heads) for sub in range(group)]

    def body(x_ref, g_ref, o_ref, s_ref, t_ref, st_ref):
        @pl.when(pl.program_id(1) == 0)
        def _():
            st_ref[...] = jnp.zeros_like(st_ref)

        masks = _chunk_masks()
        eye, causal, _ = masks
        gblks = [g_ref[rows(sub), :] for sub in range(group)]
        gcs = [_hdot(jnp.where(causal, 1.0, 0.0), gb) for gb in gblks]
        st_all = st_ref[...]

        def free(h, sub):
            q, k, v = (x_ref[rows(sub), sec * d + h * HEAD:sec * d + (h + 1) * HEAD] for sec in range(3))
            pre = _gdn_pre(q, k, v, _lane_col(gcs[sub], h), _lane_col(gblks[sub], heads + h), masks)
            yield
            t = yield from _tri_inv_steps(pre["low"], eye)
            uw = _bdot(t, jnp.concatenate([pre["vb"], pre["kbe"]], axis=1), 1, 0)
            return pre, t, uw[:, :HEAD], uw[:, HEAD:]

        pieces = dict(zip(pairs, _round_robin([free(h, sub) for h, sub in pairs])))

        def carry(h):
            st, outs, starts = st_all[h], [], []
            for sub in range(group):
                pre, _, u, w = pieces[h, sub]
                starts.append(st)
                vnew = u - _bdot(w, st, 1, 0)
                yield
                outs.append(_bdot(pre["qg"], st, 1, 0) + _bdot(pre["qk"], vnew, 1, 0))
                st = st * jnp.exp(pre["gl"]) + _bdot(pre["kdec"], vnew, 0, 0)
                yield
            return outs, starts, st

        carried = _round_robin([carry(h) for h in range(heads)])
        per_sub = lambda pick: [[pick(h, sub) for h in range(heads)] for sub in range(group)]
        o_ref[...] = jnp.concatenate([jnp.concatenate(r, axis=1) for r in per_sub(lambda h, sub: carried[h][0][sub])], axis=0)
        s_ref[...] = jnp.stack([jnp.stack(r) for r in per_sub(lambda h, sub: carried[h][1][sub])])
        t_ref[...] = jnp.stack([jnp.stack(r) for r in per_sub(lambda h, sub: pieces[h, sub][1])])
        st_ref[...] = jnp.stack([carried[h][2] for h in range(heads)])

    return pl.pallas_call(
        body, name=name, grid=(bl, n // group), in_specs=[sec, gspec], out_specs=[ospec, sspec, tspec],
        out_shape=[jax.ShapeDtypeStruct((bl, s, d), F32), jax.ShapeDtypeStruct((bl, n, heads, HEAD, HEAD), F32),
                   jax.ShapeDtypeStruct((bl, n, heads, CHUNK, CHUNK), F32)],
        scratch_shapes=[pltpu.VMEM((heads, HEAD, HEAD), F32)], compiler_params=_cparams("parallel", "arbitrary"),
    )(qkv, gbeta)


def _gdn_bwd(qkv, gbeta, dout, s_all, t_all, heads, name):
    bl, s, d, n, group, sec, gspec, ospec, sspec, tspec = _gdn_specs(qkv, gbeta, heads, True)
    rows = lambda sub: slice(sub * CHUNK, (sub + 1) * CHUNK)
    pairs = [(h, sub) for h in range(heads) for sub in range(group)]
    stack, side = functools.partial(jnp.concatenate, axis=0), functools.partial(jnp.concatenate, axis=1)

    def body(x_ref, g_ref, do_ref, s_ref, t_ref, dx_ref, dg_ref, ds_ref):
        @pl.when(pl.program_id(1) == 0)
        def _():
            ds_ref[...] = jnp.zeros_like(ds_ref)

        masks = _chunk_masks()
        eye, causal, strict = masks
        gblks = [g_ref[rows(sub), :] for sub in range(group)]
        gcs = [_hdot(jnp.where(causal, 1.0, 0.0), gb) for gb in gblks]
        lane = lax.broadcasted_iota(jnp.int32, (CHUNK, LANE), 1)
        last_row = lax.broadcasted_iota(jnp.int32, (CHUNK, 1), 0) == CHUNK - 1
        rowsum = lambda a: jnp.sum(a, axis=1, keepdims=True)
        st_all, t_all_, ds_all = s_ref[...], t_ref[...], ds_ref[...]

        def free(h, sub):
            q, k, v = (x_ref[rows(sub), sec * d + h * HEAD:sec * d + (h + 1) * HEAD] for sec in range(3))
            do = do_ref[rows(sub), h * HEAD:(h + 1) * HEAD]
            beta = _lane_col(gblks[sub], heads + h)
            st, t = st_all[sub, h], t_all_[sub, h]
            pre = _gdn_pre(q, k, v, _lane_col(gcs[sub], h), beta, masks)
            yield
            uw = _bdot(t, side([pre["vb"], pre["kbe"]]), 1, 0)
            u, w = uw[:, :HEAD], uw[:, HEAD:]
            yield
            vnew = u - _bdot(w, st, 1, 0)
            yield
            dqk = jnp.where(causal, _bdot(do, vnew, 1, 1), 0.0)
            dqg = _bdot(do, st, 1, 1)
            return dict(q=q, k=k, v=v, do=do, beta=beta, st=st, t=t, pre=pre, w=w, vnew=vnew, dqk=dqk, dqg=dqg)

        pieces = dict(zip(pairs, _round_robin([free(h, sub) for h, sub in pairs])))

        def carry(h):
            dsn, outs = ds_all[h], {}
            for sub in reversed(range(group)):
                pc = pieces[h, sub]
                pre, st, do = pc["pre"], pc["st"], pc["do"]
                egl = jnp.exp(pre["gl"])
                dkdec = _bdot(pc["vnew"], dsn, 1, 1)
                dvnew = _bdot(pre["kdec"], dsn, 1, 0) + _bdot(pre["qk"], do, 0, 0)
                dgl = jnp.sum(dsn * st, keepdims=True) * egl
                yield
                dw = -_bdot(dvnew, st, 1, 1)
                dsn = dsn * egl + _bdot(stack([pre["qg"], -pc["w"]]), stack([do, dvnew]), 0, 0)
                outs[sub] = (dkdec, dvnew, dgl, dw)
                yield
            return outs, dsn

        carried = _round_robin([carry(h) for h in range(heads)])

        def rest(h, sub):
            pc = pieces[h, sub]
            dkdec, dvnew, dgl, dw = carried[h][0][sub]
            q, k, v, beta, t, pre, dqk, dqg = (pc[x] for x in ("q", "k", "v", "beta", "t", "pre", "dqk", "dqg"))
            decay, eg, kb, vb, kbe, low, qk, qg, kdec = (pre[x] for x in ("decay", "eg", "kb", "vb", "kbe", "low", "qk", "qg", "kdec"))
            dt = _bdot(side([dvnew, dw]), side([vb, kbe]), 1, 1)
            by_t = _bdot(t, side([dvnew, dw]), 0, 0)
            dvb, dkbe = by_t[:, :HEAD], by_t[:, HEAD:]
            yield
            inner = _bdot(dt, t, 1, 1)
            yield
            dlow = -jnp.where(strict, _bdot(t, inner, 0, 0), 0.0)
            da, db = dlow * decay, dqk * decay
            yield
            m = dlow * low + dqk * qk
            kdk = dkdec * kdec
            col_of_m = jnp.sum(jnp.where(eye, jnp.sum(m, axis=0, keepdims=True), 0.0), axis=1, keepdims=True)
            dgc = rowsum(m) - col_of_m + rowsum(dqg * qg) + rowsum(dkbe * kbe) - rowsum(kdk)
            dgc = dgc + jnp.where(last_row, dgl + jnp.sum(kdk, keepdims=True), 0.0)
            by_k = _bdot(stack([da, db]), k, 1, 0)
            dkb = by_k[:CHUNK] + dkbe * eg
            yield
            dk = _bdot(stack([da, db]), stack([kb, q]), 0, 0) + dkdec * pre["rest"] + dkb * beta
            dq = by_k[CHUNK:] + dqg * eg
            dbeta = rowsum(dkb * k) + rowsum(dvb * v)
            return dq, dk, dvb * beta, jnp.where(lane == h, dgc, 0.0) + jnp.where(lane == heads + h, dbeta, 0.0)

        done = dict(zip(pairs, _round_robin([rest(h, sub) for h, sub in pairs])))
        dx_ref[...] = stack([side([done[h, sub][i] for i in range(3) for h in range(heads)]) for sub in range(group)])
        ds_ref[...] = jnp.stack([carried[h][1] for h in range(heads)])
        upper = jnp.where(jnp.logical_or(eye, jnp.logical_not(causal)), 1.0, 0.0)
        dgs = []
        for sub in range(group):
            dgb = done[0, sub][3]
            for h in range(1, heads):
                dgb = dgb + done[h, sub][3]
            dgs.append(jnp.where(lane < heads, _hdot(upper, dgb), dgb))
        dg_ref[...] = stack(dgs)

    return pl.pallas_call(
        body, name=name, grid=(bl, n // group), in_specs=[sec, gspec, ospec, sspec, tspec], out_specs=[sec, gspec],
        out_shape=[jax.ShapeDtypeStruct(qkv.shape, F32), jax.ShapeDtypeStruct((bl, s, LANE), F32)],
        scratch_shapes=[pltpu.VMEM((heads, HEAD, HEAD), F32)], compiler_params=_cparams("parallel", "arbitrary"),
    )(qkv, gbeta, dout, s_all, t_all)


def _position():
    return lax.axis_index("x"), lax.axis_index("y"), lax.axis_index("c")


def _all_gather(x, *, name):
    space = pltpu.VMEM

    def body(x_ref, out_ref, send_sems, recv_sems, local_sem):
        ax, ay, ac = _position()
        me, sibling = (ax, ay, ac), (ax, ay, 1 - ac)
        chips = [(1 - ax, ay), (ax, 1 - ay), (1 - ax, 1 - ay)]

        def slot(px, py, pc):
            return out_ref.at[4 * px + 2 * py + pc]

        def copy(k, block, to, src=None):
            return pltpu.make_async_remote_copy(
                src_ref=slot(*block) if src is None else src, dst_ref=slot(*block), send_sem=send_sems.at[k],
                recv_sem=recv_sems.at[k], device_id=to, device_id_type=MESH_IDS)

        mine = pltpu.make_async_copy(x_ref, slot(*me), local_sem)
        mine.start()
        first = [copy(0, me, sibling, src=x_ref)] + [copy(1 + j, me, (*chip, ac), src=x_ref) for j, chip in enumerate(chips)]
        for cp in first:
            cp.start()
        passed = [copy(4 + j, (*chip, ac), sibling) for j, chip in enumerate(chips)]
        for j, chip in enumerate(chips):
            copy(1 + j, (*chip, ac), me).wait_recv()
            passed[j].start()
        copy(0, sibling, me).wait_recv()
        for j, chip in enumerate(chips):
            copy(4 + j, (*chip, 1 - ac), me).wait_recv()
        for cp in first + passed:
            cp.wait_send()
        mine.wait()

    return pl.pallas_call(
        body, name=name, out_shape=jax.ShapeDtypeStruct((NDEV,) + x.shape, x.dtype),
        in_specs=[pl.BlockSpec(memory_space=space)], out_specs=pl.BlockSpec(memory_space=space),
        scratch_shapes=[pltpu.SemaphoreType.DMA((7,)), pltpu.SemaphoreType.DMA((7,)), pltpu.SemaphoreType.DMA],
    )(x)


class _Rider:
    def __init__(self, arrays, out_shapes, sems, hooks):
        self.arrays, self.out_shapes, self.sems, self.hooks = arrays, out_shapes, sems, hooks


def _gather_rider(xs):
    n = len(xs)

    def hooks(x_refs, out_refs, send_sems, recv_sems):
        ax, ay, ac = _position()
        me, sibling = (ax, ay, ac), (ax, ay, 1 - ac)
        chips = [(1 - ax, ay), (ax, 1 - ay), (1 - ax, 1 - ay)]

        def copies(k, block, to, own=False):
            out = []
            for i in range(n):
                slot = out_refs[i].at[4 * block[0] + 2 * block[1] + block[2]]
                out.append(pltpu.make_async_remote_copy(
                    src_ref=x_refs[i] if own else slot, dst_ref=slot, send_sem=send_sems.at[k, i], recv_sem=recv_sems.at[k, i],
                    device_id=to, device_id_type=MESH_IDS))
            return out

        def first():
            for cp in copies(0, me, sibling, own=True):
                cp.start()
            for j, chip in enumerate(chips):
                for cp in copies(1 + j, me, (*chip, ac), own=True):
                    cp.start()

        def mid():
            for j, chip in enumerate(chips):
                for arrived, onward in zip(copies(1 + j, (*chip, ac), me), copies(4 + j, (*chip, ac), sibling)):
                    arrived.wait_recv()
                    onward.start()

        def last():
            for cp in copies(0, sibling, me):
                cp.wait_recv()
            for j, chip in enumerate(chips):
                for cp in copies(4 + j, (*chip, 1 - ac), me):
                    cp.wait_recv()
            for cp in copies(0, me, sibling, own=True):
                cp.wait_send()
            for j, chip in enumerate(chips):
                for cp in copies(1 + j, me, (*chip, ac), own=True) + copies(4 + j, (*chip, ac), sibling):
                    cp.wait_send()

        return first, mid, last

    return _Rider(list(xs), [jax.ShapeDtypeStruct((NDEV,) + x.shape, x.dtype) for x in xs],
                  [pltpu.SemaphoreType.DMA((7, n)), pltpu.SemaphoreType.DMA((7, n))], hooks)


def _scatter_rider(parts):
    packed = sum(r for _, r in parts)
    width, dtype = parts[0][0].shape[1], parts[0][0].dtype

    def hooks(g_refs, out_refs, send_sems, recv_sems):
        (recv_ref,) = out_refs
        ax, ay, ac = _position()

        def peer(rel):
            flip = lambda a, bit: 1 - a if rel & bit else a
            return flip(ax, 4), flip(ay, 2), flip(ac, 1)

        def first():
            for rel in range(1, NDEV):
                px, py, pc = peer(rel)
                off = 0
                for g_ref, (_, r) in zip(g_refs, parts):
                    rows = g_ref.at[pl.ds(pl.multiple_of((4 * px + 2 * py + pc) * r, ROW_ALIGN), r)]
                    pltpu.make_async_remote_copy(
                        src_ref=rows, dst_ref=recv_ref.at[rel - 1, pl.ds(off, r)], send_sem=send_sems.at[rel - 1],
                        recv_sem=recv_sems.at[rel - 1], device_id=(px, py, pc), device_id_type=MESH_IDS).start()
                    off += r

        def last():
            for rel in range(1, NDEV):
                slot = recv_ref.at[rel - 1]
                pltpu.make_async_remote_copy(src_ref=slot, dst_ref=slot, send_sem=send_sems.at[rel - 1],
                                             recv_sem=recv_sems.at[rel - 1], device_id=peer(rel), device_id_type=MESH_IDS).wait()

        return first, lambda: None, last

    return _Rider([g for g, _ in parts], [jax.ShapeDtypeStruct((NDEV - 1, packed, width), dtype)],
                  [pltpu.SemaphoreType.DMA((NDEV - 1,)), pltpu.SemaphoreType.DMA((NDEV - 1,))], hooks)


def _sum_direct(own, recv, name):
    r, w = own.shape
    tr = max(t for t in range(ROW_ALIGN, 257, ROW_ALIGN) if r % t == 0)

    def body(own_ref, *refs):
        acc = own_ref[...].astype(F32)
        for ref in refs[:-1]:
            acc = acc + ref[...].astype(F32)
        refs[-1][...] = acc

    rblk = lambda k: pl.BlockSpec((None, tr, w), functools.partial(lambda i, k: (k, i, 0), k=k))
    blk = pl.BlockSpec((tr, w), lambda i: (i, 0))
    return pl.pallas_call(body, name=name, grid=(r // tr,), in_specs=[blk] + [rblk(k) for k in range(NDEV - 1)],
                          out_specs=blk, out_shape=jax.ShapeDtypeStruct((r, w), F32),
                          compiler_params=_cparams("parallel"))(own, *([recv] * (NDEV - 1)))


ROW_ALIGN = 16


def _window_start(rows_per_dev, k):
    return rows_per_dev * k // ROW_ALIGN * ROW_ALIGN


def _exchange_in_chip(parts, name, collective_id):
    packed = sum(win for _, _, win, _ in parts)
    width, dtype = parts[0][0].shape[1], parts[0][0].dtype

    def body(g_refs, out_refs, send_sems, recv_sems):
        (recv_ref,) = out_refs
        ax, ay, ac = _position()
        sibling = (ax, ay, 1 - ac)
        _handshake([sibling])
        for q in range(4):
            for g_ref, (_, r, win, off) in zip(g_refs, parts):
                there = g_ref.at[pl.ds(pl.multiple_of(_window_start(r, 2 * q + 1 - ac), ROW_ALIGN), win)]
                pltpu.make_async_remote_copy(src_ref=there, dst_ref=recv_ref.at[q, pl.ds(off, win)], send_sem=send_sems.at[q],
                                             recv_sem=recv_sems.at[q], device_id=sibling, device_id_type=MESH_IDS).start()
        for q in range(4):
            pltpu.make_async_remote_copy(src_ref=recv_ref.at[q], dst_ref=recv_ref.at[q], send_sem=send_sems.at[q],
                                         recv_sem=recv_sems.at[q], device_id=sibling, device_id_type=MESH_IDS).wait()

    return _on_sequencer(body, [g for g, _, _, _ in parts], [jax.ShapeDtypeStruct((4, packed, width), dtype)],
                         [pltpu.SemaphoreType.DMA((4,)), pltpu.SemaphoreType.DMA((4,))], name=name, collective_id=collective_id)[0]


def _on_sequencer(body, ins, out_shapes, sems, *, name, collective_id):
    hbm = pltpu.MemorySpace.HBM
    in_refs = [jax.new_ref(a, memory_space=hbm) for a in ins]
    out_refs = [jax.empty_ref(s, memory_space=hbm) for s in out_shapes]

    @pl.kernel(mesh=plsc.ScalarSubcoreMesh(axis_name="sequencer", num_cores=1), name=name, scratch_types=tuple(sems),
               compiler_params=pltpu.CompilerParams(collective_id=collective_id))
    def launch(*sem_refs):
        body(in_refs, out_refs, *sem_refs)

    launch()
    return [r[...] for r in out_refs]


def _handshake(peers):
    barrier = pltpu.get_barrier_semaphore()
    for peer in peers:
        pl.semaphore_signal(barrier, inc=1, device_id=peer, device_id_type=MESH_IDS)
    pl.semaphore_wait(barrier, len(peers))


def _exchange_chips_async(s1, name, collective_id):
    def body(in_refs, out_refs, send_sems, recv_sems):
        (src,), (got,) = in_refs, out_refs
        ax, ay, ac = _position()
        chips = [(1 - ax, ay), (ax, 1 - ay), (1 - ax, 1 - ay)]
        _handshake([(cx, cy, ac) for cx, cy in chips])
        copies = [pltpu.make_async_remote_copy(
            src_ref=src.at[2 * cx + cy], dst_ref=got.at[r], send_sem=send_sems.at[r], recv_sem=recv_sems.at[r],
            device_id=(cx, cy, ac), device_id_type=MESH_IDS) for r, (cx, cy) in enumerate(chips)]
        for cp in copies:
            cp.start()
        for cp in copies:
            cp.wait_recv()
        for cp in copies:
            cp.wait_send()

    return _on_sequencer(body, [s1], [jax.ShapeDtypeStruct((3,) + s1.shape[1:], s1.dtype)],
                         [pltpu.SemaphoreType.DMA((3,)), pltpu.SemaphoreType.DMA((3,))], name=name, collective_id=collective_id)[0]


def _gather_async(xs, name, collective_id):
    rider = _gather_rider(xs)

    def body(in_refs, out_refs, send_sems, recv_sems):
        ax, ay, ac = _position()
        _handshake([(ax, ay, 1 - ac), (1 - ax, ay, ac), (ax, 1 - ay, ac), (1 - ax, 1 - ay, ac)])
        for hook in rider.hooks(in_refs, out_refs, send_sems, recv_sems):
            hook()

    return _on_sequencer(body, rider.arrays, rider.out_shapes, rider.sems, name=name, collective_id=collective_id)


def _scatter_async(parts, name, collective_id):
    rider = _scatter_rider(parts)

    def body(in_refs, out_refs, send_sems, recv_sems):
        ax, ay, ac = _position()
        flip = lambda a, on: 1 - a if on else a
        _handshake([(flip(ax, rel & 4), flip(ay, rel & 2), flip(ac, rel & 1)) for rel in range(1, NDEV)])
        for hook in rider.hooks(in_refs, out_refs, send_sems, recv_sems):
            hook()

    return _on_sequencer(body, rider.arrays, rider.out_shapes, rider.sems, name=name, collective_id=collective_id)[0]


def _sum_in_chip(own, recv, name):
    _, r, w = own.shape
    tr = _tile(r, (256, 128))

    def body(a_ref, b_ref, o_ref):
        o_ref[...] = (a_ref[...].astype(F32) + b_ref[...].astype(F32)).astype(o_ref.dtype)

    blk = pl.BlockSpec((None, tr, w), lambda q, i: (q, i, 0))
    return pl.pallas_call(body, name=name, grid=(4, r // tr), in_specs=[blk, blk], out_specs=blk,
                          out_shape=jax.ShapeDtypeStruct(own.shape, own.dtype),
                          compiler_params=_cparams("parallel", "parallel"))(own, recv)


def _sum_chips(s1, recv, chip, name):
    _, r, w = s1.shape
    tr = _tile(r, (256, 128))

    def body(c_ref, s_ref, r0_ref, r1_ref, r2_ref, o_ref):
        f = lambda ref: ref[...].astype(F32)
        o_ref[...] = ((f(s_ref) + f(r0_ref)) + f(r1_ref)) + f(r2_ref)

    rblk = lambda k: pl.BlockSpec((None, tr, w), functools.partial(lambda i, c, k: (k, i, 0), k=k))
    grid_spec = pltpu.PrefetchScalarGridSpec(
        num_scalar_prefetch=1, grid=(r // tr,),
        in_specs=[pl.BlockSpec((None, tr, w), lambda i, c: (c[0], i, 0)), rblk(0), rblk(1), rblk(2)],
        out_specs=pl.BlockSpec((tr, w), lambda i, c: (i, 0)))
    return pl.pallas_call(body, name=name, grid_spec=grid_spec, out_shape=jax.ShapeDtypeStruct((r, w), F32),
                          compiler_params=_cparams("parallel"))(chip, s1, recv, recv, recv)


def _silu_rows(x, name):
    def body(x_ref, o_ref):
        o_ref[...] = _silu(x_ref[...])

    return pl.pallas_call(body, name=name, out_shape=jax.ShapeDtypeStruct(x.shape, F32))(x)


def _row_sum(x, name):
    def body(x_ref, o_ref):
        acc = x_ref[0:1, :]
        for i in range(1, x.shape[0]):
            acc = acc + x_ref[i:i + 1, :]
        o_ref[...] = acc

    return pl.pallas_call(body, name=name, out_shape=jax.ShapeDtypeStruct((1, x.shape[1]), F32))(x)


def _adamw(w, g, m, v, name):
    cols = w.shape[-1]
    rows = w.size // cols
    tr = _tile(rows, (128,))
    tc = LANE if (tr == rows and rows > 512 and cols % LANE == 0) else cols

    def body(w_ref, g_ref, m_ref, v_ref, d_ref, mo_ref, vo_ref):
        grad = g_ref[...]
        m_new = ADAM_B1 * m_ref[...] + (1.0 - ADAM_B1) * grad
        v_new = ADAM_B2 * v_ref[...] + (1.0 - ADAM_B2) * jnp.square(grad)
        m_hat = m_new / (1.0 - ADAM_B1 ** ADAM_STEP)
        v_hat = v_new / (1.0 - ADAM_B2 ** ADAM_STEP)
        d_ref[...] = -ADAM_LR * (m_hat / (jnp.sqrt(v_hat) + ADAM_EPS) + ADAM_WD * w_ref[...])
        mo_ref[...] = m_new
        vo_ref[...] = v_new

    blk = pl.BlockSpec((tr, tc), lambda i, j: (i, j))
    out = pl.pallas_call(
        body, name=name, grid=(rows // tr, cols // tc), in_specs=[blk] * 4, out_specs=[blk] * 3,
        out_shape=[jax.ShapeDtypeStruct((rows, cols), F32)] * 3, compiler_params=_cparams("parallel", "parallel"),
    )(*[t.reshape(rows, cols) for t in (w, g, m, v)])
    return [t.reshape(w.shape) for t in out]


def _pack(parts, width, row_mult, dtype):
    flat = jnp.concatenate([p.reshape(-1).astype(dtype) for p in parts])
    rows = -(-flat.shape[0] // (width * row_mult)) * row_mult
    return jnp.pad(flat, (0, rows * width - flat.shape[0])).reshape(rows, width)


def _unpack(flat, shapes):
    out, off = [], 0
    for shp in shapes:
        size = 1
        for dim in shp:
            size *= dim
        out.append(flat[:, off:off + size].reshape((flat.shape[0],) + tuple(shp)))
        off += size
    return out


def _devices_to_cols(a):
    _, r, c = a.shape
    return a.transpose(1, 0, 2).reshape(r, NDEV * c)


def kernel(x, c, w_ada, b_ada, norm1_w, w_in, gdn_conv_w, gdn_a_log, gdn_dt_bias, gdn_norm_w, w_gdn_proj, sc_conv_w, w_sc_out, w_o, norm2_w, w_ffn_in, w_ffn_out, w_ada_f, b_ada_f, normf_w, loss_target, m_w_ada, m_b_ada, m_norm1_w, m_w_in, m_gdn_conv_w, m_gdn_a_log, m_gdn_dt_bias, m_gdn_norm_w, m_w_gdn_proj, m_sc_conv_w, m_w_sc_out, m_w_o, m_norm2_w, m_w_ffn_in, m_w_ffn_out, m_w_ada_f, m_b_ada_f, m_normf_w, v_w_ada, v_b_ada, v_norm1_w, v_w_in, v_gdn_conv_w, v_gdn_a_log, v_gdn_dt_bias, v_gdn_norm_w, v_w_gdn_proj, v_sc_conv_w, v_w_sc_out, v_w_o, v_norm2_w, v_w_ffn_in, v_w_ffn_out, v_w_ada_f, v_b_ada_f, v_normf_w):
    bl, s, d = x.shape
    heads = gdn_a_log.shape[-1]
    dff = w_ffn_out.shape[1] * NDEV
    tok = bl * s
    ax, ay, ac = _position()
    dev = 4 * ax + 2 * ay + ac
    as_tok = lambda a: a.reshape(bl, s, a.shape[-1])
    as_mat = lambda a: a.reshape(tok, a.shape[-1])

    small = _all_gather(_pack([c, gdn_conv_w, sc_conv_w], LANE, 8, F32), name="gather_cond")
    c_all, conv_w, sc_w = _unpack(small.reshape(NDEV, -1), [(bl, d), gdn_conv_w.shape[1:], sc_conv_w.shape[1:]])
    c_act = _silu_rows(c_all.reshape(NDEV * bl, d), "cond_silu")
    conv_w, sc_w = _devices_to_cols(conv_w), _devices_to_cols(sc_w)
    n_ada, n_adaf = w_ada.shape[-1], w_ada_f.shape[-1]
    bias = jnp.broadcast_to(lax.dynamic_slice_in_dim(b_ada, dev * n_ada, n_ada, axis=1), (NDEV * bl, n_ada))
    biasf = jnp.broadcast_to(lax.dynamic_slice_in_dim(b_ada_f.reshape(1, -1), dev * n_adaf, n_adaf, axis=1), (NDEV * bl, n_adaf))
    mod_cols = _mm(c_act, w_ada[0], add=bias, name="ada_cols")
    modf_cols = _mm(c_act, w_ada_f, add=biasf, name="adaf_cols")
    mods = _all_gather(jnp.concatenate([mod_cols, modf_cols], axis=1), name="gather_mod")
    mod_all = mods[:, :, :n_ada].transpose(1, 0, 2).reshape(NDEV * bl, NDEV * n_ada)
    modf_all = mods[:, :, n_ada:].transpose(1, 0, 2).reshape(NDEV * bl, NDEV * n_adaf)
    my_rows = lambda a: lax.dynamic_slice_in_dim(a, dev * bl, bl, axis=0)
    sh1, sc1, g1, sh2, sc2, g2 = [t.reshape(bl, 1, d) for t in jnp.split(my_rows(mod_all), 6, axis=1)]
    shf, scf = [t.reshape(bl, 1, d) for t in jnp.split(my_rows(modf_all), 2, axis=1)]

    late = [t.astype(MXU_DTYPE) for t in (w_gdn_proj[0], w_sc_out[0], w_o[0], w_ffn_in[0].T, w_ffn_out[0])]
    rows = [t.shape[0] for t in late] + [w_in.shape[-1]]
    offs = [sum(rows[:i]) for i in range(5)]
    in_send = w_in[0].T.astype(MXU_DTYPE)
    with_own = lambda g, own: lax.dynamic_update_slice_in_dim(g, own[None], dev, axis=0)
    (wt_in,) = _gather_async([in_send], "gather_w_in", 1)
    wt_in = with_own(wt_in, in_send).reshape(NDEV * rows[5], d)
    gathered = _gather_async(late[:3], "gather_mixer", 2) + _gather_async(late[3:], "gather_ffn", 3)
    wgp, wso, wo, wt_fi, wfo = [with_own(g, own).reshape(NDEV * own.shape[0], d) for g, own in zip(gathered, late)]
    o_z, o_ab, o_sc, o_ga, o_gb = 3 * d, 4 * d, 4 * d + 2 * heads, 7 * d + 2 * heads, 8 * d + 2 * heads
    s_qkv, s_z, s_sc, s_gate = (0, o_z), (o_z, d), (o_sc, 3 * d), (o_ga, 2 * d)
    wt_ab = jnp.pad(wt_in[o_ab:o_sc], ((0, LANE - 2 * heads), (0, 0)))

    n1w, n2w, nfw = norm1_w.reshape(1, d), norm2_w.reshape(1, d), normf_w.reshape(1, d)
    lanes = lambda a: jnp.pad(a.reshape(1, -1), ((0, 0), (0, LANE - a.size)))
    a_log, dt_bias, gnw = lanes(gdn_a_log), lanes(gdn_dt_bias), gdn_norm_w.reshape(1, HEAD)
    f_gates = functools.partial(_f_gates, heads=heads)
    (h1,) = _tok_fwd(_f_norm_mod, [x], [sh1, sc1], [n1w], [(d, MXU_DTYPE)], name="norm1", ts=512)
    h1m = as_mat(h1)
    p_qkv = as_tok(_mm(h1m, wt_in, tb=True, b_rows=s_qkv, name="in_qkv"))
    p_z = as_tok(_mm(h1m, wt_in, tb=True, b_rows=s_z, name="in_z"))
    p_ab = as_tok(_mm(h1m, wt_ab, tb=True, name="in_ab"))
    p_sc = as_tok(_mm(h1m, wt_in, tb=True, b_rows=s_sc, name="in_sc"))
    p_g = as_tok(_mm(h1m, wt_in, tb=True, b_rows=s_gate, name="in_gate"))
    qkv = _qkv_fwd(p_qkv, conv_w, heads, "qkv_conv")
    (gbeta,) = _tok_fwd(f_gates, [p_ab], [], [a_log, dt_bias], [(LANE, F32)], name="gates", ts=512)
    o, s_all, t_all = _gdn_fwd(qkv, gbeta, heads, "gdn")
    (og,) = _tok_fwd(_f_gdn_out, [o, p_z], [], [(gnw, None)], [(d, MXU_DTYPE)], name="gdn_out", ts=2048, wb=HEAD, cols=heads)
    y_a = as_tok(_mm(as_mat(og), wgp, name="gdn_proj"))
    scp = _sc_fwd(p_sc, sc_w, "sc_conv")
    mrg, y_b = _tok_fwd(_f_merge_keep, [(p_g, 0), (p_g, 1), y_a, _Product(scp, wso)], [], [], [(d, MXU_DTYPE), (d, F32)],
                        name="merge", ts=512, wb=d)
    merge_toks = [(p_g, 0), (p_g, 1), y_a, y_b]
    x2, h2, mix = _tok_fwd(_f_res_norm_mod_keep, [x, _Product(mrg, wo)], [g1, sh2, sc2], [n2w],
                           [(d, F32), (d, MXU_DTYPE), (d, F32)], name="norm2", ts=512)
    act, gu_a, gu_b = _ffn_in_swiglu(as_mat(h2), wt_fi, dff, "ffn_in")

    loss_l, (dx2, dff_out, _), (dg2, dshf, dscf), (dnfw,) = _tok_bwd(
        _f_loss, [x2, _Product(as_tok(act), wfo), loss_target], [g2, shf, scf], [nfw], [], [True, True, False], name="loss",
        ts=256, loss=True, tok_dtype=[F32, MXU_DTYPE, None])
    dffm = as_mat(dff_out)
    dgu_a, dgu_b = _ffn_out_bwd_swiglu(dffm, wfo, gu_a, gu_b, "d_ffn_out")
    gmm = functools.partial(_mm, ta=True, out_dtype=MXU_DTYPE)
    gw_ffn_out = gmm(act, dffm, name="g_ffn_out")
    dh2 = _Product(as_tok(dgu_b), wt_fi, b_rows=(dff, dff), add=as_tok(_mm(dgu_a, wt_fi, b_rows=(0, dff), name="d_ffn_in_a")))
    h2m = as_mat(h2)
    gwt_ffn_in = gmm(dgu_a, h2m, out_rows=2 * dff, name="g_ffn_in_a")
    gwt_ffn_in = gmm(dgu_b, h2m, out_rows=2 * dff, row_off=dff, into=gwt_ffn_in, name="g_ffn_in_b")
    ffn_parts = [(gwt_ffn_in, rows[3]), (gw_ffn_out, rows[4])]
    ffn_recv = _scatter_async(ffn_parts, "scatter_ffn", 4)
    (dx_skip, dmix), (dg1, dsh2, dsc2), (dn2w,) = _tok_bwd(
        _f_res_norm_mod, [x, mix], [g1, sh2, sc2], [n2w], [dx2, dh2], [True, True], name="d_norm2", ts=256,
        tok_dtype=[F32, MXU_DTYPE], after=[gwt_ffn_in, gw_ffn_out])
    gw_o = gmm(as_mat(mrg), as_mat(dmix), name="g_mix_out")
    (dga, dgb, dya, dyb), _, _ = _tok_bwd(_f_merge, merge_toks, [], [], [_Product(dmix, wo, tb=True)], [True] * 4,
                                          name="d_merge", ts=256, wb=d, tok_dtype=MXU_DTYPE)
    dyam, dybm = as_mat(dya), as_mat(dyb)
    dog = as_tok(_mm(dyam, wgp, tb=True, name="d_gdn_proj"))
    gw_gdn_proj = gmm(as_mat(og), dyam, name="g_gdn_proj")
    dscp = as_tok(_mm(dybm, wso, tb=True, name="d_sc_out"))
    gw_sc_out = gmm(as_mat(scp), dybm, name="g_sc_out")
    dsc, g_sc_w = _sc_bwd(p_sc, sc_w, dscp, "d_sc_conv")
    mix_parts = [(gw_gdn_proj, rows[0]), (gw_sc_out, rows[1]), (gw_o, rows[2])]
    mix_recv = _scatter_async(mix_parts, "scatter_mixer", 5)
    (do, dz), _, (g_gnw,) = _tok_bwd(_f_gdn_out, [o, p_z], [], [(gnw, None)], [dog], [True, True], name="d_gdn_out",
                                     ts=2048, wb=HEAD, cols=heads, tok_dtype=[F32, MXU_DTYPE],
                                     after=[gw_gdn_proj, gw_sc_out, gw_o])
    own_rows = lambda parts: jnp.concatenate([lax.dynamic_slice_in_dim(g, dev * r, r, axis=0) for g, r in parts], axis=0)
    dqkv, dgbeta = _gdn_bwd(qkv, gbeta, do, s_all, t_all, heads, "d_gdn")
    dp_qkv, g_conv_w = _qkv_bwd(p_qkv, conv_w, dqkv, heads, "d_qkv_conv")
    ffn_red = _sum_direct(own_rows(ffn_parts), ffn_recv, "sum_ffn")
    mix_red = _sum_direct(own_rows(mix_parts), mix_recv, "sum_mix")
    (dp_ab,), _, (g_a_log, g_dt_bias) = _tok_bwd(f_gates, [p_ab], [], [a_log, dt_bias], [dgbeta], [True], name="d_gates",
                                                 ts=512, tok_dtype=MXU_DTYPE, after=[ffn_red, mix_red])
    r_in = rows[5]
    win = -(-(r_in + max(r_in * k % ROW_ALIGN for k in range(NDEV))) // 128) * 128
    need_rows = max(_window_start(r_in, k) for k in range(NDEV)) + win
    dsc_m = dsc.reshape(3, tok, d)
    gwt_in = ([gmm(as_mat(dp_qkv), h1m, name="g_in_qkv"), gmm(as_mat(dz), h1m, name="g_in_z"),
               gmm(as_mat(dp_ab), h1m, name="g_in_ab")[:2 * heads]]
              + [gmm(dsc_m, h1m, a_index=k, name=f"g_in_sc{k}") for k in range(3)]
              + [gmm(as_mat(dga), h1m, name="g_in_ga"), gmm(as_mat(dgb), h1m, name="g_in_gb")])
    gwt_in = jnp.concatenate(gwt_in + [jnp.zeros((need_rows - NDEV * r_in, d), MXU_DTYPE)], axis=0)
    assert d <= 1024
    wide = [as_mat(dp_qkv), as_mat(dz), dsc_m, as_mat(dga)]
    row_of = lambda t: d * t + jnp.where(t * d >= o_ab, 2 * heads, 0)
    recv1 = _exchange_in_chip([(gwt_in, r_in, win, 0)], "scatter_in_chip", 7)
    own = jnp.stack([lax.dynamic_slice_in_dim(gwt_in, _window_start(r_in, 2 * q + ac), win, axis=0) for q in range(4)])
    s1 = _sum_in_chip(own, recv1, "sum_in_chip")
    recv2 = _exchange_chips_async(s1, "scatter_chips", 6)

    dh1 = _mm(as_mat(dp_ab), wt_ab, name="d_in_ab")
    dh1 = _mm_chain(wide, wt_in, row_of, add=dh1, name="d_in", tk=d)
    dh1 = _Product(dgb, wt_in, b_rows=(o_gb, d), add=as_tok(dh1))
    (grad_x,), (dsh1, dsc1), (dn1w,) = _tok_bwd(_f_norm_mod_skip, [x], [sh1, sc1], [n1w], [dh1, dx_skip], [True],
                                                name="d_norm1", ts=256)
    reduced = _sum_chips(s1, recv2, (2 * ax + ay).reshape(1).astype(jnp.int32), "sum_chips")
    gt_w_in = lax.dynamic_slice_in_dim(reduced, r_in * dev - _window_start(r_in, dev), r_in, axis=0)
    g_w_in = gt_w_in.T.reshape(w_in.shape)
    gt_w_ffn_in = ffn_red[:rows[3]]
    g_w_ffn_in = gt_w_ffn_in.T.reshape(w_ffn_in.shape)
    g_w_ffn_out = ffn_red[rows[3]:].reshape(w_ffn_out.shape)
    g_w_gdn_proj, g_w_sc_out, g_w_o = (mix_red[offs[i]:offs[i] + rows[i]].reshape(ref.shape)
                                       for i, ref in enumerate((w_gdn_proj, w_sc_out, w_o)))

    dmod = jnp.concatenate([t.reshape(bl, d) for t in (dsh1, dsc1, dg1, dsh2, dsc2, dg2)], axis=1)
    dmodf = jnp.concatenate([t.reshape(bl, d) for t in (dshf, dscf)], axis=1)
    summed_parts = [dn1w, dn2w, dnfw, g_gnw, g_a_log, g_dt_bias, g_conv_w, g_sc_w, loss_l]
    partial = _all_gather(_pack([dmod, dmodf] + summed_parts, LANE, 8, F32), name="gather_small")
    partial = partial.reshape(NDEV, -1)
    n_rows = bl * (6 * d + 2 * d)
    dmod_all, dmodf_all = _unpack(partial[:, :n_rows], [(bl, 6 * d), (bl, 2 * d)])
    dmod_all, dmodf_all = dmod_all.reshape(NDEV * bl, 6 * d), dmodf_all.reshape(NDEV * bl, 2 * d)
    totals = _row_sum(partial[:, n_rows:], "sum_small")
    t_n1w, t_n2w, t_nfw, t_gnw, t_a_log, t_dt_bias, t_conv_w, t_sc_w, t_loss = [
        t[0] for t in _unpack(totals, [p.shape for p in summed_parts])]
    my_cols = lambda a, n: lax.dynamic_slice_in_dim(a, dev * n, n, axis=1)
    grads = {
        "w_ada": _mm(c_act, my_cols(dmod_all, n_ada), ta=True, name="g_ada").reshape(w_ada.shape),
        "b_ada": _row_sum(dmod_all, "g_ada_bias").reshape(b_ada.shape),
        "norm1_w": t_n1w.reshape(norm1_w.shape),
        "w_in": g_w_in,
        "gdn_conv_w": my_cols(t_conv_w, gdn_conv_w.shape[-1]).reshape(gdn_conv_w.shape),
        "gdn_a_log": t_a_log[:, :heads].reshape(gdn_a_log.shape),
        "gdn_dt_bias": t_dt_bias[:, :heads].reshape(gdn_dt_bias.shape),
        "gdn_norm_w": t_gnw.reshape(gdn_norm_w.shape),
        "w_gdn_proj": g_w_gdn_proj,
        "sc_conv_w": my_cols(t_sc_w, sc_conv_w.shape[-1]).reshape(sc_conv_w.shape),
        "w_sc_out": g_w_sc_out,
        "w_o": g_w_o,
        "norm2_w": t_n2w.reshape(norm2_w.shape),
        "w_ffn_in": g_w_ffn_in,
        "w_ffn_out": g_w_ffn_out,
        "w_ada_f": _mm(c_act, my_cols(dmodf_all, n_adaf), ta=True, name="g_adaf").reshape(w_ada_f.shape),
        "b_ada_f": _row_sum(dmodf_all, "g_adaf_bias").reshape(b_ada_f.shape),
        "normf_w": t_nfw.reshape(normf_w.shape),
    }
    weights = dict(w_ada=w_ada, b_ada=b_ada, norm1_w=norm1_w, w_in=w_in, gdn_conv_w=gdn_conv_w, gdn_a_log=gdn_a_log,
                   gdn_dt_bias=gdn_dt_bias, gdn_norm_w=gdn_norm_w, w_gdn_proj=w_gdn_proj, sc_conv_w=sc_conv_w,
                   w_sc_out=w_sc_out, w_o=w_o, norm2_w=norm2_w, w_ffn_in=w_ffn_in, w_ffn_out=w_ffn_out, w_ada_f=w_ada_f,
                   b_ada_f=b_ada_f, normf_w=normf_w)
    m_in = [m_w_ada, m_b_ada, m_norm1_w, m_w_in, m_gdn_conv_w, m_gdn_a_log, m_gdn_dt_bias, m_gdn_norm_w, m_w_gdn_proj,
            m_sc_conv_w, m_w_sc_out, m_w_o, m_norm2_w, m_w_ffn_in, m_w_ffn_out, m_w_ada_f, m_b_ada_f, m_normf_w]
    v_in = [v_w_ada, v_b_ada, v_norm1_w, v_w_in, v_gdn_conv_w, v_gdn_a_log, v_gdn_dt_bias, v_gdn_norm_w, v_w_gdn_proj,
            v_sc_conv_w, v_w_sc_out, v_w_o, v_norm2_w, v_w_ffn_in, v_w_ffn_out, v_w_ada_f, v_b_ada_f, v_normf_w]
    deltas, new_m, new_v = [], [], []
    grads_t = {"w_in": gt_w_in, "w_ffn_in": gt_w_ffn_in}
    for (wname, wt), mt, vt in zip(weights.items(), m_in, v_in):
        if wname in grads_t:
            back = lambda a, wt=wt: a.T.reshape(wt.shape)
            dl, mn, vn = (back(a) for a in _adamw(wt[0].T, grads_t[wname], mt[0].T, vt[0].T, "adamw_" + wname))
        else:
            dl, mn, vn = _adamw(wt, grads[wname], mt, vt, "adamw_" + wname)
        deltas.append(dl)
        new_m.append(mn)
        new_v.append(vn)
    loss = t_loss[0, 0]
    return (loss, grad_x, *[grads[k] for k in weights], *deltas, *new_m, *new_v)
```

```python
import functools

import jax
import jax.numpy as jnp
from jax import lax
from jax.experimental import pallas as pl
from jax.experimental.pallas import tpu as pltpu
from jax.experimental.pallas import tpu_sc as plsc

F32 = jnp.float32
MXU_DTYPE = jnp.bfloat16
NDEV = 8
CHUNK = 64
HEAD = 128
LANE = 128
EPS = 1e-6
ADAM_LR, ADAM_B1, ADAM_B2, ADAM_EPS, ADAM_WD, ADAM_STEP = 0.001, 0.9, 0.999, 1e-08, 0.01, 10
VMEM_LIMIT = 48 * 1024 * 1024
MESH_IDS = pl.DeviceIdType.MESH
HIGHEST = lax.Precision.HIGHEST


def _tile(n, cands=(512, 256, 128)):
    for c in cands:
        if n % c == 0:
            return c
    return n


def _cparams(*sem):
    return pltpu.CompilerParams(dimension_semantics=sem, vmem_limit_bytes=VMEM_LIMIT)


def _mm(a, b, *, ta=False, tb=False, add=None, out_dtype=F32, name, b_rows=None, out_rows=None, row_off=0, into=None,
        a_index=None):
    m, k = (a.shape[-1], a.shape[-2]) if ta else a.shape[-2:]
    b_shape = b.shape if b_rows is None else (b_rows[1], b.shape[1])
    n = b_shape[0] if tb else b_shape[1]
    assert k == (b_shape[1] if tb else b_shape[0])
    if ta:
        tm, tn = _tile(m), n if n <= 1024 else _tile(n)
        tk = k if k <= 4096 else _tile(k, (4096, 2048, 1024, 512))
        if tm * tk > 1024 * 2048:
            tk = _tile(k, (2048, 1024, 512))
    else:
        tk = k if k <= 1024 else _tile(k, (1024, 512))
        tn = _tile(n, (1024 if tk <= 1024 else 512, 512, 256, 128))
        tm = _tile(m, (2048 if (tn <= 512 and tk <= 1024) else 1024, 1024, 512, 256, 128))
    nk = k // tk
    dims = (((0 if ta else 1,), (1 if tb else 0,)), ((), ()))
    has_add = add is not None

    def body(*refs):
        a_ref, b_ref = refs[0], refs[1]
        add_ref = refs[2] if has_add else None
        o_ref = refs[2 + has_add + (into is not None)]
        part = lax.dot_general(a_ref[...].astype(MXU_DTYPE), b_ref[...].astype(MXU_DTYPE), dims,
                               preferred_element_type=F32)

        def finish(acc):
            if has_add:
                acc = acc + add_ref[...]
            o_ref[...] = acc.astype(o_ref.dtype)

        if nk == 1:
            finish(part)
        else:
            acc_ref = refs[-1]
            kk = pl.program_id(2)

            @pl.when(kk == 0)
            def _():
                acc_ref[...] = part

            @pl.when(kk > 0)
            def _():
                acc_ref[...] += part

            @pl.when(kk == nk - 1)
            def _():
                finish(acc_ref[...])

    a_blk, a_at = ((tk, tm), lambda i, j, kk: (kk, i)) if ta else ((tm, tk), lambda i, j, kk: (i, kk))
    a_spec = (pl.BlockSpec(a_blk, a_at) if a_index is None else
              pl.BlockSpec((None,) + a_blk, lambda i, j, kk: (a_index,) + a_at(i, j, kk)))
    if b_rows is None:
        b_spec = pl.BlockSpec((tn, tk), lambda i, j, kk: (j, kk)) if tb else pl.BlockSpec((tk, tn), lambda i, j, kk: (kk, j))
    else:
        at = lambda t: pl.multiple_of(b_rows[0] + t, ROW_ALIGN)
        b_spec = (pl.BlockSpec((pl.Element(tn), pl.Element(tk)), lambda i, j, kk: (at(j * tn), kk * tk)) if tb else
                  pl.BlockSpec((pl.Element(tk), pl.Element(tn)), lambda i, j, kk: (at(kk * tk), j * tn)))
    add_spec = pl.BlockSpec((tm, tn), lambda i, j, kk: (i, j))
    assert row_off % tm == 0
    o_spec = pl.BlockSpec((tm, tn), lambda i, j, kk: (i + row_off // tm, j))
    in_specs = [a_spec, b_spec] + ([add_spec] if has_add else []) + ([pl.BlockSpec(memory_space=pl.ANY)] if into is not None else [])
    args = [a, b] + ([add] if has_add else []) + ([into] if into is not None else [])
    return pl.pallas_call(
        body, name=name, grid=(m // tm, n // tn, nk), in_specs=in_specs, out_specs=o_spec,
        out_shape=jax.ShapeDtypeStruct((out_rows or m, n), out_dtype),
        scratch_shapes=[pltpu.VMEM((tm, tn), F32)] if nk > 1 else [],
        input_output_aliases={len(args) - 1: 0} if into is not None else {},
        compiler_params=_cparams("parallel", "parallel", "arbitrary"),
    )(*args)


def _mm_chain(parts, b, row_of_tile, *, add, name, tk=1024, tm=1024):
    m, n = parts[0].shape[-2], b.shape[1]
    tm = min(tm, m)
    tiles = [p.shape[0] if p.ndim == 3 else p.shape[1] // tk for p in parts]
    first = [sum(tiles[:s]) for s in range(len(parts))]
    nk = sum(tiles)

    def body(*refs):
        a_refs, b_ref, add_ref, o_ref, acc_ref = refs[:len(parts)], *refs[len(parts):]
        kk = pl.program_id(1)

        @pl.when(kk == 0)
        def _():
            acc_ref[...] = add_ref[...]

        for a_ref, lo, cnt in zip(a_refs, first, tiles):
            @pl.when(jnp.logical_and(kk >= lo, kk < lo + cnt))
            def _(a_ref=a_ref):
                acc_ref[...] += lax.dot_general(a_ref[...].astype(MXU_DTYPE), b_ref[...].astype(MXU_DTYPE),
                                                (((1,), (0,)), ((), ())), preferred_element_type=F32)

        @pl.when(kk == nk - 1)
        def _():
            o_ref[...] = acc_ref[...]

    tile_of = lambda kk, lo, cnt: jnp.clip(kk - lo, 0, cnt - 1)
    a_specs = [pl.BlockSpec((None, tm, tk), functools.partial(lambda i, kk, lo, cnt: (tile_of(kk, lo, cnt), i, 0), lo=lo, cnt=cnt))
               if p.ndim == 3 else
               pl.BlockSpec((tm, tk), functools.partial(lambda i, kk, lo, cnt: (i, tile_of(kk, lo, cnt)), lo=lo, cnt=cnt))
               for p, lo, cnt in zip(parts, first, tiles)]
    b_spec = pl.BlockSpec((pl.Element(tk), pl.Element(n)), lambda i, kk: (pl.multiple_of(row_of_tile(kk), ROW_ALIGN), 0))
    o_spec = pl.BlockSpec((tm, n), lambda i, kk: (i, 0))
    return pl.pallas_call(
        body, name=name, grid=(m // tm, nk), in_specs=a_specs + [b_spec, o_spec], out_specs=o_spec,
        out_shape=jax.ShapeDtypeStruct((m, n), F32), scratch_shapes=[pltpu.VMEM((tm, n), F32)],
        compiler_params=_cparams("parallel", "arbitrary"),
    )(*parts, b, add)


def _swiglu_tiles(m, half):
    tn = _tile(half, (512, 256, 128))
    return _tile(m, (2048 if tn <= 256 else 1024, 1024, 512, 256, 128)), tn


def _ffn_in_swiglu(h, wt, half, name):
    m, k = h.shape
    tm, tn = _swiglu_tiles(m, half)
    nj = half // tn
    dims = (((1,), (1,)), ((), ()))

    def body(h_ref, wa_ref, wb_ref, act_ref, a_ref, b_ref):
        lhs = h_ref[...].astype(MXU_DTYPE)
        a = lax.dot_general(lhs, wa_ref[...].astype(MXU_DTYPE), dims, preferred_element_type=F32)
        b = lax.dot_general(lhs, wb_ref[...].astype(MXU_DTYPE), dims, preferred_element_type=F32)
        act_ref[...] = (_silu(a) * b).astype(act_ref.dtype)
        a_ref[...] = a.astype(a_ref.dtype)
        b_ref[...] = b.astype(b_ref.dtype)

    out = jax.ShapeDtypeStruct((m, half), MXU_DTYPE)
    oblk = pl.BlockSpec((tm, tn), lambda i, j: (i, j))
    return pl.pallas_call(
        body, name=name, grid=(m // tm, nj),
        in_specs=[pl.BlockSpec((tm, k), lambda i, j: (i, 0)), pl.BlockSpec((tn, k), lambda i, j: (j, 0)),
                  pl.BlockSpec((tn, k), lambda i, j: (j + nj, 0))],
        out_specs=[oblk, oblk, oblk], out_shape=[out, out, out], compiler_params=_cparams("parallel", "parallel"),
    )(h, wt, wt)


def _ffn_out_bwd_swiglu(dff, w, a, b, name):
    m, k = dff.shape
    half = w.shape[0]
    tm, tn = _swiglu_tiles(m, half)

    def body(d_ref, w_ref, a_ref, b_ref, da_ref, db_ref):
        dact = lax.dot_general(d_ref[...].astype(MXU_DTYPE), w_ref[...].astype(MXU_DTYPE), (((1,), (1,)), ((), ())),
                               preferred_element_type=F32)
        av, bv = a_ref[...].astype(F32), b_ref[...].astype(F32)
        sig = jax.nn.sigmoid(av)
        da_ref[...] = (dact * bv * (sig * (1.0 + av * (1.0 - sig)))).astype(da_ref.dtype)
        db_ref[...] = (dact * (av * sig)).astype(db_ref.dtype)

    out = jax.ShapeDtypeStruct((m, half), MXU_DTYPE)
    oblk = pl.BlockSpec((tm, tn), lambda i, j: (i, j))
    return pl.pallas_call(
        body, name=name, grid=(m // tm, half // tn),
        in_specs=[pl.BlockSpec((tm, k), lambda i, j: (i, 0)), pl.BlockSpec((tn, k), lambda i, j: (j, 0)), oblk, oblk],
        out_specs=[oblk, oblk], out_shape=[out, out], compiler_params=_cparams("parallel", "parallel"),
    )(dff, w, a, b)


def _with_off(xs):
    return [x if isinstance(x, tuple) else (x, 0) for x in xs]


def _spec(kind, arr, off, ts, wb):
    w = arr.shape[-1] if wb is None else wb
    col = (lambda j: 0) if wb is None else functools.partial(lambda j, o: o + j, o=off)
    if kind == "tok":
        return pl.BlockSpec((None, ts, w), lambda j, b, i: (b, i, col(j)))
    if kind == "bat":
        return pl.BlockSpec((None, 1, w), lambda j, b, i: (b, 0, col(j)))
    if off is None:
        return pl.BlockSpec(arr.shape, lambda j, b, i: (0, 0))
    return pl.BlockSpec((arr.shape[0], w), lambda j, b, i: (0, col(j)))


class _Product:
    def __init__(self, a, b, *, tb=False, b_rows=None, add=None):
        self.a, self.b, self.tb, self.b_rows, self.add = a, b, tb, b_rows, add
        rows = b.shape[0] if b_rows is None else b_rows[1]
        self.shape = a.shape[:2] + (rows if tb else b.shape[1],)

    def inputs(self, ts):
        a_spec = pl.BlockSpec((None, ts, self.a.shape[2]), lambda j, b, i: (b, i, 0))
        if self.b_rows is None:
            b_spec = pl.BlockSpec(self.b.shape, lambda j, b, i: (0, 0))
        else:
            start, count = self.b_rows
            b_spec = pl.BlockSpec((pl.Element(count), pl.Element(self.b.shape[1])), lambda j, b, i: (start, 0))
        if isinstance(self.add, _Product):
            extra = self.add.inputs(ts)
        else:
            extra = [] if self.add is None else [(self.add, pl.BlockSpec((None, ts, self.shape[2]), lambda j, b, i: (b, i, 0)))]
        return [(self.a, a_spec), (self.b, b_spec)] + extra

    def value(self, refs):
        dims = (((1,), (1 if self.tb else 0,)), ((), ()))
        val = lax.dot_general(refs[0][...].astype(MXU_DTYPE), refs[1][...].astype(MXU_DTYPE), dims, preferred_element_type=F32)
        if isinstance(self.add, _Product):
            return val + self.add.value(refs[2:])
        return val if self.add is None else val + refs[2][...].astype(F32)


def _inputs(groups, kinds, ts, wb):
    loaded = [(a, _spec(kind, a, o, ts, wb)) for g, kind in zip(groups, kinds) for a, o in g if not isinstance(a, _Product)]
    made = [pair for g in groups for a, _ in g if isinstance(a, _Product) for pair in a.inputs(ts)]
    return [a for a, _ in loaded + made], [sp for _, sp in loaded + made]


def _values(refs, groups):
    n_loaded = sum(1 for g in groups for a, _ in g if not isinstance(a, _Product))
    loaded, pos, out = iter(refs[:n_loaded]), n_loaded, []
    for g in groups:
        vals = []
        for a, _ in g:
            if isinstance(a, _Product):
                k = len(a.inputs(1))
                vals.append(a.value(refs[pos:pos + k]))
                pos += k
            else:
                vals.append(next(loaded)[...].astype(F32))
        out.append(vals)
    return out, pos


def _tok_fwd(fn, toks, bats, pars, outs, *, name, ts, wb=None, cols=1):
    groups = [_with_off(toks), _with_off(bats), _with_off(pars)]
    bl, s, _ = groups[0][0][0].shape
    ts = min(ts, s)
    args, in_specs = _inputs(groups, ("tok", "bat", "par"), ts, wb)

    def body(*refs):
        vals, n_in = _values(refs, groups)
        res = fn(*[v for g in vals for v in g])
        for r, val in zip(refs[n_in:], res):
            r[...] = val.astype(r.dtype)

    out_specs = [pl.BlockSpec((None, ts, w if wb is None else wb), lambda j, b, i: (b, i, j)) for w, _ in outs]
    return pl.pallas_call(
        body, name=name, grid=(cols, bl, s // ts), in_specs=in_specs,
        out_specs=out_specs, out_shape=[jax.ShapeDtypeStruct((bl, s, w), dt) for w, dt in outs],
        compiler_params=_cparams("parallel", "parallel", "parallel"),
    )(*args)


def _accumulate(ref, val, first):
    @pl.when(first)
    def _():
        ref[...] = val

    @pl.when(jnp.logical_not(first))
    def _():
        ref[...] += val


def _tok_bwd(fn, toks, bats, pars, cots, need, *, name, ts, wb=None, cols=1, tok_dtype=F32, loss=False, after=()):
    toks, bats, pars, cots = _with_off(toks), _with_off(bats), _with_off(pars), _with_off(cots)
    groups = [toks, bats, pars, cots]
    bl, s, _ = toks[0][0].shape
    ts = min(ts, s)
    nt, nb, npar = len(toks), len(bats), len(pars)
    args, in_specs = _inputs(groups, ("tok", "bat", "par", "tok"), ts, wb)
    args, in_specs = args + list(after), in_specs + [pl.BlockSpec(memory_space=pl.ANY)] * len(after)

    def body(*refs):
        j, b, i = pl.program_id(0), pl.program_id(1), pl.program_id(2)
        (tok_vals, bat_vals, par_vals, cot_vals), o = _values(refs, groups)
        o += len(after)
        outs, vjp = jax.vjp(fn, *tok_vals, *bat_vals, *par_vals)
        if loss:
            ct = (jnp.ones_like(outs[0]),)
            tot = jnp.broadcast_to(jnp.sum(outs[0], keepdims=True), (1, LANE))
            _accumulate(refs[o], tot, jnp.logical_and(b == 0, i == 0))
            o += 1
        else:
            ct = tuple(cot_vals)
        grads = vjp(ct)
        for t in range(nt):
            if need[t]:
                refs[o][...] = grads[t].astype(refs[o].dtype)
                o += 1
        for t in range(nb):
            _accumulate(refs[o], grads[nt + t], i == 0)
            o += 1
        for t in range(npar):
            first = jnp.logical_and(b == 0, i == 0)
            if pars[t][1] is None:
                first = jnp.logical_and(first, j == 0)
            _accumulate(refs[o], grads[nt + nb + t], first)
            o += 1

    full = lambda arr: arr.shape[-1] if wb is None else wb * cols
    blk = lambda arr: arr.shape[-1] if wb is None else wb
    out_specs, out_shape = [], []
    if loss:
        out_specs.append(pl.BlockSpec((1, LANE), lambda j, b, i: (0, 0)))
        out_shape.append(jax.ShapeDtypeStruct((1, LANE), F32))
    for t in range(nt):
        if need[t]:
            out_specs.append(pl.BlockSpec((None, ts, blk(toks[t][0])), lambda j, b, i: (b, i, j)))
            dt = tok_dtype[t] if isinstance(tok_dtype, (list, tuple)) else tok_dtype
            out_shape.append(jax.ShapeDtypeStruct((bl, s, full(toks[t][0])), dt))
    for arr, _ in bats:
        out_specs.append(pl.BlockSpec((None, 1, blk(arr)), lambda j, b, i: (b, 0, j)))
        out_shape.append(jax.ShapeDtypeStruct((bl, 1, full(arr)), F32))
    for arr, off in pars:
        if off is None:
            out_specs.append(pl.BlockSpec(arr.shape, lambda j, b, i: (0, 0)))
            out_shape.append(jax.ShapeDtypeStruct(arr.shape, F32))
        else:
            out_specs.append(pl.BlockSpec((arr.shape[0], blk(arr)), lambda j, b, i: (0, j)))
            out_shape.append(jax.ShapeDtypeStruct((arr.shape[0], full(arr)), F32))
    res = list(pl.pallas_call(
        body, name=name, grid=(cols, bl, s // ts), in_specs=in_specs,
        out_specs=out_specs, out_shape=out_shape, compiler_params=_cparams("arbitrary", "arbitrary", "arbitrary"),
    )(*args))
    tot = res.pop(0) if loss else None
    dtoks = [res.pop(0) if need[t] else None for t in range(nt)]
    dbats = [res.pop(0) for _ in range(nb)]
    dpars = [res.pop(0) for _ in range(npar)]
    return (tot, dtoks, dbats, dpars) if loss else (dtoks, dbats, dpars)


def _silu(x):
    return x * jax.nn.sigmoid(x)


def _rms(x, w):
    return x * lax.rsqrt(jnp.mean(x * x, axis=-1, keepdims=True) + EPS) * w


def _f_norm_mod(x, shift, scale, w):
    return (_rms(x, w) * (1.0 + scale) + shift,)


def _f_norm_mod_skip(x, shift, scale, w):
    return _rms(x, w) * (1.0 + scale) + shift, x


def _f_res_norm_mod(x, mix, gate, shift, scale, w):
    x2 = x + gate * mix
    return x2, _rms(x2, w) * (1.0 + scale) + shift


def _f_res_norm_mod_keep(x, mix, gate, shift, scale, w):
    return (*_f_res_norm_mod(x, mix, gate, shift, scale, w), mix)


def _f_gates(p, a_log, dt_bias, *, heads):
    z = p + dt_bias
    g = -jnp.exp(a_log) * (jnp.maximum(z, 0.0) + jnp.log1p(jnp.exp(jnp.minimum(z, -z))))
    lane = lax.broadcasted_iota(jnp.int32, p.shape, 1)
    return (jnp.where(lane < heads, g, jax.nn.sigmoid(p)),)


def _f_gdn_out(o, z, w):
    return (_rms(o, w) * _silu(z),)


def _f_merge(ga, gb, ya, yb):
    return (jax.nn.sigmoid(ga) * ya + jax.nn.sigmoid(gb) * yb,)


def _f_merge_keep(ga, gb, ya, yb):
    return (*_f_merge(ga, gb, ya, yb), yb)


def _f_loss(x2, ff, tgt, gate, shift, scale, w):
    y = _rms(x2 + gate * ff, w) * (1.0 + scale) + shift
    return (0.5 * jnp.mean(jnp.square(y - tgt), axis=-1, keepdims=True),)


def _shift_down(x, s):
    if s == 0:
        return x
    row = lax.broadcasted_iota(jnp.int32, x.shape, 0)
    return jnp.where(row >= s, pltpu.roll(x, s, 0), 0.0)


def _shift_up(x, s):
    if s == 0:
        return x
    n = x.shape[0]
    row = lax.broadcasted_iota(jnp.int32, x.shape, 0)
    return jnp.where(row < n - s, pltpu.roll(x, n - s, 0), 0.0)


def _conv(x, w):
    width = w.shape[0]
    acc = w[width - 1:width, :] * x
    for j in range(width - 1):
        acc = acc + w[j:j + 1, :] * _shift_down(x, width - 1 - j)
    return acc


def _conv_bwd(dy, x, w, dw_ref, first):
    width = w.shape[0]
    dx = w[width - 1:width, :] * dy
    for j in range(width - 1):
        dx = dx + w[j:j + 1, :] * _shift_up(dy, width - 1 - j)
    for j in range(width):
        row = jnp.sum(dy * _shift_down(x, width - 1 - j), axis=0, keepdims=True)
        _accumulate(dw_ref.at[j:j + 1, :], row, first)
    return dx


def _qkv_act(xc, is_v, scale):
    a = _silu(xc)
    nrm = a * lax.rsqrt(jnp.sum(a * a, axis=-1, keepdims=True) + EPS) * scale
    return jnp.where(is_v, a, nrm)


def _qkv_act_bwd(xc, dout, is_v, scale):
    sig = jax.nn.sigmoid(xc)
    a = xc * sig
    r = lax.rsqrt(jnp.sum(a * a, axis=-1, keepdims=True) + EPS)
    c1 = r * scale
    da = c1 * dout - a * (c1 * r * r * jnp.sum(dout * a, axis=-1, keepdims=True))
    return jnp.where(is_v, dout, da) * (sig * (1.0 + xc * (1.0 - sig)))


def _qkv_consts(j, heads):
    is_v = j >= 2 * heads
    scale = jnp.where(j < heads, HEAD ** -0.5, 1.0).astype(F32)
    return is_v, scale


def _qkv_fwd(p, w, heads, name):
    bl, s, w3 = p.shape

    def body(p_ref, w_ref, o_ref):
        is_v, scale = _qkv_consts(pl.program_id(0), heads)
        o_ref[...] = _qkv_act(_conv(p_ref[...], w_ref[...]), is_v, scale)

    blk = pl.BlockSpec((None, s, HEAD), lambda j, b: (b, 0, j))
    return pl.pallas_call(
        body, name=name, grid=(w3 // HEAD, bl), in_specs=[blk, pl.BlockSpec((w.shape[0], HEAD), lambda j, b: (0, j))],
        out_specs=blk, out_shape=jax.ShapeDtypeStruct(p.shape, F32), compiler_params=_cparams("parallel", "parallel"),
    )(p, w)


def _qkv_bwd(p, w, dout, heads, name):
    bl, s, w3 = p.shape

    def body(p_ref, w_ref, d_ref, dp_ref, dw_ref):
        is_v, scale = _qkv_consts(pl.program_id(0), heads)
        x, wv = p_ref[...], w_ref[...]
        dxc = _qkv_act_bwd(_conv(x, wv), d_ref[...], is_v, scale)
        dp_ref[...] = _conv_bwd(dxc, x, wv, dw_ref, pl.program_id(1) == 0).astype(dp_ref.dtype)

    blk = pl.BlockSpec((None, s, HEAD), lambda j, b: (b, 0, j))
    wblk = pl.BlockSpec((w.shape[0], HEAD), lambda j, b: (0, j))
    return pl.pallas_call(
        body, name=name, grid=(w3 // HEAD, bl), in_specs=[blk, wblk, blk], out_specs=[blk, wblk],
        out_shape=[jax.ShapeDtypeStruct(p.shape, MXU_DTYPE), jax.ShapeDtypeStruct(w.shape, F32)],
        compiler_params=_cparams("arbitrary", "arbitrary"),
    )(p, w, dout)


def _sc_specs(p, w):
    bl, s, w3 = p.shape
    nblk = w3 // 3 // LANE
    sec = lambda k: pl.BlockSpec((None, s, LANE), functools.partial(lambda j, b, k: (b, 0, k * nblk + j), k=k))
    return nblk, [sec(0), sec(1), sec(2)], pl.BlockSpec((w.shape[0], LANE), lambda j, b: (0, j)), \
        pl.BlockSpec((None, s, LANE), lambda j, b: (b, 0, j))


def _sc_fwd(p, w, name):
    bl, s, w3 = p.shape
    nblk, secs, wblk, oblk = _sc_specs(p, w)

    def body(b_ref, c_ref, x_ref, w_ref, o_ref):
        o_ref[...] = (b_ref[...] * _conv(c_ref[...] * x_ref[...], w_ref[...])).astype(o_ref.dtype)

    return pl.pallas_call(
        body, name=name, grid=(nblk, bl), in_specs=secs + [wblk], out_specs=oblk,
        out_shape=jax.ShapeDtypeStruct((bl, s, w3 // 3), MXU_DTYPE), compiler_params=_cparams("parallel", "parallel"),
    )(p, p, p, w)


def _sc_bwd(p, w, dout, name):
    bl, s, w3 = p.shape
    nblk, secs, wblk, oblk = _sc_specs(p, w)

    def body(b_ref, c_ref, x_ref, w_ref, d_ref, dp_ref, dw_ref):
        gb, gc, xin, wv, d = b_ref[...], c_ref[...], x_ref[...], w_ref[...], d_ref[...]
        u = gc * xin
        dp_ref[0] = (d * _conv(u, wv)).astype(dp_ref.dtype)
        du = _conv_bwd(d * gb, u, wv, dw_ref, pl.program_id(1) == 0)
        dp_ref[1] = (du * xin).astype(dp_ref.dtype)
        dp_ref[2] = (du * gc).astype(dp_ref.dtype)

    return pl.pallas_call(
        body, name=name, grid=(nblk, bl), in_specs=secs + [wblk, oblk],
        out_specs=[pl.BlockSpec((3, None, s, LANE), lambda j, b: (0, b, 0, j)), wblk],
        out_shape=[jax.ShapeDtypeStruct((3, bl, s, w3 // 3), MXU_DTYPE), jax.ShapeDtypeStruct(w.shape, F32)],
        compiler_params=_cparams("arbitrary", "arbitrary"),
    )(p, p, p, w, dout)


def _bdot(a, b, ca, cb):
    return lax.dot_general(a.astype(MXU_DTYPE), b.astype(MXU_DTYPE), (((ca,), (cb,)), ((), ())),
                           preferred_element_type=F32)


def _hdot(a, b):
    return lax.dot_general(a, b, (((1,), (0,)), ((), ())), precision=HIGHEST, preferred_element_type=F32)


def _lane_col(x, idx):
    lane = lax.broadcasted_iota(jnp.int32, x.shape, 1)
    return jnp.sum(jnp.where(lane == idx, x, 0.0), axis=1, keepdims=True)


def _chunk_masks():
    r = lax.broadcasted_iota(jnp.int32, (CHUNK, CHUNK), 0)
    c = lax.broadcasted_iota(jnp.int32, (CHUNK, CHUNK), 1)
    return r == c, r >= c, r > c


def _dot3(a, b):
    ah, bh = a.astype(MXU_DTYPE), b.astype(MXU_DTYPE)
    al, bl = (a - ah.astype(F32)).astype(MXU_DTYPE), (b - bh.astype(F32)).astype(MXU_DTYPE)
    dot = lambda x, y: lax.dot_general(x, y, (((1,), (0,)), ((), ())), preferred_element_type=F32)
    return dot(ah, bh) + (dot(ah, bl) + dot(al, bh))


def _tri_inv_steps(low, eye):
    x = -low
    p = jnp.where(eye, 1.0, 0.0) + x
    span = 2
    while span < CHUNK:
        x = _dot3(x, x)
        yield
        p = p + _dot3(p, x)
        yield
        span *= 2
    return p


def _round_robin(gens):
    out, live = [None] * len(gens), list(range(len(gens)))
    while live:
        still = []
        for i in live:
            try:
                next(gens[i])
                still.append(i)
            except StopIteration as stop:
                out[i] = stop.value
        live = still
    return out


def _gdn_pre(q, k, v, gc, beta, masks):
    eye, causal, strict = masks
    gc_row = jnp.sum(jnp.where(eye, gc, 0.0), axis=0, keepdims=True)
    decay = jnp.where(causal, jnp.exp(jnp.where(causal, gc - gc_row, 0.0)), 0.0)
    eg = jnp.exp(gc)
    gl = gc[CHUNK - 1:CHUNK, :]
    kb, vb = k * beta, v * beta
    both = _bdot(jnp.concatenate([kb, q], axis=0), k, 1, 1)
    low = jnp.where(strict, both[:CHUNK] * decay, 0.0)
    qk = jnp.where(causal, both[CHUNK:] * decay, 0.0)
    rest = jnp.exp(gl - gc)
    return dict(decay=decay, eg=eg, gl=gl, kb=kb, vb=vb, kbe=kb * eg, low=low, qk=qk, qg=q * eg, rest=rest, kdec=k * rest)


GROUP = 4


def _gdn_specs(qkv, gbeta, heads, rev):
    bl, s, w3 = qkv.shape
    d, n = w3 // 3, s // CHUNK
    group = GROUP if n % GROUP == 0 else 1
    steps = n // group
    at = (lambda c: steps - 1 - c) if rev else (lambda c: c)
    assert d == heads * HEAD
    rows = group * CHUNK
    sec = pl.BlockSpec((None, rows, w3), lambda b, c: (b, at(c), 0))
    gspec = pl.BlockSpec((None, rows, LANE), lambda b, c: (b, at(c), 0))
    ospec = pl.BlockSpec((None, rows, d), lambda b, c: (b, at(c), 0))
    sspec = pl.BlockSpec((None, group, heads, HEAD, HEAD), lambda b, c: (b, at(c), 0, 0, 0))
    tspec = pl.BlockSpec((None, group, heads, CHUNK, CHUNK), lambda b, c: (b, at(c), 0, 0, 0))
    return bl, s, d, n, group, sec, gspec, ospec, sspec, tspec


def _gdn_fwd(qkv, gbeta, heads, name):
    bl, s, d, n, group, sec, gspec, ospec, sspec, tspec = _gdn_specs(qkv, gbeta, heads, False)
    rows = lambda sub: slice(sub * CHUNK, (sub + 1) * CHUNK)
    pairs = [(h, sub) for h in range(heads) for sub in range(group)]

    def body(x_ref, g_ref, o_ref, s_ref, t_ref, st_ref):
        @pl.when(pl.program_id(1) == 0)
        def _():
            st_ref[...] = jnp.zeros_like(st_ref)

        masks = _chunk_masks()
        eye, causal, _ = masks
        gblks = [g_ref[rows(sub), :] for sub in range(group)]
        gcs = [_hdot(jnp.where(causal, 1.0, 0.0), gb) for gb in gblks]
        st_all = st_ref[...]

        def free(h, sub):
            q, k, v = (x_ref[rows(sub), sec * d + h * HEAD:sec * d + (h + 1) * HEAD] for sec in range(3))
            pre = _gdn_pre(q, k, v, _lane_col(gcs[sub], h), _lane_col(gblks[sub], heads + h), masks)
            yield
            t = yield from _tri_inv_steps(pre["low"], eye)
            uw = _bdot(t, jnp.concatenate([pre["vb"], pre["kbe"]], axis=1), 1, 0)
            return pre, t, uw[:, :HEAD], uw[:, HEAD:]

        pieces = dict(zip(pairs, _round_robin([free(h, sub) for h, sub in pairs])))

        def carry(h):
            st, outs, starts = st_all[h], [], []
            for sub in range(group):
                pre, _, u, w = pieces[h, sub]
                starts.append(st)
                vnew = u - _bdot(w, st, 1, 0)
                yield
                outs.append(_bdot(pre["qg"], st, 1, 0) + _bdot(pre["qk"], vnew, 1, 0))
                st = st * jnp.exp(pre["gl"]) + _bdot(pre["kdec"], vnew, 0, 0)
                yield
            return outs, starts, st

        carried = _round_robin([carry(h) for h in range(heads)])
        per_sub = lambda pick: [[pick(h, sub) for h in range(heads)] for sub in range(group)]
        o_ref[...] = jnp.concatenate([jnp.concatenate(r, axis=1) for r in per_sub(lambda h, sub: carried[h][0][sub])], axis=0)
        s_ref[...] = jnp.stack([jnp.stack(r) for r in per_sub(lambda h, sub: carried[h][1][sub])])
        t_ref[...] = jnp.stack([jnp.stack(r) for r in per_sub(lambda h, sub: pieces[h, sub][1])])
        st_ref[...] = jnp.stack([carried[h][2] for h in range(heads)])

    return pl.pallas_call(
        body, name=name, grid=(bl, n // group), in_specs=[sec, gspec], out_specs=[ospec, sspec, tspec],
        out_shape=[jax.ShapeDtypeStruct((bl, s, d), F32), jax.ShapeDtypeStruct((bl, n, heads, HEAD, HEAD), F32),
                   jax.ShapeDtypeStruct((bl, n, heads, CHUNK, CHUNK), F32)],
        scratch_shapes=[pltpu.VMEM((heads, HEAD, HEAD), F32)], compiler_params=_cparams("parallel", "arbitrary"),
    )(qkv, gbeta)


def _gdn_bwd(qkv, gbeta, dout, s_all, t_all, heads, name):
    bl, s, d, n, group, sec, gspec, ospec, sspec, tspec = _gdn_specs(qkv, gbeta, heads, True)
    rows = lambda sub: slice(sub * CHUNK, (sub + 1) * CHUNK)
    pairs = [(h, sub) for h in range(heads) for sub in range(group)]
    stack, side = functools.partial(jnp.concatenate, axis=0), functools.partial(jnp.concatenate, axis=1)

    def body(x_ref, g_ref, do_ref, s_ref, t_ref, dx_ref, dg_ref, ds_ref):
        @pl.when(pl.program_id(1) == 0)
        def _():
            ds_ref[...] = jnp.zeros_like(ds_ref)

        masks = _chunk_masks()
        eye, causal, strict = masks
        gblks = [g_ref[rows(sub), :] for sub in range(group)]
        gcs = [_hdot(jnp.where(causal, 1.0, 0.0), gb) for gb in gblks]
        lane = lax.broadcasted_iota(jnp.int32, (CHUNK, LANE), 1)
        last_row = lax.broadcasted_iota(jnp.int32, (CHUNK, 1), 0) == CHUNK - 1
        rowsum = lambda a: jnp.sum(a, axis=1, keepdims=True)
        st_all, t_all_, ds_all = s_ref[...], t_ref[...], ds_ref[...]

        def free(h, sub):
            q, k, v = (x_ref[rows(sub), sec * d + h * HEAD:sec * d + (h + 1) * HEAD] for sec in range(3))
            do = do_ref[rows(sub), h * HEAD:(h + 1) * HEAD]
            beta = _lane_col(gblks[sub], heads + h)
            st, t = st_all[sub, h], t_all_[sub, h]
            pre = _gdn_pre(q, k, v, _lane_col(gcs[sub], h), beta, masks)
            yield
            uw = _bdot(t, side([pre["vb"], pre["kbe"]]), 1, 0)
            u, w = uw[:, :HEAD], uw[:, HEAD:]
            yield
            vnew = u - _bdot(w, st, 1, 0)
            yield
            dqk = jnp.where(causal, _bdot(do, vnew, 1, 1), 0.0)
            dqg = _bdot(do, st, 1, 1)
            return dict(q=q, k=k, v=v, do=do, beta=beta, st=st, t=t, pre=pre, w=w, vnew=vnew, dqk=dqk, dqg=dqg)

        pieces = dict(zip(pairs, _round_robin([free(h, sub) for h, sub in pairs])))

        def carry(h):
            dsn, outs = ds_all[h], {}
            for sub in reversed(range(group)):
                pc = pieces[h, sub]
                pre, st, do = pc["pre"], pc["st"], pc["do"]
                egl = jnp.exp(pre["gl"])
                dkdec = _bdot(pc["vnew"], dsn, 1, 1)
                dvnew = _bdot(pre["kdec"], dsn, 1, 0) + _bdot(pre["qk"], do, 0, 0)
                dgl = jnp.sum(dsn * st, keepdims=True) * egl
                yield
                dw = -_bdot(dvnew, st, 1, 1)
                dsn = dsn * egl + _bdot(stack([pre["qg"], -pc["w"]]), stack([do, dvnew]), 0, 0)
                outs[sub] = (dkdec, dvnew, dgl, dw)
                yield
            return outs, dsn

        carried = _round_robin([carry(h) for h in range(heads)])

        def rest(h, sub):
            pc = pieces[h, sub]
            dkdec, dvnew, dgl, dw = carried[h][0][sub]
            q, k, v, beta, t, pre, dqk, dqg = (pc[x] for x in ("q", "k", "v", "beta", "t", "pre", "dqk", "dqg"))
            decay, eg, kb, vb, kbe, low, qk, qg, kdec = (pre[x] for x in ("decay", "eg", "kb", "vb", "kbe", "low", "qk", "qg", "kdec"))
            dt = _bdot(side([dvnew, dw]), side([vb, kbe]), 1, 1)
            by_t = _bdot(t, side([dvnew, dw]), 0, 0)
            dvb, dkbe = by_t[:, :HEAD], by_t[:, HEAD:]
            yield
            inner = _bdot(dt, t, 1, 1)
            yield
            dlow = -jnp.where(strict, _bdot(t, inner, 0, 0), 0.0)
            da, db = dlow * decay, dqk * decay
            yield
            m = dlow * low + dqk * qk
            kdk = dkdec * kdec
            col_of_m = jnp.sum(jnp.where(eye, jnp.sum(m, axis=0, keepdims=True), 0.0), axis=1, keepdims=True)
            dgc = rowsum(m) - col_of_m + rowsum(dqg * qg) + rowsum(dkbe * kbe) - rowsum(kdk)
            dgc = dgc + jnp.where(last_row, dgl + jnp.sum(kdk, keepdims=True), 0.0)
            by_k = _bdot(stack([da, db]), k, 1, 0)
            dkb = by_k[:CHUNK] + dkbe * eg
            yield
            dk = _bdot(stack([da, db]), stack([kb, q]), 0, 0) + dkdec * pre["rest"] + dkb * beta
            dq = by_k[CHUNK:] + dqg * eg
            dbeta = rowsum(dkb * k) + rowsum(dvb * v)
            return dq, dk, dvb * beta, jnp.where(lane == h, dgc, 0.0) + jnp.where(lane == heads + h, dbeta, 0.0)

        done = dict(zip(pairs, _round_robin([rest(h, sub) for h, sub in pairs])))
        dx_ref[...] = stack([side([done[h, sub][i] for i in range(3) for h in range(heads)]) for sub in range(group)])
        ds_ref[...] = jnp.stack([carried[h][1] for h in range(heads)])
        upper = jnp.where(jnp.logical_or(eye, jnp.logical_not(causal)), 1.0, 0.0)
        dgs = []
        for sub in range(group):
            dgb = done[0, sub][3]
            for h in range(1, heads):
                dgb = dgb + done[h, sub][3]
            dgs.append(jnp.where(lane < heads, _hdot(upper, dgb), dgb))
        dg_ref[...] = stack(dgs)

    return pl.pallas_call(
        body, name=name, grid=(bl, n // group), in_specs=[sec, gspec, ospec, sspec, tspec], out_specs=[sec, gspec],
        out_shape=[jax.ShapeDtypeStruct(qkv.shape, F32), jax.ShapeDtypeStruct((bl, s, LANE), F32)],
        scratch_shapes=[pltpu.VMEM((heads, HEAD, HEAD), F32)], compiler_params=_cparams("parallel", "arbitrary"),
    )(qkv, gbeta, dout, s_all, t_all)


def _position():
    return lax.axis_index("x"), lax.axis_index("y"), lax.axis_index("c")


def _all_gather(x, *, name):
    space = pltpu.VMEM

    def body(x_ref, out_ref, send_sems, recv_sems, local_sem):
        ax, ay, ac = _position()
        me, sibling = (ax, ay, ac), (ax, ay, 1 - ac)
        chips = [(1 - ax, ay), (ax, 1 - ay), (1 - ax, 1 - ay)]

        def slot(px, py, pc):
            return out_ref.at[4 * px + 2 * py + pc]

        def copy(k, block, to, src=None):
            return pltpu.make_async_remote_copy(
                src_ref=slot(*block) if src is None else src, dst_ref=slot(*block), send_sem=send_sems.at[k],
                recv_sem=recv_sems.at[k], device_id=to, device_id_type=MESH_IDS)

        mine = pltpu.make_async_copy(x_ref, slot(*me), local_sem)
        mine.start()
        first = [copy(0, me, sibling, src=x_ref)] + [copy(1 + j, me, (*chip, ac), src=x_ref) for j, chip in enumerate(chips)]
        for cp in first:
            cp.start()
        passed = [copy(4 + j, (*chip, ac), sibling) for j, chip in enumerate(chips)]
        for j, chip in enumerate(chips):
            copy(1 + j, (*chip, ac), me).wait_recv()
            passed[j].start()
        copy(0, sibling, me).wait_recv()
        for j, chip in enumerate(chips):
            copy(4 + j, (*chip, 1 - ac), me).wait_recv()
        for cp in first + passed:
            cp.wait_send()
        mine.wait()

    return pl.pallas_call(
        body, name=name, out_shape=jax.ShapeDtypeStruct((NDEV,) + x.shape, x.dtype),
        in_specs=[pl.BlockSpec(memory_space=space)], out_specs=pl.BlockSpec(memory_space=space),
        scratch_shapes=[pltpu.SemaphoreType.DMA((7,)), pltpu.SemaphoreType.DMA((7,)), pltpu.SemaphoreType.DMA],
    )(x)


class _Rider:
    def __init__(self, arrays, out_shapes, sems, hooks):
        self.arrays, self.out_shapes, self.sems, self.hooks = arrays, out_shapes, sems, hooks


def _gather_rider(xs):
    n = len(xs)

    def hooks(x_refs, out_refs, send_sems, recv_sems):
        ax, ay, ac = _position()
        me, sibling = (ax, ay, ac), (ax, ay, 1 - ac)
        chips = [(1 - ax, ay), (ax, 1 - ay), (1 - ax, 1 - ay)]

        def copies(k, block, to, own=False):
            out = []
            for i in range(n):
                slot = out_refs[i].at[4 * block[0] + 2 * block[1] + block[2]]
                out.append(pltpu.make_async_remote_copy(
                    src_ref=x_refs[i] if own else slot, dst_ref=slot, send_sem=send_sems.at[k, i], recv_sem=recv_sems.at[k, i],
                    device_id=to, device_id_type=MESH_IDS))
            return out

        def first():
            for cp in copies(0, me, sibling, own=True):
                cp.start()
            for j, chip in enumerate(chips):
                for cp in copies(1 + j, me, (*chip, ac), own=True):
                    cp.start()

        def mid():
            for j, chip in enumerate(chips):
                for arrived, onward in zip(copies(1 + j, (*chip, ac), me), copies(4 + j, (*chip, ac), sibling)):
                    arrived.wait_recv()
                    onward.start()

        def last():
            for cp in copies(0, sibling, me):
                cp.wait_recv()
            for j, chip in enumerate(chips):
                for cp in copies(4 + j, (*chip, 1 - ac), me):
                    cp.wait_recv()
            for cp in copies(0, me, sibling, own=True):
                cp.wait_send()
            for j, chip in enumerate(chips):
                for cp in copies(1 + j, me, (*chip, ac), own=True) + copies(4 + j, (*chip, ac), sibling):
                    cp.wait_send()

        return first, mid, last

    return _Rider(list(xs), [jax.ShapeDtypeStruct((NDEV,) + x.shape, x.dtype) for x in xs],
                  [pltpu.SemaphoreType.DMA((7, n)), pltpu.SemaphoreType.DMA((7, n))], hooks)


def _scatter_rider(parts):
    packed = sum(r for _, r in parts)
    width, dtype = parts[0][0].shape[1], parts[0][0].dtype

    def hooks(g_refs, out_refs, send_sems, recv_sems):
        (recv_ref,) = out_refs
        ax, ay, ac = _position()

        def peer(rel):
            flip = lambda a, bit: 1 - a if rel & bit else a
            return flip(ax, 4), flip(ay, 2), flip(ac, 1)

        def first():
            for rel in range(1, NDEV):
                px, py, pc = peer(rel)
                off = 0
                for g_ref, (_, r) in zip(g_refs, parts):
                    rows = g_ref.at[pl.ds(pl.multiple_of((4 * px + 2 * py + pc) * r, ROW_ALIGN), r)]
                    pltpu.make_async_remote_copy(
                        src_ref=rows, dst_ref=recv_ref.at[rel - 1, pl.ds(off, r)], send_sem=send_sems.at[rel - 1],
                        recv_sem=recv_sems.at[rel - 1], device_id=(px, py, pc), device_id_type=MESH_IDS).start()
                    off += r

        def last():
            for rel in range(1, NDEV):
                slot = recv_ref.at[rel - 1]
                pltpu.make_async_remote_copy(src_ref=slot, dst_ref=slot, send_sem=send_sems.at[rel - 1],
                                             recv_sem=recv_sems.at[rel - 1], device_id=peer(rel), device_id_type=MESH_IDS).wait()

        return first, lambda: None, last

    return _Rider([g for g, _ in parts], [jax.ShapeDtypeStruct((NDEV - 1, packed, width), dtype)],
                  [pltpu.SemaphoreType.DMA((NDEV - 1,)), pltpu.SemaphoreType.DMA((NDEV - 1,))], hooks)


def _sum_direct(own, recv, name):
    r, w = own.shape
    tr = max(t for t in range(ROW_ALIGN, 257, ROW_ALIGN) if r % t == 0)

    def body(own_ref, *refs):
        acc = own_ref[...].astype(F32)
        for ref in refs[:-1]:
            acc = acc + ref[...].astype(F32)
        refs[-1][...] = acc

    rblk = lambda k: pl.BlockSpec((None, tr, w), functools.partial(lambda i, k: (k, i, 0), k=k))
    blk = pl.BlockSpec((tr, w), lambda i: (i, 0))
    return pl.pallas_call(body, name=name, grid=(r // tr,), in_specs=[blk] + [rblk(k) for k in range(NDEV - 1)],
                          out_specs=blk, out_shape=jax.ShapeDtypeStruct((r, w), F32),
                          compiler_params=_cparams("parallel"))(own, *([recv] * (NDEV - 1)))


ROW_ALIGN = 16


def _window_start(rows_per_dev, k):
    return rows_per_dev * k // ROW_ALIGN * ROW_ALIGN


def _exchange_in_chip(parts, name, collective_id):
    packed = sum(win for _, _, win, _ in parts)
    width, dtype = parts[0][0].shape[1], parts[0][0].dtype

    def body(g_refs, out_refs, send_sems, recv_sems):
        (recv_ref,) = out_refs
        ax, ay, ac = _position()
        sibling = (ax, ay, 1 - ac)
        _handshake([sibling])
        for q in range(4):
            for g_ref, (_, r, win, off) in zip(g_refs, parts):
                there = g_ref.at[pl.ds(pl.multiple_of(_window_start(r, 2 * q + 1 - ac), ROW_ALIGN), win)]
                pltpu.make_async_remote_copy(src_ref=there, dst_ref=recv_ref.at[q, pl.ds(off, win)], send_sem=send_sems.at[q],
                                             recv_sem=recv_sems.at[q], device_id=sibling, device_id_type=MESH_IDS).start()
        for q in range(4):
            pltpu.make_async_remote_copy(src_ref=recv_ref.at[q], dst_ref=recv_ref.at[q], send_sem=send_sems.at[q],
                                         recv_sem=recv_sems.at[q], device_id=sibling, device_id_type=MESH_IDS).wait()

    return _on_sequencer(body, [g for g, _, _, _ in parts], [jax.ShapeDtypeStruct((4, packed, width), dtype)],
                         [pltpu.SemaphoreType.DMA((4,)), pltpu.SemaphoreType.DMA((4,))], name=name, collective_id=collective_id)[0]


def _on_sequencer(body, ins, out_shapes, sems, *, name, collective_id):
    hbm = pltpu.MemorySpace.HBM
    in_refs = [jax.new_ref(a, memory_space=hbm) for a in ins]
    out_refs = [jax.empty_ref(s, memory_space=hbm) for s in out_shapes]

    @pl.kernel(mesh=plsc.ScalarSubcoreMesh(axis_name="sequencer", num_cores=1), name=name, scratch_types=tuple(sems),
               compiler_params=pltpu.CompilerParams(collective_id=collective_id))
    def launch(*sem_refs):
        body(in_refs, out_refs, *sem_refs)

    launch()
    return [r[...] for r in out_refs]


def _handshake(peers):
    barrier = pltpu.get_barrier_semaphore()
    for peer in peers:
        pl.semaphore_signal(barrier, inc=1, device_id=peer, device_id_type=MESH_IDS)
    pl.semaphore_wait(barrier, len(peers))


def _exchange_chips_async(s1, name, collective_id):
    def body(in_refs, out_refs, send_sems, recv_sems):
        (src,), (got,) = in_refs, out_refs
        ax, ay, ac = _position()
        chips = [(1 - ax, ay), (ax, 1 - ay), (1 - ax, 1 - ay)]
        _handshake([(cx, cy, ac) for cx, cy in chips])
        copies = [pltpu.make_async_remote_copy(
            src_ref=src.at[2 * cx + cy], dst_ref=got.at[r], send_sem=send_sems.at[r], recv_sem=recv_sems.at[r],
            device_id=(cx, cy, ac), device_id_type=MESH_IDS) for r, (cx, cy) in enumerate(chips)]
        for cp in copies:
            cp.start()
        for cp in copies:
            cp.wait_recv()
        for cp in copies:
            cp.wait_send()

    return _on_sequencer(body, [s1], [jax.ShapeDtypeStruct((3,) + s1.shape[1:], s1.dtype)],
                         [pltpu.SemaphoreType.DMA((3,)), pltpu.SemaphoreType.DMA((3,))], name=name, collective_id=collective_id)[0]


def _gather_async(xs, name, collective_id):
    rider = _gather_rider(xs)

    def body(in_refs, out_refs, send_sems, recv_sems):
        ax, ay, ac = _position()
        _handshake([(ax, ay, 1 - ac), (1 - ax, ay, ac), (ax, 1 - ay, ac), (1 - ax, 1 - ay, ac)])
        for hook in rider.hooks(in_refs, out_refs, send_sems, recv_sems):
            hook()

    return _on_sequencer(body, rider.arrays, rider.out_shapes, rider.sems, name=name, collective_id=collective_id)


def _scatter_async(parts, name, collective_id):
    rider = _scatter_rider(parts)

    def body(in_refs, out_refs, send_sems, recv_sems):
        ax, ay, ac = _position()
        flip = lambda a, on: 1 - a if on else a
        _handshake([(flip(ax, rel & 4), flip(ay, rel & 2), flip(ac, rel & 1)) for rel in range(1, NDEV)])
        for hook in rider.hooks(in_refs, out_refs, send_sems, recv_sems):
            hook()

    return _on_sequencer(body, rider.arrays, rider.out_shapes, rider.sems, name=name, collective_id=collective_id)[0]


def _sum_in_chip(own, recv, name):
    _, r, w = own.shape
    tr = _tile(r, (256, 128))

    def body(a_ref, b_ref, o_ref):
        o_ref[...] = (a_ref[...].astype(F32) + b_ref[...].astype(F32)).astype(o_ref.dtype)

    blk = pl.BlockSpec((None, tr, w), lambda q, i: (q, i, 0))
    return pl.pallas_call(body, name=name, grid=(4, r // tr), in_specs=[blk, blk], out_specs=blk,
                          out_shape=jax.ShapeDtypeStruct(own.shape, own.dtype),
                          compiler_params=_cparams("parallel", "parallel"))(own, recv)


def _sum_chips(s1, recv, chip, name):
    _, r, w = s1.shape
    tr = _tile(r, (256, 128))

    def body(c_ref, s_ref, r0_ref, r1_ref, r2_ref, o_ref):
        f = lambda ref: ref[...].astype(F32)
        o_ref[...] = ((f(s_ref) + f(r0_ref)) + f(r1_ref)) + f(r2_ref)

    rblk = lambda k: pl.BlockSpec((None, tr, w), functools.partial(lambda i, c, k: (k, i, 0), k=k))
    grid_spec = pltpu.PrefetchScalarGridSpec(
        num_scalar_prefetch=1, grid=(r // tr,),
        in_specs=[pl.BlockSpec((None, tr, w), lambda i, c: (c[0], i, 0)), rblk(0), rblk(1), rblk(2)],
        out_specs=pl.BlockSpec((tr, w), lambda i, c: (i, 0)))
    return pl.pallas_call(body, name=name, grid_spec=grid_spec, out_shape=jax.ShapeDtypeStruct((r, w), F32),
                          compiler_params=_cparams("parallel"))(chip, s1, recv, recv, recv)


def _silu_rows(x, name):
    def body(x_ref, o_ref):
        o_ref[...] = _silu(x_ref[...])

    return pl.pallas_call(body, name=name, out_shape=jax.ShapeDtypeStruct(x.shape, F32))(x)


def _row_sum(x, name):
    def body(x_ref, o_ref):
        acc = x_ref[0:1, :]
        for i in range(1, x.shape[0]):
            acc = acc + x_ref[i:i + 1, :]
        o_ref[...] = acc

    return pl.pallas_call(body, name=name, out_shape=jax.ShapeDtypeStruct((1, x.shape[1]), F32))(x)


def _adamw(w, g, m, v, name):
    cols = w.shape[-1]
    rows = w.size // cols
    tr = _tile(rows, (128,))
    tc = LANE if (tr == rows and rows > 512 and cols % LANE == 0) else cols

    def body(w_ref, g_ref, m_ref, v_ref, d_ref, mo_ref, vo_ref):
        grad = g_ref[...]
        m_new = ADAM_B1 * m_ref[...] + (1.0 - ADAM_B1) * grad
        v_new = ADAM_B2 * v_ref[...] + (1.0 - ADAM_B2) * jnp.square(grad)
        m_hat = m_new / (1.0 - ADAM_B1 ** ADAM_STEP)
        v_hat = v_new / (1.0 - ADAM_B2 ** ADAM_STEP)
        d_ref[...] = -ADAM_LR * (m_hat / (jnp.sqrt(v_hat) + ADAM_EPS) + ADAM_WD * w_ref[...])
        mo_ref[...] = m_new
        vo_ref[...] = v_new

    blk = pl.BlockSpec((tr, tc), lambda i, j: (i, j))
    out = pl.pallas_call(
        body, name=name, grid=(rows // tr, cols // tc), in_specs=[blk] * 4, out_specs=[blk] * 3,
        out_shape=[jax.ShapeDtypeStruct((rows, cols), F32)] * 3, compiler_params=_cparams("parallel", "parallel"),
    )(*[t.reshape(rows, cols) for t in (w, g, m, v)])
    return [t.reshape(w.shape) for t in out]


def _pack(parts, width, row_mult, dtype):
    flat = jnp.concatenate([p.reshape(-1).astype(dtype) for p in parts])
    rows = -(-flat.shape[0] // (width * row_mult)) * row_mult
    return jnp.pad(flat, (0, rows * width - flat.shape[0])).reshape(rows, width)


def _unpack(flat, shapes):
    out, off = [], 0
    for shp in shapes:
        size = 1
        for dim in shp:
            size *= dim
        out.append(flat[:, off:off + size].reshape((flat.shape[0],) + tuple(shp)))
        off += size
    return out


def _devices_to_cols(a):
    _, r, c = a.shape
    return a.transpose(1, 0, 2).reshape(r, NDEV * c)


def kernel(x, c, w_ada, b_ada, norm1_w, w_in, gdn_conv_w, gdn_a_log, gdn_dt_bias, gdn_norm_w, w_gdn_proj, sc_conv_w, w_sc_out, w_o, norm2_w, w_ffn_in, w_ffn_out, w_ada_f, b_ada_f, normf_w, loss_target, m_w_ada, m_b_ada, m_norm1_w, m_w_in, m_gdn_conv_w, m_gdn_a_log, m_gdn_dt_bias, m_gdn_norm_w, m_w_gdn_proj, m_sc_conv_w, m_w_sc_out, m_w_o, m_norm2_w, m_w_ffn_in, m_w_ffn_out, m_w_ada_f, m_b_ada_f, m_normf_w, v_w_ada, v_b_ada, v_norm1_w, v_w_in, v_gdn_conv_w, v_gdn_a_log, v_gdn_dt_bias, v_gdn_norm_w, v_w_gdn_proj, v_sc_conv_w, v_w_sc_out, v_w_o, v_norm2_w, v_w_ffn_in, v_w_ffn_out, v_w_ada_f, v_b_ada_f, v_normf_w):
    bl, s, d = x.shape
    heads = gdn_a_log.shape[-1]
    dff = w_ffn_out.shape[1] * NDEV
    tok = bl * s
    ax, ay, ac = _position()
    dev = 4 * ax + 2 * ay + ac
    as_tok = lambda a: a.reshape(bl, s, a.shape[-1])
    as_mat = lambda a: a.reshape(tok, a.shape[-1])

    small = _all_gather(_pack([c, gdn_conv_w, sc_conv_w], LANE, 8, F32), name="gather_cond")
    c_all, conv_w, sc_w = _unpack(small.reshape(NDEV, -1), [(bl, d), gdn_conv_w.shape[1:], sc_conv_w.shape[1:]])
    c_act = _silu_rows(c_all.reshape(NDEV * bl, d), "cond_silu")
    conv_w, sc_w = _devices_to_cols(conv_w), _devices_to_cols(sc_w)
    n_ada, n_adaf = w_ada.shape[-1], w_ada_f.shape[-1]
    bias = jnp.broadcast_to(lax.dynamic_slice_in_dim(b_ada, dev * n_ada, n_ada, axis=1), (NDEV * bl, n_ada))
    biasf = jnp.broadcast_to(lax.dynamic_slice_in_dim(b_ada_f.reshape(1, -1), dev * n_adaf, n_adaf, axis=1), (NDEV * bl, n_adaf))
    mod_cols = _mm(c_act, w_ada[0], add=bias, name="ada_cols")
    modf_cols = _mm(c_act, w_ada_f, add=biasf, name="adaf_cols")
    mods = _all_gather(jnp.concatenate([mod_cols, modf_cols], axis=1), name="gather_mod")
    mod_all = mods[:, :, :n_ada].transpose(1, 0, 2).reshape(NDEV * bl, NDEV * n_ada)
    modf_all = mods[:, :, n_ada:].transpose(1, 0, 2).reshape(NDEV * bl, NDEV * n_adaf)
    my_rows = lambda a: lax.dynamic_slice_in_dim(a, dev * bl, bl, axis=0)
    sh1, sc1, g1, sh2, sc2, g2 = [t.reshape(bl, 1, d) for t in jnp.split(my_rows(mod_all), 6, axis=1)]
    shf, scf = [t.reshape(bl, 1, d) for t in jnp.split(my_rows(modf_all), 2, axis=1)]

    late = [t.astype(MXU_DTYPE) for t in (w_gdn_proj[0], w_sc_out[0], w_o[0], w_ffn_in[0].T, w_ffn_out[0])]
    rows = [t.shape[0] for t in late] + [w_in.shape[-1]]
    offs = [sum(rows[:i]) for i in range(5)]
    in_send = w_in[0].T.astype(MXU_DTYPE)
    with_own = lambda g, own: lax.dynamic_update_slice_in_dim(g, own[None], dev, axis=0)
    (wt_in,) = _gather_async([in_send], "gather_w_in", 1)
    wt_in = with_own(wt_in, in_send).reshape(NDEV * rows[5], d)
    gathered = _gather_async(late[:3], "gather_mixer", 2) + _gather_async(late[3:], "gather_ffn", 3)
    wgp, wso, wo, wt_fi, wfo = [with_own(g, own).reshape(NDEV * own.shape[0], d) for g, own in zip(gathered, late)]
    o_z, o_ab, o_sc, o_ga, o_gb = 3 * d, 4 * d, 4 * d + 2 * heads, 7 * d + 2 * heads, 8 * d + 2 * heads
    s_qkv, s_z, s_sc, s_gate = (0, o_z), (o_z, d), (o_sc, 3 * d), (o_ga, 2 * d)
    wt_ab = jnp.pad(wt_in[o_ab:o_sc], ((0, LANE - 2 * heads), (0, 0)))

    n1w, n2w, nfw = norm1_w.reshape(1, d), norm2_w.reshape(1, d), normf_w.reshape(1, d)
    lanes = lambda a: jnp.pad(a.reshape(1, -1), ((0, 0), (0, LANE - a.size)))
    a_log, dt_bias, gnw = lanes(gdn_a_log), lanes(gdn_dt_bias), gdn_norm_w.reshape(1, HEAD)
    f_gates = functools.partial(_f_gates, heads=heads)
    (h1,) = _tok_fwd(_f_norm_mod, [x], [sh1, sc1], [n1w], [(d, MXU_DTYPE)], name="norm1", ts=512)
    h1m = as_mat(h1)
    p_qkv = as_tok(_mm(h1m, wt_in, tb=True, b_rows=s_qkv, name="in_qkv"))
    p_z = as_tok(_mm(h1m, wt_in, tb=True, b_rows=s_z, name="in_z"))
    p_ab = as_tok(_mm(h1m, wt_ab, tb=True, name="in_ab"))
    p_sc = as_tok(_mm(h1m, wt_in, tb=True, b_rows=s_sc, name="in_sc"))
    p_g = as_tok(_mm(h1m, wt_in, tb=True, b_rows=s_gate, name="in_gate"))
    qkv = _qkv_fwd(p_qkv, conv_w, heads, "qkv_conv")
    (gbeta,) = _tok_fwd(f_gates, [p_ab], [], [a_log, dt_bias], [(LANE, F32)], name="gates", ts=512)
    o, s_all, t_all = _gdn_fwd(qkv, gbeta, heads, "gdn")
    (og,) = _tok_fwd(_f_gdn_out, [o, p_z], [], [(gnw, None)], [(d, MXU_DTYPE)], name="gdn_out", ts=2048, wb=HEAD, cols=heads)
    y_a = as_tok(_mm(as_mat(og), wgp, name="gdn_proj"))
    scp = _sc_fwd(p_sc, sc_w, "sc_conv")
    mrg, y_b = _tok_fwd(_f_merge_keep, [(p_g, 0), (p_g, 1), y_a, _Product(scp, wso)], [], [], [(d, MXU_DTYPE), (d, F32)],
                        name="merge", ts=512, wb=d)
    merge_toks = [(p_g, 0), (p_g, 1), y_a, y_b]
    x2, h2, mix = _tok_fwd(_f_res_norm_mod_keep, [x, _Product(mrg, wo)], [g1, sh2, sc2], [n2w],
                           [(d, F32), (d, MXU_DTYPE), (d, F32)], name="norm2", ts=512)
    act, gu_a, gu_b = _ffn_in_swiglu(as_mat(h2), wt_fi, dff, "ffn_in")

    loss_l, (dx2, dff_out, _), (dg2, dshf, dscf), (dnfw,) = _tok_bwd(
        _f_loss, [x2, _Product(as_tok(act), wfo), loss_target], [g2, shf, scf], [nfw], [], [True, True, False], name="loss",
        ts=256, loss=True, tok_dtype=[F32, MXU_DTYPE, None])
    dffm = as_mat(dff_out)
    dgu_a, dgu_b = _ffn_out_bwd_swiglu(dffm, wfo, gu_a, gu_b, "d_ffn_out")
    gmm = functools.partial(_mm, ta=True, out_dtype=MXU_DTYPE)
    gw_ffn_out = gmm(act, dffm, name="g_ffn_out")
    dh2 = _Product(as_tok(dgu_b), wt_fi, b_rows=(dff, dff), add=_Product(as_tok(dgu_a), wt_fi, b_rows=(0, dff)))
    h2m = as_mat(h2)
    gwt_ffn_in = gmm(dgu_a, h2m, out_rows=2 * dff, name="g_ffn_in_a")
    gwt_ffn_in = gmm(dgu_b, h2m, out_rows=2 * dff, row_off=dff, into=gwt_ffn_in, name="g_ffn_in_b")
    ffn_parts = [(gwt_ffn_in, rows[3]), (gw_ffn_out, rows[4])]
    ffn_recv = _scatter_async(ffn_parts, "scatter_ffn", 4)
    (dx_skip, dmix), (dg1, dsh2, dsc2), (dn2w,) = _tok_bwd(
        _f_res_norm_mod, [x, mix], [g1, sh2, sc2], [n2w], [dx2, dh2], [True, True], name="d_norm2", ts=256,
        tok_dtype=[F32, MXU_DTYPE], after=[gwt_ffn_in, gw_ffn_out])
    gw_o = gmm(as_mat(mrg), as_mat(dmix), name="g_mix_out")
    (dga, dgb, dya, dyb), _, _ = _tok_bwd(_f_merge, merge_toks, [], [], [_Product(dmix, wo, tb=True)], [True] * 4,
                                          name="d_merge", ts=256, wb=d, tok_dtype=MXU_DTYPE)
    dyam, dybm = as_mat(dya), as_mat(dyb)
    dog = as_tok(_mm(dyam, wgp, tb=True, name="d_gdn_proj"))
    gw_gdn_proj = gmm(as_mat(og), dyam, name="g_gdn_proj")
    dscp = as_tok(_mm(dybm, wso, tb=True, name="d_sc_out"))
    gw_sc_out = gmm(as_mat(scp), dybm, name="g_sc_out")
    dsc, g_sc_w = _sc_bwd(p_sc, sc_w, dscp, "d_sc_conv")
    mix_parts = [(gw_gdn_proj, rows[0]), (gw_sc_out, rows[1]), (gw_o, rows[2])]
    mix_recv = _scatter_async(mix_parts, "scatter_mixer", 5)
    (do, dz), _, (g_gnw,) = _tok_bwd(_f_gdn_out, [o, p_z], [], [(gnw, None)], [dog], [True, True], name="d_gdn_out",
                                     ts=2048, wb=HEAD, cols=heads, tok_dtype=[F32, MXU_DTYPE],
                                     after=[gw_gdn_proj, gw_sc_out, gw_o])
    own_rows = lambda parts: jnp.concatenate([lax.dynamic_slice_in_dim(g, dev * r, r, axis=0) for g, r in parts], axis=0)
    dqkv, dgbeta = _gdn_bwd(qkv, gbeta, do, s_all, t_all, heads, "d_gdn")
    dp_qkv, g_conv_w = _qkv_bwd(p_qkv, conv_w, dqkv, heads, "d_qkv_conv")
    ffn_red = _sum_direct(own_rows(ffn_parts), ffn_recv, "sum_ffn")
    mix_red = _sum_direct(own_rows(mix_parts), mix_recv, "sum_mix")
    (dp_ab,), _, (g_a_log, g_dt_bias) = _tok_bwd(f_gates, [p_ab], [], [a_log, dt_bias], [dgbeta], [True], name="d_gates",
                                                 ts=512, tok_dtype=MXU_DTYPE, after=[ffn_red, mix_red])
    r_in = rows[5]
    win = -(-(r_in + max(r_in * k % ROW_ALIGN for k in range(NDEV))) // 128) * 128
    need_rows = max(_window_start(r_in, k) for k in range(NDEV)) + win
    dsc_m = dsc.reshape(3, tok, d)
    gwt_in = ([gmm(as_mat(dp_qkv), h1m, name="g_in_qkv"), gmm(as_mat(dz), h1m, name="g_in_z"),
               gmm(as_mat(dp_ab), h1m, name="g_in_ab")[:2 * heads]]
              + [gmm(dsc_m, h1m, a_index=k, name=f"g_in_sc{k}") for k in range(3)]
              + [gmm(as_mat(dga), h1m, name="g_in_ga"), gmm(as_mat(dgb), h1m, name="g_in_gb")])
    gwt_in = jnp.concatenate(gwt_in + [jnp.zeros((need_rows - NDEV * r_in, d), MXU_DTYPE)], axis=0)
    assert d <= 1024
    wide = [as_mat(dp_qkv), as_mat(dz), dsc_m, as_mat(dga)]
    row_of = lambda t: d * t + jnp.where(t * d >= o_ab, 2 * heads, 0)
    recv1 = _exchange_in_chip([(gwt_in, r_in, win, 0)], "scatter_in_chip", 7)
    own = jnp.stack([lax.dynamic_slice_in_dim(gwt_in, _window_start(r_in, 2 * q + ac), win, axis=0) for q in range(4)])
    s1 = _sum_in_chip(own, recv1, "sum_in_chip")
    recv2 = _exchange_chips_async(s1, "scatter_chips", 6)

    dh1 = _mm(as_mat(dp_ab), wt_ab, name="d_in_ab")
    dh1 = _mm_chain(wide, wt_in, row_of, add=dh1, name="d_in", tk=d)
    dh1 = _Product(dgb, wt_in, b_rows=(o_gb, d), add=as_tok(dh1))
    (grad_x,), (dsh1, dsc1), (dn1w,) = _tok_bwd(_f_norm_mod_skip, [x], [sh1, sc1], [n1w], [dh1, dx_skip], [True],
                                                name="d_norm1", ts=256)
    reduced = _sum_chips(s1, recv2, (2 * ax + ay).reshape(1).astype(jnp.int32), "sum_chips")
    gt_w_in = lax.dynamic_slice_in_dim(reduced, r_in * dev - _window_start(r_in, dev), r_in, axis=0)
    g_w_in = gt_w_in.T.reshape(w_in.shape)
    gt_w_ffn_in = ffn_red[:rows[3]]
    g_w_ffn_in = gt_w_ffn_in.T.reshape(w_ffn_in.shape)
    g_w_ffn_out = ffn_red[rows[3]:].reshape(w_ffn_out.shape)
    g_w_gdn_proj, g_w_sc_out, g_w_o = (mix_red[offs[i]:offs[i] + rows[i]].reshape(ref.shape)
                                       for i, ref in enumerate((w_gdn_proj, w_sc_out, w_o)))

    dmod = jnp.concatenate([t.reshape(bl, d) for t in (dsh1, dsc1, dg1, dsh2, dsc2, dg2)], axis=1)
    dmodf = jnp.concatenate([t.reshape(bl, d) for t in (dshf, dscf)], axis=1)
    summed_parts = [dn1w, dn2w, dnfw, g_gnw, g_a_log, g_dt_bias, g_conv_w, g_sc_w, loss_l]
    partial = _all_gather(_pack([dmod, dmodf] + summed_parts, LANE, 8, F32), name="gather_small")
    partial = partial.reshape(NDEV, -1)
    n_rows = bl * (6 * d + 2 * d)
    dmod_all, dmodf_all = _unpack(partial[:, :n_rows], [(bl, 6 * d), (bl, 2 * d)])
    dmod_all, dmodf_all = dmod_all.reshape(NDEV * bl, 6 * d), dmodf_all.reshape(NDEV * bl, 2 * d)
    totals = _row_sum(partial[:, n_rows:], "sum_small")
    t_n1w, t_n2w, t_nfw, t_gnw, t_a_log, t_dt_bias, t_conv_w, t_sc_w, t_loss = [
        t[0] for t in _unpack(totals, [p.shape for p in summed_parts])]
    my_cols = lambda a, n: lax.dynamic_slice_in_dim(a, dev * n, n, axis=1)
    grads = {
        "w_ada": _mm(c_act, my_cols(dmod_all, n_ada), ta=True, name="g_ada").reshape(w_ada.shape),
        "b_ada": _row_sum(dmod_all, "g_ada_bias").reshape(b_ada.shape),
        "norm1_w": t_n1w.reshape(norm1_w.shape),
        "w_in": g_w_in,
        "gdn_conv_w": my_cols(t_conv_w, gdn_conv_w.shape[-1]).reshape(gdn_conv_w.shape),
        "gdn_a_log": t_a_log[:, :heads].reshape(gdn_a_log.shape),
        "gdn_dt_bias": t_dt_bias[:, :heads].reshape(gdn_dt_bias.shape),
        "gdn_norm_w": t_gnw.reshape(gdn_norm_w.shape),
        "w_gdn_proj": g_w_gdn_proj,
        "sc_conv_w": my_cols(t_sc_w, sc_conv_w.shape[-1]).reshape(sc_conv_w.shape),
        "w_sc_out": g_w_sc_out,
        "w_o": g_w_o,
        "norm2_w": t_n2w.reshape(norm2_w.shape),
        "w_ffn_in": g_w_ffn_in,
        "w_ffn_out": g_w_ffn_out,
        "w_ada_f": _mm(c_act, my_cols(dmodf_all, n_adaf), ta=True, name="g_adaf").reshape(w_ada_f.shape),
        "b_ada_f": _row_sum(dmodf_all, "g_adaf_bias").reshape(b_ada_f.shape),
        "normf_w": t_nfw.reshape(normf_w.shape),
    }
    weights = dict(w_ada=w_ada, b_ada=b_ada, norm1_w=norm1_w, w_in=w_in, gdn_conv_w=gdn_conv_w, gdn_a_log=gdn_a_log,
                   gdn_dt_bias=gdn_dt_bias, gdn_norm_w=gdn_norm_w, w_gdn_proj=w_gdn_proj, sc_conv_w=sc_conv_w,
                   w_sc_out=w_sc_out, w_o=w_o, norm2_w=norm2_w, w_ffn_in=w_ffn_in, w_ffn_out=w_ffn_out, w_ada_f=w_ada_f,
                   b_ada_f=b_ada_f, normf_w=normf_w)
    m_in = [m_w_ada, m_b_ada, m_norm1_w, m_w_in, m_gdn_conv_w, m_gdn_a_log, m_gdn_dt_bias, m_gdn_norm_w, m_w_gdn_proj,
            m_sc_conv_w, m_w_sc_out, m_w_o, m_norm2_w, m_w_ffn_in, m_w_ffn_out, m_w_ada_f, m_b_ada_f, m_normf_w]
    v_in = [v_w_ada, v_b_ada, v_norm1_w, v_w_in, v_gdn_conv_w, v_gdn_a_log, v_gdn_dt_bias, v_gdn_norm_w, v_w_gdn_proj,
            v_sc_conv_w, v_w_sc_out, v_w_o, v_norm2_w, v_w_ffn_in, v_w_ffn_out, v_w_ada_f, v_b_ada_f, v_normf_w]
    deltas, new_m, new_v = [], [], []
    grads_t = {"w_in": gt_w_in, "w_ffn_in": gt_w_ffn_in}
    for (wname, wt), mt, vt in zip(weights.items(), m_in, v_in):
        if wname in grads_t:
            back = lambda a, wt=wt: a.T.reshape(wt.shape)
            dl, mn, vn = (back(a) for a in _adamw(wt[0].T, grads_t[wname], mt[0].T, vt[0].T, "adamw_" + wname))
        else:
            dl, mn, vn = _adamw(wt, grads[wname], mt, vt, "adamw_" + wname)
        deltas.append(dl)
        new_m.append(mn)
        new_v.append(vn)
    loss = t_loss[0, 0]
    return (loss, grad_x, *[grads[k] for k in weights], *deltas, *new_m, *new_v)
```

```python
import functools

import jax
import jax.numpy as jnp
from jax import lax
from jax.experimental import pallas as pl
from jax.experimental.pallas import tpu as pltpu
from jax.experimental.pallas import tpu_sc as plsc

F32 = jnp.float32
MXU_DTYPE = jnp.bfloat16
NDEV = 8
CHUNK = 64
HEAD = 128
LANE = 128
EPS = 1e-6
ADAM_LR, ADAM_B1, ADAM_B2, ADAM_EPS, ADAM_WD, ADAM_STEP = 0.001, 0.9, 0.999, 1e-08, 0.01, 10
VMEM_LIMIT = 48 * 1024 * 1024
MESH_IDS = pl.DeviceIdType.MESH
HIGHEST = lax.Precision.HIGHEST


def _tile(n, cands=(512, 256, 128)):
    for c in cands:
        if n % c == 0:
            return c
    return n


def _cparams(*sem):
    return pltpu.CompilerParams(dimension_semantics=sem, vmem_limit_bytes=VMEM_LIMIT)


def _mm(a, b, *, ta=False, tb=False, add=None, out_dtype=F32, name, b_rows=None, out_rows=None, row_off=0, into=None,
        a_index=None):
    m, k = (a.shape[-1], a.shape[-2]) if ta else a.shape[-2:]
    b_shape = b.shape if b_rows is None else (b_rows[1], b.shape[1])
    n = b_shape[0] if tb else b_shape[1]
    assert k == (b_shape[1] if tb else b_shape[0])
    if ta:
        tm, tn = _tile(m), n if n <= 1024 else _tile(n)
        tk = k if k <= 4096 else _tile(k, (4096, 2048, 1024, 512))
        if tm * tk > 1024 * 2048:
            tk = _tile(k, (2048, 1024, 512))
    else:
        tk = k if k <= 1024 else _tile(k, (1024, 512))
        tn = _tile(n, (1024 if tk <= 1024 else 512, 512, 256, 128))
        tm = _tile(m, (2048 if (tn <= 512 and tk <= 1024) else 1024, 1024, 512, 256, 128))
    nk = k // tk
    dims = (((0 if ta else 1,), (1 if tb else 0,)), ((), ()))
    has_add = add is not None

    def body(*refs):
        a_ref, b_ref = refs[0], refs[1]
        add_ref = refs[2] if has_add else None
        o_ref = refs[2 + has_add + (into is not None)]
        part = lax.dot_general(a_ref[...].astype(MXU_DTYPE), b_ref[...].astype(MXU_DTYPE), dims,
                               preferred_element_type=F32)

        def finish(acc):
            if has_add:
                acc = acc + add_ref[...]
            o_ref[...] = acc.astype(o_ref.dtype)

        if nk == 1:
            finish(part)
        else:
            acc_ref = refs[-1]
            kk = pl.program_id(2)

            @pl.when(kk == 0)
            def _():
                acc_ref[...] = part

            @pl.when(kk > 0)
            def _():
                acc_ref[...] += part

            @pl.when(kk == nk - 1)
            def _():
                finish(acc_ref[...])

    a_blk, a_at = ((tk, tm), lambda i, j, kk: (kk, i)) if ta else ((tm, tk), lambda i, j, kk: (i, kk))
    a_spec = (pl.BlockSpec(a_blk, a_at) if a_index is None else
              pl.BlockSpec((None,) + a_blk, lambda i, j, kk: (a_index,) + a_at(i, j, kk)))
    if b_rows is None:
        b_spec = pl.BlockSpec((tn, tk), lambda i, j, kk: (j, kk)) if tb else pl.BlockSpec((tk, tn), lambda i, j, kk: (kk, j))
    else:
        at = lambda t: pl.multiple_of(b_rows[0] + t, ROW_ALIGN)
        b_spec = (pl.BlockSpec((pl.Element(tn), pl.Element(tk)), lambda i, j, kk: (at(j * tn), kk * tk)) if tb else
                  pl.BlockSpec((pl.Element(tk), pl.Element(tn)), lambda i, j, kk: (at(kk * tk), j * tn)))
    add_spec = pl.BlockSpec((tm, tn), lambda i, j, kk: (i, j))
    assert row_off % tm == 0
    o_spec = pl.BlockSpec((tm, tn), lambda i, j, kk: (i + row_off // tm, j))
    in_specs = [a_spec, b_spec] + ([add_spec] if has_add else []) + ([pl.BlockSpec(memory_space=pl.ANY)] if into is not None else [])
    args = [a, b] + ([add] if has_add else []) + ([into] if into is not None else [])
    return pl.pallas_call(
        body, name=name, grid=(m // tm, n // tn, nk), in_specs=in_specs, out_specs=o_spec,
        out_shape=jax.ShapeDtypeStruct((out_rows or m, n), out_dtype),
        scratch_shapes=[pltpu.VMEM((tm, tn), F32)] if nk > 1 else [],
        input_output_aliases={len(args) - 1: 0} if into is not None else {},
        compiler_params=_cparams("parallel", "parallel", "arbitrary"),
    )(*args)


def _mm_chain(parts, b, row_of_tile, *, add, name, tk=1024, tm=1024):
    m, n = parts[0].shape[-2], b.shape[1]
    tm = min(tm, m)
    tiles = [p.shape[0] if p.ndim == 3 else p.shape[1] // tk for p in parts]
    first = [sum(tiles[:s]) for s in range(len(parts))]
    nk = sum(tiles)

    def body(*refs):
        a_refs, b_ref, add_ref, o_ref, acc_ref = refs[:len(parts)], *refs[len(parts):]
        kk = pl.program_id(1)

        @pl.when(kk == 0)
        def _():
            acc_ref[...] = add_ref[...]

        for a_ref, lo, cnt in zip(a_refs, first, tiles):
            @pl.when(jnp.logical_and(kk >= lo, kk < lo + cnt))
            def _(a_ref=a_ref):
                acc_ref[...] += lax.dot_general(a_ref[...].astype(MXU_DTYPE), b_ref[...].astype(MXU_DTYPE),
                                                (((1,), (0,)), ((), ())), preferred_element_type=F32)

        @pl.when(kk == nk - 1)
        def _():
            o_ref[...] = acc_ref[...]

    tile_of = lambda kk, lo, cnt: jnp.clip(kk - lo, 0, cnt - 1)
    a_specs = [pl.BlockSpec((None, tm, tk), functools.partial(lambda i, kk, lo, cnt: (tile_of(kk, lo, cnt), i, 0), lo=lo, cnt=cnt))
               if p.ndim == 3 else
               pl.BlockSpec((tm, tk), functools.partial(lambda i, kk, lo, cnt: (i, tile_of(kk, lo, cnt)), lo=lo, cnt=cnt))
               for p, lo, cnt in zip(parts, first, tiles)]
    b_spec = pl.BlockSpec((pl.Element(tk), pl.Element(n)), lambda i, kk: (pl.multiple_of(row_of_tile(kk), ROW_ALIGN), 0))
    o_spec = pl.BlockSpec((tm, n), lambda i, kk: (i, 0))
    return pl.pallas_call(
        body, name=name, grid=(m // tm, nk), in_specs=a_specs + [b_spec, o_spec], out_specs=o_spec,
        out_shape=jax.ShapeDtypeStruct((m, n), F32), scratch_shapes=[pltpu.VMEM((tm, n), F32)],
        compiler_params=_cparams("parallel", "arbitrary"),
    )(*parts, b, add)


def _swiglu_tiles(m, half):
    tn = _tile(half, (512, 256, 128))
    return _tile(m, (2048 if tn <= 256 else 1024, 1024, 512, 256, 128)), tn


def _ffn_in_swiglu(h, wt, half, name):
    m, k = h.shape
    tm, tn = _swiglu_tiles(m, half)
    nj = half // tn
    dims = (((1,), (1,)), ((), ()))

    def body(h_ref, wa_ref, wb_ref, act_ref, a_ref, b_ref):
        lhs = h_ref[...].astype(MXU_DTYPE)
        a = lax.dot_general(lhs, wa_ref[...].astype(MXU_DTYPE), dims, preferred_element_type=F32)
        b = lax.dot_general(lhs, wb_ref[...].astype(MXU_DTYPE), dims, preferred_element_type=F32)
        act_ref[...] = (_silu(a) * b).astype(act_ref.dtype)
        a_ref[...] = a.astype(a_ref.dtype)
        b_ref[...] = b.astype(b_ref.dtype)

    out = jax.ShapeDtypeStruct((m, half), MXU_DTYPE)
    oblk = pl.BlockSpec((tm, tn), lambda i, j: (i, j))
    return pl.pallas_call(
        body, name=name, grid=(m // tm, nj),
        in_specs=[pl.BlockSpec((tm, k), lambda i, j: (i, 0)), pl.BlockSpec((tn, k), lambda i, j: (j, 0)),
                  pl.BlockSpec((tn, k), lambda i, j: (j + nj, 0))],
        out_specs=[oblk, oblk, oblk], out_shape=[out, out, out], compiler_params=_cparams("parallel", "parallel"),
    )(h, wt, wt)


def _ffn_out_bwd_swiglu(dff, w, a, b, name):
    m, k = dff.shape
    half = w.shape[0]
    tm, tn = _swiglu_tiles(m, half)

    def body(d_ref, w_ref, a_ref, b_ref, da_ref, db_ref):
        dact = lax.dot_general(d_ref[...].astype(MXU_DTYPE), w_ref[...].astype(MXU_DTYPE), (((1,), (1,)), ((), ())),
                               preferred_element_type=F32)
        av, bv = a_ref[...].astype(F32), b_ref[...].astype(F32)
        sig = jax.nn.sigmoid(av)
        da_ref[...] = (dact * bv * (sig * (1.0 + av * (1.0 - sig)))).astype(da_ref.dtype)
        db_ref[...] = (dact * (av * sig)).astype(db_ref.dtype)

    out = jax.ShapeDtypeStruct((m, half), MXU_DTYPE)
    oblk = pl.BlockSpec((tm, tn), lambda i, j: (i, j))
    return pl.pallas_call(
        body, name=name, grid=(m // tm, half // tn),
        in_specs=[pl.BlockSpec((tm, k), lambda i, j: (i, 0)), pl.BlockSpec((tn, k), lambda i, j: (j, 0)), oblk, oblk],
        out_specs=[oblk, oblk], out_shape=[out, out], compiler_params=_cparams("parallel", "parallel"),
    )(dff, w, a, b)


def _with_off(xs):
    return [x if isinstance(x, tuple) else (x, 0) for x in xs]


def _spec(kind, arr, off, ts, wb):
    w = arr.shape[-1] if wb is None else wb
    col = (lambda j: 0) if wb is None else functools.partial(lambda j, o: o + j, o=off)
    if kind == "tok":
        return pl.BlockSpec((None, ts, w), lambda j, b, i: (b, i, col(j)))
    if kind == "bat":
        return pl.BlockSpec((None, 1, w), lambda j, b, i: (b, 0, col(j)))
    if off is None:
        return pl.BlockSpec(arr.shape, lambda j, b, i: (0, 0))
    return pl.BlockSpec((arr.shape[0], w), lambda j, b, i: (0, col(j)))


class _Product:
    def __init__(self, a, b, *, tb=False, b_rows=None, add=None):
        self.a, self.b, self.tb, self.b_rows, self.add = a, b, tb, b_rows, add
        rows = b.shape[0] if b_rows is None else b_rows[1]
        self.shape = a.shape[:2] + (rows if tb else b.shape[1],)

    def inputs(self, ts):
        a_spec = pl.BlockSpec((None, ts, self.a.shape[2]), lambda j, b, i: (b, i, 0))
        if self.b_rows is None:
            b_spec = pl.BlockSpec(self.b.shape, lambda j, b, i: (0, 0))
        else:
            start, count = self.b_rows
            b_spec = pl.BlockSpec((pl.Element(count), pl.Element(self.b.shape[1])), lambda j, b, i: (start, 0))
        if isinstance(self.add, _Product):
            extra = self.add.inputs(ts)
        else:
            extra = [] if self.add is None else [(self.add, pl.BlockSpec((None, ts, self.shape[2]), lambda j, b, i: (b, i, 0)))]
        return [(self.a, a_spec), (self.b, b_spec)] + extra

    def value(self, refs):
        dims = (((1,), (1 if self.tb else 0,)), ((), ()))
        val = lax.dot_general(refs[0][...].astype(MXU_DTYPE), refs[1][...].astype(MXU_DTYPE), dims, preferred_element_type=F32)
        if isinstance(self.add, _Product):
            return val + self.add.value(refs[2:])
        return val if self.add is None else val + refs[2][...].astype(F32)


def _inputs(groups, kinds, ts, wb):
    loaded = [(a, _spec(kind, a, o, ts, wb)) for g, kind in zip(groups, kinds) for a, o in g if not isinstance(a, _Product)]
    made = [pair for g in groups for a, _ in g if isinstance(a, _Product) for pair in a.inputs(ts)]
    return [a for a, _ in loaded + made], [sp for _, sp in loaded + made]


def _values(refs, groups):
    n_loaded = sum(1 for g in groups for a, _ in g if not isinstance(a, _Product))
    loaded, pos, out = iter(refs[:n_loaded]), n_loaded, []
    for g in groups:
        vals = []
        for a, _ in g:
            if isinstance(a, _Product):
                k = len(a.inputs(1))
                vals.append(a.value(refs[pos:pos + k]))
                pos += k
            else:
                vals.append(next(loaded)[...].astype(F32))
        out.append(vals)
    return out, pos


def _tok_fwd(fn, toks, bats, pars, outs, *, name, ts, wb=None, cols=1):
    groups = [_with_off(toks), _with_off(bats), _with_off(pars)]
    bl, s, _ = groups[0][0][0].shape
    ts = min(ts, s)
    args, in_specs = _inputs(groups, ("tok", "bat", "par"), ts, wb)

    def body(*refs):
        vals, n_in = _values(refs, groups)
        res = fn(*[v for g in vals for v in g])
        for r, val in zip(refs[n_in:], res):
            r[...] = val.astype(r.dtype)

    out_specs = [pl.BlockSpec((None, ts, w if wb is None else wb), lambda j, b, i: (b, i, j)) for w, _ in outs]
    return pl.pallas_call(
        body, name=name, grid=(cols, bl, s // ts), in_specs=in_specs,
        out_specs=out_specs, out_shape=[jax.ShapeDtypeStruct((bl, s, w), dt) for w, dt in outs],
        compiler_params=_cparams("parallel", "parallel", "parallel"),
    )(*args)


def _accumulate(ref, val, first):
    @pl.when(first)
    def _():
        ref[...] = val

    @pl.when(jnp.logical_not(first))
    def _():
        ref[...] += val


def _tok_bwd(fn, toks, bats, pars, cots, need, *, name, ts, wb=None, cols=1, tok_dtype=F32, loss=False, after=()):
    toks, bats, pars, cots = _with_off(toks), _with_off(bats), _with_off(pars), _with_off(cots)
    groups = [toks, bats, pars, cots]
    bl, s, _ = toks[0][0].shape
    ts = min(ts, s)
    nt, nb, npar = len(toks), len(bats), len(pars)
    args, in_specs = _inputs(groups, ("tok", "bat", "par", "tok"), ts, wb)
    args, in_specs = args + list(after), in_specs + [pl.BlockSpec(memory_space=pl.ANY)] * len(after)

    def body(*refs):
        j, b, i = pl.program_id(0), pl.program_id(1), pl.program_id(2)
        (tok_vals, bat_vals, par_vals, cot_vals), o = _values(refs, groups)
        o += len(after)
        outs, vjp = jax.vjp(fn, *tok_vals, *bat_vals, *par_vals)
        if loss:
            ct = (jnp.ones_like(outs[0]),)
            tot = jnp.broadcast_to(jnp.sum(outs[0], keepdims=True), (1, LANE))
            _accumulate(refs[o], tot, jnp.logical_and(b == 0, i == 0))
            o += 1
        else:
            ct = tuple(cot_vals)
        grads = vjp(ct)
        for t in range(nt):
            if need[t]:
                refs[o][...] = grads[t].astype(refs[o].dtype)
                o += 1
        for t in range(nb):
            _accumulate(refs[o], grads[nt + t], i == 0)
            o += 1
        for t in range(npar):
            first = jnp.logical_and(b == 0, i == 0)
            if pars[t][1] is None:
                first = jnp.logical_and(first, j == 0)
            _accumulate(refs[o], grads[nt + nb + t], first)
            o += 1

    full = lambda arr: arr.shape[-1] if wb is None else wb * cols
    blk = lambda arr: arr.shape[-1] if wb is None else wb
    out_specs, out_shape = [], []
    if loss:
        out_specs.append(pl.BlockSpec((1, LANE), lambda j, b, i: (0, 0)))
        out_shape.append(jax.ShapeDtypeStruct((1, LANE), F32))
    for t in range(nt):
        if need[t]:
            out_specs.append(pl.BlockSpec((None, ts, blk(toks[t][0])), lambda j, b, i: (b, i, j)))
            dt = tok_dtype[t] if isinstance(tok_dtype, (list, tuple)) else tok_dtype
            out_shape.append(jax.ShapeDtypeStruct((bl, s, full(toks[t][0])), dt))
    for arr, _ in bats:
        out_specs.append(pl.BlockSpec((None, 1, blk(arr)), lambda j, b, i: (b, 0, j)))
        out_shape.append(jax.ShapeDtypeStruct((bl, 1, full(arr)), F32))
    for arr, off in pars:
        if off is None:
            out_specs.append(pl.BlockSpec(arr.shape, lambda j, b, i: (0, 0)))
            out_shape.append(jax.ShapeDtypeStruct(arr.shape, F32))
        else:
            out_specs.append(pl.BlockSpec((arr.shape[0], blk(arr)), lambda j, b, i: (0, j)))
            out_shape.append(jax.ShapeDtypeStruct((arr.shape[0], full(arr)), F32))
    res = list(pl.pallas_call(
        body, name=name, grid=(cols, bl, s // ts), in_specs=in_specs,
        out_specs=out_specs, out_shape=out_shape, compiler_params=_cparams("arbitrary", "arbitrary", "arbitrary"),
    )(*args))
    tot = res.pop(0) if loss else None
    dtoks = [res.pop(0) if need[t] else None for t in range(nt)]
    dbats = [res.pop(0) for _ in range(nb)]
    dpars = [res.pop(0) for _ in range(npar)]
    return (tot, dtoks, dbats, dpars) if loss else (dtoks, dbats, dpars)


def _silu(x):
    return x * jax.nn.sigmoid(x)


def _rms(x, w):
    return x * lax.rsqrt(jnp.mean(x * x, axis=-1, keepdims=True) + EPS) * w


def _f_norm_mod(x, shift, scale, w):
    return (_rms(x, w) * (1.0 + scale) + shift,)


def _f_norm_mod_skip(x, shift, scale, w):
    return _rms(x, w) * (1.0 + scale) + shift, x


def _f_res_norm_mod(x, mix, gate, shift, scale, w):
    x2 = x + gate * mix
    return x2, _rms(x2, w) * (1.0 + scale) + shift


def _f_res_norm_mod_keep(x, mix, gate, shift, scale, w):
    return (*_f_res_norm_mod(x, mix, gate, shift, scale, w), mix)


def _f_gates(p, a_log, dt_bias, *, heads):
    z = p + dt_bias
    g = -jnp.exp(a_log) * (jnp.maximum(z, 0.0) + jnp.log1p(jnp.exp(jnp.minimum(z, -z))))
    lane = lax.broadcasted_iota(jnp.int32, p.shape, 1)
    return (jnp.where(lane < heads, g, jax.nn.sigmoid(p)),)


def _f_gdn_out(o, z, w):
    return (_rms(o, w) * _silu(z),)


def _f_merge(ga, gb, ya, yb):
    return (jax.nn.sigmoid(ga) * ya + jax.nn.sigmoid(gb) * yb,)


def _f_merge_keep(ga, gb, ya, yb):
    return (*_f_merge(ga, gb, ya, yb), yb)


def _f_loss(x2, ff, tgt, gate, shift, scale, w):
    y = _rms(x2 + gate * ff, w) * (1.0 + scale) + shift
    return (0.5 * jnp.mean(jnp.square(y - tgt), axis=-1, keepdims=True),)


def _shift_down(x, s):
    if s == 0:
        return x
    row = lax.broadcasted_iota(jnp.int32, x.shape, 0)
    return jnp.where(row >= s, pltpu.roll(x, s, 0), 0.0)


def _shift_up(x, s):
    if s == 0:
        return x
    n = x.shape[0]
    row = lax.broadcasted_iota(jnp.int32, x.shape, 0)
    return jnp.where(row < n - s, pltpu.roll(x, n - s, 0), 0.0)


def _conv(x, w):
    width = w.shape[0]
    acc = w[width - 1:width, :] * x
    for j in range(width - 1):
        acc = acc + w[j:j + 1, :] * _shift_down(x, width - 1 - j)
    return acc


def _conv_bwd(dy, x, w, dw_ref, first):
    width = w.shape[0]
    dx = w[width - 1:width, :] * dy
    for j in range(width - 1):
        dx = dx + w[j:j + 1, :] * _shift_up(dy, width - 1 - j)
    for j in range(width):
        row = jnp.sum(dy * _shift_down(x, width - 1 - j), axis=0, keepdims=True)
        _accumulate(dw_ref.at[j:j + 1, :], row, first)
    return dx


def _qkv_act(xc, is_v, scale):
    a = _silu(xc)
    nrm = a * lax.rsqrt(jnp.sum(a * a, axis=-1, keepdims=True) + EPS) * scale
    return jnp.where(is_v, a, nrm)


def _qkv_act_bwd(xc, dout, is_v, scale):
    sig = jax.nn.sigmoid(xc)
    a = xc * sig
    r = lax.rsqrt(jnp.sum(a * a, axis=-1, keepdims=True) + EPS)
    c1 = r * scale
    da = c1 * dout - a * (c1 * r * r * jnp.sum(dout * a, axis=-1, keepdims=True))
    return jnp.where(is_v, dout, da) * (sig * (1.0 + xc * (1.0 - sig)))


def _qkv_consts(j, heads):
    is_v = j >= 2 * heads
    scale = jnp.where(j < heads, HEAD ** -0.5, 1.0).astype(F32)
    return is_v, scale


def _qkv_fwd(p, w, heads, name):
    bl, s, w3 = p.shape

    def body(p_ref, w_ref, o_ref):
        is_v, scale = _qkv_consts(pl.program_id(0), heads)
        o_ref[...] = _qkv_act(_conv(p_ref[...], w_ref[...]), is_v, scale)

    blk = pl.BlockSpec((None, s, HEAD), lambda j, b: (b, 0, j))
    return pl.pallas_call(
        body, name=name, grid=(w3 // HEAD, bl), in_specs=[blk, pl.BlockSpec((w.shape[0], HEAD), lambda j, b: (0, j))],
        out_specs=blk, out_shape=jax.ShapeDtypeStruct(p.shape, F32), compiler_params=_cparams("parallel", "parallel"),
    )(p, w)


def _qkv_bwd(p, w, dout, heads, name):
    bl, s, w3 = p.shape

    def body(p_ref, w_ref, d_ref, dp_ref, dw_ref):
        is_v, scale = _qkv_consts(pl.program_id(0), heads)
        x, wv = p_ref[...], w_ref[...]
        dxc = _qkv_act_bwd(_conv(x, wv), d_ref[...], is_v, scale)
        dp_ref[...] = _conv_bwd(dxc, x, wv, dw_ref, pl.program_id(1) == 0).astype(dp_ref.dtype)

    blk = pl.BlockSpec((None, s, HEAD), lambda j, b: (b, 0, j))
    wblk = pl.BlockSpec((w.shape[0], HEAD), lambda j, b: (0, j))
    return pl.pallas_call(
        body, name=name, grid=(w3 // HEAD, bl), in_specs=[blk, wblk, blk], out_specs=[blk, wblk],
        out_shape=[jax.ShapeDtypeStruct(p.shape, MXU_DTYPE), jax.ShapeDtypeStruct(w.shape, F32)],
        compiler_params=_cparams("arbitrary", "arbitrary"),
    )(p, w, dout)


def _sc_specs(p, w):
    bl, s, w3 = p.shape
    nblk = w3 // 3 // LANE
    sec = lambda k: pl.BlockSpec((None, s, LANE), functools.partial(lambda j, b, k: (b, 0, k * nblk + j), k=k))
    return nblk, [sec(0), sec(1), sec(2)], pl.BlockSpec((w.shape[0], LANE), lambda j, b: (0, j)), \
        pl.BlockSpec((None, s, LANE), lambda j, b: (b, 0, j))


def _sc_fwd(p, w, name):
    bl, s, w3 = p.shape
    nblk, secs, wblk, oblk = _sc_specs(p, w)

    def body(b_ref, c_ref, x_ref, w_ref, o_ref):
        o_ref[...] = (b_ref[...] * _conv(c_ref[...] * x_ref[...], w_ref[...])).astype(o_ref.dtype)

    return pl.pallas_call(
        body, name=name, grid=(nblk, bl), in_specs=secs + [wblk], out_specs=oblk,
        out_shape=jax.ShapeDtypeStruct((bl, s, w3 // 3), MXU_DTYPE), compiler_params=_cparams("parallel", "parallel"),
    )(p, p, p, w)


def _sc_bwd(p, w, dout, name):
    bl, s, w3 = p.shape
    nblk, secs, wblk, oblk = _sc_specs(p, w)

    def body(b_ref, c_ref, x_ref, w_ref, d_ref, dp_ref, dw_ref):
        gb, gc, xin, wv, d = b_ref[...], c_ref[...], x_ref[...], w_ref[...], d_ref[...]
        u = gc * xin
        dp_ref[0] = (d * _conv(u, wv)).astype(dp_ref.dtype)
        du = _conv_bwd(d * gb, u, wv, dw_ref, pl.program_id(1) == 0)
        dp_ref[1] = (du * xin).astype(dp_ref.dtype)
        dp_ref[2] = (du * gc).astype(dp_ref.dtype)

    return pl.pallas_call(
        body, name=name, grid=(nblk, bl), in_specs=secs + [wblk, oblk],
        out_specs=[pl.BlockSpec((3, None, s, LANE), lambda j, b: (0, b, 0, j)), wblk],
        out_shape=[jax.ShapeDtypeStruct((3, bl, s, w3 // 3), MXU_DTYPE), jax.ShapeDtypeStruct(w.shape, F32)],
        compiler_params=_cparams("arbitrary", "arbitrary"),
    )(p, p, p, w, dout)


def _bdot(a, b, ca, cb):
    return lax.dot_general(a.astype(MXU_DTYPE), b.astype(MXU_DTYPE), (((ca,), (cb,)), ((), ())),
                           preferred_element_type=F32)


def _hdot(a, b):
    return lax.dot_general(a, b, (((1,), (0,)), ((), ())), precision=HIGHEST, preferred_element_type=F32)


def _lane_col(x, idx):
    lane = lax.broadcasted_iota(jnp.int32, x.shape, 1)
    return jnp.sum(jnp.where(lane == idx, x, 0.0), axis=1, keepdims=True)


def _chunk_masks():
    r = lax.broadcasted_iota(jnp.int32, (CHUNK, CHUNK), 0)
    c = lax.broadcasted_iota(jnp.int32, (CHUNK, CHUNK), 1)
    return r == c, r >= c, r > c


def _dot3(a, b):
    ah, bh = a.astype(MXU_DTYPE), b.astype(MXU_DTYPE)
    al, bl = (a - ah.astype(F32)).astype(MXU_DTYPE), (b - bh.astype(F32)).astype(MXU_DTYPE)
    dot = lambda x, y: lax.dot_general(x, y, (((1,), (0,)), ((), ())), preferred_element_type=F32)
    return dot(ah, bh) + (dot(ah, bl) + dot(al, bh))


def _tri_inv_steps(low, eye):
    x = -low
    p = jnp.where(eye, 1.0, 0.0) + x
    span = 2
    while span < CHUNK:
        x = _dot3(x, x)
        yield
        p = p + _dot3(p, x)
        yield
        span *= 2
    return p


def _round_robin(gens):
    out, live = [None] * len(gens), list(range(len(gens)))
    while live:
        still = []
        for i in live:
            try:
                next(gens[i])
                still.append(i)
            except StopIteration as stop:
                out[i] = stop.value
        live = still
    return out


def _gdn_pre(q, k, v, gc, beta, masks):
    eye, causal, strict = masks
    gc_row = jnp.sum(jnp.where(eye, gc, 0.0), axis=0, keepdims=True)
    decay = jnp.where(causal, jnp.exp(jnp.where(causal, gc - gc_row, 0.0)), 0.0)
    eg = jnp.exp(gc)
    gl = gc[CHUNK - 1:CHUNK, :]
    kb, vb = k * beta, v * beta
    both = _bdot(jnp.concatenate([kb, q], axis=0), k, 1, 1)
    low = jnp.where(strict, both[:CHUNK] * decay, 0.0)
    qk = jnp.where(causal, both[CHUNK:] * decay, 0.0)
    rest = jnp.exp(gl - gc)
    return dict(decay=decay, eg=eg, gl=gl, kb=kb, vb=vb, kbe=kb * eg, low=low, qk=qk, qg=q * eg, rest=rest, kdec=k * rest)


GROUP = 4


def _gdn_specs(qkv, gbeta, heads, rev):
    bl, s, w3 = qkv.shape
    d, n = w3 // 3, s // CHUNK
    group = GROUP if n % GROUP == 0 else 1
    steps = n // group
    at = (lambda c: steps - 1 - c) if rev else (lambda c: c)
    assert d == heads * HEAD
    rows = group * CHUNK
    sec = pl.BlockSpec((None, rows, w3), lambda b, c: (b, at(c), 0))
    gspec = pl.BlockSpec((None, rows, LANE), lambda b, c: (b, at(c), 0))
    ospec = pl.BlockSpec((None, rows, d), lambda b, c: (b, at(c), 0))
    sspec = pl.BlockSpec((None, group, heads, HEAD, HEAD), lambda b, c: (b, at(c), 0, 0, 0))
    tspec = pl.BlockSpec((None, group, heads, CHUNK, CHUNK), lambda b, c: (b, at(c), 0, 0, 0))
    return bl, s, d, n, group, sec, gspec, ospec, sspec, tspec


def _gdn_fwd(qkv, gbeta, heads, name):
    bl, s, d, n, group, sec, gspec, ospec, sspec, tspec = _gdn_specs(qkv, gbeta, heads, False)
    rows = lambda sub: slice(sub * CHUNK, (sub + 1) * CHUNK)
    pairs = [(h, sub) for h in range(heads) for sub in range(group)]

    def body(x_ref, g_ref, o_ref, s_ref, t_ref, st_ref):
        @pl.when(pl.program_id(1) == 0)
        def _():
            st_ref[...] = jnp.zeros_like(st_ref)

        masks = _chunk_masks()
        eye, causal, _ = masks
        gblks = [g_ref[rows(sub), :] for sub in range(group)]
        gcs = [_hdot(jnp.where(causal, 1.0, 0.0), gb) for gb in gblks]
        st_all = st_ref[...]

        def free(h, sub):
            q, k, v = (x_ref[rows(sub), sec * d + h * HEAD:sec * d + (h + 1) * HEAD] for sec in range(3))
            pre = _gdn_pre(q, k, v, _lane_col(gcs[sub], h), _lane_col(gblks[sub], heads + h), masks)
            yield
            t = yield from _tri_inv_steps(pre["low"], eye)
            uw = _bdot(t, jnp.concatenate([pre["vb"], pre["kbe"]], axis=1), 1, 0)
            return pre, t, uw[:, :HEAD], uw[:, HEAD:]

        pieces = dict(zip(pairs, _round_robin([free(h, sub) for h, sub in pairs])))

        def carry(h):
            st, outs, starts = st_all[h], [], []
            for sub in range(group):
                pre, _, u, w = pieces[h, sub]
                starts.append(st)
                vnew = u - _bdot(w, st, 1, 0)
                yield
                outs.append(_bdot(pre["qg"], st, 1, 0) + _bdot(pre["qk"], vnew, 1, 0))
                st = st * jnp.exp(pre["gl"]) + _bdot(pre["kdec"], vnew, 0, 0)
                yield
            return outs, starts, st

        carried = _round_robin([carry(h) for h in range(heads)])
        per_sub = lambda pick: [[pick(h, sub) for h in range(heads)] for sub in range(group)]
        o_ref[...] = jnp.concatenate([jnp.concatenate(r, axis=1) for r in per_sub(lambda h, sub: carried[h][0][sub])], axis=0)
        s_ref[...] = jnp.stack([jnp.stack(r) for r in per_sub(lambda h, sub: carried[h][1][sub])])
        t_ref[...] = jnp.stack([jnp.stack(r) for r in per_sub(lambda h, sub: pieces[h, sub][1])])
        st_ref[...] = jnp.stack([carried[h][2] for h in range(heads)])

    return pl.pallas_call(
        body, name=name, grid=(bl, n // group), in_specs=[sec, gspec], out_specs=[ospec, sspec, tspec],
        out_shape=[jax.ShapeDtypeStruct((bl, s, d), F32), jax.ShapeDtypeStruct((bl, n, heads, HEAD, HEAD), F32),
                   jax.ShapeDtypeStruct((bl, n, heads, CHUNK, CHUNK), F32)],
        scratch_shapes=[pltpu.VMEM((heads, HEAD, HEAD), F32)], compiler_params=_cparams("parallel", "arbitrary"),
    )(qkv, gbeta)


def _gdn_bwd(qkv, gbeta, dout, s_all, t_all, heads, name):
    bl, s, d, n, group, sec, gspec, ospec, sspec, tspec = _gdn_specs(qkv, gbeta, heads, True)
    rows = lambda sub: slice(sub * CHUNK, (sub + 1) * CHUNK)
    pairs = [(h, sub) for h in range(heads) for sub in range(group)]
    stack, side = functools.partial(jnp.concatenate, axis=0), functools.partial(jnp.concatenate, axis=1)

    def body(x_ref, g_ref, do_ref, s_ref, t_ref, dx_ref, dg_ref, ds_ref):
        @pl.when(pl.program_id(1) == 0)
        def _():
            ds_ref[...] = jnp.zeros_like(ds_ref)

        masks = _chunk_masks()
        eye, causal, strict = masks
        gblks = [g_ref[rows(sub), :] for sub in range(group)]
        gcs = [_hdot(jnp.where(causal, 1.0, 0.0), gb) for gb in gblks]
        lane = lax.broadcasted_iota(jnp.int32, (CHUNK, LANE), 1)
        last_row = lax.broadcasted_iota(jnp.int32, (CHUNK, 1), 0) == CHUNK - 1
        rowsum = lambda a: jnp.sum(a, axis=1, keepdims=True)
        st_all, t_all_, ds_all = s_ref[...], t_ref[...], ds_ref[...]

        def free(h, sub):
            q, k, v = (x_ref[rows(sub), sec * d + h * HEAD:sec * d + (h + 1) * HEAD] for sec in range(3))
            do = do_ref[rows(sub), h * HEAD:(h + 1) * HEAD]
            beta = _lane_col(gblks[sub], heads + h)
            st, t = st_all[sub, h], t_all_[sub, h]
            pre = _gdn_pre(q, k, v, _lane_col(gcs[sub], h), beta, masks)
            yield
            uw = _bdot(t, side([pre["vb"], pre["kbe"]]), 1, 0)
            u, w = uw[:, :HEAD], uw[:, HEAD:]
            yield
            vnew = u - _bdot(w, st, 1, 0)
            yield
            dqk = jnp.where(causal, _bdot(do, vnew, 1, 1), 0.0)
            dqg = _bdot(do, st, 1, 1)
            return dict(q=q, k=k, v=v, do=do, beta=beta, st=st, t=t, pre=pre, w=w, vnew=vnew, dqk=dqk, dqg=dqg)

        pieces = dict(zip(pairs, _round_robin([free(h, sub) for h, sub in pairs])))

        def carry(h):
            dsn, outs = ds_all[h], {}
            for sub in reversed(range(group)):
                pc = pieces[h, sub]
                pre, st, do = pc["pre"], pc["st"], pc["do"]
                egl = jnp.exp(pre["gl"])
                dkdec = _bdot(pc["vnew"], dsn, 1, 1)
                dvnew = _bdot(pre["kdec"], dsn, 1, 0) + _bdot(pre["qk"], do, 0, 0)
                dgl = jnp.sum(dsn * st, keepdims=True) * egl
                yield
                dw = -_bdot(dvnew, st, 1, 1)
                dsn = dsn * egl + _bdot(stack([pre["qg"], -pc["w"]]), stack([do, dvnew]), 0, 0)
                outs[sub] = (dkdec, dvnew, dgl, dw)
                yield
            return outs, dsn

        carried = _round_robin([carry(h) for h in range(heads)])

        def rest(h, sub):
            pc = pieces[h, sub]
            dkdec, dvnew, dgl, dw = carried[h][0][sub]
            q, k, v, beta, t, pre, dqk, dqg = (pc[x] for x in ("q", "k", "v", "beta", "t", "pre", "dqk", "dqg"))
            decay, eg, kb, vb, kbe, low, qk, qg, kdec = (pre[x] for x in ("decay", "eg", "kb", "vb", "kbe", "low", "qk", "qg", "kdec"))
            dt = _bdot(side([dvnew, dw]), side([vb, kbe]), 1, 1)
            by_t = _bdot(t, side([dvnew, dw]), 0, 0)
            dvb, dkbe = by_t[:, :HEAD], by_t[:, HEAD:]
            yield
            inner = _bdot(dt, t, 1, 1)
            yield
            dlow = -jnp.where(strict, _bdot(t, inner, 0, 0), 0.0)
            da, db = dlow * decay, dqk * decay
            yield
            m = dlow * low + dqk * qk
            kdk = dkdec * kdec
            col_of_m = jnp.sum(jnp.where(eye, jnp.sum(m, axis=0, keepdims=True), 0.0), axis=1, keepdims=True)
            dgc = rowsum(m) - col_of_m + rowsum(dqg * qg) + rowsum(dkbe * kbe) - rowsum(kdk)
            dgc = dgc + jnp.where(last_row, dgl + jnp.sum(kdk, keepdims=True), 0.0)
            by_k = _bdot(stack([da, db]), k, 1, 0)
            dkb = by_k[:CHUNK] + dkbe * eg
            yield
            dk = _bdot(stack([da, db]), stack([kb, q]), 0, 0) + dkdec * pre["rest"] + dkb * beta
            dq = by_k[CHUNK:] + dqg * eg
            dbeta = rowsum(dkb * k) + rowsum(dvb * v)
            return dq, dk, dvb * beta, jnp.where(lane == h, dgc, 0.0) + jnp.where(lane == heads + h, dbeta, 0.0)

        done = dict(zip(pairs, _round_robin([rest(h, sub) for h, sub in pairs])))
        dx_ref[...] = stack([side([done[h, sub][i] for i in range(3) for h in range(heads)]) for sub in range(group)])
        ds_ref[...] = jnp.stack([carried[h][1] for h in range(heads)])
        upper = jnp.where(jnp.logical_or(eye, jnp.logical_not(causal)), 1.0, 0.0)
        dgs = []
        for sub in range(group):
            dgb = done[0, sub][3]
            for h in range(1, heads):
                dgb = dgb + done[h, sub][3]
            dgs.append(jnp.where(lane < heads, _hdot(upper, dgb), dgb))
        dg_ref[...] = stack(dgs)

    return pl.pallas_call(
        body, name=name, grid=(bl, n // group), in_specs=[sec, gspec, ospec, sspec, tspec], out_specs=[sec, gspec],
        out_shape=[jax.ShapeDtypeStruct(qkv.shape, F32), jax.ShapeDtypeStruct((bl, s, LANE), F32)],
        scratch_shapes=[pltpu.VMEM((heads, HEAD, HEAD), F32)], compiler_params=_cparams("parallel", "arbitrary"),
    )(qkv, gbeta, dout, s_all, t_all)


def _position():
    return lax.axis_index("x"), lax.axis_index("y"), lax.axis_index("c")


def _all_gather(x, *, name):
    space = pltpu.VMEM

    def body(x_ref, out_ref, send_sems, recv_sems, local_sem):
        ax, ay, ac = _position()
        me, sibling = (ax, ay, ac), (ax, ay, 1 - ac)
        chips = [(1 - ax, ay), (ax, 1 - ay), (1 - ax, 1 - ay)]

        def slot(px, py, pc):
            return out_ref.at[4 * px + 2 * py + pc]

        def copy(k, block, to, src=None):
            return pltpu.make_async_remote_copy(
                src_ref=slot(*block) if src is None else src, dst_ref=slot(*block), send_sem=send_sems.at[k],
                recv_sem=recv_sems.at[k], device_id=to, device_id_type=MESH_IDS)

        mine = pltpu.make_async_copy(x_ref, slot(*me), local_sem)
        mine.start()
        first = [copy(0, me, sibling, src=x_ref)] + [copy(1 + j, me, (*chip, ac), src=x_ref) for j, chip in enumerate(chips)]
        for cp in first:
            cp.start()
        passed = [copy(4 + j, (*chip, ac), sibling) for j, chip in enumerate(chips)]
        for j, chip in enumerate(chips):
            copy(1 + j, (*chip, ac), me).wait_recv()
            passed[j].start()
        copy(0, sibling, me).wait_recv()
        for j, chip in enumerate(chips):
            copy(4 + j, (*chip, 1 - ac), me).wait_recv()
        for cp in first + passed:
            cp.wait_send()
        mine.wait()

    return pl.pallas_call(
        body, name=name, out_shape=jax.ShapeDtypeStruct((NDEV,) + x.shape, x.dtype),
        in_specs=[pl.BlockSpec(memory_space=space)], out_specs=pl.BlockSpec(memory_space=space),
        scratch_shapes=[pltpu.SemaphoreType.DMA((7,)), pltpu.SemaphoreType.DMA((7,)), pltpu.SemaphoreType.DMA],
    )(x)


class _Rider:
    def __init__(self, arrays, out_shapes, sems, hooks):
        self.arrays, self.out_shapes, self.sems, self.hooks = arrays, out_shapes, sems, hooks


def _gather_rider(xs):
    n = len(xs)

    def hooks(x_refs, out_refs, send_sems, recv_sems):
        ax, ay, ac = _position()
        me, sibling = (ax, ay, ac), (ax, ay, 1 - ac)
        chips = [(1 - ax, ay), (ax, 1 - ay), (1 - ax, 1 - ay)]

        def copies(k, block, to, own=False):
            out = []
            for i in range(n):
                slot = out_refs[i].at[4 * block[0] + 2 * block[1] + block[2]]
                out.append(pltpu.make_async_remote_copy(
                    src_ref=x_refs[i] if own else slot, dst_ref=slot, send_sem=send_sems.at[k, i], recv_sem=recv_sems.at[k, i],
                    device_id=to, device_id_type=MESH_IDS))
            return out

        def first():
            for cp in copies(0, me, sibling, own=True):
                cp.start()
            for j, chip in enumerate(chips):
                for cp in copies(1 + j, me, (*chip, ac), own=True):
                    cp.start()

        def mid():
            for j, chip in enumerate(chips):
                for arrived, onward in zip(copies(1 + j, (*chip, ac), me), copies(4 + j, (*chip, ac), sibling)):
                    arrived.wait_recv()
                    onward.start()

        def last():
            for cp in copies(0, sibling, me):
                cp.wait_recv()
            for j, chip in enumerate(chips):
                for cp in copies(4 + j, (*chip, 1 - ac), me):
                    cp.wait_recv()
            for cp in copies(0, me, sibling, own=True):
                cp.wait_send()
            for j, chip in enumerate(chips):
                for cp in copies(1 + j, me, (*chip, ac), own=True) + copies(4 + j, (*chip, ac), sibling):
                    cp.wait_send()

        return first, mid, last

    return _Rider(list(xs), [jax.ShapeDtypeStruct((NDEV,) + x.shape, x.dtype) for x in xs],
                  [pltpu.SemaphoreType.DMA((7, n)), pltpu.SemaphoreType.DMA((7, n))], hooks)


def _scatter_rider(parts):
    packed = sum(r for _, r in parts)
    width, dtype = parts[0][0].shape[1], parts[0][0].dtype

    def hooks(g_refs, out_refs, send_sems, recv_sems):
        (recv_ref,) = out_refs
        ax, ay, ac = _position()

        def peer(rel):
            flip = lambda a, bit: 1 - a if rel & bit else a
            return flip(ax, 4), flip(ay, 2), flip(ac, 1)

        def first():
            for rel in range(1, NDEV):
                px, py, pc = peer(rel)
                off = 0
                for g_ref, (_, r) in zip(g_refs, parts):
                    rows = g_ref.at[pl.ds(pl.multiple_of((4 * px + 2 * py + pc) * r, ROW_ALIGN), r)]
                    pltpu.make_async_remote_copy(
                        src_ref=rows, dst_ref=recv_ref.at[rel - 1, pl.ds(off, r)], send_sem=send_sems.at[rel - 1],
                        recv_sem=recv_sems.at[rel - 1], device_id=(px, py, pc), device_id_type=MESH_IDS).start()
                    off += r

        def last():
            for rel in range(1, NDEV):
                slot = recv_ref.at[rel - 1]
                pltpu.make_async_remote_copy(src_ref=slot, dst_ref=slot, send_sem=send_sems.at[rel - 1],
                                             recv_sem=recv_sems.at[rel - 1], device_id=peer(rel), device_id_type=MESH_IDS).wait()

        return first, lambda: None, last

    return _Rider([g for g, _ in parts], [jax.ShapeDtypeStruct((NDEV - 1, packed, width), dtype)],
                  [pltpu.SemaphoreType.DMA((NDEV - 1,)), pltpu.SemaphoreType.DMA((NDEV - 1,))], hooks)


def _sum_direct(own, recv, name):
    r, w = own.shape
    tr = max(t for t in range(ROW_ALIGN, 257, ROW_ALIGN) if r % t == 0)

    def body(own_ref, *refs):
        acc = own_ref[...].astype(F32)
        for ref in refs[:-1]:
            acc = acc + ref[...].astype(F32)
        refs[-1][...] = acc

    rblk = lambda k: pl.BlockSpec((None, tr, w), functools.partial(lambda i, k: (k, i, 0), k=k))
    blk = pl.BlockSpec((tr, w), lambda i: (i, 0))
    return pl.pallas_call(body, name=name, grid=(r // tr,), in_specs=[blk] + [rblk(k) for k in range(NDEV - 1)],
                          out_specs=blk, out_shape=jax.ShapeDtypeStruct((r, w), F32),
                          compiler_params=_cparams("parallel"))(own, *([recv] * (NDEV - 1)))


ROW_ALIGN = 16


def _window_start(rows_per_dev, k):
    return rows_per_dev * k // ROW_ALIGN * ROW_ALIGN


def _exchange_in_chip(parts, name, collective_id):
    packed = sum(win for _, _, win, _ in parts)
    width, dtype = parts[0][0].shape[1], parts[0][0].dtype

    def body(g_refs, out_refs, send_sems, recv_sems):
        (recv_ref,) = out_refs
        ax, ay, ac = _position()
        sibling = (ax, ay, 1 - ac)
        _handshake([sibling])
        for q in range(4):
            for g_ref, (_, r, win, off) in zip(g_refs, parts):
                there = g_ref.at[pl.ds(pl.multiple_of(_window_start(r, 2 * q + 1 - ac), ROW_ALIGN), win)]
                pltpu.make_async_remote_copy(src_ref=there, dst_ref=recv_ref.at[q, pl.ds(off, win)], send_sem=send_sems.at[q],
                                             recv_sem=recv_sems.at[q], device_id=sibling, device_id_type=MESH_IDS).start()
        for q in range(4):
            pltpu.make_async_remote_copy(src_ref=recv_ref.at[q], dst_ref=recv_ref.at[q], send_sem=send_sems.at[q],
                                         recv_sem=recv_sems.at[q], device_id=sibling, device_id_type=MESH_IDS).wait()

    return _on_sequencer(body, [g for g, _, _, _ in parts], [jax.ShapeDtypeStruct((4, packed, width), dtype)],
                         [pltpu.SemaphoreType.DMA((4,)), pltpu.SemaphoreType.DMA((4,))], name=name, collective_id=collective_id)[0]


def _on_sequencer(body, ins, out_shapes, sems, *, name, collective_id):
    hbm = pltpu.MemorySpace.HBM
    in_refs = [jax.new_ref(a, memory_space=hbm) for a in ins]
    out_refs = [jax.empty_ref(s, memory_space=hbm) for s in out_shapes]

    @pl.kernel(mesh=plsc.ScalarSubcoreMesh(axis_name="sequencer", num_cores=1), name=name, scratch_types=tuple(sems),
               compiler_params=pltpu.CompilerParams(collective_id=collective_id))
    def launch(*sem_refs):
        body(in_refs, out_refs, *sem_refs)

    launch()
    return [r[...] for r in out_refs]


def _handshake(peers):
    barrier = pltpu.get_barrier_semaphore()
    for peer in peers:
        pl.semaphore_signal(barrier, inc=1, device_id=peer, device_id_type=MESH_IDS)
    pl.semaphore_wait(barrier, len(peers))


def _exchange_chips_async(s1, name, collective_id):
    def body(in_refs, out_refs, send_sems, recv_sems):
        (src,), (got,) = in_refs, out_refs
        ax, ay, ac = _position()
        chips = [(1 - ax, ay), (ax, 1 - ay), (1 - ax, 1 - ay)]
        _handshake([(cx, cy, ac) for cx, cy in chips])
        copies = [pltpu.make_async_remote_copy(
            src_ref=src.at[2 * cx + cy], dst_ref=got.at[r], send_sem=send_sems.at[r], recv_sem=recv_sems.at[r],
            device_id=(cx, cy, ac), device_id_type=MESH_IDS) for r, (cx, cy) in enumerate(chips)]
        for cp in copies:
            cp.start()
        for cp in copies:
            cp.wait_recv()
        for cp in copies:
            cp.wait_send()

    return _on_sequencer(body, [s1], [jax.ShapeDtypeStruct((3,) + s1.shape[1:], s1.dtype)],
                         [pltpu.SemaphoreType.DMA((3,)), pltpu.SemaphoreType.DMA((3,))], name=name, collective_id=collective_id)[0]


def _gather_async(xs, name, collective_id):
    rider = _gather_rider(xs)

    def body(in_refs, out_refs, send_sems, recv_sems):
        ax, ay, ac = _position()
        _handshake([(ax, ay, 1 - ac), (1 - ax, ay, ac), (ax, 1 - ay, ac), (1 - ax, 1 - ay, ac)])
        for hook in rider.hooks(in_refs, out_refs, send_sems, recv_sems):
            hook()

    return _on_sequencer(body, rider.arrays, rider.out_shapes, rider.sems, name=name, collective_id=collective_id)


def _gather_balanced(x, name, collective_id):
    m = x.shape[0]
    half = m // 2 // ROW_ALIGN * ROW_ALIGN
    parts = {"all": pl.ds(0, m), "lo": pl.ds(0, half), "hi": pl.ds(half, m - half)}

    def body(in_refs, out_refs, send_sems, recv_sems):
        (x_ref,), (out_ref,) = in_refs, out_refs
        ax, ay, ac = _position()
        me, sibling = (ax, ay, ac), (ax, ay, 1 - ac)
        by_x, by_y, diag = (1 - ax, ay), (ax, 1 - ay), (1 - ax, 1 - ay)
        _handshake([sibling, (*by_x, ac), (*by_y, ac)])

        def copy(k, block, part, to, own=False):
            rows = out_ref.at[4 * block[0] + 2 * block[1] + block[2], parts[part]]
            return pltpu.make_async_remote_copy(src_ref=x_ref if own else rows, dst_ref=rows, send_sem=send_sems.at[k],
                                                recv_sem=recv_sems.at[k], device_id=to, device_id_type=MESH_IDS)

        sends = [copy(0, me, "all", sibling, own=True), copy(1, me, "all", (*by_x, ac), own=True),
                 copy(2, me, "all", (*by_y, ac), own=True),
                 copy(3, (*by_x, ac), "lo", (*by_y, ac)), copy(4, (*by_y, ac), "hi", (*by_x, ac)),
                 copy(5, (*by_x, ac), "all", sibling), copy(6, (*by_y, ac), "all", sibling),
                 copy(7, (*diag, ac), "lo", sibling), copy(8, (*diag, ac), "hi", sibling)]
        arrivals = [copy(0, sibling, "all", me), copy(1, (*by_x, ac), "all", me), copy(2, (*by_y, ac), "all", me),
                    copy(3, (*diag, ac), "lo", me), copy(4, (*diag, ac), "hi", me),
                    copy(5, (*by_x, 1 - ac), "all", me), copy(6, (*by_y, 1 - ac), "all", me),
                    copy(7, (*diag, 1 - ac), "lo", me), copy(8, (*diag, 1 - ac), "hi", me)]
        for k in (0, 1, 2):
            sends[k].start()
        arrivals[1].wait_recv()
        sends[3].start()
        sends[5].start()
        arrivals[2].wait_recv()
        sends[4].start()
        sends[6].start()
        arrivals[3].wait_recv()
        sends[7].start()
        arrivals[4].wait_recv()
        sends[8].start()
        for k in (0, 5, 6, 7, 8):
            arrivals[k].wait_recv()
        for cp in sends:
            cp.wait_send()

    return _on_sequencer(body, [x], [jax.ShapeDtypeStruct((NDEV,) + x.shape, x.dtype)],
                         [pltpu.SemaphoreType.DMA((9,)), pltpu.SemaphoreType.DMA((9,))], name=name, collective_id=collective_id)[0]


def _scatter_async(parts, name, collective_id):
    rider = _scatter_rider(parts)

    def body(in_refs, out_refs, send_sems, recv_sems):
        ax, ay, ac = _position()
        flip = lambda a, on: 1 - a if on else a
        _handshake([(flip(ax, rel & 4), flip(ay, rel & 2), flip(ac, rel & 1)) for rel in range(1, NDEV)])
        for hook in rider.hooks(in_refs, out_refs, send_sems, recv_sems):
            hook()

    return _on_sequencer(body, rider.arrays, rider.out_shapes, rider.sems, name=name, collective_id=collective_id)[0]


def _sum_in_chip(own, recv, name):
    _, r, w = own.shape
    tr = _tile(r, (256, 128))

    def body(a_ref, b_ref, o_ref):
        o_ref[...] = (a_ref[...].astype(F32) + b_ref[...].astype(F32)).astype(o_ref.dtype)

    blk = pl.BlockSpec((None, tr, w), lambda q, i: (q, i, 0))
    return pl.pallas_call(body, name=name, grid=(4, r // tr), in_specs=[blk, blk], out_specs=blk,
                          out_shape=jax.ShapeDtypeStruct(own.shape, own.dtype),
                          compiler_params=_cparams("parallel", "parallel"))(own, recv)


def _sum_chips(s1, recv, chip, name):
    _, r, w = s1.shape
    tr = _tile(r, (256, 128))

    def body(c_ref, s_ref, r0_ref, r1_ref, r2_ref, o_ref):
        f = lambda ref: ref[...].astype(F32)
        o_ref[...] = ((f(s_ref) + f(r0_ref)) + f(r1_ref)) + f(r2_ref)

    rblk = lambda k: pl.BlockSpec((None, tr, w), functools.partial(lambda i, c, k: (k, i, 0), k=k))
    grid_spec = pltpu.PrefetchScalarGridSpec(
        num_scalar_prefetch=1, grid=(r // tr,),
        in_specs=[pl.BlockSpec((None, tr, w), lambda i, c: (c[0], i, 0)), rblk(0), rblk(1), rblk(2)],
        out_specs=pl.BlockSpec((tr, w), lambda i, c: (i, 0)))
    return pl.pallas_call(body, name=name, grid_spec=grid_spec, out_shape=jax.ShapeDtypeStruct((r, w), F32),
                          compiler_params=_cparams("parallel"))(chip, s1, recv, recv, recv)


def _silu_rows(x, name):
    def body(x_ref, o_ref):
        o_ref[...] = _silu(x_ref[...])

    return pl.pallas_call(body, name=name, out_shape=jax.ShapeDtypeStruct(x.shape, F32))(x)


def _row_sum(x, name):
    def body(x_ref, o_ref):
        acc = x_ref[0:1, :]
        for i in range(1, x.shape[0]):
            acc = acc + x_ref[i:i + 1, :]
        o_ref[...] = acc

    return pl.pallas_call(body, name=name, out_shape=jax.ShapeDtypeStruct((1, x.shape[1]), F32))(x)


def _adamw(w, g, m, v, name):
    cols = w.shape[-1]
    rows = w.size // cols
    tr = _tile(rows, (128,))
    tc = LANE if (tr == rows and rows > 512 and cols % LANE == 0) else cols

    def body(w_ref, g_ref, m_ref, v_ref, d_ref, mo_ref, vo_ref):
        grad = g_ref[...]
        m_new = ADAM_B1 * m_ref[...] + (1.0 - ADAM_B1) * grad
        v_new = ADAM_B2 * v_ref[...] + (1.0 - ADAM_B2) * jnp.square(grad)
        m_hat = m_new / (1.0 - ADAM_B1 ** ADAM_STEP)
        v_hat = v_new / (1.0 - ADAM_B2 ** ADAM_STEP)
        d_ref[...] = -ADAM_LR * (m_hat / (jnp.sqrt(v_hat) + ADAM_EPS) + ADAM_WD * w_ref[...])
        mo_ref[...] = m_new
        vo_ref[...] = v_new

    blk = pl.BlockSpec((tr, tc), lambda i, j: (i, j))
    out = pl.pallas_call(
        body, name=name, grid=(rows // tr, cols // tc), in_specs=[blk] * 4, out_specs=[blk] * 3,
        out_shape=[jax.ShapeDtypeStruct((rows, cols), F32)] * 3, compiler_params=_cparams("parallel", "parallel"),
    )(*[t.reshape(rows, cols) for t in (w, g, m, v)])
    return [t.reshape(w.shape) for t in out]


def _pack(parts, width, row_mult, dtype):
    flat = jnp.concatenate([p.reshape(-1).astype(dtype) for p in parts])
    rows = -(-flat.shape[0] // (width * row_mult)) * row_mult
    return jnp.pad(flat, (0, rows * width - flat.shape[0])).reshape(rows, width)


def _unpack(flat, shapes):
    out, off = [], 0
    for shp in shapes:
        size = 1
        for dim in shp:
            size *= dim
        out.append(flat[:, off:off + size].reshape((flat.shape[0],) + tuple(shp)))
        off += size
    return out


def _devices_to_cols(a):
    _, r, c = a.shape
    return a.transpose(1, 0, 2).reshape(r, NDEV * c)


def kernel(x, c, w_ada, b_ada, norm1_w, w_in, gdn_conv_w, gdn_a_log, gdn_dt_bias, gdn_norm_w, w_gdn_proj, sc_conv_w, w_sc_out, w_o, norm2_w, w_ffn_in, w_ffn_out, w_ada_f, b_ada_f, normf_w, loss_target, m_w_ada, m_b_ada, m_norm1_w, m_w_in, m_gdn_conv_w, m_gdn_a_log, m_gdn_dt_bias, m_gdn_norm_w, m_w_gdn_proj, m_sc_conv_w, m_w_sc_out, m_w_o, m_norm2_w, m_w_ffn_in, m_w_ffn_out, m_w_ada_f, m_b_ada_f, m_normf_w, v_w_ada, v_b_ada, v_norm1_w, v_w_in, v_gdn_conv_w, v_gdn_a_log, v_gdn_dt_bias, v_gdn_norm_w, v_w_gdn_proj, v_sc_conv_w, v_w_sc_out, v_w_o, v_norm2_w, v_w_ffn_in, v_w_ffn_out, v_w_ada_f, v_b_ada_f, v_normf_w):
    bl, s, d = x.shape
    heads = gdn_a_log.shape[-1]
    dff = w_ffn_out.shape[1] * NDEV
    tok = bl * s
    ax, ay, ac = _position()
    dev = 4 * ax + 2 * ay + ac
    as_tok = lambda a: a.reshape(bl, s, a.shape[-1])
    as_mat = lambda a: a.reshape(tok, a.shape[-1])

    small = _all_gather(_pack([c, gdn_conv_w, sc_conv_w], LANE, 8, F32), name="gather_cond")
    c_all, conv_w, sc_w = _unpack(small.reshape(NDEV, -1), [(bl, d), gdn_conv_w.shape[1:], sc_conv_w.shape[1:]])
    c_act = _silu_rows(c_all.reshape(NDEV * bl, d), "cond_silu")
    conv_w, sc_w = _devices_to_cols(conv_w), _devices_to_cols(sc_w)
    n_ada, n_adaf = w_ada.shape[-1], w_ada_f.shape[-1]
    bias = jnp.broadcast_to(lax.dynamic_slice_in_dim(b_ada, dev * n_ada, n_ada, axis=1), (NDEV * bl, n_ada))
    biasf = jnp.broadcast_to(lax.dynamic_slice_in_dim(b_ada_f.reshape(1, -1), dev * n_adaf, n_adaf, axis=1), (NDEV * bl, n_adaf))
    mod_cols = _mm(c_act, w_ada[0], add=bias, name="ada_cols")
    modf_cols = _mm(c_act, w_ada_f, add=biasf, name="adaf_cols")
    mods = _all_gather(jnp.concatenate([mod_cols, modf_cols], axis=1), name="gather_mod")
    mod_all = mods[:, :, :n_ada].transpose(1, 0, 2).reshape(NDEV * bl, NDEV * n_ada)
    modf_all = mods[:, :, n_ada:].transpose(1, 0, 2).reshape(NDEV * bl, NDEV * n_adaf)
    my_rows = lambda a: lax.dynamic_slice_in_dim(a, dev * bl, bl, axis=0)
    sh1, sc1, g1, sh2, sc2, g2 = [t.reshape(bl, 1, d) for t in jnp.split(my_rows(mod_all), 6, axis=1)]
    shf, scf = [t.reshape(bl, 1, d) for t in jnp.split(my_rows(modf_all), 2, axis=1)]

    late = [t.astype(MXU_DTYPE) for t in (w_gdn_proj[0], w_sc_out[0], w_o[0], w_ffn_in[0].T, w_ffn_out[0])]
    rows = [t.shape[0] for t in late] + [w_in.shape[-1]]
    offs = [sum(rows[:i]) for i in range(5)]
    in_send = w_in[0].T.astype(MXU_DTYPE)
    with_own = lambda g, own: lax.dynamic_update_slice_in_dim(g, own[None], dev, axis=0)
    wt_in = with_own(_gather_balanced(in_send, "gather_w_in", 1), in_send).reshape(NDEV * rows[5], d)
    gathered = _gather_async(late[:3], "gather_mixer", 2) + _gather_async(late[3:], "gather_ffn", 3)
    wgp, wso, wo, wt_fi, wfo = [with_own(g, own).reshape(NDEV * own.shape[0], d) for g, own in zip(gathered, late)]
    o_z, o_ab, o_sc, o_ga, o_gb = 3 * d, 4 * d, 4 * d + 2 * heads, 7 * d + 2 * heads, 8 * d + 2 * heads
    s_qkv, s_z, s_sc, s_gate = (0, o_z), (o_z, d), (o_sc, 3 * d), (o_ga, 2 * d)
    wt_ab = jnp.pad(wt_in[o_ab:o_sc], ((0, LANE - 2 * heads), (0, 0)))

    n1w, n2w, nfw = norm1_w.reshape(1, d), norm2_w.reshape(1, d), normf_w.reshape(1, d)
    lanes = lambda a: jnp.pad(a.reshape(1, -1), ((0, 0), (0, LANE - a.size)))
    a_log, dt_bias, gnw = lanes(gdn_a_log), lanes(gdn_dt_bias), gdn_norm_w.reshape(1, HEAD)
    f_gates = functools.partial(_f_gates, heads=heads)
    (h1,) = _tok_fwd(_f_norm_mod, [x], [sh1, sc1], [n1w], [(d, MXU_DTYPE)], name="norm1", ts=512)
    h1m = as_mat(h1)
    p_qkv = as_tok(_mm(h1m, wt_in, tb=True, b_rows=s_qkv, name="in_qkv"))
    p_z = as_tok(_mm(h1m, wt_in, tb=True, b_rows=s_z, name="in_z"))
    p_ab = as_tok(_mm(h1m, wt_ab, tb=True, name="in_ab"))
    p_sc = as_tok(_mm(h1m, wt_in, tb=True, b_rows=s_sc, name="in_sc"))
    p_g = as_tok(_mm(h1m, wt_in, tb=True, b_rows=s_gate, name="in_gate"))
    qkv = _qkv_fwd(p_qkv, conv_w, heads, "qkv_conv")
    (gbeta,) = _tok_fwd(f_gates, [p_ab], [], [a_log, dt_bias], [(LANE, F32)], name="gates", ts=512)
    o, s_all, t_all = _gdn_fwd(qkv, gbeta, heads, "gdn")
    (og,) = _tok_fwd(_f_gdn_out, [o, p_z], [], [(gnw, None)], [(d, MXU_DTYPE)], name="gdn_out", ts=2048, wb=HEAD, cols=heads)
    y_a = as_tok(_mm(as_mat(og), wgp, name="gdn_proj"))
    scp = _sc_fwd(p_sc, sc_w, "sc_conv")
    mrg, y_b = _tok_fwd(_f_merge_keep, [(p_g, 0), (p_g, 1), y_a, _Product(scp, wso)], [], [], [(d, MXU_DTYPE), (d, F32)],
                        name="merge", ts=512, wb=d)
    merge_toks = [(p_g, 0), (p_g, 1), y_a, y_b]
    x2, h2, mix = _tok_fwd(_f_res_norm_mod_keep, [x, _Product(mrg, wo)], [g1, sh2, sc2], [n2w],
                           [(d, F32), (d, MXU_DTYPE), (d, F32)], name="norm2", ts=512)
    act, gu_a, gu_b = _ffn_in_swiglu(as_mat(h2), wt_fi, dff, "ffn_in")

    loss_l, (dx2, dff_out, _), (dg2, dshf, dscf), (dnfw,) = _tok_bwd(
        _f_loss, [x2, _Product(as_tok(act), wfo), loss_target], [g2, shf, scf], [nfw], [], [True, True, False], name="loss",
        ts=256, loss=True, tok_dtype=[F32, MXU_DTYPE, None])
    dffm = as_mat(dff_out)
    dgu_a, dgu_b = _ffn_out_bwd_swiglu(dffm, wfo, gu_a, gu_b, "d_ffn_out")
    gmm = functools.partial(_mm, ta=True, out_dtype=MXU_DTYPE)
    gw_ffn_out = gmm(act, dffm, name="g_ffn_out")
    dh2 = _Product(as_tok(dgu_b), wt_fi, b_rows=(dff, dff), add=_Product(as_tok(dgu_a), wt_fi, b_rows=(0, dff)))
    h2m = as_mat(h2)
    gwt_ffn_in = gmm(dgu_a, h2m, out_rows=2 * dff, name="g_ffn_in_a")
    gwt_ffn_in = gmm(dgu_b, h2m, out_rows=2 * dff, row_off=dff, into=gwt_ffn_in, name="g_ffn_in_b")
    ffn_parts = [(gwt_ffn_in, rows[3]), (gw_ffn_out, rows[4])]
    ffn_recv = _scatter_async(ffn_parts, "scatter_ffn", 4)
    (dx_skip, dmix), (dg1, dsh2, dsc2), (dn2w,) = _tok_bwd(
        _f_res_norm_mod, [x, mix], [g1, sh2, sc2], [n2w], [dx2, dh2], [True, True], name="d_norm2", ts=256,
        tok_dtype=[F32, MXU_DTYPE], after=[gwt_ffn_in, gw_ffn_out])
    gw_o = gmm(as_mat(mrg), as_mat(dmix), name="g_mix_out")
    (dga, dgb, dya, dyb), _, _ = _tok_bwd(_f_merge, merge_toks, [], [], [_Product(dmix, wo, tb=True)], [True] * 4,
                                          name="d_merge", ts=256, wb=d, tok_dtype=MXU_DTYPE)
    dyam, dybm = as_mat(dya), as_mat(dyb)
    dog = as_tok(_mm(dyam, wgp, tb=True, name="d_gdn_proj"))
    gw_gdn_proj = gmm(as_mat(og), dyam, name="g_gdn_proj")
    dscp = as_tok(_mm(dybm, wso, tb=True, name="d_sc_out"))
    gw_sc_out = gmm(as_mat(scp), dybm, name="g_sc_out")
    dsc, g_sc_w = _sc_bwd(p_sc, sc_w, dscp, "d_sc_conv")
    mix_parts = [(gw_gdn_proj, rows[0]), (gw_sc_out, rows[1]), (gw_o, rows[2])]
    mix_recv = _scatter_async(mix_parts, "scatter_mixer", 5)
    (do, dz), _, (g_gnw,) = _tok_bwd(_f_gdn_out, [o, p_z], [], [(gnw, None)], [dog], [True, True], name="d_gdn_out",
                                     ts=2048, wb=HEAD, cols=heads, tok_dtype=[F32, MXU_DTYPE],
                                     after=[gw_gdn_proj, gw_sc_out, gw_o])
    own_rows = lambda parts: jnp.concatenate([lax.dynamic_slice_in_dim(g, dev * r, r, axis=0) for g, r in parts], axis=0)
    dqkv, dgbeta = _gdn_bwd(qkv, gbeta, do, s_all, t_all, heads, "d_gdn")
    dp_qkv, g_conv_w = _qkv_bwd(p_qkv, conv_w, dqkv, heads, "d_qkv_conv")
    ffn_red = _sum_direct(own_rows(ffn_parts), ffn_recv, "sum_ffn")
    mix_red = _sum_direct(own_rows(mix_parts), mix_recv, "sum_mix")
    (dp_ab,), _, (g_a_log, g_dt_bias) = _tok_bwd(f_gates, [p_ab], [], [a_log, dt_bias], [dgbeta], [True], name="d_gates",
                                                 ts=512, tok_dtype=MXU_DTYPE, after=[ffn_red, mix_red])
    r_in = rows[5]
    win = -(-(r_in + max(r_in * k % ROW_ALIGN for k in range(NDEV))) // 128) * 128
    need_rows = max(_window_start(r_in, k) for k in range(NDEV)) + win
    dsc_m = dsc.reshape(3, tok, d)
    gwt_in = ([gmm(as_mat(dp_qkv), h1m, name="g_in_qkv"), gmm(as_mat(dz), h1m, name="g_in_z"),
               gmm(as_mat(dp_ab), h1m, name="g_in_ab")[:2 * heads]]
              + [gmm(dsc_m, h1m, a_index=k, name=f"g_in_sc{k}") for k in range(3)]
              + [gmm(as_mat(dga), h1m, name="g_in_ga"), gmm(as_mat(dgb), h1m, name="g_in_gb")])
    gwt_in = jnp.concatenate(gwt_in + [jnp.zeros((need_rows - NDEV * r_in, d), MXU_DTYPE)], axis=0)
    assert d <= 1024
    wide = [as_mat(dp_qkv), as_mat(dz), dsc_m, as_mat(dga)]
    row_of = lambda t: d * t + jnp.where(t * d >= o_ab, 2 * heads, 0)
    recv1 = _exchange_in_chip([(gwt_in, r_in, win, 0)], "scatter_in_chip", 7)
    own = jnp.stack([lax.dynamic_slice_in_dim(gwt_in, _window_start(r_in, 2 * q + ac), win, axis=0) for q in range(4)])
    s1 = _sum_in_chip(own, recv1, "sum_in_chip")
    recv2 = _exchange_chips_async(s1, "scatter_chips", 6)

    dh1 = _mm(as_mat(dp_ab), wt_ab, name="d_in_ab")
    dh1 = _mm_chain(wide, wt_in, row_of, add=dh1, name="d_in", tk=d)
    dh1 = _Product(dgb, wt_in, b_rows=(o_gb, d), add=as_tok(dh1))
    (grad_x,), (dsh1, dsc1), (dn1w,) = _tok_bwd(_f_norm_mod_skip, [x], [sh1, sc1], [n1w], [dh1, dx_skip], [True],
                                                name="d_norm1", ts=256)
    reduced = _sum_chips(s1, recv2, (2 * ax + ay).reshape(1).astype(jnp.int32), "sum_chips")
    gt_w_in = lax.dynamic_slice_in_dim(reduced, r_in * dev - _window_start(r_in, dev), r_in, axis=0)
    g_w_in = gt_w_in.T.reshape(w_in.shape)
    gt_w_ffn_in = ffn_red[:rows[3]]
    g_w_ffn_in = gt_w_ffn_in.T.reshape(w_ffn_in.shape)
    g_w_ffn_out = ffn_red[rows[3]:].reshape(w_ffn_out.shape)
    g_w_gdn_proj, g_w_sc_out, g_w_o = (mix_red[offs[i]:offs[i] + rows[i]].reshape(ref.shape)
                                       for i, ref in enumerate((w_gdn_proj, w_sc_out, w_o)))

    dmod = jnp.concatenate([t.reshape(bl, d) for t in (dsh1, dsc1, dg1, dsh2, dsc2, dg2)], axis=1)
    dmodf = jnp.concatenate([t.reshape(bl, d) for t in (dshf, dscf)], axis=1)
    summed_parts = [dn1w, dn2w, dnfw, g_gnw, g_a_log, g_dt_bias, g_conv_w, g_sc_w, loss_l]
    partial = _all_gather(_pack([dmod, dmodf] + summed_parts, LANE, 8, F32), name="gather_small")
    partial = partial.reshape(NDEV, -1)
    n_rows = bl * (6 * d + 2 * d)
    dmod_all, dmodf_all = _unpack(partial[:, :n_rows], [(bl, 6 * d), (bl, 2 * d)])
    dmod_all, dmodf_all = dmod_all.reshape(NDEV * bl, 6 * d), dmodf_all.reshape(NDEV * bl, 2 * d)
    totals = _row_sum(partial[:, n_rows:], "sum_small")
    t_n1w, t_n2w, t_nfw, t_gnw, t_a_log, t_dt_bias, t_conv_w, t_sc_w, t_loss = [
        t[0] for t in _unpack(totals, [p.shape for p in summed_parts])]
    my_cols = lambda a, n: lax.dynamic_slice_in_dim(a, dev * n, n, axis=1)
    grads = {
        "w_ada": _mm(c_act, my_cols(dmod_all, n_ada), ta=True, name="g_ada").reshape(w_ada.shape),
        "b_ada": _row_sum(dmod_all, "g_ada_bias").reshape(b_ada.shape),
        "norm1_w": t_n1w.reshape(norm1_w.shape),
        "w_in": g_w_in,
        "gdn_conv_w": my_cols(t_conv_w, gdn_conv_w.shape[-1]).reshape(gdn_conv_w.shape),
        "gdn_a_log": t_a_log[:, :heads].reshape(gdn_a_log.shape),
        "gdn_dt_bias": t_dt_bias[:, :heads].reshape(gdn_dt_bias.shape),
        "gdn_norm_w": t_gnw.reshape(gdn_norm_w.shape),
        "w_gdn_proj": g_w_gdn_proj,
        "sc_conv_w": my_cols(t_sc_w, sc_conv_w.shape[-1]).reshape(sc_conv_w.shape),
        "w_sc_out": g_w_sc_out,
        "w_o": g_w_o,
        "norm2_w": t_n2w.reshape(norm2_w.shape),
        "w_ffn_in": g_w_ffn_in,
        "w_ffn_out": g_w_ffn_out,
        "w_ada_f": _mm(c_act, my_cols(dmodf_all, n_adaf), ta=True, name="g_adaf").reshape(w_ada_f.shape),
        "b_ada_f": _row_sum(dmodf_all, "g_adaf_bias").reshape(b_ada_f.shape),
        "normf_w": t_nfw.reshape(normf_w.shape),
    }
    weights = dict(w_ada=w_ada, b_ada=b_ada, norm1_w=norm1_w, w_in=w_in, gdn_conv_w=gdn_conv_w, gdn_a_log=gdn_a_log,
                   gdn_dt_bias=gdn_dt_bias, gdn_norm_w=gdn_norm_w, w_gdn_proj=w_gdn_proj, sc_conv_w=sc_conv_w,
                   w_sc_out=w_sc_out, w_o=w_o, norm2_w=norm2_w, w_ffn_in=w_ffn_in, w_ffn_out=w_ffn_out, w_ada_f=w_ada_f,
                   b_ada_f=b_ada_f, normf_w=normf_w)
    m_in = [m_w_ada, m_b_ada, m_norm1_w, m_w_in, m_gdn_conv_w, m_gdn_a_log, m_gdn_dt_bias, m_gdn_norm_w, m_w_gdn_proj,
            m_sc_conv_w, m_w_sc_out, m_w_o, m_norm2_w, m_w_ffn_in, m_w_ffn_out, m_w_ada_f, m_b_ada_f, m_normf_w]
    v_in = [v_w_ada, v_b_ada, v_norm1_w, v_w_in, v_gdn_conv_w, v_gdn_a_log, v_gdn_dt_bias, v_gdn_norm_w, v_w_gdn_proj,
            v_sc_conv_w, v_w_sc_out, v_w_o, v_norm2_w, v_w_ffn_in, v_w_ffn_out, v_w_ada_f, v_b_ada_f, v_normf_w]
    deltas, new_m, new_v = [], [], []
    grads_t = {"w_in": gt_w_in, "w_ffn_in": gt_w_ffn_in}
    for (wname, wt), mt, vt in zip(weights.items(), m_in, v_in):
        if wname in grads_t:
            back = lambda a, wt=wt: a.T.reshape(wt.shape)
            dl, mn, vn = (back(a) for a in _adamw(wt[0].T, grads_t[wname], mt[0].T, vt[0].T, "adamw_" + wname))
        else:
            dl, mn, vn = _adamw(wt, grads[wname], mt, vt, "adamw_" + wname)
        deltas.append(dl)
        new_m.append(mn)
        new_v.append(vn)
    loss = t_loss[0, 0]
    return (loss, grad_x, *[grads[k] for k in weights], *deltas, *new_m, *new_v)
```

```python
import functools

import jax
import jax.numpy as jnp
from jax import lax
from jax.experimental import pallas as pl
from jax.experimental.pallas import tpu as pltpu
from jax.experimental.pallas import tpu_sc as plsc

F32 = jnp.float32
MXU_DTYPE = jnp.bfloat16
NDEV = 8
CHUNK = 64
HEAD = 128
LANE = 128
EPS = 1e-6
ADAM_LR, ADAM_B1, ADAM_B2, ADAM_EPS, ADAM_WD, ADAM_STEP = 0.001, 0.9, 0.999, 1e-08, 0.01, 10
VMEM_LIMIT = 48 * 1024 * 1024
MESH_IDS = pl.DeviceIdType.MESH
HIGHEST = lax.Precision.HIGHEST


def _tile(n, cands=(512, 256, 128)):
    for c in cands:
        if n % c == 0:
            return c
    return n


def _cparams(*sem):
    return pltpu.CompilerParams(dimension_semantics=sem, vmem_limit_bytes=VMEM_LIMIT)


def _mm(a, b, *, ta=False, tb=False, add=None, out_dtype=F32, name, b_rows=None, out_rows=None, row_off=0, into=None,
        a_index=None):
    m, k = (a.shape[-1], a.shape[-2]) if ta else a.shape[-2:]
    b_shape = b.shape if b_rows is None else (b_rows[1], b.shape[1])
    n = b_shape[0] if tb else b_shape[1]
    assert k == (b_shape[1] if tb else b_shape[0])
    if ta:
        tm, tn = _tile(m), n if n <= 1024 else _tile(n)
        tk = k if k <= 4096 else _tile(k, (4096, 2048, 1024, 512))
        if tm * tk > 1024 * 2048:
            tk = _tile(k, (2048, 1024, 512))
    else:
        tk = k if k <= 1024 else _tile(k, (1024, 512))
        tn = _tile(n, (1024 if tk <= 1024 else 512, 512, 256, 128))
        tm = _tile(m, (2048 if (tn <= 512 and tk <= 1024) else 1024, 1024, 512, 256, 128))
    nk = k // tk
    dims = (((0 if ta else 1,), (1 if tb else 0,)), ((), ()))
    has_add = add is not None

    def body(*refs):
        a_ref, b_ref = refs[0], refs[1]
        add_ref = refs[2] if has_add else None
        o_ref = refs[2 + has_add + (into is not None)]
        part = lax.dot_general(a_ref[...].astype(MXU_DTYPE), b_ref[...].astype(MXU_DTYPE), dims,
                               preferred_element_type=F32)

        def finish(acc):
            if has_add:
                acc = acc + add_ref[...]
            o_ref[...] = acc.astype(o_ref.dtype)

        if nk == 1:
            finish(part)
        else:
            acc_ref = refs[-1]
            kk = pl.program_id(2)

            @pl.when(kk == 0)
            def _():
                acc_ref[...] = part

            @pl.when(kk > 0)
            def _():
                acc_ref[...] += part

            @pl.when(kk == nk - 1)
            def _():
                finish(acc_ref[...])

    a_blk, a_at = ((tk, tm), lambda i, j, kk: (kk, i)) if ta else ((tm, tk), lambda i, j, kk: (i, kk))
    a_spec = (pl.BlockSpec(a_blk, a_at) if a_index is None else
              pl.BlockSpec((None,) + a_blk, lambda i, j, kk: (a_index,) + a_at(i, j, kk)))
    if b_rows is None:
        b_spec = pl.BlockSpec((tn, tk), lambda i, j, kk: (j, kk)) if tb else pl.BlockSpec((tk, tn), lambda i, j, kk: (kk, j))
    else:
        at = lambda t: pl.multiple_of(b_rows[0] + t, ROW_ALIGN)
        b_spec = (pl.BlockSpec((pl.Element(tn), pl.Element(tk)), lambda i, j, kk: (at(j * tn), kk * tk)) if tb else
                  pl.BlockSpec((pl.Element(tk), pl.Element(tn)), lambda i, j, kk: (at(kk * tk), j * tn)))
    add_spec = pl.BlockSpec((tm, tn), lambda i, j, kk: (i, j))
    assert row_off % tm == 0
    o_spec = pl.BlockSpec((tm, tn), lambda i, j, kk: (i + row_off // tm, j))
    in_specs = [a_spec, b_spec] + ([add_spec] if has_add else []) + ([pl.BlockSpec(memory_space=pl.ANY)] if into is not None else [])
    args = [a, b] + ([add] if has_add else []) + ([into] if into is not None else [])
    return pl.pallas_call(
        body, name=name, grid=(m // tm, n // tn, nk), in_specs=in_specs, out_specs=o_spec,
        out_shape=jax.ShapeDtypeStruct((out_rows or m, n), out_dtype),
        scratch_shapes=[pltpu.VMEM((tm, tn), F32)] if nk > 1 else [],
        input_output_aliases={len(args) - 1: 0} if into is not None else {},
        compiler_params=_cparams("parallel", "parallel", "arbitrary"),
    )(*args)


def _mm_chain(parts, b, row_of_tile, *, add, name, tk=1024, tm=1024):
    m, n = parts[0].shape[-2], b.shape[1]
    tm = min(tm, m)
    tiles = [p.shape[0] if p.ndim == 3 else p.shape[1] // tk for p in parts]
    first = [sum(tiles[:s]) for s in range(len(parts))]
    nk = sum(tiles)

    def body(*refs):
        a_refs, b_ref, add_ref, o_ref, acc_ref = refs[:len(parts)], *refs[len(parts):]
        kk = pl.program_id(1)

        @pl.when(kk == 0)
        def _():
            acc_ref[...] = add_ref[...]

        for a_ref, lo, cnt in zip(a_refs, first, tiles):
            @pl.when(jnp.logical_and(kk >= lo, kk < lo + cnt))
            def _(a_ref=a_ref):
                acc_ref[...] += lax.dot_general(a_ref[...].astype(MXU_DTYPE), b_ref[...].astype(MXU_DTYPE),
                                                (((1,), (0,)), ((), ())), preferred_element_type=F32)

        @pl.when(kk == nk - 1)
        def _():
            o_ref[...] = acc_ref[...]

    tile_of = lambda kk, lo, cnt: jnp.clip(kk - lo, 0, cnt - 1)
    a_specs = [pl.BlockSpec((None, tm, tk), functools.partial(lambda i, kk, lo, cnt: (tile_of(kk, lo, cnt), i, 0), lo=lo, cnt=cnt))
               if p.ndim == 3 else
               pl.BlockSpec((tm, tk), functools.partial(lambda i, kk, lo, cnt: (i, tile_of(kk, lo, cnt)), lo=lo, cnt=cnt))
               for p, lo, cnt in zip(parts, first, tiles)]
    b_spec = pl.BlockSpec((pl.Element(tk), pl.Element(n)), lambda i, kk: (pl.multiple_of(row_of_tile(kk), ROW_ALIGN), 0))
    o_spec = pl.BlockSpec((tm, n), lambda i, kk: (i, 0))
    return pl.pallas_call(
        body, name=name, grid=(m // tm, nk), in_specs=a_specs + [b_spec, o_spec], out_specs=o_spec,
        out_shape=jax.ShapeDtypeStruct((m, n), F32), scratch_shapes=[pltpu.VMEM((tm, n), F32)],
        compiler_params=_cparams("parallel", "arbitrary"),
    )(*parts, b, add)


def _swiglu_tiles(m, half):
    tn = _tile(half, (512, 256, 128))
    return _tile(m, (2048 if tn <= 256 else 1024, 1024, 512, 256, 128)), tn


def _ffn_in_swiglu(h, wt, half, name):
    m, k = h.shape
    tm, tn = _swiglu_tiles(m, half)
    nj = half // tn
    dims = (((1,), (1,)), ((), ()))

    def body(h_ref, wa_ref, wb_ref, act_ref, a_ref, b_ref):
        lhs = h_ref[...].astype(MXU_DTYPE)
        a = lax.dot_general(lhs, wa_ref[...].astype(MXU_DTYPE), dims, preferred_element_type=F32)
        b = lax.dot_general(lhs, wb_ref[...].astype(MXU_DTYPE), dims, preferred_element_type=F32)
        act_ref[...] = (_silu(a) * b).astype(act_ref.dtype)
        a_ref[...] = a.astype(a_ref.dtype)
        b_ref[...] = b.astype(b_ref.dtype)

    out = jax.ShapeDtypeStruct((m, half), MXU_DTYPE)
    oblk = pl.BlockSpec((tm, tn), lambda i, j: (i, j))
    return pl.pallas_call(
        body, name=name, grid=(m // tm, nj),
        in_specs=[pl.BlockSpec((tm, k), lambda i, j: (i, 0)), pl.BlockSpec((tn, k), lambda i, j: (j, 0)),
                  pl.BlockSpec((tn, k), lambda i, j: (j + nj, 0))],
        out_specs=[oblk, oblk, oblk], out_shape=[out, out, out], compiler_params=_cparams("parallel", "parallel"),
    )(h, wt, wt)


def _ffn_out_bwd_swiglu(dff, w, a, b, name):
    m, k = dff.shape
    half = w.shape[0]
    tm, tn = _swiglu_tiles(m, half)

    def body(d_ref, w_ref, a_ref, b_ref, da_ref, db_ref):
        dact = lax.dot_general(d_ref[...].astype(MXU_DTYPE), w_ref[...].astype(MXU_DTYPE), (((1,), (1,)), ((), ())),
                               preferred_element_type=F32)
        av, bv = a_ref[...].astype(F32), b_ref[...].astype(F32)
        sig = jax.nn.sigmoid(av)
        da_ref[...] = (dact * bv * (sig * (1.0 + av * (1.0 - sig)))).astype(da_ref.dtype)
        db_ref[...] = (dact * (av * sig)).astype(db_ref.dtype)

    out = jax.ShapeDtypeStruct((m, half), MXU_DTYPE)
    oblk = pl.BlockSpec((tm, tn), lambda i, j: (i, j))
    return pl.pallas_call(
        body, name=name, grid=(m // tm, half // tn),
        in_specs=[pl.BlockSpec((tm, k), lambda i, j: (i, 0)), pl.BlockSpec((tn, k), lambda i, j: (j, 0)), oblk, oblk],
        out_specs=[oblk, oblk], out_shape=[out, out], compiler_params=_cparams("parallel", "parallel"),
    )(dff, w, a, b)


def _with_off(xs):
    return [x if isinstance(x, tuple) else (x, 0) for x in xs]


def _spec(kind, arr, off, ts, wb):
    w = arr.shape[-1] if wb is None else wb
    col = (lambda j: 0) if wb is None else functools.partial(lambda j, o: o + j, o=off)
    if kind == "tok":
        return pl.BlockSpec((None, ts, w), lambda j, b, i: (b, i, col(j)))
    if kind == "bat":
        return pl.BlockSpec((None, 1, w), lambda j, b, i: (b, 0, col(j)))
    if off is None:
        return pl.BlockSpec(arr.shape, lambda j, b, i: (0, 0))
    return pl.BlockSpec((arr.shape[0], w), lambda j, b, i: (0, col(j)))


class _Product:
    def __init__(self, a, b, *, tb=False, b_rows=None, add=None):
        self.a, self.b, self.tb, self.b_rows, self.add = a, b, tb, b_rows, add
        rows = b.shape[0] if b_rows is None else b_rows[1]
        self.shape = a.shape[:2] + (rows if tb else b.shape[1],)

    def inputs(self, ts):
        a_spec = pl.BlockSpec((None, ts, self.a.shape[2]), lambda j, b, i: (b, i, 0))
        if self.b_rows is None:
            b_spec = pl.BlockSpec(self.b.shape, lambda j, b, i: (0, 0))
        else:
            start, count = self.b_rows
            b_spec = pl.BlockSpec((pl.Element(count), pl.Element(self.b.shape[1])), lambda j, b, i: (start, 0))
        if isinstance(self.add, _Product):
            extra = self.add.inputs(ts)
        else:
            extra = [] if self.add is None else [(self.add, pl.BlockSpec((None, ts, self.shape[2]), lambda j, b, i: (b, i, 0)))]
        return [(self.a, a_spec), (self.b, b_spec)] + extra

    def value(self, refs):
        dims = (((1,), (1 if self.tb else 0,)), ((), ()))
        val = lax.dot_general(refs[0][...].astype(MXU_DTYPE), refs[1][...].astype(MXU_DTYPE), dims, preferred_element_type=F32)
        if isinstance(self.add, _Product):
            return val + self.add.value(refs[2:])
        return val if self.add is None else val + refs[2][...].astype(F32)


def _inputs(groups, kinds, ts, wb):
    loaded = [(a, _spec(kind, a, o, ts, wb)) for g, kind in zip(groups, kinds) for a, o in g if not isinstance(a, _Product)]
    made = [pair for g in groups for a, _ in g if isinstance(a, _Product) for pair in a.inputs(ts)]
    return [a for a, _ in loaded + made], [sp for _, sp in loaded + made]


def _values(refs, groups):
    n_loaded = sum(1 for g in groups for a, _ in g if not isinstance(a, _Product))
    loaded, pos, out = iter(refs[:n_loaded]), n_loaded, []
    for g in groups:
        vals = []
        for a, _ in g:
            if isinstance(a, _Product):
                k = len(a.inputs(1))
                vals.append(a.value(refs[pos:pos + k]))
                pos += k
            else:
                vals.append(next(loaded)[...].astype(F32))
        out.append(vals)
    return out, pos


def _tok_fwd(fn, toks, bats, pars, outs, *, name, ts, wb=None, cols=1):
    groups = [_with_off(toks), _with_off(bats), _with_off(pars)]
    bl, s, _ = groups[0][0][0].shape
    ts = min(ts, s)
    args, in_specs = _inputs(groups, ("tok", "bat", "par"), ts, wb)

    def body(*refs):
        vals, n_in = _values(refs, groups)
        res = fn(*[v for g in vals for v in g])
        for r, val in zip(refs[n_in:], res):
            r[...] = val.astype(r.dtype)

    out_specs = [pl.BlockSpec((None, ts, w if wb is None else wb), lambda j, b, i: (b, i, j)) for w, _ in outs]
    return pl.pallas_call(
        body, name=name, grid=(cols, bl, s // ts), in_specs=in_specs,
        out_specs=out_specs, out_shape=[jax.ShapeDtypeStruct((bl, s, w), dt) for w, dt in outs],
        compiler_params=_cparams("parallel", "parallel", "parallel"),
    )(*args)


def _accumulate(ref, val, first):
    @pl.when(first)
    def _():
        ref[...] = val

    @pl.when(jnp.logical_not(first))
    def _():
        ref[...] += val


def _tok_bwd(fn, toks, bats, pars, cots, need, *, name, ts, wb=None, cols=1, tok_dtype=F32, loss=False, after=()):
    toks, bats, pars, cots = _with_off(toks), _with_off(bats), _with_off(pars), _with_off(cots)
    groups = [toks, bats, pars, cots]
    bl, s, _ = toks[0][0].shape
    ts = min(ts, s)
    nt, nb, npar = len(toks), len(bats), len(pars)
    args, in_specs = _inputs(groups, ("tok", "bat", "par", "tok"), ts, wb)
    args, in_specs = args + list(after), in_specs + [pl.BlockSpec(memory_space=pl.ANY)] * len(after)

    def body(*refs):
        j, b, i = pl.program_id(0), pl.program_id(1), pl.program_id(2)
        (tok_vals, bat_vals, par_vals, cot_vals), o = _values(refs, groups)
        o += len(after)
        outs, vjp = jax.vjp(fn, *tok_vals, *bat_vals, *par_vals)
        if loss:
            ct = (jnp.ones_like(outs[0]),)
            tot = jnp.broadcast_to(jnp.sum(outs[0], keepdims=True), (1, LANE))
            _accumulate(refs[o], tot, jnp.logical_and(b == 0, i == 0))
            o += 1
        else:
            ct = tuple(cot_vals)
        grads = vjp(ct)
        for t in range(nt):
            if need[t]:
                refs[o][...] = grads[t].astype(refs[o].dtype)
                o += 1
        for t in range(nb):
            _accumulate(refs[o], grads[nt + t], i == 0)
            o += 1
        for t in range(npar):
            first = jnp.logical_and(b == 0, i == 0)
            if pars[t][1] is None:
                first = jnp.logical_and(first, j == 0)
            _accumulate(refs[o], grads[nt + nb + t], first)
            o += 1

    full = lambda arr: arr.shape[-1] if wb is None else wb * cols
    blk = lambda arr: arr.shape[-1] if wb is None else wb
    out_specs, out_shape = [], []
    if loss:
        out_specs.append(pl.BlockSpec((1, LANE), lambda j, b, i: (0, 0)))
        out_shape.append(jax.ShapeDtypeStruct((1, LANE), F32))
    for t in range(nt):
        if need[t]:
            out_specs.append(pl.BlockSpec((None, ts, blk(toks[t][0])), lambda j, b, i: (b, i, j)))
            dt = tok_dtype[t] if isinstance(tok_dtype, (list, tuple)) else tok_dtype
            out_shape.append(jax.ShapeDtypeStruct((bl, s, full(toks[t][0])), dt))
    for arr, _ in bats:
        out_specs.append(pl.BlockSpec((None, 1, blk(arr)), lambda j, b, i: (b, 0, j)))
        out_shape.append(jax.ShapeDtypeStruct((bl, 1, full(arr)), F32))
    for arr, off in pars:
        if off is None:
            out_specs.append(pl.BlockSpec(arr.shape, lambda j, b, i: (0, 0)))
            out_shape.append(jax.ShapeDtypeStruct(arr.shape, F32))
        else:
            out_specs.append(pl.BlockSpec((arr.shape[0], blk(arr)), lambda j, b, i: (0, j)))
            out_shape.append(jax.ShapeDtypeStruct((arr.shape[0], full(arr)), F32))
    res = list(pl.pallas_call(
        body, name=name, grid=(cols, bl, s // ts), in_specs=in_specs,
        out_specs=out_specs, out_shape=out_shape, compiler_params=_cparams("arbitrary", "arbitrary", "arbitrary"),
    )(*args))
    tot = res.pop(0) if loss else None
    dtoks = [res.pop(0) if need[t] else None for t in range(nt)]
    dbats = [res.pop(0) for _ in range(nb)]
    dpars = [res.pop(0) for _ in range(npar)]
    return (tot, dtoks, dbats, dpars) if loss else (dtoks, dbats, dpars)


def _silu(x):
    return x * jax.nn.sigmoid(x)


def _rms(x, w):
    return x * lax.rsqrt(jnp.mean(x * x, axis=-1, keepdims=True) + EPS) * w


def _f_norm_mod(x, shift, scale, w):
    return (_rms(x, w) * (1.0 + scale) + shift,)


def _f_norm_mod_skip(x, shift, scale, w):
    return _rms(x, w) * (1.0 + scale) + shift, x


def _f_res_norm_mod(x, mix, gate, shift, scale, w):
    x2 = x + gate * mix
    return x2, _rms(x2, w) * (1.0 + scale) + shift


def _f_res_norm_mod_keep(x, mix, gate, shift, scale, w):
    return (*_f_res_norm_mod(x, mix, gate, shift, scale, w), mix)


def _f_gates(p, a_log, dt_bias, *, heads):
    z = p + dt_bias
    g = -jnp.exp(a_log) * (jnp.maximum(z, 0.0) + jnp.log1p(jnp.exp(jnp.minimum(z, -z))))
    lane = lax.broadcasted_iota(jnp.int32, p.shape, 1)
    return (jnp.where(lane < heads, g, jax.nn.sigmoid(p)),)


def _f_gdn_out(o, z, w):
    return (_rms(o, w) * _silu(z),)


def _f_merge(ga, gb, ya, yb):
    return (jax.nn.sigmoid(ga) * ya + jax.nn.sigmoid(gb) * yb,)


def _f_merge_keep(ga, gb, ya, yb):
    return (*_f_merge(ga, gb, ya, yb), yb)


def _f_loss(x2, ff, tgt, gate, shift, scale, w):
    y = _rms(x2 + gate * ff, w) * (1.0 + scale) + shift
    return (0.5 * jnp.mean(jnp.square(y - tgt), axis=-1, keepdims=True),)


def _shift_down(x, s):
    if s == 0:
        return x
    row = lax.broadcasted_iota(jnp.int32, x.shape, 0)
    return jnp.where(row >= s, pltpu.roll(x, s, 0), 0.0)


def _shift_up(x, s):
    if s == 0:
        return x
    n = x.shape[0]
    row = lax.broadcasted_iota(jnp.int32, x.shape, 0)
    return jnp.where(row < n - s, pltpu.roll(x, n - s, 0), 0.0)


def _conv(x, w):
    width = w.shape[0]
    acc = w[width - 1:width, :] * x
    for j in range(width - 1):
        acc = acc + w[j:j + 1, :] * _shift_down(x, width - 1 - j)
    return acc


def _conv_bwd(dy, x, w, dw_ref, first):
    width = w.shape[0]
    dx = w[width - 1:width, :] * dy
    for j in range(width - 1):
        dx = dx + w[j:j + 1, :] * _shift_up(dy, width - 1 - j)
    for j in range(width):
        row = jnp.sum(dy * _shift_down(x, width - 1 - j), axis=0, keepdims=True)
        _accumulate(dw_ref.at[j:j + 1, :], row, first)
    return dx


def _qkv_act(xc, is_v, scale):
    a = _silu(xc)
    nrm = a * lax.rsqrt(jnp.sum(a * a, axis=-1, keepdims=True) + EPS) * scale
    return jnp.where(is_v, a, nrm)


def _qkv_act_bwd(xc, dout, is_v, scale):
    sig = jax.nn.sigmoid(xc)
    a = xc * sig
    r = lax.rsqrt(jnp.sum(a * a, axis=-1, keepdims=True) + EPS)
    c1 = r * scale
    da = c1 * dout - a * (c1 * r * r * jnp.sum(dout * a, axis=-1, keepdims=True))
    return jnp.where(is_v, dout, da) * (sig * (1.0 + xc * (1.0 - sig)))


def _qkv_consts(j, heads):
    is_v = j >= 2 * heads
    scale = jnp.where(j < heads, HEAD ** -0.5, 1.0).astype(F32)
    return is_v, scale


def _qkv_fwd(p, w, heads, name):
    bl, s, w3 = p.shape

    def body(p_ref, w_ref, o_ref):
        is_v, scale = _qkv_consts(pl.program_id(0), heads)
        o_ref[...] = _qkv_act(_conv(p_ref[...], w_ref[...]), is_v, scale)

    blk = pl.BlockSpec((None, s, HEAD), lambda j, b: (b, 0, j))
    return pl.pallas_call(
        body, name=name, grid=(w3 // HEAD, bl), in_specs=[blk, pl.BlockSpec((w.shape[0], HEAD), lambda j, b: (0, j))],
        out_specs=blk, out_shape=jax.ShapeDtypeStruct(p.shape, F32), compiler_params=_cparams("parallel", "parallel"),
    )(p, w)


def _qkv_bwd(p, w, dout, heads, name):
    bl, s, w3 = p.shape

    def body(p_ref, w_ref, d_ref, dp_ref, dw_ref):
        is_v, scale = _qkv_consts(pl.program_id(0), heads)
        x, wv = p_ref[...], w_ref[...]
        dxc = _qkv_act_bwd(_conv(x, wv), d_ref[...], is_v, scale)
        dp_ref[...] = _conv_bwd(dxc, x, wv, dw_ref, pl.program_id(1) == 0).astype(dp_ref.dtype)

    blk = pl.BlockSpec((None, s, HEAD), lambda j, b: (b, 0, j))
    wblk = pl.BlockSpec((w.shape[0], HEAD), lambda j, b: (0, j))
    return pl.pallas_call(
        body, name=name, grid=(w3 // HEAD, bl), in_specs=[blk, wblk, blk], out_specs=[blk, wblk],
        out_shape=[jax.ShapeDtypeStruct(p.shape, MXU_DTYPE), jax.ShapeDtypeStruct(w.shape, F32)],
        compiler_params=_cparams("arbitrary", "arbitrary"),
    )(p, w, dout)


def _sc_specs(p, w):
    bl, s, w3 = p.shape
    nblk = w3 // 3 // LANE
    sec = lambda k: pl.BlockSpec((None, s, LANE), functools.partial(lambda j, b, k: (b, 0, k * nblk + j), k=k))
    return nblk, [sec(0), sec(1), sec(2)], pl.BlockSpec((w.shape[0], LANE), lambda j, b: (0, j)), \
        pl.BlockSpec((None, s, LANE), lambda j, b: (b, 0, j))


def _sc_fwd(p, w, name):
    bl, s, w3 = p.shape
    nblk, secs, wblk, oblk = _sc_specs(p, w)

    def body(b_ref, c_ref, x_ref, w_ref, o_ref):
        o_ref[...] = (b_ref[...] * _conv(c_ref[...] * x_ref[...], w_ref[...])).astype(o_ref.dtype)

    return pl.pallas_call(
        body, name=name, grid=(nblk, bl), in_specs=secs + [wblk], out_specs=oblk,
        out_shape=jax.ShapeDtypeStruct((bl, s, w3 // 3), MXU_DTYPE), compiler_params=_cparams("parallel", "parallel"),
    )(p, p, p, w)


def _sc_bwd(p, w, dout, name):
    bl, s, w3 = p.shape
    nblk, secs, wblk, oblk = _sc_specs(p, w)

    def body(b_ref, c_ref, x_ref, w_ref, d_ref, dp_ref, dw_ref):
        gb, gc, xin, wv, d = b_ref[...], c_ref[...], x_ref[...], w_ref[...], d_ref[...]
        u = gc * xin
        dp_ref[0] = (d * _conv(u, wv)).astype(dp_ref.dtype)
        du = _conv_bwd(d * gb, u, wv, dw_ref, pl.program_id(1) == 0)
        dp_ref[1] = (du * xin).astype(dp_ref.dtype)
        dp_ref[2] = (du * gc).astype(dp_ref.dtype)

    return pl.pallas_call(
        body, name=name, grid=(nblk, bl), in_specs=secs + [wblk, oblk],
        out_specs=[pl.BlockSpec((3, None, s, LANE), lambda j, b: (0, b, 0, j)), wblk],
        out_shape=[jax.ShapeDtypeStruct((3, bl, s, w3 // 3), MXU_DTYPE), jax.ShapeDtypeStruct(w.shape, F32)],
        compiler_params=_cparams("arbitrary", "arbitrary"),
    )(p, p, p, w, dout)


def _bdot(a, b, ca, cb):
    return lax.dot_general(a.astype(MXU_DTYPE), b.astype(MXU_DTYPE), (((ca,), (cb,)), ((), ())),
                           preferred_element_type=F32)


def _hdot(a, b):
    return lax.dot_general(a, b, (((1,), (0,)), ((), ())), precision=HIGHEST, preferred_element_type=F32)


def _lane_col(x, idx):
    lane = lax.broadcasted_iota(jnp.int32, x.shape, 1)
    return jnp.sum(jnp.where(lane == idx, x, 0.0), axis=1, keepdims=True)


def _chunk_masks():
    r = lax.broadcasted_iota(jnp.int32, (CHUNK, CHUNK), 0)
    c = lax.broadcasted_iota(jnp.int32, (CHUNK, CHUNK), 1)
    return r == c, r >= c, r > c


def _dot3(a, b):
    ah, bh = a.astype(MXU_DTYPE), b.astype(MXU_DTYPE)
    al, bl = (a - ah.astype(F32)).astype(MXU_DTYPE), (b - bh.astype(F32)).astype(MXU_DTYPE)
    dot = lambda x, y: lax.dot_general(x, y, (((1,), (0,)), ((), ())), preferred_element_type=F32)
    return dot(ah, bh) + (dot(ah, bl) + dot(al, bh))


def _tri_inv_steps(low, eye):
    x = -low
    p = jnp.where(eye, 1.0, 0.0) + x
    span = 2
    while span < CHUNK:
        x = _dot3(x, x)
        yield
        p = p + _dot3(p, x)
        yield
        span *= 2
    return p


def _round_robin(gens):
    out, live = [None] * len(gens), list(range(len(gens)))
    while live:
        still = []
        for i in live:
            try:
                next(gens[i])
                still.append(i)
            except StopIteration as stop:
                out[i] = stop.value
        live = still
    return out


def _gdn_pre(q, k, v, gc, beta, masks):
    eye, causal, strict = masks
    gc_row = jnp.sum(jnp.where(eye, gc, 0.0), axis=0, keepdims=True)
    decay = jnp.where(causal, jnp.exp(jnp.where(causal, gc - gc_row, 0.0)), 0.0)
    eg = jnp.exp(gc)
    gl = gc[CHUNK - 1:CHUNK, :]
    kb, vb = k * beta, v * beta
    both = _bdot(jnp.concatenate([kb, q], axis=0), k, 1, 1)
    low = jnp.where(strict, both[:CHUNK] * decay, 0.0)
    qk = jnp.where(causal, both[CHUNK:] * decay, 0.0)
    rest = jnp.exp(gl - gc)
    return dict(decay=decay, eg=eg, gl=gl, kb=kb, vb=vb, kbe=kb * eg, low=low, qk=qk, qg=q * eg, rest=rest, kdec=k * rest)


GROUP = 4


def _gdn_specs(qkv, gbeta, heads, rev):
    bl, s, w3 = qkv.shape
    d, n = w3 // 3, s // CHUNK
    group = GROUP if n % GROUP == 0 else 1
    steps = n // group
    at = (lambda c: steps - 1 - c) if rev else (lambda c: c)
    assert d == heads * HEAD
    rows = group * CHUNK
    sec = pl.BlockSpec((None, rows, w3), lambda b, c: (b, at(c), 0))
    gspec = pl.BlockSpec((None, rows, LANE), lambda b, c: (b, at(c), 0))
    ospec = pl.BlockSpec((None, rows, d), lambda b, c: (b, at(c), 0))
    sspec = pl.BlockSpec((None, group, heads, HEAD, HEAD), lambda b, c: (b, at(c), 0, 0, 0))
    tspec = pl.BlockSpec((None, group, heads, CHUNK, CHUNK), lambda b, c: (b, at(c), 0, 0, 0))
    return bl, s, d, n, group, sec, gspec, ospec, sspec, tspec


def _gdn_fwd(qkv, gbeta, heads, name):
    bl, s, d, n, group, sec, gspec, ospec, sspec, tspec = _gdn_specs(qkv, gbeta, heads, False)
    rows = lambda sub: slice(sub * CHUNK, (sub + 1) * CHUNK)
    pairs = [(h, sub) for h in range(heads) for sub in range(group)]

    def body(x_ref, g_ref, o_ref, s_ref, t_ref, st_ref):
        @pl.when(pl.program_id(1) == 0)
        def _():
            st_ref[...] = jnp.zeros_like(st_ref)

        masks = _chunk_masks()
        eye, causal, _ = masks
        gblks = [g_ref[rows(sub), :] for sub in range(group)]
        gcs = [_hdot(jnp.where(causal, 1.0, 0.0), gb) for gb in gblks]
        st_all = st_ref[...]

        def free(h, sub):
            q, k, v = (x_ref[rows(sub), sec * d + h * HEAD:sec * d + (h + 1) * HEAD] for sec in range(3))
            pre = _gdn_pre(q, k, v, _lane_col(gcs[sub], h), _lane_col(gblks[sub], heads + h), masks)
            yield
            t = yield from _tri_inv_steps(pre["low"], eye)
            uw = _bdot(t, jnp.concatenate([pre["vb"], pre["kbe"]], axis=1), 1, 0)
            return pre, t, uw[:, :HEAD], uw[:, HEAD:]

        pieces = dict(zip(pairs, _round_robin([free(h, sub) for h, sub in pairs])))

        def carry(h):
            st, outs, starts = st_all[h], [], []
            for sub in range(group):
                pre, _, u, w = pieces[h, sub]
                starts.append(st)
                vnew = u - _bdot(w, st, 1, 0)
                yield
                outs.append(_bdot(pre["qg"], st, 1, 0) + _bdot(pre["qk"], vnew, 1, 0))
                st = st * jnp.exp(pre["gl"]) + _bdot(pre["kdec"], vnew, 0, 0)
                yield
            return outs, starts, st

        carried = _round_robin([carry(h) for h in range(heads)])
        per_sub = lambda pick: [[pick(h, sub) for h in range(heads)] for sub in range(group)]
        o_ref[...] = jnp.concatenate([jnp.concatenate(r, axis=1) for r in per_sub(lambda h, sub: carried[h][0][sub])], axis=0)
        s_ref[...] = jnp.stack([jnp.stack(r) for r in per_sub(lambda h, sub: carried[h][1][sub])])
        t_ref[...] = jnp.stack([jnp.stack(r) for r in per_sub(lambda h, sub: pieces[h, sub][1])])
        st_ref[...] = jnp.stack([carried[h][2] for h in range(heads)])

    return pl.pallas_call(
        body, name=name, grid=(bl, n // group), in_specs=[sec, gspec], out_specs=[ospec, sspec, tspec],
        out_shape=[jax.ShapeDtypeStruct((bl, s, d), F32), jax.ShapeDtypeStruct((bl, n, heads, HEAD, HEAD), F32),
                   jax.ShapeDtypeStruct((bl, n, heads, CHUNK, CHUNK), F32)],
        scratch_shapes=[pltpu.VMEM((heads, HEAD, HEAD), F32)], compiler_params=_cparams("parallel", "arbitrary"),
    )(qkv, gbeta)


def _gdn_bwd(qkv, gbeta, dout, s_all, t_all, heads, name):
    bl, s, d, n, group, sec, gspec, ospec, sspec, tspec = _gdn_specs(qkv, gbeta, heads, True)
    rows = lambda sub: slice(sub * CHUNK, (sub + 1) * CHUNK)
    pairs = [(h, sub) for h in range(heads) for sub in range(group)]
    stack, side = functools.partial(jnp.concatenate, axis=0), functools.partial(jnp.concatenate, axis=1)

    def body(x_ref, g_ref, do_ref, s_ref, t_ref, dx_ref, dg_ref, ds_ref):
        @pl.when(pl.program_id(1) == 0)
        def _():
            ds_ref[...] = jnp.zeros_like(ds_ref)

        masks = _chunk_masks()
        eye, causal, strict = masks
        gblks = [g_ref[rows(sub), :] for sub in range(group)]
        gcs = [_hdot(jnp.where(causal, 1.0, 0.0), gb) for gb in gblks]
        lane = lax.broadcasted_iota(jnp.int32, (CHUNK, LANE), 1)
        last_row = lax.broadcasted_iota(jnp.int32, (CHUNK, 1), 0) == CHUNK - 1
        rowsum = lambda a: jnp.sum(a, axis=1, keepdims=True)
        st_all, t_all_, ds_all = s_ref[...], t_ref[...], ds_ref[...]

        def free(h, sub):
            q, k, v = (x_ref[rows(sub), sec * d + h * HEAD:sec * d + (h + 1) * HEAD] for sec in range(3))
            do = do_ref[rows(sub), h * HEAD:(h + 1) * HEAD]
            beta = _lane_col(gblks[sub], heads + h)
            st, t = st_all[sub, h], t_all_[sub, h]
            pre = _gdn_pre(q, k, v, _lane_col(gcs[sub], h), beta, masks)
            yield
            uw = _bdot(t, side([pre["vb"], pre["kbe"]]), 1, 0)
            u, w = uw[:, :HEAD], uw[:, HEAD:]
            yield
            vnew = u - _bdot(w, st, 1, 0)
            yield
            dqk = jnp.where(causal, _bdot(do, vnew, 1, 1), 0.0)
            dqg = _bdot(do, st, 1, 1)
            return dict(q=q, k=k, v=v, do=do, beta=beta, st=st, t=t, pre=pre, w=w, vnew=vnew, dqk=dqk, dqg=dqg)

        pieces = dict(zip(pairs, _round_robin([free(h, sub) for h, sub in pairs])))

        def carry(h):
            dsn, outs = ds_all[h], {}
            for sub in reversed(range(group)):
                pc = pieces[h, sub]
                pre, st, do = pc["pre"], pc["st"], pc["do"]
                egl = jnp.exp(pre["gl"])
                dkdec = _bdot(pc["vnew"], dsn, 1, 1)
                dvnew = _bdot(pre["kdec"], dsn, 1, 0) + _bdot(pre["qk"], do, 0, 0)
                dgl = jnp.sum(dsn * st, keepdims=True) * egl
                yield
                dw = -_bdot(dvnew, st, 1, 1)
                dsn = dsn * egl + _bdot(stack([pre["qg"], -pc["w"]]), stack([do, dvnew]), 0, 0)
                outs[sub] = (dkdec, dvnew, dgl, dw)
                yield
            return outs, dsn

        carried = _round_robin([carry(h) for h in range(heads)])

        def rest(h, sub):
            pc = pieces[h, sub]
            dkdec, dvnew, dgl, dw = carried[h][0][sub]
            q, k, v, beta, t, pre, dqk, dqg = (pc[x] for x in ("q", "k", "v", "beta", "t", "pre", "dqk", "dqg"))
            decay, eg, kb, vb, kbe, low, qk, qg, kdec = (pre[x] for x in ("decay", "eg", "kb", "vb", "kbe", "low", "qk", "qg", "kdec"))
            dt = _bdot(side([dvnew, dw]), side([vb, kbe]), 1, 1)
            by_t = _bdot(t, side([dvnew, dw]), 0, 0)
            dvb, dkbe = by_t[:, :HEAD], by_t[:, HEAD:]
            yield
            inner = _bdot(dt, t, 1, 1)
            yield
            dlow = -jnp.where(strict, _bdot(t, inner, 0, 0), 0.0)
            da, db = dlow * decay, dqk * decay
            yield
            m = dlow * low + dqk * qk
            kdk = dkdec * kdec
            col_of_m = jnp.sum(jnp.where(eye, jnp.sum(m, axis=0, keepdims=True), 0.0), axis=1, keepdims=True)
            dgc = rowsum(m) - col_of_m + rowsum(dqg * qg) + rowsum(dkbe * kbe) - rowsum(kdk)
            dgc = dgc + jnp.where(last_row, dgl + jnp.sum(kdk, keepdims=True), 0.0)
            by_k = _bdot(stack([da, db]), k, 1, 0)
            dkb = by_k[:CHUNK] + dkbe * eg
            yield
            dk = _bdot(stack([da, db]), stack([kb, q]), 0, 0) + dkdec * pre["rest"] + dkb * beta
            dq = by_k[CHUNK:] + dqg * eg
            dbeta = rowsum(dkb * k) + rowsum(dvb * v)
            return dq, dk, dvb * beta, jnp.where(lane == h, dgc, 0.0) + jnp.where(lane == heads + h, dbeta, 0.0)

        done = dict(zip(pairs, _round_robin([rest(h, sub) for h, sub in pairs])))
        dx_ref[...] = stack([side([done[h, sub][i] for i in range(3) for h in range(heads)]) for sub in range(group)])
        ds_ref[...] = jnp.stack([carried[h][1] for h in range(heads)])
        upper = jnp.where(jnp.logical_or(eye, jnp.logical_not(causal)), 1.0, 0.0)
        dgs = []
        for sub in range(group):
            dgb = done[0, sub][3]
            for h in range(1, heads):
                dgb = dgb + done[h, sub][3]
            dgs.append(jnp.where(lane < heads, _hdot(upper, dgb), dgb))
        dg_ref[...] = stack(dgs)

    return pl.pallas_call(
        body, name=name, grid=(bl, n // group), in_specs=[sec, gspec, ospec, sspec, tspec], out_specs=[sec, gspec],
        out_shape=[jax.ShapeDtypeStruct(qkv.shape, F32), jax.ShapeDtypeStruct((bl, s, LANE), F32)],
        scratch_shapes=[pltpu.VMEM((heads, HEAD, HEAD), F32)], compiler_params=_cparams("parallel", "arbitrary"),
    )(qkv, gbeta, dout, s_all, t_all)


def _position():
    return lax.axis_index("x"), lax.axis_index("y"), lax.axis_index("c")


def _all_gather(x, *, name):
    space = pltpu.VMEM

    def body(x_ref, out_ref, send_sems, recv_sems, local_sem):
        ax, ay, ac = _position()
        me, sibling = (ax, ay, ac), (ax, ay, 1 - ac)
        chips = [(1 - ax, ay), (ax, 1 - ay), (1 - ax, 1 - ay)]

        def slot(px, py, pc):
            return out_ref.at[4 * px + 2 * py + pc]

        def copy(k, block, to, src=None):
            return pltpu.make_async_remote_copy(
                src_ref=slot(*block) if src is None else src, dst_ref=slot(*block), send_sem=send_sems.at[k],
                recv_sem=recv_sems.at[k], device_id=to, device_id_type=MESH_IDS)

        mine = pltpu.make_async_copy(x_ref, slot(*me), local_sem)
        mine.start()
        first = [copy(0, me, sibling, src=x_ref)] + [copy(1 + j, me, (*chip, ac), src=x_ref) for j, chip in enumerate(chips)]
        for cp in first:
            cp.start()
        passed = [copy(4 + j, (*chip, ac), sibling) for j, chip in enumerate(chips)]
        for j, chip in enumerate(chips):
            copy(1 + j, (*chip, ac), me).wait_recv()
            passed[j].start()
        copy(0, sibling, me).wait_recv()
        for j, chip in enumerate(chips):
            copy(4 + j, (*chip, 1 - ac), me).wait_recv()
        for cp in first + passed:
            cp.wait_send()
        mine.wait()

    return pl.pallas_call(
        body, name=name, out_shape=jax.ShapeDtypeStruct((NDEV,) + x.shape, x.dtype),
        in_specs=[pl.BlockSpec(memory_space=space)], out_specs=pl.BlockSpec(memory_space=space),
        scratch_shapes=[pltpu.SemaphoreType.DMA((7,)), pltpu.SemaphoreType.DMA((7,)), pltpu.SemaphoreType.DMA],
    )(x)


class _Rider:
    def __init__(self, arrays, out_shapes, sems, hooks):
        self.arrays, self.out_shapes, self.sems, self.hooks = arrays, out_shapes, sems, hooks


def _gather_rider(xs):
    n = len(xs)

    def hooks(x_refs, out_refs, send_sems, recv_sems):
        ax, ay, ac = _position()
        me, sibling = (ax, ay, ac), (ax, ay, 1 - ac)
        chips = [(1 - ax, ay), (ax, 1 - ay), (1 - ax, 1 - ay)]

        def copies(k, block, to, own=False):
            out = []
            for i in range(n):
                slot = out_refs[i].at[4 * block[0] + 2 * block[1] + block[2]]
                out.append(pltpu.make_async_remote_copy(
                    src_ref=x_refs[i] if own else slot, dst_ref=slot, send_sem=send_sems.at[k, i], recv_sem=recv_sems.at[k, i],
                    device_id=to, device_id_type=MESH_IDS))
            return out

        def first():
            for cp in copies(0, me, sibling, own=True):
                cp.start()
            for j, chip in enumerate(chips):
                for cp in copies(1 + j, me, (*chip, ac), own=True):
                    cp.start()

        def mid():
            for j, chip in enumerate(chips):
                for arrived, onward in zip(copies(1 + j, (*chip, ac), me), copies(4 + j, (*chip, ac), sibling)):
                    arrived.wait_recv()
                    onward.start()

        def last():
            for cp in copies(0, sibling, me):
                cp.wait_recv()
            for j, chip in enumerate(chips):
                for cp in copies(4 + j, (*chip, 1 - ac), me):
                    cp.wait_recv()
            for cp in copies(0, me, sibling, own=True):
                cp.wait_send()
            for j, chip in enumerate(chips):
                for cp in copies(1 + j, me, (*chip, ac), own=True) + copies(4 + j, (*chip, ac), sibling):
                    cp.wait_send()

        return first, mid, last

    return _Rider(list(xs), [jax.ShapeDtypeStruct((NDEV,) + x.shape, x.dtype) for x in xs],
                  [pltpu.SemaphoreType.DMA((7, n)), pltpu.SemaphoreType.DMA((7, n))], hooks)


def _scatter_rider(parts):
    packed = sum(r for _, r in parts)
    width, dtype = parts[0][0].shape[1], parts[0][0].dtype

    def hooks(g_refs, out_refs, send_sems, recv_sems):
        (recv_ref,) = out_refs
        ax, ay, ac = _position()

        def peer(rel):
            flip = lambda a, bit: 1 - a if rel & bit else a
            return flip(ax, 4), flip(ay, 2), flip(ac, 1)

        def first():
            for rel in range(1, NDEV):
                px, py, pc = peer(rel)
                off = 0
                for g_ref, (_, r) in zip(g_refs, parts):
                    rows = g_ref.at[pl.ds(pl.multiple_of((4 * px + 2 * py + pc) * r, ROW_ALIGN), r)]
                    pltpu.make_async_remote_copy(
                        src_ref=rows, dst_ref=recv_ref.at[rel - 1, pl.ds(off, r)], send_sem=send_sems.at[rel - 1],
                        recv_sem=recv_sems.at[rel - 1], device_id=(px, py, pc), device_id_type=MESH_IDS).start()
                    off += r

        def last():
            for rel in range(1, NDEV):
                slot = recv_ref.at[rel - 1]
                pltpu.make_async_remote_copy(src_ref=slot, dst_ref=slot, send_sem=send_sems.at[rel - 1],
                                             recv_sem=recv_sems.at[rel - 1], device_id=peer(rel), device_id_type=MESH_IDS).wait()

        return first, lambda: None, last

    return _Rider([g for g, _ in parts], [jax.ShapeDtypeStruct((NDEV - 1, packed, width), dtype)],
                  [pltpu.SemaphoreType.DMA((NDEV - 1,)), pltpu.SemaphoreType.DMA((NDEV - 1,))], hooks)


def _sum_direct(own, recv, name):
    r, w = own.shape
    tr = max(t for t in range(ROW_ALIGN, 257, ROW_ALIGN) if r % t == 0)

    def body(own_ref, *refs):
        acc = own_ref[...].astype(F32)
        for ref in refs[:-1]:
            acc = acc + ref[...].astype(F32)
        refs[-1][...] = acc

    rblk = lambda k: pl.BlockSpec((None, tr, w), functools.partial(lambda i, k: (k, i, 0), k=k))
    blk = pl.BlockSpec((tr, w), lambda i: (i, 0))
    return pl.pallas_call(body, name=name, grid=(r // tr,), in_specs=[blk] + [rblk(k) for k in range(NDEV - 1)],
                          out_specs=blk, out_shape=jax.ShapeDtypeStruct((r, w), F32),
                          compiler_params=_cparams("parallel"))(own, *([recv] * (NDEV - 1)))


ROW_ALIGN = 16


def _window_start(rows_per_dev, k):
    return rows_per_dev * k // ROW_ALIGN * ROW_ALIGN


def _exchange_in_chip(parts, name, collective_id):
    packed = sum(win for _, _, win, _ in parts)
    width, dtype = parts[0][0].shape[1], parts[0][0].dtype

    def body(g_refs, out_refs, send_sems, recv_sems):
        (recv_ref,) = out_refs
        ax, ay, ac = _position()
        sibling = (ax, ay, 1 - ac)
        _handshake([sibling])
        for q in range(4):
            for g_ref, (_, r, win, off) in zip(g_refs, parts):
                there = g_ref.at[pl.ds(pl.multiple_of(_window_start(r, 2 * q + 1 - ac), ROW_ALIGN), win)]
                pltpu.make_async_remote_copy(src_ref=there, dst_ref=recv_ref.at[q, pl.ds(off, win)], send_sem=send_sems.at[q],
                                             recv_sem=recv_sems.at[q], device_id=sibling, device_id_type=MESH_IDS).start()
        for q in range(4):
            pltpu.make_async_remote_copy(src_ref=recv_ref.at[q], dst_ref=recv_ref.at[q], send_sem=send_sems.at[q],
                                         recv_sem=recv_sems.at[q], device_id=sibling, device_id_type=MESH_IDS).wait()

    return _on_sequencer(body, [g for g, _, _, _ in parts], [jax.ShapeDtypeStruct((4, packed, width), dtype)],
                         [pltpu.SemaphoreType.DMA((4,)), pltpu.SemaphoreType.DMA((4,))], name=name, collective_id=collective_id)[0]


def _on_sequencer(body, ins, out_shapes, sems, *, name, collective_id):
    hbm = pltpu.MemorySpace.HBM
    in_refs = [jax.new_ref(a, memory_space=hbm) for a in ins]
    out_refs = [jax.empty_ref(s, memory_space=hbm) for s in out_shapes]

    @pl.kernel(mesh=plsc.ScalarSubcoreMesh(axis_name="sequencer", num_cores=1), name=name, scratch_types=tuple(sems),
               compiler_params=pltpu.CompilerParams(collective_id=collective_id))
    def launch(*sem_refs):
        body(in_refs, out_refs, *sem_refs)

    launch()
    return [r[...] for r in out_refs]


def _handshake(peers):
    barrier = pltpu.get_barrier_semaphore()
    for peer in peers:
        pl.semaphore_signal(barrier, inc=1, device_id=peer, device_id_type=MESH_IDS)
    pl.semaphore_wait(barrier, len(peers))


def _exchange_chips_async(s1, name, collective_id):
    def body(in_refs, out_refs, send_sems, recv_sems):
        (src,), (got,) = in_refs, out_refs
        ax, ay, ac = _position()
        chips = [(1 - ax, ay), (ax, 1 - ay), (1 - ax, 1 - ay)]
        _handshake([(cx, cy, ac) for cx, cy in chips])
        copies = [pltpu.make_async_remote_copy(
            src_ref=src.at[2 * cx + cy], dst_ref=got.at[r], send_sem=send_sems.at[r], recv_sem=recv_sems.at[r],
            device_id=(cx, cy, ac), device_id_type=MESH_IDS) for r, (cx, cy) in enumerate(chips)]
        for cp in copies:
            cp.start()
        for cp in copies:
            cp.wait_recv()
        for cp in copies:
            cp.wait_send()

    return _on_sequencer(body, [s1], [jax.ShapeDtypeStruct((3,) + s1.shape[1:], s1.dtype)],
                         [pltpu.SemaphoreType.DMA((3,)), pltpu.SemaphoreType.DMA((3,))], name=name, collective_id=collective_id)[0]


def _gather_async(xs, name, collective_id):
    rider = _gather_rider(xs)

    def body(in_refs, out_refs, send_sems, recv_sems):
        ax, ay, ac = _position()
        _handshake([(ax, ay, 1 - ac), (1 - ax, ay, ac), (ax, 1 - ay, ac), (1 - ax, 1 - ay, ac)])
        for hook in rider.hooks(in_refs, out_refs, send_sems, recv_sems):
            hook()

    return _on_sequencer(body, rider.arrays, rider.out_shapes, rider.sems, name=name, collective_id=collective_id)


def _gather_balanced(x, name, collective_id):
    m = x.shape[0]
    half = m // 2 // ROW_ALIGN * ROW_ALIGN
    parts = {"all": pl.ds(0, m), "lo": pl.ds(0, half), "hi": pl.ds(half, m - half)}

    def body(in_refs, out_refs, send_sems, recv_sems):
        (x_ref,), (out_ref,) = in_refs, out_refs
        ax, ay, ac = _position()
        me, sibling = (ax, ay, ac), (ax, ay, 1 - ac)
        by_x, by_y, diag = (1 - ax, ay), (ax, 1 - ay), (1 - ax, 1 - ay)
        _handshake([sibling, (*by_x, ac), (*by_y, ac)])

        def copy(k, block, part, to, own=False):
            rows = out_ref.at[4 * block[0] + 2 * block[1] + block[2], parts[part]]
            return pltpu.make_async_remote_copy(src_ref=x_ref if own else rows, dst_ref=rows, send_sem=send_sems.at[k],
                                                recv_sem=recv_sems.at[k], device_id=to, device_id_type=MESH_IDS)

        def own_half(k, part, to):
            rows = out_ref.at[4 * ax + 2 * ay + ac, parts[part]]
            return pltpu.make_async_remote_copy(src_ref=x_ref.at[parts[part]], dst_ref=rows, send_sem=send_sems.at[k],
                                                recv_sem=recv_sems.at[k], device_id=to, device_id_type=MESH_IDS)

        nx, ny, nd = (*by_x, ac), (*by_y, ac), (*diag, ac)
        sends = [copy(0, me, "all", sibling, own=True), own_half(1, "lo", nx), own_half(2, "hi", nx),
                 own_half(3, "hi", ny), own_half(4, "lo", ny), copy(5, nx, "lo", ny), copy(6, ny, "hi", nx),
                 copy(7, nx, "lo", sibling), copy(8, nx, "hi", sibling), copy(9, ny, "hi", sibling),
                 copy(10, ny, "lo", sibling), copy(11, nd, "lo", sibling), copy(12, nd, "hi", sibling)]
        sx, sy, sd = (*by_x, 1 - ac), (*by_y, 1 - ac), (*diag, 1 - ac)
        arrivals = [copy(0, sibling, "all", me), copy(1, nx, "lo", me), copy(2, nx, "hi", me), copy(3, ny, "hi", me),
                    copy(4, ny, "lo", me), copy(5, nd, "lo", me), copy(6, nd, "hi", me), copy(7, sx, "lo", me),
                    copy(8, sx, "hi", me), copy(9, sy, "hi", me), copy(10, sy, "lo", me), copy(11, sd, "lo", me),
                    copy(12, sd, "hi", me)]
        for k in range(5):
            sends[k].start()
        for arrived, onward in ((1, (5, 7)), (3, (6, 9)), (2, (8,)), (4, (10,)), (5, (11,)), (6, (12,))):
            arrivals[arrived].wait_recv()
            for k in onward:
                sends[k].start()
        for k in (0, 7, 8, 9, 10, 11, 12):
            arrivals[k].wait_recv()
        for cp in sends:
            cp.wait_send()

    return _on_sequencer(body, [x], [jax.ShapeDtypeStruct((NDEV,) + x.shape, x.dtype)],
                         [pltpu.SemaphoreType.DMA((13,)), pltpu.SemaphoreType.DMA((13,))], name=name, collective_id=collective_id)[0]


def _scatter_async(parts, name, collective_id):
    rider = _scatter_rider(parts)

    def body(in_refs, out_refs, send_sems, recv_sems):
        ax, ay, ac = _position()
        flip = lambda a, on: 1 - a if on else a
        _handshake([(flip(ax, rel & 4), flip(ay, rel & 2), flip(ac, rel & 1)) for rel in range(1, NDEV)])
        for hook in rider.hooks(in_refs, out_refs, send_sems, recv_sems):
            hook()

    return _on_sequencer(body, rider.arrays, rider.out_shapes, rider.sems, name=name, collective_id=collective_id)[0]


def _sum_in_chip(own, recv, name):
    _, r, w = own.shape
    tr = _tile(r, (256, 128))

    def body(a_ref, b_ref, o_ref):
        o_ref[...] = (a_ref[...].astype(F32) + b_ref[...].astype(F32)).astype(o_ref.dtype)

    blk = pl.BlockSpec((None, tr, w), lambda q, i: (q, i, 0))
    return pl.pallas_call(body, name=name, grid=(4, r // tr), in_specs=[blk, blk], out_specs=blk,
                          out_shape=jax.ShapeDtypeStruct(own.shape, own.dtype),
                          compiler_params=_cparams("parallel", "parallel"))(own, recv)


def _sum_chips(s1, recv, chip, name):
    _, r, w = s1.shape
    tr = _tile(r, (256, 128))

    def body(c_ref, s_ref, r0_ref, r1_ref, r2_ref, o_ref):
        f = lambda ref: ref[...].astype(F32)
        o_ref[...] = ((f(s_ref) + f(r0_ref)) + f(r1_ref)) + f(r2_ref)

    rblk = lambda k: pl.BlockSpec((None, tr, w), functools.partial(lambda i, c, k: (k, i, 0), k=k))
    grid_spec = pltpu.PrefetchScalarGridSpec(
        num_scalar_prefetch=1, grid=(r // tr,),
        in_specs=[pl.BlockSpec((None, tr, w), lambda i, c: (c[0], i, 0)), rblk(0), rblk(1), rblk(2)],
        out_specs=pl.BlockSpec((tr, w), lambda i, c: (i, 0)))
    return pl.pallas_call(body, name=name, grid_spec=grid_spec, out_shape=jax.ShapeDtypeStruct((r, w), F32),
                          compiler_params=_cparams("parallel"))(chip, s1, recv, recv, recv)


def _silu_rows(x, name):
    def body(x_ref, o_ref):
        o_ref[...] = _silu(x_ref[...])

    return pl.pallas_call(body, name=name, out_shape=jax.ShapeDtypeStruct(x.shape, F32))(x)


def _row_sum(x, name):
    def body(x_ref, o_ref):
        acc = x_ref[0:1, :]
        for i in range(1, x.shape[0]):
            acc = acc + x_ref[i:i + 1, :]
        o_ref[...] = acc

    return pl.pallas_call(body, name=name, out_shape=jax.ShapeDtypeStruct((1, x.shape[1]), F32))(x)


def _adamw(w, g, m, v, name):
    cols = w.shape[-1]
    rows = w.size // cols
    tr = _tile(rows, (128,))
    tc = LANE if (tr == rows and rows > 512 and cols % LANE == 0) else cols

    def body(w_ref, g_ref, m_ref, v_ref, d_ref, mo_ref, vo_ref):
        grad = g_ref[...]
        m_new = ADAM_B1 * m_ref[...] + (1.0 - ADAM_B1) * grad
        v_new = ADAM_B2 * v_ref[...] + (1.0 - ADAM_B2) * jnp.square(grad)
        m_hat = m_new / (1.0 - ADAM_B1 ** ADAM_STEP)
        v_hat = v_new / (1.0 - ADAM_B2 ** ADAM_STEP)
        d_ref[...] = -ADAM_LR * (m_hat / (jnp.sqrt(v_hat) + ADAM_EPS) + ADAM_WD * w_ref[...])
        mo_ref[...] = m_new
        vo_ref[...] = v_new

    blk = pl.BlockSpec((tr, tc), lambda i, j: (i, j))
    out = pl.pallas_call(
        body, name=name, grid=(rows // tr, cols // tc), in_specs=[blk] * 4, out_specs=[blk] * 3,
        out_shape=[jax.ShapeDtypeStruct((rows, cols), F32)] * 3, compiler_params=_cparams("parallel", "parallel"),
    )(*[t.reshape(rows, cols) for t in (w, g, m, v)])
    return [t.reshape(w.shape) for t in out]


def _pack(parts, width, row_mult, dtype):
    flat = jnp.concatenate([p.reshape(-1).astype(dtype) for p in parts])
    rows = -(-flat.shape[0] // (width * row_mult)) * row_mult
    return jnp.pad(flat, (0, rows * width - flat.shape[0])).reshape(rows, width)


def _unpack(flat, shapes):
    out, off = [], 0
    for shp in shapes:
        size = 1
        for dim in shp:
            size *= dim
        out.append(flat[:, off:off + size].reshape((flat.shape[0],) + tuple(shp)))
        off += size
    return out


def _devices_to_cols(a):
    _, r, c = a.shape
    return a.transpose(1, 0, 2).reshape(r, NDEV * c)


def kernel(x, c, w_ada, b_ada, norm1_w, w_in, gdn_conv_w, gdn_a_log, gdn_dt_bias, gdn_norm_w, w_gdn_proj, sc_conv_w, w_sc_out, w_o, norm2_w, w_ffn_in, w_ffn_out, w_ada_f, b_ada_f, normf_w, loss_target, m_w_ada, m_b_ada, m_norm1_w, m_w_in, m_gdn_conv_w, m_gdn_a_log, m_gdn_dt_bias, m_gdn_norm_w, m_w_gdn_proj, m_sc_conv_w, m_w_sc_out, m_w_o, m_norm2_w, m_w_ffn_in, m_w_ffn_out, m_w_ada_f, m_b_ada_f, m_normf_w, v_w_ada, v_b_ada, v_norm1_w, v_w_in, v_gdn_conv_w, v_gdn_a_log, v_gdn_dt_bias, v_gdn_norm_w, v_w_gdn_proj, v_sc_conv_w, v_w_sc_out, v_w_o, v_norm2_w, v_w_ffn_in, v_w_ffn_out, v_w_ada_f, v_b_ada_f, v_normf_w):
    bl, s, d = x.shape
    heads = gdn_a_log.shape[-1]
    dff = w_ffn_out.shape[1] * NDEV
    tok = bl * s
    ax, ay, ac = _position()
    dev = 4 * ax + 2 * ay + ac
    as_tok = lambda a: a.reshape(bl, s, a.shape[-1])
    as_mat = lambda a: a.reshape(tok, a.shape[-1])

    small = _all_gather(_pack([c, gdn_conv_w, sc_conv_w], LANE, 8, F32), name="gather_cond")
    c_all, conv_w, sc_w = _unpack(small.reshape(NDEV, -1), [(bl, d), gdn_conv_w.shape[1:], sc_conv_w.shape[1:]])
    c_act = _silu_rows(c_all.reshape(NDEV * bl, d), "cond_silu")
    conv_w, sc_w = _devices_to_cols(conv_w), _devices_to_cols(sc_w)
    n_ada, n_adaf = w_ada.shape[-1], w_ada_f.shape[-1]
    bias = jnp.broadcast_to(lax.dynamic_slice_in_dim(b_ada, dev * n_ada, n_ada, axis=1), (NDEV * bl, n_ada))
    biasf = jnp.broadcast_to(lax.dynamic_slice_in_dim(b_ada_f.reshape(1, -1), dev * n_adaf, n_adaf, axis=1), (NDEV * bl, n_adaf))
    mod_cols = _mm(c_act, w_ada[0], add=bias, name="ada_cols")
    modf_cols = _mm(c_act, w_ada_f, add=biasf, name="adaf_cols")
    mods = _all_gather(jnp.concatenate([mod_cols, modf_cols], axis=1), name="gather_mod")
    mod_all = mods[:, :, :n_ada].transpose(1, 0, 2).reshape(NDEV * bl, NDEV * n_ada)
    modf_all = mods[:, :, n_ada:].transpose(1, 0, 2).reshape(NDEV * bl, NDEV * n_adaf)
    my_rows = lambda a: lax.dynamic_slice_in_dim(a, dev * bl, bl, axis=0)
    sh1, sc1, g1, sh2, sc2, g2 = [t.reshape(bl, 1, d) for t in jnp.split(my_rows(mod_all), 6, axis=1)]
    shf, scf = [t.reshape(bl, 1, d) for t in jnp.split(my_rows(modf_all), 2, axis=1)]

    late = [t.astype(MXU_DTYPE) for t in (w_gdn_proj[0], w_sc_out[0], w_o[0], w_ffn_in[0].T, w_ffn_out[0])]
    rows = [t.shape[0] for t in late] + [w_in.shape[-1]]
    offs = [sum(rows[:i]) for i in range(5)]
    in_send = w_in[0].T.astype(MXU_DTYPE)
    with_own = lambda g, own: lax.dynamic_update_slice_in_dim(g, own[None], dev, axis=0)
    wt_in = with_own(_gather_balanced(in_send, "gather_w_in", 1), in_send).reshape(NDEV * rows[5], d)
    gathered = _gather_async(late[:3], "gather_mixer", 2) + _gather_async(late[3:], "gather_ffn", 3)
    wgp, wso, wo, wt_fi, wfo = [with_own(g, own).reshape(NDEV * own.shape[0], d) for g, own in zip(gathered, late)]
    o_z, o_ab, o_sc, o_ga, o_gb = 3 * d, 4 * d, 4 * d + 2 * heads, 7 * d + 2 * heads, 8 * d + 2 * heads
    s_qkv, s_z, s_sc, s_gate = (0, o_z), (o_z, d), (o_sc, 3 * d), (o_ga, 2 * d)
    wt_ab = jnp.pad(wt_in[o_ab:o_sc], ((0, LANE - 2 * heads), (0, 0)))

    n1w, n2w, nfw = norm1_w.reshape(1, d), norm2_w.reshape(1, d), normf_w.reshape(1, d)
    lanes = lambda a: jnp.pad(a.reshape(1, -1), ((0, 0), (0, LANE - a.size)))
    a_log, dt_bias, gnw = lanes(gdn_a_log), lanes(gdn_dt_bias), gdn_norm_w.reshape(1, HEAD)
    f_gates = functools.partial(_f_gates, heads=heads)
    (h1,) = _tok_fwd(_f_norm_mod, [x], [sh1, sc1], [n1w], [(d, MXU_DTYPE)], name="norm1", ts=512)
    h1m = as_mat(h1)
    p_qkv = as_tok(_mm(h1m, wt_in, tb=True, b_rows=s_qkv, name="in_qkv"))
    p_z = as_tok(_mm(h1m, wt_in, tb=True, b_rows=s_z, name="in_z"))
    p_ab = as_tok(_mm(h1m, wt_ab, tb=True, name="in_ab"))
    p_sc = as_tok(_mm(h1m, wt_in, tb=True, b_rows=s_sc, name="in_sc"))
    p_g = as_tok(_mm(h1m, wt_in, tb=True, b_rows=s_gate, name="in_gate"))
    qkv = _qkv_fwd(p_qkv, conv_w, heads, "qkv_conv")
    (gbeta,) = _tok_fwd(f_gates, [p_ab], [], [a_log, dt_bias], [(LANE, F32)], name="gates", ts=512)
    o, s_all, t_all = _gdn_fwd(qkv, gbeta, heads, "gdn")
    (og,) = _tok_fwd(_f_gdn_out, [o, p_z], [], [(gnw, None)], [(d, MXU_DTYPE)], name="gdn_out", ts=2048, wb=HEAD, cols=heads)
    y_a = as_tok(_mm(as_mat(og), wgp, name="gdn_proj"))
    scp = _sc_fwd(p_sc, sc_w, "sc_conv")
    mrg, y_b = _tok_fwd(_f_merge_keep, [(p_g, 0), (p_g, 1), y_a, _Product(scp, wso)], [], [], [(d, MXU_DTYPE), (d, F32)],
                        name="merge", ts=512, wb=d)
    merge_toks = [(p_g, 0), (p_g, 1), y_a, y_b]
    x2, h2, mix = _tok_fwd(_f_res_norm_mod_keep, [x, _Product(mrg, wo)], [g1, sh2, sc2], [n2w],
                           [(d, F32), (d, MXU_DTYPE), (d, F32)], name="norm2", ts=512)
    act, gu_a, gu_b = _ffn_in_swiglu(as_mat(h2), wt_fi, dff, "ffn_in")

    loss_l, (dx2, dff_out, _), (dg2, dshf, dscf), (dnfw,) = _tok_bwd(
        _f_loss, [x2, _Product(as_tok(act), wfo), loss_target], [g2, shf, scf], [nfw], [], [True, True, False], name="loss",
        ts=256, loss=True, tok_dtype=[F32, MXU_DTYPE, None])
    dffm = as_mat(dff_out)
    dgu_a, dgu_b = _ffn_out_bwd_swiglu(dffm, wfo, gu_a, gu_b, "d_ffn_out")
    gmm = functools.partial(_mm, ta=True, out_dtype=MXU_DTYPE)
    gw_ffn_out = gmm(act, dffm, name="g_ffn_out")
    dh2 = _Product(as_tok(dgu_b), wt_fi, b_rows=(dff, dff), add=_Product(as_tok(dgu_a), wt_fi, b_rows=(0, dff)))
    h2m = as_mat(h2)
    gwt_ffn_in = gmm(dgu_a, h2m, out_rows=2 * dff, name="g_ffn_in_a")
    gwt_ffn_in = gmm(dgu_b, h2m, out_rows=2 * dff, row_off=dff, into=gwt_ffn_in, name="g_ffn_in_b")
    ffn_parts = [(gwt_ffn_in, rows[3]), (gw_ffn_out, rows[4])]
    ffn_recv = _scatter_async(ffn_parts, "scatter_ffn", 4)
    (dx_skip, dmix), (dg1, dsh2, dsc2), (dn2w,) = _tok_bwd(
        _f_res_norm_mod, [x, mix], [g1, sh2, sc2], [n2w], [dx2, dh2], [True, True], name="d_norm2", ts=256,
        tok_dtype=[F32, MXU_DTYPE], after=[gwt_ffn_in, gw_ffn_out])
    gw_o = gmm(as_mat(mrg), as_mat(dmix), name="g_mix_out")
    (dga, dgb, dya, dyb), _, _ = _tok_bwd(_f_merge, merge_toks, [], [], [_Product(dmix, wo, tb=True)], [True] * 4,
                                          name="d_merge", ts=256, wb=d, tok_dtype=MXU_DTYPE)
    dyam, dybm = as_mat(dya), as_mat(dyb)
    dog = as_tok(_mm(dyam, wgp, tb=True, name="d_gdn_proj"))
    gw_gdn_proj = gmm(as_mat(og), dyam, name="g_gdn_proj")
    dscp = as_tok(_mm(dybm, wso, tb=True, name="d_sc_out"))
    gw_sc_out = gmm(as_mat(scp), dybm, name="g_sc_out")
    dsc, g_sc_w = _sc_bwd(p_sc, sc_w, dscp, "d_sc_conv")
    mix_parts = [(gw_gdn_proj, rows[0]), (gw_sc_out, rows[1]), (gw_o, rows[2])]
    mix_recv = _scatter_async(mix_parts, "scatter_mixer", 5)
    (do, dz), _, (g_gnw,) = _tok_bwd(_f_gdn_out, [o, p_z], [], [(gnw, None)], [dog], [True, True], name="d_gdn_out",
                                     ts=2048, wb=HEAD, cols=heads, tok_dtype=[F32, MXU_DTYPE],
                                     after=[gw_gdn_proj, gw_sc_out, gw_o])
    own_rows = lambda parts: jnp.concatenate([lax.dynamic_slice_in_dim(g, dev * r, r, axis=0) for g, r in parts], axis=0)
    dqkv, dgbeta = _gdn_bwd(qkv, gbeta, do, s_all, t_all, heads, "d_gdn")
    dp_qkv, g_conv_w = _qkv_bwd(p_qkv, conv_w, dqkv, heads, "d_qkv_conv")
    ffn_red = _sum_direct(own_rows(ffn_parts), ffn_recv, "sum_ffn")
    mix_red = _sum_direct(own_rows(mix_parts), mix_recv, "sum_mix")
    (dp_ab,), _, (g_a_log, g_dt_bias) = _tok_bwd(f_gates, [p_ab], [], [a_log, dt_bias], [dgbeta], [True], name="d_gates",
                                                 ts=512, tok_dtype=MXU_DTYPE, after=[ffn_red, mix_red])
    r_in = rows[5]
    win = -(-(r_in + max(r_in * k % ROW_ALIGN for k in range(NDEV))) // 128) * 128
    need_rows = max(_window_start(r_in, k) for k in range(NDEV)) + win
    dsc_m = dsc.reshape(3, tok, d)
    gwt_in = ([gmm(as_mat(dp_qkv), h1m, name="g_in_qkv"), gmm(as_mat(dz), h1m, name="g_in_z"),
               gmm(as_mat(dp_ab), h1m, name="g_in_ab")[:2 * heads]]
              + [gmm(dsc_m, h1m, a_index=k, name=f"g_in_sc{k}") for k in range(3)]
              + [gmm(as_mat(dga), h1m, name="g_in_ga"), gmm(as_mat(dgb), h1m, name="g_in_gb")])
    gwt_in = jnp.concatenate(gwt_in + [jnp.zeros((need_rows - NDEV * r_in, d), MXU_DTYPE)], axis=0)
    assert d <= 1024
    wide = [as_mat(dp_qkv), as_mat(dz), dsc_m, as_mat(dga)]
    row_of = lambda t: d * t + jnp.where(t * d >= o_ab, 2 * heads, 0)
    recv1 = _exchange_in_chip([(gwt_in, r_in, win, 0)], "scatter_in_chip", 7)
    own = jnp.stack([lax.dynamic_slice_in_dim(gwt_in, _window_start(r_in, 2 * q + ac), win, axis=0) for q in range(4)])
    s1 = _sum_in_chip(own, recv1, "sum_in_chip")
    recv2 = _exchange_chips_async(s1, "scatter_chips", 6)

    dh1 = _mm(as_mat(dp_ab), wt_ab, name="d_in_ab")
    dh1 = _mm_chain(wide, wt_in, row_of, add=dh1, name="d_in", tk=d)
    dh1 = _Product(dgb, wt_in, b_rows=(o_gb, d), add=as_tok(dh1))
    (grad_x,), (dsh1, dsc1), (dn1w,) = _tok_bwd(_f_norm_mod_skip, [x], [sh1, sc1], [n1w], [dh1, dx_skip], [True],
                                                name="d_norm1", ts=256)
    reduced = _sum_chips(s1, recv2, (2 * ax + ay).reshape(1).astype(jnp.int32), "sum_chips")
    gt_w_in = lax.dynamic_slice_in_dim(reduced, r_in * dev - _window_start(r_in, dev), r_in, axis=0)
    g_w_in = gt_w_in.T.reshape(w_in.shape)
    gt_w_ffn_in = ffn_red[:rows[3]]
    g_w_ffn_in = gt_w_ffn_in.T.reshape(w_ffn_in.shape)
    g_w_ffn_out = ffn_red[rows[3]:].reshape(w_ffn_out.shape)
    g_w_gdn_proj, g_w_sc_out, g_w_o = (mix_red[offs[i]:offs[i] + rows[i]].reshape(ref.shape)
                                       for i, ref in enumerate((w_gdn_proj, w_sc_out, w_o)))

    dmod = jnp.concatenate([t.reshape(bl, d) for t in (dsh1, dsc1, dg1, dsh2, dsc2, dg2)], axis=1)
    dmodf = jnp.concatenate([t.reshape(bl, d) for t in (dshf, dscf)], axis=1)
    summed_parts = [dn1w, dn2w, dnfw, g_gnw, g_a_log, g_dt_bias, g_conv_w, g_sc_w, loss_l]
    partial = _all_gather(_pack([dmod, dmodf] + summed_parts, LANE, 8, F32), name="gather_small")
    partial = partial.reshape(NDEV, -1)
    n_rows = bl * (6 * d + 2 * d)
    dmod_all, dmodf_all = _unpack(partial[:, :n_rows], [(bl, 6 * d), (bl, 2 * d)])
    dmod_all, dmodf_all = dmod_all.reshape(NDEV * bl, 6 * d), dmodf_all.reshape(NDEV * bl, 2 * d)
    totals = _row_sum(partial[:, n_rows:], "sum_small")
    t_n1w, t_n2w, t_nfw, t_gnw, t_a_log, t_dt_bias, t_conv_w, t_sc_w, t_loss = [
        t[0] for t in _unpack(totals, [p.shape for p in summed_parts])]
    my_cols = lambda a, n: lax.dynamic_slice_in_dim(a, dev * n, n, axis=1)
    grads = {
        "w_ada": _mm(c_act, my_cols(dmod_all, n_ada), ta=True, name="g_ada").reshape(w_ada.shape),
        "b_ada": _row_sum(dmod_all, "g_ada_bias").reshape(b_ada.shape),
        "norm1_w": t_n1w.reshape(norm1_w.shape),
        "w_in": g_w_in,
        "gdn_conv_w": my_cols(t_conv_w, gdn_conv_w.shape[-1]).reshape(gdn_conv_w.shape),
        "gdn_a_log": t_a_log[:, :heads].reshape(gdn_a_log.shape),
        "gdn_dt_bias": t_dt_bias[:, :heads].reshape(gdn_dt_bias.shape),
        "gdn_norm_w": t_gnw.reshape(gdn_norm_w.shape),
        "w_gdn_proj": g_w_gdn_proj,
        "sc_conv_w": my_cols(t_sc_w, sc_conv_w.shape[-1]).reshape(sc_conv_w.shape),
        "w_sc_out": g_w_sc_out,
        "w_o": g_w_o,
        "norm2_w": t_n2w.reshape(norm2_w.shape),
        "w_ffn_in": g_w_ffn_in,
        "w_ffn_out": g_w_ffn_out,
        "w_ada_f": _mm(c_act, my_cols(dmodf_all, n_adaf), ta=True, name="g_adaf").reshape(w_ada_f.shape),
        "b_ada_f": _row_sum(dmodf_all, "g_adaf_bias").reshape(b_ada_f.shape),
        "normf_w": t_nfw.reshape(normf_w.shape),
    }
    weights = dict(w_ada=w_ada, b_ada=b_ada, norm1_w=norm1_w, w_in=w_in, gdn_conv_w=gdn_conv_w, gdn_a_log=gdn_a_log,
                   gdn_dt_bias=gdn_dt_bias, gdn_norm_w=gdn_norm_w, w_gdn_proj=w_gdn_proj, sc_conv_w=sc_conv_w,
                   w_sc_out=w_sc_out, w_o=w_o, norm2_w=norm2_w, w_ffn_in=w_ffn_in, w_ffn_out=w_ffn_out, w_ada_f=w_ada_f,
                   b_ada_f=b_ada_f, normf_w=normf_w)
    m_in = [m_w_ada, m_b_ada, m_norm1_w, m_w_in, m_gdn_conv_w, m_gdn_a_log, m_gdn_dt_bias, m_gdn_norm_w, m_w_gdn_proj,
            m_sc_conv_w, m_w_sc_out, m_w_o, m_norm2_w, m_w_ffn_in, m_w_ffn_out, m_w_ada_f, m_b_ada_f, m_normf_w]
    v_in = [v_w_ada, v_b_ada, v_norm1_w, v_w_in, v_gdn_conv_w, v_gdn_a_log, v_gdn_dt_bias, v_gdn_norm_w, v_w_gdn_proj,
            v_sc_conv_w, v_w_sc_out, v_w_o, v_norm2_w, v_w_ffn_in, v_w_ffn_out, v_w_ada_f, v_b_ada_f, v_normf_w]
    deltas, new_m, new_v = [], [], []
    grads_t = {"w_in": gt_w_in, "w_ffn_in": gt_w_ffn_in}
    for (wname, wt), mt, vt in zip(weights.items(), m_in, v_in):
        if wname in grads_t:
            back = lambda a, wt=wt: a.T.reshape(wt.shape)
            dl, mn, vn = (back(a) for a in _adamw(wt[0].T, grads_t[wname], mt[0].T, vt[0].T, "adamw_" + wname))
        else:
            dl, mn, vn = _adamw(wt, grads[wname], mt, vt, "adamw_" + wname)
        deltas.append(dl)
        new_m.append(mn)
        new_v.append(vn)
    loss = t_loss[0, 0]
    return (loss, grad_x, *[grads[k] for k in weights], *deltas, *new_m, *new_v)
```

```python
import functools

import jax
import jax.numpy as jnp
from jax import lax
from jax.experimental import pallas as pl
from jax.experimental.pallas import tpu as pltpu
from jax.experimental.pallas import tpu_sc as plsc

F32 = jnp.float32
MXU_DTYPE = jnp.bfloat16
NDEV = 8
CHUNK = 64
HEAD = 128
LANE = 128
EPS = 1e-6
ADAM_LR, ADAM_B1, ADAM_B2, ADAM_EPS, ADAM_WD, ADAM_STEP = 0.001, 0.9, 0.999, 1e-08, 0.01, 10
VMEM_LIMIT = 48 * 1024 * 1024
MESH_IDS = pl.DeviceIdType.MESH
HIGHEST = lax.Precision.HIGHEST


def _tile(n, cands=(512, 256, 128)):
    for c in cands:
        if n % c == 0:
            return c
    return n


def _cparams(*sem):
    return pltpu.CompilerParams(dimension_semantics=sem, vmem_limit_bytes=VMEM_LIMIT)


def _mm(a, b, *, ta=False, tb=False, add=None, out_dtype=F32, name, b_rows=None, out_rows=None, row_off=0, into=None,
        a_index=None):
    m, k = (a.shape[-1], a.shape[-2]) if ta else a.shape[-2:]
    b_shape = b.shape if b_rows is None else (b_rows[1], b.shape[1])
    n = b_shape[0] if tb else b_shape[1]
    assert k == (b_shape[1] if tb else b_shape[0])
    if ta:
        tm, tn = _tile(m), n if n <= 1024 else _tile(n)
        tk = k if k <= 4096 else _tile(k, (4096, 2048, 1024, 512))
        if tm * tk > 1024 * 2048:
            tk = _tile(k, (2048, 1024, 512))
    else:
        tk = k if k <= 1024 else _tile(k, (1024, 512))
        tn = _tile(n, (1024 if tk <= 1024 else 512, 512, 256, 128))
        tm = _tile(m, (2048 if (tn <= 512 and tk <= 1024) else 1024, 1024, 512, 256, 128))
    nk = k // tk
    dims = (((0 if ta else 1,), (1 if tb else 0,)), ((), ()))
    has_add = add is not None

    def body(*refs):
        a_ref, b_ref = refs[0], refs[1]
        add_ref = refs[2] if has_add else None
        o_ref = refs[2 + has_add + (into is not None)]
        part = lax.dot_general(a_ref[...].astype(MXU_DTYPE), b_ref[...].astype(MXU_DTYPE), dims,
                               preferred_element_type=F32)

        def finish(acc):
            if has_add:
                acc = acc + add_ref[...]
            o_ref[...] = acc.astype(o_ref.dtype)

        if nk == 1:
            finish(part)
        else:
            acc_ref = refs[-1]
            kk = pl.program_id(2)

            @pl.when(kk == 0)
            def _():
                acc_ref[...] = part

            @pl.when(kk > 0)
            def _():
                acc_ref[...] += part

            @pl.when(kk == nk - 1)
            def _():
                finish(acc_ref[...])

    a_blk, a_at = ((tk, tm), lambda i, j, kk: (kk, i)) if ta else ((tm, tk), lambda i, j, kk: (i, kk))
    a_spec = (pl.BlockSpec(a_blk, a_at) if a_index is None else
              pl.BlockSpec((None,) + a_blk, lambda i, j, kk: (a_index,) + a_at(i, j, kk)))
    if b_rows is None:
        b_spec = pl.BlockSpec((tn, tk), lambda i, j, kk: (j, kk)) if tb else pl.BlockSpec((tk, tn), lambda i, j, kk: (kk, j))
    else:
        at = lambda t: pl.multiple_of(b_rows[0] + t, ROW_ALIGN)
        b_spec = (pl.BlockSpec((pl.Element(tn), pl.Element(tk)), lambda i, j, kk: (at(j * tn), kk * tk)) if tb else
                  pl.BlockSpec((pl.Element(tk), pl.Element(tn)), lambda i, j, kk: (at(kk * tk), j * tn)))
    add_spec = pl.BlockSpec((tm, tn), lambda i, j, kk: (i, j))
    assert row_off % tm == 0
    o_spec = pl.BlockSpec((tm, tn), lambda i, j, kk: (i + row_off // tm, j))
    in_specs = [a_spec, b_spec] + ([add_spec] if has_add else []) + ([pl.BlockSpec(memory_space=pl.ANY)] if into is not None else [])
    args = [a, b] + ([add] if has_add else []) + ([into] if into is not None else [])
    return pl.pallas_call(
        body, name=name, grid=(m // tm, n // tn, nk), in_specs=in_specs, out_specs=o_spec,
        out_shape=jax.ShapeDtypeStruct((out_rows or m, n), out_dtype),
        scratch_shapes=[pltpu.VMEM((tm, tn), F32)] if nk > 1 else [],
        input_output_aliases={len(args) - 1: 0} if into is not None else {},
        compiler_params=_cparams("parallel", "parallel", "arbitrary"),
    )(*args)


def _mm_chain(parts, b, row_of_tile, *, add, name, tk=1024, tm=1024):
    m, n = parts[0].shape[-2], b.shape[1]
    tm = min(tm, m)
    tiles = [p.shape[0] if p.ndim == 3 else p.shape[1] // tk for p in parts]
    first = [sum(tiles[:s]) for s in range(len(parts))]
    nk = sum(tiles)

    def body(*refs):
        a_refs, b_ref, add_ref, o_ref, acc_ref = refs[:len(parts)], *refs[len(parts):]
        kk = pl.program_id(1)

        @pl.when(kk == 0)
        def _():
            acc_ref[...] = add_ref[...]

        for a_ref, lo, cnt in zip(a_refs, first, tiles):
            @pl.when(jnp.logical_and(kk >= lo, kk < lo + cnt))
            def _(a_ref=a_ref):
                acc_ref[...] += lax.dot_general(a_ref[...].astype(MXU_DTYPE), b_ref[...].astype(MXU_DTYPE),
                                                (((1,), (0,)), ((), ())), preferred_element_type=F32)

        @pl.when(kk == nk - 1)
        def _():
            o_ref[...] = acc_ref[...]

    tile_of = lambda kk, lo, cnt: jnp.clip(kk - lo, 0, cnt - 1)
    a_specs = [pl.BlockSpec((None, tm, tk), functools.partial(lambda i, kk, lo, cnt: (tile_of(kk, lo, cnt), i, 0), lo=lo, cnt=cnt))
               if p.ndim == 3 else
               pl.BlockSpec((tm, tk), functools.partial(lambda i, kk, lo, cnt: (i, tile_of(kk, lo, cnt)), lo=lo, cnt=cnt))
               for p, lo, cnt in zip(parts, first, tiles)]
    b_spec = pl.BlockSpec((pl.Element(tk), pl.Element(n)), lambda i, kk: (pl.multiple_of(row_of_tile(kk), ROW_ALIGN), 0))
    o_spec = pl.BlockSpec((tm, n), lambda i, kk: (i, 0))
    return pl.pallas_call(
        body, name=name, grid=(m // tm, nk), in_specs=a_specs + [b_spec, o_spec], out_specs=o_spec,
        out_shape=jax.ShapeDtypeStruct((m, n), F32), scratch_shapes=[pltpu.VMEM((tm, n), F32)],
        compiler_params=_cparams("parallel", "arbitrary"),
    )(*parts, b, add)


def _swiglu_tiles(m, half):
    tn = _tile(half, (512, 256, 128))
    return _tile(m, (2048 if tn <= 256 else 1024, 1024, 512, 256, 128)), tn


def _ffn_in_swiglu(h, wt, half, name):
    m, k = h.shape
    tm, tn = _swiglu_tiles(m, half)
    nj = half // tn
    dims = (((1,), (1,)), ((), ()))

    def body(h_ref, wa_ref, wb_ref, act_ref, a_ref, b_ref):
        lhs = h_ref[...].astype(MXU_DTYPE)
        a = lax.dot_general(lhs, wa_ref[...].astype(MXU_DTYPE), dims, preferred_element_type=F32)
        b = lax.dot_general(lhs, wb_ref[...].astype(MXU_DTYPE), dims, preferred_element_type=F32)
        act_ref[...] = (_silu(a) * b).astype(act_ref.dtype)
        a_ref[...] = a.astype(a_ref.dtype)
        b_ref[...] = b.astype(b_ref.dtype)

    out = jax.ShapeDtypeStruct((m, half), MXU_DTYPE)
    oblk = pl.BlockSpec((tm, tn), lambda i, j: (i, j))
    return pl.pallas_call(
        body, name=name, grid=(m // tm, nj),
        in_specs=[pl.BlockSpec((tm, k), lambda i, j: (i, 0)), pl.BlockSpec((tn, k), lambda i, j: (j, 0)),
                  pl.BlockSpec((tn, k), lambda i, j: (j + nj, 0))],
        out_specs=[oblk, oblk, oblk], out_shape=[out, out, out], compiler_params=_cparams("parallel", "parallel"),
    )(h, wt, wt)


def _ffn_out_bwd_swiglu(dff, w, a, b, name):
    m, k = dff.shape
    half = w.shape[0]
    tm, tn = _swiglu_tiles(m, half)

    def body(d_ref, w_ref, a_ref, b_ref, da_ref, db_ref):
        dact = lax.dot_general(d_ref[...].astype(MXU_DTYPE), w_ref[...].astype(MXU_DTYPE), (((1,), (1,)), ((), ())),
                               preferred_element_type=F32)
        av, bv = a_ref[...].astype(F32), b_ref[...].astype(F32)
        sig = jax.nn.sigmoid(av)
        da_ref[...] = (dact * bv * (sig * (1.0 + av * (1.0 - sig)))).astype(da_ref.dtype)
        db_ref[...] = (dact * (av * sig)).astype(db_ref.dtype)

    out = jax.ShapeDtypeStruct((m, half), MXU_DTYPE)
    oblk = pl.BlockSpec((tm, tn), lambda i, j: (i, j))
    return pl.pallas_call(
        body, name=name, grid=(m // tm, half // tn),
        in_specs=[pl.BlockSpec((tm, k), lambda i, j: (i, 0)), pl.BlockSpec((tn, k), lambda i, j: (j, 0)), oblk, oblk],
        out_specs=[oblk, oblk], out_shape=[out, out], compiler_params=_cparams("parallel", "parallel"),
    )(dff, w, a, b)


def _with_off(xs):
    return [x if isinstance(x, tuple) else (x, 0) for x in xs]


def _spec(kind, arr, off, ts, wb):
    w = arr.shape[-1] if wb is None else wb
    col = (lambda j: 0) if wb is None else functools.partial(lambda j, o: o + j, o=off)
    if kind == "tok":
        return pl.BlockSpec((None, ts, w), lambda j, b, i: (b, i, col(j)))
    if kind == "bat":
        return pl.BlockSpec((None, 1, w), lambda j, b, i: (b, 0, col(j)))
    if off is None:
        return pl.BlockSpec(arr.shape, lambda j, b, i: (0, 0))
    return pl.BlockSpec((arr.shape[0], w), lambda j, b, i: (0, col(j)))


class _Product:
    def __init__(self, a, b, *, tb=False, b_rows=None, add=None):
        self.a, self.b, self.tb, self.b_rows, self.add = a, b, tb, b_rows, add
        rows = b.shape[0] if b_rows is None else b_rows[1]
        self.shape = a.shape[:2] + (rows if tb else b.shape[1],)

    def inputs(self, ts):
        a_spec = pl.BlockSpec((None, ts, self.a.shape[2]), lambda j, b, i: (b, i, 0))
        if self.b_rows is None:
            b_spec = pl.BlockSpec(self.b.shape, lambda j, b, i: (0, 0))
        else:
            start, count = self.b_rows
            b_spec = pl.BlockSpec((pl.Element(count), pl.Element(self.b.shape[1])), lambda j, b, i: (start, 0))
        if isinstance(self.add, _Product):
            extra = self.add.inputs(ts)
        else:
            extra = [] if self.add is None else [(self.add, pl.BlockSpec((None, ts, self.shape[2]), lambda j, b, i: (b, i, 0)))]
        return [(self.a, a_spec), (self.b, b_spec)] + extra

    def value(self, refs):
        dims = (((1,), (1 if self.tb else 0,)), ((), ()))
        val = lax.dot_general(refs[0][...].astype(MXU_DTYPE), refs[1][...].astype(MXU_DTYPE), dims, preferred_element_type=F32)
        if isinstance(self.add, _Product):
            return val + self.add.value(refs[2:])
        return val if self.add is None else val + refs[2][...].astype(F32)


def _inputs(groups, kinds, ts, wb):
    loaded = [(a, _spec(kind, a, o, ts, wb)) for g, kind in zip(groups, kinds) for a, o in g if not isinstance(a, _Product)]
    made = [pair for g in groups for a, _ in g if isinstance(a, _Product) for pair in a.inputs(ts)]
    return [a for a, _ in loaded + made], [sp for _, sp in loaded + made]


def _values(refs, groups):
    n_loaded = sum(1 for g in groups for a, _ in g if not isinstance(a, _Product))
    loaded, pos, out = iter(refs[:n_loaded]), n_loaded, []
    for g in groups:
        vals = []
        for a, _ in g:
            if isinstance(a, _Product):
                k = len(a.inputs(1))
                vals.append(a.value(refs[pos:pos + k]))
                pos += k
            else:
                vals.append(next(loaded)[...].astype(F32))
        out.append(vals)
    return out, pos


def _tok_fwd(fn, toks, bats, pars, outs, *, name, ts, wb=None, cols=1):
    groups = [_with_off(toks), _with_off(bats), _with_off(pars)]
    bl, s, _ = groups[0][0][0].shape
    ts = min(ts, s)
    args, in_specs = _inputs(groups, ("tok", "bat", "par"), ts, wb)

    def body(*refs):
        vals, n_in = _values(refs, groups)
        res = fn(*[v for g in vals for v in g])
        for r, val in zip(refs[n_in:], res):
            r[...] = val.astype(r.dtype)

    out_specs = [pl.BlockSpec((None, ts, w if wb is None else wb), lambda j, b, i: (b, i, j)) for w, _ in outs]
    return pl.pallas_call(
        body, name=name, grid=(cols, bl, s // ts), in_specs=in_specs,
        out_specs=out_specs, out_shape=[jax.ShapeDtypeStruct((bl, s, w), dt) for w, dt in outs],
        compiler_params=_cparams("parallel", "parallel", "parallel"),
    )(*args)


def _accumulate(ref, val, first):
    @pl.when(first)
    def _():
        ref[...] = val

    @pl.when(jnp.logical_not(first))
    def _():
        ref[...] += val


def _tok_bwd(fn, toks, bats, pars, cots, need, *, name, ts, wb=None, cols=1, tok_dtype=F32, loss=False, after=()):
    toks, bats, pars, cots = _with_off(toks), _with_off(bats), _with_off(pars), _with_off(cots)
    groups = [toks, bats, pars, cots]
    bl, s, _ = toks[0][0].shape
    ts = min(ts, s)
    nt, nb, npar = len(toks), len(bats), len(pars)
    args, in_specs = _inputs(groups, ("tok", "bat", "par", "tok"), ts, wb)
    args, in_specs = args + list(after), in_specs + [pl.BlockSpec(memory_space=pl.ANY)] * len(after)

    def body(*refs):
        j, b, i = pl.program_id(0), pl.program_id(1), pl.program_id(2)
        (tok_vals, bat_vals, par_vals, cot_vals), o = _values(refs, groups)
        o += len(after)
        outs, vjp = jax.vjp(fn, *tok_vals, *bat_vals, *par_vals)
        if loss:
            ct = (jnp.ones_like(outs[0]),)
            tot = jnp.broadcast_to(jnp.sum(outs[0], keepdims=True), (1, LANE))
            _accumulate(refs[o], tot, jnp.logical_and(b == 0, i == 0))
            o += 1
        else:
            ct = tuple(cot_vals)
        grads = vjp(ct)
        for t in range(nt):
            if need[t]:
                refs[o][...] = grads[t].astype(refs[o].dtype)
                o += 1
        for t in range(nb):
            _accumulate(refs[o], grads[nt + t], i == 0)
            o += 1
        for t in range(npar):
            first = jnp.logical_and(b == 0, i == 0)
            if pars[t][1] is None:
                first = jnp.logical_and(first, j == 0)
            _accumulate(refs[o], grads[nt + nb + t], first)
            o += 1

    full = lambda arr: arr.shape[-1] if wb is None else wb * cols
    blk = lambda arr: arr.shape[-1] if wb is None else wb
    out_specs, out_shape = [], []
    if loss:
        out_specs.append(pl.BlockSpec((1, LANE), lambda j, b, i: (0, 0)))
        out_shape.append(jax.ShapeDtypeStruct((1, LANE), F32))
    for t in range(nt):
        if need[t]:
            out_specs.append(pl.BlockSpec((None, ts, blk(toks[t][0])), lambda j, b, i: (b, i, j)))
            dt = tok_dtype[t] if isinstance(tok_dtype, (list, tuple)) else tok_dtype
            out_shape.append(jax.ShapeDtypeStruct((bl, s, full(toks[t][0])), dt))
    for arr, _ in bats:
        out_specs.append(pl.BlockSpec((None, 1, blk(arr)), lambda j, b, i: (b, 0, j)))
        out_shape.append(jax.ShapeDtypeStruct((bl, 1, full(arr)), F32))
    for arr, off in pars:
        if off is None:
            out_specs.append(pl.BlockSpec(arr.shape, lambda j, b, i: (0, 0)))
            out_shape.append(jax.ShapeDtypeStruct(arr.shape, F32))
        else:
            out_specs.append(pl.BlockSpec((arr.shape[0], blk(arr)), lambda j, b, i: (0, j)))
            out_shape.append(jax.ShapeDtypeStruct((arr.shape[0], full(arr)), F32))
    res = list(pl.pallas_call(
        body, name=name, grid=(cols, bl, s // ts), in_specs=in_specs,
        out_specs=out_specs, out_shape=out_shape, compiler_params=_cparams("arbitrary", "arbitrary", "arbitrary"),
    )(*args))
    tot = res.pop(0) if loss else None
    dtoks = [res.pop(0) if need[t] else None for t in range(nt)]
    dbats = [res.pop(0) for _ in range(nb)]
    dpars = [res.pop(0) for _ in range(npar)]
    return (tot, dtoks, dbats, dpars) if loss else (dtoks, dbats, dpars)


def _silu(x):
    return x * jax.nn.sigmoid(x)


def _rms(x, w):
    return x * lax.rsqrt(jnp.mean(x * x, axis=-1, keepdims=True) + EPS) * w


def _f_norm_mod(x, shift, scale, w):
    return (_rms(x, w) * (1.0 + scale) + shift,)


def _f_norm_mod_skip(x, shift, scale, w):
    return _rms(x, w) * (1.0 + scale) + shift, x


def _f_res_norm_mod(x, mix, gate, shift, scale, w):
    x2 = x + gate * mix
    return x2, _rms(x2, w) * (1.0 + scale) + shift


def _f_res_norm_mod_keep(x, mix, gate, shift, scale, w):
    return (*_f_res_norm_mod(x, mix, gate, shift, scale, w), mix)


def _f_gates(p, a_log, dt_bias, *, heads):
    z = p + dt_bias
    g = -jnp.exp(a_log) * (jnp.maximum(z, 0.0) + jnp.log1p(jnp.exp(jnp.minimum(z, -z))))
    lane = lax.broadcasted_iota(jnp.int32, p.shape, 1)
    return (jnp.where(lane < heads, g, jax.nn.sigmoid(p)),)


def _f_gdn_out(o, z, w):
    return (_rms(o, w) * _silu(z),)


def _f_merge(ga, gb, ya, yb):
    return (jax.nn.sigmoid(ga) * ya + jax.nn.sigmoid(gb) * yb,)


def _f_merge_keep(ga, gb, ya, yb):
    return (*_f_merge(ga, gb, ya, yb), yb)


def _f_loss(x2, ff, tgt, gate, shift, scale, w):
    y = _rms(x2 + gate * ff, w) * (1.0 + scale) + shift
    return (0.5 * jnp.mean(jnp.square(y - tgt), axis=-1, keepdims=True),)


def _shift_down(x, s):
    if s == 0:
        return x
    row = lax.broadcasted_iota(jnp.int32, x.shape, 0)
    return jnp.where(row >= s, pltpu.roll(x, s, 0), 0.0)


def _shift_up(x, s):
    if s == 0:
        return x
    n = x.shape[0]
    row = lax.broadcasted_iota(jnp.int32, x.shape, 0)
    return jnp.where(row < n - s, pltpu.roll(x, n - s, 0), 0.0)


def _conv(x, w):
    width = w.shape[0]
    acc = w[width - 1:width, :] * x
    for j in range(width - 1):
        acc = acc + w[j:j + 1, :] * _shift_down(x, width - 1 - j)
    return acc


def _conv_bwd(dy, x, w, dw_ref, first):
    width = w.shape[0]
    dx = w[width - 1:width, :] * dy
    for j in range(width - 1):
        dx = dx + w[j:j + 1, :] * _shift_up(dy, width - 1 - j)
    for j in range(width):
        row = jnp.sum(dy * _shift_down(x, width - 1 - j), axis=0, keepdims=True)
        _accumulate(dw_ref.at[j:j + 1, :], row, first)
    return dx


def _qkv_act(xc, is_v, scale):
    a = _silu(xc)
    nrm = a * lax.rsqrt(jnp.sum(a * a, axis=-1, keepdims=True) + EPS) * scale
    return jnp.where(is_v, a, nrm)


def _qkv_act_bwd(xc, dout, is_v, scale):
    sig = jax.nn.sigmoid(xc)
    a = xc * sig
    r = lax.rsqrt(jnp.sum(a * a, axis=-1, keepdims=True) + EPS)
    c1 = r * scale
    da = c1 * dout - a * (c1 * r * r * jnp.sum(dout * a, axis=-1, keepdims=True))
    return jnp.where(is_v, dout, da) * (sig * (1.0 + xc * (1.0 - sig)))


def _qkv_consts(j, heads):
    is_v = j >= 2 * heads
    scale = jnp.where(j < heads, HEAD ** -0.5, 1.0).astype(F32)
    return is_v, scale


def _qkv_fwd(p, w, heads, name):
    bl, s, w3 = p.shape

    def body(p_ref, w_ref, o_ref):
        is_v, scale = _qkv_consts(pl.program_id(0), heads)
        o_ref[...] = _qkv_act(_conv(p_ref[...], w_ref[...]), is_v, scale)

    blk = pl.BlockSpec((None, s, HEAD), lambda j, b: (b, 0, j))
    return pl.pallas_call(
        body, name=name, grid=(w3 // HEAD, bl), in_specs=[blk, pl.BlockSpec((w.shape[0], HEAD), lambda j, b: (0, j))],
        out_specs=blk, out_shape=jax.ShapeDtypeStruct(p.shape, F32), compiler_params=_cparams("parallel", "parallel"),
    )(p, w)


def _qkv_bwd(p, w, dout, heads, name):
    bl, s, w3 = p.shape

    def body(p_ref, w_ref, d_ref, dp_ref, dw_ref):
        is_v, scale = _qkv_consts(pl.program_id(0), heads)
        x, wv = p_ref[...], w_ref[...]
        dxc = _qkv_act_bwd(_conv(x, wv), d_ref[...], is_v, scale)
        dp_ref[...] = _conv_bwd(dxc, x, wv, dw_ref, pl.program_id(1) == 0).astype(dp_ref.dtype)

    blk = pl.BlockSpec((None, s, HEAD), lambda j, b: (b, 0, j))
    wblk = pl.BlockSpec((w.shape[0], HEAD), lambda j, b: (0, j))
    return pl.pallas_call(
        body, name=name, grid=(w3 // HEAD, bl), in_specs=[blk, wblk, blk], out_specs=[blk, wblk],
        out_shape=[jax.ShapeDtypeStruct(p.shape, MXU_DTYPE), jax.ShapeDtypeStruct(w.shape, F32)],
        compiler_params=_cparams("arbitrary", "arbitrary"),
    )(p, w, dout)


def _sc_specs(p, w):
    bl, s, w3 = p.shape
    nblk = w3 // 3 // LANE
    sec = lambda k: pl.BlockSpec((None, s, LANE), functools.partial(lambda j, b, k: (b, 0, k * nblk + j), k=k))
    return nblk, [sec(0), sec(1), sec(2)], pl.BlockSpec((w.shape[0], LANE), lambda j, b: (0, j)), \
        pl.BlockSpec((None, s, LANE), lambda j, b: (b, 0, j))


def _sc_fwd(p, w, name):
    bl, s, w3 = p.shape
    nblk, secs, wblk, oblk = _sc_specs(p, w)

    def body(b_ref, c_ref, x_ref, w_ref, o_ref):
        o_ref[...] = (b_ref[...] * _conv(c_ref[...] * x_ref[...], w_ref[...])).astype(o_ref.dtype)

    return pl.pallas_call(
        body, name=name, grid=(nblk, bl), in_specs=secs + [wblk], out_specs=oblk,
        out_shape=jax.ShapeDtypeStruct((bl, s, w3 // 3), MXU_DTYPE), compiler_params=_cparams("parallel", "parallel"),
    )(p, p, p, w)


def _sc_bwd(p, w, dout, name):
    bl, s, w3 = p.shape
    nblk, secs, wblk, oblk = _sc_specs(p, w)

    def body(b_ref, c_ref, x_ref, w_ref, d_ref, dp_ref, dw_ref):
        gb, gc, xin, wv, d = b_ref[...], c_ref[...], x_ref[...], w_ref[...], d_ref[...]
        u = gc * xin
        dp_ref[0] = (d * _conv(u, wv)).astype(dp_ref.dtype)
        du = _conv_bwd(d * gb, u, wv, dw_ref, pl.program_id(1) == 0)
        dp_ref[1] = (du * xin).astype(dp_ref.dtype)
        dp_ref[2] = (du * gc).astype(dp_ref.dtype)

    return pl.pallas_call(
        body, name=name, grid=(nblk, bl), in_specs=secs + [wblk, oblk],
        out_specs=[pl.BlockSpec((3, None, s, LANE), lambda j, b: (0, b, 0, j)), wblk],
        out_shape=[jax.ShapeDtypeStruct((3, bl, s, w3 // 3), MXU_DTYPE), jax.ShapeDtypeStruct(w.shape, F32)],
        compiler_params=_cparams("arbitrary", "arbitrary"),
    )(p, p, p, w, dout)


def _bdot(a, b, ca, cb):
    return lax.dot_general(a.astype(MXU_DTYPE), b.astype(MXU_DTYPE), (((ca,), (cb,)), ((), ())),
                           preferred_element_type=F32)


def _hdot(a, b):
    return lax.dot_general(a, b, (((1,), (0,)), ((), ())), precision=HIGHEST, preferred_element_type=F32)


def _lane_col(x, idx):
    lane = lax.broadcasted_iota(jnp.int32, x.shape, 1)
    return jnp.sum(jnp.where(lane == idx, x, 0.0), axis=1, keepdims=True)


def _chunk_masks():
    r = lax.broadcasted_iota(jnp.int32, (CHUNK, CHUNK), 0)
    c = lax.broadcasted_iota(jnp.int32, (CHUNK, CHUNK), 1)
    return r == c, r >= c, r > c


def _dot3(a, b):
    ah, bh = a.astype(MXU_DTYPE), b.astype(MXU_DTYPE)
    al, bl = (a - ah.astype(F32)).astype(MXU_DTYPE), (b - bh.astype(F32)).astype(MXU_DTYPE)
    dot = lambda x, y: lax.dot_general(x, y, (((1,), (0,)), ((), ())), preferred_element_type=F32)
    return dot(ah, bh) + (dot(ah, bl) + dot(al, bh))


def _tri_inv_steps(low, eye):
    x = -low
    p = jnp.where(eye, 1.0, 0.0) + x
    span = 2
    while span < CHUNK:
        x = _dot3(x, x)
        yield
        p = p + _dot3(p, x)
        yield
        span *= 2
    return p


def _round_robin(gens):
    out, live = [None] * len(gens), list(range(len(gens)))
    while live:
        still = []
        for i in live:
            try:
                next(gens[i])
                still.append(i)
            except StopIteration as stop:
                out[i] = stop.value
        live = still
    return out


def _gdn_pre(q, k, v, gc, beta, masks):
    eye, causal, strict = masks
    gc_row = jnp.sum(jnp.where(eye, gc, 0.0), axis=0, keepdims=True)
    decay = jnp.where(causal, jnp.exp(jnp.where(causal, gc - gc_row, 0.0)), 0.0)
    eg = jnp.exp(gc)
    gl = gc[CHUNK - 1:CHUNK, :]
    kb, vb = k * beta, v * beta
    both = _bdot(jnp.concatenate([kb, q], axis=0), k, 1, 1)
    low = jnp.where(strict, both[:CHUNK] * decay, 0.0)
    qk = jnp.where(causal, both[CHUNK:] * decay, 0.0)
    rest = jnp.exp(gl - gc)
    return dict(decay=decay, eg=eg, gl=gl, kb=kb, vb=vb, kbe=kb * eg, low=low, qk=qk, qg=q * eg, rest=rest, kdec=k * rest)


GROUP = 4


def _gdn_specs(qkv, gbeta, heads, rev):
    bl, s, w3 = qkv.shape
    d, n = w3 // 3, s // CHUNK
    group = GROUP if n % GROUP == 0 else 1
    steps = n // group
    at = (lambda c: steps - 1 - c) if rev else (lambda c: c)
    assert d == heads * HEAD
    rows = group * CHUNK
    sec = pl.BlockSpec((None, rows, w3), lambda b, c: (b, at(c), 0))
    gspec = pl.BlockSpec((None, rows, LANE), lambda b, c: (b, at(c), 0))
    ospec = pl.BlockSpec((None, rows, d), lambda b, c: (b, at(c), 0))
    sspec = pl.BlockSpec((None, group, heads, HEAD, HEAD), lambda b, c: (b, at(c), 0, 0, 0))
    tspec = pl.BlockSpec((None, group, heads, CHUNK, CHUNK), lambda b, c: (b, at(c), 0, 0, 0))
    return bl, s, d, n, group, sec, gspec, ospec, sspec, tspec


def _gdn_fwd(qkv, gbeta, heads, name):
    bl, s, d, n, group, sec, gspec, ospec, sspec, tspec = _gdn_specs(qkv, gbeta, heads, False)
    rows = lambda sub: slice(sub * CHUNK, (sub + 1) * CHUNK)
    pairs = [(h, sub) for h in range(heads) for sub in range(group)]

    def body(x_ref, g_ref, o_ref, s_ref, t_ref, st_ref):
        @pl.when(pl.program_id(1) == 0)
        def _():
            st_ref[...] = jnp.zeros_like(st_ref)

        masks = _chunk_masks()
        eye, causal, _ = masks
        gblks = [g_ref[rows(sub), :] for sub in range(group)]
        gcs = [_hdot(jnp.where(causal, 1.0, 0.0), gb) for gb in gblks]
        st_all = st_ref[...]

        def free(h, sub):
            q, k, v = (x_ref[rows(sub), sec * d + h * HEAD:sec * d + (h + 1) * HEAD] for sec in range(3))
            pre = _gdn_pre(q, k, v, _lane_col(gcs[sub], h), _lane_col(gblks[sub], heads + h), masks)
            yield
            t = yield from _tri_inv_steps(pre["low"], eye)
            uw = _bdot(t, jnp.concatenate([pre["vb"], pre["kbe"]], axis=1), 1, 0)
            return pre, t, uw[:, :HEAD], uw[:, HEAD:]

        pieces = dict(zip(pairs, _round_robin([free(h, sub) for h, sub in pairs])))

        def carry(h):
            st, outs, starts = st_all[h], [], []
            for sub in range(group):
                pre, _, u, w = pieces[h, sub]
                starts.append(st)
                vnew = u - _bdot(w, st, 1, 0)
                yield
                outs.append(_bdot(pre["qg"], st, 1, 0) + _bdot(pre["qk"], vnew, 1, 0))
                st = st * jnp.exp(pre["gl"]) + _bdot(pre["kdec"], vnew, 0, 0)
                yield
            return outs, starts, st

        carried = _round_robin([carry(h) for h in range(heads)])
        per_sub = lambda pick: [[pick(h, sub) for h in range(heads)] for sub in range(group)]
        o_ref[...] = jnp.concatenate([jnp.concatenate(r, axis=1) for r in per_sub(lambda h, sub: carried[h][0][sub])], axis=0)
        s_ref[...] = jnp.stack([jnp.stack(r) for r in per_sub(lambda h, sub: carried[h][1][sub])])
        t_ref[...] = jnp.stack([jnp.stack(r) for r in per_sub(lambda h, sub: pieces[h, sub][1])])
        st_ref[...] = jnp.stack([carried[h][2] for h in range(heads)])

    return pl.pallas_call(
        body, name=name, grid=(bl, n // group), in_specs=[sec, gspec], out_specs=[ospec, sspec, tspec],
        out_shape=[jax.ShapeDtypeStruct((bl, s, d), F32), jax.ShapeDtypeStruct((bl, n, heads, HEAD, HEAD), F32),
                   jax.ShapeDtypeStruct((bl, n, heads, CHUNK, CHUNK), F32)],
        scratch_shapes=[pltpu.VMEM((heads, HEAD, HEAD), F32)], compiler_params=_cparams("parallel", "arbitrary"),
    )(qkv, gbeta)


def _gdn_bwd(qkv, gbeta, dout, s_all, t_all, heads, name):
    bl, s, d, n, group, sec, gspec, ospec, sspec, tspec = _gdn_specs(qkv, gbeta, heads, True)
    rows = lambda sub: slice(sub * CHUNK, (sub + 1) * CHUNK)
    pairs = [(h, sub) for h in range(heads) for sub in range(group)]
    stack, side = functools.partial(jnp.concatenate, axis=0), functools.partial(jnp.concatenate, axis=1)

    def body(x_ref, g_ref, do_ref, s_ref, t_ref, dx_ref, dg_ref, ds_ref):
        @pl.when(pl.program_id(1) == 0)
        def _():
            ds_ref[...] = jnp.zeros_like(ds_ref)

        masks = _chunk_masks()
        eye, causal, strict = masks
        gblks = [g_ref[rows(sub), :] for sub in range(group)]
        gcs = [_hdot(jnp.where(causal, 1.0, 0.0), gb) for gb in gblks]
        lane = lax.broadcasted_iota(jnp.int32, (CHUNK, LANE), 1)
        last_row = lax.broadcasted_iota(jnp.int32, (CHUNK, 1), 0) == CHUNK - 1
        rowsum = lambda a: jnp.sum(a, axis=1, keepdims=True)
        st_all, t_all_, ds_all = s_ref[...], t_ref[...], ds_ref[...]

        def free(h, sub):
            q, k, v = (x_ref[rows(sub), sec * d + h * HEAD:sec * d + (h + 1) * HEAD] for sec in range(3))
            do = do_ref[rows(sub), h * HEAD:(h + 1) * HEAD]
            beta = _lane_col(gblks[sub], heads + h)
            st, t = st_all[sub, h], t_all_[sub, h]
            pre = _gdn_pre(q, k, v, _lane_col(gcs[sub], h), beta, masks)
            yield
            uw = _bdot(t, side([pre["vb"], pre["kbe"]]), 1, 0)
            u, w = uw[:, :HEAD], uw[:, HEAD:]
            yield
            vnew = u - _bdot(w, st, 1, 0)
            yield
            dqk = jnp.where(causal, _bdot(do, vnew, 1, 1), 0.0)
            dqg = _bdot(do, st, 1, 1)
            return dict(q=q, k=k, v=v, do=do, beta=beta, st=st, t=t, pre=pre, w=w, vnew=vnew, dqk=dqk, dqg=dqg)

        pieces = dict(zip(pairs, _round_robin([free(h, sub) for h, sub in pairs])))

        def carry(h):
            dsn, outs = ds_all[h], {}
            for sub in reversed(range(group)):
                pc = pieces[h, sub]
                pre, st, do = pc["pre"], pc["st"], pc["do"]
                egl = jnp.exp(pre["gl"])
                dkdec = _bdot(pc["vnew"], dsn, 1, 1)
                dvnew = _bdot(pre["kdec"], dsn, 1, 0) + _bdot(pre["qk"], do, 0, 0)
                dgl = jnp.sum(dsn * st, keepdims=True) * egl
                yield
                dw = -_bdot(dvnew, st, 1, 1)
                dsn = dsn * egl + _bdot(stack([pre["qg"], -pc["w"]]), stack([do, dvnew]), 0, 0)
                outs[sub] = (dkdec, dvnew, dgl, dw)
                yield
            return outs, dsn

        carried = _round_robin([carry(h) for h in range(heads)])

        def rest(h, sub):
            pc = pieces[h, sub]
            dkdec, dvnew, dgl, dw = carried[h][0][sub]
            q, k, v, beta, t, pre, dqk, dqg = (pc[x] for x in ("q", "k", "v", "beta", "t", "pre", "dqk", "dqg"))
            decay, eg, kb, vb, kbe, low, qk, qg, kdec = (pre[x] for x in ("decay", "eg", "kb", "vb", "kbe", "low", "qk", "qg", "kdec"))
            dt = _bdot(side([dvnew, dw]), side([vb, kbe]), 1, 1)
            by_t = _bdot(t, side([dvnew, dw]), 0, 0)
            dvb, dkbe = by_t[:, :HEAD], by_t[:, HEAD:]
            yield
            inner = _bdot(dt, t, 1, 1)
            yield
            dlow = -jnp.where(strict, _bdot(t, inner, 0, 0), 0.0)
            da, db = dlow * decay, dqk * decay
            yield
            m = dlow * low + dqk * qk
            kdk = dkdec * kdec
            col_of_m = jnp.sum(jnp.where(eye, jnp.sum(m, axis=0, keepdims=True), 0.0), axis=1, keepdims=True)
            dgc = rowsum(m) - col_of_m + rowsum(dqg * qg) + rowsum(dkbe * kbe) - rowsum(kdk)
            dgc = dgc + jnp.where(last_row, dgl + jnp.sum(kdk, keepdims=True), 0.0)
            by_k = _bdot(stack([da, db]), k, 1, 0)
            dkb = by_k[:CHUNK] + dkbe * eg
            yield
            dk = _bdot(stack([da, db]), stack([kb, q]), 0, 0) + dkdec * pre["rest"] + dkb * beta
            dq = by_k[CHUNK:] + dqg * eg
            dbeta = rowsum(dkb * k) + rowsum(dvb * v)
            return dq, dk, dvb * beta, jnp.where(lane == h, dgc, 0.0) + jnp.where(lane == heads + h, dbeta, 0.0)

        done = dict(zip(pairs, _round_robin([rest(h, sub) for h, sub in pairs])))
        dx_ref[...] = stack([side([done[h, sub][i] for i in range(3) for h in range(heads)]) for sub in range(group)])
        ds_ref[...] = jnp.stack([carried[h][1] for h in range(heads)])
        upper = jnp.where(jnp.logical_or(eye, jnp.logical_not(causal)), 1.0, 0.0)
        dgs = []
        for sub in range(group):
            dgb = done[0, sub][3]
            for h in range(1, heads):
                dgb = dgb + done[h, sub][3]
            dgs.append(jnp.where(lane < heads, _hdot(upper, dgb), dgb))
        dg_ref[...] = stack(dgs)

    return pl.pallas_call(
        body, name=name, grid=(bl, n // group), in_specs=[sec, gspec, ospec, sspec, tspec], out_specs=[sec, gspec],
        out_shape=[jax.ShapeDtypeStruct(qkv.shape, F32), jax.ShapeDtypeStruct((bl, s, LANE), F32)],
        scratch_shapes=[pltpu.VMEM((heads, HEAD, HEAD), F32)], compiler_params=_cparams("parallel", "arbitrary"),
    )(qkv, gbeta, dout, s_all, t_all)


def _position():
    return lax.axis_index("x"), lax.axis_index("y"), lax.axis_index("c")


def _all_gather(x, *, name):
    space = pltpu.VMEM

    def body(x_ref, out_ref, send_sems, recv_sems, local_sem):
        ax, ay, ac = _position()
        me, sibling = (ax, ay, ac), (ax, ay, 1 - ac)
        chips = [(1 - ax, ay), (ax, 1 - ay), (1 - ax, 1 - ay)]

        def slot(px, py, pc):
            return out_ref.at[4 * px + 2 * py + pc]

        def copy(k, block, to, src=None):
            return pltpu.make_async_remote_copy(
                src_ref=slot(*block) if src is None else src, dst_ref=slot(*block), send_sem=send_sems.at[k],
                recv_sem=recv_sems.at[k], device_id=to, device_id_type=MESH_IDS)

        mine = pltpu.make_async_copy(x_ref, slot(*me), local_sem)
        mine.start()
        first = [copy(0, me, sibling, src=x_ref)] + [copy(1 + j, me, (*chip, ac), src=x_ref) for j, chip in enumerate(chips)]
        for cp in first:
            cp.start()
        passed = [copy(4 + j, (*chip, ac), sibling) for j, chip in enumerate(chips)]
        for j, chip in enumerate(chips):
            copy(1 + j, (*chip, ac), me).wait_recv()
            passed[j].start()
        copy(0, sibling, me).wait_recv()
        for j, chip in enumerate(chips):
            copy(4 + j, (*chip, 1 - ac), me).wait_recv()
        for cp in first + passed:
            cp.wait_send()
        mine.wait()

    return pl.pallas_call(
        body, name=name, out_shape=jax.ShapeDtypeStruct((NDEV,) + x.shape, x.dtype),
        in_specs=[pl.BlockSpec(memory_space=space)], out_specs=pl.BlockSpec(memory_space=space),
        scratch_shapes=[pltpu.SemaphoreType.DMA((7,)), pltpu.SemaphoreType.DMA((7,)), pltpu.SemaphoreType.DMA],
    )(x)


class _Rider:
    def __init__(self, arrays, out_shapes, sems, hooks):
        self.arrays, self.out_shapes, self.sems, self.hooks = arrays, out_shapes, sems, hooks


def _gather_rider(xs):
    n = len(xs)

    def hooks(x_refs, out_refs, send_sems, recv_sems):
        ax, ay, ac = _position()
        me, sibling = (ax, ay, ac), (ax, ay, 1 - ac)
        chips = [(1 - ax, ay), (ax, 1 - ay), (1 - ax, 1 - ay)]

        def copies(k, block, to, own=False):
            out = []
            for i in range(n):
                slot = out_refs[i].at[4 * block[0] + 2 * block[1] + block[2]]
                out.append(pltpu.make_async_remote_copy(
                    src_ref=x_refs[i] if own else slot, dst_ref=slot, send_sem=send_sems.at[k, i], recv_sem=recv_sems.at[k, i],
                    device_id=to, device_id_type=MESH_IDS))
            return out

        def first():
            for cp in copies(0, me, sibling, own=True):
                cp.start()
            for j, chip in enumerate(chips):
                for cp in copies(1 + j, me, (*chip, ac), own=True):
                    cp.start()

        def mid():
            for j, chip in enumerate(chips):
                for arrived, onward in zip(copies(1 + j, (*chip, ac), me), copies(4 + j, (*chip, ac), sibling)):
                    arrived.wait_recv()
                    onward.start()

        def last():
            for cp in copies(0, sibling, me):
                cp.wait_recv()
            for j, chip in enumerate(chips):
                for cp in copies(4 + j, (*chip, 1 - ac), me):
                    cp.wait_recv()
            for cp in copies(0, me, sibling, own=True):
                cp.wait_send()
            for j, chip in enumerate(chips):
                for cp in copies(1 + j, me, (*chip, ac), own=True) + copies(4 + j, (*chip, ac), sibling):
                    cp.wait_send()

        return first, mid, last

    return _Rider(list(xs), [jax.ShapeDtypeStruct((NDEV,) + x.shape, x.dtype) for x in xs],
                  [pltpu.SemaphoreType.DMA((7, n)), pltpu.SemaphoreType.DMA((7, n))], hooks)


def _scatter_rider(parts):
    packed = sum(r for _, r in parts)
    width, dtype = parts[0][0].shape[1], parts[0][0].dtype

    def hooks(g_refs, out_refs, send_sems, recv_sems):
        (recv_ref,) = out_refs
        ax, ay, ac = _position()

        def peer(rel):
            flip = lambda a, bit: 1 - a if rel & bit else a
            return flip(ax, 4), flip(ay, 2), flip(ac, 1)

        def first():
            for rel in range(1, NDEV):
                px, py, pc = peer(rel)
                off = 0
                for g_ref, (_, r) in zip(g_refs, parts):
                    rows = g_ref.at[pl.ds(pl.multiple_of((4 * px + 2 * py + pc) * r, ROW_ALIGN), r)]
                    pltpu.make_async_remote_copy(
                        src_ref=rows, dst_ref=recv_ref.at[rel - 1, pl.ds(off, r)], send_sem=send_sems.at[rel - 1],
                        recv_sem=recv_sems.at[rel - 1], device_id=(px, py, pc), device_id_type=MESH_IDS).start()
                    off += r

        def last():
            for rel in range(1, NDEV):
                slot = recv_ref.at[rel - 1]
                pltpu.make_async_remote_copy(src_ref=slot, dst_ref=slot, send_sem=send_sems.at[rel - 1],
                                             recv_sem=recv_sems.at[rel - 1], device_id=peer(rel), device_id_type=MESH_IDS).wait()

        return first, lambda: None, last

    return _Rider([g for g, _ in parts], [jax.ShapeDtypeStruct((NDEV - 1, packed, width), dtype)],
                  [pltpu.SemaphoreType.DMA((NDEV - 1,)), pltpu.SemaphoreType.DMA((NDEV - 1,))], hooks)


def _sum_direct(own, recv, name):
    r, w = own.shape
    tr = max(t for t in range(ROW_ALIGN, 257, ROW_ALIGN) if r % t == 0)

    def body(own_ref, *refs):
        acc = own_ref[...].astype(F32)
        for ref in refs[:-1]:
            acc = acc + ref[...].astype(F32)
        refs[-1][...] = acc

    rblk = lambda k: pl.BlockSpec((None, tr, w), functools.partial(lambda i, k: (k, i, 0), k=k))
    blk = pl.BlockSpec((tr, w), lambda i: (i, 0))
    return pl.pallas_call(body, name=name, grid=(r // tr,), in_specs=[blk] + [rblk(k) for k in range(NDEV - 1)],
                          out_specs=blk, out_shape=jax.ShapeDtypeStruct((r, w), F32),
                          compiler_params=_cparams("parallel"))(own, *([recv] * (NDEV - 1)))


ROW_ALIGN = 16


def _window_start(rows_per_dev, k):
    return rows_per_dev * k // ROW_ALIGN * ROW_ALIGN


def _exchange_in_chip(parts, name, collective_id):
    packed = sum(win for _, _, win, _ in parts)
    width, dtype = parts[0][0].shape[1], parts[0][0].dtype

    def body(g_refs, out_refs, send_sems, recv_sems):
        (recv_ref,) = out_refs
        ax, ay, ac = _position()
        sibling = (ax, ay, 1 - ac)
        _handshake([sibling])
        for q in range(4):
            for g_ref, (_, r, win, off) in zip(g_refs, parts):
                there = g_ref.at[pl.ds(pl.multiple_of(_window_start(r, 2 * q + 1 - ac), ROW_ALIGN), win)]
                pltpu.make_async_remote_copy(src_ref=there, dst_ref=recv_ref.at[q, pl.ds(off, win)], send_sem=send_sems.at[q],
                                             recv_sem=recv_sems.at[q], device_id=sibling, device_id_type=MESH_IDS).start()
        for q in range(4):
            pltpu.make_async_remote_copy(src_ref=recv_ref.at[q], dst_ref=recv_ref.at[q], send_sem=send_sems.at[q],
                                         recv_sem=recv_sems.at[q], device_id=sibling, device_id_type=MESH_IDS).wait()

    return _on_sequencer(body, [g for g, _, _, _ in parts], [jax.ShapeDtypeStruct((4, packed, width), dtype)],
                         [pltpu.SemaphoreType.DMA((4,)), pltpu.SemaphoreType.DMA((4,))], name=name, collective_id=collective_id)[0]


def _on_sequencer(body, ins, out_shapes, sems, *, name, collective_id):
    hbm = pltpu.MemorySpace.HBM
    in_refs = [jax.new_ref(a, memory_space=hbm) for a in ins]
    out_refs = [jax.empty_ref(s, memory_space=hbm) for s in out_shapes]

    @pl.kernel(mesh=plsc.ScalarSubcoreMesh(axis_name="sequencer", num_cores=1), name=name, scratch_types=tuple(sems),
               compiler_params=pltpu.CompilerParams(collective_id=collective_id))
    def launch(*sem_refs):
        body(in_refs, out_refs, *sem_refs)

    launch()
    return [r[...] for r in out_refs]


def _handshake(peers):
    barrier = pltpu.get_barrier_semaphore()
    for peer in peers:
        pl.semaphore_signal(barrier, inc=1, device_id=peer, device_id_type=MESH_IDS)
    pl.semaphore_wait(barrier, len(peers))


def _exchange_chips_async(s1, name, collective_id):
    def body(in_refs, out_refs, send_sems, recv_sems):
        (src,), (got,) = in_refs, out_refs
        ax, ay, ac = _position()
        chips = [(1 - ax, ay), (ax, 1 - ay), (1 - ax, 1 - ay)]
        _handshake([(cx, cy, ac) for cx, cy in chips])
        copies = [pltpu.make_async_remote_copy(
            src_ref=src.at[2 * cx + cy], dst_ref=got.at[r], send_sem=send_sems.at[r], recv_sem=recv_sems.at[r],
            device_id=(cx, cy, ac), device_id_type=MESH_IDS) for r, (cx, cy) in enumerate(chips)]
        for cp in copies:
            cp.start()
        for cp in copies:
            cp.wait_recv()
        for cp in copies:
            cp.wait_send()

    return _on_sequencer(body, [s1], [jax.ShapeDtypeStruct((3,) + s1.shape[1:], s1.dtype)],
                         [pltpu.SemaphoreType.DMA((3,)), pltpu.SemaphoreType.DMA((3,))], name=name, collective_id=collective_id)[0]


def _gather_async(xs, name, collective_id):
    rider = _gather_rider(xs)

    def body(in_refs, out_refs, send_sems, recv_sems):
        ax, ay, ac = _position()
        _handshake([(ax, ay, 1 - ac), (1 - ax, ay, ac), (ax, 1 - ay, ac), (1 - ax, 1 - ay, ac)])
        for hook in rider.hooks(in_refs, out_refs, send_sems, recv_sems):
            hook()

    return _on_sequencer(body, rider.arrays, rider.out_shapes, rider.sems, name=name, collective_id=collective_id)


def _gather_balanced(x, name, collective_id):
    m = x.shape[0]
    half = m // 2 // ROW_ALIGN * ROW_ALIGN
    parts = {"all": pl.ds(0, m), "lo": pl.ds(0, half), "hi": pl.ds(half, m - half)}

    def body(in_refs, out_refs, send_sems, recv_sems):
        (x_ref,), (out_ref,) = in_refs, out_refs
        ax, ay, ac = _position()
        me, sibling = (ax, ay, ac), (ax, ay, 1 - ac)
        by_x, by_y, diag = (1 - ax, ay), (ax, 1 - ay), (1 - ax, 1 - ay)
        _handshake([sibling, (*by_x, ac), (*by_y, ac)])

        def copy(k, block, part, to, own=False):
            rows = out_ref.at[4 * block[0] + 2 * block[1] + block[2], parts[part]]
            return pltpu.make_async_remote_copy(src_ref=x_ref if own else rows, dst_ref=rows, send_sem=send_sems.at[k],
                                                recv_sem=recv_sems.at[k], device_id=to, device_id_type=MESH_IDS)

        def own_half(k, part, to):
            rows = out_ref.at[4 * ax + 2 * ay + ac, parts[part]]
            return pltpu.make_async_remote_copy(src_ref=x_ref.at[parts[part]], dst_ref=rows, send_sem=send_sems.at[k],
                                                recv_sem=recv_sems.at[k], device_id=to, device_id_type=MESH_IDS)

        nx, ny, nd = (*by_x, ac), (*by_y, ac), (*diag, ac)
        sends = [copy(0, me, "all", sibling, own=True), own_half(1, "lo", nx), own_half(2, "hi", nx),
                 own_half(3, "hi", ny), own_half(4, "lo", ny), copy(5, nx, "lo", ny), copy(6, ny, "hi", nx),
                 copy(7, nx, "lo", sibling), copy(8, nx, "hi", sibling), copy(9, ny, "hi", sibling),
                 copy(10, ny, "lo", sibling), copy(11, nd, "lo", sibling), copy(12, nd, "hi", sibling)]
        sx, sy, sd = (*by_x, 1 - ac), (*by_y, 1 - ac), (*diag, 1 - ac)
        arrivals = [copy(0, sibling, "all", me), copy(1, nx, "lo", me), copy(2, nx, "hi", me), copy(3, ny, "hi", me),
                    copy(4, ny, "lo", me), copy(5, nd, "lo", me), copy(6, nd, "hi", me), copy(7, sx, "lo", me),
                    copy(8, sx, "hi", me), copy(9, sy, "hi", me), copy(10, sy, "lo", me), copy(11, sd, "lo", me),
                    copy(12, sd, "hi", me)]
        for k in range(5):
            sends[k].start()
        for arrived, onward in ((1, (5, 7)), (3, (6, 9)), (2, (8,)), (4, (10,)), (5, (11,)), (6, (12,))):
            arrivals[arrived].wait_recv()
            for k in onward:
                sends[k].start()
        for k in (0, 7, 8, 9, 10, 11, 12):
            arrivals[k].wait_recv()
        for cp in sends:
            cp.wait_send()

    return _on_sequencer(body, [x], [jax.ShapeDtypeStruct((NDEV,) + x.shape, x.dtype)],
                         [pltpu.SemaphoreType.DMA((13,)), pltpu.SemaphoreType.DMA((13,))], name=name, collective_id=collective_id)[0]


def _scatter_async(parts, name, collective_id):
    rider = _scatter_rider(parts)

    def body(in_refs, out_refs, send_sems, recv_sems):
        ax, ay, ac = _position()
        flip = lambda a, on: 1 - a if on else a
        _handshake([(flip(ax, rel & 4), flip(ay, rel & 2), flip(ac, rel & 1)) for rel in range(1, NDEV)])
        for hook in rider.hooks(in_refs, out_refs, send_sems, recv_sems):
            hook()

    return _on_sequencer(body, rider.arrays, rider.out_shapes, rider.sems, name=name, collective_id=collective_id)[0]


def _sum_in_chip(own, recv, name):
    _, r, w = own.shape
    tr = _tile(r, (256, 128))

    def body(a_ref, b_ref, o_ref):
        o_ref[...] = (a_ref[...].astype(F32) + b_ref[...].astype(F32)).astype(o_ref.dtype)

    blk = pl.BlockSpec((None, tr, w), lambda q, i: (q, i, 0))
    return pl.pallas_call(body, name=name, grid=(4, r // tr), in_specs=[blk, blk], out_specs=blk,
                          out_shape=jax.ShapeDtypeStruct(own.shape, own.dtype),
                          compiler_params=_cparams("parallel", "parallel"))(own, recv)


def _sum_chips(s1, recv, chip, name):
    _, r, w = s1.shape
    tr = _tile(r, (256, 128))

    def body(c_ref, s_ref, r0_ref, r1_ref, r2_ref, o_ref):
        f = lambda ref: ref[...].astype(F32)
        o_ref[...] = ((f(s_ref) + f(r0_ref)) + f(r1_ref)) + f(r2_ref)

    rblk = lambda k: pl.BlockSpec((None, tr, w), functools.partial(lambda i, c, k: (k, i, 0), k=k))
    grid_spec = pltpu.PrefetchScalarGridSpec(
        num_scalar_prefetch=1, grid=(r // tr,),
        in_specs=[pl.BlockSpec((None, tr, w), lambda i, c: (c[0], i, 0)), rblk(0), rblk(1), rblk(2)],
        out_specs=pl.BlockSpec((tr, w), lambda i, c: (i, 0)))
    return pl.pallas_call(body, name=name, grid_spec=grid_spec, out_shape=jax.ShapeDtypeStruct((r, w), F32),
                          compiler_params=_cparams("parallel"))(chip, s1, recv, recv, recv)


def _silu_rows(x, name):
    def body(x_ref, o_ref):
        o_ref[...] = _silu(x_ref[...])

    return pl.pallas_call(body, name=name, out_shape=jax.ShapeDtypeStruct(x.shape, F32))(x)


def _row_sum(x, name):
    def body(x_ref, o_ref):
        acc = x_ref[0:1, :]
        for i in range(1, x.shape[0]):
            acc = acc + x_ref[i:i + 1, :]
        o_ref[...] = acc

    return pl.pallas_call(body, name=name, out_shape=jax.ShapeDtypeStruct((1, x.shape[1]), F32))(x)


def _adamw(w, g, m, v, name):
    cols = w.shape[-1]
    rows = w.size // cols
    tr = _tile(rows, (128,))
    tc = LANE if (tr == rows and rows > 512 and cols % LANE == 0) else cols

    def body(w_ref, g_ref, m_ref, v_ref, d_ref, mo_ref, vo_ref):
        grad = g_ref[...]
        m_new = ADAM_B1 * m_ref[...] + (1.0 - ADAM_B1) * grad
        v_new = ADAM_B2 * v_ref[...] + (1.0 - ADAM_B2) * jnp.square(grad)
        m_hat = m_new / (1.0 - ADAM_B1 ** ADAM_STEP)
        v_hat = v_new / (1.0 - ADAM_B2 ** ADAM_STEP)
        d_ref[...] = -ADAM_LR * (m_hat / (jnp.sqrt(v_hat) + ADAM_EPS) + ADAM_WD * w_ref[...])
        mo_ref[...] = m_new
        vo_ref[...] = v_new

    blk = pl.BlockSpec((tr, tc), lambda i, j: (i, j))
    out = pl.pallas_call(
        body, name=name, grid=(rows // tr, cols // tc), in_specs=[blk] * 4, out_specs=[blk] * 3,
        out_shape=[jax.ShapeDtypeStruct((rows, cols), F32)] * 3, compiler_params=_cparams("parallel", "parallel"),
    )(*[t.reshape(rows, cols) for t in (w, g, m, v)])
    return [t.reshape(w.shape) for t in out]


def _pack(parts, width, row_mult, dtype):
    flat = jnp.concatenate([p.reshape(-1).astype(dtype) for p in parts])
    rows = -(-flat.shape[0] // (width * row_mult)) * row_mult
    return jnp.pad(flat, (0, rows * width - flat.shape[0])).reshape(rows, width)


def _unpack(flat, shapes):
    out, off = [], 0
    for shp in shapes:
        size = 1
        for dim in shp:
            size *= dim
        out.append(flat[:, off:off + size].reshape((flat.shape[0],) + tuple(shp)))
        off += size
    return out


def _devices_to_cols(a):
    _, r, c = a.shape
    return a.transpose(1, 0, 2).reshape(r, NDEV * c)


def kernel(x, c, w_ada, b_ada, norm1_w, w_in, gdn_conv_w, gdn_a_log, gdn_dt_bias, gdn_norm_w, w_gdn_proj, sc_conv_w, w_sc_out, w_o, norm2_w, w_ffn_in, w_ffn_out, w_ada_f, b_ada_f, normf_w, loss_target, m_w_ada, m_b_ada, m_norm1_w, m_w_in, m_gdn_conv_w, m_gdn_a_log, m_gdn_dt_bias, m_gdn_norm_w, m_w_gdn_proj, m_sc_conv_w, m_w_sc_out, m_w_o, m_norm2_w, m_w_ffn_in, m_w_ffn_out, m_w_ada_f, m_b_ada_f, m_normf_w, v_w_ada, v_b_ada, v_norm1_w, v_w_in, v_gdn_conv_w, v_gdn_a_log, v_gdn_dt_bias, v_gdn_norm_w, v_w_gdn_proj, v_sc_conv_w, v_w_sc_out, v_w_o, v_norm2_w, v_w_ffn_in, v_w_ffn_out, v_w_ada_f, v_b_ada_f, v_normf_w):
    bl, s, d = x.shape
    heads = gdn_a_log.shape[-1]
    dff = w_ffn_out.shape[1] * NDEV
    tok = bl * s
    ax, ay, ac = _position()
    dev = 4 * ax + 2 * ay + ac
    as_tok = lambda a: a.reshape(bl, s, a.shape[-1])
    as_mat = lambda a: a.reshape(tok, a.shape[-1])

    small = _all_gather(_pack([c, gdn_conv_w, sc_conv_w], LANE, 8, F32), name="gather_cond")
    c_all, conv_w, sc_w = _unpack(small.reshape(NDEV, -1), [(bl, d), gdn_conv_w.shape[1:], sc_conv_w.shape[1:]])
    c_act = _silu_rows(c_all.reshape(NDEV * bl, d), "cond_silu")
    conv_w, sc_w = _devices_to_cols(conv_w), _devices_to_cols(sc_w)
    n_ada, n_adaf = w_ada.shape[-1], w_ada_f.shape[-1]
    bias = jnp.broadcast_to(lax.dynamic_slice_in_dim(b_ada, dev * n_ada, n_ada, axis=1), (NDEV * bl, n_ada))
    biasf = jnp.broadcast_to(lax.dynamic_slice_in_dim(b_ada_f.reshape(1, -1), dev * n_adaf, n_adaf, axis=1), (NDEV * bl, n_adaf))
    mod_cols = _mm(c_act, w_ada[0], add=bias, name="ada_cols")
    modf_cols = _mm(c_act, w_ada_f, add=biasf, name="adaf_cols")
    mods = _all_gather(jnp.concatenate([mod_cols, modf_cols], axis=1), name="gather_mod")
    mod_all = mods[:, :, :n_ada].transpose(1, 0, 2).reshape(NDEV * bl, NDEV * n_ada)
    modf_all = mods[:, :, n_ada:].transpose(1, 0, 2).reshape(NDEV * bl, NDEV * n_adaf)
    my_rows = lambda a: lax.dynamic_slice_in_dim(a, dev * bl, bl, axis=0)
    sh1, sc1, g1, sh2, sc2, g2 = [t.reshape(bl, 1, d) for t in jnp.split(my_rows(mod_all), 6, axis=1)]
    shf, scf = [t.reshape(bl, 1, d) for t in jnp.split(my_rows(modf_all), 2, axis=1)]

    late = [t.astype(MXU_DTYPE) for t in (w_gdn_proj[0], w_sc_out[0], w_o[0], w_ffn_in[0].T, w_ffn_out[0])]
    rows = [t.shape[0] for t in late] + [w_in.shape[-1]]
    offs = [sum(rows[:i]) for i in range(5)]
    in_send = w_in[0].T.astype(MXU_DTYPE)
    with_own = lambda g, own: lax.dynamic_update_slice_in_dim(g, own[None], dev, axis=0)
    wt_in = with_own(_gather_balanced(in_send, "gather_w_in", 1), in_send).reshape(NDEV * rows[5], d)
    gathered = _gather_async(late[:3], "gather_mixer", 2) + _gather_async(late[3:], "gather_ffn", 3)
    wgp, wso, wo, wt_fi, wfo = [with_own(g, own).reshape(NDEV * own.shape[0], d) for g, own in zip(gathered, late)]
    o_z, o_ab, o_sc, o_ga, o_gb = 3 * d, 4 * d, 4 * d + 2 * heads, 7 * d + 2 * heads, 8 * d + 2 * heads
    s_qkv, s_z, s_sc, s_gate = (0, o_z), (o_z, d), (o_sc, 3 * d), (o_ga, 2 * d)
    wt_ab = jnp.pad(wt_in[o_ab:o_sc], ((0, LANE - 2 * heads), (0, 0)))

    n1w, n2w, nfw = norm1_w.reshape(1, d), norm2_w.reshape(1, d), normf_w.reshape(1, d)
    lanes = lambda a: jnp.pad(a.reshape(1, -1), ((0, 0), (0, LANE - a.size)))
    a_log, dt_bias, gnw = lanes(gdn_a_log), lanes(gdn_dt_bias), gdn_norm_w.reshape(1, HEAD)
    f_gates = functools.partial(_f_gates, heads=heads)
    (h1,) = _tok_fwd(_f_norm_mod, [x], [sh1, sc1], [n1w], [(d, MXU_DTYPE)], name="norm1", ts=512)
    h1m = as_mat(h1)
    p_qkv = as_tok(_mm(h1m, wt_in, tb=True, b_rows=s_qkv, name="in_qkv"))
    p_z = as_tok(_mm(h1m, wt_in, tb=True, b_rows=s_z, name="in_z"))
    p_ab = as_tok(_mm(h1m, wt_ab, tb=True, name="in_ab"))
    p_sc = as_tok(_mm(h1m, wt_in, tb=True, b_rows=s_sc, name="in_sc"))
    p_g = as_tok(_mm(h1m, wt_in, tb=True, b_rows=s_gate, name="in_gate"))
    qkv = _qkv_fwd(p_qkv, conv_w, heads, "qkv_conv")
    (gbeta,) = _tok_fwd(f_gates, [p_ab], [], [a_log, dt_bias], [(LANE, F32)], name="gates", ts=512)
    o, s_all, t_all = _gdn_fwd(qkv, gbeta, heads, "gdn")
    (og,) = _tok_fwd(_f_gdn_out, [o, p_z], [], [(gnw, None)], [(d, MXU_DTYPE)], name="gdn_out", ts=2048, wb=HEAD, cols=heads)
    y_a = as_tok(_mm(as_mat(og), wgp, name="gdn_proj"))
    scp = _sc_fwd(p_sc, sc_w, "sc_conv")
    mrg, y_b = _tok_fwd(_f_merge_keep, [(p_g, 0), (p_g, 1), y_a, _Product(scp, wso)], [], [], [(d, MXU_DTYPE), (d, F32)],
                        name="merge", ts=512, wb=d)
    merge_toks = [(p_g, 0), (p_g, 1), y_a, y_b]
    x2, h2, mix = _tok_fwd(_f_res_norm_mod_keep, [x, _Product(mrg, wo)], [g1, sh2, sc2], [n2w],
                           [(d, F32), (d, MXU_DTYPE), (d, F32)], name="norm2", ts=512)
    act, gu_a, gu_b = _ffn_in_swiglu(as_mat(h2), wt_fi, dff, "ffn_in")

    loss_l, (dx2, dff_out, _), (dg2, dshf, dscf), (dnfw,) = _tok_bwd(
        _f_loss, [x2, _Product(as_tok(act), wfo), loss_target], [g2, shf, scf], [nfw], [], [True, True, False], name="loss",
        ts=512, loss=True, tok_dtype=[F32, MXU_DTYPE, None])
    dffm = as_mat(dff_out)
    dgu_a, dgu_b = _ffn_out_bwd_swiglu(dffm, wfo, gu_a, gu_b, "d_ffn_out")
    gmm = functools.partial(_mm, ta=True, out_dtype=MXU_DTYPE)
    gw_ffn_out = gmm(act, dffm, name="g_ffn_out")
    dh2 = _Product(as_tok(dgu_b), wt_fi, b_rows=(dff, dff), add=_Product(as_tok(dgu_a), wt_fi, b_rows=(0, dff)))
    h2m = as_mat(h2)
    gwt_ffn_in = gmm(dgu_a, h2m, out_rows=2 * dff, name="g_ffn_in_a")
    gwt_ffn_in = gmm(dgu_b, h2m, out_rows=2 * dff, row_off=dff, into=gwt_ffn_in, name="g_ffn_in_b")
    ffn_parts = [(gwt_ffn_in, rows[3]), (gw_ffn_out, rows[4])]
    ffn_recv = _scatter_async(ffn_parts, "scatter_ffn", 4)
    (dx_skip, dmix), (dg1, dsh2, dsc2), (dn2w,) = _tok_bwd(
        _f_res_norm_mod, [x, mix], [g1, sh2, sc2], [n2w], [dx2, dh2], [True, True], name="d_norm2", ts=256,
        tok_dtype=[F32, MXU_DTYPE], after=[gwt_ffn_in, gw_ffn_out])
    gw_o = gmm(as_mat(mrg), as_mat(dmix), name="g_mix_out")
    (dga, dgb, dya, dyb), _, _ = _tok_bwd(_f_merge, merge_toks, [], [], [_Product(dmix, wo, tb=True)], [True] * 4,
                                          name="d_merge", ts=256, wb=d, tok_dtype=MXU_DTYPE)
    dyam, dybm = as_mat(dya), as_mat(dyb)
    dog = as_tok(_mm(dyam, wgp, tb=True, name="d_gdn_proj"))
    gw_gdn_proj = gmm(as_mat(og), dyam, name="g_gdn_proj")
    dscp = as_tok(_mm(dybm, wso, tb=True, name="d_sc_out"))
    gw_sc_out = gmm(as_mat(scp), dybm, name="g_sc_out")
    dsc, g_sc_w = _sc_bwd(p_sc, sc_w, dscp, "d_sc_conv")
    mix_parts = [(gw_gdn_proj, rows[0]), (gw_sc_out, rows[1]), (gw_o, rows[2])]
    mix_recv = _scatter_async(mix_parts, "scatter_mixer", 5)
    (do, dz), _, (g_gnw,) = _tok_bwd(_f_gdn_out, [o, p_z], [], [(gnw, None)], [dog], [True, True], name="d_gdn_out",
                                     ts=2048, wb=HEAD, cols=heads, tok_dtype=[F32, MXU_DTYPE],
                                     after=[gw_gdn_proj, gw_sc_out, gw_o])
    own_rows = lambda parts: jnp.concatenate([lax.dynamic_slice_in_dim(g, dev * r, r, axis=0) for g, r in parts], axis=0)
    dqkv, dgbeta = _gdn_bwd(qkv, gbeta, do, s_all, t_all, heads, "d_gdn")
    dp_qkv, g_conv_w = _qkv_bwd(p_qkv, conv_w, dqkv, heads, "d_qkv_conv")
    ffn_red = _sum_direct(own_rows(ffn_parts), ffn_recv, "sum_ffn")
    mix_red = _sum_direct(own_rows(mix_parts), mix_recv, "sum_mix")
    (dp_ab,), _, (g_a_log, g_dt_bias) = _tok_bwd(f_gates, [p_ab], [], [a_log, dt_bias], [dgbeta], [True], name="d_gates",
                                                 ts=512, tok_dtype=MXU_DTYPE, after=[ffn_red, mix_red])
    r_in = rows[5]
    win = -(-(r_in + max(r_in * k % ROW_ALIGN for k in range(NDEV))) // 128) * 128
    need_rows = max(_window_start(r_in, k) for k in range(NDEV)) + win
    dsc_m = dsc.reshape(3, tok, d)
    gwt_in = ([gmm(as_mat(dp_qkv), h1m, name="g_in_qkv"), gmm(as_mat(dz), h1m, name="g_in_z"),
               gmm(as_mat(dp_ab), h1m, name="g_in_ab")[:2 * heads]]
              + [gmm(dsc_m, h1m, a_index=k, name=f"g_in_sc{k}") for k in range(3)]
              + [gmm(as_mat(dga), h1m, name="g_in_ga"), gmm(as_mat(dgb), h1m, name="g_in_gb")])
    gwt_in = jnp.concatenate(gwt_in + [jnp.zeros((need_rows - NDEV * r_in, d), MXU_DTYPE)], axis=0)
    assert d <= 1024
    wide = [as_mat(dp_qkv), as_mat(dz), dsc_m, as_mat(dga)]
    row_of = lambda t: d * t + jnp.where(t * d >= o_ab, 2 * heads, 0)
    recv1 = _exchange_in_chip([(gwt_in, r_in, win, 0)], "scatter_in_chip", 7)
    own = jnp.stack([lax.dynamic_slice_in_dim(gwt_in, _window_start(r_in, 2 * q + ac), win, axis=0) for q in range(4)])
    s1 = _sum_in_chip(own, recv1, "sum_in_chip")
    recv2 = _exchange_chips_async(s1, "scatter_chips", 6)

    dh1 = _mm(as_mat(dp_ab), wt_ab, name="d_in_ab")
    dh1 = _mm_chain(wide, wt_in, row_of, add=dh1, name="d_in", tk=d)
    dh1 = _Product(dgb, wt_in, b_rows=(o_gb, d), add=as_tok(dh1))
    (grad_x,), (dsh1, dsc1), (dn1w,) = _tok_bwd(_f_norm_mod_skip, [x], [sh1, sc1], [n1w], [dh1, dx_skip], [True],
                                                name="d_norm1", ts=512)
    reduced = _sum_chips(s1, recv2, (2 * ax + ay).reshape(1).astype(jnp.int32), "sum_chips")
    gt_w_in = lax.dynamic_slice_in_dim(reduced, r_in * dev - _window_start(r_in, dev), r_in, axis=0)
    g_w_in = gt_w_in.T.reshape(w_in.shape)
    gt_w_ffn_in = ffn_red[:rows[3]]
    g_w_ffn_in = gt_w_ffn_in.T.reshape(w_ffn_in.shape)
    g_w_ffn_out = ffn_red[rows[3]:].reshape(w_ffn_out.shape)
    g_w_gdn_proj, g_w_sc_out, g_w_o = (mix_red[offs[i]:offs[i] + rows[i]].reshape(ref.shape)
                                       for i, ref in enumerate((w_gdn_proj, w_sc_out, w_o)))

    dmod = jnp.concatenate([t.reshape(bl, d) for t in (dsh1, dsc1, dg1, dsh2, dsc2, dg2)], axis=1)
    dmodf = jnp.concatenate([t.reshape(bl, d) for t in (dshf, dscf)], axis=1)
    summed_parts = [dn1w, dn2w, dnfw, g_gnw, g_a_log, g_dt_bias, g_conv_w, g_sc_w, loss_l]
    partial = _all_gather(_pack([dmod, dmodf] + summed_parts, LANE, 8, F32), name="gather_small")
    partial = partial.reshape(NDEV, -1)
    n_rows = bl * (6 * d + 2 * d)
    dmod_all, dmodf_all = _unpack(partial[:, :n_rows], [(bl, 6 * d), (bl, 2 * d)])
    dmod_all, dmodf_all = dmod_all.reshape(NDEV * bl, 6 * d), dmodf_all.reshape(NDEV * bl, 2 * d)
    totals = _row_sum(partial[:, n_rows:], "sum_small")
    t_n1w, t_n2w, t_nfw, t_gnw, t_a_log, t_dt_bias, t_conv_w, t_sc_w, t_loss = [
        t[0] for t in _unpack(totals, [p.shape for p in summed_parts])]
    my_cols = lambda a, n: lax.dynamic_slice_in_dim(a, dev * n, n, axis=1)
    grads = {
        "w_ada": _mm(c_act, my_cols(dmod_all, n_ada), ta=True, name="g_ada").reshape(w_ada.shape),
        "b_ada": _row_sum(dmod_all, "g_ada_bias").reshape(b_ada.shape),
        "norm1_w": t_n1w.reshape(norm1_w.shape),
        "w_in": g_w_in,
        "gdn_conv_w": my_cols(t_conv_w, gdn_conv_w.shape[-1]).reshape(gdn_conv_w.shape),
        "gdn_a_log": t_a_log[:, :heads].reshape(gdn_a_log.shape),
        "gdn_dt_bias": t_dt_bias[:, :heads].reshape(gdn_dt_bias.shape),
        "gdn_norm_w": t_gnw.reshape(gdn_norm_w.shape),
        "w_gdn_proj": g_w_gdn_proj,
        "sc_conv_w": my_cols(t_sc_w, sc_conv_w.shape[-1]).reshape(sc_conv_w.shape),
        "w_sc_out": g_w_sc_out,
        "w_o": g_w_o,
        "norm2_w": t_n2w.reshape(norm2_w.shape),
        "w_ffn_in": g_w_ffn_in,
        "w_ffn_out": g_w_ffn_out,
        "w_ada_f": _mm(c_act, my_cols(dmodf_all, n_adaf), ta=True, name="g_adaf").reshape(w_ada_f.shape),
        "b_ada_f": _row_sum(dmodf_all, "g_adaf_bias").reshape(b_ada_f.shape),
        "normf_w": t_nfw.reshape(normf_w.shape),
    }
    weights = dict(w_ada=w_ada, b_ada=b_ada, norm1_w=norm1_w, w_in=w_in, gdn_conv_w=gdn_conv_w, gdn_a_log=gdn_a_log,
                   gdn_dt_bias=gdn_dt_bias, gdn_norm_w=gdn_norm_w, w_gdn_proj=w_gdn_proj, sc_conv_w=sc_conv_w,
                   w_sc_out=w_sc_out, w_o=w_o, norm2_w=norm2_w, w_ffn_in=w_ffn_in, w_ffn_out=w_ffn_out, w_ada_f=w_ada_f,
                   b_ada_f=b_ada_f, normf_w=normf_w)
    m_in = [m_w_ada, m_b_ada, m_norm1_w, m_w_in, m_gdn_conv_w, m_gdn_a_log, m_gdn_dt_bias, m_gdn_norm_w, m_w_gdn_proj,
            m_sc_conv_w, m_w_sc_out, m_w_o, m_norm2_w, m_w_ffn_in, m_w_ffn_out, m_w_ada_f, m_b_ada_f, m_normf_w]
    v_in = [v_w_ada, v_b_ada, v_norm1_w, v_w_in, v_gdn_conv_w, v_gdn_a_log, v_gdn_dt_bias, v_gdn_norm_w, v_w_gdn_proj,
            v_sc_conv_w, v_w_sc_out, v_w_o, v_norm2_w, v_w_ffn_in, v_w_ffn_out, v_w_ada_f, v_b_ada_f, v_normf_w]
    deltas, new_m, new_v = [], [], []
    grads_t = {"w_in": gt_w_in, "w_ffn_in": gt_w_ffn_in}
    for (wname, wt), mt, vt in zip(weights.items(), m_in, v_in):
        if wname in grads_t:
            back = lambda a, wt=wt: a.T.reshape(wt.shape)
            dl, mn, vn = (back(a) for a in _adamw(wt[0].T, grads_t[wname], mt[0].T, vt[0].T, "adamw_" + wname))
        else:
            dl, mn, vn = _adamw(wt, grads[wname], mt, vt, "adamw_" + wname)
        deltas.append(dl)
        new_m.append(mn)
        new_v.append(vn)
    loss = t_loss[0, 0]
    return (loss, grad_x, *[grads[k] for k in weights], *deltas, *new_m, *new_v)
```

```python
import functools

import jax
import jax.numpy as jnp
from jax import lax
from jax.experimental import pallas as pl
from jax.experimental.pallas import tpu as pltpu
from jax.experimental.pallas import tpu_sc as plsc

F32 = jnp.float32
MXU_DTYPE = jnp.bfloat16
NDEV = 8
CHUNK = 64
HEAD = 128
LANE = 128
EPS = 1e-6
ADAM_LR, ADAM_B1, ADAM_B2, ADAM_EPS, ADAM_WD, ADAM_STEP = 0.001, 0.9, 0.999, 1e-08, 0.01, 10
VMEM_LIMIT = 48 * 1024 * 1024
MESH_IDS = pl.DeviceIdType.MESH
HIGHEST = lax.Precision.HIGHEST


def _tile(n, cands=(512, 256, 128)):
    for c in cands:
        if n % c == 0:
            return c
    return n


def _cparams(*sem):
    return pltpu.CompilerParams(dimension_semantics=sem, vmem_limit_bytes=VMEM_LIMIT)


def _mm(a, b, *, ta=False, tb=False, add=None, out_dtype=F32, name, b_rows=None, out_rows=None, row_off=0, into=None,
        a_index=None):
    m, k = (a.shape[-1], a.shape[-2]) if ta else a.shape[-2:]
    b_shape = b.shape if b_rows is None else (b_rows[1], b.shape[1])
    n = b_shape[0] if tb else b_shape[1]
    assert k == (b_shape[1] if tb else b_shape[0])
    if ta:
        tm, tn = _tile(m), n if n <= 1024 else _tile(n)
        tk = k if k <= 4096 else _tile(k, (4096, 2048, 1024, 512))
        if tm * tk > 1024 * 2048:
            tk = _tile(k, (2048, 1024, 512))
    else:
        tk = k if k <= 1024 else _tile(k, (1024, 512))
        tn = _tile(n, (1024 if tk <= 1024 else 512, 512, 256, 128))
        tm = _tile(m, (2048 if (tn <= 512 and tk <= 1024) else 1024, 1024, 512, 256, 128))
    nk = k // tk
    dims = (((0 if ta else 1,), (1 if tb else 0,)), ((), ()))
    has_add = add is not None

    def body(*refs):
        a_ref, b_ref = refs[0], refs[1]
        add_ref = refs[2] if has_add else None
        o_ref = refs[2 + has_add + (into is not None)]
        part = lax.dot_general(a_ref[...].astype(MXU_DTYPE), b_ref[...].astype(MXU_DTYPE), dims,
                               preferred_element_type=F32)

        def finish(acc):
            if has_add:
                acc = acc + add_ref[...]
            o_ref[...] = acc.astype(o_ref.dtype)

        if nk == 1:
            finish(part)
        else:
            acc_ref = refs[-1]
            kk = pl.program_id(2)

            @pl.when(kk == 0)
            def _():
                acc_ref[...] = part

            @pl.when(kk > 0)
            def _():
                acc_ref[...] += part

            @pl.when(kk == nk - 1)
            def _():
                finish(acc_ref[...])

    a_blk, a_at = ((tk, tm), lambda i, j, kk: (kk, i)) if ta else ((tm, tk), lambda i, j, kk: (i, kk))
    a_spec = (pl.BlockSpec(a_blk, a_at) if a_index is None else
              pl.BlockSpec((None,) + a_blk, lambda i, j, kk: (a_index,) + a_at(i, j, kk)))
    if b_rows is None:
        b_spec = pl.BlockSpec((tn, tk), lambda i, j, kk: (j, kk)) if tb else pl.BlockSpec((tk, tn), lambda i, j, kk: (kk, j))
    else:
        at = lambda t: pl.multiple_of(b_rows[0] + t, ROW_ALIGN)
        b_spec = (pl.BlockSpec((pl.Element(tn), pl.Element(tk)), lambda i, j, kk: (at(j * tn), kk * tk)) if tb else
                  pl.BlockSpec((pl.Element(tk), pl.Element(tn)), lambda i, j, kk: (at(kk * tk), j * tn)))
    add_spec = pl.BlockSpec((tm, tn), lambda i, j, kk: (i, j))
    assert row_off % tm == 0
    o_spec = pl.BlockSpec((tm, tn), lambda i, j, kk: (i + row_off // tm, j))
    in_specs = [a_spec, b_spec] + ([add_spec] if has_add else []) + ([pl.BlockSpec(memory_space=pl.ANY)] if into is not None else [])
    args = [a, b] + ([add] if has_add else []) + ([into] if into is not None else [])
    return pl.pallas_call(
        body, name=name, grid=(m // tm, n // tn, nk), in_specs=in_specs, out_specs=o_spec,
        out_shape=jax.ShapeDtypeStruct((out_rows or m, n), out_dtype),
        scratch_shapes=[pltpu.VMEM((tm, tn), F32)] if nk > 1 else [],
        input_output_aliases={len(args) - 1: 0} if into is not None else {},
        compiler_params=_cparams("parallel", "parallel", "arbitrary"),
    )(*args)


def _mm_chain(parts, b, row_of_tile, *, add, name, tk=1024, tm=1024):
    m, n = parts[0].shape[-2], b.shape[1]
    tm = min(tm, m)
    tiles = [p.shape[0] if p.ndim == 3 else p.shape[1] // tk for p in parts]
    first = [sum(tiles[:s]) for s in range(len(parts))]
    nk = sum(tiles)

    def body(*refs):
        a_refs, b_ref, add_ref, o_ref, acc_ref = refs[:len(parts)], *refs[len(parts):]
        kk = pl.program_id(1)

        @pl.when(kk == 0)
        def _():
            acc_ref[...] = add_ref[...]

        for a_ref, lo, cnt in zip(a_refs, first, tiles):
            @pl.when(jnp.logical_and(kk >= lo, kk < lo + cnt))
            def _(a_ref=a_ref):
                acc_ref[...] += lax.dot_general(a_ref[...].astype(MXU_DTYPE), b_ref[...].astype(MXU_DTYPE),
                                                (((1,), (0,)), ((), ())), preferred_element_type=F32)

        @pl.when(kk == nk - 1)
        def _():
            o_ref[...] = acc_ref[...]

    tile_of = lambda kk, lo, cnt: jnp.clip(kk - lo, 0, cnt - 1)
    a_specs = [pl.BlockSpec((None, tm, tk), functools.partial(lambda i, kk, lo, cnt: (tile_of(kk, lo, cnt), i, 0), lo=lo, cnt=cnt))
               if p.ndim == 3 else
               pl.BlockSpec((tm, tk), functools.partial(lambda i, kk, lo, cnt: (i, tile_of(kk, lo, cnt)), lo=lo, cnt=cnt))
               for p, lo, cnt in zip(parts, first, tiles)]
    b_spec = pl.BlockSpec((pl.Element(tk), pl.Element(n)), lambda i, kk: (pl.multiple_of(row_of_tile(kk), ROW_ALIGN), 0))
    o_spec = pl.BlockSpec((tm, n), lambda i, kk: (i, 0))
    return pl.pallas_call(
        body, name=name, grid=(m // tm, nk), in_specs=a_specs + [b_spec, o_spec], out_specs=o_spec,
        out_shape=jax.ShapeDtypeStruct((m, n), F32), scratch_shapes=[pltpu.VMEM((tm, n), F32)],
        compiler_params=_cparams("parallel", "arbitrary"),
    )(*parts, b, add)


def _swiglu_tiles(m, half):
    tn = _tile(half, (512, 256, 128))
    return _tile(m, (2048 if tn <= 256 else 1024, 1024, 512, 256, 128)), tn


def _ffn_in_swiglu(h, wt, half, name):
    m, k = h.shape
    tm, tn = _swiglu_tiles(m, half)
    nj = half // tn
    dims = (((1,), (1,)), ((), ()))

    def body(h_ref, wa_ref, wb_ref, act_ref, a_ref, b_ref):
        lhs = h_ref[...].astype(MXU_DTYPE)
        a = lax.dot_general(lhs, wa_ref[...].astype(MXU_DTYPE), dims, preferred_element_type=F32)
        b = lax.dot_general(lhs, wb_ref[...].astype(MXU_DTYPE), dims, preferred_element_type=F32)
        act_ref[...] = (_silu(a) * b).astype(act_ref.dtype)
        a_ref[...] = a.astype(a_ref.dtype)
        b_ref[...] = b.astype(b_ref.dtype)

    out = jax.ShapeDtypeStruct((m, half), MXU_DTYPE)
    oblk = pl.BlockSpec((tm, tn), lambda i, j: (i, j))
    return pl.pallas_call(
        body, name=name, grid=(m // tm, nj),
        in_specs=[pl.BlockSpec((tm, k), lambda i, j: (i, 0)), pl.BlockSpec((tn, k), lambda i, j: (j, 0)),
                  pl.BlockSpec((tn, k), lambda i, j: (j + nj, 0))],
        out_specs=[oblk, oblk, oblk], out_shape=[out, out, out], compiler_params=_cparams("parallel", "parallel"),
    )(h, wt, wt)


def _ffn_out_bwd_swiglu(dff, w, a, b, name):
    m, k = dff.shape
    half = w.shape[0]
    tm, tn = _swiglu_tiles(m, half)

    def body(d_ref, w_ref, a_ref, b_ref, da_ref, db_ref):
        dact = lax.dot_general(d_ref[...].astype(MXU_DTYPE), w_ref[...].astype(MXU_DTYPE), (((1,), (1,)), ((), ())),
                               preferred_element_type=F32)
        av, bv = a_ref[...].astype(F32), b_ref[...].astype(F32)
        sig = jax.nn.sigmoid(av)
        da_ref[...] = (dact * bv * (sig * (1.0 + av * (1.0 - sig)))).astype(da_ref.dtype)
        db_ref[...] = (dact * (av * sig)).astype(db_ref.dtype)

    out = jax.ShapeDtypeStruct((m, half), MXU_DTYPE)
    oblk = pl.BlockSpec((tm, tn), lambda i, j: (i, j))
    return pl.pallas_call(
        body, name=name, grid=(m // tm, half // tn),
        in_specs=[pl.BlockSpec((tm, k), lambda i, j: (i, 0)), pl.BlockSpec((tn, k), lambda i, j: (j, 0)), oblk, oblk],
        out_specs=[oblk, oblk], out_shape=[out, out], compiler_params=_cparams("parallel", "parallel"),
    )(dff, w, a, b)


def _with_off(xs):
    return [x if isinstance(x, tuple) else (x, 0) for x in xs]


def _spec(kind, arr, off, ts, wb):
    w = arr.shape[-1] if wb is None else wb
    col = (lambda j: 0) if wb is None else functools.partial(lambda j, o: o + j, o=off)
    if kind == "tok":
        return pl.BlockSpec((None, ts, w), lambda j, b, i: (b, i, col(j)))
    if kind == "bat":
        return pl.BlockSpec((None, 1, w), lambda j, b, i: (b, 0, col(j)))
    if off is None:
        return pl.BlockSpec(arr.shape, lambda j, b, i: (0, 0))
    return pl.BlockSpec((arr.shape[0], w), lambda j, b, i: (0, col(j)))


class _Product:
    def __init__(self, a, b, *, tb=False, b_rows=None, add=None):
        self.a, self.b, self.tb, self.b_rows, self.add = a, b, tb, b_rows, add
        rows = b.shape[0] if b_rows is None else b_rows[1]
        self.shape = a.shape[:2] + (rows if tb else b.shape[1],)

    def inputs(self, ts):
        a_spec = pl.BlockSpec((None, ts, self.a.shape[2]), lambda j, b, i: (b, i, 0))
        if self.b_rows is None:
            b_spec = pl.BlockSpec(self.b.shape, lambda j, b, i: (0, 0))
        else:
            start, count = self.b_rows
            b_spec = pl.BlockSpec((pl.Element(count), pl.Element(self.b.shape[1])), lambda j, b, i: (start, 0))
        if isinstance(self.add, _Product):
            extra = self.add.inputs(ts)
        else:
            extra = [] if self.add is None else [(self.add, pl.BlockSpec((None, ts, self.shape[2]), lambda j, b, i: (b, i, 0)))]
        return [(self.a, a_spec), (self.b, b_spec)] + extra

    def value(self, refs):
        dims = (((1,), (1 if self.tb else 0,)), ((), ()))
        val = lax.dot_general(refs[0][...].astype(MXU_DTYPE), refs[1][...].astype(MXU_DTYPE), dims, preferred_element_type=F32)
        if isinstance(self.add, _Product):
            return val + self.add.value(refs[2:])
        return val if self.add is None else val + refs[2][...].astype(F32)


def _inputs(groups, kinds, ts, wb):
    loaded = [(a, _spec(kind, a, o, ts, wb)) for g, kind in zip(groups, kinds) for a, o in g if not isinstance(a, _Product)]
    made = [pair for g in groups for a, _ in g if isinstance(a, _Product) for pair in a.inputs(ts)]
    return [a for a, _ in loaded + made], [sp for _, sp in loaded + made]


def _values(refs, groups):
    n_loaded = sum(1 for g in groups for a, _ in g if not isinstance(a, _Product))
    loaded, pos, out = iter(refs[:n_loaded]), n_loaded, []
    for g in groups:
        vals = []
        for a, _ in g:
            if isinstance(a, _Product):
                k = len(a.inputs(1))
                vals.append(a.value(refs[pos:pos + k]))
                pos += k
            else:
                vals.append(next(loaded)[...].astype(F32))
        out.append(vals)
    return out, pos


def _tok_fwd(fn, toks, bats, pars, outs, *, name, ts, wb=None, cols=1):
    groups = [_with_off(toks), _with_off(bats), _with_off(pars)]
    bl, s, _ = groups[0][0][0].shape
    ts = min(ts, s)
    args, in_specs = _inputs(groups, ("tok", "bat", "par"), ts, wb)

    def body(*refs):
        vals, n_in = _values(refs, groups)
        res = fn(*[v for g in vals for v in g])
        for r, val in zip(refs[n_in:], res):
            r[...] = val.astype(r.dtype)

    out_specs = [pl.BlockSpec((None, ts, w if wb is None else wb), lambda j, b, i: (b, i, j)) for w, _ in outs]
    return pl.pallas_call(
        body, name=name, grid=(cols, bl, s // ts), in_specs=in_specs,
        out_specs=out_specs, out_shape=[jax.ShapeDtypeStruct((bl, s, w), dt) for w, dt in outs],
        compiler_params=_cparams("parallel", "parallel", "parallel"),
    )(*args)


def _accumulate(ref, val, first):
    @pl.when(first)
    def _():
        ref[...] = val

    @pl.when(jnp.logical_not(first))
    def _():
        ref[...] += val


def _tok_bwd(fn, toks, bats, pars, cots, need, *, name, ts, wb=None, cols=1, tok_dtype=F32, loss=False, after=()):
    toks, bats, pars, cots = _with_off(toks), _with_off(bats), _with_off(pars), _with_off(cots)
    groups = [toks, bats, pars, cots]
    bl, s, _ = toks[0][0].shape
    ts = min(ts, s)
    nt, nb, npar = len(toks), len(bats), len(pars)
    args, in_specs = _inputs(groups, ("tok", "bat", "par", "tok"), ts, wb)
    args, in_specs = args + list(after), in_specs + [pl.BlockSpec(memory_space=pl.ANY)] * len(after)

    def body(*refs):
        j, b, i = pl.program_id(0), pl.program_id(1), pl.program_id(2)
        (tok_vals, bat_vals, par_vals, cot_vals), o = _values(refs, groups)
        o += len(after)
        outs, vjp = jax.vjp(fn, *tok_vals, *bat_vals, *par_vals)
        if loss:
            ct = (jnp.ones_like(outs[0]),)
            tot = jnp.broadcast_to(jnp.sum(outs[0], keepdims=True), (1, LANE))
            _accumulate(refs[o], tot, jnp.logical_and(b == 0, i == 0))
            o += 1
        else:
            ct = tuple(cot_vals)
        grads = vjp(ct)
        for t in range(nt):
            if need[t]:
                refs[o][...] = grads[t].astype(refs[o].dtype)
                o += 1
        for t in range(nb):
            _accumulate(refs[o], grads[nt + t], i == 0)
            o += 1
        for t in range(npar):
            first = jnp.logical_and(b == 0, i == 0)
            if pars[t][1] is None:
                first = jnp.logical_and(first, j == 0)
            _accumulate(refs[o], grads[nt + nb + t], first)
            o += 1

    full = lambda arr: arr.shape[-1] if wb is None else wb * cols
    blk = lambda arr: arr.shape[-1] if wb is None else wb
    out_specs, out_shape = [], []
    if loss:
        out_specs.append(pl.BlockSpec((1, LANE), lambda j, b, i: (0, 0)))
        out_shape.append(jax.ShapeDtypeStruct((1, LANE), F32))
    for t in range(nt):
        if need[t]:
            out_specs.append(pl.BlockSpec((None, ts, blk(toks[t][0])), lambda j, b, i: (b, i, j)))
            dt = tok_dtype[t] if isinstance(tok_dtype, (list, tuple)) else tok_dtype
            out_shape.append(jax.ShapeDtypeStruct((bl, s, full(toks[t][0])), dt))
    for arr, _ in bats:
        out_specs.append(pl.BlockSpec((None, 1, blk(arr)), lambda j, b, i: (b, 0, j)))
        out_shape.append(jax.ShapeDtypeStruct((bl, 1, full(arr)), F32))
    for arr, off in pars:
        if off is None:
            out_specs.append(pl.BlockSpec(arr.shape, lambda j, b, i: (0, 0)))
            out_shape.append(jax.ShapeDtypeStruct(arr.shape, F32))
        else:
            out_specs.append(pl.BlockSpec((arr.shape[0], blk(arr)), lambda j, b, i: (0, j)))
            out_shape.append(jax.ShapeDtypeStruct((arr.shape[0], full(arr)), F32))
    res = list(pl.pallas_call(
        body, name=name, grid=(cols, bl, s // ts), in_specs=in_specs,
        out_specs=out_specs, out_shape=out_shape, compiler_params=_cparams("arbitrary", "arbitrary", "arbitrary"),
    )(*args))
    tot = res.pop(0) if loss else None
    dtoks = [res.pop(0) if need[t] else None for t in range(nt)]
    dbats = [res.pop(0) for _ in range(nb)]
    dpars = [res.pop(0) for _ in range(npar)]
    return (tot, dtoks, dbats, dpars) if loss else (dtoks, dbats, dpars)


def _silu(x):
    return x * jax.nn.sigmoid(x)


def _rms(x, w):
    return x * lax.rsqrt(jnp.mean(x * x, axis=-1, keepdims=True) + EPS) * w


def _f_norm_mod(x, shift, scale, w):
    return (_rms(x, w) * (1.0 + scale) + shift,)


def _f_norm_mod_skip(x, shift, scale, w):
    return _rms(x, w) * (1.0 + scale) + shift, x


def _f_res_norm_mod(x, mix, gate, shift, scale, w):
    x2 = x + gate * mix
    return x2, _rms(x2, w) * (1.0 + scale) + shift


def _f_res_norm_mod_keep(x, mix, gate, shift, scale, w):
    return (*_f_res_norm_mod(x, mix, gate, shift, scale, w), mix)


def _f_gates(p, a_log, dt_bias, *, heads):
    z = p + dt_bias
    g = -jnp.exp(a_log) * (jnp.maximum(z, 0.0) + jnp.log1p(jnp.exp(jnp.minimum(z, -z))))
    lane = lax.broadcasted_iota(jnp.int32, p.shape, 1)
    return (jnp.where(lane < heads, g, jax.nn.sigmoid(p)),)


def _f_gdn_out(o, z, w):
    return (_rms(o, w) * _silu(z),)


def _f_merge(ga, gb, ya, yb):
    return (jax.nn.sigmoid(ga) * ya + jax.nn.sigmoid(gb) * yb,)


def _f_merge_keep(ga, gb, ya, yb):
    return (*_f_merge(ga, gb, ya, yb), yb)


def _f_loss(x2, ff, tgt, gate, shift, scale, w):
    y = _rms(x2 + gate * ff, w) * (1.0 + scale) + shift
    return (0.5 * jnp.mean(jnp.square(y - tgt), axis=-1, keepdims=True),)


def _shift_down(x, s):
    if s == 0:
        return x
    row = lax.broadcasted_iota(jnp.int32, x.shape, 0)
    return jnp.where(row >= s, pltpu.roll(x, s, 0), 0.0)


def _shift_up(x, s):
    if s == 0:
        return x
    n = x.shape[0]
    row = lax.broadcasted_iota(jnp.int32, x.shape, 0)
    return jnp.where(row < n - s, pltpu.roll(x, n - s, 0), 0.0)


def _conv(x, w):
    width = w.shape[0]
    acc = w[width - 1:width, :] * x
    for j in range(width - 1):
        acc = acc + w[j:j + 1, :] * _shift_down(x, width - 1 - j)
    return acc


def _conv_bwd(dy, x, w, dw_ref, first):
    width = w.shape[0]
    dx = w[width - 1:width, :] * dy
    for j in range(width - 1):
        dx = dx + w[j:j + 1, :] * _shift_up(dy, width - 1 - j)
    for j in range(width):
        row = jnp.sum(dy * _shift_down(x, width - 1 - j), axis=0, keepdims=True)
        _accumulate(dw_ref.at[j:j + 1, :], row, first)
    return dx


def _qkv_act(xc, is_v, scale):
    a = _silu(xc)
    nrm = a * lax.rsqrt(jnp.sum(a * a, axis=-1, keepdims=True) + EPS) * scale
    return jnp.where(is_v, a, nrm)


def _qkv_act_bwd(xc, dout, is_v, scale):
    sig = jax.nn.sigmoid(xc)
    a = xc * sig
    r = lax.rsqrt(jnp.sum(a * a, axis=-1, keepdims=True) + EPS)
    c1 = r * scale
    da = c1 * dout - a * (c1 * r * r * jnp.sum(dout * a, axis=-1, keepdims=True))
    return jnp.where(is_v, dout, da) * (sig * (1.0 + xc * (1.0 - sig)))


def _qkv_consts(j, heads):
    is_v = j >= 2 * heads
    scale = jnp.where(j < heads, HEAD ** -0.5, 1.0).astype(F32)
    return is_v, scale


def _qkv_fwd(p, w, heads, name):
    bl, s, w3 = p.shape

    def body(p_ref, w_ref, o_ref):
        is_v, scale = _qkv_consts(pl.program_id(0), heads)
        o_ref[...] = _qkv_act(_conv(p_ref[...], w_ref[...]), is_v, scale)

    blk = pl.BlockSpec((None, s, HEAD), lambda j, b: (b, 0, j))
    return pl.pallas_call(
        body, name=name, grid=(w3 // HEAD, bl), in_specs=[blk, pl.BlockSpec((w.shape[0], HEAD), lambda j, b: (0, j))],
        out_specs=blk, out_shape=jax.ShapeDtypeStruct(p.shape, F32), compiler_params=_cparams("parallel", "parallel"),
    )(p, w)


def _qkv_bwd(p, w, dout, heads, name):
    bl, s, w3 = p.shape

    def body(p_ref, w_ref, d_ref, dp_ref, dw_ref):
        is_v, scale = _qkv_consts(pl.program_id(0), heads)
        x, wv = p_ref[...], w_ref[...]
        dxc = _qkv_act_bwd(_conv(x, wv), d_ref[...], is_v, scale)
        dp_ref[...] = _conv_bwd(dxc, x, wv, dw_ref, pl.program_id(1) == 0).astype(dp_ref.dtype)

    blk = pl.BlockSpec((None, s, HEAD), lambda j, b: (b, 0, j))
    wblk = pl.BlockSpec((w.shape[0], HEAD), lambda j, b: (0, j))
    return pl.pallas_call(
        body, name=name, grid=(w3 // HEAD, bl), in_specs=[blk, wblk, blk], out_specs=[blk, wblk],
        out_shape=[jax.ShapeDtypeStruct(p.shape, MXU_DTYPE), jax.ShapeDtypeStruct(w.shape, F32)],
        compiler_params=_cparams("arbitrary", "arbitrary"),
    )(p, w, dout)


def _sc_specs(p, w):
    bl, s, w3 = p.shape
    nblk = w3 // 3 // LANE
    sec = lambda k: pl.BlockSpec((None, s, LANE), functools.partial(lambda j, b, k: (b, 0, k * nblk + j), k=k))
    return nblk, [sec(0), sec(1), sec(2)], pl.BlockSpec((w.shape[0], LANE), lambda j, b: (0, j)), \
        pl.BlockSpec((None, s, LANE), lambda j, b: (b, 0, j))


def _sc_fwd(p, w, name):
    bl, s, w3 = p.shape
    nblk, secs, wblk, oblk = _sc_specs(p, w)

    def body(b_ref, c_ref, x_ref, w_ref, o_ref):
        o_ref[...] = (b_ref[...] * _conv(c_ref[...] * x_ref[...], w_ref[...])).astype(o_ref.dtype)

    return pl.pallas_call(
        body, name=name, grid=(nblk, bl), in_specs=secs + [wblk], out_specs=oblk,
        out_shape=jax.ShapeDtypeStruct((bl, s, w3 // 3), MXU_DTYPE), compiler_params=_cparams("parallel", "parallel"),
    )(p, p, p, w)


def _sc_bwd(p, w, dout, name):
    bl, s, w3 = p.shape
    nblk, secs, wblk, oblk = _sc_specs(p, w)

    def body(b_ref, c_ref, x_ref, w_ref, d_ref, dp_ref, dw_ref):
        gb, gc, xin, wv, d = b_ref[...], c_ref[...], x_ref[...], w_ref[...], d_ref[...]
        u = gc * xin
        dp_ref[0] = (d * _conv(u, wv)).astype(dp_ref.dtype)
        du = _conv_bwd(d * gb, u, wv, dw_ref, pl.program_id(1) == 0)
        dp_ref[1] = (du * xin).astype(dp_ref.dtype)
        dp_ref[2] = (du * gc).astype(dp_ref.dtype)

    return pl.pallas_call(
        body, name=name, grid=(nblk, bl), in_specs=secs + [wblk, oblk],
        out_specs=[pl.BlockSpec((3, None, s, LANE), lambda j, b: (0, b, 0, j)), wblk],
        out_shape=[jax.ShapeDtypeStruct((3, bl, s, w3 // 3), MXU_DTYPE), jax.ShapeDtypeStruct(w.shape, F32)],
        compiler_params=_cparams("arbitrary", "arbitrary"),
    )(p, p, p, w, dout)


def _bdot(a, b, ca, cb):
    return lax.dot_general(a.astype(MXU_DTYPE), b.astype(MXU_DTYPE), (((ca,), (cb,)), ((), ())),
                           preferred_element_type=F32)


def _hdot(a, b):
    return lax.dot_general(a, b, (((1,), (0,)), ((), ())), precision=HIGHEST, preferred_element_type=F32)


def _lane_col(x, idx):
    lane = lax.broadcasted_iota(jnp.int32, x.shape, 1)
    return jnp.sum(jnp.where(lane == idx, x, 0.0), axis=1, keepdims=True)


def _chunk_masks():
    r = lax.broadcasted_iota(jnp.int32, (CHUNK, CHUNK), 0)
    c = lax.broadcasted_iota(jnp.int32, (CHUNK, CHUNK), 1)
    return r == c, r >= c, r > c


def _dot3(a, b):
    ah, bh = a.astype(MXU_DTYPE), b.astype(MXU_DTYPE)
    al, bl = (a - ah.astype(F32)).astype(MXU_DTYPE), (b - bh.astype(F32)).astype(MXU_DTYPE)
    dot = lambda x, y: lax.dot_general(x, y, (((1,), (0,)), ((), ())), preferred_element_type=F32)
    return dot(ah, bh) + (dot(ah, bl) + dot(al, bh))


def _tri_inv_steps(low, eye):
    x = -low
    p = jnp.where(eye, 1.0, 0.0) + x
    span = 2
    while span < CHUNK:
        x = _dot3(x, x)
        yield
        p = p + _dot3(p, x)
        yield
        span *= 2
    return p


def _round_robin(gens):
    out, live = [None] * len(gens), list(range(len(gens)))
    while live:
        still = []
        for i in live:
            try:
                next(gens[i])
                still.append(i)
            except StopIteration as stop:
                out[i] = stop.value
        live = still
    return out


def _gdn_pre(q, k, v, gc, beta, masks):
    eye, causal, strict = masks
    gc_row = jnp.sum(jnp.where(eye, gc, 0.0), axis=0, keepdims=True)
    decay = jnp.where(causal, jnp.exp(jnp.where(causal, gc - gc_row, 0.0)), 0.0)
    eg = jnp.exp(gc)
    gl = gc[CHUNK - 1:CHUNK, :]
    kb, vb = k * beta, v * beta
    both = _bdot(jnp.concatenate([kb, q], axis=0), k, 1, 1)
    low = jnp.where(strict, both[:CHUNK] * decay, 0.0)
    qk = jnp.where(causal, both[CHUNK:] * decay, 0.0)
    rest = jnp.exp(gl - gc)
    return dict(decay=decay, eg=eg, gl=gl, kb=kb, vb=vb, kbe=kb * eg, low=low, qk=qk, qg=q * eg, rest=rest, kdec=k * rest)


GROUP = 4


def _gdn_specs(qkv, gbeta, heads, rev):
    bl, s, w3 = qkv.shape
    d, n = w3 // 3, s // CHUNK
    group = GROUP if n % GROUP == 0 else 1
    steps = n // group
    at = (lambda c: steps - 1 - c) if rev else (lambda c: c)
    assert d == heads * HEAD
    rows = group * CHUNK
    sec = pl.BlockSpec((None, rows, w3), lambda b, c: (b, at(c), 0))
    gspec = pl.BlockSpec((None, rows, LANE), lambda b, c: (b, at(c), 0))
    ospec = pl.BlockSpec((None, rows, d), lambda b, c: (b, at(c), 0))
    sspec = pl.BlockSpec((None, group, heads, HEAD, HEAD), lambda b, c: (b, at(c), 0, 0, 0))
    tspec = pl.BlockSpec((None, group, heads, CHUNK, CHUNK), lambda b, c: (b, at(c), 0, 0, 0))
    return bl, s, d, n, group, sec, gspec, ospec, sspec, tspec


def _gdn_fwd(qkv, gbeta, heads, name):
    bl, s, d, n, group, sec, gspec, ospec, sspec, tspec = _gdn_specs(qkv, gbeta, heads, False)
    rows = lambda sub: slice(sub * CHUNK, (sub + 1) * CHUNK)
    pairs = [(h, sub) for h in range(heads) for sub in range(group)]

    def body(x_ref, g_ref, o_ref, s_ref, t_ref, st_ref):
        @pl.when(pl.program_id(1) == 0)
        def _():
            st_ref[...] = jnp.zeros_like(st_ref)

        masks = _chunk_masks()
        eye, causal, _ = masks
        gblks = [g_ref[rows(sub), :] for sub in range(group)]
        gcs = [_hdot(jnp.where(causal, 1.0, 0.0), gb) for gb in gblks]
        st_all = st_ref[...]

        def free(h, sub):
            q, k, v = (x_ref[rows(sub), sec * d + h * HEAD:sec * d + (h + 1) * HEAD] for sec in range(3))
            pre = _gdn_pre(q, k, v, _lane_col(gcs[sub], h), _lane_col(gblks[sub], heads + h), masks)
            yield
            t = yield from _tri_inv_steps(pre["low"], eye)
            uw = _bdot(t, jnp.concatenate([pre["vb"], pre["kbe"]], axis=1), 1, 0)
            return pre, t, uw[:, :HEAD], uw[:, HEAD:]

        pieces = dict(zip(pairs, _round_robin([free(h, sub) for h, sub in pairs])))

        def carry(h):
            st, outs, starts = st_all[h], [], []
            for sub in range(group):
                pre, _, u, w = pieces[h, sub]
                starts.append(st)
                vnew = u - _bdot(w, st, 1, 0)
                yield
                outs.append(_bdot(pre["qg"], st, 1, 0) + _bdot(pre["qk"], vnew, 1, 0))
                st = st * jnp.exp(pre["gl"]) + _bdot(pre["kdec"], vnew, 0, 0)
                yield
            return outs, starts, st

        carried = _round_robin([carry(h) for h in range(heads)])
        per_sub = lambda pick: [[pick(h, sub) for h in range(heads)] for sub in range(group)]
        o_ref[...] = jnp.concatenate([jnp.concatenate(r, axis=1) for r in per_sub(lambda h, sub: carried[h][0][sub])], axis=0)
        s_ref[...] = jnp.stack([jnp.stack(r) for r in per_sub(lambda h, sub: carried[h][1][sub])])
        t_ref[...] = jnp.stack([jnp.stack(r) for r in per_sub(lambda h, sub: pieces[h, sub][1])])
        st_ref[...] = jnp.stack([carried[h][2] for h in range(heads)])

    return pl.pallas_call(
        body, name=name, grid=(bl, n // group), in_specs=[sec, gspec], out_specs=[ospec, sspec, tspec],
        out_shape=[jax.ShapeDtypeStruct((bl, s, d), F32), jax.ShapeDtypeStruct((bl, n, heads, HEAD, HEAD), F32),
                   jax.ShapeDtypeStruct((bl, n, heads, CHUNK, CHUNK), F32)],
        scratch_shapes=[pltpu.VMEM((heads, HEAD, HEAD), F32)], compiler_params=_cparams("parallel", "arbitrary"),
    )(qkv, gbeta)


def _gdn_bwd(qkv, gbeta, dout, s_all, t_all, heads, name):
    bl, s, d, n, group, sec, gspec, ospec, sspec, tspec = _gdn_specs(qkv, gbeta, heads, True)
    rows = lambda sub: slice(sub * CHUNK, (sub + 1) * CHUNK)
    pairs = [(h, sub) for h in range(heads) for sub in range(group)]
    stack, side = functools.partial(jnp.concatenate, axis=0), functools.partial(jnp.concatenate, axis=1)

    def body(x_ref, g_ref, do_ref, s_ref, t_ref, dx_ref, dg_ref, ds_ref):
        @pl.when(pl.program_id(1) == 0)
        def _():
            ds_ref[...] = jnp.zeros_like(ds_ref)

        masks = _chunk_masks()
        eye, causal, strict = masks
        gblks = [g_ref[rows(sub), :] for sub in range(group)]
        gcs = [_hdot(jnp.where(causal, 1.0, 0.0), gb) for gb in gblks]
        lane = lax.broadcasted_iota(jnp.int32, (CHUNK, LANE), 1)
        last_row = lax.broadcasted_iota(jnp.int32, (CHUNK, 1), 0) == CHUNK - 1
        rowsum = lambda a: jnp.sum(a, axis=1, keepdims=True)
        st_all, t_all_, ds_all = s_ref[...], t_ref[...], ds_ref[...]

        def free(h, sub):
            q, k, v = (x_ref[rows(sub), sec * d + h * HEAD:sec * d + (h + 1) * HEAD] for sec in range(3))
            do = do_ref[rows(sub), h * HEAD:(h + 1) * HEAD]
            beta = _lane_col(gblks[sub], heads + h)
            st, t = st_all[sub, h], t_all_[sub, h]
            pre = _gdn_pre(q, k, v, _lane_col(gcs[sub], h), beta, masks)
            yield
            uw = _bdot(t, side([pre["vb"], pre["kbe"]]), 1, 0)
            u, w = uw[:, :HEAD], uw[:, HEAD:]
            yield
            vnew = u - _bdot(w, st, 1, 0)
            yield
            dqk = jnp.where(causal, _bdot(do, vnew, 1, 1), 0.0)
            dqg = _bdot(do, st, 1, 1)
            return dict(q=q, k=k, v=v, do=do, beta=beta, st=st, t=t, pre=pre, w=w, vnew=vnew, dqk=dqk, dqg=dqg)

        pieces = dict(zip(pairs, _round_robin([free(h, sub) for h, sub in pairs])))

        def carry(h):
            dsn, outs = ds_all[h], {}
            for sub in reversed(range(group)):
                pc = pieces[h, sub]
                pre, st, do = pc["pre"], pc["st"], pc["do"]
                egl = jnp.exp(pre["gl"])
                dkdec = _bdot(pc["vnew"], dsn, 1, 1)
                dvnew = _bdot(pre["kdec"], dsn, 1, 0) + _bdot(pre["qk"], do, 0, 0)
                dgl = jnp.sum(dsn * st, keepdims=True) * egl
                yield
                dw = -_bdot(dvnew, st, 1, 1)
                dsn = dsn * egl + _bdot(stack([pre["qg"], -pc["w"]]), stack([do, dvnew]), 0, 0)
                outs[sub] = (dkdec, dvnew, dgl, dw)
                yield
            return outs, dsn

        carried = _round_robin([carry(h) for h in range(heads)])

        def rest(h, sub):
            pc = pieces[h, sub]
            dkdec, dvnew, dgl, dw = carried[h][0][sub]
            q, k, v, beta, t, pre, dqk, dqg = (pc[x] for x in ("q", "k", "v", "beta", "t", "pre", "dqk", "dqg"))
            decay, eg, kb, vb, kbe, low, qk, qg, kdec = (pre[x] for x in ("decay", "eg", "kb", "vb", "kbe", "low", "qk", "qg", "kdec"))
            dt = _bdot(side([dvnew, dw]), side([vb, kbe]), 1, 1)
            by_t = _bdot(t, side([dvnew, dw]), 0, 0)
            dvb, dkbe = by_t[:, :HEAD], by_t[:, HEAD:]
            yield
            inner = _bdot(dt, t, 1, 1)
            yield
            dlow = -jnp.where(strict, _bdot(t, inner, 0, 0), 0.0)
            da, db = dlow * decay, dqk * decay
            yield
            m = dlow * low + dqk * qk
            kdk = dkdec * kdec
            col_of_m = jnp.sum(jnp.where(eye, jnp.sum(m, axis=0, keepdims=True), 0.0), axis=1, keepdims=True)
            dgc = rowsum(m) - col_of_m + rowsum(dqg * qg) + rowsum(dkbe * kbe) - rowsum(kdk)
            dgc = dgc + jnp.where(last_row, dgl + jnp.sum(kdk, keepdims=True), 0.0)
            by_k = _bdot(stack([da, db]), k, 1, 0)
            dkb = by_k[:CHUNK] + dkbe * eg
            yield
            dk = _bdot(stack([da, db]), stack([kb, q]), 0, 0) + dkdec * pre["rest"] + dkb * beta
            dq = by_k[CHUNK:] + dqg * eg
            dbeta = rowsum(dkb * k) + rowsum(dvb * v)
            return dq, dk, dvb * beta, jnp.where(lane == h, dgc, 0.0) + jnp.where(lane == heads + h, dbeta, 0.0)

        done = dict(zip(pairs, _round_robin([rest(h, sub) for h, sub in pairs])))
        dx_ref[...] = stack([side([done[h, sub][i] for i in range(3) for h in range(heads)]) for sub in range(group)])
        ds_ref[...] = jnp.stack([carried[h][1] for h in range(heads)])
        upper = jnp.where(jnp.logical_or(eye, jnp.logical_not(causal)), 1.0, 0.0)
        dgs = []
        for sub in range(group):
            dgb = done[0, sub][3]
            for h in range(1, heads):
                dgb = dgb + done[h, sub][3]
            dgs.append(jnp.where(lane < heads, _hdot(upper, dgb), dgb))
        dg_ref[...] = stack(dgs)

    return pl.pallas_call(
        body, name=name, grid=(bl, n // group), in_specs=[sec, gspec, ospec, sspec, tspec], out_specs=[sec, gspec],
        out_shape=[jax.ShapeDtypeStruct(qkv.shape, F32), jax.ShapeDtypeStruct((bl, s, LANE), F32)],
        scratch_shapes=[pltpu.VMEM((heads, HEAD, HEAD), F32)], compiler_params=_cparams("parallel", "arbitrary"),
    )(qkv, gbeta, dout, s_all, t_all)


def _position():
    return lax.axis_index("x"), lax.axis_index("y"), lax.axis_index("c")


def _all_gather(x, *, name):
    space = pltpu.VMEM

    def body(x_ref, out_ref, send_sems, recv_sems, local_sem):
        ax, ay, ac = _position()
        me, sibling = (ax, ay, ac), (ax, ay, 1 - ac)
        chips = [(1 - ax, ay), (ax, 1 - ay), (1 - ax, 1 - ay)]

        def slot(px, py, pc):
            return out_ref.at[4 * px + 2 * py + pc]

        def copy(k, block, to, src=None):
            return pltpu.make_async_remote_copy(
                src_ref=slot(*block) if src is None else src, dst_ref=slot(*block), send_sem=send_sems.at[k],
                recv_sem=recv_sems.at[k], device_id=to, device_id_type=MESH_IDS)

        mine = pltpu.make_async_copy(x_ref, slot(*me), local_sem)
        mine.start()
        first = [copy(0, me, sibling, src=x_ref)] + [copy(1 + j, me, (*chip, ac), src=x_ref) for j, chip in enumerate(chips)]
        for cp in first:
            cp.start()
        passed = [copy(4 + j, (*chip, ac), sibling) for j, chip in enumerate(chips)]
        for j, chip in enumerate(chips):
            copy(1 + j, (*chip, ac), me).wait_recv()
            passed[j].start()
        copy(0, sibling, me).wait_recv()
        for j, chip in enumerate(chips):
            copy(4 + j, (*chip, 1 - ac), me).wait_recv()
        for cp in first + passed:
            cp.wait_send()
        mine.wait()

    return pl.pallas_call(
        body, name=name, out_shape=jax.ShapeDtypeStruct((NDEV,) + x.shape, x.dtype),
        in_specs=[pl.BlockSpec(memory_space=space)], out_specs=pl.BlockSpec(memory_space=space),
        scratch_shapes=[pltpu.SemaphoreType.DMA((7,)), pltpu.SemaphoreType.DMA((7,)), pltpu.SemaphoreType.DMA],
    )(x)


class _Rider:
    def __init__(self, arrays, out_shapes, sems, hooks):
        self.arrays, self.out_shapes, self.sems, self.hooks = arrays, out_shapes, sems, hooks


def _gather_rider(xs):
    n = len(xs)

    def hooks(x_refs, out_refs, send_sems, recv_sems):
        ax, ay, ac = _position()
        me, sibling = (ax, ay, ac), (ax, ay, 1 - ac)
        chips = [(1 - ax, ay), (ax, 1 - ay), (1 - ax, 1 - ay)]

        def copies(k, block, to, own=False):
            out = []
            for i in range(n):
                slot = out_refs[i].at[4 * block[0] + 2 * block[1] + block[2]]
                out.append(pltpu.make_async_remote_copy(
                    src_ref=x_refs[i] if own else slot, dst_ref=slot, send_sem=send_sems.at[k, i], recv_sem=recv_sems.at[k, i],
                    device_id=to, device_id_type=MESH_IDS))
            return out

        def first():
            for cp in copies(0, me, sibling, own=True):
                cp.start()
            for j, chip in enumerate(chips):
                for cp in copies(1 + j, me, (*chip, ac), own=True):
                    cp.start()

        def mid():
            for j, chip in enumerate(chips):
                for arrived, onward in zip(copies(1 + j, (*chip, ac), me), copies(4 + j, (*chip, ac), sibling)):
                    arrived.wait_recv()
                    onward.start()

        def last():
            for cp in copies(0, sibling, me):
                cp.wait_recv()
            for j, chip in enumerate(chips):
                for cp in copies(4 + j, (*chip, 1 - ac), me):
                    cp.wait_recv()
            for cp in copies(0, me, sibling, own=True):
                cp.wait_send()
            for j, chip in enumerate(chips):
                for cp in copies(1 + j, me, (*chip, ac), own=True) + copies(4 + j, (*chip, ac), sibling):
                    cp.wait_send()

        return first, mid, last

    return _Rider(list(xs), [jax.ShapeDtypeStruct((NDEV,) + x.shape, x.dtype) for x in xs],
                  [pltpu.SemaphoreType.DMA((7, n)), pltpu.SemaphoreType.DMA((7, n))], hooks)


def _scatter_rider(parts):
    packed = sum(r for _, r in parts)
    width, dtype = parts[0][0].shape[1], parts[0][0].dtype

    def hooks(g_refs, out_refs, send_sems, recv_sems):
        (recv_ref,) = out_refs
        ax, ay, ac = _position()

        def peer(rel):
            flip = lambda a, bit: 1 - a if rel & bit else a
            return flip(ax, 4), flip(ay, 2), flip(ac, 1)

        def first():
            for rel in range(1, NDEV):
                px, py, pc = peer(rel)
                off = 0
                for g_ref, (_, r) in zip(g_refs, parts):
                    rows = g_ref.at[pl.ds(pl.multiple_of((4 * px + 2 * py + pc) * r, ROW_ALIGN), r)]
                    pltpu.make_async_remote_copy(
                        src_ref=rows, dst_ref=recv_ref.at[rel - 1, pl.ds(off, r)], send_sem=send_sems.at[rel - 1],
                        recv_sem=recv_sems.at[rel - 1], device_id=(px, py, pc), device_id_type=MESH_IDS).start()
                    off += r

        def last():
            for rel in range(1, NDEV):
                slot = recv_ref.at[rel - 1]
                pltpu.make_async_remote_copy(src_ref=slot, dst_ref=slot, send_sem=send_sems.at[rel - 1],
                                             recv_sem=recv_sems.at[rel - 1], device_id=peer(rel), device_id_type=MESH_IDS).wait()

        return first, lambda: None, last

    return _Rider([g for g, _ in parts], [jax.ShapeDtypeStruct((NDEV - 1, packed, width), dtype)],
                  [pltpu.SemaphoreType.DMA((NDEV - 1,)), pltpu.SemaphoreType.DMA((NDEV - 1,))], hooks)


def _sum_direct(own, recv, name):
    r, w = own.shape
    tr = max(t for t in range(ROW_ALIGN, 257, ROW_ALIGN) if r % t == 0)

    def body(own_ref, *refs):
        acc = own_ref[...].astype(F32)
        for ref in refs[:-1]:
            acc = acc + ref[...].astype(F32)
        refs[-1][...] = acc

    rblk = lambda k: pl.BlockSpec((None, tr, w), functools.partial(lambda i, k: (k, i, 0), k=k))
    blk = pl.BlockSpec((tr, w), lambda i: (i, 0))
    return pl.pallas_call(body, name=name, grid=(r // tr,), in_specs=[blk] + [rblk(k) for k in range(NDEV - 1)],
                          out_specs=blk, out_shape=jax.ShapeDtypeStruct((r, w), F32),
                          compiler_params=_cparams("parallel"))(own, *([recv] * (NDEV - 1)))


ROW_ALIGN = 16


def _window_start(rows_per_dev, k):
    return rows_per_dev * k // ROW_ALIGN * ROW_ALIGN


def _exchange_in_chip(parts, name, collective_id):
    packed = sum(win for _, _, win, _ in parts)
    width, dtype = parts[0][0].shape[1], parts[0][0].dtype

    def body(g_refs, out_refs, send_sems, recv_sems):
        (recv_ref,) = out_refs
        ax, ay, ac = _position()
        sibling = (ax, ay, 1 - ac)
        _handshake([sibling])
        for q in range(4):
            for g_ref, (_, r, win, off) in zip(g_refs, parts):
                there = g_ref.at[pl.ds(pl.multiple_of(_window_start(r, 2 * q + 1 - ac), ROW_ALIGN), win)]
                pltpu.make_async_remote_copy(src_ref=there, dst_ref=recv_ref.at[q, pl.ds(off, win)], send_sem=send_sems.at[q],
                                             recv_sem=recv_sems.at[q], device_id=sibling, device_id_type=MESH_IDS).start()
        for q in range(4):
            pltpu.make_async_remote_copy(src_ref=recv_ref.at[q], dst_ref=recv_ref.at[q], send_sem=send_sems.at[q],
                                         recv_sem=recv_sems.at[q], device_id=sibling, device_id_type=MESH_IDS).wait()

    return _on_sequencer(body, [g for g, _, _, _ in parts], [jax.ShapeDtypeStruct((4, packed, width), dtype)],
                         [pltpu.SemaphoreType.DMA((4,)), pltpu.SemaphoreType.DMA((4,))], name=name, collective_id=collective_id)[0]


def _on_sequencer(body, ins, out_shapes, sems, *, name, collective_id):
    hbm = pltpu.MemorySpace.HBM
    in_refs = [jax.new_ref(a, memory_space=hbm) for a in ins]
    out_refs = [jax.empty_ref(s, memory_space=hbm) for s in out_shapes]

    @pl.kernel(mesh=plsc.ScalarSubcoreMesh(axis_name="sequencer", num_cores=1), name=name, scratch_types=tuple(sems),
               compiler_params=pltpu.CompilerParams(collective_id=collective_id))
    def launch(*sem_refs):
        body(in_refs, out_refs, *sem_refs)

    launch()
    return [r[...] for r in out_refs]


def _handshake(peers):
    barrier = pltpu.get_barrier_semaphore()
    for peer in peers:
        pl.semaphore_signal(barrier, inc=1, device_id=peer, device_id_type=MESH_IDS)
    pl.semaphore_wait(barrier, len(peers))


def _exchange_chips_async(s1, name, collective_id):
    def body(in_refs, out_refs, send_sems, recv_sems):
        (src,), (got,) = in_refs, out_refs
        ax, ay, ac = _position()
        chips = [(1 - ax, ay), (ax, 1 - ay), (1 - ax, 1 - ay)]
        _handshake([(cx, cy, ac) for cx, cy in chips])
        copies = [pltpu.make_async_remote_copy(
            src_ref=src.at[2 * cx + cy], dst_ref=got.at[r], send_sem=send_sems.at[r], recv_sem=recv_sems.at[r],
            device_id=(cx, cy, ac), device_id_type=MESH_IDS) for r, (cx, cy) in enumerate(chips)]
        for cp in copies:
            cp.start()
        for cp in copies:
            cp.wait_recv()
        for cp in copies:
            cp.wait_send()

    return _on_sequencer(body, [s1], [jax.ShapeDtypeStruct((3,) + s1.shape[1:], s1.dtype)],
                         [pltpu.SemaphoreType.DMA((3,)), pltpu.SemaphoreType.DMA((3,))], name=name, collective_id=collective_id)[0]


def _gather_async(xs, name, collective_id):
    rider = _gather_rider(xs)

    def body(in_refs, out_refs, send_sems, recv_sems):
        ax, ay, ac = _position()
        _handshake([(ax, ay, 1 - ac), (1 - ax, ay, ac), (ax, 1 - ay, ac), (1 - ax, 1 - ay, ac)])
        for hook in rider.hooks(in_refs, out_refs, send_sems, recv_sems):
            hook()

    return _on_sequencer(body, rider.arrays, rider.out_shapes, rider.sems, name=name, collective_id=collective_id)


def _gather_balanced(x, name, collective_id):
    m = x.shape[0]
    half = m // 2 // ROW_ALIGN * ROW_ALIGN
    parts = {"all": pl.ds(0, m), "lo": pl.ds(0, half), "hi": pl.ds(half, m - half)}

    def body(in_refs, out_refs, send_sems, recv_sems):
        (x_ref,), (out_ref,) = in_refs, out_refs
        ax, ay, ac = _position()
        me, sibling = (ax, ay, ac), (ax, ay, 1 - ac)
        by_x, by_y, diag = (1 - ax, ay), (ax, 1 - ay), (1 - ax, 1 - ay)
        _handshake([sibling, (*by_x, ac), (*by_y, ac)])

        def copy(k, block, part, to, own=False):
            rows = out_ref.at[4 * block[0] + 2 * block[1] + block[2], parts[part]]
            return pltpu.make_async_remote_copy(src_ref=x_ref if own else rows, dst_ref=rows, send_sem=send_sems.at[k],
                                                recv_sem=recv_sems.at[k], device_id=to, device_id_type=MESH_IDS)

        def own_half(k, part, to):
            rows = out_ref.at[4 * ax + 2 * ay + ac, parts[part]]
            return pltpu.make_async_remote_copy(src_ref=x_ref.at[parts[part]], dst_ref=rows, send_sem=send_sems.at[k],
                                                recv_sem=recv_sems.at[k], device_id=to, device_id_type=MESH_IDS)

        nx, ny, nd = (*by_x, ac), (*by_y, ac), (*diag, ac)
        sends = [copy(0, me, "all", sibling, own=True), own_half(1, "lo", nx), own_half(2, "hi", nx),
                 own_half(3, "hi", ny), own_half(4, "lo", ny), copy(5, nx, "lo", ny), copy(6, ny, "hi", nx),
                 copy(7, nx, "lo", sibling), copy(8, nx, "hi", sibling), copy(9, ny, "hi", sibling),
                 copy(10, ny, "lo", sibling), copy(11, nd, "lo", sibling), copy(12, nd, "hi", sibling)]
        sx, sy, sd = (*by_x, 1 - ac), (*by_y, 1 - ac), (*diag, 1 - ac)
        arrivals = [copy(0, sibling, "all", me), copy(1, nx, "lo", me), copy(2, nx, "hi", me), copy(3, ny, "hi", me),
                    copy(4, ny, "lo", me), copy(5, nd, "lo", me), copy(6, nd, "hi", me), copy(7, sx, "lo", me),
                    copy(8, sx, "hi", me), copy(9, sy, "hi", me), copy(10, sy, "lo", me), copy(11, sd, "lo", me),
                    copy(12, sd, "hi", me)]
        for k in range(5):
            sends[k].start()
        for arrived, onward in ((1, (5, 7)), (3, (6, 9)), (2, (8,)), (4, (10,)), (5, (11,)), (6, (12,))):
            arrivals[arrived].wait_recv()
            for k in onward:
                sends[k].start()
        for k in (0, 7, 8, 9, 10, 11, 12):
            arrivals[k].wait_recv()
        for cp in sends:
            cp.wait_send()

    return _on_sequencer(body, [x], [jax.ShapeDtypeStruct((NDEV,) + x.shape, x.dtype)],
                         [pltpu.SemaphoreType.DMA((13,)), pltpu.SemaphoreType.DMA((13,))], name=name, collective_id=collective_id)[0]


def _scatter_async(parts, name, collective_id):
    rider = _scatter_rider(parts)

    def body(in_refs, out_refs, send_sems, recv_sems):
        ax, ay, ac = _position()
        flip = lambda a, on: 1 - a if on else a
        _handshake([(flip(ax, rel & 4), flip(ay, rel & 2), flip(ac, rel & 1)) for rel in range(1, NDEV)])
        for hook in rider.hooks(in_refs, out_refs, send_sems, recv_sems):
            hook()

    return _on_sequencer(body, rider.arrays, rider.out_shapes, rider.sems, name=name, collective_id=collective_id)[0]


def _sum_in_chip(own, recv, name):
    _, r, w = own.shape
    tr = _tile(r, (256, 128))

    def body(a_ref, b_ref, o_ref):
        o_ref[...] = (a_ref[...].astype(F32) + b_ref[...].astype(F32)).astype(o_ref.dtype)

    blk = pl.BlockSpec((None, tr, w), lambda q, i: (q, i, 0))
    return pl.pallas_call(body, name=name, grid=(4, r // tr), in_specs=[blk, blk], out_specs=blk,
                          out_shape=jax.ShapeDtypeStruct(own.shape, own.dtype),
                          compiler_params=_cparams("parallel", "parallel"))(own, recv)


def _sum_chips(s1, recv, chip, name):
    _, r, w = s1.shape
    tr = _tile(r, (256, 128))

    def body(c_ref, s_ref, r0_ref, r1_ref, r2_ref, o_ref):
        f = lambda ref: ref[...].astype(F32)
        o_ref[...] = ((f(s_ref) + f(r0_ref)) + f(r1_ref)) + f(r2_ref)

    rblk = lambda k: pl.BlockSpec((None, tr, w), functools.partial(lambda i, c, k: (k, i, 0), k=k))
    grid_spec = pltpu.PrefetchScalarGridSpec(
        num_scalar_prefetch=1, grid=(r // tr,),
        in_specs=[pl.BlockSpec((None, tr, w), lambda i, c: (c[0], i, 0)), rblk(0), rblk(1), rblk(2)],
        out_specs=pl.BlockSpec((tr, w), lambda i, c: (i, 0)))
    return pl.pallas_call(body, name=name, grid_spec=grid_spec, out_shape=jax.ShapeDtypeStruct((r, w), F32),
                          compiler_params=_cparams("parallel"))(chip, s1, recv, recv, recv)


def _silu_rows(x, name):
    def body(x_ref, o_ref):
        o_ref[...] = _silu(x_ref[...])

    return pl.pallas_call(body, name=name, out_shape=jax.ShapeDtypeStruct(x.shape, F32))(x)


def _row_sum(x, name):
    def body(x_ref, o_ref):
        acc = x_ref[0:1, :]
        for i in range(1, x.shape[0]):
            acc = acc + x_ref[i:i + 1, :]
        o_ref[...] = acc

    return pl.pallas_call(body, name=name, out_shape=jax.ShapeDtypeStruct((1, x.shape[1]), F32))(x)


def _adamw(w, g, m, v, name):
    cols = w.shape[-1]
    rows = w.size // cols
    tr = _tile(rows, (256, 128))
    tc = LANE if (tr == rows and rows > 512 and cols % LANE == 0) else cols

    def body(w_ref, g_ref, m_ref, v_ref, d_ref, mo_ref, vo_ref):
        grad = g_ref[...]
        m_new = ADAM_B1 * m_ref[...] + (1.0 - ADAM_B1) * grad
        v_new = ADAM_B2 * v_ref[...] + (1.0 - ADAM_B2) * jnp.square(grad)
        m_hat = m_new / (1.0 - ADAM_B1 ** ADAM_STEP)
        v_hat = v_new / (1.0 - ADAM_B2 ** ADAM_STEP)
        d_ref[...] = -ADAM_LR * (m_hat / (jnp.sqrt(v_hat) + ADAM_EPS) + ADAM_WD * w_ref[...])
        mo_ref[...] = m_new
        vo_ref[...] = v_new

    blk = pl.BlockSpec((tr, tc), lambda i, j: (i, j))
    out = pl.pallas_call(
        body, name=name, grid=(rows // tr, cols // tc), in_specs=[blk] * 4, out_specs=[blk] * 3,
        out_shape=[jax.ShapeDtypeStruct((rows, cols), F32)] * 3, compiler_params=_cparams("parallel", "parallel"),
    )(*[t.reshape(rows, cols) for t in (w, g, m, v)])
    return [t.reshape(w.shape) for t in out]


def _pack(parts, width, row_mult, dtype):
    flat = jnp.concatenate([p.reshape(-1).astype(dtype) for p in parts])
    rows = -(-flat.shape[0] // (width * row_mult)) * row_mult
    return jnp.pad(flat, (0, rows * width - flat.shape[0])).reshape(rows, width)


def _unpack(flat, shapes):
    out, off = [], 0
    for shp in shapes:
        size = 1
        for dim in shp:
            size *= dim
        out.append(flat[:, off:off + size].reshape((flat.shape[0],) + tuple(shp)))
        off += size
    return out


def _devices_to_cols(a):
    _, r, c = a.shape
    return a.transpose(1, 0, 2).reshape(r, NDEV * c)


def kernel(x, c, w_ada, b_ada, norm1_w, w_in, gdn_conv_w, gdn_a_log, gdn_dt_bias, gdn_norm_w, w_gdn_proj, sc_conv_w, w_sc_out, w_o, norm2_w, w_ffn_in, w_ffn_out, w_ada_f, b_ada_f, normf_w, loss_target, m_w_ada, m_b_ada, m_norm1_w, m_w_in, m_gdn_conv_w, m_gdn_a_log, m_gdn_dt_bias, m_gdn_norm_w, m_w_gdn_proj, m_sc_conv_w, m_w_sc_out, m_w_o, m_norm2_w, m_w_ffn_in, m_w_ffn_out, m_w_ada_f, m_b_ada_f, m_normf_w, v_w_ada, v_b_ada, v_norm1_w, v_w_in, v_gdn_conv_w, v_gdn_a_log, v_gdn_dt_bias, v_gdn_norm_w, v_w_gdn_proj, v_sc_conv_w, v_w_sc_out, v_w_o, v_norm2_w, v_w_ffn_in, v_w_ffn_out, v_w_ada_f, v_b_ada_f, v_normf_w):
    bl, s, d = x.shape
    heads = gdn_a_log.shape[-1]
    dff = w_ffn_out.shape[1] * NDEV
    tok = bl * s
    ax, ay, ac = _position()
    dev = 4 * ax + 2 * ay + ac
    as_tok = lambda a: a.reshape(bl, s, a.shape[-1])
    as_mat = lambda a: a.reshape(tok, a.shape[-1])

    small = _all_gather(_pack([c, gdn_conv_w, sc_conv_w], LANE, 8, F32), name="gather_cond")
    c_all, conv_w, sc_w = _unpack(small.reshape(NDEV, -1), [(bl, d), gdn_conv_w.shape[1:], sc_conv_w.shape[1:]])
    c_act = _silu_rows(c_all.reshape(NDEV * bl, d), "cond_silu")
    conv_w, sc_w = _devices_to_cols(conv_w), _devices_to_cols(sc_w)
    n_ada, n_adaf = w_ada.shape[-1], w_ada_f.shape[-1]
    bias = jnp.broadcast_to(lax.dynamic_slice_in_dim(b_ada, dev * n_ada, n_ada, axis=1), (NDEV * bl, n_ada))
    biasf = jnp.broadcast_to(lax.dynamic_slice_in_dim(b_ada_f.reshape(1, -1), dev * n_adaf, n_adaf, axis=1), (NDEV * bl, n_adaf))
    mod_cols = _mm(c_act, w_ada[0], add=bias, name="ada_cols")
    modf_cols = _mm(c_act, w_ada_f, add=biasf, name="adaf_cols")
    mods = _all_gather(jnp.concatenate([mod_cols, modf_cols], axis=1), name="gather_mod")
    mod_all = mods[:, :, :n_ada].transpose(1, 0, 2).reshape(NDEV * bl, NDEV * n_ada)
    modf_all = mods[:, :, n_ada:].transpose(1, 0, 2).reshape(NDEV * bl, NDEV * n_adaf)
    my_rows = lambda a: lax.dynamic_slice_in_dim(a, dev * bl, bl, axis=0)
    sh1, sc1, g1, sh2, sc2, g2 = [t.reshape(bl, 1, d) for t in jnp.split(my_rows(mod_all), 6, axis=1)]
    shf, scf = [t.reshape(bl, 1, d) for t in jnp.split(my_rows(modf_all), 2, axis=1)]

    late = [t.astype(MXU_DTYPE) for t in (w_gdn_proj[0], w_sc_out[0], w_o[0], w_ffn_in[0].T, w_ffn_out[0])]
    rows = [t.shape[0] for t in late] + [w_in.shape[-1]]
    offs = [sum(rows[:i]) for i in range(5)]
    in_send = w_in[0].T.astype(MXU_DTYPE)
    with_own = lambda g, own: lax.dynamic_update_slice_in_dim(g, own[None], dev, axis=0)
    wt_in = with_own(_gather_balanced(in_send, "gather_w_in", 1), in_send).reshape(NDEV * rows[5], d)
    gathered = _gather_async(late[:3], "gather_mixer", 2) + _gather_async(late[3:], "gather_ffn", 3)
    wgp, wso, wo, wt_fi, wfo = [with_own(g, own).reshape(NDEV * own.shape[0], d) for g, own in zip(gathered, late)]
    o_z, o_ab, o_sc, o_ga, o_gb = 3 * d, 4 * d, 4 * d + 2 * heads, 7 * d + 2 * heads, 8 * d + 2 * heads
    s_qkv, s_z, s_sc, s_gate = (0, o_z), (o_z, d), (o_sc, 3 * d), (o_ga, 2 * d)
    wt_ab = jnp.pad(wt_in[o_ab:o_sc], ((0, LANE - 2 * heads), (0, 0)))

    n1w, n2w, nfw = norm1_w.reshape(1, d), norm2_w.reshape(1, d), normf_w.reshape(1, d)
    lanes = lambda a: jnp.pad(a.reshape(1, -1), ((0, 0), (0, LANE - a.size)))
    a_log, dt_bias, gnw = lanes(gdn_a_log), lanes(gdn_dt_bias), gdn_norm_w.reshape(1, HEAD)
    f_gates = functools.partial(_f_gates, heads=heads)
    (h1,) = _tok_fwd(_f_norm_mod, [x], [sh1, sc1], [n1w], [(d, MXU_DTYPE)], name="norm1", ts=512)
    h1m = as_mat(h1)
    p_qkv = as_tok(_mm(h1m, wt_in, tb=True, b_rows=s_qkv, name="in_qkv"))
    p_z = as_tok(_mm(h1m, wt_in, tb=True, b_rows=s_z, name="in_z"))
    p_ab = as_tok(_mm(h1m, wt_ab, tb=True, name="in_ab"))
    p_sc = as_tok(_mm(h1m, wt_in, tb=True, b_rows=s_sc, name="in_sc"))
    p_g = as_tok(_mm(h1m, wt_in, tb=True, b_rows=s_gate, name="in_gate"))
    qkv = _qkv_fwd(p_qkv, conv_w, heads, "qkv_conv")
    (gbeta,) = _tok_fwd(f_gates, [p_ab], [], [a_log, dt_bias], [(LANE, F32)], name="gates", ts=512)
    o, s_all, t_all = _gdn_fwd(qkv, gbeta, heads, "gdn")
    (og,) = _tok_fwd(_f_gdn_out, [o, p_z], [], [(gnw, None)], [(d, MXU_DTYPE)], name="gdn_out", ts=2048, wb=HEAD, cols=heads)
    y_a = as_tok(_mm(as_mat(og), wgp, name="gdn_proj"))
    scp = _sc_fwd(p_sc, sc_w, "sc_conv")
    mrg, y_b = _tok_fwd(_f_merge_keep, [(p_g, 0), (p_g, 1), y_a, _Product(scp, wso)], [], [], [(d, MXU_DTYPE), (d, F32)],
                        name="merge", ts=512, wb=d)
    merge_toks = [(p_g, 0), (p_g, 1), y_a, y_b]
    x2, h2, mix = _tok_fwd(_f_res_norm_mod_keep, [x, _Product(mrg, wo)], [g1, sh2, sc2], [n2w],
                           [(d, F32), (d, MXU_DTYPE), (d, F32)], name="norm2", ts=512)
    act, gu_a, gu_b = _ffn_in_swiglu(as_mat(h2), wt_fi, dff, "ffn_in")

    loss_l, (dx2, dff_out, _), (dg2, dshf, dscf), (dnfw,) = _tok_bwd(
        _f_loss, [x2, _Product(as_tok(act), wfo), loss_target], [g2, shf, scf], [nfw], [], [True, True, False], name="loss",
        ts=512, loss=True, tok_dtype=[F32, MXU_DTYPE, None])
    dffm = as_mat(dff_out)
    dgu_a, dgu_b = _ffn_out_bwd_swiglu(dffm, wfo, gu_a, gu_b, "d_ffn_out")
    gmm = functools.partial(_mm, ta=True, out_dtype=MXU_DTYPE)
    gw_ffn_out = gmm(act, dffm, name="g_ffn_out")
    dh2 = _Product(as_tok(dgu_b), wt_fi, b_rows=(dff, dff), add=_Product(as_tok(dgu_a), wt_fi, b_rows=(0, dff)))
    h2m = as_mat(h2)
    gwt_ffn_in = gmm(dgu_a, h2m, out_rows=2 * dff, name="g_ffn_in_a")
    gwt_ffn_in = gmm(dgu_b, h2m, out_rows=2 * dff, row_off=dff, into=gwt_ffn_in, name="g_ffn_in_b")
    ffn_parts = [(gwt_ffn_in, rows[3]), (gw_ffn_out, rows[4])]
    ffn_recv = _scatter_async(ffn_parts, "scatter_ffn", 4)
    (dx_skip, dmix), (dg1, dsh2, dsc2), (dn2w,) = _tok_bwd(
        _f_res_norm_mod, [x, mix], [g1, sh2, sc2], [n2w], [dx2, dh2], [True, True], name="d_norm2", ts=256,
        tok_dtype=[F32, MXU_DTYPE], after=[gwt_ffn_in, gw_ffn_out])
    gw_o = gmm(as_mat(mrg), as_mat(dmix), name="g_mix_out")
    (dga, dgb, dya, dyb), _, _ = _tok_bwd(_f_merge, merge_toks, [], [], [_Product(dmix, wo, tb=True)], [True] * 4,
                                          name="d_merge", ts=512, wb=d, tok_dtype=MXU_DTYPE)
    dyam, dybm = as_mat(dya), as_mat(dyb)
    dog = as_tok(_mm(dyam, wgp, tb=True, name="d_gdn_proj"))
    gw_gdn_proj = gmm(as_mat(og), dyam, name="g_gdn_proj")
    dscp = as_tok(_mm(dybm, wso, tb=True, name="d_sc_out"))
    gw_sc_out = gmm(as_mat(scp), dybm, name="g_sc_out")
    dsc, g_sc_w = _sc_bwd(p_sc, sc_w, dscp, "d_sc_conv")
    mix_parts = [(gw_gdn_proj, rows[0]), (gw_sc_out, rows[1]), (gw_o, rows[2])]
    mix_recv = _scatter_async(mix_parts, "scatter_mixer", 5)
    (do, dz), _, (g_gnw,) = _tok_bwd(_f_gdn_out, [o, p_z], [], [(gnw, None)], [dog], [True, True], name="d_gdn_out",
                                     ts=2048, wb=HEAD, cols=heads, tok_dtype=[F32, MXU_DTYPE],
                                     after=[gw_gdn_proj, gw_sc_out, gw_o])
    own_rows = lambda parts: jnp.concatenate([lax.dynamic_slice_in_dim(g, dev * r, r, axis=0) for g, r in parts], axis=0)
    dqkv, dgbeta = _gdn_bwd(qkv, gbeta, do, s_all, t_all, heads, "d_gdn")
    dp_qkv, g_conv_w = _qkv_bwd(p_qkv, conv_w, dqkv, heads, "d_qkv_conv")
    ffn_red = _sum_direct(own_rows(ffn_parts), ffn_recv, "sum_ffn")
    mix_red = _sum_direct(own_rows(mix_parts), mix_recv, "sum_mix")
    (dp_ab,), _, (g_a_log, g_dt_bias) = _tok_bwd(f_gates, [p_ab], [], [a_log, dt_bias], [dgbeta], [True], name="d_gates",
                                                 ts=512, tok_dtype=MXU_DTYPE, after=[ffn_red, mix_red])
    r_in = rows[5]
    win = -(-(r_in + max(r_in * k % ROW_ALIGN for k in range(NDEV))) // 128) * 128
    need_rows = max(_window_start(r_in, k) for k in range(NDEV)) + win
    dsc_m = dsc.reshape(3, tok, d)
    gwt_in = ([gmm(as_mat(dp_qkv), h1m, name="g_in_qkv"), gmm(as_mat(dz), h1m, name="g_in_z"),
               gmm(as_mat(dp_ab), h1m, name="g_in_ab")[:2 * heads]]
              + [gmm(dsc_m, h1m, a_index=k, name=f"g_in_sc{k}") for k in range(3)]
              + [gmm(as_mat(dga), h1m, name="g_in_ga"), gmm(as_mat(dgb), h1m, name="g_in_gb")])
    gwt_in = jnp.concatenate(gwt_in + [jnp.zeros((need_rows - NDEV * r_in, d), MXU_DTYPE)], axis=0)
    assert d <= 1024
    wide = [as_mat(dp_qkv), as_mat(dz), dsc_m, as_mat(dga)]
    row_of = lambda t: d * t + jnp.where(t * d >= o_ab, 2 * heads, 0)
    recv1 = _exchange_in_chip([(gwt_in, r_in, win, 0)], "scatter_in_chip", 7)
    own = jnp.stack([lax.dynamic_slice_in_dim(gwt_in, _window_start(r_in, 2 * q + ac), win, axis=0) for q in range(4)])
    s1 = _sum_in_chip(own, recv1, "sum_in_chip")
    recv2 = _exchange_chips_async(s1, "scatter_chips", 6)

    dh1 = _mm(as_mat(dp_ab), wt_ab, name="d_in_ab")
    dh1 = _mm_chain(wide, wt_in, row_of, add=dh1, name="d_in", tk=d)
    dh1 = _Product(dgb, wt_in, b_rows=(o_gb, d), add=as_tok(dh1))
    (grad_x,), (dsh1, dsc1), (dn1w,) = _tok_bwd(_f_norm_mod_skip, [x], [sh1, sc1], [n1w], [dh1, dx_skip], [True],
                                                name="d_norm1", ts=512)
    reduced = _sum_chips(s1, recv2, (2 * ax + ay).reshape(1).astype(jnp.int32), "sum_chips")
    gt_w_in = lax.dynamic_slice_in_dim(reduced, r_in * dev - _window_start(r_in, dev), r_in, axis=0)
    g_w_in = gt_w_in.T.reshape(w_in.shape)
    gt_w_ffn_in = ffn_red[:rows[3]]
    g_w_ffn_in = gt_w_ffn_in.T.reshape(w_ffn_in.shape)
    g_w_ffn_out = ffn_red[rows[3]:].reshape(w_ffn_out.shape)
    g_w_gdn_proj, g_w_sc_out, g_w_o = (mix_red[offs[i]:offs[i] + rows[i]].reshape(ref.shape)
                                       for i, ref in enumerate((w_gdn_proj, w_sc_out, w_o)))

    dmod = jnp.concatenate([t.reshape(bl, d) for t in (dsh1, dsc1, dg1, dsh2, dsc2, dg2)], axis=1)
    dmodf = jnp.concatenate([t.reshape(bl, d) for t in (dshf, dscf)], axis=1)
    summed_parts = [dn1w, dn2w, dnfw, g_gnw, g_a_log, g_dt_bias, g_conv_w, g_sc_w, loss_l]
    partial = _all_gather(_pack([dmod, dmodf] + summed_parts, LANE, 8, F32), name="gather_small")
    partial = partial.reshape(NDEV, -1)
    n_rows = bl * (6 * d + 2 * d)
    dmod_all, dmodf_all = _unpack(partial[:, :n_rows], [(bl, 6 * d), (bl, 2 * d)])
    dmod_all, dmodf_all = dmod_all.reshape(NDEV * bl, 6 * d), dmodf_all.reshape(NDEV * bl, 2 * d)
    totals = _row_sum(partial[:, n_rows:], "sum_small")
    t_n1w, t_n2w, t_nfw, t_gnw, t_a_log, t_dt_bias, t_conv_w, t_sc_w, t_loss = [
        t[0] for t in _unpack(totals, [p.shape for p in summed_parts])]
    my_cols = lambda a, n: lax.dynamic_slice_in_dim(a, dev * n, n, axis=1)
    grads = {
        "w_ada": _mm(c_act, my_cols(dmod_all, n_ada), ta=True, name="g_ada").reshape(w_ada.shape),
        "b_ada": _row_sum(dmod_all, "g_ada_bias").reshape(b_ada.shape),
        "norm1_w": t_n1w.reshape(norm1_w.shape),
        "w_in": g_w_in,
        "gdn_conv_w": my_cols(t_conv_w, gdn_conv_w.shape[-1]).reshape(gdn_conv_w.shape),
        "gdn_a_log": t_a_log[:, :heads].reshape(gdn_a_log.shape),
        "gdn_dt_bias": t_dt_bias[:, :heads].reshape(gdn_dt_bias.shape),
        "gdn_norm_w": t_gnw.reshape(gdn_norm_w.shape),
        "w_gdn_proj": g_w_gdn_proj,
        "sc_conv_w": my_cols(t_sc_w, sc_conv_w.shape[-1]).reshape(sc_conv_w.shape),
        "w_sc_out": g_w_sc_out,
        "w_o": g_w_o,
        "norm2_w": t_n2w.reshape(norm2_w.shape),
        "w_ffn_in": g_w_ffn_in,
        "w_ffn_out": g_w_ffn_out,
        "w_ada_f": _mm(c_act, my_cols(dmodf_all, n_adaf), ta=True, name="g_adaf").reshape(w_ada_f.shape),
        "b_ada_f": _row_sum(dmodf_all, "g_adaf_bias").reshape(b_ada_f.shape),
        "normf_w": t_nfw.reshape(normf_w.shape),
    }
    weights = dict(w_ada=w_ada, b_ada=b_ada, norm1_w=norm1_w, w_in=w_in, gdn_conv_w=gdn_conv_w, gdn_a_log=gdn_a_log,
                   gdn_dt_bias=gdn_dt_bias, gdn_norm_w=gdn_norm_w, w_gdn_proj=w_gdn_proj, sc_conv_w=sc_conv_w,
                   w_sc_out=w_sc_out, w_o=w_o, norm2_w=norm2_w, w_ffn_in=w_ffn_in, w_ffn_out=w_ffn_out, w_ada_f=w_ada_f,
                   b_ada_f=b_ada_f, normf_w=normf_w)
    m_in = [m_w_ada, m_b_ada, m_norm1_w, m_w_in, m_gdn_conv_w, m_gdn_a_log, m_gdn_dt_bias, m_gdn_norm_w, m_w_gdn_proj,
            m_sc_conv_w, m_w_sc_out, m_w_o, m_norm2_w, m_w_ffn_in, m_w_ffn_out, m_w_ada_f, m_b_ada_f, m_normf_w]
    v_in = [v_w_ada, v_b_ada, v_norm1_w, v_w_in, v_gdn_conv_w, v_gdn_a_log, v_gdn_dt_bias, v_gdn_norm_w, v_w_gdn_proj,
            v_sc_conv_w, v_w_sc_out, v_w_o, v_norm2_w, v_w_ffn_in, v_w_ffn_out, v_w_ada_f, v_b_ada_f, v_normf_w]
    deltas, new_m, new_v = [], [], []
    grads_t = {"w_in": gt_w_in, "w_ffn_in": gt_w_ffn_in}
    for (wname, wt), mt, vt in zip(weights.items(), m_in, v_in):
        if wname in grads_t:
            back = lambda a, wt=wt: a.T.reshape(wt.shape)
            dl, mn, vn = (back(a) for a in _adamw(wt[0].T, grads_t[wname], mt[0].T, vt[0].T, "adamw_" + wname))
        else:
            dl, mn, vn = _adamw(wt, grads[wname], mt, vt, "adamw_" + wname)
        deltas.append(dl)
        new_m.append(mn)
        new_v.append(vn)
    loss = t_loss[0, 0]
    return (loss, grad_x, *[grads[k] for k in weights], *deltas, *new_m, *new_v)
```

```python
import functools

import jax
import jax.numpy as jnp
from jax import lax
from jax.experimental import pallas as pl
from jax.experimental.pallas import tpu as pltpu
from jax.experimental.pallas import tpu_sc as plsc

F32 = jnp.float32
MXU_DTYPE = jnp.bfloat16
NDEV = 8
CHUNK = 64
HEAD = 128
LANE = 128
EPS = 1e-6
ADAM_LR, ADAM_B1, ADAM_B2, ADAM_EPS, ADAM_WD, ADAM_STEP = 0.001, 0.9, 0.999, 1e-08, 0.01, 10
VMEM_LIMIT = 48 * 1024 * 1024
MESH_IDS = pl.DeviceIdType.MESH
HIGHEST = lax.Precision.HIGHEST


def _tile(n, cands=(512, 256, 128)):
    for c in cands:
        if n % c == 0:
            return c
    return n


def _cparams(*sem):
    return pltpu.CompilerParams(dimension_semantics=sem, vmem_limit_bytes=VMEM_LIMIT)


def _mm(a, b, *, ta=False, tb=False, add=None, out_dtype=F32, name, b_rows=None, out_rows=None, row_off=0, into=None,
        a_index=None):
    m, k = (a.shape[-1], a.shape[-2]) if ta else a.shape[-2:]
    b_shape = b.shape if b_rows is None else (b_rows[1], b.shape[1])
    n = b_shape[0] if tb else b_shape[1]
    assert k == (b_shape[1] if tb else b_shape[0])
    if ta:
        tm, tn = _tile(m), n if n <= 1024 else _tile(n)
        tk = k if k <= 4096 else _tile(k, (4096, 2048, 1024, 512))
        if tm * tk > 1024 * 2048:
            tk = _tile(k, (2048, 1024, 512))
    else:
        tk = k if k <= 1024 else _tile(k, (1024, 512))
        tn = _tile(n, (1024 if tk <= 1024 else 512, 512, 256, 128))
        tm = _tile(m, (2048 if (tn <= 512 and tk <= 1024) else 1024, 1024, 512, 256, 128))
    nk = k // tk
    dims = (((0 if ta else 1,), (1 if tb else 0,)), ((), ()))
    has_add = add is not None

    def body(*refs):
        a_ref, b_ref = refs[0], refs[1]
        add_ref = refs[2] if has_add else None
        o_ref = refs[2 + has_add + (into is not None)]
        part = lax.dot_general(a_ref[...].astype(MXU_DTYPE), b_ref[...].astype(MXU_DTYPE), dims,
                               preferred_element_type=F32)

        def finish(acc):
            if has_add:
                acc = acc + add_ref[...]
            o_ref[...] = acc.astype(o_ref.dtype)

        if nk == 1:
            finish(part)
        else:
            acc_ref = refs[-1]
            kk = pl.program_id(2)

            @pl.when(kk == 0)
            def _():
                acc_ref[...] = part

            @pl.when(kk > 0)
            def _():
                acc_ref[...] += part

            @pl.when(kk == nk - 1)
            def _():
                finish(acc_ref[...])

    a_blk, a_at = ((tk, tm), lambda i, j, kk: (kk, i)) if ta else ((tm, tk), lambda i, j, kk: (i, kk))
    a_spec = (pl.BlockSpec(a_blk, a_at) if a_index is None else
              pl.BlockSpec((None,) + a_blk, lambda i, j, kk: (a_index,) + a_at(i, j, kk)))
    if b_rows is None:
        b_spec = pl.BlockSpec((tn, tk), lambda i, j, kk: (j, kk)) if tb else pl.BlockSpec((tk, tn), lambda i, j, kk: (kk, j))
    else:
        at = lambda t: pl.multiple_of(b_rows[0] + t, ROW_ALIGN)
        b_spec = (pl.BlockSpec((pl.Element(tn), pl.Element(tk)), lambda i, j, kk: (at(j * tn), kk * tk)) if tb else
                  pl.BlockSpec((pl.Element(tk), pl.Element(tn)), lambda i, j, kk: (at(kk * tk), j * tn)))
    add_spec = pl.BlockSpec((tm, tn), lambda i, j, kk: (i, j))
    assert row_off % tm == 0
    o_spec = pl.BlockSpec((tm, tn), lambda i, j, kk: (i + row_off // tm, j))
    in_specs = [a_spec, b_spec] + ([add_spec] if has_add else []) + ([pl.BlockSpec(memory_space=pl.ANY)] if into is not None else [])
    args = [a, b] + ([add] if has_add else []) + ([into] if into is not None else [])
    return pl.pallas_call(
        body, name=name, grid=(m // tm, n // tn, nk), in_specs=in_specs, out_specs=o_spec,
        out_shape=jax.ShapeDtypeStruct((out_rows or m, n), out_dtype),
        scratch_shapes=[pltpu.VMEM((tm, tn), F32)] if nk > 1 else [],
        input_output_aliases={len(args) - 1: 0} if into is not None else {},
        compiler_params=_cparams("parallel", "parallel", "arbitrary"),
    )(*args)


def _mm_chain(parts, b, row_of_tile, *, add, name, tk=1024, tm=1024):
    m, n = parts[0].shape[-2], b.shape[1]
    tm = min(tm, m)
    tiles = [p.shape[0] if p.ndim == 3 else p.shape[1] // tk for p in parts]
    first = [sum(tiles[:s]) for s in range(len(parts))]
    nk = sum(tiles)

    def body(*refs):
        a_refs, b_ref, add_ref, o_ref, acc_ref = refs[:len(parts)], *refs[len(parts):]
        kk = pl.program_id(1)

        @pl.when(kk == 0)
        def _():
            acc_ref[...] = add_ref[...]

        for a_ref, lo, cnt in zip(a_refs, first, tiles):
            @pl.when(jnp.logical_and(kk >= lo, kk < lo + cnt))
            def _(a_ref=a_ref):
                acc_ref[...] += lax.dot_general(a_ref[...].astype(MXU_DTYPE), b_ref[...].astype(MXU_DTYPE),
                                                (((1,), (0,)), ((), ())), preferred_element_type=F32)

        @pl.when(kk == nk - 1)
        def _():
            o_ref[...] = acc_ref[...]

    tile_of = lambda kk, lo, cnt: jnp.clip(kk - lo, 0, cnt - 1)
    a_specs = [pl.BlockSpec((None, tm, tk), functools.partial(lambda i, kk, lo, cnt: (tile_of(kk, lo, cnt), i, 0), lo=lo, cnt=cnt))
               if p.ndim == 3 else
               pl.BlockSpec((tm, tk), functools.partial(lambda i, kk, lo, cnt: (i, tile_of(kk, lo, cnt)), lo=lo, cnt=cnt))
               for p, lo, cnt in zip(parts, first, tiles)]
    b_spec = pl.BlockSpec((pl.Element(tk), pl.Element(n)), lambda i, kk: (pl.multiple_of(row_of_tile(kk), ROW_ALIGN), 0))
    o_spec = pl.BlockSpec((tm, n), lambda i, kk: (i, 0))
    return pl.pallas_call(
        body, name=name, grid=(m // tm, nk), in_specs=a_specs + [b_spec, o_spec], out_specs=o_spec,
        out_shape=jax.ShapeDtypeStruct((m, n), F32), scratch_shapes=[pltpu.VMEM((tm, n), F32)],
        compiler_params=_cparams("parallel", "arbitrary"),
    )(*parts, b, add)


def _swiglu_tiles(m, half):
    tn = _tile(half, (512, 256, 128))
    return _tile(m, (2048 if tn <= 256 else 1024, 1024, 512, 256, 128)), tn


def _ffn_in_swiglu(h, wt, half, name):
    m, k = h.shape
    tm, tn = _swiglu_tiles(m, half)
    nj = half // tn
    dims = (((1,), (1,)), ((), ()))

    def body(h_ref, wa_ref, wb_ref, act_ref, a_ref, b_ref):
        lhs = h_ref[...].astype(MXU_DTYPE)
        a = lax.dot_general(lhs, wa_ref[...].astype(MXU_DTYPE), dims, preferred_element_type=F32)
        b = lax.dot_general(lhs, wb_ref[...].astype(MXU_DTYPE), dims, preferred_element_type=F32)
        act_ref[...] = (_silu(a) * b).astype(act_ref.dtype)
        a_ref[...] = a.astype(a_ref.dtype)
        b_ref[...] = b.astype(b_ref.dtype)

    out = jax.ShapeDtypeStruct((m, half), MXU_DTYPE)
    oblk = pl.BlockSpec((tm, tn), lambda i, j: (i, j))
    return pl.pallas_call(
        body, name=name, grid=(m // tm, nj),
        in_specs=[pl.BlockSpec((tm, k), lambda i, j: (i, 0)), pl.BlockSpec((tn, k), lambda i, j: (j, 0)),
                  pl.BlockSpec((tn, k), lambda i, j: (j + nj, 0))],
        out_specs=[oblk, oblk, oblk], out_shape=[out, out, out], compiler_params=_cparams("parallel", "parallel"),
    )(h, wt, wt)


def _ffn_out_bwd_swiglu(dff, w, a, b, name):
    m, k = dff.shape
    half = w.shape[0]
    tm, tn = _swiglu_tiles(m, half)

    def body(d_ref, w_ref, a_ref, b_ref, da_ref, db_ref):
        dact = lax.dot_general(d_ref[...].astype(MXU_DTYPE), w_ref[...].astype(MXU_DTYPE), (((1,), (1,)), ((), ())),
                               preferred_element_type=F32)
        av, bv = a_ref[...].astype(F32), b_ref[...].astype(F32)
        sig = jax.nn.sigmoid(av)
        da_ref[...] = (dact * bv * (sig * (1.0 + av * (1.0 - sig)))).astype(da_ref.dtype)
        db_ref[...] = (dact * (av * sig)).astype(db_ref.dtype)

    out = jax.ShapeDtypeStruct((m, half), MXU_DTYPE)
    oblk = pl.BlockSpec((tm, tn), lambda i, j: (i, j))
    return pl.pallas_call(
        body, name=name, grid=(m // tm, half // tn),
        in_specs=[pl.BlockSpec((tm, k), lambda i, j: (i, 0)), pl.BlockSpec((tn, k), lambda i, j: (j, 0)), oblk, oblk],
        out_specs=[oblk, oblk], out_shape=[out, out], compiler_params=_cparams("parallel", "parallel"),
    )(dff, w, a, b)


def _with_off(xs):
    return [x if isinstance(x, tuple) else (x, 0) for x in xs]


def _spec(kind, arr, off, ts, wb):
    w = arr.shape[-1] if wb is None else wb
    col = (lambda j: 0) if wb is None else functools.partial(lambda j, o: o + j, o=off)
    if kind == "tok":
        return pl.BlockSpec((None, ts, w), lambda j, b, i: (b, i, col(j)))
    if kind == "bat":
        return pl.BlockSpec((None, 1, w), lambda j, b, i: (b, 0, col(j)))
    if off is None:
        return pl.BlockSpec(arr.shape, lambda j, b, i: (0, 0))
    return pl.BlockSpec((arr.shape[0], w), lambda j, b, i: (0, col(j)))


class _Product:
    def __init__(self, a, b, *, tb=False, b_rows=None, add=None):
        self.a, self.b, self.tb, self.b_rows, self.add = a, b, tb, b_rows, add
        rows = b.shape[0] if b_rows is None else b_rows[1]
        self.shape = a.shape[:2] + (rows if tb else b.shape[1],)

    def inputs(self, ts):
        a_spec = pl.BlockSpec((None, ts, self.a.shape[2]), lambda j, b, i: (b, i, 0))
        if self.b_rows is None:
            b_spec = pl.BlockSpec(self.b.shape, lambda j, b, i: (0, 0))
        else:
            start, count = self.b_rows
            b_spec = pl.BlockSpec((pl.Element(count), pl.Element(self.b.shape[1])), lambda j, b, i: (start, 0))
        if isinstance(self.add, _Product):
            extra = self.add.inputs(ts)
        else:
            extra = [] if self.add is None else [(self.add, pl.BlockSpec((None, ts, self.shape[2]), lambda j, b, i: (b, i, 0)))]
        return [(self.a, a_spec), (self.b, b_spec)] + extra

    def value(self, refs):
        dims = (((1,), (1 if self.tb else 0,)), ((), ()))
        val = lax.dot_general(refs[0][...].astype(MXU_DTYPE), refs[1][...].astype(MXU_DTYPE), dims, preferred_element_type=F32)
        if isinstance(self.add, _Product):
            return val + self.add.value(refs[2:])
        return val if self.add is None else val + refs[2][...].astype(F32)


def _inputs(groups, kinds, ts, wb):
    loaded = [(a, _spec(kind, a, o, ts, wb)) for g, kind in zip(groups, kinds) for a, o in g if not isinstance(a, _Product)]
    made = [pair for g in groups for a, _ in g if isinstance(a, _Product) for pair in a.inputs(ts)]
    return [a for a, _ in loaded + made], [sp for _, sp in loaded + made]


def _values(refs, groups):
    n_loaded = sum(1 for g in groups for a, _ in g if not isinstance(a, _Product))
    loaded, pos, out = iter(refs[:n_loaded]), n_loaded, []
    for g in groups:
        vals = []
        for a, _ in g:
            if isinstance(a, _Product):
                k = len(a.inputs(1))
                vals.append(a.value(refs[pos:pos + k]))
                pos += k
            else:
                vals.append(next(loaded)[...].astype(F32))
        out.append(vals)
    return out, pos


def _tok_fwd(fn, toks, bats, pars, outs, *, name, ts, wb=None, cols=1):
    groups = [_with_off(toks), _with_off(bats), _with_off(pars)]
    bl, s, _ = groups[0][0][0].shape
    ts = min(ts, s)
    args, in_specs = _inputs(groups, ("tok", "bat", "par"), ts, wb)

    def body(*refs):
        vals, n_in = _values(refs, groups)
        res = fn(*[v for g in vals for v in g])
        for r, val in zip(refs[n_in:], res):
            r[...] = val.astype(r.dtype)

    out_specs = [pl.BlockSpec((None, ts, w if wb is None else wb), lambda j, b, i: (b, i, j)) for w, _ in outs]
    return pl.pallas_call(
        body, name=name, grid=(cols, bl, s // ts), in_specs=in_specs,
        out_specs=out_specs, out_shape=[jax.ShapeDtypeStruct((bl, s, w), dt) for w, dt in outs],
        compiler_params=_cparams("parallel", "parallel", "parallel"),
    )(*args)


def _accumulate(ref, val, first):
    @pl.when(first)
    def _():
        ref[...] = val

    @pl.when(jnp.logical_not(first))
    def _():
        ref[...] += val


def _tok_bwd(fn, toks, bats, pars, cots, need, *, name, ts, wb=None, cols=1, tok_dtype=F32, loss=False, after=()):
    toks, bats, pars, cots = _with_off(toks), _with_off(bats), _with_off(pars), _with_off(cots)
    groups = [toks, bats, pars, cots]
    bl, s, _ = toks[0][0].shape
    ts = min(ts, s)
    nt, nb, npar = len(toks), len(bats), len(pars)
    args, in_specs = _inputs(groups, ("tok", "bat", "par", "tok"), ts, wb)
    args, in_specs = args + list(after), in_specs + [pl.BlockSpec(memory_space=pl.ANY)] * len(after)

    def body(*refs):
        j, b, i = pl.program_id(0), pl.program_id(1), pl.program_id(2)
        (tok_vals, bat_vals, par_vals, cot_vals), o = _values(refs, groups)
        o += len(after)
        outs, vjp = jax.vjp(fn, *tok_vals, *bat_vals, *par_vals)
        if loss:
            ct = (jnp.ones_like(outs[0]),)
            tot = jnp.broadcast_to(jnp.sum(outs[0], keepdims=True), (1, LANE))
            _accumulate(refs[o], tot, jnp.logical_and(b == 0, i == 0))
            o += 1
        else:
            ct = tuple(cot_vals)
        grads = vjp(ct)
        for t in range(nt):
            if need[t]:
                refs[o][...] = grads[t].astype(refs[o].dtype)
                o += 1
        for t in range(nb):
            _accumulate(refs[o], grads[nt + t], i == 0)
            o += 1
        for t in range(npar):
            first = jnp.logical_and(b == 0, i == 0)
            if pars[t][1] is None:
                first = jnp.logical_and(first, j == 0)
            _accumulate(refs[o], grads[nt + nb + t], first)
            o += 1

    full = lambda arr: arr.shape[-1] if wb is None else wb * cols
    blk = lambda arr: arr.shape[-1] if wb is None else wb
    out_specs, out_shape = [], []
    if loss:
        out_specs.append(pl.BlockSpec((1, LANE), lambda j, b, i: (0, 0)))
        out_shape.append(jax.ShapeDtypeStruct((1, LANE), F32))
    for t in range(nt):
        if need[t]:
            out_specs.append(pl.BlockSpec((None, ts, blk(toks[t][0])), lambda j, b, i: (b, i, j)))
            dt = tok_dtype[t] if isinstance(tok_dtype, (list, tuple)) else tok_dtype
            out_shape.append(jax.ShapeDtypeStruct((bl, s, full(toks[t][0])), dt))
    for arr, _ in bats:
        out_specs.append(pl.BlockSpec((None, 1, blk(arr)), lambda j, b, i: (b, 0, j)))
        out_shape.append(jax.ShapeDtypeStruct((bl, 1, full(arr)), F32))
    for arr, off in pars:
        if off is None:
            out_specs.append(pl.BlockSpec(arr.shape, lambda j, b, i: (0, 0)))
            out_shape.append(jax.ShapeDtypeStruct(arr.shape, F32))
        else:
            out_specs.append(pl.BlockSpec((arr.shape[0], blk(arr)), lambda j, b, i: (0, j)))
            out_shape.append(jax.ShapeDtypeStruct((arr.shape[0], full(arr)), F32))
    res = list(pl.pallas_call(
        body, name=name, grid=(cols, bl, s // ts), in_specs=in_specs,
        out_specs=out_specs, out_shape=out_shape, compiler_params=_cparams("arbitrary", "arbitrary", "arbitrary"),
    )(*args))
    tot = res.pop(0) if loss else None
    dtoks = [res.pop(0) if need[t] else None for t in range(nt)]
    dbats = [res.pop(0) for _ in range(nb)]
    dpars = [res.pop(0) for _ in range(npar)]
    return (tot, dtoks, dbats, dpars) if loss else (dtoks, dbats, dpars)


def _silu(x):
    return x * jax.nn.sigmoid(x)


def _rms(x, w):
    return x * lax.rsqrt(jnp.mean(x * x, axis=-1, keepdims=True) + EPS) * w


def _f_norm_mod(x, shift, scale, w):
    return (_rms(x, w) * (1.0 + scale) + shift,)


def _f_norm_mod_skip(x, shift, scale, w):
    return _rms(x, w) * (1.0 + scale) + shift, x


def _f_res_norm_mod(x, mix, gate, shift, scale, w):
    x2 = x + gate * mix
    return x2, _rms(x2, w) * (1.0 + scale) + shift


def _f_res_norm_mod_keep(x, mix, gate, shift, scale, w):
    return (*_f_res_norm_mod(x, mix, gate, shift, scale, w), mix)


def _f_gates(p, a_log, dt_bias, *, heads):
    z = p + dt_bias
    g = -jnp.exp(a_log) * (jnp.maximum(z, 0.0) + jnp.log1p(jnp.exp(jnp.minimum(z, -z))))
    lane = lax.broadcasted_iota(jnp.int32, p.shape, 1)
    return (jnp.where(lane < heads, g, jax.nn.sigmoid(p)),)


def _f_gdn_out(o, z, w):
    return (_rms(o, w) * _silu(z),)


def _f_merge(ga, gb, ya, yb):
    return (jax.nn.sigmoid(ga) * ya + jax.nn.sigmoid(gb) * yb,)


def _f_merge_keep(ga, gb, ya, yb):
    return (*_f_merge(ga, gb, ya, yb), yb)


def _f_loss(x2, ff, tgt, gate, shift, scale, w):
    y = _rms(x2 + gate * ff, w) * (1.0 + scale) + shift
    return (0.5 * jnp.mean(jnp.square(y - tgt), axis=-1, keepdims=True),)


def _shift_down(x, s):
    if s == 0:
        return x
    row = lax.broadcasted_iota(jnp.int32, x.shape, 0)
    return jnp.where(row >= s, pltpu.roll(x, s, 0), 0.0)


def _shift_up(x, s):
    if s == 0:
        return x
    n = x.shape[0]
    row = lax.broadcasted_iota(jnp.int32, x.shape, 0)
    return jnp.where(row < n - s, pltpu.roll(x, n - s, 0), 0.0)


def _conv(x, w):
    width = w.shape[0]
    acc = w[width - 1:width, :] * x
    for j in range(width - 1):
        acc = acc + w[j:j + 1, :] * _shift_down(x, width - 1 - j)
    return acc


def _conv_bwd(dy, x, w, dw_ref, first):
    width = w.shape[0]
    dx = w[width - 1:width, :] * dy
    for j in range(width - 1):
        dx = dx + w[j:j + 1, :] * _shift_up(dy, width - 1 - j)
    for j in range(width):
        row = jnp.sum(dy * _shift_down(x, width - 1 - j), axis=0, keepdims=True)
        _accumulate(dw_ref.at[j:j + 1, :], row, first)
    return dx


def _qkv_act(xc, is_v, scale):
    a = _silu(xc)
    nrm = a * lax.rsqrt(jnp.sum(a * a, axis=-1, keepdims=True) + EPS) * scale
    return jnp.where(is_v, a, nrm)


def _qkv_act_bwd(xc, dout, is_v, scale):
    sig = jax.nn.sigmoid(xc)
    a = xc * sig
    r = lax.rsqrt(jnp.sum(a * a, axis=-1, keepdims=True) + EPS)
    c1 = r * scale
    da = c1 * dout - a * (c1 * r * r * jnp.sum(dout * a, axis=-1, keepdims=True))
    return jnp.where(is_v, dout, da) * (sig * (1.0 + xc * (1.0 - sig)))


def _qkv_consts(j, heads):
    is_v = j >= 2 * heads
    scale = jnp.where(j < heads, HEAD ** -0.5, 1.0).astype(F32)
    return is_v, scale


def _qkv_fwd(p, w, heads, name):
    bl, s, w3 = p.shape

    def body(p_ref, w_ref, o_ref):
        is_v, scale = _qkv_consts(pl.program_id(0), heads)
        o_ref[...] = _qkv_act(_conv(p_ref[...], w_ref[...]), is_v, scale)

    blk = pl.BlockSpec((None, s, HEAD), lambda j, b: (b, 0, j))
    return pl.pallas_call(
        body, name=name, grid=(w3 // HEAD, bl), in_specs=[blk, pl.BlockSpec((w.shape[0], HEAD), lambda j, b: (0, j))],
        out_specs=blk, out_shape=jax.ShapeDtypeStruct(p.shape, F32), compiler_params=_cparams("parallel", "parallel"),
    )(p, w)


def _qkv_bwd(p, w, dout, heads, name):
    bl, s, w3 = p.shape

    def body(p_ref, w_ref, d_ref, dp_ref, dw_ref):
        is_v, scale = _qkv_consts(pl.program_id(0), heads)
        x, wv = p_ref[...], w_ref[...]
        dxc = _qkv_act_bwd(_conv(x, wv), d_ref[...], is_v, scale)
        dp_ref[...] = _conv_bwd(dxc, x, wv, dw_ref, pl.program_id(1) == 0).astype(dp_ref.dtype)

    blk = pl.BlockSpec((None, s, HEAD), lambda j, b: (b, 0, j))
    wblk = pl.BlockSpec((w.shape[0], HEAD), lambda j, b: (0, j))
    return pl.pallas_call(
        body, name=name, grid=(w3 // HEAD, bl), in_specs=[blk, wblk, blk], out_specs=[blk, wblk],
        out_shape=[jax.ShapeDtypeStruct(p.shape, MXU_DTYPE), jax.ShapeDtypeStruct(w.shape, F32)],
        compiler_params=_cparams("arbitrary", "arbitrary"),
    )(p, w, dout)


def _sc_specs(p, w):
    bl, s, w3 = p.shape
    nblk = w3 // 3 // LANE
    sec = lambda k: pl.BlockSpec((None, s, LANE), functools.partial(lambda j, b, k: (b, 0, k * nblk + j), k=k))
    return nblk, [sec(0), sec(1), sec(2)], pl.BlockSpec((w.shape[0], LANE), lambda j, b: (0, j)), \
        pl.BlockSpec((None, s, LANE), lambda j, b: (b, 0, j))


def _sc_fwd(p, w, name):
    bl, s, w3 = p.shape
    nblk, secs, wblk, oblk = _sc_specs(p, w)

    def body(b_ref, c_ref, x_ref, w_ref, o_ref):
        o_ref[...] = (b_ref[...] * _conv(c_ref[...] * x_ref[...], w_ref[...])).astype(o_ref.dtype)

    return pl.pallas_call(
        body, name=name, grid=(nblk, bl), in_specs=secs + [wblk], out_specs=oblk,
        out_shape=jax.ShapeDtypeStruct((bl, s, w3 // 3), MXU_DTYPE), compiler_params=_cparams("parallel", "parallel"),
    )(p, p, p, w)


def _sc_bwd(p, w, dout, name):
    bl, s, w3 = p.shape
    nblk, secs, wblk, oblk = _sc_specs(p, w)

    def body(b_ref, c_ref, x_ref, w_ref, d_ref, dp_ref, dw_ref):
        gb, gc, xin, wv, d = b_ref[...], c_ref[...], x_ref[...], w_ref[...], d_ref[...]
        u = gc * xin
        dp_ref[0] = (d * _conv(u, wv)).astype(dp_ref.dtype)
        du = _conv_bwd(d * gb, u, wv, dw_ref, pl.program_id(1) == 0)
        dp_ref[1] = (du * xin).astype(dp_ref.dtype)
        dp_ref[2] = (du * gc).astype(dp_ref.dtype)

    return pl.pallas_call(
        body, name=name, grid=(nblk, bl), in_specs=secs + [wblk, oblk],
        out_specs=[pl.BlockSpec((3, None, s, LANE), lambda j, b: (0, b, 0, j)), wblk],
        out_shape=[jax.ShapeDtypeStruct((3, bl, s, w3 // 3), MXU_DTYPE), jax.ShapeDtypeStruct(w.shape, F32)],
        compiler_params=_cparams("arbitrary", "arbitrary"),
    )(p, p, p, w, dout)


def _bdot(a, b, ca, cb):
    return lax.dot_general(a.astype(MXU_DTYPE), b.astype(MXU_DTYPE), (((ca,), (cb,)), ((), ())),
                           preferred_element_type=F32)


def _hdot(a, b):
    return lax.dot_general(a, b, (((1,), (0,)), ((), ())), precision=HIGHEST, preferred_element_type=F32)


def _lane_col(x, idx):
    lane = lax.broadcasted_iota(jnp.int32, x.shape, 1)
    return jnp.sum(jnp.where(lane == idx, x, 0.0), axis=1, keepdims=True)


def _chunk_masks():
    r = lax.broadcasted_iota(jnp.int32, (CHUNK, CHUNK), 0)
    c = lax.broadcasted_iota(jnp.int32, (CHUNK, CHUNK), 1)
    return r == c, r >= c, r > c


def _dot3(a, b):
    ah, bh = a.astype(MXU_DTYPE), b.astype(MXU_DTYPE)
    al, bl = (a - ah.astype(F32)).astype(MXU_DTYPE), (b - bh.astype(F32)).astype(MXU_DTYPE)
    dot = lambda x, y: lax.dot_general(x, y, (((1,), (0,)), ((), ())), preferred_element_type=F32)
    return dot(ah, bh) + (dot(ah, bl) + dot(al, bh))


def _tri_inv_steps(low, eye):
    x = -low
    p = jnp.where(eye, 1.0, 0.0) + x
    span = 2
    while span < CHUNK:
        x = _dot3(x, x)
        yield
        p = p + _dot3(p, x)
        yield
        span *= 2
    return p


def _round_robin(gens):
    out, live = [None] * len(gens), list(range(len(gens)))
    while live:
        still = []
        for i in live:
            try:
                next(gens[i])
                still.append(i)
            except StopIteration as stop:
                out[i] = stop.value
        live = still
    return out


def _gdn_pre(q, k, v, gc, beta, masks):
    eye, causal, strict = masks
    gc_row = jnp.sum(jnp.where(eye, gc, 0.0), axis=0, keepdims=True)
    decay = jnp.where(causal, jnp.exp(jnp.where(causal, gc - gc_row, 0.0)), 0.0)
    eg = jnp.exp(gc)
    gl = gc[CHUNK - 1:CHUNK, :]
    kb, vb = k * beta, v * beta
    both = _bdot(jnp.concatenate([kb, q], axis=0), k, 1, 1)
    low = jnp.where(strict, both[:CHUNK] * decay, 0.0)
    qk = jnp.where(causal, both[CHUNK:] * decay, 0.0)
    rest = jnp.exp(gl - gc)
    return dict(decay=decay, eg=eg, gl=gl, kb=kb, vb=vb, kbe=kb * eg, low=low, qk=qk, qg=q * eg, rest=rest, kdec=k * rest)


GROUP = 4


def _gdn_specs(qkv, gbeta, heads, rev):
    bl, s, w3 = qkv.shape
    d, n = w3 // 3, s // CHUNK
    group = GROUP if n % GROUP == 0 else 1
    steps = n // group
    at = (lambda c: steps - 1 - c) if rev else (lambda c: c)
    assert d == heads * HEAD
    rows = group * CHUNK
    sec = pl.BlockSpec((None, rows, w3), lambda b, c: (b, at(c), 0))
    gspec = pl.BlockSpec((None, rows, LANE), lambda b, c: (b, at(c), 0))
    ospec = pl.BlockSpec((None, rows, d), lambda b, c: (b, at(c), 0))
    sspec = pl.BlockSpec((None, group, heads, HEAD, HEAD), lambda b, c: (b, at(c), 0, 0, 0))
    tspec = pl.BlockSpec((None, group, heads, CHUNK, CHUNK), lambda b, c: (b, at(c), 0, 0, 0))
    return bl, s, d, n, group, sec, gspec, ospec, sspec, tspec


def _gdn_fwd(qkv, gbeta, heads, name):
    bl, s, d, n, group, sec, gspec, ospec, sspec, tspec = _gdn_specs(qkv, gbeta, heads, False)
    rows = lambda sub: slice(sub * CHUNK, (sub + 1) * CHUNK)
    pairs = [(h, sub) for h in range(heads) for sub in range(group)]

    def body(x_ref, g_ref, o_ref, s_ref, t_ref, st_ref):
        @pl.when(pl.program_id(1) == 0)
        def _():
            st_ref[...] = jnp.zeros_like(st_ref)

        masks = _chunk_masks()
        eye, causal, _ = masks
        gblks = [g_ref[rows(sub), :] for sub in range(group)]
        gcs = [_hdot(jnp.where(causal, 1.0, 0.0), gb) for gb in gblks]
        st_all = st_ref[...]

        def free(h, sub):
            q, k, v = (x_ref[rows(sub), sec * d + h * HEAD:sec * d + (h + 1) * HEAD] for sec in range(3))
            pre = _gdn_pre(q, k, v, _lane_col(gcs[sub], h), _lane_col(gblks[sub], heads + h), masks)
            yield
            t = yield from _tri_inv_steps(pre["low"], eye)
            uw = _bdot(t, jnp.concatenate([pre["vb"], pre["kbe"]], axis=1), 1, 0)
            return pre, t, uw[:, :HEAD], uw[:, HEAD:]

        pieces = dict(zip(pairs, _round_robin([free(h, sub) for h, sub in pairs])))

        def carry(h):
            st, outs, starts = st_all[h], [], []
            for sub in range(group):
                pre, _, u, w = pieces[h, sub]
                starts.append(st)
                vnew = u - _bdot(w, st, 1, 0)
                yield
                outs.append(_bdot(pre["qg"], st, 1, 0) + _bdot(pre["qk"], vnew, 1, 0))
                st = st * jnp.exp(pre["gl"]) + _bdot(pre["kdec"], vnew, 0, 0)
                yield
            return outs, starts, st

        carried = _round_robin([carry(h) for h in range(heads)])
        per_sub = lambda pick: [[pick(h, sub) for h in range(heads)] for sub in range(group)]
        o_ref[...] = jnp.concatenate([jnp.concatenate(r, axis=1) for r in per_sub(lambda h, sub: carried[h][0][sub])], axis=0)
        s_ref[...] = jnp.stack([jnp.stack(r) for r in per_sub(lambda h, sub: carried[h][1][sub])])
        t_ref[...] = jnp.stack([jnp.stack(r) for r in per_sub(lambda h, sub: pieces[h, sub][1])])
        st_ref[...] = jnp.stack([carried[h][2] for h in range(heads)])

    return pl.pallas_call(
        body, name=name, grid=(bl, n // group), in_specs=[sec, gspec], out_specs=[ospec, sspec, tspec],
        out_shape=[jax.ShapeDtypeStruct((bl, s, d), F32), jax.ShapeDtypeStruct((bl, n, heads, HEAD, HEAD), F32),
                   jax.ShapeDtypeStruct((bl, n, heads, CHUNK, CHUNK), F32)],
        scratch_shapes=[pltpu.VMEM((heads, HEAD, HEAD), F32)], compiler_params=_cparams("parallel", "arbitrary"),
    )(qkv, gbeta)


def _gdn_bwd(qkv, gbeta, dout, s_all, t_all, heads, name):
    bl, s, d, n, group, sec, gspec, ospec, sspec, tspec = _gdn_specs(qkv, gbeta, heads, True)
    rows = lambda sub: slice(sub * CHUNK, (sub + 1) * CHUNK)
    pairs = [(h, sub) for h in range(heads) for sub in range(group)]
    stack, side = functools.partial(jnp.concatenate, axis=0), functools.partial(jnp.concatenate, axis=1)

    def body(x_ref, g_ref, do_ref, s_ref, t_ref, dx_ref, dg_ref, ds_ref):
        @pl.when(pl.program_id(1) == 0)
        def _():
            ds_ref[...] = jnp.zeros_like(ds_ref)

        masks = _chunk_masks()
        eye, causal, strict = masks
        gblks = [g_ref[rows(sub), :] for sub in range(group)]
        gcs = [_hdot(jnp.where(causal, 1.0, 0.0), gb) for gb in gblks]
        lane = lax.broadcasted_iota(jnp.int32, (CHUNK, LANE), 1)
        last_row = lax.broadcasted_iota(jnp.int32, (CHUNK, 1), 0) == CHUNK - 1
        rowsum = lambda a: jnp.sum(a, axis=1, keepdims=True)
        st_all, t_all_, ds_all = s_ref[...], t_ref[...], ds_ref[...]

        def free(h, sub):
            q, k, v = (x_ref[rows(sub), sec * d + h * HEAD:sec * d + (h + 1) * HEAD] for sec in range(3))
            do = do_ref[rows(sub), h * HEAD:(h + 1) * HEAD]
            beta = _lane_col(gblks[sub], heads + h)
            st, t = st_all[sub, h], t_all_[sub, h]
            pre = _gdn_pre(q, k, v, _lane_col(gcs[sub], h), beta, masks)
            yield
            uw = _bdot(t, side([pre["vb"], pre["kbe"]]), 1, 0)
            u, w = uw[:, :HEAD], uw[:, HEAD:]
            yield
            vnew = u - _bdot(w, st, 1, 0)
            yield
            dqk = jnp.where(causal, _bdot(do, vnew, 1, 1), 0.0)
            dqg = _bdot(do, st, 1, 1)
            return dict(q=q, k=k, v=v, do=do, beta=beta, st=st, t=t, pre=pre, w=w, vnew=vnew, dqk=dqk, dqg=dqg)

        pieces = dict(zip(pairs, _round_robin([free(h, sub) for h, sub in pairs])))

        def carry(h):
            dsn, outs = ds_all[h], {}
            for sub in reversed(range(group)):
                pc = pieces[h, sub]
                pre, st, do = pc["pre"], pc["st"], pc["do"]
                egl = jnp.exp(pre["gl"])
                dkdec = _bdot(pc["vnew"], dsn, 1, 1)
                dvnew = _bdot(pre["kdec"], dsn, 1, 0) + _bdot(pre["qk"], do, 0, 0)
                dgl = jnp.sum(dsn * st, keepdims=True) * egl
                yield
                dw = -_bdot(dvnew, st, 1, 1)
                dsn = dsn * egl + _bdot(stack([pre["qg"], -pc["w"]]), stack([do, dvnew]), 0, 0)
                outs[sub] = (dkdec, dvnew, dgl, dw)
                yield
            return outs, dsn

        carried = _round_robin([carry(h) for h in range(heads)])

        def rest(h, sub):
            pc = pieces[h, sub]
            dkdec, dvnew, dgl, dw = carried[h][0][sub]
            q, k, v, beta, t, pre, dqk, dqg = (pc[x] for x in ("q", "k", "v", "beta", "t", "pre", "dqk", "dqg"))
            decay, eg, kb, vb, kbe, low, qk, qg, kdec = (pre[x] for x in ("decay", "eg", "kb", "vb", "kbe", "low", "qk", "qg", "kdec"))
            dt = _bdot(side([dvnew, dw]), side([vb, kbe]), 1, 1)
            by_t = _bdot(t, side([dvnew, dw]), 0, 0)
            dvb, dkbe = by_t[:, :HEAD], by_t[:, HEAD:]
            yield
            inner = _bdot(dt, t, 1, 1)
            yield
            dlow = -jnp.where(strict, _bdot(t, inner, 0, 0), 0.0)
            da, db = dlow * decay, dqk * decay
            yield
            m = dlow * low + dqk * qk
            kdk = dkdec * kdec
            col_of_m = jnp.sum(jnp.where(eye, jnp.sum(m, axis=0, keepdims=True), 0.0), axis=1, keepdims=True)
            dgc = rowsum(m) - col_of_m + rowsum(dqg * qg) + rowsum(dkbe * kbe) - rowsum(kdk)
            dgc = dgc + jnp.where(last_row, dgl + jnp.sum(kdk, keepdims=True), 0.0)
            by_k = _bdot(stack([da, db]), k, 1, 0)
            dkb = by_k[:CHUNK] + dkbe * eg
            yield
            dk = _bdot(stack([da, db]), stack([kb, q]), 0, 0) + dkdec * pre["rest"] + dkb * beta
            dq = by_k[CHUNK:] + dqg * eg
            dbeta = rowsum(dkb * k) + rowsum(dvb * v)
            return dq, dk, dvb * beta, jnp.where(lane == h, dgc, 0.0) + jnp.where(lane == heads + h, dbeta, 0.0)

        done = dict(zip(pairs, _round_robin([rest(h, sub) for h, sub in pairs])))
        dx_ref[...] = stack([side([done[h, sub][i] for i in range(3) for h in range(heads)]) for sub in range(group)])
        ds_ref[...] = jnp.stack([carried[h][1] for h in range(heads)])
        upper = jnp.where(jnp.logical_or(eye, jnp.logical_not(causal)), 1.0, 0.0)
        dgs = []
        for sub in range(group):
            dgb = done[0, sub][3]
            for h in range(1, heads):
                dgb = dgb + done[h, sub][3]
            dgs.append(jnp.where(lane < heads, _hdot(upper, dgb), dgb))
        dg_ref[...] = stack(dgs)

    return pl.pallas_call(
        body, name=name, grid=(bl, n // group), in_specs=[sec, gspec, ospec, sspec, tspec], out_specs=[sec, gspec],
        out_shape=[jax.ShapeDtypeStruct(qkv.shape, F32), jax.ShapeDtypeStruct((bl, s, LANE), F32)],
        scratch_shapes=[pltpu.VMEM((heads, HEAD, HEAD), F32)], compiler_params=_cparams("parallel", "arbitrary"),
    )(qkv, gbeta, dout, s_all, t_all)


def _position():
    return lax.axis_index("x"), lax.axis_index("y"), lax.axis_index("c")


def _all_gather(x, *, name):
    space = pltpu.VMEM

    def body(x_ref, out_ref, send_sems, recv_sems, local_sem):
        ax, ay, ac = _position()
        me, sibling = (ax, ay, ac), (ax, ay, 1 - ac)
        chips = [(1 - ax, ay), (ax, 1 - ay), (1 - ax, 1 - ay)]

        def slot(px, py, pc):
            return out_ref.at[4 * px + 2 * py + pc]

        def copy(k, block, to, src=None):
            return pltpu.make_async_remote_copy(
                src_ref=slot(*block) if src is None else src, dst_ref=slot(*block), send_sem=send_sems.at[k],
                recv_sem=recv_sems.at[k], device_id=to, device_id_type=MESH_IDS)

        mine = pltpu.make_async_copy(x_ref, slot(*me), local_sem)
        mine.start()
        first = [copy(0, me, sibling, src=x_ref)] + [copy(1 + j, me, (*chip, ac), src=x_ref) for j, chip in enumerate(chips)]
        for cp in first:
            cp.start()
        passed = [copy(4 + j, (*chip, ac), sibling) for j, chip in enumerate(chips)]
        for j, chip in enumerate(chips):
            copy(1 + j, (*chip, ac), me).wait_recv()
            passed[j].start()
        copy(0, sibling, me).wait_recv()
        for j, chip in enumerate(chips):
            copy(4 + j, (*chip, 1 - ac), me).wait_recv()
        for cp in first + passed:
            cp.wait_send()
        mine.wait()

    return pl.pallas_call(
        body, name=name, out_shape=jax.ShapeDtypeStruct((NDEV,) + x.shape, x.dtype),
        in_specs=[pl.BlockSpec(memory_space=space)], out_specs=pl.BlockSpec(memory_space=space),
        scratch_shapes=[pltpu.SemaphoreType.DMA((7,)), pltpu.SemaphoreType.DMA((7,)), pltpu.SemaphoreType.DMA],
    )(x)


class _Rider:
    def __init__(self, arrays, out_shapes, sems, hooks):
        self.arrays, self.out_shapes, self.sems, self.hooks = arrays, out_shapes, sems, hooks


def _gather_rider(xs):
    n = len(xs)

    def hooks(x_refs, out_refs, send_sems, recv_sems):
        ax, ay, ac = _position()
        me, sibling = (ax, ay, ac), (ax, ay, 1 - ac)
        chips = [(1 - ax, ay), (ax, 1 - ay), (1 - ax, 1 - ay)]

        def copies(k, block, to, own=False):
            out = []
            for i in range(n):
                slot = out_refs[i].at[4 * block[0] + 2 * block[1] + block[2]]
                out.append(pltpu.make_async_remote_copy(
                    src_ref=x_refs[i] if own else slot, dst_ref=slot, send_sem=send_sems.at[k, i], recv_sem=recv_sems.at[k, i],
                    device_id=to, device_id_type=MESH_IDS))
            return out

        def first():
            for cp in copies(0, me, sibling, own=True):
                cp.start()
            for j, chip in enumerate(chips):
                for cp in copies(1 + j, me, (*chip, ac), own=True):
                    cp.start()

        def mid():
            for j, chip in enumerate(chips):
                for arrived, onward in zip(copies(1 + j, (*chip, ac), me), copies(4 + j, (*chip, ac), sibling)):
                    arrived.wait_recv()
                    onward.start()

        def last():
            for cp in copies(0, sibling, me):
                cp.wait_recv()
            for j, chip in enumerate(chips):
                for cp in copies(4 + j, (*chip, 1 - ac), me):
                    cp.wait_recv()
            for cp in copies(0, me, sibling, own=True):
                cp.wait_send()
            for j, chip in enumerate(chips):
                for cp in copies(1 + j, me, (*chip, ac), own=True) + copies(4 + j, (*chip, ac), sibling):
                    cp.wait_send()

        return first, mid, last

    return _Rider(list(xs), [jax.ShapeDtypeStruct((NDEV,) + x.shape, x.dtype) for x in xs],
                  [pltpu.SemaphoreType.DMA((7, n)), pltpu.SemaphoreType.DMA((7, n))], hooks)


def _scatter_rider(parts):
    packed = sum(r for _, r in parts)
    width, dtype = parts[0][0].shape[1], parts[0][0].dtype

    def hooks(g_refs, out_refs, send_sems, recv_sems):
        (recv_ref,) = out_refs
        ax, ay, ac = _position()

        def peer(rel):
            flip = lambda a, bit: 1 - a if rel & bit else a
            return flip(ax, 4), flip(ay, 2), flip(ac, 1)

        def first():
            for rel in range(1, NDEV):
                px, py, pc = peer(rel)
                off = 0
                for g_ref, (_, r) in zip(g_refs, parts):
                    rows = g_ref.at[pl.ds(pl.multiple_of((4 * px + 2 * py + pc) * r, ROW_ALIGN), r)]
                    pltpu.make_async_remote_copy(
                        src_ref=rows, dst_ref=recv_ref.at[rel - 1, pl.ds(off, r)], send_sem=send_sems.at[rel - 1],
                        recv_sem=recv_sems.at[rel - 1], device_id=(px, py, pc), device_id_type=MESH_IDS).start()
                    off += r

        def last():
            for rel in range(1, NDEV):
                slot = recv_ref.at[rel - 1]
                pltpu.make_async_remote_copy(src_ref=slot, dst_ref=slot, send_sem=send_sems.at[rel - 1],
                                             recv_sem=recv_sems.at[rel - 1], device_id=peer(rel), device_id_type=MESH_IDS).wait()

        return first, lambda: None, last

    return _Rider([g for g, _ in parts], [jax.ShapeDtypeStruct((NDEV - 1, packed, width), dtype)],
                  [pltpu.SemaphoreType.DMA((NDEV - 1,)), pltpu.SemaphoreType.DMA((NDEV - 1,))], hooks)


def _sum_direct(own, recv, name):
    r, w = own.shape
    tr = max(t for t in range(ROW_ALIGN, 257, ROW_ALIGN) if r % t == 0)

    def body(own_ref, *refs):
        acc = own_ref[...].astype(F32)
        for ref in refs[:-1]:
            acc = acc + ref[...].astype(F32)
        refs[-1][...] = acc

    rblk = lambda k: pl.BlockSpec((None, tr, w), functools.partial(lambda i, k: (k, i, 0), k=k))
    blk = pl.BlockSpec((tr, w), lambda i: (i, 0))
    return pl.pallas_call(body, name=name, grid=(r // tr,), in_specs=[blk] + [rblk(k) for k in range(NDEV - 1)],
                          out_specs=blk, out_shape=jax.ShapeDtypeStruct((r, w), F32),
                          compiler_params=_cparams("parallel"))(own, *([recv] * (NDEV - 1)))


ROW_ALIGN = 16


def _window_start(rows_per_dev, k):
    return rows_per_dev * k // ROW_ALIGN * ROW_ALIGN


def _exchange_in_chip(parts, name, collective_id):
    packed = sum(win for _, _, win, _ in parts)
    width, dtype = parts[0][0].shape[1], parts[0][0].dtype

    def body(g_refs, out_refs, send_sems, recv_sems):
        (recv_ref,) = out_refs
        ax, ay, ac = _position()
        sibling = (ax, ay, 1 - ac)
        _handshake([sibling])
        for q in range(4):
            for g_ref, (_, r, win, off) in zip(g_refs, parts):
                there = g_ref.at[pl.ds(pl.multiple_of(_window_start(r, 2 * q + 1 - ac), ROW_ALIGN), win)]
                pltpu.make_async_remote_copy(src_ref=there, dst_ref=recv_ref.at[q, pl.ds(off, win)], send_sem=send_sems.at[q],
                                             recv_sem=recv_sems.at[q], device_id=sibling, device_id_type=MESH_IDS).start()
        for q in range(4):
            pltpu.make_async_remote_copy(src_ref=recv_ref.at[q], dst_ref=recv_ref.at[q], send_sem=send_sems.at[q],
                                         recv_sem=recv_sems.at[q], device_id=sibling, device_id_type=MESH_IDS).wait()

    return _on_sequencer(body, [g for g, _, _, _ in parts], [jax.ShapeDtypeStruct((4, packed, width), dtype)],
                         [pltpu.SemaphoreType.DMA((4,)), pltpu.SemaphoreType.DMA((4,))], name=name, collective_id=collective_id)[0]


def _on_sequencer(body, ins, out_shapes, sems, *, name, collective_id):
    hbm = pltpu.MemorySpace.HBM
    in_refs = [jax.new_ref(a, memory_space=hbm) for a in ins]
    out_refs = [jax.empty_ref(s, memory_space=hbm) for s in out_shapes]

    @pl.kernel(mesh=plsc.ScalarSubcoreMesh(axis_name="sequencer", num_cores=1), name=name, scratch_types=tuple(sems),
               compiler_params=pltpu.CompilerParams(collective_id=collective_id))
    def launch(*sem_refs):
        body(in_refs, out_refs, *sem_refs)

    launch()
    return [r[...] for r in out_refs]


def _handshake(peers):
    barrier = pltpu.get_barrier_semaphore()
    for peer in peers:
        pl.semaphore_signal(barrier, inc=1, device_id=peer, device_id_type=MESH_IDS)
    pl.semaphore_wait(barrier, len(peers))


def _exchange_chips_async(s1, name, collective_id):
    def body(in_refs, out_refs, send_sems, recv_sems):
        (src,), (got,) = in_refs, out_refs
        ax, ay, ac = _position()
        chips = [(1 - ax, ay), (ax, 1 - ay), (1 - ax, 1 - ay)]
        _handshake([(cx, cy, ac) for cx, cy in chips])
        copies = [pltpu.make_async_remote_copy(
            src_ref=src.at[2 * cx + cy], dst_ref=got.at[r], send_sem=send_sems.at[r], recv_sem=recv_sems.at[r],
            device_id=(cx, cy, ac), device_id_type=MESH_IDS) for r, (cx, cy) in enumerate(chips)]
        for cp in copies:
            cp.start()
        for cp in copies:
            cp.wait_recv()
        for cp in copies:
            cp.wait_send()

    return _on_sequencer(body, [s1], [jax.ShapeDtypeStruct((3,) + s1.shape[1:], s1.dtype)],
                         [pltpu.SemaphoreType.DMA((3,)), pltpu.SemaphoreType.DMA((3,))], name=name, collective_id=collective_id)[0]


def _gather_async(xs, name, collective_id):
    rider = _gather_rider(xs)

    def body(in_refs, out_refs, send_sems, recv_sems):
        ax, ay, ac = _position()
        _handshake([(ax, ay, 1 - ac), (1 - ax, ay, ac), (ax, 1 - ay, ac), (1 - ax, 1 - ay, ac)])
        for hook in rider.hooks(in_refs, out_refs, send_sems, recv_sems):
            hook()

    return _on_sequencer(body, rider.arrays, rider.out_shapes, rider.sems, name=name, collective_id=collective_id)


def _gather_balanced(x, name, collective_id):
    m = x.shape[0]
    half = m // 2 // ROW_ALIGN * ROW_ALIGN
    parts = {"all": pl.ds(0, m), "lo": pl.ds(0, half), "hi": pl.ds(half, m - half)}

    def body(in_refs, out_refs, send_sems, recv_sems):
        (x_ref,), (out_ref,) = in_refs, out_refs
        ax, ay, ac = _position()
        me, sibling = (ax, ay, ac), (ax, ay, 1 - ac)
        by_x, by_y, diag = (1 - ax, ay), (ax, 1 - ay), (1 - ax, 1 - ay)
        _handshake([sibling, (*by_x, ac), (*by_y, ac)])

        def copy(k, block, part, to, own=False):
            rows = out_ref.at[4 * block[0] + 2 * block[1] + block[2], parts[part]]
            return pltpu.make_async_remote_copy(src_ref=x_ref if own else rows, dst_ref=rows, send_sem=send_sems.at[k],
                                                recv_sem=recv_sems.at[k], device_id=to, device_id_type=MESH_IDS)

        def own_half(k, part, to):
            rows = out_ref.at[4 * ax + 2 * ay + ac, parts[part]]
            return pltpu.make_async_remote_copy(src_ref=x_ref.at[parts[part]], dst_ref=rows, send_sem=send_sems.at[k],
                                                recv_sem=recv_sems.at[k], device_id=to, device_id_type=MESH_IDS)

        nx, ny, nd = (*by_x, ac), (*by_y, ac), (*diag, ac)
        sends = [copy(0, me, "all", sibling, own=True), own_half(1, "lo", nx), own_half(2, "hi", nx),
                 own_half(3, "hi", ny), own_half(4, "lo", ny), copy(5, nx, "lo", ny), copy(6, ny, "hi", nx),
                 copy(7, nx, "lo", sibling), copy(8, nx, "hi", sibling), copy(9, ny, "hi", sibling),
                 copy(10, ny, "lo", sibling), copy(11, nd, "lo", sibling), copy(12, nd, "hi", sibling)]
        sx, sy, sd = (*by_x, 1 - ac), (*by_y, 1 - ac), (*diag, 1 - ac)
        arrivals = [copy(0, sibling, "all", me), copy(1, nx, "lo", me), copy(2, nx, "hi", me), copy(3, ny, "hi", me),
                    copy(4, ny, "lo", me), copy(5, nd, "lo", me), copy(6, nd, "hi", me), copy(7, sx, "lo", me),
                    copy(8, sx, "hi", me), copy(9, sy, "hi", me), copy(10, sy, "lo", me), copy(11, sd, "lo", me),
                    copy(12, sd, "hi", me)]
        for k in range(5):
            sends[k].start()
        for arrived, onward in ((1, (5, 7)), (3, (6, 9)), (2, (8,)), (4, (10,)), (5, (11,)), (6, (12,))):
            arrivals[arrived].wait_recv()
            for k in onward:
                sends[k].start()
        for k in (0, 7, 8, 9, 10, 11, 12):
            arrivals[k].wait_recv()
        for cp in sends:
            cp.wait_send()

    return _on_sequencer(body, [x], [jax.ShapeDtypeStruct((NDEV,) + x.shape, x.dtype)],
                         [pltpu.SemaphoreType.DMA((13,)), pltpu.SemaphoreType.DMA((13,))], name=name, collective_id=collective_id)[0]


def _scatter_async(parts, name, collective_id):
    rider = _scatter_rider(parts)

    def body(in_refs, out_refs, send_sems, recv_sems):
        ax, ay, ac = _position()
        flip = lambda a, on: 1 - a if on else a
        _handshake([(flip(ax, rel & 4), flip(ay, rel & 2), flip(ac, rel & 1)) for rel in range(1, NDEV)])
        for hook in rider.hooks(in_refs, out_refs, send_sems, recv_sems):
            hook()

    return _on_sequencer(body, rider.arrays, rider.out_shapes, rider.sems, name=name, collective_id=collective_id)[0]


def _sum_in_chip(own, recv, name):
    _, r, w = own.shape
    tr = _tile(r, (256, 128))

    def body(a_ref, b_ref, o_ref):
        o_ref[...] = (a_ref[...].astype(F32) + b_ref[...].astype(F32)).astype(o_ref.dtype)

    blk = pl.BlockSpec((None, tr, w), lambda q, i: (q, i, 0))
    return pl.pallas_call(body, name=name, grid=(4, r // tr), in_specs=[blk, blk], out_specs=blk,
                          out_shape=jax.ShapeDtypeStruct(own.shape, own.dtype),
                          compiler_params=_cparams("parallel", "parallel"))(own, recv)


def _sum_chips(s1, recv, chip, name):
    _, r, w = s1.shape
    tr = _tile(r, (256, 128))

    def body(c_ref, s_ref, r0_ref, r1_ref, r2_ref, o_ref):
        f = lambda ref: ref[...].astype(F32)
        o_ref[...] = ((f(s_ref) + f(r0_ref)) + f(r1_ref)) + f(r2_ref)

    rblk = lambda k: pl.BlockSpec((None, tr, w), functools.partial(lambda i, c, k: (k, i, 0), k=k))
    grid_spec = pltpu.PrefetchScalarGridSpec(
        num_scalar_prefetch=1, grid=(r // tr,),
        in_specs=[pl.BlockSpec((None, tr, w), lambda i, c: (c[0], i, 0)), rblk(0), rblk(1), rblk(2)],
        out_specs=pl.BlockSpec((tr, w), lambda i, c: (i, 0)))
    return pl.pallas_call(body, name=name, grid_spec=grid_spec, out_shape=jax.ShapeDtypeStruct((r, w), F32),
                          compiler_params=_cparams("parallel"))(chip, s1, recv, recv, recv)


def _silu_rows(x, name):
    def body(x_ref, o_ref):
        o_ref[...] = _silu(x_ref[...])

    return pl.pallas_call(body, name=name, out_shape=jax.ShapeDtypeStruct(x.shape, F32))(x)


def _row_sum(x, name):
    def body(x_ref, o_ref):
        acc = x_ref[0:1, :]
        for i in range(1, x.shape[0]):
            acc = acc + x_ref[i:i + 1, :]
        o_ref[...] = acc

    return pl.pallas_call(body, name=name, out_shape=jax.ShapeDtypeStruct((1, x.shape[1]), F32))(x)


def _adamw(w, g, m, v, name):
    cols = w.shape[-1]
    rows = w.size // cols
    tr = _tile(rows, (256, 128))
    tc = LANE if (tr == rows and rows > 512 and cols % LANE == 0) else cols

    def body(w_ref, g_ref, m_ref, v_ref, d_ref, mo_ref, vo_ref):
        grad = g_ref[...]
        m_new = ADAM_B1 * m_ref[...] + (1.0 - ADAM_B1) * grad
        v_new = ADAM_B2 * v_ref[...] + (1.0 - ADAM_B2) * jnp.square(grad)
        m_hat = m_new / (1.0 - ADAM_B1 ** ADAM_STEP)
        v_hat = v_new / (1.0 - ADAM_B2 ** ADAM_STEP)
        d_ref[...] = -ADAM_LR * (m_hat / (jnp.sqrt(v_hat) + ADAM_EPS) + ADAM_WD * w_ref[...])
        mo_ref[...] = m_new
        vo_ref[...] = v_new

    blk = pl.BlockSpec((tr, tc), lambda i, j: (i, j))
    out = pl.pallas_call(
        body, name=name, grid=(rows // tr, cols // tc), in_specs=[blk] * 4, out_specs=[blk] * 3,
        out_shape=[jax.ShapeDtypeStruct((rows, cols), F32)] * 3, compiler_params=_cparams("parallel", "parallel"),
    )(*[t.reshape(rows, cols) for t in (w, g, m, v)])
    return [t.reshape(w.shape) for t in out]


def _pack(parts, width, row_mult, dtype):
    flat = jnp.concatenate([p.reshape(-1).astype(dtype) for p in parts])
    rows = -(-flat.shape[0] // (width * row_mult)) * row_mult
    return jnp.pad(flat, (0, rows * width - flat.shape[0])).reshape(rows, width)


def _unpack(flat, shapes):
    out, off = [], 0
    for shp in shapes:
        size = 1
        for dim in shp:
            size *= dim
        out.append(flat[:, off:off + size].reshape((flat.shape[0],) + tuple(shp)))
        off += size
    return out


def _devices_to_cols(a):
    _, r, c = a.shape
    return a.transpose(1, 0, 2).reshape(r, NDEV * c)


def kernel(x, c, w_ada, b_ada, norm1_w, w_in, gdn_conv_w, gdn_a_log, gdn_dt_bias, gdn_norm_w, w_gdn_proj, sc_conv_w, w_sc_out, w_o, norm2_w, w_ffn_in, w_ffn_out, w_ada_f, b_ada_f, normf_w, loss_target, m_w_ada, m_b_ada, m_norm1_w, m_w_in, m_gdn_conv_w, m_gdn_a_log, m_gdn_dt_bias, m_gdn_norm_w, m_w_gdn_proj, m_sc_conv_w, m_w_sc_out, m_w_o, m_norm2_w, m_w_ffn_in, m_w_ffn_out, m_w_ada_f, m_b_ada_f, m_normf_w, v_w_ada, v_b_ada, v_norm1_w, v_w_in, v_gdn_conv_w, v_gdn_a_log, v_gdn_dt_bias, v_gdn_norm_w, v_w_gdn_proj, v_sc_conv_w, v_w_sc_out, v_w_o, v_norm2_w, v_w_ffn_in, v_w_ffn_out, v_w_ada_f, v_b_ada_f, v_normf_w):
    bl, s, d = x.shape
    heads = gdn_a_log.shape[-1]
    dff = w_ffn_out.shape[1] * NDEV
    tok = bl * s
    ax, ay, ac = _position()
    dev = 4 * ax + 2 * ay + ac
    as_tok = lambda a: a.reshape(bl, s, a.shape[-1])
    as_mat = lambda a: a.reshape(tok, a.shape[-1])

    small = _all_gather(_pack([c, gdn_conv_w, sc_conv_w], LANE, 8, F32), name="gather_cond")
    c_all, conv_w, sc_w = _unpack(small.reshape(NDEV, -1), [(bl, d), gdn_conv_w.shape[1:], sc_conv_w.shape[1:]])
    c_act = _silu_rows(c_all.reshape(NDEV * bl, d), "cond_silu")
    conv_w, sc_w = _devices_to_cols(conv_w), _devices_to_cols(sc_w)
    n_ada, n_adaf = w_ada.shape[-1], w_ada_f.shape[-1]
    bias = jnp.broadcast_to(lax.dynamic_slice_in_dim(b_ada, dev * n_ada, n_ada, axis=1), (NDEV * bl, n_ada))
    biasf = jnp.broadcast_to(lax.dynamic_slice_in_dim(b_ada_f.reshape(1, -1), dev * n_adaf, n_adaf, axis=1), (NDEV * bl, n_adaf))
    mod_cols = _mm(c_act, w_ada[0], add=bias, name="ada_cols")
    modf_cols = _mm(c_act, w_ada_f, add=biasf, name="adaf_cols")
    mods = _all_gather(jnp.concatenate([mod_cols, modf_cols], axis=1), name="gather_mod")
    mod_all = mods[:, :, :n_ada].transpose(1, 0, 2).reshape(NDEV * bl, NDEV * n_ada)
    modf_all = mods[:, :, n_ada:].transpose(1, 0, 2).reshape(NDEV * bl, NDEV * n_adaf)
    my_rows = lambda a: lax.dynamic_slice_in_dim(a, dev * bl, bl, axis=0)
    sh1, sc1, g1, sh2, sc2, g2 = [t.reshape(bl, 1, d) for t in jnp.split(my_rows(mod_all), 6, axis=1)]
    shf, scf = [t.reshape(bl, 1, d) for t in jnp.split(my_rows(modf_all), 2, axis=1)]

    late = [t.astype(MXU_DTYPE) for t in (w_gdn_proj[0], w_sc_out[0], w_o[0], w_ffn_in[0].T, w_ffn_out[0])]
    rows = [t.shape[0] for t in late] + [w_in.shape[-1]]
    offs = [sum(rows[:i]) for i in range(5)]
    in_send = w_in[0].T.astype(MXU_DTYPE)
    with_own = lambda g, own: lax.dynamic_update_slice_in_dim(g, own[None], dev, axis=0)
    wt_in = with_own(_gather_balanced(in_send, "gather_w_in", 1), in_send).reshape(NDEV * rows[5], d)
    gathered = _gather_async(late[:3], "gather_mixer", 2) + _gather_async(late[3:], "gather_ffn", 3)
    wgp, wso, wo, wt_fi, wfo = [with_own(g, own).reshape(NDEV * own.shape[0], d) for g, own in zip(gathered, late)]
    o_z, o_ab, o_sc, o_ga, o_gb = 3 * d, 4 * d, 4 * d + 2 * heads, 7 * d + 2 * heads, 8 * d + 2 * heads
    s_qkv, s_z, s_sc, s_gate = (0, o_z), (o_z, d), (o_sc, 3 * d), (o_ga, 2 * d)
    wt_ab = jnp.pad(wt_in[o_ab:o_sc], ((0, LANE - 2 * heads), (0, 0)))

    n1w, n2w, nfw = norm1_w.reshape(1, d), norm2_w.reshape(1, d), normf_w.reshape(1, d)
    lanes = lambda a: jnp.pad(a.reshape(1, -1), ((0, 0), (0, LANE - a.size)))
    a_log, dt_bias, gnw = lanes(gdn_a_log), lanes(gdn_dt_bias), gdn_norm_w.reshape(1, HEAD)
    f_gates = functools.partial(_f_gates, heads=heads)
    (h1,) = _tok_fwd(_f_norm_mod, [x], [sh1, sc1], [n1w], [(d, MXU_DTYPE)], name="norm1", ts=512)
    h1m = as_mat(h1)
    p_qkv = as_tok(_mm(h1m, wt_in, tb=True, b_rows=s_qkv, name="in_qkv"))
    p_z = as_tok(_mm(h1m, wt_in, tb=True, b_rows=s_z, name="in_z"))
    p_ab = as_tok(_mm(h1m, wt_ab, tb=True, name="in_ab"))
    p_sc = as_tok(_mm(h1m, wt_in, tb=True, b_rows=s_sc, name="in_sc"))
    p_g = as_tok(_mm(h1m, wt_in, tb=True, b_rows=s_gate, name="in_gate"))
    qkv = _qkv_fwd(p_qkv, conv_w, heads, "qkv_conv")
    (gbeta,) = _tok_fwd(f_gates, [p_ab], [], [a_log, dt_bias], [(LANE, F32)], name="gates", ts=512)
    o, s_all, t_all = _gdn_fwd(qkv, gbeta, heads, "gdn")
    (og,) = _tok_fwd(_f_gdn_out, [o, p_z], [], [(gnw, None)], [(d, MXU_DTYPE)], name="gdn_out", ts=2048, wb=HEAD, cols=heads)
    y_a = as_tok(_mm(as_mat(og), wgp, name="gdn_proj"))
    scp = _sc_fwd(p_sc, sc_w, "sc_conv")
    mrg, y_b = _tok_fwd(_f_merge_keep, [(p_g, 0), (p_g, 1), y_a, _Product(scp, wso)], [], [], [(d, MXU_DTYPE), (d, F32)],
                        name="merge", ts=512, wb=d)
    merge_toks = [(p_g, 0), (p_g, 1), y_a, y_b]
    x2, h2, mix = _tok_fwd(_f_res_norm_mod_keep, [x, _Product(mrg, wo)], [g1, sh2, sc2], [n2w],
                           [(d, F32), (d, MXU_DTYPE), (d, F32)], name="norm2", ts=512)
    act, gu_a, gu_b = _ffn_in_swiglu(as_mat(h2), wt_fi, dff, "ffn_in")

    loss_l, (dx2, dff_out, _), (dg2, dshf, dscf), (dnfw,) = _tok_bwd(
        _f_loss, [x2, _Product(as_tok(act), wfo), loss_target], [g2, shf, scf], [nfw], [], [True, True, False], name="loss",
        ts=512, loss=True, tok_dtype=[F32, MXU_DTYPE, None])
    dffm = as_mat(dff_out)
    dgu_a, dgu_b = _ffn_out_bwd_swiglu(dffm, wfo, gu_a, gu_b, "d_ffn_out")
    gmm = functools.partial(_mm, ta=True, out_dtype=MXU_DTYPE)
    gw_ffn_out = gmm(act, dffm, name="g_ffn_out")
    dh2 = _Product(as_tok(dgu_b), wt_fi, b_rows=(dff, dff), add=_Product(as_tok(dgu_a), wt_fi, b_rows=(0, dff)))
    h2m = as_mat(h2)
    gwt_ffn_in = gmm(dgu_a, h2m, out_rows=2 * dff, name="g_ffn_in_a")
    gwt_ffn_in = gmm(dgu_b, h2m, out_rows=2 * dff, row_off=dff, into=gwt_ffn_in, name="g_ffn_in_b")
    ffn_parts = [(gwt_ffn_in, rows[3]), (gw_ffn_out, rows[4])]
    ffn_recv = _scatter_async(ffn_parts, "scatter_ffn", 4)
    (dx_skip, dmix), (dg1, dsh2, dsc2), (dn2w,) = _tok_bwd(
        _f_res_norm_mod, [x, mix], [g1, sh2, sc2], [n2w], [dx2, dh2], [True, True], name="d_norm2", ts=512,
        tok_dtype=[F32, MXU_DTYPE], after=[gwt_ffn_in, gw_ffn_out])
    gw_o = gmm(as_mat(mrg), as_mat(dmix), name="g_mix_out")
    (dga, dgb, dya, dyb), _, _ = _tok_bwd(_f_merge, merge_toks, [], [], [_Product(dmix, wo, tb=True)], [True] * 4,
                                          name="d_merge", ts=512, wb=d, tok_dtype=MXU_DTYPE)
    dyam, dybm = as_mat(dya), as_mat(dyb)
    dog = as_tok(_mm(dyam, wgp, tb=True, name="d_gdn_proj"))
    gw_gdn_proj = gmm(as_mat(og), dyam, name="g_gdn_proj")
    dscp = as_tok(_mm(dybm, wso, tb=True, name="d_sc_out"))
    gw_sc_out = gmm(as_mat(scp), dybm, name="g_sc_out")
    dsc, g_sc_w = _sc_bwd(p_sc, sc_w, dscp, "d_sc_conv")
    mix_parts = [(gw_gdn_proj, rows[0]), (gw_sc_out, rows[1]), (gw_o, rows[2])]
    mix_recv = _scatter_async(mix_parts, "scatter_mixer", 5)
    (do, dz), _, (g_gnw,) = _tok_bwd(_f_gdn_out, [o, p_z], [], [(gnw, None)], [dog], [True, True], name="d_gdn_out",
                                     ts=2048, wb=HEAD, cols=heads, tok_dtype=[F32, MXU_DTYPE],
                                     after=[gw_gdn_proj, gw_sc_out, gw_o])
    own_rows = lambda parts: jnp.concatenate([lax.dynamic_slice_in_dim(g, dev * r, r, axis=0) for g, r in parts], axis=0)
    dqkv, dgbeta = _gdn_bwd(qkv, gbeta, do, s_all, t_all, heads, "d_gdn")
    dp_qkv, g_conv_w = _qkv_bwd(p_qkv, conv_w, dqkv, heads, "d_qkv_conv")
    ffn_red = _sum_direct(own_rows(ffn_parts), ffn_recv, "sum_ffn")
    mix_red = _sum_direct(own_rows(mix_parts), mix_recv, "sum_mix")
    (dp_ab,), _, (g_a_log, g_dt_bias) = _tok_bwd(f_gates, [p_ab], [], [a_log, dt_bias], [dgbeta], [True], name="d_gates",
                                                 ts=512, tok_dtype=MXU_DTYPE, after=[ffn_red, mix_red])
    r_in = rows[5]
    win = -(-(r_in + max(r_in * k % ROW_ALIGN for k in range(NDEV))) // 128) * 128
    need_rows = max(_window_start(r_in, k) for k in range(NDEV)) + win
    dsc_m = dsc.reshape(3, tok, d)
    gwt_in = ([gmm(as_mat(dp_qkv), h1m, name="g_in_qkv"), gmm(as_mat(dz), h1m, name="g_in_z"),
               gmm(as_mat(dp_ab), h1m, name="g_in_ab")[:2 * heads]]
              + [gmm(dsc_m, h1m, a_index=k, name=f"g_in_sc{k}") for k in range(3)]
              + [gmm(as_mat(dga), h1m, name="g_in_ga"), gmm(as_mat(dgb), h1m, name="g_in_gb")])
    gwt_in = jnp.concatenate(gwt_in + [jnp.zeros((need_rows - NDEV * r_in, d), MXU_DTYPE)], axis=0)
    assert d <= 1024
    wide = [as_mat(dp_qkv), as_mat(dz), dsc_m, as_mat(dga)]
    row_of = lambda t: d * t + jnp.where(t * d >= o_ab, 2 * heads, 0)
    recv1 = _exchange_in_chip([(gwt_in, r_in, win, 0)], "scatter_in_chip", 7)
    own = jnp.stack([lax.dynamic_slice_in_dim(gwt_in, _window_start(r_in, 2 * q + ac), win, axis=0) for q in range(4)])
    s1 = _sum_in_chip(own, recv1, "sum_in_chip")
    recv2 = _exchange_chips_async(s1, "scatter_chips", 6)

    dh1 = _mm(as_mat(dp_ab), wt_ab, name="d_in_ab")
    dh1 = _mm_chain(wide, wt_in, row_of, add=dh1, name="d_in", tk=d)
    dh1 = _Product(dgb, wt_in, b_rows=(o_gb, d), add=as_tok(dh1))
    (grad_x,), (dsh1, dsc1), (dn1w,) = _tok_bwd(_f_norm_mod_skip, [x], [sh1, sc1], [n1w], [dh1, dx_skip], [True],
                                                name="d_norm1", ts=512)
    reduced = _sum_chips(s1, recv2, (2 * ax + ay).reshape(1).astype(jnp.int32), "sum_chips")
    gt_w_in = lax.dynamic_slice_in_dim(reduced, r_in * dev - _window_start(r_in, dev), r_in, axis=0)
    g_w_in = gt_w_in.T.reshape(w_in.shape)
    gt_w_ffn_in = ffn_red[:rows[3]]
    g_w_ffn_in = gt_w_ffn_in.T.reshape(w_ffn_in.shape)
    g_w_ffn_out = ffn_red[rows[3]:].reshape(w_ffn_out.shape)
    g_w_gdn_proj, g_w_sc_out, g_w_o = (mix_red[offs[i]:offs[i] + rows[i]].reshape(ref.shape)
                                       for i, ref in enumerate((w_gdn_proj, w_sc_out, w_o)))

    dmod = jnp.concatenate([t.reshape(bl, d) for t in (dsh1, dsc1, dg1, dsh2, dsc2, dg2)], axis=1)
    dmodf = jnp.concatenate([t.reshape(bl, d) for t in (dshf, dscf)], axis=1)
    summed_parts = [dn1w, dn2w, dnfw, g_gnw, g_a_log, g_dt_bias, g_conv_w, g_sc_w, loss_l]
    partial = _all_gather(_pack([dmod, dmodf] + summed_parts, LANE, 8, F32), name="gather_small")
    partial = partial.reshape(NDEV, -1)
    n_rows = bl * (6 * d + 2 * d)
    dmod_all, dmodf_all = _unpack(partial[:, :n_rows], [(bl, 6 * d), (bl, 2 * d)])
    dmod_all, dmodf_all = dmod_all.reshape(NDEV * bl, 6 * d), dmodf_all.reshape(NDEV * bl, 2 * d)
    totals = _row_sum(partial[:, n_rows:], "sum_small")
    t_n1w, t_n2w, t_nfw, t_gnw, t_a_log, t_dt_bias, t_conv_w, t_sc_w, t_loss = [
        t[0] for t in _unpack(totals, [p.shape for p in summed_parts])]
    my_cols = lambda a, n: lax.dynamic_slice_in_dim(a, dev * n, n, axis=1)
    grads = {
        "w_ada": _mm(c_act, my_cols(dmod_all, n_ada), ta=True, name="g_ada").reshape(w_ada.shape),
        "b_ada": _row_sum(dmod_all, "g_ada_bias").reshape(b_ada.shape),
        "norm1_w": t_n1w.reshape(norm1_w.shape),
        "w_in": g_w_in,
        "gdn_conv_w": my_cols(t_conv_w, gdn_conv_w.shape[-1]).reshape(gdn_conv_w.shape),
        "gdn_a_log": t_a_log[:, :heads].reshape(gdn_a_log.shape),
        "gdn_dt_bias": t_dt_bias[:, :heads].reshape(gdn_dt_bias.shape),
        "gdn_norm_w": t_gnw.reshape(gdn_norm_w.shape),
        "w_gdn_proj": g_w_gdn_proj,
        "sc_conv_w": my_cols(t_sc_w, sc_conv_w.shape[-1]).reshape(sc_conv_w.shape),
        "w_sc_out": g_w_sc_out,
        "w_o": g_w_o,
        "norm2_w": t_n2w.reshape(norm2_w.shape),
        "w_ffn_in": g_w_ffn_in,
        "w_ffn_out": g_w_ffn_out,
        "w_ada_f": _mm(c_act, my_cols(dmodf_all, n_adaf), ta=True, name="g_adaf").reshape(w_ada_f.shape),
        "b_ada_f": _row_sum(dmodf_all, "g_adaf_bias").reshape(b_ada_f.shape),
        "normf_w": t_nfw.reshape(normf_w.shape),
    }
    weights = dict(w_ada=w_ada, b_ada=b_ada, norm1_w=norm1_w, w_in=w_in, gdn_conv_w=gdn_conv_w, gdn_a_log=gdn_a_log,
                   gdn_dt_bias=gdn_dt_bias, gdn_norm_w=gdn_norm_w, w_gdn_proj=w_gdn_proj, sc_conv_w=sc_conv_w,
                   w_sc_out=w_sc_out, w_o=w_o, norm2_w=norm2_w, w_ffn_in=w_ffn_in, w_ffn_out=w_ffn_out, w_ada_f=w_ada_f,
                   b_ada_f=b_ada_f, normf_w=normf_w)
    m_in = [m_w_ada, m_b_ada, m_norm1_w, m_w_in, m_gdn_conv_w, m_gdn_a_log, m_gdn_dt_bias, m_gdn_norm_w, m_w_gdn_proj,
            m_sc_conv_w, m_w_sc_out, m_w_o, m_norm2_w, m_w_ffn_in, m_w_ffn_out, m_w_ada_f, m_b_ada_f, m_normf_w]
    v_in = [v_w_ada, v_b_ada, v_norm1_w, v_w_in, v_gdn_conv_w, v_gdn_a_log, v_gdn_dt_bias, v_gdn_norm_w, v_w_gdn_proj,
            v_sc_conv_w, v_w_sc_out, v_w_o, v_norm2_w, v_w_ffn_in, v_w_ffn_out, v_w_ada_f, v_b_ada_f, v_normf_w]
    deltas, new_m, new_v = [], [], []
    grads_t = {"w_in": gt_w_in, "w_ffn_in": gt_w_ffn_in}
    for (wname, wt), mt, vt in zip(weights.items(), m_in, v_in):
        if wname in grads_t:
            back = lambda a, wt=wt: a.T.reshape(wt.shape)
            dl, mn, vn = (back(a) for a in _adamw(wt[0].T, grads_t[wname], mt[0].T, vt[0].T, "adamw_" + wname))
        else:
            dl, mn, vn = _adamw(wt, grads[wname], mt, vt, "adamw_" + wname)
        deltas.append(dl)
        new_m.append(mn)
        new_v.append(vn)
    loss = t_loss[0, 0]
    return (loss, grad_x, *[grads[k] for k in weights], *deltas, *new_m, *new_v)
```

```python
import functools

import jax
import jax.numpy as jnp
from jax import lax
from jax.experimental import pallas as pl
from jax.experimental.pallas import tpu as pltpu
from jax.experimental.pallas import tpu_sc as plsc

F32 = jnp.float32
MXU_DTYPE = jnp.bfloat16
NDEV = 8
CHUNK = 64
HEAD = 128
LANE = 128
EPS = 1e-6
ADAM_LR, ADAM_B1, ADAM_B2, ADAM_EPS, ADAM_WD, ADAM_STEP = 0.001, 0.9, 0.999, 1e-08, 0.01, 10
VMEM_LIMIT = 48 * 1024 * 1024
MESH_IDS = pl.DeviceIdType.MESH
HIGHEST = lax.Precision.HIGHEST


def _tile(n, cands=(512, 256, 128)):
    for c in cands:
        if n % c == 0:
            return c
    return n


def _cparams(*sem):
    return pltpu.CompilerParams(dimension_semantics=sem, vmem_limit_bytes=VMEM_LIMIT)


def _mm(a, b, *, ta=False, tb=False, add=None, out_dtype=F32, name, b_rows=None, out_rows=None, row_off=0, into=None,
        a_index=None):
    m, k = (a.shape[-1], a.shape[-2]) if ta else a.shape[-2:]
    b_shape = b.shape if b_rows is None else (b_rows[1], b.shape[1])
    n = b_shape[0] if tb else b_shape[1]
    assert k == (b_shape[1] if tb else b_shape[0])
    if ta:
        tn = n if n <= 1024 else _tile(n)
        if m <= 1024:
            tm, tk = m, _tile(k, (1024, 512))
        else:
            tm = _tile(m)
            tk = k if k <= 4096 else _tile(k, (4096, 2048, 1024, 512))
            if tm * tk > 1024 * 2048:
                tk = _tile(k, (2048, 1024, 512))
    else:
        tk = k if k <= 1024 else _tile(k, (1024, 512))
        tn = _tile(n, (1024 if tk <= 1024 else 512, 512, 256, 128))
        tm = _tile(m, (2048 if (tn <= 512 and tk <= 1024) else 1024, 1024, 512, 256, 128))
    nk = k // tk
    dims = (((0 if ta else 1,), (1 if tb else 0,)), ((), ()))
    has_add = add is not None

    def body(*refs):
        a_ref, b_ref = refs[0], refs[1]
        add_ref = refs[2] if has_add else None
        o_ref = refs[2 + has_add + (into is not None)]
        part = lax.dot_general(a_ref[...].astype(MXU_DTYPE), b_ref[...].astype(MXU_DTYPE), dims,
                               preferred_element_type=F32)

        def finish(acc):
            if has_add:
                acc = acc + add_ref[...]
            o_ref[...] = acc.astype(o_ref.dtype)

        if nk == 1:
            finish(part)
        else:
            acc_ref = refs[-1]
            kk = pl.program_id(2)

            @pl.when(kk == 0)
            def _():
                acc_ref[...] = part

            @pl.when(kk > 0)
            def _():
                acc_ref[...] += part

            @pl.when(kk == nk - 1)
            def _():
                finish(acc_ref[...])

    a_blk, a_at = ((tk, tm), lambda i, j, kk: (kk, i)) if ta else ((tm, tk), lambda i, j, kk: (i, kk))
    a_spec = (pl.BlockSpec(a_blk, a_at) if a_index is None else
              pl.BlockSpec((None,) + a_blk, lambda i, j, kk: (a_index,) + a_at(i, j, kk)))
    if b_rows is None:
        b_spec = pl.BlockSpec((tn, tk), lambda i, j, kk: (j, kk)) if tb else pl.BlockSpec((tk, tn), lambda i, j, kk: (kk, j))
    else:
        at = lambda t: pl.multiple_of(b_rows[0] + t, ROW_ALIGN)
        b_spec = (pl.BlockSpec((pl.Element(tn), pl.Element(tk)), lambda i, j, kk: (at(j * tn), kk * tk)) if tb else
                  pl.BlockSpec((pl.Element(tk), pl.Element(tn)), lambda i, j, kk: (at(kk * tk), j * tn)))
    add_spec = pl.BlockSpec((tm, tn), lambda i, j, kk: (i, j))
    assert row_off % tm == 0
    o_spec = pl.BlockSpec((tm, tn), lambda i, j, kk: (i + row_off // tm, j))
    in_specs = [a_spec, b_spec] + ([add_spec] if has_add else []) + ([pl.BlockSpec(memory_space=pl.ANY)] if into is not None else [])
    args = [a, b] + ([add] if has_add else []) + ([into] if into is not None else [])
    return pl.pallas_call(
        body, name=name, grid=(m // tm, n // tn, nk), in_specs=in_specs, out_specs=o_spec,
        out_shape=jax.ShapeDtypeStruct((out_rows or m, n), out_dtype),
        scratch_shapes=[pltpu.VMEM((tm, tn), F32)] if nk > 1 else [],
        input_output_aliases={len(args) - 1: 0} if into is not None else {},
        compiler_params=_cparams("parallel", "parallel", "arbitrary"),
    )(*args)


def _mm_chain(parts, b, row_of_tile, *, add, name, tk=1024, tm=1024):
    m, n = parts[0].shape[-2], b.shape[1]
    tm = min(tm, m)
    tiles = [p.shape[0] if p.ndim == 3 else p.shape[1] // tk for p in parts]
    first = [sum(tiles[:s]) for s in range(len(parts))]
    nk = sum(tiles)

    def body(*refs):
        a_refs, b_ref, add_ref, o_ref, acc_ref = refs[:len(parts)], *refs[len(parts):]
        kk = pl.program_id(1)

        @pl.when(kk == 0)
        def _():
            acc_ref[...] = add_ref[...]

        for a_ref, lo, cnt in zip(a_refs, first, tiles):
            @pl.when(jnp.logical_and(kk >= lo, kk < lo + cnt))
            def _(a_ref=a_ref):
                acc_ref[...] += lax.dot_general(a_ref[...].astype(MXU_DTYPE), b_ref[...].astype(MXU_DTYPE),
                                                (((1,), (0,)), ((), ())), preferred_element_type=F32)

        @pl.when(kk == nk - 1)
        def _():
            o_ref[...] = acc_ref[...]

    tile_of = lambda kk, lo, cnt: jnp.clip(kk - lo, 0, cnt - 1)
    a_specs = [pl.BlockSpec((None, tm, tk), functools.partial(lambda i, kk, lo, cnt: (tile_of(kk, lo, cnt), i, 0), lo=lo, cnt=cnt))
               if p.ndim == 3 else
               pl.BlockSpec((tm, tk), functools.partial(lambda i, kk, lo, cnt: (i, tile_of(kk, lo, cnt)), lo=lo, cnt=cnt))
               for p, lo, cnt in zip(parts, first, tiles)]
    b_spec = pl.BlockSpec((pl.Element(tk), pl.Element(n)), lambda i, kk: (pl.multiple_of(row_of_tile(kk), ROW_ALIGN), 0))
    o_spec = pl.BlockSpec((tm, n), lambda i, kk: (i, 0))
    return pl.pallas_call(
        body, name=name, grid=(m // tm, nk), in_specs=a_specs + [b_spec, o_spec], out_specs=o_spec,
        out_shape=jax.ShapeDtypeStruct((m, n), F32), scratch_shapes=[pltpu.VMEM((tm, n), F32)],
        compiler_params=_cparams("parallel", "arbitrary"),
    )(*parts, b, add)


def _swiglu_tiles(m, half):
    tn = _tile(half, (512, 256, 128))
    return _tile(m, (2048 if tn <= 256 else 1024, 1024, 512, 256, 128)), tn


def _ffn_in_swiglu(h, wt, half, name):
    m, k = h.shape
    tm, tn = _swiglu_tiles(m, half)
    nj = half // tn
    dims = (((1,), (1,)), ((), ()))

    def body(h_ref, wa_ref, wb_ref, act_ref, a_ref, b_ref):
        lhs = h_ref[...].astype(MXU_DTYPE)
        a = lax.dot_general(lhs, wa_ref[...].astype(MXU_DTYPE), dims, preferred_element_type=F32)
        b = lax.dot_general(lhs, wb_ref[...].astype(MXU_DTYPE), dims, preferred_element_type=F32)
        act_ref[...] = (_silu(a) * b).astype(act_ref.dtype)
        a_ref[...] = a.astype(a_ref.dtype)
        b_ref[...] = b.astype(b_ref.dtype)

    out = jax.ShapeDtypeStruct((m, half), MXU_DTYPE)
    oblk = pl.BlockSpec((tm, tn), lambda i, j: (i, j))
    return pl.pallas_call(
        body, name=name, grid=(m // tm, nj),
        in_specs=[pl.BlockSpec((tm, k), lambda i, j: (i, 0)), pl.BlockSpec((tn, k), lambda i, j: (j, 0)),
                  pl.BlockSpec((tn, k), lambda i, j: (j + nj, 0))],
        out_specs=[oblk, oblk, oblk], out_shape=[out, out, out], compiler_params=_cparams("parallel", "parallel"),
    )(h, wt, wt)


def _ffn_out_bwd_swiglu(dff, w, a, b, name):
    m, k = dff.shape
    half = w.shape[0]
    tm, tn = _swiglu_tiles(m, half)

    def body(d_ref, w_ref, a_ref, b_ref, da_ref, db_ref):
        dact = lax.dot_general(d_ref[...].astype(MXU_DTYPE), w_ref[...].astype(MXU_DTYPE), (((1,), (1,)), ((), ())),
                               preferred_element_type=F32)
        av, bv = a_ref[...].astype(F32), b_ref[...].astype(F32)
        sig = jax.nn.sigmoid(av)
        da_ref[...] = (dact * bv * (sig * (1.0 + av * (1.0 - sig)))).astype(da_ref.dtype)
        db_ref[...] = (dact * (av * sig)).astype(db_ref.dtype)

    out = jax.ShapeDtypeStruct((m, half), MXU_DTYPE)
    oblk = pl.BlockSpec((tm, tn), lambda i, j: (i, j))
    return pl.pallas_call(
        body, name=name, grid=(m // tm, half // tn),
        in_specs=[pl.BlockSpec((tm, k), lambda i, j: (i, 0)), pl.BlockSpec((tn, k), lambda i, j: (j, 0)), oblk, oblk],
        out_specs=[oblk, oblk], out_shape=[out, out], compiler_params=_cparams("parallel", "parallel"),
    )(dff, w, a, b)


def _with_off(xs):
    return [x if isinstance(x, tuple) else (x, 0) for x in xs]


def _spec(kind, arr, off, ts, wb):
    w = arr.shape[-1] if wb is None else wb
    col = (lambda j: 0) if wb is None else functools.partial(lambda j, o: o + j, o=off)
    if kind == "tok":
        return pl.BlockSpec((None, ts, w), lambda j, b, i: (b, i, col(j)))
    if kind == "bat":
        return pl.BlockSpec((None, 1, w), lambda j, b, i: (b, 0, col(j)))
    if off is None:
        return pl.BlockSpec(arr.shape, lambda j, b, i: (0, 0))
    return pl.BlockSpec((arr.shape[0], w), lambda j, b, i: (0, col(j)))


class _Product:
    def __init__(self, a, b, *, tb=False, b_rows=None, add=None):
        self.a, self.b, self.tb, self.b_rows, self.add = a, b, tb, b_rows, add
        rows = b.shape[0] if b_rows is None else b_rows[1]
        self.shape = a.shape[:2] + (rows if tb else b.shape[1],)

    def inputs(self, ts):
        a_spec = pl.BlockSpec((None, ts, self.a.shape[2]), lambda j, b, i: (b, i, 0))
        if self.b_rows is None:
            b_spec = pl.BlockSpec(self.b.shape, lambda j, b, i: (0, 0))
        else:
            start, count = self.b_rows
            b_spec = pl.BlockSpec((pl.Element(count), pl.Element(self.b.shape[1])), lambda j, b, i: (start, 0))
        if isinstance(self.add, _Product):
            extra = self.add.inputs(ts)
        else:
            extra = [] if self.add is None else [(self.add, pl.BlockSpec((None, ts, self.shape[2]), lambda j, b, i: (b, i, 0)))]
        return [(self.a, a_spec), (self.b, b_spec)] + extra

    def value(self, refs):
        dims = (((1,), (1 if self.tb else 0,)), ((), ()))
        val = lax.dot_general(refs[0][...].astype(MXU_DTYPE), refs[1][...].astype(MXU_DTYPE), dims, preferred_element_type=F32)
        if isinstance(self.add, _Product):
            return val + self.add.value(refs[2:])
        return val if self.add is None else val + refs[2][...].astype(F32)


def _inputs(groups, kinds, ts, wb):
    loaded = [(a, _spec(kind, a, o, ts, wb)) for g, kind in zip(groups, kinds) for a, o in g if not isinstance(a, _Product)]
    made = [pair for g in groups for a, _ in g if isinstance(a, _Product) for pair in a.inputs(ts)]
    return [a for a, _ in loaded + made], [sp for _, sp in loaded + made]


def _values(refs, groups):
    n_loaded = sum(1 for g in groups for a, _ in g if not isinstance(a, _Product))
    loaded, pos, out = iter(refs[:n_loaded]), n_loaded, []
    for g in groups:
        vals = []
        for a, _ in g:
            if isinstance(a, _Product):
                k = len(a.inputs(1))
                vals.append(a.value(refs[pos:pos + k]))
                pos += k
            else:
                vals.append(next(loaded)[...].astype(F32))
        out.append(vals)
    return out, pos


def _tok_fwd(fn, toks, bats, pars, outs, *, name, ts, wb=None, cols=1):
    groups = [_with_off(toks), _with_off(bats), _with_off(pars)]
    bl, s, _ = groups[0][0][0].shape
    ts = min(ts, s)
    args, in_specs = _inputs(groups, ("tok", "bat", "par"), ts, wb)

    def body(*refs):
        vals, n_in = _values(refs, groups)
        res = fn(*[v for g in vals for v in g])
        for r, val in zip(refs[n_in:], res):
            r[...] = val.astype(r.dtype)

    out_specs = [pl.BlockSpec((None, ts, w if wb is None else wb), lambda j, b, i: (b, i, j)) for w, _ in outs]
    return pl.pallas_call(
        body, name=name, grid=(cols, bl, s // ts), in_specs=in_specs,
        out_specs=out_specs, out_shape=[jax.ShapeDtypeStruct((bl, s, w), dt) for w, dt in outs],
        compiler_params=_cparams("parallel", "parallel", "parallel"),
    )(*args)


def _accumulate(ref, val, first):
    @pl.when(first)
    def _():
        ref[...] = val

    @pl.when(jnp.logical_not(first))
    def _():
        ref[...] += val


def _tok_bwd(fn, toks, bats, pars, cots, need, *, name, ts, wb=None, cols=1, tok_dtype=F32, loss=False, after=()):
    toks, bats, pars, cots = _with_off(toks), _with_off(bats), _with_off(pars), _with_off(cots)
    groups = [toks, bats, pars, cots]
    bl, s, _ = toks[0][0].shape
    ts = min(ts, s)
    nt, nb, npar = len(toks), len(bats), len(pars)
    args, in_specs = _inputs(groups, ("tok", "bat", "par", "tok"), ts, wb)
    args, in_specs = args + list(after), in_specs + [pl.BlockSpec(memory_space=pl.ANY)] * len(after)

    def body(*refs):
        j, b, i = pl.program_id(0), pl.program_id(1), pl.program_id(2)
        (tok_vals, bat_vals, par_vals, cot_vals), o = _values(refs, groups)
        o += len(after)
        outs, vjp = jax.vjp(fn, *tok_vals, *bat_vals, *par_vals)
        if loss:
            ct = (jnp.ones_like(outs[0]),)
            tot = jnp.broadcast_to(jnp.sum(outs[0], keepdims=True), (1, LANE))
            _accumulate(refs[o], tot, jnp.logical_and(b == 0, i == 0))
            o += 1
        else:
            ct = tuple(cot_vals)
        grads = vjp(ct)
        for t in range(nt):
            if need[t]:
                refs[o][...] = grads[t].astype(refs[o].dtype)
                o += 1
        for t in range(nb):
            _accumulate(refs[o], grads[nt + t], i == 0)
            o += 1
        for t in range(npar):
            first = jnp.logical_and(b == 0, i == 0)
            if pars[t][1] is None:
                first = jnp.logical_and(first, j == 0)
            _accumulate(refs[o], grads[nt + nb + t], first)
            o += 1

    full = lambda arr: arr.shape[-1] if wb is None else wb * cols
    blk = lambda arr: arr.shape[-1] if wb is None else wb
    out_specs, out_shape = [], []
    if loss:
        out_specs.append(pl.BlockSpec((1, LANE), lambda j, b, i: (0, 0)))
        out_shape.append(jax.ShapeDtypeStruct((1, LANE), F32))
    for t in range(nt):
        if need[t]:
            out_specs.append(pl.BlockSpec((None, ts, blk(toks[t][0])), lambda j, b, i: (b, i, j)))
            dt = tok_dtype[t] if isinstance(tok_dtype, (list, tuple)) else tok_dtype
            out_shape.append(jax.ShapeDtypeStruct((bl, s, full(toks[t][0])), dt))
    for arr, _ in bats:
        out_specs.append(pl.BlockSpec((None, 1, blk(arr)), lambda j, b, i: (b, 0, j)))
        out_shape.append(jax.ShapeDtypeStruct((bl, 1, full(arr)), F32))
    for arr, off in pars:
        if off is None:
            out_specs.append(pl.BlockSpec(arr.shape, lambda j, b, i: (0, 0)))
            out_shape.append(jax.ShapeDtypeStruct(arr.shape, F32))
        else:
            out_specs.append(pl.BlockSpec((arr.shape[0], blk(arr)), lambda j, b, i: (0, j)))
            out_shape.append(jax.ShapeDtypeStruct((arr.shape[0], full(arr)), F32))
    res = list(pl.pallas_call(
        body, name=name, grid=(cols, bl, s // ts), in_specs=in_specs,
        out_specs=out_specs, out_shape=out_shape, compiler_params=_cparams("arbitrary", "arbitrary", "arbitrary"),
    )(*args))
    tot = res.pop(0) if loss else None
    dtoks = [res.pop(0) if need[t] else None for t in range(nt)]
    dbats = [res.pop(0) for _ in range(nb)]
    dpars = [res.pop(0) for _ in range(npar)]
    return (tot, dtoks, dbats, dpars) if loss else (dtoks, dbats, dpars)


def _silu(x):
    return x * jax.nn.sigmoid(x)


def _rms(x, w):
    return x * lax.rsqrt(jnp.mean(x * x, axis=-1, keepdims=True) + EPS) * w


def _f_norm_mod(x, shift, scale, w):
    return (_rms(x, w) * (1.0 + scale) + shift,)


def _f_norm_mod_skip(x, shift, scale, w):
    return _rms(x, w) * (1.0 + scale) + shift, x


def _f_res_norm_mod(x, mix, gate, shift, scale, w):
    x2 = x + gate * mix
    return x2, _rms(x2, w) * (1.0 + scale) + shift


def _f_res_norm_mod_keep(x, mix, gate, shift, scale, w):
    return (*_f_res_norm_mod(x, mix, gate, shift, scale, w), mix)


def _f_gates(p, a_log, dt_bias, *, heads):
    z = p + dt_bias
    g = -jnp.exp(a_log) * (jnp.maximum(z, 0.0) + jnp.log1p(jnp.exp(jnp.minimum(z, -z))))
    lane = lax.broadcasted_iota(jnp.int32, p.shape, 1)
    return (jnp.where(lane < heads, g, jax.nn.sigmoid(p)),)


def _f_gdn_out(o, z, w):
    return (_rms(o, w) * _silu(z),)


def _f_merge(ga, gb, ya, yb):
    return (jax.nn.sigmoid(ga) * ya + jax.nn.sigmoid(gb) * yb,)


def _f_merge_keep(ga, gb, ya, yb):
    return (*_f_merge(ga, gb, ya, yb), yb)


def _f_loss(x2, ff, tgt, gate, shift, scale, w):
    y = _rms(x2 + gate * ff, w) * (1.0 + scale) + shift
    return (0.5 * jnp.mean(jnp.square(y - tgt), axis=-1, keepdims=True),)


def _shift_down(x, s):
    if s == 0:
        return x
    row = lax.broadcasted_iota(jnp.int32, x.shape, 0)
    return jnp.where(row >= s, pltpu.roll(x, s, 0), 0.0)


def _shift_up(x, s):
    if s == 0:
        return x
    n = x.shape[0]
    row = lax.broadcasted_iota(jnp.int32, x.shape, 0)
    return jnp.where(row < n - s, pltpu.roll(x, n - s, 0), 0.0)


def _conv(x, w):
    width = w.shape[0]
    acc = w[width - 1:width, :] * x
    for j in range(width - 1):
        acc = acc + w[j:j + 1, :] * _shift_down(x, width - 1 - j)
    return acc


def _conv_bwd(dy, x, w, dw_ref, first):
    width = w.shape[0]
    dx = w[width - 1:width, :] * dy
    for j in range(width - 1):
        dx = dx + w[j:j + 1, :] * _shift_up(dy, width - 1 - j)
    for j in range(width):
        row = jnp.sum(dy * _shift_down(x, width - 1 - j), axis=0, keepdims=True)
        _accumulate(dw_ref.at[j:j + 1, :], row, first)
    return dx


def _qkv_act(xc, is_v, scale):
    a = _silu(xc)
    nrm = a * lax.rsqrt(jnp.sum(a * a, axis=-1, keepdims=True) + EPS) * scale
    return jnp.where(is_v, a, nrm)


def _qkv_act_bwd(xc, dout, is_v, scale):
    sig = jax.nn.sigmoid(xc)
    a = xc * sig
    r = lax.rsqrt(jnp.sum(a * a, axis=-1, keepdims=True) + EPS)
    c1 = r * scale
    da = c1 * dout - a * (c1 * r * r * jnp.sum(dout * a, axis=-1, keepdims=True))
    return jnp.where(is_v, dout, da) * (sig * (1.0 + xc * (1.0 - sig)))


def _qkv_consts(j, heads):
    is_v = j >= 2 * heads
    scale = jnp.where(j < heads, HEAD ** -0.5, 1.0).astype(F32)
    return is_v, scale


def _qkv_fwd(p, w, heads, name):
    bl, s, w3 = p.shape

    def body(p_ref, w_ref, o_ref):
        is_v, scale = _qkv_consts(pl.program_id(0), heads)
        o_ref[...] = _qkv_act(_conv(p_ref[...], w_ref[...]), is_v, scale)

    blk = pl.BlockSpec((None, s, HEAD), lambda j, b: (b, 0, j))
    return pl.pallas_call(
        body, name=name, grid=(w3 // HEAD, bl), in_specs=[blk, pl.BlockSpec((w.shape[0], HEAD), lambda j, b: (0, j))],
        out_specs=blk, out_shape=jax.ShapeDtypeStruct(p.shape, F32), compiler_params=_cparams("parallel", "parallel"),
    )(p, w)


def _qkv_bwd(p, w, dout, heads, name):
    bl, s, w3 = p.shape

    def body(p_ref, w_ref, d_ref, dp_ref, dw_ref):
        is_v, scale = _qkv_consts(pl.program_id(0), heads)
        x, wv = p_ref[...], w_ref[...]
        dxc = _qkv_act_bwd(_conv(x, wv), d_ref[...], is_v, scale)
        dp_ref[...] = _conv_bwd(dxc, x, wv, dw_ref, pl.program_id(1) == 0).astype(dp_ref.dtype)

    blk = pl.BlockSpec((None, s, HEAD), lambda j, b: (b, 0, j))
    wblk = pl.BlockSpec((w.shape[0], HEAD), lambda j, b: (0, j))
    return pl.pallas_call(
        body, name=name, grid=(w3 // HEAD, bl), in_specs=[blk, wblk, blk], out_specs=[blk, wblk],
        out_shape=[jax.ShapeDtypeStruct(p.shape, MXU_DTYPE), jax.ShapeDtypeStruct(w.shape, F32)],
        compiler_params=_cparams("arbitrary", "arbitrary"),
    )(p, w, dout)


def _sc_specs(p, w):
    bl, s, w3 = p.shape
    nblk = w3 // 3 // LANE
    sec = lambda k: pl.BlockSpec((None, s, LANE), functools.partial(lambda j, b, k: (b, 0, k * nblk + j), k=k))
    return nblk, [sec(0), sec(1), sec(2)], pl.BlockSpec((w.shape[0], LANE), lambda j, b: (0, j)), \
        pl.BlockSpec((None, s, LANE), lambda j, b: (b, 0, j))


def _sc_fwd(p, w, name):
    bl, s, w3 = p.shape
    nblk, secs, wblk, oblk = _sc_specs(p, w)

    def body(b_ref, c_ref, x_ref, w_ref, o_ref):
        o_ref[...] = (b_ref[...] * _conv(c_ref[...] * x_ref[...], w_ref[...])).astype(o_ref.dtype)

    return pl.pallas_call(
        body, name=name, grid=(nblk, bl), in_specs=secs + [wblk], out_specs=oblk,
        out_shape=jax.ShapeDtypeStruct((bl, s, w3 // 3), MXU_DTYPE), compiler_params=_cparams("parallel", "parallel"),
    )(p, p, p, w)


def _sc_bwd(p, w, dout, name):
    bl, s, w3 = p.shape
    nblk, secs, wblk, oblk = _sc_specs(p, w)

    def body(b_ref, c_ref, x_ref, w_ref, d_ref, dp_ref, dw_ref):
        gb, gc, xin, wv, d = b_ref[...], c_ref[...], x_ref[...], w_ref[...], d_ref[...]
        u = gc * xin
        dp_ref[0] = (d * _conv(u, wv)).astype(dp_ref.dtype)
        du = _conv_bwd(d * gb, u, wv, dw_ref, pl.program_id(1) == 0)
        dp_ref[1] = (du * xin).astype(dp_ref.dtype)
        dp_ref[2] = (du * gc).astype(dp_ref.dtype)

    return pl.pallas_call(
        body, name=name, grid=(nblk, bl), in_specs=secs + [wblk, oblk],
        out_specs=[pl.BlockSpec((3, None, s, LANE), lambda j, b: (0, b, 0, j)), wblk],
        out_shape=[jax.ShapeDtypeStruct((3, bl, s, w3 // 3), MXU_DTYPE), jax.ShapeDtypeStruct(w.shape, F32)],
        compiler_params=_cparams("arbitrary", "arbitrary"),
    )(p, p, p, w, dout)


def _bdot(a, b, ca, cb):
    return lax.dot_general(a.astype(MXU_DTYPE), b.astype(MXU_DTYPE), (((ca,), (cb,)), ((), ())),
                           preferred_element_type=F32)


def _hdot(a, b):
    return lax.dot_general(a, b, (((1,), (0,)), ((), ())), precision=HIGHEST, preferred_element_type=F32)


def _lane_col(x, idx):
    lane = lax.broadcasted_iota(jnp.int32, x.shape, 1)
    return jnp.sum(jnp.where(lane == idx, x, 0.0), axis=1, keepdims=True)


def _chunk_masks():
    r = lax.broadcasted_iota(jnp.int32, (CHUNK, CHUNK), 0)
    c = lax.broadcasted_iota(jnp.int32, (CHUNK, CHUNK), 1)
    return r == c, r >= c, r > c


def _dot3(a, b):
    ah, bh = a.astype(MXU_DTYPE), b.astype(MXU_DTYPE)
    al, bl = (a - ah.astype(F32)).astype(MXU_DTYPE), (b - bh.astype(F32)).astype(MXU_DTYPE)
    dot = lambda x, y: lax.dot_general(x, y, (((1,), (0,)), ((), ())), preferred_element_type=F32)
    return dot(ah, bh) + (dot(ah, bl) + dot(al, bh))


def _tri_inv_steps(low, eye):
    x = -low
    p = jnp.where(eye, 1.0, 0.0) + x
    span = 2
    while span < CHUNK:
        x = _dot3(x, x)
        yield
        p = p + _dot3(p, x)
        yield
        span *= 2
    return p


def _round_robin(gens):
    out, live = [None] * len(gens), list(range(len(gens)))
    while live:
        still = []
        for i in live:
            try:
                next(gens[i])
                still.append(i)
            except StopIteration as stop:
                out[i] = stop.value
        live = still
    return out


def _gdn_pre(q, k, v, gc, beta, masks):
    eye, causal, strict = masks
    gc_row = jnp.sum(jnp.where(eye, gc, 0.0), axis=0, keepdims=True)
    decay = jnp.where(causal, jnp.exp(jnp.where(causal, gc - gc_row, 0.0)), 0.0)
    eg = jnp.exp(gc)
    gl = gc[CHUNK - 1:CHUNK, :]
    kb, vb = k * beta, v * beta
    both = _bdot(jnp.concatenate([kb, q], axis=0), k, 1, 1)
    low = jnp.where(strict, both[:CHUNK] * decay, 0.0)
    qk = jnp.where(causal, both[CHUNK:] * decay, 0.0)
    rest = jnp.exp(gl - gc)
    return dict(decay=decay, eg=eg, gl=gl, kb=kb, vb=vb, kbe=kb * eg, low=low, qk=qk, qg=q * eg, rest=rest, kdec=k * rest)


GROUP = 4


def _gdn_specs(qkv, gbeta, heads, rev):
    bl, s, w3 = qkv.shape
    d, n = w3 // 3, s // CHUNK
    group = GROUP if n % GROUP == 0 else 1
    steps = n // group
    at = (lambda c: steps - 1 - c) if rev else (lambda c: c)
    assert d == heads * HEAD
    rows = group * CHUNK
    sec = pl.BlockSpec((None, rows, w3), lambda b, c: (b, at(c), 0))
    gspec = pl.BlockSpec((None, rows, LANE), lambda b, c: (b, at(c), 0))
    ospec = pl.BlockSpec((None, rows, d), lambda b, c: (b, at(c), 0))
    sspec = pl.BlockSpec((None, group, heads, HEAD, HEAD), lambda b, c: (b, at(c), 0, 0, 0))
    tspec = pl.BlockSpec((None, group, heads, CHUNK, CHUNK), lambda b, c: (b, at(c), 0, 0, 0))
    return bl, s, d, n, group, sec, gspec, ospec, sspec, tspec


def _gdn_fwd(qkv, gbeta, heads, name):
    bl, s, d, n, group, sec, gspec, ospec, sspec, tspec = _gdn_specs(qkv, gbeta, heads, False)
    rows = lambda sub: slice(sub * CHUNK, (sub + 1) * CHUNK)
    pairs = [(h, sub) for h in range(heads) for sub in range(group)]

    def body(x_ref, g_ref, o_ref, s_ref, t_ref, st_ref):
        @pl.when(pl.program_id(1) == 0)
        def _():
            st_ref[...] = jnp.zeros_like(st_ref)

        masks = _chunk_masks()
        eye, causal, _ = masks
        gblks = [g_ref[rows(sub), :] for sub in range(group)]
        gcs = [_hdot(jnp.where(causal, 1.0, 0.0), gb) for gb in gblks]
        st_all = st_ref[...]

        def free(h, sub):
            q, k, v = (x_ref[rows(sub), sec * d + h * HEAD:sec * d + (h + 1) * HEAD] for sec in range(3))
            pre = _gdn_pre(q, k, v, _lane_col(gcs[sub], h), _lane_col(gblks[sub], heads + h), masks)
            yield
            t = yield from _tri_inv_steps(pre["low"], eye)
            uw = _bdot(t, jnp.concatenate([pre["vb"], pre["kbe"]], axis=1), 1, 0)
            return pre, t, uw[:, :HEAD], uw[:, HEAD:]

        pieces = dict(zip(pairs, _round_robin([free(h, sub) for h, sub in pairs])))

        def carry(h):
            st, outs, starts = st_all[h], [], []
            for sub in range(group):
                pre, _, u, w = pieces[h, sub]
                starts.append(st)
                vnew = u - _bdot(w, st, 1, 0)
                yield
                outs.append(_bdot(pre["qg"], st, 1, 0) + _bdot(pre["qk"], vnew, 1, 0))
                st = st * jnp.exp(pre["gl"]) + _bdot(pre["kdec"], vnew, 0, 0)
                yield
            return outs, starts, st

        carried = _round_robin([carry(h) for h in range(heads)])
        per_sub = lambda pick: [[pick(h, sub) for h in range(heads)] for sub in range(group)]
        o_ref[...] = jnp.concatenate([jnp.concatenate(r, axis=1) for r in per_sub(lambda h, sub: carried[h][0][sub])], axis=0)
        s_ref[...] = jnp.stack([jnp.stack(r) for r in per_sub(lambda h, sub: carried[h][1][sub])])
        t_ref[...] = jnp.stack([jnp.stack(r) for r in per_sub(lambda h, sub: pieces[h, sub][1])])
        st_ref[...] = jnp.stack([carried[h][2] for h in range(heads)])

    return pl.pallas_call(
        body, name=name, grid=(bl, n // group), in_specs=[sec, gspec], out_specs=[ospec, sspec, tspec],
        out_shape=[jax.ShapeDtypeStruct((bl, s, d), F32), jax.ShapeDtypeStruct((bl, n, heads, HEAD, HEAD), F32),
                   jax.ShapeDtypeStruct((bl, n, heads, CHUNK, CHUNK), F32)],
        scratch_shapes=[pltpu.VMEM((heads, HEAD, HEAD), F32)], compiler_params=_cparams("parallel", "arbitrary"),
    )(qkv, gbeta)


def _gdn_bwd(qkv, gbeta, dout, s_all, t_all, heads, name):
    bl, s, d, n, group, sec, gspec, ospec, sspec, tspec = _gdn_specs(qkv, gbeta, heads, True)
    rows = lambda sub: slice(sub * CHUNK, (sub + 1) * CHUNK)
    pairs = [(h, sub) for h in range(heads) for sub in range(group)]
    stack, side = functools.partial(jnp.concatenate, axis=0), functools.partial(jnp.concatenate, axis=1)

    def body(x_ref, g_ref, do_ref, s_ref, t_ref, dx_ref, dg_ref, ds_ref):
        @pl.when(pl.program_id(1) == 0)
        def _():
            ds_ref[...] = jnp.zeros_like(ds_ref)

        masks = _chunk_masks()
        eye, causal, strict = masks
        gblks = [g_ref[rows(sub), :] for sub in range(group)]
        gcs = [_hdot(jnp.where(causal, 1.0, 0.0), gb) for gb in gblks]
        lane = lax.broadcasted_iota(jnp.int32, (CHUNK, LANE), 1)
        last_row = lax.broadcasted_iota(jnp.int32, (CHUNK, 1), 0) == CHUNK - 1
        rowsum = lambda a: jnp.sum(a, axis=1, keepdims=True)
        st_all, t_all_, ds_all = s_ref[...], t_ref[...], ds_ref[...]

        def free(h, sub):
            q, k, v = (x_ref[rows(sub), sec * d + h * HEAD:sec * d + (h + 1) * HEAD] for sec in range(3))
            do = do_ref[rows(sub), h * HEAD:(h + 1) * HEAD]
            beta = _lane_col(gblks[sub], heads + h)
            st, t = st_all[sub, h], t_all_[sub, h]
            pre = _gdn_pre(q, k, v, _lane_col(gcs[sub], h), beta, masks)
            yield
            uw = _bdot(t, side([pre["vb"], pre["kbe"]]), 1, 0)
            u, w = uw[:, :HEAD], uw[:, HEAD:]
            yield
            vnew = u - _bdot(w, st, 1, 0)
            yield
            dqk = jnp.where(causal, _bdot(do, vnew, 1, 1), 0.0)
            dqg = _bdot(do, st, 1, 1)
            return dict(q=q, k=k, v=v, do=do, beta=beta, st=st, t=t, pre=pre, w=w, vnew=vnew, dqk=dqk, dqg=dqg)

        pieces = dict(zip(pairs, _round_robin([free(h, sub) for h, sub in pairs])))

        def carry(h):
            dsn, outs = ds_all[h], {}
            for sub in reversed(range(group)):
                pc = pieces[h, sub]
                pre, st, do = pc["pre"], pc["st"], pc["do"]
                egl = jnp.exp(pre["gl"])
                dkdec = _bdot(pc["vnew"], dsn, 1, 1)
                dvnew = _bdot(pre["kdec"], dsn, 1, 0) + _bdot(pre["qk"], do, 0, 0)
                dgl = jnp.sum(dsn * st, keepdims=True) * egl
                yield
                dw = -_bdot(dvnew, st, 1, 1)
                dsn = dsn * egl + _bdot(stack([pre["qg"], -pc["w"]]), stack([do, dvnew]), 0, 0)
                outs[sub] = (dkdec, dvnew, dgl, dw)
                yield
            return outs, dsn

        carried = _round_robin([carry(h) for h in range(heads)])

        def rest(h, sub):
            pc = pieces[h, sub]
            dkdec, dvnew, dgl, dw = carried[h][0][sub]
            q, k, v, beta, t, pre, dqk, dqg = (pc[x] for x in ("q", "k", "v", "beta", "t", "pre", "dqk", "dqg"))
            decay, eg, kb, vb, kbe, low, qk, qg, kdec = (pre[x] for x in ("decay", "eg", "kb", "vb", "kbe", "low", "qk", "qg", "kdec"))
            dt = _bdot(side([dvnew, dw]), side([vb, kbe]), 1, 1)
            by_t = _bdot(t, side([dvnew, dw]), 0, 0)
            dvb, dkbe = by_t[:, :HEAD], by_t[:, HEAD:]
            yield
            inner = _bdot(dt, t, 1, 1)
            yield
            dlow = -jnp.where(strict, _bdot(t, inner, 0, 0), 0.0)
            da, db = dlow * decay, dqk * decay
            yield
            m = dlow * low + dqk * qk
            kdk = dkdec * kdec
            col_of_m = jnp.sum(jnp.where(eye, jnp.sum(m, axis=0, keepdims=True), 0.0), axis=1, keepdims=True)
            dgc = rowsum(m) - col_of_m + rowsum(dqg * qg) + rowsum(dkbe * kbe) - rowsum(kdk)
            dgc = dgc + jnp.where(last_row, dgl + jnp.sum(kdk, keepdims=True), 0.0)
            by_k = _bdot(stack([da, db]), k, 1, 0)
            dkb = by_k[:CHUNK] + dkbe * eg
            yield
            dk = _bdot(stack([da, db]), stack([kb, q]), 0, 0) + dkdec * pre["rest"] + dkb * beta
            dq = by_k[CHUNK:] + dqg * eg
            dbeta = rowsum(dkb * k) + rowsum(dvb * v)
            return dq, dk, dvb * beta, jnp.where(lane == h, dgc, 0.0) + jnp.where(lane == heads + h, dbeta, 0.0)

        done = dict(zip(pairs, _round_robin([rest(h, sub) for h, sub in pairs])))
        dx_ref[...] = stack([side([done[h, sub][i] for i in range(3) for h in range(heads)]) for sub in range(group)])
        ds_ref[...] = jnp.stack([carried[h][1] for h in range(heads)])
        upper = jnp.where(jnp.logical_or(eye, jnp.logical_not(causal)), 1.0, 0.0)
        dgs = []
        for sub in range(group):
            dgb = done[0, sub][3]
            for h in range(1, heads):
                dgb = dgb + done[h, sub][3]
            dgs.append(jnp.where(lane < heads, _hdot(upper, dgb), dgb))
        dg_ref[...] = stack(dgs)

    return pl.pallas_call(
        body, name=name, grid=(bl, n // group), in_specs=[sec, gspec, ospec, sspec, tspec], out_specs=[sec, gspec],
        out_shape=[jax.ShapeDtypeStruct(qkv.shape, F32), jax.ShapeDtypeStruct((bl, s, LANE), F32)],
        scratch_shapes=[pltpu.VMEM((heads, HEAD, HEAD), F32)], compiler_params=_cparams("parallel", "arbitrary"),
    )(qkv, gbeta, dout, s_all, t_all)


def _position():
    return lax.axis_index("x"), lax.axis_index("y"), lax.axis_index("c")


def _all_gather(x, *, name):
    space = pltpu.VMEM

    def body(x_ref, out_ref, send_sems, recv_sems, local_sem):
        ax, ay, ac = _position()
        me, sibling = (ax, ay, ac), (ax, ay, 1 - ac)
        chips = [(1 - ax, ay), (ax, 1 - ay), (1 - ax, 1 - ay)]

        def slot(px, py, pc):
            return out_ref.at[4 * px + 2 * py + pc]

        def copy(k, block, to, src=None):
            return pltpu.make_async_remote_copy(
                src_ref=slot(*block) if src is None else src, dst_ref=slot(*block), send_sem=send_sems.at[k],
                recv_sem=recv_sems.at[k], device_id=to, device_id_type=MESH_IDS)

        mine = pltpu.make_async_copy(x_ref, slot(*me), local_sem)
        mine.start()
        first = [copy(0, me, sibling, src=x_ref)] + [copy(1 + j, me, (*chip, ac), src=x_ref) for j, chip in enumerate(chips)]
        for cp in first:
            cp.start()
        passed = [copy(4 + j, (*chip, ac), sibling) for j, chip in enumerate(chips)]
        for j, chip in enumerate(chips):
            copy(1 + j, (*chip, ac), me).wait_recv()
            passed[j].start()
        copy(0, sibling, me).wait_recv()
        for j, chip in enumerate(chips):
            copy(4 + j, (*chip, 1 - ac), me).wait_recv()
        for cp in first + passed:
            cp.wait_send()
        mine.wait()

    return pl.pallas_call(
        body, name=name, out_shape=jax.ShapeDtypeStruct((NDEV,) + x.shape, x.dtype),
        in_specs=[pl.BlockSpec(memory_space=space)], out_specs=pl.BlockSpec(memory_space=space),
        scratch_shapes=[pltpu.SemaphoreType.DMA((7,)), pltpu.SemaphoreType.DMA((7,)), pltpu.SemaphoreType.DMA],
    )(x)


class _Rider:
    def __init__(self, arrays, out_shapes, sems, hooks):
        self.arrays, self.out_shapes, self.sems, self.hooks = arrays, out_shapes, sems, hooks


def _gather_rider(xs):
    n = len(xs)

    def hooks(x_refs, out_refs, send_sems, recv_sems):
        ax, ay, ac = _position()
        me, sibling = (ax, ay, ac), (ax, ay, 1 - ac)
        chips = [(1 - ax, ay), (ax, 1 - ay), (1 - ax, 1 - ay)]

        def copies(k, block, to, own=False):
            out = []
            for i in range(n):
                slot = out_refs[i].at[4 * block[0] + 2 * block[1] + block[2]]
                out.append(pltpu.make_async_remote_copy(
                    src_ref=x_refs[i] if own else slot, dst_ref=slot, send_sem=send_sems.at[k, i], recv_sem=recv_sems.at[k, i],
                    device_id=to, device_id_type=MESH_IDS))
            return out

        def first():
            for cp in copies(0, me, sibling, own=True):
                cp.start()
            for j, chip in enumerate(chips):
                for cp in copies(1 + j, me, (*chip, ac), own=True):
                    cp.start()

        def mid():
            for j, chip in enumerate(chips):
                for arrived, onward in zip(copies(1 + j, (*chip, ac), me), copies(4 + j, (*chip, ac), sibling)):
                    arrived.wait_recv()
                    onward.start()

        def last():
            for cp in copies(0, sibling, me):
                cp.wait_recv()
            for j, chip in enumerate(chips):
                for cp in copies(4 + j, (*chip, 1 - ac), me):
                    cp.wait_recv()
            for cp in copies(0, me, sibling, own=True):
                cp.wait_send()
            for j, chip in enumerate(chips):
                for cp in copies(1 + j, me, (*chip, ac), own=True) + copies(4 + j, (*chip, ac), sibling):
                    cp.wait_send()

        return first, mid, last

    return _Rider(list(xs), [jax.ShapeDtypeStruct((NDEV,) + x.shape, x.dtype) for x in xs],
                  [pltpu.SemaphoreType.DMA((7, n)), pltpu.SemaphoreType.DMA((7, n))], hooks)


def _scatter_rider(parts):
    packed = sum(r for _, r in parts)
    width, dtype = parts[0][0].shape[1], parts[0][0].dtype

    def hooks(g_refs, out_refs, send_sems, recv_sems):
        (recv_ref,) = out_refs
        ax, ay, ac = _position()

        def peer(rel):
            flip = lambda a, bit: 1 - a if rel & bit else a
            return flip(ax, 4), flip(ay, 2), flip(ac, 1)

        def first():
            for rel in range(1, NDEV):
                px, py, pc = peer(rel)
                off = 0
                for g_ref, (_, r) in zip(g_refs, parts):
                    rows = g_ref.at[pl.ds(pl.multiple_of((4 * px + 2 * py + pc) * r, ROW_ALIGN), r)]
                    pltpu.make_async_remote_copy(
                        src_ref=rows, dst_ref=recv_ref.at[rel - 1, pl.ds(off, r)], send_sem=send_sems.at[rel - 1],
                        recv_sem=recv_sems.at[rel - 1], device_id=(px, py, pc), device_id_type=MESH_IDS).start()
                    off += r

        def last():
            for rel in range(1, NDEV):
                slot = recv_ref.at[rel - 1]
                pltpu.make_async_remote_copy(src_ref=slot, dst_ref=slot, send_sem=send_sems.at[rel - 1],
                                             recv_sem=recv_sems.at[rel - 1], device_id=peer(rel), device_id_type=MESH_IDS).wait()

        return first, lambda: None, last

    return _Rider([g for g, _ in parts], [jax.ShapeDtypeStruct((NDEV - 1, packed, width), dtype)],
                  [pltpu.SemaphoreType.DMA((NDEV - 1,)), pltpu.SemaphoreType.DMA((NDEV - 1,))], hooks)


def _sum_direct(own, recv, name):
    r, w = own.shape
    tr = max(t for t in range(ROW_ALIGN, 257, ROW_ALIGN) if r % t == 0)

    def body(own_ref, *refs):
        acc = own_ref[...].astype(F32)
        for ref in refs[:-1]:
            acc = acc + ref[...].astype(F32)
        refs[-1][...] = acc

    rblk = lambda k: pl.BlockSpec((None, tr, w), functools.partial(lambda i, k: (k, i, 0), k=k))
    blk = pl.BlockSpec((tr, w), lambda i: (i, 0))
    return pl.pallas_call(body, name=name, grid=(r // tr,), in_specs=[blk] + [rblk(k) for k in range(NDEV - 1)],
                          out_specs=blk, out_shape=jax.ShapeDtypeStruct((r, w), F32),
                          compiler_params=_cparams("parallel"))(own, *([recv] * (NDEV - 1)))


ROW_ALIGN = 16


def _window_start(rows_per_dev, k):
    return rows_per_dev * k // ROW_ALIGN * ROW_ALIGN


def _exchange_in_chip(parts, name, collective_id):
    packed = sum(win for _, _, win, _ in parts)
    width, dtype = parts[0][0].shape[1], parts[0][0].dtype

    def body(g_refs, out_refs, send_sems, recv_sems):
        (recv_ref,) = out_refs
        ax, ay, ac = _position()
        sibling = (ax, ay, 1 - ac)
        _handshake([sibling])
        for q in range(4):
            for g_ref, (_, r, win, off) in zip(g_refs, parts):
                there = g_ref.at[pl.ds(pl.multiple_of(_window_start(r, 2 * q + 1 - ac), ROW_ALIGN), win)]
                pltpu.make_async_remote_copy(src_ref=there, dst_ref=recv_ref.at[q, pl.ds(off, win)], send_sem=send_sems.at[q],
                                             recv_sem=recv_sems.at[q], device_id=sibling, device_id_type=MESH_IDS).start()
        for q in range(4):
            pltpu.make_async_remote_copy(src_ref=recv_ref.at[q], dst_ref=recv_ref.at[q], send_sem=send_sems.at[q],
                                         recv_sem=recv_sems.at[q], device_id=sibling, device_id_type=MESH_IDS).wait()

    return _on_sequencer(body, [g for g, _, _, _ in parts], [jax.ShapeDtypeStruct((4, packed, width), dtype)],
                         [pltpu.SemaphoreType.DMA((4,)), pltpu.SemaphoreType.DMA((4,))], name=name, collective_id=collective_id)[0]


def _on_sequencer(body, ins, out_shapes, sems, *, name, collective_id):
    hbm = pltpu.MemorySpace.HBM
    in_refs = [jax.new_ref(a, memory_space=hbm) for a in ins]
    out_refs = [jax.empty_ref(s, memory_space=hbm) for s in out_shapes]

    @pl.kernel(mesh=plsc.ScalarSubcoreMesh(axis_name="sequencer", num_cores=1), name=name, scratch_types=tuple(sems),
               compiler_params=pltpu.CompilerParams(collective_id=collective_id))
    def launch(*sem_refs):
        body(in_refs, out_refs, *sem_refs)

    launch()
    return [r[...] for r in out_refs]


def _handshake(peers):
    barrier = pltpu.get_barrier_semaphore()
    for peer in peers:
        pl.semaphore_signal(barrier, inc=1, device_id=peer, device_id_type=MESH_IDS)
    pl.semaphore_wait(barrier, len(peers))


def _exchange_chips_async(s1, name, collective_id):
    def body(in_refs, out_refs, send_sems, recv_sems):
        (src,), (got,) = in_refs, out_refs
        ax, ay, ac = _position()
        chips = [(1 - ax, ay), (ax, 1 - ay), (1 - ax, 1 - ay)]
        _handshake([(cx, cy, ac) for cx, cy in chips])
        copies = [pltpu.make_async_remote_copy(
            src_ref=src.at[2 * cx + cy], dst_ref=got.at[r], send_sem=send_sems.at[r], recv_sem=recv_sems.at[r],
            device_id=(cx, cy, ac), device_id_type=MESH_IDS) for r, (cx, cy) in enumerate(chips)]
        for cp in copies:
            cp.start()
        for cp in copies:
            cp.wait_recv()
        for cp in copies:
            cp.wait_send()

    return _on_sequencer(body, [s1], [jax.ShapeDtypeStruct((3,) + s1.shape[1:], s1.dtype)],
                         [pltpu.SemaphoreType.DMA((3,)), pltpu.SemaphoreType.DMA((3,))], name=name, collective_id=collective_id)[0]


def _gather_async(xs, name, collective_id):
    rider = _gather_rider(xs)

    def body(in_refs, out_refs, send_sems, recv_sems):
        ax, ay, ac = _position()
        _handshake([(ax, ay, 1 - ac), (1 - ax, ay, ac), (ax, 1 - ay, ac), (1 - ax, 1 - ay, ac)])
        for hook in rider.hooks(in_refs, out_refs, send_sems, recv_sems):
            hook()

    return _on_sequencer(body, rider.arrays, rider.out_shapes, rider.sems, name=name, collective_id=collective_id)


def _gather_balanced(x, name, collective_id):
    m = x.shape[0]
    half = m // 2 // ROW_ALIGN * ROW_ALIGN
    parts = {"all": pl.ds(0, m), "lo": pl.ds(0, half), "hi": pl.ds(half, m - half)}

    def body(in_refs, out_refs, send_sems, recv_sems):
        (x_ref,), (out_ref,) = in_refs, out_refs
        ax, ay, ac = _position()
        me, sibling = (ax, ay, ac), (ax, ay, 1 - ac)
        by_x, by_y, diag = (1 - ax, ay), (ax, 1 - ay), (1 - ax, 1 - ay)
        _handshake([sibling, (*by_x, ac), (*by_y, ac)])

        def copy(k, block, part, to, own=False):
            rows = out_ref.at[4 * block[0] + 2 * block[1] + block[2], parts[part]]
            return pltpu.make_async_remote_copy(src_ref=x_ref if own else rows, dst_ref=rows, send_sem=send_sems.at[k],
                                                recv_sem=recv_sems.at[k], device_id=to, device_id_type=MESH_IDS)

        def own_half(k, part, to):
            rows = out_ref.at[4 * ax + 2 * ay + ac, parts[part]]
            return pltpu.make_async_remote_copy(src_ref=x_ref.at[parts[part]], dst_ref=rows, send_sem=send_sems.at[k],
                                                recv_sem=recv_sems.at[k], device_id=to, device_id_type=MESH_IDS)

        nx, ny, nd = (*by_x, ac), (*by_y, ac), (*diag, ac)
        sends = [copy(0, me, "all", sibling, own=True), own_half(1, "lo", nx), own_half(2, "hi", nx),
                 own_half(3, "hi", ny), own_half(4, "lo", ny), copy(5, nx, "lo", ny), copy(6, ny, "hi", nx),
                 copy(7, nx, "lo", sibling), copy(8, nx, "hi", sibling), copy(9, ny, "hi", sibling),
                 copy(10, ny, "lo", sibling), copy(11, nd, "lo", sibling), copy(12, nd, "hi", sibling)]
        sx, sy, sd = (*by_x, 1 - ac), (*by_y, 1 - ac), (*diag, 1 - ac)
        arrivals = [copy(0, sibling, "all", me), copy(1, nx, "lo", me), copy(2, nx, "hi", me), copy(3, ny, "hi", me),
                    copy(4, ny, "lo", me), copy(5, nd, "lo", me), copy(6, nd, "hi", me), copy(7, sx, "lo", me),
                    copy(8, sx, "hi", me), copy(9, sy, "hi", me), copy(10, sy, "lo", me), copy(11, sd, "lo", me),
                    copy(12, sd, "hi", me)]
        for k in range(5):
            sends[k].start()
        for arrived, onward in ((1, (5, 7)), (3, (6, 9)), (2, (8,)), (4, (10,)), (5, (11,)), (6, (12,))):
            arrivals[arrived].wait_recv()
            for k in onward:
                sends[k].start()
        for k in (0, 7, 8, 9, 10, 11, 12):
            arrivals[k].wait_recv()
        for cp in sends:
            cp.wait_send()

    return _on_sequencer(body, [x], [jax.ShapeDtypeStruct((NDEV,) + x.shape, x.dtype)],
                         [pltpu.SemaphoreType.DMA((13,)), pltpu.SemaphoreType.DMA((13,))], name=name, collective_id=collective_id)[0]


def _scatter_async(parts, name, collective_id):
    rider = _scatter_rider(parts)

    def body(in_refs, out_refs, send_sems, recv_sems):
        ax, ay, ac = _position()
        flip = lambda a, on: 1 - a if on else a
        _handshake([(flip(ax, rel & 4), flip(ay, rel & 2), flip(ac, rel & 1)) for rel in range(1, NDEV)])
        for hook in rider.hooks(in_refs, out_refs, send_sems, recv_sems):
            hook()

    return _on_sequencer(body, rider.arrays, rider.out_shapes, rider.sems, name=name, collective_id=collective_id)[0]


def _sum_in_chip(own, recv, name):
    _, r, w = own.shape
    tr = _tile(r, (256, 128))

    def body(a_ref, b_ref, o_ref):
        o_ref[...] = (a_ref[...].astype(F32) + b_ref[...].astype(F32)).astype(o_ref.dtype)

    blk = pl.BlockSpec((None, tr, w), lambda q, i: (q, i, 0))
    return pl.pallas_call(body, name=name, grid=(4, r // tr), in_specs=[blk, blk], out_specs=blk,
                          out_shape=jax.ShapeDtypeStruct(own.shape, own.dtype),
                          compiler_params=_cparams("parallel", "parallel"))(own, recv)


def _sum_chips(s1, recv, chip, name):
    _, r, w = s1.shape
    tr = _tile(r, (256, 128))

    def body(c_ref, s_ref, r0_ref, r1_ref, r2_ref, o_ref):
        f = lambda ref: ref[...].astype(F32)
        o_ref[...] = ((f(s_ref) + f(r0_ref)) + f(r1_ref)) + f(r2_ref)

    rblk = lambda k: pl.BlockSpec((None, tr, w), functools.partial(lambda i, c, k: (k, i, 0), k=k))
    grid_spec = pltpu.PrefetchScalarGridSpec(
        num_scalar_prefetch=1, grid=(r // tr,),
        in_specs=[pl.BlockSpec((None, tr, w), lambda i, c: (c[0], i, 0)), rblk(0), rblk(1), rblk(2)],
        out_specs=pl.BlockSpec((tr, w), lambda i, c: (i, 0)))
    return pl.pallas_call(body, name=name, grid_spec=grid_spec, out_shape=jax.ShapeDtypeStruct((r, w), F32),
                          compiler_params=_cparams("parallel"))(chip, s1, recv, recv, recv)


def _silu_rows(x, name):
    def body(x_ref, o_ref):
        o_ref[...] = _silu(x_ref[...])

    return pl.pallas_call(body, name=name, out_shape=jax.ShapeDtypeStruct(x.shape, F32))(x)


def _row_sum(x, name):
    def body(x_ref, o_ref):
        acc = x_ref[0:1, :]
        for i in range(1, x.shape[0]):
            acc = acc + x_ref[i:i + 1, :]
        o_ref[...] = acc

    return pl.pallas_call(body, name=name, out_shape=jax.ShapeDtypeStruct((1, x.shape[1]), F32))(x)


def _adamw(w, g, m, v, name):
    cols = w.shape[-1]
    rows = w.size // cols
    tr = _tile(rows, (256, 128))
    tc = LANE if (tr == rows and rows > 512 and cols % LANE == 0) else cols

    def body(w_ref, g_ref, m_ref, v_ref, d_ref, mo_ref, vo_ref):
        grad = g_ref[...]
        m_new = ADAM_B1 * m_ref[...] + (1.0 - ADAM_B1) * grad
        v_new = ADAM_B2 * v_ref[...] + (1.0 - ADAM_B2) * jnp.square(grad)
        m_hat = m_new / (1.0 - ADAM_B1 ** ADAM_STEP)
        v_hat = v_new / (1.0 - ADAM_B2 ** ADAM_STEP)
        d_ref[...] = -ADAM_LR * (m_hat / (jnp.sqrt(v_hat) + ADAM_EPS) + ADAM_WD * w_ref[...])
        mo_ref[...] = m_new
        vo_ref[...] = v_new

    blk = pl.BlockSpec((tr, tc), lambda i, j: (i, j))
    out = pl.pallas_call(
        body, name=name, grid=(rows // tr, cols // tc), in_specs=[blk] * 4, out_specs=[blk] * 3,
        out_shape=[jax.ShapeDtypeStruct((rows, cols), F32)] * 3, compiler_params=_cparams("parallel", "parallel"),
    )(*[t.reshape(rows, cols) for t in (w, g, m, v)])
    return [t.reshape(w.shape) for t in out]


def _pack(parts, width, row_mult, dtype):
    flat = jnp.concatenate([p.reshape(-1).astype(dtype) for p in parts])
    rows = -(-flat.shape[0] // (width * row_mult)) * row_mult
    return jnp.pad(flat, (0, rows * width - flat.shape[0])).reshape(rows, width)


def _unpack(flat, shapes):
    out, off = [], 0
    for shp in shapes:
        size = 1
        for dim in shp:
            size *= dim
        out.append(flat[:, off:off + size].reshape((flat.shape[0],) + tuple(shp)))
        off += size
    return out


def _devices_to_cols(a):
    _, r, c = a.shape
    return a.transpose(1, 0, 2).reshape(r, NDEV * c)


def kernel(x, c, w_ada, b_ada, norm1_w, w_in, gdn_conv_w, gdn_a_log, gdn_dt_bias, gdn_norm_w, w_gdn_proj, sc_conv_w, w_sc_out, w_o, norm2_w, w_ffn_in, w_ffn_out, w_ada_f, b_ada_f, normf_w, loss_target, m_w_ada, m_b_ada, m_norm1_w, m_w_in, m_gdn_conv_w, m_gdn_a_log, m_gdn_dt_bias, m_gdn_norm_w, m_w_gdn_proj, m_sc_conv_w, m_w_sc_out, m_w_o, m_norm2_w, m_w_ffn_in, m_w_ffn_out, m_w_ada_f, m_b_ada_f, m_normf_w, v_w_ada, v_b_ada, v_norm1_w, v_w_in, v_gdn_conv_w, v_gdn_a_log, v_gdn_dt_bias, v_gdn_norm_w, v_w_gdn_proj, v_sc_conv_w, v_w_sc_out, v_w_o, v_norm2_w, v_w_ffn_in, v_w_ffn_out, v_w_ada_f, v_b_ada_f, v_normf_w):
    bl, s, d = x.shape
    heads = gdn_a_log.shape[-1]
    dff = w_ffn_out.shape[1] * NDEV
    tok = bl * s
    ax, ay, ac = _position()
    dev = 4 * ax + 2 * ay + ac
    as_tok = lambda a: a.reshape(bl, s, a.shape[-1])
    as_mat = lambda a: a.reshape(tok, a.shape[-1])

    small = _all_gather(_pack([c, gdn_conv_w, sc_conv_w], LANE, 8, F32), name="gather_cond")
    c_all, conv_w, sc_w = _unpack(small.reshape(NDEV, -1), [(bl, d), gdn_conv_w.shape[1:], sc_conv_w.shape[1:]])
    c_act = _silu_rows(c_all.reshape(NDEV * bl, d), "cond_silu")
    conv_w, sc_w = _devices_to_cols(conv_w), _devices_to_cols(sc_w)
    n_ada, n_adaf = w_ada.shape[-1], w_ada_f.shape[-1]
    bias = jnp.broadcast_to(lax.dynamic_slice_in_dim(b_ada, dev * n_ada, n_ada, axis=1), (NDEV * bl, n_ada))
    biasf = jnp.broadcast_to(lax.dynamic_slice_in_dim(b_ada_f.reshape(1, -1), dev * n_adaf, n_adaf, axis=1), (NDEV * bl, n_adaf))
    mod_cols = _mm(c_act, w_ada[0], add=bias, name="ada_cols")
    modf_cols = _mm(c_act, w_ada_f, add=biasf, name="adaf_cols")
    mods = _all_gather(jnp.concatenate([mod_cols, modf_cols], axis=1), name="gather_mod")
    mod_all = mods[:, :, :n_ada].transpose(1, 0, 2).reshape(NDEV * bl, NDEV * n_ada)
    modf_all = mods[:, :, n_ada:].transpose(1, 0, 2).reshape(NDEV * bl, NDEV * n_adaf)
    my_rows = lambda a: lax.dynamic_slice_in_dim(a, dev * bl, bl, axis=0)
    sh1, sc1, g1, sh2, sc2, g2 = [t.reshape(bl, 1, d) for t in jnp.split(my_rows(mod_all), 6, axis=1)]
    shf, scf = [t.reshape(bl, 1, d) for t in jnp.split(my_rows(modf_all), 2, axis=1)]

    late = [t.astype(MXU_DTYPE) for t in (w_gdn_proj[0], w_sc_out[0], w_o[0], w_ffn_in[0].T, w_ffn_out[0])]
    rows = [t.shape[0] for t in late] + [w_in.shape[-1]]
    offs = [sum(rows[:i]) for i in range(5)]
    in_send = w_in[0].T.astype(MXU_DTYPE)
    with_own = lambda g, own: lax.dynamic_update_slice_in_dim(g, own[None], dev, axis=0)
    wt_in = with_own(_gather_balanced(in_send, "gather_w_in", 1), in_send).reshape(NDEV * rows[5], d)
    gathered = _gather_async(late[:3], "gather_mixer", 2) + _gather_async(late[3:], "gather_ffn", 3)
    wgp, wso, wo, wt_fi, wfo = [with_own(g, own).reshape(NDEV * own.shape[0], d) for g, own in zip(gathered, late)]
    o_z, o_ab, o_sc, o_ga, o_gb = 3 * d, 4 * d, 4 * d + 2 * heads, 7 * d + 2 * heads, 8 * d + 2 * heads
    s_qkv, s_z, s_sc, s_gate = (0, o_z), (o_z, d), (o_sc, 3 * d), (o_ga, 2 * d)
    wt_ab = jnp.pad(wt_in[o_ab:o_sc], ((0, LANE - 2 * heads), (0, 0)))

    n1w, n2w, nfw = norm1_w.reshape(1, d), norm2_w.reshape(1, d), normf_w.reshape(1, d)
    lanes = lambda a: jnp.pad(a.reshape(1, -1), ((0, 0), (0, LANE - a.size)))
    a_log, dt_bias, gnw = lanes(gdn_a_log), lanes(gdn_dt_bias), gdn_norm_w.reshape(1, HEAD)
    f_gates = functools.partial(_f_gates, heads=heads)
    (h1,) = _tok_fwd(_f_norm_mod, [x], [sh1, sc1], [n1w], [(d, MXU_DTYPE)], name="norm1", ts=512)
    h1m = as_mat(h1)
    p_qkv = as_tok(_mm(h1m, wt_in, tb=True, b_rows=s_qkv, name="in_qkv"))
    p_z = as_tok(_mm(h1m, wt_in, tb=True, b_rows=s_z, name="in_z"))
    p_ab = as_tok(_mm(h1m, wt_ab, tb=True, name="in_ab"))
    p_sc = as_tok(_mm(h1m, wt_in, tb=True, b_rows=s_sc, name="in_sc"))
    p_g = as_tok(_mm(h1m, wt_in, tb=True, b_rows=s_gate, name="in_gate"))
    qkv = _qkv_fwd(p_qkv, conv_w, heads, "qkv_conv")
    (gbeta,) = _tok_fwd(f_gates, [p_ab], [], [a_log, dt_bias], [(LANE, F32)], name="gates", ts=512)
    o, s_all, t_all = _gdn_fwd(qkv, gbeta, heads, "gdn")
    (og,) = _tok_fwd(_f_gdn_out, [o, p_z], [], [(gnw, None)], [(d, MXU_DTYPE)], name="gdn_out", ts=2048, wb=HEAD, cols=heads)
    y_a = as_tok(_mm(as_mat(og), wgp, name="gdn_proj"))
    scp = _sc_fwd(p_sc, sc_w, "sc_conv")
    mrg, y_b = _tok_fwd(_f_merge_keep, [(p_g, 0), (p_g, 1), y_a, _Product(scp, wso)], [], [], [(d, MXU_DTYPE), (d, F32)],
                        name="merge", ts=512, wb=d)
    merge_toks = [(p_g, 0), (p_g, 1), y_a, y_b]
    x2, h2, mix = _tok_fwd(_f_res_norm_mod_keep, [x, _Product(mrg, wo)], [g1, sh2, sc2], [n2w],
                           [(d, F32), (d, MXU_DTYPE), (d, F32)], name="norm2", ts=512)
    act, gu_a, gu_b = _ffn_in_swiglu(as_mat(h2), wt_fi, dff, "ffn_in")

    loss_l, (dx2, dff_out, _), (dg2, dshf, dscf), (dnfw,) = _tok_bwd(
        _f_loss, [x2, _Product(as_tok(act), wfo), loss_target], [g2, shf, scf], [nfw], [], [True, True, False], name="loss",
        ts=512, loss=True, tok_dtype=[F32, MXU_DTYPE, None])
    dffm = as_mat(dff_out)
    dgu_a, dgu_b = _ffn_out_bwd_swiglu(dffm, wfo, gu_a, gu_b, "d_ffn_out")
    gmm = functools.partial(_mm, ta=True, out_dtype=MXU_DTYPE)
    gw_ffn_out = gmm(act, dffm, name="g_ffn_out")
    dh2 = _Product(as_tok(dgu_b), wt_fi, b_rows=(dff, dff), add=_Product(as_tok(dgu_a), wt_fi, b_rows=(0, dff)))
    h2m = as_mat(h2)
    gwt_ffn_in = gmm(dgu_a, h2m, out_rows=2 * dff, name="g_ffn_in_a")
    gwt_ffn_in = gmm(dgu_b, h2m, out_rows=2 * dff, row_off=dff, into=gwt_ffn_in, name="g_ffn_in_b")
    ffn_parts = [(gwt_ffn_in, rows[3]), (gw_ffn_out, rows[4])]
    ffn_recv = _scatter_async(ffn_parts, "scatter_ffn", 4)
    (dx_skip, dmix), (dg1, dsh2, dsc2), (dn2w,) = _tok_bwd(
        _f_res_norm_mod, [x, mix], [g1, sh2, sc2], [n2w], [dx2, dh2], [True, True], name="d_norm2", ts=512,
        tok_dtype=[F32, MXU_DTYPE], after=[gwt_ffn_in, gw_ffn_out])
    gw_o = gmm(as_mat(mrg), as_mat(dmix), name="g_mix_out")
    (dga, dgb, dya, dyb), _, _ = _tok_bwd(_f_merge, merge_toks, [], [], [_Product(dmix, wo, tb=True)], [True] * 4,
                                          name="d_merge", ts=512, wb=d, tok_dtype=MXU_DTYPE)
    dyam, dybm = as_mat(dya), as_mat(dyb)
    dog = as_tok(_mm(dyam, wgp, tb=True, name="d_gdn_proj"))
    gw_gdn_proj = gmm(as_mat(og), dyam, name="g_gdn_proj")
    dscp = as_tok(_mm(dybm, wso, tb=True, name="d_sc_out"))
    gw_sc_out = gmm(as_mat(scp), dybm, name="g_sc_out")
    dsc, g_sc_w = _sc_bwd(p_sc, sc_w, dscp, "d_sc_conv")
    mix_parts = [(gw_gdn_proj, rows[0]), (gw_sc_out, rows[1]), (gw_o, rows[2])]
    mix_recv = _scatter_async(mix_parts, "scatter_mixer", 5)
    (do, dz), _, (g_gnw,) = _tok_bwd(_f_gdn_out, [o, p_z], [], [(gnw, None)], [dog], [True, True], name="d_gdn_out",
                                     ts=2048, wb=HEAD, cols=heads, tok_dtype=[F32, MXU_DTYPE],
                                     after=[gw_gdn_proj, gw_sc_out, gw_o])
    own_rows = lambda parts: jnp.concatenate([lax.dynamic_slice_in_dim(g, dev * r, r, axis=0) for g, r in parts], axis=0)
    dqkv, dgbeta = _gdn_bwd(qkv, gbeta, do, s_all, t_all, heads, "d_gdn")
    dp_qkv, g_conv_w = _qkv_bwd(p_qkv, conv_w, dqkv, heads, "d_qkv_conv")
    ffn_red = _sum_direct(own_rows(ffn_parts), ffn_recv, "sum_ffn")
    mix_red = _sum_direct(own_rows(mix_parts), mix_recv, "sum_mix")
    (dp_ab,), _, (g_a_log, g_dt_bias) = _tok_bwd(f_gates, [p_ab], [], [a_log, dt_bias], [dgbeta], [True], name="d_gates",
                                                 ts=512, tok_dtype=MXU_DTYPE, after=[ffn_red, mix_red])
    r_in = rows[5]
    win = -(-(r_in + max(r_in * k % ROW_ALIGN for k in range(NDEV))) // 128) * 128
    need_rows = max(_window_start(r_in, k) for k in range(NDEV)) + win
    dsc_m = dsc.reshape(3, tok, d)
    gwt_in = ([gmm(as_mat(dp_qkv), h1m, name="g_in_qkv"), gmm(as_mat(dz), h1m, name="g_in_z"),
               gmm(as_mat(dp_ab), h1m, name="g_in_ab")[:2 * heads]]
              + [gmm(dsc_m, h1m, a_index=k, name=f"g_in_sc{k}") for k in range(3)]
              + [gmm(as_mat(dga), h1m, name="g_in_ga"), gmm(as_mat(dgb), h1m, name="g_in_gb")])
    gwt_in = jnp.concatenate(gwt_in + [jnp.zeros((need_rows - NDEV * r_in, d), MXU_DTYPE)], axis=0)
    assert d <= 1024
    wide = [as_mat(dp_qkv), as_mat(dz), dsc_m, as_mat(dga)]
    row_of = lambda t: d * t + jnp.where(t * d >= o_ab, 2 * heads, 0)
    recv1 = _exchange_in_chip([(gwt_in, r_in, win, 0)], "scatter_in_chip", 7)
    own = jnp.stack([lax.dynamic_slice_in_dim(gwt_in, _window_start(r_in, 2 * q + ac), win, axis=0) for q in range(4)])
    s1 = _sum_in_chip(own, recv1, "sum_in_chip")
    recv2 = _exchange_chips_async(s1, "scatter_chips", 6)

    dh1 = _mm(as_mat(dp_ab), wt_ab, name="d_in_ab")
    dh1 = _mm_chain(wide, wt_in, row_of, add=dh1, name="d_in", tk=d)
    dh1 = _Product(dgb, wt_in, b_rows=(o_gb, d), add=as_tok(dh1))
    (grad_x,), (dsh1, dsc1), (dn1w,) = _tok_bwd(_f_norm_mod_skip, [x], [sh1, sc1], [n1w], [dh1, dx_skip], [True],
                                                name="d_norm1", ts=512)
    reduced = _sum_chips(s1, recv2, (2 * ax + ay).reshape(1).astype(jnp.int32), "sum_chips")
    gt_w_in = lax.dynamic_slice_in_dim(reduced, r_in * dev - _window_start(r_in, dev), r_in, axis=0)
    g_w_in = gt_w_in.T.reshape(w_in.shape)
    gt_w_ffn_in = ffn_red[:rows[3]]
    g_w_ffn_in = gt_w_ffn_in.T.reshape(w_ffn_in.shape)
    g_w_ffn_out = ffn_red[rows[3]:].reshape(w_ffn_out.shape)
    g_w_gdn_proj, g_w_sc_out, g_w_o = (mix_red[offs[i]:offs[i] + rows[i]].reshape(ref.shape)
                                       for i, ref in enumerate((w_gdn_proj, w_sc_out, w_o)))

    dmod = jnp.concatenate([t.reshape(bl, d) for t in (dsh1, dsc1, dg1, dsh2, dsc2, dg2)], axis=1)
    dmodf = jnp.concatenate([t.reshape(bl, d) for t in (dshf, dscf)], axis=1)
    summed_parts = [dn1w, dn2w, dnfw, g_gnw, g_a_log, g_dt_bias, g_conv_w, g_sc_w, loss_l]
    partial = _all_gather(_pack([dmod, dmodf] + summed_parts, LANE, 8, F32), name="gather_small")
    partial = partial.reshape(NDEV, -1)
    n_rows = bl * (6 * d + 2 * d)
    dmod_all, dmodf_all = _unpack(partial[:, :n_rows], [(bl, 6 * d), (bl, 2 * d)])
    dmod_all, dmodf_all = dmod_all.reshape(NDEV * bl, 6 * d), dmodf_all.reshape(NDEV * bl, 2 * d)
    totals = _row_sum(partial[:, n_rows:], "sum_small")
    t_n1w, t_n2w, t_nfw, t_gnw, t_a_log, t_dt_bias, t_conv_w, t_sc_w, t_loss = [
        t[0] for t in _unpack(totals, [p.shape for p in summed_parts])]
    my_cols = lambda a, n: lax.dynamic_slice_in_dim(a, dev * n, n, axis=1)
    grads = {
        "w_ada": _mm(c_act, my_cols(dmod_all, n_ada), ta=True, name="g_ada").reshape(w_ada.shape),
        "b_ada": _row_sum(dmod_all, "g_ada_bias").reshape(b_ada.shape),
        "norm1_w": t_n1w.reshape(norm1_w.shape),
        "w_in": g_w_in,
        "gdn_conv_w": my_cols(t_conv_w, gdn_conv_w.shape[-1]).reshape(gdn_conv_w.shape),
        "gdn_a_log": t_a_log[:, :heads].reshape(gdn_a_log.shape),
        "gdn_dt_bias": t_dt_bias[:, :heads].reshape(gdn_dt_bias.shape),
        "gdn_norm_w": t_gnw.reshape(gdn_norm_w.shape),
        "w_gdn_proj": g_w_gdn_proj,
        "sc_conv_w": my_cols(t_sc_w, sc_conv_w.shape[-1]).reshape(sc_conv_w.shape),
        "w_sc_out": g_w_sc_out,
        "w_o": g_w_o,
        "norm2_w": t_n2w.reshape(norm2_w.shape),
        "w_ffn_in": g_w_ffn_in,
        "w_ffn_out": g_w_ffn_out,
        "w_ada_f": _mm(c_act, my_cols(dmodf_all, n_adaf), ta=True, name="g_adaf").reshape(w_ada_f.shape),
        "b_ada_f": _row_sum(dmodf_all, "g_adaf_bias").reshape(b_ada_f.shape),
        "normf_w": t_nfw.reshape(normf_w.shape),
    }
    weights = dict(w_ada=w_ada, b_ada=b_ada, norm1_w=norm1_w, w_in=w_in, gdn_conv_w=gdn_conv_w, gdn_a_log=gdn_a_log,
                   gdn_dt_bias=gdn_dt_bias, gdn_norm_w=gdn_norm_w, w_gdn_proj=w_gdn_proj, sc_conv_w=sc_conv_w,
                   w_sc_out=w_sc_out, w_o=w_o, norm2_w=norm2_w, w_ffn_in=w_ffn_in, w_ffn_out=w_ffn_out, w_ada_f=w_ada_f,
                   b_ada_f=b_ada_f, normf_w=normf_w)
    m_in = [m_w_ada, m_b_ada, m_norm1_w, m_w_in, m_gdn_conv_w, m_gdn_a_log, m_gdn_dt_bias, m_gdn_norm_w, m_w_gdn_proj,
            m_sc_conv_w, m_w_sc_out, m_w_o, m_norm2_w, m_w_ffn_in, m_w_ffn_out, m_w_ada_f, m_b_ada_f, m_normf_w]
    v_in = [v_w_ada, v_b_ada, v_norm1_w, v_w_in, v_gdn_conv_w, v_gdn_a_log, v_gdn_dt_bias, v_gdn_norm_w, v_w_gdn_proj,
            v_sc_conv_w, v_w_sc_out, v_w_o, v_norm2_w, v_w_ffn_in, v_w_ffn_out, v_w_ada_f, v_b_ada_f, v_normf_w]
    deltas, new_m, new_v = [], [], []
    grads_t = {"w_in": gt_w_in, "w_ffn_in": gt_w_ffn_in}
    for (wname, wt), mt, vt in zip(weights.items(), m_in, v_in):
        if wname in grads_t:
            back = lambda a, wt=wt: a.T.reshape(wt.shape)
            dl, mn, vn = (back(a) for a in _adamw(wt[0].T, grads_t[wname], mt[0].T, vt[0].T, "adamw_" + wname))
        else:
            dl, mn, vn = _adamw(wt, grads[wname], mt, vt, "adamw_" + wname)
        deltas.append(dl)
        new_m.append(mn)
        new_v.append(vn)
    loss = t_loss[0, 0]
    return (loss, grad_x, *[grads[k] for k in weights], *deltas, *new_m, *new_v)
```

```python
import functools

import jax
import jax.numpy as jnp
from jax import lax
from jax.experimental import pallas as pl
from jax.experimental.pallas import tpu as pltpu
from jax.experimental.pallas import tpu_sc as plsc

F32 = jnp.float32
MXU_DTYPE = jnp.bfloat16
NDEV = 8
CHUNK = 64
HEAD = 128
LANE = 128
EPS = 1e-6
ADAM_LR, ADAM_B1, ADAM_B2, ADAM_EPS, ADAM_WD, ADAM_STEP = 0.001, 0.9, 0.999, 1e-08, 0.01, 10
VMEM_LIMIT = 48 * 1024 * 1024
MESH_IDS = pl.DeviceIdType.MESH
HIGHEST = lax.Precision.HIGHEST


def _tile(n, cands=(512, 256, 128)):
    for c in cands:
        if n % c == 0:
            return c
    return n


def _cparams(*sem):
    return pltpu.CompilerParams(dimension_semantics=sem, vmem_limit_bytes=VMEM_LIMIT)


def _mm(a, b, *, ta=False, tb=False, add=None, out_dtype=F32, name, b_rows=None, out_rows=None, row_off=0, into=None,
        a_index=None):
    m, k = (a.shape[-1], a.shape[-2]) if ta else a.shape[-2:]
    b_shape = b.shape if b_rows is None else (b_rows[1], b.shape[1])
    n = b_shape[0] if tb else b_shape[1]
    assert k == (b_shape[1] if tb else b_shape[0])
    if ta:
        tm, tn = _tile(m), n if n <= 1024 else _tile(n)
        tk = k if k <= 4096 else _tile(k, (4096, 2048, 1024, 512))
        if tm * tk > 1024 * 2048:
            tk = _tile(k, (2048, 1024, 512))
    else:
        tk = k if k <= 1024 else _tile(k, (1024, 512))
        tn = _tile(n, (1024 if tk <= 1024 else 512, 512, 256, 128))
        tm = _tile(m, (2048 if (tn <= 512 and tk <= 1024) else 1024, 1024, 512, 256, 128))
    nk = k // tk
    dims = (((0 if ta else 1,), (1 if tb else 0,)), ((), ()))
    has_add = add is not None

    def body(*refs):
        a_ref, b_ref = refs[0], refs[1]
        add_ref = refs[2] if has_add else None
        o_ref = refs[2 + has_add + (into is not None)]
        part = lax.dot_general(a_ref[...].astype(MXU_DTYPE), b_ref[...].astype(MXU_DTYPE), dims,
                               preferred_element_type=F32)

        def finish(acc):
            if has_add:
                acc = acc + add_ref[...]
            o_ref[...] = acc.astype(o_ref.dtype)

        if nk == 1:
            finish(part)
        else:
            acc_ref = refs[-1]
            kk = pl.program_id(2)

            @pl.when(kk == 0)
            def _():
                acc_ref[...] = part

            @pl.when(kk > 0)
            def _():
                acc_ref[...] += part

            @pl.when(kk == nk - 1)
            def _():
                finish(acc_ref[...])

    a_blk, a_at = ((tk, tm), lambda i, j, kk: (kk, i)) if ta else ((tm, tk), lambda i, j, kk: (i, kk))
    a_spec = (pl.BlockSpec(a_blk, a_at) if a_index is None else
              pl.BlockSpec((None,) + a_blk, lambda i, j, kk: (a_index,) + a_at(i, j, kk)))
    if b_rows is None:
        b_spec = pl.BlockSpec((tn, tk), lambda i, j, kk: (j, kk)) if tb else pl.BlockSpec((tk, tn), lambda i, j, kk: (kk, j))
    else:
        at = lambda t: pl.multiple_of(b_rows[0] + t, ROW_ALIGN)
        b_spec = (pl.BlockSpec((pl.Element(tn), pl.Element(tk)), lambda i, j, kk: (at(j * tn), kk * tk)) if tb else
                  pl.BlockSpec((pl.Element(tk), pl.Element(tn)), lambda i, j, kk: (at(kk * tk), j * tn)))
    add_spec = pl.BlockSpec((tm, tn), lambda i, j, kk: (i, j))
    assert row_off % tm == 0
    o_spec = pl.BlockSpec((tm, tn), lambda i, j, kk: (i + row_off // tm, j))
    in_specs = [a_spec, b_spec] + ([add_spec] if has_add else []) + ([pl.BlockSpec(memory_space=pl.ANY)] if into is not None else [])
    args = [a, b] + ([add] if has_add else []) + ([into] if into is not None else [])
    return pl.pallas_call(
        body, name=name, grid=(m // tm, n // tn, nk), in_specs=in_specs, out_specs=o_spec,
        out_shape=jax.ShapeDtypeStruct((out_rows or m, n), out_dtype),
        scratch_shapes=[pltpu.VMEM((tm, tn), F32)] if nk > 1 else [],
        input_output_aliases={len(args) - 1: 0} if into is not None else {},
        compiler_params=_cparams("parallel", "parallel", "arbitrary"),
    )(*args)


RING = 3


def _mm_ring(a, b, b_rows, *, name):
    (m, k), (start, n) = a.shape, b_rows
    assert b.shape[1] == k and b.dtype == MXU_DTYPE
    tm, tn = _tile(m, (1024, 512, 256, 128)), _tile(n, (1024, 512, 256, 128))
    nj = n // tn
    steps = (m // tm) * nj

    def body(a_ref, b_hbm, o_ref, ring, sems):
        step = pl.program_id(0) * nj + pl.program_id(1)

        def tile_copy(t):
            rows = pl.ds(pl.multiple_of(start + (t % nj) * tn, ROW_ALIGN), tn)
            return pltpu.make_async_copy(b_hbm.at[rows, :], ring.at[t % RING], sems.at[t % RING])

        @pl.when(step == 0)
        def _():
            for t in range(min(RING - 1, steps)):
                tile_copy(jnp.int32(t)).start()

        @pl.when(step + RING - 1 < steps)
        def _():
            tile_copy(step + RING - 1).start()

        tile_copy(step).wait()
        o_ref[...] = lax.dot_general(a_ref[...].astype(MXU_DTYPE), ring[step % RING], (((1,), (1,)), ((), ())),
                                     preferred_element_type=F32)

    return pl.pallas_call(
        body, name=name, grid=(m // tm, nj),
        in_specs=[pl.BlockSpec((tm, k), lambda i, j: (i, 0)), pl.BlockSpec(memory_space=pl.ANY)],
        out_specs=pl.BlockSpec((tm, tn), lambda i, j: (i, j)), out_shape=jax.ShapeDtypeStruct((m, n), F32),
        scratch_shapes=[pltpu.VMEM((RING, tn, k), MXU_DTYPE), pltpu.SemaphoreType.DMA((RING,))],
        compiler_params=_cparams("arbitrary", "arbitrary"),
    )(a, b)


def _mm_chain(parts, b, row_of_tile, *, add, name, tk=1024, tm=1024):
    m, n = parts[0].shape[-2], b.shape[1]
    tm = min(tm, m)
    tiles = [p.shape[0] if p.ndim == 3 else p.shape[1] // tk for p in parts]
    first = [sum(tiles[:s]) for s in range(len(parts))]
    nk = sum(tiles)

    def body(*refs):
        a_refs, b_ref, add_ref, o_ref, acc_ref = refs[:len(parts)], *refs[len(parts):]
        kk = pl.program_id(1)

        @pl.when(kk == 0)
        def _():
            acc_ref[...] = add_ref[...]

        for a_ref, lo, cnt in zip(a_refs, first, tiles):
            @pl.when(jnp.logical_and(kk >= lo, kk < lo + cnt))
            def _(a_ref=a_ref):
                acc_ref[...] += lax.dot_general(a_ref[...].astype(MXU_DTYPE), b_ref[...].astype(MXU_DTYPE),
                                                (((1,), (0,)), ((), ())), preferred_element_type=F32)

        @pl.when(kk == nk - 1)
        def _():
            o_ref[...] = acc_ref[...]

    tile_of = lambda kk, lo, cnt: jnp.clip(kk - lo, 0, cnt - 1)
    a_specs = [pl.BlockSpec((None, tm, tk), functools.partial(lambda i, kk, lo, cnt: (tile_of(kk, lo, cnt), i, 0), lo=lo, cnt=cnt))
               if p.ndim == 3 else
               pl.BlockSpec((tm, tk), functools.partial(lambda i, kk, lo, cnt: (i, tile_of(kk, lo, cnt)), lo=lo, cnt=cnt))
               for p, lo, cnt in zip(parts, first, tiles)]
    b_spec = pl.BlockSpec((pl.Element(tk), pl.Element(n)), lambda i, kk: (pl.multiple_of(row_of_tile(kk), ROW_ALIGN), 0))
    o_spec = pl.BlockSpec((tm, n), lambda i, kk: (i, 0))
    return pl.pallas_call(
        body, name=name, grid=(m // tm, nk), in_specs=a_specs + [b_spec, o_spec], out_specs=o_spec,
        out_shape=jax.ShapeDtypeStruct((m, n), F32), scratch_shapes=[pltpu.VMEM((tm, n), F32)],
        compiler_params=_cparams("parallel", "arbitrary"),
    )(*parts, b, add)


def _swiglu_tiles(m, half):
    tn = _tile(half, (512, 256, 128))
    return _tile(m, (2048 if tn <= 256 else 1024, 1024, 512, 256, 128)), tn


def _ffn_in_swiglu(h, wt, half, name):
    m, k = h.shape
    tm, tn = _swiglu_tiles(m, half)
    nj = half // tn
    dims = (((1,), (1,)), ((), ()))

    def body(h_ref, wa_ref, wb_ref, act_ref, a_ref, b_ref):
        lhs = h_ref[...].astype(MXU_DTYPE)
        a = lax.dot_general(lhs, wa_ref[...].astype(MXU_DTYPE), dims, preferred_element_type=F32)
        b = lax.dot_general(lhs, wb_ref[...].astype(MXU_DTYPE), dims, preferred_element_type=F32)
        act_ref[...] = (_silu(a) * b).astype(act_ref.dtype)
        a_ref[...] = a.astype(a_ref.dtype)
        b_ref[...] = b.astype(b_ref.dtype)

    out = jax.ShapeDtypeStruct((m, half), MXU_DTYPE)
    oblk = pl.BlockSpec((tm, tn), lambda i, j: (i, j))
    return pl.pallas_call(
        body, name=name, grid=(m // tm, nj),
        in_specs=[pl.BlockSpec((tm, k), lambda i, j: (i, 0)), pl.BlockSpec((tn, k), lambda i, j: (j, 0)),
                  pl.BlockSpec((tn, k), lambda i, j: (j + nj, 0))],
        out_specs=[oblk, oblk, oblk], out_shape=[out, out, out], compiler_params=_cparams("parallel", "parallel"),
    )(h, wt, wt)


def _ffn_out_bwd_swiglu(dff, w, a, b, name):
    m, k = dff.shape
    half = w.shape[0]
    tm, tn = _swiglu_tiles(m, half)

    def body(d_ref, w_ref, a_ref, b_ref, da_ref, db_ref):
        dact = lax.dot_general(d_ref[...].astype(MXU_DTYPE), w_ref[...].astype(MXU_DTYPE), (((1,), (1,)), ((), ())),
                               preferred_element_type=F32)
        av, bv = a_ref[...].astype(F32), b_ref[...].astype(F32)
        sig = jax.nn.sigmoid(av)
        da_ref[...] = (dact * bv * (sig * (1.0 + av * (1.0 - sig)))).astype(da_ref.dtype)
        db_ref[...] = (dact * (av * sig)).astype(db_ref.dtype)

    out = jax.ShapeDtypeStruct((m, half), MXU_DTYPE)
    oblk = pl.BlockSpec((tm, tn), lambda i, j: (i, j))
    return pl.pallas_call(
        body, name=name, grid=(m // tm, half // tn),
        in_specs=[pl.BlockSpec((tm, k), lambda i, j: (i, 0)), pl.BlockSpec((tn, k), lambda i, j: (j, 0)), oblk, oblk],
        out_specs=[oblk, oblk], out_shape=[out, out], compiler_params=_cparams("parallel", "parallel"),
    )(dff, w, a, b)


def _with_off(xs):
    return [x if isinstance(x, tuple) else (x, 0) for x in xs]


def _spec(kind, arr, off, ts, wb):
    w = arr.shape[-1] if wb is None else wb
    col = (lambda j: 0) if wb is None else functools.partial(lambda j, o: o + j, o=off)
    if kind == "tok":
        return pl.BlockSpec((None, ts, w), lambda j, b, i: (b, i, col(j)))
    if kind == "bat":
        return pl.BlockSpec((None, 1, w), lambda j, b, i: (b, 0, col(j)))
    if off is None:
        return pl.BlockSpec(arr.shape, lambda j, b, i: (0, 0))
    return pl.BlockSpec((arr.shape[0], w), lambda j, b, i: (0, col(j)))


class _Product:
    def __init__(self, a, b, *, tb=False, b_rows=None, add=None):
        self.a, self.b, self.tb, self.b_rows, self.add = a, b, tb, b_rows, add
        rows = b.shape[0] if b_rows is None else b_rows[1]
        self.shape = a.shape[:2] + (rows if tb else b.shape[1],)

    def inputs(self, ts):
        a_spec = pl.BlockSpec((None, ts, self.a.shape[2]), lambda j, b, i: (b, i, 0))
        if self.b_rows is None:
            b_spec = pl.BlockSpec(self.b.shape, lambda j, b, i: (0, 0))
        else:
            start, count = self.b_rows
            b_spec = pl.BlockSpec((pl.Element(count), pl.Element(self.b.shape[1])), lambda j, b, i: (start, 0))
        if isinstance(self.add, _Product):
            extra = self.add.inputs(ts)
        else:
            extra = [] if self.add is None else [(self.add, pl.BlockSpec((None, ts, self.shape[2]), lambda j, b, i: (b, i, 0)))]
        return [(self.a, a_spec), (self.b, b_spec)] + extra

    def value(self, refs):
        dims = (((1,), (1 if self.tb else 0,)), ((), ()))
        val = lax.dot_general(refs[0][...].astype(MXU_DTYPE), refs[1][...].astype(MXU_DTYPE), dims, preferred_element_type=F32)
        if isinstance(self.add, _Product):
            return val + self.add.value(refs[2:])
        return val if self.add is None else val + refs[2][...].astype(F32)


def _inputs(groups, kinds, ts, wb):
    loaded = [(a, _spec(kind, a, o, ts, wb)) for g, kind in zip(groups, kinds) for a, o in g if not isinstance(a, _Product)]
    made = [pair for g in groups for a, _ in g if isinstance(a, _Product) for pair in a.inputs(ts)]
    return [a for a, _ in loaded + made], [sp for _, sp in loaded + made]


def _values(refs, groups):
    n_loaded = sum(1 for g in groups for a, _ in g if not isinstance(a, _Product))
    loaded, pos, out = iter(refs[:n_loaded]), n_loaded, []
    for g in groups:
        vals = []
        for a, _ in g:
            if isinstance(a, _Product):
                k = len(a.inputs(1))
                vals.append(a.value(refs[pos:pos + k]))
                pos += k
            else:
                vals.append(next(loaded)[...].astype(F32))
        out.append(vals)
    return out, pos


def _tok_fwd(fn, toks, bats, pars, outs, *, name, ts, wb=None, cols=1):
    groups = [_with_off(toks), _with_off(bats), _with_off(pars)]
    bl, s, _ = groups[0][0][0].shape
    ts = min(ts, s)
    args, in_specs = _inputs(groups, ("tok", "bat", "par"), ts, wb)

    def body(*refs):
        vals, n_in = _values(refs, groups)
        res = fn(*[v for g in vals for v in g])
        for r, val in zip(refs[n_in:], res):
            r[...] = val.astype(r.dtype)

    out_specs = [pl.BlockSpec((None, ts, w if wb is None else wb), lambda j, b, i: (b, i, j)) for w, _ in outs]
    return pl.pallas_call(
        body, name=name, grid=(cols, bl, s // ts), in_specs=in_specs,
        out_specs=out_specs, out_shape=[jax.ShapeDtypeStruct((bl, s, w), dt) for w, dt in outs],
        compiler_params=_cparams("parallel", "parallel", "parallel"),
    )(*args)


def _accumulate(ref, val, first):
    @pl.when(first)
    def _():
        ref[...] = val

    @pl.when(jnp.logical_not(first))
    def _():
        ref[...] += val


def _tok_bwd(fn, toks, bats, pars, cots, need, *, name, ts, wb=None, cols=1, tok_dtype=F32, loss=False, after=()):
    toks, bats, pars, cots = _with_off(toks), _with_off(bats), _with_off(pars), _with_off(cots)
    groups = [toks, bats, pars, cots]
    bl, s, _ = toks[0][0].shape
    ts = min(ts, s)
    nt, nb, npar = len(toks), len(bats), len(pars)
    args, in_specs = _inputs(groups, ("tok", "bat", "par", "tok"), ts, wb)
    args, in_specs = args + list(after), in_specs + [pl.BlockSpec(memory_space=pl.ANY)] * len(after)

    def body(*refs):
        j, b, i = pl.program_id(0), pl.program_id(1), pl.program_id(2)
        (tok_vals, bat_vals, par_vals, cot_vals), o = _values(refs, groups)
        o += len(after)
        outs, vjp = jax.vjp(fn, *tok_vals, *bat_vals, *par_vals)
        if loss:
            ct = (jnp.ones_like(outs[0]),)
            tot = jnp.broadcast_to(jnp.sum(outs[0], keepdims=True), (1, LANE))
            _accumulate(refs[o], tot, jnp.logical_and(b == 0, i == 0))
            o += 1
        else:
            ct = tuple(cot_vals)
        grads = vjp(ct)
        for t in range(nt):
            if need[t]:
                refs[o][...] = grads[t].astype(refs[o].dtype)
                o += 1
        for t in range(nb):
            _accumulate(refs[o], grads[nt + t], i == 0)
            o += 1
        for t in range(npar):
            first = jnp.logical_and(b == 0, i == 0)
            if pars[t][1] is None:
                first = jnp.logical_and(first, j == 0)
            _accumulate(refs[o], grads[nt + nb + t], first)
            o += 1

    full = lambda arr: arr.shape[-1] if wb is None else wb * cols
    blk = lambda arr: arr.shape[-1] if wb is None else wb
    out_specs, out_shape = [], []
    if loss:
        out_specs.append(pl.BlockSpec((1, LANE), lambda j, b, i: (0, 0)))
        out_shape.append(jax.ShapeDtypeStruct((1, LANE), F32))
    for t in range(nt):
        if need[t]:
            out_specs.append(pl.BlockSpec((None, ts, blk(toks[t][0])), lambda j, b, i: (b, i, j)))
            dt = tok_dtype[t] if isinstance(tok_dtype, (list, tuple)) else tok_dtype
            out_shape.append(jax.ShapeDtypeStruct((bl, s, full(toks[t][0])), dt))
    for arr, _ in bats:
        out_specs.append(pl.BlockSpec((None, 1, blk(arr)), lambda j, b, i: (b, 0, j)))
        out_shape.append(jax.ShapeDtypeStruct((bl, 1, full(arr)), F32))
    for arr, off in pars:
        if off is None:
            out_specs.append(pl.BlockSpec(arr.shape, lambda j, b, i: (0, 0)))
            out_shape.append(jax.ShapeDtypeStruct(arr.shape, F32))
        else:
            out_specs.append(pl.BlockSpec((arr.shape[0], blk(arr)), lambda j, b, i: (0, j)))
            out_shape.append(jax.ShapeDtypeStruct((arr.shape[0], full(arr)), F32))
    res = list(pl.pallas_call(
        body, name=name, grid=(cols, bl, s // ts), in_specs=in_specs,
        out_specs=out_specs, out_shape=out_shape, compiler_params=_cparams("arbitrary", "arbitrary", "arbitrary"),
    )(*args))
    tot = res.pop(0) if loss else None
    dtoks = [res.pop(0) if need[t] else None for t in range(nt)]
    dbats = [res.pop(0) for _ in range(nb)]
    dpars = [res.pop(0) for _ in range(npar)]
    return (tot, dtoks, dbats, dpars) if loss else (dtoks, dbats, dpars)


def _silu(x):
    return x * jax.nn.sigmoid(x)


def _rms(x, w):
    return x * lax.rsqrt(jnp.mean(x * x, axis=-1, keepdims=True) + EPS) * w


def _f_norm_mod(x, shift, scale, w):
    return (_rms(x, w) * (1.0 + scale) + shift,)


def _f_norm_mod_skip(x, shift, scale, w):
    return _rms(x, w) * (1.0 + scale) + shift, x


def _f_res_norm_mod(x, mix, gate, shift, scale, w):
    x2 = x + gate * mix
    return x2, _rms(x2, w) * (1.0 + scale) + shift


def _f_res_norm_mod_keep(x, mix, gate, shift, scale, w):
    return (*_f_res_norm_mod(x, mix, gate, shift, scale, w), mix)


def _f_gates(p, a_log, dt_bias, *, heads):
    z = p + dt_bias
    g = -jnp.exp(a_log) * (jnp.maximum(z, 0.0) + jnp.log1p(jnp.exp(jnp.minimum(z, -z))))
    lane = lax.broadcasted_iota(jnp.int32, p.shape, 1)
    return (jnp.where(lane < heads, g, jax.nn.sigmoid(p)),)


def _f_gdn_out(o, z, w):
    return (_rms(o, w) * _silu(z),)


def _f_merge(ga, gb, ya, yb):
    return (jax.nn.sigmoid(ga) * ya + jax.nn.sigmoid(gb) * yb,)


def _f_merge_keep(ga, gb, ya, yb):
    return (*_f_merge(ga, gb, ya, yb), yb)


def _f_loss(x2, ff, tgt, gate, shift, scale, w):
    y = _rms(x2 + gate * ff, w) * (1.0 + scale) + shift
    return (0.5 * jnp.mean(jnp.square(y - tgt), axis=-1, keepdims=True),)


def _shift_down(x, s):
    if s == 0:
        return x
    row = lax.broadcasted_iota(jnp.int32, x.shape, 0)
    return jnp.where(row >= s, pltpu.roll(x, s, 0), 0.0)


def _shift_up(x, s):
    if s == 0:
        return x
    n = x.shape[0]
    row = lax.broadcasted_iota(jnp.int32, x.shape, 0)
    return jnp.where(row < n - s, pltpu.roll(x, n - s, 0), 0.0)


def _conv(x, w):
    width = w.shape[0]
    acc = w[width - 1:width, :] * x
    for j in range(width - 1):
        acc = acc + w[j:j + 1, :] * _shift_down(x, width - 1 - j)
    return acc


def _conv_bwd(dy, x, w, dw_ref, first):
    width = w.shape[0]
    dx = w[width - 1:width, :] * dy
    for j in range(width - 1):
        dx = dx + w[j:j + 1, :] * _shift_up(dy, width - 1 - j)
    for j in range(width):
        row = jnp.sum(dy * _shift_down(x, width - 1 - j), axis=0, keepdims=True)
        _accumulate(dw_ref.at[j:j + 1, :], row, first)
    return dx


def _qkv_act(xc, is_v, scale):
    a = _silu(xc)
    nrm = a * lax.rsqrt(jnp.sum(a * a, axis=-1, keepdims=True) + EPS) * scale
    return jnp.where(is_v, a, nrm)


def _qkv_act_bwd(xc, dout, is_v, scale):
    sig = jax.nn.sigmoid(xc)
    a = xc * sig
    r = lax.rsqrt(jnp.sum(a * a, axis=-1, keepdims=True) + EPS)
    c1 = r * scale
    da = c1 * dout - a * (c1 * r * r * jnp.sum(dout * a, axis=-1, keepdims=True))
    return jnp.where(is_v, dout, da) * (sig * (1.0 + xc * (1.0 - sig)))


def _qkv_consts(j, heads):
    is_v = j >= 2 * heads
    scale = jnp.where(j < heads, HEAD ** -0.5, 1.0).astype(F32)
    return is_v, scale


def _qkv_fwd(p, w, heads, name):
    bl, s, w3 = p.shape

    def body(p_ref, w_ref, o_ref):
        is_v, scale = _qkv_consts(pl.program_id(0), heads)
        o_ref[...] = _qkv_act(_conv(p_ref[...], w_ref[...]), is_v, scale)

    blk = pl.BlockSpec((None, s, HEAD), lambda j, b: (b, 0, j))
    return pl.pallas_call(
        body, name=name, grid=(w3 // HEAD, bl), in_specs=[blk, pl.BlockSpec((w.shape[0], HEAD), lambda j, b: (0, j))],
        out_specs=blk, out_shape=jax.ShapeDtypeStruct(p.shape, F32), compiler_params=_cparams("parallel", "parallel"),
    )(p, w)


def _qkv_bwd(p, w, dout, heads, name):
    bl, s, w3 = p.shape

    def body(p_ref, w_ref, d_ref, dp_ref, dw_ref):
        is_v, scale = _qkv_consts(pl.program_id(0), heads)
        x, wv = p_ref[...], w_ref[...]
        dxc = _qkv_act_bwd(_conv(x, wv), d_ref[...], is_v, scale)
        dp_ref[...] = _conv_bwd(dxc, x, wv, dw_ref, pl.program_id(1) == 0).astype(dp_ref.dtype)

    blk = pl.BlockSpec((None, s, HEAD), lambda j, b: (b, 0, j))
    wblk = pl.BlockSpec((w.shape[0], HEAD), lambda j, b: (0, j))
    return pl.pallas_call(
        body, name=name, grid=(w3 // HEAD, bl), in_specs=[blk, wblk, blk], out_specs=[blk, wblk],
        out_shape=[jax.ShapeDtypeStruct(p.shape, MXU_DTYPE), jax.ShapeDtypeStruct(w.shape, F32)],
        compiler_params=_cparams("arbitrary", "arbitrary"),
    )(p, w, dout)


def _sc_specs(p, w):
    bl, s, w3 = p.shape
    nblk = w3 // 3 // LANE
    sec = lambda k: pl.BlockSpec((None, s, LANE), functools.partial(lambda j, b, k: (b, 0, k * nblk + j), k=k))
    return nblk, [sec(0), sec(1), sec(2)], pl.BlockSpec((w.shape[0], LANE), lambda j, b: (0, j)), \
        pl.BlockSpec((None, s, LANE), lambda j, b: (b, 0, j))


def _sc_fwd(p, w, name):
    bl, s, w3 = p.shape
    nblk, secs, wblk, oblk = _sc_specs(p, w)

    def body(b_ref, c_ref, x_ref, w_ref, o_ref):
        o_ref[...] = (b_ref[...] * _conv(c_ref[...] * x_ref[...], w_ref[...])).astype(o_ref.dtype)

    return pl.pallas_call(
        body, name=name, grid=(nblk, bl), in_specs=secs + [wblk], out_specs=oblk,
        out_shape=jax.ShapeDtypeStruct((bl, s, w3 // 3), MXU_DTYPE), compiler_params=_cparams("parallel", "parallel"),
    )(p, p, p, w)


def _sc_bwd(p, w, dout, name):
    bl, s, w3 = p.shape
    nblk, secs, wblk, oblk = _sc_specs(p, w)

    def body(b_ref, c_ref, x_ref, w_ref, d_ref, dp_ref, dw_ref):
        gb, gc, xin, wv, d = b_ref[...], c_ref[...], x_ref[...], w_ref[...], d_ref[...]
        u = gc * xin
        dp_ref[0] = (d * _conv(u, wv)).astype(dp_ref.dtype)
        du = _conv_bwd(d * gb, u, wv, dw_ref, pl.program_id(1) == 0)
        dp_ref[1] = (du * xin).astype(dp_ref.dtype)
        dp_ref[2] = (du * gc).astype(dp_ref.dtype)

    return pl.pallas_call(
        body, name=name, grid=(nblk, bl), in_specs=secs + [wblk, oblk],
        out_specs=[pl.BlockSpec((3, None, s, LANE), lambda j, b: (0, b, 0, j)), wblk],
        out_shape=[jax.ShapeDtypeStruct((3, bl, s, w3 // 3), MXU_DTYPE), jax.ShapeDtypeStruct(w.shape, F32)],
        compiler_params=_cparams("arbitrary", "arbitrary"),
    )(p, p, p, w, dout)


def _bdot(a, b, ca, cb):
    return lax.dot_general(a.astype(MXU_DTYPE), b.astype(MXU_DTYPE), (((ca,), (cb,)), ((), ())),
                           preferred_element_type=F32)


def _hdot(a, b):
    return lax.dot_general(a, b, (((1,), (0,)), ((), ())), precision=HIGHEST, preferred_element_type=F32)


def _lane_col(x, idx):
    lane = lax.broadcasted_iota(jnp.int32, x.shape, 1)
    return jnp.sum(jnp.where(lane == idx, x, 0.0), axis=1, keepdims=True)


def _chunk_masks():
    r = lax.broadcasted_iota(jnp.int32, (CHUNK, CHUNK), 0)
    c = lax.broadcasted_iota(jnp.int32, (CHUNK, CHUNK), 1)
    return r == c, r >= c, r > c


def _dot3(a, b):
    ah, bh = a.astype(MXU_DTYPE), b.astype(MXU_DTYPE)
    al, bl = (a - ah.astype(F32)).astype(MXU_DTYPE), (b - bh.astype(F32)).astype(MXU_DTYPE)
    dot = lambda x, y: lax.dot_general(x, y, (((1,), (0,)), ((), ())), preferred_element_type=F32)
    return dot(ah, bh) + (dot(ah, bl) + dot(al, bh))


def _tri_inv_steps(low, eye):
    x = -low
    p = jnp.where(eye, 1.0, 0.0) + x
    span = 2
    while span < CHUNK:
        x = _dot3(x, x)
        yield
        p = p + _dot3(p, x)
        yield
        span *= 2
    return p


def _round_robin(gens):
    out, live = [None] * len(gens), list(range(len(gens)))
    while live:
        still = []
        for i in live:
            try:
                next(gens[i])
                still.append(i)
            except StopIteration as stop:
                out[i] = stop.value
        live = still
    return out


def _gdn_pre(q, k, v, gc, beta, masks):
    eye, causal, strict = masks
    gc_row = jnp.sum(jnp.where(eye, gc, 0.0), axis=0, keepdims=True)
    decay = jnp.where(causal, jnp.exp(jnp.where(causal, gc - gc_row, 0.0)), 0.0)
    eg = jnp.exp(gc)
    gl = gc[CHUNK - 1:CHUNK, :]
    kb, vb = k * beta, v * beta
    both = _bdot(jnp.concatenate([kb, q], axis=0), k, 1, 1)
    low = jnp.where(strict, both[:CHUNK] * decay, 0.0)
    qk = jnp.where(causal, both[CHUNK:] * decay, 0.0)
    rest = jnp.exp(gl - gc)
    return dict(decay=decay, eg=eg, gl=gl, kb=kb, vb=vb, kbe=kb * eg, low=low, qk=qk, qg=q * eg, rest=rest, kdec=k * rest)


GROUP = 4


def _gdn_specs(qkv, gbeta, heads, rev):
    bl, s, w3 = qkv.shape
    d, n = w3 // 3, s // CHUNK
    group = GROUP if n % GROUP == 0 else 1
    steps = n // group
    at = (lambda c: steps - 1 - c) if rev else (lambda c: c)
    assert d == heads * HEAD
    rows = group * CHUNK
    sec = pl.BlockSpec((None, rows, w3), lambda b, c: (b, at(c), 0))
    gspec = pl.BlockSpec((None, rows, LANE), lambda b, c: (b, at(c), 0))
    ospec = pl.BlockSpec((None, rows, d), lambda b, c: (b, at(c), 0))
    sspec = pl.BlockSpec((None, group, heads, HEAD, HEAD), lambda b, c: (b, at(c), 0, 0, 0))
    tspec = pl.BlockSpec((None, group, heads, CHUNK, CHUNK), lambda b, c: (b, at(c), 0, 0, 0))
    return bl, s, d, n, group, sec, gspec, ospec, sspec, tspec


def _gdn_fwd(qkv, gbeta, heads, name):
    bl, s, d, n, group, sec, gspec, ospec, sspec, tspec = _gdn_specs(qkv, gbeta, heads, False)
    rows = lambda sub: slice(sub * CHUNK, (sub + 1) * CHUNK)
    pairs = [(h, sub) for h in range(heads) for sub in range(group)]

    def body(x_ref, g_ref, o_ref, s_ref, t_ref, st_ref):
        @pl.when(pl.program_id(1) == 0)
        def _():
            st_ref[...] = jnp.zeros_like(st_ref)

        masks = _chunk_masks()
        eye, causal, _ = masks
        gblks = [g_ref[rows(sub), :] for sub in range(group)]
        gcs = [_hdot(jnp.where(causal, 1.0, 0.0), gb) for gb in gblks]
        st_all = st_ref[...]

        def free(h, sub):
            q, k, v = (x_ref[rows(sub), sec * d + h * HEAD:sec * d + (h + 1) * HEAD] for sec in range(3))
            pre = _gdn_pre(q, k, v, _lane_col(gcs[sub], h), _lane_col(gblks[sub], heads + h), masks)
            yield
            t = yield from _tri_inv_steps(pre["low"], eye)
            uw = _bdot(t, jnp.concatenate([pre["vb"], pre["kbe"]], axis=1), 1, 0)
            return pre, t, uw[:, :HEAD], uw[:, HEAD:]

        pieces = dict(zip(pairs, _round_robin([free(h, sub) for h, sub in pairs])))

        def carry(h):
            st, outs, starts = st_all[h], [], []
            for sub in range(group):
                pre, _, u, w = pieces[h, sub]
                starts.append(st)
                vnew = u - _bdot(w, st, 1, 0)
                yield
                outs.append(_bdot(pre["qg"], st, 1, 0) + _bdot(pre["qk"], vnew, 1, 0))
                st = st * jnp.exp(pre["gl"]) + _bdot(pre["kdec"], vnew, 0, 0)
                yield
            return outs, starts, st

        carried = _round_robin([carry(h) for h in range(heads)])
        per_sub = lambda pick: [[pick(h, sub) for h in range(heads)] for sub in range(group)]
        o_ref[...] = jnp.concatenate([jnp.concatenate(r, axis=1) for r in per_sub(lambda h, sub: carried[h][0][sub])], axis=0)
        s_ref[...] = jnp.stack([jnp.stack(r) for r in per_sub(lambda h, sub: carried[h][1][sub])])
        t_ref[...] = jnp.stack([jnp.stack(r) for r in per_sub(lambda h, sub: pieces[h, sub][1])])
        st_ref[...] = jnp.stack([carried[h][2] for h in range(heads)])

    return pl.pallas_call(
        body, name=name, grid=(bl, n // group), in_specs=[sec, gspec], out_specs=[ospec, sspec, tspec],
        out_shape=[jax.ShapeDtypeStruct((bl, s, d), F32), jax.ShapeDtypeStruct((bl, n, heads, HEAD, HEAD), F32),
                   jax.ShapeDtypeStruct((bl, n, heads, CHUNK, CHUNK), F32)],
        scratch_shapes=[pltpu.VMEM((heads, HEAD, HEAD), F32)], compiler_params=_cparams("parallel", "arbitrary"),
    )(qkv, gbeta)


def _gdn_bwd(qkv, gbeta, dout, s_all, t_all, heads, name):
    bl, s, d, n, group, sec, gspec, ospec, sspec, tspec = _gdn_specs(qkv, gbeta, heads, True)
    rows = lambda sub: slice(sub * CHUNK, (sub + 1) * CHUNK)
    pairs = [(h, sub) for h in range(heads) for sub in range(group)]
    stack, side = functools.partial(jnp.concatenate, axis=0), functools.partial(jnp.concatenate, axis=1)

    def body(x_ref, g_ref, do_ref, s_ref, t_ref, dx_ref, dg_ref, ds_ref):
        @pl.when(pl.program_id(1) == 0)
        def _():
            ds_ref[...] = jnp.zeros_like(ds_ref)

        masks = _chunk_masks()
        eye, causal, strict = masks
        gblks = [g_ref[rows(sub), :] for sub in range(group)]
        gcs = [_hdot(jnp.where(causal, 1.0, 0.0), gb) for gb in gblks]
        lane = lax.broadcasted_iota(jnp.int32, (CHUNK, LANE), 1)
        last_row = lax.broadcasted_iota(jnp.int32, (CHUNK, 1), 0) == CHUNK - 1
        rowsum = lambda a: jnp.sum(a, axis=1, keepdims=True)
        st_all, t_all_, ds_all = s_ref[...], t_ref[...], ds_ref[...]

        def free(h, sub):
            q, k, v = (x_ref[rows(sub), sec * d + h * HEAD:sec * d + (h + 1) * HEAD] for sec in range(3))
            do = do_ref[rows(sub), h * HEAD:(h + 1) * HEAD]
            beta = _lane_col(gblks[sub], heads + h)
            st, t = st_all[sub, h], t_all_[sub, h]
            pre = _gdn_pre(q, k, v, _lane_col(gcs[sub], h), beta, masks)
            yield
            uw = _bdot(t, side([pre["vb"], pre["kbe"]]), 1, 0)
            u, w = uw[:, :HEAD], uw[:, HEAD:]
            yield
            vnew = u - _bdot(w, st, 1, 0)
            yield
            dqk = jnp.where(causal, _bdot(do, vnew, 1, 1), 0.0)
            dqg = _bdot(do, st, 1, 1)
            return dict(q=q, k=k, v=v, do=do, beta=beta, st=st, t=t, pre=pre, w=w, vnew=vnew, dqk=dqk, dqg=dqg)

        pieces = dict(zip(pairs, _round_robin([free(h, sub) for h, sub in pairs])))

        def carry(h):
            dsn, outs = ds_all[h], {}
            for sub in reversed(range(group)):
                pc = pieces[h, sub]
                pre, st, do = pc["pre"], pc["st"], pc["do"]
                egl = jnp.exp(pre["gl"])
                dkdec = _bdot(pc["vnew"], dsn, 1, 1)
                dvnew = _bdot(pre["kdec"], dsn, 1, 0) + _bdot(pre["qk"], do, 0, 0)
                dgl = jnp.sum(dsn * st, keepdims=True) * egl
                yield
                dw = -_bdot(dvnew, st, 1, 1)
                dsn = dsn * egl + _bdot(stack([pre["qg"], -pc["w"]]), stack([do, dvnew]), 0, 0)
                outs[sub] = (dkdec, dvnew, dgl, dw)
                yield
            return outs, dsn

        carried = _round_robin([carry(h) for h in range(heads)])

        def rest(h, sub):
            pc = pieces[h, sub]
            dkdec, dvnew, dgl, dw = carried[h][0][sub]
            q, k, v, beta, t, pre, dqk, dqg = (pc[x] for x in ("q", "k", "v", "beta", "t", "pre", "dqk", "dqg"))
            decay, eg, kb, vb, kbe, low, qk, qg, kdec = (pre[x] for x in ("decay", "eg", "kb", "vb", "kbe", "low", "qk", "qg", "kdec"))
            dt = _bdot(side([dvnew, dw]), side([vb, kbe]), 1, 1)
            by_t = _bdot(t, side([dvnew, dw]), 0, 0)
            dvb, dkbe = by_t[:, :HEAD], by_t[:, HEAD:]
            yield
            inner = _bdot(dt, t, 1, 1)
            yield
            dlow = -jnp.where(strict, _bdot(t, inner, 0, 0), 0.0)
            da, db = dlow * decay, dqk * decay
            yield
            m = dlow * low + dqk * qk
            kdk = dkdec * kdec
            col_of_m = jnp.sum(jnp.where(eye, jnp.sum(m, axis=0, keepdims=True), 0.0), axis=1, keepdims=True)
            dgc = rowsum(m) - col_of_m + rowsum(dqg * qg) + rowsum(dkbe * kbe) - rowsum(kdk)
            dgc = dgc + jnp.where(last_row, dgl + jnp.sum(kdk, keepdims=True), 0.0)
            by_k = _bdot(stack([da, db]), k, 1, 0)
            dkb = by_k[:CHUNK] + dkbe * eg
            yield
            dk = _bdot(stack([da, db]), stack([kb, q]), 0, 0) + dkdec * pre["rest"] + dkb * beta
            dq = by_k[CHUNK:] + dqg * eg
            dbeta = rowsum(dkb * k) + rowsum(dvb * v)
            return dq, dk, dvb * beta, jnp.where(lane == h, dgc, 0.0) + jnp.where(lane == heads + h, dbeta, 0.0)

        done = dict(zip(pairs, _round_robin([rest(h, sub) for h, sub in pairs])))
        dx_ref[...] = stack([side([done[h, sub][i] for i in range(3) for h in range(heads)]) for sub in range(group)])
        ds_ref[...] = jnp.stack([carried[h][1] for h in range(heads)])
        upper = jnp.where(jnp.logical_or(eye, jnp.logical_not(causal)), 1.0, 0.0)
        dgs = []
        for sub in range(group):
            dgb = done[0, sub][3]
            for h in range(1, heads):
                dgb = dgb + done[h, sub][3]
            dgs.append(jnp.where(lane < heads, _hdot(upper, dgb), dgb))
        dg_ref[...] = stack(dgs)

    return pl.pallas_call(
        body, name=name, grid=(bl, n // group), in_specs=[sec, gspec, ospec, sspec, tspec], out_specs=[sec, gspec],
        out_shape=[jax.ShapeDtypeStruct(qkv.shape, F32), jax.ShapeDtypeStruct((bl, s, LANE), F32)],
        scratch_shapes=[pltpu.VMEM((heads, HEAD, HEAD), F32)], compiler_params=_cparams("parallel", "arbitrary"),
    )(qkv, gbeta, dout, s_all, t_all)


def _position():
    return lax.axis_index("x"), lax.axis_index("y"), lax.axis_index("c")


def _all_gather(x, *, name):
    space = pltpu.VMEM

    def body(x_ref, out_ref, send_sems, recv_sems, local_sem):
        ax, ay, ac = _position()
        me, sibling = (ax, ay, ac), (ax, ay, 1 - ac)
        chips = [(1 - ax, ay), (ax, 1 - ay), (1 - ax, 1 - ay)]

        def slot(px, py, pc):
            return out_ref.at[4 * px + 2 * py + pc]

        def copy(k, block, to, src=None):
            return pltpu.make_async_remote_copy(
                src_ref=slot(*block) if src is None else src, dst_ref=slot(*block), send_sem=send_sems.at[k],
                recv_sem=recv_sems.at[k], device_id=to, device_id_type=MESH_IDS)

        mine = pltpu.make_async_copy(x_ref, slot(*me), local_sem)
        mine.start()
        first = [copy(0, me, sibling, src=x_ref)] + [copy(1 + j, me, (*chip, ac), src=x_ref) for j, chip in enumerate(chips)]
        for cp in first:
            cp.start()
        passed = [copy(4 + j, (*chip, ac), sibling) for j, chip in enumerate(chips)]
        for j, chip in enumerate(chips):
            copy(1 + j, (*chip, ac), me).wait_recv()
            passed[j].start()
        copy(0, sibling, me).wait_recv()
        for j, chip in enumerate(chips):
            copy(4 + j, (*chip, 1 - ac), me).wait_recv()
        for cp in first + passed:
            cp.wait_send()
        mine.wait()

    return pl.pallas_call(
        body, name=name, out_shape=jax.ShapeDtypeStruct((NDEV,) + x.shape, x.dtype),
        in_specs=[pl.BlockSpec(memory_space=space)], out_specs=pl.BlockSpec(memory_space=space),
        scratch_shapes=[pltpu.SemaphoreType.DMA((7,)), pltpu.SemaphoreType.DMA((7,)), pltpu.SemaphoreType.DMA],
    )(x)


class _Rider:
    def __init__(self, arrays, out_shapes, sems, hooks):
        self.arrays, self.out_shapes, self.sems, self.hooks = arrays, out_shapes, sems, hooks


def _gather_rider(xs):
    n = len(xs)

    def hooks(x_refs, out_refs, send_sems, recv_sems):
        ax, ay, ac = _position()
        me, sibling = (ax, ay, ac), (ax, ay, 1 - ac)
        chips = [(1 - ax, ay), (ax, 1 - ay), (1 - ax, 1 - ay)]

        def copies(k, block, to, own=False):
            out = []
            for i in range(n):
                slot = out_refs[i].at[4 * block[0] + 2 * block[1] + block[2]]
                out.append(pltpu.make_async_remote_copy(
                    src_ref=x_refs[i] if own else slot, dst_ref=slot, send_sem=send_sems.at[k, i], recv_sem=recv_sems.at[k, i],
                    device_id=to, device_id_type=MESH_IDS))
            return out

        def first():
            for cp in copies(0, me, sibling, own=True):
                cp.start()
            for j, chip in enumerate(chips):
                for cp in copies(1 + j, me, (*chip, ac), own=True):
                    cp.start()

        def mid():
            for j, chip in enumerate(chips):
                for arrived, onward in zip(copies(1 + j, (*chip, ac), me), copies(4 + j, (*chip, ac), sibling)):
                    arrived.wait_recv()
                    onward.start()

        def last():
            for cp in copies(0, sibling, me):
                cp.wait_recv()
            for j, chip in enumerate(chips):
                for cp in copies(4 + j, (*chip, 1 - ac), me):
                    cp.wait_recv()
            for cp in copies(0, me, sibling, own=True):
                cp.wait_send()
            for j, chip in enumerate(chips):
                for cp in copies(1 + j, me, (*chip, ac), own=True) + copies(4 + j, (*chip, ac), sibling):
                    cp.wait_send()

        return first, mid, last

    return _Rider(list(xs), [jax.ShapeDtypeStruct((NDEV,) + x.shape, x.dtype) for x in xs],
                  [pltpu.SemaphoreType.DMA((7, n)), pltpu.SemaphoreType.DMA((7, n))], hooks)


def _scatter_rider(parts):
    packed = sum(r for _, r in parts)
    width, dtype = parts[0][0].shape[1], parts[0][0].dtype

    def hooks(g_refs, out_refs, send_sems, recv_sems):
        (recv_ref,) = out_refs
        ax, ay, ac = _position()

        def peer(rel):
            flip = lambda a, bit: 1 - a if rel & bit else a
            return flip(ax, 4), flip(ay, 2), flip(ac, 1)

        def first():
            for rel in range(1, NDEV):
                px, py, pc = peer(rel)
                off = 0
                for g_ref, (_, r) in zip(g_refs, parts):
                    rows = g_ref.at[pl.ds(pl.multiple_of((4 * px + 2 * py + pc) * r, ROW_ALIGN), r)]
                    pltpu.make_async_remote_copy(
                        src_ref=rows, dst_ref=recv_ref.at[rel - 1, pl.ds(off, r)], send_sem=send_sems.at[rel - 1],
                        recv_sem=recv_sems.at[rel - 1], device_id=(px, py, pc), device_id_type=MESH_IDS).start()
                    off += r

        def last():
            for rel in range(1, NDEV):
                slot = recv_ref.at[rel - 1]
                pltpu.make_async_remote_copy(src_ref=slot, dst_ref=slot, send_sem=send_sems.at[rel - 1],
                                             recv_sem=recv_sems.at[rel - 1], device_id=peer(rel), device_id_type=MESH_IDS).wait()

        return first, lambda: None, last

    return _Rider([g for g, _ in parts], [jax.ShapeDtypeStruct((NDEV - 1, packed, width), dtype)],
                  [pltpu.SemaphoreType.DMA((NDEV - 1,)), pltpu.SemaphoreType.DMA((NDEV - 1,))], hooks)


def _sum_direct(own, recv, name):
    r, w = own.shape
    tr = max(t for t in range(ROW_ALIGN, 257, ROW_ALIGN) if r % t == 0)

    def body(own_ref, *refs):
        acc = own_ref[...].astype(F32)
        for ref in refs[:-1]:
            acc = acc + ref[...].astype(F32)
        refs[-1][...] = acc

    rblk = lambda k: pl.BlockSpec((None, tr, w), functools.partial(lambda i, k: (k, i, 0), k=k))
    blk = pl.BlockSpec((tr, w), lambda i: (i, 0))
    return pl.pallas_call(body, name=name, grid=(r // tr,), in_specs=[blk] + [rblk(k) for k in range(NDEV - 1)],
                          out_specs=blk, out_shape=jax.ShapeDtypeStruct((r, w), F32),
                          compiler_params=_cparams("parallel"))(own, *([recv] * (NDEV - 1)))


ROW_ALIGN = 16


def _window_start(rows_per_dev, k):
    return rows_per_dev * k // ROW_ALIGN * ROW_ALIGN


def _exchange_in_chip(parts, name, collective_id):
    packed = sum(win for _, _, win, _ in parts)
    width, dtype = parts[0][0].shape[1], parts[0][0].dtype

    def body(g_refs, out_refs, send_sems, recv_sems):
        (recv_ref,) = out_refs
        ax, ay, ac = _position()
        sibling = (ax, ay, 1 - ac)
        _handshake([sibling])
        for q in range(4):
            for g_ref, (_, r, win, off) in zip(g_refs, parts):
                there = g_ref.at[pl.ds(pl.multiple_of(_window_start(r, 2 * q + 1 - ac), ROW_ALIGN), win)]
                pltpu.make_async_remote_copy(src_ref=there, dst_ref=recv_ref.at[q, pl.ds(off, win)], send_sem=send_sems.at[q],
                                             recv_sem=recv_sems.at[q], device_id=sibling, device_id_type=MESH_IDS).start()
        for q in range(4):
            pltpu.make_async_remote_copy(src_ref=recv_ref.at[q], dst_ref=recv_ref.at[q], send_sem=send_sems.at[q],
                                         recv_sem=recv_sems.at[q], device_id=sibling, device_id_type=MESH_IDS).wait()

    return _on_sequencer(body, [g for g, _, _, _ in parts], [jax.ShapeDtypeStruct((4, packed, width), dtype)],
                         [pltpu.SemaphoreType.DMA((4,)), pltpu.SemaphoreType.DMA((4,))], name=name, collective_id=collective_id)[0]


def _on_sequencer(body, ins, out_shapes, sems, *, name, collective_id):
    hbm = pltpu.MemorySpace.HBM
    in_refs = [jax.new_ref(a, memory_space=hbm) for a in ins]
    out_refs = [jax.empty_ref(s, memory_space=hbm) for s in out_shapes]

    @pl.kernel(mesh=plsc.ScalarSubcoreMesh(axis_name="sequencer", num_cores=1), name=name, scratch_types=tuple(sems),
               compiler_params=pltpu.CompilerParams(collective_id=collective_id))
    def launch(*sem_refs):
        body(in_refs, out_refs, *sem_refs)

    launch()
    return [r[...] for r in out_refs]


def _handshake(peers):
    barrier = pltpu.get_barrier_semaphore()
    for peer in peers:
        pl.semaphore_signal(barrier, inc=1, device_id=peer, device_id_type=MESH_IDS)
    pl.semaphore_wait(barrier, len(peers))


def _exchange_chips_async(s1, name, collective_id):
    def body(in_refs, out_refs, send_sems, recv_sems):
        (src,), (got,) = in_refs, out_refs
        ax, ay, ac = _position()
        chips = [(1 - ax, ay), (ax, 1 - ay), (1 - ax, 1 - ay)]
        _handshake([(cx, cy, ac) for cx, cy in chips])
        copies = [pltpu.make_async_remote_copy(
            src_ref=src.at[2 * cx + cy], dst_ref=got.at[r], send_sem=send_sems.at[r], recv_sem=recv_sems.at[r],
            device_id=(cx, cy, ac), device_id_type=MESH_IDS) for r, (cx, cy) in enumerate(chips)]
        for cp in copies:
            cp.start()
        for cp in copies:
            cp.wait_recv()
        for cp in copies:
            cp.wait_send()

    return _on_sequencer(body, [s1], [jax.ShapeDtypeStruct((3,) + s1.shape[1:], s1.dtype)],
                         [pltpu.SemaphoreType.DMA((3,)), pltpu.SemaphoreType.DMA((3,))], name=name, collective_id=collective_id)[0]


def _gather_async(xs, name, collective_id):
    rider = _gather_rider(xs)

    def body(in_refs, out_refs, send_sems, recv_sems):
        ax, ay, ac = _position()
        _handshake([(ax, ay, 1 - ac), (1 - ax, ay, ac), (ax, 1 - ay, ac), (1 - ax, 1 - ay, ac)])
        for hook in rider.hooks(in_refs, out_refs, send_sems, recv_sems):
            hook()

    return _on_sequencer(body, rider.arrays, rider.out_shapes, rider.sems, name=name, collective_id=collective_id)


def _gather_balanced(x, name, collective_id):
    m = x.shape[0]
    half = m // 2 // ROW_ALIGN * ROW_ALIGN
    parts = {"all": pl.ds(0, m), "lo": pl.ds(0, half), "hi": pl.ds(half, m - half)}

    def body(in_refs, out_refs, send_sems, recv_sems):
        (x_ref,), (out_ref,) = in_refs, out_refs
        ax, ay, ac = _position()
        me, sibling = (ax, ay, ac), (ax, ay, 1 - ac)
        by_x, by_y, diag = (1 - ax, ay), (ax, 1 - ay), (1 - ax, 1 - ay)
        _handshake([sibling, (*by_x, ac), (*by_y, ac)])

        def copy(k, block, part, to, own=False):
            rows = out_ref.at[4 * block[0] + 2 * block[1] + block[2], parts[part]]
            return pltpu.make_async_remote_copy(src_ref=x_ref if own else rows, dst_ref=rows, send_sem=send_sems.at[k],
                                                recv_sem=recv_sems.at[k], device_id=to, device_id_type=MESH_IDS)

        def own_half(k, part, to):
            rows = out_ref.at[4 * ax + 2 * ay + ac, parts[part]]
            return pltpu.make_async_remote_copy(src_ref=x_ref.at[parts[part]], dst_ref=rows, send_sem=send_sems.at[k],
                                                recv_sem=recv_sems.at[k], device_id=to, device_id_type=MESH_IDS)

        nx, ny, nd = (*by_x, ac), (*by_y, ac), (*diag, ac)
        sends = [copy(0, me, "all", sibling, own=True), own_half(1, "lo", nx), own_half(2, "hi", nx),
                 own_half(3, "hi", ny), own_half(4, "lo", ny), copy(5, nx, "lo", ny), copy(6, ny, "hi", nx),
                 copy(7, nx, "lo", sibling), copy(8, nx, "hi", sibling), copy(9, ny, "hi", sibling),
                 copy(10, ny, "lo", sibling), copy(11, nd, "lo", sibling), copy(12, nd, "hi", sibling)]
        sx, sy, sd = (*by_x, 1 - ac), (*by_y, 1 - ac), (*diag, 1 - ac)
        arrivals = [copy(0, sibling, "all", me), copy(1, nx, "lo", me), copy(2, nx, "hi", me), copy(3, ny, "hi", me),
                    copy(4, ny, "lo", me), copy(5, nd, "lo", me), copy(6, nd, "hi", me), copy(7, sx, "lo", me),
                    copy(8, sx, "hi", me), copy(9, sy, "hi", me), copy(10, sy, "lo", me), copy(11, sd, "lo", me),
                    copy(12, sd, "hi", me)]
        for k in range(5):
            sends[k].start()
        for arrived, onward in ((1, (5, 7)), (3, (6, 9)), (2, (8,)), (4, (10,)), (5, (11,)), (6, (12,))):
            arrivals[arrived].wait_recv()
            for k in onward:
                sends[k].start()
        for k in (0, 7, 8, 9, 10, 11, 12):
            arrivals[k].wait_recv()
        for cp in sends:
            cp.wait_send()

    return _on_sequencer(body, [x], [jax.ShapeDtypeStruct((NDEV,) + x.shape, x.dtype)],
                         [pltpu.SemaphoreType.DMA((13,)), pltpu.SemaphoreType.DMA((13,))], name=name, collective_id=collective_id)[0]


def _scatter_async(parts, name, collective_id):
    rider = _scatter_rider(parts)

    def body(in_refs, out_refs, send_sems, recv_sems):
        ax, ay, ac = _position()
        flip = lambda a, on: 1 - a if on else a
        _handshake([(flip(ax, rel & 4), flip(ay, rel & 2), flip(ac, rel & 1)) for rel in range(1, NDEV)])
        for hook in rider.hooks(in_refs, out_refs, send_sems, recv_sems):
            hook()

    return _on_sequencer(body, rider.arrays, rider.out_shapes, rider.sems, name=name, collective_id=collective_id)[0]


def _sum_in_chip(own, recv, name):
    _, r, w = own.shape
    tr = _tile(r, (256, 128))

    def body(a_ref, b_ref, o_ref):
        o_ref[...] = (a_ref[...].astype(F32) + b_ref[...].astype(F32)).astype(o_ref.dtype)

    blk = pl.BlockSpec((None, tr, w), lambda q, i: (q, i, 0))
    return pl.pallas_call(body, name=name, grid=(4, r // tr), in_specs=[blk, blk], out_specs=blk,
                          out_shape=jax.ShapeDtypeStruct(own.shape, own.dtype),
                          compiler_params=_cparams("parallel", "parallel"))(own, recv)


def _sum_chips(s1, recv, chip, name):
    _, r, w = s1.shape
    tr = _tile(r, (256, 128))

    def body(c_ref, s_ref, r0_ref, r1_ref, r2_ref, o_ref):
        f = lambda ref: ref[...].astype(F32)
        o_ref[...] = ((f(s_ref) + f(r0_ref)) + f(r1_ref)) + f(r2_ref)

    rblk = lambda k: pl.BlockSpec((None, tr, w), functools.partial(lambda i, c, k: (k, i, 0), k=k))
    grid_spec = pltpu.PrefetchScalarGridSpec(
        num_scalar_prefetch=1, grid=(r // tr,),
        in_specs=[pl.BlockSpec((None, tr, w), lambda i, c: (c[0], i, 0)), rblk(0), rblk(1), rblk(2)],
        out_specs=pl.BlockSpec((tr, w), lambda i, c: (i, 0)))
    return pl.pallas_call(body, name=name, grid_spec=grid_spec, out_shape=jax.ShapeDtypeStruct((r, w), F32),
                          compiler_params=_cparams("parallel"))(chip, s1, recv, recv, recv)


def _silu_rows(x, name):
    def body(x_ref, o_ref):
        o_ref[...] = _silu(x_ref[...])

    return pl.pallas_call(body, name=name, out_shape=jax.ShapeDtypeStruct(x.shape, F32))(x)


def _row_sum(x, name):
    def body(x_ref, o_ref):
        acc = x_ref[0:1, :]
        for i in range(1, x.shape[0]):
            acc = acc + x_ref[i:i + 1, :]
        o_ref[...] = acc

    return pl.pallas_call(body, name=name, out_shape=jax.ShapeDtypeStruct((1, x.shape[1]), F32))(x)


def _adamw(w, g, m, v, name):
    cols = w.shape[-1]
    rows = w.size // cols
    tr = _tile(rows, (256, 128))
    tc = LANE if (tr == rows and rows > 512 and cols % LANE == 0) else cols

    def body(w_ref, g_ref, m_ref, v_ref, d_ref, mo_ref, vo_ref):
        grad = g_ref[...]
        m_new = ADAM_B1 * m_ref[...] + (1.0 - ADAM_B1) * grad
        v_new = ADAM_B2 * v_ref[...] + (1.0 - ADAM_B2) * jnp.square(grad)
        m_hat = m_new / (1.0 - ADAM_B1 ** ADAM_STEP)
        v_hat = v_new / (1.0 - ADAM_B2 ** ADAM_STEP)
        d_ref[...] = -ADAM_LR * (m_hat / (jnp.sqrt(v_hat) + ADAM_EPS) + ADAM_WD * w_ref[...])
        mo_ref[...] = m_new
        vo_ref[...] = v_new

    blk = pl.BlockSpec((tr, tc), lambda i, j: (i, j))
    out = pl.pallas_call(
        body, name=name, grid=(rows // tr, cols // tc), in_specs=[blk] * 4, out_specs=[blk] * 3,
        out_shape=[jax.ShapeDtypeStruct((rows, cols), F32)] * 3, compiler_params=_cparams("parallel", "parallel"),
    )(*[t.reshape(rows, cols) for t in (w, g, m, v)])
    return [t.reshape(w.shape) for t in out]


def _pack(parts, width, row_mult, dtype):
    flat = jnp.concatenate([p.reshape(-1).astype(dtype) for p in parts])
    rows = -(-flat.shape[0] // (width * row_mult)) * row_mult
    return jnp.pad(flat, (0, rows * width - flat.shape[0])).reshape(rows, width)


def _unpack(flat, shapes):
    out, off = [], 0
    for shp in shapes:
        size = 1
        for dim in shp:
            size *= dim
        out.append(flat[:, off:off + size].reshape((flat.shape[0],) + tuple(shp)))
        off += size
    return out


def _devices_to_cols(a):
    _, r, c = a.shape
    return a.transpose(1, 0, 2).reshape(r, NDEV * c)


def kernel(x, c, w_ada, b_ada, norm1_w, w_in, gdn_conv_w, gdn_a_log, gdn_dt_bias, gdn_norm_w, w_gdn_proj, sc_conv_w, w_sc_out, w_o, norm2_w, w_ffn_in, w_ffn_out, w_ada_f, b_ada_f, normf_w, loss_target, m_w_ada, m_b_ada, m_norm1_w, m_w_in, m_gdn_conv_w, m_gdn_a_log, m_gdn_dt_bias, m_gdn_norm_w, m_w_gdn_proj, m_sc_conv_w, m_w_sc_out, m_w_o, m_norm2_w, m_w_ffn_in, m_w_ffn_out, m_w_ada_f, m_b_ada_f, m_normf_w, v_w_ada, v_b_ada, v_norm1_w, v_w_in, v_gdn_conv_w, v_gdn_a_log, v_gdn_dt_bias, v_gdn_norm_w, v_w_gdn_proj, v_sc_conv_w, v_w_sc_out, v_w_o, v_norm2_w, v_w_ffn_in, v_w_ffn_out, v_w_ada_f, v_b_ada_f, v_normf_w):
    bl, s, d = x.shape
    heads = gdn_a_log.shape[-1]
    dff = w_ffn_out.shape[1] * NDEV
    tok = bl * s
    ax, ay, ac = _position()
    dev = 4 * ax + 2 * ay + ac
    as_tok = lambda a: a.reshape(bl, s, a.shape[-1])
    as_mat = lambda a: a.reshape(tok, a.shape[-1])

    small = _all_gather(_pack([c, gdn_conv_w, sc_conv_w], LANE, 8, F32), name="gather_cond")
    c_all, conv_w, sc_w = _unpack(small.reshape(NDEV, -1), [(bl, d), gdn_conv_w.shape[1:], sc_conv_w.shape[1:]])
    c_act = _silu_rows(c_all.reshape(NDEV * bl, d), "cond_silu")
    conv_w, sc_w = _devices_to_cols(conv_w), _devices_to_cols(sc_w)
    n_ada, n_adaf = w_ada.shape[-1], w_ada_f.shape[-1]
    bias = jnp.broadcast_to(lax.dynamic_slice_in_dim(b_ada, dev * n_ada, n_ada, axis=1), (NDEV * bl, n_ada))
    biasf = jnp.broadcast_to(lax.dynamic_slice_in_dim(b_ada_f.reshape(1, -1), dev * n_adaf, n_adaf, axis=1), (NDEV * bl, n_adaf))
    mod_cols = _mm(c_act, w_ada[0], add=bias, name="ada_cols")
    modf_cols = _mm(c_act, w_ada_f, add=biasf, name="adaf_cols")
    mods = _all_gather(jnp.concatenate([mod_cols, modf_cols], axis=1), name="gather_mod")
    mod_all = mods[:, :, :n_ada].transpose(1, 0, 2).reshape(NDEV * bl, NDEV * n_ada)
    modf_all = mods[:, :, n_ada:].transpose(1, 0, 2).reshape(NDEV * bl, NDEV * n_adaf)
    my_rows = lambda a: lax.dynamic_slice_in_dim(a, dev * bl, bl, axis=0)
    sh1, sc1, g1, sh2, sc2, g2 = [t.reshape(bl, 1, d) for t in jnp.split(my_rows(mod_all), 6, axis=1)]
    shf, scf = [t.reshape(bl, 1, d) for t in jnp.split(my_rows(modf_all), 2, axis=1)]

    late = [t.astype(MXU_DTYPE) for t in (w_gdn_proj[0], w_sc_out[0], w_o[0], w_ffn_in[0].T, w_ffn_out[0])]
    rows = [t.shape[0] for t in late] + [w_in.shape[-1]]
    offs = [sum(rows[:i]) for i in range(5)]
    in_send = w_in[0].T.astype(MXU_DTYPE)
    with_own = lambda g, own: lax.dynamic_update_slice_in_dim(g, own[None], dev, axis=0)
    wt_in = with_own(_gather_balanced(in_send, "gather_w_in", 1), in_send).reshape(NDEV * rows[5], d)
    gathered = _gather_async(late[:3], "gather_mixer", 2) + _gather_async(late[3:], "gather_ffn", 3)
    wgp, wso, wo, wt_fi, wfo = [with_own(g, own).reshape(NDEV * own.shape[0], d) for g, own in zip(gathered, late)]
    o_z, o_ab, o_sc, o_ga, o_gb = 3 * d, 4 * d, 4 * d + 2 * heads, 7 * d + 2 * heads, 8 * d + 2 * heads
    s_qkv, s_z, s_sc, s_gate = (0, o_z), (o_z, d), (o_sc, 3 * d), (o_ga, 2 * d)
    wt_ab = jnp.pad(wt_in[o_ab:o_sc], ((0, LANE - 2 * heads), (0, 0)))

    n1w, n2w, nfw = norm1_w.reshape(1, d), norm2_w.reshape(1, d), normf_w.reshape(1, d)
    lanes = lambda a: jnp.pad(a.reshape(1, -1), ((0, 0), (0, LANE - a.size)))
    a_log, dt_bias, gnw = lanes(gdn_a_log), lanes(gdn_dt_bias), gdn_norm_w.reshape(1, HEAD)
    f_gates = functools.partial(_f_gates, heads=heads)
    (h1,) = _tok_fwd(_f_norm_mod, [x], [sh1, sc1], [n1w], [(d, MXU_DTYPE)], name="norm1", ts=512)
    h1m = as_mat(h1)
    p_qkv = as_tok(_mm_ring(h1m, wt_in, s_qkv, name="in_qkv"))
    p_z = as_tok(_mm_ring(h1m, wt_in, s_z, name="in_z"))
    p_ab = as_tok(_mm(h1m, wt_ab, tb=True, name="in_ab"))
    p_sc = as_tok(_mm_ring(h1m, wt_in, s_sc, name="in_sc"))
    p_g = as_tok(_mm_ring(h1m, wt_in, s_gate, name="in_gate"))
    qkv = _qkv_fwd(p_qkv, conv_w, heads, "qkv_conv")
    (gbeta,) = _tok_fwd(f_gates, [p_ab], [], [a_log, dt_bias], [(LANE, F32)], name="gates", ts=512)
    o, s_all, t_all = _gdn_fwd(qkv, gbeta, heads, "gdn")
    (og,) = _tok_fwd(_f_gdn_out, [o, p_z], [], [(gnw, None)], [(d, MXU_DTYPE)], name="gdn_out", ts=2048, wb=HEAD, cols=heads)
    y_a = as_tok(_mm(as_mat(og), wgp, name="gdn_proj"))
    scp = _sc_fwd(p_sc, sc_w, "sc_conv")
    mrg, y_b = _tok_fwd(_f_merge_keep, [(p_g, 0), (p_g, 1), y_a, _Product(scp, wso)], [], [], [(d, MXU_DTYPE), (d, F32)],
                        name="merge", ts=512, wb=d)
    merge_toks = [(p_g, 0), (p_g, 1), y_a, y_b]
    x2, h2, mix = _tok_fwd(_f_res_norm_mod_keep, [x, _Product(mrg, wo)], [g1, sh2, sc2], [n2w],
                           [(d, F32), (d, MXU_DTYPE), (d, F32)], name="norm2", ts=512)
    act, gu_a, gu_b = _ffn_in_swiglu(as_mat(h2), wt_fi, dff, "ffn_in")

    loss_l, (dx2, dff_out, _), (dg2, dshf, dscf), (dnfw,) = _tok_bwd(
        _f_loss, [x2, _Product(as_tok(act), wfo), loss_target], [g2, shf, scf], [nfw], [], [True, True, False], name="loss",
        ts=512, loss=True, tok_dtype=[F32, MXU_DTYPE, None])
    dffm = as_mat(dff_out)
    dgu_a, dgu_b = _ffn_out_bwd_swiglu(dffm, wfo, gu_a, gu_b, "d_ffn_out")
    gmm = functools.partial(_mm, ta=True, out_dtype=MXU_DTYPE)
    gw_ffn_out = gmm(act, dffm, name="g_ffn_out")
    dh2 = _Product(as_tok(dgu_b), wt_fi, b_rows=(dff, dff), add=_Product(as_tok(dgu_a), wt_fi, b_rows=(0, dff)))
    h2m = as_mat(h2)
    gwt_ffn_in = gmm(dgu_a, h2m, out_rows=2 * dff, name="g_ffn_in_a")
    gwt_ffn_in = gmm(dgu_b, h2m, out_rows=2 * dff, row_off=dff, into=gwt_ffn_in, name="g_ffn_in_b")
    ffn_parts = [(gwt_ffn_in, rows[3]), (gw_ffn_out, rows[4])]
    ffn_recv = _scatter_async(ffn_parts, "scatter_ffn", 4)
    (dx_skip, dmix), (dg1, dsh2, dsc2), (dn2w,) = _tok_bwd(
        _f_res_norm_mod, [x, mix], [g1, sh2, sc2], [n2w], [dx2, dh2], [True, True], name="d_norm2", ts=512,
        tok_dtype=[F32, MXU_DTYPE], after=[gwt_ffn_in, gw_ffn_out])
    gw_o = gmm(as_mat(mrg), as_mat(dmix), name="g_mix_out")
    (dga, dgb, dya, dyb), _, _ = _tok_bwd(_f_merge, merge_toks, [], [], [_Product(dmix, wo, tb=True)], [True] * 4,
                                          name="d_merge", ts=512, wb=d, tok_dtype=MXU_DTYPE)
    dyam, dybm = as_mat(dya), as_mat(dyb)
    dog = as_tok(_mm(dyam, wgp, tb=True, name="d_gdn_proj"))
    gw_gdn_proj = gmm(as_mat(og), dyam, name="g_gdn_proj")
    dscp = as_tok(_mm(dybm, wso, tb=True, name="d_sc_out"))
    gw_sc_out = gmm(as_mat(scp), dybm, name="g_sc_out")
    dsc, g_sc_w = _sc_bwd(p_sc, sc_w, dscp, "d_sc_conv")
    mix_parts = [(gw_gdn_proj, rows[0]), (gw_sc_out, rows[1]), (gw_o, rows[2])]
    mix_recv = _scatter_async(mix_parts, "scatter_mixer", 5)
    (do, dz), _, (g_gnw,) = _tok_bwd(_f_gdn_out, [o, p_z], [], [(gnw, None)], [dog], [True, True], name="d_gdn_out",
                                     ts=2048, wb=HEAD, cols=heads, tok_dtype=[F32, MXU_DTYPE],
                                     after=[gw_gdn_proj, gw_sc_out, gw_o])
    own_rows = lambda parts: jnp.concatenate([lax.dynamic_slice_in_dim(g, dev * r, r, axis=0) for g, r in parts], axis=0)
    dqkv, dgbeta = _gdn_bwd(qkv, gbeta, do, s_all, t_all, heads, "d_gdn")
    dp_qkv, g_conv_w = _qkv_bwd(p_qkv, conv_w, dqkv, heads, "d_qkv_conv")
    ffn_red = _sum_direct(own_rows(ffn_parts), ffn_recv, "sum_ffn")
    mix_red = _sum_direct(own_rows(mix_parts), mix_recv, "sum_mix")
    (dp_ab,), _, (g_a_log, g_dt_bias) = _tok_bwd(f_gates, [p_ab], [], [a_log, dt_bias], [dgbeta], [True], name="d_gates",
                                                 ts=512, tok_dtype=MXU_DTYPE, after=[ffn_red, mix_red])
    r_in = rows[5]
    win = -(-(r_in + max(r_in * k % ROW_ALIGN for k in range(NDEV))) // 128) * 128
    need_rows = max(_window_start(r_in, k) for k in range(NDEV)) + win
    dsc_m = dsc.reshape(3, tok, d)
    gwt_in = ([gmm(as_mat(dp_qkv), h1m, name="g_in_qkv"), gmm(as_mat(dz), h1m, name="g_in_z"),
               gmm(as_mat(dp_ab), h1m, name="g_in_ab")[:2 * heads]]
              + [gmm(dsc_m, h1m, a_index=k, name=f"g_in_sc{k}") for k in range(3)]
              + [gmm(as_mat(dga), h1m, name="g_in_ga"), gmm(as_mat(dgb), h1m, name="g_in_gb")])
    gwt_in = jnp.concatenate(gwt_in + [jnp.zeros((need_rows - NDEV * r_in, d), MXU_DTYPE)], axis=0)
    assert d <= 1024
    wide = [as_mat(dp_qkv), as_mat(dz), dsc_m, as_mat(dga)]
    row_of = lambda t: d * t + jnp.where(t * d >= o_ab, 2 * heads, 0)
    recv1 = _exchange_in_chip([(gwt_in, r_in, win, 0)], "scatter_in_chip", 7)
    own = jnp.stack([lax.dynamic_slice_in_dim(gwt_in, _window_start(r_in, 2 * q + ac), win, axis=0) for q in range(4)])
    s1 = _sum_in_chip(own, recv1, "sum_in_chip")
    recv2 = _exchange_chips_async(s1, "scatter_chips", 6)

    dh1 = _mm(as_mat(dp_ab), wt_ab, name="d_in_ab")
    dh1 = _mm_chain(wide, wt_in, row_of, add=dh1, name="d_in", tk=d)
    dh1 = _Product(dgb, wt_in, b_rows=(o_gb, d), add=as_tok(dh1))
    (grad_x,), (dsh1, dsc1), (dn1w,) = _tok_bwd(_f_norm_mod_skip, [x], [sh1, sc1], [n1w], [dh1, dx_skip], [True],
                                                name="d_norm1", ts=512)
    reduced = _sum_chips(s1, recv2, (2 * ax + ay).reshape(1).astype(jnp.int32), "sum_chips")
    gt_w_in = lax.dynamic_slice_in_dim(reduced, r_in * dev - _window_start(r_in, dev), r_in, axis=0)
    g_w_in = gt_w_in.T.reshape(w_in.shape)
    gt_w_ffn_in = ffn_red[:rows[3]]
    g_w_ffn_in = gt_w_ffn_in.T.reshape(w_ffn_in.shape)
    g_w_ffn_out = ffn_red[rows[3]:].reshape(w_ffn_out.shape)
    g_w_gdn_proj, g_w_sc_out, g_w_o = (mix_red[offs[i]:offs[i] + rows[i]].reshape(ref.shape)
                                       for i, ref in enumerate((w_gdn_proj, w_sc_out, w_o)))

    dmod = jnp.concatenate([t.reshape(bl, d) for t in (dsh1, dsc1, dg1, dsh2, dsc2, dg2)], axis=1)
    dmodf = jnp.concatenate([t.reshape(bl, d) for t in (dshf, dscf)], axis=1)
    summed_parts = [dn1w, dn2w, dnfw, g_gnw, g_a_log, g_dt_bias, g_conv_w, g_sc_w, loss_l]
    partial = _all_gather(_pack([dmod, dmodf] + summed_parts, LANE, 8, F32), name="gather_small")
    partial = partial.reshape(NDEV, -1)
    n_rows = bl * (6 * d + 2 * d)
    dmod_all, dmodf_all = _unpack(partial[:, :n_rows], [(bl, 6 * d), (bl, 2 * d)])
    dmod_all, dmodf_all = dmod_all.reshape(NDEV * bl, 6 * d), dmodf_all.reshape(NDEV * bl, 2 * d)
    totals = _row_sum(partial[:, n_rows:], "sum_small")
    t_n1w, t_n2w, t_nfw, t_gnw, t_a_log, t_dt_bias, t_conv_w, t_sc_w, t_loss = [
        t[0] for t in _unpack(totals, [p.shape for p in summed_parts])]
    my_cols = lambda a, n: lax.dynamic_slice_in_dim(a, dev * n, n, axis=1)
    grads = {
        "w_ada": _mm(c_act, my_cols(dmod_all, n_ada), ta=True, name="g_ada").reshape(w_ada.shape),
        "b_ada": _row_sum(dmod_all, "g_ada_bias").reshape(b_ada.shape),
        "norm1_w": t_n1w.reshape(norm1_w.shape),
        "w_in": g_w_in,
        "gdn_conv_w": my_cols(t_conv_w, gdn_conv_w.shape[-1]).reshape(gdn_conv_w.shape),
        "gdn_a_log": t_a_log[:, :heads].reshape(gdn_a_log.shape),
        "gdn_dt_bias": t_dt_bias[:, :heads].reshape(gdn_dt_bias.shape),
        "gdn_norm_w": t_gnw.reshape(gdn_norm_w.shape),
        "w_gdn_proj": g_w_gdn_proj,
        "sc_conv_w": my_cols(t_sc_w, sc_conv_w.shape[-1]).reshape(sc_conv_w.shape),
        "w_sc_out": g_w_sc_out,
        "w_o": g_w_o,
        "norm2_w": t_n2w.reshape(norm2_w.shape),
        "w_ffn_in": g_w_ffn_in,
        "w_ffn_out": g_w_ffn_out,
        "w_ada_f": _mm(c_act, my_cols(dmodf_all, n_adaf), ta=True, name="g_adaf").reshape(w_ada_f.shape),
        "b_ada_f": _row_sum(dmodf_all, "g_adaf_bias").reshape(b_ada_f.shape),
        "normf_w": t_nfw.reshape(normf_w.shape),
    }
    weights = dict(w_ada=w_ada, b_ada=b_ada, norm1_w=norm1_w, w_in=w_in, gdn_conv_w=gdn_conv_w, gdn_a_log=gdn_a_log,
                   gdn_dt_bias=gdn_dt_bias, gdn_norm_w=gdn_norm_w, w_gdn_proj=w_gdn_proj, sc_conv_w=sc_conv_w,
                   w_sc_out=w_sc_out, w_o=w_o, norm2_w=norm2_w, w_ffn_in=w_ffn_in, w_ffn_out=w_ffn_out, w_ada_f=w_ada_f,
                   b_ada_f=b_ada_f, normf_w=normf_w)
    m_in = [m_w_ada, m_b_ada, m_norm1_w, m_w_in, m_gdn_conv_w, m_gdn_a_log, m_gdn_dt_bias, m_gdn_norm_w, m_w_gdn_proj,
            m_sc_conv_w, m_w_sc_out, m_w_o, m_norm2_w, m_w_ffn_in, m_w_ffn_out, m_w_ada_f, m_b_ada_f, m_normf_w]
    v_in = [v_w_ada, v_b_ada, v_norm1_w, v_w_in, v_gdn_conv_w, v_gdn_a_log, v_gdn_dt_bias, v_gdn_norm_w, v_w_gdn_proj,
            v_sc_conv_w, v_w_sc_out, v_w_o, v_norm2_w, v_w_ffn_in, v_w_ffn_out, v_w_ada_f, v_b_ada_f, v_normf_w]
    deltas, new_m, new_v = [], [], []
    grads_t = {"w_in": gt_w_in, "w_ffn_in": gt_w_ffn_in}
    for (wname, wt), mt, vt in zip(weights.items(), m_in, v_in):
        if wname in grads_t:
            back = lambda a, wt=wt: a.T.reshape(wt.shape)
            dl, mn, vn = (back(a) for a in _adamw(wt[0].T, grads_t[wname], mt[0].T, vt[0].T, "adamw_" + wname))
        else:
            dl, mn, vn = _adamw(wt, grads[wname], mt, vt, "adamw_" + wname)
        deltas.append(dl)
        new_m.append(mn)
        new_v.append(vn)
    loss = t_loss[0, 0]
    return (loss, grad_x, *[grads[k] for k in weights], *deltas, *new_m, *new_v)
```

```python
import functools

import jax
import jax.numpy as jnp
from jax import lax
from jax.experimental import pallas as pl
from jax.experimental.pallas import tpu as pltpu
from jax.experimental.pallas import tpu_sc as plsc

F32 = jnp.float32
MXU_DTYPE = jnp.bfloat16
NDEV = 8
CHUNK = 64
HEAD = 128
LANE = 128
EPS = 1e-6
ADAM_LR, ADAM_B1, ADAM_B2, ADAM_EPS, ADAM_WD, ADAM_STEP = 0.001, 0.9, 0.999, 1e-08, 0.01, 10
VMEM_LIMIT = 48 * 1024 * 1024
MESH_IDS = pl.DeviceIdType.MESH
HIGHEST = lax.Precision.HIGHEST


def _tile(n, cands=(512, 256, 128)):
    for c in cands:
        if n % c == 0:
            return c
    return n


def _cparams(*sem):
    return pltpu.CompilerParams(dimension_semantics=sem, vmem_limit_bytes=VMEM_LIMIT)


def _mm(a, b, *, ta=False, tb=False, add=None, out_dtype=F32, name, b_rows=None, out_rows=None, row_off=0, into=None,
        a_index=None):
    m, k = (a.shape[-1], a.shape[-2]) if ta else a.shape[-2:]
    b_shape = b.shape if b_rows is None else (b_rows[1], b.shape[1])
    n = b_shape[0] if tb else b_shape[1]
    assert k == (b_shape[1] if tb else b_shape[0])
    if ta:
        tm, tn = _tile(m), n if n <= 1024 else _tile(n)
        tk = k if k <= 4096 else _tile(k, (4096, 2048, 1024, 512))
        if tm * tk > 1024 * 2048:
            tk = _tile(k, (2048, 1024, 512))
    else:
        tk = k if k <= 1024 else _tile(k, (1024, 512))
        tn = _tile(n, (1024 if tk <= 1024 else 512, 512, 256, 128))
        tm = _tile(m, (2048 if (tn <= 512 and tk <= 1024) else 1024, 1024, 512, 256, 128))
    nk = k // tk
    dims = (((0 if ta else 1,), (1 if tb else 0,)), ((), ()))
    has_add = add is not None

    def body(*refs):
        a_ref, b_ref = refs[0], refs[1]
        add_ref = refs[2] if has_add else None
        o_ref = refs[2 + has_add + (into is not None)]
        part = lax.dot_general(a_ref[...].astype(MXU_DTYPE), b_ref[...].astype(MXU_DTYPE), dims,
                               preferred_element_type=F32)

        def finish(acc):
            if has_add:
                acc = acc + add_ref[...]
            o_ref[...] = acc.astype(o_ref.dtype)

        if nk == 1:
            finish(part)
        else:
            acc_ref = refs[-1]
            kk = pl.program_id(2)

            @pl.when(kk == 0)
            def _():
                acc_ref[...] = part

            @pl.when(kk > 0)
            def _():
                acc_ref[...] += part

            @pl.when(kk == nk - 1)
            def _():
                finish(acc_ref[...])

    a_blk, a_at = ((tk, tm), lambda i, j, kk: (kk, i)) if ta else ((tm, tk), lambda i, j, kk: (i, kk))
    a_spec = (pl.BlockSpec(a_blk, a_at) if a_index is None else
              pl.BlockSpec((None,) + a_blk, lambda i, j, kk: (a_index,) + a_at(i, j, kk)))
    if b_rows is None:
        b_spec = pl.BlockSpec((tn, tk), lambda i, j, kk: (j, kk)) if tb else pl.BlockSpec((tk, tn), lambda i, j, kk: (kk, j))
    else:
        at = lambda t: pl.multiple_of(b_rows[0] + t, ROW_ALIGN)
        b_spec = (pl.BlockSpec((pl.Element(tn), pl.Element(tk)), lambda i, j, kk: (at(j * tn), kk * tk)) if tb else
                  pl.BlockSpec((pl.Element(tk), pl.Element(tn)), lambda i, j, kk: (at(kk * tk), j * tn)))
    add_spec = pl.BlockSpec((tm, tn), lambda i, j, kk: (i, j))
    assert row_off % tm == 0
    o_spec = pl.BlockSpec((tm, tn), lambda i, j, kk: (i + row_off // tm, j))
    in_specs = [a_spec, b_spec] + ([add_spec] if has_add else []) + ([pl.BlockSpec(memory_space=pl.ANY)] if into is not None else [])
    args = [a, b] + ([add] if has_add else []) + ([into] if into is not None else [])
    return pl.pallas_call(
        body, name=name, grid=(m // tm, n // tn, nk), in_specs=in_specs, out_specs=o_spec,
        out_shape=jax.ShapeDtypeStruct((out_rows or m, n), out_dtype),
        scratch_shapes=[pltpu.VMEM((tm, tn), F32)] if nk > 1 else [],
        input_output_aliases={len(args) - 1: 0} if into is not None else {},
        compiler_params=_cparams("parallel", "parallel", "arbitrary"),
    )(*args)


RING = 3


def _mm_ring(a, b, b_rows, *, name):
    (m, k), (start, n) = a.shape, b_rows
    assert b.shape[1] == k and b.dtype == MXU_DTYPE
    tm, tn = _tile(m, (1024, 512, 256, 128)), _tile(n, (1024, 512, 256, 128))
    nj = n // tn
    steps = (m // tm) * nj

    def body(a_ref, b_hbm, o_ref, ring, sems):
        step = pl.program_id(0) * nj + pl.program_id(1)

        def tile_copy(t):
            rows = pl.ds(pl.multiple_of(start + (t % nj) * tn, ROW_ALIGN), tn)
            return pltpu.make_async_copy(b_hbm.at[rows, :], ring.at[t % RING], sems.at[t % RING])

        @pl.when(step == 0)
        def _():
            for t in range(min(RING - 1, steps)):
                tile_copy(jnp.int32(t)).start()

        @pl.when(step + RING - 1 < steps)
        def _():
            tile_copy(step + RING - 1).start()

        tile_copy(step).wait()
        o_ref[...] = lax.dot_general(a_ref[...].astype(MXU_DTYPE), ring[step % RING], (((1,), (1,)), ((), ())),
                                     preferred_element_type=F32)

    return pl.pallas_call(
        body, name=name, grid=(m // tm, nj),
        in_specs=[pl.BlockSpec((tm, k), lambda i, j: (i, 0)), pl.BlockSpec(memory_space=pl.ANY)],
        out_specs=pl.BlockSpec((tm, tn), lambda i, j: (i, j)), out_shape=jax.ShapeDtypeStruct((m, n), F32),
        scratch_shapes=[pltpu.VMEM((RING, tn, k), MXU_DTYPE), pltpu.SemaphoreType.DMA((RING,))],
        compiler_params=_cparams("arbitrary", "arbitrary"),
    )(a, b)


def _mm_chain(parts, b, row_of_tile, *, add, name, tk=1024, tm=1024):
    m, n = parts[0].shape[-2], b.shape[1]
    tm = min(tm, m)
    tiles = [p.shape[0] if p.ndim == 3 else p.shape[1] // tk for p in parts]
    first = [sum(tiles[:s]) for s in range(len(parts))]
    nk = sum(tiles)

    def body(*refs):
        a_refs, b_ref, add_ref, o_ref, acc_ref = refs[:len(parts)], *refs[len(parts):]
        kk = pl.program_id(1)

        @pl.when(kk == 0)
        def _():
            acc_ref[...] = add_ref[...]

        for a_ref, lo, cnt in zip(a_refs, first, tiles):
            @pl.when(jnp.logical_and(kk >= lo, kk < lo + cnt))
            def _(a_ref=a_ref):
                acc_ref[...] += lax.dot_general(a_ref[...].astype(MXU_DTYPE), b_ref[...].astype(MXU_DTYPE),
                                                (((1,), (0,)), ((), ())), preferred_element_type=F32)

        @pl.when(kk == nk - 1)
        def _():
            o_ref[...] = acc_ref[...]

    tile_of = lambda kk, lo, cnt: jnp.clip(kk - lo, 0, cnt - 1)
    a_specs = [pl.BlockSpec((None, tm, tk), functools.partial(lambda i, kk, lo, cnt: (tile_of(kk, lo, cnt), i, 0), lo=lo, cnt=cnt))
               if p.ndim == 3 else
               pl.BlockSpec((tm, tk), functools.partial(lambda i, kk, lo, cnt: (i, tile_of(kk, lo, cnt)), lo=lo, cnt=cnt))
               for p, lo, cnt in zip(parts, first, tiles)]
    b_spec = pl.BlockSpec((pl.Element(tk), pl.Element(n)), lambda i, kk: (pl.multiple_of(row_of_tile(kk), ROW_ALIGN), 0))
    o_spec = pl.BlockSpec((tm, n), lambda i, kk: (i, 0))
    return pl.pallas_call(
        body, name=name, grid=(m // tm, nk), in_specs=a_specs + [b_spec, o_spec], out_specs=o_spec,
        out_shape=jax.ShapeDtypeStruct((m, n), F32), scratch_shapes=[pltpu.VMEM((tm, n), F32)],
        compiler_params=_cparams("parallel", "arbitrary"),
    )(*parts, b, add)


def _swiglu_tiles(m, half):
    tn = _tile(half, (512, 256, 128))
    return _tile(m, (2048 if tn <= 256 else 1024, 1024, 512, 256, 128)), tn


def _ffn_in_swiglu(h, wt, half, name):
    m, k = h.shape
    tm, tn = _swiglu_tiles(m, half)
    nj = half // tn
    dims = (((1,), (1,)), ((), ()))

    assert wt.dtype == MXU_DTYPE and half % ROW_ALIGN == 0
    steps = (m // tm) * nj

    def body(h_ref, w_hbm, act_ref, a_ref, b_ref, ring, sems):
        step = pl.program_id(0) * nj + pl.program_id(1)

        def tile_copies(t):
            rows = [pl.ds(pl.multiple_of(g * half + (t % nj) * tn, ROW_ALIGN), tn) for g in range(2)]
            return [pltpu.make_async_copy(w_hbm.at[rows[g], :], ring.at[t % RING, g], sems.at[t % RING, g]) for g in range(2)]

        @pl.when(step == 0)
        def _():
            for t in range(min(RING - 1, steps)):
                for copy in tile_copies(jnp.int32(t)):
                    copy.start()

        @pl.when(step + RING - 1 < steps)
        def _():
            for copy in tile_copies(step + RING - 1):
                copy.start()

        for copy in tile_copies(step):
            copy.wait()
        lhs = h_ref[...].astype(MXU_DTYPE)
        a = lax.dot_general(lhs, ring[step % RING, 0], dims, preferred_element_type=F32)
        b = lax.dot_general(lhs, ring[step % RING, 1], dims, preferred_element_type=F32)
        act_ref[...] = (_silu(a) * b).astype(act_ref.dtype)
        a_ref[...] = a.astype(a_ref.dtype)
        b_ref[...] = b.astype(b_ref.dtype)

    out = jax.ShapeDtypeStruct((m, half), MXU_DTYPE)
    oblk = pl.BlockSpec((tm, tn), lambda i, j: (i, j))
    return pl.pallas_call(
        body, name=name, grid=(m // tm, nj),
        in_specs=[pl.BlockSpec((tm, k), lambda i, j: (i, 0)), pl.BlockSpec(memory_space=pl.ANY)],
        out_specs=[oblk, oblk, oblk], out_shape=[out, out, out],
        scratch_shapes=[pltpu.VMEM((RING, 2, tn, k), MXU_DTYPE), pltpu.SemaphoreType.DMA((RING, 2))],
        compiler_params=_cparams("arbitrary", "arbitrary"),
    )(h, wt)


def _ffn_out_bwd_swiglu(dff, w, a, b, name):
    m, k = dff.shape
    half = w.shape[0]
    tm, tn = _swiglu_tiles(m, half)

    def body(d_ref, w_ref, a_ref, b_ref, da_ref, db_ref):
        dact = lax.dot_general(d_ref[...].astype(MXU_DTYPE), w_ref[...].astype(MXU_DTYPE), (((1,), (1,)), ((), ())),
                               preferred_element_type=F32)
        av, bv = a_ref[...].astype(F32), b_ref[...].astype(F32)
        sig = jax.nn.sigmoid(av)
        da_ref[...] = (dact * bv * (sig * (1.0 + av * (1.0 - sig)))).astype(da_ref.dtype)
        db_ref[...] = (dact * (av * sig)).astype(db_ref.dtype)

    out = jax.ShapeDtypeStruct((m, half), MXU_DTYPE)
    oblk = pl.BlockSpec((tm, tn), lambda i, j: (i, j))
    return pl.pallas_call(
        body, name=name, grid=(m // tm, half // tn),
        in_specs=[pl.BlockSpec((tm, k), lambda i, j: (i, 0)), pl.BlockSpec((tn, k), lambda i, j: (j, 0)), oblk, oblk],
        out_specs=[oblk, oblk], out_shape=[out, out], compiler_params=_cparams("parallel", "parallel"),
    )(dff, w, a, b)


def _with_off(xs):
    return [x if isinstance(x, tuple) else (x, 0) for x in xs]


def _spec(kind, arr, off, ts, wb):
    w = arr.shape[-1] if wb is None else wb
    col = (lambda j: 0) if wb is None else functools.partial(lambda j, o: o + j, o=off)
    if kind == "tok":
        return pl.BlockSpec((None, ts, w), lambda j, b, i: (b, i, col(j)))
    if kind == "bat":
        return pl.BlockSpec((None, 1, w), lambda j, b, i: (b, 0, col(j)))
    if off is None:
        return pl.BlockSpec(arr.shape, lambda j, b, i: (0, 0))
    return pl.BlockSpec((arr.shape[0], w), lambda j, b, i: (0, col(j)))


class _Product:
    def __init__(self, a, b, *, tb=False, b_rows=None, add=None):
        self.a, self.b, self.tb, self.b_rows, self.add = a, b, tb, b_rows, add
        rows = b.shape[0] if b_rows is None else b_rows[1]
        self.shape = a.shape[:2] + (rows if tb else b.shape[1],)

    def inputs(self, ts):
        a_spec = pl.BlockSpec((None, ts, self.a.shape[2]), lambda j, b, i: (b, i, 0))
        if self.b_rows is None:
            b_spec = pl.BlockSpec(self.b.shape, lambda j, b, i: (0, 0))
        else:
            start, count = self.b_rows
            b_spec = pl.BlockSpec((pl.Element(count), pl.Element(self.b.shape[1])), lambda j, b, i: (start, 0))
        if isinstance(self.add, _Product):
            extra = self.add.inputs(ts)
        else:
            extra = [] if self.add is None else [(self.add, pl.BlockSpec((None, ts, self.shape[2]), lambda j, b, i: (b, i, 0)))]
        return [(self.a, a_spec), (self.b, b_spec)] + extra

    def value(self, refs):
        dims = (((1,), (1 if self.tb else 0,)), ((), ()))
        val = lax.dot_general(refs[0][...].astype(MXU_DTYPE), refs[1][...].astype(MXU_DTYPE), dims, preferred_element_type=F32)
        if isinstance(self.add, _Product):
            return val + self.add.value(refs[2:])
        return val if self.add is None else val + refs[2][...].astype(F32)


def _inputs(groups, kinds, ts, wb):
    loaded = [(a, _spec(kind, a, o, ts, wb)) for g, kind in zip(groups, kinds) for a, o in g if not isinstance(a, _Product)]
    made = [pair for g in groups for a, _ in g if isinstance(a, _Product) for pair in a.inputs(ts)]
    return [a for a, _ in loaded + made], [sp for _, sp in loaded + made]


def _values(refs, groups):
    n_loaded = sum(1 for g in groups for a, _ in g if not isinstance(a, _Product))
    loaded, pos, out = iter(refs[:n_loaded]), n_loaded, []
    for g in groups:
        vals = []
        for a, _ in g:
            if isinstance(a, _Product):
                k = len(a.inputs(1))
                vals.append(a.value(refs[pos:pos + k]))
                pos += k
            else:
                vals.append(next(loaded)[...].astype(F32))
        out.append(vals)
    return out, pos


def _tok_fwd(fn, toks, bats, pars, outs, *, name, ts, wb=None, cols=1):
    groups = [_with_off(toks), _with_off(bats), _with_off(pars)]
    bl, s, _ = groups[0][0][0].shape
    ts = min(ts, s)
    args, in_specs = _inputs(groups, ("tok", "bat", "par"), ts, wb)

    def body(*refs):
        vals, n_in = _values(refs, groups)
        res = fn(*[v for g in vals for v in g])
        for r, val in zip(refs[n_in:], res):
            r[...] = val.astype(r.dtype)

    out_specs = [pl.BlockSpec((None, ts, w if wb is None else wb), lambda j, b, i: (b, i, j)) for w, _ in outs]
    return pl.pallas_call(
        body, name=name, grid=(cols, bl, s // ts), in_specs=in_specs,
        out_specs=out_specs, out_shape=[jax.ShapeDtypeStruct((bl, s, w), dt) for w, dt in outs],
        compiler_params=_cparams("parallel", "parallel", "parallel"),
    )(*args)


def _accumulate(ref, val, first):
    @pl.when(first)
    def _():
        ref[...] = val

    @pl.when(jnp.logical_not(first))
    def _():
        ref[...] += val


def _tok_bwd(fn, toks, bats, pars, cots, need, *, name, ts, wb=None, cols=1, tok_dtype=F32, loss=False, after=()):
    toks, bats, pars, cots = _with_off(toks), _with_off(bats), _with_off(pars), _with_off(cots)
    groups = [toks, bats, pars, cots]
    bl, s, _ = toks[0][0].shape
    ts = min(ts, s)
    nt, nb, npar = len(toks), len(bats), len(pars)
    args, in_specs = _inputs(groups, ("tok", "bat", "par", "tok"), ts, wb)
    args, in_specs = args + list(after), in_specs + [pl.BlockSpec(memory_space=pl.ANY)] * len(after)

    def body(*refs):
        j, b, i = pl.program_id(0), pl.program_id(1), pl.program_id(2)
        (tok_vals, bat_vals, par_vals, cot_vals), o = _values(refs, groups)
        o += len(after)
        outs, vjp = jax.vjp(fn, *tok_vals, *bat_vals, *par_vals)
        if loss:
            ct = (jnp.ones_like(outs[0]),)
            tot = jnp.broadcast_to(jnp.sum(outs[0], keepdims=True), (1, LANE))
            _accumulate(refs[o], tot, jnp.logical_and(b == 0, i == 0))
            o += 1
        else:
            ct = tuple(cot_vals)
        grads = vjp(ct)
        for t in range(nt):
            if need[t]:
                refs[o][...] = grads[t].astype(refs[o].dtype)
                o += 1
        for t in range(nb):
            _accumulate(refs[o], grads[nt + t], i == 0)
            o += 1
        for t in range(npar):
            first = jnp.logical_and(b == 0, i == 0)
            if pars[t][1] is None:
                first = jnp.logical_and(first, j == 0)
            _accumulate(refs[o], grads[nt + nb + t], first)
            o += 1

    full = lambda arr: arr.shape[-1] if wb is None else wb * cols
    blk = lambda arr: arr.shape[-1] if wb is None else wb
    out_specs, out_shape = [], []
    if loss:
        out_specs.append(pl.BlockSpec((1, LANE), lambda j, b, i: (0, 0)))
        out_shape.append(jax.ShapeDtypeStruct((1, LANE), F32))
    for t in range(nt):
        if need[t]:
            out_specs.append(pl.BlockSpec((None, ts, blk(toks[t][0])), lambda j, b, i: (b, i, j)))
            dt = tok_dtype[t] if isinstance(tok_dtype, (list, tuple)) else tok_dtype
            out_shape.append(jax.ShapeDtypeStruct((bl, s, full(toks[t][0])), dt))
    for arr, _ in bats:
        out_specs.append(pl.BlockSpec((None, 1, blk(arr)), lambda j, b, i: (b, 0, j)))
        out_shape.append(jax.ShapeDtypeStruct((bl, 1, full(arr)), F32))
    for arr, off in pars:
        if off is None:
            out_specs.append(pl.BlockSpec(arr.shape, lambda j, b, i: (0, 0)))
            out_shape.append(jax.ShapeDtypeStruct(arr.shape, F32))
        else:
            out_specs.append(pl.BlockSpec((arr.shape[0], blk(arr)), lambda j, b, i: (0, j)))
            out_shape.append(jax.ShapeDtypeStruct((arr.shape[0], full(arr)), F32))
    res = list(pl.pallas_call(
        body, name=name, grid=(cols, bl, s // ts), in_specs=in_specs,
        out_specs=out_specs, out_shape=out_shape, compiler_params=_cparams("arbitrary", "arbitrary", "arbitrary"),
    )(*args))
    tot = res.pop(0) if loss else None
    dtoks = [res.pop(0) if need[t] else None for t in range(nt)]
    dbats = [res.pop(0) for _ in range(nb)]
    dpars = [res.pop(0) for _ in range(npar)]
    return (tot, dtoks, dbats, dpars) if loss else (dtoks, dbats, dpars)


def _silu(x):
    return x * jax.nn.sigmoid(x)


def _rms(x, w):
    return x * lax.rsqrt(jnp.mean(x * x, axis=-1, keepdims=True) + EPS) * w


def _f_norm_mod(x, shift, scale, w):
    return (_rms(x, w) * (1.0 + scale) + shift,)


def _f_norm_mod_skip(x, shift, scale, w):
    return _rms(x, w) * (1.0 + scale) + shift, x


def _f_res_norm_mod(x, mix, gate, shift, scale, w):
    x2 = x + gate * mix
    return x2, _rms(x2, w) * (1.0 + scale) + shift


def _f_res_norm_mod_keep(x, mix, gate, shift, scale, w):
    return (*_f_res_norm_mod(x, mix, gate, shift, scale, w), mix)


def _f_gates(p, a_log, dt_bias, *, heads):
    z = p + dt_bias
    g = -jnp.exp(a_log) * (jnp.maximum(z, 0.0) + jnp.log1p(jnp.exp(jnp.minimum(z, -z))))
    lane = lax.broadcasted_iota(jnp.int32, p.shape, 1)
    return (jnp.where(lane < heads, g, jax.nn.sigmoid(p)),)


def _f_gdn_out(o, z, w):
    return (_rms(o, w) * _silu(z),)


def _f_merge(ga, gb, ya, yb):
    return (jax.nn.sigmoid(ga) * ya + jax.nn.sigmoid(gb) * yb,)


def _f_merge_keep(ga, gb, ya, yb):
    return (*_f_merge(ga, gb, ya, yb), yb)


def _f_loss(x2, ff, tgt, gate, shift, scale, w):
    y = _rms(x2 + gate * ff, w) * (1.0 + scale) + shift
    return (0.5 * jnp.mean(jnp.square(y - tgt), axis=-1, keepdims=True),)


def _shift_down(x, s):
    if s == 0:
        return x
    row = lax.broadcasted_iota(jnp.int32, x.shape, 0)
    return jnp.where(row >= s, pltpu.roll(x, s, 0), 0.0)


def _shift_up(x, s):
    if s == 0:
        return x
    n = x.shape[0]
    row = lax.broadcasted_iota(jnp.int32, x.shape, 0)
    return jnp.where(row < n - s, pltpu.roll(x, n - s, 0), 0.0)


def _conv(x, w):
    width = w.shape[0]
    acc = w[width - 1:width, :] * x
    for j in range(width - 1):
        acc = acc + w[j:j + 1, :] * _shift_down(x, width - 1 - j)
    return acc


def _conv_bwd(dy, x, w, dw_ref, first):
    width = w.shape[0]
    dx = w[width - 1:width, :] * dy
    for j in range(width - 1):
        dx = dx + w[j:j + 1, :] * _shift_up(dy, width - 1 - j)
    for j in range(width):
        row = jnp.sum(dy * _shift_down(x, width - 1 - j), axis=0, keepdims=True)
        _accumulate(dw_ref.at[j:j + 1, :], row, first)
    return dx


def _qkv_act(xc, is_v, scale):
    a = _silu(xc)
    nrm = a * lax.rsqrt(jnp.sum(a * a, axis=-1, keepdims=True) + EPS) * scale
    return jnp.where(is_v, a, nrm)


def _qkv_act_bwd(xc, dout, is_v, scale):
    sig = jax.nn.sigmoid(xc)
    a = xc * sig
    r = lax.rsqrt(jnp.sum(a * a, axis=-1, keepdims=True) + EPS)
    c1 = r * scale
    da = c1 * dout - a * (c1 * r * r * jnp.sum(dout * a, axis=-1, keepdims=True))
    return jnp.where(is_v, dout, da) * (sig * (1.0 + xc * (1.0 - sig)))


def _qkv_consts(j, heads):
    is_v = j >= 2 * heads
    scale = jnp.where(j < heads, HEAD ** -0.5, 1.0).astype(F32)
    return is_v, scale


def _qkv_fwd(p, w, heads, name):
    bl, s, w3 = p.shape

    def body(p_ref, w_ref, o_ref):
        is_v, scale = _qkv_consts(pl.program_id(0), heads)
        o_ref[...] = _qkv_act(_conv(p_ref[...], w_ref[...]), is_v, scale)

    blk = pl.BlockSpec((None, s, HEAD), lambda j, b: (b, 0, j))
    return pl.pallas_call(
        body, name=name, grid=(w3 // HEAD, bl), in_specs=[blk, pl.BlockSpec((w.shape[0], HEAD), lambda j, b: (0, j))],
        out_specs=blk, out_shape=jax.ShapeDtypeStruct(p.shape, F32), compiler_params=_cparams("parallel", "parallel"),
    )(p, w)


def _qkv_bwd(p, w, dout, heads, name):
    bl, s, w3 = p.shape

    def body(p_ref, w_ref, d_ref, dp_ref, dw_ref):
        is_v, scale = _qkv_consts(pl.program_id(0), heads)
        x, wv = p_ref[...], w_ref[...]
        dxc = _qkv_act_bwd(_conv(x, wv), d_ref[...], is_v, scale)
        dp_ref[...] = _conv_bwd(dxc, x, wv, dw_ref, pl.program_id(1) == 0).astype(dp_ref.dtype)

    blk = pl.BlockSpec((None, s, HEAD), lambda j, b: (b, 0, j))
    wblk = pl.BlockSpec((w.shape[0], HEAD), lambda j, b: (0, j))
    return pl.pallas_call(
        body, name=name, grid=(w3 // HEAD, bl), in_specs=[blk, wblk, blk], out_specs=[blk, wblk],
        out_shape=[jax.ShapeDtypeStruct(p.shape, MXU_DTYPE), jax.ShapeDtypeStruct(w.shape, F32)],
        compiler_params=_cparams("arbitrary", "arbitrary"),
    )(p, w, dout)


def _sc_specs(p, w):
    bl, s, w3 = p.shape
    nblk = w3 // 3 // LANE
    sec = lambda k: pl.BlockSpec((None, s, LANE), functools.partial(lambda j, b, k: (b, 0, k * nblk + j), k=k))
    return nblk, [sec(0), sec(1), sec(2)], pl.BlockSpec((w.shape[0], LANE), lambda j, b: (0, j)), \
        pl.BlockSpec((None, s, LANE), lambda j, b: (b, 0, j))


def _sc_fwd(p, w, name):
    bl, s, w3 = p.shape
    nblk, secs, wblk, oblk = _sc_specs(p, w)

    def body(b_ref, c_ref, x_ref, w_ref, o_ref):
        o_ref[...] = (b_ref[...] * _conv(c_ref[...] * x_ref[...], w_ref[...])).astype(o_ref.dtype)

    return pl.pallas_call(
        body, name=name, grid=(nblk, bl), in_specs=secs + [wblk], out_specs=oblk,
        out_shape=jax.ShapeDtypeStruct((bl, s, w3 // 3), MXU_DTYPE), compiler_params=_cparams("parallel", "parallel"),
    )(p, p, p, w)


def _sc_bwd(p, w, dout, name):
    bl, s, w3 = p.shape
    nblk, secs, wblk, oblk = _sc_specs(p, w)

    def body(b_ref, c_ref, x_ref, w_ref, d_ref, dp_ref, dw_ref):
        gb, gc, xin, wv, d = b_ref[...], c_ref[...], x_ref[...], w_ref[...], d_ref[...]
        u = gc * xin
        dp_ref[0] = (d * _conv(u, wv)).astype(dp_ref.dtype)
        du = _conv_bwd(d * gb, u, wv, dw_ref, pl.program_id(1) == 0)
        dp_ref[1] = (du * xin).astype(dp_ref.dtype)
        dp_ref[2] = (du * gc).astype(dp_ref.dtype)

    return pl.pallas_call(
        body, name=name, grid=(nblk, bl), in_specs=secs + [wblk, oblk],
        out_specs=[pl.BlockSpec((3, None, s, LANE), lambda j, b: (0, b, 0, j)), wblk],
        out_shape=[jax.ShapeDtypeStruct((3, bl, s, w3 // 3), MXU_DTYPE), jax.ShapeDtypeStruct(w.shape, F32)],
        compiler_params=_cparams("arbitrary", "arbitrary"),
    )(p, p, p, w, dout)


def _bdot(a, b, ca, cb):
    return lax.dot_general(a.astype(MXU_DTYPE), b.astype(MXU_DTYPE), (((ca,), (cb,)), ((), ())),
                           preferred_element_type=F32)


def _hdot(a, b):
    return lax.dot_general(a, b, (((1,), (0,)), ((), ())), precision=HIGHEST, preferred_element_type=F32)


def _lane_col(x, idx):
    lane = lax.broadcasted_iota(jnp.int32, x.shape, 1)
    return jnp.sum(jnp.where(lane == idx, x, 0.0), axis=1, keepdims=True)


def _chunk_masks():
    r = lax.broadcasted_iota(jnp.int32, (CHUNK, CHUNK), 0)
    c = lax.broadcasted_iota(jnp.int32, (CHUNK, CHUNK), 1)
    return r == c, r >= c, r > c


def _dot3(a, b):
    ah, bh = a.astype(MXU_DTYPE), b.astype(MXU_DTYPE)
    al, bl = (a - ah.astype(F32)).astype(MXU_DTYPE), (b - bh.astype(F32)).astype(MXU_DTYPE)
    dot = lambda x, y: lax.dot_general(x, y, (((1,), (0,)), ((), ())), preferred_element_type=F32)
    return dot(ah, bh) + (dot(ah, bl) + dot(al, bh))


def _tri_inv_steps(low, eye):
    x = -low
    p = jnp.where(eye, 1.0, 0.0) + x
    span = 2
    while span < CHUNK:
        x = _dot3(x, x)
        yield
        p = p + _dot3(p, x)
        yield
        span *= 2
    return p


def _round_robin(gens):
    out, live = [None] * len(gens), list(range(len(gens)))
    while live:
        still = []
        for i in live:
            try:
                next(gens[i])
                still.append(i)
            except StopIteration as stop:
                out[i] = stop.value
        live = still
    return out


def _gdn_pre(q, k, v, gc, beta, masks):
    eye, causal, strict = masks
    gc_row = jnp.sum(jnp.where(eye, gc, 0.0), axis=0, keepdims=True)
    decay = jnp.where(causal, jnp.exp(jnp.where(causal, gc - gc_row, 0.0)), 0.0)
    eg = jnp.exp(gc)
    gl = gc[CHUNK - 1:CHUNK, :]
    kb, vb = k * beta, v * beta
    both = _bdot(jnp.concatenate([kb, q], axis=0), k, 1, 1)
    low = jnp.where(strict, both[:CHUNK] * decay, 0.0)
    qk = jnp.where(causal, both[CHUNK:] * decay, 0.0)
    rest = jnp.exp(gl - gc)
    return dict(decay=decay, eg=eg, gl=gl, kb=kb, vb=vb, kbe=kb * eg, low=low, qk=qk, qg=q * eg, rest=rest, kdec=k * rest)


GROUP = 4


def _gdn_specs(qkv, gbeta, heads, rev):
    bl, s, w3 = qkv.shape
    d, n = w3 // 3, s // CHUNK
    group = GROUP if n % GROUP == 0 else 1
    steps = n // group
    at = (lambda c: steps - 1 - c) if rev else (lambda c: c)
    assert d == heads * HEAD
    rows = group * CHUNK
    sec = pl.BlockSpec((None, rows, w3), lambda b, c: (b, at(c), 0))
    gspec = pl.BlockSpec((None, rows, LANE), lambda b, c: (b, at(c), 0))
    ospec = pl.BlockSpec((None, rows, d), lambda b, c: (b, at(c), 0))
    sspec = pl.BlockSpec((None, group, heads, HEAD, HEAD), lambda b, c: (b, at(c), 0, 0, 0))
    tspec = pl.BlockSpec((None, group, heads, CHUNK, CHUNK), lambda b, c: (b, at(c), 0, 0, 0))
    return bl, s, d, n, group, sec, gspec, ospec, sspec, tspec


def _gdn_fwd(qkv, gbeta, heads, name):
    bl, s, d, n, group, sec, gspec, ospec, sspec, tspec = _gdn_specs(qkv, gbeta, heads, False)
    rows = lambda sub: slice(sub * CHUNK, (sub + 1) * CHUNK)
    pairs = [(h, sub) for h in range(heads) for sub in range(group)]

    def body(x_ref, g_ref, o_ref, s_ref, t_ref, st_ref):
        @pl.when(pl.program_id(1) == 0)
        def _():
            st_ref[...] = jnp.zeros_like(st_ref)

        masks = _chunk_masks()
        eye, causal, _ = masks
        gblks = [g_ref[rows(sub), :] for sub in range(group)]
        gcs = [_hdot(jnp.where(causal, 1.0, 0.0), gb) for gb in gblks]
        st_all = st_ref[...]

        def free(h, sub):
            q, k, v = (x_ref[rows(sub), sec * d + h * HEAD:sec * d + (h + 1) * HEAD] for sec in range(3))
            pre = _gdn_pre(q, k, v, _lane_col(gcs[sub], h), _lane_col(gblks[sub], heads + h), masks)
            yield
            t = yield from _tri_inv_steps(pre["low"], eye)
            uw = _bdot(t, jnp.concatenate([pre["vb"], pre["kbe"]], axis=1), 1, 0)
            return pre, t, uw[:, :HEAD], uw[:, HEAD:]

        pieces = dict(zip(pairs, _round_robin([free(h, sub) for h, sub in pairs])))

        def carry(h):
            st, outs, starts = st_all[h], [], []
            for sub in range(group):
                pre, _, u, w = pieces[h, sub]
                starts.append(st)
                vnew = u - _bdot(w, st, 1, 0)
                yield
                outs.append(_bdot(pre["qg"], st, 1, 0) + _bdot(pre["qk"], vnew, 1, 0))
                st = st * jnp.exp(pre["gl"]) + _bdot(pre["kdec"], vnew, 0, 0)
                yield
            return outs, starts, st

        carried = _round_robin([carry(h) for h in range(heads)])
        per_sub = lambda pick: [[pick(h, sub) for h in range(heads)] for sub in range(group)]
        o_ref[...] = jnp.concatenate([jnp.concatenate(r, axis=1) for r in per_sub(lambda h, sub: carried[h][0][sub])], axis=0)
        s_ref[...] = jnp.stack([jnp.stack(r) for r in per_sub(lambda h, sub: carried[h][1][sub])])
        t_ref[...] = jnp.stack([jnp.stack(r) for r in per_sub(lambda h, sub: pieces[h, sub][1])])
        st_ref[...] = jnp.stack([carried[h][2] for h in range(heads)])

    return pl.pallas_call(
        body, name=name, grid=(bl, n // group), in_specs=[sec, gspec], out_specs=[ospec, sspec, tspec],
        out_shape=[jax.ShapeDtypeStruct((bl, s, d), F32), jax.ShapeDtypeStruct((bl, n, heads, HEAD, HEAD), F32),
                   jax.ShapeDtypeStruct((bl, n, heads, CHUNK, CHUNK), F32)],
        scratch_shapes=[pltpu.VMEM((heads, HEAD, HEAD), F32)], compiler_params=_cparams("parallel", "arbitrary"),
    )(qkv, gbeta)


def _gdn_bwd(qkv, gbeta, dout, s_all, t_all, heads, name):
    bl, s, d, n, group, sec, gspec, ospec, sspec, tspec = _gdn_specs(qkv, gbeta, heads, True)
    rows = lambda sub: slice(sub * CHUNK, (sub + 1) * CHUNK)
    pairs = [(h, sub) for h in range(heads) for sub in range(group)]
    stack, side = functools.partial(jnp.concatenate, axis=0), functools.partial(jnp.concatenate, axis=1)

    def body(x_ref, g_ref, do_ref, s_ref, t_ref, dx_ref, dg_ref, ds_ref):
        @pl.when(pl.program_id(1) == 0)
        def _():
            ds_ref[...] = jnp.zeros_like(ds_ref)

        masks = _chunk_masks()
        eye, causal, strict = masks
        gblks = [g_ref[rows(sub), :] for sub in range(group)]
        gcs = [_hdot(jnp.where(causal, 1.0, 0.0), gb) for gb in gblks]
        lane = lax.broadcasted_iota(jnp.int32, (CHUNK, LANE), 1)
        last_row = lax.broadcasted_iota(jnp.int32, (CHUNK, 1), 0) == CHUNK - 1
        rowsum = lambda a: jnp.sum(a, axis=1, keepdims=True)
        st_all, t_all_, ds_all = s_ref[...], t_ref[...], ds_ref[...]

        def free(h, sub):
            q, k, v = (x_ref[rows(sub), sec * d + h * HEAD:sec * d + (h + 1) * HEAD] for sec in range(3))
            do = do_ref[rows(sub), h * HEAD:(h + 1) * HEAD]
            beta = _lane_col(gblks[sub], heads + h)
            st, t = st_all[sub, h], t_all_[sub, h]
            pre = _gdn_pre(q, k, v, _lane_col(gcs[sub], h), beta, masks)
            yield
            uw = _bdot(t, side([pre["vb"], pre["kbe"]]), 1, 0)
            u, w = uw[:, :HEAD], uw[:, HEAD:]
            yield
            vnew = u - _bdot(w, st, 1, 0)
            yield
            dqk = jnp.where(causal, _bdot(do, vnew, 1, 1), 0.0)
            dqg = _bdot(do, st, 1, 1)
            return dict(q=q, k=k, v=v, do=do, beta=beta, st=st, t=t, pre=pre, w=w, vnew=vnew, dqk=dqk, dqg=dqg)

        pieces = dict(zip(pairs, _round_robin([free(h, sub) for h, sub in pairs])))

        def carry(h):
            dsn, outs = ds_all[h], {}
            for sub in reversed(range(group)):
                pc = pieces[h, sub]
                pre, st, do = pc["pre"], pc["st"], pc["do"]
                egl = jnp.exp(pre["gl"])
                dkdec = _bdot(pc["vnew"], dsn, 1, 1)
                dvnew = _bdot(pre["kdec"], dsn, 1, 0) + _bdot(pre["qk"], do, 0, 0)
                dgl = jnp.sum(dsn * st, keepdims=True) * egl
                yield
                dw = -_bdot(dvnew, st, 1, 1)
                dsn = dsn * egl + _bdot(stack([pre["qg"], -pc["w"]]), stack([do, dvnew]), 0, 0)
                outs[sub] = (dkdec, dvnew, dgl, dw)
                yield
            return outs, dsn

        carried = _round_robin([carry(h) for h in range(heads)])

        def rest(h, sub):
            pc = pieces[h, sub]
            dkdec, dvnew, dgl, dw = carried[h][0][sub]
            q, k, v, beta, t, pre, dqk, dqg = (pc[x] for x in ("q", "k", "v", "beta", "t", "pre", "dqk", "dqg"))
            decay, eg, kb, vb, kbe, low, qk, qg, kdec = (pre[x] for x in ("decay", "eg", "kb", "vb", "kbe", "low", "qk", "qg", "kdec"))
            dt = _bdot(side([dvnew, dw]), side([vb, kbe]), 1, 1)
            by_t = _bdot(t, side([dvnew, dw]), 0, 0)
            dvb, dkbe = by_t[:, :HEAD], by_t[:, HEAD:]
            yield
            inner = _bdot(dt, t, 1, 1)
            yield
            dlow = -jnp.where(strict, _bdot(t, inner, 0, 0), 0.0)
            da, db = dlow * decay, dqk * decay
            yield
            m = dlow * low + dqk * qk
            kdk = dkdec * kdec
            col_of_m = jnp.sum(jnp.where(eye, jnp.sum(m, axis=0, keepdims=True), 0.0), axis=1, keepdims=True)
            dgc = rowsum(m) - col_of_m + rowsum(dqg * qg) + rowsum(dkbe * kbe) - rowsum(kdk)
            dgc = dgc + jnp.where(last_row, dgl + jnp.sum(kdk, keepdims=True), 0.0)
            by_k = _bdot(stack([da, db]), k, 1, 0)
            dkb = by_k[:CHUNK] + dkbe * eg
            yield
            dk = _bdot(stack([da, db]), stack([kb, q]), 0, 0) + dkdec * pre["rest"] + dkb * beta
            dq = by_k[CHUNK:] + dqg * eg
            dbeta = rowsum(dkb * k) + rowsum(dvb * v)
            return dq, dk, dvb * beta, jnp.where(lane == h, dgc, 0.0) + jnp.where(lane == heads + h, dbeta, 0.0)

        done = dict(zip(pairs, _round_robin([rest(h, sub) for h, sub in pairs])))
        dx_ref[...] = stack([side([done[h, sub][i] for i in range(3) for h in range(heads)]) for sub in range(group)])
        ds_ref[...] = jnp.stack([carried[h][1] for h in range(heads)])
        upper = jnp.where(jnp.logical_or(eye, jnp.logical_not(causal)), 1.0, 0.0)
        dgs = []
        for sub in range(group):
            dgb = done[0, sub][3]
            for h in range(1, heads):
                dgb = dgb + done[h, sub][3]
            dgs.append(jnp.where(lane < heads, _hdot(upper, dgb), dgb))
        dg_ref[...] = stack(dgs)

    return pl.pallas_call(
        body, name=name, grid=(bl, n // group), in_specs=[sec, gspec, ospec, sspec, tspec], out_specs=[sec, gspec],
        out_shape=[jax.ShapeDtypeStruct(qkv.shape, F32), jax.ShapeDtypeStruct((bl, s, LANE), F32)],
        scratch_shapes=[pltpu.VMEM((heads, HEAD, HEAD), F32)], compiler_params=_cparams("parallel", "arbitrary"),
    )(qkv, gbeta, dout, s_all, t_all)


def _position():
    return lax.axis_index("x"), lax.axis_index("y"), lax.axis_index("c")


def _all_gather(x, *, name):
    space = pltpu.VMEM

    def body(x_ref, out_ref, send_sems, recv_sems, local_sem):
        ax, ay, ac = _position()
        me, sibling = (ax, ay, ac), (ax, ay, 1 - ac)
        chips = [(1 - ax, ay), (ax, 1 - ay), (1 - ax, 1 - ay)]

        def slot(px, py, pc):
            return out_ref.at[4 * px + 2 * py + pc]

        def copy(k, block, to, src=None):
            return pltpu.make_async_remote_copy(
                src_ref=slot(*block) if src is None else src, dst_ref=slot(*block), send_sem=send_sems.at[k],
                recv_sem=recv_sems.at[k], device_id=to, device_id_type=MESH_IDS)

        mine = pltpu.make_async_copy(x_ref, slot(*me), local_sem)
        mine.start()
        first = [copy(0, me, sibling, src=x_ref)] + [copy(1 + j, me, (*chip, ac), src=x_ref) for j, chip in enumerate(chips)]
        for cp in first:
            cp.start()
        passed = [copy(4 + j, (*chip, ac), sibling) for j, chip in enumerate(chips)]
        for j, chip in enumerate(chips):
            copy(1 + j, (*chip, ac), me).wait_recv()
            passed[j].start()
        copy(0, sibling, me).wait_recv()
        for j, chip in enumerate(chips):
            copy(4 + j, (*chip, 1 - ac), me).wait_recv()
        for cp in first + passed:
            cp.wait_send()
        mine.wait()

    return pl.pallas_call(
        body, name=name, out_shape=jax.ShapeDtypeStruct((NDEV,) + x.shape, x.dtype),
        in_specs=[pl.BlockSpec(memory_space=space)], out_specs=pl.BlockSpec(memory_space=space),
        scratch_shapes=[pltpu.SemaphoreType.DMA((7,)), pltpu.SemaphoreType.DMA((7,)), pltpu.SemaphoreType.DMA],
    )(x)


class _Rider:
    def __init__(self, arrays, out_shapes, sems, hooks):
        self.arrays, self.out_shapes, self.sems, self.hooks = arrays, out_shapes, sems, hooks


def _gather_rider(xs):
    n = len(xs)

    def hooks(x_refs, out_refs, send_sems, recv_sems):
        ax, ay, ac = _position()
        me, sibling = (ax, ay, ac), (ax, ay, 1 - ac)
        chips = [(1 - ax, ay), (ax, 1 - ay), (1 - ax, 1 - ay)]

        def copies(k, block, to, own=False):
            out = []
            for i in range(n):
                slot = out_refs[i].at[4 * block[0] + 2 * block[1] + block[2]]
                out.append(pltpu.make_async_remote_copy(
                    src_ref=x_refs[i] if own else slot, dst_ref=slot, send_sem=send_sems.at[k, i], recv_sem=recv_sems.at[k, i],
                    device_id=to, device_id_type=MESH_IDS))
            return out

        def first():
            for cp in copies(0, me, sibling, own=True):
                cp.start()
            for j, chip in enumerate(chips):
                for cp in copies(1 + j, me, (*chip, ac), own=True):
                    cp.start()

        def mid():
            for j, chip in enumerate(chips):
                for arrived, onward in zip(copies(1 + j, (*chip, ac), me), copies(4 + j, (*chip, ac), sibling)):
                    arrived.wait_recv()
                    onward.start()

        def last():
            for cp in copies(0, sibling, me):
                cp.wait_recv()
            for j, chip in enumerate(chips):
                for cp in copies(4 + j, (*chip, 1 - ac), me):
                    cp.wait_recv()
            for cp in copies(0, me, sibling, own=True):
                cp.wait_send()
            for j, chip in enumerate(chips):
                for cp in copies(1 + j, me, (*chip, ac), own=True) + copies(4 + j, (*chip, ac), sibling):
                    cp.wait_send()

        return first, mid, last

    return _Rider(list(xs), [jax.ShapeDtypeStruct((NDEV,) + x.shape, x.dtype) for x in xs],
                  [pltpu.SemaphoreType.DMA((7, n)), pltpu.SemaphoreType.DMA((7, n))], hooks)


def _scatter_rider(parts):
    packed = sum(r for _, r in parts)
    width, dtype = parts[0][0].shape[1], parts[0][0].dtype

    def hooks(g_refs, out_refs, send_sems, recv_sems):
        (recv_ref,) = out_refs
        ax, ay, ac = _position()

        def peer(rel):
            flip = lambda a, bit: 1 - a if rel & bit else a
            return flip(ax, 4), flip(ay, 2), flip(ac, 1)

        def first():
            for rel in range(1, NDEV):
                px, py, pc = peer(rel)
                off = 0
                for g_ref, (_, r) in zip(g_refs, parts):
                    rows = g_ref.at[pl.ds(pl.multiple_of((4 * px + 2 * py + pc) * r, ROW_ALIGN), r)]
                    pltpu.make_async_remote_copy(
                        src_ref=rows, dst_ref=recv_ref.at[rel - 1, pl.ds(off, r)], send_sem=send_sems.at[rel - 1],
                        recv_sem=recv_sems.at[rel - 1], device_id=(px, py, pc), device_id_type=MESH_IDS).start()
                    off += r

        def last():
            for rel in range(1, NDEV):
                slot = recv_ref.at[rel - 1]
                pltpu.make_async_remote_copy(src_ref=slot, dst_ref=slot, send_sem=send_sems.at[rel - 1],
                                             recv_sem=recv_sems.at[rel - 1], device_id=peer(rel), device_id_type=MESH_IDS).wait()

        return first, lambda: None, last

    return _Rider([g for g, _ in parts], [jax.ShapeDtypeStruct((NDEV - 1, packed, width), dtype)],
                  [pltpu.SemaphoreType.DMA((NDEV - 1,)), pltpu.SemaphoreType.DMA((NDEV - 1,))], hooks)


def _sum_direct(own, recv, name):
    r, w = own.shape
    tr = max(t for t in range(ROW_ALIGN, 257, ROW_ALIGN) if r % t == 0)

    def body(own_ref, *refs):
        acc = own_ref[...].astype(F32)
        for ref in refs[:-1]:
            acc = acc + ref[...].astype(F32)
        refs[-1][...] = acc

    rblk = lambda k: pl.BlockSpec((None, tr, w), functools.partial(lambda i, k: (k, i, 0), k=k))
    blk = pl.BlockSpec((tr, w), lambda i: (i, 0))
    return pl.pallas_call(body, name=name, grid=(r // tr,), in_specs=[blk] + [rblk(k) for k in range(NDEV - 1)],
                          out_specs=blk, out_shape=jax.ShapeDtypeStruct((r, w), F32),
                          compiler_params=_cparams("parallel"))(own, *([recv] * (NDEV - 1)))


ROW_ALIGN = 16


def _window_start(rows_per_dev, k):
    return rows_per_dev * k // ROW_ALIGN * ROW_ALIGN


def _exchange_in_chip(parts, name, collective_id):
    packed = sum(win for _, _, win, _ in parts)
    width, dtype = parts[0][0].shape[1], parts[0][0].dtype

    def body(g_refs, out_refs, send_sems, recv_sems):
        (recv_ref,) = out_refs
        ax, ay, ac = _position()
        sibling = (ax, ay, 1 - ac)
        _handshake([sibling])
        for q in range(4):
            for g_ref, (_, r, win, off) in zip(g_refs, parts):
                there = g_ref.at[pl.ds(pl.multiple_of(_window_start(r, 2 * q + 1 - ac), ROW_ALIGN), win)]
                pltpu.make_async_remote_copy(src_ref=there, dst_ref=recv_ref.at[q, pl.ds(off, win)], send_sem=send_sems.at[q],
                                             recv_sem=recv_sems.at[q], device_id=sibling, device_id_type=MESH_IDS).start()
        for q in range(4):
            pltpu.make_async_remote_copy(src_ref=recv_ref.at[q], dst_ref=recv_ref.at[q], send_sem=send_sems.at[q],
                                         recv_sem=recv_sems.at[q], device_id=sibling, device_id_type=MESH_IDS).wait()

    return _on_sequencer(body, [g for g, _, _, _ in parts], [jax.ShapeDtypeStruct((4, packed, width), dtype)],
                         [pltpu.SemaphoreType.DMA((4,)), pltpu.SemaphoreType.DMA((4,))], name=name, collective_id=collective_id)[0]


def _on_sequencer(body, ins, out_shapes, sems, *, name, collective_id):
    hbm = pltpu.MemorySpace.HBM
    in_refs = [jax.new_ref(a, memory_space=hbm) for a in ins]
    out_refs = [jax.empty_ref(s, memory_space=hbm) for s in out_shapes]

    @pl.kernel(mesh=plsc.ScalarSubcoreMesh(axis_name="sequencer", num_cores=1), name=name, scratch_types=tuple(sems),
               compiler_params=pltpu.CompilerParams(collective_id=collective_id))
    def launch(*sem_refs):
        body(in_refs, out_refs, *sem_refs)

    launch()
    return [r[...] for r in out_refs]


def _handshake(peers):
    barrier = pltpu.get_barrier_semaphore()
    for peer in peers:
        pl.semaphore_signal(barrier, inc=1, device_id=peer, device_id_type=MESH_IDS)
    pl.semaphore_wait(barrier, len(peers))


def _exchange_chips_async(s1, name, collective_id):
    def body(in_refs, out_refs, send_sems, recv_sems):
        (src,), (got,) = in_refs, out_refs
        ax, ay, ac = _position()
        chips = [(1 - ax, ay), (ax, 1 - ay), (1 - ax, 1 - ay)]
        _handshake([(cx, cy, ac) for cx, cy in chips])
        copies = [pltpu.make_async_remote_copy(
            src_ref=src.at[2 * cx + cy], dst_ref=got.at[r], send_sem=send_sems.at[r], recv_sem=recv_sems.at[r],
            device_id=(cx, cy, ac), device_id_type=MESH_IDS) for r, (cx, cy) in enumerate(chips)]
        for cp in copies:
            cp.start()
        for cp in copies:
            cp.wait_recv()
        for cp in copies:
            cp.wait_send()

    return _on_sequencer(body, [s1], [jax.ShapeDtypeStruct((3,) + s1.shape[1:], s1.dtype)],
                         [pltpu.SemaphoreType.DMA((3,)), pltpu.SemaphoreType.DMA((3,))], name=name, collective_id=collective_id)[0]


def _gather_async(xs, name, collective_id):
    rider = _gather_rider(xs)

    def body(in_refs, out_refs, send_sems, recv_sems):
        ax, ay, ac = _position()
        _handshake([(ax, ay, 1 - ac), (1 - ax, ay, ac), (ax, 1 - ay, ac), (1 - ax, 1 - ay, ac)])
        for hook in rider.hooks(in_refs, out_refs, send_sems, recv_sems):
            hook()

    return _on_sequencer(body, rider.arrays, rider.out_shapes, rider.sems, name=name, collective_id=collective_id)


def _gather_balanced(x, name, collective_id):
    m = x.shape[0]
    half = m // 2 // ROW_ALIGN * ROW_ALIGN
    parts = {"all": pl.ds(0, m), "lo": pl.ds(0, half), "hi": pl.ds(half, m - half)}

    def body(in_refs, out_refs, send_sems, recv_sems):
        (x_ref,), (out_ref,) = in_refs, out_refs
        ax, ay, ac = _position()
        me, sibling = (ax, ay, ac), (ax, ay, 1 - ac)
        by_x, by_y, diag = (1 - ax, ay), (ax, 1 - ay), (1 - ax, 1 - ay)
        _handshake([sibling, (*by_x, ac), (*by_y, ac)])

        def copy(k, block, part, to, own=False):
            rows = out_ref.at[4 * block[0] + 2 * block[1] + block[2], parts[part]]
            return pltpu.make_async_remote_copy(src_ref=x_ref if own else rows, dst_ref=rows, send_sem=send_sems.at[k],
                                                recv_sem=recv_sems.at[k], device_id=to, device_id_type=MESH_IDS)

        def own_half(k, part, to):
            rows = out_ref.at[4 * ax + 2 * ay + ac, parts[part]]
            return pltpu.make_async_remote_copy(src_ref=x_ref.at[parts[part]], dst_ref=rows, send_sem=send_sems.at[k],
                                                recv_sem=recv_sems.at[k], device_id=to, device_id_type=MESH_IDS)

        nx, ny, nd = (*by_x, ac), (*by_y, ac), (*diag, ac)
        sends = [copy(0, me, "all", sibling, own=True), own_half(1, "lo", nx), own_half(2, "hi", nx),
                 own_half(3, "hi", ny), own_half(4, "lo", ny), copy(5, nx, "lo", ny), copy(6, ny, "hi", nx),
                 copy(7, nx, "lo", sibling), copy(8, nx, "hi", sibling), copy(9, ny, "hi", sibling),
                 copy(10, ny, "lo", sibling), copy(11, nd, "lo", sibling), copy(12, nd, "hi", sibling)]
        sx, sy, sd = (*by_x, 1 - ac), (*by_y, 1 - ac), (*diag, 1 - ac)
        arrivals = [copy(0, sibling, "all", me), copy(1, nx, "lo", me), copy(2, nx, "hi", me), copy(3, ny, "hi", me),
                    copy(4, ny, "lo", me), copy(5, nd, "lo", me), copy(6, nd, "hi", me), copy(7, sx, "lo", me),
                    copy(8, sx, "hi", me), copy(9, sy, "hi", me), copy(10, sy, "lo", me), copy(11, sd, "lo", me),
                    copy(12, sd, "hi", me)]
        for k in range(5):
            sends[k].start()
        for arrived, onward in ((1, (5, 7)), (3, (6, 9)), (2, (8,)), (4, (10,)), (5, (11,)), (6, (12,))):
            arrivals[arrived].wait_recv()
            for k in onward:
                sends[k].start()
        for k in (0, 7, 8, 9, 10, 11, 12):
            arrivals[k].wait_recv()
        for cp in sends:
            cp.wait_send()

    return _on_sequencer(body, [x], [jax.ShapeDtypeStruct((NDEV,) + x.shape, x.dtype)],
                         [pltpu.SemaphoreType.DMA((13,)), pltpu.SemaphoreType.DMA((13,))], name=name, collective_id=collective_id)[0]


def _scatter_async(parts, name, collective_id):
    rider = _scatter_rider(parts)

    def body(in_refs, out_refs, send_sems, recv_sems):
        ax, ay, ac = _position()
        flip = lambda a, on: 1 - a if on else a
        _handshake([(flip(ax, rel & 4), flip(ay, rel & 2), flip(ac, rel & 1)) for rel in range(1, NDEV)])
        for hook in rider.hooks(in_refs, out_refs, send_sems, recv_sems):
            hook()

    return _on_sequencer(body, rider.arrays, rider.out_shapes, rider.sems, name=name, collective_id=collective_id)[0]


def _sum_in_chip(own, recv, name):
    _, r, w = own.shape
    tr = _tile(r, (256, 128))

    def body(a_ref, b_ref, o_ref):
        o_ref[...] = (a_ref[...].astype(F32) + b_ref[...].astype(F32)).astype(o_ref.dtype)

    blk = pl.BlockSpec((None, tr, w), lambda q, i: (q, i, 0))
    return pl.pallas_call(body, name=name, grid=(4, r // tr), in_specs=[blk, blk], out_specs=blk,
                          out_shape=jax.ShapeDtypeStruct(own.shape, own.dtype),
                          compiler_params=_cparams("parallel", "parallel"))(own, recv)


def _sum_chips(s1, recv, chip, name):
    _, r, w = s1.shape
    tr = _tile(r, (256, 128))

    def body(c_ref, s_ref, r0_ref, r1_ref, r2_ref, o_ref):
        f = lambda ref: ref[...].astype(F32)
        o_ref[...] = ((f(s_ref) + f(r0_ref)) + f(r1_ref)) + f(r2_ref)

    rblk = lambda k: pl.BlockSpec((None, tr, w), functools.partial(lambda i, c, k: (k, i, 0), k=k))
    grid_spec = pltpu.PrefetchScalarGridSpec(
        num_scalar_prefetch=1, grid=(r // tr,),
        in_specs=[pl.BlockSpec((None, tr, w), lambda i, c: (c[0], i, 0)), rblk(0), rblk(1), rblk(2)],
        out_specs=pl.BlockSpec((tr, w), lambda i, c: (i, 0)))
    return pl.pallas_call(body, name=name, grid_spec=grid_spec, out_shape=jax.ShapeDtypeStruct((r, w), F32),
                          compiler_params=_cparams("parallel"))(chip, s1, recv, recv, recv)


def _silu_rows(x, name):
    def body(x_ref, o_ref):
        o_ref[...] = _silu(x_ref[...])

    return pl.pallas_call(body, name=name, out_shape=jax.ShapeDtypeStruct(x.shape, F32))(x)


def _row_sum(x, name):
    def body(x_ref, o_ref):
        acc = x_ref[0:1, :]
        for i in range(1, x.shape[0]):
            acc = acc + x_ref[i:i + 1, :]
        o_ref[...] = acc

    return pl.pallas_call(body, name=name, out_shape=jax.ShapeDtypeStruct((1, x.shape[1]), F32))(x)


def _adamw(w, g, m, v, name):
    cols = w.shape[-1]
    rows = w.size // cols
    tr = _tile(rows, (256, 128))
    tc = LANE if (tr == rows and rows > 512 and cols % LANE == 0) else cols

    def body(w_ref, g_ref, m_ref, v_ref, d_ref, mo_ref, vo_ref):
        grad = g_ref[...]
        m_new = ADAM_B1 * m_ref[...] + (1.0 - ADAM_B1) * grad
        v_new = ADAM_B2 * v_ref[...] + (1.0 - ADAM_B2) * jnp.square(grad)
        m_hat = m_new / (1.0 - ADAM_B1 ** ADAM_STEP)
        v_hat = v_new / (1.0 - ADAM_B2 ** ADAM_STEP)
        d_ref[...] = -ADAM_LR * (m_hat / (jnp.sqrt(v_hat) + ADAM_EPS) + ADAM_WD * w_ref[...])
        mo_ref[...] = m_new
        vo_ref[...] = v_new

    blk = pl.BlockSpec((tr, tc), lambda i, j: (i, j))
    out = pl.pallas_call(
        body, name=name, grid=(rows // tr, cols // tc), in_specs=[blk] * 4, out_specs=[blk] * 3,
        out_shape=[jax.ShapeDtypeStruct((rows, cols), F32)] * 3, compiler_params=_cparams("parallel", "parallel"),
    )(*[t.reshape(rows, cols) for t in (w, g, m, v)])
    return [t.reshape(w.shape) for t in out]


def _pack(parts, width, row_mult, dtype):
    flat = jnp.concatenate([p.reshape(-1).astype(dtype) for p in parts])
    rows = -(-flat.shape[0] // (width * row_mult)) * row_mult
    return jnp.pad(flat, (0, rows * width - flat.shape[0])).reshape(rows, width)


def _unpack(flat, shapes):
    out, off = [], 0
    for shp in shapes:
        size = 1
        for dim in shp:
            size *= dim
        out.append(flat[:, off:off + size].reshape((flat.shape[0],) + tuple(shp)))
        off += size
    return out


def _devices_to_cols(a):
    _, r, c = a.shape
    return a.transpose(1, 0, 2).reshape(r, NDEV * c)


def kernel(x, c, w_ada, b_ada, norm1_w, w_in, gdn_conv_w, gdn_a_log, gdn_dt_bias, gdn_norm_w, w_gdn_proj, sc_conv_w, w_sc_out, w_o, norm2_w, w_ffn_in, w_ffn_out, w_ada_f, b_ada_f, normf_w, loss_target, m_w_ada, m_b_ada, m_norm1_w, m_w_in, m_gdn_conv_w, m_gdn_a_log, m_gdn_dt_bias, m_gdn_norm_w, m_w_gdn_proj, m_sc_conv_w, m_w_sc_out, m_w_o, m_norm2_w, m_w_ffn_in, m_w_ffn_out, m_w_ada_f, m_b_ada_f, m_normf_w, v_w_ada, v_b_ada, v_norm1_w, v_w_in, v_gdn_conv_w, v_gdn_a_log, v_gdn_dt_bias, v_gdn_norm_w, v_w_gdn_proj, v_sc_conv_w, v_w_sc_out, v_w_o, v_norm2_w, v_w_ffn_in, v_w_ffn_out, v_w_ada_f, v_b_ada_f, v_normf_w):
    bl, s, d = x.shape
    heads = gdn_a_log.shape[-1]
    dff = w_ffn_out.shape[1] * NDEV
    tok = bl * s
    ax, ay, ac = _position()
    dev = 4 * ax + 2 * ay + ac
    as_tok = lambda a: a.reshape(bl, s, a.shape[-1])
    as_mat = lambda a: a.reshape(tok, a.shape[-1])

    small = _all_gather(_pack([c, gdn_conv_w, sc_conv_w], LANE, 8, F32), name="gather_cond")
    c_all, conv_w, sc_w = _unpack(small.reshape(NDEV, -1), [(bl, d), gdn_conv_w.shape[1:], sc_conv_w.shape[1:]])
    c_act = _silu_rows(c_all.reshape(NDEV * bl, d), "cond_silu")
    conv_w, sc_w = _devices_to_cols(conv_w), _devices_to_cols(sc_w)
    n_ada, n_adaf = w_ada.shape[-1], w_ada_f.shape[-1]
    bias = jnp.broadcast_to(lax.dynamic_slice_in_dim(b_ada, dev * n_ada, n_ada, axis=1), (NDEV * bl, n_ada))
    biasf = jnp.broadcast_to(lax.dynamic_slice_in_dim(b_ada_f.reshape(1, -1), dev * n_adaf, n_adaf, axis=1), (NDEV * bl, n_adaf))
    mod_cols = _mm(c_act, w_ada[0], add=bias, name="ada_cols")
    modf_cols = _mm(c_act, w_ada_f, add=biasf, name="adaf_cols")
    mods = _all_gather(jnp.concatenate([mod_cols, modf_cols], axis=1), name="gather_mod")
    mod_all = mods[:, :, :n_ada].transpose(1, 0, 2).reshape(NDEV * bl, NDEV * n_ada)
    modf_all = mods[:, :, n_ada:].transpose(1, 0, 2).reshape(NDEV * bl, NDEV * n_adaf)
    my_rows = lambda a: lax.dynamic_slice_in_dim(a, dev * bl, bl, axis=0)
    sh1, sc1, g1, sh2, sc2, g2 = [t.reshape(bl, 1, d) for t in jnp.split(my_rows(mod_all), 6, axis=1)]
    shf, scf = [t.reshape(bl, 1, d) for t in jnp.split(my_rows(modf_all), 2, axis=1)]

    late = [t.astype(MXU_DTYPE) for t in (w_gdn_proj[0], w_sc_out[0], w_o[0], w_ffn_in[0].T, w_ffn_out[0])]
    rows = [t.shape[0] for t in late] + [w_in.shape[-1]]
    offs = [sum(rows[:i]) for i in range(5)]
    in_send = w_in[0].T.astype(MXU_DTYPE)
    with_own = lambda g, own: lax.dynamic_update_slice_in_dim(g, own[None], dev, axis=0)
    wt_in = with_own(_gather_balanced(in_send, "gather_w_in", 1), in_send).reshape(NDEV * rows[5], d)
    gathered = _gather_async(late[:3], "gather_mixer", 2) + _gather_async(late[3:], "gather_ffn", 3)
    wgp, wso, wo, wt_fi, wfo = [with_own(g, own).reshape(NDEV * own.shape[0], d) for g, own in zip(gathered, late)]
    o_z, o_ab, o_sc, o_ga, o_gb = 3 * d, 4 * d, 4 * d + 2 * heads, 7 * d + 2 * heads, 8 * d + 2 * heads
    s_qkv, s_z, s_sc, s_gate = (0, o_z), (o_z, d), (o_sc, 3 * d), (o_ga, 2 * d)
    wt_ab = jnp.pad(wt_in[o_ab:o_sc], ((0, LANE - 2 * heads), (0, 0)))

    n1w, n2w, nfw = norm1_w.reshape(1, d), norm2_w.reshape(1, d), normf_w.reshape(1, d)
    lanes = lambda a: jnp.pad(a.reshape(1, -1), ((0, 0), (0, LANE - a.size)))
    a_log, dt_bias, gnw = lanes(gdn_a_log), lanes(gdn_dt_bias), gdn_norm_w.reshape(1, HEAD)
    f_gates = functools.partial(_f_gates, heads=heads)
    (h1,) = _tok_fwd(_f_norm_mod, [x], [sh1, sc1], [n1w], [(d, MXU_DTYPE)], name="norm1", ts=512)
    h1m = as_mat(h1)
    p_qkv = as_tok(_mm_ring(h1m, wt_in, s_qkv, name="in_qkv"))
    p_z = as_tok(_mm(h1m, wt_in, tb=True, b_rows=s_z, name="in_z"))
    p_ab = as_tok(_mm(h1m, wt_ab, tb=True, name="in_ab"))
    p_sc = as_tok(_mm_ring(h1m, wt_in, s_sc, name="in_sc"))
    p_g = as_tok(_mm_ring(h1m, wt_in, s_gate, name="in_gate"))
    qkv = _qkv_fwd(p_qkv, conv_w, heads, "qkv_conv")
    (gbeta,) = _tok_fwd(f_gates, [p_ab], [], [a_log, dt_bias], [(LANE, F32)], name="gates", ts=512)
    o, s_all, t_all = _gdn_fwd(qkv, gbeta, heads, "gdn")
    (og,) = _tok_fwd(_f_gdn_out, [o, p_z], [], [(gnw, None)], [(d, MXU_DTYPE)], name="gdn_out", ts=2048, wb=HEAD, cols=heads)
    y_a = as_tok(_mm(as_mat(og), wgp, name="gdn_proj"))
    scp = _sc_fwd(p_sc, sc_w, "sc_conv")
    mrg, y_b = _tok_fwd(_f_merge_keep, [(p_g, 0), (p_g, 1), y_a, _Product(scp, wso)], [], [], [(d, MXU_DTYPE), (d, F32)],
                        name="merge", ts=512, wb=d)
    merge_toks = [(p_g, 0), (p_g, 1), y_a, y_b]
    x2, h2, mix = _tok_fwd(_f_res_norm_mod_keep, [x, _Product(mrg, wo)], [g1, sh2, sc2], [n2w],
                           [(d, F32), (d, MXU_DTYPE), (d, F32)], name="norm2", ts=512)
    act, gu_a, gu_b = _ffn_in_swiglu(as_mat(h2), wt_fi, dff, "ffn_in")

    loss_l, (dx2, dff_out, _), (dg2, dshf, dscf), (dnfw,) = _tok_bwd(
        _f_loss, [x2, _Product(as_tok(act), wfo), loss_target], [g2, shf, scf], [nfw], [], [True, True, False], name="loss",
        ts=512, loss=True, tok_dtype=[F32, MXU_DTYPE, None])
    dffm = as_mat(dff_out)
    dgu_a, dgu_b = _ffn_out_bwd_swiglu(dffm, wfo, gu_a, gu_b, "d_ffn_out")
    gmm = functools.partial(_mm, ta=True, out_dtype=MXU_DTYPE)
    gw_ffn_out = gmm(act, dffm, name="g_ffn_out")
    dh2 = _Product(as_tok(dgu_b), wt_fi, b_rows=(dff, dff), add=_Product(as_tok(dgu_a), wt_fi, b_rows=(0, dff)))
    h2m = as_mat(h2)
    gwt_ffn_in = gmm(dgu_a, h2m, out_rows=2 * dff, name="g_ffn_in_a")
    gwt_ffn_in = gmm(dgu_b, h2m, out_rows=2 * dff, row_off=dff, into=gwt_ffn_in, name="g_ffn_in_b")
    ffn_parts = [(gwt_ffn_in, rows[3]), (gw_ffn_out, rows[4])]
    ffn_recv = _scatter_async(ffn_parts, "scatter_ffn", 4)
    (dx_skip, dmix), (dg1, dsh2, dsc2), (dn2w,) = _tok_bwd(
        _f_res_norm_mod, [x, mix], [g1, sh2, sc2], [n2w], [dx2, dh2], [True, True], name="d_norm2", ts=512,
        tok_dtype=[F32, MXU_DTYPE], after=[gwt_ffn_in, gw_ffn_out])
    gw_o = gmm(as_mat(mrg), as_mat(dmix), name="g_mix_out")
    (dga, dgb, dya, dyb), _, _ = _tok_bwd(_f_merge, merge_toks, [], [], [_Product(dmix, wo, tb=True)], [True] * 4,
                                          name="d_merge", ts=512, wb=d, tok_dtype=MXU_DTYPE)
    dyam, dybm = as_mat(dya), as_mat(dyb)
    dog = as_tok(_mm(dyam, wgp, tb=True, name="d_gdn_proj"))
    gw_gdn_proj = gmm(as_mat(og), dyam, name="g_gdn_proj")
    dscp = as_tok(_mm(dybm, wso, tb=True, name="d_sc_out"))
    gw_sc_out = gmm(as_mat(scp), dybm, name="g_sc_out")
    dsc, g_sc_w = _sc_bwd(p_sc, sc_w, dscp, "d_sc_conv")
    mix_parts = [(gw_gdn_proj, rows[0]), (gw_sc_out, rows[1]), (gw_o, rows[2])]
    mix_recv = _scatter_async(mix_parts, "scatter_mixer", 5)
    (do, dz), _, (g_gnw,) = _tok_bwd(_f_gdn_out, [o, p_z], [], [(gnw, None)], [dog], [True, True], name="d_gdn_out",
                                     ts=2048, wb=HEAD, cols=heads, tok_dtype=[F32, MXU_DTYPE],
                                     after=[gw_gdn_proj, gw_sc_out, gw_o])
    own_rows = lambda parts: jnp.concatenate([lax.dynamic_slice_in_dim(g, dev * r, r, axis=0) for g, r in parts], axis=0)
    dqkv, dgbeta = _gdn_bwd(qkv, gbeta, do, s_all, t_all, heads, "d_gdn")
    dp_qkv, g_conv_w = _qkv_bwd(p_qkv, conv_w, dqkv, heads, "d_qkv_conv")
    ffn_red = _sum_direct(own_rows(ffn_parts), ffn_recv, "sum_ffn")
    mix_red = _sum_direct(own_rows(mix_parts), mix_recv, "sum_mix")
    (dp_ab,), _, (g_a_log, g_dt_bias) = _tok_bwd(f_gates, [p_ab], [], [a_log, dt_bias], [dgbeta], [True], name="d_gates",
                                                 ts=512, tok_dtype=MXU_DTYPE, after=[ffn_red, mix_red])
    r_in = rows[5]
    win = -(-(r_in + max(r_in * k % ROW_ALIGN for k in range(NDEV))) // 128) * 128
    need_rows = max(_window_start(r_in, k) for k in range(NDEV)) + win
    dsc_m = dsc.reshape(3, tok, d)
    gwt_in = ([gmm(as_mat(dp_qkv), h1m, name="g_in_qkv"), gmm(as_mat(dz), h1m, name="g_in_z"),
               gmm(as_mat(dp_ab), h1m, name="g_in_ab")[:2 * heads]]
              + [gmm(dsc_m, h1m, a_index=k, name=f"g_in_sc{k}") for k in range(3)]
              + [gmm(as_mat(dga), h1m, name="g_in_ga"), gmm(as_mat(dgb), h1m, name="g_in_gb")])
    gwt_in = jnp.concatenate(gwt_in + [jnp.zeros((need_rows - NDEV * r_in, d), MXU_DTYPE)], axis=0)
    assert d <= 1024
    wide = [as_mat(dp_qkv), as_mat(dz), dsc_m, as_mat(dga)]
    row_of = lambda t: d * t + jnp.where(t * d >= o_ab, 2 * heads, 0)
    recv1 = _exchange_in_chip([(gwt_in, r_in, win, 0)], "scatter_in_chip", 7)
    own = jnp.stack([lax.dynamic_slice_in_dim(gwt_in, _window_start(r_in, 2 * q + ac), win, axis=0) for q in range(4)])
    s1 = _sum_in_chip(own, recv1, "sum_in_chip")
    recv2 = _exchange_chips_async(s1, "scatter_chips", 6)

    dh1 = _mm(as_mat(dp_ab), wt_ab, name="d_in_ab")
    dh1 = _mm_chain(wide, wt_in, row_of, add=dh1, name="d_in", tk=d)
    dh1 = _Product(dgb, wt_in, b_rows=(o_gb, d), add=as_tok(dh1))
    (grad_x,), (dsh1, dsc1), (dn1w,) = _tok_bwd(_f_norm_mod_skip, [x], [sh1, sc1], [n1w], [dh1, dx_skip], [True],
                                                name="d_norm1", ts=512)
    reduced = _sum_chips(s1, recv2, (2 * ax + ay).reshape(1).astype(jnp.int32), "sum_chips")
    gt_w_in = lax.dynamic_slice_in_dim(reduced, r_in * dev - _window_start(r_in, dev), r_in, axis=0)
    g_w_in = gt_w_in.T.reshape(w_in.shape)
    gt_w_ffn_in = ffn_red[:rows[3]]
    g_w_ffn_in = gt_w_ffn_in.T.reshape(w_ffn_in.shape)
    g_w_ffn_out = ffn_red[rows[3]:].reshape(w_ffn_out.shape)
    g_w_gdn_proj, g_w_sc_out, g_w_o = (mix_red[offs[i]:offs[i] + rows[i]].reshape(ref.shape)
                                       for i, ref in enumerate((w_gdn_proj, w_sc_out, w_o)))

    dmod = jnp.concatenate([t.reshape(bl, d) for t in (dsh1, dsc1, dg1, dsh2, dsc2, dg2)], axis=1)
    dmodf = jnp.concatenate([t.reshape(bl, d) for t in (dshf, dscf)], axis=1)
    summed_parts = [dn1w, dn2w, dnfw, g_gnw, g_a_log, g_dt_bias, g_conv_w, g_sc_w, loss_l]
    partial = _all_gather(_pack([dmod, dmodf] + summed_parts, LANE, 8, F32), name="gather_small")
    partial = partial.reshape(NDEV, -1)
    n_rows = bl * (6 * d + 2 * d)
    dmod_all, dmodf_all = _unpack(partial[:, :n_rows], [(bl, 6 * d), (bl, 2 * d)])
    dmod_all, dmodf_all = dmod_all.reshape(NDEV * bl, 6 * d), dmodf_all.reshape(NDEV * bl, 2 * d)
    totals = _row_sum(partial[:, n_rows:], "sum_small")
    t_n1w, t_n2w, t_nfw, t_gnw, t_a_log, t_dt_bias, t_conv_w, t_sc_w, t_loss = [
        t[0] for t in _unpack(totals, [p.shape for p in summed_parts])]
    my_cols = lambda a, n: lax.dynamic_slice_in_dim(a, dev * n, n, axis=1)
    grads = {
        "w_ada": _mm(c_act, my_cols(dmod_all, n_ada), ta=True, name="g_ada").reshape(w_ada.shape),
        "b_ada": _row_sum(dmod_all, "g_ada_bias").reshape(b_ada.shape),
        "norm1_w": t_n1w.reshape(norm1_w.shape),
        "w_in": g_w_in,
        "gdn_conv_w": my_cols(t_conv_w, gdn_conv_w.shape[-1]).reshape(gdn_conv_w.shape),
        "gdn_a_log": t_a_log[:, :heads].reshape(gdn_a_log.shape),
        "gdn_dt_bias": t_dt_bias[:, :heads].reshape(gdn_dt_bias.shape),
        "gdn_norm_w": t_gnw.reshape(gdn_norm_w.shape),
        "w_gdn_proj": g_w_gdn_proj,
        "sc_conv_w": my_cols(t_sc_w, sc_conv_w.shape[-1]).reshape(sc_conv_w.shape),
        "w_sc_out": g_w_sc_out,
        "w_o": g_w_o,
        "norm2_w": t_n2w.reshape(norm2_w.shape),
        "w_ffn_in": g_w_ffn_in,
        "w_ffn_out": g_w_ffn_out,
        "w_ada_f": _mm(c_act, my_cols(dmodf_all, n_adaf), ta=True, name="g_adaf").reshape(w_ada_f.shape),
        "b_ada_f": _row_sum(dmodf_all, "g_adaf_bias").reshape(b_ada_f.shape),
        "normf_w": t_nfw.reshape(normf_w.shape),
    }
    weights = dict(w_ada=w_ada, b_ada=b_ada, norm1_w=norm1_w, w_in=w_in, gdn_conv_w=gdn_conv_w, gdn_a_log=gdn_a_log,
                   gdn_dt_bias=gdn_dt_bias, gdn_norm_w=gdn_norm_w, w_gdn_proj=w_gdn_proj, sc_conv_w=sc_conv_w,
                   w_sc_out=w_sc_out, w_o=w_o, norm2_w=norm2_w, w_ffn_in=w_ffn_in, w_ffn_out=w_ffn_out, w_ada_f=w_ada_f,
                   b_ada_f=b_ada_f, normf_w=normf_w)
    m_in = [m_w_ada, m_b_ada, m_norm1_w, m_w_in, m_gdn_conv_w, m_gdn_a_log, m_gdn_dt_bias, m_gdn_norm_w, m_w_gdn_proj,
            m_sc_conv_w, m_w_sc_out, m_w_o, m_norm2_w, m_w_ffn_in, m_w_ffn_out, m_w_ada_f, m_b_ada_f, m_normf_w]
    v_in = [v_w_ada, v_b_ada, v_norm1_w, v_w_in, v_gdn_conv_w, v_gdn_a_log, v_gdn_dt_bias, v_gdn_norm_w, v_w_gdn_proj,
            v_sc_conv_w, v_w_sc_out, v_w_o, v_norm2_w, v_w_ffn_in, v_w_ffn_out, v_w_ada_f, v_b_ada_f, v_normf_w]
    deltas, new_m, new_v = [], [], []
    grads_t = {"w_in": gt_w_in, "w_ffn_in": gt_w_ffn_in}
    for (wname, wt), mt, vt in zip(weights.items(), m_in, v_in):
        if wname in grads_t:
            back = lambda a, wt=wt: a.T.reshape(wt.shape)
            dl, mn, vn = (back(a) for a in _adamw(wt[0].T, grads_t[wname], mt[0].T, vt[0].T, "adamw_" + wname))
        else:
            dl, mn, vn = _adamw(wt, grads[wname], mt, vt, "adamw_" + wname)
        deltas.append(dl)
        new_m.append(mn)
        new_v.append(vn)
    loss = t_loss[0, 0]
    return (loss, grad_x, *[grads[k] for k in weights], *deltas, *new_m, *new_v)
```

```python
import functools

import jax
import jax.numpy as jnp
from jax import lax
from jax.experimental import pallas as pl
from jax.experimental.pallas import tpu as pltpu
from jax.experimental.pallas import tpu_sc as plsc

F32 = jnp.float32
MXU_DTYPE = jnp.bfloat16
NDEV = 8
CHUNK = 64
HEAD = 128
LANE = 128
EPS = 1e-6
ADAM_LR, ADAM_B1, ADAM_B2, ADAM_EPS, ADAM_WD, ADAM_STEP = 0.001, 0.9, 0.999, 1e-08, 0.01, 10
VMEM_LIMIT = 48 * 1024 * 1024
MESH_IDS = pl.DeviceIdType.MESH
HIGHEST = lax.Precision.HIGHEST


def _tile(n, cands=(512, 256, 128)):
    for c in cands:
        if n % c == 0:
            return c
    return n


def _cparams(*sem):
    return pltpu.CompilerParams(dimension_semantics=sem, vmem_limit_bytes=VMEM_LIMIT)


def _mm(a, b, *, ta=False, tb=False, add=None, out_dtype=F32, name, b_rows=None, out_rows=None, row_off=0, into=None,
        a_index=None):
    m, k = (a.shape[-1], a.shape[-2]) if ta else a.shape[-2:]
    b_shape = b.shape if b_rows is None else (b_rows[1], b.shape[1])
    n = b_shape[0] if tb else b_shape[1]
    assert k == (b_shape[1] if tb else b_shape[0])
    if ta:
        tm, tn = _tile(m), n if n <= 1024 else _tile(n)
        tk = k if k <= 4096 else _tile(k, (4096, 2048, 1024, 512))
        if tm * tk > 1024 * 2048:
            tk = _tile(k, (2048, 1024, 512))
    else:
        tk = k if k <= 1024 else _tile(k, (1024, 512))
        tn = _tile(n, (1024 if tk <= 1024 else 512, 512, 256, 128))
        tm = _tile(m, (2048 if (tn <= 512 and tk <= 1024) else 1024, 1024, 512, 256, 128))
    nk = k // tk
    dims = (((0 if ta else 1,), (1 if tb else 0,)), ((), ()))
    has_add = add is not None

    def body(*refs):
        a_ref, b_ref = refs[0], refs[1]
        add_ref = refs[2] if has_add else None
        o_ref = refs[2 + has_add + (into is not None)]
        part = lax.dot_general(a_ref[...].astype(MXU_DTYPE), b_ref[...].astype(MXU_DTYPE), dims,
                               preferred_element_type=F32)

        def finish(acc):
            if has_add:
                acc = acc + add_ref[...]
            o_ref[...] = acc.astype(o_ref.dtype)

        if nk == 1:
            finish(part)
        else:
            acc_ref = refs[-1]
            kk = pl.program_id(2)

            @pl.when(kk == 0)
            def _():
                acc_ref[...] = part

            @pl.when(kk > 0)
            def _():
                acc_ref[...] += part

            @pl.when(kk == nk - 1)
            def _():
                finish(acc_ref[...])

    a_blk, a_at = ((tk, tm), lambda i, j, kk: (kk, i)) if ta else ((tm, tk), lambda i, j, kk: (i, kk))
    a_spec = (pl.BlockSpec(a_blk, a_at) if a_index is None else
              pl.BlockSpec((None,) + a_blk, lambda i, j, kk: (a_index,) + a_at(i, j, kk)))
    if b_rows is None:
        b_spec = pl.BlockSpec((tn, tk), lambda i, j, kk: (j, kk)) if tb else pl.BlockSpec((tk, tn), lambda i, j, kk: (kk, j))
    else:
        at = lambda t: pl.multiple_of(b_rows[0] + t, ROW_ALIGN)
        b_spec = (pl.BlockSpec((pl.Element(tn), pl.Element(tk)), lambda i, j, kk: (at(j * tn), kk * tk)) if tb else
                  pl.BlockSpec((pl.Element(tk), pl.Element(tn)), lambda i, j, kk: (at(kk * tk), j * tn)))
    add_spec = pl.BlockSpec((tm, tn), lambda i, j, kk: (i, j))
    assert row_off % tm == 0
    o_spec = pl.BlockSpec((tm, tn), lambda i, j, kk: (i + row_off // tm, j))
    in_specs = [a_spec, b_spec] + ([add_spec] if has_add else []) + ([pl.BlockSpec(memory_space=pl.ANY)] if into is not None else [])
    args = [a, b] + ([add] if has_add else []) + ([into] if into is not None else [])
    return pl.pallas_call(
        body, name=name, grid=(m // tm, n // tn, nk), in_specs=in_specs, out_specs=o_spec,
        out_shape=jax.ShapeDtypeStruct((out_rows or m, n), out_dtype),
        scratch_shapes=[pltpu.VMEM((tm, tn), F32)] if nk > 1 else [],
        input_output_aliases={len(args) - 1: 0} if into is not None else {},
        compiler_params=_cparams("parallel", "parallel", "arbitrary"),
    )(*args)


RING = 3


def _mm_ring(a, b, b_rows, *, name):
    (m, k), (start, n) = a.shape, b_rows
    assert b.shape[1] == k and b.dtype == MXU_DTYPE
    tm, tn = _tile(m, (1024, 512, 256, 128)), _tile(n, (1024, 512, 256, 128))
    nj = n // tn
    steps = (m // tm) * nj

    def body(a_ref, b_hbm, o_ref, ring, sems):
        step = pl.program_id(0) * nj + pl.program_id(1)

        def tile_copy(t):
            rows = pl.ds(pl.multiple_of(start + (t % nj) * tn, ROW_ALIGN), tn)
            return pltpu.make_async_copy(b_hbm.at[rows, :], ring.at[t % RING], sems.at[t % RING])

        @pl.when(step == 0)
        def _():
            for t in range(min(RING - 1, steps)):
                tile_copy(jnp.int32(t)).start()

        @pl.when(step + RING - 1 < steps)
        def _():
            tile_copy(step + RING - 1).start()

        tile_copy(step).wait()
        o_ref[...] = lax.dot_general(a_ref[...].astype(MXU_DTYPE), ring[step % RING], (((1,), (1,)), ((), ())),
                                     preferred_element_type=F32)

    return pl.pallas_call(
        body, name=name, grid=(m // tm, nj),
        in_specs=[pl.BlockSpec((tm, k), lambda i, j: (i, 0)), pl.BlockSpec(memory_space=pl.ANY)],
        out_specs=pl.BlockSpec((tm, tn), lambda i, j: (i, j)), out_shape=jax.ShapeDtypeStruct((m, n), F32),
        scratch_shapes=[pltpu.VMEM((RING, tn, k), MXU_DTYPE), pltpu.SemaphoreType.DMA((RING,))],
        compiler_params=_cparams("arbitrary", "arbitrary"),
    )(a, b)


def _mm_chain(parts, b, row_of_tile, *, add, name, tk=1024, tm=1024):
    m, n = parts[0].shape[-2], b.shape[1]
    tm = min(tm, m)
    tiles = [p.shape[0] if p.ndim == 3 else p.shape[1] // tk for p in parts]
    first = [sum(tiles[:s]) for s in range(len(parts))]
    nk = sum(tiles)

    def body(*refs):
        a_refs, b_ref, add_ref, o_ref, acc_ref = refs[:len(parts)], *refs[len(parts):]
        kk = pl.program_id(1)

        @pl.when(kk == 0)
        def _():
            acc_ref[...] = add_ref[...]

        for a_ref, lo, cnt in zip(a_refs, first, tiles):
            @pl.when(jnp.logical_and(kk >= lo, kk < lo + cnt))
            def _(a_ref=a_ref):
                acc_ref[...] += lax.dot_general(a_ref[...].astype(MXU_DTYPE), b_ref[...].astype(MXU_DTYPE),
                                                (((1,), (0,)), ((), ())), preferred_element_type=F32)

        @pl.when(kk == nk - 1)
        def _():
            o_ref[...] = acc_ref[...]

    tile_of = lambda kk, lo, cnt: jnp.clip(kk - lo, 0, cnt - 1)
    a_specs = [pl.BlockSpec((None, tm, tk), functools.partial(lambda i, kk, lo, cnt: (tile_of(kk, lo, cnt), i, 0), lo=lo, cnt=cnt))
               if p.ndim == 3 else
               pl.BlockSpec((tm, tk), functools.partial(lambda i, kk, lo, cnt: (i, tile_of(kk, lo, cnt)), lo=lo, cnt=cnt))
               for p, lo, cnt in zip(parts, first, tiles)]
    b_spec = pl.BlockSpec((pl.Element(tk), pl.Element(n)), lambda i, kk: (pl.multiple_of(row_of_tile(kk), ROW_ALIGN), 0))
    o_spec = pl.BlockSpec((tm, n), lambda i, kk: (i, 0))
    return pl.pallas_call(
        body, name=name, grid=(m // tm, nk), in_specs=a_specs + [b_spec, o_spec], out_specs=o_spec,
        out_shape=jax.ShapeDtypeStruct((m, n), F32), scratch_shapes=[pltpu.VMEM((tm, n), F32)],
        compiler_params=_cparams("parallel", "arbitrary"),
    )(*parts, b, add)


def _swiglu_tiles(m, half):
    tn = _tile(half, (512, 256, 128))
    return _tile(m, (2048 if tn <= 256 else 1024, 1024, 512, 256, 128)), tn


def _ffn_in_swiglu(h, wt, half, name):
    m, k = h.shape
    tm, tn = _swiglu_tiles(m, half)
    nj = half // tn
    dims = (((1,), (1,)), ((), ()))

    def body(h_ref, wa_ref, wb_ref, act_ref, a_ref, b_ref):
        lhs = h_ref[...].astype(MXU_DTYPE)
        a = lax.dot_general(lhs, wa_ref[...].astype(MXU_DTYPE), dims, preferred_element_type=F32)
        b = lax.dot_general(lhs, wb_ref[...].astype(MXU_DTYPE), dims, preferred_element_type=F32)
        act_ref[...] = (_silu(a) * b).astype(act_ref.dtype)
        a_ref[...] = a.astype(a_ref.dtype)
        b_ref[...] = b.astype(b_ref.dtype)

    out = jax.ShapeDtypeStruct((m, half), MXU_DTYPE)
    oblk = pl.BlockSpec((tm, tn), lambda i, j: (i, j))
    return pl.pallas_call(
        body, name=name, grid=(m // tm, nj),
        in_specs=[pl.BlockSpec((tm, k), lambda i, j: (i, 0)), pl.BlockSpec((tn, k), lambda i, j: (j, 0)),
                  pl.BlockSpec((tn, k), lambda i, j: (j + nj, 0))],
        out_specs=[oblk, oblk, oblk], out_shape=[out, out, out], compiler_params=_cparams("parallel", "parallel"),
    )(h, wt, wt)


def _ffn_out_bwd_swiglu(dff, w, a, b, name):
    m, k = dff.shape
    half = w.shape[0]
    tm, tn = _swiglu_tiles(m, half)

    def body(d_ref, w_ref, a_ref, b_ref, da_ref, db_ref):
        dact = lax.dot_general(d_ref[...].astype(MXU_DTYPE), w_ref[...].astype(MXU_DTYPE), (((1,), (1,)), ((), ())),
                               preferred_element_type=F32)
        av, bv = a_ref[...].astype(F32), b_ref[...].astype(F32)
        sig = jax.nn.sigmoid(av)
        da_ref[...] = (dact * bv * (sig * (1.0 + av * (1.0 - sig)))).astype(da_ref.dtype)
        db_ref[...] = (dact * (av * sig)).astype(db_ref.dtype)

    out = jax.ShapeDtypeStruct((m, half), MXU_DTYPE)
    oblk = pl.BlockSpec((tm, tn), lambda i, j: (i, j))
    return pl.pallas_call(
        body, name=name, grid=(m // tm, half // tn),
        in_specs=[pl.BlockSpec((tm, k), lambda i, j: (i, 0)), pl.BlockSpec((tn, k), lambda i, j: (j, 0)), oblk, oblk],
        out_specs=[oblk, oblk], out_shape=[out, out], compiler_params=_cparams("parallel", "parallel"),
    )(dff, w, a, b)


def _with_off(xs):
    return [x if isinstance(x, tuple) else (x, 0) for x in xs]


def _spec(kind, arr, off, ts, wb):
    w = arr.shape[-1] if wb is None else wb
    col = (lambda j: 0) if wb is None else functools.partial(lambda j, o: o + j, o=off)
    if kind == "tok":
        return pl.BlockSpec((None, ts, w), lambda j, b, i: (b, i, col(j)))
    if kind == "bat":
        return pl.BlockSpec((None, 1, w), lambda j, b, i: (b, 0, col(j)))
    if off is None:
        return pl.BlockSpec(arr.shape, lambda j, b, i: (0, 0))
    return pl.BlockSpec((arr.shape[0], w), lambda j, b, i: (0, col(j)))


class _Product:
    def __init__(self, a, b, *, tb=False, b_rows=None, add=None):
        self.a, self.b, self.tb, self.b_rows, self.add = a, b, tb, b_rows, add
        rows = b.shape[0] if b_rows is None else b_rows[1]
        self.shape = a.shape[:2] + (rows if tb else b.shape[1],)

    def inputs(self, ts):
        a_spec = pl.BlockSpec((None, ts, self.a.shape[2]), lambda j, b, i: (b, i, 0))
        if self.b_rows is None:
            b_spec = pl.BlockSpec(self.b.shape, lambda j, b, i: (0, 0))
        else:
            start, count = self.b_rows
            b_spec = pl.BlockSpec((pl.Element(count), pl.Element(self.b.shape[1])), lambda j, b, i: (start, 0))
        if isinstance(self.add, _Product):
            extra = self.add.inputs(ts)
        else:
            extra = [] if self.add is None else [(self.add, pl.BlockSpec((None, ts, self.shape[2]), lambda j, b, i: (b, i, 0)))]
        return [(self.a, a_spec), (self.b, b_spec)] + extra

    def value(self, refs):
        dims = (((1,), (1 if self.tb else 0,)), ((), ()))
        val = lax.dot_general(refs[0][...].astype(MXU_DTYPE), refs[1][...].astype(MXU_DTYPE), dims, preferred_element_type=F32)
        if isinstance(self.add, _Product):
            return val + self.add.value(refs[2:])
        return val if self.add is None else val + refs[2][...].astype(F32)


def _inputs(groups, kinds, ts, wb):
    loaded = [(a, _spec(kind, a, o, ts, wb)) for g, kind in zip(groups, kinds) for a, o in g if not isinstance(a, _Product)]
    made = [pair for g in groups for a, _ in g if isinstance(a, _Product) for pair in a.inputs(ts)]
    return [a for a, _ in loaded + made], [sp for _, sp in loaded + made]


def _values(refs, groups):
    n_loaded = sum(1 for g in groups for a, _ in g if not isinstance(a, _Product))
    loaded, pos, out = iter(refs[:n_loaded]), n_loaded, []
    for g in groups:
        vals = []
        for a, _ in g:
            if isinstance(a, _Product):
                k = len(a.inputs(1))
                vals.append(a.value(refs[pos:pos + k]))
                pos += k
            else:
                vals.append(next(loaded)[...].astype(F32))
        out.append(vals)
    return out, pos


def _tok_fwd(fn, toks, bats, pars, outs, *, name, ts, wb=None, cols=1):
    groups = [_with_off(toks), _with_off(bats), _with_off(pars)]
    bl, s, _ = groups[0][0][0].shape
    ts = min(ts, s)
    args, in_specs = _inputs(groups, ("tok", "bat", "par"), ts, wb)

    def body(*refs):
        vals, n_in = _values(refs, groups)
        res = fn(*[v for g in vals for v in g])
        for r, val in zip(refs[n_in:], res):
            r[...] = val.astype(r.dtype)

    out_specs = [pl.BlockSpec((None, ts, w if wb is None else wb), lambda j, b, i: (b, i, j)) for w, _ in outs]
    return pl.pallas_call(
        body, name=name, grid=(cols, bl, s // ts), in_specs=in_specs,
        out_specs=out_specs, out_shape=[jax.ShapeDtypeStruct((bl, s, w), dt) for w, dt in outs],
        compiler_params=_cparams("parallel", "parallel", "parallel"),
    )(*args)


def _accumulate(ref, val, first):
    @pl.when(first)
    def _():
        ref[...] = val

    @pl.when(jnp.logical_not(first))
    def _():
        ref[...] += val


def _tok_bwd(fn, toks, bats, pars, cots, need, *, name, ts, wb=None, cols=1, tok_dtype=F32, loss=False, after=()):
    toks, bats, pars, cots = _with_off(toks), _with_off(bats), _with_off(pars), _with_off(cots)
    groups = [toks, bats, pars, cots]
    bl, s, _ = toks[0][0].shape
    ts = min(ts, s)
    nt, nb, npar = len(toks), len(bats), len(pars)
    args, in_specs = _inputs(groups, ("tok", "bat", "par", "tok"), ts, wb)
    args, in_specs = args + list(after), in_specs + [pl.BlockSpec(memory_space=pl.ANY)] * len(after)

    def body(*refs):
        j, b, i = pl.program_id(0), pl.program_id(1), pl.program_id(2)
        (tok_vals, bat_vals, par_vals, cot_vals), o = _values(refs, groups)
        o += len(after)
        outs, vjp = jax.vjp(fn, *tok_vals, *bat_vals, *par_vals)
        if loss:
            ct = (jnp.ones_like(outs[0]),)
            tot = jnp.broadcast_to(jnp.sum(outs[0], keepdims=True), (1, LANE))
            _accumulate(refs[o], tot, jnp.logical_and(b == 0, i == 0))
            o += 1
        else:
            ct = tuple(cot_vals)
        grads = vjp(ct)
        for t in range(nt):
            if need[t]:
                refs[o][...] = grads[t].astype(refs[o].dtype)
                o += 1
        for t in range(nb):
            _accumulate(refs[o], grads[nt + t], i == 0)
            o += 1
        for t in range(npar):
            first = jnp.logical_and(b == 0, i == 0)
            if pars[t][1] is None:
                first = jnp.logical_and(first, j == 0)
            _accumulate(refs[o], grads[nt + nb + t], first)
            o += 1

    full = lambda arr: arr.shape[-1] if wb is None else wb * cols
    blk = lambda arr: arr.shape[-1] if wb is None else wb
    out_specs, out_shape = [], []
    if loss:
        out_specs.append(pl.BlockSpec((1, LANE), lambda j, b, i: (0, 0)))
        out_shape.append(jax.ShapeDtypeStruct((1, LANE), F32))
    for t in range(nt):
        if need[t]:
            out_specs.append(pl.BlockSpec((None, ts, blk(toks[t][0])), lambda j, b, i: (b, i, j)))
            dt = tok_dtype[t] if isinstance(tok_dtype, (list, tuple)) else tok_dtype
            out_shape.append(jax.ShapeDtypeStruct((bl, s, full(toks[t][0])), dt))
    for arr, _ in bats:
        out_specs.append(pl.BlockSpec((None, 1, blk(arr)), lambda j, b, i: (b, 0, j)))
        out_shape.append(jax.ShapeDtypeStruct((bl, 1, full(arr)), F32))
    for arr, off in pars:
        if off is None:
            out_specs.append(pl.BlockSpec(arr.shape, lambda j, b, i: (0, 0)))
            out_shape.append(jax.ShapeDtypeStruct(arr.shape, F32))
        else:
            out_specs.append(pl.BlockSpec((arr.shape[0], blk(arr)), lambda j, b, i: (0, j)))
            out_shape.append(jax.ShapeDtypeStruct((arr.shape[0], full(arr)), F32))
    res = list(pl.pallas_call(
        body, name=name, grid=(cols, bl, s // ts), in_specs=in_specs,
        out_specs=out_specs, out_shape=out_shape, compiler_params=_cparams("arbitrary", "arbitrary", "arbitrary"),
    )(*args))
    tot = res.pop(0) if loss else None
    dtoks = [res.pop(0) if need[t] else None for t in range(nt)]
    dbats = [res.pop(0) for _ in range(nb)]
    dpars = [res.pop(0) for _ in range(npar)]
    return (tot, dtoks, dbats, dpars) if loss else (dtoks, dbats, dpars)


def _silu(x):
    return x * jax.nn.sigmoid(x)


def _rms(x, w):
    return x * lax.rsqrt(jnp.mean(x * x, axis=-1, keepdims=True) + EPS) * w


def _f_norm_mod(x, shift, scale, w):
    return (_rms(x, w) * (1.0 + scale) + shift,)


def _f_norm_mod_skip(x, shift, scale, w):
    return _rms(x, w) * (1.0 + scale) + shift, x


def _f_res_norm_mod(x, mix, gate, shift, scale, w):
    x2 = x + gate * mix
    return x2, _rms(x2, w) * (1.0 + scale) + shift


def _f_res_norm_mod_keep(x, mix, gate, shift, scale, w):
    return (*_f_res_norm_mod(x, mix, gate, shift, scale, w), mix)


def _f_gates(p, a_log, dt_bias, *, heads):
    z = p + dt_bias
    g = -jnp.exp(a_log) * (jnp.maximum(z, 0.0) + jnp.log1p(jnp.exp(jnp.minimum(z, -z))))
    lane = lax.broadcasted_iota(jnp.int32, p.shape, 1)
    return (jnp.where(lane < heads, g, jax.nn.sigmoid(p)),)


def _f_gdn_out(o, z, w):
    return (_rms(o, w) * _silu(z),)


def _f_merge(ga, gb, ya, yb):
    return (jax.nn.sigmoid(ga) * ya + jax.nn.sigmoid(gb) * yb,)


def _f_merge_keep(ga, gb, ya, yb):
    return (*_f_merge(ga, gb, ya, yb), yb)


def _f_loss(x2, ff, tgt, gate, shift, scale, w):
    y = _rms(x2 + gate * ff, w) * (1.0 + scale) + shift
    return (0.5 * jnp.mean(jnp.square(y - tgt), axis=-1, keepdims=True),)


def _shift_down(x, s):
    if s == 0:
        return x
    row = lax.broadcasted_iota(jnp.int32, x.shape, 0)
    return jnp.where(row >= s, pltpu.roll(x, s, 0), 0.0)


def _shift_up(x, s):
    if s == 0:
        return x
    n = x.shape[0]
    row = lax.broadcasted_iota(jnp.int32, x.shape, 0)
    return jnp.where(row < n - s, pltpu.roll(x, n - s, 0), 0.0)


def _conv(x, w):
    width = w.shape[0]
    acc = w[width - 1:width, :] * x
    for j in range(width - 1):
        acc = acc + w[j:j + 1, :] * _shift_down(x, width - 1 - j)
    return acc


def _conv_bwd(dy, x, w, dw_ref, first):
    width = w.shape[0]
    dx = w[width - 1:width, :] * dy
    for j in range(width - 1):
        dx = dx + w[j:j + 1, :] * _shift_up(dy, width - 1 - j)
    for j in range(width):
        row = jnp.sum(dy * _shift_down(x, width - 1 - j), axis=0, keepdims=True)
        _accumulate(dw_ref.at[j:j + 1, :], row, first)
    return dx


def _qkv_act(xc, is_v, scale):
    a = _silu(xc)
    nrm = a * lax.rsqrt(jnp.sum(a * a, axis=-1, keepdims=True) + EPS) * scale
    return jnp.where(is_v, a, nrm)


def _qkv_act_bwd(xc, dout, is_v, scale):
    sig = jax.nn.sigmoid(xc)
    a = xc * sig
    r = lax.rsqrt(jnp.sum(a * a, axis=-1, keepdims=True) + EPS)
    c1 = r * scale
    da = c1 * dout - a * (c1 * r * r * jnp.sum(dout * a, axis=-1, keepdims=True))
    return jnp.where(is_v, dout, da) * (sig * (1.0 + xc * (1.0 - sig)))


def _qkv_consts(j, heads):
    is_v = j >= 2 * heads
    scale = jnp.where(j < heads, HEAD ** -0.5, 1.0).astype(F32)
    return is_v, scale


def _qkv_fwd(p, w, heads, name):
    bl, s, w3 = p.shape

    def body(p_ref, w_ref, o_ref):
        is_v, scale = _qkv_consts(pl.program_id(0), heads)
        o_ref[...] = _qkv_act(_conv(p_ref[...], w_ref[...]), is_v, scale)

    blk = pl.BlockSpec((None, s, HEAD), lambda j, b: (b, 0, j))
    return pl.pallas_call(
        body, name=name, grid=(w3 // HEAD, bl), in_specs=[blk, pl.BlockSpec((w.shape[0], HEAD), lambda j, b: (0, j))],
        out_specs=blk, out_shape=jax.ShapeDtypeStruct(p.shape, F32), compiler_params=_cparams("parallel", "parallel"),
    )(p, w)


def _qkv_bwd(p, w, dout, heads, name):
    bl, s, w3 = p.shape

    def body(p_ref, w_ref, d_ref, dp_ref, dw_ref):
        is_v, scale = _qkv_consts(pl.program_id(0), heads)
        x, wv = p_ref[...], w_ref[...]
        dxc = _qkv_act_bwd(_conv(x, wv), d_ref[...], is_v, scale)
        dp_ref[...] = _conv_bwd(dxc, x, wv, dw_ref, pl.program_id(1) == 0).astype(dp_ref.dtype)

    blk = pl.BlockSpec((None, s, HEAD), lambda j, b: (b, 0, j))
    wblk = pl.BlockSpec((w.shape[0], HEAD), lambda j, b: (0, j))
    return pl.pallas_call(
        body, name=name, grid=(w3 // HEAD, bl), in_specs=[blk, wblk, blk], out_specs=[blk, wblk],
        out_shape=[jax.ShapeDtypeStruct(p.shape, MXU_DTYPE), jax.ShapeDtypeStruct(w.shape, F32)],
        compiler_params=_cparams("arbitrary", "arbitrary"),
    )(p, w, dout)


def _sc_specs(p, w):
    bl, s, w3 = p.shape
    nblk = w3 // 3 // LANE
    sec = lambda k: pl.BlockSpec((None, s, LANE), functools.partial(lambda j, b, k: (b, 0, k * nblk + j), k=k))
    return nblk, [sec(0), sec(1), sec(2)], pl.BlockSpec((w.shape[0], LANE), lambda j, b: (0, j)), \
        pl.BlockSpec((None, s, LANE), lambda j, b: (b, 0, j))


def _sc_fwd(p, w, name):
    bl, s, w3 = p.shape
    nblk, secs, wblk, oblk = _sc_specs(p, w)

    def body(b_ref, c_ref, x_ref, w_ref, o_ref):
        o_ref[...] = (b_ref[...] * _conv(c_ref[...] * x_ref[...], w_ref[...])).astype(o_ref.dtype)

    return pl.pallas_call(
        body, name=name, grid=(nblk, bl), in_specs=secs + [wblk], out_specs=oblk,
        out_shape=jax.ShapeDtypeStruct((bl, s, w3 // 3), MXU_DTYPE), compiler_params=_cparams("parallel", "parallel"),
    )(p, p, p, w)


def _sc_bwd(p, w, dout, name):
    bl, s, w3 = p.shape
    nblk, secs, wblk, oblk = _sc_specs(p, w)

    def body(b_ref, c_ref, x_ref, w_ref, d_ref, dp_ref, dw_ref):
        gb, gc, xin, wv, d = b_ref[...], c_ref[...], x_ref[...], w_ref[...], d_ref[...]
        u = gc * xin
        dp_ref[0] = (d * _conv(u, wv)).astype(dp_ref.dtype)
        du = _conv_bwd(d * gb, u, wv, dw_ref, pl.program_id(1) == 0)
        dp_ref[1] = (du * xin).astype(dp_ref.dtype)
        dp_ref[2] = (du * gc).astype(dp_ref.dtype)

    return pl.pallas_call(
        body, name=name, grid=(nblk, bl), in_specs=secs + [wblk, oblk],
        out_specs=[pl.BlockSpec((3, None, s, LANE), lambda j, b: (0, b, 0, j)), wblk],
        out_shape=[jax.ShapeDtypeStruct((3, bl, s, w3 // 3), MXU_DTYPE), jax.ShapeDtypeStruct(w.shape, F32)],
        compiler_params=_cparams("arbitrary", "arbitrary"),
    )(p, p, p, w, dout)


def _bdot(a, b, ca, cb):
    return lax.dot_general(a.astype(MXU_DTYPE), b.astype(MXU_DTYPE), (((ca,), (cb,)), ((), ())),
                           preferred_element_type=F32)


def _hdot(a, b):
    return lax.dot_general(a, b, (((1,), (0,)), ((), ())), precision=HIGHEST, preferred_element_type=F32)


def _lane_col(x, idx):
    lane = lax.broadcasted_iota(jnp.int32, x.shape, 1)
    return jnp.sum(jnp.where(lane == idx, x, 0.0), axis=1, keepdims=True)


def _chunk_masks():
    r = lax.broadcasted_iota(jnp.int32, (CHUNK, CHUNK), 0)
    c = lax.broadcasted_iota(jnp.int32, (CHUNK, CHUNK), 1)
    return r == c, r >= c, r > c


def _dot3(a, b):
    ah, bh = a.astype(MXU_DTYPE), b.astype(MXU_DTYPE)
    al, bl = (a - ah.astype(F32)).astype(MXU_DTYPE), (b - bh.astype(F32)).astype(MXU_DTYPE)
    dot = lambda x, y: lax.dot_general(x, y, (((1,), (0,)), ((), ())), preferred_element_type=F32)
    return dot(ah, bh) + (dot(ah, bl) + dot(al, bh))


def _tri_inv_steps(low, eye):
    x = -low
    p = jnp.where(eye, 1.0, 0.0) + x
    span = 2
    while span < CHUNK:
        x = _dot3(x, x)
        yield
        p = p + _dot3(p, x)
        yield
        span *= 2
    return p


def _round_robin(gens):
    out, live = [None] * len(gens), list(range(len(gens)))
    while live:
        still = []
        for i in live:
            try:
                next(gens[i])
                still.append(i)
            except StopIteration as stop:
                out[i] = stop.value
        live = still
    return out


def _gdn_pre(q, k, v, gc, beta, masks):
    eye, causal, strict = masks
    gc_row = jnp.sum(jnp.where(eye, gc, 0.0), axis=0, keepdims=True)
    decay = jnp.where(causal, jnp.exp(jnp.where(causal, gc - gc_row, 0.0)), 0.0)
    eg = jnp.exp(gc)
    gl = gc[CHUNK - 1:CHUNK, :]
    kb, vb = k * beta, v * beta
    both = _bdot(jnp.concatenate([kb, q], axis=0), k, 1, 1)
    low = jnp.where(strict, both[:CHUNK] * decay, 0.0)
    qk = jnp.where(causal, both[CHUNK:] * decay, 0.0)
    rest = jnp.exp(gl - gc)
    return dict(decay=decay, eg=eg, gl=gl, kb=kb, vb=vb, kbe=kb * eg, low=low, qk=qk, qg=q * eg, rest=rest, kdec=k * rest)


GROUP = 4


def _gdn_specs(qkv, gbeta, heads, rev):
    bl, s, w3 = qkv.shape
    d, n = w3 // 3, s // CHUNK
    group = GROUP if n % GROUP == 0 else 1
    steps = n // group
    at = (lambda c: steps - 1 - c) if rev else (lambda c: c)
    assert d == heads * HEAD
    rows = group * CHUNK
    sec = pl.BlockSpec((None, rows, w3), lambda b, c: (b, at(c), 0))
    gspec = pl.BlockSpec((None, rows, LANE), lambda b, c: (b, at(c), 0))
    ospec = pl.BlockSpec((None, rows, d), lambda b, c: (b, at(c), 0))
    sspec = pl.BlockSpec((None, group, heads, HEAD, HEAD), lambda b, c: (b, at(c), 0, 0, 0))
    tspec = pl.BlockSpec((None, group, heads, CHUNK, CHUNK), lambda b, c: (b, at(c), 0, 0, 0))
    return bl, s, d, n, group, sec, gspec, ospec, sspec, tspec


def _gdn_fwd(qkv, gbeta, heads, name):
    bl, s, d, n, group, sec, gspec, ospec, sspec, tspec = _gdn_specs(qkv, gbeta, heads, False)
    rows = lambda sub: slice(sub * CHUNK, (sub + 1) * CHUNK)
    pairs = [(h, sub) for h in range(heads) for sub in range(group)]

    def body(x_ref, g_ref, o_ref, s_ref, t_ref, st_ref):
        @pl.when(pl.program_id(1) == 0)
        def _():
            st_ref[...] = jnp.zeros_like(st_ref)

        masks = _chunk_masks()
        eye, causal, _ = masks
        gblks = [g_ref[rows(sub), :] for sub in range(group)]
        gcs = [_hdot(jnp.where(causal, 1.0, 0.0), gb) for gb in gblks]
        st_all = st_ref[...]

        def free(h, sub):
            q, k, v = (x_ref[rows(sub), sec * d + h * HEAD:sec * d + (h + 1) * HEAD] for sec in range(3))
            pre = _gdn_pre(q, k, v, _lane_col(gcs[sub], h), _lane_col(gblks[sub], heads + h), masks)
            yield
            t = yield from _tri_inv_steps(pre["low"], eye)
            uw = _bdot(t, jnp.concatenate([pre["vb"], pre["kbe"]], axis=1), 1, 0)
            return pre, t, uw[:, :HEAD], uw[:, HEAD:]

        pieces = dict(zip(pairs, _round_robin([free(h, sub) for h, sub in pairs])))

        def carry(h):
            st, outs, starts = st_all[h], [], []
            for sub in range(group):
                pre, _, u, w = pieces[h, sub]
                starts.append(st)
                vnew = u - _bdot(w, st, 1, 0)
                yield
                outs.append(_bdot(pre["qg"], st, 1, 0) + _bdot(pre["qk"], vnew, 1, 0))
                st = st * jnp.exp(pre["gl"]) + _bdot(pre["kdec"], vnew, 0, 0)
                yield
            return outs, starts, st

        carried = _round_robin([carry(h) for h in range(heads)])
        per_sub = lambda pick: [[pick(h, sub) for h in range(heads)] for sub in range(group)]
        o_ref[...] = jnp.concatenate([jnp.concatenate(r, axis=1) for r in per_sub(lambda h, sub: carried[h][0][sub])], axis=0)
        s_ref[...] = jnp.stack([jnp.stack(r) for r in per_sub(lambda h, sub: carried[h][1][sub])])
        t_ref[...] = jnp.stack([jnp.stack(r) for r in per_sub(lambda h, sub: pieces[h, sub][1])])
        st_ref[...] = jnp.stack([carried[h][2] for h in range(heads)])

    return pl.pallas_call(
        body, name=name, grid=(bl, n // group), in_specs=[sec, gspec], out_specs=[ospec, sspec, tspec],
        out_shape=[jax.ShapeDtypeStruct((bl, s, d), F32), jax.ShapeDtypeStruct((bl, n, heads, HEAD, HEAD), F32),
                   jax.ShapeDtypeStruct((bl, n, heads, CHUNK, CHUNK), F32)],
        scratch_shapes=[pltpu.VMEM((heads, HEAD, HEAD), F32)], compiler_params=_cparams("parallel", "arbitrary"),
    )(qkv, gbeta)


def _gdn_bwd(qkv, gbeta, dout, s_all, t_all, heads, name):
    bl, s, d, n, group, sec, gspec, ospec, sspec, tspec = _gdn_specs(qkv, gbeta, heads, True)
    rows = lambda sub: slice(sub * CHUNK, (sub + 1) * CHUNK)
    pairs = [(h, sub) for h in range(heads) for sub in range(group)]
    stack, side = functools.partial(jnp.concatenate, axis=0), functools.partial(jnp.concatenate, axis=1)

    def body(x_ref, g_ref, do_ref, s_ref, t_ref, dx_ref, dg_ref, ds_ref):
        @pl.when(pl.program_id(1) == 0)
        def _():
            ds_ref[...] = jnp.zeros_like(ds_ref)

        masks = _chunk_masks()
        eye, causal, strict = masks
        gblks = [g_ref[rows(sub), :] for sub in range(group)]
        gcs = [_hdot(jnp.where(causal, 1.0, 0.0), gb) for gb in gblks]
        lane = lax.broadcasted_iota(jnp.int32, (CHUNK, LANE), 1)
        last_row = lax.broadcasted_iota(jnp.int32, (CHUNK, 1), 0) == CHUNK - 1
        rowsum = lambda a: jnp.sum(a, axis=1, keepdims=True)
        st_all, t_all_, ds_all = s_ref[...], t_ref[...], ds_ref[...]

        def free(h, sub):
            q, k, v = (x_ref[rows(sub), sec * d + h * HEAD:sec * d + (h + 1) * HEAD] for sec in range(3))
            do = do_ref[rows(sub), h * HEAD:(h + 1) * HEAD]
            beta = _lane_col(gblks[sub], heads + h)
            st, t = st_all[sub, h], t_all_[sub, h]
            pre = _gdn_pre(q, k, v, _lane_col(gcs[sub], h), beta, masks)
            yield
            uw = _bdot(t, side([pre["vb"], pre["kbe"]]), 1, 0)
            u, w = uw[:, :HEAD], uw[:, HEAD:]
            yield
            vnew = u - _bdot(w, st, 1, 0)
            yield
            dqk = jnp.where(causal, _bdot(do, vnew, 1, 1), 0.0)
            dqg = _bdot(do, st, 1, 1)
            return dict(q=q, k=k, v=v, do=do, beta=beta, st=st, t=t, pre=pre, w=w, vnew=vnew, dqk=dqk, dqg=dqg)

        pieces = dict(zip(pairs, _round_robin([free(h, sub) for h, sub in pairs])))

        def carry(h):
            dsn, outs = ds_all[h], {}
            for sub in reversed(range(group)):
                pc = pieces[h, sub]
                pre, st, do = pc["pre"], pc["st"], pc["do"]
                egl = jnp.exp(pre["gl"])
                dkdec = _bdot(pc["vnew"], dsn, 1, 1)
                dvnew = _bdot(pre["kdec"], dsn, 1, 0) + _bdot(pre["qk"], do, 0, 0)
                dgl = jnp.sum(dsn * st, keepdims=True) * egl
                yield
                dw = -_bdot(dvnew, st, 1, 1)
                dsn = dsn * egl + _bdot(stack([pre["qg"], -pc["w"]]), stack([do, dvnew]), 0, 0)
                outs[sub] = (dkdec, dvnew, dgl, dw)
                yield
            return outs, dsn

        carried = _round_robin([carry(h) for h in range(heads)])

        def rest(h, sub):
            pc = pieces[h, sub]
            dkdec, dvnew, dgl, dw = carried[h][0][sub]
            q, k, v, beta, t, pre, dqk, dqg = (pc[x] for x in ("q", "k", "v", "beta", "t", "pre", "dqk", "dqg"))
            decay, eg, kb, vb, kbe, low, qk, qg, kdec = (pre[x] for x in ("decay", "eg", "kb", "vb", "kbe", "low", "qk", "qg", "kdec"))
            dt = _bdot(side([dvnew, dw]), side([vb, kbe]), 1, 1)
            by_t = _bdot(t, side([dvnew, dw]), 0, 0)
            dvb, dkbe = by_t[:, :HEAD], by_t[:, HEAD:]
            yield
            inner = _bdot(dt, t, 1, 1)
            yield
            dlow = -jnp.where(strict, _bdot(t, inner, 0, 0), 0.0)
            da, db = dlow * decay, dqk * decay
            yield
            m = dlow * low + dqk * qk
            kdk = dkdec * kdec
            col_of_m = jnp.sum(jnp.where(eye, jnp.sum(m, axis=0, keepdims=True), 0.0), axis=1, keepdims=True)
            dgc = rowsum(m) - col_of_m + rowsum(dqg * qg) + rowsum(dkbe * kbe) - rowsum(kdk)
            dgc = dgc + jnp.where(last_row, dgl + jnp.sum(kdk, keepdims=True), 0.0)
            by_k = _bdot(stack([da, db]), k, 1, 0)
            dkb = by_k[:CHUNK] + dkbe * eg
            yield
            dk = _bdot(stack([da, db]), stack([kb, q]), 0, 0) + dkdec * pre["rest"] + dkb * beta
            dq = by_k[CHUNK:] + dqg * eg
            dbeta = rowsum(dkb * k) + rowsum(dvb * v)
            return dq, dk, dvb * beta, jnp.where(lane == h, dgc, 0.0) + jnp.where(lane == heads + h, dbeta, 0.0)

        done = dict(zip(pairs, _round_robin([rest(h, sub) for h, sub in pairs])))
        dx_ref[...] = stack([side([done[h, sub][i] for i in range(3) for h in range(heads)]) for sub in range(group)])
        ds_ref[...] = jnp.stack([carried[h][1] for h in range(heads)])
        upper = jnp.where(jnp.logical_or(eye, jnp.logical_not(causal)), 1.0, 0.0)
        dgs = []
        for sub in range(group):
            dgb = done[0, sub][3]
            for h in range(1, heads):
                dgb = dgb + done[h, sub][3]
            dgs.append(jnp.where(lane < heads, _hdot(upper, dgb), dgb))
        dg_ref[...] = stack(dgs)

    return pl.pallas_call(
        body, name=name, grid=(bl, n // group), in_specs=[sec, gspec, ospec, sspec, tspec], out_specs=[sec, gspec],
        out_shape=[jax.ShapeDtypeStruct(qkv.shape, F32), jax.ShapeDtypeStruct((bl, s, LANE), F32)],
        scratch_shapes=[pltpu.VMEM((heads, HEAD, HEAD), F32)], compiler_params=_cparams("parallel", "arbitrary"),
    )(qkv, gbeta, dout, s_all, t_all)


def _position():
    return lax.axis_index("x"), lax.axis_index("y"), lax.axis_index("c")


def _all_gather(x, *, name):
    space = pltpu.VMEM

    def body(x_ref, out_ref, send_sems, recv_sems, local_sem):
        ax, ay, ac = _position()
        me, sibling = (ax, ay, ac), (ax, ay, 1 - ac)
        chips = [(1 - ax, ay), (ax, 1 - ay), (1 - ax, 1 - ay)]

        def slot(px, py, pc):
            return out_ref.at[4 * px + 2 * py + pc]

        def copy(k, block, to, src=None):
            return pltpu.make_async_remote_copy(
                src_ref=slot(*block) if src is None else src, dst_ref=slot(*block), send_sem=send_sems.at[k],
                recv_sem=recv_sems.at[k], device_id=to, device_id_type=MESH_IDS)

        mine = pltpu.make_async_copy(x_ref, slot(*me), local_sem)
        mine.start()
        first = [copy(0, me, sibling, src=x_ref)] + [copy(1 + j, me, (*chip, ac), src=x_ref) for j, chip in enumerate(chips)]
        for cp in first:
            cp.start()
        passed = [copy(4 + j, (*chip, ac), sibling) for j, chip in enumerate(chips)]
        for j, chip in enumerate(chips):
            copy(1 + j, (*chip, ac), me).wait_recv()
            passed[j].start()
        copy(0, sibling, me).wait_recv()
        for j, chip in enumerate(chips):
            copy(4 + j, (*chip, 1 - ac), me).wait_recv()
        for cp in first + passed:
            cp.wait_send()
        mine.wait()

    return pl.pallas_call(
        body, name=name, out_shape=jax.ShapeDtypeStruct((NDEV,) + x.shape, x.dtype),
        in_specs=[pl.BlockSpec(memory_space=space)], out_specs=pl.BlockSpec(memory_space=space),
        scratch_shapes=[pltpu.SemaphoreType.DMA((7,)), pltpu.SemaphoreType.DMA((7,)), pltpu.SemaphoreType.DMA],
    )(x)


class _Rider:
    def __init__(self, arrays, out_shapes, sems, hooks):
        self.arrays, self.out_shapes, self.sems, self.hooks = arrays, out_shapes, sems, hooks


def _gather_rider(xs):
    n = len(xs)

    def hooks(x_refs, out_refs, send_sems, recv_sems):
        ax, ay, ac = _position()
        me, sibling = (ax, ay, ac), (ax, ay, 1 - ac)
        chips = [(1 - ax, ay), (ax, 1 - ay), (1 - ax, 1 - ay)]

        def copies(k, block, to, own=False):
            out = []
            for i in range(n):
                slot = out_refs[i].at[4 * block[0] + 2 * block[1] + block[2]]
                out.append(pltpu.make_async_remote_copy(
                    src_ref=x_refs[i] if own else slot, dst_ref=slot, send_sem=send_sems.at[k, i], recv_sem=recv_sems.at[k, i],
                    device_id=to, device_id_type=MESH_IDS))
            return out

        def first():
            for cp in copies(0, me, sibling, own=True):
                cp.start()
            for j, chip in enumerate(chips):
                for cp in copies(1 + j, me, (*chip, ac), own=True):
                    cp.start()

        def mid():
            for j, chip in enumerate(chips):
                for arrived, onward in zip(copies(1 + j, (*chip, ac), me), copies(4 + j, (*chip, ac), sibling)):
                    arrived.wait_recv()
                    onward.start()

        def last():
            for cp in copies(0, sibling, me):
                cp.wait_recv()
            for j, chip in enumerate(chips):
                for cp in copies(4 + j, (*chip, 1 - ac), me):
                    cp.wait_recv()
            for cp in copies(0, me, sibling, own=True):
                cp.wait_send()
            for j, chip in enumerate(chips):
                for cp in copies(1 + j, me, (*chip, ac), own=True) + copies(4 + j, (*chip, ac), sibling):
                    cp.wait_send()

        return first, mid, last

    return _Rider(list(xs), [jax.ShapeDtypeStruct((NDEV,) + x.shape, x.dtype) for x in xs],
                  [pltpu.SemaphoreType.DMA((7, n)), pltpu.SemaphoreType.DMA((7, n))], hooks)


def _scatter_rider(parts):
    packed = sum(r for _, r in parts)
    width, dtype = parts[0][0].shape[1], parts[0][0].dtype

    def hooks(g_refs, out_refs, send_sems, recv_sems):
        (recv_ref,) = out_refs
        ax, ay, ac = _position()

        def peer(rel):
            flip = lambda a, bit: 1 - a if rel & bit else a
            return flip(ax, 4), flip(ay, 2), flip(ac, 1)

        def first():
            for rel in range(1, NDEV):
                px, py, pc = peer(rel)
                off = 0
                for g_ref, (_, r) in zip(g_refs, parts):
                    rows = g_ref.at[pl.ds(pl.multiple_of((4 * px + 2 * py + pc) * r, ROW_ALIGN), r)]
                    pltpu.make_async_remote_copy(
                        src_ref=rows, dst_ref=recv_ref.at[rel - 1, pl.ds(off, r)], send_sem=send_sems.at[rel - 1],
                        recv_sem=recv_sems.at[rel - 1], device_id=(px, py, pc), device_id_type=MESH_IDS).start()
                    off += r

        def last():
            for rel in range(1, NDEV):
                slot = recv_ref.at[rel - 1]
                pltpu.make_async_remote_copy(src_ref=slot, dst_ref=slot, send_sem=send_sems.at[rel - 1],
                                             recv_sem=recv_sems.at[rel - 1], device_id=peer(rel), device_id_type=MESH_IDS).wait()

        return first, lambda: None, last

    return _Rider([g for g, _ in parts], [jax.ShapeDtypeStruct((NDEV - 1, packed, width), dtype)],
                  [pltpu.SemaphoreType.DMA((NDEV - 1,)), pltpu.SemaphoreType.DMA((NDEV - 1,))], hooks)


def _sum_direct(own, recv, name):
    r, w = own.shape
    tr = max(t for t in range(ROW_ALIGN, 257, ROW_ALIGN) if r % t == 0)

    def body(own_ref, *refs):
        acc = own_ref[...].astype(F32)
        for ref in refs[:-1]:
            acc = acc + ref[...].astype(F32)
        refs[-1][...] = acc

    rblk = lambda k: pl.BlockSpec((None, tr, w), functools.partial(lambda i, k: (k, i, 0), k=k))
    blk = pl.BlockSpec((tr, w), lambda i: (i, 0))
    return pl.pallas_call(body, name=name, grid=(r // tr,), in_specs=[blk] + [rblk(k) for k in range(NDEV - 1)],
                          out_specs=blk, out_shape=jax.ShapeDtypeStruct((r, w), F32),
                          compiler_params=_cparams("parallel"))(own, *([recv] * (NDEV - 1)))


ROW_ALIGN = 16


def _window_start(rows_per_dev, k):
    return rows_per_dev * k // ROW_ALIGN * ROW_ALIGN


def _exchange_in_chip(parts, name, collective_id):
    packed = sum(win for _, _, win, _ in parts)
    width, dtype = parts[0][0].shape[1], parts[0][0].dtype

    def body(g_refs, out_refs, send_sems, recv_sems):
        (recv_ref,) = out_refs
        ax, ay, ac = _position()
        sibling = (ax, ay, 1 - ac)
        _handshake([sibling])
        for q in range(4):
            for g_ref, (_, r, win, off) in zip(g_refs, parts):
                there = g_ref.at[pl.ds(pl.multiple_of(_window_start(r, 2 * q + 1 - ac), ROW_ALIGN), win)]
                pltpu.make_async_remote_copy(src_ref=there, dst_ref=recv_ref.at[q, pl.ds(off, win)], send_sem=send_sems.at[q],
                                             recv_sem=recv_sems.at[q], device_id=sibling, device_id_type=MESH_IDS).start()
        for q in range(4):
            pltpu.make_async_remote_copy(src_ref=recv_ref.at[q], dst_ref=recv_ref.at[q], send_sem=send_sems.at[q],
                                         recv_sem=recv_sems.at[q], device_id=sibling, device_id_type=MESH_IDS).wait()

    return _on_sequencer(body, [g for g, _, _, _ in parts], [jax.ShapeDtypeStruct((4, packed, width), dtype)],
                         [pltpu.SemaphoreType.DMA((4,)), pltpu.SemaphoreType.DMA((4,))], name=name, collective_id=collective_id)[0]


def _on_sequencer(body, ins, out_shapes, sems, *, name, collective_id):
    hbm = pltpu.MemorySpace.HBM
    in_refs = [jax.new_ref(a, memory_space=hbm) for a in ins]
    out_refs = [jax.empty_ref(s, memory_space=hbm) for s in out_shapes]

    @pl.kernel(mesh=plsc.ScalarSubcoreMesh(axis_name="sequencer", num_cores=1), name=name, scratch_types=tuple(sems),
               compiler_params=pltpu.CompilerParams(collective_id=collective_id))
    def launch(*sem_refs):
        body(in_refs, out_refs, *sem_refs)

    launch()
    return [r[...] for r in out_refs]


def _handshake(peers):
    barrier = pltpu.get_barrier_semaphore()
    for peer in peers:
        pl.semaphore_signal(barrier, inc=1, device_id=peer, device_id_type=MESH_IDS)
    pl.semaphore_wait(barrier, len(peers))


def _exchange_chips_async(s1, name, collective_id):
    def body(in_refs, out_refs, send_sems, recv_sems):
        (src,), (got,) = in_refs, out_refs
        ax, ay, ac = _position()
        chips = [(1 - ax, ay), (ax, 1 - ay), (1 - ax, 1 - ay)]
        _handshake([(cx, cy, ac) for cx, cy in chips])
        copies = [pltpu.make_async_remote_copy(
            src_ref=src.at[2 * cx + cy], dst_ref=got.at[r], send_sem=send_sems.at[r], recv_sem=recv_sems.at[r],
            device_id=(cx, cy, ac), device_id_type=MESH_IDS) for r, (cx, cy) in enumerate(chips)]
        for cp in copies:
            cp.start()
        for cp in copies:
            cp.wait_recv()
        for cp in copies:
            cp.wait_send()

    return _on_sequencer(body, [s1], [jax.ShapeDtypeStruct((3,) + s1.shape[1:], s1.dtype)],
                         [pltpu.SemaphoreType.DMA((3,)), pltpu.SemaphoreType.DMA((3,))], name=name, collective_id=collective_id)[0]


def _gather_async(xs, name, collective_id):
    rider = _gather_rider(xs)

    def body(in_refs, out_refs, send_sems, recv_sems):
        ax, ay, ac = _position()
        _handshake([(ax, ay, 1 - ac), (1 - ax, ay, ac), (ax, 1 - ay, ac), (1 - ax, 1 - ay, ac)])
        for hook in rider.hooks(in_refs, out_refs, send_sems, recv_sems):
            hook()

    return _on_sequencer(body, rider.arrays, rider.out_shapes, rider.sems, name=name, collective_id=collective_id)


def _gather_balanced(x, name, collective_id):
    m = x.shape[0]
    half = m // 2 // ROW_ALIGN * ROW_ALIGN
    parts = {"all": pl.ds(0, m), "lo": pl.ds(0, half), "hi": pl.ds(half, m - half)}

    def body(in_refs, out_refs, send_sems, recv_sems):
        (x_ref,), (out_ref,) = in_refs, out_refs
        ax, ay, ac = _position()
        me, sibling = (ax, ay, ac), (ax, ay, 1 - ac)
        by_x, by_y, diag = (1 - ax, ay), (ax, 1 - ay), (1 - ax, 1 - ay)
        _handshake([sibling, (*by_x, ac), (*by_y, ac)])

        def copy(k, block, part, to, own=False):
            rows = out_ref.at[4 * block[0] + 2 * block[1] + block[2], parts[part]]
            return pltpu.make_async_remote_copy(src_ref=x_ref if own else rows, dst_ref=rows, send_sem=send_sems.at[k],
                                                recv_sem=recv_sems.at[k], device_id=to, device_id_type=MESH_IDS)

        def own_half(k, part, to):
            rows = out_ref.at[4 * ax + 2 * ay + ac, parts[part]]
            return pltpu.make_async_remote_copy(src_ref=x_ref.at[parts[part]], dst_ref=rows, send_sem=send_sems.at[k],
                                                recv_sem=recv_sems.at[k], device_id=to, device_id_type=MESH_IDS)

        nx, ny, nd = (*by_x, ac), (*by_y, ac), (*diag, ac)
        sends = [copy(0, me, "all", sibling, own=True), own_half(1, "lo", nx), own_half(2, "hi", nx),
                 own_half(3, "hi", ny), own_half(4, "lo", ny), copy(5, nx, "lo", ny), copy(6, ny, "hi", nx),
                 copy(7, nx, "lo", sibling), copy(8, nx, "hi", sibling), copy(9, ny, "hi", sibling),
                 copy(10, ny, "lo", sibling), copy(11, nd, "lo", sibling), copy(12, nd, "hi", sibling)]
        sx, sy, sd = (*by_x, 1 - ac), (*by_y, 1 - ac), (*diag, 1 - ac)
        arrivals = [copy(0, sibling, "all", me), copy(1, nx, "lo", me), copy(2, nx, "hi", me), copy(3, ny, "hi", me),
                    copy(4, ny, "lo", me), copy(5, nd, "lo", me), copy(6, nd, "hi", me), copy(7, sx, "lo", me),
                    copy(8, sx, "hi", me), copy(9, sy, "hi", me), copy(10, sy, "lo", me), copy(11, sd, "lo", me),
                    copy(12, sd, "hi", me)]
        for k in range(5):
            sends[k].start()
        for arrived, onward in ((1, (5, 7)), (3, (6, 9)), (2, (8,)), (4, (10,)), (5, (11,)), (6, (12,))):
            arrivals[arrived].wait_recv()
            for k in onward:
                sends[k].start()
        for k in (0, 7, 8, 9, 10, 11, 12):
            arrivals[k].wait_recv()
        for cp in sends:
            cp.wait_send()

    return _on_sequencer(body, [x], [jax.ShapeDtypeStruct((NDEV,) + x.shape, x.dtype)],
                         [pltpu.SemaphoreType.DMA((13,)), pltpu.SemaphoreType.DMA((13,))], name=name, collective_id=collective_id)[0]


def _scatter_async(parts, name, collective_id):
    rider = _scatter_rider(parts)

    def body(in_refs, out_refs, send_sems, recv_sems):
        ax, ay, ac = _position()
        flip = lambda a, on: 1 - a if on else a
        _handshake([(flip(ax, rel & 4), flip(ay, rel & 2), flip(ac, rel & 1)) for rel in range(1, NDEV)])
        for hook in rider.hooks(in_refs, out_refs, send_sems, recv_sems):
            hook()

    return _on_sequencer(body, rider.arrays, rider.out_shapes, rider.sems, name=name, collective_id=collective_id)[0]


def _sum_in_chip(own, recv, name):
    _, r, w = own.shape
    tr = _tile(r, (256, 128))

    def body(a_ref, b_ref, o_ref):
        o_ref[...] = (a_ref[...].astype(F32) + b_ref[...].astype(F32)).astype(o_ref.dtype)

    blk = pl.BlockSpec((None, tr, w), lambda q, i: (q, i, 0))
    return pl.pallas_call(body, name=name, grid=(4, r // tr), in_specs=[blk, blk], out_specs=blk,
                          out_shape=jax.ShapeDtypeStruct(own.shape, own.dtype),
                          compiler_params=_cparams("parallel", "parallel"))(own, recv)


def _sum_chips(s1, recv, chip, name):
    _, r, w = s1.shape
    tr = _tile(r, (256, 128))

    def body(c_ref, s_ref, r0_ref, r1_ref, r2_ref, o_ref):
        f = lambda ref: ref[...].astype(F32)
        o_ref[...] = ((f(s_ref) + f(r0_ref)) + f(r1_ref)) + f(r2_ref)

    rblk = lambda k: pl.BlockSpec((None, tr, w), functools.partial(lambda i, c, k: (k, i, 0), k=k))
    grid_spec = pltpu.PrefetchScalarGridSpec(
        num_scalar_prefetch=1, grid=(r // tr,),
        in_specs=[pl.BlockSpec((None, tr, w), lambda i, c: (c[0], i, 0)), rblk(0), rblk(1), rblk(2)],
        out_specs=pl.BlockSpec((tr, w), lambda i, c: (i, 0)))
    return pl.pallas_call(body, name=name, grid_spec=grid_spec, out_shape=jax.ShapeDtypeStruct((r, w), F32),
                          compiler_params=_cparams("parallel"))(chip, s1, recv, recv, recv)


def _silu_rows(x, name):
    def body(x_ref, o_ref):
        o_ref[...] = _silu(x_ref[...])

    return pl.pallas_call(body, name=name, out_shape=jax.ShapeDtypeStruct(x.shape, F32))(x)


def _row_sum(x, name):
    def body(x_ref, o_ref):
        acc = x_ref[0:1, :]
        for i in range(1, x.shape[0]):
            acc = acc + x_ref[i:i + 1, :]
        o_ref[...] = acc

    return pl.pallas_call(body, name=name, out_shape=jax.ShapeDtypeStruct((1, x.shape[1]), F32))(x)


def _adamw(w, g, m, v, name):
    cols = w.shape[-1]
    rows = w.size // cols
    tr = _tile(rows, (256, 128))
    tc = LANE if (tr == rows and rows > 512 and cols % LANE == 0) else cols

    def body(w_ref, g_ref, m_ref, v_ref, d_ref, mo_ref, vo_ref):
        grad = g_ref[...]
        m_new = ADAM_B1 * m_ref[...] + (1.0 - ADAM_B1) * grad
        v_new = ADAM_B2 * v_ref[...] + (1.0 - ADAM_B2) * jnp.square(grad)
        m_hat = m_new / (1.0 - ADAM_B1 ** ADAM_STEP)
        v_hat = v_new / (1.0 - ADAM_B2 ** ADAM_STEP)
        d_ref[...] = -ADAM_LR * (m_hat / (jnp.sqrt(v_hat) + ADAM_EPS) + ADAM_WD * w_ref[...])
        mo_ref[...] = m_new
        vo_ref[...] = v_new

    blk = pl.BlockSpec((tr, tc), lambda i, j: (i, j))
    out = pl.pallas_call(
        body, name=name, grid=(rows // tr, cols // tc), in_specs=[blk] * 4, out_specs=[blk] * 3,
        out_shape=[jax.ShapeDtypeStruct((rows, cols), F32)] * 3, compiler_params=_cparams("parallel", "parallel"),
    )(*[t.reshape(rows, cols) for t in (w, g, m, v)])
    return [t.reshape(w.shape) for t in out]


def _pack(parts, width, row_mult, dtype):
    flat = jnp.concatenate([p.reshape(-1).astype(dtype) for p in parts])
    rows = -(-flat.shape[0] // (width * row_mult)) * row_mult
    return jnp.pad(flat, (0, rows * width - flat.shape[0])).reshape(rows, width)


def _unpack(flat, shapes):
    out, off = [], 0
    for shp in shapes:
        size = 1
        for dim in shp:
            size *= dim
        out.append(flat[:, off:off + size].reshape((flat.shape[0],) + tuple(shp)))
        off += size
    return out


def _devices_to_cols(a):
    _, r, c = a.shape
    return a.transpose(1, 0, 2).reshape(r, NDEV * c)


def kernel(x, c, w_ada, b_ada, norm1_w, w_in, gdn_conv_w, gdn_a_log, gdn_dt_bias, gdn_norm_w, w_gdn_proj, sc_conv_w, w_sc_out, w_o, norm2_w, w_ffn_in, w_ffn_out, w_ada_f, b_ada_f, normf_w, loss_target, m_w_ada, m_b_ada, m_norm1_w, m_w_in, m_gdn_conv_w, m_gdn_a_log, m_gdn_dt_bias, m_gdn_norm_w, m_w_gdn_proj, m_sc_conv_w, m_w_sc_out, m_w_o, m_norm2_w, m_w_ffn_in, m_w_ffn_out, m_w_ada_f, m_b_ada_f, m_normf_w, v_w_ada, v_b_ada, v_norm1_w, v_w_in, v_gdn_conv_w, v_gdn_a_log, v_gdn_dt_bias, v_gdn_norm_w, v_w_gdn_proj, v_sc_conv_w, v_w_sc_out, v_w_o, v_norm2_w, v_w_ffn_in, v_w_ffn_out, v_w_ada_f, v_b_ada_f, v_normf_w):
    bl, s, d = x.shape
    heads = gdn_a_log.shape[-1]
    dff = w_ffn_out.shape[1] * NDEV
    tok = bl * s
    ax, ay, ac = _position()
    dev = 4 * ax + 2 * ay + ac
    as_tok = lambda a: a.reshape(bl, s, a.shape[-1])
    as_mat = lambda a: a.reshape(tok, a.shape[-1])

    small = _all_gather(_pack([c, gdn_conv_w, sc_conv_w], LANE, 8, F32), name="gather_cond")
    c_all, conv_w, sc_w = _unpack(small.reshape(NDEV, -1), [(bl, d), gdn_conv_w.shape[1:], sc_conv_w.shape[1:]])
    c_act = _silu_rows(c_all.reshape(NDEV * bl, d), "cond_silu")
    conv_w, sc_w = _devices_to_cols(conv_w), _devices_to_cols(sc_w)
    n_ada, n_adaf = w_ada.shape[-1], w_ada_f.shape[-1]
    bias = jnp.broadcast_to(lax.dynamic_slice_in_dim(b_ada, dev * n_ada, n_ada, axis=1), (NDEV * bl, n_ada))
    biasf = jnp.broadcast_to(lax.dynamic_slice_in_dim(b_ada_f.reshape(1, -1), dev * n_adaf, n_adaf, axis=1), (NDEV * bl, n_adaf))
    mod_cols = _mm(c_act, w_ada[0], add=bias, name="ada_cols")
    modf_cols = _mm(c_act, w_ada_f, add=biasf, name="adaf_cols")
    mods = _all_gather(jnp.concatenate([mod_cols, modf_cols], axis=1), name="gather_mod")
    mod_all = mods[:, :, :n_ada].transpose(1, 0, 2).reshape(NDEV * bl, NDEV * n_ada)
    modf_all = mods[:, :, n_ada:].transpose(1, 0, 2).reshape(NDEV * bl, NDEV * n_adaf)
    my_rows = lambda a: lax.dynamic_slice_in_dim(a, dev * bl, bl, axis=0)
    sh1, sc1, g1, sh2, sc2, g2 = [t.reshape(bl, 1, d) for t in jnp.split(my_rows(mod_all), 6, axis=1)]
    shf, scf = [t.reshape(bl, 1, d) for t in jnp.split(my_rows(modf_all), 2, axis=1)]

    late = [t.astype(MXU_DTYPE) for t in (w_gdn_proj[0], w_sc_out[0], w_o[0], w_ffn_in[0].T, w_ffn_out[0])]
    rows = [t.shape[0] for t in late] + [w_in.shape[-1]]
    offs = [sum(rows[:i]) for i in range(5)]
    in_send = w_in[0].T.astype(MXU_DTYPE)
    with_own = lambda g, own: lax.dynamic_update_slice_in_dim(g, own[None], dev, axis=0)
    wt_in = with_own(_gather_balanced(in_send, "gather_w_in", 1), in_send).reshape(NDEV * rows[5], d)
    gathered = _gather_async(late[:3], "gather_mixer", 2) + _gather_async(late[3:], "gather_ffn", 3)
    wgp, wso, wo, wt_fi, wfo = [with_own(g, own).reshape(NDEV * own.shape[0], d) for g, own in zip(gathered, late)]
    o_z, o_ab, o_sc, o_ga, o_gb = 3 * d, 4 * d, 4 * d + 2 * heads, 7 * d + 2 * heads, 8 * d + 2 * heads
    s_qkv, s_z, s_sc, s_gate = (0, o_z), (o_z, d), (o_sc, 3 * d), (o_ga, 2 * d)
    wt_ab = jnp.pad(wt_in[o_ab:o_sc], ((0, LANE - 2 * heads), (0, 0)))

    n1w, n2w, nfw = norm1_w.reshape(1, d), norm2_w.reshape(1, d), normf_w.reshape(1, d)
    lanes = lambda a: jnp.pad(a.reshape(1, -1), ((0, 0), (0, LANE - a.size)))
    a_log, dt_bias, gnw = lanes(gdn_a_log), lanes(gdn_dt_bias), gdn_norm_w.reshape(1, HEAD)
    f_gates = functools.partial(_f_gates, heads=heads)
    (h1,) = _tok_fwd(_f_norm_mod, [x], [sh1, sc1], [n1w], [(d, MXU_DTYPE)], name="norm1", ts=512)
    h1m = as_mat(h1)
    p_qkv = as_tok(_mm_ring(h1m, wt_in, s_qkv, name="in_qkv"))
    p_z = as_tok(_mm(h1m, wt_in, tb=True, b_rows=s_z, name="in_z"))
    p_ab = as_tok(_mm(h1m, wt_ab, tb=True, name="in_ab"))
    p_sc = as_tok(_mm_ring(h1m, wt_in, s_sc, name="in_sc"))
    p_g = as_tok(_mm_ring(h1m, wt_in, s_gate, name="in_gate"))
    qkv = _qkv_fwd(p_qkv, conv_w, heads, "qkv_conv")
    (gbeta,) = _tok_fwd(f_gates, [p_ab], [], [a_log, dt_bias], [(LANE, F32)], name="gates", ts=512)
    o, s_all, t_all = _gdn_fwd(qkv, gbeta, heads, "gdn")
    (og,) = _tok_fwd(_f_gdn_out, [o, p_z], [], [(gnw, None)], [(d, MXU_DTYPE)], name="gdn_out", ts=2048, wb=HEAD, cols=heads)
    y_a = as_tok(_mm(as_mat(og), wgp, name="gdn_proj"))
    scp = _sc_fwd(p_sc, sc_w, "sc_conv")
    mrg, y_b = _tok_fwd(_f_merge_keep, [(p_g, 0), (p_g, 1), y_a, _Product(scp, wso)], [], [], [(d, MXU_DTYPE), (d, F32)],
                        name="merge", ts=512, wb=d)
    merge_toks = [(p_g, 0), (p_g, 1), y_a, y_b]
    x2, h2, mix = _tok_fwd(_f_res_norm_mod_keep, [x, _Product(mrg, wo)], [g1, sh2, sc2], [n2w],
                           [(d, F32), (d, MXU_DTYPE), (d, F32)], name="norm2", ts=512)
    act, gu_a, gu_b = _ffn_in_swiglu(as_mat(h2), wt_fi, dff, "ffn_in")

    loss_l, (dx2, dff_out, _), (dg2, dshf, dscf), (dnfw,) = _tok_bwd(
        _f_loss, [x2, _Product(as_tok(act), wfo), loss_target], [g2, shf, scf], [nfw], [], [True, True, False], name="loss",
        ts=512, loss=True, tok_dtype=[F32, MXU_DTYPE, None])
    dffm = as_mat(dff_out)
    dgu_a, dgu_b = _ffn_out_bwd_swiglu(dffm, wfo, gu_a, gu_b, "d_ffn_out")
    gmm = functools.partial(_mm, ta=True, out_dtype=MXU_DTYPE)
    gw_ffn_out = gmm(act, dffm, name="g_ffn_out")
    dh2 = _Product(as_tok(dgu_b), wt_fi, b_rows=(dff, dff), add=_Product(as_tok(dgu_a), wt_fi, b_rows=(0, dff)))
    h2m = as_mat(h2)
    gwt_ffn_in = gmm(dgu_a, h2m, out_rows=2 * dff, name="g_ffn_in_a")
    gwt_ffn_in = gmm(dgu_b, h2m, out_rows=2 * dff, row_off=dff, into=gwt_ffn_in, name="g_ffn_in_b")
    ffn_parts = [(gwt_ffn_in, rows[3]), (gw_ffn_out, rows[4])]
    ffn_recv = _scatter_async(ffn_parts, "scatter_ffn", 4)
    (dx_skip, dmix), (dg1, dsh2, dsc2), (dn2w,) = _tok_bwd(
        _f_res_norm_mod, [x, mix], [g1, sh2, sc2], [n2w], [dx2, dh2], [True, True], name="d_norm2", ts=512,
        tok_dtype=[F32, MXU_DTYPE], after=[gwt_ffn_in, gw_ffn_out])
    gw_o = gmm(as_mat(mrg), as_mat(dmix), name="g_mix_out")
    (dga, dgb, dya, dyb), _, _ = _tok_bwd(_f_merge, merge_toks, [], [], [_Product(dmix, wo, tb=True)], [True] * 4,
                                          name="d_merge", ts=512, wb=d, tok_dtype=MXU_DTYPE)
    dyam, dybm = as_mat(dya), as_mat(dyb)
    dog = as_tok(_mm(dyam, wgp, tb=True, name="d_gdn_proj"))
    gw_gdn_proj = gmm(as_mat(og), dyam, name="g_gdn_proj")
    dscp = as_tok(_mm(dybm, wso, tb=True, name="d_sc_out"))
    gw_sc_out = gmm(as_mat(scp), dybm, name="g_sc_out")
    dsc, g_sc_w = _sc_bwd(p_sc, sc_w, dscp, "d_sc_conv")
    mix_parts = [(gw_gdn_proj, rows[0]), (gw_sc_out, rows[1]), (gw_o, rows[2])]
    mix_recv = _scatter_async(mix_parts, "scatter_mixer", 5)
    (do, dz), _, (g_gnw,) = _tok_bwd(_f_gdn_out, [o, p_z], [], [(gnw, None)], [dog], [True, True], name="d_gdn_out",
                                     ts=2048, wb=HEAD, cols=heads, tok_dtype=[F32, MXU_DTYPE],
                                     after=[gw_gdn_proj, gw_sc_out, gw_o])
    own_rows = lambda parts: jnp.concatenate([lax.dynamic_slice_in_dim(g, dev * r, r, axis=0) for g, r in parts], axis=0)
    dqkv, dgbeta = _gdn_bwd(qkv, gbeta, do, s_all, t_all, heads, "d_gdn")
    dp_qkv, g_conv_w = _qkv_bwd(p_qkv, conv_w, dqkv, heads, "d_qkv_conv")
    ffn_red = _sum_direct(own_rows(ffn_parts), ffn_recv, "sum_ffn")
    mix_red = _sum_direct(own_rows(mix_parts), mix_recv, "sum_mix")
    (dp_ab,), _, (g_a_log, g_dt_bias) = _tok_bwd(f_gates, [p_ab], [], [a_log, dt_bias], [dgbeta], [True], name="d_gates",
                                                 ts=512, tok_dtype=MXU_DTYPE, after=[ffn_red, mix_red])
    r_in = rows[5]
    win = -(-(r_in + max(r_in * k % ROW_ALIGN for k in range(NDEV))) // 128) * 128
    need_rows = max(_window_start(r_in, k) for k in range(NDEV)) + win
    dsc_m = dsc.reshape(3, tok, d)
    gwt_in = ([gmm(as_mat(dp_qkv), h1m, name="g_in_qkv"), gmm(as_mat(dz), h1m, name="g_in_z"),
               gmm(as_mat(dp_ab), h1m, name="g_in_ab")[:2 * heads]]
              + [gmm(dsc_m, h1m, a_index=k, name=f"g_in_sc{k}") for k in range(3)]
              + [gmm(as_mat(dga), h1m, name="g_in_ga"), gmm(as_mat(dgb), h1m, name="g_in_gb")])
    gwt_in = jnp.concatenate(gwt_in + [jnp.zeros((need_rows - NDEV * r_in, d), MXU_DTYPE)], axis=0)
    assert d <= 1024
    wide = [as_mat(dp_qkv), as_mat(dz), dsc_m, as_mat(dga)]
    row_of = lambda t: d * t + jnp.where(t * d >= o_ab, 2 * heads, 0)
    recv1 = _exchange_in_chip([(gwt_in, r_in, win, 0)], "scatter_in_chip", 7)
    own = jnp.stack([lax.dynamic_slice_in_dim(gwt_in, _window_start(r_in, 2 * q + ac), win, axis=0) for q in range(4)])
    s1 = _sum_in_chip(own, recv1, "sum_in_chip")
    recv2 = _exchange_chips_async(s1, "scatter_chips", 6)

    dh1 = _mm(as_mat(dp_ab), wt_ab, name="d_in_ab")
    dh1 = _mm_chain(wide, wt_in, row_of, add=dh1, name="d_in", tk=d)
    dh1 = _Product(dgb, wt_in, b_rows=(o_gb, d), add=as_tok(dh1))
    (grad_x,), (dsh1, dsc1), (dn1w,) = _tok_bwd(_f_norm_mod_skip, [x], [sh1, sc1], [n1w], [dh1, dx_skip], [True],
                                                name="d_norm1", ts=512)
    reduced = _sum_chips(s1, recv2, (2 * ax + ay).reshape(1).astype(jnp.int32), "sum_chips")
    gt_w_in = lax.dynamic_slice_in_dim(reduced, r_in * dev - _window_start(r_in, dev), r_in, axis=0)
    g_w_in = gt_w_in.T.reshape(w_in.shape)
    gt_w_ffn_in = ffn_red[:rows[3]]
    g_w_ffn_in = gt_w_ffn_in.T.reshape(w_ffn_in.shape)
    g_w_ffn_out = ffn_red[rows[3]:].reshape(w_ffn_out.shape)
    g_w_gdn_proj, g_w_sc_out, g_w_o = (mix_red[offs[i]:offs[i] + rows[i]].reshape(ref.shape)
                                       for i, ref in enumerate((w_gdn_proj, w_sc_out, w_o)))

    dmod = jnp.concatenate([t.reshape(bl, d) for t in (dsh1, dsc1, dg1, dsh2, dsc2, dg2)], axis=1)
    dmodf = jnp.concatenate([t.reshape(bl, d) for t in (dshf, dscf)], axis=1)
    summed_parts = [dn1w, dn2w, dnfw, g_gnw, g_a_log, g_dt_bias, g_conv_w, g_sc_w, loss_l]
    partial = _all_gather(_pack([dmod, dmodf] + summed_parts, LANE, 8, F32), name="gather_small")
    partial = partial.reshape(NDEV, -1)
    n_rows = bl * (6 * d + 2 * d)
    dmod_all, dmodf_all = _unpack(partial[:, :n_rows], [(bl, 6 * d), (bl, 2 * d)])
    dmod_all, dmodf_all = dmod_all.reshape(NDEV * bl, 6 * d), dmodf_all.reshape(NDEV * bl, 2 * d)
    totals = _row_sum(partial[:, n_rows:], "sum_small")
    t_n1w, t_n2w, t_nfw, t_gnw, t_a_log, t_dt_bias, t_conv_w, t_sc_w, t_loss = [
        t[0] for t in _unpack(totals, [p.shape for p in summed_parts])]
    my_cols = lambda a, n: lax.dynamic_slice_in_dim(a, dev * n, n, axis=1)
    grads = {
        "w_ada": _mm(c_act, my_cols(dmod_all, n_ada), ta=True, name="g_ada").reshape(w_ada.shape),
        "b_ada": _row_sum(dmod_all, "g_ada_bias").reshape(b_ada.shape),
        "norm1_w": t_n1w.reshape(norm1_w.shape),
        "w_in": g_w_in,
        "gdn_conv_w": my_cols(t_conv_w, gdn_conv_w.shape[-1]).reshape(gdn_conv_w.shape),
        "gdn_a_log": t_a_log[:, :heads].reshape(gdn_a_log.shape),
        "gdn_dt_bias": t_dt_bias[:, :heads].reshape(gdn_dt_bias.shape),
        "gdn_norm_w": t_gnw.reshape(gdn_norm_w.shape),
        "w_gdn_proj": g_w_gdn_proj,
        "sc_conv_w": my_cols(t_sc_w, sc_conv_w.shape[-1]).reshape(sc_conv_w.shape),
        "w_sc_out": g_w_sc_out,
        "w_o": g_w_o,
        "norm2_w": t_n2w.reshape(norm2_w.shape),
        "w_ffn_in": g_w_ffn_in,
        "w_ffn_out": g_w_ffn_out,
        "w_ada_f": _mm(c_act, my_cols(dmodf_all, n_adaf), ta=True, name="g_adaf").reshape(w_ada_f.shape),
        "b_ada_f": _row_sum(dmodf_all, "g_adaf_bias").reshape(b_ada_f.shape),
        "normf_w": t_nfw.reshape(normf_w.shape),
    }
    weights = dict(w_ada=w_ada, b_ada=b_ada, norm1_w=norm1_w, w_in=w_in, gdn_conv_w=gdn_conv_w, gdn_a_log=gdn_a_log,
                   gdn_dt_bias=gdn_dt_bias, gdn_norm_w=gdn_norm_w, w_gdn_proj=w_gdn_proj, sc_conv_w=sc_conv_w,
                   w_sc_out=w_sc_out, w_o=w_o, norm2_w=norm2_w, w_ffn_in=w_ffn_in, w_ffn_out=w_ffn_out, w_ada_f=w_ada_f,
                   b_ada_f=b_ada_f, normf_w=normf_w)
    m_in = [m_w_ada, m_b_ada, m_norm1_w, m_w_in, m_gdn_conv_w, m_gdn_a_log, m_gdn_dt_bias, m_gdn_norm_w, m_w_gdn_proj,
            m_sc_conv_w, m_w_sc_out, m_w_o, m_norm2_w, m_w_ffn_in, m_w_ffn_out, m_w_ada_f, m_b_ada_f, m_normf_w]
    v_in = [v_w_ada, v_b_ada, v_norm1_w, v_w_in, v_gdn_conv_w, v_gdn_a_log, v_gdn_dt_bias, v_gdn_norm_w, v_w_gdn_proj,
            v_sc_conv_w, v_w_sc_out, v_w_o, v_norm2_w, v_w_ffn_in, v_w_ffn_out, v_w_ada_f, v_b_ada_f, v_normf_w]
    deltas, new_m, new_v = [], [], []
    grads_t = {"w_in": gt_w_in, "w_ffn_in": gt_w_ffn_in}
    for (wname, wt), mt, vt in zip(weights.items(), m_in, v_in):
        if wname in grads_t:
            back = lambda a, wt=wt: a.T.reshape(wt.shape)
            dl, mn, vn = (back(a) for a in _adamw(wt[0].T, grads_t[wname], mt[0].T, vt[0].T, "adamw_" + wname))
        else:
            dl, mn, vn = _adamw(wt, grads[wname], mt, vt, "adamw_" + wname)
        deltas.append(dl)
        new_m.append(mn)
        new_v.append(vn)
    loss = t_loss[0, 0]
    return (loss, grad_x, *[grads[k] for k in weights], *deltas, *new_m, *new_v)
```
